```python
import jax
import jax.numpy as jnp
from jax import lax
import numpy as np

D_MODEL = 1024
BATCH = 8
SEQ = 4096
DEPTH = 1

ATTN_HEADS = 8
ATTN_HEAD_DIM = 64
ATTN_WIDTH = ATTN_HEADS * ATTN_HEAD_DIM
DILATED_PATTERNS = ((128, 1), (512, 4), (2048, 16))
ATTN_BLOCK = 128
DN_HEADS = 4
DN_HEAD_DIM = 128
DN_WIDTH = DN_HEADS * DN_HEAD_DIM
DN_CHUNK = 64
CONV_WIDTH = 4
MIX_WIDTH = ATTN_WIDTH + DN_WIDTH
IN_SECTIONS = (ATTN_WIDTH, ATTN_WIDTH, ATTN_WIDTH, DN_WIDTH, DN_WIDTH, DN_WIDTH, DN_HEADS, DN_HEADS, DN_WIDTH)
IN_COLS = sum(IN_SECTIONS)
D_FF = 2816
NORM_EPS = 1e-6
L2_EPS = 1e-6
INIT_NOISE = 0.02

kernel_name = "hybrid_dilated_attn_gated_deltanet_macaron"


def rms_norm(x, gain):
    xf = x.astype(jnp.float32)
    y = xf * lax.rsqrt(jnp.mean(xf * xf, axis=-1, keepdims=True) + NORM_EPS)
    return (y * gain.astype(jnp.float32)).astype(x.dtype)


def swiglu(h, w_gate, w_up, w_down):
    return (jax.nn.silu(h @ w_gate) * (h @ w_up)) @ w_down


def alibi_slopes(n_heads):
    return jnp.asarray(np.array([2.0 ** (-8.0 * (i + 1) / n_heads) for i in range(n_heads)], dtype=np.float32))


def dilated_window_attention(q, k, v, slopes, window, dilation):
    B, S, H, Dh = q.shape
    L = S // dilation
    W = window // dilation
    nb = -(-L // ATTN_BLOCK)
    Lp = nb * ATTN_BLOCK

    def residues(t):
        return t.reshape(B, L, dilation, H, Dh).transpose(0, 2, 3, 1, 4).reshape(B * dilation, H, L, Dh)

    qr = jnp.pad(residues(q), ((0, 0), (0, 0), (0, Lp - L), (0, 0))).reshape(B * dilation, H, nb, ATTN_BLOCK, Dh)

    def band(t):
        t = jnp.pad(residues(t), ((0, 0), (0, 0), (ATTN_BLOCK, Lp - L), (0, 0)))
        t = t.reshape(B * dilation, H, nb + 1, ATTN_BLOCK, Dh)
        return jnp.concatenate([t[:, :, :-1], t[:, :, 1:]], axis=3)

    kb, vb = band(k), band(v)
    s = jnp.einsum('zhnqd,zhnkd->zhnqk', qr, kb) * (Dh ** -0.5)
    qi = jnp.arange(ATTN_BLOCK)[:, None]
    kj = jnp.arange(2 * ATTN_BLOCK)[None, :]
    steps = qi + ATTN_BLOCK - kj
    key_idx = jnp.arange(nb)[:, None, None] * ATTN_BLOCK + kj - ATTN_BLOCK
    valid = (steps >= 0) & (steps <= W) & (key_idx >= 0)
    bias = -slopes[:, None, None, None] * (steps * dilation).astype(jnp.float32)
    s = jnp.where(valid, s + bias, -jnp.inf)
    m = jnp.max(s, axis=-1)
    p = jnp.exp(s - m[..., None])
    l = jnp.sum(p, axis=-1)
    o = jnp.einsum('zhnqk,zhnkd->zhnqd', p, vb) / l[..., None]

    def back(t):
        extra = t.shape[4:]
        t = t.reshape(B, dilation, H, Lp, *extra)[:, :, :, :L]
        perm = (0, 3, 1, 2) + tuple(range(4, 4 + len(extra)))
        return t.transpose(perm).reshape(B, S, H, *extra)

    return back(o), back(m), back(l)


def dilated_attention(q, k, v):
    slopes = alibi_slopes(q.shape[2])
    outs, maxes, denoms = [], [], []
    for window, dilation in DILATED_PATTERNS:
        o, m, l = dilated_window_attention(q, k, v, slopes, window, dilation)
        outs.append(o)
        maxes.append(m)
        denoms.append(l)
    m_all = jnp.stack(maxes)
    wts = jnp.stack(denoms) * jnp.exp(m_all - jnp.max(m_all, axis=0, keepdims=True))
    return jnp.einsum('pbsh,pbshd->bshd', wts, jnp.stack(outs)) / jnp.sum(wts, axis=0)[..., None]


def gated_delta_rule(q, k, v, beta, g):
    B, S, H, Dk = q.shape
    Dv = v.shape[-1]
    C = DN_CHUNK
    N = S // C

    def chunks(t):
        return jnp.moveaxis(t.reshape(B, N, C, H, *t.shape[3:]), 3, 1)

    q = chunks(q) * (Dk ** -0.5)
    k, v, beta = chunks(k), chunks(v), chunks(beta)
    g = jnp.cumsum(chunks(g), axis=-1)
    causal = jnp.tril(jnp.ones((C, C), dtype=bool))
    strict = jnp.tril(jnp.ones((C, C), dtype=bool), k=-1)
    decay = jnp.exp(jnp.where(causal, g[..., :, None] - g[..., None, :], -jnp.inf))
    k_beta = k * beta[..., None]
    a_mat = jnp.where(strict, jnp.einsum('bhnck,bhnjk->bhncj', k_beta, k) * decay, 0.0)
    eye = jnp.eye(C, dtype=q.dtype)
    rhs = jnp.concatenate([v * beta[..., None], k_beta * jnp.exp(g)[..., None]], axis=-1)
    sol = lax.linalg.triangular_solve(eye + a_mat, rhs, left_side=True, lower=True)
    u, w = sol[..., :Dv], sol[..., Dv:]
    attn_intra = jnp.where(causal, jnp.einsum('bhnck,bhnjk->bhncj', q, k) * decay, 0.0)

    def step(state, xs):
        q_n, k_n, u_n, w_n, g_n, a_n = xs
        v_new = u_n - jnp.einsum('bhck,bhkv->bhcv', w_n, state)
        o_n = (jnp.einsum('bhck,bhkv->bhcv', q_n * jnp.exp(g_n)[..., None], state)
               + jnp.einsum('bhcj,bhjv->bhcv', a_n, v_new))
        g_last = g_n[..., -1]
        k_dec = k_n * jnp.exp(g_last[..., None] - g_n)[..., None]
        state = state * jnp.exp(g_last)[..., None, None] + jnp.einsum('bhck,bhcv->bhkv', k_dec, v_new)
        return state, o_n

    xs = tuple(jnp.moveaxis(t, 2, 0) for t in (q, k, u, w, g, attn_intra))
    state0 = jnp.zeros((B, H, Dk, Dv), q.dtype)
    _, o = lax.scan(step, state0, xs)
    return o.transpose(1, 0, 3, 2, 4).reshape(B, S, H, Dv)


def hybrid_mixer(h, w_in, conv_w, a_log, dt_bias, dn_norm, w_out):
    B, S, _ = h.shape
    f32 = jnp.float32
    split_points = np.cumsum(IN_SECTIONS)[:-1].tolist()
    aq, ak, av, dq, dk, dv, beta_raw, decay_raw, gate = jnp.split(h @ w_in, split_points, axis=-1)

    def attn_heads(t):
        return t.reshape(B, S, ATTN_HEADS, ATTN_HEAD_DIM).astype(f32)

    attn = dilated_attention(attn_heads(aq), attn_heads(ak), attn_heads(av))
    attn = attn.reshape(B, S, ATTN_WIDTH).astype(h.dtype)

    qkv = jnp.concatenate([dq, dk, dv], axis=-1)
    qkv_pad = jnp.pad(qkv, ((0, 0), (CONV_WIDTH - 1, 0), (0, 0)))
    conv = qkv_pad[:, 0:S] * conv_w[0]
    for j in range(1, CONV_WIDTH):
        conv = conv + qkv_pad[:, j:j + S] * conv_w[j]
    dq, dk, dv = jnp.split(jax.nn.silu(conv).astype(f32), 3, axis=-1)

    def dn_heads(t):
        return t.reshape(B, S, DN_HEADS, DN_HEAD_DIM)

    def l2n(t):
        return t * lax.rsqrt(jnp.sum(t * t, axis=-1, keepdims=True) + L2_EPS)

    beta = jax.nn.sigmoid(beta_raw.astype(f32))
    g = -jnp.exp(a_log.astype(f32)) * jax.nn.softplus(decay_raw.astype(f32) + dt_bias.astype(f32))
    o = gated_delta_rule(l2n(dn_heads(dq)), l2n(dn_heads(dk)), dn_heads(dv), beta, g)
    o = (o * lax.rsqrt(jnp.mean(o * o, axis=-1, keepdims=True) + NORM_EPS) * dn_norm.astype(f32)
         * jax.nn.silu(dn_heads(gate.astype(f32))))
    dn = o.reshape(B, S, DN_WIDTH).astype(h.dtype)

    return jnp.concatenate([attn, dn], axis=-1) @ w_out


def _fwd_setup_inputs(seed: int = 0) -> dict:
    key = jax.random.key(seed)
    ks = jax.random.split(key, 17)
    f32 = jnp.float32
    L = DEPTH

    def normal(k, shape, scale):
        return jax.random.normal(k, shape, f32) * scale

    def gain(k, shape):
        return 1.0 + INIT_NOISE * jax.random.normal(k, shape, f32)

    dt = jnp.exp(jax.random.uniform(ks[9], (L, DN_HEADS), f32, float(np.log(1e-3)), float(np.log(1e-1))))
    return {
        "x": normal(ks[0], (BATCH, SEQ, D_MODEL), 1.0),
        "norm_ffn1": gain(ks[1], (L, D_MODEL)),
        "ffn1_gate": normal(ks[2], (L, D_MODEL, D_FF), D_MODEL ** -0.5),
        "ffn1_up": normal(ks[3], (L, D_MODEL, D_FF), D_MODEL ** -0.5),
        "ffn1_down": normal(ks[4], (L, D_FF, D_MODEL), D_FF ** -0.5),
        "norm_mix": gain(ks[5], (L, D_MODEL)),
        "w_in": normal(ks[6], (L, D_MODEL, IN_COLS), D_MODEL ** -0.5),
        "conv_w": normal(ks[7], (L, CONV_WIDTH, 3 * DN_WIDTH), CONV_WIDTH ** -0.5),
        "a_log": jnp.log(jax.random.uniform(ks[8], (L, DN_HEADS), f32, 1.0, 16.0)),
        "dt_bias": dt + jnp.log(-jnp.expm1(-dt)),
        "dn_norm": gain(ks[10], (L, DN_HEAD_DIM)),
        "w_out": normal(ks[11], (L, MIX_WIDTH, D_MODEL), MIX_WIDTH ** -0.5),
        "norm_ffn2": gain(ks[12], (L, D_MODEL)),
        "ffn2_gate": normal(ks[13], (L, D_MODEL, D_FF), D_MODEL ** -0.5),
        "ffn2_up": normal(ks[14], (L, D_MODEL, D_FF), D_MODEL ** -0.5),
        "ffn2_down": normal(ks[15], (L, D_FF, D_MODEL), D_FF ** -0.5),
        "norm_final": gain(ks[16], (D_MODEL,)),
    }


def _fwd_reference(x, norm_ffn1, ffn1_gate, ffn1_up, ffn1_down, norm_mix, w_in, conv_w, a_log, dt_bias,
              dn_norm, w_out, norm_ffn2, ffn2_gate, ffn2_up, ffn2_down, norm_final):
    for i in range(DEPTH):
        x = x + 0.5 * swiglu(rms_norm(x, norm_ffn1[i]), ffn1_gate[i], ffn1_up[i], ffn1_down[i])
        x = x + hybrid_mixer(rms_norm(x, norm_mix[i]), w_in[i], conv_w[i], a_log[i], dt_bias[i],
                             dn_norm[i], w_out[i])
        x = x + 0.5 * swiglu(rms_norm(x, norm_ffn2[i]), ffn2_gate[i], ffn2_up[i], ffn2_down[i])
    return rms_norm(x, norm_final)


import jax as _jax
import jax.numpy as _jnp

TWIN_FORMAT = 'train_step'
FWD_PARAMS = ['x', 'norm_ffn1', 'ffn1_gate', 'ffn1_up', 'ffn1_down', 'norm_mix', 'w_in', 'conv_w', 'a_log', 'dt_bias', 'dn_norm', 'w_out', 'norm_ffn2', 'ffn2_gate', 'ffn2_up', 'ffn2_down', 'norm_final']
TWIN_WEIGHTS = ['norm_ffn1', 'ffn1_gate', 'ffn1_up', 'ffn1_down', 'norm_mix', 'w_in', 'conv_w', 'a_log', 'dt_bias', 'dn_norm', 'w_out', 'norm_ffn2', 'ffn2_gate', 'ffn2_up', 'ffn2_down', 'norm_final']
TWIN_DIFF_INPUT = 'x'
TWIN_INPUTS = ['x', 'norm_ffn1', 'ffn1_gate', 'ffn1_up', 'ffn1_down', 'norm_mix', 'w_in', 'conv_w', 'a_log', 'dt_bias', 'dn_norm', 'w_out', 'norm_ffn2', 'ffn2_gate', 'ffn2_up', 'ffn2_down', 'norm_final', 'loss_target', 'm_norm_ffn1', 'm_ffn1_gate', 'm_ffn1_up', 'm_ffn1_down', 'm_norm_mix', 'm_w_in', 'm_conv_w', 'm_a_log', 'm_dt_bias', 'm_dn_norm', 'm_w_out', 'm_norm_ffn2', 'm_ffn2_gate', 'm_ffn2_up', 'm_ffn2_down', 'm_norm_final', 'v_norm_ffn1', 'v_ffn1_gate', 'v_ffn1_up', 'v_ffn1_down', 'v_norm_mix', 'v_w_in', 'v_conv_w', 'v_a_log', 'v_dt_bias', 'v_dn_norm', 'v_w_out', 'v_norm_ffn2', 'v_ffn2_gate', 'v_ffn2_up', 'v_ffn2_down', 'v_norm_final']
TWIN_OUTPUTS = ['loss', 'grad_x', 'grad_norm_ffn1', 'grad_ffn1_gate', 'grad_ffn1_up', 'grad_ffn1_down', 'grad_norm_mix', 'grad_w_in', 'grad_conv_w', 'grad_a_log', 'grad_dt_bias', 'grad_dn_norm', 'grad_w_out', 'grad_norm_ffn2', 'grad_ffn2_gate', 'grad_ffn2_up', 'grad_ffn2_down', 'grad_norm_final', 'delta_norm_ffn1', 'delta_ffn1_gate', 'delta_ffn1_up', 'delta_ffn1_down', 'delta_norm_mix', 'delta_w_in', 'delta_conv_w', 'delta_a_log', 'delta_dt_bias', 'delta_dn_norm', 'delta_w_out', 'delta_norm_ffn2', 'delta_ffn2_gate', 'delta_ffn2_up', 'delta_ffn2_down', 'delta_norm_final', 'new_m_norm_ffn1', 'new_m_ffn1_gate', 'new_m_ffn1_up', 'new_m_ffn1_down', 'new_m_norm_mix', 'new_m_w_in', 'new_m_conv_w', 'new_m_a_log', 'new_m_dt_bias', 'new_m_dn_norm', 'new_m_w_out', 'new_m_norm_ffn2', 'new_m_ffn2_gate', 'new_m_ffn2_up', 'new_m_ffn2_down', 'new_m_norm_final', 'new_v_norm_ffn1', 'new_v_ffn1_gate', 'new_v_ffn1_up', 'new_v_ffn1_down', 'new_v_norm_mix', 'new_v_w_in', 'new_v_conv_w', 'new_v_a_log', 'new_v_dt_bias', 'new_v_dn_norm', 'new_v_w_out', 'new_v_norm_ffn2', 'new_v_ffn2_gate', 'new_v_ffn2_up', 'new_v_ffn2_down', 'new_v_norm_final']
TWIN_LEAF_KINDS = {'loss': 'loss', 'grad_x': 'grad_x', 'grad_norm_ffn1': 'grad_w', 'grad_ffn1_gate': 'grad_w', 'grad_ffn1_up': 'grad_w', 'grad_ffn1_down': 'grad_w', 'grad_norm_mix': 'grad_w', 'grad_w_in': 'grad_w', 'grad_conv_w': 'grad_w', 'grad_a_log': 'grad_w', 'grad_dt_bias': 'grad_w', 'grad_dn_norm': 'grad_w', 'grad_w_out': 'grad_w', 'grad_norm_ffn2': 'grad_w', 'grad_ffn2_gate': 'grad_w', 'grad_ffn2_up': 'grad_w', 'grad_ffn2_down': 'grad_w', 'grad_norm_final': 'grad_w', 'delta_norm_ffn1': 'delta_w', 'delta_ffn1_gate': 'delta_w', 'delta_ffn1_up': 'delta_w', 'delta_ffn1_down': 'delta_w', 'delta_norm_mix': 'delta_w', 'delta_w_in': 'delta_w', 'delta_conv_w': 'delta_w', 'delta_a_log': 'delta_w', 'delta_dt_bias': 'delta_w', 'delta_dn_norm': 'delta_w', 'delta_w_out': 'delta_w', 'delta_norm_ffn2': 'delta_w', 'delta_ffn2_gate': 'delta_w', 'delta_ffn2_up': 'delta_w', 'delta_ffn2_down': 'delta_w', 'delta_norm_final': 'delta_w', 'new_m_norm_ffn1': 'new_m', 'new_m_ffn1_gate': 'new_m', 'new_m_ffn1_up': 'new_m', 'new_m_ffn1_down': 'new_m', 'new_m_norm_mix': 'new_m', 'new_m_w_in': 'new_m', 'new_m_conv_w': 'new_m', 'new_m_a_log': 'new_m', 'new_m_dt_bias': 'new_m', 'new_m_dn_norm': 'new_m', 'new_m_w_out': 'new_m', 'new_m_norm_ffn2': 'new_m', 'new_m_ffn2_gate': 'new_m', 'new_m_ffn2_up': 'new_m', 'new_m_ffn2_down': 'new_m', 'new_m_norm_final': 'new_m', 'new_v_norm_ffn1': 'new_v', 'new_v_ffn1_gate': 'new_v', 'new_v_ffn1_up': 'new_v', 'new_v_ffn1_down': 'new_v', 'new_v_norm_mix': 'new_v', 'new_v_w_in': 'new_v', 'new_v_conv_w': 'new_v', 'new_v_a_log': 'new_v', 'new_v_dt_bias': 'new_v', 'new_v_dn_norm': 'new_v', 'new_v_w_out': 'new_v', 'new_v_norm_ffn2': 'new_v', 'new_v_ffn2_gate': 'new_v', 'new_v_ffn2_up': 'new_v', 'new_v_ffn2_down': 'new_v', 'new_v_norm_final': 'new_v'}


def _forward(args):
    return _fwd_reference(*[args[k] for k in FWD_PARAMS])


def _output_shape():
    def fwd():
        inp = _fwd_setup_inputs(0)
        return _fwd_reference(*[inp[k] for k in FWD_PARAMS])
    out = _jax.eval_shape(fwd)
    return out.shape, out.dtype

N_MICROBATCH = 1
ADAM_LR = 0.001
ADAM_B1 = 0.9
ADAM_B2 = 0.999
ADAM_EPS = 1e-08
ADAM_WD = 0.01
ADAM_STEP = 10
PER_EXAMPLE_BATCH_AXIS = {'x': 0, 'loss_target': 0}
SHARED_INPUTS = []
_WEIGHT_DTYPES = {'norm_ffn1': _jnp.float32, 'ffn1_gate': _jnp.float32, 'ffn1_up': _jnp.float32, 'ffn1_down': _jnp.float32, 'norm_mix': _jnp.float32, 'w_in': _jnp.float32, 'conv_w': _jnp.float32, 'a_log': _jnp.float32, 'dt_bias': _jnp.float32, 'dn_norm': _jnp.float32, 'w_out': _jnp.float32, 'norm_ffn2': _jnp.float32, 'ffn2_gate': _jnp.float32, 'ffn2_up': _jnp.float32, 'ffn2_down': _jnp.float32, 'norm_final': _jnp.float32}
MOMENT_SCALE = {'norm_ffn1': 8.880018e-02, 'ffn1_gate': 3.699896e-02, 'ffn1_up': 3.581392e-02, 'ffn1_down': 5.937875e-02, 'norm_mix': 1.317632e-01, 'w_in': 6.745889e-02, 'conv_w': 7.826297e-02, 'a_log': 6.923197e-01, 'dt_bias': 6.834490e-01, 'dn_norm': 2.010951e-01, 'w_out': 7.903803e-02, 'norm_ffn2': 6.675150e-02, 'ffn2_gate': 2.788614e-02, 'ffn2_up': 2.703864e-02, 'ffn2_down': 4.480249e-02, 'norm_final': 3.204558e+01}


def _to_microbatches(a, axis):
    t = _jnp.moveaxis(a, axis, 0)
    t = t.reshape((N_MICROBATCH, t.shape[0] // N_MICROBATCH) + t.shape[1:])
    return _jnp.moveaxis(t, 1, axis + 1)


def setup_inputs(seed: int = 0) -> dict:
    inp = _fwd_setup_inputs(seed)
    key = _jax.random.fold_in(_jax.random.key(seed), 7919)
    shape, _ = _output_shape()
    out = dict(inp)
    out["loss_target"] = _jax.random.normal(_jax.random.fold_in(key, 0), shape, _jnp.float32)
    for i, name in enumerate(TWIN_WEIGHTS):
        w = inp[name].astype(_jnp.float32)
        if MOMENT_SCALE is None:
            s = _jnp.sqrt(_jnp.mean(_jnp.square(w)) + 1e-30)
        else:
            s = MOMENT_SCALE[name]
        km, kv = _jax.random.split(_jax.random.fold_in(key, i + 1))
        out[name] = w
        out["m_" + name] = s * _jax.random.normal(km, w.shape, _jnp.float32)
        out["v_" + name] = (s * s) * _jax.random.uniform(kv, w.shape, _jnp.float32, 0.5, 1.5)
    if N_MICROBATCH > 1:
        for name, axis in PER_EXAMPLE_BATCH_AXIS.items():
            out[name] = _to_microbatches(out[name], axis)
    return {'x': out['x'], 'norm_ffn1': out['norm_ffn1'], 'ffn1_gate': out['ffn1_gate'], 'ffn1_up': out['ffn1_up'], 'ffn1_down': out['ffn1_down'], 'norm_mix': out['norm_mix'], 'w_in': out['w_in'], 'conv_w': out['conv_w'], 'a_log': out['a_log'], 'dt_bias': out['dt_bias'], 'dn_norm': out['dn_norm'], 'w_out': out['w_out'], 'norm_ffn2': out['norm_ffn2'], 'ffn2_gate': out['ffn2_gate'], 'ffn2_up': out['ffn2_up'], 'ffn2_down': out['ffn2_down'], 'norm_final': out['norm_final'], 'loss_target': out['loss_target'], 'm_norm_ffn1': out['m_norm_ffn1'], 'm_ffn1_gate': out['m_ffn1_gate'], 'm_ffn1_up': out['m_ffn1_up'], 'm_ffn1_down': out['m_ffn1_down'], 'm_norm_mix': out['m_norm_mix'], 'm_w_in': out['m_w_in'], 'm_conv_w': out['m_conv_w'], 'm_a_log': out['m_a_log'], 'm_dt_bias': out['m_dt_bias'], 'm_dn_norm': out['m_dn_norm'], 'm_w_out': out['m_w_out'], 'm_norm_ffn2': out['m_norm_ffn2'], 'm_ffn2_gate': out['m_ffn2_gate'], 'm_ffn2_up': out['m_ffn2_up'], 'm_ffn2_down': out['m_ffn2_down'], 'm_norm_final': out['m_norm_final'], 'v_norm_ffn1': out['v_norm_ffn1'], 'v_ffn1_gate': out['v_ffn1_gate'], 'v_ffn1_up': out['v_ffn1_up'], 'v_ffn1_down': out['v_ffn1_down'], 'v_norm_mix': out['v_norm_mix'], 'v_w_in': out['v_w_in'], 'v_conv_w': out['v_conv_w'], 'v_a_log': out['v_a_log'], 'v_dt_bias': out['v_dt_bias'], 'v_dn_norm': out['v_dn_norm'], 'v_w_out': out['v_w_out'], 'v_norm_ffn2': out['v_norm_ffn2'], 'v_ffn2_gate': out['v_ffn2_gate'], 'v_ffn2_up': out['v_ffn2_up'], 'v_ffn2_down': out['v_ffn2_down'], 'v_norm_final': out['v_norm_final']}


def _loss(weights, diff, rest, loss_target):
    with _jax.named_scope("forward"):
        args = {**rest, TWIN_DIFF_INPUT: diff, **{k: w.astype(_WEIGHT_DTYPES[k]) for k, w in weights.items()}}
        y = _forward(args)
    with _jax.named_scope("loss_head"):
        err = _jnp.square(y.astype(_jnp.float32) - loss_target)
        return 0.5 * _jnp.sum(_jnp.mean(err, axis=-1)) if err.ndim else 0.5 * err


def _adamw(w, g, m, v):
    m = ADAM_B1 * m + (1.0 - ADAM_B1) * g
    v = ADAM_B2 * v + (1.0 - ADAM_B2) * _jnp.square(g)
    m_hat = m / (1.0 - ADAM_B1 ** ADAM_STEP)
    v_hat = v / (1.0 - ADAM_B2 ** ADAM_STEP)
    delta = -ADAM_LR * (m_hat / (_jnp.sqrt(v_hat) + ADAM_EPS) + ADAM_WD * w)
    return delta, m, v


def reference(x, norm_ffn1, ffn1_gate, ffn1_up, ffn1_down, norm_mix, w_in, conv_w, a_log, dt_bias, dn_norm, w_out, norm_ffn2, ffn2_gate, ffn2_up, ffn2_down, norm_final, loss_target, m_norm_ffn1, m_ffn1_gate, m_ffn1_up, m_ffn1_down, m_norm_mix, m_w_in, m_conv_w, m_a_log, m_dt_bias, m_dn_norm, m_w_out, m_norm_ffn2, m_ffn2_gate, m_ffn2_up, m_ffn2_down, m_norm_final, v_norm_ffn1, v_ffn1_gate, v_ffn1_up, v_ffn1_down, v_norm_mix, v_w_in, v_conv_w, v_a_log, v_dt_bias, v_dn_norm, v_w_out, v_norm_ffn2, v_ffn2_gate, v_ffn2_up, v_ffn2_down, v_norm_final):
    given = dict(x=x, norm_ffn1=norm_ffn1, ffn1_gate=ffn1_gate, ffn1_up=ffn1_up, ffn1_down=ffn1_down, norm_mix=norm_mix, w_in=w_in, conv_w=conv_w, a_log=a_log, dt_bias=dt_bias, dn_norm=dn_norm, w_out=w_out, norm_ffn2=norm_ffn2, ffn2_gate=ffn2_gate, ffn2_up=ffn2_up, ffn2_down=ffn2_down, norm_final=norm_final, loss_target=loss_target, m_norm_ffn1=m_norm_ffn1, m_ffn1_gate=m_ffn1_gate, m_ffn1_up=m_ffn1_up, m_ffn1_down=m_ffn1_down, m_norm_mix=m_norm_mix, m_w_in=m_w_in, m_conv_w=m_conv_w, m_a_log=m_a_log, m_dt_bias=m_dt_bias, m_dn_norm=m_dn_norm, m_w_out=m_w_out, m_norm_ffn2=m_norm_ffn2, m_ffn2_gate=m_ffn2_gate, m_ffn2_up=m_ffn2_up, m_ffn2_down=m_ffn2_down, m_norm_final=m_norm_final, v_norm_ffn1=v_norm_ffn1, v_ffn1_gate=v_ffn1_gate, v_ffn1_up=v_ffn1_up, v_ffn1_down=v_ffn1_down, v_norm_mix=v_norm_mix, v_w_in=v_w_in, v_conv_w=v_conv_w, v_a_log=v_a_log, v_dt_bias=v_dt_bias, v_dn_norm=v_dn_norm, v_w_out=v_w_out, v_norm_ffn2=v_norm_ffn2, v_ffn2_gate=v_ffn2_gate, v_ffn2_up=v_ffn2_up, v_ffn2_down=v_ffn2_down, v_norm_final=v_norm_final)
    weights = {n: given[n] for n in TWIN_WEIGHTS}
    shared = {n: given[n] for n in SHARED_INPUTS}
    per_example = {n: given[n] for n in ['x']}
    grad_fn = _jax.value_and_grad(_loss, argnums=(0, 1))

    def one_microbatch(ex, loss_target):
        ex = dict(ex)
        diff = ex.pop(TWIN_DIFF_INPUT)
        return grad_fn(weights, diff, {**shared, **ex}, loss_target)

    if N_MICROBATCH == 1:
        loss, (grad_w, grad_x) = one_microbatch(per_example, given["loss_target"])
    else:
        def body(carry, xs):
            loss_sum, grad_sum = carry
            l_k, (gw_k, gx_k) = one_microbatch(xs[0], xs[1])
            with _jax.named_scope("update"):
                return (loss_sum + l_k, _jax.tree.map(_jnp.add, grad_sum, gw_k)), gx_k

        init = (_jnp.zeros((), _jnp.float32), _jax.tree.map(_jnp.zeros_like, weights))
        (loss, grad_w), grad_x = _jax.lax.scan(body, init, (per_example, given["loss_target"]))
    with _jax.named_scope("update"):
        delta_w, new_m, new_v = {}, {}, {}
        for n in TWIN_WEIGHTS:
            delta_w[n], new_m[n], new_v[n] = _adamw(weights[n], grad_w[n], given["m_" + n], given["v_" + n])
    return (loss, grad_x, *[grad_w[n] for n in TWIN_WEIGHTS], *[delta_w[n] for n in TWIN_WEIGHTS],
            *[new_m[n] for n in TWIN_WEIGHTS], *[new_v[n] for n in TWIN_WEIGHTS])
```

```python
import functools
import math

import jax
import jax.numpy as jnp
from jax import lax
from jax.experimental import pallas as pl
from jax.experimental.pallas import tpu as pltpu

F32 = jnp.float32
BF16 = jnp.bfloat16
HI = lax.Precision.HIGHEST

D_MODEL = 1024
D_FF = 2816
ATTN_HEADS = 8
ATTN_WIDTH = 512
ATTN_BLOCK = 128
DILATIONS = (1, 4, 16)
DN_HEADS = 4
DN_HEAD_DIM = 128
DN_WIDTH = 512
DN_CHUNK = 64
CONV_WIDTH = 4
NORM_EPS = 1e-6
L2_EPS = 1e-6
IN_COLS = 3592
IN_COLS_PADDED = 3712
N_CHIPS = 4

ADAM_LR = 0.001
ADAM_B1 = 0.9
ADAM_B2 = 0.999
ADAM_EPS = 1e-08
ADAM_WD = 0.01
ADAM_STEP = 10

VMEM_LIMIT = 56 * 1024 * 1024
NEG_BIG = -1e30
MESH = pl.DeviceIdType.MESH


def _params(n_grid, vmem=VMEM_LIMIT):
    return pltpu.CompilerParams(dimension_semantics=("arbitrary",) * n_grid, vmem_limit_bytes=vmem)


def _nt(a, b, precision=None):
    return lax.dot_general(a, b, (((1,), (1,)), ((), ())), preferred_element_type=F32, precision=precision)


def _tn(a, b, precision=None):
    return lax.dot_general(a, b, (((0,), (0,)), ((), ())), preferred_element_type=F32, precision=precision)


def _nn(a, b, precision=None):
    return jnp.dot(a, b, preferred_element_type=F32, precision=precision)


def _sigmoid(x):
    return 1.0 / (1.0 + jnp.exp(-x))


def _ffn_fwd(x, gain, wg, wu, wd, name):
    S, D = x.shape
    F = wg.shape[1]
    tm, tf = 512, F // 2
    nf = F // tf

    def body(x_ref, gain_ref, wg_ref, wu_ref, wd_ref, xo_ref, h_ref, g_ref, u_ref, acc_ref, hs_ref):
        j = pl.program_id(1)

        @pl.when(j == 0)
        def _():
            xf = x_ref[...]
            r = lax.rsqrt(jnp.mean(xf * xf, axis=-1, keepdims=True) + NORM_EPS)
            h = (xf * r * gain_ref[...]).astype(BF16)
            hs_ref[...] = h
            h_ref[...] = h
            acc_ref[...] = jnp.zeros_like(acc_ref)

        h = hs_ref[...]
        g = _nn(h, wg_ref[...])
        u = _nn(h, wu_ref[...])
        g_ref[...] = g.astype(BF16)
        u_ref[...] = u.astype(BF16)
        act = g * _sigmoid(g) * u
        acc_ref[...] += _nn(act.astype(BF16), wd_ref[...])

        @pl.when(j == nf - 1)
        def _():
            xo_ref[...] = x_ref[...] + 0.5 * acc_ref[...]

    return pl.pallas_call(
        body, name=name, grid=(S // tm, nf),
        in_specs=[pl.BlockSpec((tm, D), lambda i, j: (i, 0)),
                  pl.BlockSpec((1, D), lambda i, j: (0, 0)),
                  pl.BlockSpec((D, tf), lambda i, j: (0, j)),
                  pl.BlockSpec((D, tf), lambda i, j: (0, j)),
                  pl.BlockSpec((tf, D), lambda i, j: (j, 0))],
        out_specs=[pl.BlockSpec((tm, D), lambda i, j: (i, 0)),
                   pl.BlockSpec((tm, D), lambda i, j: (i, 0)),
                   pl.BlockSpec((tm, tf), lambda i, j: (i, j)),
                   pl.BlockSpec((tm, tf), lambda i, j: (i, j))],
        out_shape=[jax.ShapeDtypeStruct((S, D), F32), jax.ShapeDtypeStruct((S, D), BF16),
                   jax.ShapeDtypeStruct((S, F), BF16), jax.ShapeDtypeStruct((S, F), BF16)],
        scratch_shapes=[pltpu.VMEM((tm, D), F32), pltpu.VMEM((tm, D), BF16)],
        compiler_params=_params(2),
    )(x, gain, wg, wu, wd)


def _rmsnorm_bwd(dh, xf, gain):
    r = lax.rsqrt(jnp.mean(xf * xf, axis=-1, keepdims=True) + NORM_EPS)
    xhat = xf * r
    dgain = jnp.sum(dh * xhat, axis=0, keepdims=True)
    dxh = dh * gain
    dx = r * (dxh - xhat * jnp.mean(dxh * xhat, axis=-1, keepdims=True))
    return dx, dgain


def _ffn_bwd(dxo, x, gain, g, u, wdT, wgT, wuT, name):
    S, D = x.shape
    F = g.shape[1]
    tm, tf = 256, F // 2
    nf = F // tf

    def body(dxo_ref, x_ref, gain_ref, g_ref, u_ref, wdT_ref, wgT_ref, wuT_ref,
             dx_ref, dgain_ref, dg_ref, du_ref, act_ref, dout_ref, acc_ref, ds_ref):
        i = pl.program_id(0)
        j = pl.program_id(1)

        @pl.when(j == 0)
        def _():
            d = (0.5 * dxo_ref[...]).astype(BF16)
            ds_ref[...] = d
            dout_ref[...] = d
            acc_ref[...] = jnp.zeros_like(acc_ref)

        @pl.when((i == 0) & (j == 0))
        def _():
            dgain_ref[...] = jnp.zeros_like(dgain_ref)

        dact = _nn(ds_ref[...], wdT_ref[...])
        gv = g_ref[...].astype(F32)
        uv = u_ref[...].astype(F32)
        sg = _sigmoid(gv)
        silu = gv * sg
        act_ref[...] = (silu * uv).astype(BF16)
        dgv = (dact * uv * (sg * (1.0 + gv * (1.0 - sg)))).astype(BF16)
        duv = (dact * silu).astype(BF16)
        dg_ref[...] = dgv
        du_ref[...] = duv
        acc_ref[...] += _nn(dgv, wgT_ref[...]) + _nn(duv, wuT_ref[...])

        @pl.when(j == nf - 1)
        def _():
            dx, dgain = _rmsnorm_bwd(acc_ref[...], x_ref[...], gain_ref[...])
            dx_ref[...] = dxo_ref[...] + dx
            dgain_ref[...] += dgain

    return pl.pallas_call(
        body, name=name, grid=(S // tm, nf),
        in_specs=[pl.BlockSpec((tm, D), lambda i, j: (i, 0)),
                  pl.BlockSpec((tm, D), lambda i, j: (i, 0)),
                  pl.BlockSpec((1, D), lambda i, j: (0, 0)),
                  pl.BlockSpec((tm, tf), lambda i, j: (i, j)),
                  pl.BlockSpec((tm, tf), lambda i, j: (i, j)),
                  pl.BlockSpec((D, tf), lambda i, j: (0, j)),
                  pl.BlockSpec((tf, D), lambda i, j: (j, 0)),
                  pl.BlockSpec((tf, D), lambda i, j: (j, 0))],
        out_specs=[pl.BlockSpec((tm, D), lambda i, j: (i, 0)),
                   pl.BlockSpec((1, D), lambda i, j: (0, 0)),
                   pl.BlockSpec((tm, tf), lambda i, j: (i, j)),
                   pl.BlockSpec((tm, tf), lambda i, j: (i, j)),
                   pl.BlockSpec((tm, tf), lambda i, j: (i, j)),
                   pl.BlockSpec((tm, D), lambda i, j: (i, 0))],
        out_shape=[jax.ShapeDtypeStruct((S, D), F32), jax.ShapeDtypeStruct((1, D), F32),
                   jax.ShapeDtypeStruct((S, F), BF16), jax.ShapeDtypeStruct((S, F), BF16),
                   jax.ShapeDtypeStruct((S, F), BF16), jax.ShapeDtypeStruct((S, D), BF16)],
        scratch_shapes=[pltpu.VMEM((tm, D), F32), pltpu.VMEM((tm, D), BF16)],
        compiler_params=_params(2),
    )(dxo, x, gain, g, u, wdT, wgT, wuT)


def _matmul(a, b, tm, tn, tk, name):
    M, K = a.shape
    N = b.shape[1]

    def body(a_ref, b_ref, o_ref):
        @pl.when(pl.program_id(2) == 0)
        def _():
            o_ref[...] = jnp.zeros_like(o_ref)

        o_ref[...] += _nn(a_ref[...], b_ref[...])

    return pl.pallas_call(
        body, name=name, grid=(M // tm, N // tn, K // tk),
        in_specs=[pl.BlockSpec((tm, tk), lambda i, j, k: (i, k)),
                  pl.BlockSpec((tk, tn), lambda i, j, k: (k, j))],
        out_specs=pl.BlockSpec((tm, tn), lambda i, j, k: (i, j)),
        out_shape=jax.ShapeDtypeStruct((M, N), F32),
        compiler_params=_params(3),
    )(a, b)


def _inproj_fwd(x, gain, w_in_p):
    S, D = x.shape
    tm = 512
    W = ATTN_WIDTH

    def body(x_ref, gain_ref, w_ref, h_ref, aq_ref, ak_ref, av_ref, dq_ref, dk_ref, dv_ref, gate_ref, bd_ref):
        xf = x_ref[...]
        r = lax.rsqrt(jnp.mean(xf * xf, axis=-1, keepdims=True) + NORM_EPS)
        h = (xf * r * gain_ref[...]).astype(BF16)
        h_ref[...] = h
        aq_ref[...] = (_nn(h, w_ref[:, 0:W]) * 0.125).astype(BF16)
        ak_ref[...] = _nn(h, w_ref[:, W:2 * W]).astype(BF16)
        av_ref[...] = _nn(h, w_ref[:, 2 * W:3 * W]).astype(BF16)
        dq_ref[...] = _nn(h, w_ref[:, 3 * W:4 * W])
        dk_ref[...] = _nn(h, w_ref[:, 4 * W:5 * W])
        dv_ref[...] = _nn(h, w_ref[:, 5 * W:6 * W])
        gate_ref[...] = _nn(h, w_ref[:, 6 * W:7 * W])
        bd_ref[...] = _nn(h, w_ref[:, 7 * W:7 * W + 128])

    tok = lambda w: pl.BlockSpec((tm, w), lambda i: (i, 0))
    return pl.pallas_call(
        body, name="inproj_fwd", grid=(S // tm,),
        in_specs=[tok(D), pl.BlockSpec((1, D), lambda i: (0, 0)),
                  pl.BlockSpec((D, IN_COLS_PADDED), lambda i: (0, 0))],
        out_specs=[tok(D)] + [tok(W)] * 7 + [tok(128)],
        out_shape=[jax.ShapeDtypeStruct((S, D), BF16)] + [jax.ShapeDtypeStruct((S, W), BF16)] * 3
                  + [jax.ShapeDtypeStruct((S, W), F32)] * 4 + [jax.ShapeDtypeStruct((S, 128), F32)],
        compiler_params=_params(1),
    )(x, gain, w_in_p)


def _inproj_bwd(dxo, x, gain, dsecs, dbd, w_in_pT):
    S, D = x.shape
    tm = 512
    W = ATTN_WIDTH

    def body(dxo_ref, x_ref, gain_ref, s0, s1, s2, s3, s4, s5, s6, dbd_ref, wT_ref, dx_ref, dgain_ref, dproj_ref):
        @pl.when(pl.program_id(0) == 0)
        def _():
            dgain_ref[...] = jnp.zeros_like(dgain_ref)

        dh = jnp.zeros((tm, D), F32)
        for k, s in enumerate((s0, s1, s2, s3, s4, s5, s6)):
            d = s[...].astype(BF16)
            dproj_ref[:, k * W:(k + 1) * W] = d
            dh += _nn(d, wT_ref[k * W:(k + 1) * W, :])
        d = dbd_ref[...].astype(BF16)
        dproj_ref[:, 7 * W:7 * W + 128] = d
        dh += _nn(d, wT_ref[7 * W:7 * W + 128, :])
        dx, dgain = _rmsnorm_bwd(dh, x_ref[...], gain_ref[...])
        dx_ref[...] = dxo_ref[...] + dx
        dgain_ref[...] += dgain

    tok = lambda w: pl.BlockSpec((tm, w), lambda i: (i, 0))
    return pl.pallas_call(
        body, name="inproj_bwd", grid=(S // tm,),
        in_specs=[tok(D), tok(D), pl.BlockSpec((1, D), lambda i: (0, 0))] + [tok(W)] * 7 + [tok(128)]
                 + [pl.BlockSpec((IN_COLS_PADDED, D), lambda i: (0, 0))],
        out_specs=[tok(D), pl.BlockSpec((1, D), lambda i: (0, 0)), tok(IN_COLS_PADDED)],
        out_shape=[jax.ShapeDtypeStruct((S, D), F32), jax.ShapeDtypeStruct((1, D), F32),
                   jax.ShapeDtypeStruct((S, IN_COLS_PADDED), BF16)],
        compiler_params=_params(1),
    )(dxo, x, gain, *dsecs, dbd, w_in_pT)


def _slope(h):
    return 2.0 ** (-8.0 * (h + 1) / ATTN_HEADS)


def _attn_fwd(q, k, v, d, name):
    S = q.shape[0]
    L = S // d
    nb = L // ATTN_BLOCK
    B = ATTN_BLOCK
    view = lambda t: t.reshape(L, d * ATTN_WIDTH)

    def body(q_ref, kp_ref, kc_ref, vp_ref, vc_ref, acc_ref, m_ref, l_ref):
        n = pl.program_id(1)
        qi = lax.broadcasted_iota(jnp.int32, (B, 2 * B), 0)
        kj = lax.broadcasted_iota(jnp.int32, (B, 2 * B), 1)
        steps = qi + B - kj
        valid = (steps >= 0) & (steps <= B) & ((kj >= B) | (n > 0))
        stepsf = steps.astype(F32)
        lo = lax.broadcasted_iota(jnp.int32, (B, 128), 1) < 64
        for G in range(4):
            sl = slice(G * 128, (G + 1) * 128)
            qg = q_ref[:, sl]
            kg = jnp.concatenate([kp_ref[:, sl], kc_ref[:, sl]], axis=0)
            vg = jnp.concatenate([vp_ref[:, sl], vc_ref[:, sl]], axis=0)
            res = []
            for half in (0, 1):
                msk = lo if half == 0 else jnp.logical_not(lo)
                qm = jnp.where(msk, qg, jnp.zeros_like(qg))
                s = _nt(qm, kg)
                s = jnp.where(valid, s - (_slope(2 * G + half) * d) * stepsf, NEG_BIG)
                m = jnp.max(s, axis=-1, keepdims=True)
                p = jnp.exp(s - m)
                l = jnp.sum(p, axis=-1, keepdims=True)
                a = _nn(p.astype(BF16), vg)
                res.append((a, m, l))
            (a0, m0, l0), (a1, m1, l1) = res
            acc_ref[:, sl] = jnp.where(lo, a0, a1)
            m_ref[:, sl] = jnp.where(lo, m0, m1)
            l_ref[:, sl] = jnp.where(lo, l0, l1)

    cur = pl.BlockSpec((B, ATTN_WIDTH), lambda r, n: (n, r))
    prev = pl.BlockSpec((B, ATTN_WIDTH), lambda r, n: (jnp.maximum(n - 1, 0), r))
    outs = pl.pallas_call(
        body, name=name, grid=(d, nb),
        in_specs=[cur, prev, cur, prev, cur],
        out_specs=[cur, cur, cur],
        out_shape=[jax.ShapeDtypeStruct((L, d * ATTN_WIDTH), F32)] * 3,
        compiler_params=_params(2),
    )(view(q), view(k), view(k), view(v), view(v))
    return [o.reshape(S, ATTN_WIDTH) for o in outs]


def _attn_merge(parts):
    S = parts[0][0].shape[0]
    tm = 512

    def body(a1, m1, l1, a2, m2, l2, a3, m3, l3, o_ref, lse_ref):
        ms = [m1[...], m2[...], m3[...]]
        mx = jnp.maximum(jnp.maximum(ms[0], ms[1]), ms[2])
        es = [jnp.exp(m - mx) for m in ms]
        den = es[0] * l1[...] + es[1] * l2[...] + es[2] * l3[...]
        num = es[0] * a1[...] + es[1] * a2[...] + es[2] * a3[...]
        o_ref[...] = num / den
        lse_ref[...] = mx + jnp.log(den)

    tok = pl.BlockSpec((tm, ATTN_WIDTH), lambda i: (i, 0))
    flat = [t for p in parts for t in p]
    return pl.pallas_call(
        body, name="attn_merge", grid=(S // tm,), in_specs=[tok] * 9, out_specs=[tok, tok],
        out_shape=[jax.ShapeDtypeStruct((S, ATTN_WIDTH), F32)] * 2, compiler_params=_params(1),
    )(*flat)


def _attn_delta(do, o):
    S = o.shape[0]
    tm = 512

    def body(do_ref, o_ref, dd_ref):
        lo = lax.broadcasted_iota(jnp.int32, (tm, 128), 1) < 64
        for G in range(4):
            sl = slice(G * 128, (G + 1) * 128)
            t = do_ref[:, sl] * o_ref[:, sl]
            d0 = jnp.sum(jnp.where(lo, t, 0.0), axis=-1, keepdims=True)
            d1 = jnp.sum(jnp.where(lo, 0.0, t), axis=-1, keepdims=True)
            dd_ref[:, sl] = jnp.where(lo, d0, d1)

    tok = pl.BlockSpec((tm, ATTN_WIDTH), lambda i: (i, 0))
    return pl.pallas_call(
        body, name="attn_delta", grid=(S // tm,), in_specs=[tok, tok], out_specs=tok,
        out_shape=jax.ShapeDtypeStruct((S, ATTN_WIDTH), F32), compiler_params=_params(1),
    )(do, o)


def _head_col(t, msk, big):
    if big:
        return jnp.max(jnp.where(msk, t, NEG_BIG), axis=-1, keepdims=True)
    return jnp.sum(jnp.where(msk, t, 0.0), axis=-1, keepdims=True) * (1.0 / 64.0)


def _attn_bwd_q(q, k, v, do, lse, dd, dq_run, d, name):
    S = q.shape[0]
    L = S // d
    nb = L // ATTN_BLOCK
    B = ATTN_BLOCK
    view = lambda t: t.reshape(L, d * ATTN_WIDTH)
    has_run = dq_run is not None

    def body(q_ref, kp_ref, kc_ref, vp_ref, vc_ref, do_ref, lse_ref, dd_ref, *rest):
        dq_ref = rest[-1]
        n = pl.program_id(1)
        qi = lax.broadcasted_iota(jnp.int32, (B, 2 * B), 0)
        kj = lax.broadcasted_iota(jnp.int32, (B, 2 * B), 1)
        steps = qi + B - kj
        valid = (steps >= 0) & (steps <= B) & ((kj >= B) | (n > 0))
        stepsf = steps.astype(F32)
        lo = lax.broadcasted_iota(jnp.int32, (B, 128), 1) < 64
        for G in range(4):
            sl = slice(G * 128, (G + 1) * 128)
            qg = q_ref[:, sl]
            kg = jnp.concatenate([kp_ref[:, sl], kc_ref[:, sl]], axis=0)
            vg = jnp.concatenate([vp_ref[:, sl], vc_ref[:, sl]], axis=0)
            dog = do_ref[:, sl]
            res = []
            for half in (0, 1):
                msk = lo if half == 0 else jnp.logical_not(lo)
                qm = jnp.where(msk, qg, jnp.zeros_like(qg))
                s = _nt(qm, kg) - (_slope(2 * G + half) * d) * stepsf
                lse_c = _head_col(lse_ref[:, sl], msk, True)
                p = jnp.where(valid, jnp.exp(jnp.where(valid, s, NEG_BIG) - lse_c), 0.0)
                dom = jnp.where(msk, dog, 0.0).astype(BF16)
                dp = _nt(dom, vg)
                dcol = _head_col(dd_ref[:, sl], msk, False)
                ds = p * (dp - dcol)
                res.append(_nn(ds.astype(BF16), kg) * 0.125)
            dq = jnp.where(lo, res[0], res[1])
            if has_run:
                dq = dq + rest[0][:, sl]
            dq_ref[:, sl] = dq

    cur = pl.BlockSpec((B, ATTN_WIDTH), lambda r, n: (n, r))
    prev = pl.BlockSpec((B, ATTN_WIDTH), lambda r, n: (jnp.maximum(n - 1, 0), r))
    args = [view(q), view(k), view(k), view(v), view(v), view(do), view(lse), view(dd)]
    specs = [cur, prev, cur, prev, cur, cur, cur, cur]
    if has_run:
        args.append(view(dq_run))
        specs.append(cur)
    out = pl.pallas_call(
        body, name=name, grid=(d, nb), in_specs=specs, out_specs=cur,
        out_shape=jax.ShapeDtypeStruct((L, d * ATTN_WIDTH), F32), compiler_params=_params(2),
    )(*args)
    return out.reshape(S, ATTN_WIDTH)


def _attn_bwd_kv(q, k, v, do, lse, dd, dk_run, dv_run, d, name):
    S = q.shape[0]
    L = S // d
    nb = L // ATTN_BLOCK
    B = ATTN_BLOCK
    view = lambda t: t.reshape(L, d * ATTN_WIDTH)
    has_run = dk_run is not None

    def body(k_ref, v_ref, qc_ref, qn_ref, doc_ref, don_ref, lsec_ref, lsen_ref, ddc_ref, ddn_ref, *rest):
        dk_ref, dv_ref = rest[-2], rest[-1]
        j = pl.program_id(1)
        qrow = lax.broadcasted_iota(jnp.int32, (2 * B, B), 0)
        kk = lax.broadcasted_iota(jnp.int32, (2 * B, B), 1)
        steps = qrow - kk
        valid = (steps >= 0) & (steps <= B) & ((qrow < B) | (j < nb - 1))
        stepsf = steps.astype(F32)
        lo2 = lax.broadcasted_iota(jnp.int32, (2 * B, 128), 1) < 64
        lo = lax.broadcasted_iota(jnp.int32, (B, 128), 1) < 64
        for G in range(4):
            sl = slice(G * 128, (G + 1) * 128)
            kg = k_ref[:, sl]
            vg = v_ref[:, sl]
            qq = jnp.concatenate([qc_ref[:, sl], qn_ref[:, sl]], axis=0)
            doo = jnp.concatenate([doc_ref[:, sl], don_ref[:, sl]], axis=0)
            lse2 = jnp.concatenate([lsec_ref[:, sl], lsen_ref[:, sl]], axis=0)
            dd2 = jnp.concatenate([ddc_ref[:, sl], ddn_ref[:, sl]], axis=0)
            doo_b = doo.astype(BF16)
            dks, dvs = [], []
            for half in (0, 1):
                msk = lo2 if half == 0 else jnp.logical_not(lo2)
                qm = jnp.where(msk, qq, jnp.zeros_like(qq))
                s = _nt(qm, kg) - (_slope(2 * G + half) * d) * stepsf
                lse_c = _head_col(lse2, msk, True)
                p = jnp.where(valid, jnp.exp(jnp.where(valid, s, NEG_BIG) - lse_c), 0.0)
                dvs.append(_tn(p.astype(BF16), doo_b))
                dom = jnp.where(msk, doo, 0.0).astype(BF16)
                dp = _nt(dom, vg)
                dcol = _head_col(dd2, msk, False)
                ds = p * (dp - dcol)
                dks.append(_tn(ds.astype(BF16), qq))
            dk = jnp.where(lo, dks[0], dks[1])
            dv = jnp.where(lo, dvs[0], dvs[1])
            if has_run:
                dk = dk + rest[0][:, sl]
                dv = dv + rest[1][:, sl]
            dk_ref[:, sl] = dk
            dv_ref[:, sl] = dv

    cur = pl.BlockSpec((B, ATTN_WIDTH), lambda r, j: (j, r))
    nxt = pl.BlockSpec((B, ATTN_WIDTH), lambda r, j: (jnp.minimum(j + 1, nb - 1), r))
    args = [view(k), view(v), view(q), view(q), view(do), view(do), view(lse), view(lse), view(dd), view(dd)]
    specs = [cur, cur, cur, nxt, cur, nxt, cur, nxt, cur, nxt]
    if has_run:
        args += [view(dk_run), view(dv_run)]
        specs += [cur, cur]
    outs = pl.pallas_call(
        body, name=name, grid=(d, nb), in_specs=specs, out_specs=[cur, cur],
        out_shape=[jax.ShapeDtypeStruct((L, d * ATTN_WIDTH), F32)] * 2, compiler_params=_params(2),
    )(*args)
    return [o.reshape(S, ATTN_WIDTH) for o in outs]


CONV_T = 512
HALO = 8


def _conv_taps(pad_ref, w, T):
    acc = pad_ref[pl.ds(HALO - 3, T), :] * w[0:1, :]
    for j in range(1, CONV_WIDTH):
        acc = acc + pad_ref[pl.ds(HALO - 3 + j, T), :] * w[j:j + 1, :]
    return acc


def _conv_fwd(xq, xk, xv, conv_w):
    S = xq.shape[0]
    T = CONV_T

    def body(xq_ref, xqh_ref, xk_ref, xkh_ref, xv_ref, xvh_ref, wq_ref, wk_ref, wv_ref,
             qn_ref, kn_ref, v_ref, pad_ref):
        i = pl.program_id(0)

        def act(x_ref, xh_ref, w_ref):
            pad_ref[pl.ds(0, HALO), :] = jnp.where(i > 0, xh_ref[...], 0.0)
            pad_ref[pl.ds(HALO, T), :] = x_ref[...]
            c = _conv_taps(pad_ref, w_ref[...], T)
            return c * _sigmoid(c)

        def l2n(t):
            return t * lax.rsqrt(jnp.sum(t * t, axis=-1, keepdims=True) + L2_EPS)

        qn_ref[...] = l2n(act(xq_ref, xqh_ref, wq_ref))
        kn_ref[...] = l2n(act(xk_ref, xkh_ref, wk_ref))
        v_ref[...] = act(xv_ref, xvh_ref, wv_ref)

    tile = pl.BlockSpec((T, 128), lambda i, h: (i, h))
    halo = pl.BlockSpec((HALO, 128), lambda i, h: (jnp.maximum(i * (T // HALO) - 1, 0), h))
    wspec = lambda sec: pl.BlockSpec((CONV_WIDTH, 128), lambda i, h, sec=sec: (0, 4 * sec + h))
    return pl.pallas_call(
        body, name="dn_conv_fwd", grid=(S // T, DN_HEADS),
        in_specs=[tile, halo, tile, halo, tile, halo, wspec(0), wspec(1), wspec(2)],
        out_specs=[tile, tile, tile],
        out_shape=[jax.ShapeDtypeStruct((S, DN_WIDTH), F32)] * 3,
        scratch_shapes=[pltpu.VMEM((T + HALO, 128), F32)],
        compiler_params=_params(2),
    )(xq, xq, xk, xk, xv, xv, conv_w, conv_w, conv_w)


def _conv_bwd_pre(xq, xk, xv, conv_w, dqn, dkn, dv):
    S = xq.shape[0]
    T = CONV_T

    def body(xq_ref, xqh_ref, xk_ref, xkh_ref, xv_ref, xvh_ref, wq_ref, wk_ref, wv_ref,
             dqn_ref, dkn_ref, dv_ref, dcq_ref, dck_ref, dcv_ref, dwq_ref, dwk_ref, dwv_ref, pad_ref):
        i = pl.program_id(1)

        def one(x_ref, xh_ref, w_ref, dy_ref, dc_ref, dw_ref, normed):
            pad_ref[pl.ds(0, HALO), :] = jnp.where(i > 0, xh_ref[...], 0.0)
            pad_ref[pl.ds(HALO, T), :] = x_ref[...]
            c = _conv_taps(pad_ref, w_ref[...], T)
            sg = _sigmoid(c)
            a = c * sg
            dy = dy_ref[...]
            if normed:
                r = lax.rsqrt(jnp.sum(a * a, axis=-1, keepdims=True) + L2_EPS)
                y = a * r
                da = r * (dy - y * jnp.sum(dy * y, axis=-1, keepdims=True))
            else:
                da = dy
            dc = da * (sg * (1.0 + c * (1.0 - sg)))
            dc_ref[...] = dc

            @pl.when(i == 0)
            def _():
                dw_ref[...] = jnp.zeros_like(dw_ref)

            rows = [jnp.sum(dc * pad_ref[pl.ds(HALO - 3 + j, T), :], axis=0, keepdims=True) for j in range(CONV_WIDTH)]
            dw_ref[...] += jnp.concatenate(rows + [jnp.zeros((8 - CONV_WIDTH, 128), F32)], axis=0)

        one(xq_ref, xqh_ref, wq_ref, dqn_ref, dcq_ref, dwq_ref, True)
        one(xk_ref, xkh_ref, wk_ref, dkn_ref, dck_ref, dwk_ref, True)
        one(xv_ref, xvh_ref, wv_ref, dv_ref, dcv_ref, dwv_ref, False)

    tile = pl.BlockSpec((T, 128), lambda h, i: (i, h))
    halo = pl.BlockSpec((HALO, 128), lambda h, i: (jnp.maximum(i * (T // HALO) - 1, 0), h))
    wspec = lambda sec: pl.BlockSpec((CONV_WIDTH, 128), lambda h, i, sec=sec: (0, 4 * sec + h))
    dwspec = pl.BlockSpec((8, 128), lambda h, i: (0, h))
    return pl.pallas_call(
        body, name="dn_conv_bwd_pre", grid=(DN_HEADS, S // T),
        in_specs=[tile, halo, tile, halo, tile, halo, wspec(0), wspec(1), wspec(2), tile, tile, tile],
        out_specs=[tile, tile, tile, dwspec, dwspec, dwspec],
        out_shape=[jax.ShapeDtypeStruct((S, DN_WIDTH), F32)] * 3 + [jax.ShapeDtypeStruct((8, DN_WIDTH), F32)] * 3,
        scratch_shapes=[pltpu.VMEM((T + HALO, 128), F32)],
        compiler_params=_params(2),
    )(xq, xq, xk, xk, xv, xv, conv_w, conv_w, conv_w, dqn, dkn, dv)


def _conv_bwd_x(dcq, dck, dcv, conv_w):
    S = dcq.shape[0]
    T = CONV_T
    nt = S // T

    def body(dq_ref, dqh_ref, dk_ref, dkh_ref, dv_ref, dvh_ref, wq_ref, wk_ref, wv_ref,
             oq_ref, ok_ref, ov_ref, pad_ref):
        i = pl.program_id(0)

        def one(d_ref, dh_ref, w_ref, o_ref):
            pad_ref[pl.ds(0, T), :] = d_ref[...]
            pad_ref[pl.ds(T, HALO), :] = jnp.where(i < nt - 1, dh_ref[...], 0.0)
            w = w_ref[...]
            acc = pad_ref[pl.ds(3, T), :] * w[0:1, :]
            for j in range(1, CONV_WIDTH):
                acc = acc + pad_ref[pl.ds(3 - j, T), :] * w[j:j + 1, :]
            o_ref[...] = acc

        one(dq_ref, dqh_ref, wq_ref, oq_ref)
        one(dk_ref, dkh_ref, wk_ref, ok_ref)
        one(dv_ref, dvh_ref, wv_ref, ov_ref)

    tile = pl.BlockSpec((T, 128), lambda i, h: (i, h))
    halo = pl.BlockSpec((HALO, 128), lambda i, h: (jnp.minimum((i + 1) * (T // HALO), S // HALO - 1), h))
    wspec = lambda sec: pl.BlockSpec((CONV_WIDTH, 128), lambda i, h, sec=sec: (0, 4 * sec + h))
    return pl.pallas_call(
        body, name="dn_conv_bwd_x", grid=(nt, DN_HEADS),
        in_specs=[tile, halo, tile, halo, tile, halo, wspec(0), wspec(1), wspec(2)],
        out_specs=[tile, tile, tile],
        out_shape=[jax.ShapeDtypeStruct((S, DN_WIDTH), F32)] * 3,
        scratch_shapes=[pltpu.VMEM((T + HALO, 128), F32)],
        compiler_params=_params(2),
    )(dcq, dcq, dck, dck, dcv, dcv, conv_w, conv_w, conv_w)


def _tri_inverse(a, blk, eye):
    mm = functools.partial(_nn, precision=HI)
    dg = jnp.where(blk, a, 0.0)
    lo = a - dg
    d2 = mm(dg, dg)
    d4 = mm(d2, d2)
    d8 = mm(d4, d4)
    td = mm(mm(mm(eye - dg, eye + d2), eye + d4), eye + d8)
    b = mm(td, lo)
    b2 = mm(b, b)
    tb = mm(eye - b, eye + b2)
    return mm(tb, td)


def _dn_chunk_common(bd, avec, dvec, h, q_raw, k, v):
    C = DN_CHUNK
    lane = lax.broadcasted_iota(jnp.int32, (C, 128), 1)
    row = lax.broadcasted_iota(jnp.int32, (C, C), 0)
    col = lax.broadcasted_iota(jnp.int32, (C, C), 1)
    incl = row >= col
    strict = row > col
    eye = (row == col).astype(F32)
    blk = (row // 16) == (col // 16)
    beta_all = _sigmoid(bd)
    z = bd + dvec
    sp = jnp.maximum(z, 0.0) + jnp.log(1.0 + jnp.exp(-jnp.abs(z)))
    g_all = -jnp.exp(avec) * sp
    pick = lambda t, ln: jnp.sum(jnp.where(lane == ln, t, 0.0), axis=-1, keepdims=True)
    beta = pick(beta_all, h)
    graw = pick(g_all, DN_HEADS + h)
    zc = pick(z, DN_HEADS + h)
    to_row = lambda c: jnp.sum(eye * c, axis=0, keepdims=True)
    gc = jnp.sum(jnp.where(incl, to_row(graw), 0.0), axis=-1, keepdims=True)
    gc_row = to_row(gc)
    decay = jnp.exp(jnp.where(incl, gc - gc_row, NEG_BIG))
    q = q_raw * (DN_HEAD_DIM ** -0.5)
    kb = k * beta
    kk = _nt(kb, k, HI)
    a = jnp.where(strict, kk * decay, 0.0)
    t = _tri_inverse(a, blk, eye)
    eg = jnp.exp(gc)
    rhs_u = v * beta
    rhs_w = kb * eg
    u = _nn(t, rhs_u, HI)
    w = _nn(t, rhs_w, HI)
    qk = _nt(q, k, HI)
    aq = jnp.where(incl, qk * decay, 0.0)
    g_last = jnp.sum(jnp.where(lax.broadcasted_iota(jnp.int32, (C, 1), 0) == C - 1, gc, 0.0), axis=0, keepdims=True)
    ekd = jnp.exp(g_last - gc)
    return dict(beta=beta, graw=graw, zc=zc, gc=gc, decay=decay, q=q, kb=kb, kk=kk, t=t, eg=eg, rhs_w=rhs_w,
                u=u, w=w, qk=qk, aq=aq, g_last=g_last, ekd=ekd, kd=k * ekd, qg=q * eg,
                incl=incl, strict=strict, eye=eye, lane=lane, row=row, col=col)


def _dn_fwd(qn, kn, v, bd, gate, avec, dvec, dn_gain):
    S = qn.shape[0]
    C = DN_CHUNK
    N = S // C
    HD = DN_HEAD_DIM

    def body(q_ref, k_ref, v_ref, bd_ref, gate_ref, a_ref, d_ref, gain_ref, dn_ref, o_ref, st_ref, state_ref):
        @pl.when(pl.program_id(0) == 0)
        def _():
            state_ref[...] = jnp.zeros_like(state_ref)

        st_ref[...] = state_ref[...]
        bd = bd_ref[...]
        for h in range(DN_HEADS):
            sl = slice(h * HD, (h + 1) * HD)
            k = k_ref[:, sl]
            c = _dn_chunk_common(bd, a_ref[...], d_ref[...], h, q_ref[:, sl], k, v_ref[:, sl])
            st = state_ref[sl, :]
            v_new = c["u"] - _nn(c["w"], st, HI)
            o = _nn(c["qg"], st, HI) + _nn(c["aq"], v_new, HI)
            state_ref[sl, :] = st * jnp.exp(c["g_last"]) + _tn(c["kd"], v_new, HI)
            o_ref[:, sl] = o
            r = lax.rsqrt(jnp.mean(o * o, axis=-1, keepdims=True) + NORM_EPS)
            gt = gate_ref[:, sl]
            dn_ref[:, sl] = o * r * gain_ref[...] * (gt * _sigmoid(gt))

    tok = lambda w: pl.BlockSpec((C, w), lambda n: (n, 0))
    vec = pl.BlockSpec((1, 128), lambda n: (0, 0))
    return pl.pallas_call(
        body, name="dn_fwd", grid=(N,),
        in_specs=[tok(DN_WIDTH)] * 3 + [tok(128), tok(DN_WIDTH), vec, vec, vec],
        out_specs=[tok(DN_WIDTH), tok(DN_WIDTH), pl.BlockSpec((DN_WIDTH, HD), lambda n: (n, 0))],
        out_shape=[jax.ShapeDtypeStruct((S, DN_WIDTH), F32)] * 2 + [jax.ShapeDtypeStruct((N * DN_WIDTH, HD), F32)],
        scratch_shapes=[pltpu.VMEM((DN_WIDTH, HD), F32)],
        compiler_params=_params(1),
    )(qn, kn, v, bd, gate, avec, dvec, dn_gain)


def _dn_bwd(qn, kn, v, bd, gate, avec, dvec, dn_gain, o, states, ddn):
    S = qn.shape[0]
    C = DN_CHUNK
    N = S // C
    HD = DN_HEAD_DIM

    def body(q_ref, k_ref, v_ref, bd_ref, gate_ref, a_ref, d_ref, gain_ref, o_ref, st_ref, ddn_ref,
             dq_ref, dk_ref, dv_ref, dgate_ref, dbd_ref, small_ref, dstate_ref):
        @pl.when(pl.program_id(0) == 0)
        def _():
            dstate_ref[...] = jnp.zeros_like(dstate_ref)
            small_ref[...] = jnp.zeros_like(small_ref)

        bd = bd_ref[...]
        avec = a_ref[...]
        gain = gain_ref[...]
        dbd = jnp.zeros((C, 128), F32)
        d_alog = jnp.zeros((1, 128), F32)
        d_dt = jnp.zeros((1, 128), F32)
        d_gain = jnp.zeros((1, 128), F32)
        for h in range(DN_HEADS):
            sl = slice(h * HD, (h + 1) * HD)
            k = k_ref[:, sl]
            vv = v_ref[:, sl]
            c = _dn_chunk_common(bd, avec, d_ref[...], h, q_ref[:, sl], k, vv)
            q, kb, t, eg, u, w = c["q"], c["kb"], c["t"], c["eg"], c["u"], c["w"]
            beta, decay, incl, strict, eye = c["beta"], c["decay"], c["incl"], c["strict"], c["eye"]
            lane = c["lane"]
            st = st_ref[sl, :]
            dsn = dstate_ref[sl, :]
            v_new = u - _nn(w, st, HI)
            ov = o_ref[:, sl]
            r = lax.rsqrt(jnp.mean(ov * ov, axis=-1, keepdims=True) + NORM_EPS)
            on = ov * r
            gt = gate_ref[:, sl]
            sgt = _sigmoid(gt)
            silu_g = gt * sgt
            dy = ddn_ref[:, sl]
            d_gain = d_gain + jnp.sum(dy * on * silu_g, axis=0, keepdims=True)
            dgate_ref[:, sl] = dy * on * gain * (sgt * (1.0 + gt * (1.0 - sgt)))
            don = dy * gain * silu_g
            do = r * (don - on * jnp.mean(don * on, axis=-1, keepdims=True))
            egl = jnp.exp(c["g_last"])
            d_vnew = _tn(c["aq"], do, HI) + _nn(c["kd"], dsn, HI)
            daq = jnp.where(incl, _nt(do, v_new, HI), 0.0)
            d_qg = _nt(do, st, HI)
            d_kd = _nt(v_new, dsn, HI)
            dstate_ref[sl, :] = _tn(c["qg"], do, HI) + egl * dsn - _tn(w, d_vnew, HI)
            d_glast = jnp.sum(jnp.sum(dsn * st, axis=-1, keepdims=True), axis=0, keepdims=True) * egl
            d_w = -_nt(d_vnew, st, HI)
            d_ru = _tn(t, d_vnew, HI)
            d_rw = _tn(t, d_w, HI)
            da = -jnp.where(strict, _nt(d_ru, u, HI) + _nt(d_rw, w, HI), 0.0)
            dv_ref[:, sl] = d_ru * beta
            dbeta = jnp.sum(d_ru * vv, axis=-1, keepdims=True)
            dkb = d_rw * eg
            dgc = jnp.sum(d_rw * c["rhs_w"], axis=-1, keepdims=True)
            dkk = da * decay
            ddecay = da * c["kk"]
            dkb = dkb + _nn(dkk, k, HI)
            dk = _tn(dkk, kb, HI)
            dqk = daq * decay
            ddecay = ddecay + daq * c["qk"]
            dq = _nn(dqk, k, HI)
            dk = dk + _tn(dqk, q, HI)
            m = ddecay * decay
            col_sum = jnp.sum(m, axis=0, keepdims=True)
            dgc = dgc + jnp.sum(m, axis=-1, keepdims=True) - jnp.sum(eye * col_sum, axis=-1, keepdims=True)
            dq = dq + d_qg * eg
            dgc = dgc + jnp.sum(d_qg * c["qg"], axis=-1, keepdims=True)
            dk = dk + d_kd * c["ekd"]
            tk = jnp.sum(d_kd * c["kd"], axis=-1, keepdims=True)
            dgc = dgc - tk
            d_glast = d_glast + jnp.sum(tk, axis=0, keepdims=True)
            dk = dk + dkb * beta
            dbeta = dbeta + jnp.sum(dkb * k, axis=-1, keepdims=True)
            dgc = dgc + jnp.where(lax.broadcasted_iota(jnp.int32, (C, 1), 0) == C - 1, d_glast, 0.0)
            dgc_row = jnp.sum(eye * dgc, axis=0, keepdims=True)
            dgraw = jnp.sum(jnp.where(c["col"] >= c["row"], dgc_row, 0.0), axis=-1, keepdims=True)
            dq_ref[:, sl] = dq * (HD ** -0.5)
            dk_ref[:, sl] = dk
            dbraw = dbeta * beta * (1.0 - beta)
            dz = dgraw * (-jnp.exp(avec)) * _sigmoid(c["zc"])
            dbd = dbd + jnp.where(lane == h, dbraw, 0.0) + jnp.where(lane == DN_HEADS + h, dz, 0.0)
            lane1 = lax.broadcasted_iota(jnp.int32, (1, 128), 1)
            d_alog = d_alog + jnp.where(lane1 == DN_HEADS + h, jnp.sum(dgraw * c["graw"], axis=0, keepdims=True), 0.0)
            d_dt = d_dt + jnp.where(lane1 == DN_HEADS + h, jnp.sum(dz, axis=0, keepdims=True), 0.0)
        dbd_ref[...] = dbd
        small_ref[...] += jnp.concatenate([d_alog, d_dt, d_gain, jnp.zeros((5, 128), F32)], axis=0)

    rev = lambda w: pl.BlockSpec((C, w), lambda i: (N - 1 - i, 0))
    vec = pl.BlockSpec((1, 128), lambda i: (0, 0))
    return pl.pallas_call(
        body, name="dn_bwd", grid=(N,),
        in_specs=[rev(DN_WIDTH)] * 3 + [rev(128), rev(DN_WIDTH), vec, vec, vec, rev(DN_WIDTH),
                                       pl.BlockSpec((DN_WIDTH, HD), lambda i: (N - 1 - i, 0)), rev(DN_WIDTH)],
        out_specs=[rev(DN_WIDTH)] * 4 + [rev(128), pl.BlockSpec((8, 128), lambda i: (0, 0))],
        out_shape=[jax.ShapeDtypeStruct((S, DN_WIDTH), F32)] * 4 + [jax.ShapeDtypeStruct((S, 128), F32),
                                                                  jax.ShapeDtypeStruct((8, 128), F32)],
        scratch_shapes=[pltpu.VMEM((DN_WIDTH, HD), F32)],
        compiler_params=_params(1),
    )(qn, kn, v, bd, gate, avec, dvec, dn_gain, o, states, ddn)


def _outproj_fwd(x, attn, dn, w_out):
    S, D = x.shape
    tm = 512

    def body(x_ref, a_ref, d_ref, w_ref, xo_ref, mix_ref):
        a = a_ref[...].astype(BF16)
        dd = d_ref[...].astype(BF16)
        mix_ref[:, 0:ATTN_WIDTH] = a
        mix_ref[:, ATTN_WIDTH:] = dd
        xo_ref[...] = x_ref[...] + _nn(a, w_ref[0:ATTN_WIDTH, :]) + _nn(dd, w_ref[ATTN_WIDTH:, :])

    tok = lambda w: pl.BlockSpec((tm, w), lambda i: (i, 0))
    return pl.pallas_call(
        body, name="outproj_fwd", grid=(S // tm,),
        in_specs=[tok(D), tok(ATTN_WIDTH), tok(DN_WIDTH), pl.BlockSpec((D, D), lambda i: (0, 0))],
        out_specs=[tok(D), tok(D)],
        out_shape=[jax.ShapeDtypeStruct((S, D), F32), jax.ShapeDtypeStruct((S, D), BF16)],
        compiler_params=_params(1),
    )(x, attn, dn, w_out)


def _outproj_bwd(dx, w_outT):
    S, D = dx.shape
    tm = 512

    def body(dx_ref, wT_ref, da_ref, dd_ref, dxb_ref):
        d = dx_ref[...].astype(BF16)
        dxb_ref[...] = d
        da_ref[...] = _nn(d, wT_ref[:, 0:ATTN_WIDTH])
        dd_ref[...] = _nn(d, wT_ref[:, ATTN_WIDTH:])

    tok = lambda w: pl.BlockSpec((tm, w), lambda i: (i, 0))
    return pl.pallas_call(
        body, name="outproj_bwd", grid=(S // tm,),
        in_specs=[tok(D), pl.BlockSpec((D, D), lambda i: (0, 0))],
        out_specs=[tok(ATTN_WIDTH), tok(DN_WIDTH), tok(D)],
        out_shape=[jax.ShapeDtypeStruct((S, ATTN_WIDTH), F32), jax.ShapeDtypeStruct((S, DN_WIDTH), F32),
                   jax.ShapeDtypeStruct((S, D), BF16)],
        compiler_params=_params(1),
    )(dx, w_outT)


def _loss_head(x, gain, target):
    S, D = x.shape
    tm = 512

    def body(x_ref, gain_ref, t_ref, loss_ref, dx_ref, dgain_ref):
        @pl.when(pl.program_id(0) == 0)
        def _():
            loss_ref[...] = jnp.zeros_like(loss_ref)
            dgain_ref[...] = jnp.zeros_like(dgain_ref)

        xf = x_ref[...]
        gain = gain_ref[...]
        r = lax.rsqrt(jnp.mean(xf * xf, axis=-1, keepdims=True) + NORM_EPS)
        xhat = xf * r
        err = xhat * gain - t_ref[...]
        part = 0.5 * jnp.sum(jnp.mean(err * err, axis=-1, keepdims=True), axis=0, keepdims=True)
        first = (lax.broadcasted_iota(jnp.int32, (8, 128), 0) == 0) & (lax.broadcasted_iota(jnp.int32, (8, 128), 1) == 0)
        loss_ref[...] += jnp.where(first, part, 0.0)
        dy = err * (1.0 / D)
        dgain_ref[...] += jnp.sum(dy * xhat, axis=0, keepdims=True)
        dxh = dy * gain
        dx_ref[...] = r * (dxh - xhat * jnp.mean(dxh * xhat, axis=-1, keepdims=True))

    tok = pl.BlockSpec((tm, D), lambda i: (i, 0))
    row = pl.BlockSpec((1, D), lambda i: (0, 0))
    return pl.pallas_call(
        body, name="loss_head", grid=(S // tm,),
        in_specs=[tok, row, tok],
        out_specs=[pl.BlockSpec((8, 128), lambda i: (0, 0)), tok, row],
        out_shape=[jax.ShapeDtypeStruct((8, 128), F32), jax.ShapeDtypeStruct((S, D), F32),
                   jax.ShapeDtypeStruct((1, D), F32)],
        compiler_params=_params(1),
    )(x, gain, target)


def _adamw(w, g, m, v, name):
    R, Ccols = w.shape
    tr = R
    for cand in (256, 128, 64, 32, 16, 8):
        if R % cand == 0:
            tr = cand
            break
    c1 = 1.0 - ADAM_B1 ** ADAM_STEP
    c2 = 1.0 - ADAM_B2 ** ADAM_STEP

    def body(w_ref, g_ref, m_ref, v_ref, d_ref, nm_ref, nv_ref):
        gv = g_ref[...]
        mn = ADAM_B1 * m_ref[...] + (1.0 - ADAM_B1) * gv
        vn = ADAM_B2 * v_ref[...] + (1.0 - ADAM_B2) * (gv * gv)
        nm_ref[...] = mn
        nv_ref[...] = vn
        d_ref[...] = -ADAM_LR * ((mn / c1) / (jnp.sqrt(vn / c2) + ADAM_EPS) + ADAM_WD * w_ref[...])

    spec = pl.BlockSpec((tr, Ccols), lambda i: (i, 0))
    return pl.pallas_call(
        body, name=name, grid=(R // tr,), in_specs=[spec] * 4, out_specs=[spec] * 3,
        out_shape=[jax.ShapeDtypeStruct((R, Ccols), F32)] * 3, compiler_params=_params(1),
    )(w, g, m, v)


def _local_step(x, target, wts, small):
    T = lambda a: a.T
    g1, g2, gm, gf = small["norm_ffn1"], small["norm_ffn2"], small["norm_mix"], small["norm_final"]

    x1, h1, fg1, fu1 = _ffn_fwd(x, g1, wts["ffn1_gate"], wts["ffn1_up"], wts["ffn1_down"], "ffn1_fwd")
    h2, aq, ak, av, xq, xk, xv, gate, bd = _inproj_fwd(x1, gm, wts["w_in"])
    parts = [_attn_fwd(aq, ak, av, d, f"attn_fwd_d{d}") for d in DILATIONS]
    attn, lse = _attn_merge(parts)
    conv_w = small["conv_w"]
    qn, kn, vv = _conv_fwd(xq, xk, xv, conv_w)
    dn, o_dn, states = _dn_fwd(qn, kn, vv, bd, gate, small["avec"], small["dvec"], small["dn_norm"])
    x2, mix = _outproj_fwd(x1, attn, dn, wts["w_out"])
    x3, h3, fg2, fu2 = _ffn_fwd(x2, g2, wts["ffn2_gate"], wts["ffn2_up"], wts["ffn2_down"], "ffn2_fwd")
    loss, dx3, d_gf = _loss_head(x3, gf, target)

    grads = {}
    dx2, d_g2, dfg2, dfu2, act2, dout2 = _ffn_bwd(dx3, x2, g2, fg2, fu2, T(wts["ffn2_down"]), T(wts["ffn2_gate"]),
                                                 T(wts["ffn2_up"]), "ffn2_bwd")
    S = x.shape[0]
    tk = 1024
    grads["ffn2_gate"] = _matmul(T(h3), dfg2, 512, D_FF // 2, tk, "dw_ffn2_gate")
    grads["ffn2_up"] = _matmul(T(h3), dfu2, 512, D_FF // 2, tk, "dw_ffn2_up")
    grads["ffn2_down"] = _matmul(T(act2), dout2, D_FF // 2, D_MODEL, tk, "dw_ffn2_down")

    dattn, ddn, dx2b = _outproj_bwd(dx2, T(wts["w_out"]))
    grads["w_out"] = _matmul(T(mix), dx2b, 512, D_MODEL, tk, "dw_out")

    dd = _attn_delta(dattn, attn)
    daq = dak = dav = None
    for d in DILATIONS:
        daq = _attn_bwd_q(aq, ak, av, dattn, lse, dd, daq, d, f"attn_bwd_q_d{d}")
        dak, dav = _attn_bwd_kv(aq, ak, av, dattn, lse, dd, dak, dav, d, f"attn_bwd_kv_d{d}")

    dqn, dkn, dvv, dgate, dbd, dn_small = _dn_bwd(qn, kn, vv, bd, gate, small["avec"], small["dvec"], small["dn_norm"],
                                                o_dn, states, ddn)
    dcq, dck, dcv, dwq, dwk, dwv = _conv_bwd_pre(xq, xk, xv, conv_w, dqn, dkn, dvv)
    dxq, dxk, dxv = _conv_bwd_x(dcq, dck, dcv, conv_w)
    d_conv = jnp.concatenate([dwq[:CONV_WIDTH], dwk[:CONV_WIDTH], dwv[:CONV_WIDTH]], axis=1)

    dx1, d_gm, dproj = _inproj_bwd(dx2, x1, gm, [daq, dak, dav, dxq, dxk, dxv, dgate], dbd, T(wts["w_in"]))
    grads["w_in"] = _matmul(T(h2), dproj, 512, IN_COLS_PADDED, tk, "dw_in")

    dx0, d_g1, dfg1, dfu1, act1, dout1 = _ffn_bwd(dx1, x, g1, fg1, fu1, T(wts["ffn1_down"]), T(wts["ffn1_gate"]),
                                                 T(wts["ffn1_up"]), "ffn1_bwd")
    grads["ffn1_gate"] = _matmul(T(h1), dfg1, 512, D_FF // 2, tk, "dw_ffn1_gate")
    grads["ffn1_up"] = _matmul(T(h1), dfu1, 512, D_FF // 2, tk, "dw_ffn1_up")
    grads["ffn1_down"] = _matmul(T(act1), dout1, D_FF // 2, D_MODEL, tk, "dw_ffn1_down")

    small_grads = dict(norm_ffn1=d_g1, norm_mix=d_gm, norm_ffn2=d_g2, norm_final=d_gf, conv_w=d_conv,
                       a_log=dn_small[0:1], dt_bias=dn_small[1:2], dn_norm=dn_small[2:3])
    return loss, dx0, grads, small_grads


PACK_SECTIONS = (("ffn1_gate", 704), ("ffn1_up", 704), ("ffn1_down", 704), ("w_in", 898), ("w_out", 256),
                 ("ffn2_gate", 704), ("ffn2_up", 704), ("ffn2_down", 704))
PACK_ROWS = 5408
HALF_ROWS = PACK_ROWS // 2
ADD_ROWS = 208

HBM = pl.BlockSpec(memory_space=pl.ANY)
VMEM_SPEC = pl.BlockSpec(memory_space=pltpu.VMEM)


def _coords():
    return lax.axis_index("x"), lax.axis_index("y"), lax.axis_index("c")


def _remote(src, dst, send_sems, recv_sems, k, dev):
    return pltpu.make_async_remote_copy(src_ref=src, dst_ref=dst, send_sem=send_sems.at[k], recv_sem=recv_sems.at[k],
                                        device_id=dev, device_id_type=MESH)


def _allreduce_small(buf, name):
    R, Cc = buf.shape

    def body(src_ref, out_ref, recv_ref, send_sems, recv_sems):
        x, y, c = _coords()
        copies = []
        for m in range(1, 8):
            fx, fy, fc = (m >> 2) & 1, (m >> 1) & 1, m & 1
            dev = (x ^ fx if fx else x, y ^ fy if fy else y, c ^ fc if fc else c)
            cp = _remote(src_ref, recv_ref.at[m - 1], send_sems, recv_sems, m - 1, dev)
            cp.start()
            copies.append(cp)
        for cp in copies:
            cp.wait()
        r = [src_ref[...]] + [recv_ref[m] for m in range(7)]
        out_ref[...] = ((r[0] + r[1]) + (r[2] + r[3])) + ((r[4] + r[5]) + (r[6] + r[7]))

    return pl.pallas_call(
        body, name=name, out_shape=jax.ShapeDtypeStruct((R, Cc), F32),
        in_specs=[VMEM_SPEC], out_specs=VMEM_SPEC,
        scratch_shapes=[pltpu.VMEM((7, R, Cc), F32), pltpu.SemaphoreType.DMA((7,)), pltpu.SemaphoreType.DMA((7,))],
    )(buf)


def _allgather_weights(pack2):
    _, Hh, Cc = pack2.shape

    def body(src_ref, out_ref, send_sems, recv_sems, local_sem):
        x, y, c = _coords()
        sib = (x, y, 1 - c)
        others = [(1 - x, y), (x, 1 - y), (1 - x, 1 - y)]
        blk = lambda cx, cy, half: out_ref.at[2 * cx + cy, half]
        mine = pltpu.make_async_copy(src_ref, out_ref.at[2 * x + y], local_sem)
        mine.start()
        first = [_remote(src_ref.at[c], blk(x, y, c), send_sems, recv_sems, j, (ox, oy, c)) for j, (ox, oy) in enumerate(others)]
        for cp in first:
            cp.start()
        passed = [_remote(blk(ox, oy, c), blk(ox, oy, c), send_sems, recv_sems, 3 + j, sib) for j, (ox, oy) in enumerate(others)]
        for j, (ox, oy) in enumerate(others):
            _remote(src_ref.at[c], blk(ox, oy, c), send_sems, recv_sems, j, (ox, oy, c)).wait_recv()
            passed[j].start()
        for j, (ox, oy) in enumerate(others):
            _remote(src_ref.at[c], blk(ox, oy, 1 - c), send_sems, recv_sems, 3 + j, sib).wait_recv()
        for cp in first + passed:
            cp.wait_send()
        mine.wait()

    return pl.pallas_call(
        body, name="allgather_weights", out_shape=jax.ShapeDtypeStruct((N_CHIPS, 2, Hh, Cc), pack2.dtype),
        in_specs=[HBM], out_specs=HBM,
        scratch_shapes=[pltpu.SemaphoreType.DMA((6,)), pltpu.SemaphoreType.DMA((6,)), pltpu.SemaphoreType.DMA],
    )(pack2)


def _rs_swap_halves(gpack):
    _, nj, Hh, Cc = gpack.shape

    def body(src_ref, out_ref, send_sems, recv_sems):
        x, y, c = _coords()
        cp = _remote(src_ref.at[1 - c], out_ref, send_sems, recv_sems, 0, (x, y, 1 - c))
        cp.start()
        cp.wait()

    return pl.pallas_call(
        body, name="rs_swap_halves", out_shape=jax.ShapeDtypeStruct((nj, Hh, Cc), gpack.dtype),
        in_specs=[HBM], out_specs=HBM,
        scratch_shapes=[pltpu.SemaphoreType.DMA((1,)), pltpu.SemaphoreType.DMA((1,))],
    )(gpack)


def _rs_add_pair(gpack, other, c):
    _, nj, Hh, Cc = gpack.shape
    tr = ADD_ROWS

    def body(c_ref, a_ref, b_ref, o_ref):
        o_ref[...] = (a_ref[...] + b_ref[...]).astype(BF16)

    return pl.pallas_call(
        body, name="rs_add_pair",
        grid_spec=pltpu.PrefetchScalarGridSpec(
            num_scalar_prefetch=1, grid=(nj, Hh // tr),
            in_specs=[pl.BlockSpec((None, None, tr, Cc), lambda j, i, c_ref: (c_ref[0], j, i, 0)),
                      pl.BlockSpec((None, tr, Cc), lambda j, i, c_ref: (j, i, 0))],
            out_specs=pl.BlockSpec((None, tr, Cc), lambda j, i, c_ref: (j, i, 0))),
        out_shape=jax.ShapeDtypeStruct((nj, Hh, Cc), BF16),
        compiler_params=_params(2),
    )(c, gpack, other)


def _rs_exchange_chips(part):
    nj, Hh, Cc = part.shape

    def body(src_ref, out_ref, send_sems, recv_sems):
        x, y, c = _coords()
        others = [(1 - x, y), (x, 1 - y), (1 - x, 1 - y)]
        cps = [_remote(src_ref.at[2 * ox + oy], out_ref.at[k], send_sems, recv_sems, k, (ox, oy, c))
               for k, (ox, oy) in enumerate(others)]
        for cp in cps:
            cp.start()
        for cp in cps:
            cp.wait()

    return pl.pallas_call(
        body, name="rs_exchange_chips", out_shape=jax.ShapeDtypeStruct((3, Hh, Cc), part.dtype),
        in_specs=[HBM], out_specs=HBM,
        scratch_shapes=[pltpu.SemaphoreType.DMA((3,)), pltpu.SemaphoreType.DMA((3,))],
    )(part)


def _rs_add_total(part, recv, chip):
    nj, Hh, Cc = part.shape
    tr = ADD_ROWS

    def body(chip_ref, p_ref, r0_ref, r1_ref, r2_ref, o_ref):
        f = lambda r: r[...].astype(F32)
        o_ref[...] = (f(p_ref) + f(r0_ref)) + (f(r1_ref) + f(r2_ref))

    rk = lambda k: pl.BlockSpec((None, tr, Cc), lambda i, chip_ref, k=k: (k, i, 0))
    return pl.pallas_call(
        body, name="rs_add_total",
        grid_spec=pltpu.PrefetchScalarGridSpec(
            num_scalar_prefetch=1, grid=(Hh // tr,),
            in_specs=[pl.BlockSpec((None, tr, Cc), lambda i, chip_ref: (chip_ref[0], i, 0)), rk(0), rk(1), rk(2)],
            out_specs=pl.BlockSpec((tr, Cc), lambda i, chip_ref: (i, 0))),
        out_shape=jax.ShapeDtypeStruct((Hh, Cc), F32),
        compiler_params=_params(1),
    )(chip, part, recv, recv, recv)


def _rs_share_total(total):
    Hh, Cc = total.shape

    def body(src_ref, out_ref, send_sems, recv_sems, local_sem):
        x, y, c = _coords()
        mine = pltpu.make_async_copy(src_ref, out_ref.at[c], local_sem)
        mine.start()
        cp = _remote(src_ref, out_ref.at[c], send_sems, recv_sems, 0, (x, y, 1 - c))
        cp.start()
        cp.wait_send()
        _remote(src_ref, out_ref.at[1 - c], send_sems, recv_sems, 0, (x, y, 1 - c)).wait_recv()
        mine.wait()

    return pl.pallas_call(
        body, name="rs_share_total", out_shape=jax.ShapeDtypeStruct((2, Hh, Cc), total.dtype),
        in_specs=[HBM], out_specs=HBM,
        scratch_shapes=[pltpu.SemaphoreType.DMA((1,)), pltpu.SemaphoreType.DMA((1,)), pltpu.SemaphoreType.DMA],
    )(total)


def _permute_w_in(w):
    return jnp.concatenate([w[:, :3072], w[:, 3080:IN_COLS], w[:, 3072:3080],
                            jnp.zeros((w.shape[0], IN_COLS_PADDED - IN_COLS), w.dtype)], axis=1)


def _unpermute_w_in(w):
    return jnp.concatenate([w[:, :3072], w[:, 3584:3592], w[:, 3072:3584]], axis=1)


def _pack_rows(shards, dtype):
    rows = [shards[n].astype(dtype).reshape(r, D_MODEL) for n, r in PACK_SECTIONS]
    used = sum(r for _, r in PACK_SECTIONS)
    rows.append(jnp.zeros((PACK_ROWS - used, D_MODEL), dtype))
    return jnp.concatenate(rows, axis=0)


def _unpack_rows(pack, shapes):
    out, at = {}, 0
    for n, r in PACK_SECTIONS:
        out[n] = pack[at:at + r].reshape(shapes[n])
        at += r
    return out


SHARD_SHAPES = dict(ffn1_gate=(1024, 704), ffn1_up=(1024, 704), ffn1_down=(704, 1024), w_in=(1024, 898),
                    w_out=(256, 1024), ffn2_gate=(1024, 704), ffn2_up=(1024, 704), ffn2_down=(704, 1024))
ROW_SHARDED = ("ffn1_down", "w_out", "ffn2_down")
SMALL_ROWS = 16


def _pad_row(v):
    v = v.reshape(1, -1)
    return jnp.pad(v, ((0, 0), (0, D_MODEL - v.shape[1])))


def kernel(x, norm_ffn1, ffn1_gate, ffn1_up, ffn1_down, norm_mix, w_in, conv_w, a_log, dt_bias, dn_norm, w_out, norm_ffn2, ffn2_gate, ffn2_up, ffn2_down, norm_final, loss_target, m_norm_ffn1, m_ffn1_gate, m_ffn1_up, m_ffn1_down, m_norm_mix, m_w_in, m_conv_w, m_a_log, m_dt_bias, m_dn_norm, m_w_out, m_norm_ffn2, m_ffn2_gate, m_ffn2_up, m_ffn2_down, m_norm_final, v_norm_ffn1, v_ffn1_gate, v_ffn1_up, v_ffn1_down, v_norm_mix, v_w_in, v_conv_w, v_a_log, v_dt_bias, v_dn_norm, v_w_out, v_norm_ffn2, v_ffn2_gate, v_ffn2_up, v_ffn2_down, v_norm_final):
    cx, cy, cc = _coords()
    chip = 2 * cx + cy
    big_w = dict(ffn1_gate=ffn1_gate[0], ffn1_up=ffn1_up[0], ffn1_down=ffn1_down[0], w_in=w_in[0], w_out=w_out[0],
                 ffn2_gate=ffn2_gate[0], ffn2_up=ffn2_up[0], ffn2_down=ffn2_down[0])
    big_m = dict(ffn1_gate=m_ffn1_gate[0], ffn1_up=m_ffn1_up[0], ffn1_down=m_ffn1_down[0], w_in=m_w_in[0], w_out=m_w_out[0],
                 ffn2_gate=m_ffn2_gate[0], ffn2_up=m_ffn2_up[0], ffn2_down=m_ffn2_down[0])
    big_v = dict(ffn1_gate=v_ffn1_gate[0], ffn1_up=v_ffn1_up[0], ffn1_down=v_ffn1_down[0], w_in=v_w_in[0], w_out=v_w_out[0],
                 ffn2_gate=v_ffn2_gate[0], ffn2_up=v_ffn2_up[0], ffn2_down=v_ffn2_down[0])

    pack = _pack_rows(big_w, BF16).reshape(2, HALF_ROWS, D_MODEL)
    gathered = _allgather_weights(pack).reshape(N_CHIPS, PACK_ROWS, D_MODEL)
    per_chip = [_unpack_rows(gathered[j], SHARD_SHAPES) for j in range(N_CHIPS)]
    wts = {n: jnp.concatenate([per_chip[j][n] for j in range(N_CHIPS)], axis=0 if n in ROW_SHARDED else 1)
           for n, _ in PACK_SECTIONS}
    wts["w_in"] = _permute_w_in(wts["w_in"])

    conv_shard = conv_w[0]
    emb = jnp.concatenate([jnp.where((chip == j) & (cc == 0), conv_shard, 0.0) for j in range(N_CHIPS)], axis=1)
    emb = jnp.pad(emb.reshape(6, D_MODEL), ((0, 2), (0, 0)))
    conv_full = _allreduce_small(emb, "allgather_conv_w")[:6].reshape(CONV_WIDTH, 3 * DN_WIDTH)

    zvec = jnp.zeros((1, 128), F32)
    small = dict(norm_ffn1=norm_ffn1, norm_mix=norm_mix, norm_ffn2=norm_ffn2, norm_final=norm_final[None],
                 conv_w=conv_full, avec=zvec.at[0, DN_HEADS:2 * DN_HEADS].set(a_log[0]),
                 dvec=zvec.at[0, DN_HEADS:2 * DN_HEADS].set(dt_bias[0]), dn_norm=dn_norm)

    loss, grad_x, grads, sg = _local_step(x[0], loss_target[0], wts, small)

    rows = [sg["norm_ffn1"], sg["norm_mix"], sg["norm_ffn2"], sg["norm_final"], _pad_row(sg["a_log"]), _pad_row(sg["dt_bias"]),
            _pad_row(sg["dn_norm"]), _pad_row(loss[0:1]), sg["conv_w"].reshape(6, D_MODEL), jnp.zeros((2, D_MODEL), F32)]
    red = _allreduce_small(jnp.concatenate(rows, axis=0), "allreduce_small")
    loss_out = red[7, 0]
    g_conv_full = red[8:14].reshape(CONV_WIDTH, 3 * DN_WIDTH)
    g_conv = lax.dynamic_slice_in_dim(g_conv_full, chip * (3 * DN_WIDTH // N_CHIPS), 3 * DN_WIDTH // N_CHIPS, axis=1)
    g_small = dict(norm_ffn1=red[0:1], norm_mix=red[1:2], norm_ffn2=red[2:3], norm_final=red[3],
                   a_log=red[4:5, DN_HEADS:2 * DN_HEADS], dt_bias=red[5:6, DN_HEADS:2 * DN_HEADS], dn_norm=red[6:7, :DN_HEAD_DIM])

    grads["w_in"] = _unpermute_w_in(grads["w_in"])
    packs = []
    for j in range(N_CHIPS):
        sh = {}
        for n, _ in PACK_SECTIONS:
            g = grads[n]
            if n in ROW_SHARDED:
                r = SHARD_SHAPES[n][0]
                sh[n] = g[j * r:(j + 1) * r]
            else:
                w = SHARD_SHAPES[n][1]
                sh[n] = g[:, j * w:(j + 1) * w]
        packs.append(_pack_rows(sh, F32).reshape(2, HALF_ROWS, D_MODEL))
    gpack = jnp.stack(packs, axis=1)
    from_sibling = _rs_swap_halves(gpack)
    part = _rs_add_pair(gpack, from_sibling, cc.reshape(1).astype(jnp.int32))
    recv = _rs_exchange_chips(part)
    total = _rs_add_total(part, recv, chip.reshape(1).astype(jnp.int32))
    shard_g = _unpack_rows(_rs_share_total(total).reshape(PACK_ROWS, D_MODEL), SHARD_SHAPES)

    out_g, out_d, out_m, out_v = {}, {}, {}, {}
    for n, _ in PACK_SECTIONS:
        d, nm, nv = _adamw(big_w[n], shard_g[n], big_m[n], big_v[n], "adamw_" + n)
        out_g[n], out_d[n], out_m[n], out_v[n] = shard_g[n][None], d[None], nm[None], nv[None]
    d, nm, nv = _adamw(conv_w[0], g_conv, m_conv_w[0], v_conv_w[0], "adamw_conv_w")
    out_g["conv_w"], out_d["conv_w"], out_m["conv_w"], out_v["conv_w"] = g_conv[None], d[None], nm[None], nv[None]

    small_names = ("norm_ffn1", "norm_mix", "norm_ffn2", "norm_final", "a_log", "dt_bias", "dn_norm")
    small_w = dict(norm_ffn1=norm_ffn1, norm_mix=norm_mix, norm_ffn2=norm_ffn2, norm_final=norm_final, a_log=a_log,
                   dt_bias=dt_bias, dn_norm=dn_norm)
    small_m = dict(norm_ffn1=m_norm_ffn1, norm_mix=m_norm_mix, norm_ffn2=m_norm_ffn2, norm_final=m_norm_final, a_log=m_a_log,
                   dt_bias=m_dt_bias, dn_norm=m_dn_norm)
    small_v = dict(norm_ffn1=v_norm_ffn1, norm_mix=v_norm_mix, norm_ffn2=v_norm_ffn2, norm_final=v_norm_final, a_log=v_a_log,
                   dt_bias=v_dt_bias, dn_norm=v_dn_norm)
    stack = lambda dct: jnp.concatenate([_pad_row(dct[n]) for n in small_names] + [jnp.zeros((1, D_MODEL), F32)], axis=0)
    d, nm, nv = _adamw(stack(small_w), stack(g_small), stack(small_m), stack(small_v), "adamw_small")
    for k, n in enumerate(small_names):
        shape = small_w[n].shape
        size = math.prod(shape)
        out_g[n] = g_small[n].reshape(shape)
        out_d[n], out_m[n], out_v[n] = (t[k, :size].reshape(shape) for t in (d, nm, nv))

    order = ("norm_ffn1", "ffn1_gate", "ffn1_up", "ffn1_down", "norm_mix", "w_in", "conv_w", "a_log", "dt_bias", "dn_norm",
             "w_out", "norm_ffn2", "ffn2_gate", "ffn2_up", "ffn2_down", "norm_final")
    return (loss_out, grad_x[None], *[out_g[n] for n in order], *[out_d[n] for n in order],
            *[out_m[n] for n in order], *[out_v[n] for n in order])
```

```python
import functools
import math

import jax
import jax.numpy as jnp
from jax import lax
from jax.experimental import pallas as pl
from jax.experimental.pallas import tpu as pltpu

F32 = jnp.float32
BF16 = jnp.bfloat16
HI = lax.Precision.HIGHEST

D_MODEL = 1024
D_FF = 2816
ATTN_HEADS = 8
ATTN_WIDTH = 512
ATTN_BLOCK = 128
DILATIONS = (1, 4, 16)
DN_HEADS = 4
DN_HEAD_DIM = 128
DN_WIDTH = 512
DN_CHUNK = 64
CONV_WIDTH = 4
NORM_EPS = 1e-6
L2_EPS = 1e-6
IN_COLS = 3592
IN_COLS_PADDED = 3712
N_CHIPS = 4

ADAM_LR = 0.001
ADAM_B1 = 0.9
ADAM_B2 = 0.999
ADAM_EPS = 1e-08
ADAM_WD = 0.01
ADAM_STEP = 10

VMEM_LIMIT = 56 * 1024 * 1024
NEG_BIG = -1e30
MESH = pl.DeviceIdType.MESH


def _params(n_grid, vmem=VMEM_LIMIT):
    return pltpu.CompilerParams(dimension_semantics=("arbitrary",) * n_grid, vmem_limit_bytes=vmem)


def _nt(a, b, precision=None):
    return lax.dot_general(a, b, (((1,), (1,)), ((), ())), preferred_element_type=F32, precision=precision)


def _tn(a, b, precision=None):
    return lax.dot_general(a, b, (((0,), (0,)), ((), ())), preferred_element_type=F32, precision=precision)


def _nn(a, b, precision=None):
    return jnp.dot(a, b, preferred_element_type=F32, precision=precision)


def _sigmoid(x):
    return 1.0 / (1.0 + jnp.exp(-x))


def _ffn_fwd(x, gain, wg, wu, wd, name):
    S, D = x.shape
    F = wg.shape[1]
    tm, tf = 512, F // 2
    nf = F // tf

    def body(x_ref, gain_ref, wg_ref, wu_ref, wd_ref, xo_ref, h_ref, g_ref, u_ref, acc_ref, hs_ref):
        j = pl.program_id(1)

        @pl.when(j == 0)
        def _():
            xf = x_ref[...]
            r = lax.rsqrt(jnp.mean(xf * xf, axis=-1, keepdims=True) + NORM_EPS)
            h = (xf * r * gain_ref[...]).astype(BF16)
            hs_ref[...] = h
            h_ref[...] = h
            acc_ref[...] = jnp.zeros_like(acc_ref)

        h = hs_ref[...]
        g = _nn(h, wg_ref[...])
        u = _nn(h, wu_ref[...])
        g_ref[...] = g.astype(BF16)
        u_ref[...] = u.astype(BF16)
        act = g * _sigmoid(g) * u
        acc_ref[...] += _nn(act.astype(BF16), wd_ref[...])

        @pl.when(j == nf - 1)
        def _():
            xo_ref[...] = x_ref[...] + 0.5 * acc_ref[...]

    return pl.pallas_call(
        body, name=name, grid=(S // tm, nf),
        in_specs=[pl.BlockSpec((tm, D), lambda i, j: (i, 0)),
                  pl.BlockSpec((1, D), lambda i, j: (0, 0)),
                  pl.BlockSpec((D, tf), lambda i, j: (0, j)),
                  pl.BlockSpec((D, tf), lambda i, j: (0, j)),
                  pl.BlockSpec((tf, D), lambda i, j: (j, 0))],
        out_specs=[pl.BlockSpec((tm, D), lambda i, j: (i, 0)),
                   pl.BlockSpec((tm, D), lambda i, j: (i, 0)),
                   pl.BlockSpec((tm, tf), lambda i, j: (i, j)),
                   pl.BlockSpec((tm, tf), lambda i, j: (i, j))],
        out_shape=[jax.ShapeDtypeStruct((S, D), F32), jax.ShapeDtypeStruct((S, D), BF16),
                   jax.ShapeDtypeStruct((S, F), BF16), jax.ShapeDtypeStruct((S, F), BF16)],
        scratch_shapes=[pltpu.VMEM((tm, D), F32), pltpu.VMEM((tm, D), BF16)],
        compiler_params=_params(2),
    )(x, gain, wg, wu, wd)


def _rmsnorm_bwd(dh, xf, gain):
    r = lax.rsqrt(jnp.mean(xf * xf, axis=-1, keepdims=True) + NORM_EPS)
    xhat = xf * r
    dgain = jnp.sum(dh * xhat, axis=0, keepdims=True)
    dxh = dh * gain
    dx = r * (dxh - xhat * jnp.mean(dxh * xhat, axis=-1, keepdims=True))
    return dx, dgain


def _ffn_bwd(dxo, x, gain, g, u, wdT, wgT, wuT, name):
    S, D = x.shape
    F = g.shape[1]
    tm, tf = 256, F // 2
    nf = F // tf

    def body(dxo_ref, x_ref, gain_ref, g_ref, u_ref, wdT_ref, wgT_ref, wuT_ref,
             dx_ref, dgain_ref, dg_ref, du_ref, act_ref, dout_ref, acc_ref, ds_ref):
        i = pl.program_id(0)
        j = pl.program_id(1)

        @pl.when(j == 0)
        def _():
            d = (0.5 * dxo_ref[...]).astype(BF16)
            ds_ref[...] = d
            dout_ref[...] = d
            acc_ref[...] = jnp.zeros_like(acc_ref)

        @pl.when((i == 0) & (j == 0))
        def _():
            dgain_ref[...] = jnp.zeros_like(dgain_ref)

        dact = _nn(ds_ref[...], wdT_ref[...])
        gv = g_ref[...].astype(F32)
        uv = u_ref[...].astype(F32)
        sg = _sigmoid(gv)
        silu = gv * sg
        act_ref[...] = (silu * uv).astype(BF16)
        dgv = (dact * uv * (sg * (1.0 + gv * (1.0 - sg)))).astype(BF16)
        duv = (dact * silu).astype(BF16)
        dg_ref[...] = dgv
        du_ref[...] = duv
        acc_ref[...] += _nn(dgv, wgT_ref[...]) + _nn(duv, wuT_ref[...])

        @pl.when(j == nf - 1)
        def _():
            dx, dgain = _rmsnorm_bwd(acc_ref[...], x_ref[...], gain_ref[...])
            dx_ref[...] = dxo_ref[...] + dx
            dgain_ref[...] += dgain

    return pl.pallas_call(
        body, name=name, grid=(S // tm, nf),
        in_specs=[pl.BlockSpec((tm, D), lambda i, j: (i, 0)),
                  pl.BlockSpec((tm, D), lambda i, j: (i, 0)),
                  pl.BlockSpec((1, D), lambda i, j: (0, 0)),
                  pl.BlockSpec((tm, tf), lambda i, j: (i, j)),
                  pl.BlockSpec((tm, tf), lambda i, j: (i, j)),
                  pl.BlockSpec((D, tf), lambda i, j: (0, j)),
                  pl.BlockSpec((tf, D), lambda i, j: (j, 0)),
                  pl.BlockSpec((tf, D), lambda i, j: (j, 0))],
        out_specs=[pl.BlockSpec((tm, D), lambda i, j: (i, 0)),
                   pl.BlockSpec((1, D), lambda i, j: (0, 0)),
                   pl.BlockSpec((tm, tf), lambda i, j: (i, j)),
                   pl.BlockSpec((tm, tf), lambda i, j: (i, j)),
                   pl.BlockSpec((tm, tf), lambda i, j: (i, j)),
                   pl.BlockSpec((tm, D), lambda i, j: (i, 0))],
        out_shape=[jax.ShapeDtypeStruct((S, D), F32), jax.ShapeDtypeStruct((1, D), F32),
                   jax.ShapeDtypeStruct((S, F), BF16), jax.ShapeDtypeStruct((S, F), BF16),
                   jax.ShapeDtypeStruct((S, F), BF16), jax.ShapeDtypeStruct((S, D), BF16)],
        scratch_shapes=[pltpu.VMEM((tm, D), F32), pltpu.VMEM((tm, D), BF16)],
        compiler_params=_params(2),
    )(dxo, x, gain, g, u, wdT, wgT, wuT)


def _matmul_tn(a, b, tm, tk, name):
    K, M = a.shape
    N = b.shape[1]

    def body(a_ref, b_ref, o_ref):
        @pl.when(pl.program_id(1) == 0)
        def _():
            o_ref[...] = jnp.zeros_like(o_ref)

        o_ref[...] += _tn(a_ref[...], b_ref[...])

    return pl.pallas_call(
        body, name=name, grid=(M // tm, K // tk),
        in_specs=[pl.BlockSpec((tk, tm), lambda i, k: (k, i)),
                  pl.BlockSpec((tk, N), lambda i, k: (k, 0))],
        out_specs=pl.BlockSpec((tm, N), lambda i, k: (i, 0)),
        out_shape=jax.ShapeDtypeStruct((M, N), F32),
        compiler_params=_params(2),
    )(a, b)


def _inproj_fwd(x, gain, w_in_p):
    S, D = x.shape
    tm = 512
    W = ATTN_WIDTH

    def body(x_ref, gain_ref, w_ref, h_ref, aq_ref, ak_ref, av_ref, dq_ref, dk_ref, dv_ref, gate_ref, bd_ref):
        xf = x_ref[...]
        r = lax.rsqrt(jnp.mean(xf * xf, axis=-1, keepdims=True) + NORM_EPS)
        h = (xf * r * gain_ref[...]).astype(BF16)
        h_ref[...] = h
        aq_ref[...] = (_nn(h, w_ref[:, 0:W]) * 0.125).astype(BF16)
        ak_ref[...] = _nn(h, w_ref[:, W:2 * W]).astype(BF16)
        av_ref[...] = _nn(h, w_ref[:, 2 * W:3 * W]).astype(BF16)
        dq_ref[...] = _nn(h, w_ref[:, 3 * W:4 * W])
        dk_ref[...] = _nn(h, w_ref[:, 4 * W:5 * W])
        dv_ref[...] = _nn(h, w_ref[:, 5 * W:6 * W])
        gate_ref[...] = _nn(h, w_ref[:, 6 * W:7 * W])
        bd_ref[...] = _nn(h, w_ref[:, 7 * W:7 * W + 128])

    tok = lambda w: pl.BlockSpec((tm, w), lambda i: (i, 0))
    return pl.pallas_call(
        body, name="inproj_fwd", grid=(S // tm,),
        in_specs=[tok(D), pl.BlockSpec((1, D), lambda i: (0, 0)),
                  pl.BlockSpec((D, IN_COLS_PADDED), lambda i: (0, 0))],
        out_specs=[tok(D)] + [tok(W)] * 7 + [tok(128)],
        out_shape=[jax.ShapeDtypeStruct((S, D), BF16)] + [jax.ShapeDtypeStruct((S, W), BF16)] * 3
                  + [jax.ShapeDtypeStruct((S, W), F32)] * 4 + [jax.ShapeDtypeStruct((S, 128), F32)],
        compiler_params=_params(1),
    )(x, gain, w_in_p)


def _inproj_bwd(dxo, x, gain, dsecs, dbd, w_in_pT):
    S, D = x.shape
    tm = 512
    W = ATTN_WIDTH

    def body(dxo_ref, x_ref, gain_ref, s0, s1, s2, s3, s4, s5, s6, dbd_ref, wT_ref, dx_ref, dgain_ref, dproj_ref):
        @pl.when(pl.program_id(0) == 0)
        def _():
            dgain_ref[...] = jnp.zeros_like(dgain_ref)

        dh = jnp.zeros((tm, D), F32)
        for k, s in enumerate((s0, s1, s2, s3, s4, s5, s6)):
            d = s[...].astype(BF16)
            dproj_ref[:, k * W:(k + 1) * W] = d
            dh += _nn(d, wT_ref[k * W:(k + 1) * W, :])
        d = dbd_ref[...].astype(BF16)
        dproj_ref[:, 7 * W:7 * W + 128] = d
        dh += _nn(d, wT_ref[7 * W:7 * W + 128, :])
        dx, dgain = _rmsnorm_bwd(dh, x_ref[...], gain_ref[...])
        dx_ref[...] = dxo_ref[...] + dx
        dgain_ref[...] += dgain

    tok = lambda w: pl.BlockSpec((tm, w), lambda i: (i, 0))
    return pl.pallas_call(
        body, name="inproj_bwd", grid=(S // tm,),
        in_specs=[tok(D), tok(D), pl.BlockSpec((1, D), lambda i: (0, 0))] + [tok(W)] * 7 + [tok(128)]
                 + [pl.BlockSpec((IN_COLS_PADDED, D), lambda i: (0, 0))],
        out_specs=[tok(D), pl.BlockSpec((1, D), lambda i: (0, 0)), tok(IN_COLS_PADDED)],
        out_shape=[jax.ShapeDtypeStruct((S, D), F32), jax.ShapeDtypeStruct((1, D), F32),
                   jax.ShapeDtypeStruct((S, IN_COLS_PADDED), BF16)],
        compiler_params=_params(1),
    )(dxo, x, gain, *dsecs, dbd, w_in_pT)


def _slope(h):
    return 2.0 ** (-8.0 * (h + 1) / ATTN_HEADS)


def _attn_fwd(q, k, v, d, name):
    S = q.shape[0]
    L = S // d
    nb = L // ATTN_BLOCK
    B = ATTN_BLOCK
    view = lambda t: t.reshape(L, d * ATTN_WIDTH)

    def body(q_ref, kp_ref, kc_ref, vp_ref, vc_ref, acc_ref, m_ref, l_ref):
        n = pl.program_id(1)
        qi = lax.broadcasted_iota(jnp.int32, (B, 2 * B), 0)
        kj = lax.broadcasted_iota(jnp.int32, (B, 2 * B), 1)
        steps = qi + B - kj
        valid = (steps >= 0) & (steps <= B) & ((kj >= B) | (n > 0))
        stepsf = steps.astype(F32)
        lo = lax.broadcasted_iota(jnp.int32, (B, 128), 1) < 64
        for G in range(4):
            sl = slice(G * 128, (G + 1) * 128)
            qg = q_ref[:, sl]
            kg = jnp.concatenate([kp_ref[:, sl], kc_ref[:, sl]], axis=0)
            vg = jnp.concatenate([vp_ref[:, sl], vc_ref[:, sl]], axis=0)
            res = []
            for half in (0, 1):
                msk = lo if half == 0 else jnp.logical_not(lo)
                qm = jnp.where(msk, qg, jnp.zeros_like(qg))
                s = _nt(qm, kg)
                s = jnp.where(valid, s - (_slope(2 * G + half) * d) * stepsf, NEG_BIG)
                m = jnp.max(s, axis=-1, keepdims=True)
                p = jnp.exp(s - m)
                l = jnp.sum(p, axis=-1, keepdims=True)
                a = _nn(p.astype(BF16), vg)
                res.append((a, m, l))
            (a0, m0, l0), (a1, m1, l1) = res
            acc_ref[:, sl] = jnp.where(lo, a0, a1)
            m_ref[:, sl] = jnp.where(lo, m0, m1)
            l_ref[:, sl] = jnp.where(lo, l0, l1)

    cur = pl.BlockSpec((B, ATTN_WIDTH), lambda r, n: (n, r))
    prev = pl.BlockSpec((B, ATTN_WIDTH), lambda r, n: (jnp.maximum(n - 1, 0), r))
    outs = pl.pallas_call(
        body, name=name, grid=(d, nb),
        in_specs=[cur, prev, cur, prev, cur],
        out_specs=[cur, cur, cur],
        out_shape=[jax.ShapeDtypeStruct((L, d * ATTN_WIDTH), F32)] * 3,
        compiler_params=_params(2),
    )(view(q), view(k), view(k), view(v), view(v))
    return [o.reshape(S, ATTN_WIDTH) for o in outs]


def _attn_merge(parts):
    S = parts[0][0].shape[0]
    tm = 512

    def body(a1, m1, l1, a2, m2, l2, a3, m3, l3, o_ref, lse_ref):
        ms = [m1[...], m2[...], m3[...]]
        mx = jnp.maximum(jnp.maximum(ms[0], ms[1]), ms[2])
        es = [jnp.exp(m - mx) for m in ms]
        den = es[0] * l1[...] + es[1] * l2[...] + es[2] * l3[...]
        num = es[0] * a1[...] + es[1] * a2[...] + es[2] * a3[...]
        o_ref[...] = num / den
        lse_ref[...] = mx + jnp.log(den)

    tok = pl.BlockSpec((tm, ATTN_WIDTH), lambda i: (i, 0))
    flat = [t for p in parts for t in p]
    return pl.pallas_call(
        body, name="attn_merge", grid=(S // tm,), in_specs=[tok] * 9, out_specs=[tok, tok],
        out_shape=[jax.ShapeDtypeStruct((S, ATTN_WIDTH), F32)] * 2, compiler_params=_params(1),
    )(*flat)


def _attn_delta(do, o):
    S = o.shape[0]
    tm = 512

    def body(do_ref, o_ref, dd_ref):
        lo = lax.broadcasted_iota(jnp.int32, (tm, 128), 1) < 64
        for G in range(4):
            sl = slice(G * 128, (G + 1) * 128)
            t = do_ref[:, sl] * o_ref[:, sl]
            d0 = jnp.sum(jnp.where(lo, t, 0.0), axis=-1, keepdims=True)
            d1 = jnp.sum(jnp.where(lo, 0.0, t), axis=-1, keepdims=True)
            dd_ref[:, sl] = jnp.where(lo, d0, d1)

    tok = pl.BlockSpec((tm, ATTN_WIDTH), lambda i: (i, 0))
    return pl.pallas_call(
        body, name="attn_delta", grid=(S // tm,), in_specs=[tok, tok], out_specs=tok,
        out_shape=jax.ShapeDtypeStruct((S, ATTN_WIDTH), F32), compiler_params=_params(1),
    )(do, o)


def _head_col(t, msk, big):
    if big:
        return jnp.max(jnp.where(msk, t, NEG_BIG), axis=-1, keepdims=True)
    return jnp.sum(jnp.where(msk, t, 0.0), axis=-1, keepdims=True) * (1.0 / 64.0)


def _attn_bwd_q(q, k, v, do, lse, dd, dq_run, d, name):
    S = q.shape[0]
    L = S // d
    nb = L // ATTN_BLOCK
    B = ATTN_BLOCK
    view = lambda t: t.reshape(L, d * ATTN_WIDTH)
    has_run = dq_run is not None

    def body(q_ref, kp_ref, kc_ref, vp_ref, vc_ref, do_ref, lse_ref, dd_ref, *rest):
        dq_ref = rest[-1]
        n = pl.program_id(1)
        qi = lax.broadcasted_iota(jnp.int32, (B, 2 * B), 0)
        kj = lax.broadcasted_iota(jnp.int32, (B, 2 * B), 1)
        steps = qi + B - kj
        valid = (steps >= 0) & (steps <= B) & ((kj >= B) | (n > 0))
        stepsf = steps.astype(F32)
        lo = lax.broadcasted_iota(jnp.int32, (B, 128), 1) < 64
        for G in range(4):
            sl = slice(G * 128, (G + 1) * 128)
            qg = q_ref[:, sl]
            kg = jnp.concatenate([kp_ref[:, sl], kc_ref[:, sl]], axis=0)
            vg = jnp.concatenate([vp_ref[:, sl], vc_ref[:, sl]], axis=0)
            dog = do_ref[:, sl]
            res = []
            for half in (0, 1):
                msk = lo if half == 0 else jnp.logical_not(lo)
                qm = jnp.where(msk, qg, jnp.zeros_like(qg))
                s = _nt(qm, kg) - (_slope(2 * G + half) * d) * stepsf
                lse_c = _head_col(lse_ref[:, sl], msk, True)
                p = jnp.where(valid, jnp.exp(jnp.where(valid, s, NEG_BIG) - lse_c), 0.0)
                dom = jnp.where(msk, dog, 0.0).astype(BF16)
                dp = _nt(dom, vg)
                dcol = _head_col(dd_ref[:, sl], msk, False)
                ds = p * (dp - dcol)
                res.append(_nn(ds.astype(BF16), kg) * 0.125)
            dq = jnp.where(lo, res[0], res[1])
            if has_run:
                dq = dq + rest[0][:, sl]
            dq_ref[:, sl] = dq

    cur = pl.BlockSpec((B, ATTN_WIDTH), lambda r, n: (n, r))
    prev = pl.BlockSpec((B, ATTN_WIDTH), lambda r, n: (jnp.maximum(n - 1, 0), r))
    args = [view(q), view(k), view(k), view(v), view(v), view(do), view(lse), view(dd)]
    specs = [cur, prev, cur, prev, cur, cur, cur, cur]
    if has_run:
        args.append(view(dq_run))
        specs.append(cur)
    out = pl.pallas_call(
        body, name=name, grid=(d, nb), in_specs=specs, out_specs=cur,
        out_shape=jax.ShapeDtypeStruct((L, d * ATTN_WIDTH), F32), compiler_params=_params(2),
    )(*args)
    return out.reshape(S, ATTN_WIDTH)


def _attn_bwd_kv(q, k, v, do, lse, dd, dk_run, dv_run, d, name):
    S = q.shape[0]
    L = S // d
    nb = L // ATTN_BLOCK
    B = ATTN_BLOCK
    view = lambda t: t.reshape(L, d * ATTN_WIDTH)
    has_run = dk_run is not None

    def body(k_ref, v_ref, qc_ref, qn_ref, doc_ref, don_ref, lsec_ref, lsen_ref, ddc_ref, ddn_ref, *rest):
        dk_ref, dv_ref = rest[-2], rest[-1]
        j = pl.program_id(1)
        qrow = lax.broadcasted_iota(jnp.int32, (2 * B, B), 0)
        kk = lax.broadcasted_iota(jnp.int32, (2 * B, B), 1)
        steps = qrow - kk
        valid = (steps >= 0) & (steps <= B) & ((qrow < B) | (j < nb - 1))
        stepsf = steps.astype(F32)
        lo2 = lax.broadcasted_iota(jnp.int32, (2 * B, 128), 1) < 64
        lo = lax.broadcasted_iota(jnp.int32, (B, 128), 1) < 64
        for G in range(4):
            sl = slice(G * 128, (G + 1) * 128)
            kg = k_ref[:, sl]
            vg = v_ref[:, sl]
            qq = jnp.concatenate([qc_ref[:, sl], qn_ref[:, sl]], axis=0)
            doo = jnp.concatenate([doc_ref[:, sl], don_ref[:, sl]], axis=0)
            lse2 = jnp.concatenate([lsec_ref[:, sl], lsen_ref[:, sl]], axis=0)
            dd2 = jnp.concatenate([ddc_ref[:, sl], ddn_ref[:, sl]], axis=0)
            doo_b = doo.astype(BF16)
            dks, dvs = [], []
            for half in (0, 1):
                msk = lo2 if half == 0 else jnp.logical_not(lo2)
                qm = jnp.where(msk, qq, jnp.zeros_like(qq))
                s = _nt(qm, kg) - (_slope(2 * G + half) * d) * stepsf
                lse_c = _head_col(lse2, msk, True)
                p = jnp.where(valid, jnp.exp(jnp.where(valid, s, NEG_BIG) - lse_c), 0.0)
                dvs.append(_tn(p.astype(BF16), doo_b))
                dom = jnp.where(msk, doo, 0.0).astype(BF16)
                dp = _nt(dom, vg)
                dcol = _head_col(dd2, msk, False)
                ds = p * (dp - dcol)
                dks.append(_tn(ds.astype(BF16), qq))
            dk = jnp.where(lo, dks[0], dks[1])
            dv = jnp.where(lo, dvs[0], dvs[1])
            if has_run:
                dk = dk + rest[0][:, sl]
                dv = dv + rest[1][:, sl]
            dk_ref[:, sl] = dk
            dv_ref[:, sl] = dv

    cur = pl.BlockSpec((B, ATTN_WIDTH), lambda r, j: (j, r))
    nxt = pl.BlockSpec((B, ATTN_WIDTH), lambda r, j: (jnp.minimum(j + 1, nb - 1), r))
    args = [view(k), view(v), view(q), view(q), view(do), view(do), view(lse), view(lse), view(dd), view(dd)]
    specs = [cur, cur, cur, nxt, cur, nxt, cur, nxt, cur, nxt]
    if has_run:
        args += [view(dk_run), view(dv_run)]
        specs += [cur, cur]
    outs = pl.pallas_call(
        body, name=name, grid=(d, nb), in_specs=specs, out_specs=[cur, cur],
        out_shape=[jax.ShapeDtypeStruct((L, d * ATTN_WIDTH), F32)] * 2, compiler_params=_params(2),
    )(*args)
    return [o.reshape(S, ATTN_WIDTH) for o in outs]


CONV_T = 512
HALO = 8


def _conv_taps(pad_ref, w, T):
    acc = pad_ref[pl.ds(HALO - 3, T), :] * w[0:1, :]
    for j in range(1, CONV_WIDTH):
        acc = acc + pad_ref[pl.ds(HALO - 3 + j, T), :] * w[j:j + 1, :]
    return acc


def _conv_fwd(xq, xk, xv, conv_w):
    S = xq.shape[0]
    T = CONV_T

    def body(xq_ref, xqh_ref, xk_ref, xkh_ref, xv_ref, xvh_ref, wq_ref, wk_ref, wv_ref,
             qn_ref, kn_ref, v_ref, pad_ref):
        i = pl.program_id(0)

        def act(x_ref, xh_ref, w_ref):
            pad_ref[pl.ds(0, HALO), :] = jnp.where(i > 0, xh_ref[...], 0.0)
            pad_ref[pl.ds(HALO, T), :] = x_ref[...]
            c = _conv_taps(pad_ref, w_ref[...], T)
            return c * _sigmoid(c)

        def l2n(t):
            return t * lax.rsqrt(jnp.sum(t * t, axis=-1, keepdims=True) + L2_EPS)

        qn_ref[...] = l2n(act(xq_ref, xqh_ref, wq_ref))
        kn_ref[...] = l2n(act(xk_ref, xkh_ref, wk_ref))
        v_ref[...] = act(xv_ref, xvh_ref, wv_ref)

    tile = pl.BlockSpec((T, 128), lambda i, h: (i, h))
    halo = pl.BlockSpec((HALO, 128), lambda i, h: (jnp.maximum(i * (T // HALO) - 1, 0), h))
    wspec = lambda sec: pl.BlockSpec((CONV_WIDTH, 128), lambda i, h, sec=sec: (0, 4 * sec + h))
    return pl.pallas_call(
        body, name="dn_conv_fwd", grid=(S // T, DN_HEADS),
        in_specs=[tile, halo, tile, halo, tile, halo, wspec(0), wspec(1), wspec(2)],
        out_specs=[tile, tile, tile],
        out_shape=[jax.ShapeDtypeStruct((S, DN_WIDTH), F32)] * 3,
        scratch_shapes=[pltpu.VMEM((T + HALO, 128), F32)],
        compiler_params=_params(2),
    )(xq, xq, xk, xk, xv, xv, conv_w, conv_w, conv_w)


def _conv_bwd_pre(xq, xk, xv, conv_w, dqn, dkn, dv):
    S = xq.shape[0]
    T = CONV_T

    def body(xq_ref, xqh_ref, xk_ref, xkh_ref, xv_ref, xvh_ref, wq_ref, wk_ref, wv_ref,
             dqn_ref, dkn_ref, dv_ref, dcq_ref, dck_ref, dcv_ref, dwq_ref, dwk_ref, dwv_ref, pad_ref):
        i = pl.program_id(1)

        def one(x_ref, xh_ref, w_ref, dy_ref, dc_ref, dw_ref, normed):
            pad_ref[pl.ds(0, HALO), :] = jnp.where(i > 0, xh_ref[...], 0.0)
            pad_ref[pl.ds(HALO, T), :] = x_ref[...]
            c = _conv_taps(pad_ref, w_ref[...], T)
            sg = _sigmoid(c)
            a = c * sg
            dy = dy_ref[...]
            if normed:
                r = lax.rsqrt(jnp.sum(a * a, axis=-1, keepdims=True) + L2_EPS)
                y = a * r
                da = r * (dy - y * jnp.sum(dy * y, axis=-1, keepdims=True))
            else:
                da = dy
            dc = da * (sg * (1.0 + c * (1.0 - sg)))
            dc_ref[...] = dc

            @pl.when(i == 0)
            def _():
                dw_ref[...] = jnp.zeros_like(dw_ref)

            rows = [jnp.sum(dc * pad_ref[pl.ds(HALO - 3 + j, T), :], axis=0, keepdims=True) for j in range(CONV_WIDTH)]
            dw_ref[...] += jnp.concatenate(rows + [jnp.zeros((8 - CONV_WIDTH, 128), F32)], axis=0)

        one(xq_ref, xqh_ref, wq_ref, dqn_ref, dcq_ref, dwq_ref, True)
        one(xk_ref, xkh_ref, wk_ref, dkn_ref, dck_ref, dwk_ref, True)
        one(xv_ref, xvh_ref, wv_ref, dv_ref, dcv_ref, dwv_ref, False)

    tile = pl.BlockSpec((T, 128), lambda h, i: (i, h))
    halo = pl.BlockSpec((HALO, 128), lambda h, i: (jnp.maximum(i * (T // HALO) - 1, 0), h))
    wspec = lambda sec: pl.BlockSpec((CONV_WIDTH, 128), lambda h, i, sec=sec: (0, 4 * sec + h))
    dwspec = pl.BlockSpec((8, 128), lambda h, i: (0, h))
    return pl.pallas_call(
        body, name="dn_conv_bwd_pre", grid=(DN_HEADS, S // T),
        in_specs=[tile, halo, tile, halo, tile, halo, wspec(0), wspec(1), wspec(2), tile, tile, tile],
        out_specs=[tile, tile, tile, dwspec, dwspec, dwspec],
        out_shape=[jax.ShapeDtypeStruct((S, DN_WIDTH), F32)] * 3 + [jax.ShapeDtypeStruct((8, DN_WIDTH), F32)] * 3,
        scratch_shapes=[pltpu.VMEM((T + HALO, 128), F32)],
        compiler_params=_params(2),
    )(xq, xq, xk, xk, xv, xv, conv_w, conv_w, conv_w, dqn, dkn, dv)


def _conv_bwd_x(dcq, dck, dcv, conv_w):
    S = dcq.shape[0]
    T = CONV_T
    nt = S // T

    def body(dq_ref, dqh_ref, dk_ref, dkh_ref, dv_ref, dvh_ref, wq_ref, wk_ref, wv_ref,
             oq_ref, ok_ref, ov_ref, pad_ref):
        i = pl.program_id(0)

        def one(d_ref, dh_ref, w_ref, o_ref):
            pad_ref[pl.ds(0, T), :] = d_ref[...]
            pad_ref[pl.ds(T, HALO), :] = jnp.where(i < nt - 1, dh_ref[...], 0.0)
            w = w_ref[...]
            acc = pad_ref[pl.ds(3, T), :] * w[0:1, :]
            for j in range(1, CONV_WIDTH):
                acc = acc + pad_ref[pl.ds(3 - j, T), :] * w[j:j + 1, :]
            o_ref[...] = acc

        one(dq_ref, dqh_ref, wq_ref, oq_ref)
        one(dk_ref, dkh_ref, wk_ref, ok_ref)
        one(dv_ref, dvh_ref, wv_ref, ov_ref)

    tile = pl.BlockSpec((T, 128), lambda i, h: (i, h))
    halo = pl.BlockSpec((HALO, 128), lambda i, h: (jnp.minimum((i + 1) * (T // HALO), S // HALO - 1), h))
    wspec = lambda sec: pl.BlockSpec((CONV_WIDTH, 128), lambda i, h, sec=sec: (0, 4 * sec + h))
    return pl.pallas_call(
        body, name="dn_conv_bwd_x", grid=(nt, DN_HEADS),
        in_specs=[tile, halo, tile, halo, tile, halo, wspec(0), wspec(1), wspec(2)],
        out_specs=[tile, tile, tile],
        out_shape=[jax.ShapeDtypeStruct((S, DN_WIDTH), F32)] * 3,
        scratch_shapes=[pltpu.VMEM((T + HALO, 128), F32)],
        compiler_params=_params(2),
    )(dcq, dcq, dck, dck, dcv, dcv, conv_w, conv_w, conv_w)


def _tri_inverse(a, blk, eye):
    mm = functools.partial(_nn, precision=HI)
    dg = jnp.where(blk, a, 0.0)
    lo = a - dg
    d2 = mm(dg, dg)
    d4 = mm(d2, d2)
    d8 = mm(d4, d4)
    td = mm(mm(mm(eye - dg, eye + d2), eye + d4), eye + d8)
    b = mm(td, lo)
    b2 = mm(b, b)
    tb = mm(eye - b, eye + b2)
    return mm(tb, td)


def _dn_chunk_common(bd, avec, dvec, h, q_raw, k, v):
    C = DN_CHUNK
    lane = lax.broadcasted_iota(jnp.int32, (C, 128), 1)
    row = lax.broadcasted_iota(jnp.int32, (C, C), 0)
    col = lax.broadcasted_iota(jnp.int32, (C, C), 1)
    incl = row >= col
    strict = row > col
    eye = (row == col).astype(F32)
    blk = (row // 16) == (col // 16)
    beta_all = _sigmoid(bd)
    z = bd + dvec
    sp = jnp.maximum(z, 0.0) + jnp.log(1.0 + jnp.exp(-jnp.abs(z)))
    g_all = -jnp.exp(avec) * sp
    pick = lambda t, ln: jnp.sum(jnp.where(lane == ln, t, 0.0), axis=-1, keepdims=True)
    beta = pick(beta_all, h)
    graw = pick(g_all, DN_HEADS + h)
    zc = pick(z, DN_HEADS + h)
    to_row = lambda c: jnp.sum(eye * c, axis=0, keepdims=True)
    gc = jnp.sum(jnp.where(incl, to_row(graw), 0.0), axis=-1, keepdims=True)
    gc_row = to_row(gc)
    decay = jnp.exp(jnp.where(incl, gc - gc_row, NEG_BIG))
    q = q_raw * (DN_HEAD_DIM ** -0.5)
    kb = k * beta
    kk = _nt(kb, k, HI)
    a = jnp.where(strict, kk * decay, 0.0)
    t = _tri_inverse(a, blk, eye)
    eg = jnp.exp(gc)
    rhs_u = v * beta
    rhs_w = kb * eg
    u = _nn(t, rhs_u, HI)
    w = _nn(t, rhs_w, HI)
    qk = _nt(q, k, HI)
    aq = jnp.where(incl, qk * decay, 0.0)
    g_last = jnp.sum(jnp.where(lax.broadcasted_iota(jnp.int32, (C, 1), 0) == C - 1, gc, 0.0), axis=0, keepdims=True)
    ekd = jnp.exp(g_last - gc)
    return dict(beta=beta, graw=graw, zc=zc, gc=gc, decay=decay, q=q, kb=kb, kk=kk, t=t, eg=eg, rhs_w=rhs_w,
                u=u, w=w, qk=qk, aq=aq, g_last=g_last, ekd=ekd, kd=k * ekd, qg=q * eg,
                incl=incl, strict=strict, eye=eye, lane=lane, row=row, col=col)


def _dn_fwd(qn, kn, v, bd, gate, avec, dvec, dn_gain):
    S = qn.shape[0]
    C = DN_CHUNK
    N = S // C
    HD = DN_HEAD_DIM

    def body(q_ref, k_ref, v_ref, bd_ref, gate_ref, a_ref, d_ref, gain_ref, dn_ref, o_ref, st_ref, state_ref):
        @pl.when(pl.program_id(0) == 0)
        def _():
            state_ref[...] = jnp.zeros_like(state_ref)

        st_ref[...] = state_ref[...]
        bd = bd_ref[...]
        for h in range(DN_HEADS):
            sl = slice(h * HD, (h + 1) * HD)
            k = k_ref[:, sl]
            c = _dn_chunk_common(bd, a_ref[...], d_ref[...], h, q_ref[:, sl], k, v_ref[:, sl])
            st = state_ref[sl, :]
            v_new = c["u"] - _nn(c["w"], st, HI)
            o = _nn(c["qg"], st, HI) + _nn(c["aq"], v_new, HI)
            state_ref[sl, :] = st * jnp.exp(c["g_last"]) + _tn(c["kd"], v_new, HI)
            o_ref[:, sl] = o
            r = lax.rsqrt(jnp.mean(o * o, axis=-1, keepdims=True) + NORM_EPS)
            gt = gate_ref[:, sl]
            dn_ref[:, sl] = o * r * gain_ref[...] * (gt * _sigmoid(gt))

    tok = lambda w: pl.BlockSpec((C, w), lambda n: (n, 0))
    vec = pl.BlockSpec((1, 128), lambda n: (0, 0))
    return pl.pallas_call(
        body, name="dn_fwd", grid=(N,),
        in_specs=[tok(DN_WIDTH)] * 3 + [tok(128), tok(DN_WIDTH), vec, vec, vec],
        out_specs=[tok(DN_WIDTH), tok(DN_WIDTH), pl.BlockSpec((DN_WIDTH, HD), lambda n: (n, 0))],
        out_shape=[jax.ShapeDtypeStruct((S, DN_WIDTH), F32)] * 2 + [jax.ShapeDtypeStruct((N * DN_WIDTH, HD), F32)],
        scratch_shapes=[pltpu.VMEM((DN_WIDTH, HD), F32)],
        compiler_params=_params(1),
    )(qn, kn, v, bd, gate, avec, dvec, dn_gain)


def _dn_bwd(qn, kn, v, bd, gate, avec, dvec, dn_gain, o, states, ddn):
    S = qn.shape[0]
    C = DN_CHUNK
    N = S // C
    HD = DN_HEAD_DIM

    def body(q_ref, k_ref, v_ref, bd_ref, gate_ref, a_ref, d_ref, gain_ref, o_ref, st_ref, ddn_ref,
             dq_ref, dk_ref, dv_ref, dgate_ref, dbd_ref, small_ref, dstate_ref):
        @pl.when(pl.program_id(0) == 0)
        def _():
            dstate_ref[...] = jnp.zeros_like(dstate_ref)
            small_ref[...] = jnp.zeros_like(small_ref)

        bd = bd_ref[...]
        avec = a_ref[...]
        gain = gain_ref[...]
        dbd = jnp.zeros((C, 128), F32)
        d_alog = jnp.zeros((1, 128), F32)
        d_dt = jnp.zeros((1, 128), F32)
        d_gain = jnp.zeros((1, 128), F32)
        for h in range(DN_HEADS):
            sl = slice(h * HD, (h + 1) * HD)
            k = k_ref[:, sl]
            vv = v_ref[:, sl]
            c = _dn_chunk_common(bd, avec, d_ref[...], h, q_ref[:, sl], k, vv)
            q, kb, t, eg, u, w = c["q"], c["kb"], c["t"], c["eg"], c["u"], c["w"]
            beta, decay, incl, strict, eye = c["beta"], c["decay"], c["incl"], c["strict"], c["eye"]
            lane = c["lane"]
            st = st_ref[sl, :]
            dsn = dstate_ref[sl, :]
            v_new = u - _nn(w, st, HI)
            ov = o_ref[:, sl]
            r = lax.rsqrt(jnp.mean(ov * ov, axis=-1, keepdims=True) + NORM_EPS)
            on = ov * r
            gt = gate_ref[:, sl]
            sgt = _sigmoid(gt)
            silu_g = gt * sgt
            dy = ddn_ref[:, sl]
            d_gain = d_gain + jnp.sum(dy * on * silu_g, axis=0, keepdims=True)
            dgate_ref[:, sl] = dy * on * gain * (sgt * (1.0 + gt * (1.0 - sgt)))
            don = dy * gain * silu_g
            do = r * (don - on * jnp.mean(don * on, axis=-1, keepdims=True))
            egl = jnp.exp(c["g_last"])
            d_vnew = _tn(c["aq"], do, HI) + _nn(c["kd"], dsn, HI)
            daq = jnp.where(incl, _nt(do, v_new, HI), 0.0)
            d_qg = _nt(do, st, HI)
            d_kd = _nt(v_new, dsn, HI)
            dstate_ref[sl, :] = _tn(c["qg"], do, HI) + egl * dsn - _tn(w, d_vnew, HI)
            d_glast = jnp.sum(jnp.sum(dsn * st, axis=-1, keepdims=True), axis=0, keepdims=True) * egl
            d_w = -_nt(d_vnew, st, HI)
            d_ru = _tn(t, d_vnew, HI)
            d_rw = _tn(t, d_w, HI)
            da = -jnp.where(strict, _nt(d_ru, u, HI) + _nt(d_rw, w, HI), 0.0)
            dv_ref[:, sl] = d_ru * beta
            dbeta = jnp.sum(d_ru * vv, axis=-1, keepdims=True)
            dkb = d_rw * eg
            dgc = jnp.sum(d_rw * c["rhs_w"], axis=-1, keepdims=True)
            dkk = da * decay
            ddecay = da * c["kk"]
            dkb = dkb + _nn(dkk, k, HI)
            dk = _tn(dkk, kb, HI)
            dqk = daq * decay
            ddecay = ddecay + daq * c["qk"]
            dq = _nn(dqk, k, HI)
            dk = dk + _tn(dqk, q, HI)
            m = ddecay * decay
            col_sum = jnp.sum(m, axis=0, keepdims=True)
            dgc = dgc + jnp.sum(m, axis=-1, keepdims=True) - jnp.sum(eye * col_sum, axis=-1, keepdims=True)
            dq = dq + d_qg * eg
            dgc = dgc + jnp.sum(d_qg * c["qg"], axis=-1, keepdims=True)
            dk = dk + d_kd * c["ekd"]
            tk = jnp.sum(d_kd * c["kd"], axis=-1, keepdims=True)
            dgc = dgc - tk
            d_glast = d_glast + jnp.sum(tk, axis=0, keepdims=True)
            dk = dk + dkb * beta
            dbeta = dbeta + jnp.sum(dkb * k, axis=-1, keepdims=True)
            dgc = dgc + jnp.where(lax.broadcasted_iota(jnp.int32, (C, 1), 0) == C - 1, d_glast, 0.0)
            dgc_row = jnp.sum(eye * dgc, axis=0, keepdims=True)
            dgraw = jnp.sum(jnp.where(c["col"] >= c["row"], dgc_row, 0.0), axis=-1, keepdims=True)
            dq_ref[:, sl] = dq * (HD ** -0.5)
            dk_ref[:, sl] = dk
            dbraw = dbeta * beta * (1.0 - beta)
            dz = dgraw * (-jnp.exp(avec)) * _sigmoid(c["zc"])
            dbd = dbd + jnp.where(lane == h, dbraw, 0.0) + jnp.where(lane == DN_HEADS + h, dz, 0.0)
            lane1 = lax.broadcasted_iota(jnp.int32, (1, 128), 1)
            d_alog = d_alog + jnp.where(lane1 == DN_HEADS + h, jnp.sum(dgraw * c["graw"], axis=0, keepdims=True), 0.0)
            d_dt = d_dt + jnp.where(lane1 == DN_HEADS + h, jnp.sum(dz, axis=0, keepdims=True), 0.0)
        dbd_ref[...] = dbd
        small_ref[...] += jnp.concatenate([d_alog, d_dt, d_gain, jnp.zeros((5, 128), F32)], axis=0)

    rev = lambda w: pl.BlockSpec((C, w), lambda i: (N - 1 - i, 0))
    vec = pl.BlockSpec((1, 128), lambda i: (0, 0))
    return pl.pallas_call(
        body, name="dn_bwd", grid=(N,),
        in_specs=[rev(DN_WIDTH)] * 3 + [rev(128), rev(DN_WIDTH), vec, vec, vec, rev(DN_WIDTH),
                                       pl.BlockSpec((DN_WIDTH, HD), lambda i: (N - 1 - i, 0)), rev(DN_WIDTH)],
        out_specs=[rev(DN_WIDTH)] * 4 + [rev(128), pl.BlockSpec((8, 128), lambda i: (0, 0))],
        out_shape=[jax.ShapeDtypeStruct((S, DN_WIDTH), F32)] * 4 + [jax.ShapeDtypeStruct((S, 128), F32),
                                                                  jax.ShapeDtypeStruct((8, 128), F32)],
        scratch_shapes=[pltpu.VMEM((DN_WIDTH, HD), F32)],
        compiler_params=_params(1),
    )(qn, kn, v, bd, gate, avec, dvec, dn_gain, o, states, ddn)


def _outproj_fwd(x, attn, dn, w_out):
    S, D = x.shape
    tm = 512

    def body(x_ref, a_ref, d_ref, w_ref, xo_ref, mix_ref):
        a = a_ref[...].astype(BF16)
        dd = d_ref[...].astype(BF16)
        mix_ref[:, 0:ATTN_WIDTH] = a
        mix_ref[:, ATTN_WIDTH:] = dd
        xo_ref[...] = x_ref[...] + _nn(a, w_ref[0:ATTN_WIDTH, :]) + _nn(dd, w_ref[ATTN_WIDTH:, :])

    tok = lambda w: pl.BlockSpec((tm, w), lambda i: (i, 0))
    return pl.pallas_call(
        body, name="outproj_fwd", grid=(S // tm,),
        in_specs=[tok(D), tok(ATTN_WIDTH), tok(DN_WIDTH), pl.BlockSpec((D, D), lambda i: (0, 0))],
        out_specs=[tok(D), tok(D)],
        out_shape=[jax.ShapeDtypeStruct((S, D), F32), jax.ShapeDtypeStruct((S, D), BF16)],
        compiler_params=_params(1),
    )(x, attn, dn, w_out)


def _outproj_bwd(dx, w_outT):
    S, D = dx.shape
    tm = 512

    def body(dx_ref, wT_ref, da_ref, dd_ref, dxb_ref):
        d = dx_ref[...].astype(BF16)
        dxb_ref[...] = d
        da_ref[...] = _nn(d, wT_ref[:, 0:ATTN_WIDTH])
        dd_ref[...] = _nn(d, wT_ref[:, ATTN_WIDTH:])

    tok = lambda w: pl.BlockSpec((tm, w), lambda i: (i, 0))
    return pl.pallas_call(
        body, name="outproj_bwd", grid=(S // tm,),
        in_specs=[tok(D), pl.BlockSpec((D, D), lambda i: (0, 0))],
        out_specs=[tok(ATTN_WIDTH), tok(DN_WIDTH), tok(D)],
        out_shape=[jax.ShapeDtypeStruct((S, ATTN_WIDTH), F32), jax.ShapeDtypeStruct((S, DN_WIDTH), F32),
                   jax.ShapeDtypeStruct((S, D), BF16)],
        compiler_params=_params(1),
    )(dx, w_outT)


def _loss_head(x, gain, target):
    S, D = x.shape
    tm = 512

    def body(x_ref, gain_ref, t_ref, loss_ref, dx_ref, dgain_ref):
        @pl.when(pl.program_id(0) == 0)
        def _():
            loss_ref[...] = jnp.zeros_like(loss_ref)
            dgain_ref[...] = jnp.zeros_like(dgain_ref)

        xf = x_ref[...]
        gain = gain_ref[...]
        r = lax.rsqrt(jnp.mean(xf * xf, axis=-1, keepdims=True) + NORM_EPS)
        xhat = xf * r
        err = xhat * gain - t_ref[...]
        part = 0.5 * jnp.sum(jnp.mean(err * err, axis=-1, keepdims=True), axis=0, keepdims=True)
        first = (lax.broadcasted_iota(jnp.int32, (8, 128), 0) == 0) & (lax.broadcasted_iota(jnp.int32, (8, 128), 1) == 0)
        loss_ref[...] += jnp.where(first, part, 0.0)
        dy = err * (1.0 / D)
        dgain_ref[...] += jnp.sum(dy * xhat, axis=0, keepdims=True)
        dxh = dy * gain
        dx_ref[...] = r * (dxh - xhat * jnp.mean(dxh * xhat, axis=-1, keepdims=True))

    tok = pl.BlockSpec((tm, D), lambda i: (i, 0))
    row = pl.BlockSpec((1, D), lambda i: (0, 0))
    return pl.pallas_call(
        body, name="loss_head", grid=(S // tm,),
        in_specs=[tok, row, tok],
        out_specs=[pl.BlockSpec((8, 128), lambda i: (0, 0)), tok, row],
        out_shape=[jax.ShapeDtypeStruct((8, 128), F32), jax.ShapeDtypeStruct((S, D), F32),
                   jax.ShapeDtypeStruct((1, D), F32)],
        compiler_params=_params(1),
    )(x, gain, target)


def _adamw(w, g, m, v, name):
    R, Ccols = w.shape
    tr = R
    for cand in (256, 128, 64, 32, 16, 8):
        if R % cand == 0:
            tr = cand
            break
    c1 = 1.0 - ADAM_B1 ** ADAM_STEP
    c2 = 1.0 - ADAM_B2 ** ADAM_STEP

    def body(w_ref, g_ref, m_ref, v_ref, d_ref, nm_ref, nv_ref):
        gv = g_ref[...]
        mn = ADAM_B1 * m_ref[...] + (1.0 - ADAM_B1) * gv
        vn = ADAM_B2 * v_ref[...] + (1.0 - ADAM_B2) * (gv * gv)
        nm_ref[...] = mn
        nv_ref[...] = vn
        d_ref[...] = -ADAM_LR * ((mn / c1) / (jnp.sqrt(vn / c2) + ADAM_EPS) + ADAM_WD * w_ref[...])

    spec = pl.BlockSpec((tr, Ccols), lambda i: (i, 0))
    return pl.pallas_call(
        body, name=name, grid=(R // tr,), in_specs=[spec] * 4, out_specs=[spec] * 3,
        out_shape=[jax.ShapeDtypeStruct((R, Ccols), F32)] * 3, compiler_params=_params(1),
    )(w, g, m, v)


def _local_step(x, target, wts, small):
    T = lambda a: a.T
    g1, g2, gm, gf = small["norm_ffn1"], small["norm_ffn2"], small["norm_mix"], small["norm_final"]

    x1, h1, fg1, fu1 = _ffn_fwd(x, g1, wts["ffn1_gate"], wts["ffn1_up"], wts["ffn1_down"], "ffn1_fwd")
    h2, aq, ak, av, xq, xk, xv, gate, bd = _inproj_fwd(x1, gm, wts["w_in"])
    parts = [_attn_fwd(aq, ak, av, d, f"attn_fwd_d{d}") for d in DILATIONS]
    attn, lse = _attn_merge(parts)
    conv_w = small["conv_w"]
    qn, kn, vv = _conv_fwd(xq, xk, xv, conv_w)
    dn, o_dn, states = _dn_fwd(qn, kn, vv, bd, gate, small["avec"], small["dvec"], small["dn_norm"])
    x2, mix = _outproj_fwd(x1, attn, dn, wts["w_out"])
    x3, h3, fg2, fu2 = _ffn_fwd(x2, g2, wts["ffn2_gate"], wts["ffn2_up"], wts["ffn2_down"], "ffn2_fwd")
    loss, dx3, d_gf = _loss_head(x3, gf, target)

    grads = {}
    dx2, d_g2, dfg2, dfu2, act2, dout2 = _ffn_bwd(dx3, x2, g2, fg2, fu2, T(wts["ffn2_down"]), T(wts["ffn2_gate"]),
                                                 T(wts["ffn2_up"]), "ffn2_bwd")
    tk = 512
    grads["ffn2_gate"] = _matmul_tn(dfg2, h3, D_FF // 2, tk, "dw_ffn2_gate")
    grads["ffn2_up"] = _matmul_tn(dfu2, h3, D_FF // 2, tk, "dw_ffn2_up")
    grads["ffn2_down"] = _matmul_tn(act2, dout2, D_FF // 2, tk, "dw_ffn2_down")

    dattn, ddn, dx2b = _outproj_bwd(dx2, T(wts["w_out"]))
    grads["w_out"] = _matmul_tn(mix, dx2b, D_MODEL, tk, "dw_out")

    dd = _attn_delta(dattn, attn)
    daq = dak = dav = None
    for d in DILATIONS:
        daq = _attn_bwd_q(aq, ak, av, dattn, lse, dd, daq, d, f"attn_bwd_q_d{d}")
        dak, dav = _attn_bwd_kv(aq, ak, av, dattn, lse, dd, dak, dav, d, f"attn_bwd_kv_d{d}")

    dqn, dkn, dvv, dgate, dbd, dn_small = _dn_bwd(qn, kn, vv, bd, gate, small["avec"], small["dvec"], small["dn_norm"],
                                                o_dn, states, ddn)
    dcq, dck, dcv, dwq, dwk, dwv = _conv_bwd_pre(xq, xk, xv, conv_w, dqn, dkn, dvv)
    dxq, dxk, dxv = _conv_bwd_x(dcq, dck, dcv, conv_w)
    d_conv = jnp.concatenate([dwq[:CONV_WIDTH], dwk[:CONV_WIDTH], dwv[:CONV_WIDTH]], axis=1)

    dx1, d_gm, dproj = _inproj_bwd(dx2, x1, gm, [daq, dak, dav, dxq, dxk, dxv, dgate], dbd, T(wts["w_in"]))
    grads["w_in"] = _matmul_tn(dproj, h2, IN_COLS_PADDED, 256, "dw_in")

    dx0, d_g1, dfg1, dfu1, act1, dout1 = _ffn_bwd(dx1, x, g1, fg1, fu1, T(wts["ffn1_down"]), T(wts["ffn1_gate"]),
                                                 T(wts["ffn1_up"]), "ffn1_bwd")
    grads["ffn1_gate"] = _matmul_tn(dfg1, h1, D_FF // 2, tk, "dw_ffn1_gate")
    grads["ffn1_up"] = _matmul_tn(dfu1, h1, D_FF // 2, tk, "dw_ffn1_up")
    grads["ffn1_down"] = _matmul_tn(act1, dout1, D_FF // 2, tk, "dw_ffn1_down")

    small_grads = dict(norm_ffn1=d_g1, norm_mix=d_gm, norm_ffn2=d_g2, norm_final=d_gf, conv_w=d_conv,
                       a_log=dn_small[0:1], dt_bias=dn_small[1:2], dn_norm=dn_small[2:3])
    return loss, dx0, grads, small_grads


PACK_SECTIONS = (("ffn1_gate", 704), ("ffn1_up", 704), ("ffn1_down", 704), ("w_in", 898), ("w_out", 256),
                 ("ffn2_gate", 704), ("ffn2_up", 704), ("ffn2_down", 704))
PACK_ROWS = 5408
HALF_ROWS = PACK_ROWS // 2
ADD_ROWS = 208

HBM = pl.BlockSpec(memory_space=pl.ANY)
VMEM_SPEC = pl.BlockSpec(memory_space=pltpu.VMEM)


def _coords():
    return lax.axis_index("x"), lax.axis_index("y"), lax.axis_index("c")


def _remote(src, dst, send_sems, recv_sems, k, dev):
    return pltpu.make_async_remote_copy(src_ref=src, dst_ref=dst, send_sem=send_sems.at[k], recv_sem=recv_sems.at[k],
                                        device_id=dev, device_id_type=MESH)


def _allreduce_small(buf, name):
    R, Cc = buf.shape

    def body(src_ref, out_ref, recv_ref, send_sems, recv_sems):
        x, y, c = _coords()
        copies = []
        for m in range(1, 8):
            fx, fy, fc = (m >> 2) & 1, (m >> 1) & 1, m & 1
            dev = (x ^ fx if fx else x, y ^ fy if fy else y, c ^ fc if fc else c)
            cp = _remote(src_ref, recv_ref.at[m - 1], send_sems, recv_sems, m - 1, dev)
            cp.start()
            copies.append(cp)
        for cp in copies:
            cp.wait()
        r = [src_ref[...]] + [recv_ref[m] for m in range(7)]
        out_ref[...] = ((r[0] + r[1]) + (r[2] + r[3])) + ((r[4] + r[5]) + (r[6] + r[7]))

    return pl.pallas_call(
        body, name=name, out_shape=jax.ShapeDtypeStruct((R, Cc), F32),
        in_specs=[VMEM_SPEC], out_specs=VMEM_SPEC,
        scratch_shapes=[pltpu.VMEM((7, R, Cc), F32), pltpu.SemaphoreType.DMA((7,)), pltpu.SemaphoreType.DMA((7,))],
    )(buf)


def _allgather_weights(pack2):
    _, Hh, Cc = pack2.shape

    def body(src_ref, out_ref, send_sems, recv_sems):
        x, y, c = _coords()
        sib = (x, y, 1 - c)
        others = [(1 - x, y), (x, 1 - y), (1 - x, 1 - y)]
        blk = lambda cx, cy, half: out_ref.at[2 * cx + cy, half]
        mine = _remote(src_ref, out_ref.at[2 * x + y], send_sems, recv_sems, 6, sib)
        mine.start()
        first = [_remote(src_ref.at[c], blk(x, y, c), send_sems, recv_sems, j, (ox, oy, c)) for j, (ox, oy) in enumerate(others)]
        for cp in first:
            cp.start()
        passed = [_remote(blk(ox, oy, c), blk(ox, oy, c), send_sems, recv_sems, 3 + j, sib) for j, (ox, oy) in enumerate(others)]
        for j, (ox, oy) in enumerate(others):
            _remote(src_ref.at[c], blk(ox, oy, c), send_sems, recv_sems, j, (ox, oy, c)).wait_recv()
            passed[j].start()
        for j, (ox, oy) in enumerate(others):
            _remote(src_ref.at[c], blk(ox, oy, 1 - c), send_sems, recv_sems, 3 + j, sib).wait_recv()
        for cp in first + passed:
            cp.wait_send()
        mine.wait()

    return pl.pallas_call(
        body, name="allgather_weights", out_shape=jax.ShapeDtypeStruct((N_CHIPS, 2, Hh, Cc), pack2.dtype),
        in_specs=[HBM], out_specs=HBM,
        scratch_shapes=[pltpu.SemaphoreType.DMA((7,)), pltpu.SemaphoreType.DMA((7,))],
    )(pack2)


def _rs_swap_halves(gpack):
    _, nj, Hh, Cc = gpack.shape

    def body(src_ref, out_ref, send_sems, recv_sems):
        x, y, c = _coords()
        cp = _remote(src_ref.at[1 - c], out_ref, send_sems, recv_sems, 0, (x, y, 1 - c))
        cp.start()
        cp.wait()

    return pl.pallas_call(
        body, name="rs_swap_halves", out_shape=jax.ShapeDtypeStruct((nj, Hh, Cc), gpack.dtype),
        in_specs=[HBM], out_specs=HBM,
        scratch_shapes=[pltpu.SemaphoreType.DMA((1,)), pltpu.SemaphoreType.DMA((1,))],
    )(gpack)


def _rs_add_pair(gpack, other, c):
    _, nj, Hh, Cc = gpack.shape
    tr = ADD_ROWS

    def body(c_ref, a_ref, b_ref, o_ref):
        o_ref[...] = (a_ref[...] + b_ref[...]).astype(BF16)

    return pl.pallas_call(
        body, name="rs_add_pair",
        grid_spec=pltpu.PrefetchScalarGridSpec(
            num_scalar_prefetch=1, grid=(nj, Hh // tr),
            in_specs=[pl.BlockSpec((None, None, tr, Cc), lambda j, i, c_ref: (c_ref[0], j, i, 0)),
                      pl.BlockSpec((None, tr, Cc), lambda j, i, c_ref: (j, i, 0))],
            out_specs=pl.BlockSpec((None, tr, Cc), lambda j, i, c_ref: (j, i, 0))),
        out_shape=jax.ShapeDtypeStruct((nj, Hh, Cc), BF16),
        compiler_params=_params(2),
    )(c, gpack, other)


def _rs_exchange_chips(part):
    nj, Hh, Cc = part.shape

    def body(src_ref, out_ref, send_sems, recv_sems):
        x, y, c = _coords()
        others = [(1 - x, y), (x, 1 - y), (1 - x, 1 - y)]
        cps = [_remote(src_ref.at[2 * ox + oy], out_ref.at[k], send_sems, recv_sems, k, (ox, oy, c))
               for k, (ox, oy) in enumerate(others)]
        for cp in cps:
            cp.start()
        for cp in cps:
            cp.wait()

    return pl.pallas_call(
        body, name="rs_exchange_chips", out_shape=jax.ShapeDtypeStruct((3, Hh, Cc), part.dtype),
        in_specs=[HBM], out_specs=HBM,
        scratch_shapes=[pltpu.SemaphoreType.DMA((3,)), pltpu.SemaphoreType.DMA((3,))],
    )(part)


def _rs_add_total(part, recv, chip):
    nj, Hh, Cc = part.shape
    tr = ADD_ROWS

    def body(chip_ref, p_ref, r0_ref, r1_ref, r2_ref, o_ref):
        f = lambda r: r[...].astype(F32)
        o_ref[...] = (f(p_ref) + f(r0_ref)) + (f(r1_ref) + f(r2_ref))

    rk = lambda k: pl.BlockSpec((None, tr, Cc), lambda i, chip_ref, k=k: (k, i, 0))
    return pl.pallas_call(
        body, name="rs_add_total",
        grid_spec=pltpu.PrefetchScalarGridSpec(
            num_scalar_prefetch=1, grid=(Hh // tr,),
            in_specs=[pl.BlockSpec((None, tr, Cc), lambda i, chip_ref: (chip_ref[0], i, 0)), rk(0), rk(1), rk(2)],
            out_specs=pl.BlockSpec((tr, Cc), lambda i, chip_ref: (i, 0))),
        out_shape=jax.ShapeDtypeStruct((Hh, Cc), F32),
        compiler_params=_params(1),
    )(chip, part, recv, recv, recv)


def _rs_share_total(total):
    Hh, Cc = total.shape

    def body(src_ref, out_ref, send_sems, recv_sems):
        x, y, c = _coords()
        cp = _remote(src_ref, out_ref, send_sems, recv_sems, 0, (x, y, 1 - c))
        cp.start()
        cp.wait()

    return pl.pallas_call(
        body, name="rs_share_total", out_shape=jax.ShapeDtypeStruct((Hh, Cc), total.dtype),
        in_specs=[HBM], out_specs=HBM,
        scratch_shapes=[pltpu.SemaphoreType.DMA((1,)), pltpu.SemaphoreType.DMA((1,))],
    )(total)


def _permute_w_in(w):
    return jnp.concatenate([w[:, :3072], w[:, 3080:IN_COLS], w[:, 3072:3080],
                            jnp.zeros((w.shape[0], IN_COLS_PADDED - IN_COLS), w.dtype)], axis=1)


def _pack_rows(shards, dtype):
    rows = [shards[n].astype(dtype).reshape(r, D_MODEL) for n, r in PACK_SECTIONS]
    used = sum(r for _, r in PACK_SECTIONS)
    rows.append(jnp.zeros((PACK_ROWS - used, D_MODEL), dtype))
    return jnp.concatenate(rows, axis=0)


def _unpack_rows(pack, shapes):
    out, at = {}, 0
    for n, r in PACK_SECTIONS:
        out[n] = pack[at:at + r].reshape(shapes[n])
        at += r
    return out


SHARD_SHAPES = dict(ffn1_gate=(1024, 704), ffn1_up=(1024, 704), ffn1_down=(704, 1024), w_in=(1024, 898),
                    w_out=(256, 1024), ffn2_gate=(1024, 704), ffn2_up=(1024, 704), ffn2_down=(704, 1024))
ROW_SHARDED = ("ffn1_down", "w_out", "ffn2_down")
SMALL_ROWS = 16


def _pad_row(v):
    v = v.reshape(1, -1)
    return jnp.pad(v, ((0, 0), (0, D_MODEL - v.shape[1])))


def kernel(x, norm_ffn1, ffn1_gate, ffn1_up, ffn1_down, norm_mix, w_in, conv_w, a_log, dt_bias, dn_norm, w_out, norm_ffn2, ffn2_gate, ffn2_up, ffn2_down, norm_final, loss_target, m_norm_ffn1, m_ffn1_gate, m_ffn1_up, m_ffn1_down, m_norm_mix, m_w_in, m_conv_w, m_a_log, m_dt_bias, m_dn_norm, m_w_out, m_norm_ffn2, m_ffn2_gate, m_ffn2_up, m_ffn2_down, m_norm_final, v_norm_ffn1, v_ffn1_gate, v_ffn1_up, v_ffn1_down, v_norm_mix, v_w_in, v_conv_w, v_a_log, v_dt_bias, v_dn_norm, v_w_out, v_norm_ffn2, v_ffn2_gate, v_ffn2_up, v_ffn2_down, v_norm_final):
    cx, cy, cc = _coords()
    chip = 2 * cx + cy
    big_w = dict(ffn1_gate=ffn1_gate[0], ffn1_up=ffn1_up[0], ffn1_down=ffn1_down[0], w_in=w_in[0], w_out=w_out[0],
                 ffn2_gate=ffn2_gate[0], ffn2_up=ffn2_up[0], ffn2_down=ffn2_down[0])
    big_m = dict(ffn1_gate=m_ffn1_gate[0], ffn1_up=m_ffn1_up[0], ffn1_down=m_ffn1_down[0], w_in=m_w_in[0], w_out=m_w_out[0],
                 ffn2_gate=m_ffn2_gate[0], ffn2_up=m_ffn2_up[0], ffn2_down=m_ffn2_down[0])
    big_v = dict(ffn1_gate=v_ffn1_gate[0], ffn1_up=v_ffn1_up[0], ffn1_down=v_ffn1_down[0], w_in=v_w_in[0], w_out=v_w_out[0],
                 ffn2_gate=v_ffn2_gate[0], ffn2_up=v_ffn2_up[0], ffn2_down=v_ffn2_down[0])

    pack = _pack_rows(big_w, BF16).reshape(2, HALF_ROWS, D_MODEL)
    gathered = _allgather_weights(pack).reshape(N_CHIPS, PACK_ROWS, D_MODEL)
    per_chip = [_unpack_rows(gathered[j], SHARD_SHAPES) for j in range(N_CHIPS)]
    wts = {n: jnp.concatenate([per_chip[j][n] for j in range(N_CHIPS)], axis=0 if n in ROW_SHARDED else 1)
           for n, _ in PACK_SECTIONS}
    wts["w_in"] = _permute_w_in(wts["w_in"])

    conv_shard = conv_w[0]
    emb = jnp.concatenate([jnp.where((chip == j) & (cc == 0), conv_shard, 0.0) for j in range(N_CHIPS)], axis=1)
    emb = jnp.pad(emb.reshape(6, D_MODEL), ((0, 2), (0, 0)))
    conv_full = _allreduce_small(emb, "allgather_conv_w")[:6].reshape(CONV_WIDTH, 3 * DN_WIDTH)

    zvec = jnp.zeros((1, 128), F32)
    small = dict(norm_ffn1=norm_ffn1, norm_mix=norm_mix, norm_ffn2=norm_ffn2, norm_final=norm_final[None],
                 conv_w=conv_full, avec=zvec.at[0, DN_HEADS:2 * DN_HEADS].set(a_log[0]),
                 dvec=zvec.at[0, DN_HEADS:2 * DN_HEADS].set(dt_bias[0]), dn_norm=dn_norm)

    loss, grad_x, grads, sg = _local_step(x[0], loss_target[0], wts, small)

    rows = [sg["norm_ffn1"], sg["norm_mix"], sg["norm_ffn2"], sg["norm_final"], _pad_row(sg["a_log"]), _pad_row(sg["dt_bias"]),
            _pad_row(sg["dn_norm"]), _pad_row(loss[0:1]), sg["conv_w"].reshape(6, D_MODEL), jnp.zeros((2, D_MODEL), F32)]
    red = _allreduce_small(jnp.concatenate(rows, axis=0), "allreduce_small")
    loss_out = red[7, 0]
    g_conv_full = red[8:14].reshape(CONV_WIDTH, 3 * DN_WIDTH)
    g_conv = lax.dynamic_slice_in_dim(g_conv_full, chip * (3 * DN_WIDTH // N_CHIPS), 3 * DN_WIDTH // N_CHIPS, axis=1)
    g_small = dict(norm_ffn1=red[0:1], norm_mix=red[1:2], norm_ffn2=red[2:3], norm_final=red[3],
                   a_log=red[4:5, DN_HEADS:2 * DN_HEADS], dt_bias=red[5:6, DN_HEADS:2 * DN_HEADS], dn_norm=red[6:7, :DN_HEAD_DIM])

    gi = grads["w_in"]
    grads["w_in"] = jnp.concatenate([gi[:3072], gi[3584:3592], gi[3072:3584]], axis=0)
    packs = []
    for j in range(N_CHIPS):
        sh = {n: grads[n][j * r:(j + 1) * r] for n, r in PACK_SECTIONS}
        packs.append(_pack_rows(sh, F32).reshape(2, HALF_ROWS, D_MODEL))
    gpack = jnp.stack(packs, axis=1)
    from_sibling = _rs_swap_halves(gpack)
    part = _rs_add_pair(gpack, from_sibling, cc.reshape(1).astype(jnp.int32))
    recv = _rs_exchange_chips(part)
    total = _rs_add_total(part, recv, chip.reshape(1).astype(jnp.int32))
    other = _rs_share_total(total)
    reduced = jnp.where(cc == 0, jnp.concatenate([total, other], axis=0), jnp.concatenate([other, total], axis=0))
    shard_g, at = {}, 0
    for n, r in PACK_SECTIONS:
        sec = reduced[at:at + r]
        shard_g[n] = sec if n in ROW_SHARDED else sec.T
        at += r

    out_g, out_d, out_m, out_v = {}, {}, {}, {}
    for n, _ in PACK_SECTIONS:
        d, nm, nv = _adamw(big_w[n], shard_g[n], big_m[n], big_v[n], "adamw_" + n)
        out_g[n], out_d[n], out_m[n], out_v[n] = shard_g[n][None], d[None], nm[None], nv[None]
    d, nm, nv = _adamw(conv_w[0], g_conv, m_conv_w[0], v_conv_w[0], "adamw_conv_w")
    out_g["conv_w"], out_d["conv_w"], out_m["conv_w"], out_v["conv_w"] = g_conv[None], d[None], nm[None], nv[None]

    small_names = ("norm_ffn1", "norm_mix", "norm_ffn2", "norm_final", "a_log", "dt_bias", "dn_norm")
    small_w = dict(norm_ffn1=norm_ffn1, norm_mix=norm_mix, norm_ffn2=norm_ffn2, norm_final=norm_final, a_log=a_log,
                   dt_bias=dt_bias, dn_norm=dn_norm)
    small_m = dict(norm_ffn1=m_norm_ffn1, norm_mix=m_norm_mix, norm_ffn2=m_norm_ffn2, norm_final=m_norm_final, a_log=m_a_log,
                   dt_bias=m_dt_bias, dn_norm=m_dn_norm)
    small_v = dict(norm_ffn1=v_norm_ffn1, norm_mix=v_norm_mix, norm_ffn2=v_norm_ffn2, norm_final=v_norm_final, a_log=v_a_log,
                   dt_bias=v_dt_bias, dn_norm=v_dn_norm)
    stack = lambda dct: jnp.concatenate([_pad_row(dct[n]) for n in small_names] + [jnp.zeros((1, D_MODEL), F32)], axis=0)
    d, nm, nv = _adamw(stack(small_w), stack(g_small), stack(small_m), stack(small_v), "adamw_small")
    for k, n in enumerate(small_names):
        shape = small_w[n].shape
        size = math.prod(shape)
        out_g[n] = g_small[n].reshape(shape)
        out_d[n], out_m[n], out_v[n] = (t[k, :size].reshape(shape) for t in (d, nm, nv))

    order = ("norm_ffn1", "ffn1_gate", "ffn1_up", "ffn1_down", "norm_mix", "w_in", "conv_w", "a_log", "dt_bias", "dn_norm",
             "w_out", "norm_ffn2", "ffn2_gate", "ffn2_up", "ffn2_down", "norm_final")
    return (loss_out, grad_x[None], *[out_g[n] for n in order], *[out_d[n] for n in order],
            *[out_m[n] for n in order], *[out_v[n] for n in order])
```

```python
import functools
import math

import jax
import jax.numpy as jnp
from jax import lax
from jax.experimental import pallas as pl
from jax.experimental.pallas import tpu as pltpu

F32 = jnp.float32
BF16 = jnp.bfloat16
HI = lax.Precision.HIGH

D_MODEL = 1024
D_FF = 2816
ATTN_HEADS = 8
ATTN_WIDTH = 512
ATTN_BLOCK = 128
DILATIONS = (1, 4, 16)
DN_HEADS = 4
DN_HEAD_DIM = 128
DN_WIDTH = 512
DN_CHUNK = 64
CONV_WIDTH = 4
NORM_EPS = 1e-6
L2_EPS = 1e-6
IN_COLS = 3592
IN_COLS_PADDED = 3712
N_CHIPS = 4

ADAM_LR = 0.001
ADAM_B1 = 0.9
ADAM_B2 = 0.999
ADAM_EPS = 1e-08
ADAM_WD = 0.01
ADAM_STEP = 10

VMEM_LIMIT = 56 * 1024 * 1024
NEG_BIG = -1e30
MESH = pl.DeviceIdType.MESH


def _params(n_grid, vmem=VMEM_LIMIT):
    return pltpu.CompilerParams(dimension_semantics=("arbitrary",) * n_grid, vmem_limit_bytes=vmem)


def _nt(a, b, precision=None):
    return lax.dot_general(a, b, (((1,), (1,)), ((), ())), preferred_element_type=F32, precision=precision)


def _tn(a, b, precision=None):
    return lax.dot_general(a, b, (((0,), (0,)), ((), ())), preferred_element_type=F32, precision=precision)


def _nn(a, b, precision=None):
    return jnp.dot(a, b, preferred_element_type=F32, precision=precision)


def _sigmoid(x):
    return 1.0 / (1.0 + jnp.exp(-x))


def _ffn_fwd(x, gain, wg, wu, wd, name):
    S, D = x.shape
    F = wg.shape[1]
    tm, tf = 512, F // 2
    nf = F // tf

    def body(x_ref, gain_ref, wg_ref, wu_ref, wd_ref, xo_ref, h_ref, g_ref, u_ref, acc_ref, hs_ref):
        j = pl.program_id(1)

        @pl.when(j == 0)
        def _():
            xf = x_ref[...]
            r = lax.rsqrt(jnp.mean(xf * xf, axis=-1, keepdims=True) + NORM_EPS)
            h = (xf * r * gain_ref[...]).astype(BF16)
            hs_ref[...] = h
            h_ref[...] = h
            acc_ref[...] = jnp.zeros_like(acc_ref)

        h = hs_ref[...]
        g = _nn(h, wg_ref[...])
        u = _nn(h, wu_ref[...])
        g_ref[...] = g.astype(BF16)
        u_ref[...] = u.astype(BF16)
        act = g * _sigmoid(g) * u
        acc_ref[...] += _nn(act.astype(BF16), wd_ref[...])

        @pl.when(j == nf - 1)
        def _():
            xo_ref[...] = x_ref[...] + 0.5 * acc_ref[...]

    return pl.pallas_call(
        body, name=name, grid=(S // tm, nf),
        in_specs=[pl.BlockSpec((tm, D), lambda i, j: (i, 0)),
                  pl.BlockSpec((1, D), lambda i, j: (0, 0)),
                  pl.BlockSpec((D, tf), lambda i, j: (0, j)),
                  pl.BlockSpec((D, tf), lambda i, j: (0, j)),
                  pl.BlockSpec((tf, D), lambda i, j: (j, 0))],
        out_specs=[pl.BlockSpec((tm, D), lambda i, j: (i, 0)),
                   pl.BlockSpec((tm, D), lambda i, j: (i, 0)),
                   pl.BlockSpec((tm, tf), lambda i, j: (i, j)),
                   pl.BlockSpec((tm, tf), lambda i, j: (i, j))],
        out_shape=[jax.ShapeDtypeStruct((S, D), F32), jax.ShapeDtypeStruct((S, D), BF16),
                   jax.ShapeDtypeStruct((S, F), BF16), jax.ShapeDtypeStruct((S, F), BF16)],
        scratch_shapes=[pltpu.VMEM((tm, D), F32), pltpu.VMEM((tm, D), BF16)],
        compiler_params=_params(2),
    )(x, gain, wg, wu, wd)


def _rmsnorm_bwd(dh, xf, gain):
    r = lax.rsqrt(jnp.mean(xf * xf, axis=-1, keepdims=True) + NORM_EPS)
    xhat = xf * r
    dgain = jnp.sum(dh * xhat, axis=0, keepdims=True)
    dxh = dh * gain
    dx = r * (dxh - xhat * jnp.mean(dxh * xhat, axis=-1, keepdims=True))
    return dx, dgain


def _ffn_bwd(dxo, x, gain, g, u, wdT, wgT, wuT, name):
    S, D = x.shape
    F = g.shape[1]
    tm, tf = 256, F // 2
    nf = F // tf

    def body(dxo_ref, x_ref, gain_ref, g_ref, u_ref, wdT_ref, wgT_ref, wuT_ref,
             dx_ref, dgain_ref, dg_ref, du_ref, act_ref, dout_ref, acc_ref, ds_ref):
        i = pl.program_id(0)
        j = pl.program_id(1)

        @pl.when(j == 0)
        def _():
            d = (0.5 * dxo_ref[...]).astype(BF16)
            ds_ref[...] = d
            dout_ref[...] = d
            acc_ref[...] = jnp.zeros_like(acc_ref)

        @pl.when((i == 0) & (j == 0))
        def _():
            dgain_ref[...] = jnp.zeros_like(dgain_ref)

        dact = _nn(ds_ref[...], wdT_ref[...])
        gv = g_ref[...].astype(F32)
        uv = u_ref[...].astype(F32)
        sg = _sigmoid(gv)
        silu = gv * sg
        act_ref[...] = (silu * uv).astype(BF16)
        dgv = (dact * uv * (sg * (1.0 + gv * (1.0 - sg)))).astype(BF16)
        duv = (dact * silu).astype(BF16)
        dg_ref[...] = dgv
        du_ref[...] = duv
        acc_ref[...] += _nn(dgv, wgT_ref[...]) + _nn(duv, wuT_ref[...])

        @pl.when(j == nf - 1)
        def _():
            dx, dgain = _rmsnorm_bwd(acc_ref[...], x_ref[...], gain_ref[...])
            dx_ref[...] = dxo_ref[...] + dx
            dgain_ref[...] += dgain

    return pl.pallas_call(
        body, name=name, grid=(S // tm, nf),
        in_specs=[pl.BlockSpec((tm, D), lambda i, j: (i, 0)),
                  pl.BlockSpec((tm, D), lambda i, j: (i, 0)),
                  pl.BlockSpec((1, D), lambda i, j: (0, 0)),
                  pl.BlockSpec((tm, tf), lambda i, j: (i, j)),
                  pl.BlockSpec((tm, tf), lambda i, j: (i, j)),
                  pl.BlockSpec((D, tf), lambda i, j: (0, j)),
                  pl.BlockSpec((tf, D), lambda i, j: (j, 0)),
                  pl.BlockSpec((tf, D), lambda i, j: (j, 0))],
        out_specs=[pl.BlockSpec((tm, D), lambda i, j: (i, 0)),
                   pl.BlockSpec((1, D), lambda i, j: (0, 0)),
                   pl.BlockSpec((tm, tf), lambda i, j: (i, j)),
                   pl.BlockSpec((tm, tf), lambda i, j: (i, j)),
                   pl.BlockSpec((tm, tf), lambda i, j: (i, j)),
                   pl.BlockSpec((tm, D), lambda i, j: (i, 0))],
        out_shape=[jax.ShapeDtypeStruct((S, D), F32), jax.ShapeDtypeStruct((1, D), F32),
                   jax.ShapeDtypeStruct((S, F), BF16), jax.ShapeDtypeStruct((S, F), BF16),
                   jax.ShapeDtypeStruct((S, F), BF16), jax.ShapeDtypeStruct((S, D), BF16)],
        scratch_shapes=[pltpu.VMEM((tm, D), F32), pltpu.VMEM((tm, D), BF16)],
        compiler_params=_params(2),
    )(dxo, x, gain, g, u, wdT, wgT, wuT)


def _matmul_tn(a, b, tm, tk, name):
    K, M = a.shape
    N = b.shape[1]

    def body(a_ref, b_ref, o_ref):
        @pl.when(pl.program_id(1) == 0)
        def _():
            o_ref[...] = jnp.zeros_like(o_ref)

        o_ref[...] += _tn(a_ref[...], b_ref[...])

    return pl.pallas_call(
        body, name=name, grid=(M // tm, K // tk),
        in_specs=[pl.BlockSpec((tk, tm), lambda i, k: (k, i)),
                  pl.BlockSpec((tk, N), lambda i, k: (k, 0))],
        out_specs=pl.BlockSpec((tm, N), lambda i, k: (i, 0)),
        out_shape=jax.ShapeDtypeStruct((M, N), F32),
        compiler_params=_params(2),
    )(a, b)


def _inproj_fwd(x, gain, w_in_p):
    S, D = x.shape
    tm = 512
    W = ATTN_WIDTH

    def body(x_ref, gain_ref, w_ref, h_ref, aq_ref, ak_ref, av_ref, dq_ref, dk_ref, dv_ref, gate_ref, bd_ref):
        xf = x_ref[...]
        r = lax.rsqrt(jnp.mean(xf * xf, axis=-1, keepdims=True) + NORM_EPS)
        h = (xf * r * gain_ref[...]).astype(BF16)
        h_ref[...] = h
        aq_ref[...] = (_nn(h, w_ref[:, 0:W]) * 0.125).astype(BF16)
        ak_ref[...] = _nn(h, w_ref[:, W:2 * W]).astype(BF16)
        av_ref[...] = _nn(h, w_ref[:, 2 * W:3 * W]).astype(BF16)
        dq_ref[...] = _nn(h, w_ref[:, 3 * W:4 * W])
        dk_ref[...] = _nn(h, w_ref[:, 4 * W:5 * W])
        dv_ref[...] = _nn(h, w_ref[:, 5 * W:6 * W])
        gate_ref[...] = _nn(h, w_ref[:, 6 * W:7 * W])
        bd_ref[...] = _nn(h, w_ref[:, 7 * W:7 * W + 128])

    tok = lambda w: pl.BlockSpec((tm, w), lambda i: (i, 0))
    return pl.pallas_call(
        body, name="inproj_fwd", grid=(S // tm,),
        in_specs=[tok(D), pl.BlockSpec((1, D), lambda i: (0, 0)),
                  pl.BlockSpec((D, IN_COLS_PADDED), lambda i: (0, 0))],
        out_specs=[tok(D)] + [tok(W)] * 7 + [tok(128)],
        out_shape=[jax.ShapeDtypeStruct((S, D), BF16)] + [jax.ShapeDtypeStruct((S, W), BF16)] * 3
                  + [jax.ShapeDtypeStruct((S, W), F32)] * 4 + [jax.ShapeDtypeStruct((S, 128), F32)],
        compiler_params=_params(1),
    )(x, gain, w_in_p)


def _inproj_bwd(dxo, x, gain, dsecs, dbd, w_in_pT):
    S, D = x.shape
    tm = 512
    W = ATTN_WIDTH

    def body(dxo_ref, x_ref, gain_ref, s0, s1, s2, s3, s4, s5, s6, dbd_ref, wT_ref, dx_ref, dgain_ref, dproj_ref):
        @pl.when(pl.program_id(0) == 0)
        def _():
            dgain_ref[...] = jnp.zeros_like(dgain_ref)

        dh = jnp.zeros((tm, D), F32)
        for k, s in enumerate((s0, s1, s2, s3, s4, s5, s6)):
            d = s[...].astype(BF16)
            dproj_ref[:, k * W:(k + 1) * W] = d
            dh += _nn(d, wT_ref[k * W:(k + 1) * W, :])
        d = dbd_ref[...].astype(BF16)
        dproj_ref[:, 7 * W:7 * W + 128] = d
        dh += _nn(d, wT_ref[7 * W:7 * W + 128, :])
        dx, dgain = _rmsnorm_bwd(dh, x_ref[...], gain_ref[...])
        dx_ref[...] = dxo_ref[...] + dx
        dgain_ref[...] += dgain

    tok = lambda w: pl.BlockSpec((tm, w), lambda i: (i, 0))
    return pl.pallas_call(
        body, name="inproj_bwd", grid=(S // tm,),
        in_specs=[tok(D), tok(D), pl.BlockSpec((1, D), lambda i: (0, 0))] + [tok(W)] * 7 + [tok(128)]
                 + [pl.BlockSpec((IN_COLS_PADDED, D), lambda i: (0, 0))],
        out_specs=[tok(D), pl.BlockSpec((1, D), lambda i: (0, 0)), tok(IN_COLS_PADDED)],
        out_shape=[jax.ShapeDtypeStruct((S, D), F32), jax.ShapeDtypeStruct((1, D), F32),
                   jax.ShapeDtypeStruct((S, IN_COLS_PADDED), BF16)],
        compiler_params=_params(1),
    )(dxo, x, gain, *dsecs, dbd, w_in_pT)


def _slope(h):
    return 2.0 ** (-8.0 * (h + 1) / ATTN_HEADS)


def _attn_fwd(q, k, v, d, name):
    S = q.shape[0]
    L = S // d
    nb = L // ATTN_BLOCK
    B = ATTN_BLOCK
    view = lambda t: t.reshape(L, d * ATTN_WIDTH)

    def body(q_ref, kp_ref, kc_ref, vp_ref, vc_ref, acc_ref, m_ref, l_ref):
        n = pl.program_id(1)
        qi = lax.broadcasted_iota(jnp.int32, (B, 2 * B), 0)
        kj = lax.broadcasted_iota(jnp.int32, (B, 2 * B), 1)
        steps = qi + B - kj
        valid = (steps >= 0) & (steps <= B) & ((kj >= B) | (n > 0))
        stepsf = steps.astype(F32)
        lo = lax.broadcasted_iota(jnp.int32, (B, 128), 1) < 64
        for G in range(4):
            sl = slice(G * 128, (G + 1) * 128)
            qg = q_ref[:, sl]
            kg = jnp.concatenate([kp_ref[:, sl], kc_ref[:, sl]], axis=0)
            vg = jnp.concatenate([vp_ref[:, sl], vc_ref[:, sl]], axis=0)
            res = []
            for half in (0, 1):
                msk = lo if half == 0 else jnp.logical_not(lo)
                qm = jnp.where(msk, qg, jnp.zeros_like(qg))
                s = _nt(qm, kg)
                s = jnp.where(valid, s - (_slope(2 * G + half) * d) * stepsf, NEG_BIG)
                m = jnp.max(s, axis=-1, keepdims=True)
                p = jnp.exp(s - m)
                l = jnp.sum(p, axis=-1, keepdims=True)
                a = _nn(p.astype(BF16), vg)
                res.append((a, m, l))
            (a0, m0, l0), (a1, m1, l1) = res
            acc_ref[:, sl] = jnp.where(lo, a0, a1)
            m_ref[:, sl] = jnp.where(lo, m0, m1)
            l_ref[:, sl] = jnp.where(lo, l0, l1)

    cur = pl.BlockSpec((B, ATTN_WIDTH), lambda r, n: (n, r))
    prev = pl.BlockSpec((B, ATTN_WIDTH), lambda r, n: (jnp.maximum(n - 1, 0), r))
    outs = pl.pallas_call(
        body, name=name, grid=(d, nb),
        in_specs=[cur, prev, cur, prev, cur],
        out_specs=[cur, cur, cur],
        out_shape=[jax.ShapeDtypeStruct((L, d * ATTN_WIDTH), F32)] * 3,
        compiler_params=_params(2),
    )(view(q), view(k), view(k), view(v), view(v))
    return [o.reshape(S, ATTN_WIDTH) for o in outs]


def _attn_merge(parts):
    S = parts[0][0].shape[0]
    tm = 512

    def body(a1, m1, l1, a2, m2, l2, a3, m3, l3, o_ref, lse_ref):
        ms = [m1[...], m2[...], m3[...]]
        mx = jnp.maximum(jnp.maximum(ms[0], ms[1]), ms[2])
        es = [jnp.exp(m - mx) for m in ms]
        den = es[0] * l1[...] + es[1] * l2[...] + es[2] * l3[...]
        num = es[0] * a1[...] + es[1] * a2[...] + es[2] * a3[...]
        o_ref[...] = num / den
        lse_ref[...] = mx + jnp.log(den)

    tok = pl.BlockSpec((tm, ATTN_WIDTH), lambda i: (i, 0))
    flat = [t for p in parts for t in p]
    return pl.pallas_call(
        body, name="attn_merge", grid=(S // tm,), in_specs=[tok] * 9, out_specs=[tok, tok],
        out_shape=[jax.ShapeDtypeStruct((S, ATTN_WIDTH), F32)] * 2, compiler_params=_params(1),
    )(*flat)


def _attn_delta(do, o):
    S = o.shape[0]
    tm = 512

    def body(do_ref, o_ref, dd_ref):
        lo = lax.broadcasted_iota(jnp.int32, (tm, 128), 1) < 64
        for G in range(4):
            sl = slice(G * 128, (G + 1) * 128)
            t = do_ref[:, sl] * o_ref[:, sl]
            d0 = jnp.sum(jnp.where(lo, t, 0.0), axis=-1, keepdims=True)
            d1 = jnp.sum(jnp.where(lo, 0.0, t), axis=-1, keepdims=True)
            dd_ref[:, sl] = jnp.where(lo, d0, d1)

    tok = pl.BlockSpec((tm, ATTN_WIDTH), lambda i: (i, 0))
    return pl.pallas_call(
        body, name="attn_delta", grid=(S // tm,), in_specs=[tok, tok], out_specs=tok,
        out_shape=jax.ShapeDtypeStruct((S, ATTN_WIDTH), F32), compiler_params=_params(1),
    )(do, o)


def _head_col(t, msk, big):
    if big:
        return jnp.max(jnp.where(msk, t, NEG_BIG), axis=-1, keepdims=True)
    return jnp.sum(jnp.where(msk, t, 0.0), axis=-1, keepdims=True) * (1.0 / 64.0)


def _attn_bwd_q(q, k, v, do, lse, dd, dq_run, d, name):
    S = q.shape[0]
    L = S // d
    nb = L // ATTN_BLOCK
    B = ATTN_BLOCK
    view = lambda t: t.reshape(L, d * ATTN_WIDTH)
    has_run = dq_run is not None

    def body(q_ref, kp_ref, kc_ref, vp_ref, vc_ref, do_ref, lse_ref, dd_ref, *rest):
        dq_ref = rest[-1]
        n = pl.program_id(1)
        qi = lax.broadcasted_iota(jnp.int32, (B, 2 * B), 0)
        kj = lax.broadcasted_iota(jnp.int32, (B, 2 * B), 1)
        steps = qi + B - kj
        valid = (steps >= 0) & (steps <= B) & ((kj >= B) | (n > 0))
        stepsf = steps.astype(F32)
        lo = lax.broadcasted_iota(jnp.int32, (B, 128), 1) < 64
        for G in range(4):
            sl = slice(G * 128, (G + 1) * 128)
            qg = q_ref[:, sl]
            kg = jnp.concatenate([kp_ref[:, sl], kc_ref[:, sl]], axis=0)
            vg = jnp.concatenate([vp_ref[:, sl], vc_ref[:, sl]], axis=0)
            dog = do_ref[:, sl]
            res = []
            for half in (0, 1):
                msk = lo if half == 0 else jnp.logical_not(lo)
                qm = jnp.where(msk, qg, jnp.zeros_like(qg))
                s = _nt(qm, kg) - (_slope(2 * G + half) * d) * stepsf
                lse_c = _head_col(lse_ref[:, sl], msk, True)
                p = jnp.where(valid, jnp.exp(jnp.where(valid, s, NEG_BIG) - lse_c), 0.0)
                dom = jnp.where(msk, dog, 0.0).astype(BF16)
                dp = _nt(dom, vg)
                dcol = _head_col(dd_ref[:, sl], msk, False)
                ds = p * (dp - dcol)
                res.append(_nn(ds.astype(BF16), kg) * 0.125)
            dq = jnp.where(lo, res[0], res[1])
            if has_run:
                dq = dq + rest[0][:, sl]
            dq_ref[:, sl] = dq

    cur = pl.BlockSpec((B, ATTN_WIDTH), lambda r, n: (n, r))
    prev = pl.BlockSpec((B, ATTN_WIDTH), lambda r, n: (jnp.maximum(n - 1, 0), r))
    args = [view(q), view(k), view(k), view(v), view(v), view(do), view(lse), view(dd)]
    specs = [cur, prev, cur, prev, cur, cur, cur, cur]
    if has_run:
        args.append(view(dq_run))
        specs.append(cur)
    out = pl.pallas_call(
        body, name=name, grid=(d, nb), in_specs=specs, out_specs=cur,
        out_shape=jax.ShapeDtypeStruct((L, d * ATTN_WIDTH), F32), compiler_params=_params(2),
    )(*args)
    return out.reshape(S, ATTN_WIDTH)


def _attn_bwd_kv(q, k, v, do, lse, dd, dk_run, dv_run, d, name):
    S = q.shape[0]
    L = S // d
    nb = L // ATTN_BLOCK
    B = ATTN_BLOCK
    view = lambda t: t.reshape(L, d * ATTN_WIDTH)
    has_run = dk_run is not None

    def body(k_ref, v_ref, qc_ref, qn_ref, doc_ref, don_ref, lsec_ref, lsen_ref, ddc_ref, ddn_ref, *rest):
        dk_ref, dv_ref = rest[-2], rest[-1]
        j = pl.program_id(1)
        qrow = lax.broadcasted_iota(jnp.int32, (2 * B, B), 0)
        kk = lax.broadcasted_iota(jnp.int32, (2 * B, B), 1)
        steps = qrow - kk
        valid = (steps >= 0) & (steps <= B) & ((qrow < B) | (j < nb - 1))
        stepsf = steps.astype(F32)
        lo2 = lax.broadcasted_iota(jnp.int32, (2 * B, 128), 1) < 64
        lo = lax.broadcasted_iota(jnp.int32, (B, 128), 1) < 64
        for G in range(4):
            sl = slice(G * 128, (G + 1) * 128)
            kg = k_ref[:, sl]
            vg = v_ref[:, sl]
            qq = jnp.concatenate([qc_ref[:, sl], qn_ref[:, sl]], axis=0)
            doo = jnp.concatenate([doc_ref[:, sl], don_ref[:, sl]], axis=0)
            lse2 = jnp.concatenate([lsec_ref[:, sl], lsen_ref[:, sl]], axis=0)
            dd2 = jnp.concatenate([ddc_ref[:, sl], ddn_ref[:, sl]], axis=0)
            doo_b = doo.astype(BF16)
            dks, dvs = [], []
            for half in (0, 1):
                msk = lo2 if half == 0 else jnp.logical_not(lo2)
                qm = jnp.where(msk, qq, jnp.zeros_like(qq))
                s = _nt(qm, kg) - (_slope(2 * G + half) * d) * stepsf
                lse_c = _head_col(lse2, msk, True)
                p = jnp.where(valid, jnp.exp(jnp.where(valid, s, NEG_BIG) - lse_c), 0.0)
                dvs.append(_tn(p.astype(BF16), doo_b))
                dom = jnp.where(msk, doo, 0.0).astype(BF16)
                dp = _nt(dom, vg)
                dcol = _head_col(dd2, msk, False)
                ds = p * (dp - dcol)
                dks.append(_tn(ds.astype(BF16), qq))
            dk = jnp.where(lo, dks[0], dks[1])
            dv = jnp.where(lo, dvs[0], dvs[1])
            if has_run:
                dk = dk + rest[0][:, sl]
                dv = dv + rest[1][:, sl]
            dk_ref[:, sl] = dk
            dv_ref[:, sl] = dv

    cur = pl.BlockSpec((B, ATTN_WIDTH), lambda r, j: (j, r))
    nxt = pl.BlockSpec((B, ATTN_WIDTH), lambda r, j: (jnp.minimum(j + 1, nb - 1), r))
    args = [view(k), view(v), view(q), view(q), view(do), view(do), view(lse), view(lse), view(dd), view(dd)]
    specs = [cur, cur, cur, nxt, cur, nxt, cur, nxt, cur, nxt]
    if has_run:
        args += [view(dk_run), view(dv_run)]
        specs += [cur, cur]
    outs = pl.pallas_call(
        body, name=name, grid=(d, nb), in_specs=specs, out_specs=[cur, cur],
        out_shape=[jax.ShapeDtypeStruct((L, d * ATTN_WIDTH), F32)] * 2, compiler_params=_params(2),
    )(*args)
    return [o.reshape(S, ATTN_WIDTH) for o in outs]


CONV_T = 512
HALO = 8


def _conv_taps(pad_ref, w, T):
    acc = pad_ref[pl.ds(HALO - 3, T), :] * w[0:1, :]
    for j in range(1, CONV_WIDTH):
        acc = acc + pad_ref[pl.ds(HALO - 3 + j, T), :] * w[j:j + 1, :]
    return acc


def _conv_fwd(xq, xk, xv, conv_w):
    S = xq.shape[0]
    T = CONV_T

    def body(xq_ref, xqh_ref, xk_ref, xkh_ref, xv_ref, xvh_ref, wq_ref, wk_ref, wv_ref,
             qn_ref, kn_ref, v_ref, pad_ref):
        i = pl.program_id(0)

        def act(x_ref, xh_ref, w_ref):
            pad_ref[pl.ds(0, HALO), :] = jnp.where(i > 0, xh_ref[...], 0.0)
            pad_ref[pl.ds(HALO, T), :] = x_ref[...]
            c = _conv_taps(pad_ref, w_ref[...], T)
            return c * _sigmoid(c)

        def l2n(t):
            return t * lax.rsqrt(jnp.sum(t * t, axis=-1, keepdims=True) + L2_EPS)

        qn_ref[...] = l2n(act(xq_ref, xqh_ref, wq_ref))
        kn_ref[...] = l2n(act(xk_ref, xkh_ref, wk_ref))
        v_ref[...] = act(xv_ref, xvh_ref, wv_ref)

    tile = pl.BlockSpec((T, 128), lambda i, h: (i, h))
    halo = pl.BlockSpec((HALO, 128), lambda i, h: (jnp.maximum(i * (T // HALO) - 1, 0), h))
    wspec = lambda sec: pl.BlockSpec((CONV_WIDTH, 128), lambda i, h, sec=sec: (0, 4 * sec + h))
    return pl.pallas_call(
        body, name="dn_conv_fwd", grid=(S // T, DN_HEADS),
        in_specs=[tile, halo, tile, halo, tile, halo, wspec(0), wspec(1), wspec(2)],
        out_specs=[tile, tile, tile],
        out_shape=[jax.ShapeDtypeStruct((S, DN_WIDTH), F32)] * 3,
        scratch_shapes=[pltpu.VMEM((T + HALO, 128), F32)],
        compiler_params=_params(2),
    )(xq, xq, xk, xk, xv, xv, conv_w, conv_w, conv_w)


def _conv_bwd_pre(xq, xk, xv, conv_w, dqn, dkn, dv):
    S = xq.shape[0]
    T = CONV_T

    def body(xq_ref, xqh_ref, xk_ref, xkh_ref, xv_ref, xvh_ref, wq_ref, wk_ref, wv_ref,
             dqn_ref, dkn_ref, dv_ref, dcq_ref, dck_ref, dcv_ref, dwq_ref, dwk_ref, dwv_ref, pad_ref):
        i = pl.program_id(1)

        def one(x_ref, xh_ref, w_ref, dy_ref, dc_ref, dw_ref, normed):
            pad_ref[pl.ds(0, HALO), :] = jnp.where(i > 0, xh_ref[...], 0.0)
            pad_ref[pl.ds(HALO, T), :] = x_ref[...]
            c = _conv_taps(pad_ref, w_ref[...], T)
            sg = _sigmoid(c)
            a = c * sg
            dy = dy_ref[...]
            if normed:
                r = lax.rsqrt(jnp.sum(a * a, axis=-1, keepdims=True) + L2_EPS)
                y = a * r
                da = r * (dy - y * jnp.sum(dy * y, axis=-1, keepdims=True))
            else:
                da = dy
            dc = da * (sg * (1.0 + c * (1.0 - sg)))
            dc_ref[...] = dc

            @pl.when(i == 0)
            def _():
                dw_ref[...] = jnp.zeros_like(dw_ref)

            rows = [jnp.sum(dc * pad_ref[pl.ds(HALO - 3 + j, T), :], axis=0, keepdims=True) for j in range(CONV_WIDTH)]
            dw_ref[...] += jnp.concatenate(rows + [jnp.zeros((8 - CONV_WIDTH, 128), F32)], axis=0)

        one(xq_ref, xqh_ref, wq_ref, dqn_ref, dcq_ref, dwq_ref, True)
        one(xk_ref, xkh_ref, wk_ref, dkn_ref, dck_ref, dwk_ref, True)
        one(xv_ref, xvh_ref, wv_ref, dv_ref, dcv_ref, dwv_ref, False)

    tile = pl.BlockSpec((T, 128), lambda h, i: (i, h))
    halo = pl.BlockSpec((HALO, 128), lambda h, i: (jnp.maximum(i * (T // HALO) - 1, 0), h))
    wspec = lambda sec: pl.BlockSpec((CONV_WIDTH, 128), lambda h, i, sec=sec: (0, 4 * sec + h))
    dwspec = pl.BlockSpec((8, 128), lambda h, i: (0, h))
    return pl.pallas_call(
        body, name="dn_conv_bwd_pre", grid=(DN_HEADS, S // T),
        in_specs=[tile, halo, tile, halo, tile, halo, wspec(0), wspec(1), wspec(2), tile, tile, tile],
        out_specs=[tile, tile, tile, dwspec, dwspec, dwspec],
        out_shape=[jax.ShapeDtypeStruct((S, DN_WIDTH), F32)] * 3 + [jax.ShapeDtypeStruct((8, DN_WIDTH), F32)] * 3,
        scratch_shapes=[pltpu.VMEM((T + HALO, 128), F32)],
        compiler_params=_params(2),
    )(xq, xq, xk, xk, xv, xv, conv_w, conv_w, conv_w, dqn, dkn, dv)


def _conv_bwd_x(dcq, dck, dcv, conv_w):
    S = dcq.shape[0]
    T = CONV_T
    nt = S // T

    def body(dq_ref, dqh_ref, dk_ref, dkh_ref, dv_ref, dvh_ref, wq_ref, wk_ref, wv_ref,
             oq_ref, ok_ref, ov_ref, pad_ref):
        i = pl.program_id(0)

        def one(d_ref, dh_ref, w_ref, o_ref):
            pad_ref[pl.ds(0, T), :] = d_ref[...]
            pad_ref[pl.ds(T, HALO), :] = jnp.where(i < nt - 1, dh_ref[...], 0.0)
            w = w_ref[...]
            acc = pad_ref[pl.ds(3, T), :] * w[0:1, :]
            for j in range(1, CONV_WIDTH):
                acc = acc + pad_ref[pl.ds(3 - j, T), :] * w[j:j + 1, :]
            o_ref[...] = acc

        one(dq_ref, dqh_ref, wq_ref, oq_ref)
        one(dk_ref, dkh_ref, wk_ref, ok_ref)
        one(dv_ref, dvh_ref, wv_ref, ov_ref)

    tile = pl.BlockSpec((T, 128), lambda i, h: (i, h))
    halo = pl.BlockSpec((HALO, 128), lambda i, h: (jnp.minimum((i + 1) * (T // HALO), S // HALO - 1), h))
    wspec = lambda sec: pl.BlockSpec((CONV_WIDTH, 128), lambda i, h, sec=sec: (0, 4 * sec + h))
    return pl.pallas_call(
        body, name="dn_conv_bwd_x", grid=(nt, DN_HEADS),
        in_specs=[tile, halo, tile, halo, tile, halo, wspec(0), wspec(1), wspec(2)],
        out_specs=[tile, tile, tile],
        out_shape=[jax.ShapeDtypeStruct((S, DN_WIDTH), F32)] * 3,
        scratch_shapes=[pltpu.VMEM((T + HALO, 128), F32)],
        compiler_params=_params(2),
    )(dcq, dcq, dck, dck, dcv, dcv, conv_w, conv_w, conv_w)


def _tri_inverse(a, blk, eye):
    mm = functools.partial(_nn, precision=HI)
    dg = jnp.where(blk, a, 0.0)
    lo = a - dg
    d2 = mm(dg, dg)
    d4 = mm(d2, d2)
    d8 = mm(d4, d4)
    td = mm(mm(mm(eye - dg, eye + d2), eye + d4), eye + d8)
    b = mm(td, lo)
    b2 = mm(b, b)
    tb = mm(eye - b, eye + b2)
    return mm(tb, td)


def _dn_chunk_common(bd, avec, dvec, h, q_raw, k, v, t=None):
    C = DN_CHUNK
    lane = lax.broadcasted_iota(jnp.int32, (C, 128), 1)
    row = lax.broadcasted_iota(jnp.int32, (C, C), 0)
    col = lax.broadcasted_iota(jnp.int32, (C, C), 1)
    incl = row >= col
    strict = row > col
    eye = (row == col).astype(F32)
    blk = (row // 16) == (col // 16)
    beta_all = _sigmoid(bd)
    z = bd + dvec
    sp = jnp.maximum(z, 0.0) + jnp.log(1.0 + jnp.exp(-jnp.abs(z)))
    g_all = -jnp.exp(avec) * sp
    pick = lambda t, ln: jnp.sum(jnp.where(lane == ln, t, 0.0), axis=-1, keepdims=True)
    beta = pick(beta_all, h)
    graw = pick(g_all, DN_HEADS + h)
    zc = pick(z, DN_HEADS + h)
    to_row = lambda c: jnp.sum(eye * c, axis=0, keepdims=True)
    gc = jnp.sum(jnp.where(incl, to_row(graw), 0.0), axis=-1, keepdims=True)
    gc_row = to_row(gc)
    decay = jnp.exp(jnp.where(incl, gc - gc_row, NEG_BIG))
    q = q_raw * (DN_HEAD_DIM ** -0.5)
    kb = k * beta
    kk = _nt(kb, k, HI)
    if t is None:
        t = _tri_inverse(jnp.where(strict, kk * decay, 0.0), blk, eye)
    eg = jnp.exp(gc)
    rhs_u = v * beta
    rhs_w = kb * eg
    u = _nn(t, rhs_u, HI)
    w = _nn(t, rhs_w, HI)
    qk = _nt(q, k, HI)
    aq = jnp.where(incl, qk * decay, 0.0)
    g_last = jnp.sum(jnp.where(lax.broadcasted_iota(jnp.int32, (C, 1), 0) == C - 1, gc, 0.0), axis=0, keepdims=True)
    ekd = jnp.exp(g_last - gc)
    return dict(beta=beta, graw=graw, zc=zc, gc=gc, decay=decay, q=q, kb=kb, kk=kk, t=t, eg=eg, rhs_w=rhs_w,
                u=u, w=w, qk=qk, aq=aq, g_last=g_last, ekd=ekd, kd=k * ekd, qg=q * eg,
                incl=incl, strict=strict, eye=eye, lane=lane, row=row, col=col)


PREP_CHUNKS = 2
SCAN_CHUNKS = 8


def _dn_prep(qn, kn, v, bd, avec, dvec):
    S = qn.shape[0]
    C = DN_CHUNK
    N = S // C
    HD = DN_HEAD_DIM
    nc = PREP_CHUNKS

    def body(q_ref, k_ref, v_ref, bd_ref, a_ref, d_ref, u_ref, w_ref, qg_ref, kd_ref, aq_ref, t_ref, egl_ref):
        for ci in range(nc):
            rows = slice(ci * C, (ci + 1) * C)
            bd = bd_ref[rows, :]
            egl = []
            for h in range(DN_HEADS):
                sl = slice(h * HD, (h + 1) * HD)
                c = _dn_chunk_common(bd, a_ref[...], d_ref[...], h, q_ref[rows, sl], k_ref[rows, sl], v_ref[rows, sl])
                u_ref[rows, sl] = c["u"]
                w_ref[rows, sl] = c["w"]
                qg_ref[rows, sl] = c["qg"]
                kd_ref[rows, sl] = c["kd"]
                aq_ref[h, rows, :] = c["aq"]
                t_ref[h, rows, :] = c["t"]
                egl.append(jnp.broadcast_to(jnp.exp(c["g_last"]), (1, 128)))
            egl_ref[ci * 8:(ci + 1) * 8, :] = jnp.concatenate(egl + [jnp.zeros((8 - DN_HEADS, 128), F32)], axis=0)

    tok = lambda w: pl.BlockSpec((nc * C, w), lambda n: (n, 0))
    sq = pl.BlockSpec((DN_HEADS, nc * C, C), lambda n: (0, n, 0))
    vec = pl.BlockSpec((1, 128), lambda n: (0, 0))
    return pl.pallas_call(
        body, name="dn_prep", grid=(N // nc,),
        in_specs=[tok(DN_WIDTH)] * 3 + [tok(128), vec, vec],
        out_specs=[tok(DN_WIDTH)] * 4 + [sq, sq, pl.BlockSpec((nc * 8, 128), lambda n: (n, 0))],
        out_shape=[jax.ShapeDtypeStruct((S, DN_WIDTH), F32)] * 4 + [jax.ShapeDtypeStruct((DN_HEADS, S, C), F32)] * 2
                  + [jax.ShapeDtypeStruct((N * 8, 128), F32)],
        compiler_params=_params(1),
    )(qn, kn, v, bd, avec, dvec)


def _dn_scan_fwd(u, w, qg, kd, aq, egl, gate, dn_gain):
    S = u.shape[0]
    C = DN_CHUNK
    N = S // C
    HD = DN_HEAD_DIM
    nc = SCAN_CHUNKS

    def body(u_ref, w_ref, qg_ref, kd_ref, aq_ref, egl_ref, gate_ref, gain_ref, dn_ref, o_ref, vn_ref, st_ref, state_ref):
        @pl.when(pl.program_id(0) == 0)
        def _():
            state_ref[...] = jnp.zeros_like(state_ref)

        for ci in range(nc):
            rows = slice(ci * C, (ci + 1) * C)
            st_ref[ci * DN_WIDTH:(ci + 1) * DN_WIDTH, :] = state_ref[...]
            for h in range(DN_HEADS):
                sl = slice(h * HD, (h + 1) * HD)
                st = state_ref[sl, :]
                v_new = u_ref[rows, sl] - _nn(w_ref[rows, sl], st, HI)
                o = _nn(qg_ref[rows, sl], st, HI) + _nn(aq_ref[h, rows, :], v_new, HI)
                state_ref[sl, :] = st * egl_ref[ci * 8 + h:ci * 8 + h + 1, :] + _tn(kd_ref[rows, sl], v_new, HI)
                vn_ref[rows, sl] = v_new
                o_ref[rows, sl] = o
                r = lax.rsqrt(jnp.mean(o * o, axis=-1, keepdims=True) + NORM_EPS)
                gt = gate_ref[rows, sl]
                dn_ref[rows, sl] = o * r * gain_ref[...] * (gt * _sigmoid(gt))

    tok = lambda wd: pl.BlockSpec((nc * C, wd), lambda n: (n, 0))
    sq = pl.BlockSpec((DN_HEADS, nc * C, C), lambda n: (0, n, 0))
    vec = pl.BlockSpec((1, 128), lambda n: (0, 0))
    return pl.pallas_call(
        body, name="dn_scan_fwd", grid=(N // nc,),
        in_specs=[tok(DN_WIDTH)] * 4 + [sq, pl.BlockSpec((nc * 8, 128), lambda n: (n, 0)), tok(DN_WIDTH), vec],
        out_specs=[tok(DN_WIDTH)] * 3 + [pl.BlockSpec((nc * DN_WIDTH, HD), lambda n: (n, 0))],
        out_shape=[jax.ShapeDtypeStruct((S, DN_WIDTH), F32)] * 3 + [jax.ShapeDtypeStruct((N * DN_WIDTH, HD), F32)],
        scratch_shapes=[pltpu.VMEM((DN_WIDTH, HD), F32)],
        compiler_params=_params(1),
    )(u, w, qg, kd, aq, egl, gate, dn_gain)


def _dn_scan_bwd(w, qg, kd, aq, egl, gate, dn_gain, o, ddn):
    S = w.shape[0]
    C = DN_CHUNK
    N = S // C
    HD = DN_HEAD_DIM
    nc = SCAN_CHUNKS

    def body(w_ref, qg_ref, kd_ref, aq_ref, egl_ref, gate_ref, gain_ref, o_ref, ddn_ref,
             do_ref, dvn_ref, dgate_ref, dst_ref, small_ref, dstate_ref):
        @pl.when(pl.program_id(0) == 0)
        def _():
            dstate_ref[...] = jnp.zeros_like(dstate_ref)
            small_ref[...] = jnp.zeros_like(small_ref)

        gain = gain_ref[...]
        d_gain = jnp.zeros((1, 128), F32)
        for ci in reversed(range(nc)):
            rows = slice(ci * C, (ci + 1) * C)
            dst_ref[ci * DN_WIDTH:(ci + 1) * DN_WIDTH, :] = dstate_ref[...]
            for h in range(DN_HEADS):
                sl = slice(h * HD, (h + 1) * HD)
                ov = o_ref[rows, sl]
                r = lax.rsqrt(jnp.mean(ov * ov, axis=-1, keepdims=True) + NORM_EPS)
                on = ov * r
                gt = gate_ref[rows, sl]
                sgt = _sigmoid(gt)
                silu_g = gt * sgt
                dy = ddn_ref[rows, sl]
                d_gain = d_gain + jnp.sum(dy * on * silu_g, axis=0, keepdims=True)
                dgate_ref[rows, sl] = dy * on * gain * (sgt * (1.0 + gt * (1.0 - sgt)))
                don = dy * gain * silu_g
                do = r * (don - on * jnp.mean(don * on, axis=-1, keepdims=True))
                do_ref[rows, sl] = do
                dsn = dstate_ref[sl, :]
                d_vnew = _tn(aq_ref[h, rows, :], do, HI) + _nn(kd_ref[rows, sl], dsn, HI)
                dvn_ref[rows, sl] = d_vnew
                dstate_ref[sl, :] = (_tn(qg_ref[rows, sl], do, HI) + dsn * egl_ref[ci * 8 + h:ci * 8 + h + 1, :]
                                     - _tn(w_ref[rows, sl], d_vnew, HI))
        small_ref[...] += jnp.concatenate([d_gain, jnp.zeros((7, 128), F32)], axis=0)

    nb = N // nc
    tok = lambda wd: pl.BlockSpec((nc * C, wd), lambda i: (nb - 1 - i, 0))
    sq = pl.BlockSpec((DN_HEADS, nc * C, C), lambda i: (0, nb - 1 - i, 0))
    vec = pl.BlockSpec((1, 128), lambda i: (0, 0))
    return pl.pallas_call(
        body, name="dn_scan_bwd", grid=(nb,),
        in_specs=[tok(DN_WIDTH)] * 3 + [sq, pl.BlockSpec((nc * 8, 128), lambda i: (nb - 1 - i, 0)), tok(DN_WIDTH), vec,
                                       tok(DN_WIDTH), tok(DN_WIDTH)],
        out_specs=[tok(DN_WIDTH)] * 3 + [pl.BlockSpec((nc * DN_WIDTH, HD), lambda i: (nb - 1 - i, 0)),
                                        pl.BlockSpec((8, 128), lambda i: (0, 0))],
        out_shape=[jax.ShapeDtypeStruct((S, DN_WIDTH), F32)] * 3 + [jax.ShapeDtypeStruct((N * DN_WIDTH, HD), F32),
                                                                  jax.ShapeDtypeStruct((8, 128), F32)],
        scratch_shapes=[pltpu.VMEM((DN_WIDTH, HD), F32)],
        compiler_params=_params(1),
    )(w, qg, kd, aq, egl, gate, dn_gain, o, ddn)


def _dn_post(qn, kn, v, bd, avec, dvec, t_inv, v_new_all, states, dstates, do_all, dvn_all):
    S = qn.shape[0]
    C = DN_CHUNK
    N = S // C
    HD = DN_HEAD_DIM
    nc = PREP_CHUNKS

    def body(q_ref, k_ref, v_ref, bd_ref, a_ref, d_ref, t_ref, vn_ref, st_ref, dst_ref, do_ref, dvn_ref,
             dq_ref, dk_ref, dv_ref, dbd_ref, small_ref):
        @pl.when(pl.program_id(0) == 0)
        def _():
            small_ref[...] = jnp.zeros_like(small_ref)

        for ci in range(nc):
            tok = lambda r: r.at[pl.ds(ci * C, C)]
            big = lambda r: r.at[pl.ds(ci * DN_WIDTH, DN_WIDTH)]
            chunk(tok(q_ref), tok(k_ref), tok(v_ref), tok(bd_ref), a_ref, d_ref, t_ref.at[:, pl.ds(ci * C, C)], tok(vn_ref),
                  big(st_ref), big(dst_ref), tok(do_ref), tok(dvn_ref), tok(dq_ref), tok(dk_ref), tok(dv_ref), tok(dbd_ref),
                  small_ref)

    def chunk(q_ref, k_ref, v_ref, bd_ref, a_ref, d_ref, t_ref, vn_ref, st_ref, dstate_ref, do_ref, dvn_ref,
              dq_ref, dk_ref, dv_ref, dbd_ref, small_ref):
        bd = bd_ref[...]
        avec = a_ref[...]
        dbd = jnp.zeros((C, 128), F32)
        d_alog = jnp.zeros((1, 128), F32)
        d_dt = jnp.zeros((1, 128), F32)
        for h in range(DN_HEADS):
            sl = slice(h * HD, (h + 1) * HD)
            k = k_ref[:, sl]
            vv = v_ref[:, sl]
            c = _dn_chunk_common(bd, avec, d_ref[...], h, q_ref[:, sl], k, vv, t=t_ref[h])
            q, kb, t, eg, u, w = c["q"], c["kb"], c["t"], c["eg"], c["u"], c["w"]
            beta, decay, incl, strict, eye = c["beta"], c["decay"], c["incl"], c["strict"], c["eye"]
            lane = c["lane"]
            st = st_ref[sl, :]
            dsn = dstate_ref[sl, :]
            v_new = vn_ref[:, sl]
            do = do_ref[:, sl]
            d_vnew = dvn_ref[:, sl]
            egl = jnp.exp(c["g_last"])
            daq = jnp.where(incl, _nt(do, v_new, HI), 0.0)
            d_qg = _nt(do, st, HI)
            d_kd = _nt(v_new, dsn, HI)
            d_glast = jnp.sum(jnp.sum(dsn * st, axis=-1, keepdims=True), axis=0, keepdims=True) * egl
            d_w = -_nt(d_vnew, st, HI)
            d_ru = _tn(t, d_vnew, HI)
            d_rw = _tn(t, d_w, HI)
            da = -jnp.where(strict, _nt(d_ru, u, HI) + _nt(d_rw, w, HI), 0.0)
            dv_ref[:, sl] = d_ru * beta
            dbeta = jnp.sum(d_ru * vv, axis=-1, keepdims=True)
            dkb = d_rw * eg
            dgc = jnp.sum(d_rw * c["rhs_w"], axis=-1, keepdims=True)
            dkk = da * decay
            ddecay = da * c["kk"]
            dkb = dkb + _nn(dkk, k, HI)
            dk = _tn(dkk, kb, HI)
            dqk = daq * decay
            ddecay = ddecay + daq * c["qk"]
            dq = _nn(dqk, k, HI)
            dk = dk + _tn(dqk, q, HI)
            m = ddecay * decay
            col_sum = jnp.sum(m, axis=0, keepdims=True)
            dgc = dgc + jnp.sum(m, axis=-1, keepdims=True) - jnp.sum(eye * col_sum, axis=-1, keepdims=True)
            dq = dq + d_qg * eg
            dgc = dgc + jnp.sum(d_qg * c["qg"], axis=-1, keepdims=True)
            dk = dk + d_kd * c["ekd"]
            tk = jnp.sum(d_kd * c["kd"], axis=-1, keepdims=True)
            dgc = dgc - tk
            d_glast = d_glast + jnp.sum(tk, axis=0, keepdims=True)
            dk = dk + dkb * beta
            dbeta = dbeta + jnp.sum(dkb * k, axis=-1, keepdims=True)
            dgc = dgc + jnp.where(lax.broadcasted_iota(jnp.int32, (C, 1), 0) == C - 1, d_glast, 0.0)
            dgc_row = jnp.sum(eye * dgc, axis=0, keepdims=True)
            dgraw = jnp.sum(jnp.where(c["col"] >= c["row"], dgc_row, 0.0), axis=-1, keepdims=True)
            dq_ref[:, sl] = dq * (HD ** -0.5)
            dk_ref[:, sl] = dk
            dbraw = dbeta * beta * (1.0 - beta)
            dz = dgraw * (-jnp.exp(avec)) * _sigmoid(c["zc"])
            dbd = dbd + jnp.where(lane == h, dbraw, 0.0) + jnp.where(lane == DN_HEADS + h, dz, 0.0)
            lane1 = lax.broadcasted_iota(jnp.int32, (1, 128), 1)
            d_alog = d_alog + jnp.where(lane1 == DN_HEADS + h, jnp.sum(dgraw * c["graw"], axis=0, keepdims=True), 0.0)
            d_dt = d_dt + jnp.where(lane1 == DN_HEADS + h, jnp.sum(dz, axis=0, keepdims=True), 0.0)
        dbd_ref[...] = dbd
        small_ref[...] += jnp.concatenate([d_alog, d_dt, jnp.zeros((6, 128), F32)], axis=0)

    tok = lambda wd: pl.BlockSpec((nc * C, wd), lambda n: (n, 0))
    big = pl.BlockSpec((nc * DN_WIDTH, HD), lambda n: (n, 0))
    sq = pl.BlockSpec((DN_HEADS, nc * C, C), lambda n: (0, n, 0))
    vec = pl.BlockSpec((1, 128), lambda n: (0, 0))
    return pl.pallas_call(
        body, name="dn_post", grid=(N // nc,),
        in_specs=[tok(DN_WIDTH)] * 3 + [tok(128), vec, vec, sq, tok(DN_WIDTH), big, big, tok(DN_WIDTH), tok(DN_WIDTH)],
        out_specs=[tok(DN_WIDTH)] * 3 + [tok(128), pl.BlockSpec((8, 128), lambda n: (0, 0))],
        out_shape=[jax.ShapeDtypeStruct((S, DN_WIDTH), F32)] * 3 + [jax.ShapeDtypeStruct((S, 128), F32),
                                                                  jax.ShapeDtypeStruct((8, 128), F32)],
        compiler_params=_params(1),
    )(qn, kn, v, bd, avec, dvec, t_inv, v_new_all, states, dstates, do_all, dvn_all)


def _outproj_fwd(x, attn, dn, w_out):
    S, D = x.shape
    tm = 512

    def body(x_ref, a_ref, d_ref, w_ref, xo_ref, mix_ref):
        a = a_ref[...].astype(BF16)
        dd = d_ref[...].astype(BF16)
        mix_ref[:, 0:ATTN_WIDTH] = a
        mix_ref[:, ATTN_WIDTH:] = dd
        xo_ref[...] = x_ref[...] + _nn(a, w_ref[0:ATTN_WIDTH, :]) + _nn(dd, w_ref[ATTN_WIDTH:, :])

    tok = lambda w: pl.BlockSpec((tm, w), lambda i: (i, 0))
    return pl.pallas_call(
        body, name="outproj_fwd", grid=(S // tm,),
        in_specs=[tok(D), tok(ATTN_WIDTH), tok(DN_WIDTH), pl.BlockSpec((D, D), lambda i: (0, 0))],
        out_specs=[tok(D), tok(D)],
        out_shape=[jax.ShapeDtypeStruct((S, D), F32), jax.ShapeDtypeStruct((S, D), BF16)],
        compiler_params=_params(1),
    )(x, attn, dn, w_out)


def _outproj_bwd(dx, w_outT):
    S, D = dx.shape
    tm = 512

    def body(dx_ref, wT_ref, da_ref, dd_ref, dxb_ref):
        d = dx_ref[...].astype(BF16)
        dxb_ref[...] = d
        da_ref[...] = _nn(d, wT_ref[:, 0:ATTN_WIDTH])
        dd_ref[...] = _nn(d, wT_ref[:, ATTN_WIDTH:])

    tok = lambda w: pl.BlockSpec((tm, w), lambda i: (i, 0))
    return pl.pallas_call(
        body, name="outproj_bwd", grid=(S // tm,),
        in_specs=[tok(D), pl.BlockSpec((D, D), lambda i: (0, 0))],
        out_specs=[tok(ATTN_WIDTH), tok(DN_WIDTH), tok(D)],
        out_shape=[jax.ShapeDtypeStruct((S, ATTN_WIDTH), F32), jax.ShapeDtypeStruct((S, DN_WIDTH), F32),
                   jax.ShapeDtypeStruct((S, D), BF16)],
        compiler_params=_params(1),
    )(dx, w_outT)


def _loss_head(x, gain, target):
    S, D = x.shape
    tm = 512

    def body(x_ref, gain_ref, t_ref, loss_ref, dx_ref, dgain_ref):
        @pl.when(pl.program_id(0) == 0)
        def _():
            loss_ref[...] = jnp.zeros_like(loss_ref)
            dgain_ref[...] = jnp.zeros_like(dgain_ref)

        xf = x_ref[...]
        gain = gain_ref[...]
        r = lax.rsqrt(jnp.mean(xf * xf, axis=-1, keepdims=True) + NORM_EPS)
        xhat = xf * r
        err = xhat * gain - t_ref[...]
        part = 0.5 * jnp.sum(jnp.mean(err * err, axis=-1, keepdims=True), axis=0, keepdims=True)
        first = (lax.broadcasted_iota(jnp.int32, (8, 128), 0) == 0) & (lax.broadcasted_iota(jnp.int32, (8, 128), 1) == 0)
        loss_ref[...] += jnp.where(first, part, 0.0)
        dy = err * (1.0 / D)
        dgain_ref[...] += jnp.sum(dy * xhat, axis=0, keepdims=True)
        dxh = dy * gain
        dx_ref[...] = r * (dxh - xhat * jnp.mean(dxh * xhat, axis=-1, keepdims=True))

    tok = pl.BlockSpec((tm, D), lambda i: (i, 0))
    row = pl.BlockSpec((1, D), lambda i: (0, 0))
    return pl.pallas_call(
        body, name="loss_head", grid=(S // tm,),
        in_specs=[tok, row, tok],
        out_specs=[pl.BlockSpec((8, 128), lambda i: (0, 0)), tok, row],
        out_shape=[jax.ShapeDtypeStruct((8, 128), F32), jax.ShapeDtypeStruct((S, D), F32),
                   jax.ShapeDtypeStruct((1, D), F32)],
        compiler_params=_params(1),
    )(x, gain, target)


def _adamw(w, g, m, v, name):
    R, Ccols = w.shape
    tr = R
    for cand in (256, 128, 64, 32, 16, 8):
        if R % cand == 0:
            tr = cand
            break
    c1 = 1.0 - ADAM_B1 ** ADAM_STEP
    c2 = 1.0 - ADAM_B2 ** ADAM_STEP

    def body(w_ref, g_ref, m_ref, v_ref, d_ref, nm_ref, nv_ref):
        gv = g_ref[...]
        mn = ADAM_B1 * m_ref[...] + (1.0 - ADAM_B1) * gv
        vn = ADAM_B2 * v_ref[...] + (1.0 - ADAM_B2) * (gv * gv)
        nm_ref[...] = mn
        nv_ref[...] = vn
        d_ref[...] = -ADAM_LR * ((mn / c1) / (jnp.sqrt(vn / c2) + ADAM_EPS) + ADAM_WD * w_ref[...])

    spec = pl.BlockSpec((tr, Ccols), lambda i: (i, 0))
    return pl.pallas_call(
        body, name=name, grid=(R // tr,), in_specs=[spec] * 4, out_specs=[spec] * 3,
        out_shape=[jax.ShapeDtypeStruct((R, Ccols), F32)] * 3, compiler_params=_params(1),
    )(w, g, m, v)


def _local_step(x, target, wts, small):
    T = lambda a: a.T
    g1, g2, gm, gf = small["norm_ffn1"], small["norm_ffn2"], small["norm_mix"], small["norm_final"]

    x1, h1, fg1, fu1 = _ffn_fwd(x, g1, wts["ffn1_gate"], wts["ffn1_up"], wts["ffn1_down"], "ffn1_fwd")
    h2, aq, ak, av, xq, xk, xv, gate, bd = _inproj_fwd(x1, gm, wts["w_in"])
    parts = [_attn_fwd(aq, ak, av, d, f"attn_fwd_d{d}") for d in DILATIONS]
    attn, lse = _attn_merge(parts)
    conv_w = small["conv_w"]
    qn, kn, vv = _conv_fwd(xq, xk, xv, conv_w)
    dn_u, dn_w, dn_qg, dn_kd, dn_aq, dn_t, dn_egl = _dn_prep(qn, kn, vv, bd, small["avec"], small["dvec"])
    dn, o_dn, v_new, states = _dn_scan_fwd(dn_u, dn_w, dn_qg, dn_kd, dn_aq, dn_egl, gate, small["dn_norm"])
    x2, mix = _outproj_fwd(x1, attn, dn, wts["w_out"])
    x3, h3, fg2, fu2 = _ffn_fwd(x2, g2, wts["ffn2_gate"], wts["ffn2_up"], wts["ffn2_down"], "ffn2_fwd")
    loss, dx3, d_gf = _loss_head(x3, gf, target)

    grads = {}
    dx2, d_g2, dfg2, dfu2, act2, dout2 = _ffn_bwd(dx3, x2, g2, fg2, fu2, T(wts["ffn2_down"]), T(wts["ffn2_gate"]),
                                                 T(wts["ffn2_up"]), "ffn2_bwd")
    tk = 512
    grads["ffn2_gate"] = _matmul_tn(dfg2, h3, D_FF // 2, tk, "dw_ffn2_gate")
    grads["ffn2_up"] = _matmul_tn(dfu2, h3, D_FF // 2, tk, "dw_ffn2_up")
    grads["ffn2_down"] = _matmul_tn(act2, dout2, D_FF // 2, tk, "dw_ffn2_down")

    dattn, ddn, dx2b = _outproj_bwd(dx2, T(wts["w_out"]))
    grads["w_out"] = _matmul_tn(mix, dx2b, D_MODEL, tk, "dw_out")

    dd = _attn_delta(dattn, attn)
    daq = dak = dav = None
    for d in DILATIONS:
        daq = _attn_bwd_q(aq, ak, av, dattn, lse, dd, daq, d, f"attn_bwd_q_d{d}")
        dak, dav = _attn_bwd_kv(aq, ak, av, dattn, lse, dd, dak, dav, d, f"attn_bwd_kv_d{d}")

    do_dn, dvn, dgate, dstates, d_dn_gain = _dn_scan_bwd(dn_w, dn_qg, dn_kd, dn_aq, dn_egl, gate, small["dn_norm"], o_dn, ddn)
    dqn, dkn, dvv, dbd, dn_small = _dn_post(qn, kn, vv, bd, small["avec"], small["dvec"], dn_t, v_new, states, dstates,
                                            do_dn, dvn)
    dcq, dck, dcv, dwq, dwk, dwv = _conv_bwd_pre(xq, xk, xv, conv_w, dqn, dkn, dvv)
    dxq, dxk, dxv = _conv_bwd_x(dcq, dck, dcv, conv_w)
    d_conv = jnp.concatenate([dwq[:CONV_WIDTH], dwk[:CONV_WIDTH], dwv[:CONV_WIDTH]], axis=1)

    dx1, d_gm, dproj = _inproj_bwd(dx2, x1, gm, [daq, dak, dav, dxq, dxk, dxv, dgate], dbd, T(wts["w_in"]))
    grads["w_in"] = _matmul_tn(dproj, h2, IN_COLS_PADDED, 256, "dw_in")

    dx0, d_g1, dfg1, dfu1, act1, dout1 = _ffn_bwd(dx1, x, g1, fg1, fu1, T(wts["ffn1_down"]), T(wts["ffn1_gate"]),
                                                 T(wts["ffn1_up"]), "ffn1_bwd")
    grads["ffn1_gate"] = _matmul_tn(dfg1, h1, D_FF // 2, tk, "dw_ffn1_gate")
    grads["ffn1_up"] = _matmul_tn(dfu1, h1, D_FF // 2, tk, "dw_ffn1_up")
    grads["ffn1_down"] = _matmul_tn(act1, dout1, D_FF // 2, tk, "dw_ffn1_down")

    small_grads = dict(norm_ffn1=d_g1, norm_mix=d_gm, norm_ffn2=d_g2, norm_final=d_gf, conv_w=d_conv,
                       a_log=dn_small[0:1], dt_bias=dn_small[1:2], dn_norm=d_dn_gain[0:1])
    return loss, dx0, grads, small_grads


PACK_SECTIONS = (("ffn1_gate", 704), ("ffn1_up", 704), ("ffn1_down", 704), ("w_in", 898), ("w_out", 256),
                 ("ffn2_gate", 704), ("ffn2_up", 704), ("ffn2_down", 704))
PACK_ROWS = 5408
HALF_ROWS = PACK_ROWS // 2
ADD_ROWS = 208

HBM = pl.BlockSpec(memory_space=pl.ANY)
VMEM_SPEC = pl.BlockSpec(memory_space=pltpu.VMEM)


def _coords():
    return lax.axis_index("x"), lax.axis_index("y"), lax.axis_index("c")


def _remote(src, dst, send_sems, recv_sems, k, dev):
    return pltpu.make_async_remote_copy(src_ref=src, dst_ref=dst, send_sem=send_sems.at[k], recv_sem=recv_sems.at[k],
                                        device_id=dev, device_id_type=MESH)


def _allreduce_small(buf, name):
    R, Cc = buf.shape

    def body(src_ref, out_ref, recv_ref, send_sems, recv_sems):
        x, y, c = _coords()
        copies = []
        for m in range(1, 8):
            fx, fy, fc = (m >> 2) & 1, (m >> 1) & 1, m & 1
            dev = (x ^ fx if fx else x, y ^ fy if fy else y, c ^ fc if fc else c)
            cp = _remote(src_ref, recv_ref.at[m - 1], send_sems, recv_sems, m - 1, dev)
            cp.start()
            copies.append(cp)
        for cp in copies:
            cp.wait()
        r = [src_ref[...]] + [recv_ref[m] for m in range(7)]
        out_ref[...] = ((r[0] + r[1]) + (r[2] + r[3])) + ((r[4] + r[5]) + (r[6] + r[7]))

    return pl.pallas_call(
        body, name=name, out_shape=jax.ShapeDtypeStruct((R, Cc), F32),
        in_specs=[VMEM_SPEC], out_specs=VMEM_SPEC,
        scratch_shapes=[pltpu.VMEM((7, R, Cc), F32), pltpu.SemaphoreType.DMA((7,)), pltpu.SemaphoreType.DMA((7,))],
    )(buf)


def _allgather_weights(pack2):
    _, Hh, Cc = pack2.shape

    def body(src_ref, out_ref, send_sems, recv_sems):
        x, y, c = _coords()
        sib = (x, y, 1 - c)
        others = [(1 - x, y), (x, 1 - y), (1 - x, 1 - y)]
        blk = lambda cx, cy, half: out_ref.at[2 * cx + cy, half]
        mine = _remote(src_ref, out_ref.at[2 * x + y], send_sems, recv_sems, 6, sib)
        mine.start()
        first = [_remote(src_ref.at[c], blk(x, y, c), send_sems, recv_sems, j, (ox, oy, c)) for j, (ox, oy) in enumerate(others)]
        for cp in first:
            cp.start()
        passed = [_remote(blk(ox, oy, c), blk(ox, oy, c), send_sems, recv_sems, 3 + j, sib) for j, (ox, oy) in enumerate(others)]
        for j, (ox, oy) in enumerate(others):
            _remote(src_ref.at[c], blk(ox, oy, c), send_sems, recv_sems, j, (ox, oy, c)).wait_recv()
            passed[j].start()
        for j, (ox, oy) in enumerate(others):
            _remote(src_ref.at[c], blk(ox, oy, 1 - c), send_sems, recv_sems, 3 + j, sib).wait_recv()
        for cp in first + passed:
            cp.wait_send()
        mine.wait()

    return pl.pallas_call(
        body, name="allgather_weights", out_shape=jax.ShapeDtypeStruct((N_CHIPS, 2, Hh, Cc), pack2.dtype),
        in_specs=[HBM], out_specs=HBM,
        scratch_shapes=[pltpu.SemaphoreType.DMA((7,)), pltpu.SemaphoreType.DMA((7,))],
    )(pack2)


def _rs_swap_halves(gpack):
    _, nj, Hh, Cc = gpack.shape

    def body(src_ref, out_ref, send_sems, recv_sems):
        x, y, c = _coords()
        cp = _remote(src_ref.at[1 - c], out_ref, send_sems, recv_sems, 0, (x, y, 1 - c))
        cp.start()
        cp.wait()

    return pl.pallas_call(
        body, name="rs_swap_halves", out_shape=jax.ShapeDtypeStruct((nj, Hh, Cc), gpack.dtype),
        in_specs=[HBM], out_specs=HBM,
        scratch_shapes=[pltpu.SemaphoreType.DMA((1,)), pltpu.SemaphoreType.DMA((1,))],
    )(gpack)


def _rs_add_pair(gpack, other, c):
    _, nj, Hh, Cc = gpack.shape
    tr = ADD_ROWS

    def body(c_ref, a_ref, b_ref, o_ref):
        o_ref[...] = (a_ref[...] + b_ref[...]).astype(BF16)

    return pl.pallas_call(
        body, name="rs_add_pair",
        grid_spec=pltpu.PrefetchScalarGridSpec(
            num_scalar_prefetch=1, grid=(nj, Hh // tr),
            in_specs=[pl.BlockSpec((None, None, tr, Cc), lambda j, i, c_ref: (c_ref[0], j, i, 0)),
                      pl.BlockSpec((None, tr, Cc), lambda j, i, c_ref: (j, i, 0))],
            out_specs=pl.BlockSpec((None, tr, Cc), lambda j, i, c_ref: (j, i, 0))),
        out_shape=jax.ShapeDtypeStruct((nj, Hh, Cc), BF16),
        compiler_params=_params(2),
    )(c, gpack, other)


def _rs_exchange_chips(part):
    nj, Hh, Cc = part.shape

    def body(src_ref, out_ref, send_sems, recv_sems):
        x, y, c = _coords()
        others = [(1 - x, y), (x, 1 - y), (1 - x, 1 - y)]
        cps = [_remote(src_ref.at[2 * ox + oy], out_ref.at[k], send_sems, recv_sems, k, (ox, oy, c))
               for k, (ox, oy) in enumerate(others)]
        for cp in cps:
            cp.start()
        for cp in cps:
            cp.wait()

    return pl.pallas_call(
        body, name="rs_exchange_chips", out_shape=jax.ShapeDtypeStruct((3, Hh, Cc), part.dtype),
        in_specs=[HBM], out_specs=HBM,
        scratch_shapes=[pltpu.SemaphoreType.DMA((3,)), pltpu.SemaphoreType.DMA((3,))],
    )(part)


def _rs_add_total(part, recv, chip):
    nj, Hh, Cc = part.shape
    tr = ADD_ROWS

    def body(chip_ref, p_ref, r0_ref, r1_ref, r2_ref, o_ref):
        f = lambda r: r[...].astype(F32)
        o_ref[...] = (f(p_ref) + f(r0_ref)) + (f(r1_ref) + f(r2_ref))

    rk = lambda k: pl.BlockSpec((None, tr, Cc), lambda i, chip_ref, k=k: (k, i, 0))
    return pl.pallas_call(
        body, name="rs_add_total",
        grid_spec=pltpu.PrefetchScalarGridSpec(
            num_scalar_prefetch=1, grid=(Hh // tr,),
            in_specs=[pl.BlockSpec((None, tr, Cc), lambda i, chip_ref: (chip_ref[0], i, 0)), rk(0), rk(1), rk(2)],
            out_specs=pl.BlockSpec((tr, Cc), lambda i, chip_ref: (i, 0))),
        out_shape=jax.ShapeDtypeStruct((Hh, Cc), F32),
        compiler_params=_params(1),
    )(chip, part, recv, recv, recv)


def _rs_share_total(total):
    Hh, Cc = total.shape

    def body(src_ref, out_ref, send_sems, recv_sems):
        x, y, c = _coords()
        cp = _remote(src_ref, out_ref, send_sems, recv_sems, 0, (x, y, 1 - c))
        cp.start()
        cp.wait()

    return pl.pallas_call(
        body, name="rs_share_total", out_shape=jax.ShapeDtypeStruct((Hh, Cc), total.dtype),
        in_specs=[HBM], out_specs=HBM,
        scratch_shapes=[pltpu.SemaphoreType.DMA((1,)), pltpu.SemaphoreType.DMA((1,))],
    )(total)


def _permute_w_in(w):
    return jnp.concatenate([w[:, :3072], w[:, 3080:IN_COLS], w[:, 3072:3080],
                            jnp.zeros((w.shape[0], IN_COLS_PADDED - IN_COLS), w.dtype)], axis=1)


def _pack_rows(shards, dtype):
    rows = [shards[n].astype(dtype).reshape(r, D_MODEL) for n, r in PACK_SECTIONS]
    used = sum(r for _, r in PACK_SECTIONS)
    rows.append(jnp.zeros((PACK_ROWS - used, D_MODEL), dtype))
    return jnp.concatenate(rows, axis=0)


def _unpack_rows(pack, shapes):
    out, at = {}, 0
    for n, r in PACK_SECTIONS:
        out[n] = pack[at:at + r].reshape(shapes[n])
        at += r
    return out


SHARD_SHAPES = dict(ffn1_gate=(1024, 704), ffn1_up=(1024, 704), ffn1_down=(704, 1024), w_in=(1024, 898),
                    w_out=(256, 1024), ffn2_gate=(1024, 704), ffn2_up=(1024, 704), ffn2_down=(704, 1024))
ROW_SHARDED = ("ffn1_down", "w_out", "ffn2_down")
SMALL_ROWS = 16


def _pad_row(v):
    v = v.reshape(1, -1)
    return jnp.pad(v, ((0, 0), (0, D_MODEL - v.shape[1])))


def kernel(x, norm_ffn1, ffn1_gate, ffn1_up, ffn1_down, norm_mix, w_in, conv_w, a_log, dt_bias, dn_norm, w_out, norm_ffn2, ffn2_gate, ffn2_up, ffn2_down, norm_final, loss_target, m_norm_ffn1, m_ffn1_gate, m_ffn1_up, m_ffn1_down, m_norm_mix, m_w_in, m_conv_w, m_a_log, m_dt_bias, m_dn_norm, m_w_out, m_norm_ffn2, m_ffn2_gate, m_ffn2_up, m_ffn2_down, m_norm_final, v_norm_ffn1, v_ffn1_gate, v_ffn1_up, v_ffn1_down, v_norm_mix, v_w_in, v_conv_w, v_a_log, v_dt_bias, v_dn_norm, v_w_out, v_norm_ffn2, v_ffn2_gate, v_ffn2_up, v_ffn2_down, v_norm_final):
    cx, cy, cc = _coords()
    chip = 2 * cx + cy
    big_w = dict(ffn1_gate=ffn1_gate[0], ffn1_up=ffn1_up[0], ffn1_down=ffn1_down[0], w_in=w_in[0], w_out=w_out[0],
                 ffn2_gate=ffn2_gate[0], ffn2_up=ffn2_up[0], ffn2_down=ffn2_down[0])
    big_m = dict(ffn1_gate=m_ffn1_gate[0], ffn1_up=m_ffn1_up[0], ffn1_down=m_ffn1_down[0], w_in=m_w_in[0], w_out=m_w_out[0],
                 ffn2_gate=m_ffn2_gate[0], ffn2_up=m_ffn2_up[0], ffn2_down=m_ffn2_down[0])
    big_v = dict(ffn1_gate=v_ffn1_gate[0], ffn1_up=v_ffn1_up[0], ffn1_down=v_ffn1_down[0], w_in=v_w_in[0], w_out=v_w_out[0],
                 ffn2_gate=v_ffn2_gate[0], ffn2_up=v_ffn2_up[0], ffn2_down=v_ffn2_down[0])

    pack = _pack_rows(big_w, BF16).reshape(2, HALF_ROWS, D_MODEL)
    gathered = _allgather_weights(pack).reshape(N_CHIPS, PACK_ROWS, D_MODEL)
    per_chip = [_unpack_rows(gathered[j], SHARD_SHAPES) for j in range(N_CHIPS)]
    wts = {n: jnp.concatenate([per_chip[j][n] for j in range(N_CHIPS)], axis=0 if n in ROW_SHARDED else 1)
           for n, _ in PACK_SECTIONS}
    wts["w_in"] = _permute_w_in(wts["w_in"])

    conv_shard = conv_w[0]
    emb = jnp.concatenate([jnp.where((chip == j) & (cc == 0), conv_shard, 0.0) for j in range(N_CHIPS)], axis=1)
    emb = jnp.pad(emb.reshape(6, D_MODEL), ((0, 2), (0, 0)))
    conv_full = _allreduce_small(emb, "allgather_conv_w")[:6].reshape(CONV_WIDTH, 3 * DN_WIDTH)

    zvec = jnp.zeros((1, 128), F32)
    small = dict(norm_ffn1=norm_ffn1, norm_mix=norm_mix, norm_ffn2=norm_ffn2, norm_final=norm_final[None],
                 conv_w=conv_full, avec=zvec.at[0, DN_HEADS:2 * DN_HEADS].set(a_log[0]),
                 dvec=zvec.at[0, DN_HEADS:2 * DN_HEADS].set(dt_bias[0]), dn_norm=dn_norm)

    loss, grad_x, grads, sg = _local_step(x[0], loss_target[0], wts, small)

    rows = [sg["norm_ffn1"], sg["norm_mix"], sg["norm_ffn2"], sg["norm_final"], _pad_row(sg["a_log"]), _pad_row(sg["dt_bias"]),
            _pad_row(sg["dn_norm"]), _pad_row(loss[0:1]), sg["conv_w"].reshape(6, D_MODEL), jnp.zeros((2, D_MODEL), F32)]
    red = _allreduce_small(jnp.concatenate(rows, axis=0), "allreduce_small")
    loss_out = red[7, 0]
    g_conv_full = red[8:14].reshape(CONV_WIDTH, 3 * DN_WIDTH)
    g_conv = lax.dynamic_slice_in_dim(g_conv_full, chip * (3 * DN_WIDTH // N_CHIPS), 3 * DN_WIDTH // N_CHIPS, axis=1)
    g_small = dict(norm_ffn1=red[0:1], norm_mix=red[1:2], norm_ffn2=red[2:3], norm_final=red[3],
                   a_log=red[4:5, DN_HEADS:2 * DN_HEADS], dt_bias=red[5:6, DN_HEADS:2 * DN_HEADS], dn_norm=red[6:7, :DN_HEAD_DIM])

    gi = grads["w_in"]
    grads["w_in"] = jnp.concatenate([gi[:3072], gi[3584:3592], gi[3072:3584]], axis=0)
    packs = []
    for j in range(N_CHIPS):
        sh = {n: grads[n][j * r:(j + 1) * r] for n, r in PACK_SECTIONS}
        packs.append(_pack_rows(sh, F32).reshape(2, HALF_ROWS, D_MODEL))
    gpack = jnp.stack(packs, axis=1)
    from_sibling = _rs_swap_halves(gpack)
    part = _rs_add_pair(gpack, from_sibling, cc.reshape(1).astype(jnp.int32))
    recv = _rs_exchange_chips(part)
    total = _rs_add_total(part, recv, chip.reshape(1).astype(jnp.int32))
    other = _rs_share_total(total)
    reduced = jnp.where(cc == 0, jnp.concatenate([total, other], axis=0), jnp.concatenate([other, total], axis=0))
    shard_g, at = {}, 0
    for n, r in PACK_SECTIONS:
        sec = reduced[at:at + r]
        shard_g[n] = sec if n in ROW_SHARDED else sec.T
        at += r

    out_g, out_d, out_m, out_v = {}, {}, {}, {}
    for n, _ in PACK_SECTIONS:
        d, nm, nv = _adamw(big_w[n], shard_g[n], big_m[n], big_v[n], "adamw_" + n)
        out_g[n], out_d[n], out_m[n], out_v[n] = shard_g[n][None], d[None], nm[None], nv[None]
    d, nm, nv = _adamw(conv_w[0], g_conv, m_conv_w[0], v_conv_w[0], "adamw_conv_w")
    out_g["conv_w"], out_d["conv_w"], out_m["conv_w"], out_v["conv_w"] = g_conv[None], d[None], nm[None], nv[None]

    small_names = ("norm_ffn1", "norm_mix", "norm_ffn2", "norm_final", "a_log", "dt_bias", "dn_norm")
    small_w = dict(norm_ffn1=norm_ffn1, norm_mix=norm_mix, norm_ffn2=norm_ffn2, norm_final=norm_final, a_log=a_log,
                   dt_bias=dt_bias, dn_norm=dn_norm)
    small_m = dict(norm_ffn1=m_norm_ffn1, norm_mix=m_norm_mix, norm_ffn2=m_norm_ffn2, norm_final=m_norm_final, a_log=m_a_log,
                   dt_bias=m_dt_bias, dn_norm=m_dn_norm)
    small_v = dict(norm_ffn1=v_norm_ffn1, norm_mix=v_norm_mix, norm_ffn2=v_norm_ffn2, norm_final=v_norm_final, a_log=v_a_log,
                   dt_bias=v_dt_bias, dn_norm=v_dn_norm)
    stack = lambda dct: jnp.concatenate([_pad_row(dct[n]) for n in small_names] + [jnp.zeros((1, D_MODEL), F32)], axis=0)
    d, nm, nv = _adamw(stack(small_w), stack(g_small), stack(small_m), stack(small_v), "adamw_small")
    for k, n in enumerate(small_names):
        shape = small_w[n].shape
        size = math.prod(shape)
        out_g[n] = g_small[n].reshape(shape)
        out_d[n], out_m[n], out_v[n] = (t[k, :size].reshape(shape) for t in (d, nm, nv))

    order = ("norm_ffn1", "ffn1_gate", "ffn1_up", "ffn1_down", "norm_mix", "w_in", "conv_w", "a_log", "dt_bias", "dn_norm",
             "w_out", "norm_ffn2", "ffn2_gate", "ffn2_up", "ffn2_down", "norm_final")
    return (loss_out, grad_x[None], *[out_g[n] for n in order], *[out_d[n] for n in order],
            *[out_m[n] for n in order], *[out_v[n] for n in order])
```

```python
import functools
import math

import jax
import jax.numpy as jnp
from jax import lax
from jax.experimental import pallas as pl
from jax.experimental.pallas import tpu as pltpu

F32 = jnp.float32
BF16 = jnp.bfloat16
HI = lax.Precision.HIGH

D_MODEL = 1024
D_FF = 2816
ATTN_HEADS = 8
ATTN_WIDTH = 512
ATTN_BLOCK = 128
DILATIONS = (1, 4, 16)
DN_HEADS = 4
DN_HEAD_DIM = 128
DN_WIDTH = 512
DN_CHUNK = 64
CONV_WIDTH = 4
NORM_EPS = 1e-6
L2_EPS = 1e-6
IN_COLS = 3592
IN_COLS_PADDED = 3712
N_CHIPS = 4

ADAM_LR = 0.001
ADAM_B1 = 0.9
ADAM_B2 = 0.999
ADAM_EPS = 1e-08
ADAM_WD = 0.01
ADAM_STEP = 10

VMEM_LIMIT = 56 * 1024 * 1024
NEG_BIG = -1e30
MESH = pl.DeviceIdType.MESH


def _params(n_grid, vmem=VMEM_LIMIT):
    return pltpu.CompilerParams(dimension_semantics=("arbitrary",) * n_grid, vmem_limit_bytes=vmem)


def _nt(a, b, precision=None):
    return lax.dot_general(a, b, (((1,), (1,)), ((), ())), preferred_element_type=F32, precision=precision)


def _tn(a, b, precision=None):
    return lax.dot_general(a, b, (((0,), (0,)), ((), ())), preferred_element_type=F32, precision=precision)


def _nn(a, b, precision=None):
    return jnp.dot(a, b, preferred_element_type=F32, precision=precision)


def _sigmoid(x):
    return 1.0 / (1.0 + jnp.exp(-x))


def _ffn_fwd(x, gain, wg, wu, wd, name):
    S, D = x.shape
    F = wg.shape[1]
    tm, tf = 512, F // 2
    nf = F // tf

    def body(x_ref, gain_ref, wg_ref, wu_ref, wd_ref, xo_ref, h_ref, g_ref, u_ref, acc_ref, hs_ref):
        j = pl.program_id(1)

        @pl.when(j == 0)
        def _():
            xf = x_ref[...]
            r = lax.rsqrt(jnp.mean(xf * xf, axis=-1, keepdims=True) + NORM_EPS)
            h = (xf * r * gain_ref[...]).astype(BF16)
            hs_ref[...] = h
            h_ref[...] = h
            acc_ref[...] = jnp.zeros_like(acc_ref)

        h = hs_ref[...]
        g = _nn(h, wg_ref[...])
        u = _nn(h, wu_ref[...])
        g_ref[...] = g.astype(BF16)
        u_ref[...] = u.astype(BF16)
        act = g * _sigmoid(g) * u
        acc_ref[...] += _nn(act.astype(BF16), wd_ref[...])

        @pl.when(j == nf - 1)
        def _():
            xo_ref[...] = x_ref[...] + 0.5 * acc_ref[...]

    return pl.pallas_call(
        body, name=name, grid=(S // tm, nf),
        in_specs=[pl.BlockSpec((tm, D), lambda i, j: (i, 0)),
                  pl.BlockSpec((1, D), lambda i, j: (0, 0)),
                  pl.BlockSpec((D, tf), lambda i, j: (0, j)),
                  pl.BlockSpec((D, tf), lambda i, j: (0, j)),
                  pl.BlockSpec((tf, D), lambda i, j: (j, 0))],
        out_specs=[pl.BlockSpec((tm, D), lambda i, j: (i, 0)),
                   pl.BlockSpec((tm, D), lambda i, j: (i, 0)),
                   pl.BlockSpec((tm, tf), lambda i, j: (i, j)),
                   pl.BlockSpec((tm, tf), lambda i, j: (i, j))],
        out_shape=[jax.ShapeDtypeStruct((S, D), F32), jax.ShapeDtypeStruct((S, D), BF16),
                   jax.ShapeDtypeStruct((S, F), BF16), jax.ShapeDtypeStruct((S, F), BF16)],
        scratch_shapes=[pltpu.VMEM((tm, D), F32), pltpu.VMEM((tm, D), BF16)],
        compiler_params=_params(2),
    )(x, gain, wg, wu, wd)


def _rmsnorm_bwd(dh, xf, gain):
    r = lax.rsqrt(jnp.mean(xf * xf, axis=-1, keepdims=True) + NORM_EPS)
    xhat = xf * r
    dgain = jnp.sum(dh * xhat, axis=0, keepdims=True)
    dxh = dh * gain
    dx = r * (dxh - xhat * jnp.mean(dxh * xhat, axis=-1, keepdims=True))
    return dx, dgain


def _ffn_bwd(dxo, x, gain, g, u, wd, wg, wu, name):
    S, D = x.shape
    F = g.shape[1]
    tm, tf = 256, F // 2
    nf = F // tf

    def body(dxo_ref, x_ref, gain_ref, g_ref, u_ref, wd_ref, wg_ref, wu_ref,
             dx_ref, dgain_ref, dg_ref, du_ref, act_ref, dout_ref, acc_ref, ds_ref):
        i = pl.program_id(0)
        j = pl.program_id(1)

        @pl.when(j == 0)
        def _():
            d = (0.5 * dxo_ref[...]).astype(BF16)
            ds_ref[...] = d
            dout_ref[...] = d
            acc_ref[...] = jnp.zeros_like(acc_ref)

        @pl.when((i == 0) & (j == 0))
        def _():
            dgain_ref[...] = jnp.zeros_like(dgain_ref)

        dact = _nt(ds_ref[...], wd_ref[...])
        gv = g_ref[...].astype(F32)
        uv = u_ref[...].astype(F32)
        sg = _sigmoid(gv)
        silu = gv * sg
        act_ref[...] = (silu * uv).astype(BF16)
        dgv = (dact * uv * (sg * (1.0 + gv * (1.0 - sg)))).astype(BF16)
        duv = (dact * silu).astype(BF16)
        dg_ref[...] = dgv
        du_ref[...] = duv
        acc_ref[...] += _nt(dgv, wg_ref[...]) + _nt(duv, wu_ref[...])

        @pl.when(j == nf - 1)
        def _():
            dx, dgain = _rmsnorm_bwd(acc_ref[...], x_ref[...], gain_ref[...])
            dx_ref[...] = dxo_ref[...] + dx
            dgain_ref[...] += dgain

    return pl.pallas_call(
        body, name=name, grid=(S // tm, nf),
        in_specs=[pl.BlockSpec((tm, D), lambda i, j: (i, 0)),
                  pl.BlockSpec((tm, D), lambda i, j: (i, 0)),
                  pl.BlockSpec((1, D), lambda i, j: (0, 0)),
                  pl.BlockSpec((tm, tf), lambda i, j: (i, j)),
                  pl.BlockSpec((tm, tf), lambda i, j: (i, j)),
                  pl.BlockSpec((tf, D), lambda i, j: (j, 0)),
                  pl.BlockSpec((D, tf), lambda i, j: (0, j)),
                  pl.BlockSpec((D, tf), lambda i, j: (0, j))],
        out_specs=[pl.BlockSpec((tm, D), lambda i, j: (i, 0)),
                   pl.BlockSpec((1, D), lambda i, j: (0, 0)),
                   pl.BlockSpec((tm, tf), lambda i, j: (i, j)),
                   pl.BlockSpec((tm, tf), lambda i, j: (i, j)),
                   pl.BlockSpec((tm, tf), lambda i, j: (i, j)),
                   pl.BlockSpec((tm, D), lambda i, j: (i, 0))],
        out_shape=[jax.ShapeDtypeStruct((S, D), F32), jax.ShapeDtypeStruct((1, D), F32),
                   jax.ShapeDtypeStruct((S, F), BF16), jax.ShapeDtypeStruct((S, F), BF16),
                   jax.ShapeDtypeStruct((S, F), BF16), jax.ShapeDtypeStruct((S, D), BF16)],
        scratch_shapes=[pltpu.VMEM((tm, D), F32), pltpu.VMEM((tm, D), BF16)],
        compiler_params=_params(2),
    )(dxo, x, gain, g, u, wd, wg, wu)


def _matmul_tn(a, b, tm, tk, name):
    K, M = a.shape
    N = b.shape[1]

    def body(a_ref, b_ref, o_ref):
        @pl.when(pl.program_id(1) == 0)
        def _():
            o_ref[...] = jnp.zeros_like(o_ref)

        o_ref[...] += _tn(a_ref[...], b_ref[...])

    return pl.pallas_call(
        body, name=name, grid=(M // tm, K // tk),
        in_specs=[pl.BlockSpec((tk, tm), lambda i, k: (k, i)),
                  pl.BlockSpec((tk, N), lambda i, k: (k, 0))],
        out_specs=pl.BlockSpec((tm, N), lambda i, k: (i, 0)),
        out_shape=jax.ShapeDtypeStruct((M, N), F32),
        compiler_params=_params(2),
    )(a, b)


def _inproj_fwd(x, gain, w_in_p):
    S, D = x.shape
    tm = 512
    W = ATTN_WIDTH

    def body(x_ref, gain_ref, w_ref, h_ref, aq_ref, ak_ref, av_ref, dq_ref, dk_ref, dv_ref, gate_ref, bd_ref):
        xf = x_ref[...]
        r = lax.rsqrt(jnp.mean(xf * xf, axis=-1, keepdims=True) + NORM_EPS)
        h = (xf * r * gain_ref[...]).astype(BF16)
        h_ref[...] = h
        aq_ref[...] = (_nn(h, w_ref[:, 0:W]) * 0.125).astype(BF16)
        ak_ref[...] = _nn(h, w_ref[:, W:2 * W]).astype(BF16)
        av_ref[...] = _nn(h, w_ref[:, 2 * W:3 * W]).astype(BF16)
        dq_ref[...] = _nn(h, w_ref[:, 3 * W:4 * W])
        dk_ref[...] = _nn(h, w_ref[:, 4 * W:5 * W])
        dv_ref[...] = _nn(h, w_ref[:, 5 * W:6 * W])
        gate_ref[...] = _nn(h, w_ref[:, 6 * W:7 * W])
        bd_ref[...] = _nn(h, w_ref[:, 7 * W:7 * W + 128])

    tok = lambda w: pl.BlockSpec((tm, w), lambda i: (i, 0))
    return pl.pallas_call(
        body, name="inproj_fwd", grid=(S // tm,),
        in_specs=[tok(D), pl.BlockSpec((1, D), lambda i: (0, 0)),
                  pl.BlockSpec((D, IN_COLS_PADDED), lambda i: (0, 0))],
        out_specs=[tok(D)] + [tok(W)] * 7 + [tok(128)],
        out_shape=[jax.ShapeDtypeStruct((S, D), BF16)] + [jax.ShapeDtypeStruct((S, W), BF16)] * 3
                  + [jax.ShapeDtypeStruct((S, W), F32)] * 4 + [jax.ShapeDtypeStruct((S, 128), F32)],
        compiler_params=_params(1),
    )(x, gain, w_in_p)


def _inproj_bwd(dxo, x, gain, dsecs, dbd, w_in_p):
    S, D = x.shape
    tm = 512
    W = ATTN_WIDTH

    def body(dxo_ref, x_ref, gain_ref, s0, s1, s2, s3, s4, s5, s6, dbd_ref, w_ref, dx_ref, dgain_ref, dproj_ref):
        @pl.when(pl.program_id(0) == 0)
        def _():
            dgain_ref[...] = jnp.zeros_like(dgain_ref)

        dh = jnp.zeros((tm, D), F32)
        for k, s in enumerate((s0, s1, s2, s3, s4, s5, s6)):
            d = s[...].astype(BF16)
            dproj_ref[:, k * W:(k + 1) * W] = d
            dh += _nt(d, w_ref[:, k * W:(k + 1) * W])
        d = dbd_ref[...].astype(BF16)
        dproj_ref[:, 7 * W:7 * W + 128] = d
        dh += _nt(d, w_ref[:, 7 * W:7 * W + 128])
        dx, dgain = _rmsnorm_bwd(dh, x_ref[...], gain_ref[...])
        dx_ref[...] = dxo_ref[...] + dx
        dgain_ref[...] += dgain

    tok = lambda w: pl.BlockSpec((tm, w), lambda i: (i, 0))
    return pl.pallas_call(
        body, name="inproj_bwd", grid=(S // tm,),
        in_specs=[tok(D), tok(D), pl.BlockSpec((1, D), lambda i: (0, 0))] + [tok(W)] * 7 + [tok(128)]
                 + [pl.BlockSpec((D, IN_COLS_PADDED), lambda i: (0, 0))],
        out_specs=[tok(D), pl.BlockSpec((1, D), lambda i: (0, 0)), tok(IN_COLS_PADDED)],
        out_shape=[jax.ShapeDtypeStruct((S, D), F32), jax.ShapeDtypeStruct((1, D), F32),
                   jax.ShapeDtypeStruct((S, IN_COLS_PADDED), BF16)],
        compiler_params=_params(1),
    )(dxo, x, gain, *dsecs, dbd, w_in_p)


def _slope(h):
    return 2.0 ** (-8.0 * (h + 1) / ATTN_HEADS)


def _attn_fwd(q, k, v, d, name):
    S = q.shape[0]
    L = S // d
    nb = L // ATTN_BLOCK
    B = ATTN_BLOCK
    view = lambda t: t.reshape(L, d * ATTN_WIDTH)

    def body(q_ref, kp_ref, kc_ref, vp_ref, vc_ref, acc_ref, m_ref, l_ref):
        n = pl.program_id(1)
        qi = lax.broadcasted_iota(jnp.int32, (B, 2 * B), 0)
        kj = lax.broadcasted_iota(jnp.int32, (B, 2 * B), 1)
        steps = qi + B - kj
        valid = (steps >= 0) & (steps <= B) & ((kj >= B) | (n > 0))
        stepsf = steps.astype(F32)
        lo = lax.broadcasted_iota(jnp.int32, (B, 128), 1) < 64
        for G in range(4):
            sl = slice(G * 128, (G + 1) * 128)
            qg = q_ref[:, sl]
            kg = jnp.concatenate([kp_ref[:, sl], kc_ref[:, sl]], axis=0)
            vg = jnp.concatenate([vp_ref[:, sl], vc_ref[:, sl]], axis=0)
            res = []
            for half in (0, 1):
                msk = lo if half == 0 else jnp.logical_not(lo)
                qm = jnp.where(msk, qg, jnp.zeros_like(qg))
                s = _nt(qm, kg)
                s = jnp.where(valid, s - (_slope(2 * G + half) * d) * stepsf, NEG_BIG)
                m = jnp.max(s, axis=-1, keepdims=True)
                p = jnp.exp(s - m)
                l = jnp.sum(p, axis=-1, keepdims=True)
                a = _nn(p.astype(BF16), vg)
                res.append((a, m, l))
            (a0, m0, l0), (a1, m1, l1) = res
            acc_ref[:, sl] = jnp.where(lo, a0, a1)
            m_ref[:, sl] = jnp.where(lo, m0, m1)
            l_ref[:, sl] = jnp.where(lo, l0, l1)

    cur = pl.BlockSpec((B, ATTN_WIDTH), lambda r, n: (n, r))
    prev = pl.BlockSpec((B, ATTN_WIDTH), lambda r, n: (jnp.maximum(n - 1, 0), r))
    outs = pl.pallas_call(
        body, name=name, grid=(d, nb),
        in_specs=[cur, prev, cur, prev, cur],
        out_specs=[cur, cur, cur],
        out_shape=[jax.ShapeDtypeStruct((L, d * ATTN_WIDTH), F32)] * 3,
        compiler_params=_params(2),
    )(view(q), view(k), view(k), view(v), view(v))
    return [o.reshape(S, ATTN_WIDTH) for o in outs]


def _attn_merge(parts):
    S = parts[0][0].shape[0]
    tm = 512

    def body(a1, m1, l1, a2, m2, l2, a3, m3, l3, o_ref, lse_ref):
        ms = [m1[...], m2[...], m3[...]]
        mx = jnp.maximum(jnp.maximum(ms[0], ms[1]), ms[2])
        es = [jnp.exp(m - mx) for m in ms]
        den = es[0] * l1[...] + es[1] * l2[...] + es[2] * l3[...]
        num = es[0] * a1[...] + es[1] * a2[...] + es[2] * a3[...]
        o_ref[...] = num / den
        lse_ref[...] = mx + jnp.log(den)

    tok = pl.BlockSpec((tm, ATTN_WIDTH), lambda i: (i, 0))
    flat = [t for p in parts for t in p]
    return pl.pallas_call(
        body, name="attn_merge", grid=(S // tm,), in_specs=[tok] * 9, out_specs=[tok, tok],
        out_shape=[jax.ShapeDtypeStruct((S, ATTN_WIDTH), F32)] * 2, compiler_params=_params(1),
    )(*flat)


def _attn_delta(do, o):
    S = o.shape[0]
    tm = 512

    def body(do_ref, o_ref, dd_ref):
        lo = lax.broadcasted_iota(jnp.int32, (tm, 128), 1) < 64
        for G in range(4):
            sl = slice(G * 128, (G + 1) * 128)
            t = do_ref[:, sl] * o_ref[:, sl]
            d0 = jnp.sum(jnp.where(lo, t, 0.0), axis=-1, keepdims=True)
            d1 = jnp.sum(jnp.where(lo, 0.0, t), axis=-1, keepdims=True)
            dd_ref[:, sl] = jnp.where(lo, d0, d1)

    tok = pl.BlockSpec((tm, ATTN_WIDTH), lambda i: (i, 0))
    return pl.pallas_call(
        body, name="attn_delta", grid=(S // tm,), in_specs=[tok, tok], out_specs=tok,
        out_shape=jax.ShapeDtypeStruct((S, ATTN_WIDTH), F32), compiler_params=_params(1),
    )(do, o)


def _head_col(t, msk, big):
    if big:
        return jnp.max(jnp.where(msk, t, NEG_BIG), axis=-1, keepdims=True)
    return jnp.sum(jnp.where(msk, t, 0.0), axis=-1, keepdims=True) * (1.0 / 64.0)


def _attn_bwd_q(q, k, v, do, lse, dd, dq_run, d, name):
    S = q.shape[0]
    L = S // d
    nb = L // ATTN_BLOCK
    B = ATTN_BLOCK
    view = lambda t: t.reshape(L, d * ATTN_WIDTH)
    has_run = dq_run is not None

    def body(q_ref, kp_ref, kc_ref, vp_ref, vc_ref, do_ref, lse_ref, dd_ref, *rest):
        dq_ref = rest[-1]
        n = pl.program_id(1)
        qi = lax.broadcasted_iota(jnp.int32, (B, 2 * B), 0)
        kj = lax.broadcasted_iota(jnp.int32, (B, 2 * B), 1)
        steps = qi + B - kj
        valid = (steps >= 0) & (steps <= B) & ((kj >= B) | (n > 0))
        stepsf = steps.astype(F32)
        lo = lax.broadcasted_iota(jnp.int32, (B, 128), 1) < 64
        for G in range(4):
            sl = slice(G * 128, (G + 1) * 128)
            qg = q_ref[:, sl]
            kg = jnp.concatenate([kp_ref[:, sl], kc_ref[:, sl]], axis=0)
            vg = jnp.concatenate([vp_ref[:, sl], vc_ref[:, sl]], axis=0)
            dog = do_ref[:, sl]
            res = []
            for half in (0, 1):
                msk = lo if half == 0 else jnp.logical_not(lo)
                qm = jnp.where(msk, qg, jnp.zeros_like(qg))
                s = _nt(qm, kg) - (_slope(2 * G + half) * d) * stepsf
                lse_c = _head_col(lse_ref[:, sl], msk, True)
                p = jnp.where(valid, jnp.exp(jnp.where(valid, s, NEG_BIG) - lse_c), 0.0)
                dom = jnp.where(msk, dog, 0.0).astype(BF16)
                dp = _nt(dom, vg)
                dcol = _head_col(dd_ref[:, sl], msk, False)
                ds = p * (dp - dcol)
                res.append(_nn(ds.astype(BF16), kg) * 0.125)
            dq = jnp.where(lo, res[0], res[1])
            if has_run:
                dq = dq + rest[0][:, sl]
            dq_ref[:, sl] = dq

    cur = pl.BlockSpec((B, ATTN_WIDTH), lambda r, n: (n, r))
    prev = pl.BlockSpec((B, ATTN_WIDTH), lambda r, n: (jnp.maximum(n - 1, 0), r))
    args = [view(q), view(k), view(k), view(v), view(v), view(do), view(lse), view(dd)]
    specs = [cur, prev, cur, prev, cur, cur, cur, cur]
    if has_run:
        args.append(view(dq_run))
        specs.append(cur)
    out = pl.pallas_call(
        body, name=name, grid=(d, nb), in_specs=specs, out_specs=cur,
        out_shape=jax.ShapeDtypeStruct((L, d * ATTN_WIDTH), F32), compiler_params=_params(2),
    )(*args)
    return out.reshape(S, ATTN_WIDTH)


def _attn_bwd_kv(q, k, v, do, lse, dd, dk_run, dv_run, d, name):
    S = q.shape[0]
    L = S // d
    nb = L // ATTN_BLOCK
    B = ATTN_BLOCK
    view = lambda t: t.reshape(L, d * ATTN_WIDTH)
    has_run = dk_run is not None

    def body(k_ref, v_ref, qc_ref, qn_ref, doc_ref, don_ref, lsec_ref, lsen_ref, ddc_ref, ddn_ref, *rest):
        dk_ref, dv_ref = rest[-2], rest[-1]
        j = pl.program_id(1)
        qrow = lax.broadcasted_iota(jnp.int32, (2 * B, B), 0)
        kk = lax.broadcasted_iota(jnp.int32, (2 * B, B), 1)
        steps = qrow - kk
        valid = (steps >= 0) & (steps <= B) & ((qrow < B) | (j < nb - 1))
        stepsf = steps.astype(F32)
        lo2 = lax.broadcasted_iota(jnp.int32, (2 * B, 128), 1) < 64
        lo = lax.broadcasted_iota(jnp.int32, (B, 128), 1) < 64
        for G in range(4):
            sl = slice(G * 128, (G + 1) * 128)
            kg = k_ref[:, sl]
            vg = v_ref[:, sl]
            qq = jnp.concatenate([qc_ref[:, sl], qn_ref[:, sl]], axis=0)
            doo = jnp.concatenate([doc_ref[:, sl], don_ref[:, sl]], axis=0)
            lse2 = jnp.concatenate([lsec_ref[:, sl], lsen_ref[:, sl]], axis=0)
            dd2 = jnp.concatenate([ddc_ref[:, sl], ddn_ref[:, sl]], axis=0)
            doo_b = doo.astype(BF16)
            dks, dvs = [], []
            for half in (0, 1):
                msk = lo2 if half == 0 else jnp.logical_not(lo2)
                qm = jnp.where(msk, qq, jnp.zeros_like(qq))
                s = _nt(qm, kg) - (_slope(2 * G + half) * d) * stepsf
                lse_c = _head_col(lse2, msk, True)
                p = jnp.where(valid, jnp.exp(jnp.where(valid, s, NEG_BIG) - lse_c), 0.0)
                dvs.append(_tn(p.astype(BF16), doo_b))
                dom = jnp.where(msk, doo, 0.0).astype(BF16)
                dp = _nt(dom, vg)
                dcol = _head_col(dd2, msk, False)
                ds = p * (dp - dcol)
                dks.append(_tn(ds.astype(BF16), qq))
            dk = jnp.where(lo, dks[0], dks[1])
            dv = jnp.where(lo, dvs[0], dvs[1])
            if has_run:
                dk = dk + rest[0][:, sl]
                dv = dv + rest[1][:, sl]
            dk_ref[:, sl] = dk
            dv_ref[:, sl] = dv

    cur = pl.BlockSpec((B, ATTN_WIDTH), lambda r, j: (j, r))
    nxt = pl.BlockSpec((B, ATTN_WIDTH), lambda r, j: (jnp.minimum(j + 1, nb - 1), r))
    args = [view(k), view(v), view(q), view(q), view(do), view(do), view(lse), view(lse), view(dd), view(dd)]
    specs = [cur, cur, cur, nxt, cur, nxt, cur, nxt, cur, nxt]
    if has_run:
        args += [view(dk_run), view(dv_run)]
        specs += [cur, cur]
    outs = pl.pallas_call(
        body, name=name, grid=(d, nb), in_specs=specs, out_specs=[cur, cur],
        out_shape=[jax.ShapeDtypeStruct((L, d * ATTN_WIDTH), F32)] * 2, compiler_params=_params(2),
    )(*args)
    return [o.reshape(S, ATTN_WIDTH) for o in outs]


CONV_T = 512
HALO = 8


def _conv_taps(pad_ref, w, T):
    acc = pad_ref[pl.ds(HALO - 3, T), :] * w[0:1, :]
    for j in range(1, CONV_WIDTH):
        acc = acc + pad_ref[pl.ds(HALO - 3 + j, T), :] * w[j:j + 1, :]
    return acc


def _conv_fwd(xq, xk, xv, conv_w):
    S = xq.shape[0]
    T = CONV_T

    def body(xq_ref, xqh_ref, xk_ref, xkh_ref, xv_ref, xvh_ref, wq_ref, wk_ref, wv_ref,
             qn_ref, kn_ref, v_ref, pad_ref):
        i = pl.program_id(0)

        def act(x_ref, xh_ref, w_ref):
            pad_ref[pl.ds(0, HALO), :] = jnp.where(i > 0, xh_ref[...], 0.0)
            pad_ref[pl.ds(HALO, T), :] = x_ref[...]
            c = _conv_taps(pad_ref, w_ref[...], T)
            return c * _sigmoid(c)

        def l2n(t):
            return t * lax.rsqrt(jnp.sum(t * t, axis=-1, keepdims=True) + L2_EPS)

        qn_ref[...] = l2n(act(xq_ref, xqh_ref, wq_ref))
        kn_ref[...] = l2n(act(xk_ref, xkh_ref, wk_ref))
        v_ref[...] = act(xv_ref, xvh_ref, wv_ref)

    tile = pl.BlockSpec((T, 128), lambda i, h: (i, h))
    halo = pl.BlockSpec((HALO, 128), lambda i, h: (jnp.maximum(i * (T // HALO) - 1, 0), h))
    wspec = lambda sec: pl.BlockSpec((CONV_WIDTH, 128), lambda i, h, sec=sec: (0, 4 * sec + h))
    return pl.pallas_call(
        body, name="dn_conv_fwd", grid=(S // T, DN_HEADS),
        in_specs=[tile, halo, tile, halo, tile, halo, wspec(0), wspec(1), wspec(2)],
        out_specs=[tile, tile, tile],
        out_shape=[jax.ShapeDtypeStruct((S, DN_WIDTH), F32)] * 3,
        scratch_shapes=[pltpu.VMEM((T + HALO, 128), F32)],
        compiler_params=_params(2),
    )(xq, xq, xk, xk, xv, xv, conv_w, conv_w, conv_w)


def _conv_bwd_pre(xq, xk, xv, conv_w, dqn, dkn, dv):
    S = xq.shape[0]
    T = CONV_T

    def body(xq_ref, xqh_ref, xk_ref, xkh_ref, xv_ref, xvh_ref, wq_ref, wk_ref, wv_ref,
             dqn_ref, dkn_ref, dv_ref, dcq_ref, dck_ref, dcv_ref, dwq_ref, dwk_ref, dwv_ref, pad_ref):
        i = pl.program_id(1)

        def one(x_ref, xh_ref, w_ref, dy_ref, dc_ref, dw_ref, normed):
            pad_ref[pl.ds(0, HALO), :] = jnp.where(i > 0, xh_ref[...], 0.0)
            pad_ref[pl.ds(HALO, T), :] = x_ref[...]
            c = _conv_taps(pad_ref, w_ref[...], T)
            sg = _sigmoid(c)
            a = c * sg
            dy = dy_ref[...]
            if normed:
                r = lax.rsqrt(jnp.sum(a * a, axis=-1, keepdims=True) + L2_EPS)
                y = a * r
                da = r * (dy - y * jnp.sum(dy * y, axis=-1, keepdims=True))
            else:
                da = dy
            dc = da * (sg * (1.0 + c * (1.0 - sg)))
            dc_ref[...] = dc

            @pl.when(i == 0)
            def _():
                dw_ref[...] = jnp.zeros_like(dw_ref)

            rows = [jnp.sum(dc * pad_ref[pl.ds(HALO - 3 + j, T), :], axis=0, keepdims=True) for j in range(CONV_WIDTH)]
            dw_ref[...] += jnp.concatenate(rows + [jnp.zeros((8 - CONV_WIDTH, 128), F32)], axis=0)

        one(xq_ref, xqh_ref, wq_ref, dqn_ref, dcq_ref, dwq_ref, True)
        one(xk_ref, xkh_ref, wk_ref, dkn_ref, dck_ref, dwk_ref, True)
        one(xv_ref, xvh_ref, wv_ref, dv_ref, dcv_ref, dwv_ref, False)

    tile = pl.BlockSpec((T, 128), lambda h, i: (i, h))
    halo = pl.BlockSpec((HALO, 128), lambda h, i: (jnp.maximum(i * (T // HALO) - 1, 0), h))
    wspec = lambda sec: pl.BlockSpec((CONV_WIDTH, 128), lambda h, i, sec=sec: (0, 4 * sec + h))
    dwspec = pl.BlockSpec((8, 128), lambda h, i: (0, h))
    return pl.pallas_call(
        body, name="dn_conv_bwd_pre", grid=(DN_HEADS, S // T),
        in_specs=[tile, halo, tile, halo, tile, halo, wspec(0), wspec(1), wspec(2), tile, tile, tile],
        out_specs=[tile, tile, tile, dwspec, dwspec, dwspec],
        out_shape=[jax.ShapeDtypeStruct((S, DN_WIDTH), F32)] * 3 + [jax.ShapeDtypeStruct((8, DN_WIDTH), F32)] * 3,
        scratch_shapes=[pltpu.VMEM((T + HALO, 128), F32)],
        compiler_params=_params(2),
    )(xq, xq, xk, xk, xv, xv, conv_w, conv_w, conv_w, dqn, dkn, dv)


def _conv_bwd_x(dcq, dck, dcv, conv_w):
    S = dcq.shape[0]
    T = CONV_T
    nt = S // T

    def body(dq_ref, dqh_ref, dk_ref, dkh_ref, dv_ref, dvh_ref, wq_ref, wk_ref, wv_ref,
             oq_ref, ok_ref, ov_ref, pad_ref):
        i = pl.program_id(0)

        def one(d_ref, dh_ref, w_ref, o_ref):
            pad_ref[pl.ds(0, T), :] = d_ref[...]
            pad_ref[pl.ds(T, HALO), :] = jnp.where(i < nt - 1, dh_ref[...], 0.0)
            w = w_ref[...]
            acc = pad_ref[pl.ds(3, T), :] * w[0:1, :]
            for j in range(1, CONV_WIDTH):
                acc = acc + pad_ref[pl.ds(3 - j, T), :] * w[j:j + 1, :]
            o_ref[...] = acc

        one(dq_ref, dqh_ref, wq_ref, oq_ref)
        one(dk_ref, dkh_ref, wk_ref, ok_ref)
        one(dv_ref, dvh_ref, wv_ref, ov_ref)

    tile = pl.BlockSpec((T, 128), lambda i, h: (i, h))
    halo = pl.BlockSpec((HALO, 128), lambda i, h: (jnp.minimum((i + 1) * (T // HALO), S // HALO - 1), h))
    wspec = lambda sec: pl.BlockSpec((CONV_WIDTH, 128), lambda i, h, sec=sec: (0, 4 * sec + h))
    return pl.pallas_call(
        body, name="dn_conv_bwd_x", grid=(nt, DN_HEADS),
        in_specs=[tile, halo, tile, halo, tile, halo, wspec(0), wspec(1), wspec(2)],
        out_specs=[tile, tile, tile],
        out_shape=[jax.ShapeDtypeStruct((S, DN_WIDTH), F32)] * 3,
        scratch_shapes=[pltpu.VMEM((T + HALO, 128), F32)],
        compiler_params=_params(2),
    )(dcq, dcq, dck, dck, dcv, dcv, conv_w, conv_w, conv_w)


def _tri_inverse(a, blk, eye):
    mm = functools.partial(_nn, precision=HI)
    dg = jnp.where(blk, a, 0.0)
    lo = a - dg
    d2 = mm(dg, dg)
    d4 = mm(d2, d2)
    d8 = mm(d4, d4)
    td = mm(mm(mm(eye - dg, eye + d2), eye + d4), eye + d8)
    b = mm(td, lo)
    b2 = mm(b, b)
    tb = mm(eye - b, eye + b2)
    return mm(tb, td)


def _dn_chunk_common(bd, avec, dvec, h, q_raw, k, v, t=None):
    C = DN_CHUNK
    lane = lax.broadcasted_iota(jnp.int32, (C, 128), 1)
    row = lax.broadcasted_iota(jnp.int32, (C, C), 0)
    col = lax.broadcasted_iota(jnp.int32, (C, C), 1)
    incl = row >= col
    strict = row > col
    eye = (row == col).astype(F32)
    blk = (row // 16) == (col // 16)
    beta_all = _sigmoid(bd)
    z = bd + dvec
    sp = jnp.maximum(z, 0.0) + jnp.log(1.0 + jnp.exp(-jnp.abs(z)))
    g_all = -jnp.exp(avec) * sp
    pick = lambda t, ln: jnp.sum(jnp.where(lane == ln, t, 0.0), axis=-1, keepdims=True)
    beta = pick(beta_all, h)
    graw = pick(g_all, DN_HEADS + h)
    zc = pick(z, DN_HEADS + h)
    to_row = lambda c: jnp.sum(eye * c, axis=0, keepdims=True)
    gc = jnp.sum(jnp.where(incl, to_row(graw), 0.0), axis=-1, keepdims=True)
    gc_row = to_row(gc)
    decay = jnp.exp(jnp.where(incl, gc - gc_row, NEG_BIG))
    q = q_raw * (DN_HEAD_DIM ** -0.5)
    kb = k * beta
    kk = _nt(kb, k, HI)
    if t is None:
        t = _tri_inverse(jnp.where(strict, kk * decay, 0.0), blk, eye)
    eg = jnp.exp(gc)
    rhs_u = v * beta
    rhs_w = kb * eg
    u = _nn(t, rhs_u, HI)
    w = _nn(t, rhs_w, HI)
    qk = _nt(q, k, HI)
    aq = jnp.where(incl, qk * decay, 0.0)
    g_last = jnp.sum(jnp.where(lax.broadcasted_iota(jnp.int32, (C, 1), 0) == C - 1, gc, 0.0), axis=0, keepdims=True)
    ekd = jnp.exp(g_last - gc)
    return dict(beta=beta, graw=graw, zc=zc, gc=gc, decay=decay, q=q, kb=kb, kk=kk, t=t, eg=eg, rhs_w=rhs_w,
                u=u, w=w, qk=qk, aq=aq, g_last=g_last, ekd=ekd, kd=k * ekd, qg=q * eg,
                incl=incl, strict=strict, eye=eye, lane=lane, row=row, col=col)


PREP_CHUNKS = 2
SCAN_CHUNKS = 8


def _dn_prep(qn, kn, v, bd, avec, dvec):
    S = qn.shape[0]
    C = DN_CHUNK
    N = S // C
    HD = DN_HEAD_DIM
    nc = PREP_CHUNKS

    def body(q_ref, k_ref, v_ref, bd_ref, a_ref, d_ref, u_ref, w_ref, qg_ref, kd_ref, aq_ref, t_ref, egl_ref):
        for ci in range(nc):
            rows = slice(ci * C, (ci + 1) * C)
            bd = bd_ref[rows, :]
            egl = []
            for h in range(DN_HEADS):
                sl = slice(h * HD, (h + 1) * HD)
                c = _dn_chunk_common(bd, a_ref[...], d_ref[...], h, q_ref[rows, sl], k_ref[rows, sl], v_ref[rows, sl])
                u_ref[rows, sl] = c["u"]
                w_ref[rows, sl] = c["w"]
                qg_ref[rows, sl] = c["qg"]
                kd_ref[rows, sl] = c["kd"]
                aq_ref[h, rows, :] = c["aq"]
                t_ref[h, rows, :] = c["t"]
                egl.append(jnp.broadcast_to(jnp.exp(c["g_last"]), (1, 128)))
            egl_ref[ci * 8:(ci + 1) * 8, :] = jnp.concatenate(egl + [jnp.zeros((8 - DN_HEADS, 128), F32)], axis=0)

    tok = lambda w: pl.BlockSpec((nc * C, w), lambda n: (n, 0))
    sq = pl.BlockSpec((DN_HEADS, nc * C, C), lambda n: (0, n, 0))
    vec = pl.BlockSpec((1, 128), lambda n: (0, 0))
    return pl.pallas_call(
        body, name="dn_prep", grid=(N // nc,),
        in_specs=[tok(DN_WIDTH)] * 3 + [tok(128), vec, vec],
        out_specs=[tok(DN_WIDTH)] * 4 + [sq, sq, pl.BlockSpec((nc * 8, 128), lambda n: (n, 0))],
        out_shape=[jax.ShapeDtypeStruct((S, DN_WIDTH), F32)] * 4 + [jax.ShapeDtypeStruct((DN_HEADS, S, C), F32)] * 2
                  + [jax.ShapeDtypeStruct((N * 8, 128), F32)],
        compiler_params=_params(1),
    )(qn, kn, v, bd, avec, dvec)


def _dn_scan_fwd(u, w, qg, kd, aq, egl, gate, dn_gain):
    S = u.shape[0]
    C = DN_CHUNK
    N = S // C
    HD = DN_HEAD_DIM
    nc = SCAN_CHUNKS

    def body(u_ref, w_ref, qg_ref, kd_ref, aq_ref, egl_ref, gate_ref, gain_ref, dn_ref, o_ref, vn_ref, st_ref, state_ref):
        @pl.when(pl.program_id(0) == 0)
        def _():
            state_ref[...] = jnp.zeros_like(state_ref)

        for ci in range(nc):
            rows = slice(ci * C, (ci + 1) * C)
            st_ref[ci * DN_WIDTH:(ci + 1) * DN_WIDTH, :] = state_ref[...]
            for h in range(DN_HEADS):
                sl = slice(h * HD, (h + 1) * HD)
                st = state_ref[sl, :]
                v_new = u_ref[rows, sl] - _nn(w_ref[rows, sl], st, HI)
                o = _nn(qg_ref[rows, sl], st, HI) + _nn(aq_ref[h, rows, :], v_new, HI)
                state_ref[sl, :] = st * egl_ref[ci * 8 + h:ci * 8 + h + 1, :] + _tn(kd_ref[rows, sl], v_new, HI)
                vn_ref[rows, sl] = v_new
                o_ref[rows, sl] = o
                r = lax.rsqrt(jnp.mean(o * o, axis=-1, keepdims=True) + NORM_EPS)
                gt = gate_ref[rows, sl]
                dn_ref[rows, sl] = o * r * gain_ref[...] * (gt * _sigmoid(gt))

    tok = lambda wd: pl.BlockSpec((nc * C, wd), lambda n: (n, 0))
    sq = pl.BlockSpec((DN_HEADS, nc * C, C), lambda n: (0, n, 0))
    vec = pl.BlockSpec((1, 128), lambda n: (0, 0))
    return pl.pallas_call(
        body, name="dn_scan_fwd", grid=(N // nc,),
        in_specs=[tok(DN_WIDTH)] * 4 + [sq, pl.BlockSpec((nc * 8, 128), lambda n: (n, 0)), tok(DN_WIDTH), vec],
        out_specs=[tok(DN_WIDTH)] * 3 + [pl.BlockSpec((nc * DN_WIDTH, HD), lambda n: (n, 0))],
        out_shape=[jax.ShapeDtypeStruct((S, DN_WIDTH), F32)] * 3 + [jax.ShapeDtypeStruct((N * DN_WIDTH, HD), F32)],
        scratch_shapes=[pltpu.VMEM((DN_WIDTH, HD), F32)],
        compiler_params=_params(1),
    )(u, w, qg, kd, aq, egl, gate, dn_gain)


def _dn_scan_bwd(w, qg, kd, aq, egl, gate, dn_gain, o, ddn):
    S = w.shape[0]
    C = DN_CHUNK
    N = S // C
    HD = DN_HEAD_DIM
    nc = SCAN_CHUNKS

    def body(w_ref, qg_ref, kd_ref, aq_ref, egl_ref, gate_ref, gain_ref, o_ref, ddn_ref,
             do_ref, dvn_ref, dgate_ref, dst_ref, small_ref, dstate_ref):
        @pl.when(pl.program_id(0) == 0)
        def _():
            dstate_ref[...] = jnp.zeros_like(dstate_ref)
            small_ref[...] = jnp.zeros_like(small_ref)

        gain = gain_ref[...]
        d_gain = jnp.zeros((1, 128), F32)
        for ci in reversed(range(nc)):
            rows = slice(ci * C, (ci + 1) * C)
            dst_ref[ci * DN_WIDTH:(ci + 1) * DN_WIDTH, :] = dstate_ref[...]
            for h in range(DN_HEADS):
                sl = slice(h * HD, (h + 1) * HD)
                ov = o_ref[rows, sl]
                r = lax.rsqrt(jnp.mean(ov * ov, axis=-1, keepdims=True) + NORM_EPS)
                on = ov * r
                gt = gate_ref[rows, sl]
                sgt = _sigmoid(gt)
                silu_g = gt * sgt
                dy = ddn_ref[rows, sl]
                d_gain = d_gain + jnp.sum(dy * on * silu_g, axis=0, keepdims=True)
                dgate_ref[rows, sl] = dy * on * gain * (sgt * (1.0 + gt * (1.0 - sgt)))
                don = dy * gain * silu_g
                do = r * (don - on * jnp.mean(don * on, axis=-1, keepdims=True))
                do_ref[rows, sl] = do
                dsn = dstate_ref[sl, :]
                d_vnew = _tn(aq_ref[h, rows, :], do, HI) + _nn(kd_ref[rows, sl], dsn, HI)
                dvn_ref[rows, sl] = d_vnew
                dstate_ref[sl, :] = (_tn(qg_ref[rows, sl], do, HI) + dsn * egl_ref[ci * 8 + h:ci * 8 + h + 1, :]
                                     - _tn(w_ref[rows, sl], d_vnew, HI))
        small_ref[...] += jnp.concatenate([d_gain, jnp.zeros((7, 128), F32)], axis=0)

    nb = N // nc
    tok = lambda wd: pl.BlockSpec((nc * C, wd), lambda i: (nb - 1 - i, 0))
    sq = pl.BlockSpec((DN_HEADS, nc * C, C), lambda i: (0, nb - 1 - i, 0))
    vec = pl.BlockSpec((1, 128), lambda i: (0, 0))
    return pl.pallas_call(
        body, name="dn_scan_bwd", grid=(nb,),
        in_specs=[tok(DN_WIDTH)] * 3 + [sq, pl.BlockSpec((nc * 8, 128), lambda i: (nb - 1 - i, 0)), tok(DN_WIDTH), vec,
                                       tok(DN_WIDTH), tok(DN_WIDTH)],
        out_specs=[tok(DN_WIDTH)] * 3 + [pl.BlockSpec((nc * DN_WIDTH, HD), lambda i: (nb - 1 - i, 0)),
                                        pl.BlockSpec((8, 128), lambda i: (0, 0))],
        out_shape=[jax.ShapeDtypeStruct((S, DN_WIDTH), F32)] * 3 + [jax.ShapeDtypeStruct((N * DN_WIDTH, HD), F32),
                                                                  jax.ShapeDtypeStruct((8, 128), F32)],
        scratch_shapes=[pltpu.VMEM((DN_WIDTH, HD), F32)],
        compiler_params=_params(1),
    )(w, qg, kd, aq, egl, gate, dn_gain, o, ddn)


def _dn_post(qn, kn, v, bd, avec, dvec, t_inv, v_new_all, states, dstates, do_all, dvn_all):
    S = qn.shape[0]
    C = DN_CHUNK
    N = S // C
    HD = DN_HEAD_DIM
    nc = PREP_CHUNKS

    def body(q_ref, k_ref, v_ref, bd_ref, a_ref, d_ref, t_ref, vn_ref, st_ref, dst_ref, do_ref, dvn_ref,
             dq_ref, dk_ref, dv_ref, dbd_ref, small_ref):
        @pl.when(pl.program_id(0) == 0)
        def _():
            small_ref[...] = jnp.zeros_like(small_ref)

        for ci in range(nc):
            tok = lambda r: r.at[pl.ds(ci * C, C)]
            big = lambda r: r.at[pl.ds(ci * DN_WIDTH, DN_WIDTH)]
            chunk(tok(q_ref), tok(k_ref), tok(v_ref), tok(bd_ref), a_ref, d_ref, t_ref.at[:, pl.ds(ci * C, C)], tok(vn_ref),
                  big(st_ref), big(dst_ref), tok(do_ref), tok(dvn_ref), tok(dq_ref), tok(dk_ref), tok(dv_ref), tok(dbd_ref),
                  small_ref)

    def chunk(q_ref, k_ref, v_ref, bd_ref, a_ref, d_ref, t_ref, vn_ref, st_ref, dstate_ref, do_ref, dvn_ref,
              dq_ref, dk_ref, dv_ref, dbd_ref, small_ref):
        bd = bd_ref[...]
        avec = a_ref[...]
        dbd = jnp.zeros((C, 128), F32)
        d_alog = jnp.zeros((1, 128), F32)
        d_dt = jnp.zeros((1, 128), F32)
        for h in range(DN_HEADS):
            sl = slice(h * HD, (h + 1) * HD)
            k = k_ref[:, sl]
            vv = v_ref[:, sl]
            c = _dn_chunk_common(bd, avec, d_ref[...], h, q_ref[:, sl], k, vv, t=t_ref[h])
            q, kb, t, eg, u, w = c["q"], c["kb"], c["t"], c["eg"], c["u"], c["w"]
            beta, decay, incl, strict, eye = c["beta"], c["decay"], c["incl"], c["strict"], c["eye"]
            lane = c["lane"]
            st = st_ref[sl, :]
            dsn = dstate_ref[sl, :]
            v_new = vn_ref[:, sl]
            do = do_ref[:, sl]
            d_vnew = dvn_ref[:, sl]
            egl = jnp.exp(c["g_last"])
            daq = jnp.where(incl, _nt(do, v_new, HI), 0.0)
            d_qg = _nt(do, st, HI)
            d_kd = _nt(v_new, dsn, HI)
            d_glast = jnp.sum(jnp.sum(dsn * st, axis=-1, keepdims=True), axis=0, keepdims=True) * egl
            d_w = -_nt(d_vnew, st, HI)
            d_ru = _tn(t, d_vnew, HI)
            d_rw = _tn(t, d_w, HI)
            da = -jnp.where(strict, _nt(d_ru, u, HI) + _nt(d_rw, w, HI), 0.0)
            dv_ref[:, sl] = d_ru * beta
            dbeta = jnp.sum(d_ru * vv, axis=-1, keepdims=True)
            dkb = d_rw * eg
            dgc = jnp.sum(d_rw * c["rhs_w"], axis=-1, keepdims=True)
            dkk = da * decay
            ddecay = da * c["kk"]
            dkb = dkb + _nn(dkk, k, HI)
            dk = _tn(dkk, kb, HI)
            dqk = daq * decay
            ddecay = ddecay + daq * c["qk"]
            dq = _nn(dqk, k, HI)
            dk = dk + _tn(dqk, q, HI)
            m = ddecay * decay
            col_sum = jnp.sum(m, axis=0, keepdims=True)
            dgc = dgc + jnp.sum(m, axis=-1, keepdims=True) - jnp.sum(eye * col_sum, axis=-1, keepdims=True)
            dq = dq + d_qg * eg
            dgc = dgc + jnp.sum(d_qg * c["qg"], axis=-1, keepdims=True)
            dk = dk + d_kd * c["ekd"]
            tk = jnp.sum(d_kd * c["kd"], axis=-1, keepdims=True)
            dgc = dgc - tk
            d_glast = d_glast + jnp.sum(tk, axis=0, keepdims=True)
            dk = dk + dkb * beta
            dbeta = dbeta + jnp.sum(dkb * k, axis=-1, keepdims=True)
            dgc = dgc + jnp.where(lax.broadcasted_iota(jnp.int32, (C, 1), 0) == C - 1, d_glast, 0.0)
            dgc_row = jnp.sum(eye * dgc, axis=0, keepdims=True)
            dgraw = jnp.sum(jnp.where(c["col"] >= c["row"], dgc_row, 0.0), axis=-1, keepdims=True)
            dq_ref[:, sl] = dq * (HD ** -0.5)
            dk_ref[:, sl] = dk
            dbraw = dbeta * beta * (1.0 - beta)
            dz = dgraw * (-jnp.exp(avec)) * _sigmoid(c["zc"])
            dbd = dbd + jnp.where(lane == h, dbraw, 0.0) + jnp.where(lane == DN_HEADS + h, dz, 0.0)
            lane1 = lax.broadcasted_iota(jnp.int32, (1, 128), 1)
            d_alog = d_alog + jnp.where(lane1 == DN_HEADS + h, jnp.sum(dgraw * c["graw"], axis=0, keepdims=True), 0.0)
            d_dt = d_dt + jnp.where(lane1 == DN_HEADS + h, jnp.sum(dz, axis=0, keepdims=True), 0.0)
        dbd_ref[...] = dbd
        small_ref[...] += jnp.concatenate([d_alog, d_dt, jnp.zeros((6, 128), F32)], axis=0)

    tok = lambda wd: pl.BlockSpec((nc * C, wd), lambda n: (n, 0))
    big = pl.BlockSpec((nc * DN_WIDTH, HD), lambda n: (n, 0))
    sq = pl.BlockSpec((DN_HEADS, nc * C, C), lambda n: (0, n, 0))
    vec = pl.BlockSpec((1, 128), lambda n: (0, 0))
    return pl.pallas_call(
        body, name="dn_post", grid=(N // nc,),
        in_specs=[tok(DN_WIDTH)] * 3 + [tok(128), vec, vec, sq, tok(DN_WIDTH), big, big, tok(DN_WIDTH), tok(DN_WIDTH)],
        out_specs=[tok(DN_WIDTH)] * 3 + [tok(128), pl.BlockSpec((8, 128), lambda n: (0, 0))],
        out_shape=[jax.ShapeDtypeStruct((S, DN_WIDTH), F32)] * 3 + [jax.ShapeDtypeStruct((S, 128), F32),
                                                                  jax.ShapeDtypeStruct((8, 128), F32)],
        compiler_params=_params(1),
    )(qn, kn, v, bd, avec, dvec, t_inv, v_new_all, states, dstates, do_all, dvn_all)


def _bnn(a, b):
    return lax.dot_general(a, b, (((2,), (1,)), ((0,), (0,))), preferred_element_type=F32, precision=HI)


def _bnt(a, b):
    return lax.dot_general(a, b, (((2,), (2,)), ((0,), (0,))), preferred_element_type=F32, precision=HI)


def _btn(a, b):
    return lax.dot_general(a, b, (((1,), (1,)), ((0,), (0,))), preferred_element_type=F32, precision=HI)


def _tri_inverse_b(a, blk, eye):
    dg = jnp.where(blk, a, 0.0)
    lo = a - dg
    d2 = _bnn(dg, dg)
    d4 = _bnn(d2, d2)
    d8 = _bnn(d4, d4)
    td = _bnn(_bnn(_bnn(eye - dg, eye + d2), eye + d4), eye + d8)
    b = _bnn(td, lo)
    b2 = _bnn(b, b)
    return _bnn(_bnn(eye - b, eye + b2), td)


def _dn_common_b(bds, avec, dvec, q_raw, k, v, t=None):
    C = DN_CHUNK
    lane = lax.broadcasted_iota(jnp.int32, (C, 128), 1)
    row = lax.broadcasted_iota(jnp.int32, (1, C, C), 1)
    col = lax.broadcasted_iota(jnp.int32, (1, C, C), 2)
    incl = row >= col
    strict = row > col
    eye = (row == col).astype(F32)
    blk = (row // 16) == (col // 16)
    pick = lambda tile, ln: jnp.sum(jnp.where(lane == ln, tile, 0.0), axis=-1, keepdims=True)
    betas, graws, zcs = [], [], []
    for bd in bds:
        z = bd + dvec
        g_all = -jnp.exp(avec) * (jnp.maximum(z, 0.0) + jnp.log(1.0 + jnp.exp(-jnp.abs(z))))
        beta_all = _sigmoid(bd)
        for h in range(DN_HEADS):
            betas.append(pick(beta_all, h))
            graws.append(pick(g_all, DN_HEADS + h))
            zcs.append(pick(z, DN_HEADS + h))
    beta, graw, zc = jnp.stack(betas), jnp.stack(graws), jnp.stack(zcs)
    to_row = lambda c: jnp.sum(eye * c, axis=1, keepdims=True)
    gc = jnp.sum(jnp.where(incl, to_row(graw), 0.0), axis=-1, keepdims=True)
    decay = jnp.exp(jnp.where(incl, gc - to_row(gc), NEG_BIG))
    q = q_raw * (DN_HEAD_DIM ** -0.5)
    kb = k * beta
    kk = _bnt(kb, k)
    if t is None:
        t = _tri_inverse_b(jnp.where(strict, kk * decay, 0.0), blk, eye)
    eg = jnp.exp(gc)
    rhs_w = kb * eg
    u = _bnn(t, v * beta)
    w = _bnn(t, rhs_w)
    qk = _bnt(q, k)
    aq = jnp.where(incl, qk * decay, 0.0)
    last = lax.broadcasted_iota(jnp.int32, (1, C, 1), 1) == C - 1
    g_last = jnp.sum(jnp.where(last, gc, 0.0), axis=1, keepdims=True)
    ekd = jnp.exp(g_last - gc)
    return dict(beta=beta, graw=graw, zc=zc, gc=gc, decay=decay, q=q, kb=kb, kk=kk, t=t, eg=eg, rhs_w=rhs_w,
                u=u, w=w, qk=qk, aq=aq, g_last=g_last, ekd=ekd, kd=k * ekd, qg=q * eg,
                incl=incl, strict=strict, eye=eye, lane=lane, row=row, col=col, last=last)


def _stack_heads(ref, rows):
    return jnp.stack([ref[rows, h * DN_HEAD_DIM:(h + 1) * DN_HEAD_DIM] for h in range(DN_HEADS)])


def _stack_units(ref, nc):
    C = DN_CHUNK
    return jnp.concatenate([_stack_heads(ref, slice(ci * C, (ci + 1) * C)) for ci in range(nc)], axis=0)


def _store_units(ref, val, nc):
    C = DN_CHUNK
    for ci in range(nc):
        for h in range(DN_HEADS):
            ref[ci * C:(ci + 1) * C, h * DN_HEAD_DIM:(h + 1) * DN_HEAD_DIM] = val[ci * DN_HEADS + h]


def _dn_prep(qn, kn, v, bd, avec, dvec):
    S = qn.shape[0]
    C = DN_CHUNK
    N = S // C
    nc = PREP_CHUNKS

    def body(q_ref, k_ref, v_ref, bd_ref, a_ref, d_ref, u_ref, w_ref, qg_ref, kd_ref, aq_ref, t_ref, egl_ref):
        bds = [bd_ref[ci * C:(ci + 1) * C, :] for ci in range(nc)]
        c = _dn_common_b(bds, a_ref[...], d_ref[...], _stack_units(q_ref, nc), _stack_units(k_ref, nc), _stack_units(v_ref, nc))
        _store_units(u_ref, c["u"], nc)
        _store_units(w_ref, c["w"], nc)
        _store_units(qg_ref, c["qg"], nc)
        _store_units(kd_ref, c["kd"], nc)
        egl = jnp.broadcast_to(jnp.exp(c["g_last"]), (nc * DN_HEADS, 1, 128))
        for ci in range(nc):
            for h in range(DN_HEADS):
                aq_ref[h, ci * C:(ci + 1) * C, :] = c["aq"][ci * DN_HEADS + h]
                t_ref[h, ci * C:(ci + 1) * C, :] = c["t"][ci * DN_HEADS + h]
            egl_ref[ci * 8:(ci + 1) * 8, :] = jnp.concatenate(
                [egl[ci * DN_HEADS + h] for h in range(DN_HEADS)] + [jnp.zeros((8 - DN_HEADS, 128), F32)], axis=0)

    tok = lambda w: pl.BlockSpec((nc * C, w), lambda n: (n, 0))
    sq = pl.BlockSpec((DN_HEADS, nc * C, C), lambda n: (0, n, 0))
    vec = pl.BlockSpec((1, 128), lambda n: (0, 0))
    return pl.pallas_call(
        body, name="dn_prep", grid=(N // nc,),
        in_specs=[tok(DN_WIDTH)] * 3 + [tok(128), vec, vec],
        out_specs=[tok(DN_WIDTH)] * 4 + [sq, sq, pl.BlockSpec((nc * 8, 128), lambda n: (n, 0))],
        out_shape=[jax.ShapeDtypeStruct((S, DN_WIDTH), F32)] * 4 + [jax.ShapeDtypeStruct((DN_HEADS, S, C), F32)] * 2
                  + [jax.ShapeDtypeStruct((N * 8, 128), F32)],
        compiler_params=_params(1),
    )(qn, kn, v, bd, avec, dvec)


def _dn_scan_fwd(u, w, qg, kd, aq, egl, gate, dn_gain):
    S = u.shape[0]
    C = DN_CHUNK
    N = S // C
    HD = DN_HEAD_DIM
    nc = SCAN_CHUNKS

    def body(u_ref, w_ref, qg_ref, kd_ref, aq_ref, egl_ref, gate_ref, gain_ref, dn_ref, o_ref, vn_ref, st_ref, state_ref):
        @pl.when(pl.program_id(0) == 0)
        def _():
            state_ref[...] = jnp.zeros_like(state_ref)

        gain = gain_ref[...]
        for ci in range(nc):
            rows = slice(ci * C, (ci + 1) * C)
            st = state_ref[...]
            for h in range(DN_HEADS):
                st_ref[ci * DN_WIDTH + h * HD:ci * DN_WIDTH + (h + 1) * HD, :] = st[h]
            v_new = _stack_heads(u_ref, rows) - _bnn(_stack_heads(w_ref, rows), st)
            o = _bnn(_stack_heads(qg_ref, rows), st) + _bnn(aq_ref[:, rows, :], v_new)
            egl = jnp.stack([egl_ref[ci * 8 + h:ci * 8 + h + 1, :] for h in range(DN_HEADS)])
            state_ref[...] = st * egl + _btn(_stack_heads(kd_ref, rows), v_new)
            r = lax.rsqrt(jnp.mean(o * o, axis=-1, keepdims=True) + NORM_EPS)
            gt = _stack_heads(gate_ref, rows)
            dn = o * r * gain * (gt * _sigmoid(gt))
            for h in range(DN_HEADS):
                sl = slice(h * HD, (h + 1) * HD)
                vn_ref[rows, sl] = v_new[h]
                o_ref[rows, sl] = o[h]
                dn_ref[rows, sl] = dn[h]

    tok = lambda wd: pl.BlockSpec((nc * C, wd), lambda n: (n, 0))
    sq = pl.BlockSpec((DN_HEADS, nc * C, C), lambda n: (0, n, 0))
    vec = pl.BlockSpec((1, 128), lambda n: (0, 0))
    return pl.pallas_call(
        body, name="dn_scan_fwd", grid=(N // nc,),
        in_specs=[tok(DN_WIDTH)] * 4 + [sq, pl.BlockSpec((nc * 8, 128), lambda n: (n, 0)), tok(DN_WIDTH), vec],
        out_specs=[tok(DN_WIDTH)] * 3 + [pl.BlockSpec((nc * DN_WIDTH, HD), lambda n: (n, 0))],
        out_shape=[jax.ShapeDtypeStruct((S, DN_WIDTH), F32)] * 3 + [jax.ShapeDtypeStruct((N * DN_WIDTH, HD), F32)],
        scratch_shapes=[pltpu.VMEM((DN_HEADS, HD, HD), F32)],
        compiler_params=_params(1),
    )(u, w, qg, kd, aq, egl, gate, dn_gain)


def _dn_scan_bwd(w, qg, kd, aq, egl, gate, dn_gain, o, ddn):
    S = w.shape[0]
    C = DN_CHUNK
    N = S // C
    HD = DN_HEAD_DIM
    nc = SCAN_CHUNKS

    def body(w_ref, qg_ref, kd_ref, aq_ref, egl_ref, gate_ref, gain_ref, o_ref, ddn_ref,
             do_ref, dvn_ref, dgate_ref, dst_ref, small_ref, dstate_ref):
        @pl.when(pl.program_id(0) == 0)
        def _():
            dstate_ref[...] = jnp.zeros_like(dstate_ref)
            small_ref[...] = jnp.zeros_like(small_ref)

        gain = gain_ref[...]
        d_gain = jnp.zeros((1, 128), F32)
        for ci in reversed(range(nc)):
            rows = slice(ci * C, (ci + 1) * C)
            dsn = dstate_ref[...]
            for h in range(DN_HEADS):
                dst_ref[ci * DN_WIDTH + h * HD:ci * DN_WIDTH + (h + 1) * HD, :] = dsn[h]
            ov = _stack_heads(o_ref, rows)
            r = lax.rsqrt(jnp.mean(ov * ov, axis=-1, keepdims=True) + NORM_EPS)
            on = ov * r
            gt = _stack_heads(gate_ref, rows)
            sgt = _sigmoid(gt)
            silu_g = gt * sgt
            dy = _stack_heads(ddn_ref, rows)
            d_gain = d_gain + jnp.sum(jnp.sum(dy * on * silu_g, axis=1, keepdims=True), axis=0)
            dgate = dy * on * gain * (sgt * (1.0 + gt * (1.0 - sgt)))
            don = dy * gain * silu_g
            do = r * (don - on * jnp.mean(don * on, axis=-1, keepdims=True))
            d_vnew = _btn(aq_ref[:, rows, :], do) + _bnn(_stack_heads(kd_ref, rows), dsn)
            egl = jnp.stack([egl_ref[ci * 8 + h:ci * 8 + h + 1, :] for h in range(DN_HEADS)])
            dstate_ref[...] = _btn(_stack_heads(qg_ref, rows), do) + dsn * egl - _btn(_stack_heads(w_ref, rows), d_vnew)
            for h in range(DN_HEADS):
                sl = slice(h * HD, (h + 1) * HD)
                do_ref[rows, sl] = do[h]
                dvn_ref[rows, sl] = d_vnew[h]
                dgate_ref[rows, sl] = dgate[h]
        small_ref[...] += jnp.concatenate([d_gain, jnp.zeros((7, 128), F32)], axis=0)

    nb = N // nc
    tok = lambda wd: pl.BlockSpec((nc * C, wd), lambda i: (nb - 1 - i, 0))
    sq = pl.BlockSpec((DN_HEADS, nc * C, C), lambda i: (0, nb - 1 - i, 0))
    vec = pl.BlockSpec((1, 128), lambda i: (0, 0))
    return pl.pallas_call(
        body, name="dn_scan_bwd", grid=(nb,),
        in_specs=[tok(DN_WIDTH)] * 3 + [sq, pl.BlockSpec((nc * 8, 128), lambda i: (nb - 1 - i, 0)), tok(DN_WIDTH), vec,
                                       tok(DN_WIDTH), tok(DN_WIDTH)],
        out_specs=[tok(DN_WIDTH)] * 3 + [pl.BlockSpec((nc * DN_WIDTH, HD), lambda i: (nb - 1 - i, 0)),
                                        pl.BlockSpec((8, 128), lambda i: (0, 0))],
        out_shape=[jax.ShapeDtypeStruct((S, DN_WIDTH), F32)] * 3 + [jax.ShapeDtypeStruct((N * DN_WIDTH, HD), F32),
                                                                  jax.ShapeDtypeStruct((8, 128), F32)],
        scratch_shapes=[pltpu.VMEM((DN_HEADS, HD, HD), F32)],
        compiler_params=_params(1),
    )(w, qg, kd, aq, egl, gate, dn_gain, o, ddn)


def _dn_post(qn, kn, v, bd, avec, dvec, t_inv, v_new_all, states, dstates, do_all, dvn_all):
    S = qn.shape[0]
    C = DN_CHUNK
    N = S // C
    HD = DN_HEAD_DIM
    nc = PREP_CHUNKS
    B = nc * DN_HEADS

    def body(q_ref, k_ref, v_ref, bd_ref, a_ref, d_ref, t_ref, vn_ref, st_ref, dst_ref, do_ref, dvn_ref,
             dq_ref, dk_ref, dv_ref, dbd_ref, small_ref):
        @pl.when(pl.program_id(0) == 0)
        def _():
            small_ref[...] = jnp.zeros_like(small_ref)

        avec = a_ref[...]
        bds = [bd_ref[ci * C:(ci + 1) * C, :] for ci in range(nc)]
        k = _stack_units(k_ref, nc)
        vv = _stack_units(v_ref, nc)
        t = jnp.concatenate([t_ref[:, ci * C:(ci + 1) * C, :] for ci in range(nc)], axis=0)
        c = _dn_common_b(bds, avec, d_ref[...], _stack_units(q_ref, nc), k, vv, t=t)
        q, kb, eg, u, w = c["q"], c["kb"], c["eg"], c["u"], c["w"]
        beta, decay, incl, strict, eye = c["beta"], c["decay"], c["incl"], c["strict"], c["eye"]
        st = jnp.stack([st_ref[b * HD:(b + 1) * HD, :] for b in range(B)])
        dsn = jnp.stack([dst_ref[b * HD:(b + 1) * HD, :] for b in range(B)])
        v_new = _stack_units(vn_ref, nc)
        do = _stack_units(do_ref, nc)
        d_vnew = _stack_units(dvn_ref, nc)
        egl = jnp.exp(c["g_last"])
        daq = jnp.where(incl, _bnt(do, v_new), 0.0)
        d_qg = _bnt(do, st)
        d_kd = _bnt(v_new, dsn)
        d_glast = jnp.sum(jnp.sum(dsn * st, axis=-1, keepdims=True), axis=1, keepdims=True) * egl
        d_w = -_bnt(d_vnew, st)
        d_ru = _btn(t, d_vnew)
        d_rw = _btn(t, d_w)
        da = -jnp.where(strict, _bnt(d_ru, u) + _bnt(d_rw, w), 0.0)
        dv = d_ru * beta
        dbeta = jnp.sum(d_ru * vv, axis=-1, keepdims=True)
        dkb = d_rw * eg
        dgc = jnp.sum(d_rw * c["rhs_w"], axis=-1, keepdims=True)
        dkk = da * decay
        ddecay = da * c["kk"]
        dkb = dkb + _bnn(dkk, k)
        dk = _btn(dkk, kb)
        dqk = daq * decay
        ddecay = ddecay + daq * c["qk"]
        dq = _bnn(dqk, k)
        dk = dk + _btn(dqk, q)
        m = ddecay * decay
        col_sum = jnp.sum(m, axis=1, keepdims=True)
        dgc = dgc + jnp.sum(m, axis=-1, keepdims=True) - jnp.sum(eye * col_sum, axis=-1, keepdims=True)
        dq = dq + d_qg * eg
        dgc = dgc + jnp.sum(d_qg * c["qg"], axis=-1, keepdims=True)
        dk = dk + d_kd * c["ekd"]
        tk = jnp.sum(d_kd * c["kd"], axis=-1, keepdims=True)
        dgc = dgc - tk
        d_glast = d_glast + jnp.sum(tk, axis=1, keepdims=True)
        dk = dk + dkb * beta
        dbeta = dbeta + jnp.sum(dkb * k, axis=-1, keepdims=True)
        dgc = dgc + jnp.where(c["last"], d_glast, 0.0)
        dgc_row = jnp.sum(eye * dgc, axis=1, keepdims=True)
        dgraw = jnp.sum(jnp.where(c["col"] >= c["row"], dgc_row, 0.0), axis=-1, keepdims=True)
        _store_units(dq_ref, dq * (HD ** -0.5), nc)
        _store_units(dk_ref, dk, nc)
        _store_units(dv_ref, dv, nc)
        dbraw = dbeta * beta * (1.0 - beta)
        dzc = dgraw * _sigmoid(c["zc"])
        ga = dgraw * c["graw"]
        lane = c["lane"]
        lane1 = lax.broadcasted_iota(jnp.int32, (1, 128), 1)
        neg_ea = -jnp.exp(avec)
        d_alog = jnp.zeros((1, 128), F32)
        d_dt = jnp.zeros((1, 128), F32)
        for ci in range(nc):
            dbd = jnp.zeros((C, 128), F32)
            for h in range(DN_HEADS):
                b = ci * DN_HEADS + h
                dz = dzc[b] * neg_ea
                dbd = dbd + jnp.where(lane == h, dbraw[b], 0.0) + jnp.where(lane == DN_HEADS + h, dz, 0.0)
                d_alog = d_alog + jnp.where(lane1 == DN_HEADS + h, jnp.sum(ga[b], axis=0, keepdims=True), 0.0)
                d_dt = d_dt + jnp.where(lane1 == DN_HEADS + h, jnp.sum(dz, axis=0, keepdims=True), 0.0)
            dbd_ref[ci * C:(ci + 1) * C, :] = dbd
        small_ref[...] += jnp.concatenate([d_alog, d_dt, jnp.zeros((6, 128), F32)], axis=0)

    tok = lambda wd: pl.BlockSpec((nc * C, wd), lambda n: (n, 0))
    big = pl.BlockSpec((nc * DN_WIDTH, HD), lambda n: (n, 0))
    sq = pl.BlockSpec((DN_HEADS, nc * C, C), lambda n: (0, n, 0))
    vec = pl.BlockSpec((1, 128), lambda n: (0, 0))
    return pl.pallas_call(
        body, name="dn_post", grid=(N // nc,),
        in_specs=[tok(DN_WIDTH)] * 3 + [tok(128), vec, vec, sq, tok(DN_WIDTH), big, big, tok(DN_WIDTH), tok(DN_WIDTH)],
        out_specs=[tok(DN_WIDTH)] * 3 + [tok(128), pl.BlockSpec((8, 128), lambda n: (0, 0))],
        out_shape=[jax.ShapeDtypeStruct((S, DN_WIDTH), F32)] * 3 + [jax.ShapeDtypeStruct((S, 128), F32),
                                                                  jax.ShapeDtypeStruct((8, 128), F32)],
        compiler_params=_params(1),
    )(qn, kn, v, bd, avec, dvec, t_inv, v_new_all, states, dstates, do_all, dvn_all)


def _outproj_fwd(x, attn, dn, w_out):
    S, D = x.shape
    tm = 512

    def body(x_ref, a_ref, d_ref, w_ref, xo_ref, mix_ref):
        a = a_ref[...].astype(BF16)
        dd = d_ref[...].astype(BF16)
        mix_ref[:, 0:ATTN_WIDTH] = a
        mix_ref[:, ATTN_WIDTH:] = dd
        xo_ref[...] = x_ref[...] + _nn(a, w_ref[0:ATTN_WIDTH, :]) + _nn(dd, w_ref[ATTN_WIDTH:, :])

    tok = lambda w: pl.BlockSpec((tm, w), lambda i: (i, 0))
    return pl.pallas_call(
        body, name="outproj_fwd", grid=(S // tm,),
        in_specs=[tok(D), tok(ATTN_WIDTH), tok(DN_WIDTH), pl.BlockSpec((D, D), lambda i: (0, 0))],
        out_specs=[tok(D), tok(D)],
        out_shape=[jax.ShapeDtypeStruct((S, D), F32), jax.ShapeDtypeStruct((S, D), BF16)],
        compiler_params=_params(1),
    )(x, attn, dn, w_out)


def _outproj_bwd(dx, w_out):
    S, D = dx.shape
    tm = 512

    def body(dx_ref, w_ref, da_ref, dd_ref, dxb_ref):
        d = dx_ref[...].astype(BF16)
        dxb_ref[...] = d
        da_ref[...] = _nt(d, w_ref[0:ATTN_WIDTH, :])
        dd_ref[...] = _nt(d, w_ref[ATTN_WIDTH:, :])

    tok = lambda w: pl.BlockSpec((tm, w), lambda i: (i, 0))
    return pl.pallas_call(
        body, name="outproj_bwd", grid=(S // tm,),
        in_specs=[tok(D), pl.BlockSpec((D, D), lambda i: (0, 0))],
        out_specs=[tok(ATTN_WIDTH), tok(DN_WIDTH), tok(D)],
        out_shape=[jax.ShapeDtypeStruct((S, ATTN_WIDTH), F32), jax.ShapeDtypeStruct((S, DN_WIDTH), F32),
                   jax.ShapeDtypeStruct((S, D), BF16)],
        compiler_params=_params(1),
    )(dx, w_out)


def _loss_head(x, gain, target):
    S, D = x.shape
    tm = 512

    def body(x_ref, gain_ref, t_ref, loss_ref, dx_ref, dgain_ref):
        @pl.when(pl.program_id(0) == 0)
        def _():
            loss_ref[...] = jnp.zeros_like(loss_ref)
            dgain_ref[...] = jnp.zeros_like(dgain_ref)

        xf = x_ref[...]
        gain = gain_ref[...]
        r = lax.rsqrt(jnp.mean(xf * xf, axis=-1, keepdims=True) + NORM_EPS)
        xhat = xf * r
        err = xhat * gain - t_ref[...]
        part = 0.5 * jnp.sum(jnp.mean(err * err, axis=-1, keepdims=True), axis=0, keepdims=True)
        first = (lax.broadcasted_iota(jnp.int32, (8, 128), 0) == 0) & (lax.broadcasted_iota(jnp.int32, (8, 128), 1) == 0)
        loss_ref[...] += jnp.where(first, part, 0.0)
        dy = err * (1.0 / D)
        dgain_ref[...] += jnp.sum(dy * xhat, axis=0, keepdims=True)
        dxh = dy * gain
        dx_ref[...] = r * (dxh - xhat * jnp.mean(dxh * xhat, axis=-1, keepdims=True))

    tok = pl.BlockSpec((tm, D), lambda i: (i, 0))
    row = pl.BlockSpec((1, D), lambda i: (0, 0))
    return pl.pallas_call(
        body, name="loss_head", grid=(S // tm,),
        in_specs=[tok, row, tok],
        out_specs=[pl.BlockSpec((8, 128), lambda i: (0, 0)), tok, row],
        out_shape=[jax.ShapeDtypeStruct((8, 128), F32), jax.ShapeDtypeStruct((S, D), F32),
                   jax.ShapeDtypeStruct((1, D), F32)],
        compiler_params=_params(1),
    )(x, gain, target)


def _adamw(w, g, m, v, name):
    R, Ccols = w.shape
    tr = R
    for cand in (256, 128, 64, 32, 16, 8):
        if R % cand == 0:
            tr = cand
            break
    c1 = 1.0 - ADAM_B1 ** ADAM_STEP
    c2 = 1.0 - ADAM_B2 ** ADAM_STEP

    def body(w_ref, g_ref, m_ref, v_ref, d_ref, nm_ref, nv_ref):
        gv = g_ref[...]
        mn = ADAM_B1 * m_ref[...] + (1.0 - ADAM_B1) * gv
        vn = ADAM_B2 * v_ref[...] + (1.0 - ADAM_B2) * (gv * gv)
        nm_ref[...] = mn
        nv_ref[...] = vn
        d_ref[...] = -ADAM_LR * ((mn / c1) / (jnp.sqrt(vn / c2) + ADAM_EPS) + ADAM_WD * w_ref[...])

    spec = pl.BlockSpec((tr, Ccols), lambda i: (i, 0))
    return pl.pallas_call(
        body, name=name, grid=(R // tr,), in_specs=[spec] * 4, out_specs=[spec] * 3,
        out_shape=[jax.ShapeDtypeStruct((R, Ccols), F32)] * 3, compiler_params=_params(1),
    )(w, g, m, v)


def _local_step(x, target, wts, small):
    g1, g2, gm, gf = small["norm_ffn1"], small["norm_ffn2"], small["norm_mix"], small["norm_final"]

    x1, h1, fg1, fu1 = _ffn_fwd(x, g1, wts["ffn1_gate"], wts["ffn1_up"], wts["ffn1_down"], "ffn1_fwd")
    h2, aq, ak, av, xq, xk, xv, gate, bd = _inproj_fwd(x1, gm, wts["w_in"])
    parts = [_attn_fwd(aq, ak, av, d, f"attn_fwd_d{d}") for d in DILATIONS]
    attn, lse = _attn_merge(parts)
    conv_w = small["conv_w"]
    qn, kn, vv = _conv_fwd(xq, xk, xv, conv_w)
    dn_u, dn_w, dn_qg, dn_kd, dn_aq, dn_t, dn_egl = _dn_prep(qn, kn, vv, bd, small["avec"], small["dvec"])
    dn, o_dn, v_new, states = _dn_scan_fwd(dn_u, dn_w, dn_qg, dn_kd, dn_aq, dn_egl, gate, small["dn_norm"])
    x2, mix = _outproj_fwd(x1, attn, dn, wts["w_out"])
    x3, h3, fg2, fu2 = _ffn_fwd(x2, g2, wts["ffn2_gate"], wts["ffn2_up"], wts["ffn2_down"], "ffn2_fwd")
    loss, dx3, d_gf = _loss_head(x3, gf, target)

    grads = {}
    dx2, d_g2, dfg2, dfu2, act2, dout2 = _ffn_bwd(dx3, x2, g2, fg2, fu2, wts["ffn2_down"], wts["ffn2_gate"],
                                                 wts["ffn2_up"], "ffn2_bwd")
    tk = 512
    grads["ffn2_gate"] = _matmul_tn(dfg2, h3, D_FF // 2, tk, "dw_ffn2_gate")
    grads["ffn2_up"] = _matmul_tn(dfu2, h3, D_FF // 2, tk, "dw_ffn2_up")
    grads["ffn2_down"] = _matmul_tn(act2, dout2, D_FF // 2, tk, "dw_ffn2_down")

    dattn, ddn, dx2b = _outproj_bwd(dx2, wts["w_out"])
    grads["w_out"] = _matmul_tn(mix, dx2b, D_MODEL, tk, "dw_out")

    dd = _attn_delta(dattn, attn)
    daq = dak = dav = None
    for d in DILATIONS:
        daq = _attn_bwd_q(aq, ak, av, dattn, lse, dd, daq, d, f"attn_bwd_q_d{d}")
        dak, dav = _attn_bwd_kv(aq, ak, av, dattn, lse, dd, dak, dav, d, f"attn_bwd_kv_d{d}")

    do_dn, dvn, dgate, dstates, d_dn_gain = _dn_scan_bwd(dn_w, dn_qg, dn_kd, dn_aq, dn_egl, gate, small["dn_norm"], o_dn, ddn)
    dqn, dkn, dvv, dbd, dn_small = _dn_post(qn, kn, vv, bd, small["avec"], small["dvec"], dn_t, v_new, states, dstates,
                                            do_dn, dvn)
    dcq, dck, dcv, dwq, dwk, dwv = _conv_bwd_pre(xq, xk, xv, conv_w, dqn, dkn, dvv)
    dxq, dxk, dxv = _conv_bwd_x(dcq, dck, dcv, conv_w)
    d_conv = jnp.concatenate([dwq[:CONV_WIDTH], dwk[:CONV_WIDTH], dwv[:CONV_WIDTH]], axis=1)

    dx1, d_gm, dproj = _inproj_bwd(dx2, x1, gm, [daq, dak, dav, dxq, dxk, dxv, dgate], dbd, wts["w_in"])
    grads["w_in"] = _matmul_tn(dproj, h2, IN_COLS_PADDED, 256, "dw_in")

    dx0, d_g1, dfg1, dfu1, act1, dout1 = _ffn_bwd(dx1, x, g1, fg1, fu1, wts["ffn1_down"], wts["ffn1_gate"],
                                                 wts["ffn1_up"], "ffn1_bwd")
    grads["ffn1_gate"] = _matmul_tn(dfg1, h1, D_FF // 2, tk, "dw_ffn1_gate")
    grads["ffn1_up"] = _matmul_tn(dfu1, h1, D_FF // 2, tk, "dw_ffn1_up")
    grads["ffn1_down"] = _matmul_tn(act1, dout1, D_FF // 2, tk, "dw_ffn1_down")

    small_grads = dict(norm_ffn1=d_g1, norm_mix=d_gm, norm_ffn2=d_g2, norm_final=d_gf, conv_w=d_conv,
                       a_log=dn_small[0:1], dt_bias=dn_small[1:2], dn_norm=d_dn_gain[0:1])
    return loss, dx0, grads, small_grads


PACK_SECTIONS = (("ffn1_gate", 704), ("ffn1_up", 704), ("ffn1_down", 704), ("w_in", 898), ("w_out", 256),
                 ("ffn2_gate", 704), ("ffn2_up", 704), ("ffn2_down", 704))
PACK_ROWS = 5408
HALF_ROWS = PACK_ROWS // 2
ADD_ROWS = 208

HBM = pl.BlockSpec(memory_space=pl.ANY)
VMEM_SPEC = pl.BlockSpec(memory_space=pltpu.VMEM)


def _coords():
    return lax.axis_index("x"), lax.axis_index("y"), lax.axis_index("c")


def _remote(src, dst, send_sems, recv_sems, k, dev):
    return pltpu.make_async_remote_copy(src_ref=src, dst_ref=dst, send_sem=send_sems.at[k], recv_sem=recv_sems.at[k],
                                        device_id=dev, device_id_type=MESH)


def _allreduce_small(buf, name):
    R, Cc = buf.shape

    def body(src_ref, out_ref, recv_ref, send_sems, recv_sems):
        x, y, c = _coords()
        copies = []
        for m in range(1, 8):
            fx, fy, fc = (m >> 2) & 1, (m >> 1) & 1, m & 1
            dev = (x ^ fx if fx else x, y ^ fy if fy else y, c ^ fc if fc else c)
            cp = _remote(src_ref, recv_ref.at[m - 1], send_sems, recv_sems, m - 1, dev)
            cp.start()
            copies.append(cp)
        for cp in copies:
            cp.wait()
        r = [src_ref[...]] + [recv_ref[m] for m in range(7)]
        out_ref[...] = ((r[0] + r[1]) + (r[2] + r[3])) + ((r[4] + r[5]) + (r[6] + r[7]))

    return pl.pallas_call(
        body, name=name, out_shape=jax.ShapeDtypeStruct((R, Cc), F32),
        in_specs=[VMEM_SPEC], out_specs=VMEM_SPEC,
        scratch_shapes=[pltpu.VMEM((7, R, Cc), F32), pltpu.SemaphoreType.DMA((7,)), pltpu.SemaphoreType.DMA((7,))],
    )(buf)


def _allgather_weights(pack2):
    _, Hh, Cc = pack2.shape

    def body(src_ref, out_ref, send_sems, recv_sems):
        x, y, c = _coords()
        sib = (x, y, 1 - c)
        others = [(1 - x, y), (x, 1 - y), (1 - x, 1 - y)]
        blk = lambda cx, cy, half: out_ref.at[2 * cx + cy, half]
        mine = _remote(src_ref, out_ref.at[2 * x + y], send_sems, recv_sems, 6, sib)
        mine.start()
        first = [_remote(src_ref.at[c], blk(x, y, c), send_sems, recv_sems, j, (ox, oy, c)) for j, (ox, oy) in enumerate(others)]
        for cp in first:
            cp.start()
        passed = [_remote(blk(ox, oy, c), blk(ox, oy, c), send_sems, recv_sems, 3 + j, sib) for j, (ox, oy) in enumerate(others)]
        for j, (ox, oy) in enumerate(others):
            _remote(src_ref.at[c], blk(ox, oy, c), send_sems, recv_sems, j, (ox, oy, c)).wait_recv()
            passed[j].start()
        for j, (ox, oy) in enumerate(others):
            _remote(src_ref.at[c], blk(ox, oy, 1 - c), send_sems, recv_sems, 3 + j, sib).wait_recv()
        for cp in first + passed:
            cp.wait_send()
        mine.wait()

    return pl.pallas_call(
        body, name="allgather_weights", out_shape=jax.ShapeDtypeStruct((N_CHIPS, 2, Hh, Cc), pack2.dtype),
        in_specs=[HBM], out_specs=HBM,
        scratch_shapes=[pltpu.SemaphoreType.DMA((7,)), pltpu.SemaphoreType.DMA((7,))],
    )(pack2)


def _rs_swap_halves(gpack):
    _, nj, Hh, Cc = gpack.shape

    def body(src_ref, out_ref, send_sems, recv_sems):
        x, y, c = _coords()
        cp = _remote(src_ref.at[1 - c], out_ref, send_sems, recv_sems, 0, (x, y, 1 - c))
        cp.start()
        cp.wait()

    return pl.pallas_call(
        body, name="rs_swap_halves", out_shape=jax.ShapeDtypeStruct((nj, Hh, Cc), gpack.dtype),
        in_specs=[HBM], out_specs=HBM,
        scratch_shapes=[pltpu.SemaphoreType.DMA((1,)), pltpu.SemaphoreType.DMA((1,))],
    )(gpack)


def _rs_add_pair(gpack, other, c):
    _, nj, Hh, Cc = gpack.shape
    tr = ADD_ROWS

    def body(c_ref, a_ref, b_ref, o_ref):
        o_ref[...] = (a_ref[...] + b_ref[...]).astype(BF16)

    return pl.pallas_call(
        body, name="rs_add_pair",
        grid_spec=pltpu.PrefetchScalarGridSpec(
            num_scalar_prefetch=1, grid=(nj, Hh // tr),
            in_specs=[pl.BlockSpec((None, None, tr, Cc), lambda j, i, c_ref: (c_ref[0], j, i, 0)),
                      pl.BlockSpec((None, tr, Cc), lambda j, i, c_ref: (j, i, 0))],
            out_specs=pl.BlockSpec((None, tr, Cc), lambda j, i, c_ref: (j, i, 0))),
        out_shape=jax.ShapeDtypeStruct((nj, Hh, Cc), BF16),
        compiler_params=_params(2),
    )(c, gpack, other)


def _rs_exchange_chips(part):
    nj, Hh, Cc = part.shape

    def body(src_ref, out_ref, send_sems, recv_sems):
        x, y, c = _coords()
        others = [(1 - x, y), (x, 1 - y), (1 - x, 1 - y)]
        cps = [_remote(src_ref.at[2 * ox + oy], out_ref.at[k], send_sems, recv_sems, k, (ox, oy, c))
               for k, (ox, oy) in enumerate(others)]
        for cp in cps:
            cp.start()
        for cp in cps:
            cp.wait()

    return pl.pallas_call(
        body, name="rs_exchange_chips", out_shape=jax.ShapeDtypeStruct((3, Hh, Cc), part.dtype),
        in_specs=[HBM], out_specs=HBM,
        scratch_shapes=[pltpu.SemaphoreType.DMA((3,)), pltpu.SemaphoreType.DMA((3,))],
    )(part)


def _rs_add_total(part, recv, chip):
    nj, Hh, Cc = part.shape
    tr = ADD_ROWS

    def body(chip_ref, p_ref, r0_ref, r1_ref, r2_ref, o_ref):
        f = lambda r: r[...].astype(F32)
        o_ref[...] = (f(p_ref) + f(r0_ref)) + (f(r1_ref) + f(r2_ref))

    rk = lambda k: pl.BlockSpec((None, tr, Cc), lambda i, chip_ref, k=k: (k, i, 0))
    return pl.pallas_call(
        body, name="rs_add_total",
        grid_spec=pltpu.PrefetchScalarGridSpec(
            num_scalar_prefetch=1, grid=(Hh // tr,),
            in_specs=[pl.BlockSpec((None, tr, Cc), lambda i, chip_ref: (chip_ref[0], i, 0)), rk(0), rk(1), rk(2)],
            out_specs=pl.BlockSpec((tr, Cc), lambda i, chip_ref: (i, 0))),
        out_shape=jax.ShapeDtypeStruct((Hh, Cc), F32),
        compiler_params=_params(1),
    )(chip, part, recv, recv, recv)


def _rs_share_total(total):
    Hh, Cc = total.shape

    def body(src_ref, out_ref, send_sems, recv_sems):
        x, y, c = _coords()
        cp = _remote(src_ref, out_ref, send_sems, recv_sems, 0, (x, y, 1 - c))
        cp.start()
        cp.wait()

    return pl.pallas_call(
        body, name="rs_share_total", out_shape=jax.ShapeDtypeStruct((Hh, Cc), total.dtype),
        in_specs=[HBM], out_specs=HBM,
        scratch_shapes=[pltpu.SemaphoreType.DMA((1,)), pltpu.SemaphoreType.DMA((1,))],
    )(total)


def _permute_w_in(w):
    return jnp.concatenate([w[:, :3072], w[:, 3080:IN_COLS], w[:, 3072:3080],
                            jnp.zeros((w.shape[0], IN_COLS_PADDED - IN_COLS), w.dtype)], axis=1)


def _pack_rows(shards, dtype):
    rows = [shards[n].astype(dtype).reshape(r, D_MODEL) for n, r in PACK_SECTIONS]
    used = sum(r for _, r in PACK_SECTIONS)
    rows.append(jnp.zeros((PACK_ROWS - used, D_MODEL), dtype))
    return jnp.concatenate(rows, axis=0)


def _unpack_rows(pack, shapes):
    out, at = {}, 0
    for n, r in PACK_SECTIONS:
        out[n] = pack[at:at + r].reshape(shapes[n])
        at += r
    return out


SHARD_SHAPES = dict(ffn1_gate=(1024, 704), ffn1_up=(1024, 704), ffn1_down=(704, 1024), w_in=(1024, 898),
                    w_out=(256, 1024), ffn2_gate=(1024, 704), ffn2_up=(1024, 704), ffn2_down=(704, 1024))
ROW_SHARDED = ("ffn1_down", "w_out", "ffn2_down")
SMALL_ROWS = 16


def _pad_row(v):
    v = v.reshape(1, -1)
    return jnp.pad(v, ((0, 0), (0, D_MODEL - v.shape[1])))


def kernel(x, norm_ffn1, ffn1_gate, ffn1_up, ffn1_down, norm_mix, w_in, conv_w, a_log, dt_bias, dn_norm, w_out, norm_ffn2, ffn2_gate, ffn2_up, ffn2_down, norm_final, loss_target, m_norm_ffn1, m_ffn1_gate, m_ffn1_up, m_ffn1_down, m_norm_mix, m_w_in, m_conv_w, m_a_log, m_dt_bias, m_dn_norm, m_w_out, m_norm_ffn2, m_ffn2_gate, m_ffn2_up, m_ffn2_down, m_norm_final, v_norm_ffn1, v_ffn1_gate, v_ffn1_up, v_ffn1_down, v_norm_mix, v_w_in, v_conv_w, v_a_log, v_dt_bias, v_dn_norm, v_w_out, v_norm_ffn2, v_ffn2_gate, v_ffn2_up, v_ffn2_down, v_norm_final):
    cx, cy, cc = _coords()
    chip = 2 * cx + cy
    big_w = dict(ffn1_gate=ffn1_gate[0], ffn1_up=ffn1_up[0], ffn1_down=ffn1_down[0], w_in=w_in[0], w_out=w_out[0],
                 ffn2_gate=ffn2_gate[0], ffn2_up=ffn2_up[0], ffn2_down=ffn2_down[0])
    big_m = dict(ffn1_gate=m_ffn1_gate[0], ffn1_up=m_ffn1_up[0], ffn1_down=m_ffn1_down[0], w_in=m_w_in[0], w_out=m_w_out[0],
                 ffn2_gate=m_ffn2_gate[0], ffn2_up=m_ffn2_up[0], ffn2_down=m_ffn2_down[0])
    big_v = dict(ffn1_gate=v_ffn1_gate[0], ffn1_up=v_ffn1_up[0], ffn1_down=v_ffn1_down[0], w_in=v_w_in[0], w_out=v_w_out[0],
                 ffn2_gate=v_ffn2_gate[0], ffn2_up=v_ffn2_up[0], ffn2_down=v_ffn2_down[0])

    pack = _pack_rows(big_w, BF16).reshape(2, HALF_ROWS, D_MODEL)
    gathered = _allgather_weights(pack).reshape(N_CHIPS, PACK_ROWS, D_MODEL)
    per_chip = [_unpack_rows(gathered[j], SHARD_SHAPES) for j in range(N_CHIPS)]
    wts = {n: jnp.concatenate([per_chip[j][n] for j in range(N_CHIPS)], axis=0 if n in ROW_SHARDED else 1)
           for n, _ in PACK_SECTIONS}
    wts["w_in"] = _permute_w_in(wts["w_in"])

    conv_shard = conv_w[0]
    emb = jnp.concatenate([jnp.where((chip == j) & (cc == 0), conv_shard, 0.0) for j in range(N_CHIPS)], axis=1)
    emb = jnp.pad(emb.reshape(6, D_MODEL), ((0, 2), (0, 0)))
    conv_full = _allreduce_small(emb, "allgather_conv_w")[:6].reshape(CONV_WIDTH, 3 * DN_WIDTH)

    zvec = jnp.zeros((1, 128), F32)
    small = dict(norm_ffn1=norm_ffn1, norm_mix=norm_mix, norm_ffn2=norm_ffn2, norm_final=norm_final[None],
                 conv_w=conv_full, avec=zvec.at[0, DN_HEADS:2 * DN_HEADS].set(a_log[0]),
                 dvec=zvec.at[0, DN_HEADS:2 * DN_HEADS].set(dt_bias[0]), dn_norm=dn_norm)

    loss, grad_x, grads, sg = _local_step(x[0], loss_target[0], wts, small)

    rows = [sg["norm_ffn1"], sg["norm_mix"], sg["norm_ffn2"], sg["norm_final"], _pad_row(sg["a_log"]), _pad_row(sg["dt_bias"]),
            _pad_row(sg["dn_norm"]), _pad_row(loss[0:1]), sg["conv_w"].reshape(6, D_MODEL), jnp.zeros((2, D_MODEL), F32)]
    red = _allreduce_small(jnp.concatenate(rows, axis=0), "allreduce_small")
    loss_out = red[7, 0]
    g_conv_full = red[8:14].reshape(CONV_WIDTH, 3 * DN_WIDTH)
    g_conv = lax.dynamic_slice_in_dim(g_conv_full, chip * (3 * DN_WIDTH // N_CHIPS), 3 * DN_WIDTH // N_CHIPS, axis=1)
    g_small = dict(norm_ffn1=red[0:1], norm_mix=red[1:2], norm_ffn2=red[2:3], norm_final=red[3],
                   a_log=red[4:5, DN_HEADS:2 * DN_HEADS], dt_bias=red[5:6, DN_HEADS:2 * DN_HEADS], dn_norm=red[6:7, :DN_HEAD_DIM])

    gi = grads["w_in"]
    grads["w_in"] = jnp.concatenate([gi[:3072], gi[3584:3592], gi[3072:3584]], axis=0)
    packs = []
    for j in range(N_CHIPS):
        sh = {n: grads[n][j * r:(j + 1) * r] for n, r in PACK_SECTIONS}
        packs.append(_pack_rows(sh, F32).reshape(2, HALF_ROWS, D_MODEL))
    gpack = jnp.stack(packs, axis=1)
    from_sibling = _rs_swap_halves(gpack)
    part = _rs_add_pair(gpack, from_sibling, cc.reshape(1).astype(jnp.int32))
    recv = _rs_exchange_chips(part)
    total = _rs_add_total(part, recv, chip.reshape(1).astype(jnp.int32))
    other = _rs_share_total(total)
    reduced = jnp.where(cc == 0, jnp.concatenate([total, other], axis=0), jnp.concatenate([other, total], axis=0))
    shard_g, at = {}, 0
    for n, r in PACK_SECTIONS:
        sec = reduced[at:at + r]
        shard_g[n] = sec if n in ROW_SHARDED else sec.T
        at += r

    out_g, out_d, out_m, out_v = {}, {}, {}, {}
    for n, _ in PACK_SECTIONS:
        d, nm, nv = _adamw(big_w[n], shard_g[n], big_m[n], big_v[n], "adamw_" + n)
        out_g[n], out_d[n], out_m[n], out_v[n] = shard_g[n][None], d[None], nm[None], nv[None]
    d, nm, nv = _adamw(conv_w[0], g_conv, m_conv_w[0], v_conv_w[0], "adamw_conv_w")
    out_g["conv_w"], out_d["conv_w"], out_m["conv_w"], out_v["conv_w"] = g_conv[None], d[None], nm[None], nv[None]

    small_names = ("norm_ffn1", "norm_mix", "norm_ffn2", "norm_final", "a_log", "dt_bias", "dn_norm")
    small_w = dict(norm_ffn1=norm_ffn1, norm_mix=norm_mix, norm_ffn2=norm_ffn2, norm_final=norm_final, a_log=a_log,
                   dt_bias=dt_bias, dn_norm=dn_norm)
    small_m = dict(norm_ffn1=m_norm_ffn1, norm_mix=m_norm_mix, norm_ffn2=m_norm_ffn2, norm_final=m_norm_final, a_log=m_a_log,
                   dt_bias=m_dt_bias, dn_norm=m_dn_norm)
    small_v = dict(norm_ffn1=v_norm_ffn1, norm_mix=v_norm_mix, norm_ffn2=v_norm_ffn2, norm_final=v_norm_final, a_log=v_a_log,
                   dt_bias=v_dt_bias, dn_norm=v_dn_norm)
    stack = lambda dct: jnp.concatenate([_pad_row(dct[n]) for n in small_names] + [jnp.zeros((1, D_MODEL), F32)], axis=0)
    d, nm, nv = _adamw(stack(small_w), stack(g_small), stack(small_m), stack(small_v), "adamw_small")
    for k, n in enumerate(small_names):
        shape = small_w[n].shape
        size = math.prod(shape)
        out_g[n] = g_small[n].reshape(shape)
        out_d[n], out_m[n], out_v[n] = (t[k, :size].reshape(shape) for t in (d, nm, nv))

    order = ("norm_ffn1", "ffn1_gate", "ffn1_up", "ffn1_down", "norm_mix", "w_in", "conv_w", "a_log", "dt_bias", "dn_norm",
             "w_out", "norm_ffn2", "ffn2_gate", "ffn2_up", "ffn2_down", "norm_final")
    return (loss_out, grad_x[None], *[out_g[n] for n in order], *[out_d[n] for n in order],
            *[out_m[n] for n in order], *[out_v[n] for n in order])
```

```python
import functools
import math

import jax
import jax.numpy as jnp
from jax import lax
from jax.experimental import pallas as pl
from jax.experimental.pallas import tpu as pltpu

F32 = jnp.float32
BF16 = jnp.bfloat16
HI = lax.Precision.HIGH

D_MODEL = 1024
D_FF = 2816
ATTN_HEADS = 8
ATTN_WIDTH = 512
ATTN_BLOCK = 128
DILATIONS = (1, 4, 16)
DN_HEADS = 4
DN_HEAD_DIM = 128
DN_WIDTH = 512
DN_CHUNK = 64
CONV_WIDTH = 4
NORM_EPS = 1e-6
L2_EPS = 1e-6
IN_COLS = 3592
IN_COLS_PADDED = 3712
N_CHIPS = 4

ADAM_LR = 0.001
ADAM_B1 = 0.9
ADAM_B2 = 0.999
ADAM_EPS = 1e-08
ADAM_WD = 0.01
ADAM_STEP = 10

VMEM_LIMIT = 56 * 1024 * 1024
NEG_BIG = -1e30
MESH = pl.DeviceIdType.MESH


def _params(n_grid, vmem=VMEM_LIMIT):
    return pltpu.CompilerParams(dimension_semantics=("arbitrary",) * n_grid, vmem_limit_bytes=vmem)


def _nt(a, b, precision=None):
    return lax.dot_general(a, b, (((1,), (1,)), ((), ())), preferred_element_type=F32, precision=precision)


def _tn(a, b, precision=None):
    return lax.dot_general(a, b, (((0,), (0,)), ((), ())), preferred_element_type=F32, precision=precision)


def _nn(a, b, precision=None):
    return jnp.dot(a, b, preferred_element_type=F32, precision=precision)


def _sigmoid(x):
    return 1.0 / (1.0 + jnp.exp(-x))


def _ffn_fwd(x, gain, wg, wu, wd, name):
    S, D = x.shape
    nf, _, tf = wg.shape
    tm = 512

    def body(x_ref, gain_ref, wg_ref, wu_ref, wd_ref, xo_ref, h_ref, g_ref, u_ref, acc_ref, hs_ref):
        j = pl.program_id(1)

        @pl.when(j == 0)
        def _():
            xf = x_ref[...]
            r = lax.rsqrt(jnp.mean(xf * xf, axis=-1, keepdims=True) + NORM_EPS)
            h = (xf * r * gain_ref[...]).astype(BF16)
            hs_ref[...] = h
            h_ref[...] = h
            acc_ref[...] = jnp.zeros_like(acc_ref)

        h = hs_ref[...]
        g = _nn(h, wg_ref[...])
        u = _nn(h, wu_ref[...])
        g_ref[...] = g.astype(BF16)
        u_ref[...] = u.astype(BF16)
        act = g * _sigmoid(g) * u
        acc_ref[...] += _nn(act.astype(BF16), wd_ref[...])

        @pl.when(j == nf - 1)
        def _():
            xo_ref[...] = x_ref[...] + 0.5 * acc_ref[...]

    return pl.pallas_call(
        body, name=name, grid=(S // tm, nf),
        in_specs=[pl.BlockSpec((tm, D), lambda i, j: (i, 0)),
                  pl.BlockSpec((1, D), lambda i, j: (0, 0)),
                  pl.BlockSpec((None, D, tf), lambda i, j: (j, 0, 0)),
                  pl.BlockSpec((None, D, tf), lambda i, j: (j, 0, 0)),
                  pl.BlockSpec((None, tf, D), lambda i, j: (j, 0, 0))],
        out_specs=[pl.BlockSpec((tm, D), lambda i, j: (i, 0)),
                   pl.BlockSpec((tm, D), lambda i, j: (i, 0)),
                   pl.BlockSpec((None, tm, tf), lambda i, j: (j, i, 0)),
                   pl.BlockSpec((None, tm, tf), lambda i, j: (j, i, 0))],
        out_shape=[jax.ShapeDtypeStruct((S, D), F32), jax.ShapeDtypeStruct((S, D), BF16),
                   jax.ShapeDtypeStruct((nf, S, tf), BF16), jax.ShapeDtypeStruct((nf, S, tf), BF16)],
        scratch_shapes=[pltpu.VMEM((tm, D), F32), pltpu.VMEM((tm, D), BF16)],
        compiler_params=_params(2),
    )(x, gain, wg, wu, wd)


def _rmsnorm_bwd(dh, xf, gain):
    r = lax.rsqrt(jnp.mean(xf * xf, axis=-1, keepdims=True) + NORM_EPS)
    xhat = xf * r
    dgain = jnp.sum(dh * xhat, axis=0, keepdims=True)
    dxh = dh * gain
    dx = r * (dxh - xhat * jnp.mean(dxh * xhat, axis=-1, keepdims=True))
    return dx, dgain


def _ffn_bwd(dxo, x, gain, g, u, wd, wg, wu, name):
    S, D = x.shape
    nf, _, tf = g.shape
    tm = 512

    def body(dxo_ref, x_ref, gain_ref, g_ref, u_ref, wd_ref, wg_ref, wu_ref,
             dx_ref, dgain_ref, dg_ref, du_ref, act_ref, dout_ref, acc_ref, ds_ref):
        i = pl.program_id(0)
        j = pl.program_id(1)

        @pl.when(j == 0)
        def _():
            d = (0.5 * dxo_ref[...]).astype(BF16)
            ds_ref[...] = d
            dout_ref[...] = d
            acc_ref[...] = jnp.zeros_like(acc_ref)

        @pl.when((i == 0) & (j == 0))
        def _():
            dgain_ref[...] = jnp.zeros_like(dgain_ref)

        dact = _nt(ds_ref[...], wd_ref[...])
        gv = g_ref[...].astype(F32)
        uv = u_ref[...].astype(F32)
        sg = _sigmoid(gv)
        silu = gv * sg
        act_ref[...] = (silu * uv).astype(BF16)
        dgv = (dact * uv * (sg * (1.0 + gv * (1.0 - sg)))).astype(BF16)
        duv = (dact * silu).astype(BF16)
        dg_ref[...] = dgv
        du_ref[...] = duv
        acc_ref[...] += _nt(dgv, wg_ref[...]) + _nt(duv, wu_ref[...])

        @pl.when(j == nf - 1)
        def _():
            dx, dgain = _rmsnorm_bwd(acc_ref[...], x_ref[...], gain_ref[...])
            dx_ref[...] = dxo_ref[...] + dx
            dgain_ref[...] += dgain

    return pl.pallas_call(
        body, name=name, grid=(S // tm, nf),
        in_specs=[pl.BlockSpec((tm, D), lambda i, j: (i, 0)),
                  pl.BlockSpec((tm, D), lambda i, j: (i, 0)),
                  pl.BlockSpec((1, D), lambda i, j: (0, 0)),
                  pl.BlockSpec((None, tm, tf), lambda i, j: (j, i, 0)),
                  pl.BlockSpec((None, tm, tf), lambda i, j: (j, i, 0)),
                  pl.BlockSpec((None, tf, D), lambda i, j: (j, 0, 0)),
                  pl.BlockSpec((None, D, tf), lambda i, j: (j, 0, 0)),
                  pl.BlockSpec((None, D, tf), lambda i, j: (j, 0, 0))],
        out_specs=[pl.BlockSpec((tm, D), lambda i, j: (i, 0)),
                   pl.BlockSpec((1, D), lambda i, j: (0, 0)),
                   pl.BlockSpec((None, tm, tf), lambda i, j: (j, i, 0)),
                   pl.BlockSpec((None, tm, tf), lambda i, j: (j, i, 0)),
                   pl.BlockSpec((None, tm, tf), lambda i, j: (j, i, 0)),
                   pl.BlockSpec((tm, D), lambda i, j: (i, 0))],
        out_shape=[jax.ShapeDtypeStruct((S, D), F32), jax.ShapeDtypeStruct((1, D), F32),
                   jax.ShapeDtypeStruct((nf, S, tf), BF16), jax.ShapeDtypeStruct((nf, S, tf), BF16),
                   jax.ShapeDtypeStruct((nf, S, tf), BF16), jax.ShapeDtypeStruct((S, D), BF16)],
        scratch_shapes=[pltpu.VMEM((tm, D), F32), pltpu.VMEM((tm, D), BF16)],
        compiler_params=_params(2),
    )(dxo, x, gain, g, u, wd, wg, wu)


def _matmul_tn(a, b, tm, tk, name):
    K, M = a.shape
    N = b.shape[1]

    def body(a_ref, b_ref, o_ref):
        @pl.when(pl.program_id(1) == 0)
        def _():
            o_ref[...] = jnp.zeros_like(o_ref)

        o_ref[...] += _tn(a_ref[...], b_ref[...])

    return pl.pallas_call(
        body, name=name, grid=(M // tm, K // tk),
        in_specs=[pl.BlockSpec((tk, tm), lambda i, k: (k, i)),
                  pl.BlockSpec((tk, N), lambda i, k: (k, 0))],
        out_specs=pl.BlockSpec((tm, N), lambda i, k: (i, 0)),
        out_shape=jax.ShapeDtypeStruct((M, N), F32),
        compiler_params=_params(2),
    )(a, b)


def _dw_chunks(a, b, tk, name):
    nf, S, tf = a.shape
    N = b.shape[1]

    def body(a_ref, b_ref, o_ref):
        @pl.when(pl.program_id(1) == 0)
        def _():
            o_ref[...] = jnp.zeros_like(o_ref)

        o_ref[...] += _tn(a_ref[...], b_ref[...])

    return pl.pallas_call(
        body, name=name, grid=(nf, S // tk),
        in_specs=[pl.BlockSpec((None, tk, tf), lambda j, k: (j, k, 0)),
                  pl.BlockSpec((tk, N), lambda j, k: (k, 0))],
        out_specs=pl.BlockSpec((None, tf, N), lambda j, k: (j, 0, 0)),
        out_shape=jax.ShapeDtypeStruct((nf, tf, N), F32),
        compiler_params=_params(2),
    )(a, b)


def _inproj_fwd(x, gain, w_in_p):
    S, D = x.shape
    tm = 512
    W = ATTN_WIDTH

    def body(x_ref, gain_ref, w_ref, h_ref, aq_ref, ak_ref, av_ref, dq_ref, dk_ref, dv_ref, gate_ref, bd_ref):
        xf = x_ref[...]
        r = lax.rsqrt(jnp.mean(xf * xf, axis=-1, keepdims=True) + NORM_EPS)
        h = (xf * r * gain_ref[...]).astype(BF16)
        h_ref[...] = h
        aq_ref[...] = (_nn(h, w_ref[:, 0:W]) * 0.125).astype(BF16)
        ak_ref[...] = _nn(h, w_ref[:, W:2 * W]).astype(BF16)
        av_ref[...] = _nn(h, w_ref[:, 2 * W:3 * W]).astype(BF16)
        dq_ref[...] = _nn(h, w_ref[:, 3 * W:4 * W])
        dk_ref[...] = _nn(h, w_ref[:, 4 * W:5 * W])
        dv_ref[...] = _nn(h, w_ref[:, 5 * W:6 * W])
        gate_ref[...] = _nn(h, w_ref[:, 6 * W:7 * W])
        bd_ref[...] = _nn(h, w_ref[:, 7 * W:7 * W + 128])

    tok = lambda w: pl.BlockSpec((tm, w), lambda i: (i, 0))
    return pl.pallas_call(
        body, name="inproj_fwd", grid=(S // tm,),
        in_specs=[tok(D), pl.BlockSpec((1, D), lambda i: (0, 0)),
                  pl.BlockSpec((D, IN_COLS_PADDED), lambda i: (0, 0))],
        out_specs=[tok(D)] + [tok(W)] * 7 + [tok(128)],
        out_shape=[jax.ShapeDtypeStruct((S, D), BF16)] + [jax.ShapeDtypeStruct((S, W), BF16)] * 3
                  + [jax.ShapeDtypeStruct((S, W), F32)] * 4 + [jax.ShapeDtypeStruct((S, 128), F32)],
        compiler_params=_params(1),
    )(x, gain, w_in_p)


def _inproj_bwd(dxo, x, gain, dsecs, dbd, w_in_p):
    S, D = x.shape
    tm = 512
    W = ATTN_WIDTH

    def body(dxo_ref, x_ref, gain_ref, s0, s1, s2, s3, s4, s5, s6, dbd_ref, w_ref, dx_ref, dgain_ref, dproj_ref):
        @pl.when(pl.program_id(0) == 0)
        def _():
            dgain_ref[...] = jnp.zeros_like(dgain_ref)

        dh = jnp.zeros((tm, D), F32)
        for k, s in enumerate((s0, s1, s2, s3, s4, s5, s6)):
            d = s[...].astype(BF16)
            dproj_ref[:, k * W:(k + 1) * W] = d
            dh += _nt(d, w_ref[:, k * W:(k + 1) * W])
        d = dbd_ref[...].astype(BF16)
        dproj_ref[:, 7 * W:7 * W + 128] = d
        dh += _nt(d, w_ref[:, 7 * W:7 * W + 128])
        dx, dgain = _rmsnorm_bwd(dh, x_ref[...], gain_ref[...])
        dx_ref[...] = dxo_ref[...] + dx
        dgain_ref[...] += dgain

    tok = lambda w: pl.BlockSpec((tm, w), lambda i: (i, 0))
    return pl.pallas_call(
        body, name="inproj_bwd", grid=(S // tm,),
        in_specs=[tok(D), tok(D), pl.BlockSpec((1, D), lambda i: (0, 0))] + [tok(W)] * 7 + [tok(128)]
                 + [pl.BlockSpec((D, IN_COLS_PADDED), lambda i: (0, 0))],
        out_specs=[tok(D), pl.BlockSpec((1, D), lambda i: (0, 0)), tok(IN_COLS_PADDED)],
        out_shape=[jax.ShapeDtypeStruct((S, D), F32), jax.ShapeDtypeStruct((1, D), F32),
                   jax.ShapeDtypeStruct((S, IN_COLS_PADDED), BF16)],
        compiler_params=_params(1),
    )(dxo, x, gain, *dsecs, dbd, w_in_p)


def _slope(h):
    return 2.0 ** (-8.0 * (h + 1) / ATTN_HEADS)


def _attn_fwd(q, k, v, d, name):
    S = q.shape[0]
    L = S // d
    nb = L // ATTN_BLOCK
    B = ATTN_BLOCK
    view = lambda t: t.reshape(L, d * ATTN_WIDTH)

    def body(q_ref, kp_ref, kc_ref, vp_ref, vc_ref, acc_ref, m_ref, l_ref):
        n = pl.program_id(1)
        qi = lax.broadcasted_iota(jnp.int32, (B, 2 * B), 0)
        kj = lax.broadcasted_iota(jnp.int32, (B, 2 * B), 1)
        steps = qi + B - kj
        valid = (steps >= 0) & (steps <= B) & ((kj >= B) | (n > 0))
        stepsf = steps.astype(F32)
        lo = lax.broadcasted_iota(jnp.int32, (B, 128), 1) < 64
        for G in range(4):
            sl = slice(G * 128, (G + 1) * 128)
            qg = q_ref[:, sl]
            kg = jnp.concatenate([kp_ref[:, sl], kc_ref[:, sl]], axis=0)
            vg = jnp.concatenate([vp_ref[:, sl], vc_ref[:, sl]], axis=0)
            res = []
            for half in (0, 1):
                msk = lo if half == 0 else jnp.logical_not(lo)
                qm = jnp.where(msk, qg, jnp.zeros_like(qg))
                s = _nt(qm, kg)
                s = jnp.where(valid, s - (_slope(2 * G + half) * d) * stepsf, NEG_BIG)
                m = jnp.max(s, axis=-1, keepdims=True)
                p = jnp.exp(s - m)
                l = jnp.sum(p, axis=-1, keepdims=True)
                a = _nn(p.astype(BF16), vg)
                res.append((a, m, l))
            (a0, m0, l0), (a1, m1, l1) = res
            acc_ref[:, sl] = jnp.where(lo, a0, a1)
            m_ref[:, sl] = jnp.where(lo, m0, m1)
            l_ref[:, sl] = jnp.where(lo, l0, l1)

    cur = pl.BlockSpec((B, ATTN_WIDTH), lambda r, n: (n, r))
    prev = pl.BlockSpec((B, ATTN_WIDTH), lambda r, n: (jnp.maximum(n - 1, 0), r))
    outs = pl.pallas_call(
        body, name=name, grid=(d, nb),
        in_specs=[cur, prev, cur, prev, cur],
        out_specs=[cur, cur, cur],
        out_shape=[jax.ShapeDtypeStruct((L, d * ATTN_WIDTH), F32)] * 3,
        compiler_params=_params(2),
    )(view(q), view(k), view(k), view(v), view(v))
    return [o.reshape(S, ATTN_WIDTH) for o in outs]


def _attn_merge(parts):
    S = parts[0][0].shape[0]
    tm = 512

    def body(a1, m1, l1, a2, m2, l2, a3, m3, l3, o_ref, lse_ref):
        ms = [m1[...], m2[...], m3[...]]
        mx = jnp.maximum(jnp.maximum(ms[0], ms[1]), ms[2])
        es = [jnp.exp(m - mx) for m in ms]
        den = es[0] * l1[...] + es[1] * l2[...] + es[2] * l3[...]
        num = es[0] * a1[...] + es[1] * a2[...] + es[2] * a3[...]
        o_ref[...] = num / den
        lse_ref[...] = mx + jnp.log(den)

    tok = pl.BlockSpec((tm, ATTN_WIDTH), lambda i: (i, 0))
    flat = [t for p in parts for t in p]
    return pl.pallas_call(
        body, name="attn_merge", grid=(S // tm,), in_specs=[tok] * 9, out_specs=[tok, tok],
        out_shape=[jax.ShapeDtypeStruct((S, ATTN_WIDTH), F32)] * 2, compiler_params=_params(1),
    )(*flat)


def _attn_delta(do, o):
    S = o.shape[0]
    tm = 512

    def body(do_ref, o_ref, dd_ref):
        lo = lax.broadcasted_iota(jnp.int32, (tm, 128), 1) < 64
        for G in range(4):
            sl = slice(G * 128, (G + 1) * 128)
            t = do_ref[:, sl] * o_ref[:, sl]
            d0 = jnp.sum(jnp.where(lo, t, 0.0), axis=-1, keepdims=True)
            d1 = jnp.sum(jnp.where(lo, 0.0, t), axis=-1, keepdims=True)
            dd_ref[:, sl] = jnp.where(lo, d0, d1)

    tok = pl.BlockSpec((tm, ATTN_WIDTH), lambda i: (i, 0))
    return pl.pallas_call(
        body, name="attn_delta", grid=(S // tm,), in_specs=[tok, tok], out_specs=tok,
        out_shape=jax.ShapeDtypeStruct((S, ATTN_WIDTH), F32), compiler_params=_params(1),
    )(do, o)


def _head_col(t, msk, big):
    if big:
        return jnp.max(jnp.where(msk, t, NEG_BIG), axis=-1, keepdims=True)
    return jnp.sum(jnp.where(msk, t, 0.0), axis=-1, keepdims=True) * (1.0 / 64.0)


def _attn_bwd_q(q, k, v, do, lse, dd, dq_run, d, name):
    S = q.shape[0]
    L = S // d
    nb = L // ATTN_BLOCK
    B = ATTN_BLOCK
    view = lambda t: t.reshape(L, d * ATTN_WIDTH)
    has_run = dq_run is not None

    def body(q_ref, kp_ref, kc_ref, vp_ref, vc_ref, do_ref, lse_ref, dd_ref, *rest):
        dq_ref = rest[-1]
        n = pl.program_id(1)
        qi = lax.broadcasted_iota(jnp.int32, (B, 2 * B), 0)
        kj = lax.broadcasted_iota(jnp.int32, (B, 2 * B), 1)
        steps = qi + B - kj
        valid = (steps >= 0) & (steps <= B) & ((kj >= B) | (n > 0))
        stepsf = steps.astype(F32)
        lo = lax.broadcasted_iota(jnp.int32, (B, 128), 1) < 64
        for G in range(4):
            sl = slice(G * 128, (G + 1) * 128)
            qg = q_ref[:, sl]
            kg = jnp.concatenate([kp_ref[:, sl], kc_ref[:, sl]], axis=0)
            vg = jnp.concatenate([vp_ref[:, sl], vc_ref[:, sl]], axis=0)
            dog = do_ref[:, sl]
            res = []
            for half in (0, 1):
                msk = lo if half == 0 else jnp.logical_not(lo)
                qm = jnp.where(msk, qg, jnp.zeros_like(qg))
                s = _nt(qm, kg) - (_slope(2 * G + half) * d) * stepsf
                lse_c = _head_col(lse_ref[:, sl], msk, True)
                p = jnp.where(valid, jnp.exp(jnp.where(valid, s, NEG_BIG) - lse_c), 0.0)
                dom = jnp.where(msk, dog, 0.0).astype(BF16)
                dp = _nt(dom, vg)
                dcol = _head_col(dd_ref[:, sl], msk, False)
                ds = p * (dp - dcol)
                res.append(_nn(ds.astype(BF16), kg) * 0.125)
            dq = jnp.where(lo, res[0], res[1])
            if has_run:
                dq = dq + rest[0][:, sl]
            dq_ref[:, sl] = dq

    cur = pl.BlockSpec((B, ATTN_WIDTH), lambda r, n: (n, r))
    prev = pl.BlockSpec((B, ATTN_WIDTH), lambda r, n: (jnp.maximum(n - 1, 0), r))
    args = [view(q), view(k), view(k), view(v), view(v), view(do), view(lse), view(dd)]
    specs = [cur, prev, cur, prev, cur, cur, cur, cur]
    if has_run:
        args.append(view(dq_run))
        specs.append(cur)
    out = pl.pallas_call(
        body, name=name, grid=(d, nb), in_specs=specs, out_specs=cur,
        out_shape=jax.ShapeDtypeStruct((L, d * ATTN_WIDTH), F32), compiler_params=_params(2),
    )(*args)
    return out.reshape(S, ATTN_WIDTH)


def _attn_bwd_kv(q, k, v, do, lse, dd, dk_run, dv_run, d, name):
    S = q.shape[0]
    L = S // d
    nb = L // ATTN_BLOCK
    B = ATTN_BLOCK
    view = lambda t: t.reshape(L, d * ATTN_WIDTH)
    has_run = dk_run is not None

    def body(k_ref, v_ref, qc_ref, qn_ref, doc_ref, don_ref, lsec_ref, lsen_ref, ddc_ref, ddn_ref, *rest):
        dk_ref, dv_ref = rest[-2], rest[-1]
        j = pl.program_id(1)
        qrow = lax.broadcasted_iota(jnp.int32, (2 * B, B), 0)
        kk = lax.broadcasted_iota(jnp.int32, (2 * B, B), 1)
        steps = qrow - kk
        valid = (steps >= 0) & (steps <= B) & ((qrow < B) | (j < nb - 1))
        stepsf = steps.astype(F32)
        lo2 = lax.broadcasted_iota(jnp.int32, (2 * B, 128), 1) < 64
        lo = lax.broadcasted_iota(jnp.int32, (B, 128), 1) < 64
        for G in range(4):
            sl = slice(G * 128, (G + 1) * 128)
            kg = k_ref[:, sl]
            vg = v_ref[:, sl]
            qq = jnp.concatenate([qc_ref[:, sl], qn_ref[:, sl]], axis=0)
            doo = jnp.concatenate([doc_ref[:, sl], don_ref[:, sl]], axis=0)
            lse2 = jnp.concatenate([lsec_ref[:, sl], lsen_ref[:, sl]], axis=0)
            dd2 = jnp.concatenate([ddc_ref[:, sl], ddn_ref[:, sl]], axis=0)
            doo_b = doo.astype(BF16)
            dks, dvs = [], []
            for half in (0, 1):
                msk = lo2 if half == 0 else jnp.logical_not(lo2)
                qm = jnp.where(msk, qq, jnp.zeros_like(qq))
                s = _nt(qm, kg) - (_slope(2 * G + half) * d) * stepsf
                lse_c = _head_col(lse2, msk, True)
                p = jnp.where(valid, jnp.exp(jnp.where(valid, s, NEG_BIG) - lse_c), 0.0)
                dvs.append(_tn(p.astype(BF16), doo_b))
                dom = jnp.where(msk, doo, 0.0).astype(BF16)
                dp = _nt(dom, vg)
                dcol = _head_col(dd2, msk, False)
                ds = p * (dp - dcol)
                dks.append(_tn(ds.astype(BF16), qq))
            dk = jnp.where(lo, dks[0], dks[1])
            dv = jnp.where(lo, dvs[0], dvs[1])
            if has_run:
                dk = dk + rest[0][:, sl]
                dv = dv + rest[1][:, sl]
            dk_ref[:, sl] = dk
            dv_ref[:, sl] = dv

    cur = pl.BlockSpec((B, ATTN_WIDTH), lambda r, j: (j, r))
    nxt = pl.BlockSpec((B, ATTN_WIDTH), lambda r, j: (jnp.minimum(j + 1, nb - 1), r))
    args = [view(k), view(v), view(q), view(q), view(do), view(do), view(lse), view(lse), view(dd), view(dd)]
    specs = [cur, cur, cur, nxt, cur, nxt, cur, nxt, cur, nxt]
    if has_run:
        args += [view(dk_run), view(dv_run)]
        specs += [cur, cur]
    outs = pl.pallas_call(
        body, name=name, grid=(d, nb), in_specs=specs, out_specs=[cur, cur],
        out_shape=[jax.ShapeDtypeStruct((L, d * ATTN_WIDTH), F32)] * 2, compiler_params=_params(2),
    )(*args)
    return [o.reshape(S, ATTN_WIDTH) for o in outs]


CONV_T = 512
HALO = 8


def _conv_taps(pad_ref, w, T):
    acc = pad_ref[pl.ds(HALO - 3, T), :] * w[0:1, :]
    for j in range(1, CONV_WIDTH):
        acc = acc + pad_ref[pl.ds(HALO - 3 + j, T), :] * w[j:j + 1, :]
    return acc


def _conv_fwd(xq, xk, xv, conv_w):
    S = xq.shape[0]
    T = CONV_T

    def body(xq_ref, xqh_ref, xk_ref, xkh_ref, xv_ref, xvh_ref, wq_ref, wk_ref, wv_ref,
             qn_ref, kn_ref, v_ref, pad_ref):
        i = pl.program_id(0)

        def act(x_ref, xh_ref, w_ref):
            pad_ref[pl.ds(0, HALO), :] = jnp.where(i > 0, xh_ref[...], 0.0)
            pad_ref[pl.ds(HALO, T), :] = x_ref[...]
            c = _conv_taps(pad_ref, w_ref[...], T)
            return c * _sigmoid(c)

        def l2n(t):
            return t * lax.rsqrt(jnp.sum(t * t, axis=-1, keepdims=True) + L2_EPS)

        qn_ref[...] = l2n(act(xq_ref, xqh_ref, wq_ref))
        kn_ref[...] = l2n(act(xk_ref, xkh_ref, wk_ref))
        v_ref[...] = act(xv_ref, xvh_ref, wv_ref)

    tile = pl.BlockSpec((T, 128), lambda i, h: (i, h))
    halo = pl.BlockSpec((HALO, 128), lambda i, h: (jnp.maximum(i * (T // HALO) - 1, 0), h))
    wspec = lambda sec: pl.BlockSpec((CONV_WIDTH, 128), lambda i, h, sec=sec: (0, 4 * sec + h))
    return pl.pallas_call(
        body, name="dn_conv_fwd", grid=(S // T, DN_HEADS),
        in_specs=[tile, halo, tile, halo, tile, halo, wspec(0), wspec(1), wspec(2)],
        out_specs=[tile, tile, tile],
        out_shape=[jax.ShapeDtypeStruct((S, DN_WIDTH), F32)] * 3,
        scratch_shapes=[pltpu.VMEM((T + HALO, 128), F32)],
        compiler_params=_params(2),
    )(xq, xq, xk, xk, xv, xv, conv_w, conv_w, conv_w)


def _conv_bwd_pre(xq, xk, xv, conv_w, dqn, dkn, dv):
    S = xq.shape[0]
    T = CONV_T

    def body(xq_ref, xqh_ref, xk_ref, xkh_ref, xv_ref, xvh_ref, wq_ref, wk_ref, wv_ref,
             dqn_ref, dkn_ref, dv_ref, dcq_ref, dck_ref, dcv_ref, dwq_ref, dwk_ref, dwv_ref, pad_ref):
        i = pl.program_id(1)

        def one(x_ref, xh_ref, w_ref, dy_ref, dc_ref, dw_ref, normed):
            pad_ref[pl.ds(0, HALO), :] = jnp.where(i > 0, xh_ref[...], 0.0)
            pad_ref[pl.ds(HALO, T), :] = x_ref[...]
            c = _conv_taps(pad_ref, w_ref[...], T)
            sg = _sigmoid(c)
            a = c * sg
            dy = dy_ref[...]
            if normed:
                r = lax.rsqrt(jnp.sum(a * a, axis=-1, keepdims=True) + L2_EPS)
                y = a * r
                da = r * (dy - y * jnp.sum(dy * y, axis=-1, keepdims=True))
            else:
                da = dy
            dc = da * (sg * (1.0 + c * (1.0 - sg)))
            dc_ref[...] = dc

            @pl.when(i == 0)
            def _():
                dw_ref[...] = jnp.zeros_like(dw_ref)

            rows = [jnp.sum(dc * pad_ref[pl.ds(HALO - 3 + j, T), :], axis=0, keepdims=True) for j in range(CONV_WIDTH)]
            dw_ref[...] += jnp.concatenate(rows + [jnp.zeros((8 - CONV_WIDTH, 128), F32)], axis=0)

        one(xq_ref, xqh_ref, wq_ref, dqn_ref, dcq_ref, dwq_ref, True)
        one(xk_ref, xkh_ref, wk_ref, dkn_ref, dck_ref, dwk_ref, True)
        one(xv_ref, xvh_ref, wv_ref, dv_ref, dcv_ref, dwv_ref, False)

    tile = pl.BlockSpec((T, 128), lambda h, i: (i, h))
    halo = pl.BlockSpec((HALO, 128), lambda h, i: (jnp.maximum(i * (T // HALO) - 1, 0), h))
    wspec = lambda sec: pl.BlockSpec((CONV_WIDTH, 128), lambda h, i, sec=sec: (0, 4 * sec + h))
    dwspec = pl.BlockSpec((8, 128), lambda h, i: (0, h))
    return pl.pallas_call(
        body, name="dn_conv_bwd_pre", grid=(DN_HEADS, S // T),
        in_specs=[tile, halo, tile, halo, tile, halo, wspec(0), wspec(1), wspec(2), tile, tile, tile],
        out_specs=[tile, tile, tile, dwspec, dwspec, dwspec],
        out_shape=[jax.ShapeDtypeStruct((S, DN_WIDTH), F32)] * 3 + [jax.ShapeDtypeStruct((8, DN_WIDTH), F32)] * 3,
        scratch_shapes=[pltpu.VMEM((T + HALO, 128), F32)],
        compiler_params=_params(2),
    )(xq, xq, xk, xk, xv, xv, conv_w, conv_w, conv_w, dqn, dkn, dv)


def _conv_bwd_x(dcq, dck, dcv, conv_w):
    S = dcq.shape[0]
    T = CONV_T
    nt = S // T

    def body(dq_ref, dqh_ref, dk_ref, dkh_ref, dv_ref, dvh_ref, wq_ref, wk_ref, wv_ref,
             oq_ref, ok_ref, ov_ref, pad_ref):
        i = pl.program_id(0)

        def one(d_ref, dh_ref, w_ref, o_ref):
            pad_ref[pl.ds(0, T), :] = d_ref[...]
            pad_ref[pl.ds(T, HALO), :] = jnp.where(i < nt - 1, dh_ref[...], 0.0)
            w = w_ref[...]
            acc = pad_ref[pl.ds(3, T), :] * w[0:1, :]
            for j in range(1, CONV_WIDTH):
                acc = acc + pad_ref[pl.ds(3 - j, T), :] * w[j:j + 1, :]
            o_ref[...] = acc

        one(dq_ref, dqh_ref, wq_ref, oq_ref)
        one(dk_ref, dkh_ref, wk_ref, ok_ref)
        one(dv_ref, dvh_ref, wv_ref, ov_ref)

    tile = pl.BlockSpec((T, 128), lambda i, h: (i, h))
    halo = pl.BlockSpec((HALO, 128), lambda i, h: (jnp.minimum((i + 1) * (T // HALO), S // HALO - 1), h))
    wspec = lambda sec: pl.BlockSpec((CONV_WIDTH, 128), lambda i, h, sec=sec: (0, 4 * sec + h))
    return pl.pallas_call(
        body, name="dn_conv_bwd_x", grid=(nt, DN_HEADS),
        in_specs=[tile, halo, tile, halo, tile, halo, wspec(0), wspec(1), wspec(2)],
        out_specs=[tile, tile, tile],
        out_shape=[jax.ShapeDtypeStruct((S, DN_WIDTH), F32)] * 3,
        scratch_shapes=[pltpu.VMEM((T + HALO, 128), F32)],
        compiler_params=_params(2),
    )(dcq, dcq, dck, dck, dcv, dcv, conv_w, conv_w, conv_w)


def _tri_inverse(a, blk, eye):
    mm = functools.partial(_nn, precision=HI)
    dg = jnp.where(blk, a, 0.0)
    lo = a - dg
    d2 = mm(dg, dg)
    d4 = mm(d2, d2)
    d8 = mm(d4, d4)
    td = mm(mm(mm(eye - dg, eye + d2), eye + d4), eye + d8)
    b = mm(td, lo)
    b2 = mm(b, b)
    tb = mm(eye - b, eye + b2)
    return mm(tb, td)


def _dn_chunk_common(bd, avec, dvec, h, q_raw, k, v, t=None):
    C = DN_CHUNK
    lane = lax.broadcasted_iota(jnp.int32, (C, 128), 1)
    row = lax.broadcasted_iota(jnp.int32, (C, C), 0)
    col = lax.broadcasted_iota(jnp.int32, (C, C), 1)
    incl = row >= col
    strict = row > col
    eye = (row == col).astype(F32)
    blk = (row // 16) == (col // 16)
    beta_all = _sigmoid(bd)
    z = bd + dvec
    sp = jnp.maximum(z, 0.0) + jnp.log(1.0 + jnp.exp(-jnp.abs(z)))
    g_all = -jnp.exp(avec) * sp
    pick = lambda t, ln: jnp.sum(jnp.where(lane == ln, t, 0.0), axis=-1, keepdims=True)
    beta = pick(beta_all, h)
    graw = pick(g_all, DN_HEADS + h)
    zc = pick(z, DN_HEADS + h)
    to_row = lambda c: jnp.sum(eye * c, axis=0, keepdims=True)
    gc = jnp.sum(jnp.where(incl, to_row(graw), 0.0), axis=-1, keepdims=True)
    gc_row = to_row(gc)
    decay = jnp.exp(jnp.where(incl, gc - gc_row, NEG_BIG))
    q = q_raw * (DN_HEAD_DIM ** -0.5)
    kb = k * beta
    kk = _nt(kb, k, HI)
    if t is None:
        t = _tri_inverse(jnp.where(strict, kk * decay, 0.0), blk, eye)
    eg = jnp.exp(gc)
    rhs_u = v * beta
    rhs_w = kb * eg
    u = _nn(t, rhs_u, HI)
    w = _nn(t, rhs_w, HI)
    qk = _nt(q, k, HI)
    aq = jnp.where(incl, qk * decay, 0.0)
    g_last = jnp.sum(jnp.where(lax.broadcasted_iota(jnp.int32, (C, 1), 0) == C - 1, gc, 0.0), axis=0, keepdims=True)
    ekd = jnp.exp(g_last - gc)
    return dict(beta=beta, graw=graw, zc=zc, gc=gc, decay=decay, q=q, kb=kb, kk=kk, t=t, eg=eg, rhs_w=rhs_w,
                u=u, w=w, qk=qk, aq=aq, g_last=g_last, ekd=ekd, kd=k * ekd, qg=q * eg,
                incl=incl, strict=strict, eye=eye, lane=lane, row=row, col=col)


PREP_CHUNKS = 2
SCAN_CHUNKS = 8


def _dn_prep(qn, kn, v, bd, avec, dvec):
    S = qn.shape[0]
    C = DN_CHUNK
    N = S // C
    HD = DN_HEAD_DIM
    nc = PREP_CHUNKS

    def body(q_ref, k_ref, v_ref, bd_ref, a_ref, d_ref, u_ref, w_ref, qg_ref, kd_ref, aq_ref, t_ref, egl_ref):
        for ci in range(nc):
            rows = slice(ci * C, (ci + 1) * C)
            bd = bd_ref[rows, :]
            egl = []
            for h in range(DN_HEADS):
                sl = slice(h * HD, (h + 1) * HD)
                c = _dn_chunk_common(bd, a_ref[...], d_ref[...], h, q_ref[rows, sl], k_ref[rows, sl], v_ref[rows, sl])
                u_ref[rows, sl] = c["u"]
                w_ref[rows, sl] = c["w"]
                qg_ref[rows, sl] = c["qg"]
                kd_ref[rows, sl] = c["kd"]
                aq_ref[h, rows, :] = c["aq"]
                t_ref[h, rows, :] = c["t"]
                egl.append(jnp.broadcast_to(jnp.exp(c["g_last"]), (1, 128)))
            egl_ref[ci * 8:(ci + 1) * 8, :] = jnp.concatenate(egl + [jnp.zeros((8 - DN_HEADS, 128), F32)], axis=0)

    tok = lambda w: pl.BlockSpec((nc * C, w), lambda n: (n, 0))
    sq = pl.BlockSpec((DN_HEADS, nc * C, C), lambda n: (0, n, 0))
    vec = pl.BlockSpec((1, 128), lambda n: (0, 0))
    return pl.pallas_call(
        body, name="dn_prep", grid=(N // nc,),
        in_specs=[tok(DN_WIDTH)] * 3 + [tok(128), vec, vec],
        out_specs=[tok(DN_WIDTH)] * 4 + [sq, sq, pl.BlockSpec((nc * 8, 128), lambda n: (n, 0))],
        out_shape=[jax.ShapeDtypeStruct((S, DN_WIDTH), F32)] * 4 + [jax.ShapeDtypeStruct((DN_HEADS, S, C), F32)] * 2
                  + [jax.ShapeDtypeStruct((N * 8, 128), F32)],
        compiler_params=_params(1),
    )(qn, kn, v, bd, avec, dvec)


def _dn_scan_fwd(u, w, qg, kd, aq, egl, gate, dn_gain):
    S = u.shape[0]
    C = DN_CHUNK
    N = S // C
    HD = DN_HEAD_DIM
    nc = SCAN_CHUNKS

    def body(u_ref, w_ref, qg_ref, kd_ref, aq_ref, egl_ref, gate_ref, gain_ref, dn_ref, o_ref, vn_ref, st_ref, state_ref):
        @pl.when(pl.program_id(0) == 0)
        def _():
            state_ref[...] = jnp.zeros_like(state_ref)

        for ci in range(nc):
            rows = slice(ci * C, (ci + 1) * C)
            st_ref[ci * DN_WIDTH:(ci + 1) * DN_WIDTH, :] = state_ref[...]
            for h in range(DN_HEADS):
                sl = slice(h * HD, (h + 1) * HD)
                st = state_ref[sl, :]
                v_new = u_ref[rows, sl] - _nn(w_ref[rows, sl], st, HI)
                o = _nn(qg_ref[rows, sl], st, HI) + _nn(aq_ref[h, rows, :], v_new, HI)
                state_ref[sl, :] = st * egl_ref[ci * 8 + h:ci * 8 + h + 1, :] + _tn(kd_ref[rows, sl], v_new, HI)
                vn_ref[rows, sl] = v_new
                o_ref[rows, sl] = o
                r = lax.rsqrt(jnp.mean(o * o, axis=-1, keepdims=True) + NORM_EPS)
                gt = gate_ref[rows, sl]
                dn_ref[rows, sl] = o * r * gain_ref[...] * (gt * _sigmoid(gt))

    tok = lambda wd: pl.BlockSpec((nc * C, wd), lambda n: (n, 0))
    sq = pl.BlockSpec((DN_HEADS, nc * C, C), lambda n: (0, n, 0))
    vec = pl.BlockSpec((1, 128), lambda n: (0, 0))
    return pl.pallas_call(
        body, name="dn_scan_fwd", grid=(N // nc,),
        in_specs=[tok(DN_WIDTH)] * 4 + [sq, pl.BlockSpec((nc * 8, 128), lambda n: (n, 0)), tok(DN_WIDTH), vec],
        out_specs=[tok(DN_WIDTH)] * 3 + [pl.BlockSpec((nc * DN_WIDTH, HD), lambda n: (n, 0))],
        out_shape=[jax.ShapeDtypeStruct((S, DN_WIDTH), F32)] * 3 + [jax.ShapeDtypeStruct((N * DN_WIDTH, HD), F32)],
        scratch_shapes=[pltpu.VMEM((DN_WIDTH, HD), F32)],
        compiler_params=_params(1),
    )(u, w, qg, kd, aq, egl, gate, dn_gain)


def _dn_scan_bwd(w, qg, kd, aq, egl, gate, dn_gain, o, ddn):
    S = w.shape[0]
    C = DN_CHUNK
    N = S // C
    HD = DN_HEAD_DIM
    nc = SCAN_CHUNKS

    def body(w_ref, qg_ref, kd_ref, aq_ref, egl_ref, gate_ref, gain_ref, o_ref, ddn_ref,
             do_ref, dvn_ref, dgate_ref, dst_ref, small_ref, dstate_ref):
        @pl.when(pl.program_id(0) == 0)
        def _():
            dstate_ref[...] = jnp.zeros_like(dstate_ref)
            small_ref[...] = jnp.zeros_like(small_ref)

        gain = gain_ref[...]
        d_gain = jnp.zeros((1, 128), F32)
        for ci in reversed(range(nc)):
            rows = slice(ci * C, (ci + 1) * C)
            dst_ref[ci * DN_WIDTH:(ci + 1) * DN_WIDTH, :] = dstate_ref[...]
            for h in range(DN_HEADS):
                sl = slice(h * HD, (h + 1) * HD)
                ov = o_ref[rows, sl]
                r = lax.rsqrt(jnp.mean(ov * ov, axis=-1, keepdims=True) + NORM_EPS)
                on = ov * r
                gt = gate_ref[rows, sl]
                sgt = _sigmoid(gt)
                silu_g = gt * sgt
                dy = ddn_ref[rows, sl]
                d_gain = d_gain + jnp.sum(dy * on * silu_g, axis=0, keepdims=True)
                dgate_ref[rows, sl] = dy * on * gain * (sgt * (1.0 + gt * (1.0 - sgt)))
                don = dy * gain * silu_g
                do = r * (don - on * jnp.mean(don * on, axis=-1, keepdims=True))
                do_ref[rows, sl] = do
                dsn = dstate_ref[sl, :]
                d_vnew = _tn(aq_ref[h, rows, :], do, HI) + _nn(kd_ref[rows, sl], dsn, HI)
                dvn_ref[rows, sl] = d_vnew
                dstate_ref[sl, :] = (_tn(qg_ref[rows, sl], do, HI) + dsn * egl_ref[ci * 8 + h:ci * 8 + h + 1, :]
                                     - _tn(w_ref[rows, sl], d_vnew, HI))
        small_ref[...] += jnp.concatenate([d_gain, jnp.zeros((7, 128), F32)], axis=0)

    nb = N // nc
    tok = lambda wd: pl.BlockSpec((nc * C, wd), lambda i: (nb - 1 - i, 0))
    sq = pl.BlockSpec((DN_HEADS, nc * C, C), lambda i: (0, nb - 1 - i, 0))
    vec = pl.BlockSpec((1, 128), lambda i: (0, 0))
    return pl.pallas_call(
        body, name="dn_scan_bwd", grid=(nb,),
        in_specs=[tok(DN_WIDTH)] * 3 + [sq, pl.BlockSpec((nc * 8, 128), lambda i: (nb - 1 - i, 0)), tok(DN_WIDTH), vec,
                                       tok(DN_WIDTH), tok(DN_WIDTH)],
        out_specs=[tok(DN_WIDTH)] * 3 + [pl.BlockSpec((nc * DN_WIDTH, HD), lambda i: (nb - 1 - i, 0)),
                                        pl.BlockSpec((8, 128), lambda i: (0, 0))],
        out_shape=[jax.ShapeDtypeStruct((S, DN_WIDTH), F32)] * 3 + [jax.ShapeDtypeStruct((N * DN_WIDTH, HD), F32),
                                                                  jax.ShapeDtypeStruct((8, 128), F32)],
        scratch_shapes=[pltpu.VMEM((DN_WIDTH, HD), F32)],
        compiler_params=_params(1),
    )(w, qg, kd, aq, egl, gate, dn_gain, o, ddn)


def _dn_post(qn, kn, v, bd, avec, dvec, t_inv, v_new_all, states, dstates, do_all, dvn_all):
    S = qn.shape[0]
    C = DN_CHUNK
    N = S // C
    HD = DN_HEAD_DIM
    nc = PREP_CHUNKS

    def body(q_ref, k_ref, v_ref, bd_ref, a_ref, d_ref, t_ref, vn_ref, st_ref, dst_ref, do_ref, dvn_ref,
             dq_ref, dk_ref, dv_ref, dbd_ref, small_ref):
        @pl.when(pl.program_id(0) == 0)
        def _():
            small_ref[...] = jnp.zeros_like(small_ref)

        for ci in range(nc):
            tok = lambda r: r.at[pl.ds(ci * C, C)]
            big = lambda r: r.at[pl.ds(ci * DN_WIDTH, DN_WIDTH)]
            chunk(tok(q_ref), tok(k_ref), tok(v_ref), tok(bd_ref), a_ref, d_ref, t_ref.at[:, pl.ds(ci * C, C)], tok(vn_ref),
                  big(st_ref), big(dst_ref), tok(do_ref), tok(dvn_ref), tok(dq_ref), tok(dk_ref), tok(dv_ref), tok(dbd_ref),
                  small_ref)

    def chunk(q_ref, k_ref, v_ref, bd_ref, a_ref, d_ref, t_ref, vn_ref, st_ref, dstate_ref, do_ref, dvn_ref,
              dq_ref, dk_ref, dv_ref, dbd_ref, small_ref):
        bd = bd_ref[...]
        avec = a_ref[...]
        dbd = jnp.zeros((C, 128), F32)
        d_alog = jnp.zeros((1, 128), F32)
        d_dt = jnp.zeros((1, 128), F32)
        for h in range(DN_HEADS):
            sl = slice(h * HD, (h + 1) * HD)
            k = k_ref[:, sl]
            vv = v_ref[:, sl]
            c = _dn_chunk_common(bd, avec, d_ref[...], h, q_ref[:, sl], k, vv, t=t_ref[h])
            q, kb, t, eg, u, w = c["q"], c["kb"], c["t"], c["eg"], c["u"], c["w"]
            beta, decay, incl, strict, eye = c["beta"], c["decay"], c["incl"], c["strict"], c["eye"]
            lane = c["lane"]
            st = st_ref[sl, :]
            dsn = dstate_ref[sl, :]
            v_new = vn_ref[:, sl]
            do = do_ref[:, sl]
            d_vnew = dvn_ref[:, sl]
            egl = jnp.exp(c["g_last"])
            daq = jnp.where(incl, _nt(do, v_new, HI), 0.0)
            d_qg = _nt(do, st, HI)
            d_kd = _nt(v_new, dsn, HI)
            d_glast = jnp.sum(jnp.sum(dsn * st, axis=-1, keepdims=True), axis=0, keepdims=True) * egl
            d_w = -_nt(d_vnew, st, HI)
            d_ru = _tn(t, d_vnew, HI)
            d_rw = _tn(t, d_w, HI)
            da = -jnp.where(strict, _nt(d_ru, u, HI) + _nt(d_rw, w, HI), 0.0)
            dv_ref[:, sl] = d_ru * beta
            dbeta = jnp.sum(d_ru * vv, axis=-1, keepdims=True)
            dkb = d_rw * eg
            dgc = jnp.sum(d_rw * c["rhs_w"], axis=-1, keepdims=True)
            dkk = da * decay
            ddecay = da * c["kk"]
            dkb = dkb + _nn(dkk, k, HI)
            dk = _tn(dkk, kb, HI)
            dqk = daq * decay
            ddecay = ddecay + daq * c["qk"]
            dq = _nn(dqk, k, HI)
            dk = dk + _tn(dqk, q, HI)
            m = ddecay * decay
            col_sum = jnp.sum(m, axis=0, keepdims=True)
            dgc = dgc + jnp.sum(m, axis=-1, keepdims=True) - jnp.sum(eye * col_sum, axis=-1, keepdims=True)
            dq = dq + d_qg * eg
            dgc = dgc + jnp.sum(d_qg * c["qg"], axis=-1, keepdims=True)
            dk = dk + d_kd * c["ekd"]
            tk = jnp.sum(d_kd * c["kd"], axis=-1, keepdims=True)
            dgc = dgc - tk
            d_glast = d_glast + jnp.sum(tk, axis=0, keepdims=True)
            dk = dk + dkb * beta
            dbeta = dbeta + jnp.sum(dkb * k, axis=-1, keepdims=True)
            dgc = dgc + jnp.where(lax.broadcasted_iota(jnp.int32, (C, 1), 0) == C - 1, d_glast, 0.0)
            dgc_row = jnp.sum(eye * dgc, axis=0, keepdims=True)
            dgraw = jnp.sum(jnp.where(c["col"] >= c["row"], dgc_row, 0.0), axis=-1, keepdims=True)
            dq_ref[:, sl] = dq * (HD ** -0.5)
            dk_ref[:, sl] = dk
            dbraw = dbeta * beta * (1.0 - beta)
            dz = dgraw * (-jnp.exp(avec)) * _sigmoid(c["zc"])
            dbd = dbd + jnp.where(lane == h, dbraw, 0.0) + jnp.where(lane == DN_HEADS + h, dz, 0.0)
            lane1 = lax.broadcasted_iota(jnp.int32, (1, 128), 1)
            d_alog = d_alog + jnp.where(lane1 == DN_HEADS + h, jnp.sum(dgraw * c["graw"], axis=0, keepdims=True), 0.0)
            d_dt = d_dt + jnp.where(lane1 == DN_HEADS + h, jnp.sum(dz, axis=0, keepdims=True), 0.0)
        dbd_ref[...] = dbd
        small_ref[...] += jnp.concatenate([d_alog, d_dt, jnp.zeros((6, 128), F32)], axis=0)

    tok = lambda wd: pl.BlockSpec((nc * C, wd), lambda n: (n, 0))
    big = pl.BlockSpec((nc * DN_WIDTH, HD), lambda n: (n, 0))
    sq = pl.BlockSpec((DN_HEADS, nc * C, C), lambda n: (0, n, 0))
    vec = pl.BlockSpec((1, 128), lambda n: (0, 0))
    return pl.pallas_call(
        body, name="dn_post", grid=(N // nc,),
        in_specs=[tok(DN_WIDTH)] * 3 + [tok(128), vec, vec, sq, tok(DN_WIDTH), big, big, tok(DN_WIDTH), tok(DN_WIDTH)],
        out_specs=[tok(DN_WIDTH)] * 3 + [tok(128), pl.BlockSpec((8, 128), lambda n: (0, 0))],
        out_shape=[jax.ShapeDtypeStruct((S, DN_WIDTH), F32)] * 3 + [jax.ShapeDtypeStruct((S, 128), F32),
                                                                  jax.ShapeDtypeStruct((8, 128), F32)],
        compiler_params=_params(1),
    )(qn, kn, v, bd, avec, dvec, t_inv, v_new_all, states, dstates, do_all, dvn_all)


def _bnn(a, b):
    return lax.dot_general(a, b, (((2,), (1,)), ((0,), (0,))), preferred_element_type=F32, precision=HI)


def _bnt(a, b):
    return lax.dot_general(a, b, (((2,), (2,)), ((0,), (0,))), preferred_element_type=F32, precision=HI)


def _btn(a, b):
    return lax.dot_general(a, b, (((1,), (1,)), ((0,), (0,))), preferred_element_type=F32, precision=HI)


def _tri_inverse_b(a, blk, eye):
    dg = jnp.where(blk, a, 0.0)
    lo = a - dg
    d2 = _bnn(dg, dg)
    d4 = _bnn(d2, d2)
    d8 = _bnn(d4, d4)
    td = _bnn(_bnn(_bnn(eye - dg, eye + d2), eye + d4), eye + d8)
    b = _bnn(td, lo)
    b2 = _bnn(b, b)
    return _bnn(_bnn(eye - b, eye + b2), td)


def _dn_common_b(bds, avec, dvec, q_raw, k, v, t=None):
    C = DN_CHUNK
    lane = lax.broadcasted_iota(jnp.int32, (C, 128), 1)
    row = lax.broadcasted_iota(jnp.int32, (1, C, C), 1)
    col = lax.broadcasted_iota(jnp.int32, (1, C, C), 2)
    incl = row >= col
    strict = row > col
    eye = (row == col).astype(F32)
    blk = (row // 16) == (col // 16)
    pick = lambda tile, ln: jnp.sum(jnp.where(lane == ln, tile, 0.0), axis=-1, keepdims=True)
    betas, graws, zcs = [], [], []
    for bd in bds:
        z = bd + dvec
        g_all = -jnp.exp(avec) * (jnp.maximum(z, 0.0) + jnp.log(1.0 + jnp.exp(-jnp.abs(z))))
        beta_all = _sigmoid(bd)
        for h in range(DN_HEADS):
            betas.append(pick(beta_all, h))
            graws.append(pick(g_all, DN_HEADS + h))
            zcs.append(pick(z, DN_HEADS + h))
    beta, graw, zc = jnp.stack(betas), jnp.stack(graws), jnp.stack(zcs)
    to_row = lambda c: jnp.sum(eye * c, axis=1, keepdims=True)
    gc = jnp.sum(jnp.where(incl, to_row(graw), 0.0), axis=-1, keepdims=True)
    decay = jnp.exp(jnp.where(incl, gc - to_row(gc), NEG_BIG))
    q = q_raw * (DN_HEAD_DIM ** -0.5)
    kb = k * beta
    kk = _bnt(kb, k)
    if t is None:
        t = _tri_inverse_b(jnp.where(strict, kk * decay, 0.0), blk, eye)
    eg = jnp.exp(gc)
    rhs_w = kb * eg
    u = _bnn(t, v * beta)
    w = _bnn(t, rhs_w)
    qk = _bnt(q, k)
    aq = jnp.where(incl, qk * decay, 0.0)
    last = lax.broadcasted_iota(jnp.int32, (1, C, 1), 1) == C - 1
    g_last = jnp.sum(jnp.where(last, gc, 0.0), axis=1, keepdims=True)
    ekd = jnp.exp(g_last - gc)
    return dict(beta=beta, graw=graw, zc=zc, gc=gc, decay=decay, q=q, kb=kb, kk=kk, t=t, eg=eg, rhs_w=rhs_w,
                u=u, w=w, qk=qk, aq=aq, g_last=g_last, ekd=ekd, kd=k * ekd, qg=q * eg,
                incl=incl, strict=strict, eye=eye, lane=lane, row=row, col=col, last=last)


def _stack_heads(ref, rows):
    return jnp.stack([ref[rows, h * DN_HEAD_DIM:(h + 1) * DN_HEAD_DIM] for h in range(DN_HEADS)])


def _stack_units(ref, nc):
    C = DN_CHUNK
    return jnp.concatenate([_stack_heads(ref, slice(ci * C, (ci + 1) * C)) for ci in range(nc)], axis=0)


def _store_units(ref, val, nc):
    C = DN_CHUNK
    for ci in range(nc):
        for h in range(DN_HEADS):
            ref[ci * C:(ci + 1) * C, h * DN_HEAD_DIM:(h + 1) * DN_HEAD_DIM] = val[ci * DN_HEADS + h]


def _dn_prep(qn, kn, v, bd, avec, dvec):
    S = qn.shape[0]
    C = DN_CHUNK
    N = S // C
    nc = PREP_CHUNKS

    def body(q_ref, k_ref, v_ref, bd_ref, a_ref, d_ref, u_ref, w_ref, qg_ref, kd_ref, aq_ref, t_ref, egl_ref):
        bds = [bd_ref[ci * C:(ci + 1) * C, :] for ci in range(nc)]
        c = _dn_common_b(bds, a_ref[...], d_ref[...], _stack_units(q_ref, nc), _stack_units(k_ref, nc), _stack_units(v_ref, nc))
        _store_units(u_ref, c["u"], nc)
        _store_units(w_ref, c["w"], nc)
        _store_units(qg_ref, c["qg"], nc)
        _store_units(kd_ref, c["kd"], nc)
        egl = jnp.broadcast_to(jnp.exp(c["g_last"]), (nc * DN_HEADS, 1, 128))
        for ci in range(nc):
            for h in range(DN_HEADS):
                aq_ref[h, ci * C:(ci + 1) * C, :] = c["aq"][ci * DN_HEADS + h]
                t_ref[h, ci * C:(ci + 1) * C, :] = c["t"][ci * DN_HEADS + h]
            egl_ref[ci * 8:(ci + 1) * 8, :] = jnp.concatenate(
                [egl[ci * DN_HEADS + h] for h in range(DN_HEADS)] + [jnp.zeros((8 - DN_HEADS, 128), F32)], axis=0)

    tok = lambda w: pl.BlockSpec((nc * C, w), lambda n: (n, 0))
    sq = pl.BlockSpec((DN_HEADS, nc * C, C), lambda n: (0, n, 0))
    vec = pl.BlockSpec((1, 128), lambda n: (0, 0))
    return pl.pallas_call(
        body, name="dn_prep", grid=(N // nc,),
        in_specs=[tok(DN_WIDTH)] * 3 + [tok(128), vec, vec],
        out_specs=[tok(DN_WIDTH)] * 4 + [sq, sq, pl.BlockSpec((nc * 8, 128), lambda n: (n, 0))],
        out_shape=[jax.ShapeDtypeStruct((S, DN_WIDTH), F32)] * 4 + [jax.ShapeDtypeStruct((DN_HEADS, S, C), F32)] * 2
                  + [jax.ShapeDtypeStruct((N * 8, 128), F32)],
        compiler_params=_params(1),
    )(qn, kn, v, bd, avec, dvec)


def _dn_scan_fwd(u, w, qg, kd, aq, egl, gate, dn_gain):
    S = u.shape[0]
    C = DN_CHUNK
    N = S // C
    HD = DN_HEAD_DIM
    nc = SCAN_CHUNKS

    def body(u_ref, w_ref, qg_ref, kd_ref, aq_ref, egl_ref, gate_ref, gain_ref, dn_ref, o_ref, vn_ref, st_ref, state_ref):
        @pl.when(pl.program_id(0) == 0)
        def _():
            state_ref[...] = jnp.zeros_like(state_ref)

        gain = gain_ref[...]
        for ci in range(nc):
            rows = slice(ci * C, (ci + 1) * C)
            st = state_ref[...]
            for h in range(DN_HEADS):
                st_ref[ci * DN_WIDTH + h * HD:ci * DN_WIDTH + (h + 1) * HD, :] = st[h]
            v_new = _stack_heads(u_ref, rows) - _bnn(_stack_heads(w_ref, rows), st)
            o = _bnn(_stack_heads(qg_ref, rows), st) + _bnn(aq_ref[:, rows, :], v_new)
            egl = jnp.stack([egl_ref[ci * 8 + h:ci * 8 + h + 1, :] for h in range(DN_HEADS)])
            state_ref[...] = st * egl + _btn(_stack_heads(kd_ref, rows), v_new)
            r = lax.rsqrt(jnp.mean(o * o, axis=-1, keepdims=True) + NORM_EPS)
            gt = _stack_heads(gate_ref, rows)
            dn = o * r * gain * (gt * _sigmoid(gt))
            for h in range(DN_HEADS):
                sl = slice(h * HD, (h + 1) * HD)
                vn_ref[rows, sl] = v_new[h]
                o_ref[rows, sl] = o[h]
                dn_ref[rows, sl] = dn[h]

    tok = lambda wd: pl.BlockSpec((nc * C, wd), lambda n: (n, 0))
    sq = pl.BlockSpec((DN_HEADS, nc * C, C), lambda n: (0, n, 0))
    vec = pl.BlockSpec((1, 128), lambda n: (0, 0))
    return pl.pallas_call(
        body, name="dn_scan_fwd", grid=(N // nc,),
        in_specs=[tok(DN_WIDTH)] * 4 + [sq, pl.BlockSpec((nc * 8, 128), lambda n: (n, 0)), tok(DN_WIDTH), vec],
        out_specs=[tok(DN_WIDTH)] * 3 + [pl.BlockSpec((nc * DN_WIDTH, HD), lambda n: (n, 0))],
        out_shape=[jax.ShapeDtypeStruct((S, DN_WIDTH), F32)] * 3 + [jax.ShapeDtypeStruct((N * DN_WIDTH, HD), F32)],
        scratch_shapes=[pltpu.VMEM((DN_HEADS, HD, HD), F32)],
        compiler_params=_params(1),
    )(u, w, qg, kd, aq, egl, gate, dn_gain)


def _dn_scan_bwd(w, qg, kd, aq, egl, gate, dn_gain, o, ddn):
    S = w.shape[0]
    C = DN_CHUNK
    N = S // C
    HD = DN_HEAD_DIM
    nc = SCAN_CHUNKS

    def body(w_ref, qg_ref, kd_ref, aq_ref, egl_ref, gate_ref, gain_ref, o_ref, ddn_ref,
             do_ref, dvn_ref, dgate_ref, dst_ref, small_ref, dstate_ref):
        @pl.when(pl.program_id(0) == 0)
        def _():
            dstate_ref[...] = jnp.zeros_like(dstate_ref)
            small_ref[...] = jnp.zeros_like(small_ref)

        gain = gain_ref[...]
        d_gain = jnp.zeros((1, 128), F32)
        for ci in reversed(range(nc)):
            rows = slice(ci * C, (ci + 1) * C)
            dsn = dstate_ref[...]
            for h in range(DN_HEADS):
                dst_ref[ci * DN_WIDTH + h * HD:ci * DN_WIDTH + (h + 1) * HD, :] = dsn[h]
            ov = _stack_heads(o_ref, rows)
            r = lax.rsqrt(jnp.mean(ov * ov, axis=-1, keepdims=True) + NORM_EPS)
            on = ov * r
            gt = _stack_heads(gate_ref, rows)
            sgt = _sigmoid(gt)
            silu_g = gt * sgt
            dy = _stack_heads(ddn_ref, rows)
            d_gain = d_gain + jnp.sum(jnp.sum(dy * on * silu_g, axis=1, keepdims=True), axis=0)
            dgate = dy * on * gain * (sgt * (1.0 + gt * (1.0 - sgt)))
            don = dy * gain * silu_g
            do = r * (don - on * jnp.mean(don * on, axis=-1, keepdims=True))
            d_vnew = _btn(aq_ref[:, rows, :], do) + _bnn(_stack_heads(kd_ref, rows), dsn)
            egl = jnp.stack([egl_ref[ci * 8 + h:ci * 8 + h + 1, :] for h in range(DN_HEADS)])
            dstate_ref[...] = _btn(_stack_heads(qg_ref, rows), do) + dsn * egl - _btn(_stack_heads(w_ref, rows), d_vnew)
            for h in range(DN_HEADS):
                sl = slice(h * HD, (h + 1) * HD)
                do_ref[rows, sl] = do[h]
                dvn_ref[rows, sl] = d_vnew[h]
                dgate_ref[rows, sl] = dgate[h]
        small_ref[...] += jnp.concatenate([d_gain, jnp.zeros((7, 128), F32)], axis=0)

    nb = N // nc
    tok = lambda wd: pl.BlockSpec((nc * C, wd), lambda i: (nb - 1 - i, 0))
    sq = pl.BlockSpec((DN_HEADS, nc * C, C), lambda i: (0, nb - 1 - i, 0))
    vec = pl.BlockSpec((1, 128), lambda i: (0, 0))
    return pl.pallas_call(
        body, name="dn_scan_bwd", grid=(nb,),
        in_specs=[tok(DN_WIDTH)] * 3 + [sq, pl.BlockSpec((nc * 8, 128), lambda i: (nb - 1 - i, 0)), tok(DN_WIDTH), vec,
                                       tok(DN_WIDTH), tok(DN_WIDTH)],
        out_specs=[tok(DN_WIDTH)] * 3 + [pl.BlockSpec((nc * DN_WIDTH, HD), lambda i: (nb - 1 - i, 0)),
                                        pl.BlockSpec((8, 128), lambda i: (0, 0))],
        out_shape=[jax.ShapeDtypeStruct((S, DN_WIDTH), F32)] * 3 + [jax.ShapeDtypeStruct((N * DN_WIDTH, HD), F32),
                                                                  jax.ShapeDtypeStruct((8, 128), F32)],
        scratch_shapes=[pltpu.VMEM((DN_HEADS, HD, HD), F32)],
        compiler_params=_params(1),
    )(w, qg, kd, aq, egl, gate, dn_gain, o, ddn)


def _dn_post(qn, kn, v, bd, avec, dvec, t_inv, v_new_all, states, dstates, do_all, dvn_all):
    S = qn.shape[0]
    C = DN_CHUNK
    N = S // C
    HD = DN_HEAD_DIM
    nc = PREP_CHUNKS
    B = nc * DN_HEADS

    def body(q_ref, k_ref, v_ref, bd_ref, a_ref, d_ref, t_ref, vn_ref, st_ref, dst_ref, do_ref, dvn_ref,
             dq_ref, dk_ref, dv_ref, dbd_ref, small_ref):
        @pl.when(pl.program_id(0) == 0)
        def _():
            small_ref[...] = jnp.zeros_like(small_ref)

        avec = a_ref[...]
        bds = [bd_ref[ci * C:(ci + 1) * C, :] for ci in range(nc)]
        k = _stack_units(k_ref, nc)
        vv = _stack_units(v_ref, nc)
        t = jnp.concatenate([t_ref[:, ci * C:(ci + 1) * C, :] for ci in range(nc)], axis=0)
        c = _dn_common_b(bds, avec, d_ref[...], _stack_units(q_ref, nc), k, vv, t=t)
        q, kb, eg, u, w = c["q"], c["kb"], c["eg"], c["u"], c["w"]
        beta, decay, incl, strict, eye = c["beta"], c["decay"], c["incl"], c["strict"], c["eye"]
        st = jnp.stack([st_ref[b * HD:(b + 1) * HD, :] for b in range(B)])
        dsn = jnp.stack([dst_ref[b * HD:(b + 1) * HD, :] for b in range(B)])
        v_new = _stack_units(vn_ref, nc)
        do = _stack_units(do_ref, nc)
        d_vnew = _stack_units(dvn_ref, nc)
        egl = jnp.exp(c["g_last"])
        daq = jnp.where(incl, _bnt(do, v_new), 0.0)
        d_qg = _bnt(do, st)
        d_kd = _bnt(v_new, dsn)
        d_glast = jnp.sum(jnp.sum(dsn * st, axis=-1, keepdims=True), axis=1, keepdims=True) * egl
        d_w = -_bnt(d_vnew, st)
        d_ru = _btn(t, d_vnew)
        d_rw = _btn(t, d_w)
        da = -jnp.where(strict, _bnt(d_ru, u) + _bnt(d_rw, w), 0.0)
        dv = d_ru * beta
        dbeta = jnp.sum(d_ru * vv, axis=-1, keepdims=True)
        dkb = d_rw * eg
        dgc = jnp.sum(d_rw * c["rhs_w"], axis=-1, keepdims=True)
        dkk = da * decay
        ddecay = da * c["kk"]
        dkb = dkb + _bnn(dkk, k)
        dk = _btn(dkk, kb)
        dqk = daq * decay
        ddecay = ddecay + daq * c["qk"]
        dq = _bnn(dqk, k)
        dk = dk + _btn(dqk, q)
        m = ddecay * decay
        col_sum = jnp.sum(m, axis=1, keepdims=True)
        dgc = dgc + jnp.sum(m, axis=-1, keepdims=True) - jnp.sum(eye * col_sum, axis=-1, keepdims=True)
        dq = dq + d_qg * eg
        dgc = dgc + jnp.sum(d_qg * c["qg"], axis=-1, keepdims=True)
        dk = dk + d_kd * c["ekd"]
        tk = jnp.sum(d_kd * c["kd"], axis=-1, keepdims=True)
        dgc = dgc - tk
        d_glast = d_glast + jnp.sum(tk, axis=1, keepdims=True)
        dk = dk + dkb * beta
        dbeta = dbeta + jnp.sum(dkb * k, axis=-1, keepdims=True)
        dgc = dgc + jnp.where(c["last"], d_glast, 0.0)
        dgc_row = jnp.sum(eye * dgc, axis=1, keepdims=True)
        dgraw = jnp.sum(jnp.where(c["col"] >= c["row"], dgc_row, 0.0), axis=-1, keepdims=True)
        _store_units(dq_ref, dq * (HD ** -0.5), nc)
        _store_units(dk_ref, dk, nc)
        _store_units(dv_ref, dv, nc)
        dbraw = dbeta * beta * (1.0 - beta)
        dzc = dgraw * _sigmoid(c["zc"])
        ga = dgraw * c["graw"]
        lane = c["lane"]
        lane1 = lax.broadcasted_iota(jnp.int32, (1, 128), 1)
        neg_ea = -jnp.exp(avec)
        d_alog = jnp.zeros((1, 128), F32)
        d_dt = jnp.zeros((1, 128), F32)
        for ci in range(nc):
            dbd = jnp.zeros((C, 128), F32)
            for h in range(DN_HEADS):
                b = ci * DN_HEADS + h
                dz = dzc[b] * neg_ea
                dbd = dbd + jnp.where(lane == h, dbraw[b], 0.0) + jnp.where(lane == DN_HEADS + h, dz, 0.0)
                d_alog = d_alog + jnp.where(lane1 == DN_HEADS + h, jnp.sum(ga[b], axis=0, keepdims=True), 0.0)
                d_dt = d_dt + jnp.where(lane1 == DN_HEADS + h, jnp.sum(dz, axis=0, keepdims=True), 0.0)
            dbd_ref[ci * C:(ci + 1) * C, :] = dbd
        small_ref[...] += jnp.concatenate([d_alog, d_dt, jnp.zeros((6, 128), F32)], axis=0)

    tok = lambda wd: pl.BlockSpec((nc * C, wd), lambda n: (n, 0))
    big = pl.BlockSpec((nc * DN_WIDTH, HD), lambda n: (n, 0))
    sq = pl.BlockSpec((DN_HEADS, nc * C, C), lambda n: (0, n, 0))
    vec = pl.BlockSpec((1, 128), lambda n: (0, 0))
    return pl.pallas_call(
        body, name="dn_post", grid=(N // nc,),
        in_specs=[tok(DN_WIDTH)] * 3 + [tok(128), vec, vec, sq, tok(DN_WIDTH), big, big, tok(DN_WIDTH), tok(DN_WIDTH)],
        out_specs=[tok(DN_WIDTH)] * 3 + [tok(128), pl.BlockSpec((8, 128), lambda n: (0, 0))],
        out_shape=[jax.ShapeDtypeStruct((S, DN_WIDTH), F32)] * 3 + [jax.ShapeDtypeStruct((S, 128), F32),
                                                                  jax.ShapeDtypeStruct((8, 128), F32)],
        compiler_params=_params(1),
    )(qn, kn, v, bd, avec, dvec, t_inv, v_new_all, states, dstates, do_all, dvn_all)


def _outproj_fwd(x, attn, dn, w_out):
    S, D = x.shape
    tm = 512

    def body(x_ref, a_ref, d_ref, w_ref, xo_ref, mix_ref):
        a = a_ref[...].astype(BF16)
        dd = d_ref[...].astype(BF16)
        mix_ref[:, 0:ATTN_WIDTH] = a
        mix_ref[:, ATTN_WIDTH:] = dd
        xo_ref[...] = x_ref[...] + _nn(a, w_ref[0:ATTN_WIDTH, :]) + _nn(dd, w_ref[ATTN_WIDTH:, :])

    tok = lambda w: pl.BlockSpec((tm, w), lambda i: (i, 0))
    return pl.pallas_call(
        body, name="outproj_fwd", grid=(S // tm,),
        in_specs=[tok(D), tok(ATTN_WIDTH), tok(DN_WIDTH), pl.BlockSpec((D, D), lambda i: (0, 0))],
        out_specs=[tok(D), tok(D)],
        out_shape=[jax.ShapeDtypeStruct((S, D), F32), jax.ShapeDtypeStruct((S, D), BF16)],
        compiler_params=_params(1),
    )(x, attn, dn, w_out)


def _outproj_bwd(dx, w_out):
    S, D = dx.shape
    tm = 512

    def body(dx_ref, w_ref, da_ref, dd_ref, dxb_ref):
        d = dx_ref[...].astype(BF16)
        dxb_ref[...] = d
        da_ref[...] = _nt(d, w_ref[0:ATTN_WIDTH, :])
        dd_ref[...] = _nt(d, w_ref[ATTN_WIDTH:, :])

    tok = lambda w: pl.BlockSpec((tm, w), lambda i: (i, 0))
    return pl.pallas_call(
        body, name="outproj_bwd", grid=(S // tm,),
        in_specs=[tok(D), pl.BlockSpec((D, D), lambda i: (0, 0))],
        out_specs=[tok(ATTN_WIDTH), tok(DN_WIDTH), tok(D)],
        out_shape=[jax.ShapeDtypeStruct((S, ATTN_WIDTH), F32), jax.ShapeDtypeStruct((S, DN_WIDTH), F32),
                   jax.ShapeDtypeStruct((S, D), BF16)],
        compiler_params=_params(1),
    )(dx, w_out)


def _loss_head(x, gain, target):
    S, D = x.shape
    tm = 512

    def body(x_ref, gain_ref, t_ref, loss_ref, dx_ref, dgain_ref):
        @pl.when(pl.program_id(0) == 0)
        def _():
            loss_ref[...] = jnp.zeros_like(loss_ref)
            dgain_ref[...] = jnp.zeros_like(dgain_ref)

        xf = x_ref[...]
        gain = gain_ref[...]
        r = lax.rsqrt(jnp.mean(xf * xf, axis=-1, keepdims=True) + NORM_EPS)
        xhat = xf * r
        err = xhat * gain - t_ref[...]
        part = 0.5 * jnp.sum(jnp.mean(err * err, axis=-1, keepdims=True), axis=0, keepdims=True)
        first = (lax.broadcasted_iota(jnp.int32, (8, 128), 0) == 0) & (lax.broadcasted_iota(jnp.int32, (8, 128), 1) == 0)
        loss_ref[...] += jnp.where(first, part, 0.0)
        dy = err * (1.0 / D)
        dgain_ref[...] += jnp.sum(dy * xhat, axis=0, keepdims=True)
        dxh = dy * gain
        dx_ref[...] = r * (dxh - xhat * jnp.mean(dxh * xhat, axis=-1, keepdims=True))

    tok = pl.BlockSpec((tm, D), lambda i: (i, 0))
    row = pl.BlockSpec((1, D), lambda i: (0, 0))
    return pl.pallas_call(
        body, name="loss_head", grid=(S // tm,),
        in_specs=[tok, row, tok],
        out_specs=[pl.BlockSpec((8, 128), lambda i: (0, 0)), tok, row],
        out_shape=[jax.ShapeDtypeStruct((8, 128), F32), jax.ShapeDtypeStruct((S, D), F32),
                   jax.ShapeDtypeStruct((1, D), F32)],
        compiler_params=_params(1),
    )(x, gain, target)


def _adamw(w, g, m, v, name):
    R, Ccols = w.shape
    tr = R
    for cand in (256, 128, 64, 32, 16, 8):
        if R % cand == 0:
            tr = cand
            break
    c1 = 1.0 - ADAM_B1 ** ADAM_STEP
    c2 = 1.0 - ADAM_B2 ** ADAM_STEP

    def body(w_ref, g_ref, m_ref, v_ref, d_ref, nm_ref, nv_ref):
        gv = g_ref[...]
        mn = ADAM_B1 * m_ref[...] + (1.0 - ADAM_B1) * gv
        vn = ADAM_B2 * v_ref[...] + (1.0 - ADAM_B2) * (gv * gv)
        nm_ref[...] = mn
        nv_ref[...] = vn
        d_ref[...] = -ADAM_LR * ((mn / c1) / (jnp.sqrt(vn / c2) + ADAM_EPS) + ADAM_WD * w_ref[...])

    spec = pl.BlockSpec((tr, Ccols), lambda i: (i, 0))
    return pl.pallas_call(
        body, name=name, grid=(R // tr,), in_specs=[spec] * 4, out_specs=[spec] * 3,
        out_shape=[jax.ShapeDtypeStruct((R, Ccols), F32)] * 3, compiler_params=_params(1),
    )(w, g, m, v)


def _local_step(x, target, wts, small):
    g1, g2, gm, gf = small["norm_ffn1"], small["norm_ffn2"], small["norm_mix"], small["norm_final"]

    x1, h1, fg1, fu1 = _ffn_fwd(x, g1, wts["ffn1_gate"], wts["ffn1_up"], wts["ffn1_down"], "ffn1_fwd")
    h2, aq, ak, av, xq, xk, xv, gate, bd = _inproj_fwd(x1, gm, wts["w_in"])
    parts = [_attn_fwd(aq, ak, av, d, f"attn_fwd_d{d}") for d in DILATIONS]
    attn, lse = _attn_merge(parts)
    conv_w = small["conv_w"]
    qn, kn, vv = _conv_fwd(xq, xk, xv, conv_w)
    dn_u, dn_w, dn_qg, dn_kd, dn_aq, dn_t, dn_egl = _dn_prep(qn, kn, vv, bd, small["avec"], small["dvec"])
    dn, o_dn, v_new, states = _dn_scan_fwd(dn_u, dn_w, dn_qg, dn_kd, dn_aq, dn_egl, gate, small["dn_norm"])
    x2, mix = _outproj_fwd(x1, attn, dn, wts["w_out"])
    x3, h3, fg2, fu2 = _ffn_fwd(x2, g2, wts["ffn2_gate"], wts["ffn2_up"], wts["ffn2_down"], "ffn2_fwd")
    loss, dx3, d_gf = _loss_head(x3, gf, target)

    grads = {}
    dx2, d_g2, dfg2, dfu2, act2, dout2 = _ffn_bwd(dx3, x2, g2, fg2, fu2, wts["ffn2_down"], wts["ffn2_gate"],
                                                 wts["ffn2_up"], "ffn2_bwd")
    tk = 512
    grads["ffn2_gate"] = _dw_chunks(dfg2, h3, tk, "dw_ffn2_gate")
    grads["ffn2_up"] = _dw_chunks(dfu2, h3, tk, "dw_ffn2_up")
    grads["ffn2_down"] = _dw_chunks(act2, dout2, tk, "dw_ffn2_down")

    dattn, ddn, dx2b = _outproj_bwd(dx2, wts["w_out"])
    grads["w_out"] = _matmul_tn(mix, dx2b, D_MODEL, tk, "dw_out")

    dd = _attn_delta(dattn, attn)
    daq = dak = dav = None
    for d in DILATIONS:
        daq = _attn_bwd_q(aq, ak, av, dattn, lse, dd, daq, d, f"attn_bwd_q_d{d}")
        dak, dav = _attn_bwd_kv(aq, ak, av, dattn, lse, dd, dak, dav, d, f"attn_bwd_kv_d{d}")

    do_dn, dvn, dgate, dstates, d_dn_gain = _dn_scan_bwd(dn_w, dn_qg, dn_kd, dn_aq, dn_egl, gate, small["dn_norm"], o_dn, ddn)
    dqn, dkn, dvv, dbd, dn_small = _dn_post(qn, kn, vv, bd, small["avec"], small["dvec"], dn_t, v_new, states, dstates,
                                            do_dn, dvn)
    dcq, dck, dcv, dwq, dwk, dwv = _conv_bwd_pre(xq, xk, xv, conv_w, dqn, dkn, dvv)
    dxq, dxk, dxv = _conv_bwd_x(dcq, dck, dcv, conv_w)
    d_conv = jnp.concatenate([dwq[:CONV_WIDTH], dwk[:CONV_WIDTH], dwv[:CONV_WIDTH]], axis=1)

    dx1, d_gm, dproj = _inproj_bwd(dx2, x1, gm, [daq, dak, dav, dxq, dxk, dxv, dgate], dbd, wts["w_in"])
    grads["w_in"] = _matmul_tn(dproj, h2, IN_COLS_PADDED, 256, "dw_in")

    dx0, d_g1, dfg1, dfu1, act1, dout1 = _ffn_bwd(dx1, x, g1, fg1, fu1, wts["ffn1_down"], wts["ffn1_gate"],
                                                 wts["ffn1_up"], "ffn1_bwd")
    grads["ffn1_gate"] = _dw_chunks(dfg1, h1, tk, "dw_ffn1_gate")
    grads["ffn1_up"] = _dw_chunks(dfu1, h1, tk, "dw_ffn1_up")
    grads["ffn1_down"] = _dw_chunks(act1, dout1, tk, "dw_ffn1_down")

    small_grads = dict(norm_ffn1=d_g1, norm_mix=d_gm, norm_ffn2=d_g2, norm_final=d_gf, conv_w=d_conv,
                       a_log=dn_small[0:1], dt_bias=dn_small[1:2], dn_norm=d_dn_gain[0:1])
    return loss, dx0, grads, small_grads


PACK_SECTIONS = (("ffn1_gate", 704), ("ffn1_up", 704), ("ffn1_down", 704), ("w_in", 898), ("w_out", 256),
                 ("ffn2_gate", 704), ("ffn2_up", 704), ("ffn2_down", 704))
PACK_ROWS = 5408
HALF_ROWS = PACK_ROWS // 2
ADD_ROWS = 208

HBM = pl.BlockSpec(memory_space=pl.ANY)
VMEM_SPEC = pl.BlockSpec(memory_space=pltpu.VMEM)


def _coords():
    return lax.axis_index("x"), lax.axis_index("y"), lax.axis_index("c")


def _remote(src, dst, send_sems, recv_sems, k, dev):
    return pltpu.make_async_remote_copy(src_ref=src, dst_ref=dst, send_sem=send_sems.at[k], recv_sem=recv_sems.at[k],
                                        device_id=dev, device_id_type=MESH)


def _allreduce_small(buf, name):
    R, Cc = buf.shape

    def body(src_ref, out_ref, recv_ref, send_sems, recv_sems):
        x, y, c = _coords()
        copies = []
        for m in range(1, 8):
            fx, fy, fc = (m >> 2) & 1, (m >> 1) & 1, m & 1
            dev = (x ^ fx if fx else x, y ^ fy if fy else y, c ^ fc if fc else c)
            cp = _remote(src_ref, recv_ref.at[m - 1], send_sems, recv_sems, m - 1, dev)
            cp.start()
            copies.append(cp)
        for cp in copies:
            cp.wait()
        r = [src_ref[...]] + [recv_ref[m] for m in range(7)]
        out_ref[...] = ((r[0] + r[1]) + (r[2] + r[3])) + ((r[4] + r[5]) + (r[6] + r[7]))

    return pl.pallas_call(
        body, name=name, out_shape=jax.ShapeDtypeStruct((R, Cc), F32),
        in_specs=[VMEM_SPEC], out_specs=VMEM_SPEC,
        scratch_shapes=[pltpu.VMEM((7, R, Cc), F32), pltpu.SemaphoreType.DMA((7,)), pltpu.SemaphoreType.DMA((7,))],
    )(buf)


def _allgather_weights(pack2):
    _, Hh, Cc = pack2.shape

    def body(src_ref, out_ref, send_sems, recv_sems):
        x, y, c = _coords()
        sib = (x, y, 1 - c)
        others = [(1 - x, y), (x, 1 - y), (1 - x, 1 - y)]
        blk = lambda cx, cy, half: out_ref.at[2 * cx + cy, half]
        mine = _remote(src_ref, out_ref.at[2 * x + y], send_sems, recv_sems, 6, sib)
        mine.start()
        first = [_remote(src_ref.at[c], blk(x, y, c), send_sems, recv_sems, j, (ox, oy, c)) for j, (ox, oy) in enumerate(others)]
        for cp in first:
            cp.start()
        passed = [_remote(blk(ox, oy, c), blk(ox, oy, c), send_sems, recv_sems, 3 + j, sib) for j, (ox, oy) in enumerate(others)]
        for j, (ox, oy) in enumerate(others):
            _remote(src_ref.at[c], blk(ox, oy, c), send_sems, recv_sems, j, (ox, oy, c)).wait_recv()
            passed[j].start()
        for j, (ox, oy) in enumerate(others):
            _remote(src_ref.at[c], blk(ox, oy, 1 - c), send_sems, recv_sems, 3 + j, sib).wait_recv()
        for cp in first + passed:
            cp.wait_send()
        mine.wait()

    return pl.pallas_call(
        body, name="allgather_weights", out_shape=jax.ShapeDtypeStruct((N_CHIPS, 2, Hh, Cc), pack2.dtype),
        in_specs=[HBM], out_specs=HBM,
        scratch_shapes=[pltpu.SemaphoreType.DMA((7,)), pltpu.SemaphoreType.DMA((7,))],
    )(pack2)


def _rs_swap_halves(gpack):
    _, nj, Hh, Cc = gpack.shape

    def body(src_ref, out_ref, send_sems, recv_sems):
        x, y, c = _coords()
        cp = _remote(src_ref.at[1 - c], out_ref, send_sems, recv_sems, 0, (x, y, 1 - c))
        cp.start()
        cp.wait()

    return pl.pallas_call(
        body, name="rs_swap_halves", out_shape=jax.ShapeDtypeStruct((nj, Hh, Cc), gpack.dtype),
        in_specs=[HBM], out_specs=HBM,
        scratch_shapes=[pltpu.SemaphoreType.DMA((1,)), pltpu.SemaphoreType.DMA((1,))],
    )(gpack)


def _rs_add_pair(gpack, other, c):
    _, nj, Hh, Cc = gpack.shape
    tr = ADD_ROWS

    def body(c_ref, a_ref, b_ref, o_ref):
        o_ref[...] = (a_ref[...] + b_ref[...]).astype(BF16)

    return pl.pallas_call(
        body, name="rs_add_pair",
        grid_spec=pltpu.PrefetchScalarGridSpec(
            num_scalar_prefetch=1, grid=(nj, Hh // tr),
            in_specs=[pl.BlockSpec((None, None, tr, Cc), lambda j, i, c_ref: (c_ref[0], j, i, 0)),
                      pl.BlockSpec((None, tr, Cc), lambda j, i, c_ref: (j, i, 0))],
            out_specs=pl.BlockSpec((None, tr, Cc), lambda j, i, c_ref: (j, i, 0))),
        out_shape=jax.ShapeDtypeStruct((nj, Hh, Cc), BF16),
        compiler_params=_params(2),
    )(c, gpack, other)


def _rs_exchange_chips(part):
    nj, Hh, Cc = part.shape

    def body(src_ref, out_ref, send_sems, recv_sems):
        x, y, c = _coords()
        others = [(1 - x, y), (x, 1 - y), (1 - x, 1 - y)]
        cps = [_remote(src_ref.at[2 * ox + oy], out_ref.at[k], send_sems, recv_sems, k, (ox, oy, c))
               for k, (ox, oy) in enumerate(others)]
        for cp in cps:
            cp.start()
        for cp in cps:
            cp.wait()

    return pl.pallas_call(
        body, name="rs_exchange_chips", out_shape=jax.ShapeDtypeStruct((3, Hh, Cc), part.dtype),
        in_specs=[HBM], out_specs=HBM,
        scratch_shapes=[pltpu.SemaphoreType.DMA((3,)), pltpu.SemaphoreType.DMA((3,))],
    )(part)


def _rs_add_total(part, recv, chip):
    nj, Hh, Cc = part.shape
    tr = ADD_ROWS

    def body(chip_ref, p_ref, r0_ref, r1_ref, r2_ref, o_ref):
        f = lambda r: r[...].astype(F32)
        o_ref[...] = (f(p_ref) + f(r0_ref)) + (f(r1_ref) + f(r2_ref))

    rk = lambda k: pl.BlockSpec((None, tr, Cc), lambda i, chip_ref, k=k: (k, i, 0))
    return pl.pallas_call(
        body, name="rs_add_total",
        grid_spec=pltpu.PrefetchScalarGridSpec(
            num_scalar_prefetch=1, grid=(Hh // tr,),
            in_specs=[pl.BlockSpec((None, tr, Cc), lambda i, chip_ref: (chip_ref[0], i, 0)), rk(0), rk(1), rk(2)],
            out_specs=pl.BlockSpec((tr, Cc), lambda i, chip_ref: (i, 0))),
        out_shape=jax.ShapeDtypeStruct((Hh, Cc), F32),
        compiler_params=_params(1),
    )(chip, part, recv, recv, recv)


def _rs_share_total(total):
    Hh, Cc = total.shape

    def body(src_ref, out_ref, send_sems, recv_sems):
        x, y, c = _coords()
        cp = _remote(src_ref, out_ref, send_sems, recv_sems, 0, (x, y, 1 - c))
        cp.start()
        cp.wait()

    return pl.pallas_call(
        body, name="rs_share_total", out_shape=jax.ShapeDtypeStruct((Hh, Cc), total.dtype),
        in_specs=[HBM], out_specs=HBM,
        scratch_shapes=[pltpu.SemaphoreType.DMA((1,)), pltpu.SemaphoreType.DMA((1,))],
    )(total)


BIG = ("ffn1_gate", "ffn1_up", "ffn1_down", "w_in", "w_out", "ffn2_gate", "ffn2_up", "ffn2_down")
W_IN_ROWS = 960


def _rows(ref, start, size):
    return ref.at[pl.ds(pl.multiple_of(start, 16), size)]


def _allgather_arrays(shards):
    n = len(shards)
    halves = [s.shape[0] // 2 for s in shards]

    def body(*refs):
        srcs, outs, send_sems, recv_sems = refs[:n], refs[n:2 * n], refs[2 * n], refs[2 * n + 1]
        x, y, c = _coords()
        sib = (x, y, 1 - c)
        me = 2 * x + y
        others = [(1 - x, y), (x, 1 - y), (1 - x, 1 - y)]
        started = []
        for a in range(n):
            h = halves[a]
            cp = _remote(srcs[a], outs[a].at[me], send_sems, recv_sems, 7 * a + 6, sib)
            cp.start()
            started.append(cp)
            for j, (ox, oy) in enumerate(others):
                cp = _remote(_rows(srcs[a], c * h, h), _rows(outs[a].at[me], c * h, h), send_sems, recv_sems, 7 * a + j, (ox, oy, c))
                cp.start()
                started.append(cp)
        for a in range(n):
            h = halves[a]
            for j, (ox, oy) in enumerate(others):
                got = _rows(outs[a].at[2 * ox + oy], c * h, h)
                _remote(got, got, send_sems, recv_sems, 7 * a + j, sib).wait_recv()
                cp = _remote(got, got, send_sems, recv_sems, 7 * a + 3 + j, sib)
                cp.start()
                started.append(cp)
        for a in range(n):
            h = halves[a]
            for j, (ox, oy) in enumerate(others):
                got = _rows(outs[a].at[2 * ox + oy], (1 - c) * h, h)
                _remote(got, got, send_sems, recv_sems, 7 * a + 3 + j, sib).wait_recv()
            _remote(srcs[a], outs[a].at[me], send_sems, recv_sems, 7 * a + 6, sib).wait_recv()
        for cp in started:
            cp.wait_send()

    return pl.pallas_call(
        body, name="allgather_weights",
        out_shape=[jax.ShapeDtypeStruct((N_CHIPS,) + s.shape, s.dtype) for s in shards],
        in_specs=[HBM] * n, out_specs=[HBM] * n,
        scratch_shapes=[pltpu.SemaphoreType.DMA((7 * n,)), pltpu.SemaphoreType.DMA((7 * n,))],
    )(*shards)


def _swap_sibling(arrs, pick_other_half, name):
    n = len(arrs)
    outs = [jax.ShapeDtypeStruct((a.shape[0], a.shape[1] // 2) + a.shape[2:] if pick_other_half else a.shape, a.dtype) for a in arrs]

    def body(*refs):
        srcs, dsts, send_sems, recv_sems = refs[:n], refs[n:2 * n], refs[2 * n], refs[2 * n + 1]
        x, y, c = _coords()
        cps = []
        for a in range(n):
            src = srcs[a]
            if pick_other_half:
                h = src.shape[1] // 2
                src = src.at[:, pl.ds(pl.multiple_of((1 - c) * h, 16), h)]
            cp = _remote(src, dsts[a], send_sems, recv_sems, a, (x, y, 1 - c))
            cp.start()
            cps.append(cp)
        for cp in cps:
            cp.wait()

    return pl.pallas_call(
        body, name=name, out_shape=outs, in_specs=[HBM] * n, out_specs=[HBM] * n,
        scratch_shapes=[pltpu.SemaphoreType.DMA((n,)), pltpu.SemaphoreType.DMA((n,))],
    )(*arrs)


def _rs_add_pairs(gs, others, c):
    n = len(gs)
    hbs = [g.shape[1] // 4 for g in gs]

    def body(c_ref, *refs):
        for a in range(n):
            refs[2 * n + a][...] = (refs[a][...] + refs[n + a][...]).astype(BF16)

    mine = lambda hb: pl.BlockSpec((None, hb, D_MODEL), lambda j, s, c_ref: (j, c_ref[0] * 2 + s, 0))
    flat = lambda hb: pl.BlockSpec((None, hb, D_MODEL), lambda j, s, c_ref: (j, s, 0))
    return pl.pallas_call(
        body, name="rs_add_pairs",
        grid_spec=pltpu.PrefetchScalarGridSpec(
            num_scalar_prefetch=1, grid=(N_CHIPS, 2),
            in_specs=[mine(hb) for hb in hbs] + [flat(hb) for hb in hbs],
            out_specs=[flat(hb) for hb in hbs]),
        out_shape=[jax.ShapeDtypeStruct(o.shape, BF16) for o in others],
        compiler_params=_params(2),
    )(c, *gs, *others)


def _rs_exchange_arrays(parts):
    n = len(parts)

    def body(*refs):
        srcs, dsts, send_sems, recv_sems = refs[:n], refs[n:2 * n], refs[2 * n], refs[2 * n + 1]
        x, y, c = _coords()
        others = [(1 - x, y), (x, 1 - y), (1 - x, 1 - y)]
        cps = []
        for a in range(n):
            for k, (ox, oy) in enumerate(others):
                cp = _remote(srcs[a].at[2 * ox + oy], dsts[a].at[k], send_sems, recv_sems, 3 * a + k, (ox, oy, c))
                cp.start()
                cps.append(cp)
        for cp in cps:
            cp.wait()

    return pl.pallas_call(
        body, name="rs_exchange_chips", out_shape=[jax.ShapeDtypeStruct((3,) + p.shape[1:], p.dtype) for p in parts],
        in_specs=[HBM] * n, out_specs=[HBM] * n,
        scratch_shapes=[pltpu.SemaphoreType.DMA((3 * n,)), pltpu.SemaphoreType.DMA((3 * n,))],
    )(*parts)


def _rs_add_totals(parts, recvs, chip):
    n = len(parts)
    hbs = [p.shape[1] // 2 for p in parts]

    def body(chip_ref, *refs):
        f = lambda r: r[...].astype(F32)
        for a in range(n):
            p, r0, r1, r2 = refs[a], refs[n + 3 * a], refs[n + 3 * a + 1], refs[n + 3 * a + 2]
            refs[4 * n + a][...] = (f(p) + f(r0)) + (f(r1) + f(r2))

    own = lambda hb: pl.BlockSpec((None, hb, D_MODEL), lambda s, chip_ref: (chip_ref[0], s, 0))
    slot = lambda hb, k: pl.BlockSpec((None, hb, D_MODEL), lambda s, chip_ref, k=k: (k, s, 0))
    recv_specs = [slot(hb, k) for hb in hbs for k in range(3)]
    recv_args = [r for r in recvs for _ in range(3)]
    return pl.pallas_call(
        body, name="rs_add_totals",
        grid_spec=pltpu.PrefetchScalarGridSpec(
            num_scalar_prefetch=1, grid=(2,),
            in_specs=[own(hb) for hb in hbs] + recv_specs,
            out_specs=[pl.BlockSpec((hb, D_MODEL), lambda s, chip_ref: (s, 0)) for hb in hbs]),
        out_shape=[jax.ShapeDtypeStruct(p.shape[1:], F32) for p in parts],
        compiler_params=_params(1),
    )(chip, *parts, *recv_args)


def _permute_w_in(w):
    return jnp.concatenate([w[:, :3072], w[:, 3080:IN_COLS], w[:, 3072:3080],
                            jnp.zeros((w.shape[0], IN_COLS_PADDED - IN_COLS), w.dtype)], axis=1)


def _pack_rows(shards, dtype):
    rows = [shards[n].astype(dtype).reshape(r, D_MODEL) for n, r in PACK_SECTIONS]
    used = sum(r for _, r in PACK_SECTIONS)
    rows.append(jnp.zeros((PACK_ROWS - used, D_MODEL), dtype))
    return jnp.concatenate(rows, axis=0)


def _unpack_rows(pack, shapes):
    out, at = {}, 0
    for n, r in PACK_SECTIONS:
        out[n] = pack[at:at + r].reshape(shapes[n])
        at += r
    return out


SHARD_SHAPES = dict(ffn1_gate=(1024, 704), ffn1_up=(1024, 704), ffn1_down=(704, 1024), w_in=(1024, 898),
                    w_out=(256, 1024), ffn2_gate=(1024, 704), ffn2_up=(1024, 704), ffn2_down=(704, 1024))
ROW_SHARDED = ("ffn1_down", "w_out", "ffn2_down")
SMALL_ROWS = 16


def _pad_row(v):
    v = v.reshape(1, -1)
    return jnp.pad(v, ((0, 0), (0, D_MODEL - v.shape[1])))


def kernel(x, norm_ffn1, ffn1_gate, ffn1_up, ffn1_down, norm_mix, w_in, conv_w, a_log, dt_bias, dn_norm, w_out, norm_ffn2, ffn2_gate, ffn2_up, ffn2_down, norm_final, loss_target, m_norm_ffn1, m_ffn1_gate, m_ffn1_up, m_ffn1_down, m_norm_mix, m_w_in, m_conv_w, m_a_log, m_dt_bias, m_dn_norm, m_w_out, m_norm_ffn2, m_ffn2_gate, m_ffn2_up, m_ffn2_down, m_norm_final, v_norm_ffn1, v_ffn1_gate, v_ffn1_up, v_ffn1_down, v_norm_mix, v_w_in, v_conv_w, v_a_log, v_dt_bias, v_dn_norm, v_w_out, v_norm_ffn2, v_ffn2_gate, v_ffn2_up, v_ffn2_down, v_norm_final):
    cx, cy, cc = _coords()
    chip = 2 * cx + cy
    big_w = dict(ffn1_gate=ffn1_gate[0], ffn1_up=ffn1_up[0], ffn1_down=ffn1_down[0], w_in=w_in[0], w_out=w_out[0],
                 ffn2_gate=ffn2_gate[0], ffn2_up=ffn2_up[0], ffn2_down=ffn2_down[0])
    big_m = dict(ffn1_gate=m_ffn1_gate[0], ffn1_up=m_ffn1_up[0], ffn1_down=m_ffn1_down[0], w_in=m_w_in[0], w_out=m_w_out[0],
                 ffn2_gate=m_ffn2_gate[0], ffn2_up=m_ffn2_up[0], ffn2_down=m_ffn2_down[0])
    big_v = dict(ffn1_gate=v_ffn1_gate[0], ffn1_up=v_ffn1_up[0], ffn1_down=v_ffn1_down[0], w_in=v_w_in[0], w_out=v_w_out[0],
                 ffn2_gate=v_ffn2_gate[0], ffn2_up=v_ffn2_up[0], ffn2_down=v_ffn2_down[0])

    wts = dict(zip(BIG, _allgather_arrays([big_w[n].astype(BF16) for n in BIG])))
    wts["w_out"] = wts["w_out"].reshape(D_MODEL, D_MODEL)
    wts["w_in"] = _permute_w_in(jnp.concatenate([wts["w_in"][j] for j in range(N_CHIPS)], axis=1))

    conv_shard = conv_w[0]
    emb = jnp.concatenate([jnp.where((chip == j) & (cc == 0), conv_shard, 0.0) for j in range(N_CHIPS)], axis=1)
    emb = jnp.pad(emb.reshape(6, D_MODEL), ((0, 2), (0, 0)))
    conv_full = _allreduce_small(emb, "allgather_conv_w")[:6].reshape(CONV_WIDTH, 3 * DN_WIDTH)

    zvec = jnp.zeros((1, 128), F32)
    small = dict(norm_ffn1=norm_ffn1, norm_mix=norm_mix, norm_ffn2=norm_ffn2, norm_final=norm_final[None],
                 conv_w=conv_full, avec=zvec.at[0, DN_HEADS:2 * DN_HEADS].set(a_log[0]),
                 dvec=zvec.at[0, DN_HEADS:2 * DN_HEADS].set(dt_bias[0]), dn_norm=dn_norm)

    loss, grad_x, grads, sg = _local_step(x[0], loss_target[0], wts, small)

    rows = [sg["norm_ffn1"], sg["norm_mix"], sg["norm_ffn2"], sg["norm_final"], _pad_row(sg["a_log"]), _pad_row(sg["dt_bias"]),
            _pad_row(sg["dn_norm"]), _pad_row(loss[0:1]), sg["conv_w"].reshape(6, D_MODEL), jnp.zeros((2, D_MODEL), F32)]
    red = _allreduce_small(jnp.concatenate(rows, axis=0), "allreduce_small")
    loss_out = red[7, 0]
    g_conv_full = red[8:14].reshape(CONV_WIDTH, 3 * DN_WIDTH)
    g_conv = lax.dynamic_slice_in_dim(g_conv_full, chip * (3 * DN_WIDTH // N_CHIPS), 3 * DN_WIDTH // N_CHIPS, axis=1)
    g_small = dict(norm_ffn1=red[0:1], norm_mix=red[1:2], norm_ffn2=red[2:3], norm_final=red[3],
                   a_log=red[4:5, DN_HEADS:2 * DN_HEADS], dt_bias=red[5:6, DN_HEADS:2 * DN_HEADS], dn_norm=red[6:7, :DN_HEAD_DIM])

    gi = grads["w_in"]
    gi = jnp.concatenate([gi[:3072], gi[3584:3592], gi[3072:3584]], axis=0).reshape(N_CHIPS, IN_COLS // N_CHIPS, D_MODEL)
    grads["w_in"] = jnp.pad(gi, ((0, 0), (0, W_IN_ROWS - IN_COLS // N_CHIPS), (0, 0)))
    grads["w_out"] = grads["w_out"].reshape(N_CHIPS, D_MODEL // N_CHIPS, D_MODEL)
    gs = [grads[n] for n in BIG]
    from_sibling = _swap_sibling(gs, True, "rs_swap_halves")
    parts = _rs_add_pairs(gs, from_sibling, cc.reshape(1).astype(jnp.int32))
    recvs = _rs_exchange_arrays(parts)
    totals = _rs_add_totals(parts, recvs, chip.reshape(1).astype(jnp.int32))
    theirs = _swap_sibling(totals, False, "rs_share_total")
    shard_g = {}
    for n, mine, other in zip(BIG, totals, theirs):
        full = jnp.where(cc == 0, jnp.concatenate([mine, other], axis=0), jnp.concatenate([other, mine], axis=0))
        if n == "w_in":
            full = full[:IN_COLS // N_CHIPS]
        shard_g[n] = full if n in ROW_SHARDED else full.T

    out_g, out_d, out_m, out_v = {}, {}, {}, {}
    for n in BIG:
        d, nm, nv = _adamw(big_w[n], shard_g[n], big_m[n], big_v[n], "adamw_" + n)
        out_g[n], out_d[n], out_m[n], out_v[n] = shard_g[n][None], d[None], nm[None], nv[None]
    d, nm, nv = _adamw(conv_w[0], g_conv, m_conv_w[0], v_conv_w[0], "adamw_conv_w")
    out_g["conv_w"], out_d["conv_w"], out_m["conv_w"], out_v["conv_w"] = g_conv[None], d[None], nm[None], nv[None]

    small_names = ("norm_ffn1", "norm_mix", "norm_ffn2", "norm_final", "a_log", "dt_bias", "dn_norm")
    small_w = dict(norm_ffn1=norm_ffn1, norm_mix=norm_mix, norm_ffn2=norm_ffn2, norm_final=norm_final, a_log=a_log,
                   dt_bias=dt_bias, dn_norm=dn_norm)
    small_m = dict(norm_ffn1=m_norm_ffn1, norm_mix=m_norm_mix, norm_ffn2=m_norm_ffn2, norm_final=m_norm_final, a_log=m_a_log,
                   dt_bias=m_dt_bias, dn_norm=m_dn_norm)
    small_v = dict(norm_ffn1=v_norm_ffn1, norm_mix=v_norm_mix, norm_ffn2=v_norm_ffn2, norm_final=v_norm_final, a_log=v_a_log,
                   dt_bias=v_dt_bias, dn_norm=v_dn_norm)
    stack = lambda dct: jnp.concatenate([_pad_row(dct[n]) for n in small_names] + [jnp.zeros((1, D_MODEL), F32)], axis=0)
    d, nm, nv = _adamw(stack(small_w), stack(g_small), stack(small_m), stack(small_v), "adamw_small")
    for k, n in enumerate(small_names):
        shape = small_w[n].shape
        size = math.prod(shape)
        out_g[n] = g_small[n].reshape(shape)
        out_d[n], out_m[n], out_v[n] = (t[k, :size].reshape(shape) for t in (d, nm, nv))

    order = ("norm_ffn1", "ffn1_gate", "ffn1_up", "ffn1_down", "norm_mix", "w_in", "conv_w", "a_log", "dt_bias", "dn_norm",
             "w_out", "norm_ffn2", "ffn2_gate", "ffn2_up", "ffn2_down", "norm_final")
    return (loss_out, grad_x[None], *[out_g[n] for n in order], *[out_d[n] for n in order],
            *[out_m[n] for n in order], *[out_v[n] for n in order])
```

```python
import functools
import math

import jax
import jax.numpy as jnp
from jax import lax
from jax.experimental import pallas as pl
from jax.experimental.pallas import tpu as pltpu

F32 = jnp.float32
BF16 = jnp.bfloat16
HI = lax.Precision.HIGH

D_MODEL = 1024
D_FF = 2816
ATTN_HEADS = 8
ATTN_WIDTH = 512
ATTN_BLOCK = 128
DILATIONS = (1, 4, 16)
DN_HEADS = 4
DN_HEAD_DIM = 128
DN_WIDTH = 512
DN_CHUNK = 64
CONV_WIDTH = 4
NORM_EPS = 1e-6
L2_EPS = 1e-6
IN_COLS = 3592
IN_COLS_PADDED = 3712
N_CHIPS = 4

ADAM_LR = 0.001
ADAM_B1 = 0.9
ADAM_B2 = 0.999
ADAM_EPS = 1e-08
ADAM_WD = 0.01
ADAM_STEP = 10

VMEM_LIMIT = 56 * 1024 * 1024
NEG_BIG = -1e30
MESH = pl.DeviceIdType.MESH


def _params(n_grid, vmem=VMEM_LIMIT):
    return pltpu.CompilerParams(dimension_semantics=("arbitrary",) * n_grid, vmem_limit_bytes=vmem)


def _nt(a, b, precision=None):
    return lax.dot_general(a, b, (((1,), (1,)), ((), ())), preferred_element_type=F32, precision=precision)


def _tn(a, b, precision=None):
    return lax.dot_general(a, b, (((0,), (0,)), ((), ())), preferred_element_type=F32, precision=precision)


def _nn(a, b, precision=None):
    return jnp.dot(a, b, preferred_element_type=F32, precision=precision)


def _sigmoid(x):
    return 1.0 / (1.0 + jnp.exp(-x))


def _ffn_fwd(x, gain, wg, wu, wd, name):
    S, D = x.shape
    nf, _, tf = wg.shape
    tm = 512

    def body(x_ref, gain_ref, wg_ref, wu_ref, wd_ref, xo_ref, h_ref, g_ref, u_ref, acc_ref, hs_ref):
        j = pl.program_id(1)

        @pl.when(j == 0)
        def _():
            xf = x_ref[...]
            r = lax.rsqrt(jnp.mean(xf * xf, axis=-1, keepdims=True) + NORM_EPS)
            h = (xf * r * gain_ref[...]).astype(BF16)
            hs_ref[...] = h
            h_ref[...] = h
            acc_ref[...] = jnp.zeros_like(acc_ref)

        h = hs_ref[...]
        g = _nn(h, wg_ref[...])
        u = _nn(h, wu_ref[...])
        g_ref[...] = g.astype(BF16)
        u_ref[...] = u.astype(BF16)
        act = g * _sigmoid(g) * u
        acc_ref[...] += _nn(act.astype(BF16), wd_ref[...])

        @pl.when(j == nf - 1)
        def _():
            xo_ref[...] = x_ref[...] + 0.5 * acc_ref[...]

    return pl.pallas_call(
        body, name=name, grid=(S // tm, nf),
        in_specs=[pl.BlockSpec((tm, D), lambda i, j: (i, 0)),
                  pl.BlockSpec((1, D), lambda i, j: (0, 0)),
                  pl.BlockSpec((None, D, tf), lambda i, j: (j, 0, 0)),
                  pl.BlockSpec((None, D, tf), lambda i, j: (j, 0, 0)),
                  pl.BlockSpec((None, tf, D), lambda i, j: (j, 0, 0))],
        out_specs=[pl.BlockSpec((tm, D), lambda i, j: (i, 0)),
                   pl.BlockSpec((tm, D), lambda i, j: (i, 0)),
                   pl.BlockSpec((None, tm, tf), lambda i, j: (j, i, 0)),
                   pl.BlockSpec((None, tm, tf), lambda i, j: (j, i, 0))],
        out_shape=[jax.ShapeDtypeStruct((S, D), F32), jax.ShapeDtypeStruct((S, D), BF16),
                   jax.ShapeDtypeStruct((nf, S, tf), BF16), jax.ShapeDtypeStruct((nf, S, tf), BF16)],
        scratch_shapes=[pltpu.VMEM((tm, D), F32), pltpu.VMEM((tm, D), BF16)],
        compiler_params=_params(2),
    )(x, gain, wg, wu, wd)


def _rmsnorm_bwd(dh, xf, gain):
    r = lax.rsqrt(jnp.mean(xf * xf, axis=-1, keepdims=True) + NORM_EPS)
    xhat = xf * r
    dgain = jnp.sum(dh * xhat, axis=0, keepdims=True)
    dxh = dh * gain
    dx = r * (dxh - xhat * jnp.mean(dxh * xhat, axis=-1, keepdims=True))
    return dx, dgain


def _ffn_bwd(dxo, x, gain, g, u, wd, wg, wu, name):
    S, D = x.shape
    nf, _, tf = g.shape
    tm = 512

    def body(dxo_ref, x_ref, gain_ref, g_ref, u_ref, wd_ref, wg_ref, wu_ref,
             dx_ref, dgain_ref, dg_ref, du_ref, act_ref, dout_ref, acc_ref, ds_ref):
        i = pl.program_id(0)
        j = pl.program_id(1)

        @pl.when(j == 0)
        def _():
            d = (0.5 * dxo_ref[...]).astype(BF16)
            ds_ref[...] = d
            dout_ref[...] = d
            acc_ref[...] = jnp.zeros_like(acc_ref)

        @pl.when((i == 0) & (j == 0))
        def _():
            dgain_ref[...] = jnp.zeros_like(dgain_ref)

        dact = _nt(ds_ref[...], wd_ref[...])
        gv = g_ref[...].astype(F32)
        uv = u_ref[...].astype(F32)
        sg = _sigmoid(gv)
        silu = gv * sg
        act_ref[...] = (silu * uv).astype(BF16)
        dgv = (dact * uv * (sg * (1.0 + gv * (1.0 - sg)))).astype(BF16)
        duv = (dact * silu).astype(BF16)
        dg_ref[...] = dgv
        du_ref[...] = duv
        acc_ref[...] += _nt(dgv, wg_ref[...]) + _nt(duv, wu_ref[...])

        @pl.when(j == nf - 1)
        def _():
            dx, dgain = _rmsnorm_bwd(acc_ref[...], x_ref[...], gain_ref[...])
            dx_ref[...] = dxo_ref[...] + dx
            dgain_ref[...] += dgain

    return pl.pallas_call(
        body, name=name, grid=(S // tm, nf),
        in_specs=[pl.BlockSpec((tm, D), lambda i, j: (i, 0)),
                  pl.BlockSpec((tm, D), lambda i, j: (i, 0)),
                  pl.BlockSpec((1, D), lambda i, j: (0, 0)),
                  pl.BlockSpec((None, tm, tf), lambda i, j: (j, i, 0)),
                  pl.BlockSpec((None, tm, tf), lambda i, j: (j, i, 0)),
                  pl.BlockSpec((None, tf, D), lambda i, j: (j, 0, 0)),
                  pl.BlockSpec((None, D, tf), lambda i, j: (j, 0, 0)),
                  pl.BlockSpec((None, D, tf), lambda i, j: (j, 0, 0))],
        out_specs=[pl.BlockSpec((tm, D), lambda i, j: (i, 0)),
                   pl.BlockSpec((1, D), lambda i, j: (0, 0)),
                   pl.BlockSpec((None, tm, tf), lambda i, j: (j, i, 0)),
                   pl.BlockSpec((None, tm, tf), lambda i, j: (j, i, 0)),
                   pl.BlockSpec((None, tm, tf), lambda i, j: (j, i, 0)),
                   pl.BlockSpec((tm, D), lambda i, j: (i, 0))],
        out_shape=[jax.ShapeDtypeStruct((S, D), F32), jax.ShapeDtypeStruct((1, D), F32),
                   jax.ShapeDtypeStruct((nf, S, tf), BF16), jax.ShapeDtypeStruct((nf, S, tf), BF16),
                   jax.ShapeDtypeStruct((nf, S, tf), BF16), jax.ShapeDtypeStruct((S, D), BF16)],
        scratch_shapes=[pltpu.VMEM((tm, D), F32), pltpu.VMEM((tm, D), BF16)],
        compiler_params=_params(2),
    )(dxo, x, gain, g, u, wd, wg, wu)


def _matmul_tn(a, b, tm, tk, name):
    K, M = a.shape
    N = b.shape[1]

    def body(a_ref, b_ref, o_ref):
        @pl.when(pl.program_id(1) == 0)
        def _():
            o_ref[...] = jnp.zeros_like(o_ref)

        o_ref[...] += _tn(a_ref[...], b_ref[...])

    return pl.pallas_call(
        body, name=name, grid=(M // tm, K // tk),
        in_specs=[pl.BlockSpec((tk, tm), lambda i, k: (k, i)),
                  pl.BlockSpec((tk, N), lambda i, k: (k, 0))],
        out_specs=pl.BlockSpec((tm, N), lambda i, k: (i, 0)),
        out_shape=jax.ShapeDtypeStruct((M, N), F32),
        compiler_params=_params(2),
    )(a, b)


def _dw_chunks(a, b, tk, name):
    nf, S, tf = a.shape
    N = b.shape[1]

    def body(a_ref, b_ref, o_ref):
        @pl.when(pl.program_id(1) == 0)
        def _():
            o_ref[...] = jnp.zeros_like(o_ref)

        o_ref[...] += _tn(a_ref[...], b_ref[...])

    return pl.pallas_call(
        body, name=name, grid=(nf, S // tk),
        in_specs=[pl.BlockSpec((None, tk, tf), lambda j, k: (j, k, 0)),
                  pl.BlockSpec((tk, N), lambda j, k: (k, 0))],
        out_specs=pl.BlockSpec((None, tf, N), lambda j, k: (j, 0, 0)),
        out_shape=jax.ShapeDtypeStruct((nf, tf, N), F32),
        compiler_params=_params(2),
    )(a, b)


VIEW_TILE = 512


def _view_spec(d, tile=VIEW_TILE):
    return pl.BlockSpec((tile // d, d * ATTN_WIDTH), lambda i: (i, 0))


def _view_shape(S, d, dtype):
    return jax.ShapeDtypeStruct((S // d, d * ATTN_WIDTH), dtype)


def _tile_to_views(val, planes, out_refs):
    for g in range(4):
        planes[g] = val[:, g * 128:(g + 1) * 128]
    for d, ref in zip(DILATIONS, out_refs):
        if d == 1:
            ref[...] = val.astype(ref.dtype)
            continue
        for r in range(d):
            for g in range(4):
                ref[:, r * ATTN_WIDTH + g * 128:r * ATTN_WIDTH + (g + 1) * 128] = (
                    planes[g, pl.ds(r, planes.shape[1] // d, stride=d), :].astype(ref.dtype))


def _view_to_tile(ref, d, planes):
    if d == 1:
        return ref[...]
    for r in range(d):
        for g in range(4):
            planes[g, pl.ds(r, planes.shape[1] // d, stride=d), :] = ref[:, r * ATTN_WIDTH + g * 128:r * ATTN_WIDTH + (g + 1) * 128]
    return jnp.concatenate([planes[g] for g in range(4)], axis=1)


def _inproj_fwd(x, gain, w_in_p):
    S, D = x.shape
    tm = VIEW_TILE
    W = ATTN_WIDTH

    def body(x_ref, gain_ref, w_ref, h_ref, q1, q4, q16, k1, k4, k16, v1, v4, v16, dq_ref, dk_ref, dv_ref, gate_ref, bd_ref,
             planes):
        xf = x_ref[...]
        r = lax.rsqrt(jnp.mean(xf * xf, axis=-1, keepdims=True) + NORM_EPS)
        h = (xf * r * gain_ref[...]).astype(BF16)
        h_ref[...] = h
        _tile_to_views(_nn(h, w_ref[:, 0:W]) * 0.125, planes, (q1, q4, q16))
        _tile_to_views(_nn(h, w_ref[:, W:2 * W]), planes, (k1, k4, k16))
        _tile_to_views(_nn(h, w_ref[:, 2 * W:3 * W]), planes, (v1, v4, v16))
        dq_ref[...] = _nn(h, w_ref[:, 3 * W:4 * W])
        dk_ref[...] = _nn(h, w_ref[:, 4 * W:5 * W])
        dv_ref[...] = _nn(h, w_ref[:, 5 * W:6 * W])
        gate_ref[...] = _nn(h, w_ref[:, 6 * W:7 * W])
        bd_ref[...] = _nn(h, w_ref[:, 7 * W:7 * W + 128])

    tok = lambda w: pl.BlockSpec((tm, w), lambda i: (i, 0))
    return pl.pallas_call(
        body, name="inproj_fwd", grid=(S // tm,),
        in_specs=[tok(D), pl.BlockSpec((1, D), lambda i: (0, 0)),
                  pl.BlockSpec((D, IN_COLS_PADDED), lambda i: (0, 0))],
        out_specs=[tok(D)] + [_view_spec(d) for d in DILATIONS] * 3 + [tok(W)] * 4 + [tok(128)],
        out_shape=[jax.ShapeDtypeStruct((S, D), BF16)] + [_view_shape(S, d, BF16) for d in DILATIONS] * 3
                  + [jax.ShapeDtypeStruct((S, W), F32)] * 4 + [jax.ShapeDtypeStruct((S, 128), F32)],
        scratch_shapes=[pltpu.VMEM((4, tm, 128), F32)],
        compiler_params=_params(1),
    )(x, gain, w_in_p)


def _inproj_bwd(dxo, x, gain, attn_grads, dsecs, dbd, w_in_p):
    S, D = x.shape
    tm = VIEW_TILE // 2
    W = ATTN_WIDTH

    def body(dxo_ref, x_ref, gain_ref, *rest):
        views, (s3, s4, s5, s6, dbd_ref, w_ref, dx_ref, dgain_ref, dproj_ref, planes) = rest[:9], rest[9:]

        @pl.when(pl.program_id(0) == 0)
        def _():
            dgain_ref[...] = jnp.zeros_like(dgain_ref)

        secs = []
        for k in range(3):
            parts = [_view_to_tile(views[3 * k + p], d, planes) for p, d in enumerate(DILATIONS)]
            secs.append(parts[0] + parts[1] + parts[2])
        secs += [s3[...], s4[...], s5[...], s6[...]]
        dh = jnp.zeros((tm, D), F32)
        for k, s in enumerate(secs):
            d = s.astype(BF16)
            dproj_ref[:, k * W:(k + 1) * W] = d
            dh += _nt(d, w_ref[:, k * W:(k + 1) * W])
        d = dbd_ref[...].astype(BF16)
        dproj_ref[:, 7 * W:7 * W + 128] = d
        dh += _nt(d, w_ref[:, 7 * W:7 * W + 128])
        dx, dgain = _rmsnorm_bwd(dh, x_ref[...], gain_ref[...])
        dx_ref[...] = dxo_ref[...] + dx
        dgain_ref[...] += dgain

    tok = lambda w: pl.BlockSpec((tm, w), lambda i: (i, 0))
    return pl.pallas_call(
        body, name="inproj_bwd", grid=(S // tm,),
        in_specs=[tok(D), tok(D), pl.BlockSpec((1, D), lambda i: (0, 0))] + [_view_spec(d, tm) for d in DILATIONS] * 3
                 + [tok(W)] * 4 + [tok(128)] + [pl.BlockSpec((D, IN_COLS_PADDED), lambda i: (0, 0))],
        out_specs=[tok(D), pl.BlockSpec((1, D), lambda i: (0, 0)), tok(IN_COLS_PADDED)],
        out_shape=[jax.ShapeDtypeStruct((S, D), F32), jax.ShapeDtypeStruct((1, D), F32),
                   jax.ShapeDtypeStruct((S, IN_COLS_PADDED), BF16)],
        scratch_shapes=[pltpu.VMEM((4, tm, 128), F32)],
        compiler_params=_params(1),
    )(dxo, x, gain, *[g for grads in attn_grads for g in grads], *dsecs, dbd, w_in_p)


def _slope(h):
    return 2.0 ** (-8.0 * (h + 1) / ATTN_HEADS)


def _attn_fwd(q, k, v, d, name):
    L = q.shape[0]
    nb = L // ATTN_BLOCK
    B = ATTN_BLOCK

    def body(q_ref, kp_ref, kc_ref, vp_ref, vc_ref, acc_ref, m_ref, l_ref):
        n = pl.program_id(1)
        qi = lax.broadcasted_iota(jnp.int32, (B, 2 * B), 0)
        kj = lax.broadcasted_iota(jnp.int32, (B, 2 * B), 1)
        steps = qi + B - kj
        valid = (steps >= 0) & (steps <= B) & ((kj >= B) | (n > 0))
        stepsf = steps.astype(F32)
        lo = lax.broadcasted_iota(jnp.int32, (B, 128), 1) < 64
        for G in range(4):
            sl = slice(G * 128, (G + 1) * 128)
            qg = q_ref[:, sl]
            kg = jnp.concatenate([kp_ref[:, sl], kc_ref[:, sl]], axis=0)
            vg = jnp.concatenate([vp_ref[:, sl], vc_ref[:, sl]], axis=0)
            res = []
            for half in (0, 1):
                msk = lo if half == 0 else jnp.logical_not(lo)
                qm = jnp.where(msk, qg, jnp.zeros_like(qg))
                s = _nt(qm, kg)
                s = jnp.where(valid, s - (_slope(2 * G + half) * d) * stepsf, NEG_BIG)
                m = jnp.max(s, axis=-1, keepdims=True)
                p = jnp.exp(s - m)
                l = jnp.sum(p, axis=-1, keepdims=True)
                a = _nn(p.astype(BF16), vg)
                res.append((a, m, l))
            (a0, m0, l0), (a1, m1, l1) = res
            acc_ref[:, sl] = jnp.where(lo, a0, a1)
            m_ref[:, sl] = jnp.where(lo, m0, m1)
            l_ref[:, sl] = jnp.where(lo, l0, l1)

    cur = pl.BlockSpec((B, ATTN_WIDTH), lambda r, n: (n, r))
    prev = pl.BlockSpec((B, ATTN_WIDTH), lambda r, n: (jnp.maximum(n - 1, 0), r))
    return pl.pallas_call(
        body, name=name, grid=(d, nb),
        in_specs=[cur, prev, cur, prev, cur],
        out_specs=[cur, cur, cur],
        out_shape=[jax.ShapeDtypeStruct((L, d * ATTN_WIDTH), F32)] * 3,
        compiler_params=_params(2),
    )(q, k, k, v, v)


def _attn_merge(parts):
    S = parts[0][0].shape[0]
    tm = VIEW_TILE

    def body(a1, m1, l1, a2, m2, l2, a3, m3, l3, o_ref, lse1, lse4, lse16, planes):
        ins = ((a1, m1, l1), (a2, m2, l2), (a3, m3, l3))
        acc, ms, ls = [], [], []
        for d, (a, m, l) in zip(DILATIONS, ins):
            acc.append(_view_to_tile(a, d, planes))
            ms.append(_view_to_tile(m, d, planes))
            ls.append(_view_to_tile(l, d, planes))
        mx = jnp.maximum(jnp.maximum(ms[0], ms[1]), ms[2])
        es = [jnp.exp(m - mx) for m in ms]
        den = es[0] * ls[0] + es[1] * ls[1] + es[2] * ls[2]
        num = es[0] * acc[0] + es[1] * acc[1] + es[2] * acc[2]
        o_ref[...] = num / den
        _tile_to_views(mx + jnp.log(den), planes, (lse1, lse4, lse16))

    views = [_view_spec(d) for d in DILATIONS]
    flat = [t for p in parts for t in p]
    return pl.pallas_call(
        body, name="attn_merge", grid=(S // tm,),
        in_specs=[views[p] for p in range(3) for _ in range(3)],
        out_specs=[views[0]] + views,
        out_shape=[jax.ShapeDtypeStruct((S, ATTN_WIDTH), F32)] + [_view_shape(S, d, F32) for d in DILATIONS],
        scratch_shapes=[pltpu.VMEM((4, tm, 128), F32)],
        compiler_params=_params(1),
    )(*flat)


def _head_col(t, msk, big):
    if big:
        return jnp.max(jnp.where(msk, t, NEG_BIG), axis=-1, keepdims=True)
    return jnp.sum(jnp.where(msk, t, 0.0), axis=-1, keepdims=True) * (1.0 / 64.0)


def _attn_bwd_q(q, k, v, do, lse, dd, d, name):
    L = q.shape[0]
    nb = L // ATTN_BLOCK
    B = ATTN_BLOCK

    def body(q_ref, kp_ref, kc_ref, vp_ref, vc_ref, do_ref, lse_ref, dd_ref, dq_ref):
        n = pl.program_id(1)
        qi = lax.broadcasted_iota(jnp.int32, (B, 2 * B), 0)
        kj = lax.broadcasted_iota(jnp.int32, (B, 2 * B), 1)
        steps = qi + B - kj
        valid = (steps >= 0) & (steps <= B) & ((kj >= B) | (n > 0))
        stepsf = steps.astype(F32)
        lo = lax.broadcasted_iota(jnp.int32, (B, 128), 1) < 64
        for G in range(4):
            sl = slice(G * 128, (G + 1) * 128)
            qg = q_ref[:, sl]
            kg = jnp.concatenate([kp_ref[:, sl], kc_ref[:, sl]], axis=0)
            vg = jnp.concatenate([vp_ref[:, sl], vc_ref[:, sl]], axis=0)
            dog = do_ref[:, sl]
            res = []
            for half in (0, 1):
                msk = lo if half == 0 else jnp.logical_not(lo)
                qm = jnp.where(msk, qg, jnp.zeros_like(qg))
                s = _nt(qm, kg) - (_slope(2 * G + half) * d) * stepsf
                lse_c = _head_col(lse_ref[:, sl], msk, True)
                p = jnp.where(valid, jnp.exp(jnp.where(valid, s, NEG_BIG) - lse_c), 0.0)
                dom = jnp.where(msk, dog, 0.0).astype(BF16)
                dp = _nt(dom, vg)
                dcol = _head_col(dd_ref[:, sl], msk, False)
                ds = p * (dp - dcol)
                res.append(_nn(ds.astype(BF16), kg) * 0.125)
            dq_ref[:, sl] = jnp.where(lo, res[0], res[1])

    cur = pl.BlockSpec((B, ATTN_WIDTH), lambda r, n: (n, r))
    prev = pl.BlockSpec((B, ATTN_WIDTH), lambda r, n: (jnp.maximum(n - 1, 0), r))
    return pl.pallas_call(
        body, name=name, grid=(d, nb), in_specs=[cur, prev, cur, prev, cur, cur, cur, cur], out_specs=cur,
        out_shape=jax.ShapeDtypeStruct((L, d * ATTN_WIDTH), F32), compiler_params=_params(2),
    )(q, k, k, v, v, do, lse, dd)


def _attn_bwd_kv(q, k, v, do, lse, dd, d, name):
    L = q.shape[0]
    nb = L // ATTN_BLOCK
    B = ATTN_BLOCK

    def body(k_ref, v_ref, qc_ref, qn_ref, doc_ref, don_ref, lsec_ref, lsen_ref, ddc_ref, ddn_ref, dk_ref, dv_ref):
        j = pl.program_id(1)
        qrow = lax.broadcasted_iota(jnp.int32, (2 * B, B), 0)
        kk = lax.broadcasted_iota(jnp.int32, (2 * B, B), 1)
        steps = qrow - kk
        valid = (steps >= 0) & (steps <= B) & ((qrow < B) | (j < nb - 1))
        stepsf = steps.astype(F32)
        lo2 = lax.broadcasted_iota(jnp.int32, (2 * B, 128), 1) < 64
        lo = lax.broadcasted_iota(jnp.int32, (B, 128), 1) < 64
        for G in range(4):
            sl = slice(G * 128, (G + 1) * 128)
            kg = k_ref[:, sl]
            vg = v_ref[:, sl]
            qq = jnp.concatenate([qc_ref[:, sl], qn_ref[:, sl]], axis=0)
            doo = jnp.concatenate([doc_ref[:, sl], don_ref[:, sl]], axis=0)
            lse2 = jnp.concatenate([lsec_ref[:, sl], lsen_ref[:, sl]], axis=0)
            dd2 = jnp.concatenate([ddc_ref[:, sl], ddn_ref[:, sl]], axis=0)
            doo_b = doo.astype(BF16)
            dks, dvs = [], []
            for half in (0, 1):
                msk = lo2 if half == 0 else jnp.logical_not(lo2)
                qm = jnp.where(msk, qq, jnp.zeros_like(qq))
                s = _nt(qm, kg) - (_slope(2 * G + half) * d) * stepsf
                lse_c = _head_col(lse2, msk, True)
                p = jnp.where(valid, jnp.exp(jnp.where(valid, s, NEG_BIG) - lse_c), 0.0)
                dvs.append(_tn(p.astype(BF16), doo_b))
                dom = jnp.where(msk, doo, 0.0).astype(BF16)
                dp = _nt(dom, vg)
                dcol = _head_col(dd2, msk, False)
                ds = p * (dp - dcol)
                dks.append(_tn(ds.astype(BF16), qq))
            dk_ref[:, sl] = jnp.where(lo, dks[0], dks[1])
            dv_ref[:, sl] = jnp.where(lo, dvs[0], dvs[1])

    cur = pl.BlockSpec((B, ATTN_WIDTH), lambda r, j: (j, r))
    nxt = pl.BlockSpec((B, ATTN_WIDTH), lambda r, j: (jnp.minimum(j + 1, nb - 1), r))
    return pl.pallas_call(
        body, name=name, grid=(d, nb), in_specs=[cur, cur, cur, nxt, cur, nxt, cur, nxt, cur, nxt], out_specs=[cur, cur],
        out_shape=[jax.ShapeDtypeStruct((L, d * ATTN_WIDTH), F32)] * 2, compiler_params=_params(2),
    )(k, v, q, q, do, do, lse, lse, dd, dd)


CONV_T = 512
HALO = 8


def _conv_taps(pad_ref, w, T):
    acc = pad_ref[pl.ds(HALO - 3, T), :] * w[0:1, :]
    for j in range(1, CONV_WIDTH):
        acc = acc + pad_ref[pl.ds(HALO - 3 + j, T), :] * w[j:j + 1, :]
    return acc


def _conv_fwd(xq, xk, xv, conv_w):
    S = xq.shape[0]
    T = CONV_T

    def body(xq_ref, xqh_ref, xk_ref, xkh_ref, xv_ref, xvh_ref, wq_ref, wk_ref, wv_ref,
             qn_ref, kn_ref, v_ref, pad_ref):
        i = pl.program_id(0)

        def act(x_ref, xh_ref, w_ref):
            pad_ref[pl.ds(0, HALO), :] = jnp.where(i > 0, xh_ref[...], 0.0)
            pad_ref[pl.ds(HALO, T), :] = x_ref[...]
            c = _conv_taps(pad_ref, w_ref[...], T)
            return c * _sigmoid(c)

        def l2n(t):
            return t * lax.rsqrt(jnp.sum(t * t, axis=-1, keepdims=True) + L2_EPS)

        qn_ref[...] = l2n(act(xq_ref, xqh_ref, wq_ref))
        kn_ref[...] = l2n(act(xk_ref, xkh_ref, wk_ref))
        v_ref[...] = act(xv_ref, xvh_ref, wv_ref)

    tile = pl.BlockSpec((T, 128), lambda i, h: (i, h))
    halo = pl.BlockSpec((HALO, 128), lambda i, h: (jnp.maximum(i * (T // HALO) - 1, 0), h))
    wspec = lambda sec: pl.BlockSpec((CONV_WIDTH, 128), lambda i, h, sec=sec: (0, 4 * sec + h))
    return pl.pallas_call(
        body, name="dn_conv_fwd", grid=(S // T, DN_HEADS),
        in_specs=[tile, halo, tile, halo, tile, halo, wspec(0), wspec(1), wspec(2)],
        out_specs=[tile, tile, tile],
        out_shape=[jax.ShapeDtypeStruct((S, DN_WIDTH), F32)] * 3,
        scratch_shapes=[pltpu.VMEM((T + HALO, 128), F32)],
        compiler_params=_params(2),
    )(xq, xq, xk, xk, xv, xv, conv_w, conv_w, conv_w)


def _conv_bwd_pre(xq, xk, xv, conv_w, dqn, dkn, dv):
    S = xq.shape[0]
    T = CONV_T

    def body(xq_ref, xqh_ref, xk_ref, xkh_ref, xv_ref, xvh_ref, wq_ref, wk_ref, wv_ref,
             dqn_ref, dkn_ref, dv_ref, dcq_ref, dck_ref, dcv_ref, dwq_ref, dwk_ref, dwv_ref, pad_ref):
        i = pl.program_id(1)

        def one(x_ref, xh_ref, w_ref, dy_ref, dc_ref, dw_ref, normed):
            pad_ref[pl.ds(0, HALO), :] = jnp.where(i > 0, xh_ref[...], 0.0)
            pad_ref[pl.ds(HALO, T), :] = x_ref[...]
            c = _conv_taps(pad_ref, w_ref[...], T)
            sg = _sigmoid(c)
            a = c * sg
            dy = dy_ref[...]
            if normed:
                r = lax.rsqrt(jnp.sum(a * a, axis=-1, keepdims=True) + L2_EPS)
                y = a * r
                da = r * (dy - y * jnp.sum(dy * y, axis=-1, keepdims=True))
            else:
                da = dy
            dc = da * (sg * (1.0 + c * (1.0 - sg)))
            dc_ref[...] = dc

            @pl.when(i == 0)
            def _():
                dw_ref[...] = jnp.zeros_like(dw_ref)

            rows = [jnp.sum(dc * pad_ref[pl.ds(HALO - 3 + j, T), :], axis=0, keepdims=True) for j in range(CONV_WIDTH)]
            dw_ref[...] += jnp.concatenate(rows + [jnp.zeros((8 - CONV_WIDTH, 128), F32)], axis=0)

        one(xq_ref, xqh_ref, wq_ref, dqn_ref, dcq_ref, dwq_ref, True)
        one(xk_ref, xkh_ref, wk_ref, dkn_ref, dck_ref, dwk_ref, True)
        one(xv_ref, xvh_ref, wv_ref, dv_ref, dcv_ref, dwv_ref, False)

    tile = pl.BlockSpec((T, 128), lambda h, i: (i, h))
    halo = pl.BlockSpec((HALO, 128), lambda h, i: (jnp.maximum(i * (T // HALO) - 1, 0), h))
    wspec = lambda sec: pl.BlockSpec((CONV_WIDTH, 128), lambda h, i, sec=sec: (0, 4 * sec + h))
    dwspec = pl.BlockSpec((8, 128), lambda h, i: (0, h))
    return pl.pallas_call(
        body, name="dn_conv_bwd_pre", grid=(DN_HEADS, S // T),
        in_specs=[tile, halo, tile, halo, tile, halo, wspec(0), wspec(1), wspec(2), tile, tile, tile],
        out_specs=[tile, tile, tile, dwspec, dwspec, dwspec],
        out_shape=[jax.ShapeDtypeStruct((S, DN_WIDTH), F32)] * 3 + [jax.ShapeDtypeStruct((8, DN_WIDTH), F32)] * 3,
        scratch_shapes=[pltpu.VMEM((T + HALO, 128), F32)],
        compiler_params=_params(2),
    )(xq, xq, xk, xk, xv, xv, conv_w, conv_w, conv_w, dqn, dkn, dv)


def _conv_bwd_x(dcq, dck, dcv, conv_w):
    S = dcq.shape[0]
    T = CONV_T
    nt = S // T

    def body(dq_ref, dqh_ref, dk_ref, dkh_ref, dv_ref, dvh_ref, wq_ref, wk_ref, wv_ref,
             oq_ref, ok_ref, ov_ref, pad_ref):
        i = pl.program_id(0)

        def one(d_ref, dh_ref, w_ref, o_ref):
            pad_ref[pl.ds(0, T), :] = d_ref[...]
            pad_ref[pl.ds(T, HALO), :] = jnp.where(i < nt - 1, dh_ref[...], 0.0)
            w = w_ref[...]
            acc = pad_ref[pl.ds(3, T), :] * w[0:1, :]
            for j in range(1, CONV_WIDTH):
                acc = acc + pad_ref[pl.ds(3 - j, T), :] * w[j:j + 1, :]
            o_ref[...] = acc

        one(dq_ref, dqh_ref, wq_ref, oq_ref)
        one(dk_ref, dkh_ref, wk_ref, ok_ref)
        one(dv_ref, dvh_ref, wv_ref, ov_ref)

    tile = pl.BlockSpec((T, 128), lambda i, h: (i, h))
    halo = pl.BlockSpec((HALO, 128), lambda i, h: (jnp.minimum((i + 1) * (T // HALO), S // HALO - 1), h))
    wspec = lambda sec: pl.BlockSpec((CONV_WIDTH, 128), lambda i, h, sec=sec: (0, 4 * sec + h))
    return pl.pallas_call(
        body, name="dn_conv_bwd_x", grid=(nt, DN_HEADS),
        in_specs=[tile, halo, tile, halo, tile, halo, wspec(0), wspec(1), wspec(2)],
        out_specs=[tile, tile, tile],
        out_shape=[jax.ShapeDtypeStruct((S, DN_WIDTH), F32)] * 3,
        scratch_shapes=[pltpu.VMEM((T + HALO, 128), F32)],
        compiler_params=_params(2),
    )(dcq, dcq, dck, dck, dcv, dcv, conv_w, conv_w, conv_w)


def _tri_inverse(a, blk, eye):
    mm = functools.partial(_nn, precision=HI)
    dg = jnp.where(blk, a, 0.0)
    lo = a - dg
    d2 = mm(dg, dg)
    d4 = mm(d2, d2)
    d8 = mm(d4, d4)
    td = mm(mm(mm(eye - dg, eye + d2), eye + d4), eye + d8)
    b = mm(td, lo)
    b2 = mm(b, b)
    tb = mm(eye - b, eye + b2)
    return mm(tb, td)


def _dn_chunk_common(bd, avec, dvec, h, q_raw, k, v, t=None):
    C = DN_CHUNK
    lane = lax.broadcasted_iota(jnp.int32, (C, 128), 1)
    row = lax.broadcasted_iota(jnp.int32, (C, C), 0)
    col = lax.broadcasted_iota(jnp.int32, (C, C), 1)
    incl = row >= col
    strict = row > col
    eye = (row == col).astype(F32)
    blk = (row // 16) == (col // 16)
    beta_all = _sigmoid(bd)
    z = bd + dvec
    sp = jnp.maximum(z, 0.0) + jnp.log(1.0 + jnp.exp(-jnp.abs(z)))
    g_all = -jnp.exp(avec) * sp
    pick = lambda t, ln: jnp.sum(jnp.where(lane == ln, t, 0.0), axis=-1, keepdims=True)
    beta = pick(beta_all, h)
    graw = pick(g_all, DN_HEADS + h)
    zc = pick(z, DN_HEADS + h)
    to_row = lambda c: jnp.sum(eye * c, axis=0, keepdims=True)
    gc = jnp.sum(jnp.where(incl, to_row(graw), 0.0), axis=-1, keepdims=True)
    gc_row = to_row(gc)
    decay = jnp.exp(jnp.where(incl, gc - gc_row, NEG_BIG))
    q = q_raw * (DN_HEAD_DIM ** -0.5)
    kb = k * beta
    kk = _nt(kb, k, HI)
    if t is None:
        t = _tri_inverse(jnp.where(strict, kk * decay, 0.0), blk, eye)
    eg = jnp.exp(gc)
    rhs_u = v * beta
    rhs_w = kb * eg
    u = _nn(t, rhs_u, HI)
    w = _nn(t, rhs_w, HI)
    qk = _nt(q, k, HI)
    aq = jnp.where(incl, qk * decay, 0.0)
    g_last = jnp.sum(jnp.where(lax.broadcasted_iota(jnp.int32, (C, 1), 0) == C - 1, gc, 0.0), axis=0, keepdims=True)
    ekd = jnp.exp(g_last - gc)
    return dict(beta=beta, graw=graw, zc=zc, gc=gc, decay=decay, q=q, kb=kb, kk=kk, t=t, eg=eg, rhs_w=rhs_w,
                u=u, w=w, qk=qk, aq=aq, g_last=g_last, ekd=ekd, kd=k * ekd, qg=q * eg,
                incl=incl, strict=strict, eye=eye, lane=lane, row=row, col=col)


PREP_CHUNKS = 2
SCAN_CHUNKS = 8


def _dn_prep(qn, kn, v, bd, avec, dvec):
    S = qn.shape[0]
    C = DN_CHUNK
    N = S // C
    HD = DN_HEAD_DIM
    nc = PREP_CHUNKS

    def body(q_ref, k_ref, v_ref, bd_ref, a_ref, d_ref, u_ref, w_ref, qg_ref, kd_ref, aq_ref, t_ref, egl_ref):
        for ci in range(nc):
            rows = slice(ci * C, (ci + 1) * C)
            bd = bd_ref[rows, :]
            egl = []
            for h in range(DN_HEADS):
                sl = slice(h * HD, (h + 1) * HD)
                c = _dn_chunk_common(bd, a_ref[...], d_ref[...], h, q_ref[rows, sl], k_ref[rows, sl], v_ref[rows, sl])
                u_ref[rows, sl] = c["u"]
                w_ref[rows, sl] = c["w"]
                qg_ref[rows, sl] = c["qg"]
                kd_ref[rows, sl] = c["kd"]
                aq_ref[h, rows, :] = c["aq"]
                t_ref[h, rows, :] = c["t"]
                egl.append(jnp.broadcast_to(jnp.exp(c["g_last"]), (1, 128)))
            egl_ref[ci * 8:(ci + 1) * 8, :] = jnp.concatenate(egl + [jnp.zeros((8 - DN_HEADS, 128), F32)], axis=0)

    tok = lambda w: pl.BlockSpec((nc * C, w), lambda n: (n, 0))
    sq = pl.BlockSpec((DN_HEADS, nc * C, C), lambda n: (0, n, 0))
    vec = pl.BlockSpec((1, 128), lambda n: (0, 0))
    return pl.pallas_call(
        body, name="dn_prep", grid=(N // nc,),
        in_specs=[tok(DN_WIDTH)] * 3 + [tok(128), vec, vec],
        out_specs=[tok(DN_WIDTH)] * 4 + [sq, sq, pl.BlockSpec((nc * 8, 128), lambda n: (n, 0))],
        out_shape=[jax.ShapeDtypeStruct((S, DN_WIDTH), F32)] * 4 + [jax.ShapeDtypeStruct((DN_HEADS, S, C), F32)] * 2
                  + [jax.ShapeDtypeStruct((N * 8, 128), F32)],
        compiler_params=_params(1),
    )(qn, kn, v, bd, avec, dvec)


def _dn_scan_fwd(u, w, qg, kd, aq, egl, gate, dn_gain):
    S = u.shape[0]
    C = DN_CHUNK
    N = S // C
    HD = DN_HEAD_DIM
    nc = SCAN_CHUNKS

    def body(u_ref, w_ref, qg_ref, kd_ref, aq_ref, egl_ref, gate_ref, gain_ref, dn_ref, o_ref, vn_ref, st_ref, state_ref):
        @pl.when(pl.program_id(0) == 0)
        def _():
            state_ref[...] = jnp.zeros_like(state_ref)

        for ci in range(nc):
            rows = slice(ci * C, (ci + 1) * C)
            st_ref[ci * DN_WIDTH:(ci + 1) * DN_WIDTH, :] = state_ref[...]
            for h in range(DN_HEADS):
                sl = slice(h * HD, (h + 1) * HD)
                st = state_ref[sl, :]
                v_new = u_ref[rows, sl] - _nn(w_ref[rows, sl], st, HI)
                o = _nn(qg_ref[rows, sl], st, HI) + _nn(aq_ref[h, rows, :], v_new, HI)
                state_ref[sl, :] = st * egl_ref[ci * 8 + h:ci * 8 + h + 1, :] + _tn(kd_ref[rows, sl], v_new, HI)
                vn_ref[rows, sl] = v_new
                o_ref[rows, sl] = o
                r = lax.rsqrt(jnp.mean(o * o, axis=-1, keepdims=True) + NORM_EPS)
                gt = gate_ref[rows, sl]
                dn_ref[rows, sl] = o * r * gain_ref[...] * (gt * _sigmoid(gt))

    tok = lambda wd: pl.BlockSpec((nc * C, wd), lambda n: (n, 0))
    sq = pl.BlockSpec((DN_HEADS, nc * C, C), lambda n: (0, n, 0))
    vec = pl.BlockSpec((1, 128), lambda n: (0, 0))
    return pl.pallas_call(
        body, name="dn_scan_fwd", grid=(N // nc,),
        in_specs=[tok(DN_WIDTH)] * 4 + [sq, pl.BlockSpec((nc * 8, 128), lambda n: (n, 0)), tok(DN_WIDTH), vec],
        out_specs=[tok(DN_WIDTH)] * 3 + [pl.BlockSpec((nc * DN_WIDTH, HD), lambda n: (n, 0))],
        out_shape=[jax.ShapeDtypeStruct((S, DN_WIDTH), F32)] * 3 + [jax.ShapeDtypeStruct((N * DN_WIDTH, HD), F32)],
        scratch_shapes=[pltpu.VMEM((DN_WIDTH, HD), F32)],
        compiler_params=_params(1),
    )(u, w, qg, kd, aq, egl, gate, dn_gain)


def _dn_scan_bwd(w, qg, kd, aq, egl, gate, dn_gain, o, ddn):
    S = w.shape[0]
    C = DN_CHUNK
    N = S // C
    HD = DN_HEAD_DIM
    nc = SCAN_CHUNKS

    def body(w_ref, qg_ref, kd_ref, aq_ref, egl_ref, gate_ref, gain_ref, o_ref, ddn_ref,
             do_ref, dvn_ref, dgate_ref, dst_ref, small_ref, dstate_ref):
        @pl.when(pl.program_id(0) == 0)
        def _():
            dstate_ref[...] = jnp.zeros_like(dstate_ref)
            small_ref[...] = jnp.zeros_like(small_ref)

        gain = gain_ref[...]
        d_gain = jnp.zeros((1, 128), F32)
        for ci in reversed(range(nc)):
            rows = slice(ci * C, (ci + 1) * C)
            dst_ref[ci * DN_WIDTH:(ci + 1) * DN_WIDTH, :] = dstate_ref[...]
            for h in range(DN_HEADS):
                sl = slice(h * HD, (h + 1) * HD)
                ov = o_ref[rows, sl]
                r = lax.rsqrt(jnp.mean(ov * ov, axis=-1, keepdims=True) + NORM_EPS)
                on = ov * r
                gt = gate_ref[rows, sl]
                sgt = _sigmoid(gt)
                silu_g = gt * sgt
                dy = ddn_ref[rows, sl]
                d_gain = d_gain + jnp.sum(dy * on * silu_g, axis=0, keepdims=True)
                dgate_ref[rows, sl] = dy * on * gain * (sgt * (1.0 + gt * (1.0 - sgt)))
                don = dy * gain * silu_g
                do = r * (don - on * jnp.mean(don * on, axis=-1, keepdims=True))
                do_ref[rows, sl] = do
                dsn = dstate_ref[sl, :]
                d_vnew = _tn(aq_ref[h, rows, :], do, HI) + _nn(kd_ref[rows, sl], dsn, HI)
                dvn_ref[rows, sl] = d_vnew
                dstate_ref[sl, :] = (_tn(qg_ref[rows, sl], do, HI) + dsn * egl_ref[ci * 8 + h:ci * 8 + h + 1, :]
                                     - _tn(w_ref[rows, sl], d_vnew, HI))
        small_ref[...] += jnp.concatenate([d_gain, jnp.zeros((7, 128), F32)], axis=0)

    nb = N // nc
    tok = lambda wd: pl.BlockSpec((nc * C, wd), lambda i: (nb - 1 - i, 0))
    sq = pl.BlockSpec((DN_HEADS, nc * C, C), lambda i: (0, nb - 1 - i, 0))
    vec = pl.BlockSpec((1, 128), lambda i: (0, 0))
    return pl.pallas_call(
        body, name="dn_scan_bwd", grid=(nb,),
        in_specs=[tok(DN_WIDTH)] * 3 + [sq, pl.BlockSpec((nc * 8, 128), lambda i: (nb - 1 - i, 0)), tok(DN_WIDTH), vec,
                                       tok(DN_WIDTH), tok(DN_WIDTH)],
        out_specs=[tok(DN_WIDTH)] * 3 + [pl.BlockSpec((nc * DN_WIDTH, HD), lambda i: (nb - 1 - i, 0)),
                                        pl.BlockSpec((8, 128), lambda i: (0, 0))],
        out_shape=[jax.ShapeDtypeStruct((S, DN_WIDTH), F32)] * 3 + [jax.ShapeDtypeStruct((N * DN_WIDTH, HD), F32),
                                                                  jax.ShapeDtypeStruct((8, 128), F32)],
        scratch_shapes=[pltpu.VMEM((DN_WIDTH, HD), F32)],
        compiler_params=_params(1),
    )(w, qg, kd, aq, egl, gate, dn_gain, o, ddn)


def _dn_post(qn, kn, v, bd, avec, dvec, t_inv, v_new_all, states, dstates, do_all, dvn_all):
    S = qn.shape[0]
    C = DN_CHUNK
    N = S // C
    HD = DN_HEAD_DIM
    nc = PREP_CHUNKS

    def body(q_ref, k_ref, v_ref, bd_ref, a_ref, d_ref, t_ref, vn_ref, st_ref, dst_ref, do_ref, dvn_ref,
             dq_ref, dk_ref, dv_ref, dbd_ref, small_ref):
        @pl.when(pl.program_id(0) == 0)
        def _():
            small_ref[...] = jnp.zeros_like(small_ref)

        for ci in range(nc):
            tok = lambda r: r.at[pl.ds(ci * C, C)]
            big = lambda r: r.at[pl.ds(ci * DN_WIDTH, DN_WIDTH)]
            chunk(tok(q_ref), tok(k_ref), tok(v_ref), tok(bd_ref), a_ref, d_ref, t_ref.at[:, pl.ds(ci * C, C)], tok(vn_ref),
                  big(st_ref), big(dst_ref), tok(do_ref), tok(dvn_ref), tok(dq_ref), tok(dk_ref), tok(dv_ref), tok(dbd_ref),
                  small_ref)

    def chunk(q_ref, k_ref, v_ref, bd_ref, a_ref, d_ref, t_ref, vn_ref, st_ref, dstate_ref, do_ref, dvn_ref,
              dq_ref, dk_ref, dv_ref, dbd_ref, small_ref):
        bd = bd_ref[...]
        avec = a_ref[...]
        dbd = jnp.zeros((C, 128), F32)
        d_alog = jnp.zeros((1, 128), F32)
        d_dt = jnp.zeros((1, 128), F32)
        for h in range(DN_HEADS):
            sl = slice(h * HD, (h + 1) * HD)
            k = k_ref[:, sl]
            vv = v_ref[:, sl]
            c = _dn_chunk_common(bd, avec, d_ref[...], h, q_ref[:, sl], k, vv, t=t_ref[h])
            q, kb, t, eg, u, w = c["q"], c["kb"], c["t"], c["eg"], c["u"], c["w"]
            beta, decay, incl, strict, eye = c["beta"], c["decay"], c["incl"], c["strict"], c["eye"]
            lane = c["lane"]
            st = st_ref[sl, :]
            dsn = dstate_ref[sl, :]
            v_new = vn_ref[:, sl]
            do = do_ref[:, sl]
            d_vnew = dvn_ref[:, sl]
            egl = jnp.exp(c["g_last"])
            daq = jnp.where(incl, _nt(do, v_new, HI), 0.0)
            d_qg = _nt(do, st, HI)
            d_kd = _nt(v_new, dsn, HI)
            d_glast = jnp.sum(jnp.sum(dsn * st, axis=-1, keepdims=True), axis=0, keepdims=True) * egl
            d_w = -_nt(d_vnew, st, HI)
            d_ru = _tn(t, d_vnew, HI)
            d_rw = _tn(t, d_w, HI)
            da = -jnp.where(strict, _nt(d_ru, u, HI) + _nt(d_rw, w, HI), 0.0)
            dv_ref[:, sl] = d_ru * beta
            dbeta = jnp.sum(d_ru * vv, axis=-1, keepdims=True)
            dkb = d_rw * eg
            dgc = jnp.sum(d_rw * c["rhs_w"], axis=-1, keepdims=True)
            dkk = da * decay
            ddecay = da * c["kk"]
            dkb = dkb + _nn(dkk, k, HI)
            dk = _tn(dkk, kb, HI)
            dqk = daq * decay
            ddecay = ddecay + daq * c["qk"]
            dq = _nn(dqk, k, HI)
            dk = dk + _tn(dqk, q, HI)
            m = ddecay * decay
            col_sum = jnp.sum(m, axis=0, keepdims=True)
            dgc = dgc + jnp.sum(m, axis=-1, keepdims=True) - jnp.sum(eye * col_sum, axis=-1, keepdims=True)
            dq = dq + d_qg * eg
            dgc = dgc + jnp.sum(d_qg * c["qg"], axis=-1, keepdims=True)
            dk = dk + d_kd * c["ekd"]
            tk = jnp.sum(d_kd * c["kd"], axis=-1, keepdims=True)
            dgc = dgc - tk
            d_glast = d_glast + jnp.sum(tk, axis=0, keepdims=True)
            dk = dk + dkb * beta
            dbeta = dbeta + jnp.sum(dkb * k, axis=-1, keepdims=True)
            dgc = dgc + jnp.where(lax.broadcasted_iota(jnp.int32, (C, 1), 0) == C - 1, d_glast, 0.0)
            dgc_row = jnp.sum(eye * dgc, axis=0, keepdims=True)
            dgraw = jnp.sum(jnp.where(c["col"] >= c["row"], dgc_row, 0.0), axis=-1, keepdims=True)
            dq_ref[:, sl] = dq * (HD ** -0.5)
            dk_ref[:, sl] = dk
            dbraw = dbeta * beta * (1.0 - beta)
            dz = dgraw * (-jnp.exp(avec)) * _sigmoid(c["zc"])
            dbd = dbd + jnp.where(lane == h, dbraw, 0.0) + jnp.where(lane == DN_HEADS + h, dz, 0.0)
            lane1 = lax.broadcasted_iota(jnp.int32, (1, 128), 1)
            d_alog = d_alog + jnp.where(lane1 == DN_HEADS + h, jnp.sum(dgraw * c["graw"], axis=0, keepdims=True), 0.0)
            d_dt = d_dt + jnp.where(lane1 == DN_HEADS + h, jnp.sum(dz, axis=0, keepdims=True), 0.0)
        dbd_ref[...] = dbd
        small_ref[...] += jnp.concatenate([d_alog, d_dt, jnp.zeros((6, 128), F32)], axis=0)

    tok = lambda wd: pl.BlockSpec((nc * C, wd), lambda n: (n, 0))
    big = pl.BlockSpec((nc * DN_WIDTH, HD), lambda n: (n, 0))
    sq = pl.BlockSpec((DN_HEADS, nc * C, C), lambda n: (0, n, 0))
    vec = pl.BlockSpec((1, 128), lambda n: (0, 0))
    return pl.pallas_call(
        body, name="dn_post", grid=(N // nc,),
        in_specs=[tok(DN_WIDTH)] * 3 + [tok(128), vec, vec, sq, tok(DN_WIDTH), big, big, tok(DN_WIDTH), tok(DN_WIDTH)],
        out_specs=[tok(DN_WIDTH)] * 3 + [tok(128), pl.BlockSpec((8, 128), lambda n: (0, 0))],
        out_shape=[jax.ShapeDtypeStruct((S, DN_WIDTH), F32)] * 3 + [jax.ShapeDtypeStruct((S, 128), F32),
                                                                  jax.ShapeDtypeStruct((8, 128), F32)],
        compiler_params=_params(1),
    )(qn, kn, v, bd, avec, dvec, t_inv, v_new_all, states, dstates, do_all, dvn_all)


def _bnn(a, b):
    return lax.dot_general(a, b, (((2,), (1,)), ((0,), (0,))), preferred_element_type=F32, precision=HI)


def _bnt(a, b):
    return lax.dot_general(a, b, (((2,), (2,)), ((0,), (0,))), preferred_element_type=F32, precision=HI)


def _btn(a, b):
    return lax.dot_general(a, b, (((1,), (1,)), ((0,), (0,))), preferred_element_type=F32, precision=HI)


def _tri_inverse_b(a, blk, eye):
    dg = jnp.where(blk, a, 0.0)
    lo = a - dg
    d2 = _bnn(dg, dg)
    d4 = _bnn(d2, d2)
    d8 = _bnn(d4, d4)
    td = _bnn(_bnn(_bnn(eye - dg, eye + d2), eye + d4), eye + d8)
    b = _bnn(td, lo)
    b2 = _bnn(b, b)
    return _bnn(_bnn(eye - b, eye + b2), td)


def _dn_common_b(bds, avec, dvec, q_raw, k, v, t=None):
    C = DN_CHUNK
    lane = lax.broadcasted_iota(jnp.int32, (C, 128), 1)
    row = lax.broadcasted_iota(jnp.int32, (1, C, C), 1)
    col = lax.broadcasted_iota(jnp.int32, (1, C, C), 2)
    incl = row >= col
    strict = row > col
    eye = (row == col).astype(F32)
    blk = (row // 16) == (col // 16)
    pick = lambda tile, ln: jnp.sum(jnp.where(lane == ln, tile, 0.0), axis=-1, keepdims=True)
    betas, graws, zcs = [], [], []
    for bd in bds:
        z = bd + dvec
        g_all = -jnp.exp(avec) * (jnp.maximum(z, 0.0) + jnp.log(1.0 + jnp.exp(-jnp.abs(z))))
        beta_all = _sigmoid(bd)
        for h in range(DN_HEADS):
            betas.append(pick(beta_all, h))
            graws.append(pick(g_all, DN_HEADS + h))
            zcs.append(pick(z, DN_HEADS + h))
    beta, graw, zc = jnp.stack(betas), jnp.stack(graws), jnp.stack(zcs)
    to_row = lambda c: jnp.sum(eye * c, axis=1, keepdims=True)
    gc = jnp.sum(jnp.where(incl, to_row(graw), 0.0), axis=-1, keepdims=True)
    decay = jnp.exp(jnp.where(incl, gc - to_row(gc), NEG_BIG))
    q = q_raw * (DN_HEAD_DIM ** -0.5)
    kb = k * beta
    kk = _bnt(kb, k)
    if t is None:
        t = _tri_inverse_b(jnp.where(strict, kk * decay, 0.0), blk, eye)
    eg = jnp.exp(gc)
    rhs_w = kb * eg
    u = _bnn(t, v * beta)
    w = _bnn(t, rhs_w)
    qk = _bnt(q, k)
    aq = jnp.where(incl, qk * decay, 0.0)
    last = lax.broadcasted_iota(jnp.int32, (1, C, 1), 1) == C - 1
    g_last = jnp.sum(jnp.where(last, gc, 0.0), axis=1, keepdims=True)
    ekd = jnp.exp(g_last - gc)
    return dict(beta=beta, graw=graw, zc=zc, gc=gc, decay=decay, q=q, kb=kb, kk=kk, t=t, eg=eg, rhs_w=rhs_w,
                u=u, w=w, qk=qk, aq=aq, g_last=g_last, ekd=ekd, kd=k * ekd, qg=q * eg,
                incl=incl, strict=strict, eye=eye, lane=lane, row=row, col=col, last=last)


def _stack_heads(ref, rows):
    return jnp.stack([ref[rows, h * DN_HEAD_DIM:(h + 1) * DN_HEAD_DIM] for h in range(DN_HEADS)])


def _stack_units(ref, nc):
    C = DN_CHUNK
    return jnp.concatenate([_stack_heads(ref, slice(ci * C, (ci + 1) * C)) for ci in range(nc)], axis=0)


def _store_units(ref, val, nc):
    C = DN_CHUNK
    for ci in range(nc):
        for h in range(DN_HEADS):
            ref[ci * C:(ci + 1) * C, h * DN_HEAD_DIM:(h + 1) * DN_HEAD_DIM] = val[ci * DN_HEADS + h]


def _dn_prep(qn, kn, v, bd, avec, dvec):
    S = qn.shape[0]
    C = DN_CHUNK
    N = S // C
    nc = PREP_CHUNKS

    def body(q_ref, k_ref, v_ref, bd_ref, a_ref, d_ref, u_ref, w_ref, qg_ref, kd_ref, aq_ref, t_ref, egl_ref):
        bds = [bd_ref[ci * C:(ci + 1) * C, :] for ci in range(nc)]
        c = _dn_common_b(bds, a_ref[...], d_ref[...], _stack_units(q_ref, nc), _stack_units(k_ref, nc), _stack_units(v_ref, nc))
        _store_units(u_ref, c["u"], nc)
        _store_units(w_ref, c["w"], nc)
        _store_units(qg_ref, c["qg"], nc)
        _store_units(kd_ref, c["kd"], nc)
        egl = jnp.broadcast_to(jnp.exp(c["g_last"]), (nc * DN_HEADS, 1, 128))
        for ci in range(nc):
            for h in range(DN_HEADS):
                aq_ref[h, ci * C:(ci + 1) * C, :] = c["aq"][ci * DN_HEADS + h]
                t_ref[h, ci * C:(ci + 1) * C, :] = c["t"][ci * DN_HEADS + h]
            egl_ref[ci * 8:(ci + 1) * 8, :] = jnp.concatenate(
                [egl[ci * DN_HEADS + h] for h in range(DN_HEADS)] + [jnp.zeros((8 - DN_HEADS, 128), F32)], axis=0)

    tok = lambda w: pl.BlockSpec((nc * C, w), lambda n: (n, 0))
    sq = pl.BlockSpec((DN_HEADS, nc * C, C), lambda n: (0, n, 0))
    vec = pl.BlockSpec((1, 128), lambda n: (0, 0))
    return pl.pallas_call(
        body, name="dn_prep", grid=(N // nc,),
        in_specs=[tok(DN_WIDTH)] * 3 + [tok(128), vec, vec],
        out_specs=[tok(DN_WIDTH)] * 4 + [sq, sq, pl.BlockSpec((nc * 8, 128), lambda n: (n, 0))],
        out_shape=[jax.ShapeDtypeStruct((S, DN_WIDTH), F32)] * 4 + [jax.ShapeDtypeStruct((DN_HEADS, S, C), F32)] * 2
                  + [jax.ShapeDtypeStruct((N * 8, 128), F32)],
        compiler_params=_params(1),
    )(qn, kn, v, bd, avec, dvec)


def _dn_scan_fwd(u, w, qg, kd, aq, egl, gate, dn_gain):
    S = u.shape[0]
    C = DN_CHUNK
    N = S // C
    HD = DN_HEAD_DIM
    nc = SCAN_CHUNKS

    def body(u_ref, w_ref, qg_ref, kd_ref, aq_ref, egl_ref, gate_ref, gain_ref, dn_ref, o_ref, vn_ref, st_ref, state_ref):
        @pl.when(pl.program_id(0) == 0)
        def _():
            state_ref[...] = jnp.zeros_like(state_ref)

        gain = gain_ref[...]
        for ci in range(nc):
            rows = slice(ci * C, (ci + 1) * C)
            st = state_ref[...]
            for h in range(DN_HEADS):
                st_ref[ci * DN_WIDTH + h * HD:ci * DN_WIDTH + (h + 1) * HD, :] = st[h]
            v_new = _stack_heads(u_ref, rows) - _bnn(_stack_heads(w_ref, rows), st)
            o = _bnn(_stack_heads(qg_ref, rows), st) + _bnn(aq_ref[:, rows, :], v_new)
            egl = jnp.stack([egl_ref[ci * 8 + h:ci * 8 + h + 1, :] for h in range(DN_HEADS)])
            state_ref[...] = st * egl + _btn(_stack_heads(kd_ref, rows), v_new)
            r = lax.rsqrt(jnp.mean(o * o, axis=-1, keepdims=True) + NORM_EPS)
            gt = _stack_heads(gate_ref, rows)
            dn = o * r * gain * (gt * _sigmoid(gt))
            for h in range(DN_HEADS):
                sl = slice(h * HD, (h + 1) * HD)
                vn_ref[rows, sl] = v_new[h]
                o_ref[rows, sl] = o[h]
                dn_ref[rows, sl] = dn[h]

    tok = lambda wd: pl.BlockSpec((nc * C, wd), lambda n: (n, 0))
    sq = pl.BlockSpec((DN_HEADS, nc * C, C), lambda n: (0, n, 0))
    vec = pl.BlockSpec((1, 128), lambda n: (0, 0))
    return pl.pallas_call(
        body, name="dn_scan_fwd", grid=(N // nc,),
        in_specs=[tok(DN_WIDTH)] * 4 + [sq, pl.BlockSpec((nc * 8, 128), lambda n: (n, 0)), tok(DN_WIDTH), vec],
        out_specs=[tok(DN_WIDTH)] * 3 + [pl.BlockSpec((nc * DN_WIDTH, HD), lambda n: (n, 0))],
        out_shape=[jax.ShapeDtypeStruct((S, DN_WIDTH), F32)] * 3 + [jax.ShapeDtypeStruct((N * DN_WIDTH, HD), F32)],
        scratch_shapes=[pltpu.VMEM((DN_HEADS, HD, HD), F32)],
        compiler_params=_params(1),
    )(u, w, qg, kd, aq, egl, gate, dn_gain)


def _dn_scan_bwd(w, qg, kd, aq, egl, gate, dn_gain, o, ddn):
    S = w.shape[0]
    C = DN_CHUNK
    N = S // C
    HD = DN_HEAD_DIM
    nc = SCAN_CHUNKS

    def body(w_ref, qg_ref, kd_ref, aq_ref, egl_ref, gate_ref, gain_ref, o_ref, ddn_ref,
             do_ref, dvn_ref, dgate_ref, dst_ref, small_ref, dstate_ref):
        @pl.when(pl.program_id(0) == 0)
        def _():
            dstate_ref[...] = jnp.zeros_like(dstate_ref)
            small_ref[...] = jnp.zeros_like(small_ref)

        gain = gain_ref[...]
        d_gain = jnp.zeros((1, 128), F32)
        for ci in reversed(range(nc)):
            rows = slice(ci * C, (ci + 1) * C)
            dsn = dstate_ref[...]
            for h in range(DN_HEADS):
                dst_ref[ci * DN_WIDTH + h * HD:ci * DN_WIDTH + (h + 1) * HD, :] = dsn[h]
            ov = _stack_heads(o_ref, rows)
            r = lax.rsqrt(jnp.mean(ov * ov, axis=-1, keepdims=True) + NORM_EPS)
            on = ov * r
            gt = _stack_heads(gate_ref, rows)
            sgt = _sigmoid(gt)
            silu_g = gt * sgt
            dy = _stack_heads(ddn_ref, rows)
            d_gain = d_gain + jnp.sum(jnp.sum(dy * on * silu_g, axis=1, keepdims=True), axis=0)
            dgate = dy * on * gain * (sgt * (1.0 + gt * (1.0 - sgt)))
            don = dy * gain * silu_g
            do = r * (don - on * jnp.mean(don * on, axis=-1, keepdims=True))
            d_vnew = _btn(aq_ref[:, rows, :], do) + _bnn(_stack_heads(kd_ref, rows), dsn)
            egl = jnp.stack([egl_ref[ci * 8 + h:ci * 8 + h + 1, :] for h in range(DN_HEADS)])
            dstate_ref[...] = _btn(_stack_heads(qg_ref, rows), do) + dsn * egl - _btn(_stack_heads(w_ref, rows), d_vnew)
            for h in range(DN_HEADS):
                sl = slice(h * HD, (h + 1) * HD)
                do_ref[rows, sl] = do[h]
                dvn_ref[rows, sl] = d_vnew[h]
                dgate_ref[rows, sl] = dgate[h]
        small_ref[...] += jnp.concatenate([d_gain, jnp.zeros((7, 128), F32)], axis=0)

    nb = N // nc
    tok = lambda wd: pl.BlockSpec((nc * C, wd), lambda i: (nb - 1 - i, 0))
    sq = pl.BlockSpec((DN_HEADS, nc * C, C), lambda i: (0, nb - 1 - i, 0))
    vec = pl.BlockSpec((1, 128), lambda i: (0, 0))
    return pl.pallas_call(
        body, name="dn_scan_bwd", grid=(nb,),
        in_specs=[tok(DN_WIDTH)] * 3 + [sq, pl.BlockSpec((nc * 8, 128), lambda i: (nb - 1 - i, 0)), tok(DN_WIDTH), vec,
                                       tok(DN_WIDTH), tok(DN_WIDTH)],
        out_specs=[tok(DN_WIDTH)] * 3 + [pl.BlockSpec((nc * DN_WIDTH, HD), lambda i: (nb - 1 - i, 0)),
                                        pl.BlockSpec((8, 128), lambda i: (0, 0))],
        out_shape=[jax.ShapeDtypeStruct((S, DN_WIDTH), F32)] * 3 + [jax.ShapeDtypeStruct((N * DN_WIDTH, HD), F32),
                                                                  jax.ShapeDtypeStruct((8, 128), F32)],
        scratch_shapes=[pltpu.VMEM((DN_HEADS, HD, HD), F32)],
        compiler_params=_params(1),
    )(w, qg, kd, aq, egl, gate, dn_gain, o, ddn)


def _dn_post(qn, kn, v, bd, avec, dvec, t_inv, v_new_all, states, dstates, do_all, dvn_all):
    S = qn.shape[0]
    C = DN_CHUNK
    N = S // C
    HD = DN_HEAD_DIM
    nc = PREP_CHUNKS
    B = nc * DN_HEADS

    def body(q_ref, k_ref, v_ref, bd_ref, a_ref, d_ref, t_ref, vn_ref, st_ref, dst_ref, do_ref, dvn_ref,
             dq_ref, dk_ref, dv_ref, dbd_ref, small_ref):
        @pl.when(pl.program_id(0) == 0)
        def _():
            small_ref[...] = jnp.zeros_like(small_ref)

        avec = a_ref[...]
        bds = [bd_ref[ci * C:(ci + 1) * C, :] for ci in range(nc)]
        k = _stack_units(k_ref, nc)
        vv = _stack_units(v_ref, nc)
        t = jnp.concatenate([t_ref[:, ci * C:(ci + 1) * C, :] for ci in range(nc)], axis=0)
        c = _dn_common_b(bds, avec, d_ref[...], _stack_units(q_ref, nc), k, vv, t=t)
        q, kb, eg, u, w = c["q"], c["kb"], c["eg"], c["u"], c["w"]
        beta, decay, incl, strict, eye = c["beta"], c["decay"], c["incl"], c["strict"], c["eye"]
        st = jnp.stack([st_ref[b * HD:(b + 1) * HD, :] for b in range(B)])
        dsn = jnp.stack([dst_ref[b * HD:(b + 1) * HD, :] for b in range(B)])
        v_new = _stack_units(vn_ref, nc)
        do = _stack_units(do_ref, nc)
        d_vnew = _stack_units(dvn_ref, nc)
        egl = jnp.exp(c["g_last"])
        daq = jnp.where(incl, _bnt(do, v_new), 0.0)
        d_qg = _bnt(do, st)
        d_kd = _bnt(v_new, dsn)
        d_glast = jnp.sum(jnp.sum(dsn * st, axis=-1, keepdims=True), axis=1, keepdims=True) * egl
        d_w = -_bnt(d_vnew, st)
        d_ru = _btn(t, d_vnew)
        d_rw = _btn(t, d_w)
        da = -jnp.where(strict, _bnt(d_ru, u) + _bnt(d_rw, w), 0.0)
        dv = d_ru * beta
        dbeta = jnp.sum(d_ru * vv, axis=-1, keepdims=True)
        dkb = d_rw * eg
        dgc = jnp.sum(d_rw * c["rhs_w"], axis=-1, keepdims=True)
        dkk = da * decay
        ddecay = da * c["kk"]
        dkb = dkb + _bnn(dkk, k)
        dk = _btn(dkk, kb)
        dqk = daq * decay
        ddecay = ddecay + daq * c["qk"]
        dq = _bnn(dqk, k)
        dk = dk + _btn(dqk, q)
        m = ddecay * decay
        col_sum = jnp.sum(m, axis=1, keepdims=True)
        dgc = dgc + jnp.sum(m, axis=-1, keepdims=True) - jnp.sum(eye * col_sum, axis=-1, keepdims=True)
        dq = dq + d_qg * eg
        dgc = dgc + jnp.sum(d_qg * c["qg"], axis=-1, keepdims=True)
        dk = dk + d_kd * c["ekd"]
        tk = jnp.sum(d_kd * c["kd"], axis=-1, keepdims=True)
        dgc = dgc - tk
        d_glast = d_glast + jnp.sum(tk, axis=1, keepdims=True)
        dk = dk + dkb * beta
        dbeta = dbeta + jnp.sum(dkb * k, axis=-1, keepdims=True)
        dgc = dgc + jnp.where(c["last"], d_glast, 0.0)
        dgc_row = jnp.sum(eye * dgc, axis=1, keepdims=True)
        dgraw = jnp.sum(jnp.where(c["col"] >= c["row"], dgc_row, 0.0), axis=-1, keepdims=True)
        _store_units(dq_ref, dq * (HD ** -0.5), nc)
        _store_units(dk_ref, dk, nc)
        _store_units(dv_ref, dv, nc)
        dbraw = dbeta * beta * (1.0 - beta)
        dzc = dgraw * _sigmoid(c["zc"])
        ga = dgraw * c["graw"]
        lane = c["lane"]
        lane1 = lax.broadcasted_iota(jnp.int32, (1, 128), 1)
        neg_ea = -jnp.exp(avec)
        d_alog = jnp.zeros((1, 128), F32)
        d_dt = jnp.zeros((1, 128), F32)
        for ci in range(nc):
            dbd = jnp.zeros((C, 128), F32)
            for h in range(DN_HEADS):
                b = ci * DN_HEADS + h
                dz = dzc[b] * neg_ea
                dbd = dbd + jnp.where(lane == h, dbraw[b], 0.0) + jnp.where(lane == DN_HEADS + h, dz, 0.0)
                d_alog = d_alog + jnp.where(lane1 == DN_HEADS + h, jnp.sum(ga[b], axis=0, keepdims=True), 0.0)
                d_dt = d_dt + jnp.where(lane1 == DN_HEADS + h, jnp.sum(dz, axis=0, keepdims=True), 0.0)
            dbd_ref[ci * C:(ci + 1) * C, :] = dbd
        small_ref[...] += jnp.concatenate([d_alog, d_dt, jnp.zeros((6, 128), F32)], axis=0)

    tok = lambda wd: pl.BlockSpec((nc * C, wd), lambda n: (n, 0))
    big = pl.BlockSpec((nc * DN_WIDTH, HD), lambda n: (n, 0))
    sq = pl.BlockSpec((DN_HEADS, nc * C, C), lambda n: (0, n, 0))
    vec = pl.BlockSpec((1, 128), lambda n: (0, 0))
    return pl.pallas_call(
        body, name="dn_post", grid=(N // nc,),
        in_specs=[tok(DN_WIDTH)] * 3 + [tok(128), vec, vec, sq, tok(DN_WIDTH), big, big, tok(DN_WIDTH), tok(DN_WIDTH)],
        out_specs=[tok(DN_WIDTH)] * 3 + [tok(128), pl.BlockSpec((8, 128), lambda n: (0, 0))],
        out_shape=[jax.ShapeDtypeStruct((S, DN_WIDTH), F32)] * 3 + [jax.ShapeDtypeStruct((S, 128), F32),
                                                                  jax.ShapeDtypeStruct((8, 128), F32)],
        compiler_params=_params(1),
    )(qn, kn, v, bd, avec, dvec, t_inv, v_new_all, states, dstates, do_all, dvn_all)


def _outproj_fwd(x, attn, dn, w_out):
    S, D = x.shape
    tm = 512

    def body(x_ref, a_ref, d_ref, w_ref, xo_ref, mix_ref):
        a = a_ref[...].astype(BF16)
        dd = d_ref[...].astype(BF16)
        mix_ref[:, 0:ATTN_WIDTH] = a
        mix_ref[:, ATTN_WIDTH:] = dd
        xo_ref[...] = x_ref[...] + _nn(a, w_ref[0:ATTN_WIDTH, :]) + _nn(dd, w_ref[ATTN_WIDTH:, :])

    tok = lambda w: pl.BlockSpec((tm, w), lambda i: (i, 0))
    return pl.pallas_call(
        body, name="outproj_fwd", grid=(S // tm,),
        in_specs=[tok(D), tok(ATTN_WIDTH), tok(DN_WIDTH), pl.BlockSpec((D, D), lambda i: (0, 0))],
        out_specs=[tok(D), tok(D)],
        out_shape=[jax.ShapeDtypeStruct((S, D), F32), jax.ShapeDtypeStruct((S, D), BF16)],
        compiler_params=_params(1),
    )(x, attn, dn, w_out)


def _outproj_bwd(dx, w_out, attn):
    S, D = dx.shape
    tm = VIEW_TILE

    def body(dx_ref, w_ref, attn_ref, da1, da4, da16, dl1, dl4, dl16, ddn_ref, dxb_ref, planes):
        d = dx_ref[...].astype(BF16)
        dxb_ref[...] = d
        da = _nt(d, w_ref[0:ATTN_WIDTH, :])
        ddn_ref[...] = _nt(d, w_ref[ATTN_WIDTH:, :])
        _tile_to_views(da, planes, (da1, da4, da16))
        lo = lax.broadcasted_iota(jnp.int32, (tm, 128), 1) < 64
        cols = []
        for G in range(4):
            sl = slice(G * 128, (G + 1) * 128)
            t = da[:, sl] * attn_ref[:, sl]
            d0 = jnp.sum(jnp.where(lo, t, 0.0), axis=-1, keepdims=True)
            d1 = jnp.sum(jnp.where(lo, 0.0, t), axis=-1, keepdims=True)
            cols.append(jnp.where(lo, d0, d1))
        _tile_to_views(jnp.concatenate(cols, axis=1), planes, (dl1, dl4, dl16))

    tok = lambda w: pl.BlockSpec((tm, w), lambda i: (i, 0))
    views = [_view_spec(d) for d in DILATIONS]
    return pl.pallas_call(
        body, name="outproj_bwd", grid=(S // tm,),
        in_specs=[tok(D), pl.BlockSpec((D, D), lambda i: (0, 0)), tok(ATTN_WIDTH)],
        out_specs=views + views + [tok(DN_WIDTH), tok(D)],
        out_shape=[_view_shape(S, d, F32) for d in DILATIONS] * 2
                  + [jax.ShapeDtypeStruct((S, DN_WIDTH), F32), jax.ShapeDtypeStruct((S, D), BF16)],
        scratch_shapes=[pltpu.VMEM((4, tm, 128), F32)],
        compiler_params=_params(1),
    )(dx, w_out, attn)


def _loss_head(x, gain, target):
    S, D = x.shape
    tm = 512

    def body(x_ref, gain_ref, t_ref, loss_ref, dx_ref, dgain_ref):
        @pl.when(pl.program_id(0) == 0)
        def _():
            loss_ref[...] = jnp.zeros_like(loss_ref)
            dgain_ref[...] = jnp.zeros_like(dgain_ref)

        xf = x_ref[...]
        gain = gain_ref[...]
        r = lax.rsqrt(jnp.mean(xf * xf, axis=-1, keepdims=True) + NORM_EPS)
        xhat = xf * r
        err = xhat * gain - t_ref[...]
        part = 0.5 * jnp.sum(jnp.mean(err * err, axis=-1, keepdims=True), axis=0, keepdims=True)
        first = (lax.broadcasted_iota(jnp.int32, (8, 128), 0) == 0) & (lax.broadcasted_iota(jnp.int32, (8, 128), 1) == 0)
        loss_ref[...] += jnp.where(first, part, 0.0)
        dy = err * (1.0 / D)
        dgain_ref[...] += jnp.sum(dy * xhat, axis=0, keepdims=True)
        dxh = dy * gain
        dx_ref[...] = r * (dxh - xhat * jnp.mean(dxh * xhat, axis=-1, keepdims=True))

    tok = pl.BlockSpec((tm, D), lambda i: (i, 0))
    row = pl.BlockSpec((1, D), lambda i: (0, 0))
    return pl.pallas_call(
        body, name="loss_head", grid=(S // tm,),
        in_specs=[tok, row, tok],
        out_specs=[pl.BlockSpec((8, 128), lambda i: (0, 0)), tok, row],
        out_shape=[jax.ShapeDtypeStruct((8, 128), F32), jax.ShapeDtypeStruct((S, D), F32),
                   jax.ShapeDtypeStruct((1, D), F32)],
        compiler_params=_params(1),
    )(x, gain, target)


def _adamw(w, g, m, v, name):
    R, Ccols = w.shape
    tr = R
    for cand in (256, 128, 64, 32, 16, 8):
        if R % cand == 0:
            tr = cand
            break
    c1 = 1.0 - ADAM_B1 ** ADAM_STEP
    c2 = 1.0 - ADAM_B2 ** ADAM_STEP

    def body(w_ref, g_ref, m_ref, v_ref, d_ref, nm_ref, nv_ref):
        gv = g_ref[...]
        mn = ADAM_B1 * m_ref[...] + (1.0 - ADAM_B1) * gv
        vn = ADAM_B2 * v_ref[...] + (1.0 - ADAM_B2) * (gv * gv)
        nm_ref[...] = mn
        nv_ref[...] = vn
        d_ref[...] = -ADAM_LR * ((mn / c1) / (jnp.sqrt(vn / c2) + ADAM_EPS) + ADAM_WD * w_ref[...])

    spec = pl.BlockSpec((tr, Ccols), lambda i: (i, 0))
    return pl.pallas_call(
        body, name=name, grid=(R // tr,), in_specs=[spec] * 4, out_specs=[spec] * 3,
        out_shape=[jax.ShapeDtypeStruct((R, Ccols), F32)] * 3, compiler_params=_params(1),
    )(w, g, m, v)


def _local_step(x, target, wts, small):
    g1, g2, gm, gf = small["norm_ffn1"], small["norm_ffn2"], small["norm_mix"], small["norm_final"]

    x1, h1, fg1, fu1 = _ffn_fwd(x, g1, wts["ffn1_gate"], wts["ffn1_up"], wts["ffn1_down"], "ffn1_fwd")
    h2, *qkv, xq, xk, xv, gate, bd = _inproj_fwd(x1, gm, wts["w_in"])
    aq, ak, av = qkv[0:3], qkv[3:6], qkv[6:9]
    parts = [_attn_fwd(aq[p], ak[p], av[p], d, f"attn_fwd_d{d}") for p, d in enumerate(DILATIONS)]
    attn, *lse = _attn_merge(parts)
    conv_w = small["conv_w"]
    qn, kn, vv = _conv_fwd(xq, xk, xv, conv_w)
    dn_u, dn_w, dn_qg, dn_kd, dn_aq, dn_t, dn_egl = _dn_prep(qn, kn, vv, bd, small["avec"], small["dvec"])
    dn, o_dn, v_new, states = _dn_scan_fwd(dn_u, dn_w, dn_qg, dn_kd, dn_aq, dn_egl, gate, small["dn_norm"])
    x2, mix = _outproj_fwd(x1, attn, dn, wts["w_out"])
    x3, h3, fg2, fu2 = _ffn_fwd(x2, g2, wts["ffn2_gate"], wts["ffn2_up"], wts["ffn2_down"], "ffn2_fwd")
    loss, dx3, d_gf = _loss_head(x3, gf, target)

    grads = {}
    dx2, d_g2, dfg2, dfu2, act2, dout2 = _ffn_bwd(dx3, x2, g2, fg2, fu2, wts["ffn2_down"], wts["ffn2_gate"],
                                                 wts["ffn2_up"], "ffn2_bwd")
    tk = 512
    grads["ffn2_gate"] = _dw_chunks(dfg2, h3, tk, "dw_ffn2_gate")
    grads["ffn2_up"] = _dw_chunks(dfu2, h3, tk, "dw_ffn2_up")
    grads["ffn2_down"] = _dw_chunks(act2, dout2, tk, "dw_ffn2_down")

    *dviews, ddn, dx2b = _outproj_bwd(dx2, wts["w_out"], attn)
    dattn, dd = dviews[0:3], dviews[3:6]
    grads["w_out"] = _matmul_tn(mix, dx2b, D_MODEL, tk, "dw_out")

    daq, dak, dav = [], [], []
    for p, d in enumerate(DILATIONS):
        daq.append(_attn_bwd_q(aq[p], ak[p], av[p], dattn[p], lse[p], dd[p], d, f"attn_bwd_q_d{d}"))
        dk_p, dv_p = _attn_bwd_kv(aq[p], ak[p], av[p], dattn[p], lse[p], dd[p], d, f"attn_bwd_kv_d{d}")
        dak.append(dk_p)
        dav.append(dv_p)

    do_dn, dvn, dgate, dstates, d_dn_gain = _dn_scan_bwd(dn_w, dn_qg, dn_kd, dn_aq, dn_egl, gate, small["dn_norm"], o_dn, ddn)
    dqn, dkn, dvv, dbd, dn_small = _dn_post(qn, kn, vv, bd, small["avec"], small["dvec"], dn_t, v_new, states, dstates,
                                            do_dn, dvn)
    dcq, dck, dcv, dwq, dwk, dwv = _conv_bwd_pre(xq, xk, xv, conv_w, dqn, dkn, dvv)
    dxq, dxk, dxv = _conv_bwd_x(dcq, dck, dcv, conv_w)
    d_conv = jnp.concatenate([dwq[:CONV_WIDTH], dwk[:CONV_WIDTH], dwv[:CONV_WIDTH]], axis=1)

    dx1, d_gm, dproj = _inproj_bwd(dx2, x1, gm, [daq, dak, dav], [dxq, dxk, dxv, dgate], dbd, wts["w_in"])
    grads["w_in"] = _matmul_tn(dproj, h2, IN_COLS_PADDED, 256, "dw_in")

    dx0, d_g1, dfg1, dfu1, act1, dout1 = _ffn_bwd(dx1, x, g1, fg1, fu1, wts["ffn1_down"], wts["ffn1_gate"],
                                                 wts["ffn1_up"], "ffn1_bwd")
    grads["ffn1_gate"] = _dw_chunks(dfg1, h1, tk, "dw_ffn1_gate")
    grads["ffn1_up"] = _dw_chunks(dfu1, h1, tk, "dw_ffn1_up")
    grads["ffn1_down"] = _dw_chunks(act1, dout1, tk, "dw_ffn1_down")

    small_grads = dict(norm_ffn1=d_g1, norm_mix=d_gm, norm_ffn2=d_g2, norm_final=d_gf, conv_w=d_conv,
                       a_log=dn_small[0:1], dt_bias=dn_small[1:2], dn_norm=d_dn_gain[0:1])
    return loss, dx0, grads, small_grads


PACK_SECTIONS = (("ffn1_gate", 704), ("ffn1_up", 704), ("ffn1_down", 704), ("w_in", 898), ("w_out", 256),
                 ("ffn2_gate", 704), ("ffn2_up", 704), ("ffn2_down", 704))
PACK_ROWS = 5408
HALF_ROWS = PACK_ROWS // 2
ADD_ROWS = 208

HBM = pl.BlockSpec(memory_space=pl.ANY)
VMEM_SPEC = pl.BlockSpec(memory_space=pltpu.VMEM)


def _coords():
    return lax.axis_index("x"), lax.axis_index("y"), lax.axis_index("c")


def _remote(src, dst, send_sems, recv_sems, k, dev):
    return pltpu.make_async_remote_copy(src_ref=src, dst_ref=dst, send_sem=send_sems.at[k], recv_sem=recv_sems.at[k],
                                        device_id=dev, device_id_type=MESH)


def _allreduce_small(buf, name):
    R, Cc = buf.shape

    def body(src_ref, out_ref, recv_ref, send_sems, recv_sems):
        x, y, c = _coords()
        copies = []
        for m in range(1, 8):
            fx, fy, fc = (m >> 2) & 1, (m >> 1) & 1, m & 1
            dev = (x ^ fx if fx else x, y ^ fy if fy else y, c ^ fc if fc else c)
            cp = _remote(src_ref, recv_ref.at[m - 1], send_sems, recv_sems, m - 1, dev)
            cp.start()
            copies.append(cp)
        for cp in copies:
            cp.wait()
        r = [src_ref[...]] + [recv_ref[m] for m in range(7)]
        out_ref[...] = ((r[0] + r[1]) + (r[2] + r[3])) + ((r[4] + r[5]) + (r[6] + r[7]))

    return pl.pallas_call(
        body, name=name, out_shape=jax.ShapeDtypeStruct((R, Cc), F32),
        in_specs=[VMEM_SPEC], out_specs=VMEM_SPEC,
        scratch_shapes=[pltpu.VMEM((7, R, Cc), F32), pltpu.SemaphoreType.DMA((7,)), pltpu.SemaphoreType.DMA((7,))],
    )(buf)


def _allgather_weights(pack2):
    _, Hh, Cc = pack2.shape

    def body(src_ref, out_ref, send_sems, recv_sems):
        x, y, c = _coords()
        sib = (x, y, 1 - c)
        others = [(1 - x, y), (x, 1 - y), (1 - x, 1 - y)]
        blk = lambda cx, cy, half: out_ref.at[2 * cx + cy, half]
        mine = _remote(src_ref, out_ref.at[2 * x + y], send_sems, recv_sems, 6, sib)
        mine.start()
        first = [_remote(src_ref.at[c], blk(x, y, c), send_sems, recv_sems, j, (ox, oy, c)) for j, (ox, oy) in enumerate(others)]
        for cp in first:
            cp.start()
        passed = [_remote(blk(ox, oy, c), blk(ox, oy, c), send_sems, recv_sems, 3 + j, sib) for j, (ox, oy) in enumerate(others)]
        for j, (ox, oy) in enumerate(others):
            _remote(src_ref.at[c], blk(ox, oy, c), send_sems, recv_sems, j, (ox, oy, c)).wait_recv()
            passed[j].start()
        for j, (ox, oy) in enumerate(others):
            _remote(src_ref.at[c], blk(ox, oy, 1 - c), send_sems, recv_sems, 3 + j, sib).wait_recv()
        for cp in first + passed:
            cp.wait_send()
        mine.wait()

    return pl.pallas_call(
        body, name="allgather_weights", out_shape=jax.ShapeDtypeStruct((N_CHIPS, 2, Hh, Cc), pack2.dtype),
        in_specs=[HBM], out_specs=HBM,
        scratch_shapes=[pltpu.SemaphoreType.DMA((7,)), pltpu.SemaphoreType.DMA((7,))],
    )(pack2)


def _rs_swap_halves(gpack):
    _, nj, Hh, Cc = gpack.shape

    def body(src_ref, out_ref, send_sems, recv_sems):
        x, y, c = _coords()
        cp = _remote(src_ref.at[1 - c], out_ref, send_sems, recv_sems, 0, (x, y, 1 - c))
        cp.start()
        cp.wait()

    return pl.pallas_call(
        body, name="rs_swap_halves", out_shape=jax.ShapeDtypeStruct((nj, Hh, Cc), gpack.dtype),
        in_specs=[HBM], out_specs=HBM,
        scratch_shapes=[pltpu.SemaphoreType.DMA((1,)), pltpu.SemaphoreType.DMA((1,))],
    )(gpack)


def _rs_add_pair(gpack, other, c):
    _, nj, Hh, Cc = gpack.shape
    tr = ADD_ROWS

    def body(c_ref, a_ref, b_ref, o_ref):
        o_ref[...] = (a_ref[...] + b_ref[...]).astype(BF16)

    return pl.pallas_call(
        body, name="rs_add_pair",
        grid_spec=pltpu.PrefetchScalarGridSpec(
            num_scalar_prefetch=1, grid=(nj, Hh // tr),
            in_specs=[pl.BlockSpec((None, None, tr, Cc), lambda j, i, c_ref: (c_ref[0], j, i, 0)),
                      pl.BlockSpec((None, tr, Cc), lambda j, i, c_ref: (j, i, 0))],
            out_specs=pl.BlockSpec((None, tr, Cc), lambda j, i, c_ref: (j, i, 0))),
        out_shape=jax.ShapeDtypeStruct((nj, Hh, Cc), BF16),
        compiler_params=_params(2),
    )(c, gpack, other)


def _rs_exchange_chips(part):
    nj, Hh, Cc = part.shape

    def body(src_ref, out_ref, send_sems, recv_sems):
        x, y, c = _coords()
        others = [(1 - x, y), (x, 1 - y), (1 - x, 1 - y)]
        cps = [_remote(src_ref.at[2 * ox + oy], out_ref.at[k], send_sems, recv_sems, k, (ox, oy, c))
               for k, (ox, oy) in enumerate(others)]
        for cp in cps:
            cp.start()
        for cp in cps:
            cp.wait()

    return pl.pallas_call(
        body, name="rs_exchange_chips", out_shape=jax.ShapeDtypeStruct((3, Hh, Cc), part.dtype),
        in_specs=[HBM], out_specs=HBM,
        scratch_shapes=[pltpu.SemaphoreType.DMA((3,)), pltpu.SemaphoreType.DMA((3,))],
    )(part)


def _rs_add_total(part, recv, chip):
    nj, Hh, Cc = part.shape
    tr = ADD_ROWS

    def body(chip_ref, p_ref, r0_ref, r1_ref, r2_ref, o_ref):
        f = lambda r: r[...].astype(F32)
        o_ref[...] = (f(p_ref) + f(r0_ref)) + (f(r1_ref) + f(r2_ref))

    rk = lambda k: pl.BlockSpec((None, tr, Cc), lambda i, chip_ref, k=k: (k, i, 0))
    return pl.pallas_call(
        body, name="rs_add_total",
        grid_spec=pltpu.PrefetchScalarGridSpec(
            num_scalar_prefetch=1, grid=(Hh // tr,),
            in_specs=[pl.BlockSpec((None, tr, Cc), lambda i, chip_ref: (chip_ref[0], i, 0)), rk(0), rk(1), rk(2)],
            out_specs=pl.BlockSpec((tr, Cc), lambda i, chip_ref: (i, 0))),
        out_shape=jax.ShapeDtypeStruct((Hh, Cc), F32),
        compiler_params=_params(1),
    )(chip, part, recv, recv, recv)


def _rs_share_total(total):
    Hh, Cc = total.shape

    def body(src_ref, out_ref, send_sems, recv_sems):
        x, y, c = _coords()
        cp = _remote(src_ref, out_ref, send_sems, recv_sems, 0, (x, y, 1 - c))
        cp.start()
        cp.wait()

    return pl.pallas_call(
        body, name="rs_share_total", out_shape=jax.ShapeDtypeStruct((Hh, Cc), total.dtype),
        in_specs=[HBM], out_specs=HBM,
        scratch_shapes=[pltpu.SemaphoreType.DMA((1,)), pltpu.SemaphoreType.DMA((1,))],
    )(total)


BIG = ("ffn1_gate", "ffn1_up", "ffn1_down", "w_in", "w_out", "ffn2_gate", "ffn2_up", "ffn2_down")
W_IN_ROWS = 960


def _rows(ref, start, size):
    return ref.at[pl.ds(pl.multiple_of(start, 16), size)]


def _allgather_arrays(shards):
    n = len(shards)
    halves = [s.shape[0] // 2 for s in shards]

    def body(*refs):
        srcs, outs, send_sems, recv_sems = refs[:n], refs[n:2 * n], refs[2 * n], refs[2 * n + 1]
        x, y, c = _coords()
        sib = (x, y, 1 - c)
        me = 2 * x + y
        others = [(1 - x, y), (x, 1 - y), (1 - x, 1 - y)]
        started = []
        for a in range(n):
            h = halves[a]
            cp = _remote(srcs[a], outs[a].at[me], send_sems, recv_sems, 7 * a + 6, sib)
            cp.start()
            started.append(cp)
            for j, (ox, oy) in enumerate(others):
                cp = _remote(_rows(srcs[a], c * h, h), _rows(outs[a].at[me], c * h, h), send_sems, recv_sems, 7 * a + j, (ox, oy, c))
                cp.start()
                started.append(cp)
        for a in range(n):
            h = halves[a]
            for j, (ox, oy) in enumerate(others):
                got = _rows(outs[a].at[2 * ox + oy], c * h, h)
                _remote(got, got, send_sems, recv_sems, 7 * a + j, sib).wait_recv()
                cp = _remote(got, got, send_sems, recv_sems, 7 * a + 3 + j, sib)
                cp.start()
                started.append(cp)
        for a in range(n):
            h = halves[a]
            for j, (ox, oy) in enumerate(others):
                got = _rows(outs[a].at[2 * ox + oy], (1 - c) * h, h)
                _remote(got, got, send_sems, recv_sems, 7 * a + 3 + j, sib).wait_recv()
            _remote(srcs[a], outs[a].at[me], send_sems, recv_sems, 7 * a + 6, sib).wait_recv()
        for cp in started:
            cp.wait_send()

    return pl.pallas_call(
        body, name="allgather_weights",
        out_shape=[jax.ShapeDtypeStruct((N_CHIPS,) + s.shape, s.dtype) for s in shards],
        in_specs=[HBM] * n, out_specs=[HBM] * n,
        scratch_shapes=[pltpu.SemaphoreType.DMA((7 * n,)), pltpu.SemaphoreType.DMA((7 * n,))],
    )(*shards)


def _swap_sibling(arrs, pick_other_half, name):
    n = len(arrs)
    outs = [jax.ShapeDtypeStruct((a.shape[0], a.shape[1] // 2) + a.shape[2:] if pick_other_half else a.shape, a.dtype) for a in arrs]

    def body(*refs):
        srcs, dsts, send_sems, recv_sems = refs[:n], refs[n:2 * n], refs[2 * n], refs[2 * n + 1]
        x, y, c = _coords()
        cps = []
        for a in range(n):
            src = srcs[a]
            if pick_other_half:
                h = src.shape[1] // 2
                src = src.at[:, pl.ds(pl.multiple_of((1 - c) * h, 16), h)]
            cp = _remote(src, dsts[a], send_sems, recv_sems, a, (x, y, 1 - c))
            cp.start()
            cps.append(cp)
        for cp in cps:
            cp.wait()

    return pl.pallas_call(
        body, name=name, out_shape=outs, in_specs=[HBM] * n, out_specs=[HBM] * n,
        scratch_shapes=[pltpu.SemaphoreType.DMA((n,)), pltpu.SemaphoreType.DMA((n,))],
    )(*arrs)


def _rs_add_pairs(gs, others, c):
    n = len(gs)
    hbs = [g.shape[1] // 4 for g in gs]

    def body(c_ref, *refs):
        for a in range(n):
            refs[2 * n + a][...] = (refs[a][...] + refs[n + a][...]).astype(BF16)

    mine = lambda hb: pl.BlockSpec((None, hb, D_MODEL), lambda j, s, c_ref: (j, c_ref[0] * 2 + s, 0))
    flat = lambda hb: pl.BlockSpec((None, hb, D_MODEL), lambda j, s, c_ref: (j, s, 0))
    return pl.pallas_call(
        body, name="rs_add_pairs",
        grid_spec=pltpu.PrefetchScalarGridSpec(
            num_scalar_prefetch=1, grid=(N_CHIPS, 2),
            in_specs=[mine(hb) for hb in hbs] + [flat(hb) for hb in hbs],
            out_specs=[flat(hb) for hb in hbs]),
        out_shape=[jax.ShapeDtypeStruct(o.shape, BF16) for o in others],
        compiler_params=_params(2),
    )(c, *gs, *others)


def _rs_exchange_arrays(parts):
    n = len(parts)

    def body(*refs):
        srcs, dsts, send_sems, recv_sems = refs[:n], refs[n:2 * n], refs[2 * n], refs[2 * n + 1]
        x, y, c = _coords()
        others = [(1 - x, y), (x, 1 - y), (1 - x, 1 - y)]
        cps = []
        for a in range(n):
            for k, (ox, oy) in enumerate(others):
                cp = _remote(srcs[a].at[2 * ox + oy], dsts[a].at[k], send_sems, recv_sems, 3 * a + k, (ox, oy, c))
                cp.start()
                cps.append(cp)
        for cp in cps:
            cp.wait()

    return pl.pallas_call(
        body, name="rs_exchange_chips", out_shape=[jax.ShapeDtypeStruct((3,) + p.shape[1:], p.dtype) for p in parts],
        in_specs=[HBM] * n, out_specs=[HBM] * n,
        scratch_shapes=[pltpu.SemaphoreType.DMA((3 * n,)), pltpu.SemaphoreType.DMA((3 * n,))],
    )(*parts)


def _rs_add_totals(parts, recvs, chip):
    n = len(parts)
    hbs = [p.shape[1] // 2 for p in parts]

    def body(chip_ref, *refs):
        f = lambda r: r[...].astype(F32)
        for a in range(n):
            p, r0, r1, r2 = refs[a], refs[n + 3 * a], refs[n + 3 * a + 1], refs[n + 3 * a + 2]
            refs[4 * n + a][...] = (f(p) + f(r0)) + (f(r1) + f(r2))

    own = lambda hb: pl.BlockSpec((None, hb, D_MODEL), lambda s, chip_ref: (chip_ref[0], s, 0))
    slot = lambda hb, k: pl.BlockSpec((None, hb, D_MODEL), lambda s, chip_ref, k=k: (k, s, 0))
    recv_specs = [slot(hb, k) for hb in hbs for k in range(3)]
    recv_args = [r for r in recvs for _ in range(3)]
    return pl.pallas_call(
        body, name="rs_add_totals",
        grid_spec=pltpu.PrefetchScalarGridSpec(
            num_scalar_prefetch=1, grid=(2,),
            in_specs=[own(hb) for hb in hbs] + recv_specs,
            out_specs=[pl.BlockSpec((hb, D_MODEL), lambda s, chip_ref: (s, 0)) for hb in hbs]),
        out_shape=[jax.ShapeDtypeStruct(p.shape[1:], F32) for p in parts],
        compiler_params=_params(1),
    )(chip, *parts, *recv_args)


def _permute_w_in(w):
    return jnp.concatenate([w[:, :3072], w[:, 3080:IN_COLS], w[:, 3072:3080],
                            jnp.zeros((w.shape[0], IN_COLS_PADDED - IN_COLS), w.dtype)], axis=1)


def _pack_rows(shards, dtype):
    rows = [shards[n].astype(dtype).reshape(r, D_MODEL) for n, r in PACK_SECTIONS]
    used = sum(r for _, r in PACK_SECTIONS)
    rows.append(jnp.zeros((PACK_ROWS - used, D_MODEL), dtype))
    return jnp.concatenate(rows, axis=0)


def _unpack_rows(pack, shapes):
    out, at = {}, 0
    for n, r in PACK_SECTIONS:
        out[n] = pack[at:at + r].reshape(shapes[n])
        at += r
    return out


SHARD_SHAPES = dict(ffn1_gate=(1024, 704), ffn1_up=(1024, 704), ffn1_down=(704, 1024), w_in=(1024, 898),
                    w_out=(256, 1024), ffn2_gate=(1024, 704), ffn2_up=(1024, 704), ffn2_down=(704, 1024))
ROW_SHARDED = ("ffn1_down", "w_out", "ffn2_down")
SMALL_ROWS = 16


def _pad_row(v):
    v = v.reshape(1, -1)
    return jnp.pad(v, ((0, 0), (0, D_MODEL - v.shape[1])))


def kernel(x, norm_ffn1, ffn1_gate, ffn1_up, ffn1_down, norm_mix, w_in, conv_w, a_log, dt_bias, dn_norm, w_out, norm_ffn2, ffn2_gate, ffn2_up, ffn2_down, norm_final, loss_target, m_norm_ffn1, m_ffn1_gate, m_ffn1_up, m_ffn1_down, m_norm_mix, m_w_in, m_conv_w, m_a_log, m_dt_bias, m_dn_norm, m_w_out, m_norm_ffn2, m_ffn2_gate, m_ffn2_up, m_ffn2_down, m_norm_final, v_norm_ffn1, v_ffn1_gate, v_ffn1_up, v_ffn1_down, v_norm_mix, v_w_in, v_conv_w, v_a_log, v_dt_bias, v_dn_norm, v_w_out, v_norm_ffn2, v_ffn2_gate, v_ffn2_up, v_ffn2_down, v_norm_final):
    cx, cy, cc = _coords()
    chip = 2 * cx + cy
    big_w = dict(ffn1_gate=ffn1_gate[0], ffn1_up=ffn1_up[0], ffn1_down=ffn1_down[0], w_in=w_in[0], w_out=w_out[0],
                 ffn2_gate=ffn2_gate[0], ffn2_up=ffn2_up[0], ffn2_down=ffn2_down[0])
    big_m = dict(ffn1_gate=m_ffn1_gate[0], ffn1_up=m_ffn1_up[0], ffn1_down=m_ffn1_down[0], w_in=m_w_in[0], w_out=m_w_out[0],
                 ffn2_gate=m_ffn2_gate[0], ffn2_up=m_ffn2_up[0], ffn2_down=m_ffn2_down[0])
    big_v = dict(ffn1_gate=v_ffn1_gate[0], ffn1_up=v_ffn1_up[0], ffn1_down=v_ffn1_down[0], w_in=v_w_in[0], w_out=v_w_out[0],
                 ffn2_gate=v_ffn2_gate[0], ffn2_up=v_ffn2_up[0], ffn2_down=v_ffn2_down[0])

    wts = dict(zip(BIG, _allgather_arrays([big_w[n].astype(BF16) for n in BIG])))
    wts["w_out"] = wts["w_out"].reshape(D_MODEL, D_MODEL)
    wts["w_in"] = _permute_w_in(jnp.concatenate([wts["w_in"][j] for j in range(N_CHIPS)], axis=1))

    conv_shard = conv_w[0]
    emb = jnp.concatenate([jnp.where((chip == j) & (cc == 0), conv_shard, 0.0) for j in range(N_CHIPS)], axis=1)
    emb = jnp.pad(emb.reshape(6, D_MODEL), ((0, 2), (0, 0)))
    conv_full = _allreduce_small(emb, "allgather_conv_w")[:6].reshape(CONV_WIDTH, 3 * DN_WIDTH)

    zvec = jnp.zeros((1, 128), F32)
    small = dict(norm_ffn1=norm_ffn1, norm_mix=norm_mix, norm_ffn2=norm_ffn2, norm_final=norm_final[None],
                 conv_w=conv_full, avec=zvec.at[0, DN_HEADS:2 * DN_HEADS].set(a_log[0]),
                 dvec=zvec.at[0, DN_HEADS:2 * DN_HEADS].set(dt_bias[0]), dn_norm=dn_norm)

    loss, grad_x, grads, sg = _local_step(x[0], loss_target[0], wts, small)

    rows = [sg["norm_ffn1"], sg["norm_mix"], sg["norm_ffn2"], sg["norm_final"], _pad_row(sg["a_log"]), _pad_row(sg["dt_bias"]),
            _pad_row(sg["dn_norm"]), _pad_row(loss[0:1]), sg["conv_w"].reshape(6, D_MODEL), jnp.zeros((2, D_MODEL), F32)]
    red = _allreduce_small(jnp.concatenate(rows, axis=0), "allreduce_small")
    loss_out = red[7, 0]
    g_conv_full = red[8:14].reshape(CONV_WIDTH, 3 * DN_WIDTH)
    g_conv = lax.dynamic_slice_in_dim(g_conv_full, chip * (3 * DN_WIDTH // N_CHIPS), 3 * DN_WIDTH // N_CHIPS, axis=1)
    g_small = dict(norm_ffn1=red[0:1], norm_mix=red[1:2], norm_ffn2=red[2:3], norm_final=red[3],
                   a_log=red[4:5, DN_HEADS:2 * DN_HEADS], dt_bias=red[5:6, DN_HEADS:2 * DN_HEADS], dn_norm=red[6:7, :DN_HEAD_DIM])

    gi = grads["w_in"]
    gi = jnp.concatenate([gi[:3072], gi[3584:3592], gi[3072:3584]], axis=0).reshape(N_CHIPS, IN_COLS // N_CHIPS, D_MODEL)
    grads["w_in"] = jnp.pad(gi, ((0, 0), (0, W_IN_ROWS - IN_COLS // N_CHIPS), (0, 0)))
    grads["w_out"] = grads["w_out"].reshape(N_CHIPS, D_MODEL // N_CHIPS, D_MODEL)
    gs = [grads[n] for n in BIG]
    from_sibling = _swap_sibling(gs, True, "rs_swap_halves")
    parts = _rs_add_pairs(gs, from_sibling, cc.reshape(1).astype(jnp.int32))
    recvs = _rs_exchange_arrays(parts)
    totals = _rs_add_totals(parts, recvs, chip.reshape(1).astype(jnp.int32))
    theirs = _swap_sibling(totals, False, "rs_share_total")
    shard_g = {}
    for n, mine, other in zip(BIG, totals, theirs):
        full = jnp.where(cc == 0, jnp.concatenate([mine, other], axis=0), jnp.concatenate([other, mine], axis=0))
        if n == "w_in":
            full = full[:IN_COLS // N_CHIPS]
        shard_g[n] = full if n in ROW_SHARDED else full.T

    out_g, out_d, out_m, out_v = {}, {}, {}, {}
    for n in BIG:
        d, nm, nv = _adamw(big_w[n], shard_g[n], big_m[n], big_v[n], "adamw_" + n)
        out_g[n], out_d[n], out_m[n], out_v[n] = shard_g[n][None], d[None], nm[None], nv[None]
    d, nm, nv = _adamw(conv_w[0], g_conv, m_conv_w[0], v_conv_w[0], "adamw_conv_w")
    out_g["conv_w"], out_d["conv_w"], out_m["conv_w"], out_v["conv_w"] = g_conv[None], d[None], nm[None], nv[None]

    small_names = ("norm_ffn1", "norm_mix", "norm_ffn2", "norm_final", "a_log", "dt_bias", "dn_norm")
    small_w = dict(norm_ffn1=norm_ffn1, norm_mix=norm_mix, norm_ffn2=norm_ffn2, norm_final=norm_final, a_log=a_log,
                   dt_bias=dt_bias, dn_norm=dn_norm)
    small_m = dict(norm_ffn1=m_norm_ffn1, norm_mix=m_norm_mix, norm_ffn2=m_norm_ffn2, norm_final=m_norm_final, a_log=m_a_log,
                   dt_bias=m_dt_bias, dn_norm=m_dn_norm)
    small_v = dict(norm_ffn1=v_norm_ffn1, norm_mix=v_norm_mix, norm_ffn2=v_norm_ffn2, norm_final=v_norm_final, a_log=v_a_log,
                   dt_bias=v_dt_bias, dn_norm=v_dn_norm)
    stack = lambda dct: jnp.concatenate([_pad_row(dct[n]) for n in small_names] + [jnp.zeros((1, D_MODEL), F32)], axis=0)
    d, nm, nv = _adamw(stack(small_w), stack(g_small), stack(small_m), stack(small_v), "adamw_small")
    for k, n in enumerate(small_names):
        shape = small_w[n].shape
        size = math.prod(shape)
        out_g[n] = g_small[n].reshape(shape)
        out_d[n], out_m[n], out_v[n] = (t[k, :size].reshape(shape) for t in (d, nm, nv))

    order = ("norm_ffn1", "ffn1_gate", "ffn1_up", "ffn1_down", "norm_mix", "w_in", "conv_w", "a_log", "dt_bias", "dn_norm",
             "w_out", "norm_ffn2", "ffn2_gate", "ffn2_up", "ffn2_down", "norm_final")
    return (loss_out, grad_x[None], *[out_g[n] for n in order], *[out_d[n] for n in order],
            *[out_m[n] for n in order], *[out_v[n] for n in order])
```

```python
import functools
import math

import jax
import jax.numpy as jnp
from jax import lax
from jax.experimental import pallas as pl
from jax.experimental.pallas import tpu as pltpu

F32 = jnp.float32
BF16 = jnp.bfloat16
HI = lax.Precision.HIGH

D_MODEL = 1024
D_FF = 2816
ATTN_HEADS = 8
ATTN_WIDTH = 512
ATTN_BLOCK = 128
DILATIONS = (1, 4, 16)
DN_HEADS = 4
DN_HEAD_DIM = 128
DN_WIDTH = 512
DN_CHUNK = 64
CONV_WIDTH = 4
NORM_EPS = 1e-6
L2_EPS = 1e-6
IN_COLS = 3592
IN_COLS_PADDED = 3712
N_CHIPS = 4

ADAM_LR = 0.001
ADAM_B1 = 0.9
ADAM_B2 = 0.999
ADAM_EPS = 1e-08
ADAM_WD = 0.01
ADAM_STEP = 10

VMEM_LIMIT = 56 * 1024 * 1024
NEG_BIG = -1e30
MESH = pl.DeviceIdType.MESH


def _params(n_grid, vmem=VMEM_LIMIT):
    return pltpu.CompilerParams(dimension_semantics=("arbitrary",) * n_grid, vmem_limit_bytes=vmem)


def _call(body, args, *, name, grid, in_specs, out_specs, out_shape, scratch_shapes=(), comm=None):
    n_in, n_out, n_scr = len(in_specs), len(out_specs), len(scratch_shapes)
    hbm = pl.BlockSpec(memory_space=pl.ANY)
    srcs, dst_shapes, n_sems, start, finish = comm if comm is not None else ((), (), 0, None, None)
    ns, nd = len(srcs), len(dst_shapes)

    def full(*refs):
        ins, c_src = refs[:n_in], refs[n_in:n_in + ns]
        at = n_in + ns
        outs, c_dst = refs[at:at + n_out], refs[at + n_out:at + n_out + nd]
        scr = refs[at + n_out + nd:at + n_out + nd + n_scr]
        if comm is not None:
            ids = [pl.program_id(a) for a in range(len(grid))]
            first = functools.reduce(jnp.logical_and, [i == 0 for i in ids])
            last = functools.reduce(jnp.logical_and, [i == g - 1 for i, g in zip(ids, grid)])

            @pl.when(first)
            def _():
                start(c_src, c_dst, refs[-2], refs[-1])

        body(*ins, *outs, *scr)
        if comm is not None:
            @pl.when(last)
            def _():
                finish(c_src, c_dst, refs[-2], refs[-1])

    sems = [pltpu.SemaphoreType.DMA((n_sems,)), pltpu.SemaphoreType.DMA((n_sems,))] if comm is not None else []
    res = pl.pallas_call(
        full, name=name, grid=grid, in_specs=list(in_specs) + [hbm] * ns, out_specs=list(out_specs) + [hbm] * nd,
        out_shape=list(out_shape) + list(dst_shapes), scratch_shapes=list(scratch_shapes) + sems,
        compiler_params=_params(len(grid)),
    )(*args, *srcs)
    return res[:n_out], res[n_out:]


def _nt(a, b, precision=None):
    return lax.dot_general(a, b, (((1,), (1,)), ((), ())), preferred_element_type=F32, precision=precision)


def _tn(a, b, precision=None):
    return lax.dot_general(a, b, (((0,), (0,)), ((), ())), preferred_element_type=F32, precision=precision)


def _nn(a, b, precision=None):
    return jnp.dot(a, b, preferred_element_type=F32, precision=precision)


def _sigmoid(x):
    return 1.0 / (1.0 + jnp.exp(-x))


def _ffn_fwd(x, gain, wg, wu, wd, name, comm=None):
    S, D = x.shape
    nf, _, tf = wg.shape
    tm = 512

    def body(x_ref, gain_ref, wg_ref, wu_ref, wd_ref, xo_ref, h_ref, g_ref, u_ref, acc_ref, hs_ref):
        j = pl.program_id(1)

        @pl.when(j == 0)
        def _():
            xf = x_ref[...]
            r = lax.rsqrt(jnp.mean(xf * xf, axis=-1, keepdims=True) + NORM_EPS)
            h = (xf * r * gain_ref[...]).astype(BF16)
            hs_ref[...] = h
            h_ref[...] = h
            acc_ref[...] = jnp.zeros_like(acc_ref)

        h = hs_ref[...]
        g = _nn(h, wg_ref[...])
        u = _nn(h, wu_ref[...])
        g_ref[...] = g.astype(BF16)
        u_ref[...] = u.astype(BF16)
        act = g * _sigmoid(g) * u
        acc_ref[...] += _nn(act.astype(BF16), wd_ref[...])

        @pl.when(j == nf - 1)
        def _():
            xo_ref[...] = x_ref[...] + 0.5 * acc_ref[...]

    return _call(
        body, (x, gain, wg, wu, wd), name=name, grid=(S // tm, nf), comm=comm,
        in_specs=[pl.BlockSpec((tm, D), lambda i, j: (i, 0)),
                  pl.BlockSpec((1, D), lambda i, j: (0, 0)),
                  pl.BlockSpec((None, D, tf), lambda i, j: (j, 0, 0)),
                  pl.BlockSpec((None, D, tf), lambda i, j: (j, 0, 0)),
                  pl.BlockSpec((None, tf, D), lambda i, j: (j, 0, 0))],
        out_specs=[pl.BlockSpec((tm, D), lambda i, j: (i, 0)),
                   pl.BlockSpec((tm, D), lambda i, j: (i, 0)),
                   pl.BlockSpec((None, tm, tf), lambda i, j: (j, i, 0)),
                   pl.BlockSpec((None, tm, tf), lambda i, j: (j, i, 0))],
        out_shape=[jax.ShapeDtypeStruct((S, D), F32), jax.ShapeDtypeStruct((S, D), BF16),
                   jax.ShapeDtypeStruct((nf, S, tf), BF16), jax.ShapeDtypeStruct((nf, S, tf), BF16)],
        scratch_shapes=[pltpu.VMEM((tm, D), F32), pltpu.VMEM((tm, D), BF16)])


def _rmsnorm_bwd(dh, xf, gain):
    r = lax.rsqrt(jnp.mean(xf * xf, axis=-1, keepdims=True) + NORM_EPS)
    xhat = xf * r
    dgain = jnp.sum(dh * xhat, axis=0, keepdims=True)
    dxh = dh * gain
    dx = r * (dxh - xhat * jnp.mean(dxh * xhat, axis=-1, keepdims=True))
    return dx, dgain


def _ffn_bwd(dxo, x, gain, g, u, wd, wg, wu, name, comm=None):
    S, D = x.shape
    nf, _, tf = g.shape
    tm = 512

    def body(dxo_ref, x_ref, gain_ref, g_ref, u_ref, wd_ref, wg_ref, wu_ref,
             dx_ref, dgain_ref, dg_ref, du_ref, act_ref, dout_ref, acc_ref, ds_ref):
        i = pl.program_id(0)
        j = pl.program_id(1)

        @pl.when(j == 0)
        def _():
            d = (0.5 * dxo_ref[...]).astype(BF16)
            ds_ref[...] = d
            dout_ref[...] = d
            acc_ref[...] = jnp.zeros_like(acc_ref)

        @pl.when((i == 0) & (j == 0))
        def _():
            dgain_ref[...] = jnp.zeros_like(dgain_ref)

        dact = _nt(ds_ref[...], wd_ref[...])
        gv = g_ref[...].astype(F32)
        uv = u_ref[...].astype(F32)
        sg = _sigmoid(gv)
        silu = gv * sg
        act_ref[...] = (silu * uv).astype(BF16)
        dgv = (dact * uv * (sg * (1.0 + gv * (1.0 - sg)))).astype(BF16)
        duv = (dact * silu).astype(BF16)
        dg_ref[...] = dgv
        du_ref[...] = duv
        acc_ref[...] += _nt(dgv, wg_ref[...]) + _nt(duv, wu_ref[...])

        @pl.when(j == nf - 1)
        def _():
            dx, dgain = _rmsnorm_bwd(acc_ref[...], x_ref[...], gain_ref[...])
            dx_ref[...] = dxo_ref[...] + dx
            dgain_ref[...] += dgain

    return _call(
        body, (dxo, x, gain, g, u, wd, wg, wu), name=name, grid=(S // tm, nf), comm=comm,
        in_specs=[pl.BlockSpec((tm, D), lambda i, j: (i, 0)),
                  pl.BlockSpec((tm, D), lambda i, j: (i, 0)),
                  pl.BlockSpec((1, D), lambda i, j: (0, 0)),
                  pl.BlockSpec((None, tm, tf), lambda i, j: (j, i, 0)),
                  pl.BlockSpec((None, tm, tf), lambda i, j: (j, i, 0)),
                  pl.BlockSpec((None, tf, D), lambda i, j: (j, 0, 0)),
                  pl.BlockSpec((None, D, tf), lambda i, j: (j, 0, 0)),
                  pl.BlockSpec((None, D, tf), lambda i, j: (j, 0, 0))],
        out_specs=[pl.BlockSpec((tm, D), lambda i, j: (i, 0)),
                   pl.BlockSpec((1, D), lambda i, j: (0, 0)),
                   pl.BlockSpec((None, tm, tf), lambda i, j: (j, i, 0)),
                   pl.BlockSpec((None, tm, tf), lambda i, j: (j, i, 0)),
                   pl.BlockSpec((None, tm, tf), lambda i, j: (j, i, 0)),
                   pl.BlockSpec((tm, D), lambda i, j: (i, 0))],
        out_shape=[jax.ShapeDtypeStruct((S, D), F32), jax.ShapeDtypeStruct((1, D), F32),
                   jax.ShapeDtypeStruct((nf, S, tf), BF16), jax.ShapeDtypeStruct((nf, S, tf), BF16),
                   jax.ShapeDtypeStruct((nf, S, tf), BF16), jax.ShapeDtypeStruct((S, D), BF16)],
        scratch_shapes=[pltpu.VMEM((tm, D), F32), pltpu.VMEM((tm, D), BF16)])


def _matmul_tn(a, b, tm, tk, name):
    K, M = a.shape
    N = b.shape[1]

    def body(a_ref, b_ref, o_ref):
        @pl.when(pl.program_id(1) == 0)
        def _():
            o_ref[...] = jnp.zeros_like(o_ref)

        o_ref[...] += _tn(a_ref[...], b_ref[...])

    return pl.pallas_call(
        body, name=name, grid=(M // tm, K // tk),
        in_specs=[pl.BlockSpec((tk, tm), lambda i, k: (k, i)),
                  pl.BlockSpec((tk, N), lambda i, k: (k, 0))],
        out_specs=pl.BlockSpec((tm, N), lambda i, k: (i, 0)),
        out_shape=jax.ShapeDtypeStruct((M, N), F32),
        compiler_params=_params(2),
    )(a, b)


def _dw_chunks(a, b, tk, name):
    nf, S, tf = a.shape
    N = b.shape[1]

    def body(a_ref, b_ref, o_ref):
        @pl.when(pl.program_id(1) == 0)
        def _():
            o_ref[...] = jnp.zeros_like(o_ref)

        o_ref[...] += _tn(a_ref[...], b_ref[...])

    return pl.pallas_call(
        body, name=name, grid=(nf, S // tk),
        in_specs=[pl.BlockSpec((None, tk, tf), lambda j, k: (j, k, 0)),
                  pl.BlockSpec((tk, N), lambda j, k: (k, 0))],
        out_specs=pl.BlockSpec((None, tf, N), lambda j, k: (j, 0, 0)),
        out_shape=jax.ShapeDtypeStruct((nf, tf, N), F32),
        compiler_params=_params(2),
    )(a, b)


VIEW_TILE = 512


def _view_spec(d, tile=VIEW_TILE):
    return pl.BlockSpec((tile // d, d * ATTN_WIDTH), lambda i: (i, 0))


def _view_shape(S, d, dtype):
    return jax.ShapeDtypeStruct((S // d, d * ATTN_WIDTH), dtype)


def _tile_to_views(val, planes, out_refs):
    for g in range(4):
        planes[g] = val[:, g * 128:(g + 1) * 128]
    for d, ref in zip(DILATIONS, out_refs):
        if d == 1:
            ref[...] = val.astype(ref.dtype)
            continue
        for r in range(d):
            for g in range(4):
                ref[:, r * ATTN_WIDTH + g * 128:r * ATTN_WIDTH + (g + 1) * 128] = (
                    planes[g, pl.ds(r, planes.shape[1] // d, stride=d), :].astype(ref.dtype))


def _view_to_tile(ref, d, planes):
    if d == 1:
        return ref[...]
    for r in range(d):
        for g in range(4):
            planes[g, pl.ds(r, planes.shape[1] // d, stride=d), :] = ref[:, r * ATTN_WIDTH + g * 128:r * ATTN_WIDTH + (g + 1) * 128]
    return jnp.concatenate([planes[g] for g in range(4)], axis=1)


def _inproj_fwd(x, gain, w_in_p):
    S, D = x.shape
    tm = VIEW_TILE
    W = ATTN_WIDTH

    def body(x_ref, gain_ref, w_ref, h_ref, q1, q4, q16, k1, k4, k16, v1, v4, v16, dq_ref, dk_ref, dv_ref, gate_ref, bd_ref,
             planes):
        xf = x_ref[...]
        r = lax.rsqrt(jnp.mean(xf * xf, axis=-1, keepdims=True) + NORM_EPS)
        h = (xf * r * gain_ref[...]).astype(BF16)
        h_ref[...] = h
        _tile_to_views(_nn(h, w_ref[:, 0:W]) * 0.125, planes, (q1, q4, q16))
        _tile_to_views(_nn(h, w_ref[:, W:2 * W]), planes, (k1, k4, k16))
        _tile_to_views(_nn(h, w_ref[:, 2 * W:3 * W]), planes, (v1, v4, v16))
        dq_ref[...] = _nn(h, w_ref[:, 3 * W:4 * W])
        dk_ref[...] = _nn(h, w_ref[:, 4 * W:5 * W])
        dv_ref[...] = _nn(h, w_ref[:, 5 * W:6 * W])
        gate_ref[...] = _nn(h, w_ref[:, 6 * W:7 * W])
        bd_ref[...] = _nn(h, w_ref[:, 7 * W:7 * W + 128])

    tok = lambda w: pl.BlockSpec((tm, w), lambda i: (i, 0))
    return pl.pallas_call(
        body, name="inproj_fwd", grid=(S // tm,),
        in_specs=[tok(D), pl.BlockSpec((1, D), lambda i: (0, 0)),
                  pl.BlockSpec((D, IN_COLS_PADDED), lambda i: (0, 0))],
        out_specs=[tok(D)] + [_view_spec(d) for d in DILATIONS] * 3 + [tok(W)] * 4 + [tok(128)],
        out_shape=[jax.ShapeDtypeStruct((S, D), BF16)] + [_view_shape(S, d, BF16) for d in DILATIONS] * 3
                  + [jax.ShapeDtypeStruct((S, W), F32)] * 4 + [jax.ShapeDtypeStruct((S, 128), F32)],
        scratch_shapes=[pltpu.VMEM((4, tm, 128), F32)],
        compiler_params=_params(1),
    )(x, gain, w_in_p)


def _inproj_bwd(dxo, x, gain, attn_grads, dsecs, dbd, w_in_p):
    S, D = x.shape
    tm = VIEW_TILE // 2
    W = ATTN_WIDTH

    def body(dxo_ref, x_ref, gain_ref, *rest):
        views, (s3, s4, s5, s6, dbd_ref, w_ref, dx_ref, dgain_ref, dproj_ref, planes) = rest[:9], rest[9:]

        @pl.when(pl.program_id(0) == 0)
        def _():
            dgain_ref[...] = jnp.zeros_like(dgain_ref)

        secs = []
        for k in range(3):
            parts = [_view_to_tile(views[3 * k + p], d, planes) for p, d in enumerate(DILATIONS)]
            secs.append(parts[0] + parts[1] + parts[2])
        secs += [s3[...], s4[...], s5[...], s6[...]]
        dh = jnp.zeros((tm, D), F32)
        for k, s in enumerate(secs):
            d = s.astype(BF16)
            dproj_ref[:, k * W:(k + 1) * W] = d
            dh += _nt(d, w_ref[:, k * W:(k + 1) * W])
        d = dbd_ref[...].astype(BF16)
        dproj_ref[:, 7 * W:7 * W + 128] = d
        dh += _nt(d, w_ref[:, 7 * W:7 * W + 128])
        dx, dgain = _rmsnorm_bwd(dh, x_ref[...], gain_ref[...])
        dx_ref[...] = dxo_ref[...] + dx
        dgain_ref[...] += dgain

    tok = lambda w: pl.BlockSpec((tm, w), lambda i: (i, 0))
    return pl.pallas_call(
        body, name="inproj_bwd", grid=(S // tm,),
        in_specs=[tok(D), tok(D), pl.BlockSpec((1, D), lambda i: (0, 0))] + [_view_spec(d, tm) for d in DILATIONS] * 3
                 + [tok(W)] * 4 + [tok(128)] + [pl.BlockSpec((D, IN_COLS_PADDED), lambda i: (0, 0))],
        out_specs=[tok(D), pl.BlockSpec((1, D), lambda i: (0, 0)), tok(IN_COLS_PADDED)],
        out_shape=[jax.ShapeDtypeStruct((S, D), F32), jax.ShapeDtypeStruct((1, D), F32),
                   jax.ShapeDtypeStruct((S, IN_COLS_PADDED), BF16)],
        scratch_shapes=[pltpu.VMEM((4, tm, 128), F32)],
        compiler_params=_params(1),
    )(dxo, x, gain, *[g for grads in attn_grads for g in grads], *dsecs, dbd, w_in_p)


def _slope(h):
    return 2.0 ** (-8.0 * (h + 1) / ATTN_HEADS)


def _attn_fwd(q, k, v, d, name):
    L = q.shape[0]
    nb = L // ATTN_BLOCK
    B = ATTN_BLOCK

    def body(q_ref, kp_ref, kc_ref, vp_ref, vc_ref, acc_ref, m_ref, l_ref):
        n = pl.program_id(1)
        qi = lax.broadcasted_iota(jnp.int32, (B, 2 * B), 0)
        kj = lax.broadcasted_iota(jnp.int32, (B, 2 * B), 1)
        steps = qi + B - kj
        valid = (steps >= 0) & (steps <= B) & ((kj >= B) | (n > 0))
        stepsf = steps.astype(F32)
        lo = lax.broadcasted_iota(jnp.int32, (B, 128), 1) < 64
        for G in range(4):
            sl = slice(G * 128, (G + 1) * 128)
            qg = q_ref[:, sl]
            kg = jnp.concatenate([kp_ref[:, sl], kc_ref[:, sl]], axis=0)
            vg = jnp.concatenate([vp_ref[:, sl], vc_ref[:, sl]], axis=0)
            res = []
            for half in (0, 1):
                msk = lo if half == 0 else jnp.logical_not(lo)
                qm = jnp.where(msk, qg, jnp.zeros_like(qg))
                s = _nt(qm, kg)
                s = jnp.where(valid, s - (_slope(2 * G + half) * d) * stepsf, NEG_BIG)
                m = jnp.max(s, axis=-1, keepdims=True)
                p = jnp.exp(s - m)
                l = jnp.sum(p, axis=-1, keepdims=True)
                a = _nn(p.astype(BF16), vg)
                res.append((a, m, l))
            (a0, m0, l0), (a1, m1, l1) = res
            acc_ref[:, sl] = jnp.where(lo, a0, a1)
            m_ref[:, sl] = jnp.where(lo, m0, m1)
            l_ref[:, sl] = jnp.where(lo, l0, l1)

    cur = pl.BlockSpec((B, ATTN_WIDTH), lambda r, n: (n, r))
    prev = pl.BlockSpec((B, ATTN_WIDTH), lambda r, n: (jnp.maximum(n - 1, 0), r))
    return pl.pallas_call(
        body, name=name, grid=(d, nb),
        in_specs=[cur, prev, cur, prev, cur],
        out_specs=[cur, cur, cur],
        out_shape=[jax.ShapeDtypeStruct((L, d * ATTN_WIDTH), F32)] * 3,
        compiler_params=_params(2),
    )(q, k, k, v, v)


def _attn_merge(parts):
    S = parts[0][0].shape[0]
    tm = VIEW_TILE

    def body(a1, m1, l1, a2, m2, l2, a3, m3, l3, o_ref, lse1, lse4, lse16, planes):
        ins = ((a1, m1, l1), (a2, m2, l2), (a3, m3, l3))
        acc, ms, ls = [], [], []
        for d, (a, m, l) in zip(DILATIONS, ins):
            acc.append(_view_to_tile(a, d, planes))
            ms.append(_view_to_tile(m, d, planes))
            ls.append(_view_to_tile(l, d, planes))
        mx = jnp.maximum(jnp.maximum(ms[0], ms[1]), ms[2])
        es = [jnp.exp(m - mx) for m in ms]
        den = es[0] * ls[0] + es[1] * ls[1] + es[2] * ls[2]
        num = es[0] * acc[0] + es[1] * acc[1] + es[2] * acc[2]
        o_ref[...] = num / den
        _tile_to_views(mx + jnp.log(den), planes, (lse1, lse4, lse16))

    views = [_view_spec(d) for d in DILATIONS]
    flat = [t for p in parts for t in p]
    return pl.pallas_call(
        body, name="attn_merge", grid=(S // tm,),
        in_specs=[views[p] for p in range(3) for _ in range(3)],
        out_specs=[views[0]] + views,
        out_shape=[jax.ShapeDtypeStruct((S, ATTN_WIDTH), F32)] + [_view_shape(S, d, F32) for d in DILATIONS],
        scratch_shapes=[pltpu.VMEM((4, tm, 128), F32)],
        compiler_params=_params(1),
    )(*flat)


def _head_col(t, msk, big):
    if big:
        return jnp.max(jnp.where(msk, t, NEG_BIG), axis=-1, keepdims=True)
    return jnp.sum(jnp.where(msk, t, 0.0), axis=-1, keepdims=True) * (1.0 / 64.0)


def _attn_bwd_q(q, k, v, do, lse, dd, d, name):
    L = q.shape[0]
    nb = L // ATTN_BLOCK
    B = ATTN_BLOCK

    def body(q_ref, kp_ref, kc_ref, vp_ref, vc_ref, do_ref, lse_ref, dd_ref, dq_ref):
        n = pl.program_id(1)
        qi = lax.broadcasted_iota(jnp.int32, (B, 2 * B), 0)
        kj = lax.broadcasted_iota(jnp.int32, (B, 2 * B), 1)
        steps = qi + B - kj
        valid = (steps >= 0) & (steps <= B) & ((kj >= B) | (n > 0))
        stepsf = steps.astype(F32)
        lo = lax.broadcasted_iota(jnp.int32, (B, 128), 1) < 64
        for G in range(4):
            sl = slice(G * 128, (G + 1) * 128)
            qg = q_ref[:, sl]
            kg = jnp.concatenate([kp_ref[:, sl], kc_ref[:, sl]], axis=0)
            vg = jnp.concatenate([vp_ref[:, sl], vc_ref[:, sl]], axis=0)
            dog = do_ref[:, sl]
            res = []
            for half in (0, 1):
                msk = lo if half == 0 else jnp.logical_not(lo)
                qm = jnp.where(msk, qg, jnp.zeros_like(qg))
                s = _nt(qm, kg) - (_slope(2 * G + half) * d) * stepsf
                lse_c = _head_col(lse_ref[:, sl], msk, True)
                p = jnp.where(valid, jnp.exp(jnp.where(valid, s, NEG_BIG) - lse_c), 0.0)
                dom = jnp.where(msk, dog, 0.0).astype(BF16)
                dp = _nt(dom, vg)
                dcol = _head_col(dd_ref[:, sl], msk, False)
                ds = p * (dp - dcol)
                res.append(_nn(ds.astype(BF16), kg) * 0.125)
            dq_ref[:, sl] = jnp.where(lo, res[0], res[1])

    cur = pl.BlockSpec((B, ATTN_WIDTH), lambda r, n: (n, r))
    prev = pl.BlockSpec((B, ATTN_WIDTH), lambda r, n: (jnp.maximum(n - 1, 0), r))
    return pl.pallas_call(
        body, name=name, grid=(d, nb), in_specs=[cur, prev, cur, prev, cur, cur, cur, cur], out_specs=cur,
        out_shape=jax.ShapeDtypeStruct((L, d * ATTN_WIDTH), F32), compiler_params=_params(2),
    )(q, k, k, v, v, do, lse, dd)


def _attn_bwd_kv(q, k, v, do, lse, dd, d, name):
    L = q.shape[0]
    nb = L // ATTN_BLOCK
    B = ATTN_BLOCK

    def body(k_ref, v_ref, qc_ref, qn_ref, doc_ref, don_ref, lsec_ref, lsen_ref, ddc_ref, ddn_ref, dk_ref, dv_ref):
        j = pl.program_id(1)
        qrow = lax.broadcasted_iota(jnp.int32, (2 * B, B), 0)
        kk = lax.broadcasted_iota(jnp.int32, (2 * B, B), 1)
        steps = qrow - kk
        valid = (steps >= 0) & (steps <= B) & ((qrow < B) | (j < nb - 1))
        stepsf = steps.astype(F32)
        lo2 = lax.broadcasted_iota(jnp.int32, (2 * B, 128), 1) < 64
        lo = lax.broadcasted_iota(jnp.int32, (B, 128), 1) < 64
        for G in range(4):
            sl = slice(G * 128, (G + 1) * 128)
            kg = k_ref[:, sl]
            vg = v_ref[:, sl]
            qq = jnp.concatenate([qc_ref[:, sl], qn_ref[:, sl]], axis=0)
            doo = jnp.concatenate([doc_ref[:, sl], don_ref[:, sl]], axis=0)
            lse2 = jnp.concatenate([lsec_ref[:, sl], lsen_ref[:, sl]], axis=0)
            dd2 = jnp.concatenate([ddc_ref[:, sl], ddn_ref[:, sl]], axis=0)
            doo_b = doo.astype(BF16)
            dks, dvs = [], []
            for half in (0, 1):
                msk = lo2 if half == 0 else jnp.logical_not(lo2)
                qm = jnp.where(msk, qq, jnp.zeros_like(qq))
                s = _nt(qm, kg) - (_slope(2 * G + half) * d) * stepsf
                lse_c = _head_col(lse2, msk, True)
                p = jnp.where(valid, jnp.exp(jnp.where(valid, s, NEG_BIG) - lse_c), 0.0)
                dvs.append(_tn(p.astype(BF16), doo_b))
                dom = jnp.where(msk, doo, 0.0).astype(BF16)
                dp = _nt(dom, vg)
                dcol = _head_col(dd2, msk, False)
                ds = p * (dp - dcol)
                dks.append(_tn(ds.astype(BF16), qq))
            dk_ref[:, sl] = jnp.where(lo, dks[0], dks[1])
            dv_ref[:, sl] = jnp.where(lo, dvs[0], dvs[1])

    cur = pl.BlockSpec((B, ATTN_WIDTH), lambda r, j: (j, r))
    nxt = pl.BlockSpec((B, ATTN_WIDTH), lambda r, j: (jnp.minimum(j + 1, nb - 1), r))
    return pl.pallas_call(
        body, name=name, grid=(d, nb), in_specs=[cur, cur, cur, nxt, cur, nxt, cur, nxt, cur, nxt], out_specs=[cur, cur],
        out_shape=[jax.ShapeDtypeStruct((L, d * ATTN_WIDTH), F32)] * 2, compiler_params=_params(2),
    )(k, v, q, q, do, do, lse, lse, dd, dd)


CONV_T = 512
HALO = 8


def _conv_taps(pad_ref, w, T):
    acc = pad_ref[pl.ds(HALO - 3, T), :] * w[0:1, :]
    for j in range(1, CONV_WIDTH):
        acc = acc + pad_ref[pl.ds(HALO - 3 + j, T), :] * w[j:j + 1, :]
    return acc


def _conv_fwd(xq, xk, xv, conv_w):
    S = xq.shape[0]
    T = CONV_T

    def body(xq_ref, xqh_ref, xk_ref, xkh_ref, xv_ref, xvh_ref, wq_ref, wk_ref, wv_ref,
             qn_ref, kn_ref, v_ref, pad_ref):
        i = pl.program_id(0)

        def act(x_ref, xh_ref, w_ref):
            pad_ref[pl.ds(0, HALO), :] = jnp.where(i > 0, xh_ref[...], 0.0)
            pad_ref[pl.ds(HALO, T), :] = x_ref[...]
            c = _conv_taps(pad_ref, w_ref[...], T)
            return c * _sigmoid(c)

        def l2n(t):
            return t * lax.rsqrt(jnp.sum(t * t, axis=-1, keepdims=True) + L2_EPS)

        qn_ref[...] = l2n(act(xq_ref, xqh_ref, wq_ref))
        kn_ref[...] = l2n(act(xk_ref, xkh_ref, wk_ref))
        v_ref[...] = act(xv_ref, xvh_ref, wv_ref)

    tile = pl.BlockSpec((T, 128), lambda i, h: (i, h))
    halo = pl.BlockSpec((HALO, 128), lambda i, h: (jnp.maximum(i * (T // HALO) - 1, 0), h))
    wspec = lambda sec: pl.BlockSpec((CONV_WIDTH, 128), lambda i, h, sec=sec: (0, 4 * sec + h))
    return pl.pallas_call(
        body, name="dn_conv_fwd", grid=(S // T, DN_HEADS),
        in_specs=[tile, halo, tile, halo, tile, halo, wspec(0), wspec(1), wspec(2)],
        out_specs=[tile, tile, tile],
        out_shape=[jax.ShapeDtypeStruct((S, DN_WIDTH), F32)] * 3,
        scratch_shapes=[pltpu.VMEM((T + HALO, 128), F32)],
        compiler_params=_params(2),
    )(xq, xq, xk, xk, xv, xv, conv_w, conv_w, conv_w)


def _conv_bwd_pre(xq, xk, xv, conv_w, dqn, dkn, dv):
    S = xq.shape[0]
    T = CONV_T

    def body(xq_ref, xqh_ref, xk_ref, xkh_ref, xv_ref, xvh_ref, wq_ref, wk_ref, wv_ref,
             dqn_ref, dkn_ref, dv_ref, dcq_ref, dck_ref, dcv_ref, dwq_ref, dwk_ref, dwv_ref, pad_ref):
        i = pl.program_id(1)

        def one(x_ref, xh_ref, w_ref, dy_ref, dc_ref, dw_ref, normed):
            pad_ref[pl.ds(0, HALO), :] = jnp.where(i > 0, xh_ref[...], 0.0)
            pad_ref[pl.ds(HALO, T), :] = x_ref[...]
            c = _conv_taps(pad_ref, w_ref[...], T)
            sg = _sigmoid(c)
            a = c * sg
            dy = dy_ref[...]
            if normed:
                r = lax.rsqrt(jnp.sum(a * a, axis=-1, keepdims=True) + L2_EPS)
                y = a * r
                da = r * (dy - y * jnp.sum(dy * y, axis=-1, keepdims=True))
            else:
                da = dy
            dc = da * (sg * (1.0 + c * (1.0 - sg)))
            dc_ref[...] = dc

            @pl.when(i == 0)
            def _():
                dw_ref[...] = jnp.zeros_like(dw_ref)

            rows = [jnp.sum(dc * pad_ref[pl.ds(HALO - 3 + j, T), :], axis=0, keepdims=True) for j in range(CONV_WIDTH)]
            dw_ref[...] += jnp.concatenate(rows + [jnp.zeros((8 - CONV_WIDTH, 128), F32)], axis=0)

        one(xq_ref, xqh_ref, wq_ref, dqn_ref, dcq_ref, dwq_ref, True)
        one(xk_ref, xkh_ref, wk_ref, dkn_ref, dck_ref, dwk_ref, True)
        one(xv_ref, xvh_ref, wv_ref, dv_ref, dcv_ref, dwv_ref, False)

    tile = pl.BlockSpec((T, 128), lambda h, i: (i, h))
    halo = pl.BlockSpec((HALO, 128), lambda h, i: (jnp.maximum(i * (T // HALO) - 1, 0), h))
    wspec = lambda sec: pl.BlockSpec((CONV_WIDTH, 128), lambda h, i, sec=sec: (0, 4 * sec + h))
    dwspec = pl.BlockSpec((8, 128), lambda h, i: (0, h))
    return pl.pallas_call(
        body, name="dn_conv_bwd_pre", grid=(DN_HEADS, S // T),
        in_specs=[tile, halo, tile, halo, tile, halo, wspec(0), wspec(1), wspec(2), tile, tile, tile],
        out_specs=[tile, tile, tile, dwspec, dwspec, dwspec],
        out_shape=[jax.ShapeDtypeStruct((S, DN_WIDTH), F32)] * 3 + [jax.ShapeDtypeStruct((8, DN_WIDTH), F32)] * 3,
        scratch_shapes=[pltpu.VMEM((T + HALO, 128), F32)],
        compiler_params=_params(2),
    )(xq, xq, xk, xk, xv, xv, conv_w, conv_w, conv_w, dqn, dkn, dv)


def _conv_bwd_x(dcq, dck, dcv, conv_w):
    S = dcq.shape[0]
    T = CONV_T
    nt = S // T

    def body(dq_ref, dqh_ref, dk_ref, dkh_ref, dv_ref, dvh_ref, wq_ref, wk_ref, wv_ref,
             oq_ref, ok_ref, ov_ref, pad_ref):
        i = pl.program_id(0)

        def one(d_ref, dh_ref, w_ref, o_ref):
            pad_ref[pl.ds(0, T), :] = d_ref[...]
            pad_ref[pl.ds(T, HALO), :] = jnp.where(i < nt - 1, dh_ref[...], 0.0)
            w = w_ref[...]
            acc = pad_ref[pl.ds(3, T), :] * w[0:1, :]
            for j in range(1, CONV_WIDTH):
                acc = acc + pad_ref[pl.ds(3 - j, T), :] * w[j:j + 1, :]
            o_ref[...] = acc

        one(dq_ref, dqh_ref, wq_ref, oq_ref)
        one(dk_ref, dkh_ref, wk_ref, ok_ref)
        one(dv_ref, dvh_ref, wv_ref, ov_ref)

    tile = pl.BlockSpec((T, 128), lambda i, h: (i, h))
    halo = pl.BlockSpec((HALO, 128), lambda i, h: (jnp.minimum((i + 1) * (T // HALO), S // HALO - 1), h))
    wspec = lambda sec: pl.BlockSpec((CONV_WIDTH, 128), lambda i, h, sec=sec: (0, 4 * sec + h))
    return pl.pallas_call(
        body, name="dn_conv_bwd_x", grid=(nt, DN_HEADS),
        in_specs=[tile, halo, tile, halo, tile, halo, wspec(0), wspec(1), wspec(2)],
        out_specs=[tile, tile, tile],
        out_shape=[jax.ShapeDtypeStruct((S, DN_WIDTH), F32)] * 3,
        scratch_shapes=[pltpu.VMEM((T + HALO, 128), F32)],
        compiler_params=_params(2),
    )(dcq, dcq, dck, dck, dcv, dcv, conv_w, conv_w, conv_w)


PREP_CHUNKS = 2
SCAN_CHUNKS = 8


def _bnn(a, b):
    return lax.dot_general(a, b, (((2,), (1,)), ((0,), (0,))), preferred_element_type=F32, precision=HI)


def _bnt(a, b):
    return lax.dot_general(a, b, (((2,), (2,)), ((0,), (0,))), preferred_element_type=F32, precision=HI)


def _btn(a, b):
    return lax.dot_general(a, b, (((1,), (1,)), ((0,), (0,))), preferred_element_type=F32, precision=HI)


def _tri_inverse_b(a, blk, eye):
    dg = jnp.where(blk, a, 0.0)
    lo = a - dg
    d2 = _bnn(dg, dg)
    d4 = _bnn(d2, d2)
    d8 = _bnn(d4, d4)
    td = _bnn(_bnn(_bnn(eye - dg, eye + d2), eye + d4), eye + d8)
    b = _bnn(td, lo)
    b2 = _bnn(b, b)
    return _bnn(_bnn(eye - b, eye + b2), td)


def _dn_common_b(bds, avec, dvec, q_raw, k, v, t=None):
    C = DN_CHUNK
    lane = lax.broadcasted_iota(jnp.int32, (C, 128), 1)
    row = lax.broadcasted_iota(jnp.int32, (1, C, C), 1)
    col = lax.broadcasted_iota(jnp.int32, (1, C, C), 2)
    incl = row >= col
    strict = row > col
    eye = (row == col).astype(F32)
    blk = (row // 16) == (col // 16)
    pick = lambda tile, ln: jnp.sum(jnp.where(lane == ln, tile, 0.0), axis=-1, keepdims=True)
    betas, graws, zcs = [], [], []
    for bd in bds:
        z = bd + dvec
        g_all = -jnp.exp(avec) * (jnp.maximum(z, 0.0) + jnp.log(1.0 + jnp.exp(-jnp.abs(z))))
        beta_all = _sigmoid(bd)
        for h in range(DN_HEADS):
            betas.append(pick(beta_all, h))
            graws.append(pick(g_all, DN_HEADS + h))
            zcs.append(pick(z, DN_HEADS + h))
    beta, graw, zc = jnp.stack(betas), jnp.stack(graws), jnp.stack(zcs)
    to_row = lambda c: jnp.sum(eye * c, axis=1, keepdims=True)
    gc = jnp.sum(jnp.where(incl, to_row(graw), 0.0), axis=-1, keepdims=True)
    decay = jnp.exp(jnp.where(incl, gc - to_row(gc), NEG_BIG))
    q = q_raw * (DN_HEAD_DIM ** -0.5)
    kb = k * beta
    kk = _bnt(kb, k)
    if t is None:
        t = _tri_inverse_b(jnp.where(strict, kk * decay, 0.0), blk, eye)
    eg = jnp.exp(gc)
    rhs_w = kb * eg
    u = _bnn(t, v * beta)
    w = _bnn(t, rhs_w)
    qk = _bnt(q, k)
    aq = jnp.where(incl, qk * decay, 0.0)
    last = lax.broadcasted_iota(jnp.int32, (1, C, 1), 1) == C - 1
    g_last = jnp.sum(jnp.where(last, gc, 0.0), axis=1, keepdims=True)
    ekd = jnp.exp(g_last - gc)
    return dict(beta=beta, graw=graw, zc=zc, gc=gc, decay=decay, q=q, kb=kb, kk=kk, t=t, eg=eg, rhs_w=rhs_w,
                u=u, w=w, qk=qk, aq=aq, g_last=g_last, ekd=ekd, kd=k * ekd, qg=q * eg,
                incl=incl, strict=strict, eye=eye, lane=lane, row=row, col=col, last=last)


def _stack_heads(ref, rows):
    return jnp.stack([ref[rows, h * DN_HEAD_DIM:(h + 1) * DN_HEAD_DIM] for h in range(DN_HEADS)])


def _stack_units(ref, nc):
    C = DN_CHUNK
    return jnp.concatenate([_stack_heads(ref, slice(ci * C, (ci + 1) * C)) for ci in range(nc)], axis=0)


def _store_units(ref, val, nc):
    C = DN_CHUNK
    for ci in range(nc):
        for h in range(DN_HEADS):
            ref[ci * C:(ci + 1) * C, h * DN_HEAD_DIM:(h + 1) * DN_HEAD_DIM] = val[ci * DN_HEADS + h]


def _dn_prep(qn, kn, v, bd, avec, dvec):
    S = qn.shape[0]
    C = DN_CHUNK
    N = S // C
    nc = PREP_CHUNKS

    def body(q_ref, k_ref, v_ref, bd_ref, a_ref, d_ref, u_ref, w_ref, qg_ref, kd_ref, aq_ref, t_ref, egl_ref):
        bds = [bd_ref[ci * C:(ci + 1) * C, :] for ci in range(nc)]
        c = _dn_common_b(bds, a_ref[...], d_ref[...], _stack_units(q_ref, nc), _stack_units(k_ref, nc), _stack_units(v_ref, nc))
        _store_units(u_ref, c["u"], nc)
        _store_units(w_ref, c["w"], nc)
        _store_units(qg_ref, c["qg"], nc)
        _store_units(kd_ref, c["kd"], nc)
        egl = jnp.broadcast_to(jnp.exp(c["g_last"]), (nc * DN_HEADS, 1, 128))
        for ci in range(nc):
            for h in range(DN_HEADS):
                aq_ref[h, ci * C:(ci + 1) * C, :] = c["aq"][ci * DN_HEADS + h]
                t_ref[h, ci * C:(ci + 1) * C, :] = c["t"][ci * DN_HEADS + h]
            egl_ref[ci * 8:(ci + 1) * 8, :] = jnp.concatenate(
                [egl[ci * DN_HEADS + h] for h in range(DN_HEADS)] + [jnp.zeros((8 - DN_HEADS, 128), F32)], axis=0)

    tok = lambda w: pl.BlockSpec((nc * C, w), lambda n: (n, 0))
    sq = pl.BlockSpec((DN_HEADS, nc * C, C), lambda n: (0, n, 0))
    vec = pl.BlockSpec((1, 128), lambda n: (0, 0))
    return pl.pallas_call(
        body, name="dn_prep", grid=(N // nc,),
        in_specs=[tok(DN_WIDTH)] * 3 + [tok(128), vec, vec],
        out_specs=[tok(DN_WIDTH)] * 4 + [sq, sq, pl.BlockSpec((nc * 8, 128), lambda n: (n, 0))],
        out_shape=[jax.ShapeDtypeStruct((S, DN_WIDTH), F32)] * 4 + [jax.ShapeDtypeStruct((DN_HEADS, S, C), F32)] * 2
                  + [jax.ShapeDtypeStruct((N * 8, 128), F32)],
        compiler_params=_params(1),
    )(qn, kn, v, bd, avec, dvec)


def _dn_scan_fwd(u, w, qg, kd, aq, egl, gate, dn_gain):
    S = u.shape[0]
    C = DN_CHUNK
    N = S // C
    HD = DN_HEAD_DIM
    nc = SCAN_CHUNKS

    def body(u_ref, w_ref, qg_ref, kd_ref, aq_ref, egl_ref, gate_ref, gain_ref, dn_ref, o_ref, vn_ref, st_ref, state_ref):
        @pl.when(pl.program_id(0) == 0)
        def _():
            state_ref[...] = jnp.zeros_like(state_ref)

        gain = gain_ref[...]
        for ci in range(nc):
            rows = slice(ci * C, (ci + 1) * C)
            st = state_ref[...]
            for h in range(DN_HEADS):
                st_ref[ci * DN_WIDTH + h * HD:ci * DN_WIDTH + (h + 1) * HD, :] = st[h]
            v_new = _stack_heads(u_ref, rows) - _bnn(_stack_heads(w_ref, rows), st)
            o = _bnn(_stack_heads(qg_ref, rows), st) + _bnn(aq_ref[:, rows, :], v_new)
            egl = jnp.stack([egl_ref[ci * 8 + h:ci * 8 + h + 1, :] for h in range(DN_HEADS)])
            state_ref[...] = st * egl + _btn(_stack_heads(kd_ref, rows), v_new)
            r = lax.rsqrt(jnp.mean(o * o, axis=-1, keepdims=True) + NORM_EPS)
            gt = _stack_heads(gate_ref, rows)
            dn = o * r * gain * (gt * _sigmoid(gt))
            for h in range(DN_HEADS):
                sl = slice(h * HD, (h + 1) * HD)
                vn_ref[rows, sl] = v_new[h]
                o_ref[rows, sl] = o[h]
                dn_ref[rows, sl] = dn[h]

    tok = lambda wd: pl.BlockSpec((nc * C, wd), lambda n: (n, 0))
    sq = pl.BlockSpec((DN_HEADS, nc * C, C), lambda n: (0, n, 0))
    vec = pl.BlockSpec((1, 128), lambda n: (0, 0))
    return pl.pallas_call(
        body, name="dn_scan_fwd", grid=(N // nc,),
        in_specs=[tok(DN_WIDTH)] * 4 + [sq, pl.BlockSpec((nc * 8, 128), lambda n: (n, 0)), tok(DN_WIDTH), vec],
        out_specs=[tok(DN_WIDTH)] * 3 + [pl.BlockSpec((nc * DN_WIDTH, HD), lambda n: (n, 0))],
        out_shape=[jax.ShapeDtypeStruct((S, DN_WIDTH), F32)] * 3 + [jax.ShapeDtypeStruct((N * DN_WIDTH, HD), F32)],
        scratch_shapes=[pltpu.VMEM((DN_HEADS, HD, HD), F32)],
        compiler_params=_params(1),
    )(u, w, qg, kd, aq, egl, gate, dn_gain)


def _dn_scan_bwd(w, qg, kd, aq, egl, gate, dn_gain, o, ddn):
    S = w.shape[0]
    C = DN_CHUNK
    N = S // C
    HD = DN_HEAD_DIM
    nc = SCAN_CHUNKS

    def body(w_ref, qg_ref, kd_ref, aq_ref, egl_ref, gate_ref, gain_ref, o_ref, ddn_ref,
             do_ref, dvn_ref, dgate_ref, dst_ref, small_ref, dstate_ref):
        @pl.when(pl.program_id(0) == 0)
        def _():
            dstate_ref[...] = jnp.zeros_like(dstate_ref)
            small_ref[...] = jnp.zeros_like(small_ref)

        gain = gain_ref[...]
        d_gain = jnp.zeros((1, 128), F32)
        for ci in reversed(range(nc)):
            rows = slice(ci * C, (ci + 1) * C)
            dsn = dstate_ref[...]
            for h in range(DN_HEADS):
                dst_ref[ci * DN_WIDTH + h * HD:ci * DN_WIDTH + (h + 1) * HD, :] = dsn[h]
            ov = _stack_heads(o_ref, rows)
            r = lax.rsqrt(jnp.mean(ov * ov, axis=-1, keepdims=True) + NORM_EPS)
            on = ov * r
            gt = _stack_heads(gate_ref, rows)
            sgt = _sigmoid(gt)
            silu_g = gt * sgt
            dy = _stack_heads(ddn_ref, rows)
            d_gain = d_gain + jnp.sum(jnp.sum(dy * on * silu_g, axis=1, keepdims=True), axis=0)
            dgate = dy * on * gain * (sgt * (1.0 + gt * (1.0 - sgt)))
            don = dy * gain * silu_g
            do = r * (don - on * jnp.mean(don * on, axis=-1, keepdims=True))
            d_vnew = _btn(aq_ref[:, rows, :], do) + _bnn(_stack_heads(kd_ref, rows), dsn)
            egl = jnp.stack([egl_ref[ci * 8 + h:ci * 8 + h + 1, :] for h in range(DN_HEADS)])
            dstate_ref[...] = _btn(_stack_heads(qg_ref, rows), do) + dsn * egl - _btn(_stack_heads(w_ref, rows), d_vnew)
            for h in range(DN_HEADS):
                sl = slice(h * HD, (h + 1) * HD)
                do_ref[rows, sl] = do[h]
                dvn_ref[rows, sl] = d_vnew[h]
                dgate_ref[rows, sl] = dgate[h]
        small_ref[...] += jnp.concatenate([d_gain, jnp.zeros((7, 128), F32)], axis=0)

    nb = N // nc
    tok = lambda wd: pl.BlockSpec((nc * C, wd), lambda i: (nb - 1 - i, 0))
    sq = pl.BlockSpec((DN_HEADS, nc * C, C), lambda i: (0, nb - 1 - i, 0))
    vec = pl.BlockSpec((1, 128), lambda i: (0, 0))
    return pl.pallas_call(
        body, name="dn_scan_bwd", grid=(nb,),
        in_specs=[tok(DN_WIDTH)] * 3 + [sq, pl.BlockSpec((nc * 8, 128), lambda i: (nb - 1 - i, 0)), tok(DN_WIDTH), vec,
                                       tok(DN_WIDTH), tok(DN_WIDTH)],
        out_specs=[tok(DN_WIDTH)] * 3 + [pl.BlockSpec((nc * DN_WIDTH, HD), lambda i: (nb - 1 - i, 0)),
                                        pl.BlockSpec((8, 128), lambda i: (0, 0))],
        out_shape=[jax.ShapeDtypeStruct((S, DN_WIDTH), F32)] * 3 + [jax.ShapeDtypeStruct((N * DN_WIDTH, HD), F32),
                                                                  jax.ShapeDtypeStruct((8, 128), F32)],
        scratch_shapes=[pltpu.VMEM((DN_HEADS, HD, HD), F32)],
        compiler_params=_params(1),
    )(w, qg, kd, aq, egl, gate, dn_gain, o, ddn)


def _dn_post(qn, kn, v, bd, avec, dvec, t_inv, v_new_all, states, dstates, do_all, dvn_all, comm=None):
    S = qn.shape[0]
    C = DN_CHUNK
    N = S // C
    HD = DN_HEAD_DIM
    nc = PREP_CHUNKS
    B = nc * DN_HEADS

    def body(q_ref, k_ref, v_ref, bd_ref, a_ref, d_ref, t_ref, vn_ref, st_ref, dst_ref, do_ref, dvn_ref,
             dq_ref, dk_ref, dv_ref, dbd_ref, small_ref):
        @pl.when(pl.program_id(0) == 0)
        def _():
            small_ref[...] = jnp.zeros_like(small_ref)

        avec = a_ref[...]
        bds = [bd_ref[ci * C:(ci + 1) * C, :] for ci in range(nc)]
        k = _stack_units(k_ref, nc)
        vv = _stack_units(v_ref, nc)
        t = jnp.concatenate([t_ref[:, ci * C:(ci + 1) * C, :] for ci in range(nc)], axis=0)
        c = _dn_common_b(bds, avec, d_ref[...], _stack_units(q_ref, nc), k, vv, t=t)
        q, kb, eg, u, w = c["q"], c["kb"], c["eg"], c["u"], c["w"]
        beta, decay, incl, strict, eye = c["beta"], c["decay"], c["incl"], c["strict"], c["eye"]
        st = jnp.stack([st_ref[b * HD:(b + 1) * HD, :] for b in range(B)])
        dsn = jnp.stack([dst_ref[b * HD:(b + 1) * HD, :] for b in range(B)])
        v_new = _stack_units(vn_ref, nc)
        do = _stack_units(do_ref, nc)
        d_vnew = _stack_units(dvn_ref, nc)
        egl = jnp.exp(c["g_last"])
        daq = jnp.where(incl, _bnt(do, v_new), 0.0)
        d_qg = _bnt(do, st)
        d_kd = _bnt(v_new, dsn)
        d_glast = jnp.sum(jnp.sum(dsn * st, axis=-1, keepdims=True), axis=1, keepdims=True) * egl
        d_w = -_bnt(d_vnew, st)
        d_ru = _btn(t, d_vnew)
        d_rw = _btn(t, d_w)
        da = -jnp.where(strict, _bnt(d_ru, u) + _bnt(d_rw, w), 0.0)
        dv = d_ru * beta
        dbeta = jnp.sum(d_ru * vv, axis=-1, keepdims=True)
        dkb = d_rw * eg
        dgc = jnp.sum(d_rw * c["rhs_w"], axis=-1, keepdims=True)
        dkk = da * decay
        ddecay = da * c["kk"]
        dkb = dkb + _bnn(dkk, k)
        dk = _btn(dkk, kb)
        dqk = daq * decay
        ddecay = ddecay + daq * c["qk"]
        dq = _bnn(dqk, k)
        dk = dk + _btn(dqk, q)
        m = ddecay * decay
        col_sum = jnp.sum(m, axis=1, keepdims=True)
        dgc = dgc + jnp.sum(m, axis=-1, keepdims=True) - jnp.sum(eye * col_sum, axis=-1, keepdims=True)
        dq = dq + d_qg * eg
        dgc = dgc + jnp.sum(d_qg * c["qg"], axis=-1, keepdims=True)
        dk = dk + d_kd * c["ekd"]
        tk = jnp.sum(d_kd * c["kd"], axis=-1, keepdims=True)
        dgc = dgc - tk
        d_glast = d_glast + jnp.sum(tk, axis=1, keepdims=True)
        dk = dk + dkb * beta
        dbeta = dbeta + jnp.sum(dkb * k, axis=-1, keepdims=True)
        dgc = dgc + jnp.where(c["last"], d_glast, 0.0)
        dgc_row = jnp.sum(eye * dgc, axis=1, keepdims=True)
        dgraw = jnp.sum(jnp.where(c["col"] >= c["row"], dgc_row, 0.0), axis=-1, keepdims=True)
        _store_units(dq_ref, dq * (HD ** -0.5), nc)
        _store_units(dk_ref, dk, nc)
        _store_units(dv_ref, dv, nc)
        dbraw = dbeta * beta * (1.0 - beta)
        dzc = dgraw * _sigmoid(c["zc"])
        ga = dgraw * c["graw"]
        lane = c["lane"]
        lane1 = lax.broadcasted_iota(jnp.int32, (1, 128), 1)
        neg_ea = -jnp.exp(avec)
        d_alog = jnp.zeros((1, 128), F32)
        d_dt = jnp.zeros((1, 128), F32)
        for ci in range(nc):
            dbd = jnp.zeros((C, 128), F32)
            for h in range(DN_HEADS):
                b = ci * DN_HEADS + h
                dz = dzc[b] * neg_ea
                dbd = dbd + jnp.where(lane == h, dbraw[b], 0.0) + jnp.where(lane == DN_HEADS + h, dz, 0.0)
                d_alog = d_alog + jnp.where(lane1 == DN_HEADS + h, jnp.sum(ga[b], axis=0, keepdims=True), 0.0)
                d_dt = d_dt + jnp.where(lane1 == DN_HEADS + h, jnp.sum(dz, axis=0, keepdims=True), 0.0)
            dbd_ref[ci * C:(ci + 1) * C, :] = dbd
        small_ref[...] += jnp.concatenate([d_alog, d_dt, jnp.zeros((6, 128), F32)], axis=0)

    tok = lambda wd: pl.BlockSpec((nc * C, wd), lambda n: (n, 0))
    big = pl.BlockSpec((nc * DN_WIDTH, HD), lambda n: (n, 0))
    sq = pl.BlockSpec((DN_HEADS, nc * C, C), lambda n: (0, n, 0))
    vec = pl.BlockSpec((1, 128), lambda n: (0, 0))
    return _call(
        body, (qn, kn, v, bd, avec, dvec, t_inv, v_new_all, states, dstates, do_all, dvn_all),
        name="dn_post", grid=(N // nc,), comm=comm,
        in_specs=[tok(DN_WIDTH)] * 3 + [tok(128), vec, vec, sq, tok(DN_WIDTH), big, big, tok(DN_WIDTH), tok(DN_WIDTH)],
        out_specs=[tok(DN_WIDTH)] * 3 + [tok(128), pl.BlockSpec((8, 128), lambda n: (0, 0))],
        out_shape=[jax.ShapeDtypeStruct((S, DN_WIDTH), F32)] * 3 + [jax.ShapeDtypeStruct((S, 128), F32),
                                                                  jax.ShapeDtypeStruct((8, 128), F32)])


def _outproj_fwd(x, attn, dn, w_out):
    S, D = x.shape
    tm = 512

    def body(x_ref, a_ref, d_ref, w_ref, xo_ref, mix_ref):
        a = a_ref[...].astype(BF16)
        dd = d_ref[...].astype(BF16)
        mix_ref[:, 0:ATTN_WIDTH] = a
        mix_ref[:, ATTN_WIDTH:] = dd
        xo_ref[...] = x_ref[...] + _nn(a, w_ref[0:ATTN_WIDTH, :]) + _nn(dd, w_ref[ATTN_WIDTH:, :])

    tok = lambda w: pl.BlockSpec((tm, w), lambda i: (i, 0))
    return pl.pallas_call(
        body, name="outproj_fwd", grid=(S // tm,),
        in_specs=[tok(D), tok(ATTN_WIDTH), tok(DN_WIDTH), pl.BlockSpec((D, D), lambda i: (0, 0))],
        out_specs=[tok(D), tok(D)],
        out_shape=[jax.ShapeDtypeStruct((S, D), F32), jax.ShapeDtypeStruct((S, D), BF16)],
        compiler_params=_params(1),
    )(x, attn, dn, w_out)


def _outproj_bwd(dx, w_out, attn):
    S, D = dx.shape
    tm = VIEW_TILE

    def body(dx_ref, w_ref, attn_ref, da1, da4, da16, dl1, dl4, dl16, ddn_ref, dxb_ref, planes):
        d = dx_ref[...].astype(BF16)
        dxb_ref[...] = d
        da = _nt(d, w_ref[0:ATTN_WIDTH, :])
        ddn_ref[...] = _nt(d, w_ref[ATTN_WIDTH:, :])
        _tile_to_views(da, planes, (da1, da4, da16))
        lo = lax.broadcasted_iota(jnp.int32, (tm, 128), 1) < 64
        cols = []
        for G in range(4):
            sl = slice(G * 128, (G + 1) * 128)
            t = da[:, sl] * attn_ref[:, sl]
            d0 = jnp.sum(jnp.where(lo, t, 0.0), axis=-1, keepdims=True)
            d1 = jnp.sum(jnp.where(lo, 0.0, t), axis=-1, keepdims=True)
            cols.append(jnp.where(lo, d0, d1))
        _tile_to_views(jnp.concatenate(cols, axis=1), planes, (dl1, dl4, dl16))

    tok = lambda w: pl.BlockSpec((tm, w), lambda i: (i, 0))
    views = [_view_spec(d) for d in DILATIONS]
    return pl.pallas_call(
        body, name="outproj_bwd", grid=(S // tm,),
        in_specs=[tok(D), pl.BlockSpec((D, D), lambda i: (0, 0)), tok(ATTN_WIDTH)],
        out_specs=views + views + [tok(DN_WIDTH), tok(D)],
        out_shape=[_view_shape(S, d, F32) for d in DILATIONS] * 2
                  + [jax.ShapeDtypeStruct((S, DN_WIDTH), F32), jax.ShapeDtypeStruct((S, D), BF16)],
        scratch_shapes=[pltpu.VMEM((4, tm, 128), F32)],
        compiler_params=_params(1),
    )(dx, w_out, attn)


def _loss_head(x, gain, target):
    S, D = x.shape
    tm = 512

    def body(x_ref, gain_ref, t_ref, loss_ref, dx_ref, dgain_ref):
        @pl.when(pl.program_id(0) == 0)
        def _():
            loss_ref[...] = jnp.zeros_like(loss_ref)
            dgain_ref[...] = jnp.zeros_like(dgain_ref)

        xf = x_ref[...]
        gain = gain_ref[...]
        r = lax.rsqrt(jnp.mean(xf * xf, axis=-1, keepdims=True) + NORM_EPS)
        xhat = xf * r
        err = xhat * gain - t_ref[...]
        part = 0.5 * jnp.sum(jnp.mean(err * err, axis=-1, keepdims=True), axis=0, keepdims=True)
        first = (lax.broadcasted_iota(jnp.int32, (8, 128), 0) == 0) & (lax.broadcasted_iota(jnp.int32, (8, 128), 1) == 0)
        loss_ref[...] += jnp.where(first, part, 0.0)
        dy = err * (1.0 / D)
        dgain_ref[...] += jnp.sum(dy * xhat, axis=0, keepdims=True)
        dxh = dy * gain
        dx_ref[...] = r * (dxh - xhat * jnp.mean(dxh * xhat, axis=-1, keepdims=True))

    tok = pl.BlockSpec((tm, D), lambda i: (i, 0))
    row = pl.BlockSpec((1, D), lambda i: (0, 0))
    return pl.pallas_call(
        body, name="loss_head", grid=(S // tm,),
        in_specs=[tok, row, tok],
        out_specs=[pl.BlockSpec((8, 128), lambda i: (0, 0)), tok, row],
        out_shape=[jax.ShapeDtypeStruct((8, 128), F32), jax.ShapeDtypeStruct((S, D), F32),
                   jax.ShapeDtypeStruct((1, D), F32)],
        compiler_params=_params(1),
    )(x, gain, target)


def _adamw(w, g, m, v, name):
    R, Ccols = w.shape
    tr = R
    for cand in (256, 128, 64, 32, 16, 8):
        if R % cand == 0:
            tr = cand
            break
    c1 = 1.0 - ADAM_B1 ** ADAM_STEP
    c2 = 1.0 - ADAM_B2 ** ADAM_STEP

    def body(w_ref, g_ref, m_ref, v_ref, d_ref, nm_ref, nv_ref):
        gv = g_ref[...]
        mn = ADAM_B1 * m_ref[...] + (1.0 - ADAM_B1) * gv
        vn = ADAM_B2 * v_ref[...] + (1.0 - ADAM_B2) * (gv * gv)
        nm_ref[...] = mn
        nv_ref[...] = vn
        d_ref[...] = -ADAM_LR * ((mn / c1) / (jnp.sqrt(vn / c2) + ADAM_EPS) + ADAM_WD * w_ref[...])

    spec = pl.BlockSpec((tr, Ccols), lambda i: (i, 0))
    return pl.pallas_call(
        body, name=name, grid=(R // tr,), in_specs=[spec] * 4, out_specs=[spec] * 3,
        out_shape=[jax.ShapeDtypeStruct((R, Ccols), F32)] * 3, compiler_params=_params(1),
    )(w, g, m, v)


LATE_WEIGHTS = ("w_out", "ffn2_gate", "ffn2_up", "ffn2_down")


def _local_step(x, target, wts, small, dist=None):
    g1, g2, gm, gf = small["norm_ffn1"], small["norm_ffn2"], small["norm_mix"], small["norm_final"]
    wts = dict(wts)

    def reduce_start(gs, tag):
        return _rs_add_pairs(gs, _swap_sibling(gs, True, "rs_swap_halves_" + tag), dist["c"], "rs_add_pairs_" + tag)

    (x1, h1, fg1, fu1), late = _ffn_fwd(x, g1, wts["ffn1_gate"], wts["ffn1_up"], wts["ffn1_down"], "ffn1_fwd",
                                        comm=_ag_comm(dist["late"]) if dist else None)
    if dist:
        wts.update(zip(LATE_WEIGHTS, late))
        wts["w_out"] = wts["w_out"].reshape(D_MODEL, D_MODEL)
    h2, *qkv, xq, xk, xv, gate, bd = _inproj_fwd(x1, gm, wts["w_in"])
    aq, ak, av = qkv[0:3], qkv[3:6], qkv[6:9]
    parts = [_attn_fwd(aq[p], ak[p], av[p], d, f"attn_fwd_d{d}") for p, d in enumerate(DILATIONS)]
    attn, *lse = _attn_merge(parts)
    conv_w = small["conv_w"]
    qn, kn, vv = _conv_fwd(xq, xk, xv, conv_w)
    dn_u, dn_w, dn_qg, dn_kd, dn_aq, dn_t, dn_egl = _dn_prep(qn, kn, vv, bd, small["avec"], small["dvec"])
    dn, o_dn, v_new, states = _dn_scan_fwd(dn_u, dn_w, dn_qg, dn_kd, dn_aq, dn_egl, gate, small["dn_norm"])
    x2, mix = _outproj_fwd(x1, attn, dn, wts["w_out"])
    (x3, h3, fg2, fu2), _ = _ffn_fwd(x2, g2, wts["ffn2_gate"], wts["ffn2_up"], wts["ffn2_down"], "ffn2_fwd")
    loss, dx3, d_gf = _loss_head(x3, gf, target)

    grads = {}
    (dx2, d_g2, dfg2, dfu2, act2, dout2), _ = _ffn_bwd(dx3, x2, g2, fg2, fu2, wts["ffn2_down"], wts["ffn2_gate"],
                                                      wts["ffn2_up"], "ffn2_bwd")
    tk = 512
    grads["ffn2_gate"] = _dw_chunks(dfg2, h3, tk, "dw_ffn2_gate")
    grads["ffn2_up"] = _dw_chunks(dfu2, h3, tk, "dw_ffn2_up")
    grads["ffn2_down"] = _dw_chunks(act2, dout2, tk, "dw_ffn2_down")
    group_a = ("ffn2_gate", "ffn2_up", "ffn2_down")
    parts_a = reduce_start([grads[n] for n in group_a], "a") if dist else None

    *dviews, ddn, dx2b = _outproj_bwd(dx2, wts["w_out"], attn)
    dattn, dd = dviews[0:3], dviews[3:6]
    grads["w_out"] = _matmul_tn(mix, dx2b, D_MODEL, tk, "dw_out").reshape(N_CHIPS, D_MODEL // N_CHIPS, D_MODEL)

    daq, dak, dav = [], [], []
    for p, d in enumerate(DILATIONS):
        daq.append(_attn_bwd_q(aq[p], ak[p], av[p], dattn[p], lse[p], dd[p], d, f"attn_bwd_q_d{d}"))
        dk_p, dv_p = _attn_bwd_kv(aq[p], ak[p], av[p], dattn[p], lse[p], dd[p], d, f"attn_bwd_kv_d{d}")
        dak.append(dk_p)
        dav.append(dv_p)

    do_dn, dvn, dgate, dstates, d_dn_gain = _dn_scan_bwd(dn_w, dn_qg, dn_kd, dn_aq, dn_egl, gate, small["dn_norm"], o_dn, ddn)
    (dqn, dkn, dvv, dbd, dn_small), recv_a = _dn_post(qn, kn, vv, bd, small["avec"], small["dvec"], dn_t, v_new, states,
                                                      dstates, do_dn, dvn, comm=_rsx_comm(parts_a) if dist else None)
    dcq, dck, dcv, dwq, dwk, dwv = _conv_bwd_pre(xq, xk, xv, conv_w, dqn, dkn, dvv)
    dxq, dxk, dxv = _conv_bwd_x(dcq, dck, dcv, conv_w)
    d_conv = jnp.concatenate([dwq[:CONV_WIDTH], dwk[:CONV_WIDTH], dwv[:CONV_WIDTH]], axis=1)

    dx1, d_gm, dproj = _inproj_bwd(dx2, x1, gm, [daq, dak, dav], [dxq, dxk, dxv, dgate], dbd, wts["w_in"])
    gi = _matmul_tn(dproj, h2, IN_COLS_PADDED, 256, "dw_in")
    if dist:
        gi = jnp.concatenate([gi[:3072], gi[3584:3592], gi[3072:3584]], axis=0).reshape(N_CHIPS, IN_COLS // N_CHIPS, D_MODEL)
        gi = jnp.pad(gi, ((0, 0), (0, W_IN_ROWS - IN_COLS // N_CHIPS), (0, 0)))
    grads["w_in"] = gi
    group_b = ("w_in", "w_out")
    parts_b = reduce_start([grads[n] for n in group_b], "b") if dist else None

    (dx0, d_g1, dfg1, dfu1, act1, dout1), recv_b = _ffn_bwd(dx1, x, g1, fg1, fu1, wts["ffn1_down"], wts["ffn1_gate"],
                                                           wts["ffn1_up"], "ffn1_bwd",
                                                           comm=_rsx_comm(parts_b) if dist else None)
    grads["ffn1_gate"] = _dw_chunks(dfg1, h1, tk, "dw_ffn1_gate")
    grads["ffn1_up"] = _dw_chunks(dfu1, h1, tk, "dw_ffn1_up")
    grads["ffn1_down"] = _dw_chunks(act1, dout1, tk, "dw_ffn1_down")

    small_grads = dict(norm_ffn1=d_g1, norm_mix=d_gm, norm_ffn2=d_g2, norm_final=d_gf, conv_w=d_conv,
                       a_log=dn_small[0:1], dt_bias=dn_small[1:2], dn_norm=d_dn_gain[0:1])
    if dist:
        group_c = ("ffn1_gate", "ffn1_up", "ffn1_down")
        parts_c = reduce_start([grads[n] for n in group_c], "c")
        recv_c = _rs_exchange_arrays(parts_c)
        names = group_a + group_b + group_c
        totals = _rs_add_totals(list(parts_a) + list(parts_b) + list(parts_c), list(recv_a) + list(recv_b) + list(recv_c),
                                dist["chip"])
        theirs = _swap_sibling(totals, False, "rs_share_total")
        grads = {n: (mine, other) for n, mine, other in zip(names, totals, theirs)}
    return loss, dx0, grads, small_grads


PACK_SECTIONS = (("ffn1_gate", 704), ("ffn1_up", 704), ("ffn1_down", 704), ("w_in", 898), ("w_out", 256),
                 ("ffn2_gate", 704), ("ffn2_up", 704), ("ffn2_down", 704))
PACK_ROWS = 5408
HALF_ROWS = PACK_ROWS // 2
ADD_ROWS = 208

HBM = pl.BlockSpec(memory_space=pl.ANY)
VMEM_SPEC = pl.BlockSpec(memory_space=pltpu.VMEM)


def _coords():
    return lax.axis_index("x"), lax.axis_index("y"), lax.axis_index("c")


def _remote(src, dst, send_sems, recv_sems, k, dev):
    return pltpu.make_async_remote_copy(src_ref=src, dst_ref=dst, send_sem=send_sems.at[k], recv_sem=recv_sems.at[k],
                                        device_id=dev, device_id_type=MESH)


def _allreduce_small(buf, name):
    R, Cc = buf.shape

    def body(src_ref, out_ref, recv_ref, send_sems, recv_sems):
        x, y, c = _coords()
        copies = []
        for m in range(1, 8):
            fx, fy, fc = (m >> 2) & 1, (m >> 1) & 1, m & 1
            dev = (x ^ fx if fx else x, y ^ fy if fy else y, c ^ fc if fc else c)
            cp = _remote(src_ref, recv_ref.at[m - 1], send_sems, recv_sems, m - 1, dev)
            cp.start()
            copies.append(cp)
        for cp in copies:
            cp.wait()
        r = [src_ref[...]] + [recv_ref[m] for m in range(7)]
        out_ref[...] = ((r[0] + r[1]) + (r[2] + r[3])) + ((r[4] + r[5]) + (r[6] + r[7]))

    return pl.pallas_call(
        body, name=name, out_shape=jax.ShapeDtypeStruct((R, Cc), F32),
        in_specs=[VMEM_SPEC], out_specs=VMEM_SPEC,
        scratch_shapes=[pltpu.VMEM((7, R, Cc), F32), pltpu.SemaphoreType.DMA((7,)), pltpu.SemaphoreType.DMA((7,))],
    )(buf)


def _allgather_weights(pack2):
    _, Hh, Cc = pack2.shape

    def body(src_ref, out_ref, send_sems, recv_sems):
        x, y, c = _coords()
        sib = (x, y, 1 - c)
        others = [(1 - x, y), (x, 1 - y), (1 - x, 1 - y)]
        blk = lambda cx, cy, half: out_ref.at[2 * cx + cy, half]
        mine = _remote(src_ref, out_ref.at[2 * x + y], send_sems, recv_sems, 6, sib)
        mine.start()
        first = [_remote(src_ref.at[c], blk(x, y, c), send_sems, recv_sems, j, (ox, oy, c)) for j, (ox, oy) in enumerate(others)]
        for cp in first:
            cp.start()
        passed = [_remote(blk(ox, oy, c), blk(ox, oy, c), send_sems, recv_sems, 3 + j, sib) for j, (ox, oy) in enumerate(others)]
        for j, (ox, oy) in enumerate(others):
            _remote(src_ref.at[c], blk(ox, oy, c), send_sems, recv_sems, j, (ox, oy, c)).wait_recv()
            passed[j].start()
        for j, (ox, oy) in enumerate(others):
            _remote(src_ref.at[c], blk(ox, oy, 1 - c), send_sems, recv_sems, 3 + j, sib).wait_recv()
        for cp in first + passed:
            cp.wait_send()
        mine.wait()

    return pl.pallas_call(
        body, name="allgather_weights", out_shape=jax.ShapeDtypeStruct((N_CHIPS, 2, Hh, Cc), pack2.dtype),
        in_specs=[HBM], out_specs=HBM,
        scratch_shapes=[pltpu.SemaphoreType.DMA((7,)), pltpu.SemaphoreType.DMA((7,))],
    )(pack2)


def _rs_swap_halves(gpack):
    _, nj, Hh, Cc = gpack.shape

    def body(src_ref, out_ref, send_sems, recv_sems):
        x, y, c = _coords()
        cp = _remote(src_ref.at[1 - c], out_ref, send_sems, recv_sems, 0, (x, y, 1 - c))
        cp.start()
        cp.wait()

    return pl.pallas_call(
        body, name="rs_swap_halves", out_shape=jax.ShapeDtypeStruct((nj, Hh, Cc), gpack.dtype),
        in_specs=[HBM], out_specs=HBM,
        scratch_shapes=[pltpu.SemaphoreType.DMA((1,)), pltpu.SemaphoreType.DMA((1,))],
    )(gpack)


def _rs_add_pair(gpack, other, c):
    _, nj, Hh, Cc = gpack.shape
    tr = ADD_ROWS

    def body(c_ref, a_ref, b_ref, o_ref):
        o_ref[...] = (a_ref[...] + b_ref[...]).astype(BF16)

    return pl.pallas_call(
        body, name="rs_add_pair",
        grid_spec=pltpu.PrefetchScalarGridSpec(
            num_scalar_prefetch=1, grid=(nj, Hh // tr),
            in_specs=[pl.BlockSpec((None, None, tr, Cc), lambda j, i, c_ref: (c_ref[0], j, i, 0)),
                      pl.BlockSpec((None, tr, Cc), lambda j, i, c_ref: (j, i, 0))],
            out_specs=pl.BlockSpec((None, tr, Cc), lambda j, i, c_ref: (j, i, 0))),
        out_shape=jax.ShapeDtypeStruct((nj, Hh, Cc), BF16),
        compiler_params=_params(2),
    )(c, gpack, other)


def _rs_exchange_chips(part):
    nj, Hh, Cc = part.shape

    def body(src_ref, out_ref, send_sems, recv_sems):
        x, y, c = _coords()
        others = [(1 - x, y), (x, 1 - y), (1 - x, 1 - y)]
        cps = [_remote(src_ref.at[2 * ox + oy], out_ref.at[k], send_sems, recv_sems, k, (ox, oy, c))
               for k, (ox, oy) in enumerate(others)]
        for cp in cps:
            cp.start()
        for cp in cps:
            cp.wait()

    return pl.pallas_call(
        body, name="rs_exchange_chips", out_shape=jax.ShapeDtypeStruct((3, Hh, Cc), part.dtype),
        in_specs=[HBM], out_specs=HBM,
        scratch_shapes=[pltpu.SemaphoreType.DMA((3,)), pltpu.SemaphoreType.DMA((3,))],
    )(part)


def _rs_add_total(part, recv, chip):
    nj, Hh, Cc = part.shape
    tr = ADD_ROWS

    def body(chip_ref, p_ref, r0_ref, r1_ref, r2_ref, o_ref):
        f = lambda r: r[...].astype(F32)
        o_ref[...] = (f(p_ref) + f(r0_ref)) + (f(r1_ref) + f(r2_ref))

    rk = lambda k: pl.BlockSpec((None, tr, Cc), lambda i, chip_ref, k=k: (k, i, 0))
    return pl.pallas_call(
        body, name="rs_add_total",
        grid_spec=pltpu.PrefetchScalarGridSpec(
            num_scalar_prefetch=1, grid=(Hh // tr,),
            in_specs=[pl.BlockSpec((None, tr, Cc), lambda i, chip_ref: (chip_ref[0], i, 0)), rk(0), rk(1), rk(2)],
            out_specs=pl.BlockSpec((tr, Cc), lambda i, chip_ref: (i, 0))),
        out_shape=jax.ShapeDtypeStruct((Hh, Cc), F32),
        compiler_params=_params(1),
    )(chip, part, recv, recv, recv)


def _rs_share_total(total):
    Hh, Cc = total.shape

    def body(src_ref, out_ref, send_sems, recv_sems):
        x, y, c = _coords()
        cp = _remote(src_ref, out_ref, send_sems, recv_sems, 0, (x, y, 1 - c))
        cp.start()
        cp.wait()

    return pl.pallas_call(
        body, name="rs_share_total", out_shape=jax.ShapeDtypeStruct((Hh, Cc), total.dtype),
        in_specs=[HBM], out_specs=HBM,
        scratch_shapes=[pltpu.SemaphoreType.DMA((1,)), pltpu.SemaphoreType.DMA((1,))],
    )(total)


BIG = ("ffn1_gate", "ffn1_up", "ffn1_down", "w_in", "w_out", "ffn2_gate", "ffn2_up", "ffn2_down")
W_IN_ROWS = 960


def _rows(ref, start, size):
    return ref.at[pl.ds(pl.multiple_of(start, 16), size)]


def _allgather_arrays(shards):
    n = len(shards)

    def body(*refs):
        _ag_start(refs[:n], refs[n:2 * n], refs[2 * n], refs[2 * n + 1])
        _ag_finish(refs[:n], refs[n:2 * n], refs[2 * n], refs[2 * n + 1])

    _, shapes, n_sems, _, _ = _ag_comm(shards)
    return pl.pallas_call(
        body, name="allgather_weights", out_shape=shapes, in_specs=[HBM] * n, out_specs=[HBM] * n,
        scratch_shapes=[pltpu.SemaphoreType.DMA((n_sems,)), pltpu.SemaphoreType.DMA((n_sems,))],
    )(*shards)


def _ag_copies(srcs, outs, send_sems, recv_sems):
    x, y, c = _coords()
    sib = (x, y, 1 - c)
    me = 2 * x + y
    others = [(1 - x, y), (x, 1 - y), (1 - x, 1 - y)]
    plan = []
    for a, (src, out) in enumerate(zip(srcs, outs)):
        h = src.shape[0] // 2
        cp = lambda s, d, k, dev: _remote(s, d, send_sems, recv_sems, 7 * a + k, dev)
        own = cp(src, out.at[me], 6, sib)
        sends = [cp(_rows(src, c * h, h), _rows(out.at[me], c * h, h), j, (ox, oy, c)) for j, (ox, oy) in enumerate(others)]
        mine = [_rows(out.at[2 * ox + oy], c * h, h) for ox, oy in others]
        theirs = [_rows(out.at[2 * ox + oy], (1 - c) * h, h) for ox, oy in others]
        arrivals = [cp(m, m, j, sib) for j, m in enumerate(mine)]
        forwards = [cp(m, m, 3 + j, sib) for j, m in enumerate(mine)]
        forwarded = [cp(t, t, 3 + j, sib) for j, t in enumerate(theirs)]
        plan.append((own, sends, forwards, arrivals, forwarded))
    return plan


def _ag_start(srcs, outs, send_sems, recv_sems):
    for own, sends, _, _, _ in _ag_copies(srcs, outs, send_sems, recv_sems):
        own.start()
        for cp in sends:
            cp.start()


def _ag_finish(srcs, outs, send_sems, recv_sems):
    plan = _ag_copies(srcs, outs, send_sems, recv_sems)
    for _, _, forwards, arrivals, _ in plan:
        for arrived, fwd in zip(arrivals, forwards):
            arrived.wait_recv()
            fwd.start()
    for own, sends, forwards, _, forwarded in plan:
        for cp in forwarded:
            cp.wait_recv()
        own.wait_recv()
        for cp in [own] + sends + forwards:
            cp.wait_send()


def _ag_comm(shards):
    shapes = [jax.ShapeDtypeStruct((N_CHIPS,) + s.shape, s.dtype) for s in shards]
    return (list(shards), shapes, 7 * len(shards), _ag_start, _ag_finish)


def _swap_sibling(arrs, pick_other_half, name):
    n = len(arrs)
    outs = [jax.ShapeDtypeStruct((a.shape[0], a.shape[1] // 2) + a.shape[2:] if pick_other_half else a.shape, a.dtype) for a in arrs]

    def body(*refs):
        srcs, dsts, send_sems, recv_sems = refs[:n], refs[n:2 * n], refs[2 * n], refs[2 * n + 1]
        x, y, c = _coords()
        cps = []
        for a in range(n):
            src = srcs[a]
            if pick_other_half:
                h = src.shape[1] // 2
                src = src.at[:, pl.ds(pl.multiple_of((1 - c) * h, 16), h)]
            cp = _remote(src, dsts[a], send_sems, recv_sems, a, (x, y, 1 - c))
            cp.start()
            cps.append(cp)
        for cp in cps:
            cp.wait()

    return pl.pallas_call(
        body, name=name, out_shape=outs, in_specs=[HBM] * n, out_specs=[HBM] * n,
        scratch_shapes=[pltpu.SemaphoreType.DMA((n,)), pltpu.SemaphoreType.DMA((n,))],
    )(*arrs)


def _rs_add_pairs(gs, others, c, name):
    n = len(gs)
    hbs = [g.shape[1] // 4 for g in gs]

    def body(c_ref, *refs):
        for a in range(n):
            refs[2 * n + a][...] = (refs[a][...] + refs[n + a][...]).astype(BF16)

    mine = lambda hb: pl.BlockSpec((None, hb, D_MODEL), lambda j, s, c_ref: (j, c_ref[0] * 2 + s, 0))
    flat = lambda hb: pl.BlockSpec((None, hb, D_MODEL), lambda j, s, c_ref: (j, s, 0))
    return pl.pallas_call(
        body, name=name,
        grid_spec=pltpu.PrefetchScalarGridSpec(
            num_scalar_prefetch=1, grid=(N_CHIPS, 2),
            in_specs=[mine(hb) for hb in hbs] + [flat(hb) for hb in hbs],
            out_specs=[flat(hb) for hb in hbs]),
        out_shape=[jax.ShapeDtypeStruct(o.shape, BF16) for o in others],
        compiler_params=_params(2),
    )(c, *gs, *others)


def _rs_exchange_arrays(parts):
    n = len(parts)

    def body(*refs):
        _rsx_start(refs[:n], refs[n:2 * n], refs[2 * n], refs[2 * n + 1])
        _rsx_finish(refs[:n], refs[n:2 * n], refs[2 * n], refs[2 * n + 1])

    _, shapes, n_sems, _, _ = _rsx_comm(parts)
    return pl.pallas_call(
        body, name="rs_exchange_chips", out_shape=shapes, in_specs=[HBM] * n, out_specs=[HBM] * n,
        scratch_shapes=[pltpu.SemaphoreType.DMA((n_sems,)), pltpu.SemaphoreType.DMA((n_sems,))],
    )(*parts)


def _rsx_copies(srcs, dsts, send_sems, recv_sems):
    x, y, c = _coords()
    others = [(1 - x, y), (x, 1 - y), (1 - x, 1 - y)]
    return [_remote(src.at[2 * ox + oy], dst.at[k], send_sems, recv_sems, 3 * a + k, (ox, oy, c))
            for a, (src, dst) in enumerate(zip(srcs, dsts)) for k, (ox, oy) in enumerate(others)]


def _rsx_start(srcs, dsts, send_sems, recv_sems):
    for cp in _rsx_copies(srcs, dsts, send_sems, recv_sems):
        cp.start()


def _rsx_finish(srcs, dsts, send_sems, recv_sems):
    for cp in _rsx_copies(srcs, dsts, send_sems, recv_sems):
        cp.wait()


def _rsx_comm(parts):
    shapes = [jax.ShapeDtypeStruct((3,) + p.shape[1:], p.dtype) for p in parts]
    return (list(parts), shapes, 3 * len(parts), _rsx_start, _rsx_finish)


def _rs_add_totals(parts, recvs, chip):
    n = len(parts)
    hbs = [p.shape[1] // 2 for p in parts]

    def body(chip_ref, *refs):
        f = lambda r: r[...].astype(F32)
        for a in range(n):
            p, r0, r1, r2 = refs[a], refs[n + 3 * a], refs[n + 3 * a + 1], refs[n + 3 * a + 2]
            refs[4 * n + a][...] = (f(p) + f(r0)) + (f(r1) + f(r2))

    own = lambda hb: pl.BlockSpec((None, hb, D_MODEL), lambda s, chip_ref: (chip_ref[0], s, 0))
    slot = lambda hb, k: pl.BlockSpec((None, hb, D_MODEL), lambda s, chip_ref, k=k: (k, s, 0))
    recv_specs = [slot(hb, k) for hb in hbs for k in range(3)]
    recv_args = [r for r in recvs for _ in range(3)]
    return pl.pallas_call(
        body, name="rs_add_totals",
        grid_spec=pltpu.PrefetchScalarGridSpec(
            num_scalar_prefetch=1, grid=(2,),
            in_specs=[own(hb) for hb in hbs] + recv_specs,
            out_specs=[pl.BlockSpec((hb, D_MODEL), lambda s, chip_ref: (s, 0)) for hb in hbs]),
        out_shape=[jax.ShapeDtypeStruct(p.shape[1:], F32) for p in parts],
        compiler_params=_params(1),
    )(chip, *parts, *recv_args)


def _permute_w_in(w):
    return jnp.concatenate([w[:, :3072], w[:, 3080:IN_COLS], w[:, 3072:3080],
                            jnp.zeros((w.shape[0], IN_COLS_PADDED - IN_COLS), w.dtype)], axis=1)


def _pack_rows(shards, dtype):
    rows = [shards[n].astype(dtype).reshape(r, D_MODEL) for n, r in PACK_SECTIONS]
    used = sum(r for _, r in PACK_SECTIONS)
    rows.append(jnp.zeros((PACK_ROWS - used, D_MODEL), dtype))
    return jnp.concatenate(rows, axis=0)


def _unpack_rows(pack, shapes):
    out, at = {}, 0
    for n, r in PACK_SECTIONS:
        out[n] = pack[at:at + r].reshape(shapes[n])
        at += r
    return out


SHARD_SHAPES = dict(ffn1_gate=(1024, 704), ffn1_up=(1024, 704), ffn1_down=(704, 1024), w_in=(1024, 898),
                    w_out=(256, 1024), ffn2_gate=(1024, 704), ffn2_up=(1024, 704), ffn2_down=(704, 1024))
ROW_SHARDED = ("ffn1_down", "w_out", "ffn2_down")
SMALL_ROWS = 16


def _pad_row(v):
    v = v.reshape(1, -1)
    return jnp.pad(v, ((0, 0), (0, D_MODEL - v.shape[1])))


def kernel(x, norm_ffn1, ffn1_gate, ffn1_up, ffn1_down, norm_mix, w_in, conv_w, a_log, dt_bias, dn_norm, w_out, norm_ffn2, ffn2_gate, ffn2_up, ffn2_down, norm_final, loss_target, m_norm_ffn1, m_ffn1_gate, m_ffn1_up, m_ffn1_down, m_norm_mix, m_w_in, m_conv_w, m_a_log, m_dt_bias, m_dn_norm, m_w_out, m_norm_ffn2, m_ffn2_gate, m_ffn2_up, m_ffn2_down, m_norm_final, v_norm_ffn1, v_ffn1_gate, v_ffn1_up, v_ffn1_down, v_norm_mix, v_w_in, v_conv_w, v_a_log, v_dt_bias, v_dn_norm, v_w_out, v_norm_ffn2, v_ffn2_gate, v_ffn2_up, v_ffn2_down, v_norm_final):
    cx, cy, cc = _coords()
    chip = 2 * cx + cy
    big_w = dict(ffn1_gate=ffn1_gate[0], ffn1_up=ffn1_up[0], ffn1_down=ffn1_down[0], w_in=w_in[0], w_out=w_out[0],
                 ffn2_gate=ffn2_gate[0], ffn2_up=ffn2_up[0], ffn2_down=ffn2_down[0])
    big_m = dict(ffn1_gate=m_ffn1_gate[0], ffn1_up=m_ffn1_up[0], ffn1_down=m_ffn1_down[0], w_in=m_w_in[0], w_out=m_w_out[0],
                 ffn2_gate=m_ffn2_gate[0], ffn2_up=m_ffn2_up[0], ffn2_down=m_ffn2_down[0])
    big_v = dict(ffn1_gate=v_ffn1_gate[0], ffn1_up=v_ffn1_up[0], ffn1_down=v_ffn1_down[0], w_in=v_w_in[0], w_out=v_w_out[0],
                 ffn2_gate=v_ffn2_gate[0], ffn2_up=v_ffn2_up[0], ffn2_down=v_ffn2_down[0])

    early = tuple(n for n in BIG if n not in LATE_WEIGHTS)
    wts = dict(zip(early, _allgather_arrays([big_w[n].astype(BF16) for n in early])))
    wts["w_in"] = _permute_w_in(jnp.concatenate([wts["w_in"][j] for j in range(N_CHIPS)], axis=1))
    dist = dict(late=[big_w[n].astype(BF16) for n in LATE_WEIGHTS], c=cc.reshape(1).astype(jnp.int32),
                chip=chip.reshape(1).astype(jnp.int32))

    conv_shard = conv_w[0]
    emb = jnp.concatenate([jnp.where((chip == j) & (cc == 0), conv_shard, 0.0) for j in range(N_CHIPS)], axis=1)
    emb = jnp.pad(emb.reshape(6, D_MODEL), ((0, 2), (0, 0)))
    conv_full = _allreduce_small(emb, "allgather_conv_w")[:6].reshape(CONV_WIDTH, 3 * DN_WIDTH)

    zvec = jnp.zeros((1, 128), F32)
    small = dict(norm_ffn1=norm_ffn1, norm_mix=norm_mix, norm_ffn2=norm_ffn2, norm_final=norm_final[None],
                 conv_w=conv_full, avec=zvec.at[0, DN_HEADS:2 * DN_HEADS].set(a_log[0]),
                 dvec=zvec.at[0, DN_HEADS:2 * DN_HEADS].set(dt_bias[0]), dn_norm=dn_norm)

    loss, grad_x, reduced, sg = _local_step(x[0], loss_target[0], wts, small, dist)

    rows = [sg["norm_ffn1"], sg["norm_mix"], sg["norm_ffn2"], sg["norm_final"], _pad_row(sg["a_log"]), _pad_row(sg["dt_bias"]),
            _pad_row(sg["dn_norm"]), _pad_row(loss[0:1]), sg["conv_w"].reshape(6, D_MODEL), jnp.zeros((2, D_MODEL), F32)]
    red = _allreduce_small(jnp.concatenate(rows, axis=0), "allreduce_small")
    loss_out = red[7, 0]
    g_conv_full = red[8:14].reshape(CONV_WIDTH, 3 * DN_WIDTH)
    g_conv = lax.dynamic_slice_in_dim(g_conv_full, chip * (3 * DN_WIDTH // N_CHIPS), 3 * DN_WIDTH // N_CHIPS, axis=1)
    g_small = dict(norm_ffn1=red[0:1], norm_mix=red[1:2], norm_ffn2=red[2:3], norm_final=red[3],
                   a_log=red[4:5, DN_HEADS:2 * DN_HEADS], dt_bias=red[5:6, DN_HEADS:2 * DN_HEADS], dn_norm=red[6:7, :DN_HEAD_DIM])

    shard_g = {}
    for n in BIG:
        mine, other = reduced[n]
        full = jnp.where(cc == 0, jnp.concatenate([mine, other], axis=0), jnp.concatenate([other, mine], axis=0))
        if n == "w_in":
            full = full[:IN_COLS // N_CHIPS]
        shard_g[n] = full if n in ROW_SHARDED else full.T

    out_g, out_d, out_m, out_v = {}, {}, {}, {}
    for n in BIG:
        d, nm, nv = _adamw(big_w[n], shard_g[n], big_m[n], big_v[n], "adamw_" + n)
        out_g[n], out_d[n], out_m[n], out_v[n] = shard_g[n][None], d[None], nm[None], nv[None]
    d, nm, nv = _adamw(conv_w[0], g_conv, m_conv_w[0], v_conv_w[0], "adamw_conv_w")
    out_g["conv_w"], out_d["conv_w"], out_m["conv_w"], out_v["conv_w"] = g_conv[None], d[None], nm[None], nv[None]

    small_names = ("norm_ffn1", "norm_mix", "norm_ffn2", "norm_final", "a_log", "dt_bias", "dn_norm")
    small_w = dict(norm_ffn1=norm_ffn1, norm_mix=norm_mix, norm_ffn2=norm_ffn2, norm_final=norm_final, a_log=a_log,
                   dt_bias=dt_bias, dn_norm=dn_norm)
    small_m = dict(norm_ffn1=m_norm_ffn1, norm_mix=m_norm_mix, norm_ffn2=m_norm_ffn2, norm_final=m_norm_final, a_log=m_a_log,
                   dt_bias=m_dt_bias, dn_norm=m_dn_norm)
    small_v = dict(norm_ffn1=v_norm_ffn1, norm_mix=v_norm_mix, norm_ffn2=v_norm_ffn2, norm_final=v_norm_final, a_log=v_a_log,
                   dt_bias=v_dt_bias, dn_norm=v_dn_norm)
    stack = lambda dct: jnp.concatenate([_pad_row(dct[n]) for n in small_names] + [jnp.zeros((1, D_MODEL), F32)], axis=0)
    d, nm, nv = _adamw(stack(small_w), stack(g_small), stack(small_m), stack(small_v), "adamw_small")
    for k, n in enumerate(small_names):
        shape = small_w[n].shape
        size = math.prod(shape)
        out_g[n] = g_small[n].reshape(shape)
        out_d[n], out_m[n], out_v[n] = (t[k, :size].reshape(shape) for t in (d, nm, nv))

    order = ("norm_ffn1", "ffn1_gate", "ffn1_up", "ffn1_down", "norm_mix", "w_in", "conv_w", "a_log", "dt_bias", "dn_norm",
             "w_out", "norm_ffn2", "ffn2_gate", "ffn2_up", "ffn2_down", "norm_final")
    return (loss_out, grad_x[None], *[out_g[n] for n in order], *[out_d[n] for n in order],
            *[out_m[n] for n in order], *[out_v[n] for n in order])
```

```python
import functools
import math

import jax
import jax.numpy as jnp
from jax import lax
from jax.experimental import pallas as pl
from jax.experimental.pallas import tpu as pltpu

F32 = jnp.float32
BF16 = jnp.bfloat16
HI = lax.Precision.HIGH

D_MODEL = 1024
D_FF = 2816
ATTN_HEADS = 8
ATTN_WIDTH = 512
ATTN_BLOCK = 128
DILATIONS = (1, 4, 16)
DN_HEADS = 4
DN_HEAD_DIM = 128
DN_WIDTH = 512
DN_CHUNK = 64
CONV_WIDTH = 4
NORM_EPS = 1e-6
L2_EPS = 1e-6
IN_COLS = 3592
IN_COLS_PADDED = 3712
N_CHIPS = 4

ADAM_LR = 0.001
ADAM_B1 = 0.9
ADAM_B2 = 0.999
ADAM_EPS = 1e-08
ADAM_WD = 0.01
ADAM_STEP = 10

VMEM_LIMIT = 56 * 1024 * 1024
NEG_BIG = -1e30
MESH = pl.DeviceIdType.MESH


def _params(n_grid, vmem=VMEM_LIMIT):
    return pltpu.CompilerParams(dimension_semantics=("arbitrary",) * n_grid, vmem_limit_bytes=vmem)


def _call(body, args, *, name, grid, in_specs, out_specs, out_shape, scratch_shapes=(), comm=None):
    n_in, n_out, n_scr = len(in_specs), len(out_specs), len(scratch_shapes)
    hbm = pl.BlockSpec(memory_space=pl.ANY)
    srcs, dst_shapes, n_sems, start, finish = comm if comm is not None else ((), (), 0, None, None)
    ns, nd = len(srcs), len(dst_shapes)

    def full(*refs):
        ins, c_src = refs[:n_in], refs[n_in:n_in + ns]
        at = n_in + ns
        outs, c_dst = refs[at:at + n_out], refs[at + n_out:at + n_out + nd]
        scr = refs[at + n_out + nd:at + n_out + nd + n_scr]
        if comm is not None:
            ids = [pl.program_id(a) for a in range(len(grid))]
            first = functools.reduce(jnp.logical_and, [i == 0 for i in ids])
            last = functools.reduce(jnp.logical_and, [i == g - 1 for i, g in zip(ids, grid)])

            @pl.when(first)
            def _():
                start(c_src, c_dst, refs[-2], refs[-1])

        body(*ins, *outs, *scr)
        if comm is not None:
            @pl.when(last)
            def _():
                finish(c_src, c_dst, refs[-2], refs[-1])

    sems = [pltpu.SemaphoreType.DMA((n_sems,)), pltpu.SemaphoreType.DMA((n_sems,))] if comm is not None else []
    res = pl.pallas_call(
        full, name=name, grid=grid, in_specs=list(in_specs) + [hbm] * ns, out_specs=list(out_specs) + [hbm] * nd,
        out_shape=list(out_shape) + list(dst_shapes), scratch_shapes=list(scratch_shapes) + sems,
        compiler_params=_params(len(grid)),
    )(*args, *srcs)
    return res[:n_out], res[n_out:]


def _nt(a, b, precision=None):
    return lax.dot_general(a, b, (((1,), (1,)), ((), ())), preferred_element_type=F32, precision=precision)


def _tn(a, b, precision=None):
    return lax.dot_general(a, b, (((0,), (0,)), ((), ())), preferred_element_type=F32, precision=precision)


def _nn(a, b, precision=None):
    return jnp.dot(a, b, preferred_element_type=F32, precision=precision)


def _sigmoid(x):
    return 1.0 / (1.0 + jnp.exp(-x))


def _ffn_fwd(x, gain, wg, wu, wd, name, comm=None):
    S, D = x.shape
    nf, _, tf = wg.shape
    tm = 512

    def body(x_ref, gain_ref, wg_ref, wu_ref, wd_ref, xo_ref, h_ref, g_ref, u_ref, acc_ref, hs_ref):
        j = pl.program_id(1)

        @pl.when(j == 0)
        def _():
            xf = x_ref[...]
            r = lax.rsqrt(jnp.mean(xf * xf, axis=-1, keepdims=True) + NORM_EPS)
            h = (xf * r * gain_ref[...]).astype(BF16)
            hs_ref[...] = h
            h_ref[...] = h
            acc_ref[...] = jnp.zeros_like(acc_ref)

        h = hs_ref[...]
        g = _nn(h, wg_ref[...])
        u = _nn(h, wu_ref[...])
        g_ref[...] = g.astype(BF16)
        u_ref[...] = u.astype(BF16)
        act = g * _sigmoid(g) * u
        acc_ref[...] += _nn(act.astype(BF16), wd_ref[...])

        @pl.when(j == nf - 1)
        def _():
            xo_ref[...] = x_ref[...] + 0.5 * acc_ref[...]

    return _call(
        body, (x, gain, wg, wu, wd), name=name, grid=(S // tm, nf), comm=comm,
        in_specs=[pl.BlockSpec((tm, D), lambda i, j: (i, 0)),
                  pl.BlockSpec((1, D), lambda i, j: (0, 0)),
                  pl.BlockSpec((None, D, tf), lambda i, j: (j, 0, 0)),
                  pl.BlockSpec((None, D, tf), lambda i, j: (j, 0, 0)),
                  pl.BlockSpec((None, tf, D), lambda i, j: (j, 0, 0))],
        out_specs=[pl.BlockSpec((tm, D), lambda i, j: (i, 0)),
                   pl.BlockSpec((tm, D), lambda i, j: (i, 0)),
                   pl.BlockSpec((None, tm, tf), lambda i, j: (j, i, 0)),
                   pl.BlockSpec((None, tm, tf), lambda i, j: (j, i, 0))],
        out_shape=[jax.ShapeDtypeStruct((S, D), F32), jax.ShapeDtypeStruct((S, D), BF16),
                   jax.ShapeDtypeStruct((nf, S, tf), BF16), jax.ShapeDtypeStruct((nf, S, tf), BF16)],
        scratch_shapes=[pltpu.VMEM((tm, D), F32), pltpu.VMEM((tm, D), BF16)])


def _rmsnorm_bwd(dh, xf, gain):
    r = lax.rsqrt(jnp.mean(xf * xf, axis=-1, keepdims=True) + NORM_EPS)
    xhat = xf * r
    dgain = jnp.sum(dh * xhat, axis=0, keepdims=True)
    dxh = dh * gain
    dx = r * (dxh - xhat * jnp.mean(dxh * xhat, axis=-1, keepdims=True))
    return dx, dgain


def _ffn_bwd(dxo, x, gain, g, u, wd, wg, wu, name, comm=None):
    S, D = x.shape
    nf, _, tf = g.shape
    tm = 512

    def body(dxo_ref, x_ref, gain_ref, g_ref, u_ref, wd_ref, wg_ref, wu_ref,
             dx_ref, dgain_ref, dg_ref, du_ref, act_ref, dout_ref, acc_ref, ds_ref):
        i = pl.program_id(0)
        j = pl.program_id(1)

        @pl.when(j == 0)
        def _():
            d = (0.5 * dxo_ref[...]).astype(BF16)
            ds_ref[...] = d
            dout_ref[...] = d
            acc_ref[...] = jnp.zeros_like(acc_ref)

        @pl.when((i == 0) & (j == 0))
        def _():
            dgain_ref[...] = jnp.zeros_like(dgain_ref)

        dact = _nt(ds_ref[...], wd_ref[...])
        gv = g_ref[...].astype(F32)
        uv = u_ref[...].astype(F32)
        sg = _sigmoid(gv)
        silu = gv * sg
        act_ref[...] = (silu * uv).astype(BF16)
        dgv = (dact * uv * (sg * (1.0 + gv * (1.0 - sg)))).astype(BF16)
        duv = (dact * silu).astype(BF16)
        dg_ref[...] = dgv
        du_ref[...] = duv
        acc_ref[...] += _nt(dgv, wg_ref[...]) + _nt(duv, wu_ref[...])

        @pl.when(j == nf - 1)
        def _():
            dx, dgain = _rmsnorm_bwd(acc_ref[...], x_ref[...], gain_ref[...])
            dx_ref[...] = dxo_ref[...] + dx
            dgain_ref[...] += dgain

    return _call(
        body, (dxo, x, gain, g, u, wd, wg, wu), name=name, grid=(S // tm, nf), comm=comm,
        in_specs=[pl.BlockSpec((tm, D), lambda i, j: (i, 0)),
                  pl.BlockSpec((tm, D), lambda i, j: (i, 0)),
                  pl.BlockSpec((1, D), lambda i, j: (0, 0)),
                  pl.BlockSpec((None, tm, tf), lambda i, j: (j, i, 0)),
                  pl.BlockSpec((None, tm, tf), lambda i, j: (j, i, 0)),
                  pl.BlockSpec((None, tf, D), lambda i, j: (j, 0, 0)),
                  pl.BlockSpec((None, D, tf), lambda i, j: (j, 0, 0)),
                  pl.BlockSpec((None, D, tf), lambda i, j: (j, 0, 0))],
        out_specs=[pl.BlockSpec((tm, D), lambda i, j: (i, 0)),
                   pl.BlockSpec((1, D), lambda i, j: (0, 0)),
                   pl.BlockSpec((None, tm, tf), lambda i, j: (j, i, 0)),
                   pl.BlockSpec((None, tm, tf), lambda i, j: (j, i, 0)),
                   pl.BlockSpec((None, tm, tf), lambda i, j: (j, i, 0)),
                   pl.BlockSpec((tm, D), lambda i, j: (i, 0))],
        out_shape=[jax.ShapeDtypeStruct((S, D), F32), jax.ShapeDtypeStruct((1, D), F32),
                   jax.ShapeDtypeStruct((nf, S, tf), BF16), jax.ShapeDtypeStruct((nf, S, tf), BF16),
                   jax.ShapeDtypeStruct((nf, S, tf), BF16), jax.ShapeDtypeStruct((S, D), BF16)],
        scratch_shapes=[pltpu.VMEM((tm, D), F32), pltpu.VMEM((tm, D), BF16)])


def _matmul_tn(a, b, tm, tk, name):
    K, M = a.shape
    N = b.shape[1]

    def body(a_ref, b_ref, o_ref):
        @pl.when(pl.program_id(1) == 0)
        def _():
            o_ref[...] = jnp.zeros_like(o_ref)

        o_ref[...] += _tn(a_ref[...], b_ref[...])

    return pl.pallas_call(
        body, name=name, grid=(M // tm, K // tk),
        in_specs=[pl.BlockSpec((tk, tm), lambda i, k: (k, i)),
                  pl.BlockSpec((tk, N), lambda i, k: (k, 0))],
        out_specs=pl.BlockSpec((tm, N), lambda i, k: (i, 0)),
        out_shape=jax.ShapeDtypeStruct((M, N), F32),
        compiler_params=_params(2),
    )(a, b)


def _dw_chunks(a, b, tk, name, comm=None):
    nf, S, tf = a.shape
    N = b.shape[1]

    def body(a_ref, b_ref, o_ref):
        @pl.when(pl.program_id(1) == 0)
        def _():
            o_ref[...] = jnp.zeros_like(o_ref)

        o_ref[...] += _tn(a_ref[...], b_ref[...])

    (out,), landed = _call(
        body, (a, b), name=name, grid=(nf, S // tk), comm=comm,
        in_specs=[pl.BlockSpec((None, tk, tf), lambda j, k: (j, k, 0)),
                  pl.BlockSpec((tk, N), lambda j, k: (k, 0))],
        out_specs=[pl.BlockSpec((None, tf, N), lambda j, k: (j, 0, 0))],
        out_shape=[jax.ShapeDtypeStruct((nf, tf, N), F32)])
    return out, landed


VIEW_TILE = 512


def _view_spec(d, tile=VIEW_TILE):
    return pl.BlockSpec((tile // d, d * ATTN_WIDTH), lambda i: (i, 0))


def _view_shape(S, d, dtype):
    return jax.ShapeDtypeStruct((S // d, d * ATTN_WIDTH), dtype)


def _tile_to_views(val, planes, out_refs):
    for g in range(4):
        planes[g] = val[:, g * 128:(g + 1) * 128]
    for d, ref in zip(DILATIONS, out_refs):
        if d == 1:
            ref[...] = val.astype(ref.dtype)
            continue
        for r in range(d):
            for g in range(4):
                ref[:, r * ATTN_WIDTH + g * 128:r * ATTN_WIDTH + (g + 1) * 128] = (
                    planes[g, pl.ds(r, planes.shape[1] // d, stride=d), :].astype(ref.dtype))


def _view_to_tile(ref, d, planes):
    if d == 1:
        return ref[...]
    for r in range(d):
        for g in range(4):
            planes[g, pl.ds(r, planes.shape[1] // d, stride=d), :] = ref[:, r * ATTN_WIDTH + g * 128:r * ATTN_WIDTH + (g + 1) * 128]
    return jnp.concatenate([planes[g] for g in range(4)], axis=1)


def _inproj_fwd(x, gain, w_in_p):
    S, D = x.shape
    tm = VIEW_TILE
    W = ATTN_WIDTH

    def body(x_ref, gain_ref, w_ref, h_ref, q1, q4, q16, k1, k4, k16, v1, v4, v16, dq_ref, dk_ref, dv_ref, gate_ref, bd_ref,
             planes):
        xf = x_ref[...]
        r = lax.rsqrt(jnp.mean(xf * xf, axis=-1, keepdims=True) + NORM_EPS)
        h = (xf * r * gain_ref[...]).astype(BF16)
        h_ref[...] = h
        _tile_to_views(_nn(h, w_ref[:, 0:W]) * 0.125, planes, (q1, q4, q16))
        _tile_to_views(_nn(h, w_ref[:, W:2 * W]), planes, (k1, k4, k16))
        _tile_to_views(_nn(h, w_ref[:, 2 * W:3 * W]), planes, (v1, v4, v16))
        dq_ref[...] = _nn(h, w_ref[:, 3 * W:4 * W])
        dk_ref[...] = _nn(h, w_ref[:, 4 * W:5 * W])
        dv_ref[...] = _nn(h, w_ref[:, 5 * W:6 * W])
        gate_ref[...] = _nn(h, w_ref[:, 6 * W:7 * W])
        bd_ref[...] = _nn(h, w_ref[:, 7 * W:7 * W + 128])

    tok = lambda w: pl.BlockSpec((tm, w), lambda i: (i, 0))
    return pl.pallas_call(
        body, name="inproj_fwd", grid=(S // tm,),
        in_specs=[tok(D), pl.BlockSpec((1, D), lambda i: (0, 0)),
                  pl.BlockSpec((D, IN_COLS_PADDED), lambda i: (0, 0))],
        out_specs=[tok(D)] + [_view_spec(d) for d in DILATIONS] * 3 + [tok(W)] * 4 + [tok(128)],
        out_shape=[jax.ShapeDtypeStruct((S, D), BF16)] + [_view_shape(S, d, BF16) for d in DILATIONS] * 3
                  + [jax.ShapeDtypeStruct((S, W), F32)] * 4 + [jax.ShapeDtypeStruct((S, 128), F32)],
        scratch_shapes=[pltpu.VMEM((4, tm, 128), F32)],
        compiler_params=_params(1),
    )(x, gain, w_in_p)


def _inproj_bwd(dxo, x, gain, attn_grads, dsecs, dbd, w_in_p):
    S, D = x.shape
    tm = VIEW_TILE // 2
    W = ATTN_WIDTH

    def body(dxo_ref, x_ref, gain_ref, *rest):
        views, (s3, s4, s5, s6, dbd_ref, w_ref, dx_ref, dgain_ref, dproj_ref, planes) = rest[:9], rest[9:]

        @pl.when(pl.program_id(0) == 0)
        def _():
            dgain_ref[...] = jnp.zeros_like(dgain_ref)

        secs = []
        for k in range(3):
            parts = [_view_to_tile(views[3 * k + p], d, planes) for p, d in enumerate(DILATIONS)]
            secs.append(parts[0] + parts[1] + parts[2])
        secs += [s3[...], s4[...], s5[...], s6[...]]
        dh = jnp.zeros((tm, D), F32)
        for k, s in enumerate(secs):
            d = s.astype(BF16)
            dproj_ref[:, k * W:(k + 1) * W] = d
            dh += _nt(d, w_ref[:, k * W:(k + 1) * W])
        d = dbd_ref[...].astype(BF16)
        dproj_ref[:, 7 * W:7 * W + 128] = d
        dh += _nt(d, w_ref[:, 7 * W:7 * W + 128])
        dx, dgain = _rmsnorm_bwd(dh, x_ref[...], gain_ref[...])
        dx_ref[...] = dxo_ref[...] + dx
        dgain_ref[...] += dgain

    tok = lambda w: pl.BlockSpec((tm, w), lambda i: (i, 0))
    return pl.pallas_call(
        body, name="inproj_bwd", grid=(S // tm,),
        in_specs=[tok(D), tok(D), pl.BlockSpec((1, D), lambda i: (0, 0))] + [_view_spec(d, tm) for d in DILATIONS] * 3
                 + [tok(W)] * 4 + [tok(128)] + [pl.BlockSpec((D, IN_COLS_PADDED), lambda i: (0, 0))],
        out_specs=[tok(D), pl.BlockSpec((1, D), lambda i: (0, 0)), tok(IN_COLS_PADDED)],
        out_shape=[jax.ShapeDtypeStruct((S, D), F32), jax.ShapeDtypeStruct((1, D), F32),
                   jax.ShapeDtypeStruct((S, IN_COLS_PADDED), BF16)],
        scratch_shapes=[pltpu.VMEM((4, tm, 128), F32)],
        compiler_params=_params(1),
    )(dxo, x, gain, *[g for grads in attn_grads for g in grads], *dsecs, dbd, w_in_p)


def _slope(h):
    return 2.0 ** (-8.0 * (h + 1) / ATTN_HEADS)


def _head_bias(steps, d, heads=tuple(range(ATTN_HEADS))):
    stepsf = steps.astype(F32)
    return jnp.stack([stepsf * (-_slope(h) * d) for h in heads])


def _hnt(a, b):
    return lax.dot_general(a, b, (((2,), (2,)), ((0,), (0,))), preferred_element_type=F32)


def _hnn(a, b):
    return lax.dot_general(a, b, (((2,), (1,)), ((0,), (0,))), preferred_element_type=F32)


def _head_cols(tile, lo, big):
    return [_head_col(tile, lo, big), _head_col(tile, jnp.logical_not(lo), big)]


def _attn_fwd(q, k, v, d, name):
    L = q.shape[0]
    nb = L // ATTN_BLOCK
    B = ATTN_BLOCK

    def body(q_ref, kp_ref, kc_ref, vp_ref, vc_ref, acc_ref, m_ref, l_ref):
        n = pl.program_id(1)
        qi = lax.broadcasted_iota(jnp.int32, (B, 2 * B), 0)
        kj = lax.broadcasted_iota(jnp.int32, (B, 2 * B), 1)
        steps = qi + B - kj
        valid = (steps >= 0) & (steps <= B) & ((kj >= B) | (n > 0))
        lo = lax.broadcasted_iota(jnp.int32, (B, 128), 1) < 64
        qs, ks, vs = [], [], []
        for G in range(4):
            sl = slice(G * 128, (G + 1) * 128)
            qg = q_ref[:, sl]
            kg = jnp.concatenate([kp_ref[:, sl], kc_ref[:, sl]], axis=0)
            vg = jnp.concatenate([vp_ref[:, sl], vc_ref[:, sl]], axis=0)
            qs += [jnp.where(lo, qg, jnp.zeros_like(qg)), jnp.where(lo, jnp.zeros_like(qg), qg)]
            ks += [kg, kg]
            vs += [vg, vg]
        s = _hnt(jnp.stack(qs), jnp.stack(ks)) + _head_bias(steps, d)
        s = jnp.where(valid, s, NEG_BIG)
        m = jnp.max(s, axis=-1, keepdims=True)
        p = jnp.exp(s - m)
        l = jnp.sum(p, axis=-1, keepdims=True)
        a = _hnn(p.astype(BF16), jnp.stack(vs))
        for G in range(4):
            sl = slice(G * 128, (G + 1) * 128)
            acc_ref[:, sl] = jnp.where(lo, a[2 * G], a[2 * G + 1])
            m_ref[:, sl] = jnp.where(lo, m[2 * G], m[2 * G + 1])
            l_ref[:, sl] = jnp.where(lo, l[2 * G], l[2 * G + 1])

    cur = pl.BlockSpec((B, ATTN_WIDTH), lambda r, n: (n, r))
    prev = pl.BlockSpec((B, ATTN_WIDTH), lambda r, n: (jnp.maximum(n - 1, 0), r))
    return pl.pallas_call(
        body, name=name, grid=(d, nb),
        in_specs=[cur, prev, cur, prev, cur],
        out_specs=[cur, cur, cur],
        out_shape=[jax.ShapeDtypeStruct((L, d * ATTN_WIDTH), F32)] * 3,
        compiler_params=_params(2),
    )(q, k, k, v, v)


def _attn_merge(parts):
    S = parts[0][0].shape[0]
    tm = VIEW_TILE

    def body(a1, m1, l1, a2, m2, l2, a3, m3, l3, o_ref, lse1, lse4, lse16, planes):
        ins = ((a1, m1, l1), (a2, m2, l2), (a3, m3, l3))
        acc, ms, ls = [], [], []
        for d, (a, m, l) in zip(DILATIONS, ins):
            acc.append(_view_to_tile(a, d, planes))
            ms.append(_view_to_tile(m, d, planes))
            ls.append(_view_to_tile(l, d, planes))
        mx = jnp.maximum(jnp.maximum(ms[0], ms[1]), ms[2])
        es = [jnp.exp(m - mx) for m in ms]
        den = es[0] * ls[0] + es[1] * ls[1] + es[2] * ls[2]
        num = es[0] * acc[0] + es[1] * acc[1] + es[2] * acc[2]
        o_ref[...] = num / den
        _tile_to_views(mx + jnp.log(den), planes, (lse1, lse4, lse16))

    views = [_view_spec(d) for d in DILATIONS]
    flat = [t for p in parts for t in p]
    return pl.pallas_call(
        body, name="attn_merge", grid=(S // tm,),
        in_specs=[views[p] for p in range(3) for _ in range(3)],
        out_specs=[views[0]] + views,
        out_shape=[jax.ShapeDtypeStruct((S, ATTN_WIDTH), F32)] + [_view_shape(S, d, F32) for d in DILATIONS],
        scratch_shapes=[pltpu.VMEM((4, tm, 128), F32)],
        compiler_params=_params(1),
    )(*flat)


def _head_col(t, msk, big):
    if big:
        return jnp.max(jnp.where(msk, t, NEG_BIG), axis=-1, keepdims=True)
    return jnp.sum(jnp.where(msk, t, 0.0), axis=-1, keepdims=True) * (1.0 / 64.0)


def _attn_bwd_q(q, k, v, do, lse, dd, d, name):
    L = q.shape[0]
    nb = L // ATTN_BLOCK
    B = ATTN_BLOCK

    def body(q_ref, kp_ref, kc_ref, vp_ref, vc_ref, do_ref, lse_ref, dd_ref, dq_ref):
        n = pl.program_id(1)
        qi = lax.broadcasted_iota(jnp.int32, (B, 2 * B), 0)
        kj = lax.broadcasted_iota(jnp.int32, (B, 2 * B), 1)
        steps = qi + B - kj
        valid = (steps >= 0) & (steps <= B) & ((kj >= B) | (n > 0))
        lo = lax.broadcasted_iota(jnp.int32, (B, 128), 1) < 64
        qs, ks, vs, dos, lses, dcols = [], [], [], [], [], []
        for G in range(4):
            sl = slice(G * 128, (G + 1) * 128)
            qg = q_ref[:, sl]
            kg = jnp.concatenate([kp_ref[:, sl], kc_ref[:, sl]], axis=0)
            vg = jnp.concatenate([vp_ref[:, sl], vc_ref[:, sl]], axis=0)
            dog = do_ref[:, sl]
            qs += [jnp.where(lo, qg, jnp.zeros_like(qg)), jnp.where(lo, jnp.zeros_like(qg), qg)]
            dos += [jnp.where(lo, dog, 0.0).astype(BF16), jnp.where(lo, 0.0, dog).astype(BF16)]
            ks += [kg, kg]
            vs += [vg, vg]
            lses += _head_cols(lse_ref[:, sl], lo, True)
            dcols += _head_cols(dd_ref[:, sl], lo, False)
        kb = jnp.stack(ks)
        s = _hnt(jnp.stack(qs), kb) + _head_bias(steps, d)
        p = jnp.where(valid, jnp.exp(jnp.where(valid, s, NEG_BIG) - jnp.stack(lses)), 0.0)
        dp = _hnt(jnp.stack(dos), jnp.stack(vs))
        ds = p * (dp - jnp.stack(dcols))
        dq = _hnn(ds.astype(BF16), kb) * 0.125
        for G in range(4):
            dq_ref[:, G * 128:(G + 1) * 128] = jnp.where(lo, dq[2 * G], dq[2 * G + 1])

    cur = pl.BlockSpec((B, ATTN_WIDTH), lambda r, n: (n, r))
    prev = pl.BlockSpec((B, ATTN_WIDTH), lambda r, n: (jnp.maximum(n - 1, 0), r))
    return pl.pallas_call(
        body, name=name, grid=(d, nb), in_specs=[cur, prev, cur, prev, cur, cur, cur, cur], out_specs=cur,
        out_shape=jax.ShapeDtypeStruct((L, d * ATTN_WIDTH), F32), compiler_params=_params(2),
    )(q, k, k, v, v, do, lse, dd)


def _attn_bwd_kv(q, k, v, do, lse, dd, d, name):
    L = q.shape[0]
    nb = L // ATTN_BLOCK
    B = ATTN_BLOCK

    def body(k_ref, v_ref, qc_ref, qn_ref, doc_ref, don_ref, lsec_ref, lsen_ref, ddc_ref, ddn_ref, dk_ref, dv_ref):
        j = pl.program_id(1)
        qrow = lax.broadcasted_iota(jnp.int32, (2 * B, B), 0)
        kk = lax.broadcasted_iota(jnp.int32, (2 * B, B), 1)
        steps = qrow - kk
        valid = (steps >= 0) & (steps <= B) & ((qrow < B) | (j < nb - 1))
        lo2 = lax.broadcasted_iota(jnp.int32, (2 * B, 128), 1) < 64
        lo = lax.broadcasted_iota(jnp.int32, (B, 128), 1) < 64
        stepsf = steps.astype(F32)
        for G in range(4):
            sl = slice(G * 128, (G + 1) * 128)
            kg = k_ref[:, sl]
            vg = v_ref[:, sl]
            qq = jnp.concatenate([qc_ref[:, sl], qn_ref[:, sl]], axis=0)
            doo = jnp.concatenate([doc_ref[:, sl], don_ref[:, sl]], axis=0)
            lse2 = jnp.concatenate([lsec_ref[:, sl], lsen_ref[:, sl]], axis=0)
            dd2 = jnp.concatenate([ddc_ref[:, sl], ddn_ref[:, sl]], axis=0)
            doo_b = doo.astype(BF16)
            dks, dvs = [], []
            for half in (0, 1):
                msk = lo2 if half == 0 else jnp.logical_not(lo2)
                qm = jnp.where(msk, qq, jnp.zeros_like(qq))
                s = _nt(qm, kg) - (_slope(2 * G + half) * d) * stepsf
                lse_c = _head_col(lse2, msk, True)
                p = jnp.where(valid, jnp.exp(jnp.where(valid, s, NEG_BIG) - lse_c), 0.0)
                dvs.append(_tn(p.astype(BF16), doo_b))
                dom = jnp.where(msk, doo, 0.0).astype(BF16)
                dp = _nt(dom, vg)
                dcol = _head_col(dd2, msk, False)
                ds = p * (dp - dcol)
                dks.append(_tn(ds.astype(BF16), qq))
            dk_ref[:, sl] = jnp.where(lo, dks[0], dks[1])
            dv_ref[:, sl] = jnp.where(lo, dvs[0], dvs[1])

    cur = pl.BlockSpec((B, ATTN_WIDTH), lambda r, j: (j, r))
    nxt = pl.BlockSpec((B, ATTN_WIDTH), lambda r, j: (jnp.minimum(j + 1, nb - 1), r))
    return pl.pallas_call(
        body, name=name, grid=(d, nb), in_specs=[cur, cur, cur, nxt, cur, nxt, cur, nxt, cur, nxt], out_specs=[cur, cur],
        out_shape=[jax.ShapeDtypeStruct((L, d * ATTN_WIDTH), F32)] * 2, compiler_params=_params(2),
    )(k, v, q, q, do, do, lse, lse, dd, dd)


CONV_T = 512
HALO = 8


def _conv_taps(pad_ref, w, T):
    acc = pad_ref[pl.ds(HALO - 3, T), :] * w[0:1, :]
    for j in range(1, CONV_WIDTH):
        acc = acc + pad_ref[pl.ds(HALO - 3 + j, T), :] * w[j:j + 1, :]
    return acc


def _conv_fwd(xq, xk, xv, conv_w):
    S = xq.shape[0]
    T = CONV_T

    def body(xq_ref, xqh_ref, xk_ref, xkh_ref, xv_ref, xvh_ref, wq_ref, wk_ref, wv_ref,
             qn_ref, kn_ref, v_ref, pad_ref):
        i = pl.program_id(0)

        def act(x_ref, xh_ref, w_ref):
            pad_ref[pl.ds(0, HALO), :] = jnp.where(i > 0, xh_ref[...], 0.0)
            pad_ref[pl.ds(HALO, T), :] = x_ref[...]
            c = _conv_taps(pad_ref, w_ref[...], T)
            return c * _sigmoid(c)

        def l2n(t):
            return t * lax.rsqrt(jnp.sum(t * t, axis=-1, keepdims=True) + L2_EPS)

        qn_ref[...] = l2n(act(xq_ref, xqh_ref, wq_ref))
        kn_ref[...] = l2n(act(xk_ref, xkh_ref, wk_ref))
        v_ref[...] = act(xv_ref, xvh_ref, wv_ref)

    tile = pl.BlockSpec((T, 128), lambda i, h: (i, h))
    halo = pl.BlockSpec((HALO, 128), lambda i, h: (jnp.maximum(i * (T // HALO) - 1, 0), h))
    wspec = lambda sec: pl.BlockSpec((CONV_WIDTH, 128), lambda i, h, sec=sec: (0, 4 * sec + h))
    return pl.pallas_call(
        body, name="dn_conv_fwd", grid=(S // T, DN_HEADS),
        in_specs=[tile, halo, tile, halo, tile, halo, wspec(0), wspec(1), wspec(2)],
        out_specs=[tile, tile, tile],
        out_shape=[jax.ShapeDtypeStruct((S, DN_WIDTH), F32)] * 3,
        scratch_shapes=[pltpu.VMEM((T + HALO, 128), F32)],
        compiler_params=_params(2),
    )(xq, xq, xk, xk, xv, xv, conv_w, conv_w, conv_w)


def _conv_bwd_pre(xq, xk, xv, conv_w, dqn, dkn, dv):
    S = xq.shape[0]
    T = CONV_T

    def body(xq_ref, xqh_ref, xk_ref, xkh_ref, xv_ref, xvh_ref, wq_ref, wk_ref, wv_ref,
             dqn_ref, dkn_ref, dv_ref, dcq_ref, dck_ref, dcv_ref, dwq_ref, dwk_ref, dwv_ref, pad_ref):
        i = pl.program_id(1)

        def one(x_ref, xh_ref, w_ref, dy_ref, dc_ref, dw_ref, normed):
            pad_ref[pl.ds(0, HALO), :] = jnp.where(i > 0, xh_ref[...], 0.0)
            pad_ref[pl.ds(HALO, T), :] = x_ref[...]
            c = _conv_taps(pad_ref, w_ref[...], T)
            sg = _sigmoid(c)
            a = c * sg
            dy = dy_ref[...]
            if normed:
                r = lax.rsqrt(jnp.sum(a * a, axis=-1, keepdims=True) + L2_EPS)
                y = a * r
                da = r * (dy - y * jnp.sum(dy * y, axis=-1, keepdims=True))
            else:
                da = dy
            dc = da * (sg * (1.0 + c * (1.0 - sg)))
            dc_ref[...] = dc

            @pl.when(i == 0)
            def _():
                dw_ref[...] = jnp.zeros_like(dw_ref)

            rows = [jnp.sum(dc * pad_ref[pl.ds(HALO - 3 + j, T), :], axis=0, keepdims=True) for j in range(CONV_WIDTH)]
            dw_ref[...] += jnp.concatenate(rows + [jnp.zeros((8 - CONV_WIDTH, 128), F32)], axis=0)

        one(xq_ref, xqh_ref, wq_ref, dqn_ref, dcq_ref, dwq_ref, True)
        one(xk_ref, xkh_ref, wk_ref, dkn_ref, dck_ref, dwk_ref, True)
        one(xv_ref, xvh_ref, wv_ref, dv_ref, dcv_ref, dwv_ref, False)

    tile = pl.BlockSpec((T, 128), lambda h, i: (i, h))
    halo = pl.BlockSpec((HALO, 128), lambda h, i: (jnp.maximum(i * (T // HALO) - 1, 0), h))
    wspec = lambda sec: pl.BlockSpec((CONV_WIDTH, 128), lambda h, i, sec=sec: (0, 4 * sec + h))
    dwspec = pl.BlockSpec((8, 128), lambda h, i: (0, h))
    return pl.pallas_call(
        body, name="dn_conv_bwd_pre", grid=(DN_HEADS, S // T),
        in_specs=[tile, halo, tile, halo, tile, halo, wspec(0), wspec(1), wspec(2), tile, tile, tile],
        out_specs=[tile, tile, tile, dwspec, dwspec, dwspec],
        out_shape=[jax.ShapeDtypeStruct((S, DN_WIDTH), F32)] * 3 + [jax.ShapeDtypeStruct((8, DN_WIDTH), F32)] * 3,
        scratch_shapes=[pltpu.VMEM((T + HALO, 128), F32)],
        compiler_params=_params(2),
    )(xq, xq, xk, xk, xv, xv, conv_w, conv_w, conv_w, dqn, dkn, dv)


def _conv_bwd_x(dcq, dck, dcv, conv_w):
    S = dcq.shape[0]
    T = CONV_T
    nt = S // T

    def body(dq_ref, dqh_ref, dk_ref, dkh_ref, dv_ref, dvh_ref, wq_ref, wk_ref, wv_ref,
             oq_ref, ok_ref, ov_ref, pad_ref):
        i = pl.program_id(0)

        def one(d_ref, dh_ref, w_ref, o_ref):
            pad_ref[pl.ds(0, T), :] = d_ref[...]
            pad_ref[pl.ds(T, HALO), :] = jnp.where(i < nt - 1, dh_ref[...], 0.0)
            w = w_ref[...]
            acc = pad_ref[pl.ds(3, T), :] * w[0:1, :]
            for j in range(1, CONV_WIDTH):
                acc = acc + pad_ref[pl.ds(3 - j, T), :] * w[j:j + 1, :]
            o_ref[...] = acc

        one(dq_ref, dqh_ref, wq_ref, oq_ref)
        one(dk_ref, dkh_ref, wk_ref, ok_ref)
        one(dv_ref, dvh_ref, wv_ref, ov_ref)

    tile = pl.BlockSpec((T, 128), lambda i, h: (i, h))
    halo = pl.BlockSpec((HALO, 128), lambda i, h: (jnp.minimum((i + 1) * (T // HALO), S // HALO - 1), h))
    wspec = lambda sec: pl.BlockSpec((CONV_WIDTH, 128), lambda i, h, sec=sec: (0, 4 * sec + h))
    return pl.pallas_call(
        body, name="dn_conv_bwd_x", grid=(nt, DN_HEADS),
        in_specs=[tile, halo, tile, halo, tile, halo, wspec(0), wspec(1), wspec(2)],
        out_specs=[tile, tile, tile],
        out_shape=[jax.ShapeDtypeStruct((S, DN_WIDTH), F32)] * 3,
        scratch_shapes=[pltpu.VMEM((T + HALO, 128), F32)],
        compiler_params=_params(2),
    )(dcq, dcq, dck, dck, dcv, dcv, conv_w, conv_w, conv_w)


PREP_CHUNKS = 2
SCAN_CHUNKS = 8


def _bnn(a, b):
    return lax.dot_general(a, b, (((2,), (1,)), ((0,), (0,))), preferred_element_type=F32, precision=HI)


def _bnt(a, b):
    return lax.dot_general(a, b, (((2,), (2,)), ((0,), (0,))), preferred_element_type=F32, precision=HI)


def _btn(a, b):
    return lax.dot_general(a, b, (((1,), (1,)), ((0,), (0,))), preferred_element_type=F32, precision=HI)


def _tri_inverse_b(a, blk, eye):
    dg = jnp.where(blk, a, 0.0)
    lo = a - dg
    d2 = _bnn(dg, dg)
    d4 = _bnn(d2, d2)
    d8 = _bnn(d4, d4)
    td = _bnn(_bnn(_bnn(eye - dg, eye + d2), eye + d4), eye + d8)
    b = _bnn(td, lo)
    b2 = _bnn(b, b)
    return _bnn(_bnn(eye - b, eye + b2), td)


def _dn_common_b(bds, avec, dvec, q_raw, k, v, t=None):
    C = DN_CHUNK
    lane = lax.broadcasted_iota(jnp.int32, (C, 128), 1)
    row = lax.broadcasted_iota(jnp.int32, (1, C, C), 1)
    col = lax.broadcasted_iota(jnp.int32, (1, C, C), 2)
    incl = row >= col
    strict = row > col
    eye = (row == col).astype(F32)
    blk = (row // 16) == (col // 16)
    pick = lambda tile, ln: jnp.sum(jnp.where(lane == ln, tile, 0.0), axis=-1, keepdims=True)
    betas, graws, zcs = [], [], []
    for bd in bds:
        z = bd + dvec
        g_all = -jnp.exp(avec) * (jnp.maximum(z, 0.0) + jnp.log(1.0 + jnp.exp(-jnp.abs(z))))
        beta_all = _sigmoid(bd)
        for h in range(DN_HEADS):
            betas.append(pick(beta_all, h))
            graws.append(pick(g_all, DN_HEADS + h))
            zcs.append(pick(z, DN_HEADS + h))
    beta, graw, zc = jnp.stack(betas), jnp.stack(graws), jnp.stack(zcs)
    to_row = lambda c: jnp.sum(eye * c, axis=1, keepdims=True)
    gc = jnp.sum(jnp.where(incl, to_row(graw), 0.0), axis=-1, keepdims=True)
    decay = jnp.exp(jnp.where(incl, gc - to_row(gc), NEG_BIG))
    q = q_raw * (DN_HEAD_DIM ** -0.5)
    kb = k * beta
    kk = _bnt(kb, k)
    if t is None:
        t = _tri_inverse_b(jnp.where(strict, kk * decay, 0.0), blk, eye)
    eg = jnp.exp(gc)
    rhs_w = kb * eg
    u = _bnn(t, v * beta)
    w = _bnn(t, rhs_w)
    qk = _bnt(q, k)
    aq = jnp.where(incl, qk * decay, 0.0)
    last = lax.broadcasted_iota(jnp.int32, (1, C, 1), 1) == C - 1
    g_last = jnp.sum(jnp.where(last, gc, 0.0), axis=1, keepdims=True)
    ekd = jnp.exp(g_last - gc)
    return dict(beta=beta, graw=graw, zc=zc, gc=gc, decay=decay, q=q, kb=kb, kk=kk, t=t, eg=eg, rhs_w=rhs_w,
                u=u, w=w, qk=qk, aq=aq, g_last=g_last, ekd=ekd, kd=k * ekd, qg=q * eg,
                incl=incl, strict=strict, eye=eye, lane=lane, row=row, col=col, last=last)


def _stack_heads(ref, rows):
    return jnp.stack([ref[rows, h * DN_HEAD_DIM:(h + 1) * DN_HEAD_DIM] for h in range(DN_HEADS)])


def _stack_units(ref, nc):
    C = DN_CHUNK
    return jnp.concatenate([_stack_heads(ref, slice(ci * C, (ci + 1) * C)) for ci in range(nc)], axis=0)


def _store_units(ref, val, nc):
    C = DN_CHUNK
    for ci in range(nc):
        for h in range(DN_HEADS):
            ref[ci * C:(ci + 1) * C, h * DN_HEAD_DIM:(h + 1) * DN_HEAD_DIM] = val[ci * DN_HEADS + h]


def _dn_prep(qn, kn, v, bd, avec, dvec):
    S = qn.shape[0]
    C = DN_CHUNK
    N = S // C
    nc = PREP_CHUNKS

    def body(q_ref, k_ref, v_ref, bd_ref, a_ref, d_ref, u_ref, w_ref, qg_ref, kd_ref, aq_ref, t_ref, egl_ref):
        bds = [bd_ref[ci * C:(ci + 1) * C, :] for ci in range(nc)]
        c = _dn_common_b(bds, a_ref[...], d_ref[...], _stack_units(q_ref, nc), _stack_units(k_ref, nc), _stack_units(v_ref, nc))
        _store_units(u_ref, c["u"], nc)
        _store_units(w_ref, c["w"], nc)
        _store_units(qg_ref, c["qg"], nc)
        _store_units(kd_ref, c["kd"], nc)
        egl = jnp.broadcast_to(jnp.exp(c["g_last"]), (nc * DN_HEADS, 1, 128))
        for ci in range(nc):
            for h in range(DN_HEADS):
                aq_ref[h, ci * C:(ci + 1) * C, :] = c["aq"][ci * DN_HEADS + h]
                t_ref[h, ci * C:(ci + 1) * C, :] = c["t"][ci * DN_HEADS + h]
            egl_ref[ci * 8:(ci + 1) * 8, :] = jnp.concatenate(
                [egl[ci * DN_HEADS + h] for h in range(DN_HEADS)] + [jnp.zeros((8 - DN_HEADS, 128), F32)], axis=0)

    tok = lambda w: pl.BlockSpec((nc * C, w), lambda n: (n, 0))
    sq = pl.BlockSpec((DN_HEADS, nc * C, C), lambda n: (0, n, 0))
    vec = pl.BlockSpec((1, 128), lambda n: (0, 0))
    return pl.pallas_call(
        body, name="dn_prep", grid=(N // nc,),
        in_specs=[tok(DN_WIDTH)] * 3 + [tok(128), vec, vec],
        out_specs=[tok(DN_WIDTH)] * 4 + [sq, sq, pl.BlockSpec((nc * 8, 128), lambda n: (n, 0))],
        out_shape=[jax.ShapeDtypeStruct((S, DN_WIDTH), F32)] * 4 + [jax.ShapeDtypeStruct((DN_HEADS, S, C), F32)] * 2
                  + [jax.ShapeDtypeStruct((N * 8, 128), F32)],
        compiler_params=_params(1),
    )(qn, kn, v, bd, avec, dvec)


def _dn_scan_fwd(u, w, qg, kd, aq, egl, gate, dn_gain):
    S = u.shape[0]
    C = DN_CHUNK
    N = S // C
    HD = DN_HEAD_DIM
    nc = SCAN_CHUNKS

    def body(u_ref, w_ref, qg_ref, kd_ref, aq_ref, egl_ref, gate_ref, gain_ref, dn_ref, o_ref, vn_ref, st_ref, state_ref):
        @pl.when(pl.program_id(0) == 0)
        def _():
            state_ref[...] = jnp.zeros_like(state_ref)

        gain = gain_ref[...]
        for ci in range(nc):
            rows = slice(ci * C, (ci + 1) * C)
            st = state_ref[...]
            for h in range(DN_HEADS):
                st_ref[ci * DN_WIDTH + h * HD:ci * DN_WIDTH + (h + 1) * HD, :] = st[h]
            v_new = _stack_heads(u_ref, rows) - _bnn(_stack_heads(w_ref, rows), st)
            o = _bnn(_stack_heads(qg_ref, rows), st) + _bnn(aq_ref[:, rows, :], v_new)
            egl = jnp.stack([egl_ref[ci * 8 + h:ci * 8 + h + 1, :] for h in range(DN_HEADS)])
            state_ref[...] = st * egl + _btn(_stack_heads(kd_ref, rows), v_new)
            r = lax.rsqrt(jnp.mean(o * o, axis=-1, keepdims=True) + NORM_EPS)
            gt = _stack_heads(gate_ref, rows)
            dn = o * r * gain * (gt * _sigmoid(gt))
            for h in range(DN_HEADS):
                sl = slice(h * HD, (h + 1) * HD)
                vn_ref[rows, sl] = v_new[h]
                o_ref[rows, sl] = o[h]
                dn_ref[rows, sl] = dn[h]

    tok = lambda wd: pl.BlockSpec((nc * C, wd), lambda n: (n, 0))
    sq = pl.BlockSpec((DN_HEADS, nc * C, C), lambda n: (0, n, 0))
    vec = pl.BlockSpec((1, 128), lambda n: (0, 0))
    return pl.pallas_call(
        body, name="dn_scan_fwd", grid=(N // nc,),
        in_specs=[tok(DN_WIDTH)] * 4 + [sq, pl.BlockSpec((nc * 8, 128), lambda n: (n, 0)), tok(DN_WIDTH), vec],
        out_specs=[tok(DN_WIDTH)] * 3 + [pl.BlockSpec((nc * DN_WIDTH, HD), lambda n: (n, 0))],
        out_shape=[jax.ShapeDtypeStruct((S, DN_WIDTH), F32)] * 3 + [jax.ShapeDtypeStruct((N * DN_WIDTH, HD), F32)],
        scratch_shapes=[pltpu.VMEM((DN_HEADS, HD, HD), F32)],
        compiler_params=_params(1),
    )(u, w, qg, kd, aq, egl, gate, dn_gain)


def _dn_scan_bwd(w, qg, kd, aq, egl, gate, dn_gain, o, ddn):
    S = w.shape[0]
    C = DN_CHUNK
    N = S // C
    HD = DN_HEAD_DIM
    nc = SCAN_CHUNKS

    def body(w_ref, qg_ref, kd_ref, aq_ref, egl_ref, gate_ref, gain_ref, o_ref, ddn_ref,
             do_ref, dvn_ref, dgate_ref, dst_ref, small_ref, dstate_ref):
        @pl.when(pl.program_id(0) == 0)
        def _():
            dstate_ref[...] = jnp.zeros_like(dstate_ref)
            small_ref[...] = jnp.zeros_like(small_ref)

        gain = gain_ref[...]
        d_gain = jnp.zeros((1, 128), F32)
        for ci in reversed(range(nc)):
            rows = slice(ci * C, (ci + 1) * C)
            dsn = dstate_ref[...]
            for h in range(DN_HEADS):
                dst_ref[ci * DN_WIDTH + h * HD:ci * DN_WIDTH + (h + 1) * HD, :] = dsn[h]
            ov = _stack_heads(o_ref, rows)
            r = lax.rsqrt(jnp.mean(ov * ov, axis=-1, keepdims=True) + NORM_EPS)
            on = ov * r
            gt = _stack_heads(gate_ref, rows)
            sgt = _sigmoid(gt)
            silu_g = gt * sgt
            dy = _stack_heads(ddn_ref, rows)
            d_gain = d_gain + jnp.sum(jnp.sum(dy * on * silu_g, axis=1, keepdims=True), axis=0)
            dgate = dy * on * gain * (sgt * (1.0 + gt * (1.0 - sgt)))
            don = dy * gain * silu_g
            do = r * (don - on * jnp.mean(don * on, axis=-1, keepdims=True))
            d_vnew = _btn(aq_ref[:, rows, :], do) + _bnn(_stack_heads(kd_ref, rows), dsn)
            egl = jnp.stack([egl_ref[ci * 8 + h:ci * 8 + h + 1, :] for h in range(DN_HEADS)])
            dstate_ref[...] = _btn(_stack_heads(qg_ref, rows), do) + dsn * egl - _btn(_stack_heads(w_ref, rows), d_vnew)
            for h in range(DN_HEADS):
                sl = slice(h * HD, (h + 1) * HD)
                do_ref[rows, sl] = do[h]
                dvn_ref[rows, sl] = d_vnew[h]
                dgate_ref[rows, sl] = dgate[h]
        small_ref[...] += jnp.concatenate([d_gain, jnp.zeros((7, 128), F32)], axis=0)

    nb = N // nc
    tok = lambda wd: pl.BlockSpec((nc * C, wd), lambda i: (nb - 1 - i, 0))
    sq = pl.BlockSpec((DN_HEADS, nc * C, C), lambda i: (0, nb - 1 - i, 0))
    vec = pl.BlockSpec((1, 128), lambda i: (0, 0))
    return pl.pallas_call(
        body, name="dn_scan_bwd", grid=(nb,),
        in_specs=[tok(DN_WIDTH)] * 3 + [sq, pl.BlockSpec((nc * 8, 128), lambda i: (nb - 1 - i, 0)), tok(DN_WIDTH), vec,
                                       tok(DN_WIDTH), tok(DN_WIDTH)],
        out_specs=[tok(DN_WIDTH)] * 3 + [pl.BlockSpec((nc * DN_WIDTH, HD), lambda i: (nb - 1 - i, 0)),
                                        pl.BlockSpec((8, 128), lambda i: (0, 0))],
        out_shape=[jax.ShapeDtypeStruct((S, DN_WIDTH), F32)] * 3 + [jax.ShapeDtypeStruct((N * DN_WIDTH, HD), F32),
                                                                  jax.ShapeDtypeStruct((8, 128), F32)],
        scratch_shapes=[pltpu.VMEM((DN_HEADS, HD, HD), F32)],
        compiler_params=_params(1),
    )(w, qg, kd, aq, egl, gate, dn_gain, o, ddn)


def _dn_post(qn, kn, v, bd, avec, dvec, t_inv, v_new_all, states, dstates, do_all, dvn_all, comm=None):
    S = qn.shape[0]
    C = DN_CHUNK
    N = S // C
    HD = DN_HEAD_DIM
    nc = PREP_CHUNKS
    B = nc * DN_HEADS

    def body(q_ref, k_ref, v_ref, bd_ref, a_ref, d_ref, t_ref, vn_ref, st_ref, dst_ref, do_ref, dvn_ref,
             dq_ref, dk_ref, dv_ref, dbd_ref, small_ref):
        @pl.when(pl.program_id(0) == 0)
        def _():
            small_ref[...] = jnp.zeros_like(small_ref)

        avec = a_ref[...]
        bds = [bd_ref[ci * C:(ci + 1) * C, :] for ci in range(nc)]
        k = _stack_units(k_ref, nc)
        vv = _stack_units(v_ref, nc)
        t = jnp.concatenate([t_ref[:, ci * C:(ci + 1) * C, :] for ci in range(nc)], axis=0)
        c = _dn_common_b(bds, avec, d_ref[...], _stack_units(q_ref, nc), k, vv, t=t)
        q, kb, eg, u, w = c["q"], c["kb"], c["eg"], c["u"], c["w"]
        beta, decay, incl, strict, eye = c["beta"], c["decay"], c["incl"], c["strict"], c["eye"]
        st = jnp.stack([st_ref[b * HD:(b + 1) * HD, :] for b in range(B)])
        dsn = jnp.stack([dst_ref[b * HD:(b + 1) * HD, :] for b in range(B)])
        v_new = _stack_units(vn_ref, nc)
        do = _stack_units(do_ref, nc)
        d_vnew = _stack_units(dvn_ref, nc)
        egl = jnp.exp(c["g_last"])
        daq = jnp.where(incl, _bnt(do, v_new), 0.0)
        d_qg = _bnt(do, st)
        d_kd = _bnt(v_new, dsn)
        d_glast = jnp.sum(jnp.sum(dsn * st, axis=-1, keepdims=True), axis=1, keepdims=True) * egl
        d_w = -_bnt(d_vnew, st)
        d_ru = _btn(t, d_vnew)
        d_rw = _btn(t, d_w)
        da = -jnp.where(strict, _bnt(d_ru, u) + _bnt(d_rw, w), 0.0)
        dv = d_ru * beta
        dbeta = jnp.sum(d_ru * vv, axis=-1, keepdims=True)
        dkb = d_rw * eg
        dgc = jnp.sum(d_rw * c["rhs_w"], axis=-1, keepdims=True)
        dkk = da * decay
        ddecay = da * c["kk"]
        dkb = dkb + _bnn(dkk, k)
        dk = _btn(dkk, kb)
        dqk = daq * decay
        ddecay = ddecay + daq * c["qk"]
        dq = _bnn(dqk, k)
        dk = dk + _btn(dqk, q)
        m = ddecay * decay
        col_sum = jnp.sum(m, axis=1, keepdims=True)
        dgc = dgc + jnp.sum(m, axis=-1, keepdims=True) - jnp.sum(eye * col_sum, axis=-1, keepdims=True)
        dq = dq + d_qg * eg
        dgc = dgc + jnp.sum(d_qg * c["qg"], axis=-1, keepdims=True)
        dk = dk + d_kd * c["ekd"]
        tk = jnp.sum(d_kd * c["kd"], axis=-1, keepdims=True)
        dgc = dgc - tk
        d_glast = d_glast + jnp.sum(tk, axis=1, keepdims=True)
        dk = dk + dkb * beta
        dbeta = dbeta + jnp.sum(dkb * k, axis=-1, keepdims=True)
        dgc = dgc + jnp.where(c["last"], d_glast, 0.0)
        dgc_row = jnp.sum(eye * dgc, axis=1, keepdims=True)
        dgraw = jnp.sum(jnp.where(c["col"] >= c["row"], dgc_row, 0.0), axis=-1, keepdims=True)
        _store_units(dq_ref, dq * (HD ** -0.5), nc)
        _store_units(dk_ref, dk, nc)
        _store_units(dv_ref, dv, nc)
        dbraw = dbeta * beta * (1.0 - beta)
        dzc = dgraw * _sigmoid(c["zc"])
        ga = dgraw * c["graw"]
        lane = c["lane"]
        lane1 = lax.broadcasted_iota(jnp.int32, (1, 128), 1)
        neg_ea = -jnp.exp(avec)
        d_alog = jnp.zeros((1, 128), F32)
        d_dt = jnp.zeros((1, 128), F32)
        for ci in range(nc):
            dbd = jnp.zeros((C, 128), F32)
            for h in range(DN_HEADS):
                b = ci * DN_HEADS + h
                dz = dzc[b] * neg_ea
                dbd = dbd + jnp.where(lane == h, dbraw[b], 0.0) + jnp.where(lane == DN_HEADS + h, dz, 0.0)
                d_alog = d_alog + jnp.where(lane1 == DN_HEADS + h, jnp.sum(ga[b], axis=0, keepdims=True), 0.0)
                d_dt = d_dt + jnp.where(lane1 == DN_HEADS + h, jnp.sum(dz, axis=0, keepdims=True), 0.0)
            dbd_ref[ci * C:(ci + 1) * C, :] = dbd
        small_ref[...] += jnp.concatenate([d_alog, d_dt, jnp.zeros((6, 128), F32)], axis=0)

    tok = lambda wd: pl.BlockSpec((nc * C, wd), lambda n: (n, 0))
    big = pl.BlockSpec((nc * DN_WIDTH, HD), lambda n: (n, 0))
    sq = pl.BlockSpec((DN_HEADS, nc * C, C), lambda n: (0, n, 0))
    vec = pl.BlockSpec((1, 128), lambda n: (0, 0))
    return _call(
        body, (qn, kn, v, bd, avec, dvec, t_inv, v_new_all, states, dstates, do_all, dvn_all),
        name="dn_post", grid=(N // nc,), comm=comm,
        in_specs=[tok(DN_WIDTH)] * 3 + [tok(128), vec, vec, sq, tok(DN_WIDTH), big, big, tok(DN_WIDTH), tok(DN_WIDTH)],
        out_specs=[tok(DN_WIDTH)] * 3 + [tok(128), pl.BlockSpec((8, 128), lambda n: (0, 0))],
        out_shape=[jax.ShapeDtypeStruct((S, DN_WIDTH), F32)] * 3 + [jax.ShapeDtypeStruct((S, 128), F32),
                                                                  jax.ShapeDtypeStruct((8, 128), F32)])


def _outproj_fwd(x, attn, dn, w_out):
    S, D = x.shape
    tm = 512

    def body(x_ref, a_ref, d_ref, w_ref, xo_ref, mix_ref):
        a = a_ref[...].astype(BF16)
        dd = d_ref[...].astype(BF16)
        mix_ref[:, 0:ATTN_WIDTH] = a
        mix_ref[:, ATTN_WIDTH:] = dd
        xo_ref[...] = x_ref[...] + _nn(a, w_ref[0:ATTN_WIDTH, :]) + _nn(dd, w_ref[ATTN_WIDTH:, :])

    tok = lambda w: pl.BlockSpec((tm, w), lambda i: (i, 0))
    return pl.pallas_call(
        body, name="outproj_fwd", grid=(S // tm,),
        in_specs=[tok(D), tok(ATTN_WIDTH), tok(DN_WIDTH), pl.BlockSpec((D, D), lambda i: (0, 0))],
        out_specs=[tok(D), tok(D)],
        out_shape=[jax.ShapeDtypeStruct((S, D), F32), jax.ShapeDtypeStruct((S, D), BF16)],
        compiler_params=_params(1),
    )(x, attn, dn, w_out)


def _outproj_bwd(dx, w_out, attn):
    S, D = dx.shape
    tm = VIEW_TILE

    def body(dx_ref, w_ref, attn_ref, da1, da4, da16, dl1, dl4, dl16, ddn_ref, dxb_ref, planes):
        d = dx_ref[...].astype(BF16)
        dxb_ref[...] = d
        da = _nt(d, w_ref[0:ATTN_WIDTH, :])
        ddn_ref[...] = _nt(d, w_ref[ATTN_WIDTH:, :])
        _tile_to_views(da, planes, (da1, da4, da16))
        lo = lax.broadcasted_iota(jnp.int32, (tm, 128), 1) < 64
        cols = []
        for G in range(4):
            sl = slice(G * 128, (G + 1) * 128)
            t = da[:, sl] * attn_ref[:, sl]
            d0 = jnp.sum(jnp.where(lo, t, 0.0), axis=-1, keepdims=True)
            d1 = jnp.sum(jnp.where(lo, 0.0, t), axis=-1, keepdims=True)
            cols.append(jnp.where(lo, d0, d1))
        _tile_to_views(jnp.concatenate(cols, axis=1), planes, (dl1, dl4, dl16))

    tok = lambda w: pl.BlockSpec((tm, w), lambda i: (i, 0))
    views = [_view_spec(d) for d in DILATIONS]
    return pl.pallas_call(
        body, name="outproj_bwd", grid=(S // tm,),
        in_specs=[tok(D), pl.BlockSpec((D, D), lambda i: (0, 0)), tok(ATTN_WIDTH)],
        out_specs=views + views + [tok(DN_WIDTH), tok(D)],
        out_shape=[_view_shape(S, d, F32) for d in DILATIONS] * 2
                  + [jax.ShapeDtypeStruct((S, DN_WIDTH), F32), jax.ShapeDtypeStruct((S, D), BF16)],
        scratch_shapes=[pltpu.VMEM((4, tm, 128), F32)],
        compiler_params=_params(1),
    )(dx, w_out, attn)


def _loss_head(x, gain, target):
    S, D = x.shape
    tm = 512

    def body(x_ref, gain_ref, t_ref, loss_ref, dx_ref, dgain_ref):
        @pl.when(pl.program_id(0) == 0)
        def _():
            loss_ref[...] = jnp.zeros_like(loss_ref)
            dgain_ref[...] = jnp.zeros_like(dgain_ref)

        xf = x_ref[...]
        gain = gain_ref[...]
        r = lax.rsqrt(jnp.mean(xf * xf, axis=-1, keepdims=True) + NORM_EPS)
        xhat = xf * r
        err = xhat * gain - t_ref[...]
        part = 0.5 * jnp.sum(jnp.mean(err * err, axis=-1, keepdims=True), axis=0, keepdims=True)
        first = (lax.broadcasted_iota(jnp.int32, (8, 128), 0) == 0) & (lax.broadcasted_iota(jnp.int32, (8, 128), 1) == 0)
        loss_ref[...] += jnp.where(first, part, 0.0)
        dy = err * (1.0 / D)
        dgain_ref[...] += jnp.sum(dy * xhat, axis=0, keepdims=True)
        dxh = dy * gain
        dx_ref[...] = r * (dxh - xhat * jnp.mean(dxh * xhat, axis=-1, keepdims=True))

    tok = pl.BlockSpec((tm, D), lambda i: (i, 0))
    row = pl.BlockSpec((1, D), lambda i: (0, 0))
    return pl.pallas_call(
        body, name="loss_head", grid=(S // tm,),
        in_specs=[tok, row, tok],
        out_specs=[pl.BlockSpec((8, 128), lambda i: (0, 0)), tok, row],
        out_shape=[jax.ShapeDtypeStruct((8, 128), F32), jax.ShapeDtypeStruct((S, D), F32),
                   jax.ShapeDtypeStruct((1, D), F32)],
        compiler_params=_params(1),
    )(x, gain, target)


def _adamw(w, g, m, v, name):
    R, Ccols = w.shape
    tr = R
    for cand in (256, 128, 64, 32, 16, 8):
        if R % cand == 0:
            tr = cand
            break
    c1 = 1.0 - ADAM_B1 ** ADAM_STEP
    c2 = 1.0 - ADAM_B2 ** ADAM_STEP

    def body(w_ref, g_ref, m_ref, v_ref, d_ref, nm_ref, nv_ref):
        gv = g_ref[...]
        mn = ADAM_B1 * m_ref[...] + (1.0 - ADAM_B1) * gv
        vn = ADAM_B2 * v_ref[...] + (1.0 - ADAM_B2) * (gv * gv)
        nm_ref[...] = mn
        nv_ref[...] = vn
        d_ref[...] = -ADAM_LR * ((mn / c1) / (jnp.sqrt(vn / c2) + ADAM_EPS) + ADAM_WD * w_ref[...])

    spec = pl.BlockSpec((tr, Ccols), lambda i: (i, 0))
    return pl.pallas_call(
        body, name=name, grid=(R // tr,), in_specs=[spec] * 4, out_specs=[spec] * 3,
        out_shape=[jax.ShapeDtypeStruct((R, Ccols), F32)] * 3, compiler_params=_params(1),
    )(w, g, m, v)


LATE_WEIGHTS = ("w_out", "ffn2_gate", "ffn2_up", "ffn2_down")


def _local_step(x, target, wts, small, dist=None):
    g1, g2, gm, gf = small["norm_ffn1"], small["norm_ffn2"], small["norm_mix"], small["norm_final"]
    wts = dict(wts)

    def reduce_start(gs, tag):
        return _rs_add_pairs(gs, _swap_sibling(gs, True, "rs_swap_halves_" + tag), dist["c"], "rs_add_pairs_" + tag)

    (x1, h1, fg1, fu1), late = _ffn_fwd(x, g1, wts["ffn1_gate"], wts["ffn1_up"], wts["ffn1_down"], "ffn1_fwd",
                                        comm=_ag_comm(dist["late"]) if dist else None)
    if dist:
        wts.update(zip(LATE_WEIGHTS, late))
        wts["w_out"] = wts["w_out"].reshape(D_MODEL, D_MODEL)
    h2, *qkv, xq, xk, xv, gate, bd = _inproj_fwd(x1, gm, wts["w_in"])
    aq, ak, av = qkv[0:3], qkv[3:6], qkv[6:9]
    parts = [_attn_fwd(aq[p], ak[p], av[p], d, f"attn_fwd_d{d}") for p, d in enumerate(DILATIONS)]
    attn, *lse = _attn_merge(parts)
    conv_w = small["conv_w"]
    qn, kn, vv = _conv_fwd(xq, xk, xv, conv_w)
    dn_u, dn_w, dn_qg, dn_kd, dn_aq, dn_t, dn_egl = _dn_prep(qn, kn, vv, bd, small["avec"], small["dvec"])
    dn, o_dn, v_new, states = _dn_scan_fwd(dn_u, dn_w, dn_qg, dn_kd, dn_aq, dn_egl, gate, small["dn_norm"])
    x2, mix = _outproj_fwd(x1, attn, dn, wts["w_out"])
    (x3, h3, fg2, fu2), _ = _ffn_fwd(x2, g2, wts["ffn2_gate"], wts["ffn2_up"], wts["ffn2_down"], "ffn2_fwd")
    loss, dx3, d_gf = _loss_head(x3, gf, target)

    grads = {}
    (dx2, d_g2, dfg2, dfu2, act2, dout2), _ = _ffn_bwd(dx3, x2, g2, fg2, fu2, wts["ffn2_down"], wts["ffn2_gate"],
                                                      wts["ffn2_up"], "ffn2_bwd")
    tk = 2048
    grads["ffn2_gate"], _ = _dw_chunks(dfg2, h3, tk, "dw_ffn2_gate")
    grads["ffn2_up"], _ = _dw_chunks(dfu2, h3, tk, "dw_ffn2_up")
    grads["ffn2_down"], _ = _dw_chunks(act2, dout2, tk, "dw_ffn2_down")
    group_a = ("ffn2_gate", "ffn2_up", "ffn2_down")
    parts_a = reduce_start([grads[n] for n in group_a], "a") if dist else None

    *dviews, ddn, dx2b = _outproj_bwd(dx2, wts["w_out"], attn)
    dattn, dd = dviews[0:3], dviews[3:6]
    grads["w_out"] = _matmul_tn(mix, dx2b, D_MODEL, tk, "dw_out").reshape(N_CHIPS, D_MODEL // N_CHIPS, D_MODEL)

    daq, dak, dav = [], [], []
    for p, d in enumerate(DILATIONS):
        daq.append(_attn_bwd_q(aq[p], ak[p], av[p], dattn[p], lse[p], dd[p], d, f"attn_bwd_q_d{d}"))
        dk_p, dv_p = _attn_bwd_kv(aq[p], ak[p], av[p], dattn[p], lse[p], dd[p], d, f"attn_bwd_kv_d{d}")
        dak.append(dk_p)
        dav.append(dv_p)

    do_dn, dvn, dgate, dstates, d_dn_gain = _dn_scan_bwd(dn_w, dn_qg, dn_kd, dn_aq, dn_egl, gate, small["dn_norm"], o_dn, ddn)
    (dqn, dkn, dvv, dbd, dn_small), recv_a = _dn_post(qn, kn, vv, bd, small["avec"], small["dvec"], dn_t, v_new, states,
                                                      dstates, do_dn, dvn, comm=_rsx_comm(parts_a) if dist else None)
    dcq, dck, dcv, dwq, dwk, dwv = _conv_bwd_pre(xq, xk, xv, conv_w, dqn, dkn, dvv)
    dxq, dxk, dxv = _conv_bwd_x(dcq, dck, dcv, conv_w)
    d_conv = jnp.concatenate([dwq[:CONV_WIDTH], dwk[:CONV_WIDTH], dwv[:CONV_WIDTH]], axis=1)

    dx1, d_gm, dproj = _inproj_bwd(dx2, x1, gm, [daq, dak, dav], [dxq, dxk, dxv, dgate], dbd, wts["w_in"])
    gi = _matmul_tn(dproj, h2, IN_COLS_PADDED, 512, "dw_in")
    if dist:
        gi = jnp.concatenate([gi[:3072], gi[3584:3592], gi[3072:3584]], axis=0).reshape(N_CHIPS, IN_COLS // N_CHIPS, D_MODEL)
        gi = jnp.pad(gi, ((0, 0), (0, W_IN_ROWS - IN_COLS // N_CHIPS), (0, 0)))
    grads["w_in"] = gi
    group_b = ("w_in", "w_out")
    parts_b = reduce_start([grads[n] for n in group_b], "b") if dist else None

    (dx0, d_g1, dfg1, dfu1, act1, dout1), recv_b = _ffn_bwd(dx1, x, g1, fg1, fu1, wts["ffn1_down"], wts["ffn1_gate"],
                                                           wts["ffn1_up"], "ffn1_bwd",
                                                           comm=_rsx_comm(parts_b) if dist else None)
    group_c = ("ffn1_gate", "ffn1_up", "ffn1_down")
    parts_c, recv_c = [], []
    for n, (lhs, rhs) in zip(group_c, ((dfg1, h1), (dfu1, h1), (act1, dout1))):
        grads[n], landed = _dw_chunks(lhs, rhs, tk, "dw_" + n, comm=_rsx_comm(parts_c[-1:]) if parts_c else None)
        recv_c += list(landed)
        if dist:
            parts_c += reduce_start([grads[n]], n)

    small_grads = dict(norm_ffn1=d_g1, norm_mix=d_gm, norm_ffn2=d_g2, norm_final=d_gf, conv_w=d_conv,
                       a_log=dn_small[0:1], dt_bias=dn_small[1:2], dn_norm=d_dn_gain[0:1])
    if dist:
        recv_c += _rs_exchange_arrays(parts_c[-1:])
        names = group_a + group_b + group_c
        totals = _rs_add_totals(list(parts_a) + list(parts_b) + list(parts_c), list(recv_a) + list(recv_b) + list(recv_c),
                                dist["chip"])
        theirs = _swap_sibling(totals, False, "rs_share_total")
        grads = {n: (mine, other) for n, mine, other in zip(names, totals, theirs)}
    return loss, dx0, grads, small_grads


PACK_SECTIONS = (("ffn1_gate", 704), ("ffn1_up", 704), ("ffn1_down", 704), ("w_in", 898), ("w_out", 256),
                 ("ffn2_gate", 704), ("ffn2_up", 704), ("ffn2_down", 704))
PACK_ROWS = 5408
HALF_ROWS = PACK_ROWS // 2
ADD_ROWS = 208

HBM = pl.BlockSpec(memory_space=pl.ANY)
VMEM_SPEC = pl.BlockSpec(memory_space=pltpu.VMEM)


def _coords():
    return lax.axis_index("x"), lax.axis_index("y"), lax.axis_index("c")


def _remote(src, dst, send_sems, recv_sems, k, dev):
    return pltpu.make_async_remote_copy(src_ref=src, dst_ref=dst, send_sem=send_sems.at[k], recv_sem=recv_sems.at[k],
                                        device_id=dev, device_id_type=MESH)


def _allreduce_small(buf, name):
    R, Cc = buf.shape

    def body(src_ref, out_ref, recv_ref, send_sems, recv_sems):
        x, y, c = _coords()
        copies = []
        for m in range(1, 8):
            fx, fy, fc = (m >> 2) & 1, (m >> 1) & 1, m & 1
            dev = (x ^ fx if fx else x, y ^ fy if fy else y, c ^ fc if fc else c)
            cp = _remote(src_ref, recv_ref.at[m - 1], send_sems, recv_sems, m - 1, dev)
            cp.start()
            copies.append(cp)
        for cp in copies:
            cp.wait()
        r = [src_ref[...]] + [recv_ref[m] for m in range(7)]
        out_ref[...] = ((r[0] + r[1]) + (r[2] + r[3])) + ((r[4] + r[5]) + (r[6] + r[7]))

    return pl.pallas_call(
        body, name=name, out_shape=jax.ShapeDtypeStruct((R, Cc), F32),
        in_specs=[VMEM_SPEC], out_specs=VMEM_SPEC,
        scratch_shapes=[pltpu.VMEM((7, R, Cc), F32), pltpu.SemaphoreType.DMA((7,)), pltpu.SemaphoreType.DMA((7,))],
    )(buf)


def _allgather_weights(pack2):
    _, Hh, Cc = pack2.shape

    def body(src_ref, out_ref, send_sems, recv_sems):
        x, y, c = _coords()
        sib = (x, y, 1 - c)
        others = [(1 - x, y), (x, 1 - y), (1 - x, 1 - y)]
        blk = lambda cx, cy, half: out_ref.at[2 * cx + cy, half]
        mine = _remote(src_ref, out_ref.at[2 * x + y], send_sems, recv_sems, 6, sib)
        mine.start()
        first = [_remote(src_ref.at[c], blk(x, y, c), send_sems, recv_sems, j, (ox, oy, c)) for j, (ox, oy) in enumerate(others)]
        for cp in first:
            cp.start()
        passed = [_remote(blk(ox, oy, c), blk(ox, oy, c), send_sems, recv_sems, 3 + j, sib) for j, (ox, oy) in enumerate(others)]
        for j, (ox, oy) in enumerate(others):
            _remote(src_ref.at[c], blk(ox, oy, c), send_sems, recv_sems, j, (ox, oy, c)).wait_recv()
            passed[j].start()
        for j, (ox, oy) in enumerate(others):
            _remote(src_ref.at[c], blk(ox, oy, 1 - c), send_sems, recv_sems, 3 + j, sib).wait_recv()
        for cp in first + passed:
            cp.wait_send()
        mine.wait()

    return pl.pallas_call(
        body, name="allgather_weights", out_shape=jax.ShapeDtypeStruct((N_CHIPS, 2, Hh, Cc), pack2.dtype),
        in_specs=[HBM], out_specs=HBM,
        scratch_shapes=[pltpu.SemaphoreType.DMA((7,)), pltpu.SemaphoreType.DMA((7,))],
    )(pack2)


def _rs_swap_halves(gpack):
    _, nj, Hh, Cc = gpack.shape

    def body(src_ref, out_ref, send_sems, recv_sems):
        x, y, c = _coords()
        cp = _remote(src_ref.at[1 - c], out_ref, send_sems, recv_sems, 0, (x, y, 1 - c))
        cp.start()
        cp.wait()

    return pl.pallas_call(
        body, name="rs_swap_halves", out_shape=jax.ShapeDtypeStruct((nj, Hh, Cc), gpack.dtype),
        in_specs=[HBM], out_specs=HBM,
        scratch_shapes=[pltpu.SemaphoreType.DMA((1,)), pltpu.SemaphoreType.DMA((1,))],
    )(gpack)


def _rs_add_pair(gpack, other, c):
    _, nj, Hh, Cc = gpack.shape
    tr = ADD_ROWS

    def body(c_ref, a_ref, b_ref, o_ref):
        o_ref[...] = (a_ref[...] + b_ref[...]).astype(BF16)

    return pl.pallas_call(
        body, name="rs_add_pair",
        grid_spec=pltpu.PrefetchScalarGridSpec(
            num_scalar_prefetch=1, grid=(nj, Hh // tr),
            in_specs=[pl.BlockSpec((None, None, tr, Cc), lambda j, i, c_ref: (c_ref[0], j, i, 0)),
                      pl.BlockSpec((None, tr, Cc), lambda j, i, c_ref: (j, i, 0))],
            out_specs=pl.BlockSpec((None, tr, Cc), lambda j, i, c_ref: (j, i, 0))),
        out_shape=jax.ShapeDtypeStruct((nj, Hh, Cc), BF16),
        compiler_params=_params(2),
    )(c, gpack, other)


def _rs_exchange_chips(part):
    nj, Hh, Cc = part.shape

    def body(src_ref, out_ref, send_sems, recv_sems):
        x, y, c = _coords()
        others = [(1 - x, y), (x, 1 - y), (1 - x, 1 - y)]
        cps = [_remote(src_ref.at[2 * ox + oy], out_ref.at[k], send_sems, recv_sems, k, (ox, oy, c))
               for k, (ox, oy) in enumerate(others)]
        for cp in cps:
            cp.start()
        for cp in cps:
            cp.wait()

    return pl.pallas_call(
        body, name="rs_exchange_chips", out_shape=jax.ShapeDtypeStruct((3, Hh, Cc), part.dtype),
        in_specs=[HBM], out_specs=HBM,
        scratch_shapes=[pltpu.SemaphoreType.DMA((3,)), pltpu.SemaphoreType.DMA((3,))],
    )(part)


def _rs_add_total(part, recv, chip):
    nj, Hh, Cc = part.shape
    tr = ADD_ROWS

    def body(chip_ref, p_ref, r0_ref, r1_ref, r2_ref, o_ref):
        f = lambda r: r[...].astype(F32)
        o_ref[...] = (f(p_ref) + f(r0_ref)) + (f(r1_ref) + f(r2_ref))

    rk = lambda k: pl.BlockSpec((None, tr, Cc), lambda i, chip_ref, k=k: (k, i, 0))
    return pl.pallas_call(
        body, name="rs_add_total",
        grid_spec=pltpu.PrefetchScalarGridSpec(
            num_scalar_prefetch=1, grid=(Hh // tr,),
            in_specs=[pl.BlockSpec((None, tr, Cc), lambda i, chip_ref: (chip_ref[0], i, 0)), rk(0), rk(1), rk(2)],
            out_specs=pl.BlockSpec((tr, Cc), lambda i, chip_ref: (i, 0))),
        out_shape=jax.ShapeDtypeStruct((Hh, Cc), F32),
        compiler_params=_params(1),
    )(chip, part, recv, recv, recv)


def _rs_share_total(total):
    Hh, Cc = total.shape

    def body(src_ref, out_ref, send_sems, recv_sems):
        x, y, c = _coords()
        cp = _remote(src_ref, out_ref, send_sems, recv_sems, 0, (x, y, 1 - c))
        cp.start()
        cp.wait()

    return pl.pallas_call(
        body, name="rs_share_total", out_shape=jax.ShapeDtypeStruct((Hh, Cc), total.dtype),
        in_specs=[HBM], out_specs=HBM,
        scratch_shapes=[pltpu.SemaphoreType.DMA((1,)), pltpu.SemaphoreType.DMA((1,))],
    )(total)


BIG = ("ffn1_gate", "ffn1_up", "ffn1_down", "w_in", "w_out", "ffn2_gate", "ffn2_up", "ffn2_down")
W_IN_ROWS = 960


def _rows(ref, start, size):
    return ref.at[pl.ds(pl.multiple_of(start, 16), size)]


def _allgather_arrays(shards):
    n = len(shards)

    def body(*refs):
        _ag_start(refs[:n], refs[n:2 * n], refs[2 * n], refs[2 * n + 1])
        _ag_finish(refs[:n], refs[n:2 * n], refs[2 * n], refs[2 * n + 1])

    _, shapes, n_sems, _, _ = _ag_comm(shards)
    return pl.pallas_call(
        body, name="allgather_weights", out_shape=shapes, in_specs=[HBM] * n, out_specs=[HBM] * n,
        scratch_shapes=[pltpu.SemaphoreType.DMA((n_sems,)), pltpu.SemaphoreType.DMA((n_sems,))],
    )(*shards)


def _ag_copies(srcs, outs, send_sems, recv_sems):
    x, y, c = _coords()
    sib = (x, y, 1 - c)
    me = 2 * x + y
    others = [(1 - x, y), (x, 1 - y), (1 - x, 1 - y)]
    plan = []
    for a, (src, out) in enumerate(zip(srcs, outs)):
        h = src.shape[0] // 2
        cp = lambda s, d, k, dev: _remote(s, d, send_sems, recv_sems, 7 * a + k, dev)
        own = cp(src, out.at[me], 6, sib)
        sends = [cp(_rows(src, c * h, h), _rows(out.at[me], c * h, h), j, (ox, oy, c)) for j, (ox, oy) in enumerate(others)]
        mine = [_rows(out.at[2 * ox + oy], c * h, h) for ox, oy in others]
        theirs = [_rows(out.at[2 * ox + oy], (1 - c) * h, h) for ox, oy in others]
        arrivals = [cp(m, m, j, sib) for j, m in enumerate(mine)]
        forwards = [cp(m, m, 3 + j, sib) for j, m in enumerate(mine)]
        forwarded = [cp(t, t, 3 + j, sib) for j, t in enumerate(theirs)]
        plan.append((own, sends, forwards, arrivals, forwarded))
    return plan


def _ag_start(srcs, outs, send_sems, recv_sems):
    for own, sends, _, _, _ in _ag_copies(srcs, outs, send_sems, recv_sems):
        own.start()
        for cp in sends:
            cp.start()


def _ag_finish(srcs, outs, send_sems, recv_sems):
    plan = _ag_copies(srcs, outs, send_sems, recv_sems)
    for _, _, forwards, arrivals, _ in plan:
        for arrived, fwd in zip(arrivals, forwards):
            arrived.wait_recv()
            fwd.start()
    for own, sends, forwards, _, forwarded in plan:
        for cp in forwarded:
            cp.wait_recv()
        own.wait_recv()
        for cp in [own] + sends + forwards:
            cp.wait_send()


def _ag_comm(shards):
    shapes = [jax.ShapeDtypeStruct((N_CHIPS,) + s.shape, s.dtype) for s in shards]
    return (list(shards), shapes, 7 * len(shards), _ag_start, _ag_finish)


def _swap_sibling(arrs, pick_other_half, name):
    n = len(arrs)
    outs = [jax.ShapeDtypeStruct((a.shape[0], a.shape[1] // 2) + a.shape[2:] if pick_other_half else a.shape, a.dtype) for a in arrs]

    def body(*refs):
        srcs, dsts, send_sems, recv_sems = refs[:n], refs[n:2 * n], refs[2 * n], refs[2 * n + 1]
        x, y, c = _coords()
        cps = []
        for a in range(n):
            src = srcs[a]
            if pick_other_half:
                h = src.shape[1] // 2
                src = src.at[:, pl.ds(pl.multiple_of((1 - c) * h, 16), h)]
            cp = _remote(src, dsts[a], send_sems, recv_sems, a, (x, y, 1 - c))
            cp.start()
            cps.append(cp)
        for cp in cps:
            cp.wait()

    return pl.pallas_call(
        body, name=name, out_shape=outs, in_specs=[HBM] * n, out_specs=[HBM] * n,
        scratch_shapes=[pltpu.SemaphoreType.DMA((n,)), pltpu.SemaphoreType.DMA((n,))],
    )(*arrs)


def _rs_add_pairs(gs, others, c, name):
    n = len(gs)
    hbs = [g.shape[1] // 4 for g in gs]

    def body(c_ref, *refs):
        for a in range(n):
            refs[2 * n + a][...] = (refs[a][...] + refs[n + a][...]).astype(BF16)

    mine = lambda hb: pl.BlockSpec((None, hb, D_MODEL), lambda j, s, c_ref: (j, c_ref[0] * 2 + s, 0))
    flat = lambda hb: pl.BlockSpec((None, hb, D_MODEL), lambda j, s, c_ref: (j, s, 0))
    return pl.pallas_call(
        body, name=name,
        grid_spec=pltpu.PrefetchScalarGridSpec(
            num_scalar_prefetch=1, grid=(N_CHIPS, 2),
            in_specs=[mine(hb) for hb in hbs] + [flat(hb) for hb in hbs],
            out_specs=[flat(hb) for hb in hbs]),
        out_shape=[jax.ShapeDtypeStruct(o.shape, BF16) for o in others],
        compiler_params=_params(2),
    )(c, *gs, *others)


def _rs_exchange_arrays(parts):
    n = len(parts)

    def body(*refs):
        _rsx_start(refs[:n], refs[n:2 * n], refs[2 * n], refs[2 * n + 1])
        _rsx_finish(refs[:n], refs[n:2 * n], refs[2 * n], refs[2 * n + 1])

    _, shapes, n_sems, _, _ = _rsx_comm(parts)
    return pl.pallas_call(
        body, name="rs_exchange_chips", out_shape=shapes, in_specs=[HBM] * n, out_specs=[HBM] * n,
        scratch_shapes=[pltpu.SemaphoreType.DMA((n_sems,)), pltpu.SemaphoreType.DMA((n_sems,))],
    )(*parts)


def _rsx_copies(srcs, dsts, send_sems, recv_sems):
    x, y, c = _coords()
    others = [(1 - x, y), (x, 1 - y), (1 - x, 1 - y)]
    return [_remote(src.at[2 * ox + oy], dst.at[k], send_sems, recv_sems, 3 * a + k, (ox, oy, c))
            for a, (src, dst) in enumerate(zip(srcs, dsts)) for k, (ox, oy) in enumerate(others)]


def _rsx_start(srcs, dsts, send_sems, recv_sems):
    for cp in _rsx_copies(srcs, dsts, send_sems, recv_sems):
        cp.start()


def _rsx_finish(srcs, dsts, send_sems, recv_sems):
    for cp in _rsx_copies(srcs, dsts, send_sems, recv_sems):
        cp.wait()


def _rsx_comm(parts):
    shapes = [jax.ShapeDtypeStruct((3,) + p.shape[1:], p.dtype) for p in parts]
    return (list(parts), shapes, 3 * len(parts), _rsx_start, _rsx_finish)


def _rs_add_totals(parts, recvs, chip):
    n = len(parts)
    hbs = [p.shape[1] // 2 for p in parts]

    def body(chip_ref, *refs):
        f = lambda r: r[...].astype(F32)
        for a in range(n):
            p, r0, r1, r2 = refs[a], refs[n + 3 * a], refs[n + 3 * a + 1], refs[n + 3 * a + 2]
            refs[4 * n + a][...] = (f(p) + f(r0)) + (f(r1) + f(r2))

    own = lambda hb: pl.BlockSpec((None, hb, D_MODEL), lambda s, chip_ref: (chip_ref[0], s, 0))
    slot = lambda hb, k: pl.BlockSpec((None, hb, D_MODEL), lambda s, chip_ref, k=k: (k, s, 0))
    recv_specs = [slot(hb, k) for hb in hbs for k in range(3)]
    recv_args = [r for r in recvs for _ in range(3)]
    return pl.pallas_call(
        body, name="rs_add_totals",
        grid_spec=pltpu.PrefetchScalarGridSpec(
            num_scalar_prefetch=1, grid=(2,),
            in_specs=[own(hb) for hb in hbs] + recv_specs,
            out_specs=[pl.BlockSpec((hb, D_MODEL), lambda s, chip_ref: (s, 0)) for hb in hbs]),
        out_shape=[jax.ShapeDtypeStruct(p.shape[1:], F32) for p in parts],
        compiler_params=_params(1),
    )(chip, *parts, *recv_args)


def _permute_w_in(w):
    return jnp.concatenate([w[:, :3072], w[:, 3080:IN_COLS], w[:, 3072:3080],
                            jnp.zeros((w.shape[0], IN_COLS_PADDED - IN_COLS), w.dtype)], axis=1)


def _pack_rows(shards, dtype):
    rows = [shards[n].astype(dtype).reshape(r, D_MODEL) for n, r in PACK_SECTIONS]
    used = sum(r for _, r in PACK_SECTIONS)
    rows.append(jnp.zeros((PACK_ROWS - used, D_MODEL), dtype))
    return jnp.concatenate(rows, axis=0)


def _unpack_rows(pack, shapes):
    out, at = {}, 0
    for n, r in PACK_SECTIONS:
        out[n] = pack[at:at + r].reshape(shapes[n])
        at += r
    return out


SHARD_SHAPES = dict(ffn1_gate=(1024, 704), ffn1_up=(1024, 704), ffn1_down=(704, 1024), w_in=(1024, 898),
                    w_out=(256, 1024), ffn2_gate=(1024, 704), ffn2_up=(1024, 704), ffn2_down=(704, 1024))
ROW_SHARDED = ("ffn1_down", "w_out", "ffn2_down")
SMALL_ROWS = 16


def _pad_row(v):
    v = v.reshape(1, -1)
    return jnp.pad(v, ((0, 0), (0, D_MODEL - v.shape[1])))


def kernel(x, norm_ffn1, ffn1_gate, ffn1_up, ffn1_down, norm_mix, w_in, conv_w, a_log, dt_bias, dn_norm, w_out, norm_ffn2, ffn2_gate, ffn2_up, ffn2_down, norm_final, loss_target, m_norm_ffn1, m_ffn1_gate, m_ffn1_up, m_ffn1_down, m_norm_mix, m_w_in, m_conv_w, m_a_log, m_dt_bias, m_dn_norm, m_w_out, m_norm_ffn2, m_ffn2_gate, m_ffn2_up, m_ffn2_down, m_norm_final, v_norm_ffn1, v_ffn1_gate, v_ffn1_up, v_ffn1_down, v_norm_mix, v_w_in, v_conv_w, v_a_log, v_dt_bias, v_dn_norm, v_w_out, v_norm_ffn2, v_ffn2_gate, v_ffn2_up, v_ffn2_down, v_norm_final):
    cx, cy, cc = _coords()
    chip = 2 * cx + cy
    big_w = dict(ffn1_gate=ffn1_gate[0], ffn1_up=ffn1_up[0], ffn1_down=ffn1_down[0], w_in=w_in[0], w_out=w_out[0],
                 ffn2_gate=ffn2_gate[0], ffn2_up=ffn2_up[0], ffn2_down=ffn2_down[0])
    big_m = dict(ffn1_gate=m_ffn1_gate[0], ffn1_up=m_ffn1_up[0], ffn1_down=m_ffn1_down[0], w_in=m_w_in[0], w_out=m_w_out[0],
                 ffn2_gate=m_ffn2_gate[0], ffn2_up=m_ffn2_up[0], ffn2_down=m_ffn2_down[0])
    big_v = dict(ffn1_gate=v_ffn1_gate[0], ffn1_up=v_ffn1_up[0], ffn1_down=v_ffn1_down[0], w_in=v_w_in[0], w_out=v_w_out[0],
                 ffn2_gate=v_ffn2_gate[0], ffn2_up=v_ffn2_up[0], ffn2_down=v_ffn2_down[0])

    early = tuple(n for n in BIG if n not in LATE_WEIGHTS)
    wts = dict(zip(early, _allgather_arrays([big_w[n].astype(BF16) for n in early])))
    wts["w_in"] = _permute_w_in(jnp.concatenate([wts["w_in"][j] for j in range(N_CHIPS)], axis=1))
    dist = dict(late=[big_w[n].astype(BF16) for n in LATE_WEIGHTS], c=cc.reshape(1).astype(jnp.int32),
                chip=chip.reshape(1).astype(jnp.int32))

    conv_shard = conv_w[0]
    emb = jnp.concatenate([jnp.where((chip == j) & (cc == 0), conv_shard, 0.0) for j in range(N_CHIPS)], axis=1)
    emb = jnp.pad(emb.reshape(6, D_MODEL), ((0, 2), (0, 0)))
    conv_full = _allreduce_small(emb, "allgather_conv_w")[:6].reshape(CONV_WIDTH, 3 * DN_WIDTH)

    zvec = jnp.zeros((1, 128), F32)
    small = dict(norm_ffn1=norm_ffn1, norm_mix=norm_mix, norm_ffn2=norm_ffn2, norm_final=norm_final[None],
                 conv_w=conv_full, avec=zvec.at[0, DN_HEADS:2 * DN_HEADS].set(a_log[0]),
                 dvec=zvec.at[0, DN_HEADS:2 * DN_HEADS].set(dt_bias[0]), dn_norm=dn_norm)

    loss, grad_x, reduced, sg = _local_step(x[0], loss_target[0], wts, small, dist)

    rows = [sg["norm_ffn1"], sg["norm_mix"], sg["norm_ffn2"], sg["norm_final"], _pad_row(sg["a_log"]), _pad_row(sg["dt_bias"]),
            _pad_row(sg["dn_norm"]), _pad_row(loss[0:1]), sg["conv_w"].reshape(6, D_MODEL), jnp.zeros((2, D_MODEL), F32)]
    red = _allreduce_small(jnp.concatenate(rows, axis=0), "allreduce_small")
    loss_out = red[7, 0]
    g_conv_full = red[8:14].reshape(CONV_WIDTH, 3 * DN_WIDTH)
    g_conv = lax.dynamic_slice_in_dim(g_conv_full, chip * (3 * DN_WIDTH // N_CHIPS), 3 * DN_WIDTH // N_CHIPS, axis=1)
    g_small = dict(norm_ffn1=red[0:1], norm_mix=red[1:2], norm_ffn2=red[2:3], norm_final=red[3],
                   a_log=red[4:5, DN_HEADS:2 * DN_HEADS], dt_bias=red[5:6, DN_HEADS:2 * DN_HEADS], dn_norm=red[6:7, :DN_HEAD_DIM])

    shard_g = {}
    for n in BIG:
        mine, other = reduced[n]
        full = jnp.where(cc == 0, jnp.concatenate([mine, other], axis=0), jnp.concatenate([other, mine], axis=0))
        if n == "w_in":
            full = full[:IN_COLS // N_CHIPS]
        shard_g[n] = full if n in ROW_SHARDED else full.T

    out_g, out_d, out_m, out_v = {}, {}, {}, {}
    for n in BIG:
        d, nm, nv = _adamw(big_w[n], shard_g[n], big_m[n], big_v[n], "adamw_" + n)
        out_g[n], out_d[n], out_m[n], out_v[n] = shard_g[n][None], d[None], nm[None], nv[None]
    d, nm, nv = _adamw(conv_w[0], g_conv, m_conv_w[0], v_conv_w[0], "adamw_conv_w")
    out_g["conv_w"], out_d["conv_w"], out_m["conv_w"], out_v["conv_w"] = g_conv[None], d[None], nm[None], nv[None]

    small_names = ("norm_ffn1", "norm_mix", "norm_ffn2", "norm_final", "a_log", "dt_bias", "dn_norm")
    small_w = dict(norm_ffn1=norm_ffn1, norm_mix=norm_mix, norm_ffn2=norm_ffn2, norm_final=norm_final, a_log=a_log,
                   dt_bias=dt_bias, dn_norm=dn_norm)
    small_m = dict(norm_ffn1=m_norm_ffn1, norm_mix=m_norm_mix, norm_ffn2=m_norm_ffn2, norm_final=m_norm_final, a_log=m_a_log,
                   dt_bias=m_dt_bias, dn_norm=m_dn_norm)
    small_v = dict(norm_ffn1=v_norm_ffn1, norm_mix=v_norm_mix, norm_ffn2=v_norm_ffn2, norm_final=v_norm_final, a_log=v_a_log,
                   dt_bias=v_dt_bias, dn_norm=v_dn_norm)
    stack = lambda dct: jnp.concatenate([_pad_row(dct[n]) for n in small_names] + [jnp.zeros((1, D_MODEL), F32)], axis=0)
    d, nm, nv = _adamw(stack(small_w), stack(g_small), stack(small_m), stack(small_v), "adamw_small")
    for k, n in enumerate(small_names):
        shape = small_w[n].shape
        size = math.prod(shape)
        out_g[n] = g_small[n].reshape(shape)
        out_d[n], out_m[n], out_v[n] = (t[k, :size].reshape(shape) for t in (d, nm, nv))

    order = ("norm_ffn1", "ffn1_gate", "ffn1_up", "ffn1_down", "norm_mix", "w_in", "conv_w", "a_log", "dt_bias", "dn_norm",
             "w_out", "norm_ffn2", "ffn2_gate", "ffn2_up", "ffn2_down", "norm_final")
    return (loss_out, grad_x[None], *[out_g[n] for n in order], *[out_d[n] for n in order],
            *[out_m[n] for n in order], *[out_v[n] for n in order])
```

```python
import functools
import math

import jax
import jax.numpy as jnp
from jax import lax
from jax.experimental import pallas as pl
from jax.experimental.pallas import tpu as pltpu

F32 = jnp.float32
BF16 = jnp.bfloat16
HI = lax.Precision.HIGH

D_MODEL = 1024
D_FF = 2816
ATTN_HEADS = 8
ATTN_WIDTH = 512
ATTN_BLOCK = 128
DILATIONS = (1, 4, 16)
DN_HEADS = 4
DN_HEAD_DIM = 128
DN_WIDTH = 512
DN_CHUNK = 64
CONV_WIDTH = 4
NORM_EPS = 1e-6
L2_EPS = 1e-6
IN_COLS = 3592
IN_COLS_PADDED = 3712
N_CHIPS = 4

ADAM_LR = 0.001
ADAM_B1 = 0.9
ADAM_B2 = 0.999
ADAM_EPS = 1e-08
ADAM_WD = 0.01
ADAM_STEP = 10

VMEM_LIMIT = 56 * 1024 * 1024
NEG_BIG = -1e30
MESH = pl.DeviceIdType.MESH


def _params(n_grid, vmem=VMEM_LIMIT):
    return pltpu.CompilerParams(dimension_semantics=("arbitrary",) * n_grid, vmem_limit_bytes=vmem)


def _call(body, args, *, name, grid, in_specs, out_specs, out_shape, scratch_shapes=(), comm=None):
    n_in, n_out, n_scr = len(in_specs), len(out_specs), len(scratch_shapes)
    hbm = pl.BlockSpec(memory_space=pl.ANY)
    srcs, dst_shapes, n_sems, start, finish = comm if comm is not None else ((), (), 0, None, None)
    ns, nd = len(srcs), len(dst_shapes)

    def full(*refs):
        ins, c_src = refs[:n_in], refs[n_in:n_in + ns]
        at = n_in + ns
        outs, c_dst = refs[at:at + n_out], refs[at + n_out:at + n_out + nd]
        scr = refs[at + n_out + nd:at + n_out + nd + n_scr]
        if comm is not None:
            ids = [pl.program_id(a) for a in range(len(grid))]
            first = functools.reduce(jnp.logical_and, [i == 0 for i in ids])
            last = functools.reduce(jnp.logical_and, [i == g - 1 for i, g in zip(ids, grid)])

            @pl.when(first)
            def _():
                start(c_src, c_dst, refs[-2], refs[-1])

        body(*ins, *outs, *scr)
        if comm is not None:
            @pl.when(last)
            def _():
                finish(c_src, c_dst, refs[-2], refs[-1])

    sems = [pltpu.SemaphoreType.DMA((n_sems,)), pltpu.SemaphoreType.DMA((n_sems,))] if comm is not None else []
    res = pl.pallas_call(
        full, name=name, grid=grid, in_specs=list(in_specs) + [hbm] * ns, out_specs=list(out_specs) + [hbm] * nd,
        out_shape=list(out_shape) + list(dst_shapes), scratch_shapes=list(scratch_shapes) + sems,
        compiler_params=_params(len(grid)),
    )(*args, *srcs)
    return res[:n_out], res[n_out:]


def _nt(a, b, precision=None):
    return lax.dot_general(a, b, (((1,), (1,)), ((), ())), preferred_element_type=F32, precision=precision)


def _tn(a, b, precision=None):
    return lax.dot_general(a, b, (((0,), (0,)), ((), ())), preferred_element_type=F32, precision=precision)


def _nn(a, b, precision=None):
    return jnp.dot(a, b, preferred_element_type=F32, precision=precision)


def _sigmoid(x):
    return 1.0 / (1.0 + jnp.exp(-x))


def _ffn_fwd(x, gain, wg, wu, wd, name, comm=None):
    S, D = x.shape
    nf, _, tf = wg.shape
    tm = 512

    def body(x_ref, gain_ref, wg_ref, wu_ref, wd_ref, xo_ref, h_ref, g_ref, u_ref, acc_ref, hs_ref):
        j = pl.program_id(1)

        @pl.when(j == 0)
        def _():
            xf = x_ref[...]
            r = lax.rsqrt(jnp.mean(xf * xf, axis=-1, keepdims=True) + NORM_EPS)
            h = (xf * r * gain_ref[...]).astype(BF16)
            hs_ref[...] = h
            h_ref[...] = h
            acc_ref[...] = jnp.zeros_like(acc_ref)

        h = hs_ref[...]
        g = _nn(h, wg_ref[...])
        u = _nn(h, wu_ref[...])
        g_ref[...] = g.astype(BF16)
        u_ref[...] = u.astype(BF16)
        act = g * _sigmoid(g) * u
        acc_ref[...] += _nn(act.astype(BF16), wd_ref[...])

        @pl.when(j == nf - 1)
        def _():
            xo_ref[...] = x_ref[...] + 0.5 * acc_ref[...]

    return _call(
        body, (x, gain, wg, wu, wd), name=name, grid=(S // tm, nf), comm=comm,
        in_specs=[pl.BlockSpec((tm, D), lambda i, j: (i, 0)),
                  pl.BlockSpec((1, D), lambda i, j: (0, 0)),
                  pl.BlockSpec((None, D, tf), lambda i, j: (j, 0, 0)),
                  pl.BlockSpec((None, D, tf), lambda i, j: (j, 0, 0)),
                  pl.BlockSpec((None, tf, D), lambda i, j: (j, 0, 0))],
        out_specs=[pl.BlockSpec((tm, D), lambda i, j: (i, 0)),
                   pl.BlockSpec((tm, D), lambda i, j: (i, 0)),
                   pl.BlockSpec((None, tm, tf), lambda i, j: (j, i, 0)),
                   pl.BlockSpec((None, tm, tf), lambda i, j: (j, i, 0))],
        out_shape=[jax.ShapeDtypeStruct((S, D), F32), jax.ShapeDtypeStruct((S, D), BF16),
                   jax.ShapeDtypeStruct((nf, S, tf), BF16), jax.ShapeDtypeStruct((nf, S, tf), BF16)],
        scratch_shapes=[pltpu.VMEM((tm, D), F32), pltpu.VMEM((tm, D), BF16)])


def _rmsnorm_bwd(dh, xf, gain):
    r = lax.rsqrt(jnp.mean(xf * xf, axis=-1, keepdims=True) + NORM_EPS)
    xhat = xf * r
    dgain = jnp.sum(dh * xhat, axis=0, keepdims=True)
    dxh = dh * gain
    dx = r * (dxh - xhat * jnp.mean(dxh * xhat, axis=-1, keepdims=True))
    return dx, dgain


def _ffn_bwd(dxo, x, gain, g, u, wd, wg, wu, name, comm=None):
    S, D = x.shape
    nf, _, tf = g.shape
    tm = 512

    def body(dxo_ref, x_ref, gain_ref, g_ref, u_ref, wd_ref, wg_ref, wu_ref,
             dx_ref, dgain_ref, dg_ref, du_ref, act_ref, dout_ref, acc_ref, ds_ref):
        i = pl.program_id(0)
        j = pl.program_id(1)

        @pl.when(j == 0)
        def _():
            d = (0.5 * dxo_ref[...]).astype(BF16)
            ds_ref[...] = d
            dout_ref[...] = d
            acc_ref[...] = jnp.zeros_like(acc_ref)

        @pl.when((i == 0) & (j == 0))
        def _():
            dgain_ref[...] = jnp.zeros_like(dgain_ref)

        for half in range(2):
            rows = slice(half * (tm // 2), (half + 1) * (tm // 2))
            dact = _nt(ds_ref[rows, :], wd_ref[...])
            gv = g_ref[rows, :].astype(F32)
            uv = u_ref[rows, :].astype(F32)
            sg = _sigmoid(gv)
            silu = gv * sg
            act_ref[rows, :] = (silu * uv).astype(BF16)
            dgv = (dact * uv * (sg * (1.0 + gv * (1.0 - sg)))).astype(BF16)
            duv = (dact * silu).astype(BF16)
            dg_ref[rows, :] = dgv
            du_ref[rows, :] = duv
            acc_ref[rows, :] += _nt(dgv, wg_ref[...]) + _nt(duv, wu_ref[...])

        @pl.when(j == nf - 1)
        def _():
            dx, dgain = _rmsnorm_bwd(acc_ref[...], x_ref[...], gain_ref[...])
            dx_ref[...] = dxo_ref[...] + dx
            dgain_ref[...] += dgain

    return _call(
        body, (dxo, x, gain, g, u, wd, wg, wu), name=name, grid=(S // tm, nf), comm=comm,
        in_specs=[pl.BlockSpec((tm, D), lambda i, j: (i, 0)),
                  pl.BlockSpec((tm, D), lambda i, j: (i, 0)),
                  pl.BlockSpec((1, D), lambda i, j: (0, 0)),
                  pl.BlockSpec((None, tm, tf), lambda i, j: (j, i, 0)),
                  pl.BlockSpec((None, tm, tf), lambda i, j: (j, i, 0)),
                  pl.BlockSpec((None, tf, D), lambda i, j: (j, 0, 0)),
                  pl.BlockSpec((None, D, tf), lambda i, j: (j, 0, 0)),
                  pl.BlockSpec((None, D, tf), lambda i, j: (j, 0, 0))],
        out_specs=[pl.BlockSpec((tm, D), lambda i, j: (i, 0)),
                   pl.BlockSpec((1, D), lambda i, j: (0, 0)),
                   pl.BlockSpec((None, tm, tf), lambda i, j: (j, i, 0)),
                   pl.BlockSpec((None, tm, tf), lambda i, j: (j, i, 0)),
                   pl.BlockSpec((None, tm, tf), lambda i, j: (j, i, 0)),
                   pl.BlockSpec((tm, D), lambda i, j: (i, 0))],
        out_shape=[jax.ShapeDtypeStruct((S, D), F32), jax.ShapeDtypeStruct((1, D), F32),
                   jax.ShapeDtypeStruct((nf, S, tf), BF16), jax.ShapeDtypeStruct((nf, S, tf), BF16),
                   jax.ShapeDtypeStruct((nf, S, tf), BF16), jax.ShapeDtypeStruct((S, D), BF16)],
        scratch_shapes=[pltpu.VMEM((tm, D), F32), pltpu.VMEM((tm, D), BF16)])


def _matmul_tn(a, b, tm, tk, name):
    K, M = a.shape
    N = b.shape[1]

    def body(a_ref, b_ref, o_ref):
        @pl.when(pl.program_id(1) == 0)
        def _():
            o_ref[...] = jnp.zeros_like(o_ref)

        o_ref[...] += _tn(a_ref[...], b_ref[...])

    return pl.pallas_call(
        body, name=name, grid=(M // tm, K // tk),
        in_specs=[pl.BlockSpec((tk, tm), lambda i, k: (k, i)),
                  pl.BlockSpec((tk, N), lambda i, k: (k, 0))],
        out_specs=pl.BlockSpec((tm, N), lambda i, k: (i, 0)),
        out_shape=jax.ShapeDtypeStruct((M, N), F32),
        compiler_params=_params(2),
    )(a, b)


def _dw_chunks(a, b, tk, name, comm=None):
    nf, S, tf = a.shape
    N = b.shape[1]

    def body(a_ref, b_ref, o_ref):
        @pl.when(pl.program_id(1) == 0)
        def _():
            o_ref[...] = jnp.zeros_like(o_ref)

        o_ref[...] += _tn(a_ref[...], b_ref[...])

    (out,), landed = _call(
        body, (a, b), name=name, grid=(nf, S // tk), comm=comm,
        in_specs=[pl.BlockSpec((None, tk, tf), lambda j, k: (j, k, 0)),
                  pl.BlockSpec((tk, N), lambda j, k: (k, 0))],
        out_specs=[pl.BlockSpec((None, tf, N), lambda j, k: (j, 0, 0))],
        out_shape=[jax.ShapeDtypeStruct((nf, tf, N), F32)])
    return out, landed


VIEW_TILE = 512


def _view_spec(d, tile=VIEW_TILE):
    return pl.BlockSpec((tile // d, d * ATTN_WIDTH), lambda i: (i, 0))


def _view_shape(S, d, dtype):
    return jax.ShapeDtypeStruct((S // d, d * ATTN_WIDTH), dtype)


def _tile_to_views(val, planes, out_refs):
    for g in range(4):
        planes[g] = val[:, g * 128:(g + 1) * 128]
    for d, ref in zip(DILATIONS, out_refs):
        if d == 1:
            ref[...] = val.astype(ref.dtype)
            continue
        for r in range(d):
            for g in range(4):
                ref[:, r * ATTN_WIDTH + g * 128:r * ATTN_WIDTH + (g + 1) * 128] = (
                    planes[g, pl.ds(r, planes.shape[1] // d, stride=d), :].astype(ref.dtype))


def _view_to_tile(ref, d, planes):
    if d == 1:
        return ref[...]
    for r in range(d):
        for g in range(4):
            planes[g, pl.ds(r, planes.shape[1] // d, stride=d), :] = ref[:, r * ATTN_WIDTH + g * 128:r * ATTN_WIDTH + (g + 1) * 128]
    return jnp.concatenate([planes[g] for g in range(4)], axis=1)


def _inproj_fwd(x, gain, w_in_p):
    S, D = x.shape
    tm = VIEW_TILE
    W = ATTN_WIDTH

    def body(x_ref, gain_ref, w_ref, h_ref, q1, q4, q16, k1, k4, k16, v1, v4, v16, dq_ref, dk_ref, dv_ref, gate_ref, bd_ref,
             planes):
        xf = x_ref[...]
        r = lax.rsqrt(jnp.mean(xf * xf, axis=-1, keepdims=True) + NORM_EPS)
        h = (xf * r * gain_ref[...]).astype(BF16)
        h_ref[...] = h
        _tile_to_views(_nn(h, w_ref[:, 0:W]) * 0.125, planes, (q1, q4, q16))
        _tile_to_views(_nn(h, w_ref[:, W:2 * W]), planes, (k1, k4, k16))
        _tile_to_views(_nn(h, w_ref[:, 2 * W:3 * W]), planes, (v1, v4, v16))
        dq_ref[...] = _nn(h, w_ref[:, 3 * W:4 * W])
        dk_ref[...] = _nn(h, w_ref[:, 4 * W:5 * W])
        dv_ref[...] = _nn(h, w_ref[:, 5 * W:6 * W])
        gate_ref[...] = _nn(h, w_ref[:, 6 * W:7 * W])
        bd_ref[...] = _nn(h, w_ref[:, 7 * W:7 * W + 128])

    tok = lambda w: pl.BlockSpec((tm, w), lambda i: (i, 0))
    return pl.pallas_call(
        body, name="inproj_fwd", grid=(S // tm,),
        in_specs=[tok(D), pl.BlockSpec((1, D), lambda i: (0, 0)),
                  pl.BlockSpec((D, IN_COLS_PADDED), lambda i: (0, 0))],
        out_specs=[tok(D)] + [_view_spec(d) for d in DILATIONS] * 3 + [tok(W)] * 4 + [tok(128)],
        out_shape=[jax.ShapeDtypeStruct((S, D), BF16)] + [_view_shape(S, d, BF16) for d in DILATIONS] * 3
                  + [jax.ShapeDtypeStruct((S, W), F32)] * 4 + [jax.ShapeDtypeStruct((S, 128), F32)],
        scratch_shapes=[pltpu.VMEM((4, tm, 128), F32)],
        compiler_params=_params(1),
    )(x, gain, w_in_p)


def _inproj_bwd(dxo, x, gain, attn_grads, dsecs, dbd, w_in_p):
    S, D = x.shape
    tm = VIEW_TILE // 2
    W = ATTN_WIDTH

    def body(dxo_ref, x_ref, gain_ref, *rest):
        views, (s3, s4, s5, s6, dbd_ref, w_ref, dx_ref, dgain_ref, dproj_ref, planes) = rest[:9], rest[9:]

        @pl.when(pl.program_id(0) == 0)
        def _():
            dgain_ref[...] = jnp.zeros_like(dgain_ref)

        secs = []
        for k in range(3):
            parts = [_view_to_tile(views[3 * k + p], d, planes) for p, d in enumerate(DILATIONS)]
            secs.append(parts[0] + parts[1] + parts[2])
        secs += [s3[...], s4[...], s5[...], s6[...]]
        dh = jnp.zeros((tm, D), F32)
        for k, s in enumerate(secs):
            d = s.astype(BF16)
            dproj_ref[:, k * W:(k + 1) * W] = d
            dh += _nt(d, w_ref[:, k * W:(k + 1) * W])
        d = dbd_ref[...].astype(BF16)
        dproj_ref[:, 7 * W:7 * W + 128] = d
        dh += _nt(d, w_ref[:, 7 * W:7 * W + 128])
        dx, dgain = _rmsnorm_bwd(dh, x_ref[...], gain_ref[...])
        dx_ref[...] = dxo_ref[...] + dx
        dgain_ref[...] += dgain

    tok = lambda w: pl.BlockSpec((tm, w), lambda i: (i, 0))
    return pl.pallas_call(
        body, name="inproj_bwd", grid=(S // tm,),
        in_specs=[tok(D), tok(D), pl.BlockSpec((1, D), lambda i: (0, 0))] + [_view_spec(d, tm) for d in DILATIONS] * 3
                 + [tok(W)] * 4 + [tok(128)] + [pl.BlockSpec((D, IN_COLS_PADDED), lambda i: (0, 0))],
        out_specs=[tok(D), pl.BlockSpec((1, D), lambda i: (0, 0)), tok(IN_COLS_PADDED)],
        out_shape=[jax.ShapeDtypeStruct((S, D), F32), jax.ShapeDtypeStruct((1, D), F32),
                   jax.ShapeDtypeStruct((S, IN_COLS_PADDED), BF16)],
        scratch_shapes=[pltpu.VMEM((4, tm, 128), F32)],
        compiler_params=_params(1),
    )(dxo, x, gain, *[g for grads in attn_grads for g in grads], *dsecs, dbd, w_in_p)


def _slope(h):
    return 2.0 ** (-8.0 * (h + 1) / ATTN_HEADS)


def _head_bias(steps, d, heads=tuple(range(ATTN_HEADS))):
    stepsf = steps.astype(F32)
    return jnp.stack([stepsf * (-_slope(h) * d) for h in heads])


def _hnt(a, b):
    return lax.dot_general(a, b, (((2,), (2,)), ((0,), (0,))), preferred_element_type=F32)


def _hnn(a, b):
    return lax.dot_general(a, b, (((2,), (1,)), ((0,), (0,))), preferred_element_type=F32)


def _head_cols(tile, lo, big):
    return [_head_col(tile, lo, big), _head_col(tile, jnp.logical_not(lo), big)]


def _attn_fwd(q, k, v, d, name):
    L = q.shape[0]
    nb = L // ATTN_BLOCK
    B = ATTN_BLOCK

    def body(q_ref, kp_ref, kc_ref, vp_ref, vc_ref, acc_ref, m_ref, l_ref):
        n = pl.program_id(1)
        qi = lax.broadcasted_iota(jnp.int32, (B, 2 * B), 0)
        kj = lax.broadcasted_iota(jnp.int32, (B, 2 * B), 1)
        steps = qi + B - kj
        valid = (steps >= 0) & (steps <= B) & ((kj >= B) | (n > 0))
        lo = lax.broadcasted_iota(jnp.int32, (B, 128), 1) < 64
        qs, ks, vs = [], [], []
        for G in range(4):
            sl = slice(G * 128, (G + 1) * 128)
            qg = q_ref[:, sl]
            kg = jnp.concatenate([kp_ref[:, sl], kc_ref[:, sl]], axis=0)
            vg = jnp.concatenate([vp_ref[:, sl], vc_ref[:, sl]], axis=0)
            qs += [jnp.where(lo, qg, jnp.zeros_like(qg)), jnp.where(lo, jnp.zeros_like(qg), qg)]
            ks += [kg, kg]
            vs += [vg, vg]
        s = _hnt(jnp.stack(qs), jnp.stack(ks)) + _head_bias(steps, d)
        s = jnp.where(valid, s, NEG_BIG)
        m = jnp.max(s, axis=-1, keepdims=True)
        p = jnp.exp(s - m)
        l = jnp.sum(p, axis=-1, keepdims=True)
        a = _hnn(p.astype(BF16), jnp.stack(vs))
        for G in range(4):
            sl = slice(G * 128, (G + 1) * 128)
            acc_ref[:, sl] = jnp.where(lo, a[2 * G], a[2 * G + 1])
            m_ref[:, sl] = jnp.where(lo, m[2 * G], m[2 * G + 1])
            l_ref[:, sl] = jnp.where(lo, l[2 * G], l[2 * G + 1])

    cur = pl.BlockSpec((B, ATTN_WIDTH), lambda r, n: (n, r))
    prev = pl.BlockSpec((B, ATTN_WIDTH), lambda r, n: (jnp.maximum(n - 1, 0), r))
    return pl.pallas_call(
        body, name=name, grid=(d, nb),
        in_specs=[cur, prev, cur, prev, cur],
        out_specs=[cur, cur, cur],
        out_shape=[jax.ShapeDtypeStruct((L, d * ATTN_WIDTH), F32)] * 3,
        compiler_params=_params(2),
    )(q, k, k, v, v)


def _attn_merge(parts):
    S = parts[0][0].shape[0]
    tm = VIEW_TILE

    def body(a1, m1, l1, a2, m2, l2, a3, m3, l3, o_ref, lse1, lse4, lse16, planes):
        ins = ((a1, m1, l1), (a2, m2, l2), (a3, m3, l3))
        acc, ms, ls = [], [], []
        for d, (a, m, l) in zip(DILATIONS, ins):
            acc.append(_view_to_tile(a, d, planes))
            ms.append(_view_to_tile(m, d, planes))
            ls.append(_view_to_tile(l, d, planes))
        mx = jnp.maximum(jnp.maximum(ms[0], ms[1]), ms[2])
        es = [jnp.exp(m - mx) for m in ms]
        den = es[0] * ls[0] + es[1] * ls[1] + es[2] * ls[2]
        num = es[0] * acc[0] + es[1] * acc[1] + es[2] * acc[2]
        o_ref[...] = num / den
        _tile_to_views(mx + jnp.log(den), planes, (lse1, lse4, lse16))

    views = [_view_spec(d) for d in DILATIONS]
    flat = [t for p in parts for t in p]
    return pl.pallas_call(
        body, name="attn_merge", grid=(S // tm,),
        in_specs=[views[p] for p in range(3) for _ in range(3)],
        out_specs=[views[0]] + views,
        out_shape=[jax.ShapeDtypeStruct((S, ATTN_WIDTH), F32)] + [_view_shape(S, d, F32) for d in DILATIONS],
        scratch_shapes=[pltpu.VMEM((4, tm, 128), F32)],
        compiler_params=_params(1),
    )(*flat)


def _head_col(t, msk, big):
    if big:
        return jnp.max(jnp.where(msk, t, NEG_BIG), axis=-1, keepdims=True)
    return jnp.sum(jnp.where(msk, t, 0.0), axis=-1, keepdims=True) * (1.0 / 64.0)


def _attn_bwd_q(q, k, v, do, lse, dd, d, name):
    L = q.shape[0]
    nb = L // ATTN_BLOCK
    B = ATTN_BLOCK

    def body(q_ref, kp_ref, kc_ref, vp_ref, vc_ref, do_ref, lse_ref, dd_ref, dq_ref):
        n = pl.program_id(1)
        qi = lax.broadcasted_iota(jnp.int32, (B, 2 * B), 0)
        kj = lax.broadcasted_iota(jnp.int32, (B, 2 * B), 1)
        steps = qi + B - kj
        valid = (steps >= 0) & (steps <= B) & ((kj >= B) | (n > 0))
        lo = lax.broadcasted_iota(jnp.int32, (B, 128), 1) < 64
        qs, ks, vs, dos, lses, dcols = [], [], [], [], [], []
        for G in range(4):
            sl = slice(G * 128, (G + 1) * 128)
            qg = q_ref[:, sl]
            kg = jnp.concatenate([kp_ref[:, sl], kc_ref[:, sl]], axis=0)
            vg = jnp.concatenate([vp_ref[:, sl], vc_ref[:, sl]], axis=0)
            dog = do_ref[:, sl]
            qs += [jnp.where(lo, qg, jnp.zeros_like(qg)), jnp.where(lo, jnp.zeros_like(qg), qg)]
            dos += [jnp.where(lo, dog, 0.0).astype(BF16), jnp.where(lo, 0.0, dog).astype(BF16)]
            ks += [kg, kg]
            vs += [vg, vg]
            lses += _head_cols(lse_ref[:, sl], lo, True)
            dcols += _head_cols(dd_ref[:, sl], lo, False)
        kb = jnp.stack(ks)
        s = _hnt(jnp.stack(qs), kb) + _head_bias(steps, d)
        p = jnp.where(valid, jnp.exp(jnp.where(valid, s, NEG_BIG) - jnp.stack(lses)), 0.0)
        dp = _hnt(jnp.stack(dos), jnp.stack(vs))
        ds = p * (dp - jnp.stack(dcols))
        dq = _hnn(ds.astype(BF16), kb) * 0.125
        for G in range(4):
            dq_ref[:, G * 128:(G + 1) * 128] = jnp.where(lo, dq[2 * G], dq[2 * G + 1])

    cur = pl.BlockSpec((B, ATTN_WIDTH), lambda r, n: (n, r))
    prev = pl.BlockSpec((B, ATTN_WIDTH), lambda r, n: (jnp.maximum(n - 1, 0), r))
    return pl.pallas_call(
        body, name=name, grid=(d, nb), in_specs=[cur, prev, cur, prev, cur, cur, cur, cur], out_specs=cur,
        out_shape=jax.ShapeDtypeStruct((L, d * ATTN_WIDTH), F32), compiler_params=_params(2),
    )(q, k, k, v, v, do, lse, dd)


def _attn_bwd_kv(q, k, v, do, lse, dd, d, name):
    L = q.shape[0]
    nb = L // ATTN_BLOCK
    B = ATTN_BLOCK

    def body(k_ref, v_ref, qc_ref, qn_ref, doc_ref, don_ref, lsec_ref, lsen_ref, ddc_ref, ddn_ref, dk_ref, dv_ref):
        j = pl.program_id(1)
        qrow = lax.broadcasted_iota(jnp.int32, (2 * B, B), 0)
        kk = lax.broadcasted_iota(jnp.int32, (2 * B, B), 1)
        steps = qrow - kk
        valid = (steps >= 0) & (steps <= B) & ((qrow < B) | (j < nb - 1))
        lo2 = lax.broadcasted_iota(jnp.int32, (2 * B, 128), 1) < 64
        lo = lax.broadcasted_iota(jnp.int32, (B, 128), 1) < 64
        stepsf = steps.astype(F32)
        for G in range(4):
            sl = slice(G * 128, (G + 1) * 128)
            kg = k_ref[:, sl]
            vg = v_ref[:, sl]
            qq = jnp.concatenate([qc_ref[:, sl], qn_ref[:, sl]], axis=0)
            doo = jnp.concatenate([doc_ref[:, sl], don_ref[:, sl]], axis=0)
            lse2 = jnp.concatenate([lsec_ref[:, sl], lsen_ref[:, sl]], axis=0)
            dd2 = jnp.concatenate([ddc_ref[:, sl], ddn_ref[:, sl]], axis=0)
            doo_b = doo.astype(BF16)
            dks, dvs = [], []
            for half in (0, 1):
                msk = lo2 if half == 0 else jnp.logical_not(lo2)
                qm = jnp.where(msk, qq, jnp.zeros_like(qq))
                s = _nt(qm, kg) - (_slope(2 * G + half) * d) * stepsf
                lse_c = _head_col(lse2, msk, True)
                p = jnp.where(valid, jnp.exp(jnp.where(valid, s, NEG_BIG) - lse_c), 0.0)
                dvs.append(_tn(p.astype(BF16), doo_b))
                dom = jnp.where(msk, doo, 0.0).astype(BF16)
                dp = _nt(dom, vg)
                dcol = _head_col(dd2, msk, False)
                ds = p * (dp - dcol)
                dks.append(_tn(ds.astype(BF16), qq))
            dk_ref[:, sl] = jnp.where(lo, dks[0], dks[1])
            dv_ref[:, sl] = jnp.where(lo, dvs[0], dvs[1])

    cur = pl.BlockSpec((B, ATTN_WIDTH), lambda r, j: (j, r))
    nxt = pl.BlockSpec((B, ATTN_WIDTH), lambda r, j: (jnp.minimum(j + 1, nb - 1), r))
    return pl.pallas_call(
        body, name=name, grid=(d, nb), in_specs=[cur, cur, cur, nxt, cur, nxt, cur, nxt, cur, nxt], out_specs=[cur, cur],
        out_shape=[jax.ShapeDtypeStruct((L, d * ATTN_WIDTH), F32)] * 2, compiler_params=_params(2),
    )(k, v, q, q, do, do, lse, lse, dd, dd)


CONV_T = 512
HALO = 8


def _conv_taps(pad_ref, w, T):
    acc = pad_ref[pl.ds(HALO - 3, T), :] * w[0:1, :]
    for j in range(1, CONV_WIDTH):
        acc = acc + pad_ref[pl.ds(HALO - 3 + j, T), :] * w[j:j + 1, :]
    return acc


def _conv_fwd(xq, xk, xv, conv_w):
    S = xq.shape[0]
    T = CONV_T

    def body(xq_ref, xqh_ref, xk_ref, xkh_ref, xv_ref, xvh_ref, wq_ref, wk_ref, wv_ref,
             qn_ref, kn_ref, v_ref, pad_ref):
        i = pl.program_id(0)

        def act(x_ref, xh_ref, w_ref):
            pad_ref[pl.ds(0, HALO), :] = jnp.where(i > 0, xh_ref[...], 0.0)
            pad_ref[pl.ds(HALO, T), :] = x_ref[...]
            c = _conv_taps(pad_ref, w_ref[...], T)
            return c * _sigmoid(c)

        def l2n(t):
            return t * lax.rsqrt(jnp.sum(t * t, axis=-1, keepdims=True) + L2_EPS)

        qn_ref[...] = l2n(act(xq_ref, xqh_ref, wq_ref))
        kn_ref[...] = l2n(act(xk_ref, xkh_ref, wk_ref))
        v_ref[...] = act(xv_ref, xvh_ref, wv_ref)

    tile = pl.BlockSpec((T, 128), lambda i, h: (i, h))
    halo = pl.BlockSpec((HALO, 128), lambda i, h: (jnp.maximum(i * (T // HALO) - 1, 0), h))
    wspec = lambda sec: pl.BlockSpec((CONV_WIDTH, 128), lambda i, h, sec=sec: (0, 4 * sec + h))
    return pl.pallas_call(
        body, name="dn_conv_fwd", grid=(S // T, DN_HEADS),
        in_specs=[tile, halo, tile, halo, tile, halo, wspec(0), wspec(1), wspec(2)],
        out_specs=[tile, tile, tile],
        out_shape=[jax.ShapeDtypeStruct((S, DN_WIDTH), F32)] * 3,
        scratch_shapes=[pltpu.VMEM((T + HALO, 128), F32)],
        compiler_params=_params(2),
    )(xq, xq, xk, xk, xv, xv, conv_w, conv_w, conv_w)


def _conv_bwd_pre(xq, xk, xv, conv_w, dqn, dkn, dv):
    S = xq.shape[0]
    T = CONV_T

    def body(xq_ref, xqh_ref, xk_ref, xkh_ref, xv_ref, xvh_ref, wq_ref, wk_ref, wv_ref,
             dqn_ref, dkn_ref, dv_ref, dcq_ref, dck_ref, dcv_ref, dwq_ref, dwk_ref, dwv_ref, pad_ref):
        i = pl.program_id(1)

        def one(x_ref, xh_ref, w_ref, dy_ref, dc_ref, dw_ref, normed):
            pad_ref[pl.ds(0, HALO), :] = jnp.where(i > 0, xh_ref[...], 0.0)
            pad_ref[pl.ds(HALO, T), :] = x_ref[...]
            c = _conv_taps(pad_ref, w_ref[...], T)
            sg = _sigmoid(c)
            a = c * sg
            dy = dy_ref[...]
            if normed:
                r = lax.rsqrt(jnp.sum(a * a, axis=-1, keepdims=True) + L2_EPS)
                y = a * r
                da = r * (dy - y * jnp.sum(dy * y, axis=-1, keepdims=True))
            else:
                da = dy
            dc = da * (sg * (1.0 + c * (1.0 - sg)))
            dc_ref[...] = dc

            @pl.when(i == 0)
            def _():
                dw_ref[...] = jnp.zeros_like(dw_ref)

            rows = [jnp.sum(dc * pad_ref[pl.ds(HALO - 3 + j, T), :], axis=0, keepdims=True) for j in range(CONV_WIDTH)]
            dw_ref[...] += jnp.concatenate(rows + [jnp.zeros((8 - CONV_WIDTH, 128), F32)], axis=0)

        one(xq_ref, xqh_ref, wq_ref, dqn_ref, dcq_ref, dwq_ref, True)
        one(xk_ref, xkh_ref, wk_ref, dkn_ref, dck_ref, dwk_ref, True)
        one(xv_ref, xvh_ref, wv_ref, dv_ref, dcv_ref, dwv_ref, False)

    tile = pl.BlockSpec((T, 128), lambda h, i: (i, h))
    halo = pl.BlockSpec((HALO, 128), lambda h, i: (jnp.maximum(i * (T // HALO) - 1, 0), h))
    wspec = lambda sec: pl.BlockSpec((CONV_WIDTH, 128), lambda h, i, sec=sec: (0, 4 * sec + h))
    dwspec = pl.BlockSpec((8, 128), lambda h, i: (0, h))
    return pl.pallas_call(
        body, name="dn_conv_bwd_pre", grid=(DN_HEADS, S // T),
        in_specs=[tile, halo, tile, halo, tile, halo, wspec(0), wspec(1), wspec(2), tile, tile, tile],
        out_specs=[tile, tile, tile, dwspec, dwspec, dwspec],
        out_shape=[jax.ShapeDtypeStruct((S, DN_WIDTH), F32)] * 3 + [jax.ShapeDtypeStruct((8, DN_WIDTH), F32)] * 3,
        scratch_shapes=[pltpu.VMEM((T + HALO, 128), F32)],
        compiler_params=_params(2),
    )(xq, xq, xk, xk, xv, xv, conv_w, conv_w, conv_w, dqn, dkn, dv)


def _conv_bwd_x(dcq, dck, dcv, conv_w):
    S = dcq.shape[0]
    T = CONV_T
    nt = S // T

    def body(dq_ref, dqh_ref, dk_ref, dkh_ref, dv_ref, dvh_ref, wq_ref, wk_ref, wv_ref,
             oq_ref, ok_ref, ov_ref, pad_ref):
        i = pl.program_id(0)

        def one(d_ref, dh_ref, w_ref, o_ref):
            pad_ref[pl.ds(0, T), :] = d_ref[...]
            pad_ref[pl.ds(T, HALO), :] = jnp.where(i < nt - 1, dh_ref[...], 0.0)
            w = w_ref[...]
            acc = pad_ref[pl.ds(3, T), :] * w[0:1, :]
            for j in range(1, CONV_WIDTH):
                acc = acc + pad_ref[pl.ds(3 - j, T), :] * w[j:j + 1, :]
            o_ref[...] = acc

        one(dq_ref, dqh_ref, wq_ref, oq_ref)
        one(dk_ref, dkh_ref, wk_ref, ok_ref)
        one(dv_ref, dvh_ref, wv_ref, ov_ref)

    tile = pl.BlockSpec((T, 128), lambda i, h: (i, h))
    halo = pl.BlockSpec((HALO, 128), lambda i, h: (jnp.minimum((i + 1) * (T // HALO), S // HALO - 1), h))
    wspec = lambda sec: pl.BlockSpec((CONV_WIDTH, 128), lambda i, h, sec=sec: (0, 4 * sec + h))
    return pl.pallas_call(
        body, name="dn_conv_bwd_x", grid=(nt, DN_HEADS),
        in_specs=[tile, halo, tile, halo, tile, halo, wspec(0), wspec(1), wspec(2)],
        out_specs=[tile, tile, tile],
        out_shape=[jax.ShapeDtypeStruct((S, DN_WIDTH), F32)] * 3,
        scratch_shapes=[pltpu.VMEM((T + HALO, 128), F32)],
        compiler_params=_params(2),
    )(dcq, dcq, dck, dck, dcv, dcv, conv_w, conv_w, conv_w)


PREP_CHUNKS = 4
SCAN_CHUNKS = 8


def _bnn(a, b):
    return lax.dot_general(a, b, (((2,), (1,)), ((0,), (0,))), preferred_element_type=F32, precision=HI)


def _bnt(a, b):
    return lax.dot_general(a, b, (((2,), (2,)), ((0,), (0,))), preferred_element_type=F32, precision=HI)


def _btn(a, b):
    return lax.dot_general(a, b, (((1,), (1,)), ((0,), (0,))), preferred_element_type=F32, precision=HI)


def _tri_inverse_b(a, blk, eye):
    dg = jnp.where(blk, a, 0.0)
    lo = a - dg
    d2 = _bnn(dg, dg)
    d4 = _bnn(d2, d2)
    d8 = _bnn(d4, d4)
    td = _bnn(_bnn(_bnn(eye - dg, eye + d2), eye + d4), eye + d8)
    b = _bnn(td, lo)
    b2 = _bnn(b, b)
    return _bnn(_bnn(eye - b, eye + b2), td)


def _dn_common_b(bds, avec, dvec, q_raw, k, v, t=None):
    C = DN_CHUNK
    lane = lax.broadcasted_iota(jnp.int32, (C, 128), 1)
    row = lax.broadcasted_iota(jnp.int32, (1, C, C), 1)
    col = lax.broadcasted_iota(jnp.int32, (1, C, C), 2)
    incl = row >= col
    strict = row > col
    eye = (row == col).astype(F32)
    blk = (row // 16) == (col // 16)
    pick = lambda tile, ln: jnp.sum(jnp.where(lane == ln, tile, 0.0), axis=-1, keepdims=True)
    betas, graws, zcs = [], [], []
    for bd in bds:
        z = bd + dvec
        g_all = -jnp.exp(avec) * (jnp.maximum(z, 0.0) + jnp.log(1.0 + jnp.exp(-jnp.abs(z))))
        beta_all = _sigmoid(bd)
        for h in range(DN_HEADS):
            betas.append(pick(beta_all, h))
            graws.append(pick(g_all, DN_HEADS + h))
            zcs.append(pick(z, DN_HEADS + h))
    beta, graw, zc = jnp.stack(betas), jnp.stack(graws), jnp.stack(zcs)
    to_row = lambda c: jnp.sum(eye * c, axis=1, keepdims=True)
    gc = jnp.sum(jnp.where(incl, to_row(graw), 0.0), axis=-1, keepdims=True)
    decay = jnp.exp(jnp.where(incl, gc - to_row(gc), NEG_BIG))
    q = q_raw * (DN_HEAD_DIM ** -0.5)
    kb = k * beta
    kk = _bnt(kb, k)
    if t is None:
        t = _tri_inverse_b(jnp.where(strict, kk * decay, 0.0), blk, eye)
    eg = jnp.exp(gc)
    rhs_w = kb * eg
    u = _bnn(t, v * beta)
    w = _bnn(t, rhs_w)
    qk = _bnt(q, k)
    aq = jnp.where(incl, qk * decay, 0.0)
    last = lax.broadcasted_iota(jnp.int32, (1, C, 1), 1) == C - 1
    g_last = jnp.sum(jnp.where(last, gc, 0.0), axis=1, keepdims=True)
    ekd = jnp.exp(g_last - gc)
    return dict(beta=beta, graw=graw, zc=zc, gc=gc, decay=decay, q=q, kb=kb, kk=kk, t=t, eg=eg, rhs_w=rhs_w,
                u=u, w=w, qk=qk, aq=aq, g_last=g_last, ekd=ekd, kd=k * ekd, qg=q * eg,
                incl=incl, strict=strict, eye=eye, lane=lane, row=row, col=col, last=last)


def _stack_heads(ref, rows):
    return jnp.stack([ref[rows, h * DN_HEAD_DIM:(h + 1) * DN_HEAD_DIM] for h in range(DN_HEADS)])


def _stack_units(ref, nc):
    C = DN_CHUNK
    return jnp.concatenate([_stack_heads(ref, slice(ci * C, (ci + 1) * C)) for ci in range(nc)], axis=0)


def _store_units(ref, val, nc):
    C = DN_CHUNK
    for ci in range(nc):
        for h in range(DN_HEADS):
            ref[ci * C:(ci + 1) * C, h * DN_HEAD_DIM:(h + 1) * DN_HEAD_DIM] = val[ci * DN_HEADS + h]


def _dn_prep(qn, kn, v, bd, avec, dvec):
    S = qn.shape[0]
    C = DN_CHUNK
    N = S // C
    nc = PREP_CHUNKS

    def body(q_ref, k_ref, v_ref, bd_ref, a_ref, d_ref, u_ref, w_ref, qg_ref, kd_ref, aq_ref, t_ref, egl_ref):
        bds = [bd_ref[ci * C:(ci + 1) * C, :] for ci in range(nc)]
        c = _dn_common_b(bds, a_ref[...], d_ref[...], _stack_units(q_ref, nc), _stack_units(k_ref, nc), _stack_units(v_ref, nc))
        _store_units(u_ref, c["u"], nc)
        _store_units(w_ref, c["w"], nc)
        _store_units(qg_ref, c["qg"], nc)
        _store_units(kd_ref, c["kd"], nc)
        egl = jnp.broadcast_to(jnp.exp(c["g_last"]), (nc * DN_HEADS, 1, 128))
        for ci in range(nc):
            for h in range(DN_HEADS):
                aq_ref[h, ci * C:(ci + 1) * C, :] = c["aq"][ci * DN_HEADS + h]
                t_ref[h, ci * C:(ci + 1) * C, :] = c["t"][ci * DN_HEADS + h]
            egl_ref[ci * 8:(ci + 1) * 8, :] = jnp.concatenate(
                [egl[ci * DN_HEADS + h] for h in range(DN_HEADS)] + [jnp.zeros((8 - DN_HEADS, 128), F32)], axis=0)

    tok = lambda w: pl.BlockSpec((nc * C, w), lambda n: (n, 0))
    sq = pl.BlockSpec((DN_HEADS, nc * C, C), lambda n: (0, n, 0))
    vec = pl.BlockSpec((1, 128), lambda n: (0, 0))
    return pl.pallas_call(
        body, name="dn_prep", grid=(N // nc,),
        in_specs=[tok(DN_WIDTH)] * 3 + [tok(128), vec, vec],
        out_specs=[tok(DN_WIDTH)] * 4 + [sq, sq, pl.BlockSpec((nc * 8, 128), lambda n: (n, 0))],
        out_shape=[jax.ShapeDtypeStruct((S, DN_WIDTH), F32)] * 4 + [jax.ShapeDtypeStruct((DN_HEADS, S, C), F32)] * 2
                  + [jax.ShapeDtypeStruct((N * 8, 128), F32)],
        compiler_params=_params(1),
    )(qn, kn, v, bd, avec, dvec)


def _dn_scan_fwd(u, w, qg, kd, aq, egl, gate, dn_gain):
    S = u.shape[0]
    C = DN_CHUNK
    N = S // C
    HD = DN_HEAD_DIM
    nc = SCAN_CHUNKS

    def body(u_ref, w_ref, qg_ref, kd_ref, aq_ref, egl_ref, gate_ref, gain_ref, dn_ref, o_ref, vn_ref, st_ref, state_ref):
        @pl.when(pl.program_id(0) == 0)
        def _():
            state_ref[...] = jnp.zeros_like(state_ref)

        gain = gain_ref[...]
        for ci in range(nc):
            rows = slice(ci * C, (ci + 1) * C)
            st = state_ref[...]
            for h in range(DN_HEADS):
                st_ref[ci * DN_WIDTH + h * HD:ci * DN_WIDTH + (h + 1) * HD, :] = st[h]
            v_new = _stack_heads(u_ref, rows) - _bnn(_stack_heads(w_ref, rows), st)
            o = _bnn(_stack_heads(qg_ref, rows), st) + _bnn(aq_ref[:, rows, :], v_new)
            egl = jnp.stack([egl_ref[ci * 8 + h:ci * 8 + h + 1, :] for h in range(DN_HEADS)])
            state_ref[...] = st * egl + _btn(_stack_heads(kd_ref, rows), v_new)
            r = lax.rsqrt(jnp.mean(o * o, axis=-1, keepdims=True) + NORM_EPS)
            gt = _stack_heads(gate_ref, rows)
            dn = o * r * gain * (gt * _sigmoid(gt))
            for h in range(DN_HEADS):
                sl = slice(h * HD, (h + 1) * HD)
                vn_ref[rows, sl] = v_new[h]
                o_ref[rows, sl] = o[h]
                dn_ref[rows, sl] = dn[h]

    tok = lambda wd: pl.BlockSpec((nc * C, wd), lambda n: (n, 0))
    sq = pl.BlockSpec((DN_HEADS, nc * C, C), lambda n: (0, n, 0))
    vec = pl.BlockSpec((1, 128), lambda n: (0, 0))
    return pl.pallas_call(
        body, name="dn_scan_fwd", grid=(N // nc,),
        in_specs=[tok(DN_WIDTH)] * 4 + [sq, pl.BlockSpec((nc * 8, 128), lambda n: (n, 0)), tok(DN_WIDTH), vec],
        out_specs=[tok(DN_WIDTH)] * 3 + [pl.BlockSpec((nc * DN_WIDTH, HD), lambda n: (n, 0))],
        out_shape=[jax.ShapeDtypeStruct((S, DN_WIDTH), F32)] * 3 + [jax.ShapeDtypeStruct((N * DN_WIDTH, HD), F32)],
        scratch_shapes=[pltpu.VMEM((DN_HEADS, HD, HD), F32)],
        compiler_params=_params(1),
    )(u, w, qg, kd, aq, egl, gate, dn_gain)


def _dn_scan_bwd(w, qg, kd, aq, egl, gate, dn_gain, o, ddn):
    S = w.shape[0]
    C = DN_CHUNK
    N = S // C
    HD = DN_HEAD_DIM
    nc = SCAN_CHUNKS

    def body(w_ref, qg_ref, kd_ref, aq_ref, egl_ref, gate_ref, gain_ref, o_ref, ddn_ref,
             do_ref, dvn_ref, dgate_ref, dst_ref, small_ref, dstate_ref):
        @pl.when(pl.program_id(0) == 0)
        def _():
            dstate_ref[...] = jnp.zeros_like(dstate_ref)
            small_ref[...] = jnp.zeros_like(small_ref)

        gain = gain_ref[...]
        d_gain = jnp.zeros((1, 128), F32)
        for ci in reversed(range(nc)):
            rows = slice(ci * C, (ci + 1) * C)
            dsn = dstate_ref[...]
            for h in range(DN_HEADS):
                dst_ref[ci * DN_WIDTH + h * HD:ci * DN_WIDTH + (h + 1) * HD, :] = dsn[h]
            ov = _stack_heads(o_ref, rows)
            r = lax.rsqrt(jnp.mean(ov * ov, axis=-1, keepdims=True) + NORM_EPS)
            on = ov * r
            gt = _stack_heads(gate_ref, rows)
            sgt = _sigmoid(gt)
            silu_g = gt * sgt
            dy = _stack_heads(ddn_ref, rows)
            d_gain = d_gain + jnp.sum(jnp.sum(dy * on * silu_g, axis=1, keepdims=True), axis=0)
            dgate = dy * on * gain * (sgt * (1.0 + gt * (1.0 - sgt)))
            don = dy * gain * silu_g
            do = r * (don - on * jnp.mean(don * on, axis=-1, keepdims=True))
            d_vnew = _btn(aq_ref[:, rows, :], do) + _bnn(_stack_heads(kd_ref, rows), dsn)
            egl = jnp.stack([egl_ref[ci * 8 + h:ci * 8 + h + 1, :] for h in range(DN_HEADS)])
            dstate_ref[...] = _btn(_stack_heads(qg_ref, rows), do) + dsn * egl - _btn(_stack_heads(w_ref, rows), d_vnew)
            for h in range(DN_HEADS):
                sl = slice(h * HD, (h + 1) * HD)
                do_ref[rows, sl] = do[h]
                dvn_ref[rows, sl] = d_vnew[h]
                dgate_ref[rows, sl] = dgate[h]
        small_ref[...] += jnp.concatenate([d_gain, jnp.zeros((7, 128), F32)], axis=0)

    nb = N // nc
    tok = lambda wd: pl.BlockSpec((nc * C, wd), lambda i: (nb - 1 - i, 0))
    sq = pl.BlockSpec((DN_HEADS, nc * C, C), lambda i: (0, nb - 1 - i, 0))
    vec = pl.BlockSpec((1, 128), lambda i: (0, 0))
    return pl.pallas_call(
        body, name="dn_scan_bwd", grid=(nb,),
        in_specs=[tok(DN_WIDTH)] * 3 + [sq, pl.BlockSpec((nc * 8, 128), lambda i: (nb - 1 - i, 0)), tok(DN_WIDTH), vec,
                                       tok(DN_WIDTH), tok(DN_WIDTH)],
        out_specs=[tok(DN_WIDTH)] * 3 + [pl.BlockSpec((nc * DN_WIDTH, HD), lambda i: (nb - 1 - i, 0)),
                                        pl.BlockSpec((8, 128), lambda i: (0, 0))],
        out_shape=[jax.ShapeDtypeStruct((S, DN_WIDTH), F32)] * 3 + [jax.ShapeDtypeStruct((N * DN_WIDTH, HD), F32),
                                                                  jax.ShapeDtypeStruct((8, 128), F32)],
        scratch_shapes=[pltpu.VMEM((DN_HEADS, HD, HD), F32)],
        compiler_params=_params(1),
    )(w, qg, kd, aq, egl, gate, dn_gain, o, ddn)


def _dn_post(qn, kn, v, bd, avec, dvec, t_inv, v_new_all, states, dstates, do_all, dvn_all, comm=None):
    S = qn.shape[0]
    C = DN_CHUNK
    N = S // C
    HD = DN_HEAD_DIM
    nc = PREP_CHUNKS
    B = nc * DN_HEADS

    def body(q_ref, k_ref, v_ref, bd_ref, a_ref, d_ref, t_ref, vn_ref, st_ref, dst_ref, do_ref, dvn_ref,
             dq_ref, dk_ref, dv_ref, dbd_ref, small_ref):
        @pl.when(pl.program_id(0) == 0)
        def _():
            small_ref[...] = jnp.zeros_like(small_ref)

        avec = a_ref[...]
        bds = [bd_ref[ci * C:(ci + 1) * C, :] for ci in range(nc)]
        k = _stack_units(k_ref, nc)
        vv = _stack_units(v_ref, nc)
        t = jnp.concatenate([t_ref[:, ci * C:(ci + 1) * C, :] for ci in range(nc)], axis=0)
        c = _dn_common_b(bds, avec, d_ref[...], _stack_units(q_ref, nc), k, vv, t=t)
        q, kb, eg, u, w = c["q"], c["kb"], c["eg"], c["u"], c["w"]
        beta, decay, incl, strict, eye = c["beta"], c["decay"], c["incl"], c["strict"], c["eye"]
        st = jnp.stack([st_ref[b * HD:(b + 1) * HD, :] for b in range(B)])
        dsn = jnp.stack([dst_ref[b * HD:(b + 1) * HD, :] for b in range(B)])
        v_new = _stack_units(vn_ref, nc)
        do = _stack_units(do_ref, nc)
        d_vnew = _stack_units(dvn_ref, nc)
        egl = jnp.exp(c["g_last"])
        daq = jnp.where(incl, _bnt(do, v_new), 0.0)
        d_qg = _bnt(do, st)
        d_kd = _bnt(v_new, dsn)
        d_glast = jnp.sum(jnp.sum(dsn * st, axis=-1, keepdims=True), axis=1, keepdims=True) * egl
        d_w = -_bnt(d_vnew, st)
        d_ru = _btn(t, d_vnew)
        d_rw = _btn(t, d_w)
        da = -jnp.where(strict, _bnt(d_ru, u) + _bnt(d_rw, w), 0.0)
        dv = d_ru * beta
        dbeta = jnp.sum(d_ru * vv, axis=-1, keepdims=True)
        dkb = d_rw * eg
        dgc = jnp.sum(d_rw * c["rhs_w"], axis=-1, keepdims=True)
        dkk = da * decay
        ddecay = da * c["kk"]
        dkb = dkb + _bnn(dkk, k)
        dk = _btn(dkk, kb)
        dqk = daq * decay
        ddecay = ddecay + daq * c["qk"]
        dq = _bnn(dqk, k)
        dk = dk + _btn(dqk, q)
        m = ddecay * decay
        col_sum = jnp.sum(m, axis=1, keepdims=True)
        dgc = dgc + jnp.sum(m, axis=-1, keepdims=True) - jnp.sum(eye * col_sum, axis=-1, keepdims=True)
        dq = dq + d_qg * eg
        dgc = dgc + jnp.sum(d_qg * c["qg"], axis=-1, keepdims=True)
        dk = dk + d_kd * c["ekd"]
        tk = jnp.sum(d_kd * c["kd"], axis=-1, keepdims=True)
        dgc = dgc - tk
        d_glast = d_glast + jnp.sum(tk, axis=1, keepdims=True)
        dk = dk + dkb * beta
        dbeta = dbeta + jnp.sum(dkb * k, axis=-1, keepdims=True)
        dgc = dgc + jnp.where(c["last"], d_glast, 0.0)
        dgc_row = jnp.sum(eye * dgc, axis=1, keepdims=True)
        dgraw = jnp.sum(jnp.where(c["col"] >= c["row"], dgc_row, 0.0), axis=-1, keepdims=True)
        _store_units(dq_ref, dq * (HD ** -0.5), nc)
        _store_units(dk_ref, dk, nc)
        _store_units(dv_ref, dv, nc)
        dbraw = dbeta * beta * (1.0 - beta)
        dzc = dgraw * _sigmoid(c["zc"])
        ga = dgraw * c["graw"]
        lane = c["lane"]
        lane1 = lax.broadcasted_iota(jnp.int32, (1, 128), 1)
        neg_ea = -jnp.exp(avec)
        d_alog = jnp.zeros((1, 128), F32)
        d_dt = jnp.zeros((1, 128), F32)
        for ci in range(nc):
            dbd = jnp.zeros((C, 128), F32)
            for h in range(DN_HEADS):
                b = ci * DN_HEADS + h
                dz = dzc[b] * neg_ea
                dbd = dbd + jnp.where(lane == h, dbraw[b], 0.0) + jnp.where(lane == DN_HEADS + h, dz, 0.0)
                d_alog = d_alog + jnp.where(lane1 == DN_HEADS + h, jnp.sum(ga[b], axis=0, keepdims=True), 0.0)
                d_dt = d_dt + jnp.where(lane1 == DN_HEADS + h, jnp.sum(dz, axis=0, keepdims=True), 0.0)
            dbd_ref[ci * C:(ci + 1) * C, :] = dbd
        small_ref[...] += jnp.concatenate([d_alog, d_dt, jnp.zeros((6, 128), F32)], axis=0)

    tok = lambda wd: pl.BlockSpec((nc * C, wd), lambda n: (n, 0))
    big = pl.BlockSpec((nc * DN_WIDTH, HD), lambda n: (n, 0))
    sq = pl.BlockSpec((DN_HEADS, nc * C, C), lambda n: (0, n, 0))
    vec = pl.BlockSpec((1, 128), lambda n: (0, 0))
    return _call(
        body, (qn, kn, v, bd, avec, dvec, t_inv, v_new_all, states, dstates, do_all, dvn_all),
        name="dn_post", grid=(N // nc,), comm=comm,
        in_specs=[tok(DN_WIDTH)] * 3 + [tok(128), vec, vec, sq, tok(DN_WIDTH), big, big, tok(DN_WIDTH), tok(DN_WIDTH)],
        out_specs=[tok(DN_WIDTH)] * 3 + [tok(128), pl.BlockSpec((8, 128), lambda n: (0, 0))],
        out_shape=[jax.ShapeDtypeStruct((S, DN_WIDTH), F32)] * 3 + [jax.ShapeDtypeStruct((S, 128), F32),
                                                                  jax.ShapeDtypeStruct((8, 128), F32)])


def _outproj_fwd(x, attn, dn, w_out):
    S, D = x.shape
    tm = 512

    def body(x_ref, a_ref, d_ref, w_ref, xo_ref, mix_ref):
        a = a_ref[...].astype(BF16)
        dd = d_ref[...].astype(BF16)
        mix_ref[:, 0:ATTN_WIDTH] = a
        mix_ref[:, ATTN_WIDTH:] = dd
        xo_ref[...] = x_ref[...] + _nn(a, w_ref[0:ATTN_WIDTH, :]) + _nn(dd, w_ref[ATTN_WIDTH:, :])

    tok = lambda w: pl.BlockSpec((tm, w), lambda i: (i, 0))
    return pl.pallas_call(
        body, name="outproj_fwd", grid=(S // tm,),
        in_specs=[tok(D), tok(ATTN_WIDTH), tok(DN_WIDTH), pl.BlockSpec((D, D), lambda i: (0, 0))],
        out_specs=[tok(D), tok(D)],
        out_shape=[jax.ShapeDtypeStruct((S, D), F32), jax.ShapeDtypeStruct((S, D), BF16)],
        compiler_params=_params(1),
    )(x, attn, dn, w_out)


def _outproj_bwd(dx, w_out, attn):
    S, D = dx.shape
    tm = VIEW_TILE

    def body(dx_ref, w_ref, attn_ref, da1, da4, da16, dl1, dl4, dl16, ddn_ref, dxb_ref, planes):
        d = dx_ref[...].astype(BF16)
        dxb_ref[...] = d
        da = _nt(d, w_ref[0:ATTN_WIDTH, :])
        ddn_ref[...] = _nt(d, w_ref[ATTN_WIDTH:, :])
        _tile_to_views(da, planes, (da1, da4, da16))
        lo = lax.broadcasted_iota(jnp.int32, (tm, 128), 1) < 64
        cols = []
        for G in range(4):
            sl = slice(G * 128, (G + 1) * 128)
            t = da[:, sl] * attn_ref[:, sl]
            d0 = jnp.sum(jnp.where(lo, t, 0.0), axis=-1, keepdims=True)
            d1 = jnp.sum(jnp.where(lo, 0.0, t), axis=-1, keepdims=True)
            cols.append(jnp.where(lo, d0, d1))
        _tile_to_views(jnp.concatenate(cols, axis=1), planes, (dl1, dl4, dl16))

    tok = lambda w: pl.BlockSpec((tm, w), lambda i: (i, 0))
    views = [_view_spec(d) for d in DILATIONS]
    return pl.pallas_call(
        body, name="outproj_bwd", grid=(S // tm,),
        in_specs=[tok(D), pl.BlockSpec((D, D), lambda i: (0, 0)), tok(ATTN_WIDTH)],
        out_specs=views + views + [tok(DN_WIDTH), tok(D)],
        out_shape=[_view_shape(S, d, F32) for d in DILATIONS] * 2
                  + [jax.ShapeDtypeStruct((S, DN_WIDTH), F32), jax.ShapeDtypeStruct((S, D), BF16)],
        scratch_shapes=[pltpu.VMEM((4, tm, 128), F32)],
        compiler_params=_params(1),
    )(dx, w_out, attn)


def _loss_head(x, gain, target):
    S, D = x.shape
    tm = 512

    def body(x_ref, gain_ref, t_ref, loss_ref, dx_ref, dgain_ref):
        @pl.when(pl.program_id(0) == 0)
        def _():
            loss_ref[...] = jnp.zeros_like(loss_ref)
            dgain_ref[...] = jnp.zeros_like(dgain_ref)

        xf = x_ref[...]
        gain = gain_ref[...]
        r = lax.rsqrt(jnp.mean(xf * xf, axis=-1, keepdims=True) + NORM_EPS)
        xhat = xf * r
        err = xhat * gain - t_ref[...]
        part = 0.5 * jnp.sum(jnp.mean(err * err, axis=-1, keepdims=True), axis=0, keepdims=True)
        first = (lax.broadcasted_iota(jnp.int32, (8, 128), 0) == 0) & (lax.broadcasted_iota(jnp.int32, (8, 128), 1) == 0)
        loss_ref[...] += jnp.where(first, part, 0.0)
        dy = err * (1.0 / D)
        dgain_ref[...] += jnp.sum(dy * xhat, axis=0, keepdims=True)
        dxh = dy * gain
        dx_ref[...] = r * (dxh - xhat * jnp.mean(dxh * xhat, axis=-1, keepdims=True))

    tok = pl.BlockSpec((tm, D), lambda i: (i, 0))
    row = pl.BlockSpec((1, D), lambda i: (0, 0))
    return pl.pallas_call(
        body, name="loss_head", grid=(S // tm,),
        in_specs=[tok, row, tok],
        out_specs=[pl.BlockSpec((8, 128), lambda i: (0, 0)), tok, row],
        out_shape=[jax.ShapeDtypeStruct((8, 128), F32), jax.ShapeDtypeStruct((S, D), F32),
                   jax.ShapeDtypeStruct((1, D), F32)],
        compiler_params=_params(1),
    )(x, gain, target)


def _adamw(w, g, m, v, name):
    R, Ccols = w.shape
    tr = R
    for cand in (256, 128, 64, 32, 16, 8):
        if R % cand == 0:
            tr = cand
            break
    c1 = 1.0 - ADAM_B1 ** ADAM_STEP
    c2 = 1.0 - ADAM_B2 ** ADAM_STEP

    def body(w_ref, g_ref, m_ref, v_ref, d_ref, nm_ref, nv_ref):
        gv = g_ref[...]
        mn = ADAM_B1 * m_ref[...] + (1.0 - ADAM_B1) * gv
        vn = ADAM_B2 * v_ref[...] + (1.0 - ADAM_B2) * (gv * gv)
        nm_ref[...] = mn
        nv_ref[...] = vn
        d_ref[...] = -ADAM_LR * ((mn / c1) / (jnp.sqrt(vn / c2) + ADAM_EPS) + ADAM_WD * w_ref[...])

    spec = pl.BlockSpec((tr, Ccols), lambda i: (i, 0))
    return pl.pallas_call(
        body, name=name, grid=(R // tr,), in_specs=[spec] * 4, out_specs=[spec] * 3,
        out_shape=[jax.ShapeDtypeStruct((R, Ccols), F32)] * 3, compiler_params=_params(1),
    )(w, g, m, v)


LATE_WEIGHTS = ("w_out", "ffn2_gate", "ffn2_up", "ffn2_down")


def _local_step(x, target, wts, small, dist=None):
    g1, g2, gm, gf = small["norm_ffn1"], small["norm_ffn2"], small["norm_mix"], small["norm_final"]
    wts = dict(wts)

    def reduce_start(gs, tag):
        return _rs_add_pairs(gs, _swap_sibling(gs, True, "rs_swap_halves_" + tag), dist["c"], "rs_add_pairs_" + tag)

    (x1, h1, fg1, fu1), late = _ffn_fwd(x, g1, wts["ffn1_gate"], wts["ffn1_up"], wts["ffn1_down"], "ffn1_fwd",
                                        comm=_ag_comm(dist["late"]) if dist else None)
    if dist:
        wts.update(zip(LATE_WEIGHTS, late))
        wts["w_out"] = wts["w_out"].reshape(D_MODEL, D_MODEL)
    h2, *qkv, xq, xk, xv, gate, bd = _inproj_fwd(x1, gm, wts["w_in"])
    aq, ak, av = qkv[0:3], qkv[3:6], qkv[6:9]
    parts = [_attn_fwd(aq[p], ak[p], av[p], d, f"attn_fwd_d{d}") for p, d in enumerate(DILATIONS)]
    attn, *lse = _attn_merge(parts)
    conv_w = small["conv_w"]
    qn, kn, vv = _conv_fwd(xq, xk, xv, conv_w)
    dn_u, dn_w, dn_qg, dn_kd, dn_aq, dn_t, dn_egl = _dn_prep(qn, kn, vv, bd, small["avec"], small["dvec"])
    dn, o_dn, v_new, states = _dn_scan_fwd(dn_u, dn_w, dn_qg, dn_kd, dn_aq, dn_egl, gate, small["dn_norm"])
    x2, mix = _outproj_fwd(x1, attn, dn, wts["w_out"])
    (x3, h3, fg2, fu2), _ = _ffn_fwd(x2, g2, wts["ffn2_gate"], wts["ffn2_up"], wts["ffn2_down"], "ffn2_fwd")
    loss, dx3, d_gf = _loss_head(x3, gf, target)

    grads = {}
    (dx2, d_g2, dfg2, dfu2, act2, dout2), _ = _ffn_bwd(dx3, x2, g2, fg2, fu2, wts["ffn2_down"], wts["ffn2_gate"],
                                                      wts["ffn2_up"], "ffn2_bwd")
    tk = 2048
    grads["ffn2_gate"], _ = _dw_chunks(dfg2, h3, tk, "dw_ffn2_gate")
    grads["ffn2_up"], _ = _dw_chunks(dfu2, h3, tk, "dw_ffn2_up")
    grads["ffn2_down"], _ = _dw_chunks(act2, dout2, tk, "dw_ffn2_down")
    group_a = ("ffn2_gate", "ffn2_up", "ffn2_down")
    parts_a = reduce_start([grads[n] for n in group_a], "a") if dist else None

    *dviews, ddn, dx2b = _outproj_bwd(dx2, wts["w_out"], attn)
    dattn, dd = dviews[0:3], dviews[3:6]
    grads["w_out"] = _matmul_tn(mix, dx2b, D_MODEL, tk, "dw_out").reshape(N_CHIPS, D_MODEL // N_CHIPS, D_MODEL)

    daq, dak, dav = [], [], []
    for p, d in enumerate(DILATIONS):
        daq.append(_attn_bwd_q(aq[p], ak[p], av[p], dattn[p], lse[p], dd[p], d, f"attn_bwd_q_d{d}"))
        dk_p, dv_p = _attn_bwd_kv(aq[p], ak[p], av[p], dattn[p], lse[p], dd[p], d, f"attn_bwd_kv_d{d}")
        dak.append(dk_p)
        dav.append(dv_p)

    do_dn, dvn, dgate, dstates, d_dn_gain = _dn_scan_bwd(dn_w, dn_qg, dn_kd, dn_aq, dn_egl, gate, small["dn_norm"], o_dn, ddn)
    (dqn, dkn, dvv, dbd, dn_small), recv_a = _dn_post(qn, kn, vv, bd, small["avec"], small["dvec"], dn_t, v_new, states,
                                                      dstates, do_dn, dvn, comm=_rsx_comm(parts_a) if dist else None)
    dcq, dck, dcv, dwq, dwk, dwv = _conv_bwd_pre(xq, xk, xv, conv_w, dqn, dkn, dvv)
    dxq, dxk, dxv = _conv_bwd_x(dcq, dck, dcv, conv_w)
    d_conv = jnp.concatenate([dwq[:CONV_WIDTH], dwk[:CONV_WIDTH], dwv[:CONV_WIDTH]], axis=1)

    dx1, d_gm, dproj = _inproj_bwd(dx2, x1, gm, [daq, dak, dav], [dxq, dxk, dxv, dgate], dbd, wts["w_in"])
    gi = _matmul_tn(dproj, h2, IN_COLS_PADDED, 512, "dw_in")
    if dist:
        gi = jnp.concatenate([gi[:3072], gi[3584:3592], gi[3072:3584]], axis=0).reshape(N_CHIPS, IN_COLS // N_CHIPS, D_MODEL)
        gi = jnp.pad(gi, ((0, 0), (0, W_IN_ROWS - IN_COLS // N_CHIPS), (0, 0)))
    grads["w_in"] = gi
    group_b = ("w_in", "w_out")
    parts_b = reduce_start([grads[n] for n in group_b], "b") if dist else None

    (dx0, d_g1, dfg1, dfu1, act1, dout1), _ = _ffn_bwd(dx1, x, g1, fg1, fu1, wts["ffn1_down"], wts["ffn1_gate"],
                                                      wts["ffn1_up"], "ffn1_bwd")
    group_c = ("ffn1_gate", "ffn1_up", "ffn1_down")
    pending = parts_b if dist else []
    parts_c, recv_bc = [], []
    for n, (lhs, rhs) in zip(group_c, ((dfg1, h1), (dfu1, h1), (act1, dout1))):
        grads[n], landed = _dw_chunks(lhs, rhs, tk, "dw_" + n, comm=_rsx_comm(pending) if dist else None)
        recv_bc += list(landed)
        if dist:
            pending = reduce_start([grads[n]], n)
            parts_c += pending

    small_grads = dict(norm_ffn1=d_g1, norm_mix=d_gm, norm_ffn2=d_g2, norm_final=d_gf, conv_w=d_conv,
                       a_log=dn_small[0:1], dt_bias=dn_small[1:2], dn_norm=d_dn_gain[0:1])
    if dist:
        recv_bc += _rs_exchange_arrays(pending)
        recv_b, recv_c = recv_bc[:len(parts_b)], recv_bc[len(parts_b):]
        names = group_a + group_b + group_c
        totals = _rs_add_totals(list(parts_a) + list(parts_b) + list(parts_c), list(recv_a) + list(recv_b) + list(recv_c),
                                dist["chip"])
        theirs = _swap_sibling(totals, False, "rs_share_total")
        grads = {n: (mine, other) for n, mine, other in zip(names, totals, theirs)}
    return loss, dx0, grads, small_grads


PACK_SECTIONS = (("ffn1_gate", 704), ("ffn1_up", 704), ("ffn1_down", 704), ("w_in", 898), ("w_out", 256),
                 ("ffn2_gate", 704), ("ffn2_up", 704), ("ffn2_down", 704))
PACK_ROWS = 5408
HALF_ROWS = PACK_ROWS // 2
ADD_ROWS = 208

HBM = pl.BlockSpec(memory_space=pl.ANY)
VMEM_SPEC = pl.BlockSpec(memory_space=pltpu.VMEM)


def _coords():
    return lax.axis_index("x"), lax.axis_index("y"), lax.axis_index("c")


def _remote(src, dst, send_sems, recv_sems, k, dev):
    return pltpu.make_async_remote_copy(src_ref=src, dst_ref=dst, send_sem=send_sems.at[k], recv_sem=recv_sems.at[k],
                                        device_id=dev, device_id_type=MESH)


def _allreduce_small(buf, name):
    R, Cc = buf.shape

    def body(src_ref, out_ref, recv_ref, send_sems, recv_sems):
        x, y, c = _coords()
        copies = []
        for m in range(1, 8):
            fx, fy, fc = (m >> 2) & 1, (m >> 1) & 1, m & 1
            dev = (x ^ fx if fx else x, y ^ fy if fy else y, c ^ fc if fc else c)
            cp = _remote(src_ref, recv_ref.at[m - 1], send_sems, recv_sems, m - 1, dev)
            cp.start()
            copies.append(cp)
        for cp in copies:
            cp.wait()
        r = [src_ref[...]] + [recv_ref[m] for m in range(7)]
        out_ref[...] = ((r[0] + r[1]) + (r[2] + r[3])) + ((r[4] + r[5]) + (r[6] + r[7]))

    return pl.pallas_call(
        body, name=name, out_shape=jax.ShapeDtypeStruct((R, Cc), F32),
        in_specs=[VMEM_SPEC], out_specs=VMEM_SPEC,
        scratch_shapes=[pltpu.VMEM((7, R, Cc), F32), pltpu.SemaphoreType.DMA((7,)), pltpu.SemaphoreType.DMA((7,))],
    )(buf)


def _allgather_weights(pack2):
    _, Hh, Cc = pack2.shape

    def body(src_ref, out_ref, send_sems, recv_sems):
        x, y, c = _coords()
        sib = (x, y, 1 - c)
        others = [(1 - x, y), (x, 1 - y), (1 - x, 1 - y)]
        blk = lambda cx, cy, half: out_ref.at[2 * cx + cy, half]
        mine = _remote(src_ref, out_ref.at[2 * x + y], send_sems, recv_sems, 6, sib)
        mine.start()
        first = [_remote(src_ref.at[c], blk(x, y, c), send_sems, recv_sems, j, (ox, oy, c)) for j, (ox, oy) in enumerate(others)]
        for cp in first:
            cp.start()
        passed = [_remote(blk(ox, oy, c), blk(ox, oy, c), send_sems, recv_sems, 3 + j, sib) for j, (ox, oy) in enumerate(others)]
        for j, (ox, oy) in enumerate(others):
            _remote(src_ref.at[c], blk(ox, oy, c), send_sems, recv_sems, j, (ox, oy, c)).wait_recv()
            passed[j].start()
        for j, (ox, oy) in enumerate(others):
            _remote(src_ref.at[c], blk(ox, oy, 1 - c), send_sems, recv_sems, 3 + j, sib).wait_recv()
        for cp in first + passed:
            cp.wait_send()
        mine.wait()

    return pl.pallas_call(
        body, name="allgather_weights", out_shape=jax.ShapeDtypeStruct((N_CHIPS, 2, Hh, Cc), pack2.dtype),
        in_specs=[HBM], out_specs=HBM,
        scratch_shapes=[pltpu.SemaphoreType.DMA((7,)), pltpu.SemaphoreType.DMA((7,))],
    )(pack2)


def _rs_swap_halves(gpack):
    _, nj, Hh, Cc = gpack.shape

    def body(src_ref, out_ref, send_sems, recv_sems):
        x, y, c = _coords()
        cp = _remote(src_ref.at[1 - c], out_ref, send_sems, recv_sems, 0, (x, y, 1 - c))
        cp.start()
        cp.wait()

    return pl.pallas_call(
        body, name="rs_swap_halves", out_shape=jax.ShapeDtypeStruct((nj, Hh, Cc), gpack.dtype),
        in_specs=[HBM], out_specs=HBM,
        scratch_shapes=[pltpu.SemaphoreType.DMA((1,)), pltpu.SemaphoreType.DMA((1,))],
    )(gpack)


def _rs_add_pair(gpack, other, c):
    _, nj, Hh, Cc = gpack.shape
    tr = ADD_ROWS

    def body(c_ref, a_ref, b_ref, o_ref):
        o_ref[...] = (a_ref[...] + b_ref[...]).astype(BF16)

    return pl.pallas_call(
        body, name="rs_add_pair",
        grid_spec=pltpu.PrefetchScalarGridSpec(
            num_scalar_prefetch=1, grid=(nj, Hh // tr),
            in_specs=[pl.BlockSpec((None, None, tr, Cc), lambda j, i, c_ref: (c_ref[0], j, i, 0)),
                      pl.BlockSpec((None, tr, Cc), lambda j, i, c_ref: (j, i, 0))],
            out_specs=pl.BlockSpec((None, tr, Cc), lambda j, i, c_ref: (j, i, 0))),
        out_shape=jax.ShapeDtypeStruct((nj, Hh, Cc), BF16),
        compiler_params=_params(2),
    )(c, gpack, other)


def _rs_exchange_chips(part):
    nj, Hh, Cc = part.shape

    def body(src_ref, out_ref, send_sems, recv_sems):
        x, y, c = _coords()
        others = [(1 - x, y), (x, 1 - y), (1 - x, 1 - y)]
        cps = [_remote(src_ref.at[2 * ox + oy], out_ref.at[k], send_sems, recv_sems, k, (ox, oy, c))
               for k, (ox, oy) in enumerate(others)]
        for cp in cps:
            cp.start()
        for cp in cps:
            cp.wait()

    return pl.pallas_call(
        body, name="rs_exchange_chips", out_shape=jax.ShapeDtypeStruct((3, Hh, Cc), part.dtype),
        in_specs=[HBM], out_specs=HBM,
        scratch_shapes=[pltpu.SemaphoreType.DMA((3,)), pltpu.SemaphoreType.DMA((3,))],
    )(part)


def _rs_add_total(part, recv, chip):
    nj, Hh, Cc = part.shape
    tr = ADD_ROWS

    def body(chip_ref, p_ref, r0_ref, r1_ref, r2_ref, o_ref):
        f = lambda r: r[...].astype(F32)
        o_ref[...] = (f(p_ref) + f(r0_ref)) + (f(r1_ref) + f(r2_ref))

    rk = lambda k: pl.BlockSpec((None, tr, Cc), lambda i, chip_ref, k=k: (k, i, 0))
    return pl.pallas_call(
        body, name="rs_add_total",
        grid_spec=pltpu.PrefetchScalarGridSpec(
            num_scalar_prefetch=1, grid=(Hh // tr,),
            in_specs=[pl.BlockSpec((None, tr, Cc), lambda i, chip_ref: (chip_ref[0], i, 0)), rk(0), rk(1), rk(2)],
            out_specs=pl.BlockSpec((tr, Cc), lambda i, chip_ref: (i, 0))),
        out_shape=jax.ShapeDtypeStruct((Hh, Cc), F32),
        compiler_params=_params(1),
    )(chip, part, recv, recv, recv)


def _rs_share_total(total):
    Hh, Cc = total.shape

    def body(src_ref, out_ref, send_sems, recv_sems):
        x, y, c = _coords()
        cp = _remote(src_ref, out_ref, send_sems, recv_sems, 0, (x, y, 1 - c))
        cp.start()
        cp.wait()

    return pl.pallas_call(
        body, name="rs_share_total", out_shape=jax.ShapeDtypeStruct((Hh, Cc), total.dtype),
        in_specs=[HBM], out_specs=HBM,
        scratch_shapes=[pltpu.SemaphoreType.DMA((1,)), pltpu.SemaphoreType.DMA((1,))],
    )(total)


BIG = ("ffn1_gate", "ffn1_up", "ffn1_down", "w_in", "w_out", "ffn2_gate", "ffn2_up", "ffn2_down")
W_IN_ROWS = 960


def _rows(ref, start, size):
    return ref.at[pl.ds(pl.multiple_of(start, 16), size)]


def _allgather_arrays(shards):
    n = len(shards)

    def body(*refs):
        _ag_start(refs[:n], refs[n:2 * n], refs[2 * n], refs[2 * n + 1])
        _ag_finish(refs[:n], refs[n:2 * n], refs[2 * n], refs[2 * n + 1])

    _, shapes, n_sems, _, _ = _ag_comm(shards)
    return pl.pallas_call(
        body, name="allgather_weights", out_shape=shapes, in_specs=[HBM] * n, out_specs=[HBM] * n,
        scratch_shapes=[pltpu.SemaphoreType.DMA((n_sems,)), pltpu.SemaphoreType.DMA((n_sems,))],
    )(*shards)


def _ag_copies(srcs, outs, send_sems, recv_sems):
    x, y, c = _coords()
    sib = (x, y, 1 - c)
    me = 2 * x + y
    others = [(1 - x, y), (x, 1 - y), (1 - x, 1 - y)]
    plan = []
    for a, (src, out) in enumerate(zip(srcs, outs)):
        h = src.shape[0] // 2
        cp = lambda s, d, k, dev: _remote(s, d, send_sems, recv_sems, 7 * a + k, dev)
        own = cp(src, out.at[me], 6, sib)
        sends = [cp(_rows(src, c * h, h), _rows(out.at[me], c * h, h), j, (ox, oy, c)) for j, (ox, oy) in enumerate(others)]
        mine = [_rows(out.at[2 * ox + oy], c * h, h) for ox, oy in others]
        theirs = [_rows(out.at[2 * ox + oy], (1 - c) * h, h) for ox, oy in others]
        arrivals = [cp(m, m, j, sib) for j, m in enumerate(mine)]
        forwards = [cp(m, m, 3 + j, sib) for j, m in enumerate(mine)]
        forwarded = [cp(t, t, 3 + j, sib) for j, t in enumerate(theirs)]
        plan.append((own, sends, forwards, arrivals, forwarded))
    return plan


def _ag_start(srcs, outs, send_sems, recv_sems):
    for own, sends, _, _, _ in _ag_copies(srcs, outs, send_sems, recv_sems):
        own.start()
        for cp in sends:
            cp.start()


def _ag_finish(srcs, outs, send_sems, recv_sems):
    plan = _ag_copies(srcs, outs, send_sems, recv_sems)
    for _, _, forwards, arrivals, _ in plan:
        for arrived, fwd in zip(arrivals, forwards):
            arrived.wait_recv()
            fwd.start()
    for own, sends, forwards, _, forwarded in plan:
        for cp in forwarded:
            cp.wait_recv()
        own.wait_recv()
        for cp in [own] + sends + forwards:
            cp.wait_send()


def _ag_comm(shards):
    shapes = [jax.ShapeDtypeStruct((N_CHIPS,) + s.shape, s.dtype) for s in shards]
    return (list(shards), shapes, 7 * len(shards), _ag_start, _ag_finish)


def _swap_sibling(arrs, pick_other_half, name):
    n = len(arrs)
    outs = [jax.ShapeDtypeStruct((a.shape[0], a.shape[1] // 2) + a.shape[2:] if pick_other_half else a.shape, a.dtype) for a in arrs]

    def body(*refs):
        srcs, dsts, send_sems, recv_sems = refs[:n], refs[n:2 * n], refs[2 * n], refs[2 * n + 1]
        x, y, c = _coords()
        cps = []
        for a in range(n):
            src = srcs[a]
            if pick_other_half:
                h = src.shape[1] // 2
                src = src.at[:, pl.ds(pl.multiple_of((1 - c) * h, 16), h)]
            cp = _remote(src, dsts[a], send_sems, recv_sems, a, (x, y, 1 - c))
            cp.start()
            cps.append(cp)
        for cp in cps:
            cp.wait()

    return pl.pallas_call(
        body, name=name, out_shape=outs, in_specs=[HBM] * n, out_specs=[HBM] * n,
        scratch_shapes=[pltpu.SemaphoreType.DMA((n,)), pltpu.SemaphoreType.DMA((n,))],
    )(*arrs)


def _rs_add_pairs(gs, others, c, name):
    n = len(gs)
    hbs = [g.shape[1] // 4 for g in gs]

    def body(c_ref, *refs):
        for a in range(n):
            refs[2 * n + a][...] = (refs[a][...] + refs[n + a][...]).astype(BF16)

    mine = lambda hb: pl.BlockSpec((None, hb, D_MODEL), lambda j, s, c_ref: (j, c_ref[0] * 2 + s, 0))
    flat = lambda hb: pl.BlockSpec((None, hb, D_MODEL), lambda j, s, c_ref: (j, s, 0))
    return pl.pallas_call(
        body, name=name,
        grid_spec=pltpu.PrefetchScalarGridSpec(
            num_scalar_prefetch=1, grid=(N_CHIPS, 2),
            in_specs=[mine(hb) for hb in hbs] + [flat(hb) for hb in hbs],
            out_specs=[flat(hb) for hb in hbs]),
        out_shape=[jax.ShapeDtypeStruct(o.shape, BF16) for o in others],
        compiler_params=_params(2),
    )(c, *gs, *others)


def _rs_exchange_arrays(parts):
    n = len(parts)

    def body(*refs):
        _rsx_start(refs[:n], refs[n:2 * n], refs[2 * n], refs[2 * n + 1])
        _rsx_finish(refs[:n], refs[n:2 * n], refs[2 * n], refs[2 * n + 1])

    _, shapes, n_sems, _, _ = _rsx_comm(parts)
    return pl.pallas_call(
        body, name="rs_exchange_chips", out_shape=shapes, in_specs=[HBM] * n, out_specs=[HBM] * n,
        scratch_shapes=[pltpu.SemaphoreType.DMA((n_sems,)), pltpu.SemaphoreType.DMA((n_sems,))],
    )(*parts)


def _rsx_copies(srcs, dsts, send_sems, recv_sems):
    x, y, c = _coords()
    others = [(1 - x, y), (x, 1 - y), (1 - x, 1 - y)]
    return [_remote(src.at[2 * ox + oy], dst.at[k], send_sems, recv_sems, 3 * a + k, (ox, oy, c))
            for a, (src, dst) in enumerate(zip(srcs, dsts)) for k, (ox, oy) in enumerate(others)]


def _rsx_start(srcs, dsts, send_sems, recv_sems):
    for cp in _rsx_copies(srcs, dsts, send_sems, recv_sems):
        cp.start()


def _rsx_finish(srcs, dsts, send_sems, recv_sems):
    for cp in _rsx_copies(srcs, dsts, send_sems, recv_sems):
        cp.wait()


def _rsx_comm(parts):
    shapes = [jax.ShapeDtypeStruct((3,) + p.shape[1:], p.dtype) for p in parts]
    return (list(parts), shapes, 3 * len(parts), _rsx_start, _rsx_finish)


def _rs_add_totals(parts, recvs, chip):
    n = len(parts)
    hbs = [p.shape[1] // 2 for p in parts]

    def body(chip_ref, *refs):
        f = lambda r: r[...].astype(F32)
        for a in range(n):
            p, r0, r1, r2 = refs[a], refs[n + 3 * a], refs[n + 3 * a + 1], refs[n + 3 * a + 2]
            refs[4 * n + a][...] = (f(p) + f(r0)) + (f(r1) + f(r2))

    own = lambda hb: pl.BlockSpec((None, hb, D_MODEL), lambda s, chip_ref: (chip_ref[0], s, 0))
    slot = lambda hb, k: pl.BlockSpec((None, hb, D_MODEL), lambda s, chip_ref, k=k: (k, s, 0))
    recv_specs = [slot(hb, k) for hb in hbs for k in range(3)]
    recv_args = [r for r in recvs for _ in range(3)]
    return pl.pallas_call(
        body, name="rs_add_totals",
        grid_spec=pltpu.PrefetchScalarGridSpec(
            num_scalar_prefetch=1, grid=(2,),
            in_specs=[own(hb) for hb in hbs] + recv_specs,
            out_specs=[pl.BlockSpec((hb, D_MODEL), lambda s, chip_ref: (s, 0)) for hb in hbs]),
        out_shape=[jax.ShapeDtypeStruct(p.shape[1:], F32) for p in parts],
        compiler_params=_params(1),
    )(chip, *parts, *recv_args)


def _permute_w_in(w):
    return jnp.concatenate([w[:, :3072], w[:, 3080:IN_COLS], w[:, 3072:3080],
                            jnp.zeros((w.shape[0], IN_COLS_PADDED - IN_COLS), w.dtype)], axis=1)


def _pack_rows(shards, dtype):
    rows = [shards[n].astype(dtype).reshape(r, D_MODEL) for n, r in PACK_SECTIONS]
    used = sum(r for _, r in PACK_SECTIONS)
    rows.append(jnp.zeros((PACK_ROWS - used, D_MODEL), dtype))
    return jnp.concatenate(rows, axis=0)


def _unpack_rows(pack, shapes):
    out, at = {}, 0
    for n, r in PACK_SECTIONS:
        out[n] = pack[at:at + r].reshape(shapes[n])
        at += r
    return out


SHARD_SHAPES = dict(ffn1_gate=(1024, 704), ffn1_up=(1024, 704), ffn1_down=(704, 1024), w_in=(1024, 898),
                    w_out=(256, 1024), ffn2_gate=(1024, 704), ffn2_up=(1024, 704), ffn2_down=(704, 1024))
ROW_SHARDED = ("ffn1_down", "w_out", "ffn2_down")
SMALL_ROWS = 16


def _pad_row(v):
    v = v.reshape(1, -1)
    return jnp.pad(v, ((0, 0), (0, D_MODEL - v.shape[1])))


def kernel(x, norm_ffn1, ffn1_gate, ffn1_up, ffn1_down, norm_mix, w_in, conv_w, a_log, dt_bias, dn_norm, w_out, norm_ffn2, ffn2_gate, ffn2_up, ffn2_down, norm_final, loss_target, m_norm_ffn1, m_ffn1_gate, m_ffn1_up, m_ffn1_down, m_norm_mix, m_w_in, m_conv_w, m_a_log, m_dt_bias, m_dn_norm, m_w_out, m_norm_ffn2, m_ffn2_gate, m_ffn2_up, m_ffn2_down, m_norm_final, v_norm_ffn1, v_ffn1_gate, v_ffn1_up, v_ffn1_down, v_norm_mix, v_w_in, v_conv_w, v_a_log, v_dt_bias, v_dn_norm, v_w_out, v_norm_ffn2, v_ffn2_gate, v_ffn2_up, v_ffn2_down, v_norm_final):
    cx, cy, cc = _coords()
    chip = 2 * cx + cy
    big_w = dict(ffn1_gate=ffn1_gate[0], ffn1_up=ffn1_up[0], ffn1_down=ffn1_down[0], w_in=w_in[0], w_out=w_out[0],
                 ffn2_gate=ffn2_gate[0], ffn2_up=ffn2_up[0], ffn2_down=ffn2_down[0])
    big_m = dict(ffn1_gate=m_ffn1_gate[0], ffn1_up=m_ffn1_up[0], ffn1_down=m_ffn1_down[0], w_in=m_w_in[0], w_out=m_w_out[0],
                 ffn2_gate=m_ffn2_gate[0], ffn2_up=m_ffn2_up[0], ffn2_down=m_ffn2_down[0])
    big_v = dict(ffn1_gate=v_ffn1_gate[0], ffn1_up=v_ffn1_up[0], ffn1_down=v_ffn1_down[0], w_in=v_w_in[0], w_out=v_w_out[0],
                 ffn2_gate=v_ffn2_gate[0], ffn2_up=v_ffn2_up[0], ffn2_down=v_ffn2_down[0])

    early = tuple(n for n in BIG if n not in LATE_WEIGHTS)
    wts = dict(zip(early, _allgather_arrays([big_w[n].astype(BF16) for n in early])))
    wts["w_in"] = _permute_w_in(jnp.concatenate([wts["w_in"][j] for j in range(N_CHIPS)], axis=1))
    dist = dict(late=[big_w[n].astype(BF16) for n in LATE_WEIGHTS], c=cc.reshape(1).astype(jnp.int32),
                chip=chip.reshape(1).astype(jnp.int32))

    conv_shard = conv_w[0]
    emb = jnp.concatenate([jnp.where((chip == j) & (cc == 0), conv_shard, 0.0) for j in range(N_CHIPS)], axis=1)
    emb = jnp.pad(emb.reshape(6, D_MODEL), ((0, 2), (0, 0)))
    conv_full = _allreduce_small(emb, "allgather_conv_w")[:6].reshape(CONV_WIDTH, 3 * DN_WIDTH)

    zvec = jnp.zeros((1, 128), F32)
    small = dict(norm_ffn1=norm_ffn1, norm_mix=norm_mix, norm_ffn2=norm_ffn2, norm_final=norm_final[None],
                 conv_w=conv_full, avec=zvec.at[0, DN_HEADS:2 * DN_HEADS].set(a_log[0]),
                 dvec=zvec.at[0, DN_HEADS:2 * DN_HEADS].set(dt_bias[0]), dn_norm=dn_norm)

    loss, grad_x, reduced, sg = _local_step(x[0], loss_target[0], wts, small, dist)

    rows = [sg["norm_ffn1"], sg["norm_mix"], sg["norm_ffn2"], sg["norm_final"], _pad_row(sg["a_log"]), _pad_row(sg["dt_bias"]),
            _pad_row(sg["dn_norm"]), _pad_row(loss[0:1]), sg["conv_w"].reshape(6, D_MODEL), jnp.zeros((2, D_MODEL), F32)]
    red = _allreduce_small(jnp.concatenate(rows, axis=0), "allreduce_small")
    loss_out = red[7, 0]
    g_conv_full = red[8:14].reshape(CONV_WIDTH, 3 * DN_WIDTH)
    g_conv = lax.dynamic_slice_in_dim(g_conv_full, chip * (3 * DN_WIDTH // N_CHIPS), 3 * DN_WIDTH // N_CHIPS, axis=1)
    g_small = dict(norm_ffn1=red[0:1], norm_mix=red[1:2], norm_ffn2=red[2:3], norm_final=red[3],
                   a_log=red[4:5, DN_HEADS:2 * DN_HEADS], dt_bias=red[5:6, DN_HEADS:2 * DN_HEADS], dn_norm=red[6:7, :DN_HEAD_DIM])

    shard_g = {}
    for n in BIG:
        mine, other = reduced[n]
        full = jnp.where(cc == 0, jnp.concatenate([mine, other], axis=0), jnp.concatenate([other, mine], axis=0))
        if n == "w_in":
            full = full[:IN_COLS // N_CHIPS]
        shard_g[n] = full if n in ROW_SHARDED else full.T

    out_g, out_d, out_m, out_v = {}, {}, {}, {}
    for n in BIG:
        d, nm, nv = _adamw(big_w[n], shard_g[n], big_m[n], big_v[n], "adamw_" + n)
        out_g[n], out_d[n], out_m[n], out_v[n] = shard_g[n][None], d[None], nm[None], nv[None]
    d, nm, nv = _adamw(conv_w[0], g_conv, m_conv_w[0], v_conv_w[0], "adamw_conv_w")
    out_g["conv_w"], out_d["conv_w"], out_m["conv_w"], out_v["conv_w"] = g_conv[None], d[None], nm[None], nv[None]

    small_names = ("norm_ffn1", "norm_mix", "norm_ffn2", "norm_final", "a_log", "dt_bias", "dn_norm")
    small_w = dict(norm_ffn1=norm_ffn1, norm_mix=norm_mix, norm_ffn2=norm_ffn2, norm_final=norm_final, a_log=a_log,
                   dt_bias=dt_bias, dn_norm=dn_norm)
    small_m = dict(norm_ffn1=m_norm_ffn1, norm_mix=m_norm_mix, norm_ffn2=m_norm_ffn2, norm_final=m_norm_final, a_log=m_a_log,
                   dt_bias=m_dt_bias, dn_norm=m_dn_norm)
    small_v = dict(norm_ffn1=v_norm_ffn1, norm_mix=v_norm_mix, norm_ffn2=v_norm_ffn2, norm_final=v_norm_final, a_log=v_a_log,
                   dt_bias=v_dt_bias, dn_norm=v_dn_norm)
    stack = lambda dct: jnp.concatenate([_pad_row(dct[n]) for n in small_names] + [jnp.zeros((1, D_MODEL), F32)], axis=0)
    d, nm, nv = _adamw(stack(small_w), stack(g_small), stack(small_m), stack(small_v), "adamw_small")
    for k, n in enumerate(small_names):
        shape = small_w[n].shape
        size = math.prod(shape)
        out_g[n] = g_small[n].reshape(shape)
        out_d[n], out_m[n], out_v[n] = (t[k, :size].reshape(shape) for t in (d, nm, nv))

    order = ("norm_ffn1", "ffn1_gate", "ffn1_up", "ffn1_down", "norm_mix", "w_in", "conv_w", "a_log", "dt_bias", "dn_norm",
             "w_out", "norm_ffn2", "ffn2_gate", "ffn2_up", "ffn2_down", "norm_final")
    return (loss_out, grad_x[None], *[out_g[n] for n in order], *[out_d[n] for n in order],
            *[out_m[n] for n in order], *[out_v[n] for n in order])
```

```python
import functools
import math

import jax
import jax.numpy as jnp
from jax import lax
from jax.experimental import pallas as pl
from jax.experimental.pallas import tpu as pltpu

F32 = jnp.float32
BF16 = jnp.bfloat16
HI = lax.Precision.HIGH

D_MODEL = 1024
D_FF = 2816
ATTN_HEADS = 8
ATTN_WIDTH = 512
ATTN_BLOCK = 128
DILATIONS = (1, 4, 16)
DN_HEADS = 4
DN_HEAD_DIM = 128
DN_WIDTH = 512
DN_CHUNK = 64
CONV_WIDTH = 4
NORM_EPS = 1e-6
L2_EPS = 1e-6
IN_COLS = 3592
IN_COLS_PADDED = 3712
N_CHIPS = 4

ADAM_LR = 0.001
ADAM_B1 = 0.9
ADAM_B2 = 0.999
ADAM_EPS = 1e-08
ADAM_WD = 0.01
ADAM_STEP = 10

VMEM_LIMIT = 56 * 1024 * 1024
NEG_BIG = -1e30
MESH = pl.DeviceIdType.MESH


def _params(n_grid, vmem=VMEM_LIMIT):
    return pltpu.CompilerParams(dimension_semantics=("arbitrary",) * n_grid, vmem_limit_bytes=vmem)


def _call(body, args, *, name, grid, in_specs, out_specs, out_shape, scratch_shapes=(), comm=None):
    n_in, n_out, n_scr = len(in_specs), len(out_specs), len(scratch_shapes)
    hbm = pl.BlockSpec(memory_space=pl.ANY)
    srcs, dst_shapes, n_sems, start, finish = comm if comm is not None else ((), (), 0, None, None)
    ns, nd = len(srcs), len(dst_shapes)

    def full(*refs):
        ins, c_src = refs[:n_in], refs[n_in:n_in + ns]
        at = n_in + ns
        outs, c_dst = refs[at:at + n_out], refs[at + n_out:at + n_out + nd]
        scr = refs[at + n_out + nd:at + n_out + nd + n_scr]
        if comm is not None:
            ids = [pl.program_id(a) for a in range(len(grid))]
            first = functools.reduce(jnp.logical_and, [i == 0 for i in ids])
            last = functools.reduce(jnp.logical_and, [i == g - 1 for i, g in zip(ids, grid)])

            @pl.when(first)
            def _():
                start(c_src, c_dst, refs[-2], refs[-1])

        body(*ins, *outs, *scr)
        if comm is not None:
            @pl.when(last)
            def _():
                finish(c_src, c_dst, refs[-2], refs[-1])

    sems = [pltpu.SemaphoreType.DMA((n_sems,)), pltpu.SemaphoreType.DMA((n_sems,))] if comm is not None else []
    res = pl.pallas_call(
        full, name=name, grid=grid, in_specs=list(in_specs) + [hbm] * ns, out_specs=list(out_specs) + [hbm] * nd,
        out_shape=list(out_shape) + list(dst_shapes), scratch_shapes=list(scratch_shapes) + sems,
        compiler_params=_params(len(grid)),
    )(*args, *srcs)
    return res[:n_out], res[n_out:]


def _nt(a, b, precision=None):
    return lax.dot_general(a, b, (((1,), (1,)), ((), ())), preferred_element_type=F32, precision=precision)


def _tn(a, b, precision=None):
    return lax.dot_general(a, b, (((0,), (0,)), ((), ())), preferred_element_type=F32, precision=precision)


def _nn(a, b, precision=None):
    return jnp.dot(a, b, preferred_element_type=F32, precision=precision)


def _sigmoid(x):
    return 1.0 / (1.0 + jnp.exp(-x))


def _ffn_fwd(x, gain, wg, wu, wd, name, comm=None):
    S, D = x.shape
    nf, _, tf = wg.shape
    tm = 512

    def body(x_ref, gain_ref, wg_ref, wu_ref, wd_ref, xo_ref, h_ref, g_ref, u_ref, acc_ref, hs_ref):
        j = pl.program_id(1)

        @pl.when(j == 0)
        def _():
            xf = x_ref[...]
            r = lax.rsqrt(jnp.mean(xf * xf, axis=-1, keepdims=True) + NORM_EPS)
            h = (xf * r * gain_ref[...]).astype(BF16)
            hs_ref[...] = h
            h_ref[...] = h
            acc_ref[...] = jnp.zeros_like(acc_ref)

        h = hs_ref[...]
        g = _nn(h, wg_ref[...])
        u = _nn(h, wu_ref[...])
        g_ref[...] = g.astype(BF16)
        u_ref[...] = u.astype(BF16)
        act = g * _sigmoid(g) * u
        acc_ref[...] += _nn(act.astype(BF16), wd_ref[...])

        @pl.when(j == nf - 1)
        def _():
            xo_ref[...] = x_ref[...] + 0.5 * acc_ref[...]

    return _call(
        body, (x, gain, wg, wu, wd), name=name, grid=(S // tm, nf), comm=comm,
        in_specs=[pl.BlockSpec((tm, D), lambda i, j: (i, 0)),
                  pl.BlockSpec((1, D), lambda i, j: (0, 0)),
                  pl.BlockSpec((None, D, tf), lambda i, j: (j, 0, 0)),
                  pl.BlockSpec((None, D, tf), lambda i, j: (j, 0, 0)),
                  pl.BlockSpec((None, tf, D), lambda i, j: (j, 0, 0))],
        out_specs=[pl.BlockSpec((tm, D), lambda i, j: (i, 0)),
                   pl.BlockSpec((tm, D), lambda i, j: (i, 0)),
                   pl.BlockSpec((None, tm, tf), lambda i, j: (j, i, 0)),
                   pl.BlockSpec((None, tm, tf), lambda i, j: (j, i, 0))],
        out_shape=[jax.ShapeDtypeStruct((S, D), F32), jax.ShapeDtypeStruct((S, D), BF16),
                   jax.ShapeDtypeStruct((nf, S, tf), BF16), jax.ShapeDtypeStruct((nf, S, tf), BF16)],
        scratch_shapes=[pltpu.VMEM((tm, D), F32), pltpu.VMEM((tm, D), BF16)])


def _rmsnorm_bwd(dh, xf, gain):
    r = lax.rsqrt(jnp.mean(xf * xf, axis=-1, keepdims=True) + NORM_EPS)
    xhat = xf * r
    dgain = jnp.sum(dh * xhat, axis=0, keepdims=True)
    dxh = dh * gain
    dx = r * (dxh - xhat * jnp.mean(dxh * xhat, axis=-1, keepdims=True))
    return dx, dgain


def _ffn_bwd(dxo, x, gain, g, u, wd, wg, wu, name, comm=None):
    S, D = x.shape
    nf, _, tf = g.shape
    tm = 512

    def body(dxo_ref, x_ref, gain_ref, g_ref, u_ref, wd_ref, wg_ref, wu_ref,
             dx_ref, dgain_ref, dg_ref, du_ref, act_ref, dout_ref, acc_ref, ds_ref):
        i = pl.program_id(0)
        j = pl.program_id(1)

        @pl.when(j == 0)
        def _():
            d = (0.5 * dxo_ref[...]).astype(BF16)
            ds_ref[...] = d
            dout_ref[...] = d
            acc_ref[...] = jnp.zeros_like(acc_ref)

        @pl.when((i == 0) & (j == 0))
        def _():
            dgain_ref[...] = jnp.zeros_like(dgain_ref)

        for half in range(2):
            rows = slice(half * (tm // 2), (half + 1) * (tm // 2))
            dact = _nt(ds_ref[rows, :], wd_ref[...])
            gv = g_ref[rows, :].astype(F32)
            uv = u_ref[rows, :].astype(F32)
            sg = _sigmoid(gv)
            silu = gv * sg
            act_ref[rows, :] = (silu * uv).astype(BF16)
            dgv = (dact * uv * (sg * (1.0 + gv * (1.0 - sg)))).astype(BF16)
            duv = (dact * silu).astype(BF16)
            dg_ref[rows, :] = dgv
            du_ref[rows, :] = duv
            acc_ref[rows, :] += _nt(dgv, wg_ref[...]) + _nt(duv, wu_ref[...])

        @pl.when(j == nf - 1)
        def _():
            dx, dgain = _rmsnorm_bwd(acc_ref[...], x_ref[...], gain_ref[...])
            dx_ref[...] = dxo_ref[...] + dx
            dgain_ref[...] += dgain

    return _call(
        body, (dxo, x, gain, g, u, wd, wg, wu), name=name, grid=(S // tm, nf), comm=comm,
        in_specs=[pl.BlockSpec((tm, D), lambda i, j: (i, 0)),
                  pl.BlockSpec((tm, D), lambda i, j: (i, 0)),
                  pl.BlockSpec((1, D), lambda i, j: (0, 0)),
                  pl.BlockSpec((None, tm, tf), lambda i, j: (j, i, 0)),
                  pl.BlockSpec((None, tm, tf), lambda i, j: (j, i, 0)),
                  pl.BlockSpec((None, tf, D), lambda i, j: (j, 0, 0)),
                  pl.BlockSpec((None, D, tf), lambda i, j: (j, 0, 0)),
                  pl.BlockSpec((None, D, tf), lambda i, j: (j, 0, 0))],
        out_specs=[pl.BlockSpec((tm, D), lambda i, j: (i, 0)),
                   pl.BlockSpec((1, D), lambda i, j: (0, 0)),
                   pl.BlockSpec((None, tm, tf), lambda i, j: (j, i, 0)),
                   pl.BlockSpec((None, tm, tf), lambda i, j: (j, i, 0)),
                   pl.BlockSpec((None, tm, tf), lambda i, j: (j, i, 0)),
                   pl.BlockSpec((tm, D), lambda i, j: (i, 0))],
        out_shape=[jax.ShapeDtypeStruct((S, D), F32), jax.ShapeDtypeStruct((1, D), F32),
                   jax.ShapeDtypeStruct((nf, S, tf), BF16), jax.ShapeDtypeStruct((nf, S, tf), BF16),
                   jax.ShapeDtypeStruct((nf, S, tf), BF16), jax.ShapeDtypeStruct((S, D), BF16)],
        scratch_shapes=[pltpu.VMEM((tm, D), F32), pltpu.VMEM((tm, D), BF16)])


def _matmul_tn(a, b, tm, tk, name):
    K, M = a.shape
    N = b.shape[1]

    def body(a_ref, b_ref, o_ref):
        @pl.when(pl.program_id(1) == 0)
        def _():
            o_ref[...] = jnp.zeros_like(o_ref)

        o_ref[...] += _tn(a_ref[...], b_ref[...])

    return pl.pallas_call(
        body, name=name, grid=(M // tm, K // tk),
        in_specs=[pl.BlockSpec((tk, tm), lambda i, k: (k, i)),
                  pl.BlockSpec((tk, N), lambda i, k: (k, 0))],
        out_specs=pl.BlockSpec((tm, N), lambda i, k: (i, 0)),
        out_shape=jax.ShapeDtypeStruct((M, N), F32),
        compiler_params=_params(2),
    )(a, b)


def _dw_chunks(a, b, tk, name, comm=None):
    nf, S, tf = a.shape
    N = b.shape[1]

    def body(a_ref, b_ref, o_ref):
        @pl.when(pl.program_id(1) == 0)
        def _():
            o_ref[...] = jnp.zeros_like(o_ref)

        o_ref[...] += _tn(a_ref[...], b_ref[...])

    (out,), landed = _call(
        body, (a, b), name=name, grid=(nf, S // tk), comm=comm,
        in_specs=[pl.BlockSpec((None, tk, tf), lambda j, k: (j, k, 0)),
                  pl.BlockSpec((tk, N), lambda j, k: (k, 0))],
        out_specs=[pl.BlockSpec((None, tf, N), lambda j, k: (j, 0, 0))],
        out_shape=[jax.ShapeDtypeStruct((nf, tf, N), F32)])
    return out, landed


VIEW_TILE = 512


def _view_spec(d, tile=VIEW_TILE):
    return pl.BlockSpec((tile // d, d * ATTN_WIDTH), lambda i: (i, 0))


def _view_shape(S, d, dtype):
    return jax.ShapeDtypeStruct((S // d, d * ATTN_WIDTH), dtype)


def _tile_to_views(val, planes, out_refs):
    for g in range(4):
        planes[g] = val[:, g * 128:(g + 1) * 128]
    for d, ref in zip(DILATIONS, out_refs):
        if d == 1:
            ref[...] = val.astype(ref.dtype)
            continue
        for r in range(d):
            for g in range(4):
                ref[:, r * ATTN_WIDTH + g * 128:r * ATTN_WIDTH + (g + 1) * 128] = (
                    planes[g, pl.ds(r, planes.shape[1] // d, stride=d), :].astype(ref.dtype))


def _view_to_tile(ref, d, planes):
    if d == 1:
        return ref[...]
    for r in range(d):
        for g in range(4):
            planes[g, pl.ds(r, planes.shape[1] // d, stride=d), :] = ref[:, r * ATTN_WIDTH + g * 128:r * ATTN_WIDTH + (g + 1) * 128]
    return jnp.concatenate([planes[g] for g in range(4)], axis=1)


def _inproj_fwd(x, gain, w_in_p):
    S, D = x.shape
    tm = VIEW_TILE
    W = ATTN_WIDTH

    def body(x_ref, gain_ref, w_ref, h_ref, q1, q4, q16, k1, k4, k16, v1, v4, v16, dq_ref, dk_ref, dv_ref, gate_ref, bd_ref,
             planes):
        xf = x_ref[...]
        r = lax.rsqrt(jnp.mean(xf * xf, axis=-1, keepdims=True) + NORM_EPS)
        h = (xf * r * gain_ref[...]).astype(BF16)
        h_ref[...] = h
        _tile_to_views(_nn(h, w_ref[:, 0:W]) * 0.125, planes, (q1, q4, q16))
        _tile_to_views(_nn(h, w_ref[:, W:2 * W]), planes, (k1, k4, k16))
        _tile_to_views(_nn(h, w_ref[:, 2 * W:3 * W]), planes, (v1, v4, v16))
        dq_ref[...] = _nn(h, w_ref[:, 3 * W:4 * W])
        dk_ref[...] = _nn(h, w_ref[:, 4 * W:5 * W])
        dv_ref[...] = _nn(h, w_ref[:, 5 * W:6 * W])
        gate_ref[...] = _nn(h, w_ref[:, 6 * W:7 * W])
        bd_ref[...] = _nn(h, w_ref[:, 7 * W:7 * W + 128])

    tok = lambda w: pl.BlockSpec((tm, w), lambda i: (i, 0))
    return pl.pallas_call(
        body, name="inproj_fwd", grid=(S // tm,),
        in_specs=[tok(D), pl.BlockSpec((1, D), lambda i: (0, 0)),
                  pl.BlockSpec((D, IN_COLS_PADDED), lambda i: (0, 0))],
        out_specs=[tok(D)] + [_view_spec(d) for d in DILATIONS] * 3 + [tok(W)] * 4 + [tok(128)],
        out_shape=[jax.ShapeDtypeStruct((S, D), BF16)] + [_view_shape(S, d, BF16) for d in DILATIONS] * 3
                  + [jax.ShapeDtypeStruct((S, W), F32)] * 4 + [jax.ShapeDtypeStruct((S, 128), F32)],
        scratch_shapes=[pltpu.VMEM((4, tm, 128), F32)],
        compiler_params=_params(1),
    )(x, gain, w_in_p)


def _inproj_bwd(dxo, x, gain, attn_grads, dsecs, dbd, w_in_p):
    S, D = x.shape
    tm = VIEW_TILE // 2
    W = ATTN_WIDTH

    def body(dxo_ref, x_ref, gain_ref, *rest):
        views, (s3, s4, s5, s6, dbd_ref, w_ref, dx_ref, dgain_ref, dproj_ref, planes) = rest[:9], rest[9:]

        @pl.when(pl.program_id(0) == 0)
        def _():
            dgain_ref[...] = jnp.zeros_like(dgain_ref)

        secs = []
        for k in range(3):
            parts = [_view_to_tile(views[3 * k + p], d, planes) for p, d in enumerate(DILATIONS)]
            secs.append(parts[0] + parts[1] + parts[2])
        secs += [s3[...], s4[...], s5[...], s6[...]]
        dh = jnp.zeros((tm, D), F32)
        for k, s in enumerate(secs):
            d = s.astype(BF16)
            dproj_ref[:, k * W:(k + 1) * W] = d
            dh += _nt(d, w_ref[:, k * W:(k + 1) * W])
        d = dbd_ref[...].astype(BF16)
        dproj_ref[:, 7 * W:7 * W + 128] = d
        dh += _nt(d, w_ref[:, 7 * W:7 * W + 128])
        dx, dgain = _rmsnorm_bwd(dh, x_ref[...], gain_ref[...])
        dx_ref[...] = dxo_ref[...] + dx
        dgain_ref[...] += dgain

    tok = lambda w: pl.BlockSpec((tm, w), lambda i: (i, 0))
    return pl.pallas_call(
        body, name="inproj_bwd", grid=(S // tm,),
        in_specs=[tok(D), tok(D), pl.BlockSpec((1, D), lambda i: (0, 0))] + [_view_spec(d, tm) for d in DILATIONS] * 3
                 + [tok(W)] * 4 + [tok(128)] + [pl.BlockSpec((D, IN_COLS_PADDED), lambda i: (0, 0))],
        out_specs=[tok(D), pl.BlockSpec((1, D), lambda i: (0, 0)), tok(IN_COLS_PADDED)],
        out_shape=[jax.ShapeDtypeStruct((S, D), F32), jax.ShapeDtypeStruct((1, D), F32),
                   jax.ShapeDtypeStruct((S, IN_COLS_PADDED), BF16)],
        scratch_shapes=[pltpu.VMEM((4, tm, 128), F32)],
        compiler_params=_params(1),
    )(dxo, x, gain, *[g for grads in attn_grads for g in grads], *dsecs, dbd, w_in_p)


def _slope(h):
    return 2.0 ** (-8.0 * (h + 1) / ATTN_HEADS)


def _head_bias(steps, d, heads=tuple(range(ATTN_HEADS))):
    stepsf = steps.astype(F32)
    return jnp.stack([stepsf * (-_slope(h) * d) for h in heads])


def _hnt(a, b):
    return lax.dot_general(a, b, (((2,), (2,)), ((0,), (0,))), preferred_element_type=F32)


def _hnn(a, b):
    return lax.dot_general(a, b, (((2,), (1,)), ((0,), (0,))), preferred_element_type=F32)


def _blocks_per_step(nb):
    return 2 if nb % 2 == 0 else 1


def _query_step_specs(qb):
    B = ATTN_BLOCK
    cur = pl.BlockSpec((qb * B, ATTN_WIDTH), lambda r, n: (n, r))
    prev = pl.BlockSpec((B, ATTN_WIDTH), lambda r, n: (jnp.maximum(qb * n - 1, 0), r))
    return cur, prev


def _prev_block(prev_ref, cur_ref, sub, sl):
    B = ATTN_BLOCK
    return prev_ref[:, sl] if sub == 0 else cur_ref[(sub - 1) * B:sub * B, sl]


def _head_cols(tile, lo, big):
    return [_head_col(tile, lo, big), _head_col(tile, jnp.logical_not(lo), big)]


def _attn_fwd(q, k, v, d, name):
    L = q.shape[0]
    nb = L // ATTN_BLOCK
    B = ATTN_BLOCK
    QB = _blocks_per_step(nb)

    def body(q_ref, kp_ref, kc_ref, vp_ref, vc_ref, acc_ref, m_ref, l_ref):
        n = pl.program_id(1)
        qi = lax.broadcasted_iota(jnp.int32, (B, 2 * B), 0)
        kj = lax.broadcasted_iota(jnp.int32, (B, 2 * B), 1)
        steps = qi + B - kj
        band = (steps >= 0) & (steps <= B)
        lo = lax.broadcasted_iota(jnp.int32, (B, 128), 1) < 64
        bias = _head_bias(steps, d)
        for sub in range(QB):
            rows = slice(sub * B, (sub + 1) * B)
            valid = band & ((kj >= B) | (n > 0)) if sub == 0 else band
            qs, ks, vs = [], [], []
            for G in range(4):
                sl = slice(G * 128, (G + 1) * 128)
                qg = q_ref[rows, sl]
                kg = jnp.concatenate([_prev_block(kp_ref, kc_ref, sub, sl), kc_ref[rows, sl]], axis=0)
                vg = jnp.concatenate([_prev_block(vp_ref, vc_ref, sub, sl), vc_ref[rows, sl]], axis=0)
                qs += [jnp.where(lo, qg, jnp.zeros_like(qg)), jnp.where(lo, jnp.zeros_like(qg), qg)]
                ks += [kg, kg]
                vs += [vg, vg]
            s = jnp.where(valid, _hnt(jnp.stack(qs), jnp.stack(ks)) + bias, NEG_BIG)
            m = jnp.max(s, axis=-1, keepdims=True)
            p = jnp.exp(s - m)
            l = jnp.sum(p, axis=-1, keepdims=True)
            a = _hnn(p.astype(BF16), jnp.stack(vs))
            for G in range(4):
                sl = slice(G * 128, (G + 1) * 128)
                acc_ref[rows, sl] = jnp.where(lo, a[2 * G], a[2 * G + 1])
                m_ref[rows, sl] = jnp.where(lo, m[2 * G], m[2 * G + 1])
                l_ref[rows, sl] = jnp.where(lo, l[2 * G], l[2 * G + 1])

    cur, prev = _query_step_specs(QB)
    return pl.pallas_call(
        body, name=name, grid=(d, nb // QB),
        in_specs=[cur, prev, cur, prev, cur],
        out_specs=[cur, cur, cur],
        out_shape=[jax.ShapeDtypeStruct((L, d * ATTN_WIDTH), F32)] * 3,
        compiler_params=_params(2),
    )(q, k, k, v, v)


def _attn_merge(parts):
    S = parts[0][0].shape[0]
    tm = VIEW_TILE

    def body(a1, m1, l1, a2, m2, l2, a3, m3, l3, o_ref, lse1, lse4, lse16, planes):
        ins = ((a1, m1, l1), (a2, m2, l2), (a3, m3, l3))
        acc, ms, ls = [], [], []
        for d, (a, m, l) in zip(DILATIONS, ins):
            acc.append(_view_to_tile(a, d, planes))
            ms.append(_view_to_tile(m, d, planes))
            ls.append(_view_to_tile(l, d, planes))
        mx = jnp.maximum(jnp.maximum(ms[0], ms[1]), ms[2])
        es = [jnp.exp(m - mx) for m in ms]
        den = es[0] * ls[0] + es[1] * ls[1] + es[2] * ls[2]
        num = es[0] * acc[0] + es[1] * acc[1] + es[2] * acc[2]
        o_ref[...] = num / den
        _tile_to_views(mx + jnp.log(den), planes, (lse1, lse4, lse16))

    views = [_view_spec(d) for d in DILATIONS]
    flat = [t for p in parts for t in p]
    return pl.pallas_call(
        body, name="attn_merge", grid=(S // tm,),
        in_specs=[views[p] for p in range(3) for _ in range(3)],
        out_specs=[views[0]] + views,
        out_shape=[jax.ShapeDtypeStruct((S, ATTN_WIDTH), F32)] + [_view_shape(S, d, F32) for d in DILATIONS],
        scratch_shapes=[pltpu.VMEM((4, tm, 128), F32)],
        compiler_params=_params(1),
    )(*flat)


def _head_col(t, msk, big):
    if big:
        return jnp.max(jnp.where(msk, t, NEG_BIG), axis=-1, keepdims=True)
    return jnp.sum(jnp.where(msk, t, 0.0), axis=-1, keepdims=True) * (1.0 / 64.0)


def _attn_bwd_q(q, k, v, do, lse, dd, d, name):
    L = q.shape[0]
    nb = L // ATTN_BLOCK
    B = ATTN_BLOCK
    QB = _blocks_per_step(nb)

    def body(q_ref, kp_ref, kc_ref, vp_ref, vc_ref, do_ref, lse_ref, dd_ref, dq_ref):
        n = pl.program_id(1)
        qi = lax.broadcasted_iota(jnp.int32, (B, 2 * B), 0)
        kj = lax.broadcasted_iota(jnp.int32, (B, 2 * B), 1)
        steps = qi + B - kj
        band = (steps >= 0) & (steps <= B)
        lo = lax.broadcasted_iota(jnp.int32, (B, 128), 1) < 64
        bias = _head_bias(steps, d)
        for sub in range(QB):
            rows = slice(sub * B, (sub + 1) * B)
            valid = band & ((kj >= B) | (n > 0)) if sub == 0 else band
            qs, ks, vs, dos, lses, dcols = [], [], [], [], [], []
            for G in range(4):
                sl = slice(G * 128, (G + 1) * 128)
                qg = q_ref[rows, sl]
                kg = jnp.concatenate([_prev_block(kp_ref, kc_ref, sub, sl), kc_ref[rows, sl]], axis=0)
                vg = jnp.concatenate([_prev_block(vp_ref, vc_ref, sub, sl), vc_ref[rows, sl]], axis=0)
                dog = do_ref[rows, sl]
                qs += [jnp.where(lo, qg, jnp.zeros_like(qg)), jnp.where(lo, jnp.zeros_like(qg), qg)]
                dos += [jnp.where(lo, dog, 0.0).astype(BF16), jnp.where(lo, 0.0, dog).astype(BF16)]
                ks += [kg, kg]
                vs += [vg, vg]
                lses += _head_cols(lse_ref[rows, sl], lo, True)
                dcols += _head_cols(dd_ref[rows, sl], lo, False)
            kb = jnp.stack(ks)
            s = _hnt(jnp.stack(qs), kb) + bias
            p = jnp.where(valid, jnp.exp(jnp.where(valid, s, NEG_BIG) - jnp.stack(lses)), 0.0)
            dp = _hnt(jnp.stack(dos), jnp.stack(vs))
            ds = p * (dp - jnp.stack(dcols))
            dq = _hnn(ds.astype(BF16), kb) * 0.125
            for G in range(4):
                dq_ref[rows, G * 128:(G + 1) * 128] = jnp.where(lo, dq[2 * G], dq[2 * G + 1])

    cur, prev = _query_step_specs(QB)
    return pl.pallas_call(
        body, name=name, grid=(d, nb // QB), in_specs=[cur, prev, cur, prev, cur, cur, cur, cur], out_specs=cur,
        out_shape=jax.ShapeDtypeStruct((L, d * ATTN_WIDTH), F32), compiler_params=_params(2),
    )(q, k, k, v, v, do, lse, dd)


def _attn_bwd_kv(q, k, v, do, lse, dd, d, name):
    L = q.shape[0]
    nb = L // ATTN_BLOCK
    B = ATTN_BLOCK
    KB = _blocks_per_step(nb)
    n_steps = nb // KB

    def body(k_ref, v_ref, qc_ref, qn_ref, doc_ref, don_ref, lsec_ref, lsen_ref, ddc_ref, ddn_ref, dk_ref, dv_ref):
        j = pl.program_id(1)
        qrow = lax.broadcasted_iota(jnp.int32, (2 * B, B), 0)
        kk = lax.broadcasted_iota(jnp.int32, (2 * B, B), 1)
        steps = qrow - kk
        band = (steps >= 0) & (steps <= B)
        lo2 = lax.broadcasted_iota(jnp.int32, (2 * B, 128), 1) < 64
        lo = lax.broadcasted_iota(jnp.int32, (B, 128), 1) < 64
        stepsf = steps.astype(F32)
        for sub in range(KB):
            rows = slice(sub * B, (sub + 1) * B)
            last = sub == KB - 1
            valid = band & ((qrow < B) | (j < n_steps - 1)) if last else band
            after = lambda cur_ref, nxt_ref, sl: nxt_ref[:, sl] if last else cur_ref[(sub + 1) * B:(sub + 2) * B, sl]
            for G in range(4):
                sl = slice(G * 128, (G + 1) * 128)
                kg = k_ref[rows, sl]
                vg = v_ref[rows, sl]
                qq = jnp.concatenate([qc_ref[rows, sl], after(qc_ref, qn_ref, sl)], axis=0)
                doo = jnp.concatenate([doc_ref[rows, sl], after(doc_ref, don_ref, sl)], axis=0)
                lse2 = jnp.concatenate([lsec_ref[rows, sl], after(lsec_ref, lsen_ref, sl)], axis=0)
                dd2 = jnp.concatenate([ddc_ref[rows, sl], after(ddc_ref, ddn_ref, sl)], axis=0)
                doo_b = doo.astype(BF16)
                dks, dvs = [], []
                for half in (0, 1):
                    msk = lo2 if half == 0 else jnp.logical_not(lo2)
                    qm = jnp.where(msk, qq, jnp.zeros_like(qq))
                    s = _nt(qm, kg) - (_slope(2 * G + half) * d) * stepsf
                    lse_c = _head_col(lse2, msk, True)
                    p = jnp.where(valid, jnp.exp(jnp.where(valid, s, NEG_BIG) - lse_c), 0.0)
                    dvs.append(_tn(p.astype(BF16), doo_b))
                    dom = jnp.where(msk, doo, 0.0).astype(BF16)
                    dp = _nt(dom, vg)
                    dcol = _head_col(dd2, msk, False)
                    ds = p * (dp - dcol)
                    dks.append(_tn(ds.astype(BF16), qq))
                dk_ref[rows, sl] = jnp.where(lo, dks[0], dks[1])
                dv_ref[rows, sl] = jnp.where(lo, dvs[0], dvs[1])

    cur = pl.BlockSpec((KB * B, ATTN_WIDTH), lambda r, j: (j, r))
    nxt = pl.BlockSpec((B, ATTN_WIDTH), lambda r, j: (jnp.minimum(KB * (j + 1), nb - 1), r))
    return pl.pallas_call(
        body, name=name, grid=(d, n_steps), in_specs=[cur, cur, cur, nxt, cur, nxt, cur, nxt, cur, nxt],
        out_specs=[cur, cur],
        out_shape=[jax.ShapeDtypeStruct((L, d * ATTN_WIDTH), F32)] * 2, compiler_params=_params(2),
    )(k, v, q, q, do, do, lse, lse, dd, dd)


CONV_T = 512
HALO = 8


def _conv_taps(pad_ref, w, T):
    acc = pad_ref[pl.ds(HALO - 3, T), :] * w[0:1, :]
    for j in range(1, CONV_WIDTH):
        acc = acc + pad_ref[pl.ds(HALO - 3 + j, T), :] * w[j:j + 1, :]
    return acc


def _conv_fwd(xq, xk, xv, conv_w):
    S = xq.shape[0]
    T = CONV_T

    def body(xq_ref, xqh_ref, xk_ref, xkh_ref, xv_ref, xvh_ref, wq_ref, wk_ref, wv_ref,
             qn_ref, kn_ref, v_ref, pad_ref):
        i = pl.program_id(0)

        def act(x_ref, xh_ref, w_ref):
            pad_ref[pl.ds(0, HALO), :] = jnp.where(i > 0, xh_ref[...], 0.0)
            pad_ref[pl.ds(HALO, T), :] = x_ref[...]
            c = _conv_taps(pad_ref, w_ref[...], T)
            return c * _sigmoid(c)

        def l2n(t):
            return t * lax.rsqrt(jnp.sum(t * t, axis=-1, keepdims=True) + L2_EPS)

        qn_ref[...] = l2n(act(xq_ref, xqh_ref, wq_ref))
        kn_ref[...] = l2n(act(xk_ref, xkh_ref, wk_ref))
        v_ref[...] = act(xv_ref, xvh_ref, wv_ref)

    tile = pl.BlockSpec((T, 128), lambda i, h: (i, h))
    halo = pl.BlockSpec((HALO, 128), lambda i, h: (jnp.maximum(i * (T // HALO) - 1, 0), h))
    wspec = lambda sec: pl.BlockSpec((CONV_WIDTH, 128), lambda i, h, sec=sec: (0, 4 * sec + h))
    return pl.pallas_call(
        body, name="dn_conv_fwd", grid=(S // T, DN_HEADS),
        in_specs=[tile, halo, tile, halo, tile, halo, wspec(0), wspec(1), wspec(2)],
        out_specs=[tile, tile, tile],
        out_shape=[jax.ShapeDtypeStruct((S, DN_WIDTH), F32)] * 3,
        scratch_shapes=[pltpu.VMEM((T + HALO, 128), F32)],
        compiler_params=_params(2),
    )(xq, xq, xk, xk, xv, xv, conv_w, conv_w, conv_w)


def _conv_bwd_pre(xq, xk, xv, conv_w, dqn, dkn, dv):
    S = xq.shape[0]
    T = CONV_T

    def body(xq_ref, xqh_ref, xk_ref, xkh_ref, xv_ref, xvh_ref, wq_ref, wk_ref, wv_ref,
             dqn_ref, dkn_ref, dv_ref, dcq_ref, dck_ref, dcv_ref, dwq_ref, dwk_ref, dwv_ref, pad_ref):
        i = pl.program_id(1)

        def one(x_ref, xh_ref, w_ref, dy_ref, dc_ref, dw_ref, normed):
            pad_ref[pl.ds(0, HALO), :] = jnp.where(i > 0, xh_ref[...], 0.0)
            pad_ref[pl.ds(HALO, T), :] = x_ref[...]
            c = _conv_taps(pad_ref, w_ref[...], T)
            sg = _sigmoid(c)
            a = c * sg
            dy = dy_ref[...]
            if normed:
                r = lax.rsqrt(jnp.sum(a * a, axis=-1, keepdims=True) + L2_EPS)
                y = a * r
                da = r * (dy - y * jnp.sum(dy * y, axis=-1, keepdims=True))
            else:
                da = dy
            dc = da * (sg * (1.0 + c * (1.0 - sg)))
            dc_ref[...] = dc

            @pl.when(i == 0)
            def _():
                dw_ref[...] = jnp.zeros_like(dw_ref)

            rows = [jnp.sum(dc * pad_ref[pl.ds(HALO - 3 + j, T), :], axis=0, keepdims=True) for j in range(CONV_WIDTH)]
            dw_ref[...] += jnp.concatenate(rows + [jnp.zeros((8 - CONV_WIDTH, 128), F32)], axis=0)

        one(xq_ref, xqh_ref, wq_ref, dqn_ref, dcq_ref, dwq_ref, True)
        one(xk_ref, xkh_ref, wk_ref, dkn_ref, dck_ref, dwk_ref, True)
        one(xv_ref, xvh_ref, wv_ref, dv_ref, dcv_ref, dwv_ref, False)

    tile = pl.BlockSpec((T, 128), lambda h, i: (i, h))
    halo = pl.BlockSpec((HALO, 128), lambda h, i: (jnp.maximum(i * (T // HALO) - 1, 0), h))
    wspec = lambda sec: pl.BlockSpec((CONV_WIDTH, 128), lambda h, i, sec=sec: (0, 4 * sec + h))
    dwspec = pl.BlockSpec((8, 128), lambda h, i: (0, h))
    return pl.pallas_call(
        body, name="dn_conv_bwd_pre", grid=(DN_HEADS, S // T),
        in_specs=[tile, halo, tile, halo, tile, halo, wspec(0), wspec(1), wspec(2), tile, tile, tile],
        out_specs=[tile, tile, tile, dwspec, dwspec, dwspec],
        out_shape=[jax.ShapeDtypeStruct((S, DN_WIDTH), F32)] * 3 + [jax.ShapeDtypeStruct((8, DN_WIDTH), F32)] * 3,
        scratch_shapes=[pltpu.VMEM((T + HALO, 128), F32)],
        compiler_params=_params(2),
    )(xq, xq, xk, xk, xv, xv, conv_w, conv_w, conv_w, dqn, dkn, dv)


def _conv_bwd_x(dcq, dck, dcv, conv_w):
    S = dcq.shape[0]
    T = CONV_T
    nt = S // T

    def body(dq_ref, dqh_ref, dk_ref, dkh_ref, dv_ref, dvh_ref, wq_ref, wk_ref, wv_ref,
             oq_ref, ok_ref, ov_ref, pad_ref):
        i = pl.program_id(0)

        def one(d_ref, dh_ref, w_ref, o_ref):
            pad_ref[pl.ds(0, T), :] = d_ref[...]
            pad_ref[pl.ds(T, HALO), :] = jnp.where(i < nt - 1, dh_ref[...], 0.0)
            w = w_ref[...]
            acc = pad_ref[pl.ds(3, T), :] * w[0:1, :]
            for j in range(1, CONV_WIDTH):
                acc = acc + pad_ref[pl.ds(3 - j, T), :] * w[j:j + 1, :]
            o_ref[...] = acc

        one(dq_ref, dqh_ref, wq_ref, oq_ref)
        one(dk_ref, dkh_ref, wk_ref, ok_ref)
        one(dv_ref, dvh_ref, wv_ref, ov_ref)

    tile = pl.BlockSpec((T, 128), lambda i, h: (i, h))
    halo = pl.BlockSpec((HALO, 128), lambda i, h: (jnp.minimum((i + 1) * (T // HALO), S // HALO - 1), h))
    wspec = lambda sec: pl.BlockSpec((CONV_WIDTH, 128), lambda i, h, sec=sec: (0, 4 * sec + h))
    return pl.pallas_call(
        body, name="dn_conv_bwd_x", grid=(nt, DN_HEADS),
        in_specs=[tile, halo, tile, halo, tile, halo, wspec(0), wspec(1), wspec(2)],
        out_specs=[tile, tile, tile],
        out_shape=[jax.ShapeDtypeStruct((S, DN_WIDTH), F32)] * 3,
        scratch_shapes=[pltpu.VMEM((T + HALO, 128), F32)],
        compiler_params=_params(2),
    )(dcq, dcq, dck, dck, dcv, dcv, conv_w, conv_w, conv_w)


PREP_CHUNKS = 4
SCAN_CHUNKS = 8


def _bnn(a, b):
    return lax.dot_general(a, b, (((2,), (1,)), ((0,), (0,))), preferred_element_type=F32, precision=HI)


def _bnt(a, b):
    return lax.dot_general(a, b, (((2,), (2,)), ((0,), (0,))), preferred_element_type=F32, precision=HI)


def _btn(a, b):
    return lax.dot_general(a, b, (((1,), (1,)), ((0,), (0,))), preferred_element_type=F32, precision=HI)


def _tri_inverse_b(a, blk, eye):
    dg = jnp.where(blk, a, 0.0)
    lo = a - dg
    d2 = _bnn(dg, dg)
    d4 = _bnn(d2, d2)
    d8 = _bnn(d4, d4)
    td = _bnn(_bnn(_bnn(eye - dg, eye + d2), eye + d4), eye + d8)
    b = _bnn(td, lo)
    b2 = _bnn(b, b)
    return _bnn(_bnn(eye - b, eye + b2), td)


def _dn_common_b(bds, avec, dvec, q_raw, k, v, t=None):
    C = DN_CHUNK
    lane = lax.broadcasted_iota(jnp.int32, (C, 128), 1)
    row = lax.broadcasted_iota(jnp.int32, (1, C, C), 1)
    col = lax.broadcasted_iota(jnp.int32, (1, C, C), 2)
    incl = row >= col
    strict = row > col
    eye = (row == col).astype(F32)
    blk = (row // 16) == (col // 16)
    pick = lambda tile, ln: jnp.sum(jnp.where(lane == ln, tile, 0.0), axis=-1, keepdims=True)
    betas, graws, zcs = [], [], []
    for bd in bds:
        z = bd + dvec
        g_all = -jnp.exp(avec) * (jnp.maximum(z, 0.0) + jnp.log(1.0 + jnp.exp(-jnp.abs(z))))
        beta_all = _sigmoid(bd)
        for h in range(DN_HEADS):
            betas.append(pick(beta_all, h))
            graws.append(pick(g_all, DN_HEADS + h))
            zcs.append(pick(z, DN_HEADS + h))
    beta, graw, zc = jnp.stack(betas), jnp.stack(graws), jnp.stack(zcs)
    to_row = lambda c: jnp.sum(eye * c, axis=1, keepdims=True)
    gc = jnp.sum(jnp.where(incl, to_row(graw), 0.0), axis=-1, keepdims=True)
    decay = jnp.exp(jnp.where(incl, gc - to_row(gc), NEG_BIG))
    q = q_raw * (DN_HEAD_DIM ** -0.5)
    kb = k * beta
    kk = _bnt(kb, k)
    if t is None:
        t = _tri_inverse_b(jnp.where(strict, kk * decay, 0.0), blk, eye)
    eg = jnp.exp(gc)
    rhs_w = kb * eg
    u = _bnn(t, v * beta)
    w = _bnn(t, rhs_w)
    qk = _bnt(q, k)
    aq = jnp.where(incl, qk * decay, 0.0)
    last = lax.broadcasted_iota(jnp.int32, (1, C, 1), 1) == C - 1
    g_last = jnp.sum(jnp.where(last, gc, 0.0), axis=1, keepdims=True)
    ekd = jnp.exp(g_last - gc)
    return dict(beta=beta, graw=graw, zc=zc, gc=gc, decay=decay, q=q, kb=kb, kk=kk, t=t, eg=eg, rhs_w=rhs_w,
                u=u, w=w, qk=qk, aq=aq, g_last=g_last, ekd=ekd, kd=k * ekd, qg=q * eg,
                incl=incl, strict=strict, eye=eye, lane=lane, row=row, col=col, last=last)


def _stack_heads(ref, rows):
    return jnp.stack([ref[rows, h * DN_HEAD_DIM:(h + 1) * DN_HEAD_DIM] for h in range(DN_HEADS)])


def _stack_units(ref, nc):
    C = DN_CHUNK
    return jnp.concatenate([_stack_heads(ref, slice(ci * C, (ci + 1) * C)) for ci in range(nc)], axis=0)


def _store_units(ref, val, nc):
    C = DN_CHUNK
    for ci in range(nc):
        for h in range(DN_HEADS):
            ref[ci * C:(ci + 1) * C, h * DN_HEAD_DIM:(h + 1) * DN_HEAD_DIM] = val[ci * DN_HEADS + h]


def _dn_prep(qn, kn, v, bd, avec, dvec):
    S = qn.shape[0]
    C = DN_CHUNK
    N = S // C
    nc = PREP_CHUNKS

    def body(q_ref, k_ref, v_ref, bd_ref, a_ref, d_ref, u_ref, w_ref, qg_ref, kd_ref, aq_ref, t_ref, egl_ref):
        bds = [bd_ref[ci * C:(ci + 1) * C, :] for ci in range(nc)]
        c = _dn_common_b(bds, a_ref[...], d_ref[...], _stack_units(q_ref, nc), _stack_units(k_ref, nc), _stack_units(v_ref, nc))
        _store_units(u_ref, c["u"], nc)
        _store_units(w_ref, c["w"], nc)
        _store_units(qg_ref, c["qg"], nc)
        _store_units(kd_ref, c["kd"], nc)
        egl = jnp.broadcast_to(jnp.exp(c["g_last"]), (nc * DN_HEADS, 1, 128))
        for ci in range(nc):
            for h in range(DN_HEADS):
                aq_ref[h, ci * C:(ci + 1) * C, :] = c["aq"][ci * DN_HEADS + h]
                t_ref[h, ci * C:(ci + 1) * C, :] = c["t"][ci * DN_HEADS + h]
            egl_ref[ci * 8:(ci + 1) * 8, :] = jnp.concatenate(
                [egl[ci * DN_HEADS + h] for h in range(DN_HEADS)] + [jnp.zeros((8 - DN_HEADS, 128), F32)], axis=0)

    tok = lambda w: pl.BlockSpec((nc * C, w), lambda n: (n, 0))
    sq = pl.BlockSpec((DN_HEADS, nc * C, C), lambda n: (0, n, 0))
    vec = pl.BlockSpec((1, 128), lambda n: (0, 0))
    return pl.pallas_call(
        body, name="dn_prep", grid=(N // nc,),
        in_specs=[tok(DN_WIDTH)] * 3 + [tok(128), vec, vec],
        out_specs=[tok(DN_WIDTH)] * 4 + [sq, sq, pl.BlockSpec((nc * 8, 128), lambda n: (n, 0))],
        out_shape=[jax.ShapeDtypeStruct((S, DN_WIDTH), F32)] * 4 + [jax.ShapeDtypeStruct((DN_HEADS, S, C), F32)] * 2
                  + [jax.ShapeDtypeStruct((N * 8, 128), F32)],
        compiler_params=_params(1),
    )(qn, kn, v, bd, avec, dvec)


def _dn_scan_fwd(u, w, qg, kd, aq, egl, gate, dn_gain):
    S = u.shape[0]
    C = DN_CHUNK
    N = S // C
    HD = DN_HEAD_DIM
    nc = SCAN_CHUNKS

    def body(u_ref, w_ref, qg_ref, kd_ref, aq_ref, egl_ref, gate_ref, gain_ref, dn_ref, o_ref, vn_ref, st_ref, state_ref):
        @pl.when(pl.program_id(0) == 0)
        def _():
            state_ref[...] = jnp.zeros_like(state_ref)

        gain = gain_ref[...]
        for ci in range(nc):
            rows = slice(ci * C, (ci + 1) * C)
            st = state_ref[...]
            for h in range(DN_HEADS):
                st_ref[ci * DN_WIDTH + h * HD:ci * DN_WIDTH + (h + 1) * HD, :] = st[h]
            v_new = _stack_heads(u_ref, rows) - _bnn(_stack_heads(w_ref, rows), st)
            o = _bnn(_stack_heads(qg_ref, rows), st) + _bnn(aq_ref[:, rows, :], v_new)
            egl = jnp.stack([egl_ref[ci * 8 + h:ci * 8 + h + 1, :] for h in range(DN_HEADS)])
            state_ref[...] = st * egl + _btn(_stack_heads(kd_ref, rows), v_new)
            r = lax.rsqrt(jnp.mean(o * o, axis=-1, keepdims=True) + NORM_EPS)
            gt = _stack_heads(gate_ref, rows)
            dn = o * r * gain * (gt * _sigmoid(gt))
            for h in range(DN_HEADS):
                sl = slice(h * HD, (h + 1) * HD)
                vn_ref[rows, sl] = v_new[h]
                o_ref[rows, sl] = o[h]
                dn_ref[rows, sl] = dn[h]

    tok = lambda wd: pl.BlockSpec((nc * C, wd), lambda n: (n, 0))
    sq = pl.BlockSpec((DN_HEADS, nc * C, C), lambda n: (0, n, 0))
    vec = pl.BlockSpec((1, 128), lambda n: (0, 0))
    return pl.pallas_call(
        body, name="dn_scan_fwd", grid=(N // nc,),
        in_specs=[tok(DN_WIDTH)] * 4 + [sq, pl.BlockSpec((nc * 8, 128), lambda n: (n, 0)), tok(DN_WIDTH), vec],
        out_specs=[tok(DN_WIDTH)] * 3 + [pl.BlockSpec((nc * DN_WIDTH, HD), lambda n: (n, 0))],
        out_shape=[jax.ShapeDtypeStruct((S, DN_WIDTH), F32)] * 3 + [jax.ShapeDtypeStruct((N * DN_WIDTH, HD), F32)],
        scratch_shapes=[pltpu.VMEM((DN_HEADS, HD, HD), F32)],
        compiler_params=_params(1),
    )(u, w, qg, kd, aq, egl, gate, dn_gain)


def _dn_scan_bwd(w, qg, kd, aq, egl, gate, dn_gain, o, ddn):
    S = w.shape[0]
    C = DN_CHUNK
    N = S // C
    HD = DN_HEAD_DIM
    nc = SCAN_CHUNKS

    def body(w_ref, qg_ref, kd_ref, aq_ref, egl_ref, gate_ref, gain_ref, o_ref, ddn_ref,
             do_ref, dvn_ref, dgate_ref, dst_ref, small_ref, dstate_ref):
        @pl.when(pl.program_id(0) == 0)
        def _():
            dstate_ref[...] = jnp.zeros_like(dstate_ref)
            small_ref[...] = jnp.zeros_like(small_ref)

        gain = gain_ref[...]
        d_gain = jnp.zeros((1, 128), F32)
        for ci in reversed(range(nc)):
            rows = slice(ci * C, (ci + 1) * C)
            dsn = dstate_ref[...]
            for h in range(DN_HEADS):
                dst_ref[ci * DN_WIDTH + h * HD:ci * DN_WIDTH + (h + 1) * HD, :] = dsn[h]
            ov = _stack_heads(o_ref, rows)
            r = lax.rsqrt(jnp.mean(ov * ov, axis=-1, keepdims=True) + NORM_EPS)
            on = ov * r
            gt = _stack_heads(gate_ref, rows)
            sgt = _sigmoid(gt)
            silu_g = gt * sgt
            dy = _stack_heads(ddn_ref, rows)
            d_gain = d_gain + jnp.sum(jnp.sum(dy * on * silu_g, axis=1, keepdims=True), axis=0)
            dgate = dy * on * gain * (sgt * (1.0 + gt * (1.0 - sgt)))
            don = dy * gain * silu_g
            do = r * (don - on * jnp.mean(don * on, axis=-1, keepdims=True))
            d_vnew = _btn(aq_ref[:, rows, :], do) + _bnn(_stack_heads(kd_ref, rows), dsn)
            egl = jnp.stack([egl_ref[ci * 8 + h:ci * 8 + h + 1, :] for h in range(DN_HEADS)])
            dstate_ref[...] = _btn(_stack_heads(qg_ref, rows), do) + dsn * egl - _btn(_stack_heads(w_ref, rows), d_vnew)
            for h in range(DN_HEADS):
                sl = slice(h * HD, (h + 1) * HD)
                do_ref[rows, sl] = do[h]
                dvn_ref[rows, sl] = d_vnew[h]
                dgate_ref[rows, sl] = dgate[h]
        small_ref[...] += jnp.concatenate([d_gain, jnp.zeros((7, 128), F32)], axis=0)

    nb = N // nc
    tok = lambda wd: pl.BlockSpec((nc * C, wd), lambda i: (nb - 1 - i, 0))
    sq = pl.BlockSpec((DN_HEADS, nc * C, C), lambda i: (0, nb - 1 - i, 0))
    vec = pl.BlockSpec((1, 128), lambda i: (0, 0))
    return pl.pallas_call(
        body, name="dn_scan_bwd", grid=(nb,),
        in_specs=[tok(DN_WIDTH)] * 3 + [sq, pl.BlockSpec((nc * 8, 128), lambda i: (nb - 1 - i, 0)), tok(DN_WIDTH), vec,
                                       tok(DN_WIDTH), tok(DN_WIDTH)],
        out_specs=[tok(DN_WIDTH)] * 3 + [pl.BlockSpec((nc * DN_WIDTH, HD), lambda i: (nb - 1 - i, 0)),
                                        pl.BlockSpec((8, 128), lambda i: (0, 0))],
        out_shape=[jax.ShapeDtypeStruct((S, DN_WIDTH), F32)] * 3 + [jax.ShapeDtypeStruct((N * DN_WIDTH, HD), F32),
                                                                  jax.ShapeDtypeStruct((8, 128), F32)],
        scratch_shapes=[pltpu.VMEM((DN_HEADS, HD, HD), F32)],
        compiler_params=_params(1),
    )(w, qg, kd, aq, egl, gate, dn_gain, o, ddn)


def _dn_post(qn, kn, v, bd, avec, dvec, t_inv, v_new_all, states, dstates, do_all, dvn_all, comm=None):
    S = qn.shape[0]
    C = DN_CHUNK
    N = S // C
    HD = DN_HEAD_DIM
    nc = PREP_CHUNKS
    B = nc * DN_HEADS

    def body(q_ref, k_ref, v_ref, bd_ref, a_ref, d_ref, t_ref, vn_ref, st_ref, dst_ref, do_ref, dvn_ref,
             dq_ref, dk_ref, dv_ref, dbd_ref, small_ref):
        @pl.when(pl.program_id(0) == 0)
        def _():
            small_ref[...] = jnp.zeros_like(small_ref)

        avec = a_ref[...]
        bds = [bd_ref[ci * C:(ci + 1) * C, :] for ci in range(nc)]
        k = _stack_units(k_ref, nc)
        vv = _stack_units(v_ref, nc)
        t = jnp.concatenate([t_ref[:, ci * C:(ci + 1) * C, :] for ci in range(nc)], axis=0)
        c = _dn_common_b(bds, avec, d_ref[...], _stack_units(q_ref, nc), k, vv, t=t)
        q, kb, eg, u, w = c["q"], c["kb"], c["eg"], c["u"], c["w"]
        beta, decay, incl, strict, eye = c["beta"], c["decay"], c["incl"], c["strict"], c["eye"]
        st = jnp.stack([st_ref[b * HD:(b + 1) * HD, :] for b in range(B)])
        dsn = jnp.stack([dst_ref[b * HD:(b + 1) * HD, :] for b in range(B)])
        v_new = _stack_units(vn_ref, nc)
        do = _stack_units(do_ref, nc)
        d_vnew = _stack_units(dvn_ref, nc)
        egl = jnp.exp(c["g_last"])
        daq = jnp.where(incl, _bnt(do, v_new), 0.0)
        d_qg = _bnt(do, st)
        d_kd = _bnt(v_new, dsn)
        d_glast = jnp.sum(jnp.sum(dsn * st, axis=-1, keepdims=True), axis=1, keepdims=True) * egl
        d_w = -_bnt(d_vnew, st)
        d_ru = _btn(t, d_vnew)
        d_rw = _btn(t, d_w)
        da = -jnp.where(strict, _bnt(d_ru, u) + _bnt(d_rw, w), 0.0)
        dv = d_ru * beta
        dbeta = jnp.sum(d_ru * vv, axis=-1, keepdims=True)
        dkb = d_rw * eg
        dgc = jnp.sum(d_rw * c["rhs_w"], axis=-1, keepdims=True)
        dkk = da * decay
        ddecay = da * c["kk"]
        dkb = dkb + _bnn(dkk, k)
        dk = _btn(dkk, kb)
        dqk = daq * decay
        ddecay = ddecay + daq * c["qk"]
        dq = _bnn(dqk, k)
        dk = dk + _btn(dqk, q)
        m = ddecay * decay
        col_sum = jnp.sum(m, axis=1, keepdims=True)
        dgc = dgc + jnp.sum(m, axis=-1, keepdims=True) - jnp.sum(eye * col_sum, axis=-1, keepdims=True)
        dq = dq + d_qg * eg
        dgc = dgc + jnp.sum(d_qg * c["qg"], axis=-1, keepdims=True)
        dk = dk + d_kd * c["ekd"]
        tk = jnp.sum(d_kd * c["kd"], axis=-1, keepdims=True)
        dgc = dgc - tk
        d_glast = d_glast + jnp.sum(tk, axis=1, keepdims=True)
        dk = dk + dkb * beta
        dbeta = dbeta + jnp.sum(dkb * k, axis=-1, keepdims=True)
        dgc = dgc + jnp.where(c["last"], d_glast, 0.0)
        dgc_row = jnp.sum(eye * dgc, axis=1, keepdims=True)
        dgraw = jnp.sum(jnp.where(c["col"] >= c["row"], dgc_row, 0.0), axis=-1, keepdims=True)
        _store_units(dq_ref, dq * (HD ** -0.5), nc)
        _store_units(dk_ref, dk, nc)
        _store_units(dv_ref, dv, nc)
        dbraw = dbeta * beta * (1.0 - beta)
        dzc = dgraw * _sigmoid(c["zc"])
        ga = dgraw * c["graw"]
        lane = c["lane"]
        lane1 = lax.broadcasted_iota(jnp.int32, (1, 128), 1)
        neg_ea = -jnp.exp(avec)
        d_alog = jnp.zeros((1, 128), F32)
        d_dt = jnp.zeros((1, 128), F32)
        for ci in range(nc):
            dbd = jnp.zeros((C, 128), F32)
            for h in range(DN_HEADS):
                b = ci * DN_HEADS + h
                dz = dzc[b] * neg_ea
                dbd = dbd + jnp.where(lane == h, dbraw[b], 0.0) + jnp.where(lane == DN_HEADS + h, dz, 0.0)
                d_alog = d_alog + jnp.where(lane1 == DN_HEADS + h, jnp.sum(ga[b], axis=0, keepdims=True), 0.0)
                d_dt = d_dt + jnp.where(lane1 == DN_HEADS + h, jnp.sum(dz, axis=0, keepdims=True), 0.0)
            dbd_ref[ci * C:(ci + 1) * C, :] = dbd
        small_ref[...] += jnp.concatenate([d_alog, d_dt, jnp.zeros((6, 128), F32)], axis=0)

    tok = lambda wd: pl.BlockSpec((nc * C, wd), lambda n: (n, 0))
    big = pl.BlockSpec((nc * DN_WIDTH, HD), lambda n: (n, 0))
    sq = pl.BlockSpec((DN_HEADS, nc * C, C), lambda n: (0, n, 0))
    vec = pl.BlockSpec((1, 128), lambda n: (0, 0))
    return _call(
        body, (qn, kn, v, bd, avec, dvec, t_inv, v_new_all, states, dstates, do_all, dvn_all),
        name="dn_post", grid=(N // nc,), comm=comm,
        in_specs=[tok(DN_WIDTH)] * 3 + [tok(128), vec, vec, sq, tok(DN_WIDTH), big, big, tok(DN_WIDTH), tok(DN_WIDTH)],
        out_specs=[tok(DN_WIDTH)] * 3 + [tok(128), pl.BlockSpec((8, 128), lambda n: (0, 0))],
        out_shape=[jax.ShapeDtypeStruct((S, DN_WIDTH), F32)] * 3 + [jax.ShapeDtypeStruct((S, 128), F32),
                                                                  jax.ShapeDtypeStruct((8, 128), F32)])


def _outproj_fwd(x, attn, dn, w_out):
    S, D = x.shape
    tm = 512

    def body(x_ref, a_ref, d_ref, w_ref, xo_ref, mix_ref):
        a = a_ref[...].astype(BF16)
        dd = d_ref[...].astype(BF16)
        mix_ref[:, 0:ATTN_WIDTH] = a
        mix_ref[:, ATTN_WIDTH:] = dd
        xo_ref[...] = x_ref[...] + _nn(a, w_ref[0:ATTN_WIDTH, :]) + _nn(dd, w_ref[ATTN_WIDTH:, :])

    tok = lambda w: pl.BlockSpec((tm, w), lambda i: (i, 0))
    return pl.pallas_call(
        body, name="outproj_fwd", grid=(S // tm,),
        in_specs=[tok(D), tok(ATTN_WIDTH), tok(DN_WIDTH), pl.BlockSpec((D, D), lambda i: (0, 0))],
        out_specs=[tok(D), tok(D)],
        out_shape=[jax.ShapeDtypeStruct((S, D), F32), jax.ShapeDtypeStruct((S, D), BF16)],
        compiler_params=_params(1),
    )(x, attn, dn, w_out)


def _outproj_bwd(dx, w_out, attn):
    S, D = dx.shape
    tm = VIEW_TILE

    def body(dx_ref, w_ref, attn_ref, da1, da4, da16, dl1, dl4, dl16, ddn_ref, dxb_ref, planes):
        d = dx_ref[...].astype(BF16)
        dxb_ref[...] = d
        da = _nt(d, w_ref[0:ATTN_WIDTH, :])
        ddn_ref[...] = _nt(d, w_ref[ATTN_WIDTH:, :])
        _tile_to_views(da, planes, (da1, da4, da16))
        lo = lax.broadcasted_iota(jnp.int32, (tm, 128), 1) < 64
        cols = []
        for G in range(4):
            sl = slice(G * 128, (G + 1) * 128)
            t = da[:, sl] * attn_ref[:, sl]
            d0 = jnp.sum(jnp.where(lo, t, 0.0), axis=-1, keepdims=True)
            d1 = jnp.sum(jnp.where(lo, 0.0, t), axis=-1, keepdims=True)
            cols.append(jnp.where(lo, d0, d1))
        _tile_to_views(jnp.concatenate(cols, axis=1), planes, (dl1, dl4, dl16))

    tok = lambda w: pl.BlockSpec((tm, w), lambda i: (i, 0))
    views = [_view_spec(d) for d in DILATIONS]
    return pl.pallas_call(
        body, name="outproj_bwd", grid=(S // tm,),
        in_specs=[tok(D), pl.BlockSpec((D, D), lambda i: (0, 0)), tok(ATTN_WIDTH)],
        out_specs=views + views + [tok(DN_WIDTH), tok(D)],
        out_shape=[_view_shape(S, d, F32) for d in DILATIONS] * 2
                  + [jax.ShapeDtypeStruct((S, DN_WIDTH), F32), jax.ShapeDtypeStruct((S, D), BF16)],
        scratch_shapes=[pltpu.VMEM((4, tm, 128), F32)],
        compiler_params=_params(1),
    )(dx, w_out, attn)


def _loss_head(x, gain, target):
    S, D = x.shape
    tm = 512

    def body(x_ref, gain_ref, t_ref, loss_ref, dx_ref, dgain_ref):
        @pl.when(pl.program_id(0) == 0)
        def _():
            loss_ref[...] = jnp.zeros_like(loss_ref)
            dgain_ref[...] = jnp.zeros_like(dgain_ref)

        xf = x_ref[...]
        gain = gain_ref[...]
        r = lax.rsqrt(jnp.mean(xf * xf, axis=-1, keepdims=True) + NORM_EPS)
        xhat = xf * r
        err = xhat * gain - t_ref[...]
        part = 0.5 * jnp.sum(jnp.mean(err * err, axis=-1, keepdims=True), axis=0, keepdims=True)
        first = (lax.broadcasted_iota(jnp.int32, (8, 128), 0) == 0) & (lax.broadcasted_iota(jnp.int32, (8, 128), 1) == 0)
        loss_ref[...] += jnp.where(first, part, 0.0)
        dy = err * (1.0 / D)
        dgain_ref[...] += jnp.sum(dy * xhat, axis=0, keepdims=True)
        dxh = dy * gain
        dx_ref[...] = r * (dxh - xhat * jnp.mean(dxh * xhat, axis=-1, keepdims=True))

    tok = pl.BlockSpec((tm, D), lambda i: (i, 0))
    row = pl.BlockSpec((1, D), lambda i: (0, 0))
    return pl.pallas_call(
        body, name="loss_head", grid=(S // tm,),
        in_specs=[tok, row, tok],
        out_specs=[pl.BlockSpec((8, 128), lambda i: (0, 0)), tok, row],
        out_shape=[jax.ShapeDtypeStruct((8, 128), F32), jax.ShapeDtypeStruct((S, D), F32),
                   jax.ShapeDtypeStruct((1, D), F32)],
        compiler_params=_params(1),
    )(x, gain, target)


def _adamw(w, g, m, v, name):
    R, Ccols = w.shape
    tr = R
    for cand in (256, 128, 64, 32, 16, 8):
        if R % cand == 0:
            tr = cand
            break
    c1 = 1.0 - ADAM_B1 ** ADAM_STEP
    c2 = 1.0 - ADAM_B2 ** ADAM_STEP

    def body(w_ref, g_ref, m_ref, v_ref, d_ref, nm_ref, nv_ref):
        gv = g_ref[...]
        mn = ADAM_B1 * m_ref[...] + (1.0 - ADAM_B1) * gv
        vn = ADAM_B2 * v_ref[...] + (1.0 - ADAM_B2) * (gv * gv)
        nm_ref[...] = mn
        nv_ref[...] = vn
        d_ref[...] = -ADAM_LR * ((mn / c1) / (jnp.sqrt(vn / c2) + ADAM_EPS) + ADAM_WD * w_ref[...])

    spec = pl.BlockSpec((tr, Ccols), lambda i: (i, 0))
    return pl.pallas_call(
        body, name=name, grid=(R // tr,), in_specs=[spec] * 4, out_specs=[spec] * 3,
        out_shape=[jax.ShapeDtypeStruct((R, Ccols), F32)] * 3, compiler_params=_params(1),
    )(w, g, m, v)


LATE_WEIGHTS = ("w_out", "ffn2_gate", "ffn2_up", "ffn2_down")


def _local_step(x, target, wts, small, dist=None):
    g1, g2, gm, gf = small["norm_ffn1"], small["norm_ffn2"], small["norm_mix"], small["norm_final"]
    wts = dict(wts)

    def reduce_start(gs, tag):
        return _rs_add_pairs(gs, _swap_sibling(gs, True, "rs_swap_halves_" + tag), dist["c"], "rs_add_pairs_" + tag)

    (x1, h1, fg1, fu1), late = _ffn_fwd(x, g1, wts["ffn1_gate"], wts["ffn1_up"], wts["ffn1_down"], "ffn1_fwd",
                                        comm=_ag_comm(dist["late"]) if dist else None)
    if dist:
        wts.update(zip(LATE_WEIGHTS, late))
        wts["w_out"] = wts["w_out"].reshape(D_MODEL, D_MODEL)
    h2, *qkv, xq, xk, xv, gate, bd = _inproj_fwd(x1, gm, wts["w_in"])
    aq, ak, av = qkv[0:3], qkv[3:6], qkv[6:9]
    parts = [_attn_fwd(aq[p], ak[p], av[p], d, f"attn_fwd_d{d}") for p, d in enumerate(DILATIONS)]
    attn, *lse = _attn_merge(parts)
    conv_w = small["conv_w"]
    qn, kn, vv = _conv_fwd(xq, xk, xv, conv_w)
    dn_u, dn_w, dn_qg, dn_kd, dn_aq, dn_t, dn_egl = _dn_prep(qn, kn, vv, bd, small["avec"], small["dvec"])
    dn, o_dn, v_new, states = _dn_scan_fwd(dn_u, dn_w, dn_qg, dn_kd, dn_aq, dn_egl, gate, small["dn_norm"])
    x2, mix = _outproj_fwd(x1, attn, dn, wts["w_out"])
    (x3, h3, fg2, fu2), _ = _ffn_fwd(x2, g2, wts["ffn2_gate"], wts["ffn2_up"], wts["ffn2_down"], "ffn2_fwd")
    loss, dx3, d_gf = _loss_head(x3, gf, target)

    grads = {}
    (dx2, d_g2, dfg2, dfu2, act2, dout2), _ = _ffn_bwd(dx3, x2, g2, fg2, fu2, wts["ffn2_down"], wts["ffn2_gate"],
                                                      wts["ffn2_up"], "ffn2_bwd")
    tk = 2048
    grads["ffn2_gate"], _ = _dw_chunks(dfg2, h3, tk, "dw_ffn2_gate")
    grads["ffn2_up"], _ = _dw_chunks(dfu2, h3, tk, "dw_ffn2_up")
    grads["ffn2_down"], _ = _dw_chunks(act2, dout2, tk, "dw_ffn2_down")
    group_a = ("ffn2_gate", "ffn2_up", "ffn2_down")
    parts_a = reduce_start([grads[n] for n in group_a], "a") if dist else None

    *dviews, ddn, dx2b = _outproj_bwd(dx2, wts["w_out"], attn)
    dattn, dd = dviews[0:3], dviews[3:6]
    grads["w_out"] = _matmul_tn(mix, dx2b, D_MODEL, tk, "dw_out").reshape(N_CHIPS, D_MODEL // N_CHIPS, D_MODEL)

    daq, dak, dav = [], [], []
    for p, d in enumerate(DILATIONS):
        daq.append(_attn_bwd_q(aq[p], ak[p], av[p], dattn[p], lse[p], dd[p], d, f"attn_bwd_q_d{d}"))
        dk_p, dv_p = _attn_bwd_kv(aq[p], ak[p], av[p], dattn[p], lse[p], dd[p], d, f"attn_bwd_kv_d{d}")
        dak.append(dk_p)
        dav.append(dv_p)

    do_dn, dvn, dgate, dstates, d_dn_gain = _dn_scan_bwd(dn_w, dn_qg, dn_kd, dn_aq, dn_egl, gate, small["dn_norm"], o_dn, ddn)
    (dqn, dkn, dvv, dbd, dn_small), recv_a = _dn_post(qn, kn, vv, bd, small["avec"], small["dvec"], dn_t, v_new, states,
                                                      dstates, do_dn, dvn, comm=_rsx_comm(parts_a) if dist else None)
    dcq, dck, dcv, dwq, dwk, dwv = _conv_bwd_pre(xq, xk, xv, conv_w, dqn, dkn, dvv)
    dxq, dxk, dxv = _conv_bwd_x(dcq, dck, dcv, conv_w)
    d_conv = jnp.concatenate([dwq[:CONV_WIDTH], dwk[:CONV_WIDTH], dwv[:CONV_WIDTH]], axis=1)

    dx1, d_gm, dproj = _inproj_bwd(dx2, x1, gm, [daq, dak, dav], [dxq, dxk, dxv, dgate], dbd, wts["w_in"])
    gi = _matmul_tn(dproj, h2, IN_COLS_PADDED, 512, "dw_in")
    if dist:
        gi = jnp.concatenate([gi[:3072], gi[3584:3592], gi[3072:3584]], axis=0).reshape(N_CHIPS, IN_COLS // N_CHIPS, D_MODEL)
        gi = jnp.pad(gi, ((0, 0), (0, W_IN_ROWS - IN_COLS // N_CHIPS), (0, 0)))
    grads["w_in"] = gi
    group_b = ("w_in", "w_out")
    parts_b = reduce_start([grads[n] for n in group_b], "b") if dist else None

    (dx0, d_g1, dfg1, dfu1, act1, dout1), _ = _ffn_bwd(dx1, x, g1, fg1, fu1, wts["ffn1_down"], wts["ffn1_gate"],
                                                      wts["ffn1_up"], "ffn1_bwd")
    group_c = ("ffn1_gate", "ffn1_up", "ffn1_down")
    pending = parts_b if dist else []
    parts_c, recv_bc = [], []
    for n, (lhs, rhs) in zip(group_c, ((dfg1, h1), (dfu1, h1), (act1, dout1))):
        grads[n], landed = _dw_chunks(lhs, rhs, tk, "dw_" + n, comm=_rsx_comm(pending) if dist else None)
        recv_bc += list(landed)
        if dist:
            pending = reduce_start([grads[n]], n)
            parts_c += pending

    small_grads = dict(norm_ffn1=d_g1, norm_mix=d_gm, norm_ffn2=d_g2, norm_final=d_gf, conv_w=d_conv,
                       a_log=dn_small[0:1], dt_bias=dn_small[1:2], dn_norm=d_dn_gain[0:1])
    if dist:
        recv_bc += _rs_exchange_arrays(pending)
        recv_b, recv_c = recv_bc[:len(parts_b)], recv_bc[len(parts_b):]
        names = group_a + group_b + group_c
        totals = _rs_add_totals(list(parts_a) + list(parts_b) + list(parts_c), list(recv_a) + list(recv_b) + list(recv_c),
                                dist["chip"])
        theirs = _swap_sibling(totals, False, "rs_share_total")
        grads = {n: (mine, other) for n, mine, other in zip(names, totals, theirs)}
    return loss, dx0, grads, small_grads


PACK_SECTIONS = (("ffn1_gate", 704), ("ffn1_up", 704), ("ffn1_down", 704), ("w_in", 898), ("w_out", 256),
                 ("ffn2_gate", 704), ("ffn2_up", 704), ("ffn2_down", 704))
PACK_ROWS = 5408
HALF_ROWS = PACK_ROWS // 2
ADD_ROWS = 208

HBM = pl.BlockSpec(memory_space=pl.ANY)
VMEM_SPEC = pl.BlockSpec(memory_space=pltpu.VMEM)


def _coords():
    return lax.axis_index("x"), lax.axis_index("y"), lax.axis_index("c")


def _remote(src, dst, send_sems, recv_sems, k, dev):
    return pltpu.make_async_remote_copy(src_ref=src, dst_ref=dst, send_sem=send_sems.at[k], recv_sem=recv_sems.at[k],
                                        device_id=dev, device_id_type=MESH)


def _allreduce_small(buf, name):
    R, Cc = buf.shape

    def body(src_ref, out_ref, recv_ref, send_sems, recv_sems):
        x, y, c = _coords()
        copies = []
        for m in range(1, 8):
            fx, fy, fc = (m >> 2) & 1, (m >> 1) & 1, m & 1
            dev = (x ^ fx if fx else x, y ^ fy if fy else y, c ^ fc if fc else c)
            cp = _remote(src_ref, recv_ref.at[m - 1], send_sems, recv_sems, m - 1, dev)
            cp.start()
            copies.append(cp)
        for cp in copies:
            cp.wait()
        r = [src_ref[...]] + [recv_ref[m] for m in range(7)]
        out_ref[...] = ((r[0] + r[1]) + (r[2] + r[3])) + ((r[4] + r[5]) + (r[6] + r[7]))

    return pl.pallas_call(
        body, name=name, out_shape=jax.ShapeDtypeStruct((R, Cc), F32),
        in_specs=[VMEM_SPEC], out_specs=VMEM_SPEC,
        scratch_shapes=[pltpu.VMEM((7, R, Cc), F32), pltpu.SemaphoreType.DMA((7,)), pltpu.SemaphoreType.DMA((7,))],
    )(buf)


def _allgather_weights(pack2):
    _, Hh, Cc = pack2.shape

    def body(src_ref, out_ref, send_sems, recv_sems):
        x, y, c = _coords()
        sib = (x, y, 1 - c)
        others = [(1 - x, y), (x, 1 - y), (1 - x, 1 - y)]
        blk = lambda cx, cy, half: out_ref.at[2 * cx + cy, half]
        mine = _remote(src_ref, out_ref.at[2 * x + y], send_sems, recv_sems, 6, sib)
        mine.start()
        first = [_remote(src_ref.at[c], blk(x, y, c), send_sems, recv_sems, j, (ox, oy, c)) for j, (ox, oy) in enumerate(others)]
        for cp in first:
            cp.start()
        passed = [_remote(blk(ox, oy, c), blk(ox, oy, c), send_sems, recv_sems, 3 + j, sib) for j, (ox, oy) in enumerate(others)]
        for j, (ox, oy) in enumerate(others):
            _remote(src_ref.at[c], blk(ox, oy, c), send_sems, recv_sems, j, (ox, oy, c)).wait_recv()
            passed[j].start()
        for j, (ox, oy) in enumerate(others):
            _remote(src_ref.at[c], blk(ox, oy, 1 - c), send_sems, recv_sems, 3 + j, sib).wait_recv()
        for cp in first + passed:
            cp.wait_send()
        mine.wait()

    return pl.pallas_call(
        body, name="allgather_weights", out_shape=jax.ShapeDtypeStruct((N_CHIPS, 2, Hh, Cc), pack2.dtype),
        in_specs=[HBM], out_specs=HBM,
        scratch_shapes=[pltpu.SemaphoreType.DMA((7,)), pltpu.SemaphoreType.DMA((7,))],
    )(pack2)


def _rs_swap_halves(gpack):
    _, nj, Hh, Cc = gpack.shape

    def body(src_ref, out_ref, send_sems, recv_sems):
        x, y, c = _coords()
        cp = _remote(src_ref.at[1 - c], out_ref, send_sems, recv_sems, 0, (x, y, 1 - c))
        cp.start()
        cp.wait()

    return pl.pallas_call(
        body, name="rs_swap_halves", out_shape=jax.ShapeDtypeStruct((nj, Hh, Cc), gpack.dtype),
        in_specs=[HBM], out_specs=HBM,
        scratch_shapes=[pltpu.SemaphoreType.DMA((1,)), pltpu.SemaphoreType.DMA((1,))],
    )(gpack)


def _rs_add_pair(gpack, other, c):
    _, nj, Hh, Cc = gpack.shape
    tr = ADD_ROWS

    def body(c_ref, a_ref, b_ref, o_ref):
        o_ref[...] = (a_ref[...] + b_ref[...]).astype(BF16)

    return pl.pallas_call(
        body, name="rs_add_pair",
        grid_spec=pltpu.PrefetchScalarGridSpec(
            num_scalar_prefetch=1, grid=(nj, Hh // tr),
            in_specs=[pl.BlockSpec((None, None, tr, Cc), lambda j, i, c_ref: (c_ref[0], j, i, 0)),
                      pl.BlockSpec((None, tr, Cc), lambda j, i, c_ref: (j, i, 0))],
            out_specs=pl.BlockSpec((None, tr, Cc), lambda j, i, c_ref: (j, i, 0))),
        out_shape=jax.ShapeDtypeStruct((nj, Hh, Cc), BF16),
        compiler_params=_params(2),
    )(c, gpack, other)


def _rs_exchange_chips(part):
    nj, Hh, Cc = part.shape

    def body(src_ref, out_ref, send_sems, recv_sems):
        x, y, c = _coords()
        others = [(1 - x, y), (x, 1 - y), (1 - x, 1 - y)]
        cps = [_remote(src_ref.at[2 * ox + oy], out_ref.at[k], send_sems, recv_sems, k, (ox, oy, c))
               for k, (ox, oy) in enumerate(others)]
        for cp in cps:
            cp.start()
        for cp in cps:
            cp.wait()

    return pl.pallas_call(
        body, name="rs_exchange_chips", out_shape=jax.ShapeDtypeStruct((3, Hh, Cc), part.dtype),
        in_specs=[HBM], out_specs=HBM,
        scratch_shapes=[pltpu.SemaphoreType.DMA((3,)), pltpu.SemaphoreType.DMA((3,))],
    )(part)


def _rs_add_total(part, recv, chip):
    nj, Hh, Cc = part.shape
    tr = ADD_ROWS

    def body(chip_ref, p_ref, r0_ref, r1_ref, r2_ref, o_ref):
        f = lambda r: r[...].astype(F32)
        o_ref[...] = (f(p_ref) + f(r0_ref)) + (f(r1_ref) + f(r2_ref))

    rk = lambda k: pl.BlockSpec((None, tr, Cc), lambda i, chip_ref, k=k: (k, i, 0))
    return pl.pallas_call(
        body, name="rs_add_total",
        grid_spec=pltpu.PrefetchScalarGridSpec(
            num_scalar_prefetch=1, grid=(Hh // tr,),
            in_specs=[pl.BlockSpec((None, tr, Cc), lambda i, chip_ref: (chip_ref[0], i, 0)), rk(0), rk(1), rk(2)],
            out_specs=pl.BlockSpec((tr, Cc), lambda i, chip_ref: (i, 0))),
        out_shape=jax.ShapeDtypeStruct((Hh, Cc), F32),
        compiler_params=_params(1),
    )(chip, part, recv, recv, recv)


def _rs_share_total(total):
    Hh, Cc = total.shape

    def body(src_ref, out_ref, send_sems, recv_sems):
        x, y, c = _coords()
        cp = _remote(src_ref, out_ref, send_sems, recv_sems, 0, (x, y, 1 - c))
        cp.start()
        cp.wait()

    return pl.pallas_call(
        body, name="rs_share_total", out_shape=jax.ShapeDtypeStruct((Hh, Cc), total.dtype),
        in_specs=[HBM], out_specs=HBM,
        scratch_shapes=[pltpu.SemaphoreType.DMA((1,)), pltpu.SemaphoreType.DMA((1,))],
    )(total)


BIG = ("ffn1_gate", "ffn1_up", "ffn1_down", "w_in", "w_out", "ffn2_gate", "ffn2_up", "ffn2_down")
W_IN_ROWS = 960


def _rows(ref, start, size):
    return ref.at[pl.ds(pl.multiple_of(start, 16), size)]


def _allgather_arrays(shards):
    n = len(shards)

    def body(*refs):
        _ag_start(refs[:n], refs[n:2 * n], refs[2 * n], refs[2 * n + 1])
        _ag_finish(refs[:n], refs[n:2 * n], refs[2 * n], refs[2 * n + 1])

    _, shapes, n_sems, _, _ = _ag_comm(shards)
    return pl.pallas_call(
        body, name="allgather_weights", out_shape=shapes, in_specs=[HBM] * n, out_specs=[HBM] * n,
        scratch_shapes=[pltpu.SemaphoreType.DMA((n_sems,)), pltpu.SemaphoreType.DMA((n_sems,))],
    )(*shards)


def _ag_copies(srcs, outs, send_sems, recv_sems):
    x, y, c = _coords()
    sib = (x, y, 1 - c)
    me = 2 * x + y
    others = [(1 - x, y), (x, 1 - y), (1 - x, 1 - y)]
    plan = []
    for a, (src, out) in enumerate(zip(srcs, outs)):
        h = src.shape[0] // 2
        cp = lambda s, d, k, dev: _remote(s, d, send_sems, recv_sems, 7 * a + k, dev)
        own = cp(src, out.at[me], 6, sib)
        sends = [cp(_rows(src, c * h, h), _rows(out.at[me], c * h, h), j, (ox, oy, c)) for j, (ox, oy) in enumerate(others)]
        mine = [_rows(out.at[2 * ox + oy], c * h, h) for ox, oy in others]
        theirs = [_rows(out.at[2 * ox + oy], (1 - c) * h, h) for ox, oy in others]
        arrivals = [cp(m, m, j, sib) for j, m in enumerate(mine)]
        forwards = [cp(m, m, 3 + j, sib) for j, m in enumerate(mine)]
        forwarded = [cp(t, t, 3 + j, sib) for j, t in enumerate(theirs)]
        plan.append((own, sends, forwards, arrivals, forwarded))
    return plan


def _ag_start(srcs, outs, send_sems, recv_sems):
    for own, sends, _, _, _ in _ag_copies(srcs, outs, send_sems, recv_sems):
        own.start()
        for cp in sends:
            cp.start()


def _ag_finish(srcs, outs, send_sems, recv_sems):
    plan = _ag_copies(srcs, outs, send_sems, recv_sems)
    for _, _, forwards, arrivals, _ in plan:
        for arrived, fwd in zip(arrivals, forwards):
            arrived.wait_recv()
            fwd.start()
    for own, sends, forwards, _, forwarded in plan:
        for cp in forwarded:
            cp.wait_recv()
        own.wait_recv()
        for cp in [own] + sends + forwards:
            cp.wait_send()


def _ag_comm(shards):
    shapes = [jax.ShapeDtypeStruct((N_CHIPS,) + s.shape, s.dtype) for s in shards]
    return (list(shards), shapes, 7 * len(shards), _ag_start, _ag_finish)


def _swap_sibling(arrs, pick_other_half, name):
    n = len(arrs)
    outs = [jax.ShapeDtypeStruct((a.shape[0], a.shape[1] // 2) + a.shape[2:] if pick_other_half else a.shape, a.dtype) for a in arrs]

    def body(*refs):
        srcs, dsts, send_sems, recv_sems = refs[:n], refs[n:2 * n], refs[2 * n], refs[2 * n + 1]
        x, y, c = _coords()
        cps = []
        for a in range(n):
            src = srcs[a]
            if pick_other_half:
                h = src.shape[1] // 2
                src = src.at[:, pl.ds(pl.multiple_of((1 - c) * h, 16), h)]
            cp = _remote(src, dsts[a], send_sems, recv_sems, a, (x, y, 1 - c))
            cp.start()
            cps.append(cp)
        for cp in cps:
            cp.wait()

    return pl.pallas_call(
        body, name=name, out_shape=outs, in_specs=[HBM] * n, out_specs=[HBM] * n,
        scratch_shapes=[pltpu.SemaphoreType.DMA((n,)), pltpu.SemaphoreType.DMA((n,))],
    )(*arrs)


def _rs_add_pairs(gs, others, c, name):
    n = len(gs)
    hbs = [g.shape[1] // 4 for g in gs]

    def body(c_ref, *refs):
        for a in range(n):
            refs[2 * n + a][...] = (refs[a][...] + refs[n + a][...]).astype(BF16)

    mine = lambda hb: pl.BlockSpec((None, hb, D_MODEL), lambda j, s, c_ref: (j, c_ref[0] * 2 + s, 0))
    flat = lambda hb: pl.BlockSpec((None, hb, D_MODEL), lambda j, s, c_ref: (j, s, 0))
    return pl.pallas_call(
        body, name=name,
        grid_spec=pltpu.PrefetchScalarGridSpec(
            num_scalar_prefetch=1, grid=(N_CHIPS, 2),
            in_specs=[mine(hb) for hb in hbs] + [flat(hb) for hb in hbs],
            out_specs=[flat(hb) for hb in hbs]),
        out_shape=[jax.ShapeDtypeStruct(o.shape, BF16) for o in others],
        compiler_params=_params(2),
    )(c, *gs, *others)


def _rs_exchange_arrays(parts):
    n = len(parts)

    def body(*refs):
        _rsx_start(refs[:n], refs[n:2 * n], refs[2 * n], refs[2 * n + 1])
        _rsx_finish(refs[:n], refs[n:2 * n], refs[2 * n], refs[2 * n + 1])

    _, shapes, n_sems, _, _ = _rsx_comm(parts)
    return pl.pallas_call(
        body, name="rs_exchange_chips", out_shape=shapes, in_specs=[HBM] * n, out_specs=[HBM] * n,
        scratch_shapes=[pltpu.SemaphoreType.DMA((n_sems,)), pltpu.SemaphoreType.DMA((n_sems,))],
    )(*parts)


def _rsx_copies(srcs, dsts, send_sems, recv_sems):
    x, y, c = _coords()
    others = [(1 - x, y), (x, 1 - y), (1 - x, 1 - y)]
    return [_remote(src.at[2 * ox + oy], dst.at[k], send_sems, recv_sems, 3 * a + k, (ox, oy, c))
            for a, (src, dst) in enumerate(zip(srcs, dsts)) for k, (ox, oy) in enumerate(others)]


def _rsx_start(srcs, dsts, send_sems, recv_sems):
    for cp in _rsx_copies(srcs, dsts, send_sems, recv_sems):
        cp.start()


def _rsx_finish(srcs, dsts, send_sems, recv_sems):
    for cp in _rsx_copies(srcs, dsts, send_sems, recv_sems):
        cp.wait()


def _rsx_comm(parts):
    shapes = [jax.ShapeDtypeStruct((3,) + p.shape[1:], p.dtype) for p in parts]
    return (list(parts), shapes, 3 * len(parts), _rsx_start, _rsx_finish)


def _rs_add_totals(parts, recvs, chip):
    n = len(parts)
    hbs = [p.shape[1] // 2 for p in parts]

    def body(chip_ref, *refs):
        f = lambda r: r[...].astype(F32)
        for a in range(n):
            p, r0, r1, r2 = refs[a], refs[n + 3 * a], refs[n + 3 * a + 1], refs[n + 3 * a + 2]
            refs[4 * n + a][...] = (f(p) + f(r0)) + (f(r1) + f(r2))

    own = lambda hb: pl.BlockSpec((None, hb, D_MODEL), lambda s, chip_ref: (chip_ref[0], s, 0))
    slot = lambda hb, k: pl.BlockSpec((None, hb, D_MODEL), lambda s, chip_ref, k=k: (k, s, 0))
    recv_specs = [slot(hb, k) for hb in hbs for k in range(3)]
    recv_args = [r for r in recvs for _ in range(3)]
    return pl.pallas_call(
        body, name="rs_add_totals",
        grid_spec=pltpu.PrefetchScalarGridSpec(
            num_scalar_prefetch=1, grid=(2,),
            in_specs=[own(hb) for hb in hbs] + recv_specs,
            out_specs=[pl.BlockSpec((hb, D_MODEL), lambda s, chip_ref: (s, 0)) for hb in hbs]),
        out_shape=[jax.ShapeDtypeStruct(p.shape[1:], F32) for p in parts],
        compiler_params=_params(1),
    )(chip, *parts, *recv_args)


def _permute_w_in(w):
    return jnp.concatenate([w[:, :3072], w[:, 3080:IN_COLS], w[:, 3072:3080],
                            jnp.zeros((w.shape[0], IN_COLS_PADDED - IN_COLS), w.dtype)], axis=1)


def _pack_rows(shards, dtype):
    rows = [shards[n].astype(dtype).reshape(r, D_MODEL) for n, r in PACK_SECTIONS]
    used = sum(r for _, r in PACK_SECTIONS)
    rows.append(jnp.zeros((PACK_ROWS - used, D_MODEL), dtype))
    return jnp.concatenate(rows, axis=0)


def _unpack_rows(pack, shapes):
    out, at = {}, 0
    for n, r in PACK_SECTIONS:
        out[n] = pack[at:at + r].reshape(shapes[n])
        at += r
    return out


SHARD_SHAPES = dict(ffn1_gate=(1024, 704), ffn1_up=(1024, 704), ffn1_down=(704, 1024), w_in=(1024, 898),
                    w_out=(256, 1024), ffn2_gate=(1024, 704), ffn2_up=(1024, 704), ffn2_down=(704, 1024))
ROW_SHARDED = ("ffn1_down", "w_out", "ffn2_down")
SMALL_ROWS = 16


def _pad_row(v):
    v = v.reshape(1, -1)
    return jnp.pad(v, ((0, 0), (0, D_MODEL - v.shape[1])))


def kernel(x, norm_ffn1, ffn1_gate, ffn1_up, ffn1_down, norm_mix, w_in, conv_w, a_log, dt_bias, dn_norm, w_out, norm_ffn2, ffn2_gate, ffn2_up, ffn2_down, norm_final, loss_target, m_norm_ffn1, m_ffn1_gate, m_ffn1_up, m_ffn1_down, m_norm_mix, m_w_in, m_conv_w, m_a_log, m_dt_bias, m_dn_norm, m_w_out, m_norm_ffn2, m_ffn2_gate, m_ffn2_up, m_ffn2_down, m_norm_final, v_norm_ffn1, v_ffn1_gate, v_ffn1_up, v_ffn1_down, v_norm_mix, v_w_in, v_conv_w, v_a_log, v_dt_bias, v_dn_norm, v_w_out, v_norm_ffn2, v_ffn2_gate, v_ffn2_up, v_ffn2_down, v_norm_final):
    cx, cy, cc = _coords()
    chip = 2 * cx + cy
    big_w = dict(ffn1_gate=ffn1_gate[0], ffn1_up=ffn1_up[0], ffn1_down=ffn1_down[0], w_in=w_in[0], w_out=w_out[0],
                 ffn2_gate=ffn2_gate[0], ffn2_up=ffn2_up[0], ffn2_down=ffn2_down[0])
    big_m = dict(ffn1_gate=m_ffn1_gate[0], ffn1_up=m_ffn1_up[0], ffn1_down=m_ffn1_down[0], w_in=m_w_in[0], w_out=m_w_out[0],
                 ffn2_gate=m_ffn2_gate[0], ffn2_up=m_ffn2_up[0], ffn2_down=m_ffn2_down[0])
    big_v = dict(ffn1_gate=v_ffn1_gate[0], ffn1_up=v_ffn1_up[0], ffn1_down=v_ffn1_down[0], w_in=v_w_in[0], w_out=v_w_out[0],
                 ffn2_gate=v_ffn2_gate[0], ffn2_up=v_ffn2_up[0], ffn2_down=v_ffn2_down[0])

    early = tuple(n for n in BIG if n not in LATE_WEIGHTS)
    wts = dict(zip(early, _allgather_arrays([big_w[n].astype(BF16) for n in early])))
    wts["w_in"] = _permute_w_in(jnp.concatenate([wts["w_in"][j] for j in range(N_CHIPS)], axis=1))
    dist = dict(late=[big_w[n].astype(BF16) for n in LATE_WEIGHTS], c=cc.reshape(1).astype(jnp.int32),
                chip=chip.reshape(1).astype(jnp.int32))

    conv_shard = conv_w[0]
    emb = jnp.concatenate([jnp.where((chip == j) & (cc == 0), conv_shard, 0.0) for j in range(N_CHIPS)], axis=1)
    emb = jnp.pad(emb.reshape(6, D_MODEL), ((0, 2), (0, 0)))
    conv_full = _allreduce_small(emb, "allgather_conv_w")[:6].reshape(CONV_WIDTH, 3 * DN_WIDTH)

    zvec = jnp.zeros((1, 128), F32)
    small = dict(norm_ffn1=norm_ffn1, norm_mix=norm_mix, norm_ffn2=norm_ffn2, norm_final=norm_final[None],
                 conv_w=conv_full, avec=zvec.at[0, DN_HEADS:2 * DN_HEADS].set(a_log[0]),
                 dvec=zvec.at[0, DN_HEADS:2 * DN_HEADS].set(dt_bias[0]), dn_norm=dn_norm)

    loss, grad_x, reduced, sg = _local_step(x[0], loss_target[0], wts, small, dist)

    rows = [sg["norm_ffn1"], sg["norm_mix"], sg["norm_ffn2"], sg["norm_final"], _pad_row(sg["a_log"]), _pad_row(sg["dt_bias"]),
            _pad_row(sg["dn_norm"]), _pad_row(loss[0:1]), sg["conv_w"].reshape(6, D_MODEL), jnp.zeros((2, D_MODEL), F32)]
    red = _allreduce_small(jnp.concatenate(rows, axis=0), "allreduce_small")
    loss_out = red[7, 0]
    g_conv_full = red[8:14].reshape(CONV_WIDTH, 3 * DN_WIDTH)
    g_conv = lax.dynamic_slice_in_dim(g_conv_full, chip * (3 * DN_WIDTH // N_CHIPS), 3 * DN_WIDTH // N_CHIPS, axis=1)
    g_small = dict(norm_ffn1=red[0:1], norm_mix=red[1:2], norm_ffn2=red[2:3], norm_final=red[3],
                   a_log=red[4:5, DN_HEADS:2 * DN_HEADS], dt_bias=red[5:6, DN_HEADS:2 * DN_HEADS], dn_norm=red[6:7, :DN_HEAD_DIM])

    shard_g = {}
    for n in BIG:
        mine, other = reduced[n]
        full = jnp.where(cc == 0, jnp.concatenate([mine, other], axis=0), jnp.concatenate([other, mine], axis=0))
        if n == "w_in":
            full = full[:IN_COLS // N_CHIPS]
        shard_g[n] = full if n in ROW_SHARDED else full.T

    out_g, out_d, out_m, out_v = {}, {}, {}, {}
    for n in BIG:
        d, nm, nv = _adamw(big_w[n], shard_g[n], big_m[n], big_v[n], "adamw_" + n)
        out_g[n], out_d[n], out_m[n], out_v[n] = shard_g[n][None], d[None], nm[None], nv[None]
    d, nm, nv = _adamw(conv_w[0], g_conv, m_conv_w[0], v_conv_w[0], "adamw_conv_w")
    out_g["conv_w"], out_d["conv_w"], out_m["conv_w"], out_v["conv_w"] = g_conv[None], d[None], nm[None], nv[None]

    small_names = ("norm_ffn1", "norm_mix", "norm_ffn2", "norm_final", "a_log", "dt_bias", "dn_norm")
    small_w = dict(norm_ffn1=norm_ffn1, norm_mix=norm_mix, norm_ffn2=norm_ffn2, norm_final=norm_final, a_log=a_log,
                   dt_bias=dt_bias, dn_norm=dn_norm)
    small_m = dict(norm_ffn1=m_norm_ffn1, norm_mix=m_norm_mix, norm_ffn2=m_norm_ffn2, norm_final=m_norm_final, a_log=m_a_log,
                   dt_bias=m_dt_bias, dn_norm=m_dn_norm)
    small_v = dict(norm_ffn1=v_norm_ffn1, norm_mix=v_norm_mix, norm_ffn2=v_norm_ffn2, norm_final=v_norm_final, a_log=v_a_log,
                   dt_bias=v_dt_bias, dn_norm=v_dn_norm)
    stack = lambda dct: jnp.concatenate([_pad_row(dct[n]) for n in small_names] + [jnp.zeros((1, D_MODEL), F32)], axis=0)
    d, nm, nv = _adamw(stack(small_w), stack(g_small), stack(small_m), stack(small_v), "adamw_small")
    for k, n in enumerate(small_names):
        shape = small_w[n].shape
        size = math.prod(shape)
        out_g[n] = g_small[n].reshape(shape)
        out_d[n], out_m[n], out_v[n] = (t[k, :size].reshape(shape) for t in (d, nm, nv))

    order = ("norm_ffn1", "ffn1_gate", "ffn1_up", "ffn1_down", "norm_mix", "w_in", "conv_w", "a_log", "dt_bias", "dn_norm",
             "w_out", "norm_ffn2", "ffn2_gate", "ffn2_up", "ffn2_down", "norm_final")
    return (loss_out, grad_x[None], *[out_g[n] for n in order], *[out_d[n] for n in order],
            *[out_m[n] for n in order], *[out_v[n] for n in order])
```

```python
import functools
import math

import jax
import jax.numpy as jnp
from jax import lax
from jax.experimental import pallas as pl
from jax.experimental.pallas import tpu as pltpu

F32 = jnp.float32
BF16 = jnp.bfloat16
HI = lax.Precision.HIGH

D_MODEL = 1024
D_FF = 2816
ATTN_HEADS = 8
ATTN_WIDTH = 512
ATTN_BLOCK = 128
DILATIONS = (1, 4, 16)
DN_HEADS = 4
DN_HEAD_DIM = 128
DN_WIDTH = 512
DN_CHUNK = 64
CONV_WIDTH = 4
NORM_EPS = 1e-6
L2_EPS = 1e-6
IN_COLS = 3592
IN_COLS_PADDED = 3712
N_CHIPS = 4

ADAM_LR = 0.001
ADAM_B1 = 0.9
ADAM_B2 = 0.999
ADAM_EPS = 1e-08
ADAM_WD = 0.01
ADAM_STEP = 10

VMEM_LIMIT = 56 * 1024 * 1024
NEG_BIG = -1e30
MESH = pl.DeviceIdType.MESH


def _params(n_grid, vmem=VMEM_LIMIT):
    return pltpu.CompilerParams(dimension_semantics=("arbitrary",) * n_grid, vmem_limit_bytes=vmem)


def _call(body, args, *, name, grid, in_specs, out_specs, out_shape, scratch_shapes=(), comm=None):
    n_in, n_out, n_scr = len(in_specs), len(out_specs), len(scratch_shapes)
    hbm = pl.BlockSpec(memory_space=pl.ANY)
    srcs, dst_shapes, n_sems, start, finish = comm if comm is not None else ((), (), 0, None, None)
    ns, nd = len(srcs), len(dst_shapes)

    def full(*refs):
        ins, c_src = refs[:n_in], refs[n_in:n_in + ns]
        at = n_in + ns
        outs, c_dst = refs[at:at + n_out], refs[at + n_out:at + n_out + nd]
        scr = refs[at + n_out + nd:at + n_out + nd + n_scr]
        if comm is not None:
            ids = [pl.program_id(a) for a in range(len(grid))]
            first = functools.reduce(jnp.logical_and, [i == 0 for i in ids])
            last = functools.reduce(jnp.logical_and, [i == g - 1 for i, g in zip(ids, grid)])

            @pl.when(first)
            def _():
                start(c_src, c_dst, refs[-2], refs[-1])

        body(*ins, *outs, *scr)
        if comm is not None:
            @pl.when(last)
            def _():
                finish(c_src, c_dst, refs[-2], refs[-1])

    sems = [pltpu.SemaphoreType.DMA((n_sems,)), pltpu.SemaphoreType.DMA((n_sems,))] if comm is not None else []
    res = pl.pallas_call(
        full, name=name, grid=grid, in_specs=list(in_specs) + [hbm] * ns, out_specs=list(out_specs) + [hbm] * nd,
        out_shape=list(out_shape) + list(dst_shapes), scratch_shapes=list(scratch_shapes) + sems,
        compiler_params=_params(len(grid)),
    )(*args, *srcs)
    return res[:n_out], res[n_out:]


def _nt(a, b, precision=None):
    return lax.dot_general(a, b, (((1,), (1,)), ((), ())), preferred_element_type=F32, precision=precision)


def _tn(a, b, precision=None):
    return lax.dot_general(a, b, (((0,), (0,)), ((), ())), preferred_element_type=F32, precision=precision)


def _nn(a, b, precision=None):
    return jnp.dot(a, b, preferred_element_type=F32, precision=precision)


def _sigmoid(x):
    return 1.0 / (1.0 + jnp.exp(-x))


def _ffn_fwd(x, gain, wg, wu, wd, name, comm=None):
    S, D = x.shape
    nf, _, tf = wg.shape
    tm = 512

    def body(x_ref, gain_ref, wg_ref, wu_ref, wd_ref, xo_ref, h_ref, g_ref, u_ref, acc_ref, hs_ref):
        j = pl.program_id(1)

        @pl.when(j == 0)
        def _():
            xf = x_ref[...]
            r = lax.rsqrt(jnp.mean(xf * xf, axis=-1, keepdims=True) + NORM_EPS)
            h = (xf * r * gain_ref[...]).astype(BF16)
            hs_ref[...] = h
            h_ref[...] = h
            acc_ref[...] = jnp.zeros_like(acc_ref)

        h = hs_ref[...]
        g = _nn(h, wg_ref[...])
        u = _nn(h, wu_ref[...])
        g_ref[...] = g.astype(BF16)
        u_ref[...] = u.astype(BF16)
        act = g * _sigmoid(g) * u
        acc_ref[...] += _nn(act.astype(BF16), wd_ref[...])

        @pl.when(j == nf - 1)
        def _():
            xo_ref[...] = x_ref[...] + 0.5 * acc_ref[...]

    return _call(
        body, (x, gain, wg, wu, wd), name=name, grid=(S // tm, nf), comm=comm,
        in_specs=[pl.BlockSpec((tm, D), lambda i, j: (i, 0)),
                  pl.BlockSpec((1, D), lambda i, j: (0, 0)),
                  pl.BlockSpec((None, D, tf), lambda i, j: (j, 0, 0)),
                  pl.BlockSpec((None, D, tf), lambda i, j: (j, 0, 0)),
                  pl.BlockSpec((None, tf, D), lambda i, j: (j, 0, 0))],
        out_specs=[pl.BlockSpec((tm, D), lambda i, j: (i, 0)),
                   pl.BlockSpec((tm, D), lambda i, j: (i, 0)),
                   pl.BlockSpec((None, tm, tf), lambda i, j: (j, i, 0)),
                   pl.BlockSpec((None, tm, tf), lambda i, j: (j, i, 0))],
        out_shape=[jax.ShapeDtypeStruct((S, D), F32), jax.ShapeDtypeStruct((S, D), BF16),
                   jax.ShapeDtypeStruct((nf, S, tf), BF16), jax.ShapeDtypeStruct((nf, S, tf), BF16)],
        scratch_shapes=[pltpu.VMEM((tm, D), F32), pltpu.VMEM((tm, D), BF16)])


def _rmsnorm_bwd(dh, xf, gain):
    r = lax.rsqrt(jnp.mean(xf * xf, axis=-1, keepdims=True) + NORM_EPS)
    xhat = xf * r
    dgain = jnp.sum(dh * xhat, axis=0, keepdims=True)
    dxh = dh * gain
    dx = r * (dxh - xhat * jnp.mean(dxh * xhat, axis=-1, keepdims=True))
    return dx, dgain


def _ffn_bwd(dxo, x, gain, g, u, wd, wg, wu, name, comm=None):
    S, D = x.shape
    nf, _, tf = g.shape
    tm = 512

    def body(dxo_ref, x_ref, gain_ref, g_ref, u_ref, wd_ref, wg_ref, wu_ref,
             dx_ref, dgain_ref, dg_ref, du_ref, act_ref, dout_ref, acc_ref, ds_ref):
        i = pl.program_id(0)
        j = pl.program_id(1)

        @pl.when(j == 0)
        def _():
            d = (0.5 * dxo_ref[...]).astype(BF16)
            ds_ref[...] = d
            dout_ref[...] = d
            acc_ref[...] = jnp.zeros_like(acc_ref)

        @pl.when((i == 0) & (j == 0))
        def _():
            dgain_ref[...] = jnp.zeros_like(dgain_ref)

        for half in range(2):
            rows = slice(half * (tm // 2), (half + 1) * (tm // 2))
            dact = _nt(ds_ref[rows, :], wd_ref[...])
            gv = g_ref[rows, :].astype(F32)
            uv = u_ref[rows, :].astype(F32)
            sg = _sigmoid(gv)
            silu = gv * sg
            act_ref[rows, :] = (silu * uv).astype(BF16)
            dgv = (dact * uv * (sg * (1.0 + gv * (1.0 - sg)))).astype(BF16)
            duv = (dact * silu).astype(BF16)
            dg_ref[rows, :] = dgv
            du_ref[rows, :] = duv
            acc_ref[rows, :] += _nt(dgv, wg_ref[...]) + _nt(duv, wu_ref[...])

        @pl.when(j == nf - 1)
        def _():
            dx, dgain = _rmsnorm_bwd(acc_ref[...], x_ref[...], gain_ref[...])
            dx_ref[...] = dxo_ref[...] + dx
            dgain_ref[...] += dgain

    return _call(
        body, (dxo, x, gain, g, u, wd, wg, wu), name=name, grid=(S // tm, nf), comm=comm,
        in_specs=[pl.BlockSpec((tm, D), lambda i, j: (i, 0)),
                  pl.BlockSpec((tm, D), lambda i, j: (i, 0)),
                  pl.BlockSpec((1, D), lambda i, j: (0, 0)),
                  pl.BlockSpec((None, tm, tf), lambda i, j: (j, i, 0)),
                  pl.BlockSpec((None, tm, tf), lambda i, j: (j, i, 0)),
                  pl.BlockSpec((None, tf, D), lambda i, j: (j, 0, 0)),
                  pl.BlockSpec((None, D, tf), lambda i, j: (j, 0, 0)),
                  pl.BlockSpec((None, D, tf), lambda i, j: (j, 0, 0))],
        out_specs=[pl.BlockSpec((tm, D), lambda i, j: (i, 0)),
                   pl.BlockSpec((1, D), lambda i, j: (0, 0)),
                   pl.BlockSpec((None, tm, tf), lambda i, j: (j, i, 0)),
                   pl.BlockSpec((None, tm, tf), lambda i, j: (j, i, 0)),
                   pl.BlockSpec((None, tm, tf), lambda i, j: (j, i, 0)),
                   pl.BlockSpec((tm, D), lambda i, j: (i, 0))],
        out_shape=[jax.ShapeDtypeStruct((S, D), F32), jax.ShapeDtypeStruct((1, D), F32),
                   jax.ShapeDtypeStruct((nf, S, tf), BF16), jax.ShapeDtypeStruct((nf, S, tf), BF16),
                   jax.ShapeDtypeStruct((nf, S, tf), BF16), jax.ShapeDtypeStruct((S, D), BF16)],
        scratch_shapes=[pltpu.VMEM((tm, D), F32), pltpu.VMEM((tm, D), BF16)])


def _matmul_tn(a, b, tm, tk, name):
    K, M = a.shape
    N = b.shape[1]

    def body(a_ref, b_ref, o_ref):
        @pl.when(pl.program_id(1) == 0)
        def _():
            o_ref[...] = jnp.zeros_like(o_ref)

        o_ref[...] += _tn(a_ref[...], b_ref[...])

    return pl.pallas_call(
        body, name=name, grid=(M // tm, K // tk),
        in_specs=[pl.BlockSpec((tk, tm), lambda i, k: (k, i)),
                  pl.BlockSpec((tk, N), lambda i, k: (k, 0))],
        out_specs=pl.BlockSpec((tm, N), lambda i, k: (i, 0)),
        out_shape=jax.ShapeDtypeStruct((M, N), F32),
        compiler_params=_params(2),
    )(a, b)


def _dw_chunks(a, b, tk, name, comm=None):
    nf, S, tf = a.shape
    N = b.shape[1]

    def body(a_ref, b_ref, o_ref):
        @pl.when(pl.program_id(1) == 0)
        def _():
            o_ref[...] = jnp.zeros_like(o_ref)

        o_ref[...] += _tn(a_ref[...], b_ref[...])

    (out,), landed = _call(
        body, (a, b), name=name, grid=(nf, S // tk), comm=comm,
        in_specs=[pl.BlockSpec((None, tk, tf), lambda j, k: (j, k, 0)),
                  pl.BlockSpec((tk, N), lambda j, k: (k, 0))],
        out_specs=[pl.BlockSpec((None, tf, N), lambda j, k: (j, 0, 0))],
        out_shape=[jax.ShapeDtypeStruct((nf, tf, N), F32)])
    return out, landed


VIEW_TILE = 512


def _view_spec(d, tile=VIEW_TILE):
    return pl.BlockSpec((tile // d, d * ATTN_WIDTH), lambda i: (i, 0))


def _view_shape(S, d, dtype):
    return jax.ShapeDtypeStruct((S // d, d * ATTN_WIDTH), dtype)


def _tile_to_views(val, planes, out_refs):
    for g in range(4):
        planes[g] = val[:, g * 128:(g + 1) * 128]
    for d, ref in zip(DILATIONS, out_refs):
        if d == 1:
            ref[...] = val.astype(ref.dtype)
            continue
        for r in range(d):
            for g in range(4):
                ref[:, r * ATTN_WIDTH + g * 128:r * ATTN_WIDTH + (g + 1) * 128] = (
                    planes[g, pl.ds(r, planes.shape[1] // d, stride=d), :].astype(ref.dtype))


def _view_to_tile(ref, d, planes):
    if d == 1:
        return ref[...]
    for r in range(d):
        for g in range(4):
            planes[g, pl.ds(r, planes.shape[1] // d, stride=d), :] = ref[:, r * ATTN_WIDTH + g * 128:r * ATTN_WIDTH + (g + 1) * 128]
    return jnp.concatenate([planes[g] for g in range(4)], axis=1)


def _inproj_fwd(x, gain, w_in_p):
    S, D = x.shape
    tm = VIEW_TILE
    W = ATTN_WIDTH

    def body(x_ref, gain_ref, w_ref, h_ref, q1, q4, q16, k1, k4, k16, v1, v4, v16, dq_ref, dk_ref, dv_ref, gate_ref, bd_ref,
             planes):
        xf = x_ref[...]
        r = lax.rsqrt(jnp.mean(xf * xf, axis=-1, keepdims=True) + NORM_EPS)
        h = (xf * r * gain_ref[...]).astype(BF16)
        h_ref[...] = h
        _tile_to_views(_nn(h, w_ref[:, 0:W]) * 0.125, planes, (q1, q4, q16))
        _tile_to_views(_nn(h, w_ref[:, W:2 * W]), planes, (k1, k4, k16))
        _tile_to_views(_nn(h, w_ref[:, 2 * W:3 * W]), planes, (v1, v4, v16))
        dq_ref[...] = _nn(h, w_ref[:, 3 * W:4 * W])
        dk_ref[...] = _nn(h, w_ref[:, 4 * W:5 * W])
        dv_ref[...] = _nn(h, w_ref[:, 5 * W:6 * W])
        gate_ref[...] = _nn(h, w_ref[:, 6 * W:7 * W])
        bd_ref[...] = _nn(h, w_ref[:, 7 * W:7 * W + 128])

    tok = lambda w: pl.BlockSpec((tm, w), lambda i: (i, 0))
    return pl.pallas_call(
        body, name="inproj_fwd", grid=(S // tm,),
        in_specs=[tok(D), pl.BlockSpec((1, D), lambda i: (0, 0)),
                  pl.BlockSpec((D, IN_COLS_PADDED), lambda i: (0, 0))],
        out_specs=[tok(D)] + [_view_spec(d) for d in DILATIONS] * 3 + [tok(W)] * 4 + [tok(128)],
        out_shape=[jax.ShapeDtypeStruct((S, D), BF16)] + [_view_shape(S, d, BF16) for d in DILATIONS] * 3
                  + [jax.ShapeDtypeStruct((S, W), F32)] * 4 + [jax.ShapeDtypeStruct((S, 128), F32)],
        scratch_shapes=[pltpu.VMEM((4, tm, 128), F32)],
        compiler_params=_params(1),
    )(x, gain, w_in_p)


def _inproj_bwd(dxo, x, gain, attn_grads, dsecs, dbd, w_in_p):
    S, D = x.shape
    tm = VIEW_TILE // 2
    W = ATTN_WIDTH

    def body(dxo_ref, x_ref, gain_ref, *rest):
        views, (s3, s4, s5, s6, dbd_ref, w_ref, dx_ref, dgain_ref, dproj_ref, planes) = rest[:9], rest[9:]

        @pl.when(pl.program_id(0) == 0)
        def _():
            dgain_ref[...] = jnp.zeros_like(dgain_ref)

        secs = []
        for k in range(3):
            parts = [_view_to_tile(views[3 * k + p], d, planes) for p, d in enumerate(DILATIONS)]
            secs.append(parts[0] + parts[1] + parts[2])
        secs += [s3[...], s4[...], s5[...], s6[...]]
        dh = jnp.zeros((tm, D), F32)
        for k, s in enumerate(secs):
            d = s.astype(BF16)
            dproj_ref[:, k * W:(k + 1) * W] = d
            dh += _nt(d, w_ref[:, k * W:(k + 1) * W])
        d = dbd_ref[...].astype(BF16)
        dproj_ref[:, 7 * W:7 * W + 128] = d
        dh += _nt(d, w_ref[:, 7 * W:7 * W + 128])
        dx, dgain = _rmsnorm_bwd(dh, x_ref[...], gain_ref[...])
        dx_ref[...] = dxo_ref[...] + dx
        dgain_ref[...] += dgain

    tok = lambda w: pl.BlockSpec((tm, w), lambda i: (i, 0))
    return pl.pallas_call(
        body, name="inproj_bwd", grid=(S // tm,),
        in_specs=[tok(D), tok(D), pl.BlockSpec((1, D), lambda i: (0, 0))] + [_view_spec(d, tm) for d in DILATIONS] * 3
                 + [tok(W)] * 4 + [tok(128)] + [pl.BlockSpec((D, IN_COLS_PADDED), lambda i: (0, 0))],
        out_specs=[tok(D), pl.BlockSpec((1, D), lambda i: (0, 0)), tok(IN_COLS_PADDED)],
        out_shape=[jax.ShapeDtypeStruct((S, D), F32), jax.ShapeDtypeStruct((1, D), F32),
                   jax.ShapeDtypeStruct((S, IN_COLS_PADDED), BF16)],
        scratch_shapes=[pltpu.VMEM((4, tm, 128), F32)],
        compiler_params=_params(1),
    )(dxo, x, gain, *[g for grads in attn_grads for g in grads], *dsecs, dbd, w_in_p)


def _slope(h):
    return 2.0 ** (-8.0 * (h + 1) / ATTN_HEADS)


def _head_bias(steps, d, heads=tuple(range(ATTN_HEADS))):
    stepsf = steps.astype(F32)
    return jnp.stack([stepsf * (-_slope(h) * d) for h in heads])


def _hnt(a, b):
    return lax.dot_general(a, b, (((2,), (2,)), ((0,), (0,))), preferred_element_type=F32)


def _hnn(a, b):
    return lax.dot_general(a, b, (((2,), (1,)), ((0,), (0,))), preferred_element_type=F32)


def _blocks_per_step(nb):
    return next(n for n in (4, 2, 1) if nb % n == 0)


def _query_step_specs(qb):
    B = ATTN_BLOCK
    cur = pl.BlockSpec((qb * B, ATTN_WIDTH), lambda r, n: (n, r))
    prev = pl.BlockSpec((B, ATTN_WIDTH), lambda r, n: (jnp.maximum(qb * n - 1, 0), r))
    return cur, prev


def _prev_block(prev_ref, cur_ref, sub, sl):
    B = ATTN_BLOCK
    return prev_ref[:, sl] if sub == 0 else cur_ref[(sub - 1) * B:sub * B, sl]


def _head_cols(tile, lo, big):
    return [_head_col(tile, lo, big), _head_col(tile, jnp.logical_not(lo), big)]


def _attn_fwd(q, k, v, d, name):
    L = q.shape[0]
    nb = L // ATTN_BLOCK
    B = ATTN_BLOCK
    QB = _blocks_per_step(nb)

    def body(q_ref, kp_ref, kc_ref, vp_ref, vc_ref, acc_ref, m_ref, l_ref):
        n = pl.program_id(1)
        qi = lax.broadcasted_iota(jnp.int32, (B, 2 * B), 0)
        kj = lax.broadcasted_iota(jnp.int32, (B, 2 * B), 1)
        steps = qi + B - kj
        band = (steps >= 0) & (steps <= B)
        lo = lax.broadcasted_iota(jnp.int32, (B, 128), 1) < 64
        bias = _head_bias(steps, d)
        for sub in range(QB):
            rows = slice(sub * B, (sub + 1) * B)
            valid = band & ((kj >= B) | (n > 0)) if sub == 0 else band
            qs, ks, vs = [], [], []
            for G in range(4):
                sl = slice(G * 128, (G + 1) * 128)
                qg = q_ref[rows, sl]
                kg = jnp.concatenate([_prev_block(kp_ref, kc_ref, sub, sl), kc_ref[rows, sl]], axis=0)
                vg = jnp.concatenate([_prev_block(vp_ref, vc_ref, sub, sl), vc_ref[rows, sl]], axis=0)
                qs += [jnp.where(lo, qg, jnp.zeros_like(qg)), jnp.where(lo, jnp.zeros_like(qg), qg)]
                ks += [kg, kg]
                vs += [vg, vg]
            s = jnp.where(valid, _hnt(jnp.stack(qs), jnp.stack(ks)) + bias, NEG_BIG)
            m = jnp.max(s, axis=-1, keepdims=True)
            p = jnp.exp(s - m)
            l = jnp.sum(p, axis=-1, keepdims=True)
            a = _hnn(p.astype(BF16), jnp.stack(vs))
            for G in range(4):
                sl = slice(G * 128, (G + 1) * 128)
                acc_ref[rows, sl] = jnp.where(lo, a[2 * G], a[2 * G + 1])
                m_ref[rows, sl] = jnp.where(lo, m[2 * G], m[2 * G + 1])
                l_ref[rows, sl] = jnp.where(lo, l[2 * G], l[2 * G + 1])

    cur, prev = _query_step_specs(QB)
    return pl.pallas_call(
        body, name=name, grid=(d, nb // QB),
        in_specs=[cur, prev, cur, prev, cur],
        out_specs=[cur, cur, cur],
        out_shape=[jax.ShapeDtypeStruct((L, d * ATTN_WIDTH), F32)] * 3,
        compiler_params=_params(2),
    )(q, k, k, v, v)


def _attn_merge(parts):
    S = parts[0][0].shape[0]
    tm = VIEW_TILE

    def body(a1, m1, l1, a2, m2, l2, a3, m3, l3, o_ref, lse1, lse4, lse16, planes):
        ins = ((a1, m1, l1), (a2, m2, l2), (a3, m3, l3))
        acc, ms, ls = [], [], []
        for d, (a, m, l) in zip(DILATIONS, ins):
            acc.append(_view_to_tile(a, d, planes))
            ms.append(_view_to_tile(m, d, planes))
            ls.append(_view_to_tile(l, d, planes))
        mx = jnp.maximum(jnp.maximum(ms[0], ms[1]), ms[2])
        es = [jnp.exp(m - mx) for m in ms]
        den = es[0] * ls[0] + es[1] * ls[1] + es[2] * ls[2]
        num = es[0] * acc[0] + es[1] * acc[1] + es[2] * acc[2]
        o_ref[...] = num / den
        _tile_to_views(mx + jnp.log(den), planes, (lse1, lse4, lse16))

    views = [_view_spec(d) for d in DILATIONS]
    flat = [t for p in parts for t in p]
    return pl.pallas_call(
        body, name="attn_merge", grid=(S // tm,),
        in_specs=[views[p] for p in range(3) for _ in range(3)],
        out_specs=[views[0]] + views,
        out_shape=[jax.ShapeDtypeStruct((S, ATTN_WIDTH), F32)] + [_view_shape(S, d, F32) for d in DILATIONS],
        scratch_shapes=[pltpu.VMEM((4, tm, 128), F32)],
        compiler_params=_params(1),
    )(*flat)


def _head_col(t, msk, big):
    if big:
        return jnp.max(jnp.where(msk, t, NEG_BIG), axis=-1, keepdims=True)
    return jnp.sum(jnp.where(msk, t, 0.0), axis=-1, keepdims=True) * (1.0 / 64.0)


def _attn_bwd_q(q, k, v, do, lse, dd, d, name):
    L = q.shape[0]
    nb = L // ATTN_BLOCK
    B = ATTN_BLOCK
    QB = _blocks_per_step(nb)

    def body(q_ref, kp_ref, kc_ref, vp_ref, vc_ref, do_ref, lse_ref, dd_ref, dq_ref):
        n = pl.program_id(1)
        qi = lax.broadcasted_iota(jnp.int32, (B, 2 * B), 0)
        kj = lax.broadcasted_iota(jnp.int32, (B, 2 * B), 1)
        steps = qi + B - kj
        band = (steps >= 0) & (steps <= B)
        lo = lax.broadcasted_iota(jnp.int32, (B, 128), 1) < 64
        bias = _head_bias(steps, d)
        for sub in range(QB):
            rows = slice(sub * B, (sub + 1) * B)
            valid = band & ((kj >= B) | (n > 0)) if sub == 0 else band
            qs, ks, vs, dos, lses, dcols = [], [], [], [], [], []
            for G in range(4):
                sl = slice(G * 128, (G + 1) * 128)
                qg = q_ref[rows, sl]
                kg = jnp.concatenate([_prev_block(kp_ref, kc_ref, sub, sl), kc_ref[rows, sl]], axis=0)
                vg = jnp.concatenate([_prev_block(vp_ref, vc_ref, sub, sl), vc_ref[rows, sl]], axis=0)
                dog = do_ref[rows, sl]
                qs += [jnp.where(lo, qg, jnp.zeros_like(qg)), jnp.where(lo, jnp.zeros_like(qg), qg)]
                dos += [jnp.where(lo, dog, 0.0).astype(BF16), jnp.where(lo, 0.0, dog).astype(BF16)]
                ks += [kg, kg]
                vs += [vg, vg]
                lses += _head_cols(lse_ref[rows, sl], lo, True)
                dcols += _head_cols(dd_ref[rows, sl], lo, False)
            kb = jnp.stack(ks)
            s = _hnt(jnp.stack(qs), kb) + bias
            p = jnp.where(valid, jnp.exp(jnp.where(valid, s, NEG_BIG) - jnp.stack(lses)), 0.0)
            dp = _hnt(jnp.stack(dos), jnp.stack(vs))
            ds = p * (dp - jnp.stack(dcols))
            dq = _hnn(ds.astype(BF16), kb) * 0.125
            for G in range(4):
                dq_ref[rows, G * 128:(G + 1) * 128] = jnp.where(lo, dq[2 * G], dq[2 * G + 1])

    cur, prev = _query_step_specs(QB)
    return pl.pallas_call(
        body, name=name, grid=(d, nb // QB), in_specs=[cur, prev, cur, prev, cur, cur, cur, cur], out_specs=cur,
        out_shape=jax.ShapeDtypeStruct((L, d * ATTN_WIDTH), F32), compiler_params=_params(2),
    )(q, k, k, v, v, do, lse, dd)


def _attn_bwd_kv(q, k, v, do, lse, dd, d, name):
    L = q.shape[0]
    nb = L // ATTN_BLOCK
    B = ATTN_BLOCK
    KB = _blocks_per_step(nb)
    n_steps = nb // KB

    def body(k_ref, v_ref, qc_ref, qn_ref, doc_ref, don_ref, lsec_ref, lsen_ref, ddc_ref, ddn_ref, dk_ref, dv_ref):
        j = pl.program_id(1)
        qrow = lax.broadcasted_iota(jnp.int32, (2 * B, B), 0)
        kk = lax.broadcasted_iota(jnp.int32, (2 * B, B), 1)
        steps = qrow - kk
        band = (steps >= 0) & (steps <= B)
        lo2 = lax.broadcasted_iota(jnp.int32, (2 * B, 128), 1) < 64
        lo = lax.broadcasted_iota(jnp.int32, (B, 128), 1) < 64
        stepsf = steps.astype(F32)
        for sub in range(KB):
            rows = slice(sub * B, (sub + 1) * B)
            last = sub == KB - 1
            valid = band & ((qrow < B) | (j < n_steps - 1)) if last else band
            after = lambda cur_ref, nxt_ref, sl: nxt_ref[:, sl] if last else cur_ref[(sub + 1) * B:(sub + 2) * B, sl]
            for G in range(4):
                sl = slice(G * 128, (G + 1) * 128)
                kg = k_ref[rows, sl]
                vg = v_ref[rows, sl]
                qq = jnp.concatenate([qc_ref[rows, sl], after(qc_ref, qn_ref, sl)], axis=0)
                doo = jnp.concatenate([doc_ref[rows, sl], after(doc_ref, don_ref, sl)], axis=0)
                lse2 = jnp.concatenate([lsec_ref[rows, sl], after(lsec_ref, lsen_ref, sl)], axis=0)
                dd2 = jnp.concatenate([ddc_ref[rows, sl], after(ddc_ref, ddn_ref, sl)], axis=0)
                doo_b = doo.astype(BF16)
                dks, dvs = [], []
                for half in (0, 1):
                    msk = lo2 if half == 0 else jnp.logical_not(lo2)
                    qm = jnp.where(msk, qq, jnp.zeros_like(qq))
                    s = _nt(qm, kg) - (_slope(2 * G + half) * d) * stepsf
                    lse_c = _head_col(lse2, msk, True)
                    p = jnp.where(valid, jnp.exp(jnp.where(valid, s, NEG_BIG) - lse_c), 0.0)
                    dvs.append(_tn(p.astype(BF16), doo_b))
                    dom = jnp.where(msk, doo, 0.0).astype(BF16)
                    dp = _nt(dom, vg)
                    dcol = _head_col(dd2, msk, False)
                    ds = p * (dp - dcol)
                    dks.append(_tn(ds.astype(BF16), qq))
                dk_ref[rows, sl] = jnp.where(lo, dks[0], dks[1])
                dv_ref[rows, sl] = jnp.where(lo, dvs[0], dvs[1])

    cur = pl.BlockSpec((KB * B, ATTN_WIDTH), lambda r, j: (j, r))
    nxt = pl.BlockSpec((B, ATTN_WIDTH), lambda r, j: (jnp.minimum(KB * (j + 1), nb - 1), r))
    return pl.pallas_call(
        body, name=name, grid=(d, n_steps), in_specs=[cur, cur, cur, nxt, cur, nxt, cur, nxt, cur, nxt],
        out_specs=[cur, cur],
        out_shape=[jax.ShapeDtypeStruct((L, d * ATTN_WIDTH), F32)] * 2, compiler_params=_params(2),
    )(k, v, q, q, do, do, lse, lse, dd, dd)


CONV_T = 512
HALO = 8


def _per_head(head, refs):
    for h in range(DN_HEADS):
        lanes = pl.ds(h * DN_HEAD_DIM, DN_HEAD_DIM)
        head(*[r.at[:, lanes] for r in refs[:-1]], refs[-1])


def _conv_taps(pad_ref, w, T):
    acc = pad_ref[pl.ds(HALO - 3, T), :] * w[0:1, :]
    for j in range(1, CONV_WIDTH):
        acc = acc + pad_ref[pl.ds(HALO - 3 + j, T), :] * w[j:j + 1, :]
    return acc


def _conv_fwd(xq, xk, xv, conv_w):
    S = xq.shape[0]
    T = CONV_T

    def body(*refs):
        _per_head(head, refs)

    def head(xq_ref, xqh_ref, xk_ref, xkh_ref, xv_ref, xvh_ref, wq_ref, wk_ref, wv_ref,
             qn_ref, kn_ref, v_ref, pad_ref):
        i = pl.program_id(0)

        def act(x_ref, xh_ref, w_ref):
            pad_ref[pl.ds(0, HALO), :] = jnp.where(i > 0, xh_ref[...], 0.0)
            pad_ref[pl.ds(HALO, T), :] = x_ref[...]
            c = _conv_taps(pad_ref, w_ref[...], T)
            return c * _sigmoid(c)

        def l2n(t):
            return t * lax.rsqrt(jnp.sum(t * t, axis=-1, keepdims=True) + L2_EPS)

        qn_ref[...] = l2n(act(xq_ref, xqh_ref, wq_ref))
        kn_ref[...] = l2n(act(xk_ref, xkh_ref, wk_ref))
        v_ref[...] = act(xv_ref, xvh_ref, wv_ref)

    tile = pl.BlockSpec((T, DN_WIDTH), lambda i: (i, 0))
    halo = pl.BlockSpec((HALO, DN_WIDTH), lambda i: (jnp.maximum(i * (T // HALO) - 1, 0), 0))
    wspec = lambda sec: pl.BlockSpec((CONV_WIDTH, DN_WIDTH), lambda i, sec=sec: (0, sec))
    return pl.pallas_call(
        body, name="dn_conv_fwd", grid=(S // T,),
        in_specs=[tile, halo, tile, halo, tile, halo, wspec(0), wspec(1), wspec(2)],
        out_specs=[tile, tile, tile],
        out_shape=[jax.ShapeDtypeStruct((S, DN_WIDTH), F32)] * 3,
        scratch_shapes=[pltpu.VMEM((T + HALO, 128), F32)],
        compiler_params=_params(1),
    )(xq, xq, xk, xk, xv, xv, conv_w, conv_w, conv_w)


def _conv_bwd_pre(xq, xk, xv, conv_w, dqn, dkn, dv):
    S = xq.shape[0]
    T = CONV_T

    def body(*refs):
        _per_head(head, refs)

    def head(xq_ref, xqh_ref, xk_ref, xkh_ref, xv_ref, xvh_ref, wq_ref, wk_ref, wv_ref,
             dqn_ref, dkn_ref, dv_ref, dcq_ref, dck_ref, dcv_ref, dwq_ref, dwk_ref, dwv_ref, pad_ref):
        i = pl.program_id(0)

        def one(x_ref, xh_ref, w_ref, dy_ref, dc_ref, dw_ref, normed):
            pad_ref[pl.ds(0, HALO), :] = jnp.where(i > 0, xh_ref[...], 0.0)
            pad_ref[pl.ds(HALO, T), :] = x_ref[...]
            c = _conv_taps(pad_ref, w_ref[...], T)
            sg = _sigmoid(c)
            a = c * sg
            dy = dy_ref[...]
            if normed:
                r = lax.rsqrt(jnp.sum(a * a, axis=-1, keepdims=True) + L2_EPS)
                y = a * r
                da = r * (dy - y * jnp.sum(dy * y, axis=-1, keepdims=True))
            else:
                da = dy
            dc = da * (sg * (1.0 + c * (1.0 - sg)))
            dc_ref[...] = dc

            @pl.when(i == 0)
            def _():
                dw_ref[...] = jnp.zeros_like(dw_ref)

            rows = [jnp.sum(dc * pad_ref[pl.ds(HALO - 3 + j, T), :], axis=0, keepdims=True) for j in range(CONV_WIDTH)]
            dw_ref[...] += jnp.concatenate(rows + [jnp.zeros((8 - CONV_WIDTH, 128), F32)], axis=0)

        one(xq_ref, xqh_ref, wq_ref, dqn_ref, dcq_ref, dwq_ref, True)
        one(xk_ref, xkh_ref, wk_ref, dkn_ref, dck_ref, dwk_ref, True)
        one(xv_ref, xvh_ref, wv_ref, dv_ref, dcv_ref, dwv_ref, False)

    tile = pl.BlockSpec((T, DN_WIDTH), lambda i: (i, 0))
    halo = pl.BlockSpec((HALO, DN_WIDTH), lambda i: (jnp.maximum(i * (T // HALO) - 1, 0), 0))
    wspec = lambda sec: pl.BlockSpec((CONV_WIDTH, DN_WIDTH), lambda i, sec=sec: (0, sec))
    dwspec = pl.BlockSpec((8, DN_WIDTH), lambda i: (0, 0))
    return pl.pallas_call(
        body, name="dn_conv_bwd_pre", grid=(S // T,),
        in_specs=[tile, halo, tile, halo, tile, halo, wspec(0), wspec(1), wspec(2), tile, tile, tile],
        out_specs=[tile, tile, tile, dwspec, dwspec, dwspec],
        out_shape=[jax.ShapeDtypeStruct((S, DN_WIDTH), F32)] * 3 + [jax.ShapeDtypeStruct((8, DN_WIDTH), F32)] * 3,
        scratch_shapes=[pltpu.VMEM((T + HALO, 128), F32)],
        compiler_params=_params(1),
    )(xq, xq, xk, xk, xv, xv, conv_w, conv_w, conv_w, dqn, dkn, dv)


def _conv_bwd_x(dcq, dck, dcv, conv_w):
    S = dcq.shape[0]
    T = CONV_T
    nt = S // T

    def body(*refs):
        _per_head(head, refs)

    def head(dq_ref, dqh_ref, dk_ref, dkh_ref, dv_ref, dvh_ref, wq_ref, wk_ref, wv_ref,
             oq_ref, ok_ref, ov_ref, pad_ref):
        i = pl.program_id(0)

        def one(d_ref, dh_ref, w_ref, o_ref):
            pad_ref[pl.ds(0, T), :] = d_ref[...]
            pad_ref[pl.ds(T, HALO), :] = jnp.where(i < nt - 1, dh_ref[...], 0.0)
            w = w_ref[...]
            acc = pad_ref[pl.ds(3, T), :] * w[0:1, :]
            for j in range(1, CONV_WIDTH):
                acc = acc + pad_ref[pl.ds(3 - j, T), :] * w[j:j + 1, :]
            o_ref[...] = acc

        one(dq_ref, dqh_ref, wq_ref, oq_ref)
        one(dk_ref, dkh_ref, wk_ref, ok_ref)
        one(dv_ref, dvh_ref, wv_ref, ov_ref)

    tile = pl.BlockSpec((T, DN_WIDTH), lambda i: (i, 0))
    halo = pl.BlockSpec((HALO, DN_WIDTH), lambda i: (jnp.minimum((i + 1) * (T // HALO), S // HALO - 1), 0))
    wspec = lambda sec: pl.BlockSpec((CONV_WIDTH, DN_WIDTH), lambda i, sec=sec: (0, sec))
    return pl.pallas_call(
        body, name="dn_conv_bwd_x", grid=(nt,),
        in_specs=[tile, halo, tile, halo, tile, halo, wspec(0), wspec(1), wspec(2)],
        out_specs=[tile, tile, tile],
        out_shape=[jax.ShapeDtypeStruct((S, DN_WIDTH), F32)] * 3,
        scratch_shapes=[pltpu.VMEM((T + HALO, 128), F32)],
        compiler_params=_params(1),
    )(dcq, dcq, dck, dck, dcv, dcv, conv_w, conv_w, conv_w)


PREP_CHUNKS = 4
SCAN_CHUNKS = 8


def _bnn(a, b):
    return lax.dot_general(a, b, (((2,), (1,)), ((0,), (0,))), preferred_element_type=F32, precision=HI)


def _bnt(a, b):
    return lax.dot_general(a, b, (((2,), (2,)), ((0,), (0,))), preferred_element_type=F32, precision=HI)


def _btn(a, b):
    return lax.dot_general(a, b, (((1,), (1,)), ((0,), (0,))), preferred_element_type=F32, precision=HI)


def _tri_inverse_b(a, blk, eye):
    dg = jnp.where(blk, a, 0.0)
    lo = a - dg
    d2 = _bnn(dg, dg)
    d4 = _bnn(d2, d2)
    d8 = _bnn(d4, d4)
    td = _bnn(_bnn(_bnn(eye - dg, eye + d2), eye + d4), eye + d8)
    b = _bnn(td, lo)
    b2 = _bnn(b, b)
    return _bnn(_bnn(eye - b, eye + b2), td)


def _dn_common_b(bds, avec, dvec, q_raw, k, v, t=None):
    C = DN_CHUNK
    lane = lax.broadcasted_iota(jnp.int32, (C, 128), 1)
    row = lax.broadcasted_iota(jnp.int32, (1, C, C), 1)
    col = lax.broadcasted_iota(jnp.int32, (1, C, C), 2)
    incl = row >= col
    strict = row > col
    eye = (row == col).astype(F32)
    blk = (row // 16) == (col // 16)
    pick = lambda tile, ln: jnp.sum(jnp.where(lane == ln, tile, 0.0), axis=-1, keepdims=True)
    betas, graws, zcs = [], [], []
    for bd in bds:
        z = bd + dvec
        g_all = -jnp.exp(avec) * (jnp.maximum(z, 0.0) + jnp.log(1.0 + jnp.exp(-jnp.abs(z))))
        beta_all = _sigmoid(bd)
        for h in range(DN_HEADS):
            betas.append(pick(beta_all, h))
            graws.append(pick(g_all, DN_HEADS + h))
            zcs.append(pick(z, DN_HEADS + h))
    beta, graw, zc = jnp.stack(betas), jnp.stack(graws), jnp.stack(zcs)
    to_row = lambda c: jnp.sum(eye * c, axis=1, keepdims=True)
    gc = jnp.sum(jnp.where(incl, to_row(graw), 0.0), axis=-1, keepdims=True)
    decay = jnp.exp(jnp.where(incl, gc - to_row(gc), NEG_BIG))
    q = q_raw * (DN_HEAD_DIM ** -0.5)
    kb = k * beta
    kk = _bnt(kb, k)
    if t is None:
        t = _tri_inverse_b(jnp.where(strict, kk * decay, 0.0), blk, eye)
    eg = jnp.exp(gc)
    rhs_w = kb * eg
    u = _bnn(t, v * beta)
    w = _bnn(t, rhs_w)
    qk = _bnt(q, k)
    aq = jnp.where(incl, qk * decay, 0.0)
    last = lax.broadcasted_iota(jnp.int32, (1, C, 1), 1) == C - 1
    g_last = jnp.sum(jnp.where(last, gc, 0.0), axis=1, keepdims=True)
    ekd = jnp.exp(g_last - gc)
    return dict(beta=beta, graw=graw, zc=zc, gc=gc, decay=decay, q=q, kb=kb, kk=kk, t=t, eg=eg, rhs_w=rhs_w,
                u=u, w=w, qk=qk, aq=aq, g_last=g_last, ekd=ekd, kd=k * ekd, qg=q * eg,
                incl=incl, strict=strict, eye=eye, lane=lane, row=row, col=col, last=last)


def _stack_heads(ref, rows):
    return jnp.stack([ref[rows, h * DN_HEAD_DIM:(h + 1) * DN_HEAD_DIM] for h in range(DN_HEADS)])


def _stack_units(ref, nc):
    C = DN_CHUNK
    return jnp.concatenate([_stack_heads(ref, slice(ci * C, (ci + 1) * C)) for ci in range(nc)], axis=0)


def _store_units(ref, val, nc):
    C = DN_CHUNK
    for ci in range(nc):
        for h in range(DN_HEADS):
            ref[ci * C:(ci + 1) * C, h * DN_HEAD_DIM:(h + 1) * DN_HEAD_DIM] = val[ci * DN_HEADS + h]


def _dn_prep(qn, kn, v, bd, avec, dvec):
    S = qn.shape[0]
    C = DN_CHUNK
    N = S // C
    nc = PREP_CHUNKS

    def body(q_ref, k_ref, v_ref, bd_ref, a_ref, d_ref, u_ref, w_ref, qg_ref, kd_ref, aq_ref, t_ref, egl_ref):
        bds = [bd_ref[ci * C:(ci + 1) * C, :] for ci in range(nc)]
        c = _dn_common_b(bds, a_ref[...], d_ref[...], _stack_units(q_ref, nc), _stack_units(k_ref, nc), _stack_units(v_ref, nc))
        _store_units(u_ref, c["u"], nc)
        _store_units(w_ref, c["w"], nc)
        _store_units(qg_ref, c["qg"], nc)
        _store_units(kd_ref, c["kd"], nc)
        egl = jnp.broadcast_to(jnp.exp(c["g_last"]), (nc * DN_HEADS, 1, 128))
        for ci in range(nc):
            for h in range(DN_HEADS):
                aq_ref[h, ci * C:(ci + 1) * C, :] = c["aq"][ci * DN_HEADS + h]
                t_ref[h, ci * C:(ci + 1) * C, :] = c["t"][ci * DN_HEADS + h]
            egl_ref[ci * 8:(ci + 1) * 8, :] = jnp.concatenate(
                [egl[ci * DN_HEADS + h] for h in range(DN_HEADS)] + [jnp.zeros((8 - DN_HEADS, 128), F32)], axis=0)

    tok = lambda w: pl.BlockSpec((nc * C, w), lambda n: (n, 0))
    sq = pl.BlockSpec((DN_HEADS, nc * C, C), lambda n: (0, n, 0))
    vec = pl.BlockSpec((1, 128), lambda n: (0, 0))
    return pl.pallas_call(
        body, name="dn_prep", grid=(N // nc,),
        in_specs=[tok(DN_WIDTH)] * 3 + [tok(128), vec, vec],
        out_specs=[tok(DN_WIDTH)] * 4 + [sq, sq, pl.BlockSpec((nc * 8, 128), lambda n: (n, 0))],
        out_shape=[jax.ShapeDtypeStruct((S, DN_WIDTH), F32)] * 4 + [jax.ShapeDtypeStruct((DN_HEADS, S, C), F32)] * 2
                  + [jax.ShapeDtypeStruct((N * 8, 128), F32)],
        compiler_params=_params(1),
    )(qn, kn, v, bd, avec, dvec)


def _dn_scan_fwd(u, w, qg, kd, aq, egl, gate, dn_gain):
    S = u.shape[0]
    C = DN_CHUNK
    N = S // C
    HD = DN_HEAD_DIM
    nc = SCAN_CHUNKS

    def body(u_ref, w_ref, qg_ref, kd_ref, aq_ref, egl_ref, gate_ref, gain_ref, dn_ref, o_ref, vn_ref, st_ref, state_ref):
        @pl.when(pl.program_id(0) == 0)
        def _():
            state_ref[...] = jnp.zeros_like(state_ref)

        gain = gain_ref[...]
        for ci in range(nc):
            rows = slice(ci * C, (ci + 1) * C)
            st = state_ref[...]
            for h in range(DN_HEADS):
                st_ref[ci * DN_WIDTH + h * HD:ci * DN_WIDTH + (h + 1) * HD, :] = st[h]
            v_new = _stack_heads(u_ref, rows) - _bnn(_stack_heads(w_ref, rows), st)
            o = _bnn(_stack_heads(qg_ref, rows), st) + _bnn(aq_ref[:, rows, :], v_new)
            egl = jnp.stack([egl_ref[ci * 8 + h:ci * 8 + h + 1, :] for h in range(DN_HEADS)])
            state_ref[...] = st * egl + _btn(_stack_heads(kd_ref, rows), v_new)
            r = lax.rsqrt(jnp.mean(o * o, axis=-1, keepdims=True) + NORM_EPS)
            gt = _stack_heads(gate_ref, rows)
            dn = o * r * gain * (gt * _sigmoid(gt))
            for h in range(DN_HEADS):
                sl = slice(h * HD, (h + 1) * HD)
                vn_ref[rows, sl] = v_new[h]
                o_ref[rows, sl] = o[h]
                dn_ref[rows, sl] = dn[h]

    tok = lambda wd: pl.BlockSpec((nc * C, wd), lambda n: (n, 0))
    sq = pl.BlockSpec((DN_HEADS, nc * C, C), lambda n: (0, n, 0))
    vec = pl.BlockSpec((1, 128), lambda n: (0, 0))
    return pl.pallas_call(
        body, name="dn_scan_fwd", grid=(N // nc,),
        in_specs=[tok(DN_WIDTH)] * 4 + [sq, pl.BlockSpec((nc * 8, 128), lambda n: (n, 0)), tok(DN_WIDTH), vec],
        out_specs=[tok(DN_WIDTH)] * 3 + [pl.BlockSpec((nc * DN_WIDTH, HD), lambda n: (n, 0))],
        out_shape=[jax.ShapeDtypeStruct((S, DN_WIDTH), F32)] * 3 + [jax.ShapeDtypeStruct((N * DN_WIDTH, HD), F32)],
        scratch_shapes=[pltpu.VMEM((DN_HEADS, HD, HD), F32)],
        compiler_params=_params(1),
    )(u, w, qg, kd, aq, egl, gate, dn_gain)


def _dn_scan_bwd(w, qg, kd, aq, egl, gate, dn_gain, o, ddn):
    S = w.shape[0]
    C = DN_CHUNK
    N = S // C
    HD = DN_HEAD_DIM
    nc = SCAN_CHUNKS

    def body(w_ref, qg_ref, kd_ref, aq_ref, egl_ref, gate_ref, gain_ref, o_ref, ddn_ref,
             do_ref, dvn_ref, dgate_ref, dst_ref, small_ref, dstate_ref):
        @pl.when(pl.program_id(0) == 0)
        def _():
            dstate_ref[...] = jnp.zeros_like(dstate_ref)
            small_ref[...] = jnp.zeros_like(small_ref)

        gain = gain_ref[...]
        d_gain = jnp.zeros((1, 128), F32)
        for ci in reversed(range(nc)):
            rows = slice(ci * C, (ci + 1) * C)
            dsn = dstate_ref[...]
            for h in range(DN_HEADS):
                dst_ref[ci * DN_WIDTH + h * HD:ci * DN_WIDTH + (h + 1) * HD, :] = dsn[h]
            ov = _stack_heads(o_ref, rows)
            r = lax.rsqrt(jnp.mean(ov * ov, axis=-1, keepdims=True) + NORM_EPS)
            on = ov * r
            gt = _stack_heads(gate_ref, rows)
            sgt = _sigmoid(gt)
            silu_g = gt * sgt
            dy = _stack_heads(ddn_ref, rows)
            d_gain = d_gain + jnp.sum(jnp.sum(dy * on * silu_g, axis=1, keepdims=True), axis=0)
            dgate = dy * on * gain * (sgt * (1.0 + gt * (1.0 - sgt)))
            don = dy * gain * silu_g
            do = r * (don - on * jnp.mean(don * on, axis=-1, keepdims=True))
            d_vnew = _btn(aq_ref[:, rows, :], do) + _bnn(_stack_heads(kd_ref, rows), dsn)
            egl = jnp.stack([egl_ref[ci * 8 + h:ci * 8 + h + 1, :] for h in range(DN_HEADS)])
            dstate_ref[...] = _btn(_stack_heads(qg_ref, rows), do) + dsn * egl - _btn(_stack_heads(w_ref, rows), d_vnew)
            for h in range(DN_HEADS):
                sl = slice(h * HD, (h + 1) * HD)
                do_ref[rows, sl] = do[h]
                dvn_ref[rows, sl] = d_vnew[h]
                dgate_ref[rows, sl] = dgate[h]
        small_ref[...] += jnp.concatenate([d_gain, jnp.zeros((7, 128), F32)], axis=0)

    nb = N // nc
    tok = lambda wd: pl.BlockSpec((nc * C, wd), lambda i: (nb - 1 - i, 0))
    sq = pl.BlockSpec((DN_HEADS, nc * C, C), lambda i: (0, nb - 1 - i, 0))
    vec = pl.BlockSpec((1, 128), lambda i: (0, 0))
    return pl.pallas_call(
        body, name="dn_scan_bwd", grid=(nb,),
        in_specs=[tok(DN_WIDTH)] * 3 + [sq, pl.BlockSpec((nc * 8, 128), lambda i: (nb - 1 - i, 0)), tok(DN_WIDTH), vec,
                                       tok(DN_WIDTH), tok(DN_WIDTH)],
        out_specs=[tok(DN_WIDTH)] * 3 + [pl.BlockSpec((nc * DN_WIDTH, HD), lambda i: (nb - 1 - i, 0)),
                                        pl.BlockSpec((8, 128), lambda i: (0, 0))],
        out_shape=[jax.ShapeDtypeStruct((S, DN_WIDTH), F32)] * 3 + [jax.ShapeDtypeStruct((N * DN_WIDTH, HD), F32),
                                                                  jax.ShapeDtypeStruct((8, 128), F32)],
        scratch_shapes=[pltpu.VMEM((DN_HEADS, HD, HD), F32)],
        compiler_params=_params(1),
    )(w, qg, kd, aq, egl, gate, dn_gain, o, ddn)


def _dn_post(qn, kn, v, bd, avec, dvec, t_inv, v_new_all, states, dstates, do_all, dvn_all, comm=None):
    S = qn.shape[0]
    C = DN_CHUNK
    N = S // C
    HD = DN_HEAD_DIM
    nc = PREP_CHUNKS
    B = nc * DN_HEADS

    def body(q_ref, k_ref, v_ref, bd_ref, a_ref, d_ref, t_ref, vn_ref, st_ref, dst_ref, do_ref, dvn_ref,
             dq_ref, dk_ref, dv_ref, dbd_ref, small_ref):
        @pl.when(pl.program_id(0) == 0)
        def _():
            small_ref[...] = jnp.zeros_like(small_ref)

        avec = a_ref[...]
        bds = [bd_ref[ci * C:(ci + 1) * C, :] for ci in range(nc)]
        k = _stack_units(k_ref, nc)
        vv = _stack_units(v_ref, nc)
        t = jnp.concatenate([t_ref[:, ci * C:(ci + 1) * C, :] for ci in range(nc)], axis=0)
        c = _dn_common_b(bds, avec, d_ref[...], _stack_units(q_ref, nc), k, vv, t=t)
        q, kb, eg, u, w = c["q"], c["kb"], c["eg"], c["u"], c["w"]
        beta, decay, incl, strict, eye = c["beta"], c["decay"], c["incl"], c["strict"], c["eye"]
        st = jnp.stack([st_ref[b * HD:(b + 1) * HD, :] for b in range(B)])
        dsn = jnp.stack([dst_ref[b * HD:(b + 1) * HD, :] for b in range(B)])
        v_new = _stack_units(vn_ref, nc)
        do = _stack_units(do_ref, nc)
        d_vnew = _stack_units(dvn_ref, nc)
        egl = jnp.exp(c["g_last"])
        daq = jnp.where(incl, _bnt(do, v_new), 0.0)
        d_qg = _bnt(do, st)
        d_kd = _bnt(v_new, dsn)
        d_glast = jnp.sum(jnp.sum(dsn * st, axis=-1, keepdims=True), axis=1, keepdims=True) * egl
        d_w = -_bnt(d_vnew, st)
        d_ru = _btn(t, d_vnew)
        d_rw = _btn(t, d_w)
        da = -jnp.where(strict, _bnt(d_ru, u) + _bnt(d_rw, w), 0.0)
        dv = d_ru * beta
        dbeta = jnp.sum(d_ru * vv, axis=-1, keepdims=True)
        dkb = d_rw * eg
        dgc = jnp.sum(d_rw * c["rhs_w"], axis=-1, keepdims=True)
        dkk = da * decay
        ddecay = da * c["kk"]
        dkb = dkb + _bnn(dkk, k)
        dk = _btn(dkk, kb)
        dqk = daq * decay
        ddecay = ddecay + daq * c["qk"]
        dq = _bnn(dqk, k)
        dk = dk + _btn(dqk, q)
        m = ddecay * decay
        col_sum = jnp.sum(m, axis=1, keepdims=True)
        dgc = dgc + jnp.sum(m, axis=-1, keepdims=True) - jnp.sum(eye * col_sum, axis=-1, keepdims=True)
        dq = dq + d_qg * eg
        dgc = dgc + jnp.sum(d_qg * c["qg"], axis=-1, keepdims=True)
        dk = dk + d_kd * c["ekd"]
        tk = jnp.sum(d_kd * c["kd"], axis=-1, keepdims=True)
        dgc = dgc - tk
        d_glast = d_glast + jnp.sum(tk, axis=1, keepdims=True)
        dk = dk + dkb * beta
        dbeta = dbeta + jnp.sum(dkb * k, axis=-1, keepdims=True)
        dgc = dgc + jnp.where(c["last"], d_glast, 0.0)
        dgc_row = jnp.sum(eye * dgc, axis=1, keepdims=True)
        dgraw = jnp.sum(jnp.where(c["col"] >= c["row"], dgc_row, 0.0), axis=-1, keepdims=True)
        _store_units(dq_ref, dq * (HD ** -0.5), nc)
        _store_units(dk_ref, dk, nc)
        _store_units(dv_ref, dv, nc)
        dbraw = dbeta * beta * (1.0 - beta)
        dzc = dgraw * _sigmoid(c["zc"])
        ga = dgraw * c["graw"]
        lane = c["lane"]
        lane1 = lax.broadcasted_iota(jnp.int32, (1, 128), 1)
        neg_ea = -jnp.exp(avec)
        d_alog = jnp.zeros((1, 128), F32)
        d_dt = jnp.zeros((1, 128), F32)
        for ci in range(nc):
            dbd = jnp.zeros((C, 128), F32)
            for h in range(DN_HEADS):
                b = ci * DN_HEADS + h
                dz = dzc[b] * neg_ea
                dbd = dbd + jnp.where(lane == h, dbraw[b], 0.0) + jnp.where(lane == DN_HEADS + h, dz, 0.0)
                d_alog = d_alog + jnp.where(lane1 == DN_HEADS + h, jnp.sum(ga[b], axis=0, keepdims=True), 0.0)
                d_dt = d_dt + jnp.where(lane1 == DN_HEADS + h, jnp.sum(dz, axis=0, keepdims=True), 0.0)
            dbd_ref[ci * C:(ci + 1) * C, :] = dbd
        small_ref[...] += jnp.concatenate([d_alog, d_dt, jnp.zeros((6, 128), F32)], axis=0)

    tok = lambda wd: pl.BlockSpec((nc * C, wd), lambda n: (n, 0))
    big = pl.BlockSpec((nc * DN_WIDTH, HD), lambda n: (n, 0))
    sq = pl.BlockSpec((DN_HEADS, nc * C, C), lambda n: (0, n, 0))
    vec = pl.BlockSpec((1, 128), lambda n: (0, 0))
    return _call(
        body, (qn, kn, v, bd, avec, dvec, t_inv, v_new_all, states, dstates, do_all, dvn_all),
        name="dn_post", grid=(N // nc,), comm=comm,
        in_specs=[tok(DN_WIDTH)] * 3 + [tok(128), vec, vec, sq, tok(DN_WIDTH), big, big, tok(DN_WIDTH), tok(DN_WIDTH)],
        out_specs=[tok(DN_WIDTH)] * 3 + [tok(128), pl.BlockSpec((8, 128), lambda n: (0, 0))],
        out_shape=[jax.ShapeDtypeStruct((S, DN_WIDTH), F32)] * 3 + [jax.ShapeDtypeStruct((S, 128), F32),
                                                                  jax.ShapeDtypeStruct((8, 128), F32)])


def _outproj_fwd(x, attn, dn, w_out):
    S, D = x.shape
    tm = 512

    def body(x_ref, a_ref, d_ref, w_ref, xo_ref, mix_ref):
        a = a_ref[...].astype(BF16)
        dd = d_ref[...].astype(BF16)
        mix_ref[:, 0:ATTN_WIDTH] = a
        mix_ref[:, ATTN_WIDTH:] = dd
        xo_ref[...] = x_ref[...] + _nn(a, w_ref[0:ATTN_WIDTH, :]) + _nn(dd, w_ref[ATTN_WIDTH:, :])

    tok = lambda w: pl.BlockSpec((tm, w), lambda i: (i, 0))
    return pl.pallas_call(
        body, name="outproj_fwd", grid=(S // tm,),
        in_specs=[tok(D), tok(ATTN_WIDTH), tok(DN_WIDTH), pl.BlockSpec((D, D), lambda i: (0, 0))],
        out_specs=[tok(D), tok(D)],
        out_shape=[jax.ShapeDtypeStruct((S, D), F32), jax.ShapeDtypeStruct((S, D), BF16)],
        compiler_params=_params(1),
    )(x, attn, dn, w_out)


def _outproj_bwd(dx, w_out, attn):
    S, D = dx.shape
    tm = VIEW_TILE

    def body(dx_ref, w_ref, attn_ref, da1, da4, da16, dl1, dl4, dl16, ddn_ref, dxb_ref, planes):
        d = dx_ref[...].astype(BF16)
        dxb_ref[...] = d
        da = _nt(d, w_ref[0:ATTN_WIDTH, :])
        ddn_ref[...] = _nt(d, w_ref[ATTN_WIDTH:, :])
        _tile_to_views(da, planes, (da1, da4, da16))
        lo = lax.broadcasted_iota(jnp.int32, (tm, 128), 1) < 64
        cols = []
        for G in range(4):
            sl = slice(G * 128, (G + 1) * 128)
            t = da[:, sl] * attn_ref[:, sl]
            d0 = jnp.sum(jnp.where(lo, t, 0.0), axis=-1, keepdims=True)
            d1 = jnp.sum(jnp.where(lo, 0.0, t), axis=-1, keepdims=True)
            cols.append(jnp.where(lo, d0, d1))
        _tile_to_views(jnp.concatenate(cols, axis=1), planes, (dl1, dl4, dl16))

    tok = lambda w: pl.BlockSpec((tm, w), lambda i: (i, 0))
    views = [_view_spec(d) for d in DILATIONS]
    return pl.pallas_call(
        body, name="outproj_bwd", grid=(S // tm,),
        in_specs=[tok(D), pl.BlockSpec((D, D), lambda i: (0, 0)), tok(ATTN_WIDTH)],
        out_specs=views + views + [tok(DN_WIDTH), tok(D)],
        out_shape=[_view_shape(S, d, F32) for d in DILATIONS] * 2
                  + [jax.ShapeDtypeStruct((S, DN_WIDTH), F32), jax.ShapeDtypeStruct((S, D), BF16)],
        scratch_shapes=[pltpu.VMEM((4, tm, 128), F32)],
        compiler_params=_params(1),
    )(dx, w_out, attn)


def _loss_head(x, gain, target):
    S, D = x.shape
    tm = 512

    def body(x_ref, gain_ref, t_ref, loss_ref, dx_ref, dgain_ref):
        @pl.when(pl.program_id(0) == 0)
        def _():
            loss_ref[...] = jnp.zeros_like(loss_ref)
            dgain_ref[...] = jnp.zeros_like(dgain_ref)

        xf = x_ref[...]
        gain = gain_ref[...]
        r = lax.rsqrt(jnp.mean(xf * xf, axis=-1, keepdims=True) + NORM_EPS)
        xhat = xf * r
        err = xhat * gain - t_ref[...]
        part = 0.5 * jnp.sum(jnp.mean(err * err, axis=-1, keepdims=True), axis=0, keepdims=True)
        first = (lax.broadcasted_iota(jnp.int32, (8, 128), 0) == 0) & (lax.broadcasted_iota(jnp.int32, (8, 128), 1) == 0)
        loss_ref[...] += jnp.where(first, part, 0.0)
        dy = err * (1.0 / D)
        dgain_ref[...] += jnp.sum(dy * xhat, axis=0, keepdims=True)
        dxh = dy * gain
        dx_ref[...] = r * (dxh - xhat * jnp.mean(dxh * xhat, axis=-1, keepdims=True))

    tok = pl.BlockSpec((tm, D), lambda i: (i, 0))
    row = pl.BlockSpec((1, D), lambda i: (0, 0))
    return pl.pallas_call(
        body, name="loss_head", grid=(S // tm,),
        in_specs=[tok, row, tok],
        out_specs=[pl.BlockSpec((8, 128), lambda i: (0, 0)), tok, row],
        out_shape=[jax.ShapeDtypeStruct((8, 128), F32), jax.ShapeDtypeStruct((S, D), F32),
                   jax.ShapeDtypeStruct((1, D), F32)],
        compiler_params=_params(1),
    )(x, gain, target)


def _adamw(w, g, m, v, name):
    R, Ccols = w.shape
    tr = R
    for cand in (256, 128, 64, 32, 16, 8):
        if R % cand == 0:
            tr = cand
            break
    c1 = 1.0 - ADAM_B1 ** ADAM_STEP
    c2 = 1.0 - ADAM_B2 ** ADAM_STEP

    def body(w_ref, g_ref, m_ref, v_ref, d_ref, nm_ref, nv_ref):
        gv = g_ref[...]
        mn = ADAM_B1 * m_ref[...] + (1.0 - ADAM_B1) * gv
        vn = ADAM_B2 * v_ref[...] + (1.0 - ADAM_B2) * (gv * gv)
        nm_ref[...] = mn
        nv_ref[...] = vn
        d_ref[...] = -ADAM_LR * ((mn / c1) / (jnp.sqrt(vn / c2) + ADAM_EPS) + ADAM_WD * w_ref[...])

    spec = pl.BlockSpec((tr, Ccols), lambda i: (i, 0))
    return pl.pallas_call(
        body, name=name, grid=(R // tr,), in_specs=[spec] * 4, out_specs=[spec] * 3,
        out_shape=[jax.ShapeDtypeStruct((R, Ccols), F32)] * 3, compiler_params=_params(1),
    )(w, g, m, v)


LATE_WEIGHTS = ("w_out", "ffn2_gate", "ffn2_up", "ffn2_down")


def _local_step(x, target, wts, small, dist=None):
    g1, g2, gm, gf = small["norm_ffn1"], small["norm_ffn2"], small["norm_mix"], small["norm_final"]
    wts = dict(wts)

    def reduce_start(gs, tag):
        return _rs_add_pairs(gs, _swap_sibling(gs, True, "rs_swap_halves_" + tag), dist["c"], "rs_add_pairs_" + tag)

    (x1, h1, fg1, fu1), late = _ffn_fwd(x, g1, wts["ffn1_gate"], wts["ffn1_up"], wts["ffn1_down"], "ffn1_fwd",
                                        comm=_ag_comm(dist["late"]) if dist else None)
    if dist:
        wts.update(zip(LATE_WEIGHTS, late))
        wts["w_out"] = wts["w_out"].reshape(D_MODEL, D_MODEL)
    h2, *qkv, xq, xk, xv, gate, bd = _inproj_fwd(x1, gm, wts["w_in"])
    aq, ak, av = qkv[0:3], qkv[3:6], qkv[6:9]
    parts = [_attn_fwd(aq[p], ak[p], av[p], d, f"attn_fwd_d{d}") for p, d in enumerate(DILATIONS)]
    attn, *lse = _attn_merge(parts)
    conv_w = small["conv_w"]
    qn, kn, vv = _conv_fwd(xq, xk, xv, conv_w)
    dn_u, dn_w, dn_qg, dn_kd, dn_aq, dn_t, dn_egl = _dn_prep(qn, kn, vv, bd, small["avec"], small["dvec"])
    dn, o_dn, v_new, states = _dn_scan_fwd(dn_u, dn_w, dn_qg, dn_kd, dn_aq, dn_egl, gate, small["dn_norm"])
    x2, mix = _outproj_fwd(x1, attn, dn, wts["w_out"])
    (x3, h3, fg2, fu2), _ = _ffn_fwd(x2, g2, wts["ffn2_gate"], wts["ffn2_up"], wts["ffn2_down"], "ffn2_fwd")
    loss, dx3, d_gf = _loss_head(x3, gf, target)

    grads = {}
    (dx2, d_g2, dfg2, dfu2, act2, dout2), _ = _ffn_bwd(dx3, x2, g2, fg2, fu2, wts["ffn2_down"], wts["ffn2_gate"],
                                                      wts["ffn2_up"], "ffn2_bwd")
    tk = 2048
    grads["ffn2_gate"], _ = _dw_chunks(dfg2, h3, tk, "dw_ffn2_gate")
    grads["ffn2_up"], _ = _dw_chunks(dfu2, h3, tk, "dw_ffn2_up")
    grads["ffn2_down"], _ = _dw_chunks(act2, dout2, tk, "dw_ffn2_down")
    group_a = ("ffn2_gate", "ffn2_up", "ffn2_down")
    parts_a = reduce_start([grads[n] for n in group_a], "a") if dist else None

    *dviews, ddn, dx2b = _outproj_bwd(dx2, wts["w_out"], attn)
    dattn, dd = dviews[0:3], dviews[3:6]
    grads["w_out"] = _matmul_tn(mix, dx2b, D_MODEL, tk, "dw_out").reshape(N_CHIPS, D_MODEL // N_CHIPS, D_MODEL)

    daq, dak, dav = [], [], []
    for p, d in enumerate(DILATIONS):
        daq.append(_attn_bwd_q(aq[p], ak[p], av[p], dattn[p], lse[p], dd[p], d, f"attn_bwd_q_d{d}"))
        dk_p, dv_p = _attn_bwd_kv(aq[p], ak[p], av[p], dattn[p], lse[p], dd[p], d, f"attn_bwd_kv_d{d}")
        dak.append(dk_p)
        dav.append(dv_p)

    do_dn, dvn, dgate, dstates, d_dn_gain = _dn_scan_bwd(dn_w, dn_qg, dn_kd, dn_aq, dn_egl, gate, small["dn_norm"], o_dn, ddn)
    (dqn, dkn, dvv, dbd, dn_small), recv_a = _dn_post(qn, kn, vv, bd, small["avec"], small["dvec"], dn_t, v_new, states,
                                                      dstates, do_dn, dvn, comm=_rsx_comm(parts_a) if dist else None)
    dcq, dck, dcv, dwq, dwk, dwv = _conv_bwd_pre(xq, xk, xv, conv_w, dqn, dkn, dvv)
    dxq, dxk, dxv = _conv_bwd_x(dcq, dck, dcv, conv_w)
    d_conv = jnp.concatenate([dwq[:CONV_WIDTH], dwk[:CONV_WIDTH], dwv[:CONV_WIDTH]], axis=1)

    dx1, d_gm, dproj = _inproj_bwd(dx2, x1, gm, [daq, dak, dav], [dxq, dxk, dxv, dgate], dbd, wts["w_in"])
    gi = _matmul_tn(dproj, h2, IN_COLS_PADDED, 512, "dw_in")
    if dist:
        gi = jnp.concatenate([gi[:3072], gi[3584:3592], gi[3072:3584]], axis=0).reshape(N_CHIPS, IN_COLS // N_CHIPS, D_MODEL)
        gi = jnp.pad(gi, ((0, 0), (0, W_IN_ROWS - IN_COLS // N_CHIPS), (0, 0)))
    grads["w_in"] = gi
    group_b = ("w_in", "w_out")
    parts_b = reduce_start([grads[n] for n in group_b], "b") if dist else None

    (dx0, d_g1, dfg1, dfu1, act1, dout1), _ = _ffn_bwd(dx1, x, g1, fg1, fu1, wts["ffn1_down"], wts["ffn1_gate"],
                                                      wts["ffn1_up"], "ffn1_bwd")
    group_c = ("ffn1_gate", "ffn1_up", "ffn1_down")
    pending = parts_b if dist else []
    parts_c, recv_bc = [], []
    for n, (lhs, rhs) in zip(group_c, ((dfg1, h1), (dfu1, h1), (act1, dout1))):
        grads[n], landed = _dw_chunks(lhs, rhs, tk, "dw_" + n, comm=_rsx_comm(pending) if dist else None)
        recv_bc += list(landed)
        if dist:
            pending = reduce_start([grads[n]], n)
            parts_c += pending

    small_grads = dict(norm_ffn1=d_g1, norm_mix=d_gm, norm_ffn2=d_g2, norm_final=d_gf, conv_w=d_conv,
                       a_log=dn_small[0:1], dt_bias=dn_small[1:2], dn_norm=d_dn_gain[0:1])
    if dist:
        recv_bc += _rs_exchange_arrays(pending)
        recv_b, recv_c = recv_bc[:len(parts_b)], recv_bc[len(parts_b):]
        names = group_a + group_b + group_c
        totals = _rs_add_totals(list(parts_a) + list(parts_b) + list(parts_c), list(recv_a) + list(recv_b) + list(recv_c),
                                dist["chip"])
        theirs = _swap_sibling(totals, False, "rs_share_total")
        grads = {n: (mine, other) for n, mine, other in zip(names, totals, theirs)}
    return loss, dx0, grads, small_grads


PACK_SECTIONS = (("ffn1_gate", 704), ("ffn1_up", 704), ("ffn1_down", 704), ("w_in", 898), ("w_out", 256),
                 ("ffn2_gate", 704), ("ffn2_up", 704), ("ffn2_down", 704))
PACK_ROWS = 5408
HALF_ROWS = PACK_ROWS // 2
ADD_ROWS = 208

HBM = pl.BlockSpec(memory_space=pl.ANY)
VMEM_SPEC = pl.BlockSpec(memory_space=pltpu.VMEM)


def _coords():
    return lax.axis_index("x"), lax.axis_index("y"), lax.axis_index("c")


def _remote(src, dst, send_sems, recv_sems, k, dev):
    return pltpu.make_async_remote_copy(src_ref=src, dst_ref=dst, send_sem=send_sems.at[k], recv_sem=recv_sems.at[k],
                                        device_id=dev, device_id_type=MESH)


def _allreduce_small(buf, name):
    R, Cc = buf.shape

    def body(src_ref, out_ref, recv_ref, send_sems, recv_sems):
        x, y, c = _coords()
        copies = []
        for m in range(1, 8):
            fx, fy, fc = (m >> 2) & 1, (m >> 1) & 1, m & 1
            dev = (x ^ fx if fx else x, y ^ fy if fy else y, c ^ fc if fc else c)
            cp = _remote(src_ref, recv_ref.at[m - 1], send_sems, recv_sems, m - 1, dev)
            cp.start()
            copies.append(cp)
        for cp in copies:
            cp.wait()
        r = [src_ref[...]] + [recv_ref[m] for m in range(7)]
        out_ref[...] = ((r[0] + r[1]) + (r[2] + r[3])) + ((r[4] + r[5]) + (r[6] + r[7]))

    return pl.pallas_call(
        body, name=name, out_shape=jax.ShapeDtypeStruct((R, Cc), F32),
        in_specs=[VMEM_SPEC], out_specs=VMEM_SPEC,
        scratch_shapes=[pltpu.VMEM((7, R, Cc), F32), pltpu.SemaphoreType.DMA((7,)), pltpu.SemaphoreType.DMA((7,))],
    )(buf)


def _allgather_weights(pack2):
    _, Hh, Cc = pack2.shape

    def body(src_ref, out_ref, send_sems, recv_sems):
        x, y, c = _coords()
        sib = (x, y, 1 - c)
        others = [(1 - x, y), (x, 1 - y), (1 - x, 1 - y)]
        blk = lambda cx, cy, half: out_ref.at[2 * cx + cy, half]
        mine = _remote(src_ref, out_ref.at[2 * x + y], send_sems, recv_sems, 6, sib)
        mine.start()
        first = [_remote(src_ref.at[c], blk(x, y, c), send_sems, recv_sems, j, (ox, oy, c)) for j, (ox, oy) in enumerate(others)]
        for cp in first:
            cp.start()
        passed = [_remote(blk(ox, oy, c), blk(ox, oy, c), send_sems, recv_sems, 3 + j, sib) for j, (ox, oy) in enumerate(others)]
        for j, (ox, oy) in enumerate(others):
            _remote(src_ref.at[c], blk(ox, oy, c), send_sems, recv_sems, j, (ox, oy, c)).wait_recv()
            passed[j].start()
        for j, (ox, oy) in enumerate(others):
            _remote(src_ref.at[c], blk(ox, oy, 1 - c), send_sems, recv_sems, 3 + j, sib).wait_recv()
        for cp in first + passed:
            cp.wait_send()
        mine.wait()

    return pl.pallas_call(
        body, name="allgather_weights", out_shape=jax.ShapeDtypeStruct((N_CHIPS, 2, Hh, Cc), pack2.dtype),
        in_specs=[HBM], out_specs=HBM,
        scratch_shapes=[pltpu.SemaphoreType.DMA((7,)), pltpu.SemaphoreType.DMA((7,))],
    )(pack2)


def _rs_swap_halves(gpack):
    _, nj, Hh, Cc = gpack.shape

    def body(src_ref, out_ref, send_sems, recv_sems):
        x, y, c = _coords()
        cp = _remote(src_ref.at[1 - c], out_ref, send_sems, recv_sems, 0, (x, y, 1 - c))
        cp.start()
        cp.wait()

    return pl.pallas_call(
        body, name="rs_swap_halves", out_shape=jax.ShapeDtypeStruct((nj, Hh, Cc), gpack.dtype),
        in_specs=[HBM], out_specs=HBM,
        scratch_shapes=[pltpu.SemaphoreType.DMA((1,)), pltpu.SemaphoreType.DMA((1,))],
    )(gpack)


def _rs_add_pair(gpack, other, c):
    _, nj, Hh, Cc = gpack.shape
    tr = ADD_ROWS

    def body(c_ref, a_ref, b_ref, o_ref):
        o_ref[...] = (a_ref[...] + b_ref[...]).astype(BF16)

    return pl.pallas_call(
        body, name="rs_add_pair",
        grid_spec=pltpu.PrefetchScalarGridSpec(
            num_scalar_prefetch=1, grid=(nj, Hh // tr),
            in_specs=[pl.BlockSpec((None, None, tr, Cc), lambda j, i, c_ref: (c_ref[0], j, i, 0)),
                      pl.BlockSpec((None, tr, Cc), lambda j, i, c_ref: (j, i, 0))],
            out_specs=pl.BlockSpec((None, tr, Cc), lambda j, i, c_ref: (j, i, 0))),
        out_shape=jax.ShapeDtypeStruct((nj, Hh, Cc), BF16),
        compiler_params=_params(2),
    )(c, gpack, other)


def _rs_exchange_chips(part):
    nj, Hh, Cc = part.shape

    def body(src_ref, out_ref, send_sems, recv_sems):
        x, y, c = _coords()
        others = [(1 - x, y), (x, 1 - y), (1 - x, 1 - y)]
        cps = [_remote(src_ref.at[2 * ox + oy], out_ref.at[k], send_sems, recv_sems, k, (ox, oy, c))
               for k, (ox, oy) in enumerate(others)]
        for cp in cps:
            cp.start()
        for cp in cps:
            cp.wait()

    return pl.pallas_call(
        body, name="rs_exchange_chips", out_shape=jax.ShapeDtypeStruct((3, Hh, Cc), part.dtype),
        in_specs=[HBM], out_specs=HBM,
        scratch_shapes=[pltpu.SemaphoreType.DMA((3,)), pltpu.SemaphoreType.DMA((3,))],
    )(part)


def _rs_add_total(part, recv, chip):
    nj, Hh, Cc = part.shape
    tr = ADD_ROWS

    def body(chip_ref, p_ref, r0_ref, r1_ref, r2_ref, o_ref):
        f = lambda r: r[...].astype(F32)
        o_ref[...] = (f(p_ref) + f(r0_ref)) + (f(r1_ref) + f(r2_ref))

    rk = lambda k: pl.BlockSpec((None, tr, Cc), lambda i, chip_ref, k=k: (k, i, 0))
    return pl.pallas_call(
        body, name="rs_add_total",
        grid_spec=pltpu.PrefetchScalarGridSpec(
            num_scalar_prefetch=1, grid=(Hh // tr,),
            in_specs=[pl.BlockSpec((None, tr, Cc), lambda i, chip_ref: (chip_ref[0], i, 0)), rk(0), rk(1), rk(2)],
            out_specs=pl.BlockSpec((tr, Cc), lambda i, chip_ref: (i, 0))),
        out_shape=jax.ShapeDtypeStruct((Hh, Cc), F32),
        compiler_params=_params(1),
    )(chip, part, recv, recv, recv)


def _rs_share_total(total):
    Hh, Cc = total.shape

    def body(src_ref, out_ref, send_sems, recv_sems):
        x, y, c = _coords()
        cp = _remote(src_ref, out_ref, send_sems, recv_sems, 0, (x, y, 1 - c))
        cp.start()
        cp.wait()

    return pl.pallas_call(
        body, name="rs_share_total", out_shape=jax.ShapeDtypeStruct((Hh, Cc), total.dtype),
        in_specs=[HBM], out_specs=HBM,
        scratch_shapes=[pltpu.SemaphoreType.DMA((1,)), pltpu.SemaphoreType.DMA((1,))],
    )(total)


BIG = ("ffn1_gate", "ffn1_up", "ffn1_down", "w_in", "w_out", "ffn2_gate", "ffn2_up", "ffn2_down")
W_IN_ROWS = 960


def _rows(ref, start, size):
    return ref.at[pl.ds(pl.multiple_of(start, 16), size)]


def _allgather_arrays(shards):
    n = len(shards)

    def body(*refs):
        _ag_start(refs[:n], refs[n:2 * n], refs[2 * n], refs[2 * n + 1])
        _ag_finish(refs[:n], refs[n:2 * n], refs[2 * n], refs[2 * n + 1])

    _, shapes, n_sems, _, _ = _ag_comm(shards)
    return pl.pallas_call(
        body, name="allgather_weights", out_shape=shapes, in_specs=[HBM] * n, out_specs=[HBM] * n,
        scratch_shapes=[pltpu.SemaphoreType.DMA((n_sems,)), pltpu.SemaphoreType.DMA((n_sems,))],
    )(*shards)


def _ag_copies(srcs, outs, send_sems, recv_sems):
    x, y, c = _coords()
    sib = (x, y, 1 - c)
    me = 2 * x + y
    others = [(1 - x, y), (x, 1 - y), (1 - x, 1 - y)]
    plan = []
    for a, (src, out) in enumerate(zip(srcs, outs)):
        h = src.shape[0] // 2
        cp = lambda s, d, k, dev: _remote(s, d, send_sems, recv_sems, 7 * a + k, dev)
        own = cp(src, out.at[me], 6, sib)
        sends = [cp(_rows(src, c * h, h), _rows(out.at[me], c * h, h), j, (ox, oy, c)) for j, (ox, oy) in enumerate(others)]
        mine = [_rows(out.at[2 * ox + oy], c * h, h) for ox, oy in others]
        theirs = [_rows(out.at[2 * ox + oy], (1 - c) * h, h) for ox, oy in others]
        arrivals = [cp(m, m, j, sib) for j, m in enumerate(mine)]
        forwards = [cp(m, m, 3 + j, sib) for j, m in enumerate(mine)]
        forwarded = [cp(t, t, 3 + j, sib) for j, t in enumerate(theirs)]
        plan.append((own, sends, forwards, arrivals, forwarded))
    return plan


def _ag_start(srcs, outs, send_sems, recv_sems):
    for own, sends, _, _, _ in _ag_copies(srcs, outs, send_sems, recv_sems):
        own.start()
        for cp in sends:
            cp.start()


def _ag_finish(srcs, outs, send_sems, recv_sems):
    plan = _ag_copies(srcs, outs, send_sems, recv_sems)
    for _, _, forwards, arrivals, _ in plan:
        for arrived, fwd in zip(arrivals, forwards):
            arrived.wait_recv()
            fwd.start()
    for own, sends, forwards, _, forwarded in plan:
        for cp in forwarded:
            cp.wait_recv()
        own.wait_recv()
        for cp in [own] + sends + forwards:
            cp.wait_send()


def _ag_comm(shards):
    shapes = [jax.ShapeDtypeStruct((N_CHIPS,) + s.shape, s.dtype) for s in shards]
    return (list(shards), shapes, 7 * len(shards), _ag_start, _ag_finish)


def _swap_sibling(arrs, pick_other_half, name):
    n = len(arrs)
    outs = [jax.ShapeDtypeStruct((a.shape[0], a.shape[1] // 2) + a.shape[2:] if pick_other_half else a.shape, a.dtype) for a in arrs]

    def body(*refs):
        srcs, dsts, send_sems, recv_sems = refs[:n], refs[n:2 * n], refs[2 * n], refs[2 * n + 1]
        x, y, c = _coords()
        cps = []
        for a in range(n):
            src = srcs[a]
            if pick_other_half:
                h = src.shape[1] // 2
                src = src.at[:, pl.ds(pl.multiple_of((1 - c) * h, 16), h)]
            cp = _remote(src, dsts[a], send_sems, recv_sems, a, (x, y, 1 - c))
            cp.start()
            cps.append(cp)
        for cp in cps:
            cp.wait()

    return pl.pallas_call(
        body, name=name, out_shape=outs, in_specs=[HBM] * n, out_specs=[HBM] * n,
        scratch_shapes=[pltpu.SemaphoreType.DMA((n,)), pltpu.SemaphoreType.DMA((n,))],
    )(*arrs)


def _rs_add_pairs(gs, others, c, name):
    n = len(gs)
    hbs = [g.shape[1] // 4 for g in gs]

    def body(c_ref, *refs):
        for a in range(n):
            refs[2 * n + a][...] = (refs[a][...] + refs[n + a][...]).astype(BF16)

    mine = lambda hb: pl.BlockSpec((None, hb, D_MODEL), lambda j, s, c_ref: (j, c_ref[0] * 2 + s, 0))
    flat = lambda hb: pl.BlockSpec((None, hb, D_MODEL), lambda j, s, c_ref: (j, s, 0))
    return pl.pallas_call(
        body, name=name,
        grid_spec=pltpu.PrefetchScalarGridSpec(
            num_scalar_prefetch=1, grid=(N_CHIPS, 2),
            in_specs=[mine(hb) for hb in hbs] + [flat(hb) for hb in hbs],
            out_specs=[flat(hb) for hb in hbs]),
        out_shape=[jax.ShapeDtypeStruct(o.shape, BF16) for o in others],
        compiler_params=_params(2),
    )(c, *gs, *others)


def _rs_exchange_arrays(parts):
    n = len(parts)

    def body(*refs):
        _rsx_start(refs[:n], refs[n:2 * n], refs[2 * n], refs[2 * n + 1])
        _rsx_finish(refs[:n], refs[n:2 * n], refs[2 * n], refs[2 * n + 1])

    _, shapes, n_sems, _, _ = _rsx_comm(parts)
    return pl.pallas_call(
        body, name="rs_exchange_chips", out_shape=shapes, in_specs=[HBM] * n, out_specs=[HBM] * n,
        scratch_shapes=[pltpu.SemaphoreType.DMA((n_sems,)), pltpu.SemaphoreType.DMA((n_sems,))],
    )(*parts)


def _rsx_copies(srcs, dsts, send_sems, recv_sems):
    x, y, c = _coords()
    others = [(1 - x, y), (x, 1 - y), (1 - x, 1 - y)]
    return [_remote(src.at[2 * ox + oy], dst.at[k], send_sems, recv_sems, 3 * a + k, (ox, oy, c))
            for a, (src, dst) in enumerate(zip(srcs, dsts)) for k, (ox, oy) in enumerate(others)]


def _rsx_start(srcs, dsts, send_sems, recv_sems):
    for cp in _rsx_copies(srcs, dsts, send_sems, recv_sems):
        cp.start()


def _rsx_finish(srcs, dsts, send_sems, recv_sems):
    for cp in _rsx_copies(srcs, dsts, send_sems, recv_sems):
        cp.wait()


def _rsx_comm(parts):
    shapes = [jax.ShapeDtypeStruct((3,) + p.shape[1:], p.dtype) for p in parts]
    return (list(parts), shapes, 3 * len(parts), _rsx_start, _rsx_finish)


def _rs_add_totals(parts, recvs, chip):
    n = len(parts)
    hbs = [p.shape[1] // 2 for p in parts]

    def body(chip_ref, *refs):
        f = lambda r: r[...].astype(F32)
        for a in range(n):
            p, r0, r1, r2 = refs[a], refs[n + 3 * a], refs[n + 3 * a + 1], refs[n + 3 * a + 2]
            refs[4 * n + a][...] = (f(p) + f(r0)) + (f(r1) + f(r2))

    own = lambda hb: pl.BlockSpec((None, hb, D_MODEL), lambda s, chip_ref: (chip_ref[0], s, 0))
    slot = lambda hb, k: pl.BlockSpec((None, hb, D_MODEL), lambda s, chip_ref, k=k: (k, s, 0))
    recv_specs = [slot(hb, k) for hb in hbs for k in range(3)]
    recv_args = [r for r in recvs for _ in range(3)]
    return pl.pallas_call(
        body, name="rs_add_totals",
        grid_spec=pltpu.PrefetchScalarGridSpec(
            num_scalar_prefetch=1, grid=(2,),
            in_specs=[own(hb) for hb in hbs] + recv_specs,
            out_specs=[pl.BlockSpec((hb, D_MODEL), lambda s, chip_ref: (s, 0)) for hb in hbs]),
        out_shape=[jax.ShapeDtypeStruct(p.shape[1:], F32) for p in parts],
        compiler_params=_params(1),
    )(chip, *parts, *recv_args)


def _permute_w_in(w):
    return jnp.concatenate([w[:, :3072], w[:, 3080:IN_COLS], w[:, 3072:3080],
                            jnp.zeros((w.shape[0], IN_COLS_PADDED - IN_COLS), w.dtype)], axis=1)


def _pack_rows(shards, dtype):
    rows = [shards[n].astype(dtype).reshape(r, D_MODEL) for n, r in PACK_SECTIONS]
    used = sum(r for _, r in PACK_SECTIONS)
    rows.append(jnp.zeros((PACK_ROWS - used, D_MODEL), dtype))
    return jnp.concatenate(rows, axis=0)


def _unpack_rows(pack, shapes):
    out, at = {}, 0
    for n, r in PACK_SECTIONS:
        out[n] = pack[at:at + r].reshape(shapes[n])
        at += r
    return out


SHARD_SHAPES = dict(ffn1_gate=(1024, 704), ffn1_up=(1024, 704), ffn1_down=(704, 1024), w_in=(1024, 898),
                    w_out=(256, 1024), ffn2_gate=(1024, 704), ffn2_up=(1024, 704), ffn2_down=(704, 1024))
ROW_SHARDED = ("ffn1_down", "w_out", "ffn2_down")
SMALL_ROWS = 16


def _pad_row(v):
    v = v.reshape(1, -1)
    return jnp.pad(v, ((0, 0), (0, D_MODEL - v.shape[1])))


def kernel(x, norm_ffn1, ffn1_gate, ffn1_up, ffn1_down, norm_mix, w_in, conv_w, a_log, dt_bias, dn_norm, w_out, norm_ffn2, ffn2_gate, ffn2_up, ffn2_down, norm_final, loss_target, m_norm_ffn1, m_ffn1_gate, m_ffn1_up, m_ffn1_down, m_norm_mix, m_w_in, m_conv_w, m_a_log, m_dt_bias, m_dn_norm, m_w_out, m_norm_ffn2, m_ffn2_gate, m_ffn2_up, m_ffn2_down, m_norm_final, v_norm_ffn1, v_ffn1_gate, v_ffn1_up, v_ffn1_down, v_norm_mix, v_w_in, v_conv_w, v_a_log, v_dt_bias, v_dn_norm, v_w_out, v_norm_ffn2, v_ffn2_gate, v_ffn2_up, v_ffn2_down, v_norm_final):
    cx, cy, cc = _coords()
    chip = 2 * cx + cy
    big_w = dict(ffn1_gate=ffn1_gate[0], ffn1_up=ffn1_up[0], ffn1_down=ffn1_down[0], w_in=w_in[0], w_out=w_out[0],
                 ffn2_gate=ffn2_gate[0], ffn2_up=ffn2_up[0], ffn2_down=ffn2_down[0])
    big_m = dict(ffn1_gate=m_ffn1_gate[0], ffn1_up=m_ffn1_up[0], ffn1_down=m_ffn1_down[0], w_in=m_w_in[0], w_out=m_w_out[0],
                 ffn2_gate=m_ffn2_gate[0], ffn2_up=m_ffn2_up[0], ffn2_down=m_ffn2_down[0])
    big_v = dict(ffn1_gate=v_ffn1_gate[0], ffn1_up=v_ffn1_up[0], ffn1_down=v_ffn1_down[0], w_in=v_w_in[0], w_out=v_w_out[0],
                 ffn2_gate=v_ffn2_gate[0], ffn2_up=v_ffn2_up[0], ffn2_down=v_ffn2_down[0])

    early = tuple(n for n in BIG if n not in LATE_WEIGHTS)
    wts = dict(zip(early, _allgather_arrays([big_w[n].astype(BF16) for n in early])))
    wts["w_in"] = _permute_w_in(jnp.concatenate([wts["w_in"][j] for j in range(N_CHIPS)], axis=1))
    dist = dict(late=[big_w[n].astype(BF16) for n in LATE_WEIGHTS], c=cc.reshape(1).astype(jnp.int32),
                chip=chip.reshape(1).astype(jnp.int32))

    conv_shard = conv_w[0]
    emb = jnp.concatenate([jnp.where((chip == j) & (cc == 0), conv_shard, 0.0) for j in range(N_CHIPS)], axis=1)
    emb = jnp.pad(emb.reshape(6, D_MODEL), ((0, 2), (0, 0)))
    conv_full = _allreduce_small(emb, "allgather_conv_w")[:6].reshape(CONV_WIDTH, 3 * DN_WIDTH)

    zvec = jnp.zeros((1, 128), F32)
    small = dict(norm_ffn1=norm_ffn1, norm_mix=norm_mix, norm_ffn2=norm_ffn2, norm_final=norm_final[None],
                 conv_w=conv_full, avec=zvec.at[0, DN_HEADS:2 * DN_HEADS].set(a_log[0]),
                 dvec=zvec.at[0, DN_HEADS:2 * DN_HEADS].set(dt_bias[0]), dn_norm=dn_norm)

    loss, grad_x, reduced, sg = _local_step(x[0], loss_target[0], wts, small, dist)

    rows = [sg["norm_ffn1"], sg["norm_mix"], sg["norm_ffn2"], sg["norm_final"], _pad_row(sg["a_log"]), _pad_row(sg["dt_bias"]),
            _pad_row(sg["dn_norm"]), _pad_row(loss[0:1]), sg["conv_w"].reshape(6, D_MODEL), jnp.zeros((2, D_MODEL), F32)]
    red = _allreduce_small(jnp.concatenate(rows, axis=0), "allreduce_small")
    loss_out = red[7, 0]
    g_conv_full = red[8:14].reshape(CONV_WIDTH, 3 * DN_WIDTH)
    g_conv = lax.dynamic_slice_in_dim(g_conv_full, chip * (3 * DN_WIDTH // N_CHIPS), 3 * DN_WIDTH // N_CHIPS, axis=1)
    g_small = dict(norm_ffn1=red[0:1], norm_mix=red[1:2], norm_ffn2=red[2:3], norm_final=red[3],
                   a_log=red[4:5, DN_HEADS:2 * DN_HEADS], dt_bias=red[5:6, DN_HEADS:2 * DN_HEADS], dn_norm=red[6:7, :DN_HEAD_DIM])

    shard_g = {}
    for n in BIG:
        mine, other = reduced[n]
        full = jnp.where(cc == 0, jnp.concatenate([mine, other], axis=0), jnp.concatenate([other, mine], axis=0))
        if n == "w_in":
            full = full[:IN_COLS // N_CHIPS]
        shard_g[n] = full if n in ROW_SHARDED else full.T

    out_g, out_d, out_m, out_v = {}, {}, {}, {}
    for n in BIG:
        d, nm, nv = _adamw(big_w[n], shard_g[n], big_m[n], big_v[n], "adamw_" + n)
        out_g[n], out_d[n], out_m[n], out_v[n] = shard_g[n][None], d[None], nm[None], nv[None]
    d, nm, nv = _adamw(conv_w[0], g_conv, m_conv_w[0], v_conv_w[0], "adamw_conv_w")
    out_g["conv_w"], out_d["conv_w"], out_m["conv_w"], out_v["conv_w"] = g_conv[None], d[None], nm[None], nv[None]

    small_names = ("norm_ffn1", "norm_mix", "norm_ffn2", "norm_final", "a_log", "dt_bias", "dn_norm")
    small_w = dict(norm_ffn1=norm_ffn1, norm_mix=norm_mix, norm_ffn2=norm_ffn2, norm_final=norm_final, a_log=a_log,
                   dt_bias=dt_bias, dn_norm=dn_norm)
    small_m = dict(norm_ffn1=m_norm_ffn1, norm_mix=m_norm_mix, norm_ffn2=m_norm_ffn2, norm_final=m_norm_final, a_log=m_a_log,
                   dt_bias=m_dt_bias, dn_norm=m_dn_norm)
    small_v = dict(norm_ffn1=v_norm_ffn1, norm_mix=v_norm_mix, norm_ffn2=v_norm_ffn2, norm_final=v_norm_final, a_log=v_a_log,
                   dt_bias=v_dt_bias, dn_norm=v_dn_norm)
    stack = lambda dct: jnp.concatenate([_pad_row(dct[n]) for n in small_names] + [jnp.zeros((1, D_MODEL), F32)], axis=0)
    d, nm, nv = _adamw(stack(small_w), stack(g_small), stack(small_m), stack(small_v), "adamw_small")
    for k, n in enumerate(small_names):
        shape = small_w[n].shape
        size = math.prod(shape)
        out_g[n] = g_small[n].reshape(shape)
        out_d[n], out_m[n], out_v[n] = (t[k, :size].reshape(shape) for t in (d, nm, nv))

    order = ("norm_ffn1", "ffn1_gate", "ffn1_up", "ffn1_down", "norm_mix", "w_in", "conv_w", "a_log", "dt_bias", "dn_norm",
             "w_out", "norm_ffn2", "ffn2_gate", "ffn2_up", "ffn2_down", "norm_final")
    return (loss_out, grad_x[None], *[out_g[n] for n in order], *[out_d[n] for n in order],
            *[out_m[n] for n in order], *[out_v[n] for n in order])
```

```python
import functools
import math

import jax
import jax.numpy as jnp
from jax import lax
from jax.experimental import pallas as pl
from jax.experimental.pallas import tpu as pltpu

F32 = jnp.float32
BF16 = jnp.bfloat16
HI = lax.Precision.HIGH

D_MODEL = 1024
ATTN_HEADS = 8
ATTN_WIDTH = 512
ATTN_BLOCK = 128
DILATIONS = (1, 4, 16)
DN_HEADS = 4
DN_HEAD_DIM = 128
DN_WIDTH = 512
DN_CHUNK = 64
CONV_WIDTH = 4
NORM_EPS = 1e-6
L2_EPS = 1e-6
IN_COLS = 3592
IN_COLS_PADDED = 3712
N_CHIPS = 4

ADAM_LR = 0.001
ADAM_B1 = 0.9
ADAM_B2 = 0.999
ADAM_EPS = 1e-08
ADAM_WD = 0.01
ADAM_STEP = 10

VMEM_LIMIT = 56 * 1024 * 1024
NEG_BIG = -1e30
MESH = pl.DeviceIdType.MESH


def _params(n_grid, vmem=VMEM_LIMIT):
    return pltpu.CompilerParams(dimension_semantics=("arbitrary",) * n_grid, vmem_limit_bytes=vmem)


def _call(body, args, *, name, grid, in_specs, out_specs, out_shape, scratch_shapes=(), comm=None):
    n_in, n_out, n_scr = len(in_specs), len(out_specs), len(scratch_shapes)
    hbm = pl.BlockSpec(memory_space=pl.ANY)
    srcs, dst_shapes, n_sems, start, finish = comm if comm is not None else ((), (), 0, None, None)
    ns, nd = len(srcs), len(dst_shapes)

    def full(*refs):
        ins, c_src = refs[:n_in], refs[n_in:n_in + ns]
        at = n_in + ns
        outs, c_dst = refs[at:at + n_out], refs[at + n_out:at + n_out + nd]
        scr = refs[at + n_out + nd:at + n_out + nd + n_scr]
        if comm is not None:
            ids = [pl.program_id(a) for a in range(len(grid))]
            first = functools.reduce(jnp.logical_and, [i == 0 for i in ids])
            last = functools.reduce(jnp.logical_and, [i == g - 1 for i, g in zip(ids, grid)])

            @pl.when(first)
            def _():
                start(c_src, c_dst, refs[-2], refs[-1])

        body(*ins, *outs, *scr)
        if comm is not None:
            @pl.when(last)
            def _():
                finish(c_src, c_dst, refs[-2], refs[-1])

    sems = [pltpu.SemaphoreType.DMA((n_sems,)), pltpu.SemaphoreType.DMA((n_sems,))] if comm is not None else []
    res = pl.pallas_call(
        full, name=name, grid=grid, in_specs=list(in_specs) + [hbm] * ns, out_specs=list(out_specs) + [hbm] * nd,
        out_shape=list(out_shape) + list(dst_shapes), scratch_shapes=list(scratch_shapes) + sems,
        compiler_params=_params(len(grid)),
    )(*args, *srcs)
    return res[:n_out], res[n_out:]


def _nt(a, b, precision=None):
    return lax.dot_general(a, b, (((1,), (1,)), ((), ())), preferred_element_type=F32, precision=precision)


def _tn(a, b, precision=None):
    return lax.dot_general(a, b, (((0,), (0,)), ((), ())), preferred_element_type=F32, precision=precision)


def _nn(a, b, precision=None):
    return jnp.dot(a, b, preferred_element_type=F32, precision=precision)


def _sigmoid(x):
    return 1.0 / (1.0 + jnp.exp(-x))


def _ffn_fwd(x, gain, wg, wu, wd, name, comm=None):
    S, D = x.shape
    nf, _, tf = wg.shape
    tm = 512

    def body(x_ref, gain_ref, wg_ref, wu_ref, wd_ref, xo_ref, h_ref, g_ref, u_ref, acc_ref, hs_ref):
        j = pl.program_id(1)

        @pl.when(j == 0)
        def _():
            xf = x_ref[...]
            r = lax.rsqrt(jnp.mean(xf * xf, axis=-1, keepdims=True) + NORM_EPS)
            h = (xf * r * gain_ref[...]).astype(BF16)
            hs_ref[...] = h
            h_ref[...] = h
            acc_ref[...] = jnp.zeros_like(acc_ref)

        h = hs_ref[...]
        g = _nn(h, wg_ref[...])
        u = _nn(h, wu_ref[...])
        g_ref[...] = g.astype(BF16)
        u_ref[...] = u.astype(BF16)
        act = g * _sigmoid(g) * u
        acc_ref[...] += _nn(act.astype(BF16), wd_ref[...])

        @pl.when(j == nf - 1)
        def _():
            xo_ref[...] = x_ref[...] + 0.5 * acc_ref[...]

    return _call(
        body, (x, gain, wg, wu, wd), name=name, grid=(S // tm, nf), comm=comm,
        in_specs=[pl.BlockSpec((tm, D), lambda i, j: (i, 0)),
                  pl.BlockSpec((1, D), lambda i, j: (0, 0)),
                  pl.BlockSpec((None, D, tf), lambda i, j: (j, 0, 0)),
                  pl.BlockSpec((None, D, tf), lambda i, j: (j, 0, 0)),
                  pl.BlockSpec((None, tf, D), lambda i, j: (j, 0, 0))],
        out_specs=[pl.BlockSpec((tm, D), lambda i, j: (i, 0)),
                   pl.BlockSpec((tm, D), lambda i, j: (i, 0)),
                   pl.BlockSpec((None, tm, tf), lambda i, j: (j, i, 0)),
                   pl.BlockSpec((None, tm, tf), lambda i, j: (j, i, 0))],
        out_shape=[jax.ShapeDtypeStruct((S, D), F32), jax.ShapeDtypeStruct((S, D), BF16),
                   jax.ShapeDtypeStruct((nf, S, tf), BF16), jax.ShapeDtypeStruct((nf, S, tf), BF16)],
        scratch_shapes=[pltpu.VMEM((tm, D), F32), pltpu.VMEM((tm, D), BF16)])


def _rmsnorm_bwd(dh, xf, gain):
    r = lax.rsqrt(jnp.mean(xf * xf, axis=-1, keepdims=True) + NORM_EPS)
    xhat = xf * r
    dgain = jnp.sum(dh * xhat, axis=0, keepdims=True)
    dxh = dh * gain
    dx = r * (dxh - xhat * jnp.mean(dxh * xhat, axis=-1, keepdims=True))
    return dx, dgain


def _ffn_bwd(dxo, x, gain, g, u, wd, wg, wu, name, comm=None):
    S, D = x.shape
    nf, _, tf = g.shape
    tm = 512

    def body(dxo_ref, x_ref, gain_ref, g_ref, u_ref, wd_ref, wg_ref, wu_ref,
             dx_ref, dgain_ref, dg_ref, du_ref, act_ref, dout_ref, acc_ref, ds_ref):
        i = pl.program_id(0)
        j = pl.program_id(1)

        @pl.when(j == 0)
        def _():
            d = (0.5 * dxo_ref[...]).astype(BF16)
            ds_ref[...] = d
            dout_ref[...] = d
            acc_ref[...] = jnp.zeros_like(acc_ref)

        @pl.when((i == 0) & (j == 0))
        def _():
            dgain_ref[...] = jnp.zeros_like(dgain_ref)

        for half in range(2):
            rows = slice(half * (tm // 2), (half + 1) * (tm // 2))
            dact = _nt(ds_ref[rows, :], wd_ref[...])
            gv = g_ref[rows, :].astype(F32)
            uv = u_ref[rows, :].astype(F32)
            sg = _sigmoid(gv)
            silu = gv * sg
            act_ref[rows, :] = (silu * uv).astype(BF16)
            dgv = (dact * uv * (sg * (1.0 + gv * (1.0 - sg)))).astype(BF16)
            duv = (dact * silu).astype(BF16)
            dg_ref[rows, :] = dgv
            du_ref[rows, :] = duv
            acc_ref[rows, :] += _nt(dgv, wg_ref[...]) + _nt(duv, wu_ref[...])

        @pl.when(j == nf - 1)
        def _():
            dx, dgain = _rmsnorm_bwd(acc_ref[...], x_ref[...], gain_ref[...])
            dx_ref[...] = dxo_ref[...] + dx
            dgain_ref[...] += dgain

    return _call(
        body, (dxo, x, gain, g, u, wd, wg, wu), name=name, grid=(S // tm, nf), comm=comm,
        in_specs=[pl.BlockSpec((tm, D), lambda i, j: (i, 0)),
                  pl.BlockSpec((tm, D), lambda i, j: (i, 0)),
                  pl.BlockSpec((1, D), lambda i, j: (0, 0)),
                  pl.BlockSpec((None, tm, tf), lambda i, j: (j, i, 0)),
                  pl.BlockSpec((None, tm, tf), lambda i, j: (j, i, 0)),
                  pl.BlockSpec((None, tf, D), lambda i, j: (j, 0, 0)),
                  pl.BlockSpec((None, D, tf), lambda i, j: (j, 0, 0)),
                  pl.BlockSpec((None, D, tf), lambda i, j: (j, 0, 0))],
        out_specs=[pl.BlockSpec((tm, D), lambda i, j: (i, 0)),
                   pl.BlockSpec((1, D), lambda i, j: (0, 0)),
                   pl.BlockSpec((None, tm, tf), lambda i, j: (j, i, 0)),
                   pl.BlockSpec((None, tm, tf), lambda i, j: (j, i, 0)),
                   pl.BlockSpec((None, tm, tf), lambda i, j: (j, i, 0)),
                   pl.BlockSpec((tm, D), lambda i, j: (i, 0))],
        out_shape=[jax.ShapeDtypeStruct((S, D), F32), jax.ShapeDtypeStruct((1, D), F32),
                   jax.ShapeDtypeStruct((nf, S, tf), BF16), jax.ShapeDtypeStruct((nf, S, tf), BF16),
                   jax.ShapeDtypeStruct((nf, S, tf), BF16), jax.ShapeDtypeStruct((S, D), BF16)],
        scratch_shapes=[pltpu.VMEM((tm, D), F32), pltpu.VMEM((tm, D), BF16)])


def _matmul_tn(a, b, tm, tk, name):
    K, M = a.shape
    N = b.shape[1]

    def body(a_ref, b_ref, o_ref):
        @pl.when(pl.program_id(1) == 0)
        def _():
            o_ref[...] = jnp.zeros_like(o_ref)

        o_ref[...] += _tn(a_ref[...], b_ref[...])

    return pl.pallas_call(
        body, name=name, grid=(M // tm, K // tk),
        in_specs=[pl.BlockSpec((tk, tm), lambda i, k: (k, i)),
                  pl.BlockSpec((tk, N), lambda i, k: (k, 0))],
        out_specs=pl.BlockSpec((tm, N), lambda i, k: (i, 0)),
        out_shape=jax.ShapeDtypeStruct((M, N), F32),
        compiler_params=_params(2),
    )(a, b)


def _dw_chunks(a, b, tk, name, comm=None):
    nf, S, tf = a.shape
    N = b.shape[1]

    def body(a_ref, b_ref, o_ref):
        @pl.when(pl.program_id(1) == 0)
        def _():
            o_ref[...] = jnp.zeros_like(o_ref)

        o_ref[...] += _tn(a_ref[...], b_ref[...])

    (out,), landed = _call(
        body, (a, b), name=name, grid=(nf, S // tk), comm=comm,
        in_specs=[pl.BlockSpec((None, tk, tf), lambda j, k: (j, k, 0)),
                  pl.BlockSpec((tk, N), lambda j, k: (k, 0))],
        out_specs=[pl.BlockSpec((None, tf, N), lambda j, k: (j, 0, 0))],
        out_shape=[jax.ShapeDtypeStruct((nf, tf, N), F32)])
    return out, landed


VIEW_TILE = 512


def _view_spec(d, tile=VIEW_TILE):
    return pl.BlockSpec((tile // d, d * ATTN_WIDTH), lambda i: (i, 0))


def _view_shape(S, d, dtype):
    return jax.ShapeDtypeStruct((S // d, d * ATTN_WIDTH), dtype)


def _tile_to_views(val, planes, out_refs):
    for g in range(4):
        planes[g] = val[:, g * 128:(g + 1) * 128]
    for d, ref in zip(DILATIONS, out_refs):
        if d == 1:
            ref[...] = val.astype(ref.dtype)
            continue
        for r in range(d):
            for g in range(4):
                ref[:, r * ATTN_WIDTH + g * 128:r * ATTN_WIDTH + (g + 1) * 128] = (
                    planes[g, pl.ds(r, planes.shape[1] // d, stride=d), :].astype(ref.dtype))


def _view_to_tile(ref, d, planes):
    if d == 1:
        return ref[...]
    for r in range(d):
        for g in range(4):
            planes[g, pl.ds(r, planes.shape[1] // d, stride=d), :] = ref[:, r * ATTN_WIDTH + g * 128:r * ATTN_WIDTH + (g + 1) * 128]
    return jnp.concatenate([planes[g] for g in range(4)], axis=1)


def _inproj_fwd(x, gain, w_in_p):
    S, D = x.shape
    tm = VIEW_TILE
    W = ATTN_WIDTH

    def body(x_ref, gain_ref, w_ref, h_ref, q1, q4, q16, k1, k4, k16, v1, v4, v16, dq_ref, dk_ref, dv_ref, gate_ref, bd_ref,
             planes):
        xf = x_ref[...]
        r = lax.rsqrt(jnp.mean(xf * xf, axis=-1, keepdims=True) + NORM_EPS)
        h = (xf * r * gain_ref[...]).astype(BF16)
        h_ref[...] = h
        _tile_to_views(_nn(h, w_ref[:, 0:W]) * 0.125, planes, (q1, q4, q16))
        _tile_to_views(_nn(h, w_ref[:, W:2 * W]), planes, (k1, k4, k16))
        _tile_to_views(_nn(h, w_ref[:, 2 * W:3 * W]), planes, (v1, v4, v16))
        dq_ref[...] = _nn(h, w_ref[:, 3 * W:4 * W])
        dk_ref[...] = _nn(h, w_ref[:, 4 * W:5 * W])
        dv_ref[...] = _nn(h, w_ref[:, 5 * W:6 * W])
        gate_ref[...] = _nn(h, w_ref[:, 6 * W:7 * W])
        bd_ref[...] = _nn(h, w_ref[:, 7 * W:7 * W + 128])

    tok = lambda w: pl.BlockSpec((tm, w), lambda i: (i, 0))
    return pl.pallas_call(
        body, name="inproj_fwd", grid=(S // tm,),
        in_specs=[tok(D), pl.BlockSpec((1, D), lambda i: (0, 0)),
                  pl.BlockSpec((D, IN_COLS_PADDED), lambda i: (0, 0))],
        out_specs=[tok(D)] + [_view_spec(d) for d in DILATIONS] * 3 + [tok(W)] * 4 + [tok(128)],
        out_shape=[jax.ShapeDtypeStruct((S, D), BF16)] + [_view_shape(S, d, BF16) for d in DILATIONS] * 3
                  + [jax.ShapeDtypeStruct((S, W), F32)] * 4 + [jax.ShapeDtypeStruct((S, 128), F32)],
        scratch_shapes=[pltpu.VMEM((4, tm, 128), F32)],
        compiler_params=_params(1),
    )(x, gain, w_in_p)


def _inproj_bwd(dxo, x, gain, attn_grads, dsecs, dbd, w_in_p):
    S, D = x.shape
    tm = VIEW_TILE // 2
    W = ATTN_WIDTH

    def body(dxo_ref, x_ref, gain_ref, *rest):
        views, (s3, s4, s5, s6, dbd_ref, w_ref, dx_ref, dgain_ref, dproj_ref, planes) = rest[:9], rest[9:]

        @pl.when(pl.program_id(0) == 0)
        def _():
            dgain_ref[...] = jnp.zeros_like(dgain_ref)

        secs = []
        for k in range(3):
            parts = [_view_to_tile(views[3 * k + p], d, planes) for p, d in enumerate(DILATIONS)]
            secs.append(parts[0] + parts[1] + parts[2])
        secs += [s3[...], s4[...], s5[...], s6[...]]
        dh = jnp.zeros((tm, D), F32)
        for k, s in enumerate(secs):
            d = s.astype(BF16)
            dproj_ref[:, k * W:(k + 1) * W] = d
            dh += _nt(d, w_ref[:, k * W:(k + 1) * W])
        d = dbd_ref[...].astype(BF16)
        dproj_ref[:, 7 * W:7 * W + 128] = d
        dh += _nt(d, w_ref[:, 7 * W:7 * W + 128])
        dx, dgain = _rmsnorm_bwd(dh, x_ref[...], gain_ref[...])
        dx_ref[...] = dxo_ref[...] + dx
        dgain_ref[...] += dgain

    tok = lambda w: pl.BlockSpec((tm, w), lambda i: (i, 0))
    return pl.pallas_call(
        body, name="inproj_bwd", grid=(S // tm,),
        in_specs=[tok(D), tok(D), pl.BlockSpec((1, D), lambda i: (0, 0))] + [_view_spec(d, tm) for d in DILATIONS] * 3
                 + [tok(W)] * 4 + [tok(128)] + [pl.BlockSpec((D, IN_COLS_PADDED), lambda i: (0, 0))],
        out_specs=[tok(D), pl.BlockSpec((1, D), lambda i: (0, 0)), tok(IN_COLS_PADDED)],
        out_shape=[jax.ShapeDtypeStruct((S, D), F32), jax.ShapeDtypeStruct((1, D), F32),
                   jax.ShapeDtypeStruct((S, IN_COLS_PADDED), BF16)],
        scratch_shapes=[pltpu.VMEM((4, tm, 128), F32)],
        compiler_params=_params(1),
    )(dxo, x, gain, *[g for grads in attn_grads for g in grads], *dsecs, dbd, w_in_p)


def _slope(h):
    return 2.0 ** (-8.0 * (h + 1) / ATTN_HEADS)


def _head_bias(steps, d, heads=tuple(range(ATTN_HEADS))):
    stepsf = steps.astype(F32)
    return jnp.stack([stepsf * (-_slope(h) * d) for h in heads])


def _hnt(a, b):
    return lax.dot_general(a, b, (((2,), (2,)), ((0,), (0,))), preferred_element_type=F32)


def _hnn(a, b):
    return lax.dot_general(a, b, (((2,), (1,)), ((0,), (0,))), preferred_element_type=F32)


def _blocks_per_step(nb):
    return next(n for n in (4, 2, 1) if nb % n == 0)


def _query_step_specs(qb):
    B = ATTN_BLOCK
    cur = pl.BlockSpec((qb * B, ATTN_WIDTH), lambda r, n: (n, r))
    prev = pl.BlockSpec((B, ATTN_WIDTH), lambda r, n: (jnp.maximum(qb * n - 1, 0), r))
    return cur, prev


def _prev_block(prev_ref, cur_ref, sub, sl):
    B = ATTN_BLOCK
    return prev_ref[:, sl] if sub == 0 else cur_ref[(sub - 1) * B:sub * B, sl]


def _head_cols(tile, lo, big):
    return [_head_col(tile, lo, big), _head_col(tile, jnp.logical_not(lo), big)]


def _attn_fwd(q, k, v, d, name):
    L = q.shape[0]
    nb = L // ATTN_BLOCK
    B = ATTN_BLOCK
    QB = _blocks_per_step(nb)

    def body(q_ref, kp_ref, kc_ref, vp_ref, vc_ref, o_ref, lse_ref):
        n = pl.program_id(1)
        qi = lax.broadcasted_iota(jnp.int32, (B, 2 * B), 0)
        kj = lax.broadcasted_iota(jnp.int32, (B, 2 * B), 1)
        steps = qi + B - kj
        band = (steps >= 0) & (steps <= B)
        lo = lax.broadcasted_iota(jnp.int32, (B, 128), 1) < 64
        bias = _head_bias(steps, d)
        for sub in range(QB):
            rows = slice(sub * B, (sub + 1) * B)
            valid = band & ((kj >= B) | (n > 0)) if sub == 0 else band
            qs, ks, vs = [], [], []
            for G in range(4):
                sl = slice(G * 128, (G + 1) * 128)
                qg = q_ref[rows, sl]
                kg = jnp.concatenate([_prev_block(kp_ref, kc_ref, sub, sl), kc_ref[rows, sl]], axis=0)
                vg = jnp.concatenate([_prev_block(vp_ref, vc_ref, sub, sl), vc_ref[rows, sl]], axis=0)
                qs += [jnp.where(lo, qg, jnp.zeros_like(qg)), jnp.where(lo, jnp.zeros_like(qg), qg)]
                ks += [kg, kg]
                vs += [vg, vg]
            s = jnp.where(valid, _hnt(jnp.stack(qs), jnp.stack(ks)) + bias, NEG_BIG)
            m = jnp.max(s, axis=-1, keepdims=True)
            p = jnp.exp(s - m)
            l = jnp.sum(p, axis=-1, keepdims=True)
            o = _hnn(p.astype(BF16), jnp.stack(vs)) / l
            lse = m + jnp.log(l)
            for G in range(4):
                sl = slice(G * 128, (G + 1) * 128)
                o_ref[rows, sl] = jnp.where(lo, o[2 * G], o[2 * G + 1])
                lse_ref[rows, sl] = jnp.where(lo, lse[2 * G], lse[2 * G + 1])

    cur, prev = _query_step_specs(QB)
    return pl.pallas_call(
        body, name=name, grid=(d, nb // QB),
        in_specs=[cur, prev, cur, prev, cur],
        out_specs=[cur, cur],
        out_shape=[jax.ShapeDtypeStruct((L, d * ATTN_WIDTH), F32)] * 2,
        compiler_params=_params(2),
    )(q, k, k, v, v)


def _attn_merge(parts):
    S = parts[0][0].shape[0]
    tm = VIEW_TILE

    def body(o1, s1, o2, s2, o3, s3, o_ref, lse1, lse4, lse16, planes):
        outs, lses = [], []
        for d, (o, s) in zip(DILATIONS, ((o1, s1), (o2, s2), (o3, s3))):
            outs.append(_view_to_tile(o, d, planes))
            lses.append(_view_to_tile(s, d, planes))
        mx = jnp.maximum(jnp.maximum(lses[0], lses[1]), lses[2])
        es = [jnp.exp(s - mx) for s in lses]
        den = es[0] + es[1] + es[2]
        o_ref[...] = (es[0] * outs[0] + es[1] * outs[1] + es[2] * outs[2]) / den
        _tile_to_views(mx + jnp.log(den), planes, (lse1, lse4, lse16))

    views = [_view_spec(d) for d in DILATIONS]
    flat = [t for p in parts for t in p]
    return pl.pallas_call(
        body, name="attn_merge", grid=(S // tm,),
        in_specs=[views[p] for p in range(3) for _ in range(2)],
        out_specs=[views[0]] + views,
        out_shape=[jax.ShapeDtypeStruct((S, ATTN_WIDTH), F32)] + [_view_shape(S, d, F32) for d in DILATIONS],
        scratch_shapes=[pltpu.VMEM((4, tm, 128), F32)],
        compiler_params=_params(1),
    )(*flat)


def _head_col(t, msk, big):
    if big:
        return jnp.max(jnp.where(msk, t, NEG_BIG), axis=-1, keepdims=True)
    return jnp.sum(jnp.where(msk, t, 0.0), axis=-1, keepdims=True) * (1.0 / 64.0)


def _attn_bwd_q(q, k, v, do, lse, dd, d, name):
    L = q.shape[0]
    nb = L // ATTN_BLOCK
    B = ATTN_BLOCK
    QB = _blocks_per_step(nb)

    def body(q_ref, kp_ref, kc_ref, vp_ref, vc_ref, do_ref, lse_ref, dd_ref, dq_ref):
        n = pl.program_id(1)
        qi = lax.broadcasted_iota(jnp.int32, (B, 2 * B), 0)
        kj = lax.broadcasted_iota(jnp.int32, (B, 2 * B), 1)
        steps = qi + B - kj
        band = (steps >= 0) & (steps <= B)
        lo = lax.broadcasted_iota(jnp.int32, (B, 128), 1) < 64
        bias = _head_bias(steps, d)
        for sub in range(QB):
            rows = slice(sub * B, (sub + 1) * B)
            valid = band & ((kj >= B) | (n > 0)) if sub == 0 else band
            qs, ks, vs, dos, lses, dcols = [], [], [], [], [], []
            for G in range(4):
                sl = slice(G * 128, (G + 1) * 128)
                qg = q_ref[rows, sl]
                kg = jnp.concatenate([_prev_block(kp_ref, kc_ref, sub, sl), kc_ref[rows, sl]], axis=0)
                vg = jnp.concatenate([_prev_block(vp_ref, vc_ref, sub, sl), vc_ref[rows, sl]], axis=0)
                dog = do_ref[rows, sl]
                qs += [jnp.where(lo, qg, jnp.zeros_like(qg)), jnp.where(lo, jnp.zeros_like(qg), qg)]
                dos += [jnp.where(lo, dog, 0.0).astype(BF16), jnp.where(lo, 0.0, dog).astype(BF16)]
                ks += [kg, kg]
                vs += [vg, vg]
                lses += _head_cols(lse_ref[rows, sl], lo, True)
                dcols += _head_cols(dd_ref[rows, sl], lo, False)
            kb = jnp.stack(ks)
            s = _hnt(jnp.stack(qs), kb) + bias
            p = jnp.where(valid, jnp.exp(jnp.where(valid, s, NEG_BIG) - jnp.stack(lses)), 0.0)
            dp = _hnt(jnp.stack(dos), jnp.stack(vs))
            ds = p * (dp - jnp.stack(dcols))
            dq = _hnn(ds.astype(BF16), kb) * 0.125
            for G in range(4):
                dq_ref[rows, G * 128:(G + 1) * 128] = jnp.where(lo, dq[2 * G], dq[2 * G + 1])

    cur, prev = _query_step_specs(QB)
    return pl.pallas_call(
        body, name=name, grid=(d, nb // QB), in_specs=[cur, prev, cur, prev, cur, cur, cur, cur], out_specs=cur,
        out_shape=jax.ShapeDtypeStruct((L, d * ATTN_WIDTH), F32), compiler_params=_params(2),
    )(q, k, k, v, v, do, lse, dd)


def _attn_bwd_kv(q, k, v, do, lse, dd, d, name):
    L = q.shape[0]
    nb = L // ATTN_BLOCK
    B = ATTN_BLOCK
    KB = _blocks_per_step(nb)
    n_steps = nb // KB

    def body(k_ref, v_ref, qc_ref, qn_ref, doc_ref, don_ref, lsec_ref, lsen_ref, ddc_ref, ddn_ref, dk_ref, dv_ref):
        j = pl.program_id(1)
        qrow = lax.broadcasted_iota(jnp.int32, (2 * B, B), 0)
        kk = lax.broadcasted_iota(jnp.int32, (2 * B, B), 1)
        steps = qrow - kk
        band = (steps >= 0) & (steps <= B)
        lo2 = lax.broadcasted_iota(jnp.int32, (2 * B, 128), 1) < 64
        lo = lax.broadcasted_iota(jnp.int32, (B, 128), 1) < 64
        stepsf = steps.astype(F32)
        for sub in range(KB):
            rows = slice(sub * B, (sub + 1) * B)
            last = sub == KB - 1
            valid = band & ((qrow < B) | (j < n_steps - 1)) if last else band
            after = lambda cur_ref, nxt_ref, sl: nxt_ref[:, sl] if last else cur_ref[(sub + 1) * B:(sub + 2) * B, sl]
            for G in range(4):
                sl = slice(G * 128, (G + 1) * 128)
                kg = k_ref[rows, sl]
                vg = v_ref[rows, sl]
                qq = jnp.concatenate([qc_ref[rows, sl], after(qc_ref, qn_ref, sl)], axis=0)
                doo = jnp.concatenate([doc_ref[rows, sl], after(doc_ref, don_ref, sl)], axis=0)
                lse2 = jnp.concatenate([lsec_ref[rows, sl], after(lsec_ref, lsen_ref, sl)], axis=0)
                dd2 = jnp.concatenate([ddc_ref[rows, sl], after(ddc_ref, ddn_ref, sl)], axis=0)
                doo_b = doo.astype(BF16)
                dks, dvs = [], []
                for half in (0, 1):
                    msk = lo2 if half == 0 else jnp.logical_not(lo2)
                    qm = jnp.where(msk, qq, jnp.zeros_like(qq))
                    s = _nt(qm, kg) - (_slope(2 * G + half) * d) * stepsf
                    lse_c = _head_col(lse2, msk, True)
                    p = jnp.where(valid, jnp.exp(jnp.where(valid, s, NEG_BIG) - lse_c), 0.0)
                    dvs.append(_tn(p.astype(BF16), doo_b))
                    dom = jnp.where(msk, doo, 0.0).astype(BF16)
                    dp = _nt(dom, vg)
                    dcol = _head_col(dd2, msk, False)
                    ds = p * (dp - dcol)
                    dks.append(_tn(ds.astype(BF16), qq))
                dk_ref[rows, sl] = jnp.where(lo, dks[0], dks[1])
                dv_ref[rows, sl] = jnp.where(lo, dvs[0], dvs[1])

    cur = pl.BlockSpec((KB * B, ATTN_WIDTH), lambda r, j: (j, r))
    nxt = pl.BlockSpec((B, ATTN_WIDTH), lambda r, j: (jnp.minimum(KB * (j + 1), nb - 1), r))
    return pl.pallas_call(
        body, name=name, grid=(d, n_steps), in_specs=[cur, cur, cur, nxt, cur, nxt, cur, nxt, cur, nxt],
        out_specs=[cur, cur],
        out_shape=[jax.ShapeDtypeStruct((L, d * ATTN_WIDTH), F32)] * 2, compiler_params=_params(2),
    )(k, v, q, q, do, do, lse, lse, dd, dd)


CONV_T = 512
HALO = 8


def _per_head(head, refs):
    for h in range(DN_HEADS):
        lanes = pl.ds(h * DN_HEAD_DIM, DN_HEAD_DIM)
        head(*[r.at[:, lanes] for r in refs[:-1]], refs[-1])


def _conv_taps(pad_ref, w, T):
    acc = pad_ref[pl.ds(HALO - 3, T), :] * w[0:1, :]
    for j in range(1, CONV_WIDTH):
        acc = acc + pad_ref[pl.ds(HALO - 3 + j, T), :] * w[j:j + 1, :]
    return acc


def _conv_fwd(xq, xk, xv, conv_w):
    S = xq.shape[0]
    T = CONV_T

    def body(*refs):
        _per_head(head, refs)

    def head(xq_ref, xqh_ref, xk_ref, xkh_ref, xv_ref, xvh_ref, wq_ref, wk_ref, wv_ref,
             qn_ref, kn_ref, v_ref, pad_ref):
        i = pl.program_id(0)

        def act(x_ref, xh_ref, w_ref):
            pad_ref[pl.ds(0, HALO), :] = jnp.where(i > 0, xh_ref[...], 0.0)
            pad_ref[pl.ds(HALO, T), :] = x_ref[...]
            c = _conv_taps(pad_ref, w_ref[...], T)
            return c * _sigmoid(c)

        def l2n(t):
            return t * lax.rsqrt(jnp.sum(t * t, axis=-1, keepdims=True) + L2_EPS)

        qn_ref[...] = l2n(act(xq_ref, xqh_ref, wq_ref))
        kn_ref[...] = l2n(act(xk_ref, xkh_ref, wk_ref))
        v_ref[...] = act(xv_ref, xvh_ref, wv_ref)

    tile = pl.BlockSpec((T, DN_WIDTH), lambda i: (i, 0))
    halo = pl.BlockSpec((HALO, DN_WIDTH), lambda i: (jnp.maximum(i * (T // HALO) - 1, 0), 0))
    wspec = lambda sec: pl.BlockSpec((CONV_WIDTH, DN_WIDTH), lambda i, sec=sec: (0, sec))
    return pl.pallas_call(
        body, name="dn_conv_fwd", grid=(S // T,),
        in_specs=[tile, halo, tile, halo, tile, halo, wspec(0), wspec(1), wspec(2)],
        out_specs=[tile, tile, tile],
        out_shape=[jax.ShapeDtypeStruct((S, DN_WIDTH), F32)] * 3,
        scratch_shapes=[pltpu.VMEM((T + HALO, 128), F32)],
        compiler_params=_params(1),
    )(xq, xq, xk, xk, xv, xv, conv_w, conv_w, conv_w)


def _conv_bwd_pre(xq, xk, xv, conv_w, dqn, dkn, dv):
    S = xq.shape[0]
    T = CONV_T

    def body(*refs):
        _per_head(head, refs)

    def head(xq_ref, xqh_ref, xk_ref, xkh_ref, xv_ref, xvh_ref, wq_ref, wk_ref, wv_ref,
             dqn_ref, dkn_ref, dv_ref, dcq_ref, dck_ref, dcv_ref, dwq_ref, dwk_ref, dwv_ref, pad_ref):
        i = pl.program_id(0)

        def one(x_ref, xh_ref, w_ref, dy_ref, dc_ref, dw_ref, normed):
            pad_ref[pl.ds(0, HALO), :] = jnp.where(i > 0, xh_ref[...], 0.0)
            pad_ref[pl.ds(HALO, T), :] = x_ref[...]
            c = _conv_taps(pad_ref, w_ref[...], T)
            sg = _sigmoid(c)
            a = c * sg
            dy = dy_ref[...]
            if normed:
                r = lax.rsqrt(jnp.sum(a * a, axis=-1, keepdims=True) + L2_EPS)
                y = a * r
                da = r * (dy - y * jnp.sum(dy * y, axis=-1, keepdims=True))
            else:
                da = dy
            dc = da * (sg * (1.0 + c * (1.0 - sg)))
            dc_ref[...] = dc

            @pl.when(i == 0)
            def _():
                dw_ref[...] = jnp.zeros_like(dw_ref)

            rows = [jnp.sum(dc * pad_ref[pl.ds(HALO - 3 + j, T), :], axis=0, keepdims=True) for j in range(CONV_WIDTH)]
            dw_ref[...] += jnp.concatenate(rows + [jnp.zeros((8 - CONV_WIDTH, 128), F32)], axis=0)

        one(xq_ref, xqh_ref, wq_ref, dqn_ref, dcq_ref, dwq_ref, True)
        one(xk_ref, xkh_ref, wk_ref, dkn_ref, dck_ref, dwk_ref, True)
        one(xv_ref, xvh_ref, wv_ref, dv_ref, dcv_ref, dwv_ref, False)

    tile = pl.BlockSpec((T, DN_WIDTH), lambda i: (i, 0))
    halo = pl.BlockSpec((HALO, DN_WIDTH), lambda i: (jnp.maximum(i * (T // HALO) - 1, 0), 0))
    wspec = lambda sec: pl.BlockSpec((CONV_WIDTH, DN_WIDTH), lambda i, sec=sec: (0, sec))
    dwspec = pl.BlockSpec((8, DN_WIDTH), lambda i: (0, 0))
    return pl.pallas_call(
        body, name="dn_conv_bwd_pre", grid=(S // T,),
        in_specs=[tile, halo, tile, halo, tile, halo, wspec(0), wspec(1), wspec(2), tile, tile, tile],
        out_specs=[tile, tile, tile, dwspec, dwspec, dwspec],
        out_shape=[jax.ShapeDtypeStruct((S, DN_WIDTH), F32)] * 3 + [jax.ShapeDtypeStruct((8, DN_WIDTH), F32)] * 3,
        scratch_shapes=[pltpu.VMEM((T + HALO, 128), F32)],
        compiler_params=_params(1),
    )(xq, xq, xk, xk, xv, xv, conv_w, conv_w, conv_w, dqn, dkn, dv)


def _conv_bwd_x(dcq, dck, dcv, conv_w):
    S = dcq.shape[0]
    T = CONV_T
    nt = S // T

    def body(*refs):
        _per_head(head, refs)

    def head(dq_ref, dqh_ref, dk_ref, dkh_ref, dv_ref, dvh_ref, wq_ref, wk_ref, wv_ref,
             oq_ref, ok_ref, ov_ref, pad_ref):
        i = pl.program_id(0)

        def one(d_ref, dh_ref, w_ref, o_ref):
            pad_ref[pl.ds(0, T), :] = d_ref[...]
            pad_ref[pl.ds(T, HALO), :] = jnp.where(i < nt - 1, dh_ref[...], 0.0)
            w = w_ref[...]
            acc = pad_ref[pl.ds(3, T), :] * w[0:1, :]
            for j in range(1, CONV_WIDTH):
                acc = acc + pad_ref[pl.ds(3 - j, T), :] * w[j:j + 1, :]
            o_ref[...] = acc

        one(dq_ref, dqh_ref, wq_ref, oq_ref)
        one(dk_ref, dkh_ref, wk_ref, ok_ref)
        one(dv_ref, dvh_ref, wv_ref, ov_ref)

    tile = pl.BlockSpec((T, DN_WIDTH), lambda i: (i, 0))
    halo = pl.BlockSpec((HALO, DN_WIDTH), lambda i: (jnp.minimum((i + 1) * (T // HALO), S // HALO - 1), 0))
    wspec = lambda sec: pl.BlockSpec((CONV_WIDTH, DN_WIDTH), lambda i, sec=sec: (0, sec))
    return pl.pallas_call(
        body, name="dn_conv_bwd_x", grid=(nt,),
        in_specs=[tile, halo, tile, halo, tile, halo, wspec(0), wspec(1), wspec(2)],
        out_specs=[tile, tile, tile],
        out_shape=[jax.ShapeDtypeStruct((S, DN_WIDTH), F32)] * 3,
        scratch_shapes=[pltpu.VMEM((T + HALO, 128), F32)],
        compiler_params=_params(1),
    )(dcq, dcq, dck, dck, dcv, dcv, conv_w, conv_w, conv_w)


PREP_CHUNKS = 4
SCAN_CHUNKS = 8


def _bnn(a, b):
    return lax.dot_general(a, b, (((2,), (1,)), ((0,), (0,))), preferred_element_type=F32, precision=HI)


def _bnt(a, b):
    return lax.dot_general(a, b, (((2,), (2,)), ((0,), (0,))), preferred_element_type=F32, precision=HI)


def _btn(a, b):
    return lax.dot_general(a, b, (((1,), (1,)), ((0,), (0,))), preferred_element_type=F32, precision=HI)


def _tri_inverse_b(a, blk, eye):
    dg = jnp.where(blk, a, 0.0)
    lo = a - dg
    d2 = _bnn(dg, dg)
    d4 = _bnn(d2, d2)
    d8 = _bnn(d4, d4)
    td = _bnn(_bnn(_bnn(eye - dg, eye + d2), eye + d4), eye + d8)
    b = _bnn(td, lo)
    b2 = _bnn(b, b)
    return _bnn(_bnn(eye - b, eye + b2), td)


def _dn_common_b(bds, avec, dvec, q_raw, k, v, t=None):
    C = DN_CHUNK
    lane = lax.broadcasted_iota(jnp.int32, (C, 128), 1)
    row = lax.broadcasted_iota(jnp.int32, (1, C, C), 1)
    col = lax.broadcasted_iota(jnp.int32, (1, C, C), 2)
    incl = row >= col
    strict = row > col
    eye = (row == col).astype(F32)
    blk = (row // 16) == (col // 16)
    pick = lambda tile, ln: jnp.sum(jnp.where(lane == ln, tile, 0.0), axis=-1, keepdims=True)
    betas, graws, zcs = [], [], []
    for bd in bds:
        z = bd + dvec
        g_all = -jnp.exp(avec) * (jnp.maximum(z, 0.0) + jnp.log(1.0 + jnp.exp(-jnp.abs(z))))
        beta_all = _sigmoid(bd)
        for h in range(DN_HEADS):
            betas.append(pick(beta_all, h))
            graws.append(pick(g_all, DN_HEADS + h))
            zcs.append(pick(z, DN_HEADS + h))
    beta, graw, zc = jnp.stack(betas), jnp.stack(graws), jnp.stack(zcs)
    to_row = lambda c: jnp.sum(eye * c, axis=1, keepdims=True)
    gc = jnp.sum(jnp.where(incl, to_row(graw), 0.0), axis=-1, keepdims=True)
    decay = jnp.exp(jnp.where(incl, gc - to_row(gc), NEG_BIG))
    q = q_raw * (DN_HEAD_DIM ** -0.5)
    kb = k * beta
    kk = _bnt(kb, k)
    if t is None:
        t = _tri_inverse_b(jnp.where(strict, kk * decay, 0.0), blk, eye)
    eg = jnp.exp(gc)
    rhs_w = kb * eg
    u = _bnn(t, v * beta)
    w = _bnn(t, rhs_w)
    qk = _bnt(q, k)
    aq = jnp.where(incl, qk * decay, 0.0)
    last = lax.broadcasted_iota(jnp.int32, (1, C, 1), 1) == C - 1
    g_last = jnp.sum(jnp.where(last, gc, 0.0), axis=1, keepdims=True)
    ekd = jnp.exp(g_last - gc)
    return dict(beta=beta, graw=graw, zc=zc, gc=gc, decay=decay, q=q, kb=kb, kk=kk, t=t, eg=eg, rhs_w=rhs_w,
                u=u, w=w, qk=qk, aq=aq, g_last=g_last, ekd=ekd, kd=k * ekd, qg=q * eg,
                incl=incl, strict=strict, eye=eye, lane=lane, row=row, col=col, last=last)


def _stack_heads(ref, rows):
    return jnp.stack([ref[rows, h * DN_HEAD_DIM:(h + 1) * DN_HEAD_DIM] for h in range(DN_HEADS)])


def _stack_units(ref, nc):
    C = DN_CHUNK
    return jnp.concatenate([_stack_heads(ref, slice(ci * C, (ci + 1) * C)) for ci in range(nc)], axis=0)


def _store_units(ref, val, nc):
    C = DN_CHUNK
    for ci in range(nc):
        for h in range(DN_HEADS):
            ref[ci * C:(ci + 1) * C, h * DN_HEAD_DIM:(h + 1) * DN_HEAD_DIM] = val[ci * DN_HEADS + h]


def _dn_prep(qn, kn, v, bd, avec, dvec):
    S = qn.shape[0]
    C = DN_CHUNK
    N = S // C
    nc = PREP_CHUNKS

    def body(q_ref, k_ref, v_ref, bd_ref, a_ref, d_ref, u_ref, w_ref, qg_ref, kd_ref, aq_ref, t_ref, egl_ref):
        bds = [bd_ref[ci * C:(ci + 1) * C, :] for ci in range(nc)]
        c = _dn_common_b(bds, a_ref[...], d_ref[...], _stack_units(q_ref, nc), _stack_units(k_ref, nc), _stack_units(v_ref, nc))
        _store_units(u_ref, c["u"], nc)
        _store_units(w_ref, c["w"], nc)
        _store_units(qg_ref, c["qg"], nc)
        _store_units(kd_ref, c["kd"], nc)
        egl = jnp.broadcast_to(jnp.exp(c["g_last"]), (nc * DN_HEADS, 1, 128))
        for ci in range(nc):
            for h in range(DN_HEADS):
                aq_ref[h, ci * C:(ci + 1) * C, :] = c["aq"][ci * DN_HEADS + h]
                t_ref[h, ci * C:(ci + 1) * C, :] = c["t"][ci * DN_HEADS + h]
            egl_ref[ci * 8:(ci + 1) * 8, :] = jnp.concatenate(
                [egl[ci * DN_HEADS + h] for h in range(DN_HEADS)] + [jnp.zeros((8 - DN_HEADS, 128), F32)], axis=0)

    tok = lambda w: pl.BlockSpec((nc * C, w), lambda n: (n, 0))
    sq = pl.BlockSpec((DN_HEADS, nc * C, C), lambda n: (0, n, 0))
    vec = pl.BlockSpec((1, 128), lambda n: (0, 0))
    return pl.pallas_call(
        body, name="dn_prep", grid=(N // nc,),
        in_specs=[tok(DN_WIDTH)] * 3 + [tok(128), vec, vec],
        out_specs=[tok(DN_WIDTH)] * 4 + [sq, sq, pl.BlockSpec((nc * 8, 128), lambda n: (n, 0))],
        out_shape=[jax.ShapeDtypeStruct((S, DN_WIDTH), F32)] * 4 + [jax.ShapeDtypeStruct((DN_HEADS, S, C), F32)] * 2
                  + [jax.ShapeDtypeStruct((N * 8, 128), F32)],
        compiler_params=_params(1),
    )(qn, kn, v, bd, avec, dvec)


def _dn_scan_fwd(u, w, qg, kd, aq, egl, gate, dn_gain):
    S = u.shape[0]
    C = DN_CHUNK
    N = S // C
    HD = DN_HEAD_DIM
    nc = SCAN_CHUNKS

    def body(u_ref, w_ref, qg_ref, kd_ref, aq_ref, egl_ref, gate_ref, gain_ref, dn_ref, o_ref, vn_ref, st_ref, state_ref):
        @pl.when(pl.program_id(0) == 0)
        def _():
            state_ref[...] = jnp.zeros_like(state_ref)

        gain = gain_ref[...]
        for ci in range(nc):
            rows = slice(ci * C, (ci + 1) * C)
            st = state_ref[...]
            for h in range(DN_HEADS):
                st_ref[ci * DN_WIDTH + h * HD:ci * DN_WIDTH + (h + 1) * HD, :] = st[h]
            v_new = _stack_heads(u_ref, rows) - _bnn(_stack_heads(w_ref, rows), st)
            o = _bnn(_stack_heads(qg_ref, rows), st) + _bnn(aq_ref[:, rows, :], v_new)
            egl = jnp.stack([egl_ref[ci * 8 + h:ci * 8 + h + 1, :] for h in range(DN_HEADS)])
            state_ref[...] = st * egl + _btn(_stack_heads(kd_ref, rows), v_new)
            r = lax.rsqrt(jnp.mean(o * o, axis=-1, keepdims=True) + NORM_EPS)
            gt = _stack_heads(gate_ref, rows)
            dn = o * r * gain * (gt * _sigmoid(gt))
            for h in range(DN_HEADS):
                sl = slice(h * HD, (h + 1) * HD)
                vn_ref[rows, sl] = v_new[h]
                o_ref[rows, sl] = o[h]
                dn_ref[rows, sl] = dn[h]

    tok = lambda wd: pl.BlockSpec((nc * C, wd), lambda n: (n, 0))
    sq = pl.BlockSpec((DN_HEADS, nc * C, C), lambda n: (0, n, 0))
    vec = pl.BlockSpec((1, 128), lambda n: (0, 0))
    return pl.pallas_call(
        body, name="dn_scan_fwd", grid=(N // nc,),
        in_specs=[tok(DN_WIDTH)] * 4 + [sq, pl.BlockSpec((nc * 8, 128), lambda n: (n, 0)), tok(DN_WIDTH), vec],
        out_specs=[tok(DN_WIDTH)] * 3 + [pl.BlockSpec((nc * DN_WIDTH, HD), lambda n: (n, 0))],
        out_shape=[jax.ShapeDtypeStruct((S, DN_WIDTH), F32)] * 3 + [jax.ShapeDtypeStruct((N * DN_WIDTH, HD), F32)],
        scratch_shapes=[pltpu.VMEM((DN_HEADS, HD, HD), F32)],
        compiler_params=_params(1),
    )(u, w, qg, kd, aq, egl, gate, dn_gain)


def _dn_scan_bwd(w, qg, kd, aq, egl, gate, dn_gain, o, ddn):
    S = w.shape[0]
    C = DN_CHUNK
    N = S // C
    HD = DN_HEAD_DIM
    nc = SCAN_CHUNKS

    def body(w_ref, qg_ref, kd_ref, aq_ref, egl_ref, gate_ref, gain_ref, o_ref, ddn_ref,
             do_ref, dvn_ref, dgate_ref, dst_ref, small_ref, dstate_ref):
        @pl.when(pl.program_id(0) == 0)
        def _():
            dstate_ref[...] = jnp.zeros_like(dstate_ref)
            small_ref[...] = jnp.zeros_like(small_ref)

        gain = gain_ref[...]
        d_gain = jnp.zeros((1, 128), F32)
        for ci in reversed(range(nc)):
            rows = slice(ci * C, (ci + 1) * C)
            dsn = dstate_ref[...]
            for h in range(DN_HEADS):
                dst_ref[ci * DN_WIDTH + h * HD:ci * DN_WIDTH + (h + 1) * HD, :] = dsn[h]
            ov = _stack_heads(o_ref, rows)
            r = lax.rsqrt(jnp.mean(ov * ov, axis=-1, keepdims=True) + NORM_EPS)
            on = ov * r
            gt = _stack_heads(gate_ref, rows)
            sgt = _sigmoid(gt)
            silu_g = gt * sgt
            dy = _stack_heads(ddn_ref, rows)
            d_gain = d_gain + jnp.sum(jnp.sum(dy * on * silu_g, axis=1, keepdims=True), axis=0)
            dgate = dy * on * gain * (sgt * (1.0 + gt * (1.0 - sgt)))
            don = dy * gain * silu_g
            do = r * (don - on * jnp.mean(don * on, axis=-1, keepdims=True))
            d_vnew = _btn(aq_ref[:, rows, :], do) + _bnn(_stack_heads(kd_ref, rows), dsn)
            egl = jnp.stack([egl_ref[ci * 8 + h:ci * 8 + h + 1, :] for h in range(DN_HEADS)])
            dstate_ref[...] = _btn(_stack_heads(qg_ref, rows), do) + dsn * egl - _btn(_stack_heads(w_ref, rows), d_vnew)
            for h in range(DN_HEADS):
                sl = slice(h * HD, (h + 1) * HD)
                do_ref[rows, sl] = do[h]
                dvn_ref[rows, sl] = d_vnew[h]
                dgate_ref[rows, sl] = dgate[h]
        small_ref[...] += jnp.concatenate([d_gain, jnp.zeros((7, 128), F32)], axis=0)

    nb = N // nc
    tok = lambda wd: pl.BlockSpec((nc * C, wd), lambda i: (nb - 1 - i, 0))
    sq = pl.BlockSpec((DN_HEADS, nc * C, C), lambda i: (0, nb - 1 - i, 0))
    vec = pl.BlockSpec((1, 128), lambda i: (0, 0))
    return pl.pallas_call(
        body, name="dn_scan_bwd", grid=(nb,),
        in_specs=[tok(DN_WIDTH)] * 3 + [sq, pl.BlockSpec((nc * 8, 128), lambda i: (nb - 1 - i, 0)), tok(DN_WIDTH), vec,
                                       tok(DN_WIDTH), tok(DN_WIDTH)],
        out_specs=[tok(DN_WIDTH)] * 3 + [pl.BlockSpec((nc * DN_WIDTH, HD), lambda i: (nb - 1 - i, 0)),
                                        pl.BlockSpec((8, 128), lambda i: (0, 0))],
        out_shape=[jax.ShapeDtypeStruct((S, DN_WIDTH), F32)] * 3 + [jax.ShapeDtypeStruct((N * DN_WIDTH, HD), F32),
                                                                  jax.ShapeDtypeStruct((8, 128), F32)],
        scratch_shapes=[pltpu.VMEM((DN_HEADS, HD, HD), F32)],
        compiler_params=_params(1),
    )(w, qg, kd, aq, egl, gate, dn_gain, o, ddn)


def _dn_post(qn, kn, v, bd, avec, dvec, t_inv, v_new_all, states, dstates, do_all, dvn_all, comm=None):
    S = qn.shape[0]
    C = DN_CHUNK
    N = S // C
    HD = DN_HEAD_DIM
    nc = PREP_CHUNKS
    B = nc * DN_HEADS

    def body(q_ref, k_ref, v_ref, bd_ref, a_ref, d_ref, t_ref, vn_ref, st_ref, dst_ref, do_ref, dvn_ref,
             dq_ref, dk_ref, dv_ref, dbd_ref, small_ref):
        @pl.when(pl.program_id(0) == 0)
        def _():
            small_ref[...] = jnp.zeros_like(small_ref)

        avec = a_ref[...]
        bds = [bd_ref[ci * C:(ci + 1) * C, :] for ci in range(nc)]
        k = _stack_units(k_ref, nc)
        vv = _stack_units(v_ref, nc)
        t = jnp.concatenate([t_ref[:, ci * C:(ci + 1) * C, :] for ci in range(nc)], axis=0)
        c = _dn_common_b(bds, avec, d_ref[...], _stack_units(q_ref, nc), k, vv, t=t)
        q, kb, eg, u, w = c["q"], c["kb"], c["eg"], c["u"], c["w"]
        beta, decay, incl, strict, eye = c["beta"], c["decay"], c["incl"], c["strict"], c["eye"]
        st = jnp.stack([st_ref[b * HD:(b + 1) * HD, :] for b in range(B)])
        dsn = jnp.stack([dst_ref[b * HD:(b + 1) * HD, :] for b in range(B)])
        v_new = _stack_units(vn_ref, nc)
        do = _stack_units(do_ref, nc)
        d_vnew = _stack_units(dvn_ref, nc)
        egl = jnp.exp(c["g_last"])
        daq = jnp.where(incl, _bnt(do, v_new), 0.0)
        d_qg = _bnt(do, st)
        d_kd = _bnt(v_new, dsn)
        d_glast = jnp.sum(jnp.sum(dsn * st, axis=-1, keepdims=True), axis=1, keepdims=True) * egl
        d_w = -_bnt(d_vnew, st)
        d_ru = _btn(t, d_vnew)
        d_rw = _btn(t, d_w)
        da = -jnp.where(strict, _bnt(d_ru, u) + _bnt(d_rw, w), 0.0)
        dv = d_ru * beta
        dbeta = jnp.sum(d_ru * vv, axis=-1, keepdims=True)
        dkb = d_rw * eg
        dgc = jnp.sum(d_rw * c["rhs_w"], axis=-1, keepdims=True)
        dkk = da * decay
        ddecay = da * c["kk"]
        dkb = dkb + _bnn(dkk, k)
        dk = _btn(dkk, kb)
        dqk = daq * decay
        ddecay = ddecay + daq * c["qk"]
        dq = _bnn(dqk, k)
        dk = dk + _btn(dqk, q)
        m = ddecay * decay
        col_sum = jnp.sum(m, axis=1, keepdims=True)
        dgc = dgc + jnp.sum(m, axis=-1, keepdims=True) - jnp.sum(eye * col_sum, axis=-1, keepdims=True)
        dq = dq + d_qg * eg
        dgc = dgc + jnp.sum(d_qg * c["qg"], axis=-1, keepdims=True)
        dk = dk + d_kd * c["ekd"]
        tk = jnp.sum(d_kd * c["kd"], axis=-1, keepdims=True)
        dgc = dgc - tk
        d_glast = d_glast + jnp.sum(tk, axis=1, keepdims=True)
        dk = dk + dkb * beta
        dbeta = dbeta + jnp.sum(dkb * k, axis=-1, keepdims=True)
        dgc = dgc + jnp.where(c["last"], d_glast, 0.0)
        dgc_row = jnp.sum(eye * dgc, axis=1, keepdims=True)
        dgraw = jnp.sum(jnp.where(c["col"] >= c["row"], dgc_row, 0.0), axis=-1, keepdims=True)
        _store_units(dq_ref, dq * (HD ** -0.5), nc)
        _store_units(dk_ref, dk, nc)
        _store_units(dv_ref, dv, nc)
        dbraw = dbeta * beta * (1.0 - beta)
        dzc = dgraw * _sigmoid(c["zc"])
        ga = dgraw * c["graw"]
        lane = c["lane"]
        lane1 = lax.broadcasted_iota(jnp.int32, (1, 128), 1)
        neg_ea = -jnp.exp(avec)
        d_alog = jnp.zeros((1, 128), F32)
        d_dt = jnp.zeros((1, 128), F32)
        for ci in range(nc):
            dbd = jnp.zeros((C, 128), F32)
            for h in range(DN_HEADS):
                b = ci * DN_HEADS + h
                dz = dzc[b] * neg_ea
                dbd = dbd + jnp.where(lane == h, dbraw[b], 0.0) + jnp.where(lane == DN_HEADS + h, dz, 0.0)
                d_alog = d_alog + jnp.where(lane1 == DN_HEADS + h, jnp.sum(ga[b], axis=0, keepdims=True), 0.0)
                d_dt = d_dt + jnp.where(lane1 == DN_HEADS + h, jnp.sum(dz, axis=0, keepdims=True), 0.0)
            dbd_ref[ci * C:(ci + 1) * C, :] = dbd
        small_ref[...] += jnp.concatenate([d_alog, d_dt, jnp.zeros((6, 128), F32)], axis=0)

    tok = lambda wd: pl.BlockSpec((nc * C, wd), lambda n: (n, 0))
    big = pl.BlockSpec((nc * DN_WIDTH, HD), lambda n: (n, 0))
    sq = pl.BlockSpec((DN_HEADS, nc * C, C), lambda n: (0, n, 0))
    vec = pl.BlockSpec((1, 128), lambda n: (0, 0))
    return _call(
        body, (qn, kn, v, bd, avec, dvec, t_inv, v_new_all, states, dstates, do_all, dvn_all),
        name="dn_post", grid=(N // nc,), comm=comm,
        in_specs=[tok(DN_WIDTH)] * 3 + [tok(128), vec, vec, sq, tok(DN_WIDTH), big, big, tok(DN_WIDTH), tok(DN_WIDTH)],
        out_specs=[tok(DN_WIDTH)] * 3 + [tok(128), pl.BlockSpec((8, 128), lambda n: (0, 0))],
        out_shape=[jax.ShapeDtypeStruct((S, DN_WIDTH), F32)] * 3 + [jax.ShapeDtypeStruct((S, 128), F32),
                                                                  jax.ShapeDtypeStruct((8, 128), F32)])


def _outproj_fwd(x, attn, dn, w_out):
    S, D = x.shape
    tm = 512

    def body(x_ref, a_ref, d_ref, w_ref, xo_ref, mix_ref):
        a = a_ref[...].astype(BF16)
        dd = d_ref[...].astype(BF16)
        mix_ref[:, 0:ATTN_WIDTH] = a
        mix_ref[:, ATTN_WIDTH:] = dd
        xo_ref[...] = x_ref[...] + _nn(a, w_ref[0:ATTN_WIDTH, :]) + _nn(dd, w_ref[ATTN_WIDTH:, :])

    tok = lambda w: pl.BlockSpec((tm, w), lambda i: (i, 0))
    return pl.pallas_call(
        body, name="outproj_fwd", grid=(S // tm,),
        in_specs=[tok(D), tok(ATTN_WIDTH), tok(DN_WIDTH), pl.BlockSpec((D, D), lambda i: (0, 0))],
        out_specs=[tok(D), tok(D)],
        out_shape=[jax.ShapeDtypeStruct((S, D), F32), jax.ShapeDtypeStruct((S, D), BF16)],
        compiler_params=_params(1),
    )(x, attn, dn, w_out)


def _outproj_bwd(dx, w_out, attn):
    S, D = dx.shape
    tm = VIEW_TILE

    def body(dx_ref, w_ref, attn_ref, da1, da4, da16, dl1, dl4, dl16, ddn_ref, dxb_ref, planes):
        d = dx_ref[...].astype(BF16)
        dxb_ref[...] = d
        da = _nt(d, w_ref[0:ATTN_WIDTH, :])
        ddn_ref[...] = _nt(d, w_ref[ATTN_WIDTH:, :])
        _tile_to_views(da, planes, (da1, da4, da16))
        lo = lax.broadcasted_iota(jnp.int32, (tm, 128), 1) < 64
        cols = []
        for G in range(4):
            sl = slice(G * 128, (G + 1) * 128)
            t = da[:, sl] * attn_ref[:, sl]
            d0 = jnp.sum(jnp.where(lo, t, 0.0), axis=-1, keepdims=True)
            d1 = jnp.sum(jnp.where(lo, 0.0, t), axis=-1, keepdims=True)
            cols.append(jnp.where(lo, d0, d1))
        _tile_to_views(jnp.concatenate(cols, axis=1), planes, (dl1, dl4, dl16))

    tok = lambda w: pl.BlockSpec((tm, w), lambda i: (i, 0))
    views = [_view_spec(d) for d in DILATIONS]
    return pl.pallas_call(
        body, name="outproj_bwd", grid=(S // tm,),
        in_specs=[tok(D), pl.BlockSpec((D, D), lambda i: (0, 0)), tok(ATTN_WIDTH)],
        out_specs=views + views + [tok(DN_WIDTH), tok(D)],
        out_shape=[_view_shape(S, d, F32) for d in DILATIONS] * 2
                  + [jax.ShapeDtypeStruct((S, DN_WIDTH), F32), jax.ShapeDtypeStruct((S, D), BF16)],
        scratch_shapes=[pltpu.VMEM((4, tm, 128), F32)],
        compiler_params=_params(1),
    )(dx, w_out, attn)


def _loss_head(x, gain, target):
    S, D = x.shape
    tm = 512

    def body(x_ref, gain_ref, t_ref, loss_ref, dx_ref, dgain_ref):
        @pl.when(pl.program_id(0) == 0)
        def _():
            loss_ref[...] = jnp.zeros_like(loss_ref)
            dgain_ref[...] = jnp.zeros_like(dgain_ref)

        xf = x_ref[...]
        gain = gain_ref[...]
        r = lax.rsqrt(jnp.mean(xf * xf, axis=-1, keepdims=True) + NORM_EPS)
        xhat = xf * r
        err = xhat * gain - t_ref[...]
        part = 0.5 * jnp.sum(jnp.mean(err * err, axis=-1, keepdims=True), axis=0, keepdims=True)
        first = (lax.broadcasted_iota(jnp.int32, (8, 128), 0) == 0) & (lax.broadcasted_iota(jnp.int32, (8, 128), 1) == 0)
        loss_ref[...] += jnp.where(first, part, 0.0)
        dy = err * (1.0 / D)
        dgain_ref[...] += jnp.sum(dy * xhat, axis=0, keepdims=True)
        dxh = dy * gain
        dx_ref[...] = r * (dxh - xhat * jnp.mean(dxh * xhat, axis=-1, keepdims=True))

    tok = pl.BlockSpec((tm, D), lambda i: (i, 0))
    row = pl.BlockSpec((1, D), lambda i: (0, 0))
    return pl.pallas_call(
        body, name="loss_head", grid=(S // tm,),
        in_specs=[tok, row, tok],
        out_specs=[pl.BlockSpec((8, 128), lambda i: (0, 0)), tok, row],
        out_shape=[jax.ShapeDtypeStruct((8, 128), F32), jax.ShapeDtypeStruct((S, D), F32),
                   jax.ShapeDtypeStruct((1, D), F32)],
        compiler_params=_params(1),
    )(x, gain, target)


def _adamw(w, g, m, v, name):
    R, Ccols = w.shape
    tr = R
    for cand in (256, 128, 64, 32, 16, 8):
        if R % cand == 0:
            tr = cand
            break
    c1 = 1.0 - ADAM_B1 ** ADAM_STEP
    c2 = 1.0 - ADAM_B2 ** ADAM_STEP

    def body(w_ref, g_ref, m_ref, v_ref, d_ref, nm_ref, nv_ref):
        gv = g_ref[...]
        mn = ADAM_B1 * m_ref[...] + (1.0 - ADAM_B1) * gv
        vn = ADAM_B2 * v_ref[...] + (1.0 - ADAM_B2) * (gv * gv)
        nm_ref[...] = mn
        nv_ref[...] = vn
        d_ref[...] = -ADAM_LR * ((mn / c1) / (jnp.sqrt(vn / c2) + ADAM_EPS) + ADAM_WD * w_ref[...])

    spec = pl.BlockSpec((tr, Ccols), lambda i: (i, 0))
    return pl.pallas_call(
        body, name=name, grid=(R // tr,), in_specs=[spec] * 4, out_specs=[spec] * 3,
        out_shape=[jax.ShapeDtypeStruct((R, Ccols), F32)] * 3, compiler_params=_params(1),
    )(w, g, m, v)


LATE_WEIGHTS = ("w_out", "ffn2_gate", "ffn2_up", "ffn2_down")


def _local_step(x, target, wts, small, dist=None):
    g1, g2, gm, gf = small["norm_ffn1"], small["norm_ffn2"], small["norm_mix"], small["norm_final"]
    wts = dict(wts)

    def reduce_start(gs, tag):
        return _rs_add_pairs(gs, _swap_sibling(gs, True, "rs_swap_halves_" + tag), dist["c"], "rs_add_pairs_" + tag)

    (x1, h1, fg1, fu1), late = _ffn_fwd(x, g1, wts["ffn1_gate"], wts["ffn1_up"], wts["ffn1_down"], "ffn1_fwd",
                                        comm=_ag_comm(dist["late"]) if dist else None)
    if dist:
        wts.update(zip(LATE_WEIGHTS, late))
        wts["w_out"] = wts["w_out"].reshape(D_MODEL, D_MODEL)
    h2, *qkv, xq, xk, xv, gate, bd = _inproj_fwd(x1, gm, wts["w_in"])
    aq, ak, av = qkv[0:3], qkv[3:6], qkv[6:9]
    parts = [_attn_fwd(aq[p], ak[p], av[p], d, f"attn_fwd_d{d}") for p, d in enumerate(DILATIONS)]
    attn, *lse = _attn_merge(parts)
    conv_w = small["conv_w"]
    qn, kn, vv = _conv_fwd(xq, xk, xv, conv_w)
    dn_u, dn_w, dn_qg, dn_kd, dn_aq, dn_t, dn_egl = _dn_prep(qn, kn, vv, bd, small["avec"], small["dvec"])
    dn, o_dn, v_new, states = _dn_scan_fwd(dn_u, dn_w, dn_qg, dn_kd, dn_aq, dn_egl, gate, small["dn_norm"])
    x2, mix = _outproj_fwd(x1, attn, dn, wts["w_out"])
    (x3, h3, fg2, fu2), _ = _ffn_fwd(x2, g2, wts["ffn2_gate"], wts["ffn2_up"], wts["ffn2_down"], "ffn2_fwd")
    loss, dx3, d_gf = _loss_head(x3, gf, target)

    grads = {}
    (dx2, d_g2, dfg2, dfu2, act2, dout2), _ = _ffn_bwd(dx3, x2, g2, fg2, fu2, wts["ffn2_down"], wts["ffn2_gate"],
                                                      wts["ffn2_up"], "ffn2_bwd")
    tk = 2048
    grads["ffn2_gate"], _ = _dw_chunks(dfg2, h3, tk, "dw_ffn2_gate")
    grads["ffn2_up"], _ = _dw_chunks(dfu2, h3, tk, "dw_ffn2_up")
    grads["ffn2_down"], _ = _dw_chunks(act2, dout2, tk, "dw_ffn2_down")
    group_a = ("ffn2_gate", "ffn2_up", "ffn2_down")
    parts_a = reduce_start([grads[n] for n in group_a], "a") if dist else None

    *dviews, ddn, dx2b = _outproj_bwd(dx2, wts["w_out"], attn)
    dattn, dd = dviews[0:3], dviews[3:6]
    grads["w_out"] = _matmul_tn(mix, dx2b, D_MODEL, tk, "dw_out").reshape(N_CHIPS, D_MODEL // N_CHIPS, D_MODEL)

    daq, dak, dav = [], [], []
    for p, d in enumerate(DILATIONS):
        daq.append(_attn_bwd_q(aq[p], ak[p], av[p], dattn[p], lse[p], dd[p], d, f"attn_bwd_q_d{d}"))
        dk_p, dv_p = _attn_bwd_kv(aq[p], ak[p], av[p], dattn[p], lse[p], dd[p], d, f"attn_bwd_kv_d{d}")
        dak.append(dk_p)
        dav.append(dv_p)

    do_dn, dvn, dgate, dstates, d_dn_gain = _dn_scan_bwd(dn_w, dn_qg, dn_kd, dn_aq, dn_egl, gate, small["dn_norm"], o_dn, ddn)
    (dqn, dkn, dvv, dbd, dn_small), recv_a = _dn_post(qn, kn, vv, bd, small["avec"], small["dvec"], dn_t, v_new, states,
                                                      dstates, do_dn, dvn, comm=_rsx_comm(parts_a) if dist else None)
    dcq, dck, dcv, dwq, dwk, dwv = _conv_bwd_pre(xq, xk, xv, conv_w, dqn, dkn, dvv)
    dxq, dxk, dxv = _conv_bwd_x(dcq, dck, dcv, conv_w)
    d_conv = jnp.concatenate([dwq[:CONV_WIDTH], dwk[:CONV_WIDTH], dwv[:CONV_WIDTH]], axis=1)

    dx1, d_gm, dproj = _inproj_bwd(dx2, x1, gm, [daq, dak, dav], [dxq, dxk, dxv, dgate], dbd, wts["w_in"])
    gi = _matmul_tn(dproj, h2, IN_COLS_PADDED, 512, "dw_in")
    if dist:
        gi = jnp.concatenate([gi[:3072], gi[3584:3592], gi[3072:3584]], axis=0).reshape(N_CHIPS, IN_COLS // N_CHIPS, D_MODEL)
        gi = jnp.pad(gi, ((0, 0), (0, W_IN_ROWS - IN_COLS // N_CHIPS), (0, 0)))
    grads["w_in"] = gi
    group_b = ("w_in", "w_out")
    parts_b = reduce_start([grads[n] for n in group_b], "b") if dist else None

    (dx0, d_g1, dfg1, dfu1, act1, dout1), _ = _ffn_bwd(dx1, x, g1, fg1, fu1, wts["ffn1_down"], wts["ffn1_gate"],
                                                      wts["ffn1_up"], "ffn1_bwd")
    group_c = ("ffn1_gate", "ffn1_up", "ffn1_down")
    pending = parts_b if dist else []
    parts_c, recv_bc = [], []
    for n, (lhs, rhs) in zip(group_c, ((dfg1, h1), (dfu1, h1), (act1, dout1))):
        grads[n], landed = _dw_chunks(lhs, rhs, tk, "dw_" + n, comm=_rsx_comm(pending) if dist else None)
        recv_bc += list(landed)
        if dist:
            pending = reduce_start([grads[n]], n)
            parts_c += pending

    small_grads = dict(norm_ffn1=d_g1, norm_mix=d_gm, norm_ffn2=d_g2, norm_final=d_gf, conv_w=d_conv,
                       a_log=dn_small[0:1], dt_bias=dn_small[1:2], dn_norm=d_dn_gain[0:1])
    if dist:
        recv_bc += _rs_exchange_arrays(pending)
        recv_b, recv_c = recv_bc[:len(parts_b)], recv_bc[len(parts_b):]
        names = group_a + group_b + group_c
        totals = _rs_add_totals(list(parts_a) + list(parts_b) + list(parts_c), list(recv_a) + list(recv_b) + list(recv_c),
                                dist["chip"])
        theirs = _swap_sibling(totals, False, "rs_share_total")
        grads = {n: (mine, other) for n, mine, other in zip(names, totals, theirs)}
    return loss, dx0, grads, small_grads


HBM =pl.BlockSpec(memory_space=pl.ANY)
VMEM_SPEC = pl.BlockSpec(memory_space=pltpu.VMEM)


def _coords():
    return lax.axis_index("x"), lax.axis_index("y"), lax.axis_index("c")


def _remote(src, dst, send_sems, recv_sems, k, dev):
    return pltpu.make_async_remote_copy(src_ref=src, dst_ref=dst, send_sem=send_sems.at[k], recv_sem=recv_sems.at[k],
                                        device_id=dev, device_id_type=MESH)


def _allreduce_small(buf, name):
    R, Cc = buf.shape

    def body(src_ref, out_ref, recv_ref, send_sems, recv_sems):
        x, y, c = _coords()
        copies = []
        for m in range(1, 8):
            fx, fy, fc = (m >> 2) & 1, (m >> 1) & 1, m & 1
            dev = (x ^ fx if fx else x, y ^ fy if fy else y, c ^ fc if fc else c)
            cp = _remote(src_ref, recv_ref.at[m - 1], send_sems, recv_sems, m - 1, dev)
            cp.start()
            copies.append(cp)
        for cp in copies:
            cp.wait()
        r = [src_ref[...]] + [recv_ref[m] for m in range(7)]
        out_ref[...] = ((r[0] + r[1]) + (r[2] + r[3])) + ((r[4] + r[5]) + (r[6] + r[7]))

    return pl.pallas_call(
        body, name=name, out_shape=jax.ShapeDtypeStruct((R, Cc), F32),
        in_specs=[VMEM_SPEC], out_specs=VMEM_SPEC,
        scratch_shapes=[pltpu.VMEM((7, R, Cc), F32), pltpu.SemaphoreType.DMA((7,)), pltpu.SemaphoreType.DMA((7,))],
    )(buf)


BIG = ("ffn1_gate", "ffn1_up", "ffn1_down", "w_in", "w_out", "ffn2_gate", "ffn2_up", "ffn2_down")
W_IN_ROWS = 960


def _rows(ref, start, size):
    return ref.at[pl.ds(pl.multiple_of(start, 16), size)]


def _allgather_arrays(shards):
    n = len(shards)

    def body(*refs):
        _ag_start(refs[:n], refs[n:2 * n], refs[2 * n], refs[2 * n + 1])
        _ag_finish(refs[:n], refs[n:2 * n], refs[2 * n], refs[2 * n + 1])

    _, shapes, n_sems, _, _ = _ag_comm(shards)
    return pl.pallas_call(
        body, name="allgather_weights", out_shape=shapes, in_specs=[HBM] * n, out_specs=[HBM] * n,
        scratch_shapes=[pltpu.SemaphoreType.DMA((n_sems,)), pltpu.SemaphoreType.DMA((n_sems,))],
    )(*shards)


def _ag_copies(srcs, outs, send_sems, recv_sems):
    x, y, c = _coords()
    sib = (x, y, 1 - c)
    me = 2 * x + y
    others = [(1 - x, y), (x, 1 - y), (1 - x, 1 - y)]
    plan = []
    for a, (src, out) in enumerate(zip(srcs, outs)):
        h = src.shape[0] // 2
        cp = lambda s, d, k, dev: _remote(s, d, send_sems, recv_sems, 7 * a + k, dev)
        own = cp(src, out.at[me], 6, sib)
        sends = [cp(_rows(src, c * h, h), _rows(out.at[me], c * h, h), j, (ox, oy, c)) for j, (ox, oy) in enumerate(others)]
        mine = [_rows(out.at[2 * ox + oy], c * h, h) for ox, oy in others]
        theirs = [_rows(out.at[2 * ox + oy], (1 - c) * h, h) for ox, oy in others]
        arrivals = [cp(m, m, j, sib) for j, m in enumerate(mine)]
        forwards = [cp(m, m, 3 + j, sib) for j, m in enumerate(mine)]
        forwarded = [cp(t, t, 3 + j, sib) for j, t in enumerate(theirs)]
        plan.append((own, sends, forwards, arrivals, forwarded))
    return plan


def _ag_start(srcs, outs, send_sems, recv_sems):
    for own, sends, _, _, _ in _ag_copies(srcs, outs, send_sems, recv_sems):
        own.start()
        for cp in sends:
            cp.start()


def _ag_finish(srcs, outs, send_sems, recv_sems):
    plan = _ag_copies(srcs, outs, send_sems, recv_sems)
    for _, _, forwards, arrivals, _ in plan:
        for arrived, fwd in zip(arrivals, forwards):
            arrived.wait_recv()
            fwd.start()
    for own, sends, forwards, _, forwarded in plan:
        for cp in forwarded:
            cp.wait_recv()
        own.wait_recv()
        for cp in [own] + sends + forwards:
            cp.wait_send()


def _ag_comm(shards):
    shapes = [jax.ShapeDtypeStruct((N_CHIPS,) + s.shape, s.dtype) for s in shards]
    return (list(shards), shapes, 7 * len(shards), _ag_start, _ag_finish)


def _swap_sibling(arrs, pick_other_half, name):
    n = len(arrs)
    outs = [jax.ShapeDtypeStruct((a.shape[0], a.shape[1] // 2) + a.shape[2:] if pick_other_half else a.shape, a.dtype) for a in arrs]

    def body(*refs):
        srcs, dsts, send_sems, recv_sems = refs[:n], refs[n:2 * n], refs[2 * n], refs[2 * n + 1]
        x, y, c = _coords()
        cps = []
        for a in range(n):
            src = srcs[a]
            if pick_other_half:
                h = src.shape[1] // 2
                src = src.at[:, pl.ds(pl.multiple_of((1 - c) * h, 16), h)]
            cp = _remote(src, dsts[a], send_sems, recv_sems, a, (x, y, 1 - c))
            cp.start()
            cps.append(cp)
        for cp in cps:
            cp.wait()

    return pl.pallas_call(
        body, name=name, out_shape=outs, in_specs=[HBM] * n, out_specs=[HBM] * n,
        scratch_shapes=[pltpu.SemaphoreType.DMA((n,)), pltpu.SemaphoreType.DMA((n,))],
    )(*arrs)


def _rs_add_pairs(gs, others, c, name):
    n = len(gs)
    hbs = [g.shape[1] // 4 for g in gs]

    def body(c_ref, *refs):
        for a in range(n):
            refs[2 * n + a][...] = (refs[a][...] + refs[n + a][...]).astype(BF16)

    mine = lambda hb: pl.BlockSpec((None, hb, D_MODEL), lambda j, s, c_ref: (j, c_ref[0] * 2 + s, 0))
    flat = lambda hb: pl.BlockSpec((None, hb, D_MODEL), lambda j, s, c_ref: (j, s, 0))
    return pl.pallas_call(
        body, name=name,
        grid_spec=pltpu.PrefetchScalarGridSpec(
            num_scalar_prefetch=1, grid=(N_CHIPS, 2),
            in_specs=[mine(hb) for hb in hbs] + [flat(hb) for hb in hbs],
            out_specs=[flat(hb) for hb in hbs]),
        out_shape=[jax.ShapeDtypeStruct(o.shape, BF16) for o in others],
        compiler_params=_params(2),
    )(c, *gs, *others)


def _rs_exchange_arrays(parts):
    n = len(parts)

    def body(*refs):
        _rsx_start(refs[:n], refs[n:2 * n], refs[2 * n], refs[2 * n + 1])
        _rsx_finish(refs[:n], refs[n:2 * n], refs[2 * n], refs[2 * n + 1])

    _, shapes, n_sems, _, _ = _rsx_comm(parts)
    return pl.pallas_call(
        body, name="rs_exchange_chips", out_shape=shapes, in_specs=[HBM] * n, out_specs=[HBM] * n,
        scratch_shapes=[pltpu.SemaphoreType.DMA((n_sems,)), pltpu.SemaphoreType.DMA((n_sems,))],
    )(*parts)


def _rsx_copies(srcs, dsts, send_sems, recv_sems):
    x, y, c = _coords()
    others = [(1 - x, y), (x, 1 - y), (1 - x, 1 - y)]
    return [_remote(src.at[2 * ox + oy], dst.at[k], send_sems, recv_sems, 3 * a + k, (ox, oy, c))
            for a, (src, dst) in enumerate(zip(srcs, dsts)) for k, (ox, oy) in enumerate(others)]


def _rsx_start(srcs, dsts, send_sems, recv_sems):
    for cp in _rsx_copies(srcs, dsts, send_sems, recv_sems):
        cp.start()


def _rsx_finish(srcs, dsts, send_sems, recv_sems):
    for cp in _rsx_copies(srcs, dsts, send_sems, recv_sems):
        cp.wait()


def _rsx_comm(parts):
    shapes = [jax.ShapeDtypeStruct((3,) + p.shape[1:], p.dtype) for p in parts]
    return (list(parts), shapes, 3 * len(parts), _rsx_start, _rsx_finish)


def _rs_add_totals(parts, recvs, chip):
    n = len(parts)
    hbs = [p.shape[1] // 2 for p in parts]

    def body(chip_ref, *refs):
        f = lambda r: r[...].astype(F32)
        for a in range(n):
            p, r0, r1, r2 = refs[a], refs[n + 3 * a], refs[n + 3 * a + 1], refs[n + 3 * a + 2]
            refs[4 * n + a][...] = (f(p) + f(r0)) + (f(r1) + f(r2))

    own = lambda hb: pl.BlockSpec((None, hb, D_MODEL), lambda s, chip_ref: (chip_ref[0], s, 0))
    slot = lambda hb, k: pl.BlockSpec((None, hb, D_MODEL), lambda s, chip_ref, k=k: (k, s, 0))
    recv_specs = [slot(hb, k) for hb in hbs for k in range(3)]
    recv_args = [r for r in recvs for _ in range(3)]
    return pl.pallas_call(
        body, name="rs_add_totals",
        grid_spec=pltpu.PrefetchScalarGridSpec(
            num_scalar_prefetch=1, grid=(2,),
            in_specs=[own(hb) for hb in hbs] + recv_specs,
            out_specs=[pl.BlockSpec((hb, D_MODEL), lambda s, chip_ref: (s, 0)) for hb in hbs]),
        out_shape=[jax.ShapeDtypeStruct(p.shape[1:], F32) for p in parts],
        compiler_params=_params(1),
    )(chip, *parts, *recv_args)


def _permute_w_in(w):
    return jnp.concatenate([w[:, :3072], w[:, 3080:IN_COLS], w[:, 3072:3080],
                            jnp.zeros((w.shape[0], IN_COLS_PADDED - IN_COLS), w.dtype)], axis=1)


ROW_SHARDED = ("ffn1_down", "w_out", "ffn2_down")


def _pad_row(v):
    v = v.reshape(1, -1)
    return jnp.pad(v, ((0, 0), (0, D_MODEL - v.shape[1])))


def kernel(x, norm_ffn1, ffn1_gate, ffn1_up, ffn1_down, norm_mix, w_in, conv_w, a_log, dt_bias, dn_norm, w_out, norm_ffn2, ffn2_gate, ffn2_up, ffn2_down, norm_final, loss_target, m_norm_ffn1, m_ffn1_gate, m_ffn1_up, m_ffn1_down, m_norm_mix, m_w_in, m_conv_w, m_a_log, m_dt_bias, m_dn_norm, m_w_out, m_norm_ffn2, m_ffn2_gate, m_ffn2_up, m_ffn2_down, m_norm_final, v_norm_ffn1, v_ffn1_gate, v_ffn1_up, v_ffn1_down, v_norm_mix, v_w_in, v_conv_w, v_a_log, v_dt_bias, v_dn_norm, v_w_out, v_norm_ffn2, v_ffn2_gate, v_ffn2_up, v_ffn2_down, v_norm_final):
    cx, cy, cc = _coords()
    chip = 2 * cx + cy
    big_w = dict(ffn1_gate=ffn1_gate[0], ffn1_up=ffn1_up[0], ffn1_down=ffn1_down[0], w_in=w_in[0], w_out=w_out[0],
                 ffn2_gate=ffn2_gate[0], ffn2_up=ffn2_up[0], ffn2_down=ffn2_down[0])
    big_m = dict(ffn1_gate=m_ffn1_gate[0], ffn1_up=m_ffn1_up[0], ffn1_down=m_ffn1_down[0], w_in=m_w_in[0], w_out=m_w_out[0],
                 ffn2_gate=m_ffn2_gate[0], ffn2_up=m_ffn2_up[0], ffn2_down=m_ffn2_down[0])
    big_v = dict(ffn1_gate=v_ffn1_gate[0], ffn1_up=v_ffn1_up[0], ffn1_down=v_ffn1_down[0], w_in=v_w_in[0], w_out=v_w_out[0],
                 ffn2_gate=v_ffn2_gate[0], ffn2_up=v_ffn2_up[0], ffn2_down=v_ffn2_down[0])

    early = tuple(n for n in BIG if n not in LATE_WEIGHTS)
    wts = dict(zip(early, _allgather_arrays([big_w[n].astype(BF16) for n in early])))
    wts["w_in"] = _permute_w_in(jnp.concatenate([wts["w_in"][j] for j in range(N_CHIPS)], axis=1))
    dist = dict(late=[big_w[n].astype(BF16) for n in LATE_WEIGHTS], c=cc.reshape(1).astype(jnp.int32),
                chip=chip.reshape(1).astype(jnp.int32))

    conv_shard = conv_w[0]
    emb = jnp.concatenate([jnp.where((chip == j) & (cc == 0), conv_shard, 0.0) for j in range(N_CHIPS)], axis=1)
    emb = jnp.pad(emb.reshape(6, D_MODEL), ((0, 2), (0, 0)))
    conv_full = _allreduce_small(emb, "allgather_conv_w")[:6].reshape(CONV_WIDTH, 3 * DN_WIDTH)

    zvec = jnp.zeros((1, 128), F32)
    small = dict(norm_ffn1=norm_ffn1, norm_mix=norm_mix, norm_ffn2=norm_ffn2, norm_final=norm_final[None],
                 conv_w=conv_full, avec=zvec.at[0, DN_HEADS:2 * DN_HEADS].set(a_log[0]),
                 dvec=zvec.at[0, DN_HEADS:2 * DN_HEADS].set(dt_bias[0]), dn_norm=dn_norm)

    loss, grad_x, reduced, sg = _local_step(x[0], loss_target[0], wts, small, dist)

    rows = [sg["norm_ffn1"], sg["norm_mix"], sg["norm_ffn2"], sg["norm_final"], _pad_row(sg["a_log"]), _pad_row(sg["dt_bias"]),
            _pad_row(sg["dn_norm"]), _pad_row(loss[0:1]), sg["conv_w"].reshape(6, D_MODEL), jnp.zeros((2, D_MODEL), F32)]
    red = _allreduce_small(jnp.concatenate(rows, axis=0), "allreduce_small")
    loss_out = red[7, 0]
    g_conv_full = red[8:14].reshape(CONV_WIDTH, 3 * DN_WIDTH)
    g_conv = lax.dynamic_slice_in_dim(g_conv_full, chip * (3 * DN_WIDTH // N_CHIPS), 3 * DN_WIDTH // N_CHIPS, axis=1)
    g_small = dict(norm_ffn1=red[0:1], norm_mix=red[1:2], norm_ffn2=red[2:3], norm_final=red[3],
                   a_log=red[4:5, DN_HEADS:2 * DN_HEADS], dt_bias=red[5:6, DN_HEADS:2 * DN_HEADS], dn_norm=red[6:7, :DN_HEAD_DIM])

    shard_g = {}
    for n in BIG:
        mine, other = reduced[n]
        full = jnp.where(cc == 0, jnp.concatenate([mine, other], axis=0), jnp.concatenate([other, mine], axis=0))
        if n == "w_in":
            full = full[:IN_COLS // N_CHIPS]
        shard_g[n] = full if n in ROW_SHARDED else full.T

    out_g, out_d, out_m, out_v = {}, {}, {}, {}
    for n in BIG:
        d, nm, nv = _adamw(big_w[n], shard_g[n], big_m[n], big_v[n], "adamw_" + n)
        out_g[n], out_d[n], out_m[n], out_v[n] = shard_g[n][None], d[None], nm[None], nv[None]
    d, nm, nv = _adamw(conv_w[0], g_conv, m_conv_w[0], v_conv_w[0], "adamw_conv_w")
    out_g["conv_w"], out_d["conv_w"], out_m["conv_w"], out_v["conv_w"] = g_conv[None], d[None], nm[None], nv[None]

    small_names = ("norm_ffn1", "norm_mix", "norm_ffn2", "norm_final", "a_log", "dt_bias", "dn_norm")
    small_w = dict(norm_ffn1=norm_ffn1, norm_mix=norm_mix, norm_ffn2=norm_ffn2, norm_final=norm_final, a_log=a_log,
                   dt_bias=dt_bias, dn_norm=dn_norm)
    small_m = dict(norm_ffn1=m_norm_ffn1, norm_mix=m_norm_mix, norm_ffn2=m_norm_ffn2, norm_final=m_norm_final, a_log=m_a_log,
                   dt_bias=m_dt_bias, dn_norm=m_dn_norm)
    small_v = dict(norm_ffn1=v_norm_ffn1, norm_mix=v_norm_mix, norm_ffn2=v_norm_ffn2, norm_final=v_norm_final, a_log=v_a_log,
                   dt_bias=v_dt_bias, dn_norm=v_dn_norm)
    stack = lambda dct: jnp.concatenate([_pad_row(dct[n]) for n in small_names] + [jnp.zeros((1, D_MODEL), F32)], axis=0)
    d, nm, nv = _adamw(stack(small_w), stack(g_small), stack(small_m), stack(small_v), "adamw_small")
    for k, n in enumerate(small_names):
        shape = small_w[n].shape
        size = math.prod(shape)
        out_g[n] = g_small[n].reshape(shape)
        out_d[n], out_m[n], out_v[n] = (t[k, :size].reshape(shape) for t in (d, nm, nv))

    order = ("norm_ffn1", "ffn1_gate", "ffn1_up", "ffn1_down", "norm_mix", "w_in", "conv_w", "a_log", "dt_bias", "dn_norm",
             "w_out", "norm_ffn2", "ffn2_gate", "ffn2_up", "ffn2_down", "norm_final")
    return (loss_out, grad_x[None], *[out_g[n] for n in order], *[out_d[n] for n in order],
            *[out_m[n] for n in order], *[out_v[n] for n in order])
```

```python
import functools
import math

import jax
import jax.numpy as jnp
from jax import lax
from jax.experimental import pallas as pl
from jax.experimental.pallas import tpu as pltpu

F32 = jnp.float32
BF16 = jnp.bfloat16
HI = lax.Precision.HIGH

D_MODEL = 1024
ATTN_HEADS = 8
ATTN_WIDTH = 512
ATTN_BLOCK = 128
DILATIONS = (1, 4, 16)
DN_HEADS = 4
DN_HEAD_DIM = 128
DN_WIDTH = 512
DN_CHUNK = 64
CONV_WIDTH = 4
NORM_EPS = 1e-6
L2_EPS = 1e-6
IN_COLS = 3592
IN_COLS_PADDED = 3712
N_CHIPS = 4

ADAM_LR = 0.001
ADAM_B1 = 0.9
ADAM_B2 = 0.999
ADAM_EPS = 1e-08
ADAM_WD = 0.01
ADAM_STEP = 10

VMEM_LIMIT = 56 * 1024 * 1024
NEG_BIG = -1e30
MESH = pl.DeviceIdType.MESH


def _params(n_grid, vmem=VMEM_LIMIT):
    return pltpu.CompilerParams(dimension_semantics=("arbitrary",) * n_grid, vmem_limit_bytes=vmem)


def _call(body, args, *, name, grid, in_specs, out_specs, out_shape, scratch_shapes=(), comm=None):
    n_in, n_out, n_scr = len(in_specs), len(out_specs), len(scratch_shapes)
    hbm = pl.BlockSpec(memory_space=pl.ANY)
    srcs, dst_shapes, n_sems, start, finish = comm if comm is not None else ((), (), 0, None, None)
    ns, nd = len(srcs), len(dst_shapes)

    def full(*refs):
        ins, c_src = refs[:n_in], refs[n_in:n_in + ns]
        at = n_in + ns
        outs, c_dst = refs[at:at + n_out], refs[at + n_out:at + n_out + nd]
        scr = refs[at + n_out + nd:at + n_out + nd + n_scr]
        if comm is not None:
            ids = [pl.program_id(a) for a in range(len(grid))]
            first = functools.reduce(jnp.logical_and, [i == 0 for i in ids])
            last = functools.reduce(jnp.logical_and, [i == g - 1 for i, g in zip(ids, grid)])

            @pl.when(first)
            def _():
                start(c_src, c_dst, refs[-2], refs[-1])

        body(*ins, *outs, *scr)
        if comm is not None:
            @pl.when(last)
            def _():
                finish(c_src, c_dst, refs[-2], refs[-1])

    sems = [pltpu.SemaphoreType.DMA((n_sems,)), pltpu.SemaphoreType.DMA((n_sems,))] if comm is not None else []
    res = pl.pallas_call(
        full, name=name, grid=grid, in_specs=list(in_specs) + [hbm] * ns, out_specs=list(out_specs) + [hbm] * nd,
        out_shape=list(out_shape) + list(dst_shapes), scratch_shapes=list(scratch_shapes) + sems,
        compiler_params=_params(len(grid)),
    )(*args, *srcs)
    return res[:n_out], res[n_out:]


def _nt(a, b, precision=None):
    return lax.dot_general(a, b, (((1,), (1,)), ((), ())), preferred_element_type=F32, precision=precision)


def _tn(a, b, precision=None):
    return lax.dot_general(a, b, (((0,), (0,)), ((), ())), preferred_element_type=F32, precision=precision)


def _nn(a, b, precision=None):
    return jnp.dot(a, b, preferred_element_type=F32, precision=precision)


def _sigmoid(x):
    return 1.0 / (1.0 + jnp.exp(-x))


def _ffn_fwd(x, gain, wg, wu, wd, name, comm=None):
    S, D = x.shape
    nf, _, tf = wg.shape
    tm = 512

    def body(x_ref, gain_ref, wg_ref, wu_ref, wd_ref, xo_ref, h_ref, g_ref, u_ref, acc_ref, hs_ref):
        j = pl.program_id(1)

        @pl.when(j == 0)
        def _():
            xf = x_ref[...]
            r = lax.rsqrt(jnp.mean(xf * xf, axis=-1, keepdims=True) + NORM_EPS)
            h = (xf * r * gain_ref[...]).astype(BF16)
            hs_ref[...] = h
            h_ref[...] = h
            acc_ref[...] = jnp.zeros_like(acc_ref)

        h = hs_ref[...]
        g = _nn(h, wg_ref[...])
        u = _nn(h, wu_ref[...])
        g_ref[...] = g.astype(BF16)
        u_ref[...] = u.astype(BF16)
        act = g * _sigmoid(g) * u
        acc_ref[...] += _nn(act.astype(BF16), wd_ref[...])

        @pl.when(j == nf - 1)
        def _():
            xo_ref[...] = x_ref[...] + 0.5 * acc_ref[...]

    return _call(
        body, (x, gain, wg, wu, wd), name=name, grid=(S // tm, nf), comm=comm,
        in_specs=[pl.BlockSpec((tm, D), lambda i, j: (i, 0)),
                  pl.BlockSpec((1, D), lambda i, j: (0, 0)),
                  pl.BlockSpec((None, D, tf), lambda i, j: (j, 0, 0)),
                  pl.BlockSpec((None, D, tf), lambda i, j: (j, 0, 0)),
                  pl.BlockSpec((None, tf, D), lambda i, j: (j, 0, 0))],
        out_specs=[pl.BlockSpec((tm, D), lambda i, j: (i, 0)),
                   pl.BlockSpec((tm, D), lambda i, j: (i, 0)),
                   pl.BlockSpec((None, tm, tf), lambda i, j: (j, i, 0)),
                   pl.BlockSpec((None, tm, tf), lambda i, j: (j, i, 0))],
        out_shape=[jax.ShapeDtypeStruct((S, D), F32), jax.ShapeDtypeStruct((S, D), BF16),
                   jax.ShapeDtypeStruct((nf, S, tf), BF16), jax.ShapeDtypeStruct((nf, S, tf), BF16)],
        scratch_shapes=[pltpu.VMEM((tm, D), F32), pltpu.VMEM((tm, D), BF16)])


def _rmsnorm_bwd(dh, xf, gain):
    r = lax.rsqrt(jnp.mean(xf * xf, axis=-1, keepdims=True) + NORM_EPS)
    xhat = xf * r
    dgain = jnp.sum(dh * xhat, axis=0, keepdims=True)
    dxh = dh * gain
    dx = r * (dxh - xhat * jnp.mean(dxh * xhat, axis=-1, keepdims=True))
    return dx, dgain


def _ffn_bwd(dxo, x, gain, g, u, wd, wg, wu, name, comm=None):
    S, D = x.shape
    nf, _, tf = g.shape
    tm = 512

    def body(dxo_ref, x_ref, gain_ref, g_ref, u_ref, wd_ref, wg_ref, wu_ref,
             dx_ref, dgain_ref, dg_ref, du_ref, act_ref, dout_ref, acc_ref, ds_ref):
        i = pl.program_id(0)
        j = pl.program_id(1)

        @pl.when(j == 0)
        def _():
            d = (0.5 * dxo_ref[...]).astype(BF16)
            ds_ref[...] = d
            dout_ref[...] = d
            acc_ref[...] = jnp.zeros_like(acc_ref)

        @pl.when((i == 0) & (j == 0))
        def _():
            dgain_ref[...] = jnp.zeros_like(dgain_ref)

        for half in range(2):
            rows = slice(half * (tm // 2), (half + 1) * (tm // 2))
            dact = _nt(ds_ref[rows, :], wd_ref[...])
            gv = g_ref[rows, :].astype(F32)
            uv = u_ref[rows, :].astype(F32)
            sg = _sigmoid(gv)
            silu = gv * sg
            act_ref[rows, :] = (silu * uv).astype(BF16)
            dgv = (dact * uv * (sg * (1.0 + gv * (1.0 - sg)))).astype(BF16)
            duv = (dact * silu).astype(BF16)
            dg_ref[rows, :] = dgv
            du_ref[rows, :] = duv
            acc_ref[rows, :] += _nt(dgv, wg_ref[...]) + _nt(duv, wu_ref[...])

        @pl.when(j == nf - 1)
        def _():
            dx, dgain = _rmsnorm_bwd(acc_ref[...], x_ref[...], gain_ref[...])
            dx_ref[...] = dxo_ref[...] + dx
            dgain_ref[...] += dgain

    return _call(
        body, (dxo, x, gain, g, u, wd, wg, wu), name=name, grid=(S // tm, nf), comm=comm,
        in_specs=[pl.BlockSpec((tm, D), lambda i, j: (i, 0)),
                  pl.BlockSpec((tm, D), lambda i, j: (i, 0)),
                  pl.BlockSpec((1, D), lambda i, j: (0, 0)),
                  pl.BlockSpec((None, tm, tf), lambda i, j: (j, i, 0)),
                  pl.BlockSpec((None, tm, tf), lambda i, j: (j, i, 0)),
                  pl.BlockSpec((None, tf, D), lambda i, j: (j, 0, 0)),
                  pl.BlockSpec((None, D, tf), lambda i, j: (j, 0, 0)),
                  pl.BlockSpec((None, D, tf), lambda i, j: (j, 0, 0))],
        out_specs=[pl.BlockSpec((tm, D), lambda i, j: (i, 0)),
                   pl.BlockSpec((1, D), lambda i, j: (0, 0)),
                   pl.BlockSpec((None, tm, tf), lambda i, j: (j, i, 0)),
                   pl.BlockSpec((None, tm, tf), lambda i, j: (j, i, 0)),
                   pl.BlockSpec((None, tm, tf), lambda i, j: (j, i, 0)),
                   pl.BlockSpec((tm, D), lambda i, j: (i, 0))],
        out_shape=[jax.ShapeDtypeStruct((S, D), F32), jax.ShapeDtypeStruct((1, D), F32),
                   jax.ShapeDtypeStruct((nf, S, tf), BF16), jax.ShapeDtypeStruct((nf, S, tf), BF16),
                   jax.ShapeDtypeStruct((nf, S, tf), BF16), jax.ShapeDtypeStruct((S, D), BF16)],
        scratch_shapes=[pltpu.VMEM((tm, D), F32), pltpu.VMEM((tm, D), BF16)])


def _matmul_tn(a, b, tm, tk, name):
    K, M = a.shape
    N = b.shape[1]

    def body(a_ref, b_ref, o_ref):
        @pl.when(pl.program_id(1) == 0)
        def _():
            o_ref[...] = jnp.zeros_like(o_ref)

        o_ref[...] += _tn(a_ref[...], b_ref[...])

    return pl.pallas_call(
        body, name=name, grid=(M // tm, K // tk),
        in_specs=[pl.BlockSpec((tk, tm), lambda i, k: (k, i)),
                  pl.BlockSpec((tk, N), lambda i, k: (k, 0))],
        out_specs=pl.BlockSpec((tm, N), lambda i, k: (i, 0)),
        out_shape=jax.ShapeDtypeStruct((M, N), F32),
        compiler_params=_params(2),
    )(a, b)


def _dw_chunks(a, b, tk, name, comm=None):
    nf, S, tf = a.shape
    N = b.shape[1]

    def body(a_ref, b_ref, o_ref):
        @pl.when(pl.program_id(1) == 0)
        def _():
            o_ref[...] = jnp.zeros_like(o_ref)

        o_ref[...] += _tn(a_ref[...], b_ref[...])

    (out,), landed = _call(
        body, (a, b), name=name, grid=(nf, S // tk), comm=comm,
        in_specs=[pl.BlockSpec((None, tk, tf), lambda j, k: (j, k, 0)),
                  pl.BlockSpec((tk, N), lambda j, k: (k, 0))],
        out_specs=[pl.BlockSpec((None, tf, N), lambda j, k: (j, 0, 0))],
        out_shape=[jax.ShapeDtypeStruct((nf, tf, N), F32)])
    return out, landed


VIEW_TILE = 512


def _view_spec(d, tile=VIEW_TILE):
    return pl.BlockSpec((tile // d, d * ATTN_WIDTH), lambda i: (i, 0))


def _view_shape(S, d, dtype):
    return jax.ShapeDtypeStruct((S // d, d * ATTN_WIDTH), dtype)


def _tile_to_views(val, planes, out_refs):
    for g in range(4):
        planes[g] = val[:, g * 128:(g + 1) * 128]
    for d, ref in zip(DILATIONS, out_refs):
        if d == 1:
            ref[...] = val.astype(ref.dtype)
            continue
        for r in range(d):
            for g in range(4):
                ref[:, r * ATTN_WIDTH + g * 128:r * ATTN_WIDTH + (g + 1) * 128] = (
                    planes[g, pl.ds(r, planes.shape[1] // d, stride=d), :].astype(ref.dtype))


def _view_to_tile(ref, d, planes):
    if d == 1:
        return ref[...].astype(F32)
    for r in range(d):
        for g in range(4):
            planes[g, pl.ds(r, planes.shape[1] // d, stride=d), :] = (
                ref[:, r * ATTN_WIDTH + g * 128:r * ATTN_WIDTH + (g + 1) * 128].astype(F32))
    return jnp.concatenate([planes[g] for g in range(4)], axis=1)


def _inproj_fwd(x, gain, w_in_p):
    S, D = x.shape
    tm = VIEW_TILE
    W = ATTN_WIDTH

    def body(x_ref, gain_ref, w_ref, h_ref, q1, q4, q16, k1, k4, k16, v1, v4, v16, dq_ref, dk_ref, dv_ref, gate_ref, bd_ref,
             planes):
        xf = x_ref[...]
        r = lax.rsqrt(jnp.mean(xf * xf, axis=-1, keepdims=True) + NORM_EPS)
        h = (xf * r * gain_ref[...]).astype(BF16)
        h_ref[...] = h
        _tile_to_views(_nn(h, w_ref[:, 0:W]) * 0.125, planes, (q1, q4, q16))
        _tile_to_views(_nn(h, w_ref[:, W:2 * W]), planes, (k1, k4, k16))
        _tile_to_views(_nn(h, w_ref[:, 2 * W:3 * W]), planes, (v1, v4, v16))
        dq_ref[...] = _nn(h, w_ref[:, 3 * W:4 * W])
        dk_ref[...] = _nn(h, w_ref[:, 4 * W:5 * W])
        dv_ref[...] = _nn(h, w_ref[:, 5 * W:6 * W])
        gate_ref[...] = _nn(h, w_ref[:, 6 * W:7 * W])
        bd_ref[...] = _nn(h, w_ref[:, 7 * W:7 * W + 128])

    tok = lambda w: pl.BlockSpec((tm, w), lambda i: (i, 0))
    return pl.pallas_call(
        body, name="inproj_fwd", grid=(S // tm,),
        in_specs=[tok(D), pl.BlockSpec((1, D), lambda i: (0, 0)),
                  pl.BlockSpec((D, IN_COLS_PADDED), lambda i: (0, 0))],
        out_specs=[tok(D)] + [_view_spec(d) for d in DILATIONS] * 3 + [tok(W)] * 4 + [tok(128)],
        out_shape=[jax.ShapeDtypeStruct((S, D), BF16)] + [_view_shape(S, d, BF16) for d in DILATIONS] * 3
                  + [jax.ShapeDtypeStruct((S, W), F32)] * 4 + [jax.ShapeDtypeStruct((S, 128), F32)],
        scratch_shapes=[pltpu.VMEM((4, tm, 128), F32)],
        compiler_params=_params(1),
    )(x, gain, w_in_p)


def _inproj_bwd(dxo, x, gain, attn_grads, dsecs, dbd, w_in_p):
    S, D = x.shape
    tm = VIEW_TILE
    W = ATTN_WIDTH

    def body(dxo_ref, x_ref, gain_ref, *rest):
        views, (s3, s4, s5, s6, dbd_ref, w_ref, dx_ref, dgain_ref, dproj_ref, planes) = rest[:9], rest[9:]

        @pl.when(pl.program_id(0) == 0)
        def _():
            dgain_ref[...] = jnp.zeros_like(dgain_ref)

        secs = []
        for k in range(3):
            parts = [_view_to_tile(views[3 * k + p], d, planes) for p, d in enumerate(DILATIONS)]
            secs.append(parts[0] + parts[1] + parts[2])
        secs += [s3[...], s4[...], s5[...], s6[...]]
        dh = jnp.zeros((tm, D), F32)
        for k, s in enumerate(secs):
            d = s.astype(BF16)
            dproj_ref[:, k * W:(k + 1) * W] = d
            dh += _nt(d, w_ref[:, k * W:(k + 1) * W])
        d = dbd_ref[...].astype(BF16)
        dproj_ref[:, 7 * W:7 * W + 128] = d
        dh += _nt(d, w_ref[:, 7 * W:7 * W + 128])
        dx, dgain = _rmsnorm_bwd(dh, x_ref[...], gain_ref[...])
        dx_ref[...] = dxo_ref[...] + dx
        dgain_ref[...] += dgain

    tok = lambda w: pl.BlockSpec((tm, w), lambda i: (i, 0))
    return pl.pallas_call(
        body, name="inproj_bwd", grid=(S // tm,),
        in_specs=[tok(D), tok(D), pl.BlockSpec((1, D), lambda i: (0, 0))] + [_view_spec(d, tm) for d in DILATIONS] * 3
                 + [tok(W)] * 4 + [tok(128)] + [pl.BlockSpec((D, IN_COLS_PADDED), lambda i: (0, 0))],
        out_specs=[tok(D), pl.BlockSpec((1, D), lambda i: (0, 0)), tok(IN_COLS_PADDED)],
        out_shape=[jax.ShapeDtypeStruct((S, D), F32), jax.ShapeDtypeStruct((1, D), F32),
                   jax.ShapeDtypeStruct((S, IN_COLS_PADDED), BF16)],
        scratch_shapes=[pltpu.VMEM((4, tm, 128), F32)],
        compiler_params=_params(1),
    )(dxo, x, gain, *[g for grads in attn_grads for g in grads], *dsecs, dbd, w_in_p)


def _slope(h):
    return 2.0 ** (-8.0 * (h + 1) / ATTN_HEADS)


def _head_bias(steps, d, heads=tuple(range(ATTN_HEADS))):
    stepsf = steps.astype(F32)
    return jnp.stack([stepsf * (-_slope(h) * d) for h in heads])


def _hnt(a, b):
    return lax.dot_general(a, b, (((2,), (2,)), ((0,), (0,))), preferred_element_type=F32)


def _hnn(a, b):
    return lax.dot_general(a, b, (((2,), (1,)), ((0,), (0,))), preferred_element_type=F32)


def _blocks_per_step(nb):
    return next(n for n in (4, 2, 1) if nb % n == 0)


def _query_step_specs(qb):
    B = ATTN_BLOCK
    cur = pl.BlockSpec((qb * B, ATTN_WIDTH), lambda r, n: (n, r))
    prev = pl.BlockSpec((B, ATTN_WIDTH), lambda r, n: (jnp.maximum(qb * n - 1, 0), r))
    return cur, prev


def _prev_block(prev_ref, cur_ref, sub, sl):
    B = ATTN_BLOCK
    return prev_ref[:, sl] if sub == 0 else cur_ref[(sub - 1) * B:sub * B, sl]


def _head_cols(tile, lo, big):
    return [_head_col(tile, lo, big), _head_col(tile, jnp.logical_not(lo), big)]


def _attn_fwd(q, k, v, d, name):
    L = q.shape[0]
    nb = L // ATTN_BLOCK
    B = ATTN_BLOCK
    QB = _blocks_per_step(nb)

    def body(q_ref, kp_ref, kc_ref, vp_ref, vc_ref, o_ref, lse_ref):
        n = pl.program_id(1)
        qi = lax.broadcasted_iota(jnp.int32, (B, 2 * B), 0)
        kj = lax.broadcasted_iota(jnp.int32, (B, 2 * B), 1)
        steps = qi + B - kj
        band = (steps >= 0) & (steps <= B)
        lo = lax.broadcasted_iota(jnp.int32, (B, 128), 1) < 64
        bias = _head_bias(steps, d)
        for sub in range(QB):
            rows = slice(sub * B, (sub + 1) * B)
            valid = band & ((kj >= B) | (n > 0)) if sub == 0 else band
            qs, ks, vs = [], [], []
            for G in range(4):
                sl = slice(G * 128, (G + 1) * 128)
                qg = q_ref[rows, sl]
                kg = jnp.concatenate([_prev_block(kp_ref, kc_ref, sub, sl), kc_ref[rows, sl]], axis=0)
                vg = jnp.concatenate([_prev_block(vp_ref, vc_ref, sub, sl), vc_ref[rows, sl]], axis=0)
                qs += [jnp.where(lo, qg, jnp.zeros_like(qg)), jnp.where(lo, jnp.zeros_like(qg), qg)]
                ks += [kg, kg]
                vs += [vg, vg]
            s = jnp.where(valid, _hnt(jnp.stack(qs), jnp.stack(ks)) + bias, NEG_BIG)
            m = jnp.max(s, axis=-1, keepdims=True)
            p = jnp.exp(s - m)
            l = jnp.sum(p, axis=-1, keepdims=True)
            o = _hnn(p.astype(BF16), jnp.stack(vs)) / l
            lse = m + jnp.log(l)
            for G in range(4):
                sl = slice(G * 128, (G + 1) * 128)
                o_ref[rows, sl] = jnp.where(lo, o[2 * G], o[2 * G + 1])
                lse_ref[rows, sl] = jnp.where(lo, lse[2 * G], lse[2 * G + 1])

    cur, prev = _query_step_specs(QB)
    return pl.pallas_call(
        body, name=name, grid=(d, nb // QB),
        in_specs=[cur, prev, cur, prev, cur],
        out_specs=[cur, cur],
        out_shape=[jax.ShapeDtypeStruct((L, d * ATTN_WIDTH), F32)] * 2,
        compiler_params=_params(2),
    )(q, k, k, v, v)


def _attn_merge(parts):
    S = parts[0][0].shape[0]
    tm = VIEW_TILE

    def body(o1, s1, o2, s2, o3, s3, o_ref, lse1, lse4, lse16, planes):
        outs, lses = [], []
        for d, (o, s) in zip(DILATIONS, ((o1, s1), (o2, s2), (o3, s3))):
            outs.append(_view_to_tile(o, d, planes))
            lses.append(_view_to_tile(s, d, planes))
        mx = jnp.maximum(jnp.maximum(lses[0], lses[1]), lses[2])
        es = [jnp.exp(s - mx) for s in lses]
        den = es[0] + es[1] + es[2]
        o_ref[...] = (es[0] * outs[0] + es[1] * outs[1] + es[2] * outs[2]) / den
        _tile_to_views(mx + jnp.log(den), planes, (lse1, lse4, lse16))

    views = [_view_spec(d) for d in DILATIONS]
    flat = [t for p in parts for t in p]
    return pl.pallas_call(
        body, name="attn_merge", grid=(S // tm,),
        in_specs=[views[p] for p in range(3) for _ in range(2)],
        out_specs=[views[0]] + views,
        out_shape=[jax.ShapeDtypeStruct((S, ATTN_WIDTH), F32)] + [_view_shape(S, d, F32) for d in DILATIONS],
        scratch_shapes=[pltpu.VMEM((4, tm, 128), F32)],
        compiler_params=_params(1),
    )(*flat)


def _head_col(t, msk, big):
    if big:
        return jnp.max(jnp.where(msk, t, NEG_BIG), axis=-1, keepdims=True)
    return jnp.sum(jnp.where(msk, t, 0.0), axis=-1, keepdims=True) * (1.0 / 64.0)


def _attn_bwd_q(q, k, v, do, lse, dd, d, name):
    L = q.shape[0]
    nb = L // ATTN_BLOCK
    B = ATTN_BLOCK
    QB = _blocks_per_step(nb)

    def body(q_ref, kp_ref, kc_ref, vp_ref, vc_ref, do_ref, lse_ref, dd_ref, dq_ref):
        n = pl.program_id(1)
        qi = lax.broadcasted_iota(jnp.int32, (B, 2 * B), 0)
        kj = lax.broadcasted_iota(jnp.int32, (B, 2 * B), 1)
        steps = qi + B - kj
        band = (steps >= 0) & (steps <= B)
        lo = lax.broadcasted_iota(jnp.int32, (B, 128), 1) < 64
        bias = _head_bias(steps, d)
        for sub in range(QB):
            rows = slice(sub * B, (sub + 1) * B)
            valid = band & ((kj >= B) | (n > 0)) if sub == 0 else band
            qs, ks, vs, dos, lses, dcols = [], [], [], [], [], []
            for G in range(4):
                sl = slice(G * 128, (G + 1) * 128)
                qg = q_ref[rows, sl]
                kg = jnp.concatenate([_prev_block(kp_ref, kc_ref, sub, sl), kc_ref[rows, sl]], axis=0)
                vg = jnp.concatenate([_prev_block(vp_ref, vc_ref, sub, sl), vc_ref[rows, sl]], axis=0)
                dog = do_ref[rows, sl]
                qs += [jnp.where(lo, qg, jnp.zeros_like(qg)), jnp.where(lo, jnp.zeros_like(qg), qg)]
                dos += [jnp.where(lo, dog, 0.0).astype(BF16), jnp.where(lo, 0.0, dog).astype(BF16)]
                ks += [kg, kg]
                vs += [vg, vg]
                lses += _head_cols(lse_ref[rows, sl], lo, True)
                dcols += _head_cols(dd_ref[rows, sl], lo, False)
            kb = jnp.stack(ks)
            s = _hnt(jnp.stack(qs), kb) + bias
            p = jnp.where(valid, jnp.exp(jnp.where(valid, s, NEG_BIG) - jnp.stack(lses)), 0.0)
            dp = _hnt(jnp.stack(dos), jnp.stack(vs))
            ds = p * (dp - jnp.stack(dcols))
            dq = _hnn(ds.astype(BF16), kb) * 0.125
            for G in range(4):
                dq_ref[rows, G * 128:(G + 1) * 128] = jnp.where(lo, dq[2 * G], dq[2 * G + 1]).astype(BF16)

    cur, prev = _query_step_specs(QB)
    return pl.pallas_call(
        body, name=name, grid=(d, nb // QB), in_specs=[cur, prev, cur, prev, cur, cur, cur, cur], out_specs=cur,
        out_shape=jax.ShapeDtypeStruct((L, d * ATTN_WIDTH), BF16), compiler_params=_params(2),
    )(q, k, k, v, v, do, lse, dd)


def _attn_bwd_kv(q, k, v, do, lse, dd, d, name):
    L = q.shape[0]
    nb = L // ATTN_BLOCK
    B = ATTN_BLOCK
    KB = _blocks_per_step(nb)
    n_steps = nb // KB

    def body(k_ref, v_ref, qc_ref, qn_ref, doc_ref, don_ref, lsec_ref, lsen_ref, ddc_ref, ddn_ref, dk_ref, dv_ref):
        j = pl.program_id(1)
        qrow = lax.broadcasted_iota(jnp.int32, (2 * B, B), 0)
        kk = lax.broadcasted_iota(jnp.int32, (2 * B, B), 1)
        steps = qrow - kk
        band = (steps >= 0) & (steps <= B)
        lo2 = lax.broadcasted_iota(jnp.int32, (2 * B, 128), 1) < 64
        lo = lax.broadcasted_iota(jnp.int32, (B, 128), 1) < 64
        stepsf = steps.astype(F32)
        for sub in range(KB):
            rows = slice(sub * B, (sub + 1) * B)
            last = sub == KB - 1
            valid = band & ((qrow < B) | (j < n_steps - 1)) if last else band
            after = lambda cur_ref, nxt_ref, sl: nxt_ref[:, sl] if last else cur_ref[(sub + 1) * B:(sub + 2) * B, sl]
            for G in range(4):
                sl = slice(G * 128, (G + 1) * 128)
                kg = k_ref[rows, sl]
                vg = v_ref[rows, sl]
                qq = jnp.concatenate([qc_ref[rows, sl], after(qc_ref, qn_ref, sl)], axis=0)
                doo = jnp.concatenate([doc_ref[rows, sl], after(doc_ref, don_ref, sl)], axis=0)
                lse2 = jnp.concatenate([lsec_ref[rows, sl], after(lsec_ref, lsen_ref, sl)], axis=0)
                dd2 = jnp.concatenate([ddc_ref[rows, sl], after(ddc_ref, ddn_ref, sl)], axis=0)
                doo_b = doo.astype(BF16)
                dks, dvs = [], []
                for half in (0, 1):
                    msk = lo2 if half == 0 else jnp.logical_not(lo2)
                    qm = jnp.where(msk, qq, jnp.zeros_like(qq))
                    s = _nt(qm, kg) - (_slope(2 * G + half) * d) * stepsf
                    lse_c = _head_col(lse2, msk, True)
                    p = jnp.where(valid, jnp.exp(jnp.where(valid, s, NEG_BIG) - lse_c), 0.0)
                    dvs.append(_tn(p.astype(BF16), doo_b))
                    dom = jnp.where(msk, doo, 0.0).astype(BF16)
                    dp = _nt(dom, vg)
                    dcol = _head_col(dd2, msk, False)
                    ds = p * (dp - dcol)
                    dks.append(_tn(ds.astype(BF16), qq))
                dk_ref[rows, sl] = jnp.where(lo, dks[0], dks[1]).astype(BF16)
                dv_ref[rows, sl] = jnp.where(lo, dvs[0], dvs[1]).astype(BF16)

    cur = pl.BlockSpec((KB * B, ATTN_WIDTH), lambda r, j: (j, r))
    nxt = pl.BlockSpec((B, ATTN_WIDTH), lambda r, j: (jnp.minimum(KB * (j + 1), nb - 1), r))
    return pl.pallas_call(
        body, name=name, grid=(d, n_steps), in_specs=[cur, cur, cur, nxt, cur, nxt, cur, nxt, cur, nxt],
        out_specs=[cur, cur],
        out_shape=[jax.ShapeDtypeStruct((L, d * ATTN_WIDTH), BF16)] * 2, compiler_params=_params(2),
    )(k, v, q, q, do, do, lse, lse, dd, dd)


CONV_T = 512
HALO = 8


def _per_head(head, refs):
    for h in range(DN_HEADS):
        lanes = pl.ds(h * DN_HEAD_DIM, DN_HEAD_DIM)
        head(*[r.at[:, lanes] for r in refs[:-1]], refs[-1])


def _conv_taps(pad_ref, w, T):
    acc = pad_ref[pl.ds(HALO - 3, T), :] * w[0:1, :]
    for j in range(1, CONV_WIDTH):
        acc = acc + pad_ref[pl.ds(HALO - 3 + j, T), :] * w[j:j + 1, :]
    return acc


def _conv_fwd(xq, xk, xv, conv_w):
    S = xq.shape[0]
    T = CONV_T

    def body(*refs):
        _per_head(head, refs)

    def head(xq_ref, xqh_ref, xk_ref, xkh_ref, xv_ref, xvh_ref, wq_ref, wk_ref, wv_ref,
             qn_ref, kn_ref, v_ref, pad_ref):
        i = pl.program_id(0)

        def act(x_ref, xh_ref, w_ref):
            pad_ref[pl.ds(0, HALO), :] = jnp.where(i > 0, xh_ref[...], 0.0)
            pad_ref[pl.ds(HALO, T), :] = x_ref[...]
            c = _conv_taps(pad_ref, w_ref[...], T)
            return c * _sigmoid(c)

        def l2n(t):
            return t * lax.rsqrt(jnp.sum(t * t, axis=-1, keepdims=True) + L2_EPS)

        qn_ref[...] = l2n(act(xq_ref, xqh_ref, wq_ref))
        kn_ref[...] = l2n(act(xk_ref, xkh_ref, wk_ref))
        v_ref[...] = act(xv_ref, xvh_ref, wv_ref)

    tile = pl.BlockSpec((T, DN_WIDTH), lambda i: (i, 0))
    halo = pl.BlockSpec((HALO, DN_WIDTH), lambda i: (jnp.maximum(i * (T // HALO) - 1, 0), 0))
    wspec = lambda sec: pl.BlockSpec((CONV_WIDTH, DN_WIDTH), lambda i, sec=sec: (0, sec))
    return pl.pallas_call(
        body, name="dn_conv_fwd", grid=(S // T,),
        in_specs=[tile, halo, tile, halo, tile, halo, wspec(0), wspec(1), wspec(2)],
        out_specs=[tile, tile, tile],
        out_shape=[jax.ShapeDtypeStruct((S, DN_WIDTH), F32)] * 3,
        scratch_shapes=[pltpu.VMEM((T + HALO, 128), F32)],
        compiler_params=_params(1),
    )(xq, xq, xk, xk, xv, xv, conv_w, conv_w, conv_w)


def _conv_bwd_pre(xq, xk, xv, conv_w, dqn, dkn, dv):
    S = xq.shape[0]
    T = CONV_T

    def body(*refs):
        _per_head(head, refs)

    def head(xq_ref, xqh_ref, xk_ref, xkh_ref, xv_ref, xvh_ref, wq_ref, wk_ref, wv_ref,
             dqn_ref, dkn_ref, dv_ref, dcq_ref, dck_ref, dcv_ref, dwq_ref, dwk_ref, dwv_ref, pad_ref):
        i = pl.program_id(0)

        def one(x_ref, xh_ref, w_ref, dy_ref, dc_ref, dw_ref, normed):
            pad_ref[pl.ds(0, HALO), :] = jnp.where(i > 0, xh_ref[...], 0.0)
            pad_ref[pl.ds(HALO, T), :] = x_ref[...]
            c = _conv_taps(pad_ref, w_ref[...], T)
            sg = _sigmoid(c)
            a = c * sg
            dy = dy_ref[...]
            if normed:
                r = lax.rsqrt(jnp.sum(a * a, axis=-1, keepdims=True) + L2_EPS)
                y = a * r
                da = r * (dy - y * jnp.sum(dy * y, axis=-1, keepdims=True))
            else:
                da = dy
            dc = da * (sg * (1.0 + c * (1.0 - sg)))
            dc_ref[...] = dc

            @pl.when(i == 0)
            def _():
                dw_ref[...] = jnp.zeros_like(dw_ref)

            rows = [jnp.sum(dc * pad_ref[pl.ds(HALO - 3 + j, T), :], axis=0, keepdims=True) for j in range(CONV_WIDTH)]
            dw_ref[...] += jnp.concatenate(rows + [jnp.zeros((8 - CONV_WIDTH, 128), F32)], axis=0)

        one(xq_ref, xqh_ref, wq_ref, dqn_ref, dcq_ref, dwq_ref, True)
        one(xk_ref, xkh_ref, wk_ref, dkn_ref, dck_ref, dwk_ref, True)
        one(xv_ref, xvh_ref, wv_ref, dv_ref, dcv_ref, dwv_ref, False)

    tile = pl.BlockSpec((T, DN_WIDTH), lambda i: (i, 0))
    halo = pl.BlockSpec((HALO, DN_WIDTH), lambda i: (jnp.maximum(i * (T // HALO) - 1, 0), 0))
    wspec = lambda sec: pl.BlockSpec((CONV_WIDTH, DN_WIDTH), lambda i, sec=sec: (0, sec))
    dwspec = pl.BlockSpec((8, DN_WIDTH), lambda i: (0, 0))
    return pl.pallas_call(
        body, name="dn_conv_bwd_pre", grid=(S // T,),
        in_specs=[tile, halo, tile, halo, tile, halo, wspec(0), wspec(1), wspec(2), tile, tile, tile],
        out_specs=[tile, tile, tile, dwspec, dwspec, dwspec],
        out_shape=[jax.ShapeDtypeStruct((S, DN_WIDTH), F32)] * 3 + [jax.ShapeDtypeStruct((8, DN_WIDTH), F32)] * 3,
        scratch_shapes=[pltpu.VMEM((T + HALO, 128), F32)],
        compiler_params=_params(1),
    )(xq, xq, xk, xk, xv, xv, conv_w, conv_w, conv_w, dqn, dkn, dv)


def _conv_bwd_x(dcq, dck, dcv, conv_w):
    S = dcq.shape[0]
    T = CONV_T
    nt = S // T

    def body(*refs):
        _per_head(head, refs)

    def head(dq_ref, dqh_ref, dk_ref, dkh_ref, dv_ref, dvh_ref, wq_ref, wk_ref, wv_ref,
             oq_ref, ok_ref, ov_ref, pad_ref):
        i = pl.program_id(0)

        def one(d_ref, dh_ref, w_ref, o_ref):
            pad_ref[pl.ds(0, T), :] = d_ref[...]
            pad_ref[pl.ds(T, HALO), :] = jnp.where(i < nt - 1, dh_ref[...], 0.0)
            w = w_ref[...]
            acc = pad_ref[pl.ds(3, T), :] * w[0:1, :]
            for j in range(1, CONV_WIDTH):
                acc = acc + pad_ref[pl.ds(3 - j, T), :] * w[j:j + 1, :]
            o_ref[...] = acc

        one(dq_ref, dqh_ref, wq_ref, oq_ref)
        one(dk_ref, dkh_ref, wk_ref, ok_ref)
        one(dv_ref, dvh_ref, wv_ref, ov_ref)

    tile = pl.BlockSpec((T, DN_WIDTH), lambda i: (i, 0))
    halo = pl.BlockSpec((HALO, DN_WIDTH), lambda i: (jnp.minimum((i + 1) * (T // HALO), S // HALO - 1), 0))
    wspec = lambda sec: pl.BlockSpec((CONV_WIDTH, DN_WIDTH), lambda i, sec=sec: (0, sec))
    return pl.pallas_call(
        body, name="dn_conv_bwd_x", grid=(nt,),
        in_specs=[tile, halo, tile, halo, tile, halo, wspec(0), wspec(1), wspec(2)],
        out_specs=[tile, tile, tile],
        out_shape=[jax.ShapeDtypeStruct((S, DN_WIDTH), F32)] * 3,
        scratch_shapes=[pltpu.VMEM((T + HALO, 128), F32)],
        compiler_params=_params(1),
    )(dcq, dcq, dck, dck, dcv, dcv, conv_w, conv_w, conv_w)


PREP_CHUNKS = 4
SCAN_CHUNKS = 8


def _bnn(a, b):
    return lax.dot_general(a, b, (((2,), (1,)), ((0,), (0,))), preferred_element_type=F32, precision=HI)


def _bnt(a, b):
    return lax.dot_general(a, b, (((2,), (2,)), ((0,), (0,))), preferred_element_type=F32, precision=HI)


def _btn(a, b):
    return lax.dot_general(a, b, (((1,), (1,)), ((0,), (0,))), preferred_element_type=F32, precision=HI)


def _tri_inverse_b(a, blk, eye):
    dg = jnp.where(blk, a, 0.0)
    lo = a - dg
    d2 = _bnn(dg, dg)
    d4 = _bnn(d2, d2)
    d8 = _bnn(d4, d4)
    td = _bnn(_bnn(_bnn(eye - dg, eye + d2), eye + d4), eye + d8)
    b = _bnn(td, lo)
    b2 = _bnn(b, b)
    return _bnn(_bnn(eye - b, eye + b2), td)


def _dn_common_b(bds, avec, dvec, q_raw, k, v, t=None):
    C = DN_CHUNK
    lane = lax.broadcasted_iota(jnp.int32, (C, 128), 1)
    row = lax.broadcasted_iota(jnp.int32, (1, C, C), 1)
    col = lax.broadcasted_iota(jnp.int32, (1, C, C), 2)
    incl = row >= col
    strict = row > col
    eye = (row == col).astype(F32)
    blk = (row // 16) == (col // 16)
    pick = lambda tile, ln: jnp.sum(jnp.where(lane == ln, tile, 0.0), axis=-1, keepdims=True)
    betas, graws, zcs = [], [], []
    for bd in bds:
        z = bd + dvec
        g_all = -jnp.exp(avec) * (jnp.maximum(z, 0.0) + jnp.log(1.0 + jnp.exp(-jnp.abs(z))))
        beta_all = _sigmoid(bd)
        for h in range(DN_HEADS):
            betas.append(pick(beta_all, h))
            graws.append(pick(g_all, DN_HEADS + h))
            zcs.append(pick(z, DN_HEADS + h))
    beta, graw, zc = jnp.stack(betas), jnp.stack(graws), jnp.stack(zcs)
    to_row = lambda c: jnp.sum(eye * c, axis=1, keepdims=True)
    gc = jnp.sum(jnp.where(incl, to_row(graw), 0.0), axis=-1, keepdims=True)
    decay = jnp.exp(jnp.where(incl, gc - to_row(gc), NEG_BIG))
    q = q_raw * (DN_HEAD_DIM ** -0.5)
    kb = k * beta
    kk = _bnt(kb, k)
    if t is None:
        t = _tri_inverse_b(jnp.where(strict, kk * decay, 0.0), blk, eye)
    eg = jnp.exp(gc)
    rhs_w = kb * eg
    u = _bnn(t, v * beta)
    w = _bnn(t, rhs_w)
    qk = _bnt(q, k)
    aq = jnp.where(incl, qk * decay, 0.0)
    last = lax.broadcasted_iota(jnp.int32, (1, C, 1), 1) == C - 1
    g_last = jnp.sum(jnp.where(last, gc, 0.0), axis=1, keepdims=True)
    ekd = jnp.exp(g_last - gc)
    return dict(beta=beta, graw=graw, zc=zc, gc=gc, decay=decay, q=q, kb=kb, kk=kk, t=t, eg=eg, rhs_w=rhs_w,
                u=u, w=w, qk=qk, aq=aq, g_last=g_last, ekd=ekd, kd=k * ekd, qg=q * eg,
                incl=incl, strict=strict, eye=eye, lane=lane, row=row, col=col, last=last)


def _stack_heads(ref, rows):
    return jnp.stack([ref[rows, h * DN_HEAD_DIM:(h + 1) * DN_HEAD_DIM] for h in range(DN_HEADS)])


def _stack_units(ref, nc):
    C = DN_CHUNK
    return jnp.concatenate([_stack_heads(ref, slice(ci * C, (ci + 1) * C)) for ci in range(nc)], axis=0)


def _store_units(ref, val, nc):
    C = DN_CHUNK
    for ci in range(nc):
        for h in range(DN_HEADS):
            ref[ci * C:(ci + 1) * C, h * DN_HEAD_DIM:(h + 1) * DN_HEAD_DIM] = val[ci * DN_HEADS + h]


def _dn_prep(qn, kn, v, bd, avec, dvec):
    S = qn.shape[0]
    C = DN_CHUNK
    N = S // C
    nc = PREP_CHUNKS

    def body(q_ref, k_ref, v_ref, bd_ref, a_ref, d_ref, u_ref, w_ref, qg_ref, kd_ref, aq_ref, t_ref, egl_ref):
        bds = [bd_ref[ci * C:(ci + 1) * C, :] for ci in range(nc)]
        c = _dn_common_b(bds, a_ref[...], d_ref[...], _stack_units(q_ref, nc), _stack_units(k_ref, nc), _stack_units(v_ref, nc))
        _store_units(u_ref, c["u"], nc)
        _store_units(w_ref, c["w"], nc)
        _store_units(qg_ref, c["qg"], nc)
        _store_units(kd_ref, c["kd"], nc)
        egl = jnp.broadcast_to(jnp.exp(c["g_last"]), (nc * DN_HEADS, 1, 128))
        for ci in range(nc):
            for h in range(DN_HEADS):
                aq_ref[h, ci * C:(ci + 1) * C, :] = c["aq"][ci * DN_HEADS + h]
                t_ref[h, ci * C:(ci + 1) * C, :] = c["t"][ci * DN_HEADS + h]
            egl_ref[ci * 8:(ci + 1) * 8, :] = jnp.concatenate(
                [egl[ci * DN_HEADS + h] for h in range(DN_HEADS)] + [jnp.zeros((8 - DN_HEADS, 128), F32)], axis=0)

    tok = lambda w: pl.BlockSpec((nc * C, w), lambda n: (n, 0))
    sq = pl.BlockSpec((DN_HEADS, nc * C, C), lambda n: (0, n, 0))
    vec = pl.BlockSpec((1, 128), lambda n: (0, 0))
    return pl.pallas_call(
        body, name="dn_prep", grid=(N // nc,),
        in_specs=[tok(DN_WIDTH)] * 3 + [tok(128), vec, vec],
        out_specs=[tok(DN_WIDTH)] * 4 + [sq, sq, pl.BlockSpec((nc * 8, 128), lambda n: (n, 0))],
        out_shape=[jax.ShapeDtypeStruct((S, DN_WIDTH), F32)] * 4 + [jax.ShapeDtypeStruct((DN_HEADS, S, C), F32)] * 2
                  + [jax.ShapeDtypeStruct((N * 8, 128), F32)],
        compiler_params=_params(1),
    )(qn, kn, v, bd, avec, dvec)


def _dn_scan_fwd(u, w, qg, kd, aq, egl, gate, dn_gain):
    S = u.shape[0]
    C = DN_CHUNK
    N = S // C
    HD = DN_HEAD_DIM
    nc = SCAN_CHUNKS

    def body(u_ref, w_ref, qg_ref, kd_ref, aq_ref, egl_ref, gate_ref, gain_ref, dn_ref, o_ref, vn_ref, st_ref, state_ref):
        @pl.when(pl.program_id(0) == 0)
        def _():
            state_ref[...] = jnp.zeros_like(state_ref)

        gain = gain_ref[...]
        for ci in range(nc):
            rows = slice(ci * C, (ci + 1) * C)
            st = state_ref[...]
            for h in range(DN_HEADS):
                st_ref[ci * DN_WIDTH + h * HD:ci * DN_WIDTH + (h + 1) * HD, :] = st[h]
            v_new = _stack_heads(u_ref, rows) - _bnn(_stack_heads(w_ref, rows), st)
            o = _bnn(_stack_heads(qg_ref, rows), st) + _bnn(aq_ref[:, rows, :], v_new)
            egl = jnp.stack([egl_ref[ci * 8 + h:ci * 8 + h + 1, :] for h in range(DN_HEADS)])
            state_ref[...] = st * egl + _btn(_stack_heads(kd_ref, rows), v_new)
            r = lax.rsqrt(jnp.mean(o * o, axis=-1, keepdims=True) + NORM_EPS)
            gt = _stack_heads(gate_ref, rows)
            dn = o * r * gain * (gt * _sigmoid(gt))
            for h in range(DN_HEADS):
                sl = slice(h * HD, (h + 1) * HD)
                vn_ref[rows, sl] = v_new[h]
                o_ref[rows, sl] = o[h]
                dn_ref[rows, sl] = dn[h]

    tok = lambda wd: pl.BlockSpec((nc * C, wd), lambda n: (n, 0))
    sq = pl.BlockSpec((DN_HEADS, nc * C, C), lambda n: (0, n, 0))
    vec = pl.BlockSpec((1, 128), lambda n: (0, 0))
    return pl.pallas_call(
        body, name="dn_scan_fwd", grid=(N // nc,),
        in_specs=[tok(DN_WIDTH)] * 4 + [sq, pl.BlockSpec((nc * 8, 128), lambda n: (n, 0)), tok(DN_WIDTH), vec],
        out_specs=[tok(DN_WIDTH)] * 3 + [pl.BlockSpec((nc * DN_WIDTH, HD), lambda n: (n, 0))],
        out_shape=[jax.ShapeDtypeStruct((S, DN_WIDTH), F32)] * 3 + [jax.ShapeDtypeStruct((N * DN_WIDTH, HD), F32)],
        scratch_shapes=[pltpu.VMEM((DN_HEADS, HD, HD), F32)],
        compiler_params=_params(1),
    )(u, w, qg, kd, aq, egl, gate, dn_gain)


def _dn_scan_bwd(w, qg, kd, aq, egl, gate, dn_gain, o, ddn):
    S = w.shape[0]
    C = DN_CHUNK
    N = S // C
    HD = DN_HEAD_DIM
    nc = SCAN_CHUNKS

    def body(w_ref, qg_ref, kd_ref, aq_ref, egl_ref, gate_ref, gain_ref, o_ref, ddn_ref,
             do_ref, dvn_ref, dgate_ref, dst_ref, small_ref, dstate_ref):
        @pl.when(pl.program_id(0) == 0)
        def _():
            dstate_ref[...] = jnp.zeros_like(dstate_ref)
            small_ref[...] = jnp.zeros_like(small_ref)

        gain = gain_ref[...]
        d_gain = jnp.zeros((1, 128), F32)
        for ci in reversed(range(nc)):
            rows = slice(ci * C, (ci + 1) * C)
            dsn = dstate_ref[...]
            for h in range(DN_HEADS):
                dst_ref[ci * DN_WIDTH + h * HD:ci * DN_WIDTH + (h + 1) * HD, :] = dsn[h]
            ov = _stack_heads(o_ref, rows)
            r = lax.rsqrt(jnp.mean(ov * ov, axis=-1, keepdims=True) + NORM_EPS)
            on = ov * r
            gt = _stack_heads(gate_ref, rows)
            sgt = _sigmoid(gt)
            silu_g = gt * sgt
            dy = _stack_heads(ddn_ref, rows)
            d_gain = d_gain + jnp.sum(jnp.sum(dy * on * silu_g, axis=1, keepdims=True), axis=0)
            dgate = dy * on * gain * (sgt * (1.0 + gt * (1.0 - sgt)))
            don = dy * gain * silu_g
            do = r * (don - on * jnp.mean(don * on, axis=-1, keepdims=True))
            d_vnew = _btn(aq_ref[:, rows, :], do) + _bnn(_stack_heads(kd_ref, rows), dsn)
            egl = jnp.stack([egl_ref[ci * 8 + h:ci * 8 + h + 1, :] for h in range(DN_HEADS)])
            dstate_ref[...] = _btn(_stack_heads(qg_ref, rows), do) + dsn * egl - _btn(_stack_heads(w_ref, rows), d_vnew)
            for h in range(DN_HEADS):
                sl = slice(h * HD, (h + 1) * HD)
                do_ref[rows, sl] = do[h]
                dvn_ref[rows, sl] = d_vnew[h]
                dgate_ref[rows, sl] = dgate[h]
        small_ref[...] += jnp.concatenate([d_gain, jnp.zeros((7, 128), F32)], axis=0)

    nb = N // nc
    tok = lambda wd: pl.BlockSpec((nc * C, wd), lambda i: (nb - 1 - i, 0))
    sq = pl.BlockSpec((DN_HEADS, nc * C, C), lambda i: (0, nb - 1 - i, 0))
    vec = pl.BlockSpec((1, 128), lambda i: (0, 0))
    return pl.pallas_call(
        body, name="dn_scan_bwd", grid=(nb,),
        in_specs=[tok(DN_WIDTH)] * 3 + [sq, pl.BlockSpec((nc * 8, 128), lambda i: (nb - 1 - i, 0)), tok(DN_WIDTH), vec,
                                       tok(DN_WIDTH), tok(DN_WIDTH)],
        out_specs=[tok(DN_WIDTH)] * 3 + [pl.BlockSpec((nc * DN_WIDTH, HD), lambda i: (nb - 1 - i, 0)),
                                        pl.BlockSpec((8, 128), lambda i: (0, 0))],
        out_shape=[jax.ShapeDtypeStruct((S, DN_WIDTH), F32)] * 3 + [jax.ShapeDtypeStruct((N * DN_WIDTH, HD), F32),
                                                                  jax.ShapeDtypeStruct((8, 128), F32)],
        scratch_shapes=[pltpu.VMEM((DN_HEADS, HD, HD), F32)],
        compiler_params=_params(1),
    )(w, qg, kd, aq, egl, gate, dn_gain, o, ddn)


def _dn_post(qn, kn, v, bd, avec, dvec, t_inv, v_new_all, states, dstates, do_all, dvn_all, comm=None):
    S = qn.shape[0]
    C = DN_CHUNK
    N = S // C
    HD = DN_HEAD_DIM
    nc = PREP_CHUNKS
    B = nc * DN_HEADS

    def body(q_ref, k_ref, v_ref, bd_ref, a_ref, d_ref, t_ref, vn_ref, st_ref, dst_ref, do_ref, dvn_ref,
             dq_ref, dk_ref, dv_ref, dbd_ref, small_ref):
        @pl.when(pl.program_id(0) == 0)
        def _():
            small_ref[...] = jnp.zeros_like(small_ref)

        avec = a_ref[...]
        bds = [bd_ref[ci * C:(ci + 1) * C, :] for ci in range(nc)]
        k = _stack_units(k_ref, nc)
        vv = _stack_units(v_ref, nc)
        t = jnp.concatenate([t_ref[:, ci * C:(ci + 1) * C, :] for ci in range(nc)], axis=0)
        c = _dn_common_b(bds, avec, d_ref[...], _stack_units(q_ref, nc), k, vv, t=t)
        q, kb, eg, u, w = c["q"], c["kb"], c["eg"], c["u"], c["w"]
        beta, decay, incl, strict, eye = c["beta"], c["decay"], c["incl"], c["strict"], c["eye"]
        st = jnp.stack([st_ref[b * HD:(b + 1) * HD, :] for b in range(B)])
        dsn = jnp.stack([dst_ref[b * HD:(b + 1) * HD, :] for b in range(B)])
        v_new = _stack_units(vn_ref, nc)
        do = _stack_units(do_ref, nc)
        d_vnew = _stack_units(dvn_ref, nc)
        egl = jnp.exp(c["g_last"])
        daq = jnp.where(incl, _bnt(do, v_new), 0.0)
        d_qg = _bnt(do, st)
        d_kd = _bnt(v_new, dsn)
        d_glast = jnp.sum(jnp.sum(dsn * st, axis=-1, keepdims=True), axis=1, keepdims=True) * egl
        d_w = -_bnt(d_vnew, st)
        d_ru = _btn(t, d_vnew)
        d_rw = _btn(t, d_w)
        da = -jnp.where(strict, _bnt(d_ru, u) + _bnt(d_rw, w), 0.0)
        dv = d_ru * beta
        dbeta = jnp.sum(d_ru * vv, axis=-1, keepdims=True)
        dkb = d_rw * eg
        dgc = jnp.sum(d_rw * c["rhs_w"], axis=-1, keepdims=True)
        dkk = da * decay
        ddecay = da * c["kk"]
        dkb = dkb + _bnn(dkk, k)
        dk = _btn(dkk, kb)
        dqk = daq * decay
        ddecay = ddecay + daq * c["qk"]
        dq = _bnn(dqk, k)
        dk = dk + _btn(dqk, q)
        m = ddecay * decay
        col_sum = jnp.sum(m, axis=1, keepdims=True)
        dgc = dgc + jnp.sum(m, axis=-1, keepdims=True) - jnp.sum(eye * col_sum, axis=-1, keepdims=True)
        dq = dq + d_qg * eg
        dgc = dgc + jnp.sum(d_qg * c["qg"], axis=-1, keepdims=True)
        dk = dk + d_kd * c["ekd"]
        tk = jnp.sum(d_kd * c["kd"], axis=-1, keepdims=True)
        dgc = dgc - tk
        d_glast = d_glast + jnp.sum(tk, axis=1, keepdims=True)
        dk = dk + dkb * beta
        dbeta = dbeta + jnp.sum(dkb * k, axis=-1, keepdims=True)
        dgc = dgc + jnp.where(c["last"], d_glast, 0.0)
        dgc_row = jnp.sum(eye * dgc, axis=1, keepdims=True)
        dgraw = jnp.sum(jnp.where(c["col"] >= c["row"], dgc_row, 0.0), axis=-1, keepdims=True)
        _store_units(dq_ref, dq * (HD ** -0.5), nc)
        _store_units(dk_ref, dk, nc)
        _store_units(dv_ref, dv, nc)
        dbraw = dbeta * beta * (1.0 - beta)
        dzc = dgraw * _sigmoid(c["zc"])
        ga = dgraw * c["graw"]
        lane = c["lane"]
        lane1 = lax.broadcasted_iota(jnp.int32, (1, 128), 1)
        neg_ea = -jnp.exp(avec)
        d_alog = jnp.zeros((1, 128), F32)
        d_dt = jnp.zeros((1, 128), F32)
        for ci in range(nc):
            dbd = jnp.zeros((C, 128), F32)
            for h in range(DN_HEADS):
                b = ci * DN_HEADS + h
                dz = dzc[b] * neg_ea
                dbd = dbd + jnp.where(lane == h, dbraw[b], 0.0) + jnp.where(lane == DN_HEADS + h, dz, 0.0)
                d_alog = d_alog + jnp.where(lane1 == DN_HEADS + h, jnp.sum(ga[b], axis=0, keepdims=True), 0.0)
                d_dt = d_dt + jnp.where(lane1 == DN_HEADS + h, jnp.sum(dz, axis=0, keepdims=True), 0.0)
            dbd_ref[ci * C:(ci + 1) * C, :] = dbd
        small_ref[...] += jnp.concatenate([d_alog, d_dt, jnp.zeros((6, 128), F32)], axis=0)

    tok = lambda wd: pl.BlockSpec((nc * C, wd), lambda n: (n, 0))
    big = pl.BlockSpec((nc * DN_WIDTH, HD), lambda n: (n, 0))
    sq = pl.BlockSpec((DN_HEADS, nc * C, C), lambda n: (0, n, 0))
    vec = pl.BlockSpec((1, 128), lambda n: (0, 0))
    return _call(
        body, (qn, kn, v, bd, avec, dvec, t_inv, v_new_all, states, dstates, do_all, dvn_all),
        name="dn_post", grid=(N // nc,), comm=comm,
        in_specs=[tok(DN_WIDTH)] * 3 + [tok(128), vec, vec, sq, tok(DN_WIDTH), big, big, tok(DN_WIDTH), tok(DN_WIDTH)],
        out_specs=[tok(DN_WIDTH)] * 3 + [tok(128), pl.BlockSpec((8, 128), lambda n: (0, 0))],
        out_shape=[jax.ShapeDtypeStruct((S, DN_WIDTH), F32)] * 3 + [jax.ShapeDtypeStruct((S, 128), F32),
                                                                  jax.ShapeDtypeStruct((8, 128), F32)])


def _outproj_fwd(x, attn, dn, w_out):
    S, D = x.shape
    tm = 512

    def body(x_ref, a_ref, d_ref, w_ref, xo_ref, mix_ref):
        a = a_ref[...].astype(BF16)
        dd = d_ref[...].astype(BF16)
        mix_ref[:, 0:ATTN_WIDTH] = a
        mix_ref[:, ATTN_WIDTH:] = dd
        xo_ref[...] = x_ref[...] + _nn(a, w_ref[0:ATTN_WIDTH, :]) + _nn(dd, w_ref[ATTN_WIDTH:, :])

    tok = lambda w: pl.BlockSpec((tm, w), lambda i: (i, 0))
    return pl.pallas_call(
        body, name="outproj_fwd", grid=(S // tm,),
        in_specs=[tok(D), tok(ATTN_WIDTH), tok(DN_WIDTH), pl.BlockSpec((D, D), lambda i: (0, 0))],
        out_specs=[tok(D), tok(D)],
        out_shape=[jax.ShapeDtypeStruct((S, D), F32), jax.ShapeDtypeStruct((S, D), BF16)],
        compiler_params=_params(1),
    )(x, attn, dn, w_out)


def _outproj_bwd(dx, w_out, attn):
    S, D = dx.shape
    tm = VIEW_TILE

    def body(dx_ref, w_ref, attn_ref, da1, da4, da16, dl1, dl4, dl16, ddn_ref, dxb_ref, planes):
        d = dx_ref[...].astype(BF16)
        dxb_ref[...] = d
        da = _nt(d, w_ref[0:ATTN_WIDTH, :])
        ddn_ref[...] = _nt(d, w_ref[ATTN_WIDTH:, :])
        _tile_to_views(da, planes, (da1, da4, da16))
        lo = lax.broadcasted_iota(jnp.int32, (tm, 128), 1) < 64
        cols = []
        for G in range(4):
            sl = slice(G * 128, (G + 1) * 128)
            t = da[:, sl] * attn_ref[:, sl]
            d0 = jnp.sum(jnp.where(lo, t, 0.0), axis=-1, keepdims=True)
            d1 = jnp.sum(jnp.where(lo, 0.0, t), axis=-1, keepdims=True)
            cols.append(jnp.where(lo, d0, d1))
        _tile_to_views(jnp.concatenate(cols, axis=1), planes, (dl1, dl4, dl16))

    tok = lambda w: pl.BlockSpec((tm, w), lambda i: (i, 0))
    views = [_view_spec(d) for d in DILATIONS]
    return pl.pallas_call(
        body, name="outproj_bwd", grid=(S // tm,),
        in_specs=[tok(D), pl.BlockSpec((D, D), lambda i: (0, 0)), tok(ATTN_WIDTH)],
        out_specs=views + views + [tok(DN_WIDTH), tok(D)],
        out_shape=[_view_shape(S, d, F32) for d in DILATIONS] * 2
                  + [jax.ShapeDtypeStruct((S, DN_WIDTH), F32), jax.ShapeDtypeStruct((S, D), BF16)],
        scratch_shapes=[pltpu.VMEM((4, tm, 128), F32)],
        compiler_params=_params(1),
    )(dx, w_out, attn)


def _loss_head(x, gain, target):
    S, D = x.shape
    tm = 512

    def body(x_ref, gain_ref, t_ref, loss_ref, dx_ref, dgain_ref):
        @pl.when(pl.program_id(0) == 0)
        def _():
            loss_ref[...] = jnp.zeros_like(loss_ref)
            dgain_ref[...] = jnp.zeros_like(dgain_ref)

        xf = x_ref[...]
        gain = gain_ref[...]
        r = lax.rsqrt(jnp.mean(xf * xf, axis=-1, keepdims=True) + NORM_EPS)
        xhat = xf * r
        err = xhat * gain - t_ref[...]
        part = 0.5 * jnp.sum(jnp.mean(err * err, axis=-1, keepdims=True), axis=0, keepdims=True)
        first = (lax.broadcasted_iota(jnp.int32, (8, 128), 0) == 0) & (lax.broadcasted_iota(jnp.int32, (8, 128), 1) == 0)
        loss_ref[...] += jnp.where(first, part, 0.0)
        dy = err * (1.0 / D)
        dgain_ref[...] += jnp.sum(dy * xhat, axis=0, keepdims=True)
        dxh = dy * gain
        dx_ref[...] = r * (dxh - xhat * jnp.mean(dxh * xhat, axis=-1, keepdims=True))

    tok = pl.BlockSpec((tm, D), lambda i: (i, 0))
    row = pl.BlockSpec((1, D), lambda i: (0, 0))
    return pl.pallas_call(
        body, name="loss_head", grid=(S // tm,),
        in_specs=[tok, row, tok],
        out_specs=[pl.BlockSpec((8, 128), lambda i: (0, 0)), tok, row],
        out_shape=[jax.ShapeDtypeStruct((8, 128), F32), jax.ShapeDtypeStruct((S, D), F32),
                   jax.ShapeDtypeStruct((1, D), F32)],
        compiler_params=_params(1),
    )(x, gain, target)


def _adamw(w, g, m, v, name):
    R, Ccols = w.shape
    tr = R
    for cand in (256, 128, 64, 32, 16, 8):
        if R % cand == 0:
            tr = cand
            break
    c1 = 1.0 - ADAM_B1 ** ADAM_STEP
    c2 = 1.0 - ADAM_B2 ** ADAM_STEP

    def body(w_ref, g_ref, m_ref, v_ref, d_ref, nm_ref, nv_ref):
        gv = g_ref[...]
        mn = ADAM_B1 * m_ref[...] + (1.0 - ADAM_B1) * gv
        vn = ADAM_B2 * v_ref[...] + (1.0 - ADAM_B2) * (gv * gv)
        nm_ref[...] = mn
        nv_ref[...] = vn
        d_ref[...] = -ADAM_LR * ((mn / c1) / (jnp.sqrt(vn / c2) + ADAM_EPS) + ADAM_WD * w_ref[...])

    spec = pl.BlockSpec((tr, Ccols), lambda i: (i, 0))
    return pl.pallas_call(
        body, name=name, grid=(R // tr,), in_specs=[spec] * 4, out_specs=[spec] * 3,
        out_shape=[jax.ShapeDtypeStruct((R, Ccols), F32)] * 3, compiler_params=_params(1),
    )(w, g, m, v)


LATE_WEIGHTS = ("w_out", "ffn2_gate", "ffn2_up", "ffn2_down")


def _local_step(x, target, wts, small, dist=None):
    g1, g2, gm, gf = small["norm_ffn1"], small["norm_ffn2"], small["norm_mix"], small["norm_final"]
    wts = dict(wts)

    def reduce_start(gs, tag):
        return _rs_add_pairs(gs, _swap_sibling(gs, True, "rs_swap_halves_" + tag), dist["c"], "rs_add_pairs_" + tag)

    (x1, h1, fg1, fu1), late = _ffn_fwd(x, g1, wts["ffn1_gate"], wts["ffn1_up"], wts["ffn1_down"], "ffn1_fwd",
                                        comm=_ag_comm(dist["late"]) if dist else None)
    if dist:
        wts.update(zip(LATE_WEIGHTS, late))
        wts["w_out"] = wts["w_out"].reshape(D_MODEL, D_MODEL)
    h2, *qkv, xq, xk, xv, gate, bd = _inproj_fwd(x1, gm, wts["w_in"])
    aq, ak, av = qkv[0:3], qkv[3:6], qkv[6:9]
    parts = [_attn_fwd(aq[p], ak[p], av[p], d, f"attn_fwd_d{d}") for p, d in enumerate(DILATIONS)]
    attn, *lse = _attn_merge(parts)
    conv_w = small["conv_w"]
    qn, kn, vv = _conv_fwd(xq, xk, xv, conv_w)
    dn_u, dn_w, dn_qg, dn_kd, dn_aq, dn_t, dn_egl = _dn_prep(qn, kn, vv, bd, small["avec"], small["dvec"])
    dn, o_dn, v_new, states = _dn_scan_fwd(dn_u, dn_w, dn_qg, dn_kd, dn_aq, dn_egl, gate, small["dn_norm"])
    x2, mix = _outproj_fwd(x1, attn, dn, wts["w_out"])
    (x3, h3, fg2, fu2), _ = _ffn_fwd(x2, g2, wts["ffn2_gate"], wts["ffn2_up"], wts["ffn2_down"], "ffn2_fwd")
    loss, dx3, d_gf = _loss_head(x3, gf, target)

    grads = {}
    (dx2, d_g2, dfg2, dfu2, act2, dout2), _ = _ffn_bwd(dx3, x2, g2, fg2, fu2, wts["ffn2_down"], wts["ffn2_gate"],
                                                      wts["ffn2_up"], "ffn2_bwd")
    tk = 2048
    grads["ffn2_gate"], _ = _dw_chunks(dfg2, h3, tk, "dw_ffn2_gate")
    grads["ffn2_up"], _ = _dw_chunks(dfu2, h3, tk, "dw_ffn2_up")
    grads["ffn2_down"], _ = _dw_chunks(act2, dout2, tk, "dw_ffn2_down")
    group_a = ("ffn2_gate", "ffn2_up", "ffn2_down")
    parts_a = reduce_start([grads[n] for n in group_a], "a") if dist else None

    *dviews, ddn, dx2b = _outproj_bwd(dx2, wts["w_out"], attn)
    dattn, dd = dviews[0:3], dviews[3:6]
    grads["w_out"] = _matmul_tn(mix, dx2b, D_MODEL, tk, "dw_out").reshape(N_CHIPS, D_MODEL // N_CHIPS, D_MODEL)

    daq, dak, dav = [], [], []
    for p, d in enumerate(DILATIONS):
        daq.append(_attn_bwd_q(aq[p], ak[p], av[p], dattn[p], lse[p], dd[p], d, f"attn_bwd_q_d{d}"))
        dk_p, dv_p = _attn_bwd_kv(aq[p], ak[p], av[p], dattn[p], lse[p], dd[p], d, f"attn_bwd_kv_d{d}")
        dak.append(dk_p)
        dav.append(dv_p)

    do_dn, dvn, dgate, dstates, d_dn_gain = _dn_scan_bwd(dn_w, dn_qg, dn_kd, dn_aq, dn_egl, gate, small["dn_norm"], o_dn, ddn)
    (dqn, dkn, dvv, dbd, dn_small), recv_a = _dn_post(qn, kn, vv, bd, small["avec"], small["dvec"], dn_t, v_new, states,
                                                      dstates, do_dn, dvn, comm=_rsx_comm(parts_a) if dist else None)
    dcq, dck, dcv, dwq, dwk, dwv = _conv_bwd_pre(xq, xk, xv, conv_w, dqn, dkn, dvv)
    dxq, dxk, dxv = _conv_bwd_x(dcq, dck, dcv, conv_w)
    d_conv = jnp.concatenate([dwq[:CONV_WIDTH], dwk[:CONV_WIDTH], dwv[:CONV_WIDTH]], axis=1)

    dx1, d_gm, dproj = _inproj_bwd(dx2, x1, gm, [daq, dak, dav], [dxq, dxk, dxv, dgate], dbd, wts["w_in"])
    gi = _matmul_tn(dproj, h2, IN_COLS_PADDED, 512, "dw_in")
    if dist:
        gi = jnp.concatenate([gi[:3072], gi[3584:3592], gi[3072:3584]], axis=0).reshape(N_CHIPS, IN_COLS // N_CHIPS, D_MODEL)
        gi = jnp.pad(gi, ((0, 0), (0, W_IN_ROWS - IN_COLS // N_CHIPS), (0, 0)))
    grads["w_in"] = gi
    group_b = ("w_in", "w_out")
    parts_b = reduce_start([grads[n] for n in group_b], "b") if dist else None

    (dx0, d_g1, dfg1, dfu1, act1, dout1), _ = _ffn_bwd(dx1, x, g1, fg1, fu1, wts["ffn1_down"], wts["ffn1_gate"],
                                                      wts["ffn1_up"], "ffn1_bwd")
    group_c = ("ffn1_gate", "ffn1_up", "ffn1_down")
    pending = parts_b if dist else []
    parts_c, recv_bc = [], []
    for n, (lhs, rhs) in zip(group_c, ((dfg1, h1), (dfu1, h1), (act1, dout1))):
        grads[n], landed = _dw_chunks(lhs, rhs, tk, "dw_" + n, comm=_rsx_comm(pending) if dist else None)
        recv_bc += list(landed)
        if dist:
            pending = reduce_start([grads[n]], n)
            parts_c += pending

    small_grads = dict(norm_ffn1=d_g1, norm_mix=d_gm, norm_ffn2=d_g2, norm_final=d_gf, conv_w=d_conv,
                       a_log=dn_small[0:1], dt_bias=dn_small[1:2], dn_norm=d_dn_gain[0:1])
    if dist:
        recv_bc += _rs_exchange_arrays(pending)
        recv_b, recv_c = recv_bc[:len(parts_b)], recv_bc[len(parts_b):]
        names = group_a + group_b + group_c
        totals = _rs_add_totals(list(parts_a) + list(parts_b) + list(parts_c), list(recv_a) + list(recv_b) + list(recv_c),
                                dist["chip"])
        theirs = _swap_sibling(totals, False, "rs_share_total")
        grads = {n: (mine, other) for n, mine, other in zip(names, totals, theirs)}
    return loss, dx0, grads, small_grads


HBM =pl.BlockSpec(memory_space=pl.ANY)
VMEM_SPEC = pl.BlockSpec(memory_space=pltpu.VMEM)


def _coords():
    return lax.axis_index("x"), lax.axis_index("y"), lax.axis_index("c")


def _remote(src, dst, send_sems, recv_sems, k, dev):
    return pltpu.make_async_remote_copy(src_ref=src, dst_ref=dst, send_sem=send_sems.at[k], recv_sem=recv_sems.at[k],
                                        device_id=dev, device_id_type=MESH)


def _allreduce_small(buf, name):
    R, Cc = buf.shape

    def body(src_ref, out_ref, recv_ref, send_sems, recv_sems):
        x, y, c = _coords()
        copies = []
        for m in range(1, 8):
            fx, fy, fc = (m >> 2) & 1, (m >> 1) & 1, m & 1
            dev = (x ^ fx if fx else x, y ^ fy if fy else y, c ^ fc if fc else c)
            cp = _remote(src_ref, recv_ref.at[m - 1], send_sems, recv_sems, m - 1, dev)
            cp.start()
            copies.append(cp)
        for cp in copies:
            cp.wait()
        r = [src_ref[...]] + [recv_ref[m] for m in range(7)]
        out_ref[...] = ((r[0] + r[1]) + (r[2] + r[3])) + ((r[4] + r[5]) + (r[6] + r[7]))

    return pl.pallas_call(
        body, name=name, out_shape=jax.ShapeDtypeStruct((R, Cc), F32),
        in_specs=[VMEM_SPEC], out_specs=VMEM_SPEC,
        scratch_shapes=[pltpu.VMEM((7, R, Cc), F32), pltpu.SemaphoreType.DMA((7,)), pltpu.SemaphoreType.DMA((7,))],
    )(buf)


BIG = ("ffn1_gate", "ffn1_up", "ffn1_down", "w_in", "w_out", "ffn2_gate", "ffn2_up", "ffn2_down")
W_IN_ROWS = 960


def _rows(ref, start, size):
    return ref.at[pl.ds(pl.multiple_of(start, 16), size)]


def _allgather_arrays(shards):
    n = len(shards)

    def body(*refs):
        _ag_start(refs[:n], refs[n:2 * n], refs[2 * n], refs[2 * n + 1])
        _ag_finish(refs[:n], refs[n:2 * n], refs[2 * n], refs[2 * n + 1])

    _, shapes, n_sems, _, _ = _ag_comm(shards)
    return pl.pallas_call(
        body, name="allgather_weights", out_shape=shapes, in_specs=[HBM] * n, out_specs=[HBM] * n,
        scratch_shapes=[pltpu.SemaphoreType.DMA((n_sems,)), pltpu.SemaphoreType.DMA((n_sems,))],
    )(*shards)


def _ag_copies(srcs, outs, send_sems, recv_sems):
    x, y, c = _coords()
    sib = (x, y, 1 - c)
    me = 2 * x + y
    others = [(1 - x, y), (x, 1 - y), (1 - x, 1 - y)]
    plan = []
    for a, (src, out) in enumerate(zip(srcs, outs)):
        h = src.shape[0] // 2
        cp = lambda s, d, k, dev: _remote(s, d, send_sems, recv_sems, 7 * a + k, dev)
        own = cp(src, out.at[me], 6, sib)
        sends = [cp(_rows(src, c * h, h), _rows(out.at[me], c * h, h), j, (ox, oy, c)) for j, (ox, oy) in enumerate(others)]
        mine = [_rows(out.at[2 * ox + oy], c * h, h) for ox, oy in others]
        theirs = [_rows(out.at[2 * ox + oy], (1 - c) * h, h) for ox, oy in others]
        arrivals = [cp(m, m, j, sib) for j, m in enumerate(mine)]
        forwards = [cp(m, m, 3 + j, sib) for j, m in enumerate(mine)]
        forwarded = [cp(t, t, 3 + j, sib) for j, t in enumerate(theirs)]
        plan.append((own, sends, forwards, arrivals, forwarded))
    return plan


def _ag_start(srcs, outs, send_sems, recv_sems):
    for own, sends, _, _, _ in _ag_copies(srcs, outs, send_sems, recv_sems):
        own.start()
        for cp in sends:
            cp.start()


def _ag_finish(srcs, outs, send_sems, recv_sems):
    plan = _ag_copies(srcs, outs, send_sems, recv_sems)
    for _, _, forwards, arrivals, _ in plan:
        for arrived, fwd in zip(arrivals, forwards):
            arrived.wait_recv()
            fwd.start()
    for own, sends, forwards, _, forwarded in plan:
        for cp in forwarded:
            cp.wait_recv()
        own.wait_recv()
        for cp in [own] + sends + forwards:
            cp.wait_send()


def _ag_comm(shards):
    shapes = [jax.ShapeDtypeStruct((N_CHIPS,) + s.shape, s.dtype) for s in shards]
    return (list(shards), shapes, 7 * len(shards), _ag_start, _ag_finish)


def _swap_sibling(arrs, pick_other_half, name):
    n = len(arrs)
    outs = [jax.ShapeDtypeStruct((a.shape[0], a.shape[1] // 2) + a.shape[2:] if pick_other_half else a.shape, a.dtype) for a in arrs]

    def body(*refs):
        srcs, dsts, send_sems, recv_sems = refs[:n], refs[n:2 * n], refs[2 * n], refs[2 * n + 1]
        x, y, c = _coords()
        cps = []
        for a in range(n):
            src = srcs[a]
            if pick_other_half:
                h = src.shape[1] // 2
                src = src.at[:, pl.ds(pl.multiple_of((1 - c) * h, 16), h)]
            cp = _remote(src, dsts[a], send_sems, recv_sems, a, (x, y, 1 - c))
            cp.start()
            cps.append(cp)
        for cp in cps:
            cp.wait()

    return pl.pallas_call(
        body, name=name, out_shape=outs, in_specs=[HBM] * n, out_specs=[HBM] * n,
        scratch_shapes=[pltpu.SemaphoreType.DMA((n,)), pltpu.SemaphoreType.DMA((n,))],
    )(*arrs)


def _rs_add_pairs(gs, others, c, name):
    n = len(gs)
    hbs = [g.shape[1] // 4 for g in gs]

    def body(c_ref, *refs):
        for a in range(n):
            refs[2 * n + a][...] = (refs[a][...] + refs[n + a][...]).astype(BF16)

    mine = lambda hb: pl.BlockSpec((None, hb, D_MODEL), lambda j, s, c_ref: (j, c_ref[0] * 2 + s, 0))
    flat = lambda hb: pl.BlockSpec((None, hb, D_MODEL), lambda j, s, c_ref: (j, s, 0))
    return pl.pallas_call(
        body, name=name,
        grid_spec=pltpu.PrefetchScalarGridSpec(
            num_scalar_prefetch=1, grid=(N_CHIPS, 2),
            in_specs=[mine(hb) for hb in hbs] + [flat(hb) for hb in hbs],
            out_specs=[flat(hb) for hb in hbs]),
        out_shape=[jax.ShapeDtypeStruct(o.shape, BF16) for o in others],
        compiler_params=_params(2),
    )(c, *gs, *others)


def _rs_exchange_arrays(parts):
    n = len(parts)

    def body(*refs):
        _rsx_start(refs[:n], refs[n:2 * n], refs[2 * n], refs[2 * n + 1])
        _rsx_finish(refs[:n], refs[n:2 * n], refs[2 * n], refs[2 * n + 1])

    _, shapes, n_sems, _, _ = _rsx_comm(parts)
    return pl.pallas_call(
        body, name="rs_exchange_chips", out_shape=shapes, in_specs=[HBM] * n, out_specs=[HBM] * n,
        scratch_shapes=[pltpu.SemaphoreType.DMA((n_sems,)), pltpu.SemaphoreType.DMA((n_sems,))],
    )(*parts)


def _rsx_copies(srcs, dsts, send_sems, recv_sems):
    x, y, c = _coords()
    others = [(1 - x, y), (x, 1 - y), (1 - x, 1 - y)]
    return [_remote(src.at[2 * ox + oy], dst.at[k], send_sems, recv_sems, 3 * a + k, (ox, oy, c))
            for a, (src, dst) in enumerate(zip(srcs, dsts)) for k, (ox, oy) in enumerate(others)]


def _rsx_start(srcs, dsts, send_sems, recv_sems):
    for cp in _rsx_copies(srcs, dsts, send_sems, recv_sems):
        cp.start()


def _rsx_finish(srcs, dsts, send_sems, recv_sems):
    for cp in _rsx_copies(srcs, dsts, send_sems, recv_sems):
        cp.wait()


def _rsx_comm(parts):
    shapes = [jax.ShapeDtypeStruct((3,) + p.shape[1:], p.dtype) for p in parts]
    return (list(parts), shapes, 3 * len(parts), _rsx_start, _rsx_finish)


def _rs_add_totals(parts, recvs, chip):
    n = len(parts)
    hbs = [p.shape[1] // 2 for p in parts]

    def body(chip_ref, *refs):
        f = lambda r: r[...].astype(F32)
        for a in range(n):
            p, r0, r1, r2 = refs[a], refs[n + 3 * a], refs[n + 3 * a + 1], refs[n + 3 * a + 2]
            refs[4 * n + a][...] = (f(p) + f(r0)) + (f(r1) + f(r2))

    own = lambda hb: pl.BlockSpec((None, hb, D_MODEL), lambda s, chip_ref: (chip_ref[0], s, 0))
    slot = lambda hb, k: pl.BlockSpec((None, hb, D_MODEL), lambda s, chip_ref, k=k: (k, s, 0))
    recv_specs = [slot(hb, k) for hb in hbs for k in range(3)]
    recv_args = [r for r in recvs for _ in range(3)]
    return pl.pallas_call(
        body, name="rs_add_totals",
        grid_spec=pltpu.PrefetchScalarGridSpec(
            num_scalar_prefetch=1, grid=(2,),
            in_specs=[own(hb) for hb in hbs] + recv_specs,
            out_specs=[pl.BlockSpec((hb, D_MODEL), lambda s, chip_ref: (s, 0)) for hb in hbs]),
        out_shape=[jax.ShapeDtypeStruct(p.shape[1:], F32) for p in parts],
        compiler_params=_params(1),
    )(chip, *parts, *recv_args)


def _permute_w_in(w):
    return jnp.concatenate([w[:, :3072], w[:, 3080:IN_COLS], w[:, 3072:3080],
                            jnp.zeros((w.shape[0], IN_COLS_PADDED - IN_COLS), w.dtype)], axis=1)


ROW_SHARDED = ("ffn1_down", "w_out", "ffn2_down")


def _pad_row(v):
    v = v.reshape(1, -1)
    return jnp.pad(v, ((0, 0), (0, D_MODEL - v.shape[1])))


def kernel(x, norm_ffn1, ffn1_gate, ffn1_up, ffn1_down, norm_mix, w_in, conv_w, a_log, dt_bias, dn_norm, w_out, norm_ffn2, ffn2_gate, ffn2_up, ffn2_down, norm_final, loss_target, m_norm_ffn1, m_ffn1_gate, m_ffn1_up, m_ffn1_down, m_norm_mix, m_w_in, m_conv_w, m_a_log, m_dt_bias, m_dn_norm, m_w_out, m_norm_ffn2, m_ffn2_gate, m_ffn2_up, m_ffn2_down, m_norm_final, v_norm_ffn1, v_ffn1_gate, v_ffn1_up, v_ffn1_down, v_norm_mix, v_w_in, v_conv_w, v_a_log, v_dt_bias, v_dn_norm, v_w_out, v_norm_ffn2, v_ffn2_gate, v_ffn2_up, v_ffn2_down, v_norm_final):
    cx, cy, cc = _coords()
    chip = 2 * cx + cy
    big_w = dict(ffn1_gate=ffn1_gate[0], ffn1_up=ffn1_up[0], ffn1_down=ffn1_down[0], w_in=w_in[0], w_out=w_out[0],
                 ffn2_gate=ffn2_gate[0], ffn2_up=ffn2_up[0], ffn2_down=ffn2_down[0])
    big_m = dict(ffn1_gate=m_ffn1_gate[0], ffn1_up=m_ffn1_up[0], ffn1_down=m_ffn1_down[0], w_in=m_w_in[0], w_out=m_w_out[0],
                 ffn2_gate=m_ffn2_gate[0], ffn2_up=m_ffn2_up[0], ffn2_down=m_ffn2_down[0])
    big_v = dict(ffn1_gate=v_ffn1_gate[0], ffn1_up=v_ffn1_up[0], ffn1_down=v_ffn1_down[0], w_in=v_w_in[0], w_out=v_w_out[0],
                 ffn2_gate=v_ffn2_gate[0], ffn2_up=v_ffn2_up[0], ffn2_down=v_ffn2_down[0])

    early = tuple(n for n in BIG if n not in LATE_WEIGHTS)
    wts = dict(zip(early, _allgather_arrays([big_w[n].astype(BF16) for n in early])))
    wts["w_in"] = _permute_w_in(jnp.concatenate([wts["w_in"][j] for j in range(N_CHIPS)], axis=1))
    dist = dict(late=[big_w[n].astype(BF16) for n in LATE_WEIGHTS], c=cc.reshape(1).astype(jnp.int32),
                chip=chip.reshape(1).astype(jnp.int32))

    conv_shard = conv_w[0]
    emb = jnp.concatenate([jnp.where((chip == j) & (cc == 0), conv_shard, 0.0) for j in range(N_CHIPS)], axis=1)
    emb = jnp.pad(emb.reshape(6, D_MODEL), ((0, 2), (0, 0)))
    conv_full = _allreduce_small(emb, "allgather_conv_w")[:6].reshape(CONV_WIDTH, 3 * DN_WIDTH)

    zvec = jnp.zeros((1, 128), F32)
    small = dict(norm_ffn1=norm_ffn1, norm_mix=norm_mix, norm_ffn2=norm_ffn2, norm_final=norm_final[None],
                 conv_w=conv_full, avec=zvec.at[0, DN_HEADS:2 * DN_HEADS].set(a_log[0]),
                 dvec=zvec.at[0, DN_HEADS:2 * DN_HEADS].set(dt_bias[0]), dn_norm=dn_norm)

    loss, grad_x, reduced, sg = _local_step(x[0], loss_target[0], wts, small, dist)

    rows = [sg["norm_ffn1"], sg["norm_mix"], sg["norm_ffn2"], sg["norm_final"], _pad_row(sg["a_log"]), _pad_row(sg["dt_bias"]),
            _pad_row(sg["dn_norm"]), _pad_row(loss[0:1]), sg["conv_w"].reshape(6, D_MODEL), jnp.zeros((2, D_MODEL), F32)]
    red = _allreduce_small(jnp.concatenate(rows, axis=0), "allreduce_small")
    loss_out = red[7, 0]
    g_conv_full = red[8:14].reshape(CONV_WIDTH, 3 * DN_WIDTH)
    g_conv = lax.dynamic_slice_in_dim(g_conv_full, chip * (3 * DN_WIDTH // N_CHIPS), 3 * DN_WIDTH // N_CHIPS, axis=1)
    g_small = dict(norm_ffn1=red[0:1], norm_mix=red[1:2], norm_ffn2=red[2:3], norm_final=red[3],
                   a_log=red[4:5, DN_HEADS:2 * DN_HEADS], dt_bias=red[5:6, DN_HEADS:2 * DN_HEADS], dn_norm=red[6:7, :DN_HEAD_DIM])

    shard_g = {}
    for n in BIG:
        mine, other = reduced[n]
        full = jnp.where(cc == 0, jnp.concatenate([mine, other], axis=0), jnp.concatenate([other, mine], axis=0))
        if n == "w_in":
            full = full[:IN_COLS // N_CHIPS]
        shard_g[n] = full if n in ROW_SHARDED else full.T

    out_g, out_d, out_m, out_v = {}, {}, {}, {}
    for n in BIG:
        d, nm, nv = _adamw(big_w[n], shard_g[n], big_m[n], big_v[n], "adamw_" + n)
        out_g[n], out_d[n], out_m[n], out_v[n] = shard_g[n][None], d[None], nm[None], nv[None]
    d, nm, nv = _adamw(conv_w[0], g_conv, m_conv_w[0], v_conv_w[0], "adamw_conv_w")
    out_g["conv_w"], out_d["conv_w"], out_m["conv_w"], out_v["conv_w"] = g_conv[None], d[None], nm[None], nv[None]

    small_names = ("norm_ffn1", "norm_mix", "norm_ffn2", "norm_final", "a_log", "dt_bias", "dn_norm")
    small_w = dict(norm_ffn1=norm_ffn1, norm_mix=norm_mix, norm_ffn2=norm_ffn2, norm_final=norm_final, a_log=a_log,
                   dt_bias=dt_bias, dn_norm=dn_norm)
    small_m = dict(norm_ffn1=m_norm_ffn1, norm_mix=m_norm_mix, norm_ffn2=m_norm_ffn2, norm_final=m_norm_final, a_log=m_a_log,
                   dt_bias=m_dt_bias, dn_norm=m_dn_norm)
    small_v = dict(norm_ffn1=v_norm_ffn1, norm_mix=v_norm_mix, norm_ffn2=v_norm_ffn2, norm_final=v_norm_final, a_log=v_a_log,
                   dt_bias=v_dt_bias, dn_norm=v_dn_norm)
    stack = lambda dct: jnp.concatenate([_pad_row(dct[n]) for n in small_names] + [jnp.zeros((1, D_MODEL), F32)], axis=0)
    d, nm, nv = _adamw(stack(small_w), stack(g_small), stack(small_m), stack(small_v), "adamw_small")
    for k, n in enumerate(small_names):
        shape = small_w[n].shape
        size = math.prod(shape)
        out_g[n] = g_small[n].reshape(shape)
        out_d[n], out_m[n], out_v[n] = (t[k, :size].reshape(shape) for t in (d, nm, nv))

    order = ("norm_ffn1", "ffn1_gate", "ffn1_up", "ffn1_down", "norm_mix", "w_in", "conv_w", "a_log", "dt_bias", "dn_norm",
             "w_out", "norm_ffn2", "ffn2_gate", "ffn2_up", "ffn2_down", "norm_final")
    return (loss_out, grad_x[None], *[out_g[n] for n in order], *[out_d[n] for n in order],
            *[out_m[n] for n in order], *[out_v[n] for n in order])
```

```python
import functools
import math

import jax
import jax.numpy as jnp
from jax import lax
from jax.experimental import pallas as pl
from jax.experimental.pallas import tpu as pltpu

F32 = jnp.float32
BF16 = jnp.bfloat16
HI = lax.Precision.HIGH

D_MODEL = 1024
ATTN_HEADS = 8
ATTN_WIDTH = 512
ATTN_BLOCK = 128
DILATIONS = (1, 4, 16)
DN_HEADS = 4
DN_HEAD_DIM = 128
DN_WIDTH = 512
DN_CHUNK = 64
CONV_WIDTH = 4
NORM_EPS = 1e-6
L2_EPS = 1e-6
IN_COLS = 3592
IN_COLS_PADDED = 3712
N_CHIPS = 4

ADAM_LR = 0.001
ADAM_B1 = 0.9
ADAM_B2 = 0.999
ADAM_EPS = 1e-08
ADAM_WD = 0.01
ADAM_STEP = 10

VMEM_LIMIT = 56 * 1024 * 1024
NEG_BIG = -1e30
MESH = pl.DeviceIdType.MESH


def _params(n_grid, vmem=VMEM_LIMIT):
    return pltpu.CompilerParams(dimension_semantics=("arbitrary",) * n_grid, vmem_limit_bytes=vmem)


def _call(body, args, *, name, grid, in_specs, out_specs, out_shape, scratch_shapes=(), comm=None):
    n_in, n_out, n_scr = len(in_specs), len(out_specs), len(scratch_shapes)
    hbm = pl.BlockSpec(memory_space=pl.ANY)
    srcs, dst_shapes, n_sems, start, finish = comm if comm is not None else ((), (), 0, None, None)
    ns, nd = len(srcs), len(dst_shapes)

    def full(*refs):
        ins, c_src = refs[:n_in], refs[n_in:n_in + ns]
        at = n_in + ns
        outs, c_dst = refs[at:at + n_out], refs[at + n_out:at + n_out + nd]
        scr = refs[at + n_out + nd:at + n_out + nd + n_scr]
        if comm is not None:
            ids = [pl.program_id(a) for a in range(len(grid))]
            first = functools.reduce(jnp.logical_and, [i == 0 for i in ids])
            last = functools.reduce(jnp.logical_and, [i == g - 1 for i, g in zip(ids, grid)])

            @pl.when(first)
            def _():
                start(c_src, c_dst, refs[-2], refs[-1])

        body(*ins, *outs, *scr)
        if comm is not None:
            @pl.when(last)
            def _():
                finish(c_src, c_dst, refs[-2], refs[-1])

    sems = [pltpu.SemaphoreType.DMA((n_sems,)), pltpu.SemaphoreType.DMA((n_sems,))] if comm is not None else []
    res = pl.pallas_call(
        full, name=name, grid=grid, in_specs=list(in_specs) + [hbm] * ns, out_specs=list(out_specs) + [hbm] * nd,
        out_shape=list(out_shape) + list(dst_shapes), scratch_shapes=list(scratch_shapes) + sems,
        compiler_params=_params(len(grid)),
    )(*args, *srcs)
    return res[:n_out], res[n_out:]


def _nt(a, b, precision=None):
    return lax.dot_general(a, b, (((1,), (1,)), ((), ())), preferred_element_type=F32, precision=precision)


def _tn(a, b, precision=None):
    return lax.dot_general(a, b, (((0,), (0,)), ((), ())), preferred_element_type=F32, precision=precision)


def _nn(a, b, precision=None):
    return jnp.dot(a, b, preferred_element_type=F32, precision=precision)


def _sigmoid(x):
    return 1.0 / (1.0 + jnp.exp(-x))


def _ffn_fwd(x, gain, wg, wu, wd, name, comm=None):
    S, D = x.shape
    nf, _, tf = wg.shape
    tm = 512

    def body(x_ref, gain_ref, wg_ref, wu_ref, wd_ref, xo_ref, h_ref, g_ref, u_ref, acc_ref, hs_ref):
        j = pl.program_id(1)

        @pl.when(j == 0)
        def _():
            xf = x_ref[...]
            r = lax.rsqrt(jnp.mean(xf * xf, axis=-1, keepdims=True) + NORM_EPS)
            h = (xf * r * gain_ref[...]).astype(BF16)
            hs_ref[...] = h
            h_ref[...] = h
            acc_ref[...] = jnp.zeros_like(acc_ref)

        h = hs_ref[...]
        g = _nn(h, wg_ref[...])
        u = _nn(h, wu_ref[...])
        g_ref[...] = g.astype(BF16)
        u_ref[...] = u.astype(BF16)
        act = g * _sigmoid(g) * u
        acc_ref[...] += _nn(act.astype(BF16), wd_ref[...])

        @pl.when(j == nf - 1)
        def _():
            xo_ref[...] = x_ref[...] + 0.5 * acc_ref[...]

    return _call(
        body, (x, gain, wg, wu, wd), name=name, grid=(S // tm, nf), comm=comm,
        in_specs=[pl.BlockSpec((tm, D), lambda i, j: (i, 0)),
                  pl.BlockSpec((1, D), lambda i, j: (0, 0)),
                  pl.BlockSpec((None, D, tf), lambda i, j: (j, 0, 0)),
                  pl.BlockSpec((None, D, tf), lambda i, j: (j, 0, 0)),
                  pl.BlockSpec((None, tf, D), lambda i, j: (j, 0, 0))],
        out_specs=[pl.BlockSpec((tm, D), lambda i, j: (i, 0)),
                   pl.BlockSpec((tm, D), lambda i, j: (i, 0)),
                   pl.BlockSpec((None, tm, tf), lambda i, j: (j, i, 0)),
                   pl.BlockSpec((None, tm, tf), lambda i, j: (j, i, 0))],
        out_shape=[jax.ShapeDtypeStruct((S, D), F32), jax.ShapeDtypeStruct((S, D), BF16),
                   jax.ShapeDtypeStruct((nf, S, tf), BF16), jax.ShapeDtypeStruct((nf, S, tf), BF16)],
        scratch_shapes=[pltpu.VMEM((tm, D), F32), pltpu.VMEM((tm, D), BF16)])


def _rmsnorm_bwd(dh, xf, gain):
    r = lax.rsqrt(jnp.mean(xf * xf, axis=-1, keepdims=True) + NORM_EPS)
    xhat = xf * r
    dgain = jnp.sum(dh * xhat, axis=0, keepdims=True)
    dxh = dh * gain
    dx = r * (dxh - xhat * jnp.mean(dxh * xhat, axis=-1, keepdims=True))
    return dx, dgain


def _ffn_bwd(dxo, x, gain, g, u, wd, wg, wu, name, comm=None):
    S, D = x.shape
    nf, _, tf = g.shape
    tm = 512

    def body(dxo_ref, x_ref, gain_ref, g_ref, u_ref, wd_ref, wg_ref, wu_ref,
             dx_ref, dgain_ref, dg_ref, du_ref, act_ref, dout_ref, acc_ref, ds_ref):
        i = pl.program_id(0)
        j = pl.program_id(1)

        @pl.when(j == 0)
        def _():
            d = (0.5 * dxo_ref[...]).astype(BF16)
            ds_ref[...] = d
            dout_ref[...] = d
            acc_ref[...] = jnp.zeros_like(acc_ref)

        @pl.when((i == 0) & (j == 0))
        def _():
            dgain_ref[...] = jnp.zeros_like(dgain_ref)

        for half in range(2):
            rows = slice(half * (tm // 2), (half + 1) * (tm // 2))
            dact = _nt(ds_ref[rows, :], wd_ref[...])
            gv = g_ref[rows, :].astype(F32)
            uv = u_ref[rows, :].astype(F32)
            sg = _sigmoid(gv)
            silu = gv * sg
            act_ref[rows, :] = (silu * uv).astype(BF16)
            dgv = (dact * uv * (sg * (1.0 + gv * (1.0 - sg)))).astype(BF16)
            duv = (dact * silu).astype(BF16)
            dg_ref[rows, :] = dgv
            du_ref[rows, :] = duv
            acc_ref[rows, :] += _nt(dgv, wg_ref[...]) + _nt(duv, wu_ref[...])

        @pl.when(j == nf - 1)
        def _():
            dx, dgain = _rmsnorm_bwd(acc_ref[...], x_ref[...], gain_ref[...])
            dx_ref[...] = dxo_ref[...] + dx
            dgain_ref[...] += dgain

    return _call(
        body, (dxo, x, gain, g, u, wd, wg, wu), name=name, grid=(S // tm, nf), comm=comm,
        in_specs=[pl.BlockSpec((tm, D), lambda i, j: (i, 0)),
                  pl.BlockSpec((tm, D), lambda i, j: (i, 0)),
                  pl.BlockSpec((1, D), lambda i, j: (0, 0)),
                  pl.BlockSpec((None, tm, tf), lambda i, j: (j, i, 0)),
                  pl.BlockSpec((None, tm, tf), lambda i, j: (j, i, 0)),
                  pl.BlockSpec((None, tf, D), lambda i, j: (j, 0, 0)),
                  pl.BlockSpec((None, D, tf), lambda i, j: (j, 0, 0)),
                  pl.BlockSpec((None, D, tf), lambda i, j: (j, 0, 0))],
        out_specs=[pl.BlockSpec((tm, D), lambda i, j: (i, 0)),
                   pl.BlockSpec((1, D), lambda i, j: (0, 0)),
                   pl.BlockSpec((None, tm, tf), lambda i, j: (j, i, 0)),
                   pl.BlockSpec((None, tm, tf), lambda i, j: (j, i, 0)),
                   pl.BlockSpec((None, tm, tf), lambda i, j: (j, i, 0)),
                   pl.BlockSpec((tm, D), lambda i, j: (i, 0))],
        out_shape=[jax.ShapeDtypeStruct((S, D), F32), jax.ShapeDtypeStruct((1, D), F32),
                   jax.ShapeDtypeStruct((nf, S, tf), BF16), jax.ShapeDtypeStruct((nf, S, tf), BF16),
                   jax.ShapeDtypeStruct((nf, S, tf), BF16), jax.ShapeDtypeStruct((S, D), BF16)],
        scratch_shapes=[pltpu.VMEM((tm, D), F32), pltpu.VMEM((tm, D), BF16)])


def _matmul_tn(a, b, tm, tk, name):
    K, M = a.shape
    N = b.shape[1]

    def body(a_ref, b_ref, o_ref):
        @pl.when(pl.program_id(1) == 0)
        def _():
            o_ref[...] = jnp.zeros_like(o_ref)

        o_ref[...] += _tn(a_ref[...], b_ref[...])

    return pl.pallas_call(
        body, name=name, grid=(M // tm, K // tk),
        in_specs=[pl.BlockSpec((tk, tm), lambda i, k: (k, i)),
                  pl.BlockSpec((tk, N), lambda i, k: (k, 0))],
        out_specs=pl.BlockSpec((tm, N), lambda i, k: (i, 0)),
        out_shape=jax.ShapeDtypeStruct((M, N), F32),
        compiler_params=_params(2),
    )(a, b)


def _dw_chunks(a, b, tk, name, comm=None):
    chunked_a = a.ndim == 3
    nf, S = (a if chunked_a else b).shape[:2]
    M, N = a.shape[-1], b.shape[-1]
    spec = lambda t, w: (pl.BlockSpec((None, tk, w), lambda j, k: (j, k, 0)) if t.ndim == 3
                         else pl.BlockSpec((tk, w), lambda j, k: (k, 0)))

    def body(a_ref, b_ref, o_ref):
        @pl.when(pl.program_id(1) == 0)
        def _():
            o_ref[...] = jnp.zeros_like(o_ref)

        o_ref[...] += _tn(a_ref[...], b_ref[...])

    (out,), landed = _call(
        body, (a, b), name=name, grid=(nf, S // tk), comm=comm,
        in_specs=[spec(a, M), spec(b, N)],
        out_specs=[pl.BlockSpec((None, M, N), lambda j, k: (j, 0, 0))],
        out_shape=[jax.ShapeDtypeStruct((nf, M, N), F32)])
    return out, landed


VIEW_TILE = 512


def _view_spec(d, tile=VIEW_TILE):
    return pl.BlockSpec((tile // d, d * ATTN_WIDTH), lambda i: (i, 0))


def _view_shape(S, d, dtype):
    return jax.ShapeDtypeStruct((S // d, d * ATTN_WIDTH), dtype)


def _tile_to_views(val, planes, out_refs):
    for g in range(4):
        planes[g] = val[:, g * 128:(g + 1) * 128]
    for d, ref in zip(DILATIONS, out_refs):
        if d == 1:
            ref[...] = val.astype(ref.dtype)
            continue
        for r in range(d):
            for g in range(4):
                ref[:, r * ATTN_WIDTH + g * 128:r * ATTN_WIDTH + (g + 1) * 128] = (
                    planes[g, pl.ds(r, planes.shape[1] // d, stride=d), :].astype(ref.dtype))


def _view_to_tile(ref, d, planes):
    if d == 1:
        return ref[...].astype(F32)
    for r in range(d):
        for g in range(4):
            planes[g, pl.ds(r, planes.shape[1] // d, stride=d), :] = (
                ref[:, r * ATTN_WIDTH + g * 128:r * ATTN_WIDTH + (g + 1) * 128].astype(F32))
    return jnp.concatenate([planes[g] for g in range(4)], axis=1)


def _inproj_fwd(x, gain, w_in_p):
    S, D = x.shape
    tm = VIEW_TILE
    W = ATTN_WIDTH

    def body(x_ref, gain_ref, w_ref, h_ref, q1, q4, q16, k1, k4, k16, v1, v4, v16, dq_ref, dk_ref, dv_ref, gate_ref, bd_ref,
             planes):
        xf = x_ref[...]
        r = lax.rsqrt(jnp.mean(xf * xf, axis=-1, keepdims=True) + NORM_EPS)
        h = (xf * r * gain_ref[...]).astype(BF16)
        h_ref[...] = h
        _tile_to_views(_nn(h, w_ref[:, 0:W]) * 0.125, planes, (q1, q4, q16))
        _tile_to_views(_nn(h, w_ref[:, W:2 * W]), planes, (k1, k4, k16))
        _tile_to_views(_nn(h, w_ref[:, 2 * W:3 * W]), planes, (v1, v4, v16))
        dq_ref[...] = _nn(h, w_ref[:, 3 * W:4 * W])
        dk_ref[...] = _nn(h, w_ref[:, 4 * W:5 * W])
        dv_ref[...] = _nn(h, w_ref[:, 5 * W:6 * W])
        gate_ref[...] = _nn(h, w_ref[:, 6 * W:7 * W])
        bd_ref[...] = _nn(h, w_ref[:, 7 * W:7 * W + 128])

    tok = lambda w: pl.BlockSpec((tm, w), lambda i: (i, 0))
    return pl.pallas_call(
        body, name="inproj_fwd", grid=(S // tm,),
        in_specs=[tok(D), pl.BlockSpec((1, D), lambda i: (0, 0)),
                  pl.BlockSpec((D, IN_COLS_PADDED), lambda i: (0, 0))],
        out_specs=[tok(D)] + [_view_spec(d) for d in DILATIONS] * 3 + [tok(W)] * 4 + [tok(128)],
        out_shape=[jax.ShapeDtypeStruct((S, D), BF16)] + [_view_shape(S, d, BF16) for d in DILATIONS] * 3
                  + [jax.ShapeDtypeStruct((S, W), F32)] * 4 + [jax.ShapeDtypeStruct((S, 128), F32)],
        scratch_shapes=[pltpu.VMEM((4, tm, 128), F32)],
        compiler_params=_params(1),
    )(x, gain, w_in_p)


def _inproj_bwd(dxo, x, gain, attn_grads, dsecs, dbd, w_in_p):
    S, D = x.shape
    tm = VIEW_TILE
    W = ATTN_WIDTH

    def body(dxo_ref, x_ref, gain_ref, *rest):
        views, (s3, s4, s5, s6, dbd_ref, w_ref, dx_ref, dgain_ref, dproj_ref, planes) = rest[:9], rest[9:]

        @pl.when(pl.program_id(0) == 0)
        def _():
            dgain_ref[...] = jnp.zeros_like(dgain_ref)

        secs = []
        for k in range(3):
            parts = [_view_to_tile(views[3 * k + p], d, planes) for p, d in enumerate(DILATIONS)]
            secs.append(parts[0] + parts[1] + parts[2])
        secs += [s3[...], s4[...], s5[...], s6[...]]
        dh = jnp.zeros((tm, D), F32)
        for k, s in enumerate(secs):
            d = s.astype(BF16)
            dproj_ref[:, k * W:(k + 1) * W] = d
            dh += _nt(d, w_ref[:, k * W:(k + 1) * W])
        d = dbd_ref[...].astype(BF16)
        dproj_ref[:, 7 * W:7 * W + 128] = d
        dh += _nt(d, w_ref[:, 7 * W:7 * W + 128])
        dx, dgain = _rmsnorm_bwd(dh, x_ref[...], gain_ref[...])
        dx_ref[...] = dxo_ref[...] + dx
        dgain_ref[...] += dgain

    tok = lambda w: pl.BlockSpec((tm, w), lambda i: (i, 0))
    return pl.pallas_call(
        body, name="inproj_bwd", grid=(S // tm,),
        in_specs=[tok(D), tok(D), pl.BlockSpec((1, D), lambda i: (0, 0))] + [_view_spec(d, tm) for d in DILATIONS] * 3
                 + [tok(W)] * 4 + [tok(128)] + [pl.BlockSpec((D, IN_COLS_PADDED), lambda i: (0, 0))],
        out_specs=[tok(D), pl.BlockSpec((1, D), lambda i: (0, 0)), tok(IN_COLS_PADDED)],
        out_shape=[jax.ShapeDtypeStruct((S, D), F32), jax.ShapeDtypeStruct((1, D), F32),
                   jax.ShapeDtypeStruct((S, IN_COLS_PADDED), BF16)],
        scratch_shapes=[pltpu.VMEM((4, tm, 128), F32)],
        compiler_params=_params(1),
    )(dxo, x, gain, *[g for grads in attn_grads for g in grads], *dsecs, dbd, w_in_p)


def _slope(h):
    return 2.0 ** (-8.0 * (h + 1) / ATTN_HEADS)


def _head_bias(steps, d, heads=tuple(range(ATTN_HEADS))):
    stepsf = steps.astype(F32)
    return jnp.stack([stepsf * (-_slope(h) * d) for h in heads])


def _hnt(a, b):
    return lax.dot_general(a, b, (((2,), (2,)), ((0,), (0,))), preferred_element_type=F32)


def _hnn(a, b):
    return lax.dot_general(a, b, (((2,), (1,)), ((0,), (0,))), preferred_element_type=F32)


def _blocks_per_step(nb):
    return next(n for n in (4, 2, 1) if nb % n == 0)


def _query_step_specs(qb):
    B = ATTN_BLOCK
    cur = pl.BlockSpec((qb * B, ATTN_WIDTH), lambda r, n: (n, r))
    prev = pl.BlockSpec((B, ATTN_WIDTH), lambda r, n: (jnp.maximum(qb * n - 1, 0), r))
    return cur, prev


def _prev_block(prev_ref, cur_ref, sub, sl):
    B = ATTN_BLOCK
    return prev_ref[:, sl] if sub == 0 else cur_ref[(sub - 1) * B:sub * B, sl]


def _head_cols(tile, lo, big):
    return [_head_col(tile, lo, big), _head_col(tile, jnp.logical_not(lo), big)]


def _attn_fwd(q, k, v, d, name):
    L = q.shape[0]
    nb = L // ATTN_BLOCK
    B = ATTN_BLOCK
    QB = _blocks_per_step(nb)

    def body(q_ref, kp_ref, kc_ref, vp_ref, vc_ref, o_ref, lse_ref):
        n = pl.program_id(1)
        qi = lax.broadcasted_iota(jnp.int32, (B, 2 * B), 0)
        kj = lax.broadcasted_iota(jnp.int32, (B, 2 * B), 1)
        steps = qi + B - kj
        band = (steps >= 0) & (steps <= B)
        lo = lax.broadcasted_iota(jnp.int32, (B, 128), 1) < 64
        bias = _head_bias(steps, d)
        for sub in range(QB):
            rows = slice(sub * B, (sub + 1) * B)
            valid = band & ((kj >= B) | (n > 0)) if sub == 0 else band
            qs, ks, vs = [], [], []
            for G in range(4):
                sl = slice(G * 128, (G + 1) * 128)
                qg = q_ref[rows, sl]
                kg = jnp.concatenate([_prev_block(kp_ref, kc_ref, sub, sl), kc_ref[rows, sl]], axis=0)
                vg = jnp.concatenate([_prev_block(vp_ref, vc_ref, sub, sl), vc_ref[rows, sl]], axis=0)
                qs += [jnp.where(lo, qg, jnp.zeros_like(qg)), jnp.where(lo, jnp.zeros_like(qg), qg)]
                ks += [kg, kg]
                vs += [vg, vg]
            s = jnp.where(valid, _hnt(jnp.stack(qs), jnp.stack(ks)) + bias, NEG_BIG)
            m = jnp.max(s, axis=-1, keepdims=True)
            p = jnp.exp(s - m)
            l = jnp.sum(p, axis=-1, keepdims=True)
            o = _hnn(p.astype(BF16), jnp.stack(vs)) / l
            lse = m + jnp.log(l)
            for G in range(4):
                sl = slice(G * 128, (G + 1) * 128)
                o_ref[rows, sl] = jnp.where(lo, o[2 * G], o[2 * G + 1])
                lse_ref[rows, sl] = jnp.where(lo, lse[2 * G], lse[2 * G + 1])

    cur, prev = _query_step_specs(QB)
    return pl.pallas_call(
        body, name=name, grid=(d, nb // QB),
        in_specs=[cur, prev, cur, prev, cur],
        out_specs=[cur, cur],
        out_shape=[jax.ShapeDtypeStruct((L, d * ATTN_WIDTH), F32)] * 2,
        compiler_params=_params(2),
    )(q, k, k, v, v)


def _attn_merge(parts):
    S = parts[0][0].shape[0]
    tm = VIEW_TILE

    def body(o1, s1, o2, s2, o3, s3, o_ref, lse1, lse4, lse16, planes):
        outs, lses = [], []
        for d, (o, s) in zip(DILATIONS, ((o1, s1), (o2, s2), (o3, s3))):
            outs.append(_view_to_tile(o, d, planes))
            lses.append(_view_to_tile(s, d, planes))
        mx = jnp.maximum(jnp.maximum(lses[0], lses[1]), lses[2])
        es = [jnp.exp(s - mx) for s in lses]
        den = es[0] + es[1] + es[2]
        o_ref[...] = (es[0] * outs[0] + es[1] * outs[1] + es[2] * outs[2]) / den
        _tile_to_views(mx + jnp.log(den), planes, (lse1, lse4, lse16))

    views = [_view_spec(d) for d in DILATIONS]
    flat = [t for p in parts for t in p]
    return pl.pallas_call(
        body, name="attn_merge", grid=(S // tm,),
        in_specs=[views[p] for p in range(3) for _ in range(2)],
        out_specs=[views[0]] + views,
        out_shape=[jax.ShapeDtypeStruct((S, ATTN_WIDTH), F32)] + [_view_shape(S, d, F32) for d in DILATIONS],
        scratch_shapes=[pltpu.VMEM((4, tm, 128), F32)],
        compiler_params=_params(1),
    )(*flat)


def _head_col(t, msk, big):
    if big:
        return jnp.max(jnp.where(msk, t, NEG_BIG), axis=-1, keepdims=True)
    return jnp.sum(jnp.where(msk, t, 0.0), axis=-1, keepdims=True) * (1.0 / 64.0)


def _attn_bwd_q(q, k, v, do, lse, dd, d, name):
    L = q.shape[0]
    nb = L // ATTN_BLOCK
    B = ATTN_BLOCK
    QB = _blocks_per_step(nb)

    def body(q_ref, kp_ref, kc_ref, vp_ref, vc_ref, do_ref, lse_ref, dd_ref, dq_ref):
        n = pl.program_id(1)
        qi = lax.broadcasted_iota(jnp.int32, (B, 2 * B), 0)
        kj = lax.broadcasted_iota(jnp.int32, (B, 2 * B), 1)
        steps = qi + B - kj
        band = (steps >= 0) & (steps <= B)
        lo = lax.broadcasted_iota(jnp.int32, (B, 128), 1) < 64
        bias = _head_bias(steps, d)
        for sub in range(QB):
            rows = slice(sub * B, (sub + 1) * B)
            valid = band & ((kj >= B) | (n > 0)) if sub == 0 else band
            qs, ks, vs, dos, lses, dcols = [], [], [], [], [], []
            for G in range(4):
                sl = slice(G * 128, (G + 1) * 128)
                qg = q_ref[rows, sl]
                kg = jnp.concatenate([_prev_block(kp_ref, kc_ref, sub, sl), kc_ref[rows, sl]], axis=0)
                vg = jnp.concatenate([_prev_block(vp_ref, vc_ref, sub, sl), vc_ref[rows, sl]], axis=0)
                dog = do_ref[rows, sl]
                qs += [jnp.where(lo, qg, jnp.zeros_like(qg)), jnp.where(lo, jnp.zeros_like(qg), qg)]
                dos += [jnp.where(lo, dog, 0.0).astype(BF16), jnp.where(lo, 0.0, dog).astype(BF16)]
                ks += [kg, kg]
                vs += [vg, vg]
                lses += _head_cols(lse_ref[rows, sl], lo, True)
                dcols += _head_cols(dd_ref[rows, sl], lo, False)
            kb = jnp.stack(ks)
            s = _hnt(jnp.stack(qs), kb) + bias
            p = jnp.where(valid, jnp.exp(jnp.where(valid, s, NEG_BIG) - jnp.stack(lses)), 0.0)
            dp = _hnt(jnp.stack(dos), jnp.stack(vs))
            ds = p * (dp - jnp.stack(dcols))
            dq = _hnn(ds.astype(BF16), kb) * 0.125
            for G in range(4):
                dq_ref[rows, G * 128:(G + 1) * 128] = jnp.where(lo, dq[2 * G], dq[2 * G + 1]).astype(BF16)

    cur, prev = _query_step_specs(QB)
    return pl.pallas_call(
        body, name=name, grid=(d, nb // QB), in_specs=[cur, prev, cur, prev, cur, cur, cur, cur], out_specs=cur,
        out_shape=jax.ShapeDtypeStruct((L, d * ATTN_WIDTH), BF16), compiler_params=_params(2),
    )(q, k, k, v, v, do, lse, dd)


def _attn_bwd_kv(q, k, v, do, lse, dd, d, name):
    L = q.shape[0]
    nb = L // ATTN_BLOCK
    B = ATTN_BLOCK
    KB = _blocks_per_step(nb)
    n_steps = nb // KB

    def body(k_ref, v_ref, qc_ref, qn_ref, doc_ref, don_ref, lsec_ref, lsen_ref, ddc_ref, ddn_ref, dk_ref, dv_ref):
        j = pl.program_id(1)
        qrow = lax.broadcasted_iota(jnp.int32, (2 * B, B), 0)
        kk = lax.broadcasted_iota(jnp.int32, (2 * B, B), 1)
        steps = qrow - kk
        band = (steps >= 0) & (steps <= B)
        lo2 = lax.broadcasted_iota(jnp.int32, (2 * B, 128), 1) < 64
        lo = lax.broadcasted_iota(jnp.int32, (B, 128), 1) < 64
        stepsf = steps.astype(F32)
        for sub in range(KB):
            rows = slice(sub * B, (sub + 1) * B)
            last = sub == KB - 1
            valid = band & ((qrow < B) | (j < n_steps - 1)) if last else band
            after = lambda cur_ref, nxt_ref, sl: nxt_ref[:, sl] if last else cur_ref[(sub + 1) * B:(sub + 2) * B, sl]
            for G in range(4):
                sl = slice(G * 128, (G + 1) * 128)
                kg = k_ref[rows, sl]
                vg = v_ref[rows, sl]
                qq = jnp.concatenate([qc_ref[rows, sl], after(qc_ref, qn_ref, sl)], axis=0)
                doo = jnp.concatenate([doc_ref[rows, sl], after(doc_ref, don_ref, sl)], axis=0)
                lse2 = jnp.concatenate([lsec_ref[rows, sl], after(lsec_ref, lsen_ref, sl)], axis=0)
                dd2 = jnp.concatenate([ddc_ref[rows, sl], after(ddc_ref, ddn_ref, sl)], axis=0)
                doo_b = doo.astype(BF16)
                dks, dvs = [], []
                for half in (0, 1):
                    msk = lo2 if half == 0 else jnp.logical_not(lo2)
                    qm = jnp.where(msk, qq, jnp.zeros_like(qq))
                    s = _nt(qm, kg) - (_slope(2 * G + half) * d) * stepsf
                    lse_c = _head_col(lse2, msk, True)
                    p = jnp.where(valid, jnp.exp(jnp.where(valid, s, NEG_BIG) - lse_c), 0.0)
                    dvs.append(_tn(p.astype(BF16), doo_b))
                    dom = jnp.where(msk, doo, 0.0).astype(BF16)
                    dp = _nt(dom, vg)
                    dcol = _head_col(dd2, msk, False)
                    ds = p * (dp - dcol)
                    dks.append(_tn(ds.astype(BF16), qq))
                dk_ref[rows, sl] = jnp.where(lo, dks[0], dks[1]).astype(BF16)
                dv_ref[rows, sl] = jnp.where(lo, dvs[0], dvs[1]).astype(BF16)

    cur = pl.BlockSpec((KB * B, ATTN_WIDTH), lambda r, j: (j, r))
    nxt = pl.BlockSpec((B, ATTN_WIDTH), lambda r, j: (jnp.minimum(KB * (j + 1), nb - 1), r))
    return pl.pallas_call(
        body, name=name, grid=(d, n_steps), in_specs=[cur, cur, cur, nxt, cur, nxt, cur, nxt, cur, nxt],
        out_specs=[cur, cur],
        out_shape=[jax.ShapeDtypeStruct((L, d * ATTN_WIDTH), BF16)] * 2, compiler_params=_params(2),
    )(k, v, q, q, do, do, lse, lse, dd, dd)


CONV_T = 512
HALO = 8


def _per_head(head, refs):
    for h in range(DN_HEADS):
        lanes = pl.ds(h * DN_HEAD_DIM, DN_HEAD_DIM)
        head(*[r.at[:, lanes] for r in refs[:-1]], refs[-1])


def _conv_taps(pad_ref, w, T):
    acc = pad_ref[pl.ds(HALO - 3, T), :] * w[0:1, :]
    for j in range(1, CONV_WIDTH):
        acc = acc + pad_ref[pl.ds(HALO - 3 + j, T), :] * w[j:j + 1, :]
    return acc


def _conv_fwd(xq, xk, xv, conv_w):
    S = xq.shape[0]
    T = CONV_T

    def body(*refs):
        _per_head(head, refs)

    def head(xq_ref, xqh_ref, xk_ref, xkh_ref, xv_ref, xvh_ref, wq_ref, wk_ref, wv_ref,
             qn_ref, kn_ref, v_ref, pad_ref):
        i = pl.program_id(0)

        def act(x_ref, xh_ref, w_ref):
            pad_ref[pl.ds(0, HALO), :] = jnp.where(i > 0, xh_ref[...], 0.0)
            pad_ref[pl.ds(HALO, T), :] = x_ref[...]
            c = _conv_taps(pad_ref, w_ref[...], T)
            return c * _sigmoid(c)

        def l2n(t):
            return t * lax.rsqrt(jnp.sum(t * t, axis=-1, keepdims=True) + L2_EPS)

        qn_ref[...] = l2n(act(xq_ref, xqh_ref, wq_ref))
        kn_ref[...] = l2n(act(xk_ref, xkh_ref, wk_ref))
        v_ref[...] = act(xv_ref, xvh_ref, wv_ref)

    tile = pl.BlockSpec((T, DN_WIDTH), lambda i: (i, 0))
    halo = pl.BlockSpec((HALO, DN_WIDTH), lambda i: (jnp.maximum(i * (T // HALO) - 1, 0), 0))
    wspec = lambda sec: pl.BlockSpec((CONV_WIDTH, DN_WIDTH), lambda i, sec=sec: (0, sec))
    return pl.pallas_call(
        body, name="dn_conv_fwd", grid=(S // T,),
        in_specs=[tile, halo, tile, halo, tile, halo, wspec(0), wspec(1), wspec(2)],
        out_specs=[tile, tile, tile],
        out_shape=[jax.ShapeDtypeStruct((S, DN_WIDTH), F32)] * 3,
        scratch_shapes=[pltpu.VMEM((T + HALO, 128), F32)],
        compiler_params=_params(1),
    )(xq, xq, xk, xk, xv, xv, conv_w, conv_w, conv_w)


def _conv_bwd_pre(xq, xk, xv, conv_w, dqn, dkn, dv):
    S = xq.shape[0]
    T = CONV_T

    def body(*refs):
        _per_head(head, refs)

    def head(xq_ref, xqh_ref, xk_ref, xkh_ref, xv_ref, xvh_ref, wq_ref, wk_ref, wv_ref,
             dqn_ref, dkn_ref, dv_ref, dcq_ref, dck_ref, dcv_ref, dwq_ref, dwk_ref, dwv_ref, pad_ref):
        i = pl.program_id(0)

        def one(x_ref, xh_ref, w_ref, dy_ref, dc_ref, dw_ref, normed):
            pad_ref[pl.ds(0, HALO), :] = jnp.where(i > 0, xh_ref[...], 0.0)
            pad_ref[pl.ds(HALO, T), :] = x_ref[...]
            c = _conv_taps(pad_ref, w_ref[...], T)
            sg = _sigmoid(c)
            a = c * sg
            dy = dy_ref[...]
            if normed:
                r = lax.rsqrt(jnp.sum(a * a, axis=-1, keepdims=True) + L2_EPS)
                y = a * r
                da = r * (dy - y * jnp.sum(dy * y, axis=-1, keepdims=True))
            else:
                da = dy
            dc = da * (sg * (1.0 + c * (1.0 - sg)))
            dc_ref[...] = dc

            @pl.when(i == 0)
            def _():
                dw_ref[...] = jnp.zeros_like(dw_ref)

            rows = [jnp.sum(dc * pad_ref[pl.ds(HALO - 3 + j, T), :], axis=0, keepdims=True) for j in range(CONV_WIDTH)]
            dw_ref[...] += jnp.concatenate(rows + [jnp.zeros((8 - CONV_WIDTH, 128), F32)], axis=0)

        one(xq_ref, xqh_ref, wq_ref, dqn_ref, dcq_ref, dwq_ref, True)
        one(xk_ref, xkh_ref, wk_ref, dkn_ref, dck_ref, dwk_ref, True)
        one(xv_ref, xvh_ref, wv_ref, dv_ref, dcv_ref, dwv_ref, False)

    tile = pl.BlockSpec((T, DN_WIDTH), lambda i: (i, 0))
    halo = pl.BlockSpec((HALO, DN_WIDTH), lambda i: (jnp.maximum(i * (T // HALO) - 1, 0), 0))
    wspec = lambda sec: pl.BlockSpec((CONV_WIDTH, DN_WIDTH), lambda i, sec=sec: (0, sec))
    dwspec = pl.BlockSpec((8, DN_WIDTH), lambda i: (0, 0))
    return pl.pallas_call(
        body, name="dn_conv_bwd_pre", grid=(S // T,),
        in_specs=[tile, halo, tile, halo, tile, halo, wspec(0), wspec(1), wspec(2), tile, tile, tile],
        out_specs=[tile, tile, tile, dwspec, dwspec, dwspec],
        out_shape=[jax.ShapeDtypeStruct((S, DN_WIDTH), F32)] * 3 + [jax.ShapeDtypeStruct((8, DN_WIDTH), F32)] * 3,
        scratch_shapes=[pltpu.VMEM((T + HALO, 128), F32)],
        compiler_params=_params(1),
    )(xq, xq, xk, xk, xv, xv, conv_w, conv_w, conv_w, dqn, dkn, dv)


def _conv_bwd_x(dcq, dck, dcv, conv_w):
    S = dcq.shape[0]
    T = CONV_T
    nt = S // T

    def body(*refs):
        _per_head(head, refs)

    def head(dq_ref, dqh_ref, dk_ref, dkh_ref, dv_ref, dvh_ref, wq_ref, wk_ref, wv_ref,
             oq_ref, ok_ref, ov_ref, pad_ref):
        i = pl.program_id(0)

        def one(d_ref, dh_ref, w_ref, o_ref):
            pad_ref[pl.ds(0, T), :] = d_ref[...]
            pad_ref[pl.ds(T, HALO), :] = jnp.where(i < nt - 1, dh_ref[...], 0.0)
            w = w_ref[...]
            acc = pad_ref[pl.ds(3, T), :] * w[0:1, :]
            for j in range(1, CONV_WIDTH):
                acc = acc + pad_ref[pl.ds(3 - j, T), :] * w[j:j + 1, :]
            o_ref[...] = acc

        one(dq_ref, dqh_ref, wq_ref, oq_ref)
        one(dk_ref, dkh_ref, wk_ref, ok_ref)
        one(dv_ref, dvh_ref, wv_ref, ov_ref)

    tile = pl.BlockSpec((T, DN_WIDTH), lambda i: (i, 0))
    halo = pl.BlockSpec((HALO, DN_WIDTH), lambda i: (jnp.minimum((i + 1) * (T // HALO), S // HALO - 1), 0))
    wspec = lambda sec: pl.BlockSpec((CONV_WIDTH, DN_WIDTH), lambda i, sec=sec: (0, sec))
    return pl.pallas_call(
        body, name="dn_conv_bwd_x", grid=(nt,),
        in_specs=[tile, halo, tile, halo, tile, halo, wspec(0), wspec(1), wspec(2)],
        out_specs=[tile, tile, tile],
        out_shape=[jax.ShapeDtypeStruct((S, DN_WIDTH), F32)] * 3,
        scratch_shapes=[pltpu.VMEM((T + HALO, 128), F32)],
        compiler_params=_params(1),
    )(dcq, dcq, dck, dck, dcv, dcv, conv_w, conv_w, conv_w)


PREP_CHUNKS = 4
SCAN_CHUNKS = 8


def _bnn(a, b):
    return lax.dot_general(a, b, (((2,), (1,)), ((0,), (0,))), preferred_element_type=F32, precision=HI)


def _bnt(a, b):
    return lax.dot_general(a, b, (((2,), (2,)), ((0,), (0,))), preferred_element_type=F32, precision=HI)


def _btn(a, b):
    return lax.dot_general(a, b, (((1,), (1,)), ((0,), (0,))), preferred_element_type=F32, precision=HI)


def _tri_inverse_b(a, blk, eye):
    dg = jnp.where(blk, a, 0.0)
    lo = a - dg
    d2 = _bnn(dg, dg)
    d4 = _bnn(d2, d2)
    d8 = _bnn(d4, d4)
    td = _bnn(_bnn(_bnn(eye - dg, eye + d2), eye + d4), eye + d8)
    b = _bnn(td, lo)
    b2 = _bnn(b, b)
    return _bnn(_bnn(eye - b, eye + b2), td)


def _dn_common_b(bds, avec, dvec, q_raw, k, v, t=None):
    C = DN_CHUNK
    lane = lax.broadcasted_iota(jnp.int32, (C, 128), 1)
    row = lax.broadcasted_iota(jnp.int32, (1, C, C), 1)
    col = lax.broadcasted_iota(jnp.int32, (1, C, C), 2)
    incl = row >= col
    strict = row > col
    eye = (row == col).astype(F32)
    blk = (row // 16) == (col // 16)
    pick = lambda tile, ln: jnp.sum(jnp.where(lane == ln, tile, 0.0), axis=-1, keepdims=True)
    betas, graws, zcs = [], [], []
    for bd in bds:
        z = bd + dvec
        g_all = -jnp.exp(avec) * (jnp.maximum(z, 0.0) + jnp.log(1.0 + jnp.exp(-jnp.abs(z))))
        beta_all = _sigmoid(bd)
        for h in range(DN_HEADS):
            betas.append(pick(beta_all, h))
            graws.append(pick(g_all, DN_HEADS + h))
            zcs.append(pick(z, DN_HEADS + h))
    beta, graw, zc = jnp.stack(betas), jnp.stack(graws), jnp.stack(zcs)
    to_row = lambda c: jnp.sum(eye * c, axis=1, keepdims=True)
    gc = jnp.sum(jnp.where(incl, to_row(graw), 0.0), axis=-1, keepdims=True)
    decay = jnp.exp(jnp.where(incl, gc - to_row(gc), NEG_BIG))
    q = q_raw * (DN_HEAD_DIM ** -0.5)
    kb = k * beta
    kk = _bnt(kb, k)
    if t is None:
        t = _tri_inverse_b(jnp.where(strict, kk * decay, 0.0), blk, eye)
    eg = jnp.exp(gc)
    rhs_w = kb * eg
    u = _bnn(t, v * beta)
    w = _bnn(t, rhs_w)
    qk = _bnt(q, k)
    aq = jnp.where(incl, qk * decay, 0.0)
    last = lax.broadcasted_iota(jnp.int32, (1, C, 1), 1) == C - 1
    g_last = jnp.sum(jnp.where(last, gc, 0.0), axis=1, keepdims=True)
    ekd = jnp.exp(g_last - gc)
    return dict(beta=beta, graw=graw, zc=zc, gc=gc, decay=decay, q=q, kb=kb, kk=kk, t=t, eg=eg, rhs_w=rhs_w,
                u=u, w=w, qk=qk, aq=aq, g_last=g_last, ekd=ekd, kd=k * ekd, qg=q * eg,
                incl=incl, strict=strict, eye=eye, lane=lane, row=row, col=col, last=last)


def _stack_heads(ref, rows):
    return jnp.stack([ref[rows, h * DN_HEAD_DIM:(h + 1) * DN_HEAD_DIM] for h in range(DN_HEADS)])


def _stack_units(ref, nc):
    C = DN_CHUNK
    return jnp.concatenate([_stack_heads(ref, slice(ci * C, (ci + 1) * C)) for ci in range(nc)], axis=0)


def _store_units(ref, val, nc):
    C = DN_CHUNK
    for ci in range(nc):
        for h in range(DN_HEADS):
            ref[ci * C:(ci + 1) * C, h * DN_HEAD_DIM:(h + 1) * DN_HEAD_DIM] = val[ci * DN_HEADS + h]


def _dn_prep(qn, kn, v, bd, avec, dvec):
    S = qn.shape[0]
    C = DN_CHUNK
    N = S // C
    nc = PREP_CHUNKS

    def body(q_ref, k_ref, v_ref, bd_ref, a_ref, d_ref, u_ref, w_ref, qg_ref, kd_ref, aq_ref, t_ref, egl_ref):
        bds = [bd_ref[ci * C:(ci + 1) * C, :] for ci in range(nc)]
        c = _dn_common_b(bds, a_ref[...], d_ref[...], _stack_units(q_ref, nc), _stack_units(k_ref, nc), _stack_units(v_ref, nc))
        _store_units(u_ref, c["u"], nc)
        _store_units(w_ref, c["w"], nc)
        _store_units(qg_ref, c["qg"], nc)
        _store_units(kd_ref, c["kd"], nc)
        egl = jnp.broadcast_to(jnp.exp(c["g_last"]), (nc * DN_HEADS, 1, 128))
        for ci in range(nc):
            for h in range(DN_HEADS):
                aq_ref[h, ci * C:(ci + 1) * C, :] = c["aq"][ci * DN_HEADS + h]
                t_ref[h, ci * C:(ci + 1) * C, :] = c["t"][ci * DN_HEADS + h]
            egl_ref[ci * 8:(ci + 1) * 8, :] = jnp.concatenate(
                [egl[ci * DN_HEADS + h] for h in range(DN_HEADS)] + [jnp.zeros((8 - DN_HEADS, 128), F32)], axis=0)

    tok = lambda w: pl.BlockSpec((nc * C, w), lambda n: (n, 0))
    sq = pl.BlockSpec((DN_HEADS, nc * C, C), lambda n: (0, n, 0))
    vec = pl.BlockSpec((1, 128), lambda n: (0, 0))
    return pl.pallas_call(
        body, name="dn_prep", grid=(N // nc,),
        in_specs=[tok(DN_WIDTH)] * 3 + [tok(128), vec, vec],
        out_specs=[tok(DN_WIDTH)] * 4 + [sq, sq, pl.BlockSpec((nc * 8, 128), lambda n: (n, 0))],
        out_shape=[jax.ShapeDtypeStruct((S, DN_WIDTH), F32)] * 4 + [jax.ShapeDtypeStruct((DN_HEADS, S, C), F32)] * 2
                  + [jax.ShapeDtypeStruct((N * 8, 128), F32)],
        compiler_params=_params(1),
    )(qn, kn, v, bd, avec, dvec)


def _dn_scan_fwd(u, w, qg, kd, aq, egl, gate, dn_gain):
    S = u.shape[0]
    C = DN_CHUNK
    N = S // C
    HD = DN_HEAD_DIM
    nc = SCAN_CHUNKS

    def body(u_ref, w_ref, qg_ref, kd_ref, aq_ref, egl_ref, gate_ref, gain_ref, dn_ref, o_ref, vn_ref, st_ref, state_ref):
        @pl.when(pl.program_id(0) == 0)
        def _():
            state_ref[...] = jnp.zeros_like(state_ref)

        gain = gain_ref[...]
        for ci in range(nc):
            rows = slice(ci * C, (ci + 1) * C)
            st = state_ref[...]
            for h in range(DN_HEADS):
                st_ref[ci * DN_WIDTH + h * HD:ci * DN_WIDTH + (h + 1) * HD, :] = st[h]
            v_new = _stack_heads(u_ref, rows) - _bnn(_stack_heads(w_ref, rows), st)
            o = _bnn(_stack_heads(qg_ref, rows), st) + _bnn(aq_ref[:, rows, :], v_new)
            egl = jnp.stack([egl_ref[ci * 8 + h:ci * 8 + h + 1, :] for h in range(DN_HEADS)])
            state_ref[...] = st * egl + _btn(_stack_heads(kd_ref, rows), v_new)
            r = lax.rsqrt(jnp.mean(o * o, axis=-1, keepdims=True) + NORM_EPS)
            gt = _stack_heads(gate_ref, rows)
            dn = o * r * gain * (gt * _sigmoid(gt))
            for h in range(DN_HEADS):
                sl = slice(h * HD, (h + 1) * HD)
                vn_ref[rows, sl] = v_new[h]
                o_ref[rows, sl] = o[h]
                dn_ref[rows, sl] = dn[h]

    tok = lambda wd: pl.BlockSpec((nc * C, wd), lambda n: (n, 0))
    sq = pl.BlockSpec((DN_HEADS, nc * C, C), lambda n: (0, n, 0))
    vec = pl.BlockSpec((1, 128), lambda n: (0, 0))
    return pl.pallas_call(
        body, name="dn_scan_fwd", grid=(N // nc,),
        in_specs=[tok(DN_WIDTH)] * 4 + [sq, pl.BlockSpec((nc * 8, 128), lambda n: (n, 0)), tok(DN_WIDTH), vec],
        out_specs=[tok(DN_WIDTH)] * 3 + [pl.BlockSpec((nc * DN_WIDTH, HD), lambda n: (n, 0))],
        out_shape=[jax.ShapeDtypeStruct((S, DN_WIDTH), F32)] * 3 + [jax.ShapeDtypeStruct((N * DN_WIDTH, HD), F32)],
        scratch_shapes=[pltpu.VMEM((DN_HEADS, HD, HD), F32)],
        compiler_params=_params(1),
    )(u, w, qg, kd, aq, egl, gate, dn_gain)


def _dn_scan_bwd(w, qg, kd, aq, egl, gate, dn_gain, o, ddn):
    S = w.shape[0]
    C = DN_CHUNK
    N = S // C
    HD = DN_HEAD_DIM
    nc = SCAN_CHUNKS

    def body(w_ref, qg_ref, kd_ref, aq_ref, egl_ref, gate_ref, gain_ref, o_ref, ddn_ref,
             do_ref, dvn_ref, dgate_ref, dst_ref, small_ref, dstate_ref):
        @pl.when(pl.program_id(0) == 0)
        def _():
            dstate_ref[...] = jnp.zeros_like(dstate_ref)
            small_ref[...] = jnp.zeros_like(small_ref)

        gain = gain_ref[...]
        d_gain = jnp.zeros((1, 128), F32)
        for ci in reversed(range(nc)):
            rows = slice(ci * C, (ci + 1) * C)
            dsn = dstate_ref[...]
            for h in range(DN_HEADS):
                dst_ref[ci * DN_WIDTH + h * HD:ci * DN_WIDTH + (h + 1) * HD, :] = dsn[h]
            ov = _stack_heads(o_ref, rows)
            r = lax.rsqrt(jnp.mean(ov * ov, axis=-1, keepdims=True) + NORM_EPS)
            on = ov * r
            gt = _stack_heads(gate_ref, rows)
            sgt = _sigmoid(gt)
            silu_g = gt * sgt
            dy = _stack_heads(ddn_ref, rows)
            d_gain = d_gain + jnp.sum(jnp.sum(dy * on * silu_g, axis=1, keepdims=True), axis=0)
            dgate = dy * on * gain * (sgt * (1.0 + gt * (1.0 - sgt)))
            don = dy * gain * silu_g
            do = r * (don - on * jnp.mean(don * on, axis=-1, keepdims=True))
            d_vnew = _btn(aq_ref[:, rows, :], do) + _bnn(_stack_heads(kd_ref, rows), dsn)
            egl = jnp.stack([egl_ref[ci * 8 + h:ci * 8 + h + 1, :] for h in range(DN_HEADS)])
            dstate_ref[...] = _btn(_stack_heads(qg_ref, rows), do) + dsn * egl - _btn(_stack_heads(w_ref, rows), d_vnew)
            for h in range(DN_HEADS):
                sl = slice(h * HD, (h + 1) * HD)
                do_ref[rows, sl] = do[h]
                dvn_ref[rows, sl] = d_vnew[h]
                dgate_ref[rows, sl] = dgate[h]
        small_ref[...] += jnp.concatenate([d_gain, jnp.zeros((7, 128), F32)], axis=0)

    nb = N // nc
    tok = lambda wd: pl.BlockSpec((nc * C, wd), lambda i: (nb - 1 - i, 0))
    sq = pl.BlockSpec((DN_HEADS, nc * C, C), lambda i: (0, nb - 1 - i, 0))
    vec = pl.BlockSpec((1, 128), lambda i: (0, 0))
    return pl.pallas_call(
        body, name="dn_scan_bwd", grid=(nb,),
        in_specs=[tok(DN_WIDTH)] * 3 + [sq, pl.BlockSpec((nc * 8, 128), lambda i: (nb - 1 - i, 0)), tok(DN_WIDTH), vec,
                                       tok(DN_WIDTH), tok(DN_WIDTH)],
        out_specs=[tok(DN_WIDTH)] * 3 + [pl.BlockSpec((nc * DN_WIDTH, HD), lambda i: (nb - 1 - i, 0)),
                                        pl.BlockSpec((8, 128), lambda i: (0, 0))],
        out_shape=[jax.ShapeDtypeStruct((S, DN_WIDTH), F32)] * 3 + [jax.ShapeDtypeStruct((N * DN_WIDTH, HD), F32),
                                                                  jax.ShapeDtypeStruct((8, 128), F32)],
        scratch_shapes=[pltpu.VMEM((DN_HEADS, HD, HD), F32)],
        compiler_params=_params(1),
    )(w, qg, kd, aq, egl, gate, dn_gain, o, ddn)


def _dn_post(qn, kn, v, bd, avec, dvec, t_inv, v_new_all, states, dstates, do_all, dvn_all, comm=None):
    S = qn.shape[0]
    C = DN_CHUNK
    N = S // C
    HD = DN_HEAD_DIM
    nc = PREP_CHUNKS
    B = nc * DN_HEADS

    def body(q_ref, k_ref, v_ref, bd_ref, a_ref, d_ref, t_ref, vn_ref, st_ref, dst_ref, do_ref, dvn_ref,
             dq_ref, dk_ref, dv_ref, dbd_ref, small_ref):
        @pl.when(pl.program_id(0) == 0)
        def _():
            small_ref[...] = jnp.zeros_like(small_ref)

        avec = a_ref[...]
        bds = [bd_ref[ci * C:(ci + 1) * C, :] for ci in range(nc)]
        k = _stack_units(k_ref, nc)
        vv = _stack_units(v_ref, nc)
        t = jnp.concatenate([t_ref[:, ci * C:(ci + 1) * C, :] for ci in range(nc)], axis=0)
        c = _dn_common_b(bds, avec, d_ref[...], _stack_units(q_ref, nc), k, vv, t=t)
        q, kb, eg, u, w = c["q"], c["kb"], c["eg"], c["u"], c["w"]
        beta, decay, incl, strict, eye = c["beta"], c["decay"], c["incl"], c["strict"], c["eye"]
        st = jnp.stack([st_ref[b * HD:(b + 1) * HD, :] for b in range(B)])
        dsn = jnp.stack([dst_ref[b * HD:(b + 1) * HD, :] for b in range(B)])
        v_new = _stack_units(vn_ref, nc)
        do = _stack_units(do_ref, nc)
        d_vnew = _stack_units(dvn_ref, nc)
        egl = jnp.exp(c["g_last"])
        daq = jnp.where(incl, _bnt(do, v_new), 0.0)
        d_qg = _bnt(do, st)
        d_kd = _bnt(v_new, dsn)
        d_glast = jnp.sum(jnp.sum(dsn * st, axis=-1, keepdims=True), axis=1, keepdims=True) * egl
        d_w = -_bnt(d_vnew, st)
        d_ru = _btn(t, d_vnew)
        d_rw = _btn(t, d_w)
        da = -jnp.where(strict, _bnt(d_ru, u) + _bnt(d_rw, w), 0.0)
        dv = d_ru * beta
        dbeta = jnp.sum(d_ru * vv, axis=-1, keepdims=True)
        dkb = d_rw * eg
        dgc = jnp.sum(d_rw * c["rhs_w"], axis=-1, keepdims=True)
        dkk = da * decay
        ddecay = da * c["kk"]
        dkb = dkb + _bnn(dkk, k)
        dk = _btn(dkk, kb)
        dqk = daq * decay
        ddecay = ddecay + daq * c["qk"]
        dq = _bnn(dqk, k)
        dk = dk + _btn(dqk, q)
        m = ddecay * decay
        col_sum = jnp.sum(m, axis=1, keepdims=True)
        dgc = dgc + jnp.sum(m, axis=-1, keepdims=True) - jnp.sum(eye * col_sum, axis=-1, keepdims=True)
        dq = dq + d_qg * eg
        dgc = dgc + jnp.sum(d_qg * c["qg"], axis=-1, keepdims=True)
        dk = dk + d_kd * c["ekd"]
        tk = jnp.sum(d_kd * c["kd"], axis=-1, keepdims=True)
        dgc = dgc - tk
        d_glast = d_glast + jnp.sum(tk, axis=1, keepdims=True)
        dk = dk + dkb * beta
        dbeta = dbeta + jnp.sum(dkb * k, axis=-1, keepdims=True)
        dgc = dgc + jnp.where(c["last"], d_glast, 0.0)
        dgc_row = jnp.sum(eye * dgc, axis=1, keepdims=True)
        dgraw = jnp.sum(jnp.where(c["col"] >= c["row"], dgc_row, 0.0), axis=-1, keepdims=True)
        _store_units(dq_ref, dq * (HD ** -0.5), nc)
        _store_units(dk_ref, dk, nc)
        _store_units(dv_ref, dv, nc)
        dbraw = dbeta * beta * (1.0 - beta)
        dzc = dgraw * _sigmoid(c["zc"])
        ga = dgraw * c["graw"]
        lane = c["lane"]
        lane1 = lax.broadcasted_iota(jnp.int32, (1, 128), 1)
        neg_ea = -jnp.exp(avec)
        d_alog = jnp.zeros((1, 128), F32)
        d_dt = jnp.zeros((1, 128), F32)
        for ci in range(nc):
            dbd = jnp.zeros((C, 128), F32)
            for h in range(DN_HEADS):
                b = ci * DN_HEADS + h
                dz = dzc[b] * neg_ea
                dbd = dbd + jnp.where(lane == h, dbraw[b], 0.0) + jnp.where(lane == DN_HEADS + h, dz, 0.0)
                d_alog = d_alog + jnp.where(lane1 == DN_HEADS + h, jnp.sum(ga[b], axis=0, keepdims=True), 0.0)
                d_dt = d_dt + jnp.where(lane1 == DN_HEADS + h, jnp.sum(dz, axis=0, keepdims=True), 0.0)
            dbd_ref[ci * C:(ci + 1) * C, :] = dbd
        small_ref[...] += jnp.concatenate([d_alog, d_dt, jnp.zeros((6, 128), F32)], axis=0)

    tok = lambda wd: pl.BlockSpec((nc * C, wd), lambda n: (n, 0))
    big = pl.BlockSpec((nc * DN_WIDTH, HD), lambda n: (n, 0))
    sq = pl.BlockSpec((DN_HEADS, nc * C, C), lambda n: (0, n, 0))
    vec = pl.BlockSpec((1, 128), lambda n: (0, 0))
    return _call(
        body, (qn, kn, v, bd, avec, dvec, t_inv, v_new_all, states, dstates, do_all, dvn_all),
        name="dn_post", grid=(N // nc,), comm=comm,
        in_specs=[tok(DN_WIDTH)] * 3 + [tok(128), vec, vec, sq, tok(DN_WIDTH), big, big, tok(DN_WIDTH), tok(DN_WIDTH)],
        out_specs=[tok(DN_WIDTH)] * 3 + [tok(128), pl.BlockSpec((8, 128), lambda n: (0, 0))],
        out_shape=[jax.ShapeDtypeStruct((S, DN_WIDTH), F32)] * 3 + [jax.ShapeDtypeStruct((S, 128), F32),
                                                                  jax.ShapeDtypeStruct((8, 128), F32)])


def _outproj_fwd(x, attn, dn, w_out):
    S, D = x.shape
    tm = 512

    def body(x_ref, a_ref, d_ref, w_ref, xo_ref, mix_ref):
        a = a_ref[...].astype(BF16)
        dd = d_ref[...].astype(BF16)
        mix_ref[:, 0:ATTN_WIDTH] = a
        mix_ref[:, ATTN_WIDTH:] = dd
        xo_ref[...] = x_ref[...] + _nn(a, w_ref[0:ATTN_WIDTH, :]) + _nn(dd, w_ref[ATTN_WIDTH:, :])

    tok = lambda w: pl.BlockSpec((tm, w), lambda i: (i, 0))
    return pl.pallas_call(
        body, name="outproj_fwd", grid=(S // tm,),
        in_specs=[tok(D), tok(ATTN_WIDTH), tok(DN_WIDTH), pl.BlockSpec((D, D), lambda i: (0, 0))],
        out_specs=[tok(D), tok(D)],
        out_shape=[jax.ShapeDtypeStruct((S, D), F32), jax.ShapeDtypeStruct((S, D), BF16)],
        compiler_params=_params(1),
    )(x, attn, dn, w_out)


def _outproj_bwd(dx, w_out, attn):
    S, D = dx.shape
    tm = VIEW_TILE

    def body(dx_ref, w_ref, attn_ref, da1, da4, da16, dl1, dl4, dl16, ddn_ref, dxb_ref, planes):
        d = dx_ref[...].astype(BF16)
        dxb_ref[...] = d
        da = _nt(d, w_ref[0:ATTN_WIDTH, :])
        ddn_ref[...] = _nt(d, w_ref[ATTN_WIDTH:, :])
        _tile_to_views(da, planes, (da1, da4, da16))
        lo = lax.broadcasted_iota(jnp.int32, (tm, 128), 1) < 64
        cols = []
        for G in range(4):
            sl = slice(G * 128, (G + 1) * 128)
            t = da[:, sl] * attn_ref[:, sl]
            d0 = jnp.sum(jnp.where(lo, t, 0.0), axis=-1, keepdims=True)
            d1 = jnp.sum(jnp.where(lo, 0.0, t), axis=-1, keepdims=True)
            cols.append(jnp.where(lo, d0, d1))
        _tile_to_views(jnp.concatenate(cols, axis=1), planes, (dl1, dl4, dl16))

    tok = lambda w: pl.BlockSpec((tm, w), lambda i: (i, 0))
    views = [_view_spec(d) for d in DILATIONS]
    return pl.pallas_call(
        body, name="outproj_bwd", grid=(S // tm,),
        in_specs=[tok(D), pl.BlockSpec((D, D), lambda i: (0, 0)), tok(ATTN_WIDTH)],
        out_specs=views + views + [tok(DN_WIDTH), tok(D)],
        out_shape=[_view_shape(S, d, F32) for d in DILATIONS] * 2
                  + [jax.ShapeDtypeStruct((S, DN_WIDTH), F32), jax.ShapeDtypeStruct((S, D), BF16)],
        scratch_shapes=[pltpu.VMEM((4, tm, 128), F32)],
        compiler_params=_params(1),
    )(dx, w_out, attn)


def _loss_head(x, gain, target):
    S, D = x.shape
    tm = 512

    def body(x_ref, gain_ref, t_ref, loss_ref, dx_ref, dgain_ref):
        @pl.when(pl.program_id(0) == 0)
        def _():
            loss_ref[...] = jnp.zeros_like(loss_ref)
            dgain_ref[...] = jnp.zeros_like(dgain_ref)

        xf = x_ref[...]
        gain = gain_ref[...]
        r = lax.rsqrt(jnp.mean(xf * xf, axis=-1, keepdims=True) + NORM_EPS)
        xhat = xf * r
        err = xhat * gain - t_ref[...]
        part = 0.5 * jnp.sum(jnp.mean(err * err, axis=-1, keepdims=True), axis=0, keepdims=True)
        first = (lax.broadcasted_iota(jnp.int32, (8, 128), 0) == 0) & (lax.broadcasted_iota(jnp.int32, (8, 128), 1) == 0)
        loss_ref[...] += jnp.where(first, part, 0.0)
        dy = err * (1.0 / D)
        dgain_ref[...] += jnp.sum(dy * xhat, axis=0, keepdims=True)
        dxh = dy * gain
        dx_ref[...] = r * (dxh - xhat * jnp.mean(dxh * xhat, axis=-1, keepdims=True))

    tok = pl.BlockSpec((tm, D), lambda i: (i, 0))
    row = pl.BlockSpec((1, D), lambda i: (0, 0))
    return pl.pallas_call(
        body, name="loss_head", grid=(S // tm,),
        in_specs=[tok, row, tok],
        out_specs=[pl.BlockSpec((8, 128), lambda i: (0, 0)), tok, row],
        out_shape=[jax.ShapeDtypeStruct((8, 128), F32), jax.ShapeDtypeStruct((S, D), F32),
                   jax.ShapeDtypeStruct((1, D), F32)],
        compiler_params=_params(1),
    )(x, gain, target)


def _adamw(w, g, m, v, name):
    R, Ccols = w.shape
    tr = R
    for cand in (256, 128, 64, 32, 16, 8):
        if R % cand == 0:
            tr = cand
            break
    c1 = 1.0 - ADAM_B1 ** ADAM_STEP
    c2 = 1.0 - ADAM_B2 ** ADAM_STEP

    def body(w_ref, g_ref, m_ref, v_ref, d_ref, nm_ref, nv_ref):
        gv = g_ref[...]
        mn = ADAM_B1 * m_ref[...] + (1.0 - ADAM_B1) * gv
        vn = ADAM_B2 * v_ref[...] + (1.0 - ADAM_B2) * (gv * gv)
        nm_ref[...] = mn
        nv_ref[...] = vn
        d_ref[...] = -ADAM_LR * ((mn / c1) / (jnp.sqrt(vn / c2) + ADAM_EPS) + ADAM_WD * w_ref[...])

    spec = pl.BlockSpec((tr, Ccols), lambda i: (i, 0))
    return pl.pallas_call(
        body, name=name, grid=(R // tr,), in_specs=[spec] * 4, out_specs=[spec] * 3,
        out_shape=[jax.ShapeDtypeStruct((R, Ccols), F32)] * 3, compiler_params=_params(1),
    )(w, g, m, v)


LATE_WEIGHTS = ("w_out", "ffn2_gate", "ffn2_up", "ffn2_down")


def _local_step(x, target, wts, small, dist=None):
    g1, g2, gm, gf = small["norm_ffn1"], small["norm_ffn2"], small["norm_mix"], small["norm_final"]
    wts = dict(wts)

    def reduce_start(gs, tag):
        return _rs_add_pairs(gs, _swap_sibling(gs, True, "rs_swap_halves_" + tag), dist["c"], "rs_add_pairs_" + tag)

    (x1, h1, fg1, fu1), late = _ffn_fwd(x, g1, wts["ffn1_gate"], wts["ffn1_up"], wts["ffn1_down"], "ffn1_fwd",
                                        comm=_ag_comm(dist["late"]) if dist else None)
    if dist:
        wts.update(zip(LATE_WEIGHTS, late))
        wts["w_out"] = wts["w_out"].reshape(D_MODEL, D_MODEL)
    h2, *qkv, xq, xk, xv, gate, bd = _inproj_fwd(x1, gm, wts["w_in"])
    aq, ak, av = qkv[0:3], qkv[3:6], qkv[6:9]
    parts = [_attn_fwd(aq[p], ak[p], av[p], d, f"attn_fwd_d{d}") for p, d in enumerate(DILATIONS)]
    attn, *lse = _attn_merge(parts)
    conv_w = small["conv_w"]
    qn, kn, vv = _conv_fwd(xq, xk, xv, conv_w)
    dn_u, dn_w, dn_qg, dn_kd, dn_aq, dn_t, dn_egl = _dn_prep(qn, kn, vv, bd, small["avec"], small["dvec"])
    dn, o_dn, v_new, states = _dn_scan_fwd(dn_u, dn_w, dn_qg, dn_kd, dn_aq, dn_egl, gate, small["dn_norm"])
    x2, mix = _outproj_fwd(x1, attn, dn, wts["w_out"])
    (x3, h3, fg2, fu2), _ = _ffn_fwd(x2, g2, wts["ffn2_gate"], wts["ffn2_up"], wts["ffn2_down"], "ffn2_fwd")
    loss, dx3, d_gf = _loss_head(x3, gf, target)

    grads = {}
    (dx2, d_g2, dfg2, dfu2, act2, dout2), _ = _ffn_bwd(dx3, x2, g2, fg2, fu2, wts["ffn2_down"], wts["ffn2_gate"],
                                                      wts["ffn2_up"], "ffn2_bwd")
    tk = 2048
    grads["ffn2_gate"], _ = _dw_chunks(h3, dfg2, tk, "dw_ffn2_gate")
    grads["ffn2_up"], _ = _dw_chunks(h3, dfu2, tk, "dw_ffn2_up")
    grads["ffn2_down"], _ = _dw_chunks(act2, dout2, tk, "dw_ffn2_down")
    group_a = ("ffn2_gate", "ffn2_up", "ffn2_down")
    parts_a = reduce_start([grads[n] for n in group_a], "a") if dist else None

    *dviews, ddn, dx2b = _outproj_bwd(dx2, wts["w_out"], attn)
    dattn, dd = dviews[0:3], dviews[3:6]
    grads["w_out"] = _matmul_tn(mix, dx2b, D_MODEL, tk, "dw_out").reshape(N_CHIPS, D_MODEL // N_CHIPS, D_MODEL)

    daq, dak, dav = [], [], []
    for p, d in enumerate(DILATIONS):
        daq.append(_attn_bwd_q(aq[p], ak[p], av[p], dattn[p], lse[p], dd[p], d, f"attn_bwd_q_d{d}"))
        dk_p, dv_p = _attn_bwd_kv(aq[p], ak[p], av[p], dattn[p], lse[p], dd[p], d, f"attn_bwd_kv_d{d}")
        dak.append(dk_p)
        dav.append(dv_p)

    do_dn, dvn, dgate, dstates, d_dn_gain = _dn_scan_bwd(dn_w, dn_qg, dn_kd, dn_aq, dn_egl, gate, small["dn_norm"], o_dn, ddn)
    (dqn, dkn, dvv, dbd, dn_small), recv_a = _dn_post(qn, kn, vv, bd, small["avec"], small["dvec"], dn_t, v_new, states,
                                                      dstates, do_dn, dvn, comm=_rsx_comm(parts_a) if dist else None)
    dcq, dck, dcv, dwq, dwk, dwv = _conv_bwd_pre(xq, xk, xv, conv_w, dqn, dkn, dvv)
    dxq, dxk, dxv = _conv_bwd_x(dcq, dck, dcv, conv_w)
    d_conv = jnp.concatenate([dwq[:CONV_WIDTH], dwk[:CONV_WIDTH], dwv[:CONV_WIDTH]], axis=1)

    dx1, d_gm, dproj = _inproj_bwd(dx2, x1, gm, [daq, dak, dav], [dxq, dxk, dxv, dgate], dbd, wts["w_in"])
    gi = _matmul_tn(h2, dproj, 512, 512, "dw_in")
    if dist:
        cols = IN_COLS // N_CHIPS
        gi = jnp.concatenate([gi[:, :3072], gi[:, 3584:3592], gi[:, 3072:3584]], axis=1)
        gi = jnp.stack([gi[:, j * cols:(j + 1) * cols] for j in range(N_CHIPS)])
    grads["w_in"] = gi
    group_b = ("w_in", "w_out")
    parts_b = reduce_start([grads[n] for n in group_b], "b") if dist else None

    (dx0, d_g1, dfg1, dfu1, act1, dout1), _ = _ffn_bwd(dx1, x, g1, fg1, fu1, wts["ffn1_down"], wts["ffn1_gate"],
                                                      wts["ffn1_up"], "ffn1_bwd")
    group_c = ("ffn1_gate", "ffn1_up", "ffn1_down")
    pending = parts_b if dist else []
    parts_c, recv_bc = [], []
    for n, (lhs, rhs) in zip(group_c, ((h1, dfg1), (h1, dfu1), (act1, dout1))):
        grads[n], landed = _dw_chunks(lhs, rhs, tk, "dw_" + n, comm=_rsx_comm(pending) if dist else None)
        recv_bc += list(landed)
        if dist:
            pending = reduce_start([grads[n]], n)
            parts_c += pending

    small_grads = dict(norm_ffn1=d_g1, norm_mix=d_gm, norm_ffn2=d_g2, norm_final=d_gf, conv_w=d_conv,
                       a_log=dn_small[0:1], dt_bias=dn_small[1:2], dn_norm=d_dn_gain[0:1])
    if dist:
        recv_bc += _rs_exchange_arrays(pending)
        recv_b, recv_c = recv_bc[:len(parts_b)], recv_bc[len(parts_b):]
        names = group_a + group_b + group_c
        totals = _rs_add_totals(list(parts_a) + list(parts_b) + list(parts_c), list(recv_a) + list(recv_b) + list(recv_c),
                                dist["chip"])
        theirs = _swap_sibling(totals, False, "rs_share_total")
        grads = {n: (mine, other) for n, mine, other in zip(names, totals, theirs)}
    return loss, dx0, grads, small_grads


HBM =pl.BlockSpec(memory_space=pl.ANY)
VMEM_SPEC = pl.BlockSpec(memory_space=pltpu.VMEM)


def _coords():
    return lax.axis_index("x"), lax.axis_index("y"), lax.axis_index("c")


def _remote(src, dst, send_sems, recv_sems, k, dev):
    return pltpu.make_async_remote_copy(src_ref=src, dst_ref=dst, send_sem=send_sems.at[k], recv_sem=recv_sems.at[k],
                                        device_id=dev, device_id_type=MESH)


def _allreduce_small(buf, name):
    R, Cc = buf.shape

    def body(src_ref, out_ref, recv_ref, send_sems, recv_sems):
        x, y, c = _coords()
        copies = []
        for m in range(1, 8):
            fx, fy, fc = (m >> 2) & 1, (m >> 1) & 1, m & 1
            dev = (x ^ fx if fx else x, y ^ fy if fy else y, c ^ fc if fc else c)
            cp = _remote(src_ref, recv_ref.at[m - 1], send_sems, recv_sems, m - 1, dev)
            cp.start()
            copies.append(cp)
        for cp in copies:
            cp.wait()
        r = [src_ref[...]] + [recv_ref[m] for m in range(7)]
        out_ref[...] = ((r[0] + r[1]) + (r[2] + r[3])) + ((r[4] + r[5]) + (r[6] + r[7]))

    return pl.pallas_call(
        body, name=name, out_shape=jax.ShapeDtypeStruct((R, Cc), F32),
        in_specs=[VMEM_SPEC], out_specs=VMEM_SPEC,
        scratch_shapes=[pltpu.VMEM((7, R, Cc), F32), pltpu.SemaphoreType.DMA((7,)), pltpu.SemaphoreType.DMA((7,))],
    )(buf)


BIG = ("ffn1_gate", "ffn1_up", "ffn1_down", "w_in", "w_out", "ffn2_gate", "ffn2_up", "ffn2_down")


def _rows(ref, start, size):
    return ref.at[pl.ds(pl.multiple_of(start, 16), size)]


def _allgather_arrays(shards):
    n = len(shards)

    def body(*refs):
        _ag_start(refs[:n], refs[n:2 * n], refs[2 * n], refs[2 * n + 1])
        _ag_finish(refs[:n], refs[n:2 * n], refs[2 * n], refs[2 * n + 1])

    _, shapes, n_sems, _, _ = _ag_comm(shards)
    return pl.pallas_call(
        body, name="allgather_weights", out_shape=shapes, in_specs=[HBM] * n, out_specs=[HBM] * n,
        scratch_shapes=[pltpu.SemaphoreType.DMA((n_sems,)), pltpu.SemaphoreType.DMA((n_sems,))],
    )(*shards)


def _ag_copies(srcs, outs, send_sems, recv_sems):
    x, y, c = _coords()
    sib = (x, y, 1 - c)
    me = 2 * x + y
    others = [(1 - x, y), (x, 1 - y), (1 - x, 1 - y)]
    plan = []
    for a, (src, out) in enumerate(zip(srcs, outs)):
        h = src.shape[0] // 2
        cp = lambda s, d, k, dev: _remote(s, d, send_sems, recv_sems, 7 * a + k, dev)
        own = cp(src, out.at[me], 6, sib)
        sends = [cp(_rows(src, c * h, h), _rows(out.at[me], c * h, h), j, (ox, oy, c)) for j, (ox, oy) in enumerate(others)]
        mine = [_rows(out.at[2 * ox + oy], c * h, h) for ox, oy in others]
        theirs = [_rows(out.at[2 * ox + oy], (1 - c) * h, h) for ox, oy in others]
        arrivals = [cp(m, m, j, sib) for j, m in enumerate(mine)]
        forwards = [cp(m, m, 3 + j, sib) for j, m in enumerate(mine)]
        forwarded = [cp(t, t, 3 + j, sib) for j, t in enumerate(theirs)]
        plan.append((own, sends, forwards, arrivals, forwarded))
    return plan


def _ag_start(srcs, outs, send_sems, recv_sems):
    for own, sends, _, _, _ in _ag_copies(srcs, outs, send_sems, recv_sems):
        own.start()
        for cp in sends:
            cp.start()


def _ag_finish(srcs, outs, send_sems, recv_sems):
    plan = _ag_copies(srcs, outs, send_sems, recv_sems)
    for _, _, forwards, arrivals, _ in plan:
        for arrived, fwd in zip(arrivals, forwards):
            arrived.wait_recv()
            fwd.start()
    for own, sends, forwards, _, forwarded in plan:
        for cp in forwarded:
            cp.wait_recv()
        own.wait_recv()
        for cp in [own] + sends + forwards:
            cp.wait_send()


def _ag_comm(shards):
    shapes = [jax.ShapeDtypeStruct((N_CHIPS,) + s.shape, s.dtype) for s in shards]
    return (list(shards), shapes, 7 * len(shards), _ag_start, _ag_finish)


def _swap_sibling(arrs, pick_other_half, name):
    n = len(arrs)
    outs = [jax.ShapeDtypeStruct((a.shape[0], a.shape[1] // 2) + a.shape[2:] if pick_other_half else a.shape, a.dtype) for a in arrs]

    def body(*refs):
        srcs, dsts, send_sems, recv_sems = refs[:n], refs[n:2 * n], refs[2 * n], refs[2 * n + 1]
        x, y, c = _coords()
        cps = []
        for a in range(n):
            src = srcs[a]
            if pick_other_half:
                h = src.shape[1] // 2
                src = src.at[:, pl.ds(pl.multiple_of((1 - c) * h, 16), h)]
            cp = _remote(src, dsts[a], send_sems, recv_sems, a, (x, y, 1 - c))
            cp.start()
            cps.append(cp)
        for cp in cps:
            cp.wait()

    return pl.pallas_call(
        body, name=name, out_shape=outs, in_specs=[HBM] * n, out_specs=[HBM] * n,
        scratch_shapes=[pltpu.SemaphoreType.DMA((n,)), pltpu.SemaphoreType.DMA((n,))],
    )(*arrs)


def _rs_add_pairs(gs, others, c, name):
    n = len(gs)
    blocks = [(g.shape[1] // 4, g.shape[2]) for g in gs]

    def body(c_ref, *refs):
        for a in range(n):
            refs[2 * n + a][...] = (refs[a][...] + refs[n + a][...]).astype(BF16)

    mine = lambda b: pl.BlockSpec((None,) + b, lambda j, s, c_ref: (j, c_ref[0] * 2 + s, 0))
    flat = lambda b: pl.BlockSpec((None,) + b, lambda j, s, c_ref: (j, s, 0))
    return pl.pallas_call(
        body, name=name,
        grid_spec=pltpu.PrefetchScalarGridSpec(
            num_scalar_prefetch=1, grid=(N_CHIPS, 2),
            in_specs=[mine(b) for b in blocks] + [flat(b) for b in blocks],
            out_specs=[flat(b) for b in blocks]),
        out_shape=[jax.ShapeDtypeStruct(o.shape, BF16) for o in others],
        compiler_params=_params(2),
    )(c, *gs, *others)


def _rs_exchange_arrays(parts):
    n = len(parts)

    def body(*refs):
        _rsx_start(refs[:n], refs[n:2 * n], refs[2 * n], refs[2 * n + 1])
        _rsx_finish(refs[:n], refs[n:2 * n], refs[2 * n], refs[2 * n + 1])

    _, shapes, n_sems, _, _ = _rsx_comm(parts)
    return pl.pallas_call(
        body, name="rs_exchange_chips", out_shape=shapes, in_specs=[HBM] * n, out_specs=[HBM] * n,
        scratch_shapes=[pltpu.SemaphoreType.DMA((n_sems,)), pltpu.SemaphoreType.DMA((n_sems,))],
    )(*parts)


def _rsx_copies(srcs, dsts, send_sems, recv_sems):
    x, y, c = _coords()
    others = [(1 - x, y), (x, 1 - y), (1 - x, 1 - y)]
    return [_remote(src.at[2 * ox + oy], dst.at[k], send_sems, recv_sems, 3 * a + k, (ox, oy, c))
            for a, (src, dst) in enumerate(zip(srcs, dsts)) for k, (ox, oy) in enumerate(others)]


def _rsx_start(srcs, dsts, send_sems, recv_sems):
    for cp in _rsx_copies(srcs, dsts, send_sems, recv_sems):
        cp.start()


def _rsx_finish(srcs, dsts, send_sems, recv_sems):
    for cp in _rsx_copies(srcs, dsts, send_sems, recv_sems):
        cp.wait()


def _rsx_comm(parts):
    shapes = [jax.ShapeDtypeStruct((3,) + p.shape[1:], p.dtype) for p in parts]
    return (list(parts), shapes, 3 * len(parts), _rsx_start, _rsx_finish)


def _rs_add_totals(parts, recvs, chip):
    n = len(parts)
    blocks = [(p.shape[1] // 2, p.shape[2]) for p in parts]

    def body(chip_ref, *refs):
        f = lambda r: r[...].astype(F32)
        for a in range(n):
            p, r0, r1, r2 = refs[a], refs[n + 3 * a], refs[n + 3 * a + 1], refs[n + 3 * a + 2]
            refs[4 * n + a][...] = (f(p) + f(r0)) + (f(r1) + f(r2))

    own = lambda b: pl.BlockSpec((None,) + b, lambda s, chip_ref: (chip_ref[0], s, 0))
    slot = lambda b, k: pl.BlockSpec((None,) + b, lambda s, chip_ref, k=k: (k, s, 0))
    recv_specs = [slot(b, k) for b in blocks for k in range(3)]
    recv_args = [r for r in recvs for _ in range(3)]
    return pl.pallas_call(
        body, name="rs_add_totals",
        grid_spec=pltpu.PrefetchScalarGridSpec(
            num_scalar_prefetch=1, grid=(2,),
            in_specs=[own(b) for b in blocks] + recv_specs,
            out_specs=[pl.BlockSpec(b, lambda s, chip_ref: (s, 0)) for b in blocks]),
        out_shape=[jax.ShapeDtypeStruct(p.shape[1:], F32) for p in parts],
        compiler_params=_params(1),
    )(chip, *parts, *recv_args)


def _permute_w_in(w):
    return jnp.concatenate([w[:, :3072], w[:, 3080:IN_COLS], w[:, 3072:3080],
                            jnp.zeros((w.shape[0], IN_COLS_PADDED - IN_COLS), w.dtype)], axis=1)


def _pad_row(v):
    v = v.reshape(1, -1)
    return jnp.pad(v, ((0, 0), (0, D_MODEL - v.shape[1])))


def kernel(x, norm_ffn1, ffn1_gate, ffn1_up, ffn1_down, norm_mix, w_in, conv_w, a_log, dt_bias, dn_norm, w_out, norm_ffn2, ffn2_gate, ffn2_up, ffn2_down, norm_final, loss_target, m_norm_ffn1, m_ffn1_gate, m_ffn1_up, m_ffn1_down, m_norm_mix, m_w_in, m_conv_w, m_a_log, m_dt_bias, m_dn_norm, m_w_out, m_norm_ffn2, m_ffn2_gate, m_ffn2_up, m_ffn2_down, m_norm_final, v_norm_ffn1, v_ffn1_gate, v_ffn1_up, v_ffn1_down, v_norm_mix, v_w_in, v_conv_w, v_a_log, v_dt_bias, v_dn_norm, v_w_out, v_norm_ffn2, v_ffn2_gate, v_ffn2_up, v_ffn2_down, v_norm_final):
    cx, cy, cc = _coords()
    chip = 2 * cx + cy
    big_w = dict(ffn1_gate=ffn1_gate[0], ffn1_up=ffn1_up[0], ffn1_down=ffn1_down[0], w_in=w_in[0], w_out=w_out[0],
                 ffn2_gate=ffn2_gate[0], ffn2_up=ffn2_up[0], ffn2_down=ffn2_down[0])
    big_m = dict(ffn1_gate=m_ffn1_gate[0], ffn1_up=m_ffn1_up[0], ffn1_down=m_ffn1_down[0], w_in=m_w_in[0], w_out=m_w_out[0],
                 ffn2_gate=m_ffn2_gate[0], ffn2_up=m_ffn2_up[0], ffn2_down=m_ffn2_down[0])
    big_v = dict(ffn1_gate=v_ffn1_gate[0], ffn1_up=v_ffn1_up[0], ffn1_down=v_ffn1_down[0], w_in=v_w_in[0], w_out=v_w_out[0],
                 ffn2_gate=v_ffn2_gate[0], ffn2_up=v_ffn2_up[0], ffn2_down=v_ffn2_down[0])

    early = tuple(n for n in BIG if n not in LATE_WEIGHTS)
    wts = dict(zip(early, _allgather_arrays([big_w[n].astype(BF16) for n in early])))
    wts["w_in"] = _permute_w_in(jnp.concatenate([wts["w_in"][j] for j in range(N_CHIPS)], axis=1))
    dist = dict(late=[big_w[n].astype(BF16) for n in LATE_WEIGHTS], c=cc.reshape(1).astype(jnp.int32),
                chip=chip.reshape(1).astype(jnp.int32))

    conv_shard = conv_w[0]
    emb = jnp.concatenate([jnp.where((chip == j) & (cc == 0), conv_shard, 0.0) for j in range(N_CHIPS)], axis=1)
    emb = jnp.pad(emb.reshape(6, D_MODEL), ((0, 2), (0, 0)))
    conv_full = _allreduce_small(emb, "allgather_conv_w")[:6].reshape(CONV_WIDTH, 3 * DN_WIDTH)

    zvec = jnp.zeros((1, 128), F32)
    small = dict(norm_ffn1=norm_ffn1, norm_mix=norm_mix, norm_ffn2=norm_ffn2, norm_final=norm_final[None],
                 conv_w=conv_full, avec=zvec.at[0, DN_HEADS:2 * DN_HEADS].set(a_log[0]),
                 dvec=zvec.at[0, DN_HEADS:2 * DN_HEADS].set(dt_bias[0]), dn_norm=dn_norm)

    loss, grad_x, reduced, sg = _local_step(x[0], loss_target[0], wts, small, dist)

    rows = [sg["norm_ffn1"], sg["norm_mix"], sg["norm_ffn2"], sg["norm_final"], _pad_row(sg["a_log"]), _pad_row(sg["dt_bias"]),
            _pad_row(sg["dn_norm"]), _pad_row(loss[0:1]), sg["conv_w"].reshape(6, D_MODEL), jnp.zeros((2, D_MODEL), F32)]
    red = _allreduce_small(jnp.concatenate(rows, axis=0), "allreduce_small")
    loss_out = red[7, 0]
    g_conv_full = red[8:14].reshape(CONV_WIDTH, 3 * DN_WIDTH)
    g_conv = lax.dynamic_slice_in_dim(g_conv_full, chip * (3 * DN_WIDTH // N_CHIPS), 3 * DN_WIDTH // N_CHIPS, axis=1)
    g_small = dict(norm_ffn1=red[0:1], norm_mix=red[1:2], norm_ffn2=red[2:3], norm_final=red[3],
                   a_log=red[4:5, DN_HEADS:2 * DN_HEADS], dt_bias=red[5:6, DN_HEADS:2 * DN_HEADS], dn_norm=red[6:7, :DN_HEAD_DIM])

    shard_g = {}
    for n in BIG:
        mine, other = reduced[n]
        shard_g[n] = jnp.where(cc == 0, jnp.concatenate([mine, other], axis=0), jnp.concatenate([other, mine], axis=0))

    out_g, out_d, out_m, out_v = {}, {}, {}, {}
    for n in BIG:
        d, nm, nv = _adamw(big_w[n], shard_g[n], big_m[n], big_v[n], "adamw_" + n)
        out_g[n], out_d[n], out_m[n], out_v[n] = shard_g[n][None], d[None], nm[None], nv[None]
    d, nm, nv = _adamw(conv_w[0], g_conv, m_conv_w[0], v_conv_w[0], "adamw_conv_w")
    out_g["conv_w"], out_d["conv_w"], out_m["conv_w"], out_v["conv_w"] = g_conv[None], d[None], nm[None], nv[None]

    small_names = ("norm_ffn1", "norm_mix", "norm_ffn2", "norm_final", "a_log", "dt_bias", "dn_norm")
    small_w = dict(norm_ffn1=norm_ffn1, norm_mix=norm_mix, norm_ffn2=norm_ffn2, norm_final=norm_final, a_log=a_log,
                   dt_bias=dt_bias, dn_norm=dn_norm)
    small_m = dict(norm_ffn1=m_norm_ffn1, norm_mix=m_norm_mix, norm_ffn2=m_norm_ffn2, norm_final=m_norm_final, a_log=m_a_log,
                   dt_bias=m_dt_bias, dn_norm=m_dn_norm)
    small_v = dict(norm_ffn1=v_norm_ffn1, norm_mix=v_norm_mix, norm_ffn2=v_norm_ffn2, norm_final=v_norm_final, a_log=v_a_log,
                   dt_bias=v_dt_bias, dn_norm=v_dn_norm)
    stack = lambda dct: jnp.concatenate([_pad_row(dct[n]) for n in small_names] + [jnp.zeros((1, D_MODEL), F32)], axis=0)
    d, nm, nv = _adamw(stack(small_w), stack(g_small), stack(small_m), stack(small_v), "adamw_small")
    for k, n in enumerate(small_names):
        shape = small_w[n].shape
        size = math.prod(shape)
        out_g[n] = g_small[n].reshape(shape)
        out_d[n], out_m[n], out_v[n] = (t[k, :size].reshape(shape) for t in (d, nm, nv))

    order = ("norm_ffn1", "ffn1_gate", "ffn1_up", "ffn1_down", "norm_mix", "w_in", "conv_w", "a_log", "dt_bias", "dn_norm",
             "w_out", "norm_ffn2", "ffn2_gate", "ffn2_up", "ffn2_down", "norm_final")
    return (loss_out, grad_x[None], *[out_g[n] for n in order], *[out_d[n] for n in order],
            *[out_m[n] for n in order], *[out_v[n] for n in order])
```

```python
import functools
import math

import jax
import jax.numpy as jnp
from jax import lax
from jax.experimental import pallas as pl
from jax.experimental.pallas import tpu as pltpu

F32 = jnp.float32
BF16 = jnp.bfloat16
HI = lax.Precision.HIGH

D_MODEL = 1024
ATTN_HEADS = 8
ATTN_WIDTH = 512
ATTN_BLOCK = 128
DILATIONS = (1, 4, 16)
DN_HEADS = 4
DN_HEAD_DIM = 128
DN_WIDTH = 512
DN_CHUNK = 64
CONV_WIDTH = 4
NORM_EPS = 1e-6
L2_EPS = 1e-6
IN_COLS = 3592
IN_COLS_PADDED = 3712
N_CHIPS = 4

ADAM_LR = 0.001
ADAM_B1 = 0.9
ADAM_B2 = 0.999
ADAM_EPS = 1e-08
ADAM_WD = 0.01
ADAM_STEP = 10

VMEM_LIMIT = 56 * 1024 * 1024
NEG_BIG = -1e30
MESH = pl.DeviceIdType.MESH


def _params(n_grid, vmem=VMEM_LIMIT):
    return pltpu.CompilerParams(dimension_semantics=("arbitrary",) * n_grid, vmem_limit_bytes=vmem)


def _call(body, args, *, name, grid, in_specs, out_specs, out_shape, scratch_shapes=(), comm=None):
    n_in, n_out, n_scr = len(in_specs), len(out_specs), len(scratch_shapes)
    hbm = pl.BlockSpec(memory_space=pl.ANY)
    srcs, dst_shapes, n_sems, start, finish = comm if comm is not None else ((), (), 0, None, None)
    ns, nd = len(srcs), len(dst_shapes)

    def full(*refs):
        ins, c_src = refs[:n_in], refs[n_in:n_in + ns]
        at = n_in + ns
        outs, c_dst = refs[at:at + n_out], refs[at + n_out:at + n_out + nd]
        scr = refs[at + n_out + nd:at + n_out + nd + n_scr]
        if comm is not None:
            ids = [pl.program_id(a) for a in range(len(grid))]
            first = functools.reduce(jnp.logical_and, [i == 0 for i in ids])
            last = functools.reduce(jnp.logical_and, [i == g - 1 for i, g in zip(ids, grid)])

            @pl.when(first)
            def _():
                start(c_src, c_dst, refs[-2], refs[-1])

        body(*ins, *outs, *scr)
        if comm is not None:
            @pl.when(last)
            def _():
                finish(c_src, c_dst, refs[-2], refs[-1])

    sems = [pltpu.SemaphoreType.DMA((n_sems,)), pltpu.SemaphoreType.DMA((n_sems,))] if comm is not None else []
    res = pl.pallas_call(
        full, name=name, grid=grid, in_specs=list(in_specs) + [hbm] * ns, out_specs=list(out_specs) + [hbm] * nd,
        out_shape=list(out_shape) + list(dst_shapes), scratch_shapes=list(scratch_shapes) + sems,
        compiler_params=_params(len(grid)),
    )(*args, *srcs)
    return res[:n_out], res[n_out:]


def _nt(a, b, precision=None):
    return lax.dot_general(a, b, (((1,), (1,)), ((), ())), preferred_element_type=F32, precision=precision)


def _tn(a, b, precision=None):
    return lax.dot_general(a, b, (((0,), (0,)), ((), ())), preferred_element_type=F32, precision=precision)


def _nn(a, b, precision=None):
    return jnp.dot(a, b, preferred_element_type=F32, precision=precision)


def _sigmoid(x):
    return 1.0 / (1.0 + jnp.exp(-x))


def _ffn_fwd(x, gain, wg, wu, wd, name, comm=None):
    S, D = x.shape
    nf, _, tf = wg.shape
    tm = 512

    def body(x_ref, gain_ref, wg_ref, wu_ref, wd_ref, xo_ref, h_ref, g_ref, u_ref, acc_ref, hs_ref):
        j = pl.program_id(1)

        @pl.when(j == 0)
        def _():
            xf = x_ref[...]
            r = lax.rsqrt(jnp.mean(xf * xf, axis=-1, keepdims=True) + NORM_EPS)
            h = (xf * r * gain_ref[...]).astype(BF16)
            hs_ref[...] = h
            h_ref[...] = h
            acc_ref[...] = jnp.zeros_like(acc_ref)

        h = hs_ref[...]
        g = _nn(h, wg_ref[...])
        u = _nn(h, wu_ref[...])
        g_ref[...] = g.astype(BF16)
        u_ref[...] = u.astype(BF16)
        act = g * _sigmoid(g) * u
        acc_ref[...] += _nn(act.astype(BF16), wd_ref[...])

        @pl.when(j == nf - 1)
        def _():
            xo_ref[...] = x_ref[...] + 0.5 * acc_ref[...]

    return _call(
        body, (x, gain, wg, wu, wd), name=name, grid=(S // tm, nf), comm=comm,
        in_specs=[pl.BlockSpec((tm, D), lambda i, j: (i, 0)),
                  pl.BlockSpec((1, D), lambda i, j: (0, 0)),
                  pl.BlockSpec((None, D, tf), lambda i, j: (j, 0, 0)),
                  pl.BlockSpec((None, D, tf), lambda i, j: (j, 0, 0)),
                  pl.BlockSpec((None, tf, D), lambda i, j: (j, 0, 0))],
        out_specs=[pl.BlockSpec((tm, D), lambda i, j: (i, 0)),
                   pl.BlockSpec((tm, D), lambda i, j: (i, 0)),
                   pl.BlockSpec((None, tm, tf), lambda i, j: (j, i, 0)),
                   pl.BlockSpec((None, tm, tf), lambda i, j: (j, i, 0))],
        out_shape=[jax.ShapeDtypeStruct((S, D), F32), jax.ShapeDtypeStruct((S, D), BF16),
                   jax.ShapeDtypeStruct((nf, S, tf), BF16), jax.ShapeDtypeStruct((nf, S, tf), BF16)],
        scratch_shapes=[pltpu.VMEM((tm, D), F32), pltpu.VMEM((tm, D), BF16)])


def _rmsnorm_bwd(dh, xf, gain):
    r = lax.rsqrt(jnp.mean(xf * xf, axis=-1, keepdims=True) + NORM_EPS)
    xhat = xf * r
    dgain = jnp.sum(dh * xhat, axis=0, keepdims=True)
    dxh = dh * gain
    dx = r * (dxh - xhat * jnp.mean(dxh * xhat, axis=-1, keepdims=True))
    return dx, dgain


def _ffn_bwd(dxo, x, gain, g, u, wd, wg, wu, name, comm=None):
    S, D = x.shape
    nf, _, tf = g.shape
    tm = 512

    def body(dxo_ref, x_ref, gain_ref, g_ref, u_ref, wd_ref, wg_ref, wu_ref,
             dx_ref, dgain_ref, dg_ref, du_ref, act_ref, dout_ref, acc_ref, ds_ref):
        i = pl.program_id(0)
        j = pl.program_id(1)

        @pl.when(j == 0)
        def _():
            d = (0.5 * dxo_ref[...]).astype(BF16)
            ds_ref[...] = d
            dout_ref[...] = d
            acc_ref[...] = jnp.zeros_like(acc_ref)

        @pl.when((i == 0) & (j == 0))
        def _():
            dgain_ref[...] = jnp.zeros_like(dgain_ref)

        for half in range(2):
            rows = slice(half * (tm // 2), (half + 1) * (tm // 2))
            dact = _nt(ds_ref[rows, :], wd_ref[...])
            gv = g_ref[rows, :].astype(F32)
            uv = u_ref[rows, :].astype(F32)
            sg = _sigmoid(gv)
            silu = gv * sg
            act_ref[rows, :] = (silu * uv).astype(BF16)
            dgv = (dact * uv * (sg * (1.0 + gv * (1.0 - sg)))).astype(BF16)
            duv = (dact * silu).astype(BF16)
            dg_ref[rows, :] = dgv
            du_ref[rows, :] = duv
            acc_ref[rows, :] += _nt(dgv, wg_ref[...]) + _nt(duv, wu_ref[...])

        @pl.when(j == nf - 1)
        def _():
            dx, dgain = _rmsnorm_bwd(acc_ref[...], x_ref[...], gain_ref[...])
            dx_ref[...] = dxo_ref[...] + dx
            dgain_ref[...] += dgain

    return _call(
        body, (dxo, x, gain, g, u, wd, wg, wu), name=name, grid=(S // tm, nf), comm=comm,
        in_specs=[pl.BlockSpec((tm, D), lambda i, j: (i, 0)),
                  pl.BlockSpec((tm, D), lambda i, j: (i, 0)),
                  pl.BlockSpec((1, D), lambda i, j: (0, 0)),
                  pl.BlockSpec((None, tm, tf), lambda i, j: (j, i, 0)),
                  pl.BlockSpec((None, tm, tf), lambda i, j: (j, i, 0)),
                  pl.BlockSpec((None, tf, D), lambda i, j: (j, 0, 0)),
                  pl.BlockSpec((None, D, tf), lambda i, j: (j, 0, 0)),
                  pl.BlockSpec((None, D, tf), lambda i, j: (j, 0, 0))],
        out_specs=[pl.BlockSpec((tm, D), lambda i, j: (i, 0)),
                   pl.BlockSpec((1, D), lambda i, j: (0, 0)),
                   pl.BlockSpec((None, tm, tf), lambda i, j: (j, i, 0)),
                   pl.BlockSpec((None, tm, tf), lambda i, j: (j, i, 0)),
                   pl.BlockSpec((None, tm, tf), lambda i, j: (j, i, 0)),
                   pl.BlockSpec((tm, D), lambda i, j: (i, 0))],
        out_shape=[jax.ShapeDtypeStruct((S, D), F32), jax.ShapeDtypeStruct((1, D), F32),
                   jax.ShapeDtypeStruct((nf, S, tf), BF16), jax.ShapeDtypeStruct((nf, S, tf), BF16),
                   jax.ShapeDtypeStruct((nf, S, tf), BF16), jax.ShapeDtypeStruct((S, D), BF16)],
        scratch_shapes=[pltpu.VMEM((tm, D), F32), pltpu.VMEM((tm, D), BF16)])


def _matmul_tn(a, b, tm, tk, name):
    K, M = a.shape
    N = b.shape[1]

    def body(a_ref, b_ref, o_ref):
        @pl.when(pl.program_id(1) == 0)
        def _():
            o_ref[...] = jnp.zeros_like(o_ref)

        o_ref[...] += _tn(a_ref[...], b_ref[...])

    return pl.pallas_call(
        body, name=name, grid=(M // tm, K // tk),
        in_specs=[pl.BlockSpec((tk, tm), lambda i, k: (k, i)),
                  pl.BlockSpec((tk, N), lambda i, k: (k, 0))],
        out_specs=pl.BlockSpec((tm, N), lambda i, k: (i, 0)),
        out_shape=jax.ShapeDtypeStruct((M, N), F32),
        compiler_params=_params(2),
    )(a, b)


def _dw_chunks(a, b, tk, name, comm=None):
    nf, S, tf = a.shape
    N = b.shape[1]

    def body(a_ref, b_ref, o_ref):
        @pl.when(pl.program_id(1) == 0)
        def _():
            o_ref[...] = jnp.zeros_like(o_ref)

        o_ref[...] += _tn(a_ref[...], b_ref[...])

    (out,), landed = _call(
        body, (a, b), name=name, grid=(nf, S // tk), comm=comm,
        in_specs=[pl.BlockSpec((None, tk, tf), lambda j, k: (j, k, 0)),
                  pl.BlockSpec((tk, N), lambda j, k: (k, 0))],
        out_specs=[pl.BlockSpec((None, tf, N), lambda j, k: (j, 0, 0))],
        out_shape=[jax.ShapeDtypeStruct((nf, tf, N), F32)])
    return out, landed


VIEW_TILE = 512


def _view_spec(d, tile=VIEW_TILE):
    return pl.BlockSpec((tile // d, d * ATTN_WIDTH), lambda i: (i, 0))


def _view_shape(S, d, dtype):
    return jax.ShapeDtypeStruct((S // d, d * ATTN_WIDTH), dtype)


def _tile_to_views(val, planes, out_refs):
    for g in range(4):
        planes[g] = val[:, g * 128:(g + 1) * 128]
    for d, ref in zip(DILATIONS, out_refs):
        if d == 1:
            ref[...] = val.astype(ref.dtype)
            continue
        for r in range(d):
            for g in range(4):
                ref[:, r * ATTN_WIDTH + g * 128:r * ATTN_WIDTH + (g + 1) * 128] = (
                    planes[g, pl.ds(r, planes.shape[1] // d, stride=d), :].astype(ref.dtype))


def _view_to_tile(ref, d, planes):
    if d == 1:
        return ref[...].astype(F32)
    for r in range(d):
        for g in range(4):
            planes[g, pl.ds(r, planes.shape[1] // d, stride=d), :] = (
                ref[:, r * ATTN_WIDTH + g * 128:r * ATTN_WIDTH + (g + 1) * 128].astype(F32))
    return jnp.concatenate([planes[g] for g in range(4)], axis=1)


def _inproj_fwd(x, gain, w_in_p):
    S, D = x.shape
    tm = VIEW_TILE
    W = ATTN_WIDTH

    def body(x_ref, gain_ref, w_ref, h_ref, q1, q4, q16, k1, k4, k16, v1, v4, v16, dq_ref, dk_ref, dv_ref, gate_ref, bd_ref,
             planes):
        xf = x_ref[...]
        r = lax.rsqrt(jnp.mean(xf * xf, axis=-1, keepdims=True) + NORM_EPS)
        h = (xf * r * gain_ref[...]).astype(BF16)
        h_ref[...] = h
        _tile_to_views(_nn(h, w_ref[:, 0:W]) * 0.125, planes, (q1, q4, q16))
        _tile_to_views(_nn(h, w_ref[:, W:2 * W]), planes, (k1, k4, k16))
        _tile_to_views(_nn(h, w_ref[:, 2 * W:3 * W]), planes, (v1, v4, v16))
        dq_ref[...] = _nn(h, w_ref[:, 3 * W:4 * W])
        dk_ref[...] = _nn(h, w_ref[:, 4 * W:5 * W])
        dv_ref[...] = _nn(h, w_ref[:, 5 * W:6 * W])
        gate_ref[...] = _nn(h, w_ref[:, 6 * W:7 * W])
        bd_ref[...] = _nn(h, w_ref[:, 7 * W:7 * W + 128])

    tok = lambda w: pl.BlockSpec((tm, w), lambda i: (i, 0))
    return pl.pallas_call(
        body, name="inproj_fwd", grid=(S // tm,),
        in_specs=[tok(D), pl.BlockSpec((1, D), lambda i: (0, 0)),
                  pl.BlockSpec((D, IN_COLS_PADDED), lambda i: (0, 0))],
        out_specs=[tok(D)] + [_view_spec(d) for d in DILATIONS] * 3 + [tok(W)] * 4 + [tok(128)],
        out_shape=[jax.ShapeDtypeStruct((S, D), BF16)] + [_view_shape(S, d, BF16) for d in DILATIONS] * 3
                  + [jax.ShapeDtypeStruct((S, W), F32)] * 4 + [jax.ShapeDtypeStruct((S, 128), F32)],
        scratch_shapes=[pltpu.VMEM((4, tm, 128), F32)],
        compiler_params=_params(1),
    )(x, gain, w_in_p)


def _inproj_bwd(dxo, x, gain, attn_grads, dsecs, dbd, w_in_p):
    S, D = x.shape
    tm = VIEW_TILE
    W = ATTN_WIDTH

    def body(dxo_ref, x_ref, gain_ref, *rest):
        views, (s3, s4, s5, s6, dbd_ref, w_ref, dx_ref, dgain_ref, dproj_ref, planes) = rest[:9], rest[9:]

        @pl.when(pl.program_id(0) == 0)
        def _():
            dgain_ref[...] = jnp.zeros_like(dgain_ref)

        secs = []
        for k in range(3):
            parts = [_view_to_tile(views[3 * k + p], d, planes) for p, d in enumerate(DILATIONS)]
            secs.append(parts[0] + parts[1] + parts[2])
        secs += [s3[...], s4[...], s5[...], s6[...]]
        dh = jnp.zeros((tm, D), F32)
        for k, s in enumerate(secs):
            d = s.astype(BF16)
            dproj_ref[:, k * W:(k + 1) * W] = d
            dh += _nt(d, w_ref[:, k * W:(k + 1) * W])
        d = dbd_ref[...].astype(BF16)
        dproj_ref[:, 7 * W:7 * W + 128] = d
        dh += _nt(d, w_ref[:, 7 * W:7 * W + 128])
        dx, dgain = _rmsnorm_bwd(dh, x_ref[...], gain_ref[...])
        dx_ref[...] = dxo_ref[...] + dx
        dgain_ref[...] += dgain

    tok = lambda w: pl.BlockSpec((tm, w), lambda i: (i, 0))
    return pl.pallas_call(
        body, name="inproj_bwd", grid=(S // tm,),
        in_specs=[tok(D), tok(D), pl.BlockSpec((1, D), lambda i: (0, 0))] + [_view_spec(d, tm) for d in DILATIONS] * 3
                 + [tok(W)] * 4 + [tok(128)] + [pl.BlockSpec((D, IN_COLS_PADDED), lambda i: (0, 0))],
        out_specs=[tok(D), pl.BlockSpec((1, D), lambda i: (0, 0)), tok(IN_COLS_PADDED)],
        out_shape=[jax.ShapeDtypeStruct((S, D), F32), jax.ShapeDtypeStruct((1, D), F32),
                   jax.ShapeDtypeStruct((S, IN_COLS_PADDED), BF16)],
        scratch_shapes=[pltpu.VMEM((4, tm, 128), F32)],
        compiler_params=_params(1),
    )(dxo, x, gain, *[g for grads in attn_grads for g in grads], *dsecs, dbd, w_in_p)


def _slope(h):
    return 2.0 ** (-8.0 * (h + 1) / ATTN_HEADS)


def _head_bias(steps, d, heads=tuple(range(ATTN_HEADS))):
    stepsf = steps.astype(F32)
    return jnp.stack([stepsf * (-_slope(h) * d) for h in heads])


def _hnt(a, b):
    return lax.dot_general(a, b, (((2,), (2,)), ((0,), (0,))), preferred_element_type=F32)


def _hnn(a, b):
    return lax.dot_general(a, b, (((2,), (1,)), ((0,), (0,))), preferred_element_type=F32)


def _blocks_per_step(nb):
    return next(n for n in (4, 2, 1) if nb % n == 0)


def _query_step_specs(qb):
    B = ATTN_BLOCK
    cur = pl.BlockSpec((qb * B, ATTN_WIDTH), lambda r, n: (n, r))
    prev = pl.BlockSpec((B, ATTN_WIDTH), lambda r, n: (jnp.maximum(qb * n - 1, 0), r))
    return cur, prev


def _prev_block(prev_ref, cur_ref, sub, sl):
    B = ATTN_BLOCK
    return prev_ref[:, sl] if sub == 0 else cur_ref[(sub - 1) * B:sub * B, sl]


def _head_cols(tile, lo, big):
    return [_head_col(tile, lo, big), _head_col(tile, jnp.logical_not(lo), big)]


def _attn_fwd(q, k, v, d, name):
    L = q.shape[0]
    nb = L // ATTN_BLOCK
    B = ATTN_BLOCK
    QB = _blocks_per_step(nb)

    def body(q_ref, kp_ref, kc_ref, vp_ref, vc_ref, o_ref, lse_ref):
        n = pl.program_id(1)
        qi = lax.broadcasted_iota(jnp.int32, (B, 2 * B), 0)
        kj = lax.broadcasted_iota(jnp.int32, (B, 2 * B), 1)
        steps = qi + B - kj
        band = (steps >= 0) & (steps <= B)
        lo = lax.broadcasted_iota(jnp.int32, (B, 128), 1) < 64
        bias = _head_bias(steps, d)
        for sub in range(QB):
            rows = slice(sub * B, (sub + 1) * B)
            valid = band & ((kj >= B) | (n > 0)) if sub == 0 else band
            qs, ks, vs = [], [], []
            for G in range(4):
                sl = slice(G * 128, (G + 1) * 128)
                qg = q_ref[rows, sl]
                kg = jnp.concatenate([_prev_block(kp_ref, kc_ref, sub, sl), kc_ref[rows, sl]], axis=0)
                vg = jnp.concatenate([_prev_block(vp_ref, vc_ref, sub, sl), vc_ref[rows, sl]], axis=0)
                qs += [jnp.where(lo, qg, jnp.zeros_like(qg)), jnp.where(lo, jnp.zeros_like(qg), qg)]
                ks += [kg, kg]
                vs += [vg, vg]
            s = jnp.where(valid, _hnt(jnp.stack(qs), jnp.stack(ks)) + bias, NEG_BIG)
            m = jnp.max(s, axis=-1, keepdims=True)
            p = jnp.exp(s - m)
            l = jnp.sum(p, axis=-1, keepdims=True)
            o = _hnn(p.astype(BF16), jnp.stack(vs)) / l
            lse = m + jnp.log(l)
            for G in range(4):
                sl = slice(G * 128, (G + 1) * 128)
                o_ref[rows, sl] = jnp.where(lo, o[2 * G], o[2 * G + 1])
                lse_ref[rows, sl] = jnp.where(lo, lse[2 * G], lse[2 * G + 1])

    cur, prev = _query_step_specs(QB)
    return pl.pallas_call(
        body, name=name, grid=(d, nb // QB),
        in_specs=[cur, prev, cur, prev, cur],
        out_specs=[cur, cur],
        out_shape=[jax.ShapeDtypeStruct((L, d * ATTN_WIDTH), F32)] * 2,
        compiler_params=_params(2),
    )(q, k, k, v, v)


def _attn_merge(parts):
    S = parts[0][0].shape[0]
    tm = VIEW_TILE

    def body(o1, s1, o2, s2, o3, s3, o_ref, lse1, lse4, lse16, planes):
        outs, lses = [], []
        for d, (o, s) in zip(DILATIONS, ((o1, s1), (o2, s2), (o3, s3))):
            outs.append(_view_to_tile(o, d, planes))
            lses.append(_view_to_tile(s, d, planes))
        mx = jnp.maximum(jnp.maximum(lses[0], lses[1]), lses[2])
        es = [jnp.exp(s - mx) for s in lses]
        den = es[0] + es[1] + es[2]
        o_ref[...] = (es[0] * outs[0] + es[1] * outs[1] + es[2] * outs[2]) / den
        _tile_to_views(mx + jnp.log(den), planes, (lse1, lse4, lse16))

    views = [_view_spec(d) for d in DILATIONS]
    flat = [t for p in parts for t in p]
    return pl.pallas_call(
        body, name="attn_merge", grid=(S // tm,),
        in_specs=[views[p] for p in range(3) for _ in range(2)],
        out_specs=[views[0]] + views,
        out_shape=[jax.ShapeDtypeStruct((S, ATTN_WIDTH), F32)] + [_view_shape(S, d, F32) for d in DILATIONS],
        scratch_shapes=[pltpu.VMEM((4, tm, 128), F32)],
        compiler_params=_params(1),
    )(*flat)


def _head_col(t, msk, big):
    if big:
        return jnp.max(jnp.where(msk, t, NEG_BIG), axis=-1, keepdims=True)
    return jnp.sum(jnp.where(msk, t, 0.0), axis=-1, keepdims=True) * (1.0 / 64.0)


def _attn_bwd_q(q, k, v, do, lse, dd, d, name):
    L = q.shape[0]
    nb = L // ATTN_BLOCK
    B = ATTN_BLOCK
    QB = _blocks_per_step(nb)

    def body(q_ref, kp_ref, kc_ref, vp_ref, vc_ref, do_ref, lse_ref, dd_ref, dq_ref):
        n = pl.program_id(1)
        qi = lax.broadcasted_iota(jnp.int32, (B, 2 * B), 0)
        kj = lax.broadcasted_iota(jnp.int32, (B, 2 * B), 1)
        steps = qi + B - kj
        band = (steps >= 0) & (steps <= B)
        lo = lax.broadcasted_iota(jnp.int32, (B, 128), 1) < 64
        bias = _head_bias(steps, d)
        for sub in range(QB):
            rows = slice(sub * B, (sub + 1) * B)
            valid = band & ((kj >= B) | (n > 0)) if sub == 0 else band
            qs, ks, vs, dos, lses, dcols = [], [], [], [], [], []
            for G in range(4):
                sl = slice(G * 128, (G + 1) * 128)
                qg = q_ref[rows, sl]
                kg = jnp.concatenate([_prev_block(kp_ref, kc_ref, sub, sl), kc_ref[rows, sl]], axis=0)
                vg = jnp.concatenate([_prev_block(vp_ref, vc_ref, sub, sl), vc_ref[rows, sl]], axis=0)
                dog = do_ref[rows, sl]
                qs += [jnp.where(lo, qg, jnp.zeros_like(qg)), jnp.where(lo, jnp.zeros_like(qg), qg)]
                dos += [jnp.where(lo, dog, 0.0).astype(BF16), jnp.where(lo, 0.0, dog).astype(BF16)]
                ks += [kg, kg]
                vs += [vg, vg]
                lses += _head_cols(lse_ref[rows, sl], lo, True)
                dcols += _head_cols(dd_ref[rows, sl], lo, False)
            kb = jnp.stack(ks)
            s = _hnt(jnp.stack(qs), kb) + bias
            p = jnp.where(valid, jnp.exp(jnp.where(valid, s, NEG_BIG) - jnp.stack(lses)), 0.0)
            dp = _hnt(jnp.stack(dos), jnp.stack(vs))
            ds = p * (dp - jnp.stack(dcols))
            dq = _hnn(ds.astype(BF16), kb) * 0.125
            for G in range(4):
                dq_ref[rows, G * 128:(G + 1) * 128] = jnp.where(lo, dq[2 * G], dq[2 * G + 1]).astype(BF16)

    cur, prev = _query_step_specs(QB)
    return pl.pallas_call(
        body, name=name, grid=(d, nb // QB), in_specs=[cur, prev, cur, prev, cur, cur, cur, cur], out_specs=cur,
        out_shape=jax.ShapeDtypeStruct((L, d * ATTN_WIDTH), BF16), compiler_params=_params(2),
    )(q, k, k, v, v, do, lse, dd)


def _attn_bwd_kv(q, k, v, do, lse, dd, d, name):
    L = q.shape[0]
    nb = L // ATTN_BLOCK
    B = ATTN_BLOCK
    KB = _blocks_per_step(nb)
    n_steps = nb // KB

    def body(k_ref, v_ref, qc_ref, qn_ref, doc_ref, don_ref, lsec_ref, lsen_ref, ddc_ref, ddn_ref, dk_ref, dv_ref):
        j = pl.program_id(1)
        qrow = lax.broadcasted_iota(jnp.int32, (2 * B, B), 0)
        kk = lax.broadcasted_iota(jnp.int32, (2 * B, B), 1)
        steps = qrow - kk
        band = (steps >= 0) & (steps <= B)
        lo2 = lax.broadcasted_iota(jnp.int32, (2 * B, 128), 1) < 64
        lo = lax.broadcasted_iota(jnp.int32, (B, 128), 1) < 64
        stepsf = steps.astype(F32)
        for sub in range(KB):
            rows = slice(sub * B, (sub + 1) * B)
            last = sub == KB - 1
            valid = band & ((qrow < B) | (j < n_steps - 1)) if last else band
            after = lambda cur_ref, nxt_ref, sl: nxt_ref[:, sl] if last else cur_ref[(sub + 1) * B:(sub + 2) * B, sl]
            for G in range(4):
                sl = slice(G * 128, (G + 1) * 128)
                kg = k_ref[rows, sl]
                vg = v_ref[rows, sl]
                qq = jnp.concatenate([qc_ref[rows, sl], after(qc_ref, qn_ref, sl)], axis=0)
                doo = jnp.concatenate([doc_ref[rows, sl], after(doc_ref, don_ref, sl)], axis=0)
                lse2 = jnp.concatenate([lsec_ref[rows, sl], after(lsec_ref, lsen_ref, sl)], axis=0)
                dd2 = jnp.concatenate([ddc_ref[rows, sl], after(ddc_ref, ddn_ref, sl)], axis=0)
                doo_b = doo.astype(BF16)
                dks, dvs = [], []
                for half in (0, 1):
                    msk = lo2 if half == 0 else jnp.logical_not(lo2)
                    qm = jnp.where(msk, qq, jnp.zeros_like(qq))
                    s = _nt(qm, kg) - (_slope(2 * G + half) * d) * stepsf
                    lse_c = _head_col(lse2, msk, True)
                    p = jnp.where(valid, jnp.exp(jnp.where(valid, s, NEG_BIG) - lse_c), 0.0)
                    dvs.append(_tn(p.astype(BF16), doo_b))
                    dom = jnp.where(msk, doo, 0.0).astype(BF16)
                    dp = _nt(dom, vg)
                    dcol = _head_col(dd2, msk, False)
                    ds = p * (dp - dcol)
                    dks.append(_tn(ds.astype(BF16), qq))
                dk_ref[rows, sl] = jnp.where(lo, dks[0], dks[1]).astype(BF16)
                dv_ref[rows, sl] = jnp.where(lo, dvs[0], dvs[1]).astype(BF16)

    cur = pl.BlockSpec((KB * B, ATTN_WIDTH), lambda r, j: (j, r))
    nxt = pl.BlockSpec((B, ATTN_WIDTH), lambda r, j: (jnp.minimum(KB * (j + 1), nb - 1), r))
    return pl.pallas_call(
        body, name=name, grid=(d, n_steps), in_specs=[cur, cur, cur, nxt, cur, nxt, cur, nxt, cur, nxt],
        out_specs=[cur, cur],
        out_shape=[jax.ShapeDtypeStruct((L, d * ATTN_WIDTH), BF16)] * 2, compiler_params=_params(2),
    )(k, v, q, q, do, do, lse, lse, dd, dd)


CONV_T = 512
HALO = 8


def _per_head(head, refs):
    for h in range(DN_HEADS):
        lanes = pl.ds(h * DN_HEAD_DIM, DN_HEAD_DIM)
        head(*[r.at[:, lanes] for r in refs[:-1]], refs[-1])


def _conv_taps(pad_ref, w, T):
    acc = pad_ref[pl.ds(HALO - 3, T), :] * w[0:1, :]
    for j in range(1, CONV_WIDTH):
        acc = acc + pad_ref[pl.ds(HALO - 3 + j, T), :] * w[j:j + 1, :]
    return acc


def _conv_fwd(xq, xk, xv, conv_w):
    S = xq.shape[0]
    T = CONV_T

    def body(*refs):
        _per_head(head, refs)

    def head(xq_ref, xqh_ref, xk_ref, xkh_ref, xv_ref, xvh_ref, wq_ref, wk_ref, wv_ref,
             qn_ref, kn_ref, v_ref, pad_ref):
        i = pl.program_id(0)

        def act(x_ref, xh_ref, w_ref):
            pad_ref[pl.ds(0, HALO), :] = jnp.where(i > 0, xh_ref[...], 0.0)
            pad_ref[pl.ds(HALO, T), :] = x_ref[...]
            c = _conv_taps(pad_ref, w_ref[...], T)
            return c * _sigmoid(c)

        def l2n(t):
            return t * lax.rsqrt(jnp.sum(t * t, axis=-1, keepdims=True) + L2_EPS)

        qn_ref[...] = l2n(act(xq_ref, xqh_ref, wq_ref))
        kn_ref[...] = l2n(act(xk_ref, xkh_ref, wk_ref))
        v_ref[...] = act(xv_ref, xvh_ref, wv_ref)

    tile = pl.BlockSpec((T, DN_WIDTH), lambda i: (i, 0))
    halo = pl.BlockSpec((HALO, DN_WIDTH), lambda i: (jnp.maximum(i * (T // HALO) - 1, 0), 0))
    wspec = lambda sec: pl.BlockSpec((CONV_WIDTH, DN_WIDTH), lambda i, sec=sec: (0, sec))
    return pl.pallas_call(
        body, name="dn_conv_fwd", grid=(S // T,),
        in_specs=[tile, halo, tile, halo, tile, halo, wspec(0), wspec(1), wspec(2)],
        out_specs=[tile, tile, tile],
        out_shape=[jax.ShapeDtypeStruct((S, DN_WIDTH), F32)] * 3,
        scratch_shapes=[pltpu.VMEM((T + HALO, 128), F32)],
        compiler_params=_params(1),
    )(xq, xq, xk, xk, xv, xv, conv_w, conv_w, conv_w)


def _conv_bwd_pre(xq, xk, xv, conv_w, dqn, dkn, dv):
    S = xq.shape[0]
    T = CONV_T

    def body(*refs):
        _per_head(head, refs)

    def head(xq_ref, xqh_ref, xk_ref, xkh_ref, xv_ref, xvh_ref, wq_ref, wk_ref, wv_ref,
             dqn_ref, dkn_ref, dv_ref, dcq_ref, dck_ref, dcv_ref, dwq_ref, dwk_ref, dwv_ref, pad_ref):
        i = pl.program_id(0)

        def one(x_ref, xh_ref, w_ref, dy_ref, dc_ref, dw_ref, normed):
            pad_ref[pl.ds(0, HALO), :] = jnp.where(i > 0, xh_ref[...], 0.0)
            pad_ref[pl.ds(HALO, T), :] = x_ref[...]
            c = _conv_taps(pad_ref, w_ref[...], T)
            sg = _sigmoid(c)
            a = c * sg
            dy = dy_ref[...]
            if normed:
                r = lax.rsqrt(jnp.sum(a * a, axis=-1, keepdims=True) + L2_EPS)
                y = a * r
                da = r * (dy - y * jnp.sum(dy * y, axis=-1, keepdims=True))
            else:
                da = dy
            dc = da * (sg * (1.0 + c * (1.0 - sg)))
            dc_ref[...] = dc

            @pl.when(i == 0)
            def _():
                dw_ref[...] = jnp.zeros_like(dw_ref)

            rows = [jnp.sum(dc * pad_ref[pl.ds(HALO - 3 + j, T), :], axis=0, keepdims=True) for j in range(CONV_WIDTH)]
            dw_ref[...] += jnp.concatenate(rows + [jnp.zeros((8 - CONV_WIDTH, 128), F32)], axis=0)

        one(xq_ref, xqh_ref, wq_ref, dqn_ref, dcq_ref, dwq_ref, True)
        one(xk_ref, xkh_ref, wk_ref, dkn_ref, dck_ref, dwk_ref, True)
        one(xv_ref, xvh_ref, wv_ref, dv_ref, dcv_ref, dwv_ref, False)

    tile = pl.BlockSpec((T, DN_WIDTH), lambda i: (i, 0))
    halo = pl.BlockSpec((HALO, DN_WIDTH), lambda i: (jnp.maximum(i * (T // HALO) - 1, 0), 0))
    wspec = lambda sec: pl.BlockSpec((CONV_WIDTH, DN_WIDTH), lambda i, sec=sec: (0, sec))
    dwspec = pl.BlockSpec((8, DN_WIDTH), lambda i: (0, 0))
    return pl.pallas_call(
        body, name="dn_conv_bwd_pre", grid=(S // T,),
        in_specs=[tile, halo, tile, halo, tile, halo, wspec(0), wspec(1), wspec(2), tile, tile, tile],
        out_specs=[tile, tile, tile, dwspec, dwspec, dwspec],
        out_shape=[jax.ShapeDtypeStruct((S, DN_WIDTH), F32)] * 3 + [jax.ShapeDtypeStruct((8, DN_WIDTH), F32)] * 3,
        scratch_shapes=[pltpu.VMEM((T + HALO, 128), F32)],
        compiler_params=_params(1),
    )(xq, xq, xk, xk, xv, xv, conv_w, conv_w, conv_w, dqn, dkn, dv)


def _conv_bwd_x(dcq, dck, dcv, conv_w):
    S = dcq.shape[0]
    T = CONV_T
    nt = S // T

    def body(*refs):
        _per_head(head, refs)

    def head(dq_ref, dqh_ref, dk_ref, dkh_ref, dv_ref, dvh_ref, wq_ref, wk_ref, wv_ref,
             oq_ref, ok_ref, ov_ref, pad_ref):
        i = pl.program_id(0)

        def one(d_ref, dh_ref, w_ref, o_ref):
            pad_ref[pl.ds(0, T), :] = d_ref[...]
            pad_ref[pl.ds(T, HALO), :] = jnp.where(i < nt - 1, dh_ref[...], 0.0)
            w = w_ref[...]
            acc = pad_ref[pl.ds(3, T), :] * w[0:1, :]
            for j in range(1, CONV_WIDTH):
                acc = acc + pad_ref[pl.ds(3 - j, T), :] * w[j:j + 1, :]
            o_ref[...] = acc

        one(dq_ref, dqh_ref, wq_ref, oq_ref)
        one(dk_ref, dkh_ref, wk_ref, ok_ref)
        one(dv_ref, dvh_ref, wv_ref, ov_ref)

    tile = pl.BlockSpec((T, DN_WIDTH), lambda i: (i, 0))
    halo = pl.BlockSpec((HALO, DN_WIDTH), lambda i: (jnp.minimum((i + 1) * (T // HALO), S // HALO - 1), 0))
    wspec = lambda sec: pl.BlockSpec((CONV_WIDTH, DN_WIDTH), lambda i, sec=sec: (0, sec))
    return pl.pallas_call(
        body, name="dn_conv_bwd_x", grid=(nt,),
        in_specs=[tile, halo, tile, halo, tile, halo, wspec(0), wspec(1), wspec(2)],
        out_specs=[tile, tile, tile],
        out_shape=[jax.ShapeDtypeStruct((S, DN_WIDTH), F32)] * 3,
        scratch_shapes=[pltpu.VMEM((T + HALO, 128), F32)],
        compiler_params=_params(1),
    )(dcq, dcq, dck, dck, dcv, dcv, conv_w, conv_w, conv_w)


PREP_CHUNKS = 4
SCAN_CHUNKS = 8


def _bnn(a, b):
    return lax.dot_general(a, b, (((2,), (1,)), ((0,), (0,))), preferred_element_type=F32, precision=HI)


def _bnt(a, b):
    return lax.dot_general(a, b, (((2,), (2,)), ((0,), (0,))), preferred_element_type=F32, precision=HI)


def _btn(a, b):
    return lax.dot_general(a, b, (((1,), (1,)), ((0,), (0,))), preferred_element_type=F32, precision=HI)


def _tri_inverse_b(a, blk, eye):
    dg = jnp.where(blk, a, 0.0)
    lo = a - dg
    d2 = _bnn(dg, dg)
    d4 = _bnn(d2, d2)
    d8 = _bnn(d4, d4)
    td = _bnn(_bnn(_bnn(eye - dg, eye + d2), eye + d4), eye + d8)
    b = _bnn(td, lo)
    b2 = _bnn(b, b)
    return _bnn(_bnn(eye - b, eye + b2), td)


def _dn_common_b(bds, avec, dvec, q_raw, k, v, t=None):
    C = DN_CHUNK
    lane = lax.broadcasted_iota(jnp.int32, (C, 128), 1)
    row = lax.broadcasted_iota(jnp.int32, (1, C, C), 1)
    col = lax.broadcasted_iota(jnp.int32, (1, C, C), 2)
    incl = row >= col
    strict = row > col
    eye = (row == col).astype(F32)
    blk = (row // 16) == (col // 16)
    pick = lambda tile, ln: jnp.sum(jnp.where(lane == ln, tile, 0.0), axis=-1, keepdims=True)
    betas, graws, zcs = [], [], []
    for bd in bds:
        z = bd + dvec
        g_all = -jnp.exp(avec) * (jnp.maximum(z, 0.0) + jnp.log(1.0 + jnp.exp(-jnp.abs(z))))
        beta_all = _sigmoid(bd)
        for h in range(DN_HEADS):
            betas.append(pick(beta_all, h))
            graws.append(pick(g_all, DN_HEADS + h))
            zcs.append(pick(z, DN_HEADS + h))
    beta, graw, zc = jnp.stack(betas), jnp.stack(graws), jnp.stack(zcs)
    to_row = lambda c: jnp.sum(eye * c, axis=1, keepdims=True)
    gc = jnp.sum(jnp.where(incl, to_row(graw), 0.0), axis=-1, keepdims=True)
    decay = jnp.exp(jnp.where(incl, gc - to_row(gc), NEG_BIG))
    q = q_raw * (DN_HEAD_DIM ** -0.5)
    kb = k * beta
    kk = _bnt(kb, k)
    if t is None:
        t = _tri_inverse_b(jnp.where(strict, kk * decay, 0.0), blk, eye)
    eg = jnp.exp(gc)
    rhs_w = kb * eg
    u = _bnn(t, v * beta)
    w = _bnn(t, rhs_w)
    qk = _bnt(q, k)
    aq = jnp.where(incl, qk * decay, 0.0)
    last = lax.broadcasted_iota(jnp.int32, (1, C, 1), 1) == C - 1
    g_last = jnp.sum(jnp.where(last, gc, 0.0), axis=1, keepdims=True)
    ekd = jnp.exp(g_last - gc)
    return dict(beta=beta, graw=graw, zc=zc, gc=gc, decay=decay, q=q, kb=kb, kk=kk, t=t, eg=eg, rhs_w=rhs_w,
                u=u, w=w, qk=qk, aq=aq, g_last=g_last, ekd=ekd, kd=k * ekd, qg=q * eg,
                incl=incl, strict=strict, eye=eye, lane=lane, row=row, col=col, last=last)


def _stack_heads(ref, rows):
    return jnp.stack([ref[rows, h * DN_HEAD_DIM:(h + 1) * DN_HEAD_DIM] for h in range(DN_HEADS)])


def _stack_units(ref, nc):
    C = DN_CHUNK
    return jnp.concatenate([_stack_heads(ref, slice(ci * C, (ci + 1) * C)) for ci in range(nc)], axis=0)


def _store_units(ref, val, nc):
    C = DN_CHUNK
    for ci in range(nc):
        for h in range(DN_HEADS):
            ref[ci * C:(ci + 1) * C, h * DN_HEAD_DIM:(h + 1) * DN_HEAD_DIM] = val[ci * DN_HEADS + h]


def _dn_prep(qn, kn, v, bd, avec, dvec):
    S = qn.shape[0]
    C = DN_CHUNK
    N = S // C
    nc = PREP_CHUNKS

    def body(q_ref, k_ref, v_ref, bd_ref, a_ref, d_ref, u_ref, w_ref, qg_ref, kd_ref, aq_ref, t_ref, egl_ref):
        bds = [bd_ref[ci * C:(ci + 1) * C, :] for ci in range(nc)]
        c = _dn_common_b(bds, a_ref[...], d_ref[...], _stack_units(q_ref, nc), _stack_units(k_ref, nc), _stack_units(v_ref, nc))
        _store_units(u_ref, c["u"], nc)
        _store_units(w_ref, c["w"], nc)
        _store_units(qg_ref, c["qg"], nc)
        _store_units(kd_ref, c["kd"], nc)
        egl = jnp.broadcast_to(jnp.exp(c["g_last"]), (nc * DN_HEADS, 1, 128))
        for ci in range(nc):
            for h in range(DN_HEADS):
                aq_ref[h, ci * C:(ci + 1) * C, :] = c["aq"][ci * DN_HEADS + h]
                t_ref[h, ci * C:(ci + 1) * C, :] = c["t"][ci * DN_HEADS + h]
            egl_ref[ci * 8:(ci + 1) * 8, :] = jnp.concatenate(
                [egl[ci * DN_HEADS + h] for h in range(DN_HEADS)] + [jnp.zeros((8 - DN_HEADS, 128), F32)], axis=0)

    tok = lambda w: pl.BlockSpec((nc * C, w), lambda n: (n, 0))
    sq = pl.BlockSpec((DN_HEADS, nc * C, C), lambda n: (0, n, 0))
    vec = pl.BlockSpec((1, 128), lambda n: (0, 0))
    return pl.pallas_call(
        body, name="dn_prep", grid=(N // nc,),
        in_specs=[tok(DN_WIDTH)] * 3 + [tok(128), vec, vec],
        out_specs=[tok(DN_WIDTH)] * 4 + [sq, sq, pl.BlockSpec((nc * 8, 128), lambda n: (n, 0))],
        out_shape=[jax.ShapeDtypeStruct((S, DN_WIDTH), F32)] * 4 + [jax.ShapeDtypeStruct((DN_HEADS, S, C), F32)] * 2
                  + [jax.ShapeDtypeStruct((N * 8, 128), F32)],
        compiler_params=_params(1),
    )(qn, kn, v, bd, avec, dvec)


def _dn_scan_fwd(u, w, qg, kd, aq, egl, gate, dn_gain):
    S = u.shape[0]
    C = DN_CHUNK
    N = S // C
    HD = DN_HEAD_DIM
    nc = SCAN_CHUNKS

    def body(u_ref, w_ref, qg_ref, kd_ref, aq_ref, egl_ref, gate_ref, gain_ref, dn_ref, o_ref, vn_ref, st_ref, state_ref):
        @pl.when(pl.program_id(0) == 0)
        def _():
            state_ref[...] = jnp.zeros_like(state_ref)

        gain = gain_ref[...]
        for ci in range(nc):
            rows = slice(ci * C, (ci + 1) * C)
            st = state_ref[...]
            for h in range(DN_HEADS):
                st_ref[ci * DN_WIDTH + h * HD:ci * DN_WIDTH + (h + 1) * HD, :] = st[h]
            v_new = _stack_heads(u_ref, rows) - _bnn(_stack_heads(w_ref, rows), st)
            o = _bnn(_stack_heads(qg_ref, rows), st) + _bnn(aq_ref[:, rows, :], v_new)
            egl = jnp.stack([egl_ref[ci * 8 + h:ci * 8 + h + 1, :] for h in range(DN_HEADS)])
            state_ref[...] = st * egl + _btn(_stack_heads(kd_ref, rows), v_new)
            r = lax.rsqrt(jnp.mean(o * o, axis=-1, keepdims=True) + NORM_EPS)
            gt = _stack_heads(gate_ref, rows)
            dn = o * r * gain * (gt * _sigmoid(gt))
            for h in range(DN_HEADS):
                sl = slice(h * HD, (h + 1) * HD)
                vn_ref[rows, sl] = v_new[h]
                o_ref[rows, sl] = o[h]
                dn_ref[rows, sl] = dn[h]

    tok = lambda wd: pl.BlockSpec((nc * C, wd), lambda n: (n, 0))
    sq = pl.BlockSpec((DN_HEADS, nc * C, C), lambda n: (0, n, 0))
    vec = pl.BlockSpec((1, 128), lambda n: (0, 0))
    return pl.pallas_call(
        body, name="dn_scan_fwd", grid=(N // nc,),
        in_specs=[tok(DN_WIDTH)] * 4 + [sq, pl.BlockSpec((nc * 8, 128), lambda n: (n, 0)), tok(DN_WIDTH), vec],
        out_specs=[tok(DN_WIDTH)] * 3 + [pl.BlockSpec((nc * DN_WIDTH, HD), lambda n: (n, 0))],
        out_shape=[jax.ShapeDtypeStruct((S, DN_WIDTH), F32)] * 3 + [jax.ShapeDtypeStruct((N * DN_WIDTH, HD), F32)],
        scratch_shapes=[pltpu.VMEM((DN_HEADS, HD, HD), F32)],
        compiler_params=_params(1),
    )(u, w, qg, kd, aq, egl, gate, dn_gain)


def _dn_scan_bwd(w, qg, kd, aq, egl, gate, dn_gain, o, ddn):
    S = w.shape[0]
    C = DN_CHUNK
    N = S // C
    HD = DN_HEAD_DIM
    nc = SCAN_CHUNKS

    def body(w_ref, qg_ref, kd_ref, aq_ref, egl_ref, gate_ref, gain_ref, o_ref, ddn_ref,
             do_ref, dvn_ref, dgate_ref, dst_ref, small_ref, dstate_ref):
        @pl.when(pl.program_id(0) == 0)
        def _():
            dstate_ref[...] = jnp.zeros_like(dstate_ref)
            small_ref[...] = jnp.zeros_like(small_ref)

        gain = gain_ref[...]
        d_gain = jnp.zeros((1, 128), F32)
        for ci in reversed(range(nc)):
            rows = slice(ci * C, (ci + 1) * C)
            dsn = dstate_ref[...]
            for h in range(DN_HEADS):
                dst_ref[ci * DN_WIDTH + h * HD:ci * DN_WIDTH + (h + 1) * HD, :] = dsn[h]
            ov = _stack_heads(o_ref, rows)
            r = lax.rsqrt(jnp.mean(ov * ov, axis=-1, keepdims=True) + NORM_EPS)
            on = ov * r
            gt = _stack_heads(gate_ref, rows)
            sgt = _sigmoid(gt)
            silu_g = gt * sgt
            dy = _stack_heads(ddn_ref, rows)
            d_gain = d_gain + jnp.sum(jnp.sum(dy * on * silu_g, axis=1, keepdims=True), axis=0)
            dgate = dy * on * gain * (sgt * (1.0 + gt * (1.0 - sgt)))
            don = dy * gain * silu_g
            do = r * (don - on * jnp.mean(don * on, axis=-1, keepdims=True))
            d_vnew = _btn(aq_ref[:, rows, :], do) + _bnn(_stack_heads(kd_ref, rows), dsn)
            egl = jnp.stack([egl_ref[ci * 8 + h:ci * 8 + h + 1, :] for h in range(DN_HEADS)])
            dstate_ref[...] = _btn(_stack_heads(qg_ref, rows), do) + dsn * egl - _btn(_stack_heads(w_ref, rows), d_vnew)
            for h in range(DN_HEADS):
                sl = slice(h * HD, (h + 1) * HD)
                do_ref[rows, sl] = do[h]
                dvn_ref[rows, sl] = d_vnew[h]
                dgate_ref[rows, sl] = dgate[h]
        small_ref[...] += jnp.concatenate([d_gain, jnp.zeros((7, 128), F32)], axis=0)

    nb = N // nc
    tok = lambda wd: pl.BlockSpec((nc * C, wd), lambda i: (nb - 1 - i, 0))
    sq = pl.BlockSpec((DN_HEADS, nc * C, C), lambda i: (0, nb - 1 - i, 0))
    vec = pl.BlockSpec((1, 128), lambda i: (0, 0))
    return pl.pallas_call(
        body, name="dn_scan_bwd", grid=(nb,),
        in_specs=[tok(DN_WIDTH)] * 3 + [sq, pl.BlockSpec((nc * 8, 128), lambda i: (nb - 1 - i, 0)), tok(DN_WIDTH), vec,
                                       tok(DN_WIDTH), tok(DN_WIDTH)],
        out_specs=[tok(DN_WIDTH)] * 3 + [pl.BlockSpec((nc * DN_WIDTH, HD), lambda i: (nb - 1 - i, 0)),
                                        pl.BlockSpec((8, 128), lambda i: (0, 0))],
        out_shape=[jax.ShapeDtypeStruct((S, DN_WIDTH), F32)] * 3 + [jax.ShapeDtypeStruct((N * DN_WIDTH, HD), F32),
                                                                  jax.ShapeDtypeStruct((8, 128), F32)],
        scratch_shapes=[pltpu.VMEM((DN_HEADS, HD, HD), F32)],
        compiler_params=_params(1),
    )(w, qg, kd, aq, egl, gate, dn_gain, o, ddn)


def _dn_post(qn, kn, v, bd, avec, dvec, t_inv, v_new_all, states, dstates, do_all, dvn_all, comm=None):
    S = qn.shape[0]
    C = DN_CHUNK
    N = S // C
    HD = DN_HEAD_DIM
    nc = PREP_CHUNKS
    B = nc * DN_HEADS

    def body(q_ref, k_ref, v_ref, bd_ref, a_ref, d_ref, t_ref, vn_ref, st_ref, dst_ref, do_ref, dvn_ref,
             dq_ref, dk_ref, dv_ref, dbd_ref, small_ref):
        @pl.when(pl.program_id(0) == 0)
        def _():
            small_ref[...] = jnp.zeros_like(small_ref)

        avec = a_ref[...]
        bds = [bd_ref[ci * C:(ci + 1) * C, :] for ci in range(nc)]
        k = _stack_units(k_ref, nc)
        vv = _stack_units(v_ref, nc)
        t = jnp.concatenate([t_ref[:, ci * C:(ci + 1) * C, :] for ci in range(nc)], axis=0)
        c = _dn_common_b(bds, avec, d_ref[...], _stack_units(q_ref, nc), k, vv, t=t)
        q, kb, eg, u, w = c["q"], c["kb"], c["eg"], c["u"], c["w"]
        beta, decay, incl, strict, eye = c["beta"], c["decay"], c["incl"], c["strict"], c["eye"]
        st = jnp.stack([st_ref[b * HD:(b + 1) * HD, :] for b in range(B)])
        dsn = jnp.stack([dst_ref[b * HD:(b + 1) * HD, :] for b in range(B)])
        v_new = _stack_units(vn_ref, nc)
        do = _stack_units(do_ref, nc)
        d_vnew = _stack_units(dvn_ref, nc)
        egl = jnp.exp(c["g_last"])
        daq = jnp.where(incl, _bnt(do, v_new), 0.0)
        d_qg = _bnt(do, st)
        d_kd = _bnt(v_new, dsn)
        d_glast = jnp.sum(jnp.sum(dsn * st, axis=-1, keepdims=True), axis=1, keepdims=True) * egl
        d_w = -_bnt(d_vnew, st)
        d_ru = _btn(t, d_vnew)
        d_rw = _btn(t, d_w)
        da = -jnp.where(strict, _bnt(d_ru, u) + _bnt(d_rw, w), 0.0)
        dv = d_ru * beta
        dbeta = jnp.sum(d_ru * vv, axis=-1, keepdims=True)
        dkb = d_rw * eg
        dgc = jnp.sum(d_rw * c["rhs_w"], axis=-1, keepdims=True)
        dkk = da * decay
        ddecay = da * c["kk"]
        dkb = dkb + _bnn(dkk, k)
        dk = _btn(dkk, kb)
        dqk = daq * decay
        ddecay = ddecay + daq * c["qk"]
        dq = _bnn(dqk, k)
        dk = dk + _btn(dqk, q)
        m = ddecay * decay
        col_sum = jnp.sum(m, axis=1, keepdims=True)
        dgc = dgc + jnp.sum(m, axis=-1, keepdims=True) - jnp.sum(eye * col_sum, axis=-1, keepdims=True)
        dq = dq + d_qg * eg
        dgc = dgc + jnp.sum(d_qg * c["qg"], axis=-1, keepdims=True)
        dk = dk + d_kd * c["ekd"]
        tk = jnp.sum(d_kd * c["kd"], axis=-1, keepdims=True)
        dgc = dgc - tk
        d_glast = d_glast + jnp.sum(tk, axis=1, keepdims=True)
        dk = dk + dkb * beta
        dbeta = dbeta + jnp.sum(dkb * k, axis=-1, keepdims=True)
        dgc = dgc + jnp.where(c["last"], d_glast, 0.0)
        dgc_row = jnp.sum(eye * dgc, axis=1, keepdims=True)
        dgraw = jnp.sum(jnp.where(c["col"] >= c["row"], dgc_row, 0.0), axis=-1, keepdims=True)
        _store_units(dq_ref, dq * (HD ** -0.5), nc)
        _store_units(dk_ref, dk, nc)
        _store_units(dv_ref, dv, nc)
        dbraw = dbeta * beta * (1.0 - beta)
        dzc = dgraw * _sigmoid(c["zc"])
        ga = dgraw * c["graw"]
        lane = c["lane"]
        lane1 = lax.broadcasted_iota(jnp.int32, (1, 128), 1)
        neg_ea = -jnp.exp(avec)
        d_alog = jnp.zeros((1, 128), F32)
        d_dt = jnp.zeros((1, 128), F32)
        for ci in range(nc):
            dbd = jnp.zeros((C, 128), F32)
            for h in range(DN_HEADS):
                b = ci * DN_HEADS + h
                dz = dzc[b] * neg_ea
                dbd = dbd + jnp.where(lane == h, dbraw[b], 0.0) + jnp.where(lane == DN_HEADS + h, dz, 0.0)
                d_alog = d_alog + jnp.where(lane1 == DN_HEADS + h, jnp.sum(ga[b], axis=0, keepdims=True), 0.0)
                d_dt = d_dt + jnp.where(lane1 == DN_HEADS + h, jnp.sum(dz, axis=0, keepdims=True), 0.0)
            dbd_ref[ci * C:(ci + 1) * C, :] = dbd
        small_ref[...] += jnp.concatenate([d_alog, d_dt, jnp.zeros((6, 128), F32)], axis=0)

    tok = lambda wd: pl.BlockSpec((nc * C, wd), lambda n: (n, 0))
    big = pl.BlockSpec((nc * DN_WIDTH, HD), lambda n: (n, 0))
    sq = pl.BlockSpec((DN_HEADS, nc * C, C), lambda n: (0, n, 0))
    vec = pl.BlockSpec((1, 128), lambda n: (0, 0))
    return _call(
        body, (qn, kn, v, bd, avec, dvec, t_inv, v_new_all, states, dstates, do_all, dvn_all),
        name="dn_post", grid=(N // nc,), comm=comm,
        in_specs=[tok(DN_WIDTH)] * 3 + [tok(128), vec, vec, sq, tok(DN_WIDTH), big, big, tok(DN_WIDTH), tok(DN_WIDTH)],
        out_specs=[tok(DN_WIDTH)] * 3 + [tok(128), pl.BlockSpec((8, 128), lambda n: (0, 0))],
        out_shape=[jax.ShapeDtypeStruct((S, DN_WIDTH), F32)] * 3 + [jax.ShapeDtypeStruct((S, 128), F32),
                                                                  jax.ShapeDtypeStruct((8, 128), F32)])


def _outproj_fwd(x, attn, dn, w_out):
    S, D = x.shape
    tm = 512

    def body(x_ref, a_ref, d_ref, w_ref, xo_ref, mix_ref):
        a = a_ref[...].astype(BF16)
        dd = d_ref[...].astype(BF16)
        mix_ref[:, 0:ATTN_WIDTH] = a
        mix_ref[:, ATTN_WIDTH:] = dd
        xo_ref[...] = x_ref[...] + _nn(a, w_ref[0:ATTN_WIDTH, :]) + _nn(dd, w_ref[ATTN_WIDTH:, :])

    tok = lambda w: pl.BlockSpec((tm, w), lambda i: (i, 0))
    return pl.pallas_call(
        body, name="outproj_fwd", grid=(S // tm,),
        in_specs=[tok(D), tok(ATTN_WIDTH), tok(DN_WIDTH), pl.BlockSpec((D, D), lambda i: (0, 0))],
        out_specs=[tok(D), tok(D)],
        out_shape=[jax.ShapeDtypeStruct((S, D), F32), jax.ShapeDtypeStruct((S, D), BF16)],
        compiler_params=_params(1),
    )(x, attn, dn, w_out)


def _outproj_bwd(dx, w_out, attn):
    S, D = dx.shape
    tm = VIEW_TILE

    def body(dx_ref, w_ref, attn_ref, da1, da4, da16, dl1, dl4, dl16, ddn_ref, dxb_ref, planes):
        d = dx_ref[...].astype(BF16)
        dxb_ref[...] = d
        da = _nt(d, w_ref[0:ATTN_WIDTH, :])
        ddn_ref[...] = _nt(d, w_ref[ATTN_WIDTH:, :])
        _tile_to_views(da, planes, (da1, da4, da16))
        lo = lax.broadcasted_iota(jnp.int32, (tm, 128), 1) < 64
        cols = []
        for G in range(4):
            sl = slice(G * 128, (G + 1) * 128)
            t = da[:, sl] * attn_ref[:, sl]
            d0 = jnp.sum(jnp.where(lo, t, 0.0), axis=-1, keepdims=True)
            d1 = jnp.sum(jnp.where(lo, 0.0, t), axis=-1, keepdims=True)
            cols.append(jnp.where(lo, d0, d1))
        _tile_to_views(jnp.concatenate(cols, axis=1), planes, (dl1, dl4, dl16))

    tok = lambda w: pl.BlockSpec((tm, w), lambda i: (i, 0))
    views = [_view_spec(d) for d in DILATIONS]
    return pl.pallas_call(
        body, name="outproj_bwd", grid=(S // tm,),
        in_specs=[tok(D), pl.BlockSpec((D, D), lambda i: (0, 0)), tok(ATTN_WIDTH)],
        out_specs=views + views + [tok(DN_WIDTH), tok(D)],
        out_shape=[_view_shape(S, d, F32) for d in DILATIONS] * 2
                  + [jax.ShapeDtypeStruct((S, DN_WIDTH), F32), jax.ShapeDtypeStruct((S, D), BF16)],
        scratch_shapes=[pltpu.VMEM((4, tm, 128), F32)],
        compiler_params=_params(1),
    )(dx, w_out, attn)


def _loss_head(x, gain, target):
    S, D = x.shape
    tm = 512

    def body(x_ref, gain_ref, t_ref, loss_ref, dx_ref, dgain_ref):
        @pl.when(pl.program_id(0) == 0)
        def _():
            loss_ref[...] = jnp.zeros_like(loss_ref)
            dgain_ref[...] = jnp.zeros_like(dgain_ref)

        xf = x_ref[...]
        gain = gain_ref[...]
        r = lax.rsqrt(jnp.mean(xf * xf, axis=-1, keepdims=True) + NORM_EPS)
        xhat = xf * r
        err = xhat * gain - t_ref[...]
        part = 0.5 * jnp.sum(jnp.mean(err * err, axis=-1, keepdims=True), axis=0, keepdims=True)
        first = (lax.broadcasted_iota(jnp.int32, (8, 128), 0) == 0) & (lax.broadcasted_iota(jnp.int32, (8, 128), 1) == 0)
        loss_ref[...] += jnp.where(first, part, 0.0)
        dy = err * (1.0 / D)
        dgain_ref[...] += jnp.sum(dy * xhat, axis=0, keepdims=True)
        dxh = dy * gain
        dx_ref[...] = r * (dxh - xhat * jnp.mean(dxh * xhat, axis=-1, keepdims=True))

    tok = pl.BlockSpec((tm, D), lambda i: (i, 0))
    row = pl.BlockSpec((1, D), lambda i: (0, 0))
    return pl.pallas_call(
        body, name="loss_head", grid=(S // tm,),
        in_specs=[tok, row, tok],
        out_specs=[pl.BlockSpec((8, 128), lambda i: (0, 0)), tok, row],
        out_shape=[jax.ShapeDtypeStruct((8, 128), F32), jax.ShapeDtypeStruct((S, D), F32),
                   jax.ShapeDtypeStruct((1, D), F32)],
        compiler_params=_params(1),
    )(x, gain, target)


def _adamw(w, g, m, v, name):
    R, Ccols = w.shape
    tr = R
    for cand in (256, 128, 64, 32, 16, 8):
        if R % cand == 0:
            tr = cand
            break
    c1 = 1.0 - ADAM_B1 ** ADAM_STEP
    c2 = 1.0 - ADAM_B2 ** ADAM_STEP

    def body(w_ref, g_ref, m_ref, v_ref, d_ref, nm_ref, nv_ref):
        gv = g_ref[...]
        mn = ADAM_B1 * m_ref[...] + (1.0 - ADAM_B1) * gv
        vn = ADAM_B2 * v_ref[...] + (1.0 - ADAM_B2) * (gv * gv)
        nm_ref[...] = mn
        nv_ref[...] = vn
        d_ref[...] = -ADAM_LR * ((mn / c1) / (jnp.sqrt(vn / c2) + ADAM_EPS) + ADAM_WD * w_ref[...])

    spec = pl.BlockSpec((tr, Ccols), lambda i: (i, 0))
    return pl.pallas_call(
        body, name=name, grid=(R // tr,), in_specs=[spec] * 4, out_specs=[spec] * 3,
        out_shape=[jax.ShapeDtypeStruct((R, Ccols), F32)] * 3, compiler_params=_params(1),
    )(w, g, m, v)


LATE_WEIGHTS = ("w_out", "ffn2_gate", "ffn2_up", "ffn2_down")


def _local_step(x, target, wts, small, dist=None):
    g1, g2, gm, gf = small["norm_ffn1"], small["norm_ffn2"], small["norm_mix"], small["norm_final"]
    wts = dict(wts)

    def reduce_start(gs, tag):
        return _rs_add_pairs(gs, _swap_sibling(gs, True, "rs_swap_halves_" + tag), dist["c"], "rs_add_pairs_" + tag)

    (x1, h1, fg1, fu1), late = _ffn_fwd(x, g1, wts["ffn1_gate"], wts["ffn1_up"], wts["ffn1_down"], "ffn1_fwd",
                                        comm=_ag_comm(dist["late"]) if dist else None)
    if dist:
        wts.update(zip(LATE_WEIGHTS, late))
        wts["w_out"] = wts["w_out"].reshape(D_MODEL, D_MODEL)
    h2, *qkv, xq, xk, xv, gate, bd = _inproj_fwd(x1, gm, wts["w_in"])
    aq, ak, av = qkv[0:3], qkv[3:6], qkv[6:9]
    parts = [_attn_fwd(aq[p], ak[p], av[p], d, f"attn_fwd_d{d}") for p, d in enumerate(DILATIONS)]
    attn, *lse = _attn_merge(parts)
    conv_w = small["conv_w"]
    qn, kn, vv = _conv_fwd(xq, xk, xv, conv_w)
    dn_u, dn_w, dn_qg, dn_kd, dn_aq, dn_t, dn_egl = _dn_prep(qn, kn, vv, bd, small["avec"], small["dvec"])
    dn, o_dn, v_new, states = _dn_scan_fwd(dn_u, dn_w, dn_qg, dn_kd, dn_aq, dn_egl, gate, small["dn_norm"])
    x2, mix = _outproj_fwd(x1, attn, dn, wts["w_out"])
    (x3, h3, fg2, fu2), _ = _ffn_fwd(x2, g2, wts["ffn2_gate"], wts["ffn2_up"], wts["ffn2_down"], "ffn2_fwd")
    loss, dx3, d_gf = _loss_head(x3, gf, target)

    grads = {}
    (dx2, d_g2, dfg2, dfu2, act2, dout2), _ = _ffn_bwd(dx3, x2, g2, fg2, fu2, wts["ffn2_down"], wts["ffn2_gate"],
                                                      wts["ffn2_up"], "ffn2_bwd")
    tk = 2048
    grads["ffn2_gate"], _ = _dw_chunks(dfg2, h3, tk, "dw_ffn2_gate")
    grads["ffn2_up"], _ = _dw_chunks(dfu2, h3, tk, "dw_ffn2_up")
    grads["ffn2_down"], _ = _dw_chunks(act2, dout2, tk, "dw_ffn2_down")
    group_a = ("ffn2_gate", "ffn2_up", "ffn2_down")
    parts_a = reduce_start([grads[n] for n in group_a], "a") if dist else None

    *dviews, ddn, dx2b = _outproj_bwd(dx2, wts["w_out"], attn)
    dattn, dd = dviews[0:3], dviews[3:6]
    grads["w_out"] = _matmul_tn(mix, dx2b, D_MODEL, tk, "dw_out").reshape(N_CHIPS, D_MODEL // N_CHIPS, D_MODEL)

    daq, dak, dav = [], [], []
    for p, d in enumerate(DILATIONS):
        daq.append(_attn_bwd_q(aq[p], ak[p], av[p], dattn[p], lse[p], dd[p], d, f"attn_bwd_q_d{d}"))
        dk_p, dv_p = _attn_bwd_kv(aq[p], ak[p], av[p], dattn[p], lse[p], dd[p], d, f"attn_bwd_kv_d{d}")
        dak.append(dk_p)
        dav.append(dv_p)

    do_dn, dvn, dgate, dstates, d_dn_gain = _dn_scan_bwd(dn_w, dn_qg, dn_kd, dn_aq, dn_egl, gate, small["dn_norm"], o_dn, ddn)
    (dqn, dkn, dvv, dbd, dn_small), recv_a = _dn_post(qn, kn, vv, bd, small["avec"], small["dvec"], dn_t, v_new, states,
                                                      dstates, do_dn, dvn, comm=_rsx_comm(parts_a) if dist else None)
    dcq, dck, dcv, dwq, dwk, dwv = _conv_bwd_pre(xq, xk, xv, conv_w, dqn, dkn, dvv)
    dxq, dxk, dxv = _conv_bwd_x(dcq, dck, dcv, conv_w)
    d_conv = jnp.concatenate([dwq[:CONV_WIDTH], dwk[:CONV_WIDTH], dwv[:CONV_WIDTH]], axis=1)

    dx1, d_gm, dproj = _inproj_bwd(dx2, x1, gm, [daq, dak, dav], [dxq, dxk, dxv, dgate], dbd, wts["w_in"])
    gi = _matmul_tn(dproj, h2, IN_COLS_PADDED, 512, "dw_in")
    if dist:
        gi = jnp.concatenate([gi[:3072], gi[3584:3592], gi[3072:3584]], axis=0).reshape(N_CHIPS, IN_COLS // N_CHIPS, D_MODEL)
        gi = jnp.pad(gi, ((0, 0), (0, W_IN_ROWS - IN_COLS // N_CHIPS), (0, 0)))
    grads["w_in"] = gi
    group_b = ("w_in", "w_out")
    parts_b = reduce_start([grads[n] for n in group_b], "b") if dist else None

    (dx0, d_g1, dfg1, dfu1, act1, dout1), _ = _ffn_bwd(dx1, x, g1, fg1, fu1, wts["ffn1_down"], wts["ffn1_gate"],
                                                      wts["ffn1_up"], "ffn1_bwd")
    group_c = ("ffn1_gate", "ffn1_up", "ffn1_down")
    pending = parts_b if dist else []
    parts_c, recv_bc = [], []
    for n, (lhs, rhs) in zip(group_c, ((dfg1, h1), (dfu1, h1), (act1, dout1))):
        grads[n], landed = _dw_chunks(lhs, rhs, tk, "dw_" + n, comm=_rsx_comm(pending) if dist else None)
        recv_bc += list(landed)
        if dist:
            pending = reduce_start([grads[n]], n)
            parts_c += pending

    small_grads = dict(norm_ffn1=d_g1, norm_mix=d_gm, norm_ffn2=d_g2, norm_final=d_gf, conv_w=d_conv,
                       a_log=dn_small[0:1], dt_bias=dn_small[1:2], dn_norm=d_dn_gain[0:1])
    if dist:
        recv_bc += _rs_exchange_arrays(pending)
        recv_b, recv_c = recv_bc[:len(parts_b)], recv_bc[len(parts_b):]
        names = group_a + group_b + group_c
        totals = _rs_add_totals(list(parts_a) + list(parts_b) + list(parts_c), list(recv_a) + list(recv_b) + list(recv_c),
                                dist["chip"])
        theirs = _swap_sibling(totals, False, "rs_share_total")
        grads = {n: (mine, other) for n, mine, other in zip(names, totals, theirs)}
    return loss, dx0, grads, small_grads


HBM =pl.BlockSpec(memory_space=pl.ANY)
VMEM_SPEC = pl.BlockSpec(memory_space=pltpu.VMEM)


def _coords():
    return lax.axis_index("x"), lax.axis_index("y"), lax.axis_index("c")


def _remote(src, dst, send_sems, recv_sems, k, dev):
    return pltpu.make_async_remote_copy(src_ref=src, dst_ref=dst, send_sem=send_sems.at[k], recv_sem=recv_sems.at[k],
                                        device_id=dev, device_id_type=MESH)


def _allreduce_small(buf, name):
    R, Cc = buf.shape

    def body(src_ref, out_ref, recv_ref, send_sems, recv_sems):
        x, y, c = _coords()
        copies = []
        for m in range(1, 8):
            fx, fy, fc = (m >> 2) & 1, (m >> 1) & 1, m & 1
            dev = (x ^ fx if fx else x, y ^ fy if fy else y, c ^ fc if fc else c)
            cp = _remote(src_ref, recv_ref.at[m - 1], send_sems, recv_sems, m - 1, dev)
            cp.start()
            copies.append(cp)
        for cp in copies:
            cp.wait()
        r = [src_ref[...]] + [recv_ref[m] for m in range(7)]
        out_ref[...] = ((r[0] + r[1]) + (r[2] + r[3])) + ((r[4] + r[5]) + (r[6] + r[7]))

    return pl.pallas_call(
        body, name=name, out_shape=jax.ShapeDtypeStruct((R, Cc), F32),
        in_specs=[VMEM_SPEC], out_specs=VMEM_SPEC,
        scratch_shapes=[pltpu.VMEM((7, R, Cc), F32), pltpu.SemaphoreType.DMA((7,)), pltpu.SemaphoreType.DMA((7,))],
    )(buf)


BIG = ("ffn1_gate", "ffn1_up", "ffn1_down", "w_in", "w_out", "ffn2_gate", "ffn2_up", "ffn2_down")
ROW_SHARDED = ("ffn1_down", "w_out", "ffn2_down")
W_IN_ROWS = 960


def _rows(ref, start, size):
    return ref.at[pl.ds(pl.multiple_of(start, 16), size)]


def _allgather_arrays(shards):
    n = len(shards)

    def body(*refs):
        srcs, outs, send_sems, recv_sems = refs[:n], refs[n:2 * n], refs[2 * n], refs[2 * n + 1]
        x, y, c = _coords()
        sib = (x, y, 1 - c)
        xn, yn, dg = (1 - x, y), (x, 1 - y), (1 - x, 1 - y)
        slot = lambda out, chip: out.at[2 * chip[0] + chip[1]]
        started = []

        def go(cp):
            cp.start()
            started.append(cp)

        for a, (src, out) in enumerate(zip(srcs, outs)):
            h = src.shape[0] // 2
            cp = lambda s, d, k, dev: _remote(s, d, send_sems, recv_sems, 8 * a + k, dev)
            go(cp(src, slot(out, (x, y)), 6, sib))
            mine, dst = _rows(src, c * h, h), _rows(slot(out, (x, y)), c * h, h)
            go(cp(mine, dst, 0, (*xn, c)))
            go(cp(mine, dst, 1, (*yn, c)))
        for a, (src, out) in enumerate(zip(srcs, outs)):
            h = src.shape[0] // 2
            q = h // 2
            cp = lambda s, d, k, dev: _remote(s, d, send_sems, recv_sems, 8 * a + k, dev)
            from_x, from_y = _rows(slot(out, xn), c * h, h), _rows(slot(out, yn), c * h, h)
            cp(from_x, from_x, 0, sib).wait_recv()
            first = _rows(slot(out, xn), c * h, q)
            go(cp(first, first, 2, (*yn, c)))
            go(cp(from_x, from_x, 3, sib))
            cp(from_y, from_y, 1, sib).wait_recv()
            second = _rows(slot(out, yn), c * h + q, q)
            go(cp(second, second, 7, (*xn, c)))
            go(cp(from_y, from_y, 4, sib))
        for a, (src, out) in enumerate(zip(srcs, outs)):
            h = src.shape[0] // 2
            q = h // 2
            cp = lambda s, d, k, dev: _remote(s, d, send_sems, recv_sems, 8 * a + k, dev)
            first, second = _rows(slot(out, dg), c * h, q), _rows(slot(out, dg), c * h + q, q)
            cp(first, first, 2, sib).wait_recv()
            cp(second, second, 7, sib).wait_recv()
            from_d = _rows(slot(out, dg), c * h, h)
            go(cp(from_d, from_d, 5, sib))
        for a, (src, out) in enumerate(zip(srcs, outs)):
            h = src.shape[0] // 2
            cp = lambda s, d, k, dev: _remote(s, d, send_sems, recv_sems, 8 * a + k, dev)
            for k, chip in ((3, xn), (4, yn), (5, dg)):
                theirs = _rows(slot(out, chip), (1 - c) * h, h)
                cp(theirs, theirs, k, sib).wait_recv()
            cp(src, slot(out, (x, y)), 6, sib).wait_recv()
        for cp in started:
            cp.wait_send()

    shapes = [jax.ShapeDtypeStruct((N_CHIPS,) + s.shape, s.dtype) for s in shards]
    return pl.pallas_call(
        body, name="allgather_weights", out_shape=shapes, in_specs=[HBM] * n, out_specs=[HBM] * n,
        scratch_shapes=[pltpu.SemaphoreType.DMA((8 * n,)), pltpu.SemaphoreType.DMA((8 * n,))],
    )(*shards)


def _ag_copies(srcs, outs, send_sems, recv_sems):
    x, y, c = _coords()
    sib = (x, y, 1 - c)
    me = 2 * x + y
    others = [(1 - x, y), (x, 1 - y), (1 - x, 1 - y)]
    plan = []
    for a, (src, out) in enumerate(zip(srcs, outs)):
        h = src.shape[0] // 2
        cp = lambda s, d, k, dev: _remote(s, d, send_sems, recv_sems, 7 * a + k, dev)
        own = cp(src, out.at[me], 6, sib)
        sends = [cp(_rows(src, c * h, h), _rows(out.at[me], c * h, h), j, (ox, oy, c)) for j, (ox, oy) in enumerate(others)]
        mine = [_rows(out.at[2 * ox + oy], c * h, h) for ox, oy in others]
        theirs = [_rows(out.at[2 * ox + oy], (1 - c) * h, h) for ox, oy in others]
        arrivals = [cp(m, m, j, sib) for j, m in enumerate(mine)]
        forwards = [cp(m, m, 3 + j, sib) for j, m in enumerate(mine)]
        forwarded = [cp(t, t, 3 + j, sib) for j, t in enumerate(theirs)]
        plan.append((own, sends, forwards, arrivals, forwarded))
    return plan


def _ag_start(srcs, outs, send_sems, recv_sems):
    for own, sends, _, _, _ in _ag_copies(srcs, outs, send_sems, recv_sems):
        own.start()
        for cp in sends:
            cp.start()


def _ag_finish(srcs, outs, send_sems, recv_sems):
    plan = _ag_copies(srcs, outs, send_sems, recv_sems)
    for _, _, forwards, arrivals, _ in plan:
        for arrived, fwd in zip(arrivals, forwards):
            arrived.wait_recv()
            fwd.start()
    for own, sends, forwards, _, forwarded in plan:
        for cp in forwarded:
            cp.wait_recv()
        own.wait_recv()
        for cp in [own] + sends + forwards:
            cp.wait_send()


def _ag_comm(shards):
    shapes = [jax.ShapeDtypeStruct((N_CHIPS,) + s.shape, s.dtype) for s in shards]
    return (list(shards), shapes, 7 * len(shards), _ag_start, _ag_finish)


def _swap_sibling(arrs, pick_other_half, name):
    n = len(arrs)
    outs = [jax.ShapeDtypeStruct((a.shape[0], a.shape[1] // 2) + a.shape[2:] if pick_other_half else a.shape, a.dtype) for a in arrs]

    def body(*refs):
        srcs, dsts, send_sems, recv_sems = refs[:n], refs[n:2 * n], refs[2 * n], refs[2 * n + 1]
        x, y, c = _coords()
        cps = []
        for a in range(n):
            src = srcs[a]
            if pick_other_half:
                h = src.shape[1] // 2
                src = src.at[:, pl.ds(pl.multiple_of((1 - c) * h, 16), h)]
            cp = _remote(src, dsts[a], send_sems, recv_sems, a, (x, y, 1 - c))
            cp.start()
            cps.append(cp)
        for cp in cps:
            cp.wait()

    return pl.pallas_call(
        body, name=name, out_shape=outs, in_specs=[HBM] * n, out_specs=[HBM] * n,
        scratch_shapes=[pltpu.SemaphoreType.DMA((n,)), pltpu.SemaphoreType.DMA((n,))],
    )(*arrs)


def _rs_add_pairs(gs, others, c, name):
    n = len(gs)
    blocks = [(g.shape[1] // 4, g.shape[2]) for g in gs]

    def body(c_ref, *refs):
        for a in range(n):
            refs[2 * n + a][...] = (refs[a][...] + refs[n + a][...]).astype(BF16)

    mine = lambda b: pl.BlockSpec((None,) + b, lambda j, s, c_ref: (j, c_ref[0] * 2 + s, 0))
    flat = lambda b: pl.BlockSpec((None,) + b, lambda j, s, c_ref: (j, s, 0))
    return pl.pallas_call(
        body, name=name,
        grid_spec=pltpu.PrefetchScalarGridSpec(
            num_scalar_prefetch=1, grid=(N_CHIPS, 2),
            in_specs=[mine(b) for b in blocks] + [flat(b) for b in blocks],
            out_specs=[flat(b) for b in blocks]),
        out_shape=[jax.ShapeDtypeStruct(o.shape, BF16) for o in others],
        compiler_params=_params(2),
    )(c, *gs, *others)


def _rs_exchange_arrays(parts):
    n = len(parts)

    def body(*refs):
        _rsx_start(refs[:n], refs[n:2 * n], refs[2 * n], refs[2 * n + 1])
        _rsx_finish(refs[:n], refs[n:2 * n], refs[2 * n], refs[2 * n + 1])

    _, shapes, n_sems, _, _ = _rsx_comm(parts)
    return pl.pallas_call(
        body, name="rs_exchange_chips", out_shape=shapes, in_specs=[HBM] * n, out_specs=[HBM] * n,
        scratch_shapes=[pltpu.SemaphoreType.DMA((n_sems,)), pltpu.SemaphoreType.DMA((n_sems,))],
    )(*parts)


def _rsx_copies(srcs, dsts, send_sems, recv_sems):
    x, y, c = _coords()
    others = [(1 - x, y), (x, 1 - y), (1 - x, 1 - y)]
    return [_remote(src.at[2 * ox + oy], dst.at[k], send_sems, recv_sems, 3 * a + k, (ox, oy, c))
            for a, (src, dst) in enumerate(zip(srcs, dsts)) for k, (ox, oy) in enumerate(others)]


def _rsx_start(srcs, dsts, send_sems, recv_sems):
    for cp in _rsx_copies(srcs, dsts, send_sems, recv_sems):
        cp.start()


def _rsx_finish(srcs, dsts, send_sems, recv_sems):
    for cp in _rsx_copies(srcs, dsts, send_sems, recv_sems):
        cp.wait()


def _rsx_comm(parts):
    shapes = [jax.ShapeDtypeStruct((3,) + p.shape[1:], p.dtype) for p in parts]
    return (list(parts), shapes, 3 * len(parts), _rsx_start, _rsx_finish)


def _rs_add_totals(parts, recvs, chip):
    n = len(parts)
    blocks = [(p.shape[1] // 2, p.shape[2]) for p in parts]

    def body(chip_ref, *refs):
        f = lambda r: r[...].astype(F32)
        for a in range(n):
            p, r0, r1, r2 = refs[a], refs[n + 3 * a], refs[n + 3 * a + 1], refs[n + 3 * a + 2]
            refs[4 * n + a][...] = (f(p) + f(r0)) + (f(r1) + f(r2))

    own = lambda b: pl.BlockSpec((None,) + b, lambda s, chip_ref: (chip_ref[0], s, 0))
    slot = lambda b, k: pl.BlockSpec((None,) + b, lambda s, chip_ref, k=k: (k, s, 0))
    recv_specs = [slot(b, k) for b in blocks for k in range(3)]
    recv_args = [r for r in recvs for _ in range(3)]
    return pl.pallas_call(
        body, name="rs_add_totals",
        grid_spec=pltpu.PrefetchScalarGridSpec(
            num_scalar_prefetch=1, grid=(2,),
            in_specs=[own(b) for b in blocks] + recv_specs,
            out_specs=[pl.BlockSpec(b, lambda s, chip_ref: (s, 0)) for b in blocks]),
        out_shape=[jax.ShapeDtypeStruct(p.shape[1:], F32) for p in parts],
        compiler_params=_params(1),
    )(chip, *parts, *recv_args)


def _permute_w_in(w):
    return jnp.concatenate([w[:, :3072], w[:, 3080:IN_COLS], w[:, 3072:3080],
                            jnp.zeros((w.shape[0], IN_COLS_PADDED - IN_COLS), w.dtype)], axis=1)


def _pad_row(v):
    v = v.reshape(1, -1)
    return jnp.pad(v, ((0, 0), (0, D_MODEL - v.shape[1])))


def kernel(x, norm_ffn1, ffn1_gate, ffn1_up, ffn1_down, norm_mix, w_in, conv_w, a_log, dt_bias, dn_norm, w_out, norm_ffn2, ffn2_gate, ffn2_up, ffn2_down, norm_final, loss_target, m_norm_ffn1, m_ffn1_gate, m_ffn1_up, m_ffn1_down, m_norm_mix, m_w_in, m_conv_w, m_a_log, m_dt_bias, m_dn_norm, m_w_out, m_norm_ffn2, m_ffn2_gate, m_ffn2_up, m_ffn2_down, m_norm_final, v_norm_ffn1, v_ffn1_gate, v_ffn1_up, v_ffn1_down, v_norm_mix, v_w_in, v_conv_w, v_a_log, v_dt_bias, v_dn_norm, v_w_out, v_norm_ffn2, v_ffn2_gate, v_ffn2_up, v_ffn2_down, v_norm_final):
    cx, cy, cc = _coords()
    chip = 2 * cx + cy
    big_w = dict(ffn1_gate=ffn1_gate[0], ffn1_up=ffn1_up[0], ffn1_down=ffn1_down[0], w_in=w_in[0], w_out=w_out[0],
                 ffn2_gate=ffn2_gate[0], ffn2_up=ffn2_up[0], ffn2_down=ffn2_down[0])
    big_m = dict(ffn1_gate=m_ffn1_gate[0], ffn1_up=m_ffn1_up[0], ffn1_down=m_ffn1_down[0], w_in=m_w_in[0], w_out=m_w_out[0],
                 ffn2_gate=m_ffn2_gate[0], ffn2_up=m_ffn2_up[0], ffn2_down=m_ffn2_down[0])
    big_v = dict(ffn1_gate=v_ffn1_gate[0], ffn1_up=v_ffn1_up[0], ffn1_down=v_ffn1_down[0], w_in=v_w_in[0], w_out=v_w_out[0],
                 ffn2_gate=v_ffn2_gate[0], ffn2_up=v_ffn2_up[0], ffn2_down=v_ffn2_down[0])

    early = tuple(n for n in BIG if n not in LATE_WEIGHTS)
    wts = dict(zip(early, _allgather_arrays([big_w[n].astype(BF16) for n in early])))
    wts["w_in"] = _permute_w_in(jnp.concatenate([wts["w_in"][j] for j in range(N_CHIPS)], axis=1))
    dist = dict(late=[big_w[n].astype(BF16) for n in LATE_WEIGHTS], c=cc.reshape(1).astype(jnp.int32),
                chip=chip.reshape(1).astype(jnp.int32))

    conv_shard = conv_w[0]
    emb = jnp.concatenate([jnp.where((chip == j) & (cc == 0), conv_shard, 0.0) for j in range(N_CHIPS)], axis=1)
    emb = jnp.pad(emb.reshape(6, D_MODEL), ((0, 2), (0, 0)))
    conv_full = _allreduce_small(emb, "allgather_conv_w")[:6].reshape(CONV_WIDTH, 3 * DN_WIDTH)

    zvec = jnp.zeros((1, 128), F32)
    small = dict(norm_ffn1=norm_ffn1, norm_mix=norm_mix, norm_ffn2=norm_ffn2, norm_final=norm_final[None],
                 conv_w=conv_full, avec=zvec.at[0, DN_HEADS:2 * DN_HEADS].set(a_log[0]),
                 dvec=zvec.at[0, DN_HEADS:2 * DN_HEADS].set(dt_bias[0]), dn_norm=dn_norm)

    loss, grad_x, reduced, sg = _local_step(x[0], loss_target[0], wts, small, dist)

    rows = [sg["norm_ffn1"], sg["norm_mix"], sg["norm_ffn2"], sg["norm_final"], _pad_row(sg["a_log"]), _pad_row(sg["dt_bias"]),
            _pad_row(sg["dn_norm"]), _pad_row(loss[0:1]), sg["conv_w"].reshape(6, D_MODEL), jnp.zeros((2, D_MODEL), F32)]
    red = _allreduce_small(jnp.concatenate(rows, axis=0), "allreduce_small")
    loss_out = red[7, 0]
    g_conv_full = red[8:14].reshape(CONV_WIDTH, 3 * DN_WIDTH)
    g_conv = lax.dynamic_slice_in_dim(g_conv_full, chip * (3 * DN_WIDTH // N_CHIPS), 3 * DN_WIDTH // N_CHIPS, axis=1)
    g_small = dict(norm_ffn1=red[0:1], norm_mix=red[1:2], norm_ffn2=red[2:3], norm_final=red[3],
                   a_log=red[4:5, DN_HEADS:2 * DN_HEADS], dt_bias=red[5:6, DN_HEADS:2 * DN_HEADS], dn_norm=red[6:7, :DN_HEAD_DIM])

    shard_g = {}
    for n in BIG:
        mine, other = reduced[n]
        full = jnp.where(cc == 0, jnp.concatenate([mine, other], axis=0), jnp.concatenate([other, mine], axis=0))
        if n == "w_in":
            full = full[:IN_COLS // N_CHIPS]
        shard_g[n] = full if n in ROW_SHARDED else full.T

    out_g, out_d, out_m, out_v = {}, {}, {}, {}
    for n in BIG:
        d, nm, nv = _adamw(big_w[n], shard_g[n], big_m[n], big_v[n], "adamw_" + n)
        out_g[n], out_d[n], out_m[n], out_v[n] = shard_g[n][None], d[None], nm[None], nv[None]
    d, nm, nv = _adamw(conv_w[0], g_conv, m_conv_w[0], v_conv_w[0], "adamw_conv_w")
    out_g["conv_w"], out_d["conv_w"], out_m["conv_w"], out_v["conv_w"] = g_conv[None], d[None], nm[None], nv[None]

    small_names = ("norm_ffn1", "norm_mix", "norm_ffn2", "norm_final", "a_log", "dt_bias", "dn_norm")
    small_w = dict(norm_ffn1=norm_ffn1, norm_mix=norm_mix, norm_ffn2=norm_ffn2, norm_final=norm_final, a_log=a_log,
                   dt_bias=dt_bias, dn_norm=dn_norm)
    small_m = dict(norm_ffn1=m_norm_ffn1, norm_mix=m_norm_mix, norm_ffn2=m_norm_ffn2, norm_final=m_norm_final, a_log=m_a_log,
                   dt_bias=m_dt_bias, dn_norm=m_dn_norm)
    small_v = dict(norm_ffn1=v_norm_ffn1, norm_mix=v_norm_mix, norm_ffn2=v_norm_ffn2, norm_final=v_norm_final, a_log=v_a_log,
                   dt_bias=v_dt_bias, dn_norm=v_dn_norm)
    stack = lambda dct: jnp.concatenate([_pad_row(dct[n]) for n in small_names] + [jnp.zeros((1, D_MODEL), F32)], axis=0)
    d, nm, nv = _adamw(stack(small_w), stack(g_small), stack(small_m), stack(small_v), "adamw_small")
    for k, n in enumerate(small_names):
        shape = small_w[n].shape
        size = math.prod(shape)
        out_g[n] = g_small[n].reshape(shape)
        out_d[n], out_m[n], out_v[n] = (t[k, :size].reshape(shape) for t in (d, nm, nv))

    order = ("norm_ffn1", "ffn1_gate", "ffn1_up", "ffn1_down", "norm_mix", "w_in", "conv_w", "a_log", "dt_bias", "dn_norm",
             "w_out", "norm_ffn2", "ffn2_gate", "ffn2_up", "ffn2_down", "norm_final")
    return (loss_out, grad_x[None], *[out_g[n] for n in order], *[out_d[n] for n in order],
            *[out_m[n] for n in order], *[out_v[n] for n in order])
```

```python
import functools
import math

import jax
import jax.numpy as jnp
from jax import lax
from jax.experimental import pallas as pl
from jax.experimental.pallas import tpu as pltpu

F32 = jnp.float32
BF16 = jnp.bfloat16
HI = lax.Precision.HIGH

D_MODEL = 1024
ATTN_HEADS = 8
ATTN_WIDTH = 512
ATTN_BLOCK = 128
DILATIONS = (1, 4, 16)
DN_HEADS = 4
DN_HEAD_DIM = 128
DN_WIDTH = 512
DN_CHUNK = 64
CONV_WIDTH = 4
NORM_EPS = 1e-6
L2_EPS = 1e-6
IN_COLS = 3592
IN_COLS_PADDED = 3712
N_CHIPS = 4

ADAM_LR = 0.001
ADAM_B1 = 0.9
ADAM_B2 = 0.999
ADAM_EPS = 1e-08
ADAM_WD = 0.01
ADAM_STEP = 10

VMEM_LIMIT = 56 * 1024 * 1024
NEG_BIG = -1e30
MESH = pl.DeviceIdType.MESH


def _params(n_grid, vmem=VMEM_LIMIT):
    return pltpu.CompilerParams(dimension_semantics=("arbitrary",) * n_grid, vmem_limit_bytes=vmem)


def _call(body, args, *, name, grid, in_specs, out_specs, out_shape, scratch_shapes=(), comm=None):
    n_in, n_out, n_scr = len(in_specs), len(out_specs), len(scratch_shapes)
    hbm = pl.BlockSpec(memory_space=pl.ANY)
    srcs, dst_shapes, n_sems, start, finish = comm if comm is not None else ((), (), 0, None, None)
    ns, nd = len(srcs), len(dst_shapes)

    def full(*refs):
        ins, c_src = refs[:n_in], refs[n_in:n_in + ns]
        at = n_in + ns
        outs, c_dst = refs[at:at + n_out], refs[at + n_out:at + n_out + nd]
        scr = refs[at + n_out + nd:at + n_out + nd + n_scr]
        if comm is not None:
            ids = [pl.program_id(a) for a in range(len(grid))]
            first = functools.reduce(jnp.logical_and, [i == 0 for i in ids])
            last = functools.reduce(jnp.logical_and, [i == g - 1 for i, g in zip(ids, grid)])

            @pl.when(first)
            def _():
                start(c_src, c_dst, refs[-2], refs[-1])

        body(*ins, *outs, *scr)
        if comm is not None:
            @pl.when(last)
            def _():
                finish(c_src, c_dst, refs[-2], refs[-1])

    sems = [pltpu.SemaphoreType.DMA((n_sems,)), pltpu.SemaphoreType.DMA((n_sems,))] if comm is not None else []
    res = pl.pallas_call(
        full, name=name, grid=grid, in_specs=list(in_specs) + [hbm] * ns, out_specs=list(out_specs) + [hbm] * nd,
        out_shape=list(out_shape) + list(dst_shapes), scratch_shapes=list(scratch_shapes) + sems,
        compiler_params=_params(len(grid)),
    )(*args, *srcs)
    return res[:n_out], res[n_out:]


def _nt(a, b, precision=None):
    return lax.dot_general(a, b, (((1,), (1,)), ((), ())), preferred_element_type=F32, precision=precision)


def _tn(a, b, precision=None):
    return lax.dot_general(a, b, (((0,), (0,)), ((), ())), preferred_element_type=F32, precision=precision)


def _nn(a, b, precision=None):
    return jnp.dot(a, b, preferred_element_type=F32, precision=precision)


def _sigmoid(x):
    return 1.0 / (1.0 + jnp.exp(-x))


def _ffn_fwd(x, gain, wg, wu, wd, name, comm=None):
    S, D = x.shape
    nf, tf, _ = wg.shape
    tm = 512

    def body(x_ref, gain_ref, wg_ref, wu_ref, wd_ref, xo_ref, h_ref, g_ref, u_ref, acc_ref, hs_ref):
        j = pl.program_id(1)

        @pl.when(j == 0)
        def _():
            xf = x_ref[...]
            r = lax.rsqrt(jnp.mean(xf * xf, axis=-1, keepdims=True) + NORM_EPS)
            h = (xf * r * gain_ref[...]).astype(BF16)
            hs_ref[...] = h
            h_ref[...] = h
            acc_ref[...] = jnp.zeros_like(acc_ref)

        h = hs_ref[...]
        g = _nt(h, wg_ref[...])
        u = _nt(h, wu_ref[...])
        g_ref[...] = g.astype(BF16)
        u_ref[...] = u.astype(BF16)
        act = g * _sigmoid(g) * u
        acc_ref[...] += _nn(act.astype(BF16), wd_ref[...])

        @pl.when(j == nf - 1)
        def _():
            xo_ref[...] = x_ref[...] + 0.5 * acc_ref[...]

    return _call(
        body, (x, gain, wg, wu, wd), name=name, grid=(S // tm, nf), comm=comm,
        in_specs=[pl.BlockSpec((tm, D), lambda i, j: (i, 0)),
                  pl.BlockSpec((1, D), lambda i, j: (0, 0)),
                  pl.BlockSpec((None, tf, D), lambda i, j: (j, 0, 0)),
                  pl.BlockSpec((None, tf, D), lambda i, j: (j, 0, 0)),
                  pl.BlockSpec((None, tf, D), lambda i, j: (j, 0, 0))],
        out_specs=[pl.BlockSpec((tm, D), lambda i, j: (i, 0)),
                   pl.BlockSpec((tm, D), lambda i, j: (i, 0)),
                   pl.BlockSpec((None, tm, tf), lambda i, j: (j, i, 0)),
                   pl.BlockSpec((None, tm, tf), lambda i, j: (j, i, 0))],
        out_shape=[jax.ShapeDtypeStruct((S, D), F32), jax.ShapeDtypeStruct((S, D), BF16),
                   jax.ShapeDtypeStruct((nf, S, tf), BF16), jax.ShapeDtypeStruct((nf, S, tf), BF16)],
        scratch_shapes=[pltpu.VMEM((tm, D), F32), pltpu.VMEM((tm, D), BF16)])


def _rmsnorm_bwd(dh, xf, gain):
    r = lax.rsqrt(jnp.mean(xf * xf, axis=-1, keepdims=True) + NORM_EPS)
    xhat = xf * r
    dgain = jnp.sum(dh * xhat, axis=0, keepdims=True)
    dxh = dh * gain
    dx = r * (dxh - xhat * jnp.mean(dxh * xhat, axis=-1, keepdims=True))
    return dx, dgain


def _ffn_bwd(dxo, x, gain, g, u, wd, wg, wu, name, comm=None):
    S, D = x.shape
    nf, _, tf = g.shape
    tm = 512

    def body(dxo_ref, x_ref, gain_ref, g_ref, u_ref, wd_ref, wg_ref, wu_ref,
             dx_ref, dgain_ref, dg_ref, du_ref, act_ref, dout_ref, acc_ref, ds_ref):
        i = pl.program_id(0)
        j = pl.program_id(1)

        @pl.when(j == 0)
        def _():
            d = (0.5 * dxo_ref[...]).astype(BF16)
            ds_ref[...] = d
            dout_ref[...] = d
            acc_ref[...] = jnp.zeros_like(acc_ref)

        @pl.when((i == 0) & (j == 0))
        def _():
            dgain_ref[...] = jnp.zeros_like(dgain_ref)

        for half in range(2):
            rows = slice(half * (tm // 2), (half + 1) * (tm // 2))
            dact = _nt(ds_ref[rows, :], wd_ref[...])
            gv = g_ref[rows, :].astype(F32)
            uv = u_ref[rows, :].astype(F32)
            sg = _sigmoid(gv)
            silu = gv * sg
            act_ref[rows, :] = (silu * uv).astype(BF16)
            dgv = (dact * uv * (sg * (1.0 + gv * (1.0 - sg)))).astype(BF16)
            duv = (dact * silu).astype(BF16)
            dg_ref[rows, :] = dgv
            du_ref[rows, :] = duv
            acc_ref[rows, :] += _nn(dgv, wg_ref[...]) + _nn(duv, wu_ref[...])

        @pl.when(j == nf - 1)
        def _():
            dx, dgain = _rmsnorm_bwd(acc_ref[...], x_ref[...], gain_ref[...])
            dx_ref[...] = dxo_ref[...] + dx
            dgain_ref[...] += dgain

    return _call(
        body, (dxo, x, gain, g, u, wd, wg, wu), name=name, grid=(S // tm, nf), comm=comm,
        in_specs=[pl.BlockSpec((tm, D), lambda i, j: (i, 0)),
                  pl.BlockSpec((tm, D), lambda i, j: (i, 0)),
                  pl.BlockSpec((1, D), lambda i, j: (0, 0)),
                  pl.BlockSpec((None, tm, tf), lambda i, j: (j, i, 0)),
                  pl.BlockSpec((None, tm, tf), lambda i, j: (j, i, 0)),
                  pl.BlockSpec((None, tf, D), lambda i, j: (j, 0, 0)),
                  pl.BlockSpec((None, tf, D), lambda i, j: (j, 0, 0)),
                  pl.BlockSpec((None, tf, D), lambda i, j: (j, 0, 0))],
        out_specs=[pl.BlockSpec((tm, D), lambda i, j: (i, 0)),
                   pl.BlockSpec((1, D), lambda i, j: (0, 0)),
                   pl.BlockSpec((None, tm, tf), lambda i, j: (j, i, 0)),
                   pl.BlockSpec((None, tm, tf), lambda i, j: (j, i, 0)),
                   pl.BlockSpec((None, tm, tf), lambda i, j: (j, i, 0)),
                   pl.BlockSpec((tm, D), lambda i, j: (i, 0))],
        out_shape=[jax.ShapeDtypeStruct((S, D), F32), jax.ShapeDtypeStruct((1, D), F32),
                   jax.ShapeDtypeStruct((nf, S, tf), BF16), jax.ShapeDtypeStruct((nf, S, tf), BF16),
                   jax.ShapeDtypeStruct((nf, S, tf), BF16), jax.ShapeDtypeStruct((S, D), BF16)],
        scratch_shapes=[pltpu.VMEM((tm, D), F32), pltpu.VMEM((tm, D), BF16)])


def _matmul_tn(a, b, tm, tk, name):
    K, M = a.shape
    N = b.shape[1]

    def body(a_ref, b_ref, o_ref):
        @pl.when(pl.program_id(1) == 0)
        def _():
            o_ref[...] = jnp.zeros_like(o_ref)

        o_ref[...] += _tn(a_ref[...], b_ref[...])

    return pl.pallas_call(
        body, name=name, grid=(M // tm, K // tk),
        in_specs=[pl.BlockSpec((tk, tm), lambda i, k: (k, i)),
                  pl.BlockSpec((tk, N), lambda i, k: (k, 0))],
        out_specs=pl.BlockSpec((tm, N), lambda i, k: (i, 0)),
        out_shape=jax.ShapeDtypeStruct((M, N), F32),
        compiler_params=_params(2),
    )(a, b)


def _dw_chunks(a, b, tk, name, comm=None):
    nf, S, tf = a.shape
    N = b.shape[1]

    def body(a_ref, b_ref, o_ref):
        @pl.when(pl.program_id(1) == 0)
        def _():
            o_ref[...] = jnp.zeros_like(o_ref)

        o_ref[...] += _tn(a_ref[...], b_ref[...])

    (out,), landed = _call(
        body, (a, b), name=name, grid=(nf, S // tk), comm=comm,
        in_specs=[pl.BlockSpec((None, tk, tf), lambda j, k: (j, k, 0)),
                  pl.BlockSpec((tk, N), lambda j, k: (k, 0))],
        out_specs=[pl.BlockSpec((None, tf, N), lambda j, k: (j, 0, 0))],
        out_shape=[jax.ShapeDtypeStruct((nf, tf, N), F32)])
    return out, landed


VIEW_TILE = 512


def _view_spec(d, tile=VIEW_TILE):
    return pl.BlockSpec((tile // d, d * ATTN_WIDTH), lambda i: (i, 0))


def _view_shape(S, d, dtype):
    return jax.ShapeDtypeStruct((S // d, d * ATTN_WIDTH), dtype)


def _tile_to_views(val, planes, out_refs):
    for g in range(4):
        planes[g] = val[:, g * 128:(g + 1) * 128]
    for d, ref in zip(DILATIONS, out_refs):
        if d == 1:
            ref[...] = val.astype(ref.dtype)
            continue
        for r in range(d):
            for g in range(4):
                ref[:, r * ATTN_WIDTH + g * 128:r * ATTN_WIDTH + (g + 1) * 128] = (
                    planes[g, pl.ds(r, planes.shape[1] // d, stride=d), :].astype(ref.dtype))


def _view_to_tile(ref, d, planes):
    if d == 1:
        return ref[...].astype(F32)
    for r in range(d):
        for g in range(4):
            planes[g, pl.ds(r, planes.shape[1] // d, stride=d), :] = (
                ref[:, r * ATTN_WIDTH + g * 128:r * ATTN_WIDTH + (g + 1) * 128].astype(F32))
    return jnp.concatenate([planes[g] for g in range(4)], axis=1)


def _inproj_fwd(x, gain, w_in_p):
    S, D = x.shape
    tm = VIEW_TILE
    W = ATTN_WIDTH

    def body(x_ref, gain_ref, w_ref, h_ref, q1, q4, q16, k1, k4, k16, v1, v4, v16, dq_ref, dk_ref, dv_ref, gate_ref, bd_ref,
             planes):
        xf = x_ref[...]
        r = lax.rsqrt(jnp.mean(xf * xf, axis=-1, keepdims=True) + NORM_EPS)
        h = (xf * r * gain_ref[...]).astype(BF16)
        h_ref[...] = h
        _tile_to_views(_nt(h, w_ref[0:W, :]) * 0.125, planes, (q1, q4, q16))
        _tile_to_views(_nt(h, w_ref[W:2 * W, :]), planes, (k1, k4, k16))
        _tile_to_views(_nt(h, w_ref[2 * W:3 * W, :]), planes, (v1, v4, v16))
        dq_ref[...] = _nt(h, w_ref[3 * W:4 * W, :])
        dk_ref[...] = _nt(h, w_ref[4 * W:5 * W, :])
        dv_ref[...] = _nt(h, w_ref[5 * W:6 * W, :])
        gate_ref[...] = _nt(h, w_ref[6 * W:7 * W, :])
        bd_ref[...] = _nt(h, w_ref[7 * W:7 * W + 128, :])

    tok = lambda w: pl.BlockSpec((tm, w), lambda i: (i, 0))
    return pl.pallas_call(
        body, name="inproj_fwd", grid=(S // tm,),
        in_specs=[tok(D), pl.BlockSpec((1, D), lambda i: (0, 0)),
                  pl.BlockSpec((IN_COLS_PADDED, D), lambda i: (0, 0))],
        out_specs=[tok(D)] + [_view_spec(d) for d in DILATIONS] * 3 + [tok(W)] * 4 + [tok(128)],
        out_shape=[jax.ShapeDtypeStruct((S, D), BF16)] + [_view_shape(S, d, BF16) for d in DILATIONS] * 3
                  + [jax.ShapeDtypeStruct((S, W), F32)] * 4 + [jax.ShapeDtypeStruct((S, 128), F32)],
        scratch_shapes=[pltpu.VMEM((4, tm, 128), F32)],
        compiler_params=_params(1),
    )(x, gain, w_in_p)


def _inproj_bwd(dxo, x, gain, attn_grads, dsecs, dbd, w_in_p):
    S, D = x.shape
    tm = VIEW_TILE
    W = ATTN_WIDTH

    def body(dxo_ref, x_ref, gain_ref, *rest):
        views, (s3, s4, s5, s6, dbd_ref, w_ref, dx_ref, dgain_ref, dproj_ref, planes) = rest[:9], rest[9:]

        @pl.when(pl.program_id(0) == 0)
        def _():
            dgain_ref[...] = jnp.zeros_like(dgain_ref)

        secs = []
        for k in range(3):
            parts = [_view_to_tile(views[3 * k + p], d, planes) for p, d in enumerate(DILATIONS)]
            secs.append(parts[0] + parts[1] + parts[2])
        secs += [s3[...], s4[...], s5[...], s6[...]]
        dh = jnp.zeros((tm, D), F32)
        for k, s in enumerate(secs):
            d = s.astype(BF16)
            dproj_ref[:, k * W:(k + 1) * W] = d
            dh += _nn(d, w_ref[k * W:(k + 1) * W, :])
        d = dbd_ref[...].astype(BF16)
        dproj_ref[:, 7 * W:7 * W + 128] = d
        dh += _nn(d, w_ref[7 * W:7 * W + 128, :])
        dx, dgain = _rmsnorm_bwd(dh, x_ref[...], gain_ref[...])
        dx_ref[...] = dxo_ref[...] + dx
        dgain_ref[...] += dgain

    tok = lambda w: pl.BlockSpec((tm, w), lambda i: (i, 0))
    return pl.pallas_call(
        body, name="inproj_bwd", grid=(S // tm,),
        in_specs=[tok(D), tok(D), pl.BlockSpec((1, D), lambda i: (0, 0))] + [_view_spec(d, tm) for d in DILATIONS] * 3
                 + [tok(W)] * 4 + [tok(128)] + [pl.BlockSpec((IN_COLS_PADDED, D), lambda i: (0, 0))],
        out_specs=[tok(D), pl.BlockSpec((1, D), lambda i: (0, 0)), tok(IN_COLS_PADDED)],
        out_shape=[jax.ShapeDtypeStruct((S, D), F32), jax.ShapeDtypeStruct((1, D), F32),
                   jax.ShapeDtypeStruct((S, IN_COLS_PADDED), BF16)],
        scratch_shapes=[pltpu.VMEM((4, tm, 128), F32)],
        compiler_params=_params(1),
    )(dxo, x, gain, *[g for grads in attn_grads for g in grads], *dsecs, dbd, w_in_p)


def _slope(h):
    return 2.0 ** (-8.0 * (h + 1) / ATTN_HEADS)


def _head_bias(steps, d, heads=tuple(range(ATTN_HEADS))):
    stepsf = steps.astype(F32)
    return jnp.stack([stepsf * (-_slope(h) * d) for h in heads])


def _hnt(a, b):
    return lax.dot_general(a, b, (((2,), (2,)), ((0,), (0,))), preferred_element_type=F32)


def _hnn(a, b):
    return lax.dot_general(a, b, (((2,), (1,)), ((0,), (0,))), preferred_element_type=F32)


def _blocks_per_step(nb):
    return next(n for n in (4, 2, 1) if nb % n == 0)


def _query_step_specs(qb):
    B = ATTN_BLOCK
    cur = pl.BlockSpec((qb * B, ATTN_WIDTH), lambda r, n: (n, r))
    prev = pl.BlockSpec((B, ATTN_WIDTH), lambda r, n: (jnp.maximum(qb * n - 1, 0), r))
    return cur, prev


def _prev_block(prev_ref, cur_ref, sub, sl):
    B = ATTN_BLOCK
    return prev_ref[:, sl] if sub == 0 else cur_ref[(sub - 1) * B:sub * B, sl]


def _head_cols(tile, lo, big):
    return [_head_col(tile, lo, big), _head_col(tile, jnp.logical_not(lo), big)]


def _attn_fwd(q, k, v, d, name):
    L = q.shape[0]
    nb = L // ATTN_BLOCK
    B = ATTN_BLOCK
    QB = _blocks_per_step(nb)

    def body(q_ref, kp_ref, kc_ref, vp_ref, vc_ref, o_ref, lse_ref):
        n = pl.program_id(1)
        qi = lax.broadcasted_iota(jnp.int32, (B, 2 * B), 0)
        kj = lax.broadcasted_iota(jnp.int32, (B, 2 * B), 1)
        steps = qi + B - kj
        band = (steps >= 0) & (steps <= B)
        lo = lax.broadcasted_iota(jnp.int32, (B, 128), 1) < 64
        bias = _head_bias(steps, d)
        for sub in range(QB):
            rows = slice(sub * B, (sub + 1) * B)
            valid = band & ((kj >= B) | (n > 0)) if sub == 0 else band
            qs, ks, vs = [], [], []
            for G in range(4):
                sl = slice(G * 128, (G + 1) * 128)
                qg = q_ref[rows, sl]
                kg = jnp.concatenate([_prev_block(kp_ref, kc_ref, sub, sl), kc_ref[rows, sl]], axis=0)
                vg = jnp.concatenate([_prev_block(vp_ref, vc_ref, sub, sl), vc_ref[rows, sl]], axis=0)
                qs += [jnp.where(lo, qg, jnp.zeros_like(qg)), jnp.where(lo, jnp.zeros_like(qg), qg)]
                ks += [kg, kg]
                vs += [vg, vg]
            s = jnp.where(valid, _hnt(jnp.stack(qs), jnp.stack(ks)) + bias, NEG_BIG)
            m = jnp.max(s, axis=-1, keepdims=True)
            p = jnp.exp(s - m)
            l = jnp.sum(p, axis=-1, keepdims=True)
            o = _hnn(p.astype(BF16), jnp.stack(vs)) / l
            lse = m + jnp.log(l)
            for G in range(4):
                sl = slice(G * 128, (G + 1) * 128)
                o_ref[rows, sl] = jnp.where(lo, o[2 * G], o[2 * G + 1])
                lse_ref[rows, sl] = jnp.where(lo, lse[2 * G], lse[2 * G + 1])

    cur, prev = _query_step_specs(QB)
    return pl.pallas_call(
        body, name=name, grid=(d, nb // QB),
        in_specs=[cur, prev, cur, prev, cur],
        out_specs=[cur, cur],
        out_shape=[jax.ShapeDtypeStruct((L, d * ATTN_WIDTH), F32)] * 2,
        compiler_params=_params(2),
    )(q, k, k, v, v)


def _attn_merge(parts):
    S = parts[0][0].shape[0]
    tm = VIEW_TILE

    def body(o1, s1, o2, s2, o3, s3, o_ref, lse1, lse4, lse16, planes):
        outs, lses = [], []
        for d, (o, s) in zip(DILATIONS, ((o1, s1), (o2, s2), (o3, s3))):
            outs.append(_view_to_tile(o, d, planes))
            lses.append(_view_to_tile(s, d, planes))
        mx = jnp.maximum(jnp.maximum(lses[0], lses[1]), lses[2])
        es = [jnp.exp(s - mx) for s in lses]
        den = es[0] + es[1] + es[2]
        o_ref[...] = (es[0] * outs[0] + es[1] * outs[1] + es[2] * outs[2]) / den
        _tile_to_views(mx + jnp.log(den), planes, (lse1, lse4, lse16))

    views = [_view_spec(d) for d in DILATIONS]
    flat = [t for p in parts for t in p]
    return pl.pallas_call(
        body, name="attn_merge", grid=(S // tm,),
        in_specs=[views[p] for p in range(3) for _ in range(2)],
        out_specs=[views[0]] + views,
        out_shape=[jax.ShapeDtypeStruct((S, ATTN_WIDTH), F32)] + [_view_shape(S, d, F32) for d in DILATIONS],
        scratch_shapes=[pltpu.VMEM((4, tm, 128), F32)],
        compiler_params=_params(1),
    )(*flat)


def _head_col(t, msk, big):
    if big:
        return jnp.max(jnp.where(msk, t, NEG_BIG), axis=-1, keepdims=True)
    return jnp.sum(jnp.where(msk, t, 0.0), axis=-1, keepdims=True) * (1.0 / 64.0)


def _attn_bwd_q(q, k, v, do, lse, dd, d, name):
    L = q.shape[0]
    nb = L // ATTN_BLOCK
    B = ATTN_BLOCK
    QB = _blocks_per_step(nb)

    def body(q_ref, kp_ref, kc_ref, vp_ref, vc_ref, do_ref, lse_ref, dd_ref, dq_ref):
        n = pl.program_id(1)
        qi = lax.broadcasted_iota(jnp.int32, (B, 2 * B), 0)
        kj = lax.broadcasted_iota(jnp.int32, (B, 2 * B), 1)
        steps = qi + B - kj
        band = (steps >= 0) & (steps <= B)
        lo = lax.broadcasted_iota(jnp.int32, (B, 128), 1) < 64
        bias = _head_bias(steps, d)
        for sub in range(QB):
            rows = slice(sub * B, (sub + 1) * B)
            valid = band & ((kj >= B) | (n > 0)) if sub == 0 else band
            qs, ks, vs, dos, lses, dcols = [], [], [], [], [], []
            for G in range(4):
                sl = slice(G * 128, (G + 1) * 128)
                qg = q_ref[rows, sl]
                kg = jnp.concatenate([_prev_block(kp_ref, kc_ref, sub, sl), kc_ref[rows, sl]], axis=0)
                vg = jnp.concatenate([_prev_block(vp_ref, vc_ref, sub, sl), vc_ref[rows, sl]], axis=0)
                dog = do_ref[rows, sl]
                qs += [jnp.where(lo, qg, jnp.zeros_like(qg)), jnp.where(lo, jnp.zeros_like(qg), qg)]
                dos += [jnp.where(lo, dog, 0.0).astype(BF16), jnp.where(lo, 0.0, dog).astype(BF16)]
                ks += [kg, kg]
                vs += [vg, vg]
                lses += _head_cols(lse_ref[rows, sl], lo, True)
                dcols += _head_cols(dd_ref[rows, sl], lo, False)
            kb = jnp.stack(ks)
            s = _hnt(jnp.stack(qs), kb) + bias
            p = jnp.where(valid, jnp.exp(jnp.where(valid, s, NEG_BIG) - jnp.stack(lses)), 0.0)
            dp = _hnt(jnp.stack(dos), jnp.stack(vs))
            ds = p * (dp - jnp.stack(dcols))
            dq = _hnn(ds.astype(BF16), kb) * 0.125
            for G in range(4):
                dq_ref[rows, G * 128:(G + 1) * 128] = jnp.where(lo, dq[2 * G], dq[2 * G + 1]).astype(BF16)

    cur, prev = _query_step_specs(QB)
    return pl.pallas_call(
        body, name=name, grid=(d, nb // QB), in_specs=[cur, prev, cur, prev, cur, cur, cur, cur], out_specs=cur,
        out_shape=jax.ShapeDtypeStruct((L, d * ATTN_WIDTH), BF16), compiler_params=_params(2),
    )(q, k, k, v, v, do, lse, dd)


def _attn_bwd_kv(q, k, v, do, lse, dd, d, name):
    L = q.shape[0]
    nb = L // ATTN_BLOCK
    B = ATTN_BLOCK
    KB = _blocks_per_step(nb)
    n_steps = nb // KB

    def body(k_ref, v_ref, qc_ref, qn_ref, doc_ref, don_ref, lsec_ref, lsen_ref, ddc_ref, ddn_ref, dk_ref, dv_ref):
        j = pl.program_id(1)
        qrow = lax.broadcasted_iota(jnp.int32, (2 * B, B), 0)
        kk = lax.broadcasted_iota(jnp.int32, (2 * B, B), 1)
        steps = qrow - kk
        band = (steps >= 0) & (steps <= B)
        lo2 = lax.broadcasted_iota(jnp.int32, (2 * B, 128), 1) < 64
        lo = lax.broadcasted_iota(jnp.int32, (B, 128), 1) < 64
        stepsf = steps.astype(F32)
        for sub in range(KB):
            rows = slice(sub * B, (sub + 1) * B)
            last = sub == KB - 1
            valid = band & ((qrow < B) | (j < n_steps - 1)) if last else band
            after = lambda cur_ref, nxt_ref, sl: nxt_ref[:, sl] if last else cur_ref[(sub + 1) * B:(sub + 2) * B, sl]
            for G in range(4):
                sl = slice(G * 128, (G + 1) * 128)
                kg = k_ref[rows, sl]
                vg = v_ref[rows, sl]
                qq = jnp.concatenate([qc_ref[rows, sl], after(qc_ref, qn_ref, sl)], axis=0)
                doo = jnp.concatenate([doc_ref[rows, sl], after(doc_ref, don_ref, sl)], axis=0)
                lse2 = jnp.concatenate([lsec_ref[rows, sl], after(lsec_ref, lsen_ref, sl)], axis=0)
                dd2 = jnp.concatenate([ddc_ref[rows, sl], after(ddc_ref, ddn_ref, sl)], axis=0)
                doo_b = doo.astype(BF16)
                dks, dvs = [], []
                for half in (0, 1):
                    msk = lo2 if half == 0 else jnp.logical_not(lo2)
                    qm = jnp.where(msk, qq, jnp.zeros_like(qq))
                    s = _nt(qm, kg) - (_slope(2 * G + half) * d) * stepsf
                    lse_c = _head_col(lse2, msk, True)
                    p = jnp.where(valid, jnp.exp(jnp.where(valid, s, NEG_BIG) - lse_c), 0.0)
                    dvs.append(_tn(p.astype(BF16), doo_b))
                    dom = jnp.where(msk, doo, 0.0).astype(BF16)
                    dp = _nt(dom, vg)
                    dcol = _head_col(dd2, msk, False)
                    ds = p * (dp - dcol)
                    dks.append(_tn(ds.astype(BF16), qq))
                dk_ref[rows, sl] = jnp.where(lo, dks[0], dks[1]).astype(BF16)
                dv_ref[rows, sl] = jnp.where(lo, dvs[0], dvs[1]).astype(BF16)

    cur = pl.BlockSpec((KB * B, ATTN_WIDTH), lambda r, j: (j, r))
    nxt = pl.BlockSpec((B, ATTN_WIDTH), lambda r, j: (jnp.minimum(KB * (j + 1), nb - 1), r))
    return pl.pallas_call(
        body, name=name, grid=(d, n_steps), in_specs=[cur, cur, cur, nxt, cur, nxt, cur, nxt, cur, nxt],
        out_specs=[cur, cur],
        out_shape=[jax.ShapeDtypeStruct((L, d * ATTN_WIDTH), BF16)] * 2, compiler_params=_params(2),
    )(k, v, q, q, do, do, lse, lse, dd, dd)


CONV_T = 512
HALO = 8


def _per_head(head, refs):
    for h in range(DN_HEADS):
        lanes = pl.ds(h * DN_HEAD_DIM, DN_HEAD_DIM)
        head(*[r.at[:, lanes] for r in refs[:-1]], refs[-1])


def _conv_taps(pad_ref, w, T):
    acc = pad_ref[pl.ds(HALO - 3, T), :] * w[0:1, :]
    for j in range(1, CONV_WIDTH):
        acc = acc + pad_ref[pl.ds(HALO - 3 + j, T), :] * w[j:j + 1, :]
    return acc


def _conv_fwd(xq, xk, xv, conv_w):
    S = xq.shape[0]
    T = CONV_T

    def body(*refs):
        _per_head(head, refs)

    def head(xq_ref, xqh_ref, xk_ref, xkh_ref, xv_ref, xvh_ref, wq_ref, wk_ref, wv_ref,
             qn_ref, kn_ref, v_ref, pad_ref):
        i = pl.program_id(0)

        def act(x_ref, xh_ref, w_ref):
            pad_ref[pl.ds(0, HALO), :] = jnp.where(i > 0, xh_ref[...], 0.0)
            pad_ref[pl.ds(HALO, T), :] = x_ref[...]
            c = _conv_taps(pad_ref, w_ref[...], T)
            return c * _sigmoid(c)

        def l2n(t):
            return t * lax.rsqrt(jnp.sum(t * t, axis=-1, keepdims=True) + L2_EPS)

        qn_ref[...] = l2n(act(xq_ref, xqh_ref, wq_ref))
        kn_ref[...] = l2n(act(xk_ref, xkh_ref, wk_ref))
        v_ref[...] = act(xv_ref, xvh_ref, wv_ref)

    tile = pl.BlockSpec((T, DN_WIDTH), lambda i: (i, 0))
    halo = pl.BlockSpec((HALO, DN_WIDTH), lambda i: (jnp.maximum(i * (T // HALO) - 1, 0), 0))
    wspec = lambda sec: pl.BlockSpec((CONV_WIDTH, DN_WIDTH), lambda i, sec=sec: (0, sec))
    return pl.pallas_call(
        body, name="dn_conv_fwd", grid=(S // T,),
        in_specs=[tile, halo, tile, halo, tile, halo, wspec(0), wspec(1), wspec(2)],
        out_specs=[tile, tile, tile],
        out_shape=[jax.ShapeDtypeStruct((S, DN_WIDTH), F32)] * 3,
        scratch_shapes=[pltpu.VMEM((T + HALO, 128), F32)],
        compiler_params=_params(1),
    )(xq, xq, xk, xk, xv, xv, conv_w, conv_w, conv_w)


def _conv_bwd_pre(xq, xk, xv, conv_w, dqn, dkn, dv):
    S = xq.shape[0]
    T = CONV_T

    def body(*refs):
        _per_head(head, refs)

    def head(xq_ref, xqh_ref, xk_ref, xkh_ref, xv_ref, xvh_ref, wq_ref, wk_ref, wv_ref,
             dqn_ref, dkn_ref, dv_ref, dcq_ref, dck_ref, dcv_ref, dwq_ref, dwk_ref, dwv_ref, pad_ref):
        i = pl.program_id(0)

        def one(x_ref, xh_ref, w_ref, dy_ref, dc_ref, dw_ref, normed):
            pad_ref[pl.ds(0, HALO), :] = jnp.where(i > 0, xh_ref[...], 0.0)
            pad_ref[pl.ds(HALO, T), :] = x_ref[...]
            c = _conv_taps(pad_ref, w_ref[...], T)
            sg = _sigmoid(c)
            a = c * sg
            dy = dy_ref[...]
            if normed:
                r = lax.rsqrt(jnp.sum(a * a, axis=-1, keepdims=True) + L2_EPS)
                y = a * r
                da = r * (dy - y * jnp.sum(dy * y, axis=-1, keepdims=True))
            else:
                da = dy
            dc = da * (sg * (1.0 + c * (1.0 - sg)))
            dc_ref[...] = dc

            @pl.when(i == 0)
            def _():
                dw_ref[...] = jnp.zeros_like(dw_ref)

            rows = [jnp.sum(dc * pad_ref[pl.ds(HALO - 3 + j, T), :], axis=0, keepdims=True) for j in range(CONV_WIDTH)]
            dw_ref[...] += jnp.concatenate(rows + [jnp.zeros((8 - CONV_WIDTH, 128), F32)], axis=0)

        one(xq_ref, xqh_ref, wq_ref, dqn_ref, dcq_ref, dwq_ref, True)
        one(xk_ref, xkh_ref, wk_ref, dkn_ref, dck_ref, dwk_ref, True)
        one(xv_ref, xvh_ref, wv_ref, dv_ref, dcv_ref, dwv_ref, False)

    tile = pl.BlockSpec((T, DN_WIDTH), lambda i: (i, 0))
    halo = pl.BlockSpec((HALO, DN_WIDTH), lambda i: (jnp.maximum(i * (T // HALO) - 1, 0), 0))
    wspec = lambda sec: pl.BlockSpec((CONV_WIDTH, DN_WIDTH), lambda i, sec=sec: (0, sec))
    dwspec = pl.BlockSpec((8, DN_WIDTH), lambda i: (0, 0))
    return pl.pallas_call(
        body, name="dn_conv_bwd_pre", grid=(S // T,),
        in_specs=[tile, halo, tile, halo, tile, halo, wspec(0), wspec(1), wspec(2), tile, tile, tile],
        out_specs=[tile, tile, tile, dwspec, dwspec, dwspec],
        out_shape=[jax.ShapeDtypeStruct((S, DN_WIDTH), F32)] * 3 + [jax.ShapeDtypeStruct((8, DN_WIDTH), F32)] * 3,
        scratch_shapes=[pltpu.VMEM((T + HALO, 128), F32)],
        compiler_params=_params(1),
    )(xq, xq, xk, xk, xv, xv, conv_w, conv_w, conv_w, dqn, dkn, dv)


def _conv_bwd_x(dcq, dck, dcv, conv_w):
    S = dcq.shape[0]
    T = CONV_T
    nt = S // T

    def body(*refs):
        _per_head(head, refs)

    def head(dq_ref, dqh_ref, dk_ref, dkh_ref, dv_ref, dvh_ref, wq_ref, wk_ref, wv_ref,
             oq_ref, ok_ref, ov_ref, pad_ref):
        i = pl.program_id(0)

        def one(d_ref, dh_ref, w_ref, o_ref):
            pad_ref[pl.ds(0, T), :] = d_ref[...]
            pad_ref[pl.ds(T, HALO), :] = jnp.where(i < nt - 1, dh_ref[...], 0.0)
            w = w_ref[...]
            acc = pad_ref[pl.ds(3, T), :] * w[0:1, :]
            for j in range(1, CONV_WIDTH):
                acc = acc + pad_ref[pl.ds(3 - j, T), :] * w[j:j + 1, :]
            o_ref[...] = acc

        one(dq_ref, dqh_ref, wq_ref, oq_ref)
        one(dk_ref, dkh_ref, wk_ref, ok_ref)
        one(dv_ref, dvh_ref, wv_ref, ov_ref)

    tile = pl.BlockSpec((T, DN_WIDTH), lambda i: (i, 0))
    halo = pl.BlockSpec((HALO, DN_WIDTH), lambda i: (jnp.minimum((i + 1) * (T // HALO), S // HALO - 1), 0))
    wspec = lambda sec: pl.BlockSpec((CONV_WIDTH, DN_WIDTH), lambda i, sec=sec: (0, sec))
    return pl.pallas_call(
        body, name="dn_conv_bwd_x", grid=(nt,),
        in_specs=[tile, halo, tile, halo, tile, halo, wspec(0), wspec(1), wspec(2)],
        out_specs=[tile, tile, tile],
        out_shape=[jax.ShapeDtypeStruct((S, DN_WIDTH), F32)] * 3,
        scratch_shapes=[pltpu.VMEM((T + HALO, 128), F32)],
        compiler_params=_params(1),
    )(dcq, dcq, dck, dck, dcv, dcv, conv_w, conv_w, conv_w)


PREP_CHUNKS = 4
SCAN_CHUNKS = 8


def _bnn(a, b):
    return lax.dot_general(a, b, (((2,), (1,)), ((0,), (0,))), preferred_element_type=F32, precision=HI)


def _bnt(a, b):
    return lax.dot_general(a, b, (((2,), (2,)), ((0,), (0,))), preferred_element_type=F32, precision=HI)


def _btn(a, b):
    return lax.dot_general(a, b, (((1,), (1,)), ((0,), (0,))), preferred_element_type=F32, precision=HI)


def _tri_inverse_b(a, blk, eye):
    dg = jnp.where(blk, a, 0.0)
    lo = a - dg
    d2 = _bnn(dg, dg)
    d4 = _bnn(d2, d2)
    d8 = _bnn(d4, d4)
    td = _bnn(_bnn(_bnn(eye - dg, eye + d2), eye + d4), eye + d8)
    b = _bnn(td, lo)
    b2 = _bnn(b, b)
    return _bnn(_bnn(eye - b, eye + b2), td)


def _dn_common_b(bds, avec, dvec, q_raw, k, v, t=None):
    C = DN_CHUNK
    lane = lax.broadcasted_iota(jnp.int32, (C, 128), 1)
    row = lax.broadcasted_iota(jnp.int32, (1, C, C), 1)
    col = lax.broadcasted_iota(jnp.int32, (1, C, C), 2)
    incl = row >= col
    strict = row > col
    eye = (row == col).astype(F32)
    blk = (row // 16) == (col // 16)
    pick = lambda tile, ln: jnp.sum(jnp.where(lane == ln, tile, 0.0), axis=-1, keepdims=True)
    betas, graws, zcs = [], [], []
    for bd in bds:
        z = bd + dvec
        g_all = -jnp.exp(avec) * (jnp.maximum(z, 0.0) + jnp.log(1.0 + jnp.exp(-jnp.abs(z))))
        beta_all = _sigmoid(bd)
        for h in range(DN_HEADS):
            betas.append(pick(beta_all, h))
            graws.append(pick(g_all, DN_HEADS + h))
            zcs.append(pick(z, DN_HEADS + h))
    beta, graw, zc = jnp.stack(betas), jnp.stack(graws), jnp.stack(zcs)
    to_row = lambda c: jnp.sum(eye * c, axis=1, keepdims=True)
    gc = jnp.sum(jnp.where(incl, to_row(graw), 0.0), axis=-1, keepdims=True)
    decay = jnp.exp(jnp.where(incl, gc - to_row(gc), NEG_BIG))
    q = q_raw * (DN_HEAD_DIM ** -0.5)
    kb = k * beta
    kk = _bnt(kb, k)
    if t is None:
        t = _tri_inverse_b(jnp.where(strict, kk * decay, 0.0), blk, eye)
    eg = jnp.exp(gc)
    rhs_w = kb * eg
    u = _bnn(t, v * beta)
    w = _bnn(t, rhs_w)
    qk = _bnt(q, k)
    aq = jnp.where(incl, qk * decay, 0.0)
    last = lax.broadcasted_iota(jnp.int32, (1, C, 1), 1) == C - 1
    g_last = jnp.sum(jnp.where(last, gc, 0.0), axis=1, keepdims=True)
    ekd = jnp.exp(g_last - gc)
    return dict(beta=beta, graw=graw, zc=zc, gc=gc, decay=decay, q=q, kb=kb, kk=kk, t=t, eg=eg, rhs_w=rhs_w,
                u=u, w=w, qk=qk, aq=aq, g_last=g_last, ekd=ekd, kd=k * ekd, qg=q * eg,
                incl=incl, strict=strict, eye=eye, lane=lane, row=row, col=col, last=last)


def _stack_heads(ref, rows):
    return jnp.stack([ref[rows, h * DN_HEAD_DIM:(h + 1) * DN_HEAD_DIM] for h in range(DN_HEADS)])


def _stack_units(ref, nc):
    C = DN_CHUNK
    return jnp.concatenate([_stack_heads(ref, slice(ci * C, (ci + 1) * C)) for ci in range(nc)], axis=0)


def _store_units(ref, val, nc):
    C = DN_CHUNK
    for ci in range(nc):
        for h in range(DN_HEADS):
            ref[ci * C:(ci + 1) * C, h * DN_HEAD_DIM:(h + 1) * DN_HEAD_DIM] = val[ci * DN_HEADS + h]


def _dn_prep(qn, kn, v, bd, avec, dvec):
    S = qn.shape[0]
    C = DN_CHUNK
    N = S // C
    nc = PREP_CHUNKS

    def body(q_ref, k_ref, v_ref, bd_ref, a_ref, d_ref, u_ref, w_ref, qg_ref, kd_ref, aq_ref, t_ref, egl_ref):
        bds = [bd_ref[ci * C:(ci + 1) * C, :] for ci in range(nc)]
        c = _dn_common_b(bds, a_ref[...], d_ref[...], _stack_units(q_ref, nc), _stack_units(k_ref, nc), _stack_units(v_ref, nc))
        _store_units(u_ref, c["u"], nc)
        _store_units(w_ref, c["w"], nc)
        _store_units(qg_ref, c["qg"], nc)
        _store_units(kd_ref, c["kd"], nc)
        egl = jnp.broadcast_to(jnp.exp(c["g_last"]), (nc * DN_HEADS, 1, 128))
        for ci in range(nc):
            for h in range(DN_HEADS):
                aq_ref[h, ci * C:(ci + 1) * C, :] = c["aq"][ci * DN_HEADS + h]
                t_ref[h, ci * C:(ci + 1) * C, :] = c["t"][ci * DN_HEADS + h]
            egl_ref[ci * 8:(ci + 1) * 8, :] = jnp.concatenate(
                [egl[ci * DN_HEADS + h] for h in range(DN_HEADS)] + [jnp.zeros((8 - DN_HEADS, 128), F32)], axis=0)

    tok = lambda w: pl.BlockSpec((nc * C, w), lambda n: (n, 0))
    sq = pl.BlockSpec((DN_HEADS, nc * C, C), lambda n: (0, n, 0))
    vec = pl.BlockSpec((1, 128), lambda n: (0, 0))
    return pl.pallas_call(
        body, name="dn_prep", grid=(N // nc,),
        in_specs=[tok(DN_WIDTH)] * 3 + [tok(128), vec, vec],
        out_specs=[tok(DN_WIDTH)] * 4 + [sq, sq, pl.BlockSpec((nc * 8, 128), lambda n: (n, 0))],
        out_shape=[jax.ShapeDtypeStruct((S, DN_WIDTH), F32)] * 4 + [jax.ShapeDtypeStruct((DN_HEADS, S, C), F32)] * 2
                  + [jax.ShapeDtypeStruct((N * 8, 128), F32)],
        compiler_params=_params(1),
    )(qn, kn, v, bd, avec, dvec)


def _dn_scan_fwd(u, w, qg, kd, aq, egl, gate, dn_gain):
    S = u.shape[0]
    C = DN_CHUNK
    N = S // C
    HD = DN_HEAD_DIM
    nc = SCAN_CHUNKS

    def body(u_ref, w_ref, qg_ref, kd_ref, aq_ref, egl_ref, gate_ref, gain_ref, dn_ref, o_ref, vn_ref, st_ref, state_ref):
        @pl.when(pl.program_id(0) == 0)
        def _():
            state_ref[...] = jnp.zeros_like(state_ref)

        gain = gain_ref[...]
        for ci in range(nc):
            rows = slice(ci * C, (ci + 1) * C)
            st = state_ref[...]
            for h in range(DN_HEADS):
                st_ref[ci * DN_WIDTH + h * HD:ci * DN_WIDTH + (h + 1) * HD, :] = st[h]
            v_new = _stack_heads(u_ref, rows) - _bnn(_stack_heads(w_ref, rows), st)
            o = _bnn(_stack_heads(qg_ref, rows), st) + _bnn(aq_ref[:, rows, :], v_new)
            egl = jnp.stack([egl_ref[ci * 8 + h:ci * 8 + h + 1, :] for h in range(DN_HEADS)])
            state_ref[...] = st * egl + _btn(_stack_heads(kd_ref, rows), v_new)
            r = lax.rsqrt(jnp.mean(o * o, axis=-1, keepdims=True) + NORM_EPS)
            gt = _stack_heads(gate_ref, rows)
            dn = o * r * gain * (gt * _sigmoid(gt))
            for h in range(DN_HEADS):
                sl = slice(h * HD, (h + 1) * HD)
                vn_ref[rows, sl] = v_new[h]
                o_ref[rows, sl] = o[h]
                dn_ref[rows, sl] = dn[h]

    tok = lambda wd: pl.BlockSpec((nc * C, wd), lambda n: (n, 0))
    sq = pl.BlockSpec((DN_HEADS, nc * C, C), lambda n: (0, n, 0))
    vec = pl.BlockSpec((1, 128), lambda n: (0, 0))
    return pl.pallas_call(
        body, name="dn_scan_fwd", grid=(N // nc,),
        in_specs=[tok(DN_WIDTH)] * 4 + [sq, pl.BlockSpec((nc * 8, 128), lambda n: (n, 0)), tok(DN_WIDTH), vec],
        out_specs=[tok(DN_WIDTH)] * 3 + [pl.BlockSpec((nc * DN_WIDTH, HD), lambda n: (n, 0))],
        out_shape=[jax.ShapeDtypeStruct((S, DN_WIDTH), F32)] * 3 + [jax.ShapeDtypeStruct((N * DN_WIDTH, HD), F32)],
        scratch_shapes=[pltpu.VMEM((DN_HEADS, HD, HD), F32)],
        compiler_params=_params(1),
    )(u, w, qg, kd, aq, egl, gate, dn_gain)


def _dn_scan_bwd(w, qg, kd, aq, egl, gate, dn_gain, o, ddn):
    S = w.shape[0]
    C = DN_CHUNK
    N = S // C
    HD = DN_HEAD_DIM
    nc = SCAN_CHUNKS

    def body(w_ref, qg_ref, kd_ref, aq_ref, egl_ref, gate_ref, gain_ref, o_ref, ddn_ref,
             do_ref, dvn_ref, dgate_ref, dst_ref, small_ref, dstate_ref):
        @pl.when(pl.program_id(0) == 0)
        def _():
            dstate_ref[...] = jnp.zeros_like(dstate_ref)
            small_ref[...] = jnp.zeros_like(small_ref)

        gain = gain_ref[...]
        d_gain = jnp.zeros((1, 128), F32)
        for ci in reversed(range(nc)):
            rows = slice(ci * C, (ci + 1) * C)
            dsn = dstate_ref[...]
            for h in range(DN_HEADS):
                dst_ref[ci * DN_WIDTH + h * HD:ci * DN_WIDTH + (h + 1) * HD, :] = dsn[h]
            ov = _stack_heads(o_ref, rows)
            r = lax.rsqrt(jnp.mean(ov * ov, axis=-1, keepdims=True) + NORM_EPS)
            on = ov * r
            gt = _stack_heads(gate_ref, rows)
            sgt = _sigmoid(gt)
            silu_g = gt * sgt
            dy = _stack_heads(ddn_ref, rows)
            d_gain = d_gain + jnp.sum(jnp.sum(dy * on * silu_g, axis=1, keepdims=True), axis=0)
            dgate = dy * on * gain * (sgt * (1.0 + gt * (1.0 - sgt)))
            don = dy * gain * silu_g
            do = r * (don - on * jnp.mean(don * on, axis=-1, keepdims=True))
            d_vnew = _btn(aq_ref[:, rows, :], do) + _bnn(_stack_heads(kd_ref, rows), dsn)
            egl = jnp.stack([egl_ref[ci * 8 + h:ci * 8 + h + 1, :] for h in range(DN_HEADS)])
            dstate_ref[...] = _btn(_stack_heads(qg_ref, rows), do) + dsn * egl - _btn(_stack_heads(w_ref, rows), d_vnew)
            for h in range(DN_HEADS):
                sl = slice(h * HD, (h + 1) * HD)
                do_ref[rows, sl] = do[h]
                dvn_ref[rows, sl] = d_vnew[h]
                dgate_ref[rows, sl] = dgate[h]
        small_ref[...] += jnp.concatenate([d_gain, jnp.zeros((7, 128), F32)], axis=0)

    nb = N // nc
    tok = lambda wd: pl.BlockSpec((nc * C, wd), lambda i: (nb - 1 - i, 0))
    sq = pl.BlockSpec((DN_HEADS, nc * C, C), lambda i: (0, nb - 1 - i, 0))
    vec = pl.BlockSpec((1, 128), lambda i: (0, 0))
    return pl.pallas_call(
        body, name="dn_scan_bwd", grid=(nb,),
        in_specs=[tok(DN_WIDTH)] * 3 + [sq, pl.BlockSpec((nc * 8, 128), lambda i: (nb - 1 - i, 0)), tok(DN_WIDTH), vec,
                                       tok(DN_WIDTH), tok(DN_WIDTH)],
        out_specs=[tok(DN_WIDTH)] * 3 + [pl.BlockSpec((nc * DN_WIDTH, HD), lambda i: (nb - 1 - i, 0)),
                                        pl.BlockSpec((8, 128), lambda i: (0, 0))],
        out_shape=[jax.ShapeDtypeStruct((S, DN_WIDTH), F32)] * 3 + [jax.ShapeDtypeStruct((N * DN_WIDTH, HD), F32),
                                                                  jax.ShapeDtypeStruct((8, 128), F32)],
        scratch_shapes=[pltpu.VMEM((DN_HEADS, HD, HD), F32)],
        compiler_params=_params(1),
    )(w, qg, kd, aq, egl, gate, dn_gain, o, ddn)


def _dn_post(qn, kn, v, bd, avec, dvec, t_inv, v_new_all, states, dstates, do_all, dvn_all, comm=None):
    S = qn.shape[0]
    C = DN_CHUNK
    N = S // C
    HD = DN_HEAD_DIM
    nc = PREP_CHUNKS
    B = nc * DN_HEADS

    def body(q_ref, k_ref, v_ref, bd_ref, a_ref, d_ref, t_ref, vn_ref, st_ref, dst_ref, do_ref, dvn_ref,
             dq_ref, dk_ref, dv_ref, dbd_ref, small_ref):
        @pl.when(pl.program_id(0) == 0)
        def _():
            small_ref[...] = jnp.zeros_like(small_ref)

        avec = a_ref[...]
        bds = [bd_ref[ci * C:(ci + 1) * C, :] for ci in range(nc)]
        k = _stack_units(k_ref, nc)
        vv = _stack_units(v_ref, nc)
        t = jnp.concatenate([t_ref[:, ci * C:(ci + 1) * C, :] for ci in range(nc)], axis=0)
        c = _dn_common_b(bds, avec, d_ref[...], _stack_units(q_ref, nc), k, vv, t=t)
        q, kb, eg, u, w = c["q"], c["kb"], c["eg"], c["u"], c["w"]
        beta, decay, incl, strict, eye = c["beta"], c["decay"], c["incl"], c["strict"], c["eye"]
        st = jnp.stack([st_ref[b * HD:(b + 1) * HD, :] for b in range(B)])
        dsn = jnp.stack([dst_ref[b * HD:(b + 1) * HD, :] for b in range(B)])
        v_new = _stack_units(vn_ref, nc)
        do = _stack_units(do_ref, nc)
        d_vnew = _stack_units(dvn_ref, nc)
        egl = jnp.exp(c["g_last"])
        daq = jnp.where(incl, _bnt(do, v_new), 0.0)
        d_qg = _bnt(do, st)
        d_kd = _bnt(v_new, dsn)
        d_glast = jnp.sum(jnp.sum(dsn * st, axis=-1, keepdims=True), axis=1, keepdims=True) * egl
        d_w = -_bnt(d_vnew, st)
        d_ru = _btn(t, d_vnew)
        d_rw = _btn(t, d_w)
        da = -jnp.where(strict, _bnt(d_ru, u) + _bnt(d_rw, w), 0.0)
        dv = d_ru * beta
        dbeta = jnp.sum(d_ru * vv, axis=-1, keepdims=True)
        dkb = d_rw * eg
        dgc = jnp.sum(d_rw * c["rhs_w"], axis=-1, keepdims=True)
        dkk = da * decay
        ddecay = da * c["kk"]
        dkb = dkb + _bnn(dkk, k)
        dk = _btn(dkk, kb)
        dqk = daq * decay
        ddecay = ddecay + daq * c["qk"]
        dq = _bnn(dqk, k)
        dk = dk + _btn(dqk, q)
        m = ddecay * decay
        col_sum = jnp.sum(m, axis=1, keepdims=True)
        dgc = dgc + jnp.sum(m, axis=-1, keepdims=True) - jnp.sum(eye * col_sum, axis=-1, keepdims=True)
        dq = dq + d_qg * eg
        dgc = dgc + jnp.sum(d_qg * c["qg"], axis=-1, keepdims=True)
        dk = dk + d_kd * c["ekd"]
        tk = jnp.sum(d_kd * c["kd"], axis=-1, keepdims=True)
        dgc = dgc - tk
        d_glast = d_glast + jnp.sum(tk, axis=1, keepdims=True)
        dk = dk + dkb * beta
        dbeta = dbeta + jnp.sum(dkb * k, axis=-1, keepdims=True)
        dgc = dgc + jnp.where(c["last"], d_glast, 0.0)
        dgc_row = jnp.sum(eye * dgc, axis=1, keepdims=True)
        dgraw = jnp.sum(jnp.where(c["col"] >= c["row"], dgc_row, 0.0), axis=-1, keepdims=True)
        _store_units(dq_ref, dq * (HD ** -0.5), nc)
        _store_units(dk_ref, dk, nc)
        _store_units(dv_ref, dv, nc)
        dbraw = dbeta * beta * (1.0 - beta)
        dzc = dgraw * _sigmoid(c["zc"])
        ga = dgraw * c["graw"]
        lane = c["lane"]
        lane1 = lax.broadcasted_iota(jnp.int32, (1, 128), 1)
        neg_ea = -jnp.exp(avec)
        d_alog = jnp.zeros((1, 128), F32)
        d_dt = jnp.zeros((1, 128), F32)
        for ci in range(nc):
            dbd = jnp.zeros((C, 128), F32)
            for h in range(DN_HEADS):
                b = ci * DN_HEADS + h
                dz = dzc[b] * neg_ea
                dbd = dbd + jnp.where(lane == h, dbraw[b], 0.0) + jnp.where(lane == DN_HEADS + h, dz, 0.0)
                d_alog = d_alog + jnp.where(lane1 == DN_HEADS + h, jnp.sum(ga[b], axis=0, keepdims=True), 0.0)
                d_dt = d_dt + jnp.where(lane1 == DN_HEADS + h, jnp.sum(dz, axis=0, keepdims=True), 0.0)
            dbd_ref[ci * C:(ci + 1) * C, :] = dbd
        small_ref[...] += jnp.concatenate([d_alog, d_dt, jnp.zeros((6, 128), F32)], axis=0)

    tok = lambda wd: pl.BlockSpec((nc * C, wd), lambda n: (n, 0))
    big = pl.BlockSpec((nc * DN_WIDTH, HD), lambda n: (n, 0))
    sq = pl.BlockSpec((DN_HEADS, nc * C, C), lambda n: (0, n, 0))
    vec = pl.BlockSpec((1, 128), lambda n: (0, 0))
    return _call(
        body, (qn, kn, v, bd, avec, dvec, t_inv, v_new_all, states, dstates, do_all, dvn_all),
        name="dn_post", grid=(N // nc,), comm=comm,
        in_specs=[tok(DN_WIDTH)] * 3 + [tok(128), vec, vec, sq, tok(DN_WIDTH), big, big, tok(DN_WIDTH), tok(DN_WIDTH)],
        out_specs=[tok(DN_WIDTH)] * 3 + [tok(128), pl.BlockSpec((8, 128), lambda n: (0, 0))],
        out_shape=[jax.ShapeDtypeStruct((S, DN_WIDTH), F32)] * 3 + [jax.ShapeDtypeStruct((S, 128), F32),
                                                                  jax.ShapeDtypeStruct((8, 128), F32)])


def _outproj_fwd(x, attn, dn, w_out):
    S, D = x.shape
    tm = 512

    def body(x_ref, a_ref, d_ref, w_ref, xo_ref, mix_ref):
        a = a_ref[...].astype(BF16)
        dd = d_ref[...].astype(BF16)
        mix_ref[:, 0:ATTN_WIDTH] = a
        mix_ref[:, ATTN_WIDTH:] = dd
        xo_ref[...] = x_ref[...] + _nn(a, w_ref[0:ATTN_WIDTH, :]) + _nn(dd, w_ref[ATTN_WIDTH:, :])

    tok = lambda w: pl.BlockSpec((tm, w), lambda i: (i, 0))
    return pl.pallas_call(
        body, name="outproj_fwd", grid=(S // tm,),
        in_specs=[tok(D), tok(ATTN_WIDTH), tok(DN_WIDTH), pl.BlockSpec((D, D), lambda i: (0, 0))],
        out_specs=[tok(D), tok(D)],
        out_shape=[jax.ShapeDtypeStruct((S, D), F32), jax.ShapeDtypeStruct((S, D), BF16)],
        compiler_params=_params(1),
    )(x, attn, dn, w_out)


def _outproj_bwd(dx, w_out, attn):
    S, D = dx.shape
    tm = VIEW_TILE

    def body(dx_ref, w_ref, attn_ref, da1, da4, da16, dl1, dl4, dl16, ddn_ref, dxb_ref, planes):
        d = dx_ref[...].astype(BF16)
        dxb_ref[...] = d
        da = _nt(d, w_ref[0:ATTN_WIDTH, :])
        ddn_ref[...] = _nt(d, w_ref[ATTN_WIDTH:, :])
        _tile_to_views(da, planes, (da1, da4, da16))
        lo = lax.broadcasted_iota(jnp.int32, (tm, 128), 1) < 64
        cols = []
        for G in range(4):
            sl = slice(G * 128, (G + 1) * 128)
            t = da[:, sl] * attn_ref[:, sl]
            d0 = jnp.sum(jnp.where(lo, t, 0.0), axis=-1, keepdims=True)
            d1 = jnp.sum(jnp.where(lo, 0.0, t), axis=-1, keepdims=True)
            cols.append(jnp.where(lo, d0, d1))
        _tile_to_views(jnp.concatenate(cols, axis=1), planes, (dl1, dl4, dl16))

    tok = lambda w: pl.BlockSpec((tm, w), lambda i: (i, 0))
    views = [_view_spec(d) for d in DILATIONS]
    return pl.pallas_call(
        body, name="outproj_bwd", grid=(S // tm,),
        in_specs=[tok(D), pl.BlockSpec((D, D), lambda i: (0, 0)), tok(ATTN_WIDTH)],
        out_specs=views + views + [tok(DN_WIDTH), tok(D)],
        out_shape=[_view_shape(S, d, F32) for d in DILATIONS] * 2
                  + [jax.ShapeDtypeStruct((S, DN_WIDTH), F32), jax.ShapeDtypeStruct((S, D), BF16)],
        scratch_shapes=[pltpu.VMEM((4, tm, 128), F32)],
        compiler_params=_params(1),
    )(dx, w_out, attn)


def _loss_head(x, gain, target):
    S, D = x.shape
    tm = 512

    def body(x_ref, gain_ref, t_ref, loss_ref, dx_ref, dgain_ref):
        @pl.when(pl.program_id(0) == 0)
        def _():
            loss_ref[...] = jnp.zeros_like(loss_ref)
            dgain_ref[...] = jnp.zeros_like(dgain_ref)

        xf = x_ref[...]
        gain = gain_ref[...]
        r = lax.rsqrt(jnp.mean(xf * xf, axis=-1, keepdims=True) + NORM_EPS)
        xhat = xf * r
        err = xhat * gain - t_ref[...]
        part = 0.5 * jnp.sum(jnp.mean(err * err, axis=-1, keepdims=True), axis=0, keepdims=True)
        first = (lax.broadcasted_iota(jnp.int32, (8, 128), 0) == 0) & (lax.broadcasted_iota(jnp.int32, (8, 128), 1) == 0)
        loss_ref[...] += jnp.where(first, part, 0.0)
        dy = err * (1.0 / D)
        dgain_ref[...] += jnp.sum(dy * xhat, axis=0, keepdims=True)
        dxh = dy * gain
        dx_ref[...] = r * (dxh - xhat * jnp.mean(dxh * xhat, axis=-1, keepdims=True))

    tok = pl.BlockSpec((tm, D), lambda i: (i, 0))
    row = pl.BlockSpec((1, D), lambda i: (0, 0))
    return pl.pallas_call(
        body, name="loss_head", grid=(S // tm,),
        in_specs=[tok, row, tok],
        out_specs=[pl.BlockSpec((8, 128), lambda i: (0, 0)), tok, row],
        out_shape=[jax.ShapeDtypeStruct((8, 128), F32), jax.ShapeDtypeStruct((S, D), F32),
                   jax.ShapeDtypeStruct((1, D), F32)],
        compiler_params=_params(1),
    )(x, gain, target)


def _adamw(w, g, m, v, name):
    R, Ccols = w.shape
    tr = next((t for t in (256, 128, 64, 32, 16, 8) if R % t == 0), None)
    c1 = 1.0 - ADAM_B1 ** ADAM_STEP
    c2 = 1.0 - ADAM_B2 ** ADAM_STEP

    def body(w_ref, g_ref, m_ref, v_ref, d_ref, nm_ref, nv_ref):
        gv = g_ref[...]
        mn = ADAM_B1 * m_ref[...] + (1.0 - ADAM_B1) * gv
        vn = ADAM_B2 * v_ref[...] + (1.0 - ADAM_B2) * (gv * gv)
        nm_ref[...] = mn
        nv_ref[...] = vn
        d_ref[...] = -ADAM_LR * ((mn / c1) / (jnp.sqrt(vn / c2) + ADAM_EPS) + ADAM_WD * w_ref[...])

    if tr is not None or Ccols % 128:
        tr = tr or R
        grid, spec = (R // tr,), pl.BlockSpec((tr, Ccols), lambda i: (i, 0))
    else:
        grid, spec = (Ccols // 128,), pl.BlockSpec((R, 128), lambda i: (0, i))
    return pl.pallas_call(
        body, name=name, grid=grid, in_specs=[spec] * 4, out_specs=[spec] * 3,
        out_shape=[jax.ShapeDtypeStruct((R, Ccols), F32)] * 3, compiler_params=_params(1),
    )(w, g, m, v)


LATE_WEIGHTS = ("w_out", "ffn2_gate", "ffn2_up", "ffn2_down")


def _local_step(x, target, wts, small, dist=None):
    g1, g2, gm, gf = small["norm_ffn1"], small["norm_ffn2"], small["norm_mix"], small["norm_final"]
    wts = dict(wts)

    def reduce_start(gs, tag):
        return _rs_add_pairs(gs, _swap_sibling(gs, True, "rs_swap_halves_" + tag), dist["c"], "rs_add_pairs_" + tag)

    (x1, h1, fg1, fu1), late = _ffn_fwd(x, g1, wts["ffn1_gate"], wts["ffn1_up"], wts["ffn1_down"], "ffn1_fwd",
                                        comm=_ag_comm(dist["late"]) if dist else None)
    if dist:
        wts.update(zip(LATE_WEIGHTS, late))
        wts["w_out"] = wts["w_out"].reshape(D_MODEL, D_MODEL)
    h2, *qkv, xq, xk, xv, gate, bd = _inproj_fwd(x1, gm, wts["w_in"])
    aq, ak, av = qkv[0:3], qkv[3:6], qkv[6:9]
    parts = [_attn_fwd(aq[p], ak[p], av[p], d, f"attn_fwd_d{d}") for p, d in enumerate(DILATIONS)]
    attn, *lse = _attn_merge(parts)
    conv_w = small["conv_w"]
    qn, kn, vv = _conv_fwd(xq, xk, xv, conv_w)
    dn_u, dn_w, dn_qg, dn_kd, dn_aq, dn_t, dn_egl = _dn_prep(qn, kn, vv, bd, small["avec"], small["dvec"])
    dn, o_dn, v_new, states = _dn_scan_fwd(dn_u, dn_w, dn_qg, dn_kd, dn_aq, dn_egl, gate, small["dn_norm"])
    x2, mix = _outproj_fwd(x1, attn, dn, wts["w_out"])
    (x3, h3, fg2, fu2), _ = _ffn_fwd(x2, g2, wts["ffn2_gate"], wts["ffn2_up"], wts["ffn2_down"], "ffn2_fwd")
    loss, dx3, d_gf = _loss_head(x3, gf, target)

    grads = {}
    (dx2, d_g2, dfg2, dfu2, act2, dout2), _ = _ffn_bwd(dx3, x2, g2, fg2, fu2, wts["ffn2_down"], wts["ffn2_gate"],
                                                      wts["ffn2_up"], "ffn2_bwd")
    tk = 2048
    grads["ffn2_gate"], _ = _dw_chunks(dfg2, h3, tk, "dw_ffn2_gate")
    grads["ffn2_up"], _ = _dw_chunks(dfu2, h3, tk, "dw_ffn2_up")
    grads["ffn2_down"], _ = _dw_chunks(act2, dout2, tk, "dw_ffn2_down")
    group_a = ("ffn2_gate", "ffn2_up", "ffn2_down")
    parts_a = reduce_start([grads[n] for n in group_a], "a") if dist else None

    *dviews, ddn, dx2b = _outproj_bwd(dx2, wts["w_out"], attn)
    dattn, dd = dviews[0:3], dviews[3:6]
    grads["w_out"] = _matmul_tn(mix, dx2b, D_MODEL, tk, "dw_out").reshape(N_CHIPS, D_MODEL // N_CHIPS, D_MODEL)

    daq, dak, dav = [], [], []
    for p, d in enumerate(DILATIONS):
        daq.append(_attn_bwd_q(aq[p], ak[p], av[p], dattn[p], lse[p], dd[p], d, f"attn_bwd_q_d{d}"))
        dk_p, dv_p = _attn_bwd_kv(aq[p], ak[p], av[p], dattn[p], lse[p], dd[p], d, f"attn_bwd_kv_d{d}")
        dak.append(dk_p)
        dav.append(dv_p)

    do_dn, dvn, dgate, dstates, d_dn_gain = _dn_scan_bwd(dn_w, dn_qg, dn_kd, dn_aq, dn_egl, gate, small["dn_norm"], o_dn, ddn)
    (dqn, dkn, dvv, dbd, dn_small), recv_a = _dn_post(qn, kn, vv, bd, small["avec"], small["dvec"], dn_t, v_new, states,
                                                      dstates, do_dn, dvn, comm=_rsx_comm(parts_a) if dist else None)
    dcq, dck, dcv, dwq, dwk, dwv = _conv_bwd_pre(xq, xk, xv, conv_w, dqn, dkn, dvv)
    dxq, dxk, dxv = _conv_bwd_x(dcq, dck, dcv, conv_w)
    d_conv = jnp.concatenate([dwq[:CONV_WIDTH], dwk[:CONV_WIDTH], dwv[:CONV_WIDTH]], axis=1)

    dx1, d_gm, dproj = _inproj_bwd(dx2, x1, gm, [daq, dak, dav], [dxq, dxk, dxv, dgate], dbd, wts["w_in"])
    gi = _matmul_tn(dproj, h2, IN_COLS_PADDED, 512, "dw_in")
    if dist:
        gi = jnp.concatenate([gi[:3072], gi[3584:3592], gi[3072:3584]], axis=0).reshape(N_CHIPS, IN_COLS // N_CHIPS, D_MODEL)
        gi = jnp.pad(gi, ((0, 0), (0, W_IN_ROWS - IN_COLS // N_CHIPS), (0, 0)))
    grads["w_in"] = gi
    group_b = ("w_in", "w_out")
    parts_b = reduce_start([grads[n] for n in group_b], "b") if dist else None

    (dx0, d_g1, dfg1, dfu1, act1, dout1), _ = _ffn_bwd(dx1, x, g1, fg1, fu1, wts["ffn1_down"], wts["ffn1_gate"],
                                                      wts["ffn1_up"], "ffn1_bwd")
    group_c = ("ffn1_gate", "ffn1_up", "ffn1_down")
    pending = parts_b if dist else []
    parts_c, recv_bc = [], []
    for n, (lhs, rhs) in zip(group_c, ((dfg1, h1), (dfu1, h1), (act1, dout1))):
        grads[n], landed = _dw_chunks(lhs, rhs, tk, "dw_" + n, comm=_rsx_comm(pending) if dist else None)
        recv_bc += list(landed)
        if dist:
            pending = reduce_start([grads[n]], n)
            parts_c += pending

    small_grads = dict(norm_ffn1=d_g1, norm_mix=d_gm, norm_ffn2=d_g2, norm_final=d_gf, conv_w=d_conv,
                       a_log=dn_small[0:1], dt_bias=dn_small[1:2], dn_norm=d_dn_gain[0:1])
    if dist:
        recv_bc += _rs_exchange_arrays(pending)
        recv_b, recv_c = recv_bc[:len(parts_b)], recv_bc[len(parts_b):]
        names = group_a + group_b + group_c
        totals = _rs_add_totals(list(parts_a) + list(parts_b) + list(parts_c), list(recv_a) + list(recv_b) + list(recv_c),
                                dist["chip"])
        theirs = _swap_sibling(totals, False, "rs_share_total")
        grads = {n: (mine, other) for n, mine, other in zip(names, totals, theirs)}
    return loss, dx0, grads, small_grads


HBM =pl.BlockSpec(memory_space=pl.ANY)
VMEM_SPEC = pl.BlockSpec(memory_space=pltpu.VMEM)


def _coords():
    return lax.axis_index("x"), lax.axis_index("y"), lax.axis_index("c")


def _remote(src, dst, send_sems, recv_sems, k, dev):
    return pltpu.make_async_remote_copy(src_ref=src, dst_ref=dst, send_sem=send_sems.at[k], recv_sem=recv_sems.at[k],
                                        device_id=dev, device_id_type=MESH)


def _allreduce_small(buf, name):
    R, Cc = buf.shape

    def body(src_ref, out_ref, recv_ref, send_sems, recv_sems):
        x, y, c = _coords()
        copies = []
        for m in range(1, 8):
            fx, fy, fc = (m >> 2) & 1, (m >> 1) & 1, m & 1
            dev = (x ^ fx if fx else x, y ^ fy if fy else y, c ^ fc if fc else c)
            cp = _remote(src_ref, recv_ref.at[m - 1], send_sems, recv_sems, m - 1, dev)
            cp.start()
            copies.append(cp)
        for cp in copies:
            cp.wait()
        r = [src_ref[...]] + [recv_ref[m] for m in range(7)]
        out_ref[...] = ((r[0] + r[1]) + (r[2] + r[3])) + ((r[4] + r[5]) + (r[6] + r[7]))

    return pl.pallas_call(
        body, name=name, out_shape=jax.ShapeDtypeStruct((R, Cc), F32),
        in_specs=[VMEM_SPEC], out_specs=VMEM_SPEC,
        scratch_shapes=[pltpu.VMEM((7, R, Cc), F32), pltpu.SemaphoreType.DMA((7,)), pltpu.SemaphoreType.DMA((7,))],
    )(buf)


BIG = ("ffn1_gate", "ffn1_up", "ffn1_down", "w_in", "w_out", "ffn2_gate", "ffn2_up", "ffn2_down")
ROW_SHARDED = ("ffn1_down", "w_out", "ffn2_down")
W_IN_ROWS = 960


def _rows(ref, start, size):
    return ref.at[pl.ds(pl.multiple_of(start, 16), size)]


def _allgather_arrays(shards):
    n = len(shards)

    def body(*refs):
        srcs, outs, send_sems, recv_sems = refs[:n], refs[n:2 * n], refs[2 * n], refs[2 * n + 1]
        x, y, c = _coords()
        sib = (x, y, 1 - c)
        xn, yn, dg = (1 - x, y), (x, 1 - y), (1 - x, 1 - y)
        slot = lambda out, chip: out.at[2 * chip[0] + chip[1]]
        started = []

        def go(cp):
            cp.start()
            started.append(cp)

        for a, (src, out) in enumerate(zip(srcs, outs)):
            h = src.shape[0] // 2
            cp = lambda s, d, k, dev: _remote(s, d, send_sems, recv_sems, 8 * a + k, dev)
            go(cp(src, slot(out, (x, y)), 6, sib))
            mine, dst = _rows(src, c * h, h), _rows(slot(out, (x, y)), c * h, h)
            go(cp(mine, dst, 0, (*xn, c)))
            go(cp(mine, dst, 1, (*yn, c)))
        for a, (src, out) in enumerate(zip(srcs, outs)):
            h = src.shape[0] // 2
            q = h // 2
            cp = lambda s, d, k, dev: _remote(s, d, send_sems, recv_sems, 8 * a + k, dev)
            from_x, from_y = _rows(slot(out, xn), c * h, h), _rows(slot(out, yn), c * h, h)
            cp(from_x, from_x, 0, sib).wait_recv()
            first = _rows(slot(out, xn), c * h, q)
            go(cp(first, first, 2, (*yn, c)))
            go(cp(from_x, from_x, 3, sib))
            cp(from_y, from_y, 1, sib).wait_recv()
            second = _rows(slot(out, yn), c * h + q, q)
            go(cp(second, second, 7, (*xn, c)))
            go(cp(from_y, from_y, 4, sib))
        for a, (src, out) in enumerate(zip(srcs, outs)):
            h = src.shape[0] // 2
            q = h // 2
            cp = lambda s, d, k, dev: _remote(s, d, send_sems, recv_sems, 8 * a + k, dev)
            first, second = _rows(slot(out, dg), c * h, q), _rows(slot(out, dg), c * h + q, q)
            cp(first, first, 2, sib).wait_recv()
            cp(second, second, 7, sib).wait_recv()
            from_d = _rows(slot(out, dg), c * h, h)
            go(cp(from_d, from_d, 5, sib))
        for a, (src, out) in enumerate(zip(srcs, outs)):
            h = src.shape[0] // 2
            cp = lambda s, d, k, dev: _remote(s, d, send_sems, recv_sems, 8 * a + k, dev)
            for k, chip in ((3, xn), (4, yn), (5, dg)):
                theirs = _rows(slot(out, chip), (1 - c) * h, h)
                cp(theirs, theirs, k, sib).wait_recv()
            cp(src, slot(out, (x, y)), 6, sib).wait_recv()
        for cp in started:
            cp.wait_send()

    shapes = [jax.ShapeDtypeStruct((N_CHIPS,) + s.shape, s.dtype) for s in shards]
    return pl.pallas_call(
        body, name="allgather_weights", out_shape=shapes, in_specs=[HBM] * n, out_specs=[HBM] * n,
        scratch_shapes=[pltpu.SemaphoreType.DMA((8 * n,)), pltpu.SemaphoreType.DMA((8 * n,))],
    )(*shards)


def _ag_copies(srcs, outs, send_sems, recv_sems):
    x, y, c = _coords()
    sib = (x, y, 1 - c)
    me = 2 * x + y
    others = [(1 - x, y), (x, 1 - y), (1 - x, 1 - y)]
    plan = []
    for a, (src, out) in enumerate(zip(srcs, outs)):
        h = src.shape[0] // 2
        cp = lambda s, d, k, dev: _remote(s, d, send_sems, recv_sems, 7 * a + k, dev)
        own = cp(src, out.at[me], 6, sib)
        sends = [cp(_rows(src, c * h, h), _rows(out.at[me], c * h, h), j, (ox, oy, c)) for j, (ox, oy) in enumerate(others)]
        mine = [_rows(out.at[2 * ox + oy], c * h, h) for ox, oy in others]
        theirs = [_rows(out.at[2 * ox + oy], (1 - c) * h, h) for ox, oy in others]
        arrivals = [cp(m, m, j, sib) for j, m in enumerate(mine)]
        forwards = [cp(m, m, 3 + j, sib) for j, m in enumerate(mine)]
        forwarded = [cp(t, t, 3 + j, sib) for j, t in enumerate(theirs)]
        plan.append((own, sends, forwards, arrivals, forwarded))
    return plan


def _ag_start(srcs, outs, send_sems, recv_sems):
    for own, sends, _, _, _ in _ag_copies(srcs, outs, send_sems, recv_sems):
        own.start()
        for cp in sends:
            cp.start()


def _ag_finish(srcs, outs, send_sems, recv_sems):
    plan = _ag_copies(srcs, outs, send_sems, recv_sems)
    for _, _, forwards, arrivals, _ in plan:
        for arrived, fwd in zip(arrivals, forwards):
            arrived.wait_recv()
            fwd.start()
    for own, sends, forwards, _, forwarded in plan:
        for cp in forwarded:
            cp.wait_recv()
        own.wait_recv()
        for cp in [own] + sends + forwards:
            cp.wait_send()


def _ag_comm(shards):
    shapes = [jax.ShapeDtypeStruct((N_CHIPS,) + s.shape, s.dtype) for s in shards]
    return (list(shards), shapes, 7 * len(shards), _ag_start, _ag_finish)


def _swap_sibling(arrs, pick_other_half, name):
    n = len(arrs)
    outs = [jax.ShapeDtypeStruct((a.shape[0], a.shape[1] // 2) + a.shape[2:] if pick_other_half else a.shape, a.dtype) for a in arrs]

    def body(*refs):
        srcs, dsts, send_sems, recv_sems = refs[:n], refs[n:2 * n], refs[2 * n], refs[2 * n + 1]
        x, y, c = _coords()
        cps = []
        for a in range(n):
            src = srcs[a]
            if pick_other_half:
                h = src.shape[1] // 2
                src = src.at[:, pl.ds(pl.multiple_of((1 - c) * h, 16), h)]
            cp = _remote(src, dsts[a], send_sems, recv_sems, a, (x, y, 1 - c))
            cp.start()
            cps.append(cp)
        for cp in cps:
            cp.wait()

    return pl.pallas_call(
        body, name=name, out_shape=outs, in_specs=[HBM] * n, out_specs=[HBM] * n,
        scratch_shapes=[pltpu.SemaphoreType.DMA((n,)), pltpu.SemaphoreType.DMA((n,))],
    )(*arrs)


def _rs_add_pairs(gs, others, c, name):
    n = len(gs)
    blocks = [(g.shape[1] // 4, g.shape[2]) for g in gs]

    def body(c_ref, *refs):
        for a in range(n):
            refs[2 * n + a][...] = (refs[a][...] + refs[n + a][...]).astype(BF16)

    mine = lambda b: pl.BlockSpec((None,) + b, lambda j, s, c_ref: (j, c_ref[0] * 2 + s, 0))
    flat = lambda b: pl.BlockSpec((None,) + b, lambda j, s, c_ref: (j, s, 0))
    return pl.pallas_call(
        body, name=name,
        grid_spec=pltpu.PrefetchScalarGridSpec(
            num_scalar_prefetch=1, grid=(N_CHIPS, 2),
            in_specs=[mine(b) for b in blocks] + [flat(b) for b in blocks],
            out_specs=[flat(b) for b in blocks]),
        out_shape=[jax.ShapeDtypeStruct(o.shape, BF16) for o in others],
        compiler_params=_params(2),
    )(c, *gs, *others)


def _rs_exchange_arrays(parts):
    n = len(parts)

    def body(*refs):
        _rsx_start(refs[:n], refs[n:2 * n], refs[2 * n], refs[2 * n + 1])
        _rsx_finish(refs[:n], refs[n:2 * n], refs[2 * n], refs[2 * n + 1])

    _, shapes, n_sems, _, _ = _rsx_comm(parts)
    return pl.pallas_call(
        body, name="rs_exchange_chips", out_shape=shapes, in_specs=[HBM] * n, out_specs=[HBM] * n,
        scratch_shapes=[pltpu.SemaphoreType.DMA((n_sems,)), pltpu.SemaphoreType.DMA((n_sems,))],
    )(*parts)


def _rsx_copies(srcs, dsts, send_sems, recv_sems):
    x, y, c = _coords()
    others = [(1 - x, y), (x, 1 - y), (1 - x, 1 - y)]
    return [_remote(src.at[2 * ox + oy], dst.at[k], send_sems, recv_sems, 3 * a + k, (ox, oy, c))
            for a, (src, dst) in enumerate(zip(srcs, dsts)) for k, (ox, oy) in enumerate(others)]


def _rsx_start(srcs, dsts, send_sems, recv_sems):
    for cp in _rsx_copies(srcs, dsts, send_sems, recv_sems):
        cp.start()


def _rsx_finish(srcs, dsts, send_sems, recv_sems):
    for cp in _rsx_copies(srcs, dsts, send_sems, recv_sems):
        cp.wait()


def _rsx_comm(parts):
    shapes = [jax.ShapeDtypeStruct((3,) + p.shape[1:], p.dtype) for p in parts]
    return (list(parts), shapes, 3 * len(parts), _rsx_start, _rsx_finish)


def _rs_add_totals(parts, recvs, chip):
    n = len(parts)
    blocks = [(p.shape[1] // 2, p.shape[2]) for p in parts]

    def body(chip_ref, *refs):
        f = lambda r: r[...].astype(F32)
        for a in range(n):
            p, r0, r1, r2 = refs[a], refs[n + 3 * a], refs[n + 3 * a + 1], refs[n + 3 * a + 2]
            refs[4 * n + a][...] = (f(p) + f(r0)) + (f(r1) + f(r2))

    own = lambda b: pl.BlockSpec((None,) + b, lambda s, chip_ref: (chip_ref[0], s, 0))
    slot = lambda b, k: pl.BlockSpec((None,) + b, lambda s, chip_ref, k=k: (k, s, 0))
    recv_specs = [slot(b, k) for b in blocks for k in range(3)]
    recv_args = [r for r in recvs for _ in range(3)]
    return pl.pallas_call(
        body, name="rs_add_totals",
        grid_spec=pltpu.PrefetchScalarGridSpec(
            num_scalar_prefetch=1, grid=(2,),
            in_specs=[own(b) for b in blocks] + recv_specs,
            out_specs=[pl.BlockSpec(b, lambda s, chip_ref: (s, 0)) for b in blocks]),
        out_shape=[jax.ShapeDtypeStruct(p.shape[1:], F32) for p in parts],
        compiler_params=_params(1),
    )(chip, *parts, *recv_args)


def _permute_w_in(wt):
    return jnp.concatenate([wt[:3072], wt[3080:IN_COLS], wt[3072:3080],
                            jnp.zeros((IN_COLS_PADDED - IN_COLS, wt.shape[1]), wt.dtype)], axis=0)


def _pad_row(v):
    v = v.reshape(1, -1)
    return jnp.pad(v, ((0, 0), (0, D_MODEL - v.shape[1])))


def kernel(x, norm_ffn1, ffn1_gate, ffn1_up, ffn1_down, norm_mix, w_in, conv_w, a_log, dt_bias, dn_norm, w_out, norm_ffn2, ffn2_gate, ffn2_up, ffn2_down, norm_final, loss_target, m_norm_ffn1, m_ffn1_gate, m_ffn1_up, m_ffn1_down, m_norm_mix, m_w_in, m_conv_w, m_a_log, m_dt_bias, m_dn_norm, m_w_out, m_norm_ffn2, m_ffn2_gate, m_ffn2_up, m_ffn2_down, m_norm_final, v_norm_ffn1, v_ffn1_gate, v_ffn1_up, v_ffn1_down, v_norm_mix, v_w_in, v_conv_w, v_a_log, v_dt_bias, v_dn_norm, v_w_out, v_norm_ffn2, v_ffn2_gate, v_ffn2_up, v_ffn2_down, v_norm_final):
    cx, cy, cc = _coords()
    chip = 2 * cx + cy
    stored = lambda t, n: t[0] if n in ROW_SHARDED else t[0].T
    big_w = {n: stored(t, n) for n, t in dict(
        ffn1_gate=ffn1_gate, ffn1_up=ffn1_up, ffn1_down=ffn1_down, w_in=w_in, w_out=w_out,
        ffn2_gate=ffn2_gate, ffn2_up=ffn2_up, ffn2_down=ffn2_down).items()}
    big_m = {n: stored(t, n) for n, t in dict(
        ffn1_gate=m_ffn1_gate, ffn1_up=m_ffn1_up, ffn1_down=m_ffn1_down, w_in=m_w_in, w_out=m_w_out,
        ffn2_gate=m_ffn2_gate, ffn2_up=m_ffn2_up, ffn2_down=m_ffn2_down).items()}
    big_v = {n: stored(t, n) for n, t in dict(
        ffn1_gate=v_ffn1_gate, ffn1_up=v_ffn1_up, ffn1_down=v_ffn1_down, w_in=v_w_in, w_out=v_w_out,
        ffn2_gate=v_ffn2_gate, ffn2_up=v_ffn2_up, ffn2_down=v_ffn2_down).items()}

    cols = IN_COLS // N_CHIPS
    send = {n: big_w[n].astype(BF16) for n in BIG}
    send["w_in"] = jnp.pad(send["w_in"], ((0, W_IN_ROWS - cols), (0, 0)))
    early = tuple(n for n in BIG if n not in LATE_WEIGHTS)
    wts = dict(zip(early, _allgather_arrays([send[n] for n in early])))
    wts["w_in"] = _permute_w_in(wts["w_in"][:, :cols].reshape(IN_COLS, D_MODEL))
    dist = dict(late=[send[n] for n in LATE_WEIGHTS], c=cc.reshape(1).astype(jnp.int32),
                chip=chip.reshape(1).astype(jnp.int32))

    conv_shard = conv_w[0]
    emb = jnp.concatenate([jnp.where((chip == j) & (cc == 0), conv_shard, 0.0) for j in range(N_CHIPS)], axis=1)
    emb = jnp.pad(emb.reshape(6, D_MODEL), ((0, 2), (0, 0)))
    conv_full = _allreduce_small(emb, "allgather_conv_w")[:6].reshape(CONV_WIDTH, 3 * DN_WIDTH)

    zvec = jnp.zeros((1, 128), F32)
    small = dict(norm_ffn1=norm_ffn1, norm_mix=norm_mix, norm_ffn2=norm_ffn2, norm_final=norm_final[None],
                 conv_w=conv_full, avec=zvec.at[0, DN_HEADS:2 * DN_HEADS].set(a_log[0]),
                 dvec=zvec.at[0, DN_HEADS:2 * DN_HEADS].set(dt_bias[0]), dn_norm=dn_norm)

    loss, grad_x, reduced, sg = _local_step(x[0], loss_target[0], wts, small, dist)

    rows = [sg["norm_ffn1"], sg["norm_mix"], sg["norm_ffn2"], sg["norm_final"], _pad_row(sg["a_log"]), _pad_row(sg["dt_bias"]),
            _pad_row(sg["dn_norm"]), _pad_row(loss[0:1]), sg["conv_w"].reshape(6, D_MODEL), jnp.zeros((2, D_MODEL), F32)]
    red = _allreduce_small(jnp.concatenate(rows, axis=0), "allreduce_small")
    loss_out = red[7, 0]
    g_conv_full = red[8:14].reshape(CONV_WIDTH, 3 * DN_WIDTH)
    g_conv = lax.dynamic_slice_in_dim(g_conv_full, chip * (3 * DN_WIDTH // N_CHIPS), 3 * DN_WIDTH // N_CHIPS, axis=1)
    g_small = dict(norm_ffn1=red[0:1], norm_mix=red[1:2], norm_ffn2=red[2:3], norm_final=red[3],
                   a_log=red[4:5, DN_HEADS:2 * DN_HEADS], dt_bias=red[5:6, DN_HEADS:2 * DN_HEADS], dn_norm=red[6:7, :DN_HEAD_DIM])

    out_g, out_d, out_m, out_v = {}, {}, {}, {}
    for n in BIG:
        mine, other = reduced[n]
        g = jnp.where(cc == 0, jnp.concatenate([mine, other], axis=0), jnp.concatenate([other, mine], axis=0))
        if n == "w_in":
            g = g[:cols]
        results = (g,) + tuple(_adamw(big_w[n], g, big_m[n], big_v[n], "adamw_" + n))
        out_g[n], out_d[n], out_m[n], out_v[n] = ((t if n in ROW_SHARDED else t.T)[None] for t in results)
    d, nm, nv = _adamw(conv_w[0], g_conv, m_conv_w[0], v_conv_w[0], "adamw_conv_w")
    out_g["conv_w"], out_d["conv_w"], out_m["conv_w"], out_v["conv_w"] = g_conv[None], d[None], nm[None], nv[None]

    small_names = ("norm_ffn1", "norm_mix", "norm_ffn2", "norm_final", "a_log", "dt_bias", "dn_norm")
    small_w = dict(norm_ffn1=norm_ffn1, norm_mix=norm_mix, norm_ffn2=norm_ffn2, norm_final=norm_final, a_log=a_log,
                   dt_bias=dt_bias, dn_norm=dn_norm)
    small_m = dict(norm_ffn1=m_norm_ffn1, norm_mix=m_norm_mix, norm_ffn2=m_norm_ffn2, norm_final=m_norm_final, a_log=m_a_log,
                   dt_bias=m_dt_bias, dn_norm=m_dn_norm)
    small_v = dict(norm_ffn1=v_norm_ffn1, norm_mix=v_norm_mix, norm_ffn2=v_norm_ffn2, norm_final=v_norm_final, a_log=v_a_log,
                   dt_bias=v_dt_bias, dn_norm=v_dn_norm)
    stack = lambda dct: jnp.concatenate([_pad_row(dct[n]) for n in small_names] + [jnp.zeros((1, D_MODEL), F32)], axis=0)
    d, nm, nv = _adamw(stack(small_w), stack(g_small), stack(small_m), stack(small_v), "adamw_small")
    for k, n in enumerate(small_names):
        shape = small_w[n].shape
        size = math.prod(shape)
        out_g[n] = g_small[n].reshape(shape)
        out_d[n], out_m[n], out_v[n] = (t[k, :size].reshape(shape) for t in (d, nm, nv))

    order = ("norm_ffn1", "ffn1_gate", "ffn1_up", "ffn1_down", "norm_mix", "w_in", "conv_w", "a_log", "dt_bias", "dn_norm",
             "w_out", "norm_ffn2", "ffn2_gate", "ffn2_up", "ffn2_down", "norm_final")
    return (loss_out, grad_x[None], *[out_g[n] for n in order], *[out_d[n] for n in order],
            *[out_m[n] for n in order], *[out_v[n] for n in order])
```

```python
import functools
import math

import jax
import jax.numpy as jnp
from jax import lax
from jax.experimental import pallas as pl
from jax.experimental.pallas import tpu as pltpu

F32 = jnp.float32
BF16 = jnp.bfloat16
HI = lax.Precision.HIGH

D_MODEL = 1024
ATTN_HEADS = 8
ATTN_WIDTH = 512
ATTN_BLOCK = 128
ATTN_SCALE = (ATTN_WIDTH // ATTN_HEADS) ** -0.5
DILATIONS = (1, 4, 16)
DN_HEADS = 4
DN_HEAD_DIM = 128
DN_WIDTH = 512
DN_CHUNK = 64
CONV_WIDTH = 4
NORM_EPS = 1e-6
L2_EPS = 1e-6
QKV_COLS = 3 * ATTN_WIDTH + 3 * DN_WIDTH
LOGIT_COLS = 2 * DN_HEADS
IN_COLS = QKV_COLS + LOGIT_COLS + DN_WIDTH
IN_COLS_PADDED = 3712
N_CHIPS = 4

ADAM_LR = 0.001
ADAM_B1 = 0.9
ADAM_B2 = 0.999
ADAM_EPS = 1e-08
ADAM_WD = 0.01
ADAM_STEP = 10

VMEM_LIMIT = 56 * 1024 * 1024
NEG_BIG = -1e30
MESH = pl.DeviceIdType.MESH


def _params(n_grid, vmem=VMEM_LIMIT):
    return pltpu.CompilerParams(dimension_semantics=("arbitrary",) * n_grid, vmem_limit_bytes=vmem)


def _call(body, args, *, name, grid, in_specs, out_specs, out_shape, scratch_shapes=(), comm=None):
    n_in, n_out, n_scr = len(in_specs), len(out_specs), len(scratch_shapes)
    hbm = pl.BlockSpec(memory_space=pl.ANY)
    srcs, dst_shapes, n_sems, start, finish = comm if comm is not None else ((), (), 0, None, None)
    ns, nd = len(srcs), len(dst_shapes)

    def full(*refs):
        ins, c_src = refs[:n_in], refs[n_in:n_in + ns]
        at = n_in + ns
        outs, c_dst = refs[at:at + n_out], refs[at + n_out:at + n_out + nd]
        scr = refs[at + n_out + nd:at + n_out + nd + n_scr]
        if comm is not None:
            ids = [pl.program_id(a) for a in range(len(grid))]
            first = functools.reduce(jnp.logical_and, [i == 0 for i in ids])
            last = functools.reduce(jnp.logical_and, [i == g - 1 for i, g in zip(ids, grid)])

            @pl.when(first)
            def _():
                start(c_src, c_dst, refs[-2], refs[-1])

        body(*ins, *outs, *scr)
        if comm is not None:
            @pl.when(last)
            def _():
                finish(c_src, c_dst, refs[-2], refs[-1])

    sems = [pltpu.SemaphoreType.DMA((n_sems,)), pltpu.SemaphoreType.DMA((n_sems,))] if comm is not None else []
    res = pl.pallas_call(
        full, name=name, grid=grid, in_specs=list(in_specs) + [hbm] * ns, out_specs=list(out_specs) + [hbm] * nd,
        out_shape=list(out_shape) + list(dst_shapes), scratch_shapes=list(scratch_shapes) + sems,
        compiler_params=_params(len(grid)),
    )(*args, *srcs)
    return res[:n_out], res[n_out:]


def _nt(a, b, precision=None):
    return lax.dot_general(a, b, (((1,), (1,)), ((), ())), preferred_element_type=F32, precision=precision)


def _tn(a, b, precision=None):
    return lax.dot_general(a, b, (((0,), (0,)), ((), ())), preferred_element_type=F32, precision=precision)


def _nn(a, b, precision=None):
    return jnp.dot(a, b, preferred_element_type=F32, precision=precision)


def _sigmoid(x):
    return 1.0 / (1.0 + jnp.exp(-x))


def _ffn_fwd(x, gain, wg, wu, wd, name, comm=None):
    S, D = x.shape
    nf, tf, _ = wg.shape
    tm = 512

    def body(x_ref, gain_ref, wg_ref, wu_ref, wd_ref, xo_ref, h_ref, g_ref, u_ref, acc_ref, hs_ref):
        j = pl.program_id(1)

        @pl.when(j == 0)
        def _():
            xf = x_ref[...]
            r = lax.rsqrt(jnp.mean(xf * xf, axis=-1, keepdims=True) + NORM_EPS)
            h = (xf * r * gain_ref[...]).astype(BF16)
            hs_ref[...] = h
            h_ref[...] = h
            acc_ref[...] = jnp.zeros_like(acc_ref)

        h = hs_ref[...]
        g = _nt(h, wg_ref[...])
        u = _nt(h, wu_ref[...])
        g_ref[...] = g.astype(BF16)
        u_ref[...] = u.astype(BF16)
        act = g * _sigmoid(g) * u
        acc_ref[...] += _nn(act.astype(BF16), wd_ref[...])

        @pl.when(j == nf - 1)
        def _():
            xo_ref[...] = x_ref[...] + 0.5 * acc_ref[...]

    return _call(
        body, (x, gain, wg, wu, wd), name=name, grid=(S // tm, nf), comm=comm,
        in_specs=[pl.BlockSpec((tm, D), lambda i, j: (i, 0)),
                  pl.BlockSpec((1, D), lambda i, j: (0, 0)),
                  pl.BlockSpec((None, tf, D), lambda i, j: (j, 0, 0)),
                  pl.BlockSpec((None, tf, D), lambda i, j: (j, 0, 0)),
                  pl.BlockSpec((None, tf, D), lambda i, j: (j, 0, 0))],
        out_specs=[pl.BlockSpec((tm, D), lambda i, j: (i, 0)),
                   pl.BlockSpec((tm, D), lambda i, j: (i, 0)),
                   pl.BlockSpec((None, tm, tf), lambda i, j: (j, i, 0)),
                   pl.BlockSpec((None, tm, tf), lambda i, j: (j, i, 0))],
        out_shape=[jax.ShapeDtypeStruct((S, D), F32), jax.ShapeDtypeStruct((S, D), BF16),
                   jax.ShapeDtypeStruct((nf, S, tf), BF16), jax.ShapeDtypeStruct((nf, S, tf), BF16)],
        scratch_shapes=[pltpu.VMEM((tm, D), F32), pltpu.VMEM((tm, D), BF16)])


def _rmsnorm_bwd(dh, xf, gain):
    r = lax.rsqrt(jnp.mean(xf * xf, axis=-1, keepdims=True) + NORM_EPS)
    xhat = xf * r
    dgain = jnp.sum(dh * xhat, axis=0, keepdims=True)
    dxh = dh * gain
    dx = r * (dxh - xhat * jnp.mean(dxh * xhat, axis=-1, keepdims=True))
    return dx, dgain


def _ffn_bwd(dxo, x, gain, g, u, wd, wg, wu, name, comm=None):
    S, D = x.shape
    nf, _, tf = g.shape
    tm = 512

    def body(dxo_ref, x_ref, gain_ref, g_ref, u_ref, wd_ref, wg_ref, wu_ref,
             dx_ref, dgain_ref, dg_ref, du_ref, act_ref, dout_ref, acc_ref, ds_ref):
        i = pl.program_id(0)
        j = pl.program_id(1)

        @pl.when(j == 0)
        def _():
            d = (0.5 * dxo_ref[...]).astype(BF16)
            ds_ref[...] = d
            dout_ref[...] = d
            acc_ref[...] = jnp.zeros_like(acc_ref)

        @pl.when((i == 0) & (j == 0))
        def _():
            dgain_ref[...] = jnp.zeros_like(dgain_ref)

        for half in range(2):
            rows = slice(half * (tm // 2), (half + 1) * (tm // 2))
            dact = _nt(ds_ref[rows, :], wd_ref[...])
            gv = g_ref[rows, :].astype(F32)
            uv = u_ref[rows, :].astype(F32)
            sg = _sigmoid(gv)
            silu = gv * sg
            act_ref[rows, :] = (silu * uv).astype(BF16)
            dgv = (dact * uv * (sg * (1.0 + gv * (1.0 - sg)))).astype(BF16)
            duv = (dact * silu).astype(BF16)
            dg_ref[rows, :] = dgv
            du_ref[rows, :] = duv
            acc_ref[rows, :] += _nn(dgv, wg_ref[...]) + _nn(duv, wu_ref[...])

        @pl.when(j == nf - 1)
        def _():
            dx, dgain = _rmsnorm_bwd(acc_ref[...], x_ref[...], gain_ref[...])
            dx_ref[...] = dxo_ref[...] + dx
            dgain_ref[...] += dgain

    return _call(
        body, (dxo, x, gain, g, u, wd, wg, wu), name=name, grid=(S // tm, nf), comm=comm,
        in_specs=[pl.BlockSpec((tm, D), lambda i, j: (i, 0)),
                  pl.BlockSpec((tm, D), lambda i, j: (i, 0)),
                  pl.BlockSpec((1, D), lambda i, j: (0, 0)),
                  pl.BlockSpec((None, tm, tf), lambda i, j: (j, i, 0)),
                  pl.BlockSpec((None, tm, tf), lambda i, j: (j, i, 0)),
                  pl.BlockSpec((None, tf, D), lambda i, j: (j, 0, 0)),
                  pl.BlockSpec((None, tf, D), lambda i, j: (j, 0, 0)),
                  pl.BlockSpec((None, tf, D), lambda i, j: (j, 0, 0))],
        out_specs=[pl.BlockSpec((tm, D), lambda i, j: (i, 0)),
                   pl.BlockSpec((1, D), lambda i, j: (0, 0)),
                   pl.BlockSpec((None, tm, tf), lambda i, j: (j, i, 0)),
                   pl.BlockSpec((None, tm, tf), lambda i, j: (j, i, 0)),
                   pl.BlockSpec((None, tm, tf), lambda i, j: (j, i, 0)),
                   pl.BlockSpec((tm, D), lambda i, j: (i, 0))],
        out_shape=[jax.ShapeDtypeStruct((S, D), F32), jax.ShapeDtypeStruct((1, D), F32),
                   jax.ShapeDtypeStruct((nf, S, tf), BF16), jax.ShapeDtypeStruct((nf, S, tf), BF16),
                   jax.ShapeDtypeStruct((nf, S, tf), BF16), jax.ShapeDtypeStruct((S, D), BF16)],
        scratch_shapes=[pltpu.VMEM((tm, D), F32), pltpu.VMEM((tm, D), BF16)])


def _matmul_tn(a, b, tm, tk, name):
    K, M = a.shape
    N = b.shape[1]

    def body(a_ref, b_ref, o_ref):
        @pl.when(pl.program_id(1) == 0)
        def _():
            o_ref[...] = jnp.zeros_like(o_ref)

        o_ref[...] += _tn(a_ref[...], b_ref[...])

    return pl.pallas_call(
        body, name=name, grid=(M // tm, K // tk),
        in_specs=[pl.BlockSpec((tk, tm), lambda i, k: (k, i)),
                  pl.BlockSpec((tk, N), lambda i, k: (k, 0))],
        out_specs=pl.BlockSpec((tm, N), lambda i, k: (i, 0)),
        out_shape=jax.ShapeDtypeStruct((M, N), F32),
        compiler_params=_params(2),
    )(a, b)


def _dw_chunks(a, b, tk, name, comm=None):
    nf, S, tf = a.shape
    N = b.shape[1]

    def body(a_ref, b_ref, o_ref):
        @pl.when(pl.program_id(1) == 0)
        def _():
            o_ref[...] = jnp.zeros_like(o_ref)

        o_ref[...] += _tn(a_ref[...], b_ref[...])

    (out,), landed = _call(
        body, (a, b), name=name, grid=(nf, S // tk), comm=comm,
        in_specs=[pl.BlockSpec((None, tk, tf), lambda j, k: (j, k, 0)),
                  pl.BlockSpec((tk, N), lambda j, k: (k, 0))],
        out_specs=[pl.BlockSpec((None, tf, N), lambda j, k: (j, 0, 0))],
        out_shape=[jax.ShapeDtypeStruct((nf, tf, N), F32)])
    return out, landed


VIEW_TILE = 512


def _view_spec(d, tile=VIEW_TILE):
    return pl.BlockSpec((tile // d, d * ATTN_WIDTH), lambda i: (i, 0))


def _view_shape(S, d, dtype):
    return jax.ShapeDtypeStruct((S // d, d * ATTN_WIDTH), dtype)


def _tile_to_views(val, planes, out_refs):
    for g in range(4):
        planes[g] = val[:, g * 128:(g + 1) * 128]
    for d, ref in zip(DILATIONS, out_refs):
        if d == 1:
            ref[...] = val.astype(ref.dtype)
            continue
        for r in range(d):
            for g in range(4):
                ref[:, r * ATTN_WIDTH + g * 128:r * ATTN_WIDTH + (g + 1) * 128] = (
                    planes[g, pl.ds(r, planes.shape[1] // d, stride=d), :].astype(ref.dtype))


def _view_to_tile(ref, d, planes):
    if d == 1:
        return ref[...].astype(F32)
    for r in range(d):
        for g in range(4):
            planes[g, pl.ds(r, planes.shape[1] // d, stride=d), :] = (
                ref[:, r * ATTN_WIDTH + g * 128:r * ATTN_WIDTH + (g + 1) * 128].astype(F32))
    return jnp.concatenate([planes[g] for g in range(4)], axis=1)


def _inproj_fwd(x, gain, w_in_p):
    S, D = x.shape
    tm = VIEW_TILE
    W = ATTN_WIDTH

    def body(x_ref, gain_ref, w_ref, h_ref, q1, q4, q16, k1, k4, k16, v1, v4, v16, dq_ref, dk_ref, dv_ref, gate_ref, bd_ref,
             planes):
        xf = x_ref[...]
        r = lax.rsqrt(jnp.mean(xf * xf, axis=-1, keepdims=True) + NORM_EPS)
        h = (xf * r * gain_ref[...]).astype(BF16)
        h_ref[...] = h
        _tile_to_views(_nt(h, w_ref[0:W, :]) * ATTN_SCALE, planes, (q1, q4, q16))
        _tile_to_views(_nt(h, w_ref[W:2 * W, :]), planes, (k1, k4, k16))
        _tile_to_views(_nt(h, w_ref[2 * W:3 * W, :]), planes, (v1, v4, v16))
        dq_ref[...] = _nt(h, w_ref[3 * W:4 * W, :])
        dk_ref[...] = _nt(h, w_ref[4 * W:5 * W, :])
        dv_ref[...] = _nt(h, w_ref[5 * W:6 * W, :])
        gate_ref[...] = _nt(h, w_ref[6 * W:7 * W, :])
        bd_ref[...] = _nt(h, w_ref[7 * W:7 * W + 128, :])

    tok = lambda w: pl.BlockSpec((tm, w), lambda i: (i, 0))
    return pl.pallas_call(
        body, name="inproj_fwd", grid=(S // tm,),
        in_specs=[tok(D), pl.BlockSpec((1, D), lambda i: (0, 0)),
                  pl.BlockSpec((IN_COLS_PADDED, D), lambda i: (0, 0))],
        out_specs=[tok(D)] + [_view_spec(d) for d in DILATIONS] * 3 + [tok(W)] * 4 + [tok(128)],
        out_shape=[jax.ShapeDtypeStruct((S, D), BF16)] + [_view_shape(S, d, BF16) for d in DILATIONS] * 3
                  + [jax.ShapeDtypeStruct((S, W), F32)] * 4 + [jax.ShapeDtypeStruct((S, 128), F32)],
        scratch_shapes=[pltpu.VMEM((4, tm, 128), F32)],
        compiler_params=_params(1),
    )(x, gain, w_in_p)


def _inproj_bwd(dxo, x, gain, attn_grads, dsecs, dbd, w_in_p):
    S, D = x.shape
    tm = VIEW_TILE
    W = ATTN_WIDTH

    def body(dxo_ref, x_ref, gain_ref, *rest):
        views, (s3, s4, s5, s6, dbd_ref, w_ref, dx_ref, dgain_ref, dproj_ref, planes) = rest[:9], rest[9:]

        @pl.when(pl.program_id(0) == 0)
        def _():
            dgain_ref[...] = jnp.zeros_like(dgain_ref)

        secs = []
        for k in range(3):
            parts = [_view_to_tile(views[3 * k + p], d, planes) for p, d in enumerate(DILATIONS)]
            secs.append(parts[0] + parts[1] + parts[2])
        secs += [s3[...], s4[...], s5[...], s6[...]]
        dh = jnp.zeros((tm, D), F32)
        for k, s in enumerate(secs):
            d = s.astype(BF16)
            dproj_ref[:, k * W:(k + 1) * W] = d
            dh += _nn(d, w_ref[k * W:(k + 1) * W, :])
        d = dbd_ref[...].astype(BF16)
        dproj_ref[:, 7 * W:7 * W + 128] = d
        dh += _nn(d, w_ref[7 * W:7 * W + 128, :])
        dx, dgain = _rmsnorm_bwd(dh, x_ref[...], gain_ref[...])
        dx_ref[...] = dxo_ref[...] + dx
        dgain_ref[...] += dgain

    tok = lambda w: pl.BlockSpec((tm, w), lambda i: (i, 0))
    return pl.pallas_call(
        body, name="inproj_bwd", grid=(S // tm,),
        in_specs=[tok(D), tok(D), pl.BlockSpec((1, D), lambda i: (0, 0))] + [_view_spec(d, tm) for d in DILATIONS] * 3
                 + [tok(W)] * 4 + [tok(128)] + [pl.BlockSpec((IN_COLS_PADDED, D), lambda i: (0, 0))],
        out_specs=[tok(D), pl.BlockSpec((1, D), lambda i: (0, 0)), tok(IN_COLS_PADDED)],
        out_shape=[jax.ShapeDtypeStruct((S, D), F32), jax.ShapeDtypeStruct((1, D), F32),
                   jax.ShapeDtypeStruct((S, IN_COLS_PADDED), BF16)],
        scratch_shapes=[pltpu.VMEM((4, tm, 128), F32)],
        compiler_params=_params(1),
    )(dxo, x, gain, *[g for grads in attn_grads for g in grads], *dsecs, dbd, w_in_p)


def _slope(h):
    return 2.0 ** (-8.0 * (h + 1) / ATTN_HEADS)


def _head_bias(steps, d, heads=tuple(range(ATTN_HEADS))):
    stepsf = steps.astype(F32)
    return jnp.stack([stepsf * (-_slope(h) * d) for h in heads])


def _hnt(a, b):
    return lax.dot_general(a, b, (((2,), (2,)), ((0,), (0,))), preferred_element_type=F32)


def _hnn(a, b):
    return lax.dot_general(a, b, (((2,), (1,)), ((0,), (0,))), preferred_element_type=F32)


def _blocks_per_step(nb):
    return next(n for n in (4, 2, 1) if nb % n == 0)


def _query_step_specs(qb):
    B = ATTN_BLOCK
    cur = pl.BlockSpec((qb * B, ATTN_WIDTH), lambda r, n: (n, r))
    prev = pl.BlockSpec((B, ATTN_WIDTH), lambda r, n: (jnp.maximum(qb * n - 1, 0), r))
    return cur, prev


def _prev_block(prev_ref, cur_ref, sub, sl):
    B = ATTN_BLOCK
    return prev_ref[:, sl] if sub == 0 else cur_ref[(sub - 1) * B:sub * B, sl]


def _head_cols(tile, lo, big):
    return [_head_col(tile, lo, big), _head_col(tile, jnp.logical_not(lo), big)]


def _attn_fwd(q, k, v, d, name):
    L = q.shape[0]
    nb = L // ATTN_BLOCK
    B = ATTN_BLOCK
    QB = _blocks_per_step(nb)

    def body(q_ref, kp_ref, kc_ref, vp_ref, vc_ref, o_ref, lse_ref):
        n = pl.program_id(1)
        qi = lax.broadcasted_iota(jnp.int32, (B, 2 * B), 0)
        kj = lax.broadcasted_iota(jnp.int32, (B, 2 * B), 1)
        steps = qi + B - kj
        band = (steps >= 0) & (steps <= B)
        lo = lax.broadcasted_iota(jnp.int32, (B, 128), 1) < 64
        bias = _head_bias(steps, d)
        for sub in range(QB):
            rows = slice(sub * B, (sub + 1) * B)
            valid = band & ((kj >= B) | (n > 0)) if sub == 0 else band
            qs, ks, vs = [], [], []
            for G in range(4):
                sl = slice(G * 128, (G + 1) * 128)
                qg = q_ref[rows, sl]
                kg = jnp.concatenate([_prev_block(kp_ref, kc_ref, sub, sl), kc_ref[rows, sl]], axis=0)
                vg = jnp.concatenate([_prev_block(vp_ref, vc_ref, sub, sl), vc_ref[rows, sl]], axis=0)
                qs += [jnp.where(lo, qg, jnp.zeros_like(qg)), jnp.where(lo, jnp.zeros_like(qg), qg)]
                ks += [kg, kg]
                vs += [vg, vg]
            s = jnp.where(valid, _hnt(jnp.stack(qs), jnp.stack(ks)) + bias, NEG_BIG)
            m = jnp.max(s, axis=-1, keepdims=True)
            p = jnp.exp(s - m)
            l = jnp.sum(p, axis=-1, keepdims=True)
            o = _hnn(p.astype(BF16), jnp.stack(vs)) / l
            lse = m + jnp.log(l)
            for G in range(4):
                sl = slice(G * 128, (G + 1) * 128)
                o_ref[rows, sl] = jnp.where(lo, o[2 * G], o[2 * G + 1])
                lse_ref[rows, sl] = jnp.where(lo, lse[2 * G], lse[2 * G + 1])

    cur, prev = _query_step_specs(QB)
    return pl.pallas_call(
        body, name=name, grid=(d, nb // QB),
        in_specs=[cur, prev, cur, prev, cur],
        out_specs=[cur, cur],
        out_shape=[jax.ShapeDtypeStruct((L, d * ATTN_WIDTH), F32)] * 2,
        compiler_params=_params(2),
    )(q, k, k, v, v)


def _attn_merge(parts):
    S = parts[0][0].shape[0]
    tm = VIEW_TILE

    def body(o1, s1, o2, s2, o3, s3, o_ref, lse1, lse4, lse16, planes):
        outs, lses = [], []
        for d, (o, s) in zip(DILATIONS, ((o1, s1), (o2, s2), (o3, s3))):
            outs.append(_view_to_tile(o, d, planes))
            lses.append(_view_to_tile(s, d, planes))
        mx = jnp.maximum(jnp.maximum(lses[0], lses[1]), lses[2])
        es = [jnp.exp(s - mx) for s in lses]
        den = es[0] + es[1] + es[2]
        o_ref[...] = (es[0] * outs[0] + es[1] * outs[1] + es[2] * outs[2]) / den
        _tile_to_views(mx + jnp.log(den), planes, (lse1, lse4, lse16))

    views = [_view_spec(d) for d in DILATIONS]
    flat = [t for p in parts for t in p]
    return pl.pallas_call(
        body, name="attn_merge", grid=(S // tm,),
        in_specs=[views[p] for p in range(3) for _ in range(2)],
        out_specs=[views[0]] + views,
        out_shape=[jax.ShapeDtypeStruct((S, ATTN_WIDTH), F32)] + [_view_shape(S, d, F32) for d in DILATIONS],
        scratch_shapes=[pltpu.VMEM((4, tm, 128), F32)],
        compiler_params=_params(1),
    )(*flat)


def _head_col(t, msk, big):
    if big:
        return jnp.max(jnp.where(msk, t, NEG_BIG), axis=-1, keepdims=True)
    return jnp.sum(jnp.where(msk, t, 0.0), axis=-1, keepdims=True) * (1.0 / 64.0)


def _attn_bwd_q(q, k, v, do, lse, dd, d, name):
    L = q.shape[0]
    nb = L // ATTN_BLOCK
    B = ATTN_BLOCK
    QB = _blocks_per_step(nb)

    def body(q_ref, kp_ref, kc_ref, vp_ref, vc_ref, do_ref, lse_ref, dd_ref, dq_ref):
        n = pl.program_id(1)
        qi = lax.broadcasted_iota(jnp.int32, (B, 2 * B), 0)
        kj = lax.broadcasted_iota(jnp.int32, (B, 2 * B), 1)
        steps = qi + B - kj
        band = (steps >= 0) & (steps <= B)
        lo = lax.broadcasted_iota(jnp.int32, (B, 128), 1) < 64
        bias = _head_bias(steps, d)
        for sub in range(QB):
            rows = slice(sub * B, (sub + 1) * B)
            valid = band & ((kj >= B) | (n > 0)) if sub == 0 else band
            qs, ks, vs, dos, lses, dcols = [], [], [], [], [], []
            for G in range(4):
                sl = slice(G * 128, (G + 1) * 128)
                qg = q_ref[rows, sl]
                kg = jnp.concatenate([_prev_block(kp_ref, kc_ref, sub, sl), kc_ref[rows, sl]], axis=0)
                vg = jnp.concatenate([_prev_block(vp_ref, vc_ref, sub, sl), vc_ref[rows, sl]], axis=0)
                dog = do_ref[rows, sl]
                qs += [jnp.where(lo, qg, jnp.zeros_like(qg)), jnp.where(lo, jnp.zeros_like(qg), qg)]
                dos += [jnp.where(lo, dog, 0.0).astype(BF16), jnp.where(lo, 0.0, dog).astype(BF16)]
                ks += [kg, kg]
                vs += [vg, vg]
                lses += _head_cols(lse_ref[rows, sl], lo, True)
                dcols += _head_cols(dd_ref[rows, sl], lo, False)
            kb = jnp.stack(ks)
            s = _hnt(jnp.stack(qs), kb) + bias
            p = jnp.where(valid, jnp.exp(jnp.where(valid, s, NEG_BIG) - jnp.stack(lses)), 0.0)
            dp = _hnt(jnp.stack(dos), jnp.stack(vs))
            ds = p * (dp - jnp.stack(dcols))
            dq = _hnn(ds.astype(BF16), kb) * ATTN_SCALE
            for G in range(4):
                dq_ref[rows, G * 128:(G + 1) * 128] = jnp.where(lo, dq[2 * G], dq[2 * G + 1]).astype(BF16)

    cur, prev = _query_step_specs(QB)
    return pl.pallas_call(
        body, name=name, grid=(d, nb // QB), in_specs=[cur, prev, cur, prev, cur, cur, cur, cur], out_specs=cur,
        out_shape=jax.ShapeDtypeStruct((L, d * ATTN_WIDTH), BF16), compiler_params=_params(2),
    )(q, k, k, v, v, do, lse, dd)


def _attn_bwd_kv(q, k, v, do, lse, dd, d, name):
    L = q.shape[0]
    nb = L // ATTN_BLOCK
    B = ATTN_BLOCK
    KB = _blocks_per_step(nb)
    n_steps = nb // KB

    def body(k_ref, v_ref, qc_ref, qn_ref, doc_ref, don_ref, lsec_ref, lsen_ref, ddc_ref, ddn_ref, dk_ref, dv_ref):
        j = pl.program_id(1)
        qrow = lax.broadcasted_iota(jnp.int32, (2 * B, B), 0)
        kk = lax.broadcasted_iota(jnp.int32, (2 * B, B), 1)
        steps = qrow - kk
        band = (steps >= 0) & (steps <= B)
        lo2 = lax.broadcasted_iota(jnp.int32, (2 * B, 128), 1) < 64
        lo = lax.broadcasted_iota(jnp.int32, (B, 128), 1) < 64
        stepsf = steps.astype(F32)
        for sub in range(KB):
            rows = slice(sub * B, (sub + 1) * B)
            last = sub == KB - 1
            valid = band & ((qrow < B) | (j < n_steps - 1)) if last else band
            after = lambda cur_ref, nxt_ref, sl: nxt_ref[:, sl] if last else cur_ref[(sub + 1) * B:(sub + 2) * B, sl]
            for G in range(4):
                sl = slice(G * 128, (G + 1) * 128)
                kg = k_ref[rows, sl]
                vg = v_ref[rows, sl]
                qq = jnp.concatenate([qc_ref[rows, sl], after(qc_ref, qn_ref, sl)], axis=0)
                doo = jnp.concatenate([doc_ref[rows, sl], after(doc_ref, don_ref, sl)], axis=0)
                lse2 = jnp.concatenate([lsec_ref[rows, sl], after(lsec_ref, lsen_ref, sl)], axis=0)
                dd2 = jnp.concatenate([ddc_ref[rows, sl], after(ddc_ref, ddn_ref, sl)], axis=0)
                doo_b = doo.astype(BF16)
                dks, dvs = [], []
                for half in (0, 1):
                    msk = lo2 if half == 0 else jnp.logical_not(lo2)
                    qm = jnp.where(msk, qq, jnp.zeros_like(qq))
                    s = _nt(qm, kg) - (_slope(2 * G + half) * d) * stepsf
                    lse_c = _head_col(lse2, msk, True)
                    p = jnp.where(valid, jnp.exp(jnp.where(valid, s, NEG_BIG) - lse_c), 0.0)
                    dvs.append(_tn(p.astype(BF16), doo_b))
                    dom = jnp.where(msk, doo, 0.0).astype(BF16)
                    dp = _nt(dom, vg)
                    dcol = _head_col(dd2, msk, False)
                    ds = p * (dp - dcol)
                    dks.append(_tn(ds.astype(BF16), qq))
                dk_ref[rows, sl] = jnp.where(lo, dks[0], dks[1]).astype(BF16)
                dv_ref[rows, sl] = jnp.where(lo, dvs[0], dvs[1]).astype(BF16)

    cur = pl.BlockSpec((KB * B, ATTN_WIDTH), lambda r, j: (j, r))
    nxt = pl.BlockSpec((B, ATTN_WIDTH), lambda r, j: (jnp.minimum(KB * (j + 1), nb - 1), r))
    return pl.pallas_call(
        body, name=name, grid=(d, n_steps), in_specs=[cur, cur, cur, nxt, cur, nxt, cur, nxt, cur, nxt],
        out_specs=[cur, cur],
        out_shape=[jax.ShapeDtypeStruct((L, d * ATTN_WIDTH), BF16)] * 2, compiler_params=_params(2),
    )(k, v, q, q, do, do, lse, lse, dd, dd)


CONV_T = 512
HALO = 8


def _per_head(head, refs):
    for h in range(DN_HEADS):
        lanes = pl.ds(h * DN_HEAD_DIM, DN_HEAD_DIM)
        head(*[r.at[:, lanes] for r in refs[:-1]], refs[-1])


def _conv_taps(pad_ref, w, T):
    acc = pad_ref[pl.ds(HALO - 3, T), :] * w[0:1, :]
    for j in range(1, CONV_WIDTH):
        acc = acc + pad_ref[pl.ds(HALO - 3 + j, T), :] * w[j:j + 1, :]
    return acc


def _conv_fwd(xq, xk, xv, conv_w):
    S = xq.shape[0]
    T = CONV_T

    def body(*refs):
        _per_head(head, refs)

    def head(xq_ref, xqh_ref, xk_ref, xkh_ref, xv_ref, xvh_ref, wq_ref, wk_ref, wv_ref,
             qn_ref, kn_ref, v_ref, pad_ref):
        i = pl.program_id(0)

        def act(x_ref, xh_ref, w_ref):
            pad_ref[pl.ds(0, HALO), :] = jnp.where(i > 0, xh_ref[...], 0.0)
            pad_ref[pl.ds(HALO, T), :] = x_ref[...]
            c = _conv_taps(pad_ref, w_ref[...], T)
            return c * _sigmoid(c)

        def l2n(t):
            return t * lax.rsqrt(jnp.sum(t * t, axis=-1, keepdims=True) + L2_EPS)

        qn_ref[...] = l2n(act(xq_ref, xqh_ref, wq_ref))
        kn_ref[...] = l2n(act(xk_ref, xkh_ref, wk_ref))
        v_ref[...] = act(xv_ref, xvh_ref, wv_ref)

    tile = pl.BlockSpec((T, DN_WIDTH), lambda i: (i, 0))
    halo = pl.BlockSpec((HALO, DN_WIDTH), lambda i: (jnp.maximum(i * (T // HALO) - 1, 0), 0))
    wspec = lambda sec: pl.BlockSpec((CONV_WIDTH, DN_WIDTH), lambda i, sec=sec: (0, sec))
    return pl.pallas_call(
        body, name="dn_conv_fwd", grid=(S // T,),
        in_specs=[tile, halo, tile, halo, tile, halo, wspec(0), wspec(1), wspec(2)],
        out_specs=[tile, tile, tile],
        out_shape=[jax.ShapeDtypeStruct((S, DN_WIDTH), F32)] * 3,
        scratch_shapes=[pltpu.VMEM((T + HALO, 128), F32)],
        compiler_params=_params(1),
    )(xq, xq, xk, xk, xv, xv, conv_w, conv_w, conv_w)


def _conv_bwd_pre(xq, xk, xv, conv_w, dqn, dkn, dv):
    S = xq.shape[0]
    T = CONV_T

    def body(*refs):
        _per_head(head, refs)

    def head(xq_ref, xqh_ref, xk_ref, xkh_ref, xv_ref, xvh_ref, wq_ref, wk_ref, wv_ref,
             dqn_ref, dkn_ref, dv_ref, dcq_ref, dck_ref, dcv_ref, dwq_ref, dwk_ref, dwv_ref, pad_ref):
        i = pl.program_id(0)

        def one(x_ref, xh_ref, w_ref, dy_ref, dc_ref, dw_ref, normed):
            pad_ref[pl.ds(0, HALO), :] = jnp.where(i > 0, xh_ref[...], 0.0)
            pad_ref[pl.ds(HALO, T), :] = x_ref[...]
            c = _conv_taps(pad_ref, w_ref[...], T)
            sg = _sigmoid(c)
            a = c * sg
            dy = dy_ref[...]
            if normed:
                r = lax.rsqrt(jnp.sum(a * a, axis=-1, keepdims=True) + L2_EPS)
                y = a * r
                da = r * (dy - y * jnp.sum(dy * y, axis=-1, keepdims=True))
            else:
                da = dy
            dc = da * (sg * (1.0 + c * (1.0 - sg)))
            dc_ref[...] = dc

            @pl.when(i == 0)
            def _():
                dw_ref[...] = jnp.zeros_like(dw_ref)

            rows = [jnp.sum(dc * pad_ref[pl.ds(HALO - 3 + j, T), :], axis=0, keepdims=True) for j in range(CONV_WIDTH)]
            dw_ref[...] += jnp.concatenate(rows + [jnp.zeros((8 - CONV_WIDTH, 128), F32)], axis=0)

        one(xq_ref, xqh_ref, wq_ref, dqn_ref, dcq_ref, dwq_ref, True)
        one(xk_ref, xkh_ref, wk_ref, dkn_ref, dck_ref, dwk_ref, True)
        one(xv_ref, xvh_ref, wv_ref, dv_ref, dcv_ref, dwv_ref, False)

    tile = pl.BlockSpec((T, DN_WIDTH), lambda i: (i, 0))
    halo = pl.BlockSpec((HALO, DN_WIDTH), lambda i: (jnp.maximum(i * (T // HALO) - 1, 0), 0))
    wspec = lambda sec: pl.BlockSpec((CONV_WIDTH, DN_WIDTH), lambda i, sec=sec: (0, sec))
    dwspec = pl.BlockSpec((8, DN_WIDTH), lambda i: (0, 0))
    return pl.pallas_call(
        body, name="dn_conv_bwd_pre", grid=(S // T,),
        in_specs=[tile, halo, tile, halo, tile, halo, wspec(0), wspec(1), wspec(2), tile, tile, tile],
        out_specs=[tile, tile, tile, dwspec, dwspec, dwspec],
        out_shape=[jax.ShapeDtypeStruct((S, DN_WIDTH), F32)] * 3 + [jax.ShapeDtypeStruct((8, DN_WIDTH), F32)] * 3,
        scratch_shapes=[pltpu.VMEM((T + HALO, 128), F32)],
        compiler_params=_params(1),
    )(xq, xq, xk, xk, xv, xv, conv_w, conv_w, conv_w, dqn, dkn, dv)


def _conv_bwd_x(dcq, dck, dcv, conv_w):
    S = dcq.shape[0]
    T = CONV_T
    nt = S // T

    def body(*refs):
        _per_head(head, refs)

    def head(dq_ref, dqh_ref, dk_ref, dkh_ref, dv_ref, dvh_ref, wq_ref, wk_ref, wv_ref,
             oq_ref, ok_ref, ov_ref, pad_ref):
        i = pl.program_id(0)

        def one(d_ref, dh_ref, w_ref, o_ref):
            pad_ref[pl.ds(0, T), :] = d_ref[...]
            pad_ref[pl.ds(T, HALO), :] = jnp.where(i < nt - 1, dh_ref[...], 0.0)
            w = w_ref[...]
            acc = pad_ref[pl.ds(3, T), :] * w[0:1, :]
            for j in range(1, CONV_WIDTH):
                acc = acc + pad_ref[pl.ds(3 - j, T), :] * w[j:j + 1, :]
            o_ref[...] = acc

        one(dq_ref, dqh_ref, wq_ref, oq_ref)
        one(dk_ref, dkh_ref, wk_ref, ok_ref)
        one(dv_ref, dvh_ref, wv_ref, ov_ref)

    tile = pl.BlockSpec((T, DN_WIDTH), lambda i: (i, 0))
    halo = pl.BlockSpec((HALO, DN_WIDTH), lambda i: (jnp.minimum((i + 1) * (T // HALO), S // HALO - 1), 0))
    wspec = lambda sec: pl.BlockSpec((CONV_WIDTH, DN_WIDTH), lambda i, sec=sec: (0, sec))
    return pl.pallas_call(
        body, name="dn_conv_bwd_x", grid=(nt,),
        in_specs=[tile, halo, tile, halo, tile, halo, wspec(0), wspec(1), wspec(2)],
        out_specs=[tile, tile, tile],
        out_shape=[jax.ShapeDtypeStruct((S, DN_WIDTH), F32)] * 3,
        scratch_shapes=[pltpu.VMEM((T + HALO, 128), F32)],
        compiler_params=_params(1),
    )(dcq, dcq, dck, dck, dcv, dcv, conv_w, conv_w, conv_w)


PREP_CHUNKS = 4
SCAN_CHUNKS = 8


def _bnn(a, b):
    return lax.dot_general(a, b, (((2,), (1,)), ((0,), (0,))), preferred_element_type=F32, precision=HI)


def _bnt(a, b):
    return lax.dot_general(a, b, (((2,), (2,)), ((0,), (0,))), preferred_element_type=F32, precision=HI)


def _btn(a, b):
    return lax.dot_general(a, b, (((1,), (1,)), ((0,), (0,))), preferred_element_type=F32, precision=HI)


def _tri_inverse_b(a, blk, eye):
    dg = jnp.where(blk, a, 0.0)
    lo = a - dg
    d2 = _bnn(dg, dg)
    d4 = _bnn(d2, d2)
    d8 = _bnn(d4, d4)
    td = _bnn(_bnn(_bnn(eye - dg, eye + d2), eye + d4), eye + d8)
    b = _bnn(td, lo)
    b2 = _bnn(b, b)
    return _bnn(_bnn(eye - b, eye + b2), td)


def _dn_common_b(bds, avec, dvec, q_raw, k, v, t=None):
    C = DN_CHUNK
    lane = lax.broadcasted_iota(jnp.int32, (C, 128), 1)
    row = lax.broadcasted_iota(jnp.int32, (1, C, C), 1)
    col = lax.broadcasted_iota(jnp.int32, (1, C, C), 2)
    incl = row >= col
    strict = row > col
    eye = (row == col).astype(F32)
    blk = (row // 16) == (col // 16)
    pick = lambda tile, ln: jnp.sum(jnp.where(lane == ln, tile, 0.0), axis=-1, keepdims=True)
    betas, graws, zcs = [], [], []
    for bd in bds:
        z = bd + dvec
        g_all = -jnp.exp(avec) * (jnp.maximum(z, 0.0) + jnp.log(1.0 + jnp.exp(-jnp.abs(z))))
        beta_all = _sigmoid(bd)
        for h in range(DN_HEADS):
            betas.append(pick(beta_all, h))
            graws.append(pick(g_all, DN_HEADS + h))
            zcs.append(pick(z, DN_HEADS + h))
    beta, graw, zc = jnp.stack(betas), jnp.stack(graws), jnp.stack(zcs)
    to_row = lambda c: jnp.sum(eye * c, axis=1, keepdims=True)
    gc = jnp.sum(jnp.where(incl, to_row(graw), 0.0), axis=-1, keepdims=True)
    decay = jnp.exp(jnp.where(incl, gc - to_row(gc), NEG_BIG))
    q = q_raw * (DN_HEAD_DIM ** -0.5)
    kb = k * beta
    kk = _bnt(kb, k)
    if t is None:
        t = _tri_inverse_b(jnp.where(strict, kk * decay, 0.0), blk, eye)
    eg = jnp.exp(gc)
    rhs_w = kb * eg
    u = _bnn(t, v * beta)
    w = _bnn(t, rhs_w)
    qk = _bnt(q, k)
    aq = jnp.where(incl, qk * decay, 0.0)
    last = lax.broadcasted_iota(jnp.int32, (1, C, 1), 1) == C - 1
    g_last = jnp.sum(jnp.where(last, gc, 0.0), axis=1, keepdims=True)
    ekd = jnp.exp(g_last - gc)
    return dict(beta=beta, graw=graw, zc=zc, gc=gc, decay=decay, q=q, kb=kb, kk=kk, t=t, eg=eg, rhs_w=rhs_w,
                u=u, w=w, qk=qk, aq=aq, g_last=g_last, ekd=ekd, kd=k * ekd, qg=q * eg,
                incl=incl, strict=strict, eye=eye, lane=lane, row=row, col=col, last=last)


def _stack_heads(ref, rows):
    return jnp.stack([ref[rows, h * DN_HEAD_DIM:(h + 1) * DN_HEAD_DIM] for h in range(DN_HEADS)])


def _stack_units(ref, nc):
    C = DN_CHUNK
    return jnp.concatenate([_stack_heads(ref, slice(ci * C, (ci + 1) * C)) for ci in range(nc)], axis=0)


def _store_units(ref, val, nc):
    C = DN_CHUNK
    for ci in range(nc):
        for h in range(DN_HEADS):
            ref[ci * C:(ci + 1) * C, h * DN_HEAD_DIM:(h + 1) * DN_HEAD_DIM] = val[ci * DN_HEADS + h]


def _dn_prep(qn, kn, v, bd, avec, dvec):
    S = qn.shape[0]
    C = DN_CHUNK
    N = S // C
    nc = PREP_CHUNKS

    def body(q_ref, k_ref, v_ref, bd_ref, a_ref, d_ref, u_ref, w_ref, qg_ref, kd_ref, aq_ref, t_ref, egl_ref):
        bds = [bd_ref[ci * C:(ci + 1) * C, :] for ci in range(nc)]
        c = _dn_common_b(bds, a_ref[...], d_ref[...], _stack_units(q_ref, nc), _stack_units(k_ref, nc), _stack_units(v_ref, nc))
        _store_units(u_ref, c["u"], nc)
        _store_units(w_ref, c["w"], nc)
        _store_units(qg_ref, c["qg"], nc)
        _store_units(kd_ref, c["kd"], nc)
        egl = jnp.broadcast_to(jnp.exp(c["g_last"]), (nc * DN_HEADS, 1, 128))
        for ci in range(nc):
            for h in range(DN_HEADS):
                aq_ref[h, ci * C:(ci + 1) * C, :] = c["aq"][ci * DN_HEADS + h]
                t_ref[h, ci * C:(ci + 1) * C, :] = c["t"][ci * DN_HEADS + h]
            egl_ref[ci * 8:(ci + 1) * 8, :] = jnp.concatenate(
                [egl[ci * DN_HEADS + h] for h in range(DN_HEADS)] + [jnp.zeros((8 - DN_HEADS, 128), F32)], axis=0)

    tok = lambda w: pl.BlockSpec((nc * C, w), lambda n: (n, 0))
    sq = pl.BlockSpec((DN_HEADS, nc * C, C), lambda n: (0, n, 0))
    vec = pl.BlockSpec((1, 128), lambda n: (0, 0))
    return pl.pallas_call(
        body, name="dn_prep", grid=(N // nc,),
        in_specs=[tok(DN_WIDTH)] * 3 + [tok(128), vec, vec],
        out_specs=[tok(DN_WIDTH)] * 4 + [sq, sq, pl.BlockSpec((nc * 8, 128), lambda n: (n, 0))],
        out_shape=[jax.ShapeDtypeStruct((S, DN_WIDTH), F32)] * 4 + [jax.ShapeDtypeStruct((DN_HEADS, S, C), F32)] * 2
                  + [jax.ShapeDtypeStruct((N * 8, 128), F32)],
        compiler_params=_params(1),
    )(qn, kn, v, bd, avec, dvec)


def _dn_scan_fwd(u, w, qg, kd, aq, egl, gate, dn_gain):
    S = u.shape[0]
    C = DN_CHUNK
    N = S // C
    HD = DN_HEAD_DIM
    nc = SCAN_CHUNKS

    def body(u_ref, w_ref, qg_ref, kd_ref, aq_ref, egl_ref, gate_ref, gain_ref, dn_ref, o_ref, vn_ref, st_ref, state_ref):
        @pl.when(pl.program_id(0) == 0)
        def _():
            state_ref[...] = jnp.zeros_like(state_ref)

        gain = gain_ref[...]
        for ci in range(nc):
            rows = slice(ci * C, (ci + 1) * C)
            st = state_ref[...]
            for h in range(DN_HEADS):
                st_ref[ci * DN_WIDTH + h * HD:ci * DN_WIDTH + (h + 1) * HD, :] = st[h]
            v_new = _stack_heads(u_ref, rows) - _bnn(_stack_heads(w_ref, rows), st)
            o = _bnn(_stack_heads(qg_ref, rows), st) + _bnn(aq_ref[:, rows, :], v_new)
            egl = jnp.stack([egl_ref[ci * 8 + h:ci * 8 + h + 1, :] for h in range(DN_HEADS)])
            state_ref[...] = st * egl + _btn(_stack_heads(kd_ref, rows), v_new)
            r = lax.rsqrt(jnp.mean(o * o, axis=-1, keepdims=True) + NORM_EPS)
            gt = _stack_heads(gate_ref, rows)
            dn = o * r * gain * (gt * _sigmoid(gt))
            for h in range(DN_HEADS):
                sl = slice(h * HD, (h + 1) * HD)
                vn_ref[rows, sl] = v_new[h]
                o_ref[rows, sl] = o[h]
                dn_ref[rows, sl] = dn[h]

    tok = lambda wd: pl.BlockSpec((nc * C, wd), lambda n: (n, 0))
    sq = pl.BlockSpec((DN_HEADS, nc * C, C), lambda n: (0, n, 0))
    vec = pl.BlockSpec((1, 128), lambda n: (0, 0))
    return pl.pallas_call(
        body, name="dn_scan_fwd", grid=(N // nc,),
        in_specs=[tok(DN_WIDTH)] * 4 + [sq, pl.BlockSpec((nc * 8, 128), lambda n: (n, 0)), tok(DN_WIDTH), vec],
        out_specs=[tok(DN_WIDTH)] * 3 + [pl.BlockSpec((nc * DN_WIDTH, HD), lambda n: (n, 0))],
        out_shape=[jax.ShapeDtypeStruct((S, DN_WIDTH), F32)] * 3 + [jax.ShapeDtypeStruct((N * DN_WIDTH, HD), F32)],
        scratch_shapes=[pltpu.VMEM((DN_HEADS, HD, HD), F32)],
        compiler_params=_params(1),
    )(u, w, qg, kd, aq, egl, gate, dn_gain)


def _dn_scan_bwd(w, qg, kd, aq, egl, gate, dn_gain, o, ddn):
    S = w.shape[0]
    C = DN_CHUNK
    N = S // C
    HD = DN_HEAD_DIM
    nc = SCAN_CHUNKS

    def body(w_ref, qg_ref, kd_ref, aq_ref, egl_ref, gate_ref, gain_ref, o_ref, ddn_ref,
             do_ref, dvn_ref, dgate_ref, dst_ref, small_ref, dstate_ref):
        @pl.when(pl.program_id(0) == 0)
        def _():
            dstate_ref[...] = jnp.zeros_like(dstate_ref)
            small_ref[...] = jnp.zeros_like(small_ref)

        gain = gain_ref[...]
        d_gain = jnp.zeros((1, 128), F32)
        for ci in reversed(range(nc)):
            rows = slice(ci * C, (ci + 1) * C)
            dsn = dstate_ref[...]
            for h in range(DN_HEADS):
                dst_ref[ci * DN_WIDTH + h * HD:ci * DN_WIDTH + (h + 1) * HD, :] = dsn[h]
            ov = _stack_heads(o_ref, rows)
            r = lax.rsqrt(jnp.mean(ov * ov, axis=-1, keepdims=True) + NORM_EPS)
            on = ov * r
            gt = _stack_heads(gate_ref, rows)
            sgt = _sigmoid(gt)
            silu_g = gt * sgt
            dy = _stack_heads(ddn_ref, rows)
            d_gain = d_gain + jnp.sum(jnp.sum(dy * on * silu_g, axis=1, keepdims=True), axis=0)
            dgate = dy * on * gain * (sgt * (1.0 + gt * (1.0 - sgt)))
            don = dy * gain * silu_g
            do = r * (don - on * jnp.mean(don * on, axis=-1, keepdims=True))
            d_vnew = _btn(aq_ref[:, rows, :], do) + _bnn(_stack_heads(kd_ref, rows), dsn)
            egl = jnp.stack([egl_ref[ci * 8 + h:ci * 8 + h + 1, :] for h in range(DN_HEADS)])
            dstate_ref[...] = _btn(_stack_heads(qg_ref, rows), do) + dsn * egl - _btn(_stack_heads(w_ref, rows), d_vnew)
            for h in range(DN_HEADS):
                sl = slice(h * HD, (h + 1) * HD)
                do_ref[rows, sl] = do[h]
                dvn_ref[rows, sl] = d_vnew[h]
                dgate_ref[rows, sl] = dgate[h]
        small_ref[...] += jnp.concatenate([d_gain, jnp.zeros((7, 128), F32)], axis=0)

    nb = N // nc
    tok = lambda wd: pl.BlockSpec((nc * C, wd), lambda i: (nb - 1 - i, 0))
    sq = pl.BlockSpec((DN_HEADS, nc * C, C), lambda i: (0, nb - 1 - i, 0))
    vec = pl.BlockSpec((1, 128), lambda i: (0, 0))
    return pl.pallas_call(
        body, name="dn_scan_bwd", grid=(nb,),
        in_specs=[tok(DN_WIDTH)] * 3 + [sq, pl.BlockSpec((nc * 8, 128), lambda i: (nb - 1 - i, 0)), tok(DN_WIDTH), vec,
                                       tok(DN_WIDTH), tok(DN_WIDTH)],
        out_specs=[tok(DN_WIDTH)] * 3 + [pl.BlockSpec((nc * DN_WIDTH, HD), lambda i: (nb - 1 - i, 0)),
                                        pl.BlockSpec((8, 128), lambda i: (0, 0))],
        out_shape=[jax.ShapeDtypeStruct((S, DN_WIDTH), F32)] * 3 + [jax.ShapeDtypeStruct((N * DN_WIDTH, HD), F32),
                                                                  jax.ShapeDtypeStruct((8, 128), F32)],
        scratch_shapes=[pltpu.VMEM((DN_HEADS, HD, HD), F32)],
        compiler_params=_params(1),
    )(w, qg, kd, aq, egl, gate, dn_gain, o, ddn)


def _dn_post(qn, kn, v, bd, avec, dvec, t_inv, v_new_all, states, dstates, do_all, dvn_all, comm=None):
    S = qn.shape[0]
    C = DN_CHUNK
    N = S // C
    HD = DN_HEAD_DIM
    nc = PREP_CHUNKS
    B = nc * DN_HEADS

    def body(q_ref, k_ref, v_ref, bd_ref, a_ref, d_ref, t_ref, vn_ref, st_ref, dst_ref, do_ref, dvn_ref,
             dq_ref, dk_ref, dv_ref, dbd_ref, small_ref):
        @pl.when(pl.program_id(0) == 0)
        def _():
            small_ref[...] = jnp.zeros_like(small_ref)

        avec = a_ref[...]
        bds = [bd_ref[ci * C:(ci + 1) * C, :] for ci in range(nc)]
        k = _stack_units(k_ref, nc)
        vv = _stack_units(v_ref, nc)
        t = jnp.concatenate([t_ref[:, ci * C:(ci + 1) * C, :] for ci in range(nc)], axis=0)
        c = _dn_common_b(bds, avec, d_ref[...], _stack_units(q_ref, nc), k, vv, t=t)
        q, kb, eg, u, w = c["q"], c["kb"], c["eg"], c["u"], c["w"]
        beta, decay, incl, strict, eye = c["beta"], c["decay"], c["incl"], c["strict"], c["eye"]
        st = jnp.stack([st_ref[b * HD:(b + 1) * HD, :] for b in range(B)])
        dsn = jnp.stack([dst_ref[b * HD:(b + 1) * HD, :] for b in range(B)])
        v_new = _stack_units(vn_ref, nc)
        do = _stack_units(do_ref, nc)
        d_vnew = _stack_units(dvn_ref, nc)
        egl = jnp.exp(c["g_last"])
        daq = jnp.where(incl, _bnt(do, v_new), 0.0)
        d_qg = _bnt(do, st)
        d_kd = _bnt(v_new, dsn)
        d_glast = jnp.sum(jnp.sum(dsn * st, axis=-1, keepdims=True), axis=1, keepdims=True) * egl
        d_w = -_bnt(d_vnew, st)
        d_ru = _btn(t, d_vnew)
        d_rw = _btn(t, d_w)
        da = -jnp.where(strict, _bnt(d_ru, u) + _bnt(d_rw, w), 0.0)
        dv = d_ru * beta
        dbeta = jnp.sum(d_ru * vv, axis=-1, keepdims=True)
        dkb = d_rw * eg
        dgc = jnp.sum(d_rw * c["rhs_w"], axis=-1, keepdims=True)
        dkk = da * decay
        ddecay = da * c["kk"]
        dkb = dkb + _bnn(dkk, k)
        dk = _btn(dkk, kb)
        dqk = daq * decay
        ddecay = ddecay + daq * c["qk"]
        dq = _bnn(dqk, k)
        dk = dk + _btn(dqk, q)
        m = ddecay * decay
        col_sum = jnp.sum(m, axis=1, keepdims=True)
        dgc = dgc + jnp.sum(m, axis=-1, keepdims=True) - jnp.sum(eye * col_sum, axis=-1, keepdims=True)
        dq = dq + d_qg * eg
        dgc = dgc + jnp.sum(d_qg * c["qg"], axis=-1, keepdims=True)
        dk = dk + d_kd * c["ekd"]
        tk = jnp.sum(d_kd * c["kd"], axis=-1, keepdims=True)
        dgc = dgc - tk
        d_glast = d_glast + jnp.sum(tk, axis=1, keepdims=True)
        dk = dk + dkb * beta
        dbeta = dbeta + jnp.sum(dkb * k, axis=-1, keepdims=True)
        dgc = dgc + jnp.where(c["last"], d_glast, 0.0)
        dgc_row = jnp.sum(eye * dgc, axis=1, keepdims=True)
        dgraw = jnp.sum(jnp.where(c["col"] >= c["row"], dgc_row, 0.0), axis=-1, keepdims=True)
        _store_units(dq_ref, dq * (HD ** -0.5), nc)
        _store_units(dk_ref, dk, nc)
        _store_units(dv_ref, dv, nc)
        dbraw = dbeta * beta * (1.0 - beta)
        dzc = dgraw * _sigmoid(c["zc"])
        ga = dgraw * c["graw"]
        lane = c["lane"]
        lane1 = lax.broadcasted_iota(jnp.int32, (1, 128), 1)
        neg_ea = -jnp.exp(avec)
        d_alog = jnp.zeros((1, 128), F32)
        d_dt = jnp.zeros((1, 128), F32)
        for ci in range(nc):
            dbd = jnp.zeros((C, 128), F32)
            for h in range(DN_HEADS):
                b = ci * DN_HEADS + h
                dz = dzc[b] * neg_ea
                dbd = dbd + jnp.where(lane == h, dbraw[b], 0.0) + jnp.where(lane == DN_HEADS + h, dz, 0.0)
                d_alog = d_alog + jnp.where(lane1 == DN_HEADS + h, jnp.sum(ga[b], axis=0, keepdims=True), 0.0)
                d_dt = d_dt + jnp.where(lane1 == DN_HEADS + h, jnp.sum(dz, axis=0, keepdims=True), 0.0)
            dbd_ref[ci * C:(ci + 1) * C, :] = dbd
        small_ref[...] += jnp.concatenate([d_alog, d_dt, jnp.zeros((6, 128), F32)], axis=0)

    tok = lambda wd: pl.BlockSpec((nc * C, wd), lambda n: (n, 0))
    big = pl.BlockSpec((nc * DN_WIDTH, HD), lambda n: (n, 0))
    sq = pl.BlockSpec((DN_HEADS, nc * C, C), lambda n: (0, n, 0))
    vec = pl.BlockSpec((1, 128), lambda n: (0, 0))
    return _call(
        body, (qn, kn, v, bd, avec, dvec, t_inv, v_new_all, states, dstates, do_all, dvn_all),
        name="dn_post", grid=(N // nc,), comm=comm,
        in_specs=[tok(DN_WIDTH)] * 3 + [tok(128), vec, vec, sq, tok(DN_WIDTH), big, big, tok(DN_WIDTH), tok(DN_WIDTH)],
        out_specs=[tok(DN_WIDTH)] * 3 + [tok(128), pl.BlockSpec((8, 128), lambda n: (0, 0))],
        out_shape=[jax.ShapeDtypeStruct((S, DN_WIDTH), F32)] * 3 + [jax.ShapeDtypeStruct((S, 128), F32),
                                                                  jax.ShapeDtypeStruct((8, 128), F32)])


def _outproj_fwd(x, attn, dn, w_out):
    S, D = x.shape
    tm = 512

    def body(x_ref, a_ref, d_ref, w_ref, xo_ref, mix_ref):
        a = a_ref[...].astype(BF16)
        dd = d_ref[...].astype(BF16)
        mix_ref[:, 0:ATTN_WIDTH] = a
        mix_ref[:, ATTN_WIDTH:] = dd
        xo_ref[...] = x_ref[...] + _nn(a, w_ref[0:ATTN_WIDTH, :]) + _nn(dd, w_ref[ATTN_WIDTH:, :])

    tok = lambda w: pl.BlockSpec((tm, w), lambda i: (i, 0))
    return pl.pallas_call(
        body, name="outproj_fwd", grid=(S // tm,),
        in_specs=[tok(D), tok(ATTN_WIDTH), tok(DN_WIDTH), pl.BlockSpec((D, D), lambda i: (0, 0))],
        out_specs=[tok(D), tok(D)],
        out_shape=[jax.ShapeDtypeStruct((S, D), F32), jax.ShapeDtypeStruct((S, D), BF16)],
        compiler_params=_params(1),
    )(x, attn, dn, w_out)


def _outproj_bwd(dx, w_out, attn):
    S, D = dx.shape
    tm = VIEW_TILE

    def body(dx_ref, w_ref, attn_ref, da1, da4, da16, dl1, dl4, dl16, ddn_ref, dxb_ref, planes):
        d = dx_ref[...].astype(BF16)
        dxb_ref[...] = d
        da = _nt(d, w_ref[0:ATTN_WIDTH, :])
        ddn_ref[...] = _nt(d, w_ref[ATTN_WIDTH:, :])
        _tile_to_views(da, planes, (da1, da4, da16))
        lo = lax.broadcasted_iota(jnp.int32, (tm, 128), 1) < 64
        cols = []
        for G in range(4):
            sl = slice(G * 128, (G + 1) * 128)
            t = da[:, sl] * attn_ref[:, sl]
            d0 = jnp.sum(jnp.where(lo, t, 0.0), axis=-1, keepdims=True)
            d1 = jnp.sum(jnp.where(lo, 0.0, t), axis=-1, keepdims=True)
            cols.append(jnp.where(lo, d0, d1))
        _tile_to_views(jnp.concatenate(cols, axis=1), planes, (dl1, dl4, dl16))

    tok = lambda w: pl.BlockSpec((tm, w), lambda i: (i, 0))
    views = [_view_spec(d) for d in DILATIONS]
    return pl.pallas_call(
        body, name="outproj_bwd", grid=(S // tm,),
        in_specs=[tok(D), pl.BlockSpec((D, D), lambda i: (0, 0)), tok(ATTN_WIDTH)],
        out_specs=views + views + [tok(DN_WIDTH), tok(D)],
        out_shape=[_view_shape(S, d, F32) for d in DILATIONS] * 2
                  + [jax.ShapeDtypeStruct((S, DN_WIDTH), F32), jax.ShapeDtypeStruct((S, D), BF16)],
        scratch_shapes=[pltpu.VMEM((4, tm, 128), F32)],
        compiler_params=_params(1),
    )(dx, w_out, attn)


def _loss_head(x, gain, target):
    S, D = x.shape
    tm = 512

    def body(x_ref, gain_ref, t_ref, loss_ref, dx_ref, dgain_ref):
        @pl.when(pl.program_id(0) == 0)
        def _():
            loss_ref[...] = jnp.zeros_like(loss_ref)
            dgain_ref[...] = jnp.zeros_like(dgain_ref)

        xf = x_ref[...]
        gain = gain_ref[...]
        r = lax.rsqrt(jnp.mean(xf * xf, axis=-1, keepdims=True) + NORM_EPS)
        xhat = xf * r
        err = xhat * gain - t_ref[...]
        part = 0.5 * jnp.sum(jnp.mean(err * err, axis=-1, keepdims=True), axis=0, keepdims=True)
        first = (lax.broadcasted_iota(jnp.int32, (8, 128), 0) == 0) & (lax.broadcasted_iota(jnp.int32, (8, 128), 1) == 0)
        loss_ref[...] += jnp.where(first, part, 0.0)
        dy = err * (1.0 / D)
        dgain_ref[...] += jnp.sum(dy * xhat, axis=0, keepdims=True)
        dxh = dy * gain
        dx_ref[...] = r * (dxh - xhat * jnp.mean(dxh * xhat, axis=-1, keepdims=True))

    tok = pl.BlockSpec((tm, D), lambda i: (i, 0))
    row = pl.BlockSpec((1, D), lambda i: (0, 0))
    return pl.pallas_call(
        body, name="loss_head", grid=(S // tm,),
        in_specs=[tok, row, tok],
        out_specs=[pl.BlockSpec((8, 128), lambda i: (0, 0)), tok, row],
        out_shape=[jax.ShapeDtypeStruct((8, 128), F32), jax.ShapeDtypeStruct((S, D), F32),
                   jax.ShapeDtypeStruct((1, D), F32)],
        compiler_params=_params(1),
    )(x, gain, target)


def _adamw(w, g, m, v, name):
    R, Ccols = w.shape[0], w.shape[-1]
    tr = next((t for t in (256, 128, 64, 32, 16, 8) if R % t == 0), R)
    c1 = 1.0 - ADAM_B1 ** ADAM_STEP
    c2 = 1.0 - ADAM_B2 ** ADAM_STEP

    def body(w_ref, g_ref, m_ref, v_ref, d_ref, nm_ref, nv_ref):
        gv = g_ref[...]
        mn = ADAM_B1 * m_ref[...] + (1.0 - ADAM_B1) * gv
        vn = ADAM_B2 * v_ref[...] + (1.0 - ADAM_B2) * (gv * gv)
        nm_ref[...] = mn
        nv_ref[...] = vn
        d_ref[...] = -ADAM_LR * ((mn / c1) / (jnp.sqrt(vn / c2) + ADAM_EPS) + ADAM_WD * w_ref[...])

    if w.ndim == 2:
        grid, spec = (R // tr,), pl.BlockSpec((tr, Ccols), lambda i: (i, 0))
    else:
        grid, spec = (2,), pl.BlockSpec((R // 2, 1, Ccols), lambda i: (i, 0, 0))
    return pl.pallas_call(
        body, name=name, grid=grid, in_specs=[spec] * 4, out_specs=[spec] * 3,
        out_shape=[jax.ShapeDtypeStruct(w.shape, F32)] * 3, compiler_params=_params(1),
    )(w, g, m, v)


LATE_WEIGHTS = ("w_out", "ffn2_gate", "ffn2_up", "ffn2_down")


def _local_step(x, target, wts, small, dist=None):
    g1, g2, gm, gf = small["norm_ffn1"], small["norm_ffn2"], small["norm_mix"], small["norm_final"]
    wts = dict(wts)

    def reduce_start(gs, tag):
        return _rs_add_pairs(gs, _swap_sibling(gs, True, "rs_swap_halves_" + tag), dist["c"], "rs_add_pairs_" + tag)

    (x1, h1, fg1, fu1), late = _ffn_fwd(x, g1, wts["ffn1_gate"], wts["ffn1_up"], wts["ffn1_down"], "ffn1_fwd",
                                        comm=_ag_comm(dist["late"]) if dist else None)
    if dist:
        wts.update(zip(LATE_WEIGHTS, late))
        wts["w_out"] = wts["w_out"].reshape(D_MODEL, D_MODEL)
    h2, *qkv, xq, xk, xv, gate, bd = _inproj_fwd(x1, gm, wts["w_in"])
    aq, ak, av = qkv[0:3], qkv[3:6], qkv[6:9]
    parts = [_attn_fwd(aq[p], ak[p], av[p], d, f"attn_fwd_d{d}") for p, d in enumerate(DILATIONS)]
    attn, *lse = _attn_merge(parts)
    conv_w = small["conv_w"]
    qn, kn, vv = _conv_fwd(xq, xk, xv, conv_w)
    dn_u, dn_w, dn_qg, dn_kd, dn_aq, dn_t, dn_egl = _dn_prep(qn, kn, vv, bd, small["avec"], small["dvec"])
    dn, o_dn, v_new, states = _dn_scan_fwd(dn_u, dn_w, dn_qg, dn_kd, dn_aq, dn_egl, gate, small["dn_norm"])
    x2, mix = _outproj_fwd(x1, attn, dn, wts["w_out"])
    (x3, h3, fg2, fu2), _ = _ffn_fwd(x2, g2, wts["ffn2_gate"], wts["ffn2_up"], wts["ffn2_down"], "ffn2_fwd")
    loss, dx3, d_gf = _loss_head(x3, gf, target)

    grads = {}
    (dx2, d_g2, dfg2, dfu2, act2, dout2), _ = _ffn_bwd(dx3, x2, g2, fg2, fu2, wts["ffn2_down"], wts["ffn2_gate"],
                                                      wts["ffn2_up"], "ffn2_bwd")
    tk = 2048
    grads["ffn2_gate"], _ = _dw_chunks(dfg2, h3, tk, "dw_ffn2_gate")
    grads["ffn2_up"], _ = _dw_chunks(dfu2, h3, tk, "dw_ffn2_up")
    grads["ffn2_down"], _ = _dw_chunks(act2, dout2, tk, "dw_ffn2_down")
    group_a = ("ffn2_gate", "ffn2_up", "ffn2_down")
    parts_a = reduce_start([grads[n] for n in group_a], "a") if dist else None

    *dviews, ddn, dx2b = _outproj_bwd(dx2, wts["w_out"], attn)
    dattn, dd = dviews[0:3], dviews[3:6]
    grads["w_out"] = _matmul_tn(mix, dx2b, D_MODEL, tk, "dw_out").reshape(N_CHIPS, D_MODEL // N_CHIPS, D_MODEL)

    daq, dak, dav = [], [], []
    for p, d in enumerate(DILATIONS):
        daq.append(_attn_bwd_q(aq[p], ak[p], av[p], dattn[p], lse[p], dd[p], d, f"attn_bwd_q_d{d}"))
        dk_p, dv_p = _attn_bwd_kv(aq[p], ak[p], av[p], dattn[p], lse[p], dd[p], d, f"attn_bwd_kv_d{d}")
        dak.append(dk_p)
        dav.append(dv_p)

    do_dn, dvn, dgate, dstates, d_dn_gain = _dn_scan_bwd(dn_w, dn_qg, dn_kd, dn_aq, dn_egl, gate, small["dn_norm"], o_dn, ddn)
    (dqn, dkn, dvv, dbd, dn_small), recv_a = _dn_post(qn, kn, vv, bd, small["avec"], small["dvec"], dn_t, v_new, states,
                                                      dstates, do_dn, dvn, comm=_rsx_comm(parts_a) if dist else None)
    dcq, dck, dcv, dwq, dwk, dwv = _conv_bwd_pre(xq, xk, xv, conv_w, dqn, dkn, dvv)
    dxq, dxk, dxv = _conv_bwd_x(dcq, dck, dcv, conv_w)
    d_conv = jnp.concatenate([dwq[:CONV_WIDTH], dwk[:CONV_WIDTH], dwv[:CONV_WIDTH]], axis=1)

    dx1, d_gm, dproj = _inproj_bwd(dx2, x1, gm, [daq, dak, dav], [dxq, dxk, dxv, dgate], dbd, wts["w_in"])
    gi = _matmul_tn(dproj, h2, IN_COLS_PADDED, 512, "dw_in")
    if dist:
        gate_end = QKV_COLS + DN_WIDTH
        gi = jnp.concatenate([gi[:QKV_COLS], gi[gate_end:gate_end + LOGIT_COLS], gi[QKV_COLS:gate_end]], axis=0)
        gi = gi.reshape(N_CHIPS, IN_COLS // N_CHIPS, D_MODEL)
        gi = jnp.pad(gi, ((0, 0), (0, W_IN_ROWS - IN_COLS // N_CHIPS), (0, 0)))
    grads["w_in"] = gi
    group_b = ("w_in", "w_out")
    parts_b = reduce_start([grads[n] for n in group_b], "b") if dist else None

    (dx0, d_g1, dfg1, dfu1, act1, dout1), _ = _ffn_bwd(dx1, x, g1, fg1, fu1, wts["ffn1_down"], wts["ffn1_gate"],
                                                      wts["ffn1_up"], "ffn1_bwd")
    group_c = ("ffn1_gate", "ffn1_up", "ffn1_down")
    pending = parts_b if dist else []
    parts_c, recv_bc = [], []
    for n, (lhs, rhs) in zip(group_c, ((dfg1, h1), (dfu1, h1), (act1, dout1))):
        grads[n], landed = _dw_chunks(lhs, rhs, tk, "dw_" + n, comm=_rsx_comm(pending) if dist else None)
        recv_bc += list(landed)
        if dist:
            pending = reduce_start([grads[n]], n)
            parts_c += pending

    small_grads = dict(norm_ffn1=d_g1, norm_mix=d_gm, norm_ffn2=d_g2, norm_final=d_gf, conv_w=d_conv,
                       a_log=dn_small[0:1], dt_bias=dn_small[1:2], dn_norm=d_dn_gain[0:1])
    if dist:
        recv_bc += _rs_exchange_arrays(pending)
        recv_b, recv_c = recv_bc[:len(parts_b)], recv_bc[len(parts_b):]
        names = group_a + group_b + group_c
        totals = _rs_add_totals(list(parts_a) + list(parts_b) + list(parts_c), list(recv_a) + list(recv_b) + list(recv_c),
                                dist["chip"])
        theirs = _swap_sibling(totals, False, "rs_share_total")
        grads = {n: (mine, other) for n, mine, other in zip(names, totals, theirs)}
    return loss, dx0, grads, small_grads


HBM =pl.BlockSpec(memory_space=pl.ANY)
VMEM_SPEC = pl.BlockSpec(memory_space=pltpu.VMEM)


def _coords():
    return lax.axis_index("x"), lax.axis_index("y"), lax.axis_index("c")


def _remote(src, dst, send_sems, recv_sems, k, dev):
    return pltpu.make_async_remote_copy(src_ref=src, dst_ref=dst, send_sem=send_sems.at[k], recv_sem=recv_sems.at[k],
                                        device_id=dev, device_id_type=MESH)


def _allreduce_small(buf, name):
    R, Cc = buf.shape

    def body(src_ref, out_ref, recv_ref, send_sems, recv_sems):
        x, y, c = _coords()
        copies = []
        for m in range(1, 8):
            fx, fy, fc = (m >> 2) & 1, (m >> 1) & 1, m & 1
            dev = (x ^ fx if fx else x, y ^ fy if fy else y, c ^ fc if fc else c)
            cp = _remote(src_ref, recv_ref.at[m - 1], send_sems, recv_sems, m - 1, dev)
            cp.start()
            copies.append(cp)
        for cp in copies:
            cp.wait()
        r = [src_ref[...]] + [recv_ref[m] for m in range(7)]
        out_ref[...] = ((r[0] + r[1]) + (r[2] + r[3])) + ((r[4] + r[5]) + (r[6] + r[7]))

    return pl.pallas_call(
        body, name=name, out_shape=jax.ShapeDtypeStruct((R, Cc), F32),
        in_specs=[VMEM_SPEC], out_specs=VMEM_SPEC,
        scratch_shapes=[pltpu.VMEM((7, R, Cc), F32), pltpu.SemaphoreType.DMA((7,)), pltpu.SemaphoreType.DMA((7,))],
    )(buf)


BIG = ("ffn1_gate", "ffn1_up", "ffn1_down", "w_in", "w_out", "ffn2_gate", "ffn2_up", "ffn2_down")
ROW_SHARDED = ("ffn1_down", "w_out", "ffn2_down")
W_IN_ROWS = 960


def _rows(ref, start, size):
    return ref.at[pl.ds(pl.multiple_of(start, 16), size)]


def _allgather_arrays(shards):
    n = len(shards)

    def body(*refs):
        srcs, outs, send_sems, recv_sems = refs[:n], refs[n:2 * n], refs[2 * n], refs[2 * n + 1]
        x, y, c = _coords()
        sib = (x, y, 1 - c)
        xn, yn, dg = (1 - x, y), (x, 1 - y), (1 - x, 1 - y)
        slot = lambda out, chip: out.at[2 * chip[0] + chip[1]]
        started = []

        def go(cp):
            cp.start()
            started.append(cp)

        for a, (src, out) in enumerate(zip(srcs, outs)):
            h = src.shape[0] // 2
            cp = lambda s, d, k, dev: _remote(s, d, send_sems, recv_sems, 8 * a + k, dev)
            go(cp(src, slot(out, (x, y)), 6, sib))
            mine, dst = _rows(src, c * h, h), _rows(slot(out, (x, y)), c * h, h)
            go(cp(mine, dst, 0, (*xn, c)))
            go(cp(mine, dst, 1, (*yn, c)))
        for a, (src, out) in enumerate(zip(srcs, outs)):
            h = src.shape[0] // 2
            q = h // 2
            cp = lambda s, d, k, dev: _remote(s, d, send_sems, recv_sems, 8 * a + k, dev)
            from_x, from_y = _rows(slot(out, xn), c * h, h), _rows(slot(out, yn), c * h, h)
            cp(from_x, from_x, 0, sib).wait_recv()
            first = _rows(slot(out, xn), c * h, q)
            go(cp(first, first, 2, (*yn, c)))
            go(cp(from_x, from_x, 3, sib))
            cp(from_y, from_y, 1, sib).wait_recv()
            second = _rows(slot(out, yn), c * h + q, q)
            go(cp(second, second, 7, (*xn, c)))
            go(cp(from_y, from_y, 4, sib))
        for a, (src, out) in enumerate(zip(srcs, outs)):
            h = src.shape[0] // 2
            q = h // 2
            cp = lambda s, d, k, dev: _remote(s, d, send_sems, recv_sems, 8 * a + k, dev)
            first, second = _rows(slot(out, dg), c * h, q), _rows(slot(out, dg), c * h + q, q)
            cp(first, first, 2, sib).wait_recv()
            cp(second, second, 7, sib).wait_recv()
            from_d = _rows(slot(out, dg), c * h, h)
            go(cp(from_d, from_d, 5, sib))
        for a, (src, out) in enumerate(zip(srcs, outs)):
            h = src.shape[0] // 2
            cp = lambda s, d, k, dev: _remote(s, d, send_sems, recv_sems, 8 * a + k, dev)
            for k, chip in ((3, xn), (4, yn), (5, dg)):
                theirs = _rows(slot(out, chip), (1 - c) * h, h)
                cp(theirs, theirs, k, sib).wait_recv()
            cp(src, slot(out, (x, y)), 6, sib).wait_recv()
        for cp in started:
            cp.wait_send()

    shapes = [jax.ShapeDtypeStruct((N_CHIPS,) + s.shape, s.dtype) for s in shards]
    return pl.pallas_call(
        body, name="allgather_weights", out_shape=shapes, in_specs=[HBM] * n, out_specs=[HBM] * n,
        scratch_shapes=[pltpu.SemaphoreType.DMA((8 * n,)), pltpu.SemaphoreType.DMA((8 * n,))],
    )(*shards)


def _ag_copies(srcs, outs, send_sems, recv_sems):
    x, y, c = _coords()
    sib = (x, y, 1 - c)
    me = 2 * x + y
    others = [(1 - x, y), (x, 1 - y), (1 - x, 1 - y)]
    plan = []
    for a, (src, out) in enumerate(zip(srcs, outs)):
        h = src.shape[0] // 2
        cp = lambda s, d, k, dev: _remote(s, d, send_sems, recv_sems, 7 * a + k, dev)
        own = cp(src, out.at[me], 6, sib)
        sends = [cp(_rows(src, c * h, h), _rows(out.at[me], c * h, h), j, (ox, oy, c)) for j, (ox, oy) in enumerate(others)]
        mine = [_rows(out.at[2 * ox + oy], c * h, h) for ox, oy in others]
        theirs = [_rows(out.at[2 * ox + oy], (1 - c) * h, h) for ox, oy in others]
        arrivals = [cp(m, m, j, sib) for j, m in enumerate(mine)]
        forwards = [cp(m, m, 3 + j, sib) for j, m in enumerate(mine)]
        forwarded = [cp(t, t, 3 + j, sib) for j, t in enumerate(theirs)]
        plan.append((own, sends, forwards, arrivals, forwarded))
    return plan


def _ag_start(srcs, outs, send_sems, recv_sems):
    for own, sends, _, _, _ in _ag_copies(srcs, outs, send_sems, recv_sems):
        own.start()
        for cp in sends:
            cp.start()


def _ag_finish(srcs, outs, send_sems, recv_sems):
    plan = _ag_copies(srcs, outs, send_sems, recv_sems)
    for _, _, forwards, arrivals, _ in plan:
        for arrived, fwd in zip(arrivals, forwards):
            arrived.wait_recv()
            fwd.start()
    for own, sends, forwards, _, forwarded in plan:
        for cp in forwarded:
            cp.wait_recv()
        own.wait_recv()
        for cp in [own] + sends + forwards:
            cp.wait_send()


def _ag_comm(shards):
    shapes = [jax.ShapeDtypeStruct((N_CHIPS,) + s.shape, s.dtype) for s in shards]
    return (list(shards), shapes, 7 * len(shards), _ag_start, _ag_finish)


def _swap_sibling(arrs, pick_other_half, name):
    n = len(arrs)
    outs = [jax.ShapeDtypeStruct((a.shape[0], a.shape[1] // 2) + a.shape[2:] if pick_other_half else a.shape, a.dtype) for a in arrs]

    def body(*refs):
        srcs, dsts, send_sems, recv_sems = refs[:n], refs[n:2 * n], refs[2 * n], refs[2 * n + 1]
        x, y, c = _coords()
        cps = []
        for a in range(n):
            src = srcs[a]
            if pick_other_half:
                h = src.shape[1] // 2
                src = src.at[:, pl.ds(pl.multiple_of((1 - c) * h, 16), h)]
            cp = _remote(src, dsts[a], send_sems, recv_sems, a, (x, y, 1 - c))
            cp.start()
            cps.append(cp)
        for cp in cps:
            cp.wait()

    return pl.pallas_call(
        body, name=name, out_shape=outs, in_specs=[HBM] * n, out_specs=[HBM] * n,
        scratch_shapes=[pltpu.SemaphoreType.DMA((n,)), pltpu.SemaphoreType.DMA((n,))],
    )(*arrs)


def _rs_add_pairs(gs, others, c, name):
    n = len(gs)
    blocks = [(g.shape[1] // 4, g.shape[2]) for g in gs]

    def body(c_ref, *refs):
        for a in range(n):
            refs[2 * n + a][...] = (refs[a][...] + refs[n + a][...]).astype(BF16)

    mine = lambda b: pl.BlockSpec((None,) + b, lambda j, s, c_ref: (j, c_ref[0] * 2 + s, 0))
    flat = lambda b: pl.BlockSpec((None,) + b, lambda j, s, c_ref: (j, s, 0))
    return pl.pallas_call(
        body, name=name,
        grid_spec=pltpu.PrefetchScalarGridSpec(
            num_scalar_prefetch=1, grid=(N_CHIPS, 2),
            in_specs=[mine(b) for b in blocks] + [flat(b) for b in blocks],
            out_specs=[flat(b) for b in blocks]),
        out_shape=[jax.ShapeDtypeStruct(o.shape, BF16) for o in others],
        compiler_params=_params(2),
    )(c, *gs, *others)


def _rs_exchange_arrays(parts):
    n = len(parts)

    def body(*refs):
        _rsx_start(refs[:n], refs[n:2 * n], refs[2 * n], refs[2 * n + 1])
        _rsx_finish(refs[:n], refs[n:2 * n], refs[2 * n], refs[2 * n + 1])

    _, shapes, n_sems, _, _ = _rsx_comm(parts)
    return pl.pallas_call(
        body, name="rs_exchange_chips", out_shape=shapes, in_specs=[HBM] * n, out_specs=[HBM] * n,
        scratch_shapes=[pltpu.SemaphoreType.DMA((n_sems,)), pltpu.SemaphoreType.DMA((n_sems,))],
    )(*parts)


def _rsx_copies(srcs, dsts, send_sems, recv_sems):
    x, y, c = _coords()
    others = [(1 - x, y), (x, 1 - y), (1 - x, 1 - y)]
    return [_remote(src.at[2 * ox + oy], dst.at[k], send_sems, recv_sems, 3 * a + k, (ox, oy, c))
            for a, (src, dst) in enumerate(zip(srcs, dsts)) for k, (ox, oy) in enumerate(others)]


def _rsx_start(srcs, dsts, send_sems, recv_sems):
    for cp in _rsx_copies(srcs, dsts, send_sems, recv_sems):
        cp.start()


def _rsx_finish(srcs, dsts, send_sems, recv_sems):
    for cp in _rsx_copies(srcs, dsts, send_sems, recv_sems):
        cp.wait()


def _rsx_comm(parts):
    shapes = [jax.ShapeDtypeStruct((3,) + p.shape[1:], p.dtype) for p in parts]
    return (list(parts), shapes, 3 * len(parts), _rsx_start, _rsx_finish)


def _rs_add_totals(parts, recvs, chip):
    n = len(parts)
    blocks = [(p.shape[1] // 2, p.shape[2]) for p in parts]

    def body(chip_ref, *refs):
        f = lambda r: r[...].astype(F32)
        for a in range(n):
            p, r0, r1, r2 = refs[a], refs[n + 3 * a], refs[n + 3 * a + 1], refs[n + 3 * a + 2]
            refs[4 * n + a][...] = (f(p) + f(r0)) + (f(r1) + f(r2))

    own = lambda b: pl.BlockSpec((None,) + b, lambda s, chip_ref: (chip_ref[0], s, 0))
    slot = lambda b, k: pl.BlockSpec((None,) + b, lambda s, chip_ref, k=k: (k, s, 0))
    recv_specs = [slot(b, k) for b in blocks for k in range(3)]
    recv_args = [r for r in recvs for _ in range(3)]
    return pl.pallas_call(
        body, name="rs_add_totals",
        grid_spec=pltpu.PrefetchScalarGridSpec(
            num_scalar_prefetch=1, grid=(2,),
            in_specs=[own(b) for b in blocks] + recv_specs,
            out_specs=[pl.BlockSpec(b, lambda s, chip_ref: (s, 0)) for b in blocks]),
        out_shape=[jax.ShapeDtypeStruct(p.shape[1:], F32) for p in parts],
        compiler_params=_params(1),
    )(chip, *parts, *recv_args)


def _permute_w_in(wt):
    return jnp.concatenate([wt[:QKV_COLS], wt[QKV_COLS + LOGIT_COLS:IN_COLS], wt[QKV_COLS:QKV_COLS + LOGIT_COLS],
                            jnp.zeros((IN_COLS_PADDED - IN_COLS, wt.shape[1]), wt.dtype)], axis=0)


def _pad_row(v):
    v = v.reshape(1, -1)
    return jnp.pad(v, ((0, 0), (0, D_MODEL - v.shape[1])))


def kernel(x, norm_ffn1, ffn1_gate, ffn1_up, ffn1_down, norm_mix, w_in, conv_w, a_log, dt_bias, dn_norm, w_out, norm_ffn2, ffn2_gate, ffn2_up, ffn2_down, norm_final, loss_target, m_norm_ffn1, m_ffn1_gate, m_ffn1_up, m_ffn1_down, m_norm_mix, m_w_in, m_conv_w, m_a_log, m_dt_bias, m_dn_norm, m_w_out, m_norm_ffn2, m_ffn2_gate, m_ffn2_up, m_ffn2_down, m_norm_final, v_norm_ffn1, v_ffn1_gate, v_ffn1_up, v_ffn1_down, v_norm_mix, v_w_in, v_conv_w, v_a_log, v_dt_bias, v_dn_norm, v_w_out, v_norm_ffn2, v_ffn2_gate, v_ffn2_up, v_ffn2_down, v_norm_final):
    cx, cy, cc = _coords()
    chip = 2 * cx + cy
    stored = lambda t, n: t[0] if n in ROW_SHARDED else t[0].T
    big_w = {n: stored(t, n) for n, t in dict(
        ffn1_gate=ffn1_gate, ffn1_up=ffn1_up, ffn1_down=ffn1_down, w_in=w_in, w_out=w_out,
        ffn2_gate=ffn2_gate, ffn2_up=ffn2_up, ffn2_down=ffn2_down).items()}
    big_m = {n: stored(t, n) for n, t in dict(
        ffn1_gate=m_ffn1_gate, ffn1_up=m_ffn1_up, ffn1_down=m_ffn1_down, w_in=m_w_in, w_out=m_w_out,
        ffn2_gate=m_ffn2_gate, ffn2_up=m_ffn2_up, ffn2_down=m_ffn2_down).items()}
    big_v = {n: stored(t, n) for n, t in dict(
        ffn1_gate=v_ffn1_gate, ffn1_up=v_ffn1_up, ffn1_down=v_ffn1_down, w_in=v_w_in, w_out=v_w_out,
        ffn2_gate=v_ffn2_gate, ffn2_up=v_ffn2_up, ffn2_down=v_ffn2_down).items()}

    cols = IN_COLS // N_CHIPS
    send = {n: big_w[n].astype(BF16) for n in BIG}
    send["w_in"] = jnp.pad(send["w_in"], ((0, W_IN_ROWS - cols), (0, 0)))
    early = tuple(n for n in BIG if n not in LATE_WEIGHTS)
    wts = dict(zip(early, _allgather_arrays([send[n] for n in early])))
    wts["w_in"] = _permute_w_in(wts["w_in"][:, :cols].reshape(IN_COLS, D_MODEL))
    dist = dict(late=[send[n] for n in LATE_WEIGHTS], c=cc.reshape(1).astype(jnp.int32),
                chip=chip.reshape(1).astype(jnp.int32))

    conv_shard = conv_w[0]
    emb = jnp.concatenate([jnp.where((chip == j) & (cc == 0), conv_shard, 0.0) for j in range(N_CHIPS)], axis=1)
    emb = jnp.pad(emb.reshape(6, D_MODEL), ((0, 2), (0, 0)))
    conv_full = _allreduce_small(emb, "allgather_conv_w")[:6].reshape(CONV_WIDTH, 3 * DN_WIDTH)

    zvec = jnp.zeros((1, 128), F32)
    small = dict(norm_ffn1=norm_ffn1, norm_mix=norm_mix, norm_ffn2=norm_ffn2, norm_final=norm_final[None],
                 conv_w=conv_full, avec=zvec.at[0, DN_HEADS:2 * DN_HEADS].set(a_log[0]),
                 dvec=zvec.at[0, DN_HEADS:2 * DN_HEADS].set(dt_bias[0]), dn_norm=dn_norm)

    loss, grad_x, reduced, sg = _local_step(x[0], loss_target[0], wts, small, dist)

    rows = [sg["norm_ffn1"], sg["norm_mix"], sg["norm_ffn2"], sg["norm_final"], _pad_row(sg["a_log"]), _pad_row(sg["dt_bias"]),
            _pad_row(sg["dn_norm"]), _pad_row(loss[0:1]), sg["conv_w"].reshape(6, D_MODEL), jnp.zeros((2, D_MODEL), F32)]
    red = _allreduce_small(jnp.concatenate(rows, axis=0), "allreduce_small")
    loss_out = red[7, 0]
    g_conv_full = red[8:14].reshape(CONV_WIDTH, 3 * DN_WIDTH)
    g_conv = lax.dynamic_slice_in_dim(g_conv_full, chip * (3 * DN_WIDTH // N_CHIPS), 3 * DN_WIDTH // N_CHIPS, axis=1)
    g_small = dict(norm_ffn1=red[0:1], norm_mix=red[1:2], norm_ffn2=red[2:3], norm_final=red[3],
                   a_log=red[4:5, DN_HEADS:2 * DN_HEADS], dt_bias=red[5:6, DN_HEADS:2 * DN_HEADS], dn_norm=red[6:7, :DN_HEAD_DIM])

    out_g, out_d, out_m, out_v = {}, {}, {}, {}
    for n in BIG:
        mine, other = reduced[n]
        g = jnp.where(cc == 0, jnp.concatenate([mine, other], axis=0), jnp.concatenate([other, mine], axis=0))
        if n == "w_in":
            to3 = lambda t: jnp.transpose(t, (2, 0, 1))
            g = g[:cols].reshape(cols, 1, D_MODEL)
            results = (g,) + tuple(_adamw(to3(w_in), g, to3(m_w_in), to3(v_w_in), "adamw_w_in"))
            out_g[n], out_d[n], out_m[n], out_v[n] = (jnp.transpose(t, (1, 2, 0)) for t in results)
            continue
        results = (g,) + tuple(_adamw(big_w[n], g, big_m[n], big_v[n], "adamw_" + n))
        out_g[n], out_d[n], out_m[n], out_v[n] = ((t if n in ROW_SHARDED else t.T)[None] for t in results)
    d, nm, nv = _adamw(conv_w[0], g_conv, m_conv_w[0], v_conv_w[0], "adamw_conv_w")
    out_g["conv_w"], out_d["conv_w"], out_m["conv_w"], out_v["conv_w"] = g_conv[None], d[None], nm[None], nv[None]

    small_names = ("norm_ffn1", "norm_mix", "norm_ffn2", "norm_final", "a_log", "dt_bias", "dn_norm")
    small_w = dict(norm_ffn1=norm_ffn1, norm_mix=norm_mix, norm_ffn2=norm_ffn2, norm_final=norm_final, a_log=a_log,
                   dt_bias=dt_bias, dn_norm=dn_norm)
    small_m = dict(norm_ffn1=m_norm_ffn1, norm_mix=m_norm_mix, norm_ffn2=m_norm_ffn2, norm_final=m_norm_final, a_log=m_a_log,
                   dt_bias=m_dt_bias, dn_norm=m_dn_norm)
    small_v = dict(norm_ffn1=v_norm_ffn1, norm_mix=v_norm_mix, norm_ffn2=v_norm_ffn2, norm_final=v_norm_final, a_log=v_a_log,
                   dt_bias=v_dt_bias, dn_norm=v_dn_norm)
    stack = lambda dct: jnp.concatenate([_pad_row(dct[n]) for n in small_names] + [jnp.zeros((1, D_MODEL), F32)], axis=0)
    d, nm, nv = _adamw(stack(small_w), stack(g_small), stack(small_m), stack(small_v), "adamw_small")
    for k, n in enumerate(small_names):
        shape = small_w[n].shape
        size = math.prod(shape)
        out_g[n] = g_small[n].reshape(shape)
        out_d[n], out_m[n], out_v[n] = (t[k, :size].reshape(shape) for t in (d, nm, nv))

    order = ("norm_ffn1", "ffn1_gate", "ffn1_up", "ffn1_down", "norm_mix", "w_in", "conv_w", "a_log", "dt_bias", "dn_norm",
             "w_out", "norm_ffn2", "ffn2_gate", "ffn2_up", "ffn2_down", "norm_final")
    return (loss_out, grad_x[None], *[out_g[n] for n in order], *[out_d[n] for n in order],
            *[out_m[n] for n in order], *[out_v[n] for n in order])
```

```python
import functools
import math

import jax
import jax.numpy as jnp
from jax import lax
from jax.experimental import pallas as pl
from jax.experimental.pallas import tpu as pltpu

F32 = jnp.float32
BF16 = jnp.bfloat16
HI = lax.Precision.HIGH

D_MODEL = 1024
ATTN_HEADS = 8
ATTN_WIDTH = 512
ATTN_BLOCK = 128
ATTN_SCALE = (ATTN_WIDTH // ATTN_HEADS) ** -0.5
DILATIONS = (1, 4, 16)
DN_HEADS = 4
DN_HEAD_DIM = 128
DN_WIDTH = 512
DN_CHUNK = 64
CONV_WIDTH = 4
NORM_EPS = 1e-6
L2_EPS = 1e-6
QKV_COLS = 3 * ATTN_WIDTH + 3 * DN_WIDTH
LOGIT_COLS = 2 * DN_HEADS
IN_COLS = QKV_COLS + LOGIT_COLS + DN_WIDTH
IN_COLS_PADDED = 3712
N_CHIPS = 4

ADAM_LR = 0.001
ADAM_B1 = 0.9
ADAM_B2 = 0.999
ADAM_EPS = 1e-08
ADAM_WD = 0.01
ADAM_STEP = 10

VMEM_LIMIT = 56 * 1024 * 1024
NEG_BIG = -1e30
MESH = pl.DeviceIdType.MESH


def _params(n_grid, vmem=VMEM_LIMIT):
    return pltpu.CompilerParams(dimension_semantics=("arbitrary",) * n_grid, vmem_limit_bytes=vmem)


def _call(body, args, *, name, grid, in_specs, out_specs, out_shape, scratch_shapes=(), comm=None):
    n_in, n_out, n_scr = len(in_specs), len(out_specs), len(scratch_shapes)
    hbm = pl.BlockSpec(memory_space=pl.ANY)
    srcs, dst_shapes, n_sems, start, finish = comm if comm is not None else ((), (), 0, None, None)
    ns, nd = len(srcs), len(dst_shapes)

    def full(*refs):
        ins, c_src = refs[:n_in], refs[n_in:n_in + ns]
        at = n_in + ns
        outs, c_dst = refs[at:at + n_out], refs[at + n_out:at + n_out + nd]
        scr = refs[at + n_out + nd:at + n_out + nd + n_scr]
        if comm is not None:
            ids = [pl.program_id(a) for a in range(len(grid))]
            first = functools.reduce(jnp.logical_and, [i == 0 for i in ids])
            last = functools.reduce(jnp.logical_and, [i == g - 1 for i, g in zip(ids, grid)])

            @pl.when(first)
            def _():
                start(c_src, c_dst, refs[-2], refs[-1])

        body(*ins, *outs, *scr)
        if comm is not None:
            @pl.when(last)
            def _():
                finish(c_src, c_dst, refs[-2], refs[-1])

    sems = [pltpu.SemaphoreType.DMA((n_sems,)), pltpu.SemaphoreType.DMA((n_sems,))] if comm is not None else []
    res = pl.pallas_call(
        full, name=name, grid=grid, in_specs=list(in_specs) + [hbm] * ns, out_specs=list(out_specs) + [hbm] * nd,
        out_shape=list(out_shape) + list(dst_shapes), scratch_shapes=list(scratch_shapes) + sems,
        compiler_params=_params(len(grid)),
    )(*args, *srcs)
    return res[:n_out], res[n_out:]


def _nt(a, b, precision=None):
    return lax.dot_general(a, b, (((1,), (1,)), ((), ())), preferred_element_type=F32, precision=precision)


def _tn(a, b, precision=None):
    return lax.dot_general(a, b, (((0,), (0,)), ((), ())), preferred_element_type=F32, precision=precision)


def _nn(a, b, precision=None):
    return jnp.dot(a, b, preferred_element_type=F32, precision=precision)


def _sigmoid(x):
    return 1.0 / (1.0 + jnp.exp(-x))


def _ffn_fwd(x, gain, wg, wu, wd, name, comm=None):
    S, D = x.shape
    nf, tf, _ = wg.shape
    tm = 512

    def body(x_ref, gain_ref, wg_ref, wu_ref, wd_ref, xo_ref, h_ref, g_ref, u_ref, acc_ref, hs_ref):
        j = pl.program_id(1)

        @pl.when(j == 0)
        def _():
            xf = x_ref[...]
            r = lax.rsqrt(jnp.mean(xf * xf, axis=-1, keepdims=True) + NORM_EPS)
            h = (xf * r * gain_ref[...]).astype(BF16)
            hs_ref[...] = h
            h_ref[...] = h
            acc_ref[...] = jnp.zeros_like(acc_ref)

        h = hs_ref[...]
        g = _nt(h, wg_ref[...])
        u = _nt(h, wu_ref[...])
        g_ref[...] = g.astype(BF16)
        u_ref[...] = u.astype(BF16)
        act = g * _sigmoid(g) * u
        acc_ref[...] += _nn(act.astype(BF16), wd_ref[...])

        @pl.when(j == nf - 1)
        def _():
            xo_ref[...] = x_ref[...] + 0.5 * acc_ref[...]

    return _call(
        body, (x, gain, wg, wu, wd), name=name, grid=(S // tm, nf), comm=comm,
        in_specs=[pl.BlockSpec((tm, D), lambda i, j: (i, 0)),
                  pl.BlockSpec((1, D), lambda i, j: (0, 0)),
                  pl.BlockSpec((None, tf, D), lambda i, j: (j, 0, 0)),
                  pl.BlockSpec((None, tf, D), lambda i, j: (j, 0, 0)),
                  pl.BlockSpec((None, tf, D), lambda i, j: (j, 0, 0))],
        out_specs=[pl.BlockSpec((tm, D), lambda i, j: (i, 0)),
                   pl.BlockSpec((tm, D), lambda i, j: (i, 0)),
                   pl.BlockSpec((None, tm, tf), lambda i, j: (j, i, 0)),
                   pl.BlockSpec((None, tm, tf), lambda i, j: (j, i, 0))],
        out_shape=[jax.ShapeDtypeStruct((S, D), F32), jax.ShapeDtypeStruct((S, D), BF16),
                   jax.ShapeDtypeStruct((nf, S, tf), BF16), jax.ShapeDtypeStruct((nf, S, tf), BF16)],
        scratch_shapes=[pltpu.VMEM((tm, D), F32), pltpu.VMEM((tm, D), BF16)])


def _rmsnorm_bwd(dh, xf, gain):
    r = lax.rsqrt(jnp.mean(xf * xf, axis=-1, keepdims=True) + NORM_EPS)
    xhat = xf * r
    dgain = jnp.sum(dh * xhat, axis=0, keepdims=True)
    dxh = dh * gain
    dx = r * (dxh - xhat * jnp.mean(dxh * xhat, axis=-1, keepdims=True))
    return dx, dgain


def _ffn_bwd(dxo, x, gain, g, u, wd, wg, wu, name, comm=None):
    S, D = x.shape
    nf, _, tf = g.shape
    tm = 512

    def body(dxo_ref, x_ref, gain_ref, g_ref, u_ref, wd_ref, wg_ref, wu_ref,
             dx_ref, dgain_ref, dg_ref, du_ref, act_ref, dout_ref, acc_ref, ds_ref):
        i = pl.program_id(0)
        j = pl.program_id(1)

        @pl.when(j == 0)
        def _():
            d = (0.5 * dxo_ref[...]).astype(BF16)
            ds_ref[...] = d
            dout_ref[...] = d
            acc_ref[...] = jnp.zeros_like(acc_ref)

        @pl.when((i == 0) & (j == 0))
        def _():
            dgain_ref[...] = jnp.zeros_like(dgain_ref)

        for half in range(2):
            rows = slice(half * (tm // 2), (half + 1) * (tm // 2))
            dact = _nt(ds_ref[rows, :], wd_ref[...])
            gv = g_ref[rows, :].astype(F32)
            uv = u_ref[rows, :].astype(F32)
            sg = _sigmoid(gv)
            silu = gv * sg
            act_ref[rows, :] = (silu * uv).astype(BF16)
            dgv = (dact * uv * (sg * (1.0 + gv * (1.0 - sg)))).astype(BF16)
            duv = (dact * silu).astype(BF16)
            dg_ref[rows, :] = dgv
            du_ref[rows, :] = duv
            acc_ref[rows, :] += _nn(dgv, wg_ref[...]) + _nn(duv, wu_ref[...])

        @pl.when(j == nf - 1)
        def _():
            dx, dgain = _rmsnorm_bwd(acc_ref[...], x_ref[...], gain_ref[...])
            dx_ref[...] = dxo_ref[...] + dx
            dgain_ref[...] += dgain

    return _call(
        body, (dxo, x, gain, g, u, wd, wg, wu), name=name, grid=(S // tm, nf), comm=comm,
        in_specs=[pl.BlockSpec((tm, D), lambda i, j: (i, 0)),
                  pl.BlockSpec((tm, D), lambda i, j: (i, 0)),
                  pl.BlockSpec((1, D), lambda i, j: (0, 0)),
                  pl.BlockSpec((None, tm, tf), lambda i, j: (j, i, 0)),
                  pl.BlockSpec((None, tm, tf), lambda i, j: (j, i, 0)),
                  pl.BlockSpec((None, tf, D), lambda i, j: (j, 0, 0)),
                  pl.BlockSpec((None, tf, D), lambda i, j: (j, 0, 0)),
                  pl.BlockSpec((None, tf, D), lambda i, j: (j, 0, 0))],
        out_specs=[pl.BlockSpec((tm, D), lambda i, j: (i, 0)),
                   pl.BlockSpec((1, D), lambda i, j: (0, 0)),
                   pl.BlockSpec((None, tm, tf), lambda i, j: (j, i, 0)),
                   pl.BlockSpec((None, tm, tf), lambda i, j: (j, i, 0)),
                   pl.BlockSpec((None, tm, tf), lambda i, j: (j, i, 0)),
                   pl.BlockSpec((tm, D), lambda i, j: (i, 0))],
        out_shape=[jax.ShapeDtypeStruct((S, D), F32), jax.ShapeDtypeStruct((1, D), F32),
                   jax.ShapeDtypeStruct((nf, S, tf), BF16), jax.ShapeDtypeStruct((nf, S, tf), BF16),
                   jax.ShapeDtypeStruct((nf, S, tf), BF16), jax.ShapeDtypeStruct((S, D), BF16)],
        scratch_shapes=[pltpu.VMEM((tm, D), F32), pltpu.VMEM((tm, D), BF16)])


def _matmul_tn(a, b, tm, tk, name):
    K, M = a.shape
    N = b.shape[1]

    def body(a_ref, b_ref, o_ref):
        @pl.when(pl.program_id(1) == 0)
        def _():
            o_ref[...] = jnp.zeros_like(o_ref)

        o_ref[...] += _tn(a_ref[...], b_ref[...])

    return pl.pallas_call(
        body, name=name, grid=(M // tm, K // tk),
        in_specs=[pl.BlockSpec((tk, tm), lambda i, k: (k, i)),
                  pl.BlockSpec((tk, N), lambda i, k: (k, 0))],
        out_specs=pl.BlockSpec((tm, N), lambda i, k: (i, 0)),
        out_shape=jax.ShapeDtypeStruct((M, N), F32),
        compiler_params=_params(2),
    )(a, b)


def _dw_chunks(a, b, tk, name, comm=None):
    nf, S, tf = a.shape
    N = b.shape[1]

    def body(a_ref, b_ref, o_ref):
        @pl.when(pl.program_id(1) == 0)
        def _():
            o_ref[...] = jnp.zeros_like(o_ref)

        o_ref[...] += _tn(a_ref[...], b_ref[...])

    (out,), landed = _call(
        body, (a, b), name=name, grid=(nf, S // tk), comm=comm,
        in_specs=[pl.BlockSpec((None, tk, tf), lambda j, k: (j, k, 0)),
                  pl.BlockSpec((tk, N), lambda j, k: (k, 0))],
        out_specs=[pl.BlockSpec((None, tf, N), lambda j, k: (j, 0, 0))],
        out_shape=[jax.ShapeDtypeStruct((nf, tf, N), F32)])
    return out, landed


VIEW_TILE = 512


def _view_spec(d, tile=VIEW_TILE):
    return pl.BlockSpec((tile // d, d * ATTN_WIDTH), lambda i: (i, 0))


def _view_shape(S, d, dtype):
    return jax.ShapeDtypeStruct((S // d, d * ATTN_WIDTH), dtype)


def _tile_to_views(val, planes, out_refs):
    for g in range(4):
        planes[g] = val[:, g * 128:(g + 1) * 128]
    for d, ref in zip(DILATIONS, out_refs):
        if d == 1:
            ref[...] = val.astype(ref.dtype)
            continue
        for r in range(d):
            for g in range(4):
                ref[:, r * ATTN_WIDTH + g * 128:r * ATTN_WIDTH + (g + 1) * 128] = (
                    planes[g, pl.ds(r, planes.shape[1] // d, stride=d), :].astype(ref.dtype))


def _view_to_tile(ref, d, planes):
    if d == 1:
        return ref[...].astype(F32)
    for r in range(d):
        for g in range(4):
            planes[g, pl.ds(r, planes.shape[1] // d, stride=d), :] = (
                ref[:, r * ATTN_WIDTH + g * 128:r * ATTN_WIDTH + (g + 1) * 128].astype(F32))
    return jnp.concatenate([planes[g] for g in range(4)], axis=1)


def _inproj_fwd(x, gain, w_in_p):
    S, D = x.shape
    tm = VIEW_TILE
    W = ATTN_WIDTH

    def body(x_ref, gain_ref, w_ref, h_ref, q1, q4, q16, k1, k4, k16, v1, v4, v16, dq_ref, dk_ref, dv_ref, gate_ref, bd_ref,
             planes):
        xf = x_ref[...]
        r = lax.rsqrt(jnp.mean(xf * xf, axis=-1, keepdims=True) + NORM_EPS)
        h = (xf * r * gain_ref[...]).astype(BF16)
        h_ref[...] = h
        _tile_to_views(_nt(h, w_ref[0:W, :]) * ATTN_SCALE, planes, (q1, q4, q16))
        _tile_to_views(_nt(h, w_ref[W:2 * W, :]), planes, (k1, k4, k16))
        _tile_to_views(_nt(h, w_ref[2 * W:3 * W, :]), planes, (v1, v4, v16))
        dq_ref[...] = _nt(h, w_ref[3 * W:4 * W, :])
        dk_ref[...] = _nt(h, w_ref[4 * W:5 * W, :])
        dv_ref[...] = _nt(h, w_ref[5 * W:6 * W, :])
        gate_ref[...] = _nt(h, w_ref[6 * W:7 * W, :])
        bd_ref[...] = _nt(h, w_ref[7 * W:7 * W + 128, :])

    tok = lambda w: pl.BlockSpec((tm, w), lambda i: (i, 0))
    return pl.pallas_call(
        body, name="inproj_fwd", grid=(S // tm,),
        in_specs=[tok(D), pl.BlockSpec((1, D), lambda i: (0, 0)),
                  pl.BlockSpec((IN_COLS_PADDED, D), lambda i: (0, 0))],
        out_specs=[tok(D)] + [_view_spec(d) for d in DILATIONS] * 3 + [tok(W)] * 4 + [tok(128)],
        out_shape=[jax.ShapeDtypeStruct((S, D), BF16)] + [_view_shape(S, d, BF16) for d in DILATIONS] * 3
                  + [jax.ShapeDtypeStruct((S, W), F32)] * 4 + [jax.ShapeDtypeStruct((S, 128), F32)],
        scratch_shapes=[pltpu.VMEM((4, tm, 128), F32)],
        compiler_params=_params(1),
    )(x, gain, w_in_p)


def _inproj_bwd(dxo, x, gain, attn_grads, dsecs, dbd, w_in_p):
    S, D = x.shape
    tm = VIEW_TILE
    W = ATTN_WIDTH

    def body(dxo_ref, x_ref, gain_ref, *rest):
        views, (s3, s4, s5, s6, dbd_ref, w_ref, dx_ref, dgain_ref, dproj_ref, planes) = rest[:9], rest[9:]

        @pl.when(pl.program_id(0) == 0)
        def _():
            dgain_ref[...] = jnp.zeros_like(dgain_ref)

        secs = []
        for k in range(3):
            parts = [_view_to_tile(views[3 * k + p], d, planes) for p, d in enumerate(DILATIONS)]
            secs.append(parts[0] + parts[1] + parts[2])
        secs += [s3[...], s4[...], s5[...], s6[...]]
        dh = jnp.zeros((tm, D), F32)
        for k, s in enumerate(secs):
            d = s.astype(BF16)
            dproj_ref[:, k * W:(k + 1) * W] = d
            dh += _nn(d, w_ref[k * W:(k + 1) * W, :])
        d = dbd_ref[...].astype(BF16)
        dproj_ref[:, 7 * W:7 * W + 128] = d
        dh += _nn(d, w_ref[7 * W:7 * W + 128, :])
        dx, dgain = _rmsnorm_bwd(dh, x_ref[...], gain_ref[...])
        dx_ref[...] = dxo_ref[...] + dx
        dgain_ref[...] += dgain

    tok = lambda w: pl.BlockSpec((tm, w), lambda i: (i, 0))
    return pl.pallas_call(
        body, name="inproj_bwd", grid=(S // tm,),
        in_specs=[tok(D), tok(D), pl.BlockSpec((1, D), lambda i: (0, 0))] + [_view_spec(d, tm) for d in DILATIONS] * 3
                 + [tok(W)] * 4 + [tok(128)] + [pl.BlockSpec((IN_COLS_PADDED, D), lambda i: (0, 0))],
        out_specs=[tok(D), pl.BlockSpec((1, D), lambda i: (0, 0)), tok(IN_COLS_PADDED)],
        out_shape=[jax.ShapeDtypeStruct((S, D), F32), jax.ShapeDtypeStruct((1, D), F32),
                   jax.ShapeDtypeStruct((S, IN_COLS_PADDED), BF16)],
        scratch_shapes=[pltpu.VMEM((4, tm, 128), F32)],
        compiler_params=_params(1),
    )(dxo, x, gain, *[g for grads in attn_grads for g in grads], *dsecs, dbd, w_in_p)


def _slope(h):
    return 2.0 ** (-8.0 * (h + 1) / ATTN_HEADS)


def _head_bias(steps, d, heads=tuple(range(ATTN_HEADS))):
    stepsf = steps.astype(F32)
    return jnp.stack([stepsf * (-_slope(h) * d) for h in heads])


def _hnt(a, b):
    return lax.dot_general(a, b, (((2,), (2,)), ((0,), (0,))), preferred_element_type=F32)


def _hnn(a, b):
    return lax.dot_general(a, b, (((2,), (1,)), ((0,), (0,))), preferred_element_type=F32)


def _blocks_per_step(nb):
    return next(n for n in (4, 2, 1) if nb % n == 0)


def _query_step_specs(qb):
    B = ATTN_BLOCK
    cur = pl.BlockSpec((qb * B, ATTN_WIDTH), lambda r, n: (n, r))
    prev = pl.BlockSpec((B, ATTN_WIDTH), lambda r, n: (jnp.maximum(qb * n - 1, 0), r))
    return cur, prev


def _prev_block(prev_ref, cur_ref, sub, sl):
    B = ATTN_BLOCK
    return prev_ref[:, sl] if sub == 0 else cur_ref[(sub - 1) * B:sub * B, sl]


def _head_cols(tile, lo, big):
    return [_head_col(tile, lo, big), _head_col(tile, jnp.logical_not(lo), big)]


def _attn_fwd(q, k, v, d, name):
    L = q.shape[0]
    nb = L // ATTN_BLOCK
    B = ATTN_BLOCK
    QB = _blocks_per_step(nb)

    def body(q_ref, kp_ref, kc_ref, vp_ref, vc_ref, o_ref, lse_ref):
        n = pl.program_id(1)
        qi = lax.broadcasted_iota(jnp.int32, (B, 2 * B), 0)
        kj = lax.broadcasted_iota(jnp.int32, (B, 2 * B), 1)
        steps = qi + B - kj
        band = (steps >= 0) & (steps <= B)
        lo = lax.broadcasted_iota(jnp.int32, (B, 128), 1) < 64
        bias = _head_bias(steps, d)
        for sub in range(QB):
            rows = slice(sub * B, (sub + 1) * B)
            valid = band & ((kj >= B) | (n > 0)) if sub == 0 else band
            qs, ks, vs = [], [], []
            for G in range(4):
                sl = slice(G * 128, (G + 1) * 128)
                qg = q_ref[rows, sl]
                kg = jnp.concatenate([_prev_block(kp_ref, kc_ref, sub, sl), kc_ref[rows, sl]], axis=0)
                vg = jnp.concatenate([_prev_block(vp_ref, vc_ref, sub, sl), vc_ref[rows, sl]], axis=0)
                qs += [jnp.where(lo, qg, jnp.zeros_like(qg)), jnp.where(lo, jnp.zeros_like(qg), qg)]
                ks += [kg, kg]
                vs += [vg, vg]
            s = jnp.where(valid, _hnt(jnp.stack(qs), jnp.stack(ks)) + bias, NEG_BIG)
            m = jnp.max(s, axis=-1, keepdims=True)
            p = jnp.exp(s - m)
            l = jnp.sum(p, axis=-1, keepdims=True)
            o = _hnn(p.astype(BF16), jnp.stack(vs)) / l
            lse = m + jnp.log(l)
            for G in range(4):
                sl = slice(G * 128, (G + 1) * 128)
                o_ref[rows, sl] = jnp.where(lo, o[2 * G], o[2 * G + 1])
                lse_ref[rows, sl] = jnp.where(lo, lse[2 * G], lse[2 * G + 1])

    cur, prev = _query_step_specs(QB)
    return pl.pallas_call(
        body, name=name, grid=(d, nb // QB),
        in_specs=[cur, prev, cur, prev, cur],
        out_specs=[cur, cur],
        out_shape=[jax.ShapeDtypeStruct((L, d * ATTN_WIDTH), F32)] * 2,
        compiler_params=_params(2),
    )(q, k, k, v, v)


def _attn_merge(parts):
    S = parts[0][0].shape[0]
    tm = VIEW_TILE

    def body(o1, s1, o2, s2, o3, s3, o_ref, lse1, lse4, lse16, planes):
        outs, lses = [], []
        for d, (o, s) in zip(DILATIONS, ((o1, s1), (o2, s2), (o3, s3))):
            outs.append(_view_to_tile(o, d, planes))
            lses.append(_view_to_tile(s, d, planes))
        mx = jnp.maximum(jnp.maximum(lses[0], lses[1]), lses[2])
        es = [jnp.exp(s - mx) for s in lses]
        den = es[0] + es[1] + es[2]
        o_ref[...] = (es[0] * outs[0] + es[1] * outs[1] + es[2] * outs[2]) / den
        _tile_to_views(mx + jnp.log(den), planes, (lse1, lse4, lse16))

    views = [_view_spec(d) for d in DILATIONS]
    flat = [t for p in parts for t in p]
    return pl.pallas_call(
        body, name="attn_merge", grid=(S // tm,),
        in_specs=[views[p] for p in range(3) for _ in range(2)],
        out_specs=[views[0]] + views,
        out_shape=[jax.ShapeDtypeStruct((S, ATTN_WIDTH), F32)] + [_view_shape(S, d, F32) for d in DILATIONS],
        scratch_shapes=[pltpu.VMEM((4, tm, 128), F32)],
        compiler_params=_params(1),
    )(*flat)


def _head_col(t, msk, big):
    if big:
        return jnp.max(jnp.where(msk, t, NEG_BIG), axis=-1, keepdims=True)
    return jnp.sum(jnp.where(msk, t, 0.0), axis=-1, keepdims=True) * (1.0 / 64.0)


def _attn_bwd_q(q, k, v, do, lse, dd, d, name):
    L = q.shape[0]
    nb = L // ATTN_BLOCK
    B = ATTN_BLOCK
    QB = _blocks_per_step(nb)

    def body(q_ref, kp_ref, kc_ref, vp_ref, vc_ref, do_ref, lse_ref, dd_ref, dq_ref):
        n = pl.program_id(1)
        qi = lax.broadcasted_iota(jnp.int32, (B, 2 * B), 0)
        kj = lax.broadcasted_iota(jnp.int32, (B, 2 * B), 1)
        steps = qi + B - kj
        band = (steps >= 0) & (steps <= B)
        lo = lax.broadcasted_iota(jnp.int32, (B, 128), 1) < 64
        bias = _head_bias(steps, d)
        for sub in range(QB):
            rows = slice(sub * B, (sub + 1) * B)
            valid = band & ((kj >= B) | (n > 0)) if sub == 0 else band
            qs, ks, vs, dos, lses, dcols = [], [], [], [], [], []
            for G in range(4):
                sl = slice(G * 128, (G + 1) * 128)
                qg = q_ref[rows, sl]
                kg = jnp.concatenate([_prev_block(kp_ref, kc_ref, sub, sl), kc_ref[rows, sl]], axis=0)
                vg = jnp.concatenate([_prev_block(vp_ref, vc_ref, sub, sl), vc_ref[rows, sl]], axis=0)
                dog = do_ref[rows, sl]
                qs += [jnp.where(lo, qg, jnp.zeros_like(qg)), jnp.where(lo, jnp.zeros_like(qg), qg)]
                dos += [jnp.where(lo, dog, 0.0).astype(BF16), jnp.where(lo, 0.0, dog).astype(BF16)]
                ks += [kg, kg]
                vs += [vg, vg]
                lses += _head_cols(lse_ref[rows, sl], lo, True)
                dcols += _head_cols(dd_ref[rows, sl], lo, False)
            kb = jnp.stack(ks)
            s = _hnt(jnp.stack(qs), kb) + bias
            p = jnp.where(valid, jnp.exp(jnp.where(valid, s, NEG_BIG) - jnp.stack(lses)), 0.0)
            dp = _hnt(jnp.stack(dos), jnp.stack(vs))
            ds = p * (dp - jnp.stack(dcols))
            dq = _hnn(ds.astype(BF16), kb) * ATTN_SCALE
            for G in range(4):
                dq_ref[rows, G * 128:(G + 1) * 128] = jnp.where(lo, dq[2 * G], dq[2 * G + 1]).astype(BF16)

    cur, prev = _query_step_specs(QB)
    return pl.pallas_call(
        body, name=name, grid=(d, nb // QB), in_specs=[cur, prev, cur, prev, cur, cur, cur, cur], out_specs=cur,
        out_shape=jax.ShapeDtypeStruct((L, d * ATTN_WIDTH), BF16), compiler_params=_params(2),
    )(q, k, k, v, v, do, lse, dd)


def _attn_bwd_kv(q, k, v, do, lse, dd, d, name):
    L = q.shape[0]
    nb = L // ATTN_BLOCK
    B = ATTN_BLOCK
    KB = _blocks_per_step(nb)
    n_steps = nb // KB

    def body(k_ref, v_ref, qc_ref, qn_ref, doc_ref, don_ref, lsec_ref, lsen_ref, ddc_ref, ddn_ref, dk_ref, dv_ref):
        j = pl.program_id(1)
        qrow = lax.broadcasted_iota(jnp.int32, (2 * B, B), 0)
        kk = lax.broadcasted_iota(jnp.int32, (2 * B, B), 1)
        steps = qrow - kk
        band = (steps >= 0) & (steps <= B)
        lo2 = lax.broadcasted_iota(jnp.int32, (2 * B, 128), 1) < 64
        lo = lax.broadcasted_iota(jnp.int32, (B, 128), 1) < 64
        stepsf = steps.astype(F32)
        for sub in range(KB):
            rows = slice(sub * B, (sub + 1) * B)
            last = sub == KB - 1
            valid = band & ((qrow < B) | (j < n_steps - 1)) if last else band
            after = lambda cur_ref, nxt_ref, sl: nxt_ref[:, sl] if last else cur_ref[(sub + 1) * B:(sub + 2) * B, sl]
            for G in range(4):
                sl = slice(G * 128, (G + 1) * 128)
                kg = k_ref[rows, sl]
                vg = v_ref[rows, sl]
                qq = jnp.concatenate([qc_ref[rows, sl], after(qc_ref, qn_ref, sl)], axis=0)
                doo = jnp.concatenate([doc_ref[rows, sl], after(doc_ref, don_ref, sl)], axis=0)
                lse2 = jnp.concatenate([lsec_ref[rows, sl], after(lsec_ref, lsen_ref, sl)], axis=0)
                dd2 = jnp.concatenate([ddc_ref[rows, sl], after(ddc_ref, ddn_ref, sl)], axis=0)
                doo_b = doo.astype(BF16)
                dks, dvs = [], []
                for half in (0, 1):
                    msk = lo2 if half == 0 else jnp.logical_not(lo2)
                    qm = jnp.where(msk, qq, jnp.zeros_like(qq))
                    s = _nt(qm, kg) - (_slope(2 * G + half) * d) * stepsf
                    lse_c = _head_col(lse2, msk, True)
                    p = jnp.where(valid, jnp.exp(jnp.where(valid, s, NEG_BIG) - lse_c), 0.0)
                    dvs.append(_tn(p.astype(BF16), doo_b))
                    dom = jnp.where(msk, doo, 0.0).astype(BF16)
                    dp = _nt(dom, vg)
                    dcol = _head_col(dd2, msk, False)
                    ds = p * (dp - dcol)
                    dks.append(_tn(ds.astype(BF16), qq))
                dk_ref[rows, sl] = jnp.where(lo, dks[0], dks[1]).astype(BF16)
                dv_ref[rows, sl] = jnp.where(lo, dvs[0], dvs[1]).astype(BF16)

    cur = pl.BlockSpec((KB * B, ATTN_WIDTH), lambda r, j: (j, r))
    nxt = pl.BlockSpec((B, ATTN_WIDTH), lambda r, j: (jnp.minimum(KB * (j + 1), nb - 1), r))
    return pl.pallas_call(
        body, name=name, grid=(d, n_steps), in_specs=[cur, cur, cur, nxt, cur, nxt, cur, nxt, cur, nxt],
        out_specs=[cur, cur],
        out_shape=[jax.ShapeDtypeStruct((L, d * ATTN_WIDTH), BF16)] * 2, compiler_params=_params(2),
    )(k, v, q, q, do, do, lse, lse, dd, dd)


CONV_T = 512
HALO = 8


def _per_head(head, refs):
    for h in range(DN_HEADS):
        lanes = pl.ds(h * DN_HEAD_DIM, DN_HEAD_DIM)
        head(*[r.at[:, lanes] for r in refs[:-1]], refs[-1])


def _conv_taps(pad_ref, w, T):
    acc = pad_ref[pl.ds(HALO - 3, T), :] * w[0:1, :]
    for j in range(1, CONV_WIDTH):
        acc = acc + pad_ref[pl.ds(HALO - 3 + j, T), :] * w[j:j + 1, :]
    return acc


def _conv_fwd(xq, xk, xv, conv_w):
    S = xq.shape[0]
    T = CONV_T

    def body(*refs):
        _per_head(head, refs)

    def head(xq_ref, xqh_ref, xk_ref, xkh_ref, xv_ref, xvh_ref, wq_ref, wk_ref, wv_ref,
             qn_ref, kn_ref, v_ref, pad_ref):
        i = pl.program_id(0)

        def act(x_ref, xh_ref, w_ref):
            pad_ref[pl.ds(0, HALO), :] = jnp.where(i > 0, xh_ref[...], 0.0)
            pad_ref[pl.ds(HALO, T), :] = x_ref[...]
            c = _conv_taps(pad_ref, w_ref[...], T)
            return c * _sigmoid(c)

        def l2n(t):
            return t * lax.rsqrt(jnp.sum(t * t, axis=-1, keepdims=True) + L2_EPS)

        qn_ref[...] = l2n(act(xq_ref, xqh_ref, wq_ref))
        kn_ref[...] = l2n(act(xk_ref, xkh_ref, wk_ref))
        v_ref[...] = act(xv_ref, xvh_ref, wv_ref)

    tile = pl.BlockSpec((T, DN_WIDTH), lambda i: (i, 0))
    halo = pl.BlockSpec((HALO, DN_WIDTH), lambda i: (jnp.maximum(i * (T // HALO) - 1, 0), 0))
    wspec = lambda sec: pl.BlockSpec((CONV_WIDTH, DN_WIDTH), lambda i, sec=sec: (0, sec))
    return pl.pallas_call(
        body, name="dn_conv_fwd", grid=(S // T,),
        in_specs=[tile, halo, tile, halo, tile, halo, wspec(0), wspec(1), wspec(2)],
        out_specs=[tile, tile, tile],
        out_shape=[jax.ShapeDtypeStruct((S, DN_WIDTH), F32)] * 3,
        scratch_shapes=[pltpu.VMEM((T + HALO, 128), F32)],
        compiler_params=_params(1),
    )(xq, xq, xk, xk, xv, xv, conv_w, conv_w, conv_w)


def _conv_bwd_pre(xq, xk, xv, conv_w, dqn, dkn, dv):
    S = xq.shape[0]
    T = CONV_T

    def body(*refs):
        _per_head(head, refs)

    def head(xq_ref, xqh_ref, xk_ref, xkh_ref, xv_ref, xvh_ref, wq_ref, wk_ref, wv_ref,
             dqn_ref, dkn_ref, dv_ref, dcq_ref, dck_ref, dcv_ref, dwq_ref, dwk_ref, dwv_ref, pad_ref):
        i = pl.program_id(0)

        def one(x_ref, xh_ref, w_ref, dy_ref, dc_ref, dw_ref, normed):
            pad_ref[pl.ds(0, HALO), :] = jnp.where(i > 0, xh_ref[...], 0.0)
            pad_ref[pl.ds(HALO, T), :] = x_ref[...]
            c = _conv_taps(pad_ref, w_ref[...], T)
            sg = _sigmoid(c)
            a = c * sg
            dy = dy_ref[...]
            if normed:
                r = lax.rsqrt(jnp.sum(a * a, axis=-1, keepdims=True) + L2_EPS)
                y = a * r
                da = r * (dy - y * jnp.sum(dy * y, axis=-1, keepdims=True))
            else:
                da = dy
            dc = da * (sg * (1.0 + c * (1.0 - sg)))
            dc_ref[...] = dc

            @pl.when(i == 0)
            def _():
                dw_ref[...] = jnp.zeros_like(dw_ref)

            rows = [jnp.sum(dc * pad_ref[pl.ds(HALO - 3 + j, T), :], axis=0, keepdims=True) for j in range(CONV_WIDTH)]
            dw_ref[...] += jnp.concatenate(rows + [jnp.zeros((8 - CONV_WIDTH, 128), F32)], axis=0)

        one(xq_ref, xqh_ref, wq_ref, dqn_ref, dcq_ref, dwq_ref, True)
        one(xk_ref, xkh_ref, wk_ref, dkn_ref, dck_ref, dwk_ref, True)
        one(xv_ref, xvh_ref, wv_ref, dv_ref, dcv_ref, dwv_ref, False)

    tile = pl.BlockSpec((T, DN_WIDTH), lambda i: (i, 0))
    halo = pl.BlockSpec((HALO, DN_WIDTH), lambda i: (jnp.maximum(i * (T // HALO) - 1, 0), 0))
    wspec = lambda sec: pl.BlockSpec((CONV_WIDTH, DN_WIDTH), lambda i, sec=sec: (0, sec))
    dwspec = pl.BlockSpec((8, DN_WIDTH), lambda i: (0, 0))
    return pl.pallas_call(
        body, name="dn_conv_bwd_pre", grid=(S // T,),
        in_specs=[tile, halo, tile, halo, tile, halo, wspec(0), wspec(1), wspec(2), tile, tile, tile],
        out_specs=[tile, tile, tile, dwspec, dwspec, dwspec],
        out_shape=[jax.ShapeDtypeStruct((S, DN_WIDTH), F32)] * 3 + [jax.ShapeDtypeStruct((8, DN_WIDTH), F32)] * 3,
        scratch_shapes=[pltpu.VMEM((T + HALO, 128), F32)],
        compiler_params=_params(1),
    )(xq, xq, xk, xk, xv, xv, conv_w, conv_w, conv_w, dqn, dkn, dv)


def _conv_bwd_x(dcq, dck, dcv, conv_w):
    S = dcq.shape[0]
    T = CONV_T
    nt = S // T

    def body(*refs):
        _per_head(head, refs)

    def head(dq_ref, dqh_ref, dk_ref, dkh_ref, dv_ref, dvh_ref, wq_ref, wk_ref, wv_ref,
             oq_ref, ok_ref, ov_ref, pad_ref):
        i = pl.program_id(0)

        def one(d_ref, dh_ref, w_ref, o_ref):
            pad_ref[pl.ds(0, T), :] = d_ref[...]
            pad_ref[pl.ds(T, HALO), :] = jnp.where(i < nt - 1, dh_ref[...], 0.0)
            w = w_ref[...]
            acc = pad_ref[pl.ds(3, T), :] * w[0:1, :]
            for j in range(1, CONV_WIDTH):
                acc = acc + pad_ref[pl.ds(3 - j, T), :] * w[j:j + 1, :]
            o_ref[...] = acc

        one(dq_ref, dqh_ref, wq_ref, oq_ref)
        one(dk_ref, dkh_ref, wk_ref, ok_ref)
        one(dv_ref, dvh_ref, wv_ref, ov_ref)

    tile = pl.BlockSpec((T, DN_WIDTH), lambda i: (i, 0))
    halo = pl.BlockSpec((HALO, DN_WIDTH), lambda i: (jnp.minimum((i + 1) * (T // HALO), S // HALO - 1), 0))
    wspec = lambda sec: pl.BlockSpec((CONV_WIDTH, DN_WIDTH), lambda i, sec=sec: (0, sec))
    return pl.pallas_call(
        body, name="dn_conv_bwd_x", grid=(nt,),
        in_specs=[tile, halo, tile, halo, tile, halo, wspec(0), wspec(1), wspec(2)],
        out_specs=[tile, tile, tile],
        out_shape=[jax.ShapeDtypeStruct((S, DN_WIDTH), F32)] * 3,
        scratch_shapes=[pltpu.VMEM((T + HALO, 128), F32)],
        compiler_params=_params(1),
    )(dcq, dcq, dck, dck, dcv, dcv, conv_w, conv_w, conv_w)


PREP_CHUNKS = 4
SCAN_CHUNKS = 8


def _bnn(a, b):
    return lax.dot_general(a, b, (((2,), (1,)), ((0,), (0,))), preferred_element_type=F32, precision=HI)


def _bnt(a, b):
    return lax.dot_general(a, b, (((2,), (2,)), ((0,), (0,))), preferred_element_type=F32, precision=HI)


def _btn(a, b):
    return lax.dot_general(a, b, (((1,), (1,)), ((0,), (0,))), preferred_element_type=F32, precision=HI)


def _tri_inverse_b(a, blk, eye):
    dg = jnp.where(blk, a, 0.0)
    lo = a - dg
    d2 = _bnn(dg, dg)
    d4 = _bnn(d2, d2)
    d8 = _bnn(d4, d4)
    td = _bnn(_bnn(_bnn(eye - dg, eye + d2), eye + d4), eye + d8)
    b = _bnn(td, lo)
    b2 = _bnn(b, b)
    return _bnn(_bnn(eye - b, eye + b2), td)


def _dn_common_b(bds, avec, dvec, q_raw, k, v, t=None):
    C = DN_CHUNK
    lane = lax.broadcasted_iota(jnp.int32, (C, 128), 1)
    row = lax.broadcasted_iota(jnp.int32, (1, C, C), 1)
    col = lax.broadcasted_iota(jnp.int32, (1, C, C), 2)
    incl = row >= col
    strict = row > col
    eye = (row == col).astype(F32)
    blk = (row // 16) == (col // 16)
    pick = lambda tile, ln: jnp.sum(jnp.where(lane == ln, tile, 0.0), axis=-1, keepdims=True)
    betas, graws, zcs = [], [], []
    for bd in bds:
        z = bd + dvec
        g_all = -jnp.exp(avec) * (jnp.maximum(z, 0.0) + jnp.log(1.0 + jnp.exp(-jnp.abs(z))))
        beta_all = _sigmoid(bd)
        for h in range(DN_HEADS):
            betas.append(pick(beta_all, h))
            graws.append(pick(g_all, DN_HEADS + h))
            zcs.append(pick(z, DN_HEADS + h))
    beta, graw, zc = jnp.stack(betas), jnp.stack(graws), jnp.stack(zcs)
    to_row = lambda c: jnp.sum(eye * c, axis=1, keepdims=True)
    gc = jnp.sum(jnp.where(incl, to_row(graw), 0.0), axis=-1, keepdims=True)
    decay = jnp.exp(jnp.where(incl, gc - to_row(gc), NEG_BIG))
    q = q_raw * (DN_HEAD_DIM ** -0.5)
    kb = k * beta
    kk = _bnt(kb, k)
    if t is None:
        t = _tri_inverse_b(jnp.where(strict, kk * decay, 0.0), blk, eye)
    eg = jnp.exp(gc)
    rhs_w = kb * eg
    u = _bnn(t, v * beta)
    w = _bnn(t, rhs_w)
    qk = _bnt(q, k)
    aq = jnp.where(incl, qk * decay, 0.0)
    last = lax.broadcasted_iota(jnp.int32, (1, C, 1), 1) == C - 1
    g_last = jnp.sum(jnp.where(last, gc, 0.0), axis=1, keepdims=True)
    ekd = jnp.exp(g_last - gc)
    return dict(beta=beta, graw=graw, zc=zc, gc=gc, decay=decay, q=q, kb=kb, kk=kk, t=t, eg=eg, rhs_w=rhs_w,
                u=u, w=w, qk=qk, aq=aq, g_last=g_last, ekd=ekd, kd=k * ekd, qg=q * eg,
                incl=incl, strict=strict, eye=eye, lane=lane, row=row, col=col, last=last)


def _stack_heads(ref, rows):
    return jnp.stack([ref[rows, h * DN_HEAD_DIM:(h + 1) * DN_HEAD_DIM] for h in range(DN_HEADS)])


def _stack_units(ref, nc):
    C = DN_CHUNK
    return jnp.concatenate([_stack_heads(ref, slice(ci * C, (ci + 1) * C)) for ci in range(nc)], axis=0)


def _store_units(ref, val, nc):
    C = DN_CHUNK
    for ci in range(nc):
        for h in range(DN_HEADS):
            ref[ci * C:(ci + 1) * C, h * DN_HEAD_DIM:(h + 1) * DN_HEAD_DIM] = val[ci * DN_HEADS + h]


def _dn_prep(qn, kn, v, bd, avec, dvec):
    S = qn.shape[0]
    C = DN_CHUNK
    N = S // C
    nc = PREP_CHUNKS

    def body(q_ref, k_ref, v_ref, bd_ref, a_ref, d_ref, u_ref, w_ref, qg_ref, kd_ref, aq_ref, t_ref, egl_ref):
        bds = [bd_ref[ci * C:(ci + 1) * C, :] for ci in range(nc)]
        c = _dn_common_b(bds, a_ref[...], d_ref[...], _stack_units(q_ref, nc), _stack_units(k_ref, nc), _stack_units(v_ref, nc))
        _store_units(u_ref, c["u"], nc)
        _store_units(w_ref, c["w"], nc)
        _store_units(qg_ref, c["qg"], nc)
        _store_units(kd_ref, c["kd"], nc)
        egl = jnp.broadcast_to(jnp.exp(c["g_last"]), (nc * DN_HEADS, 1, 128))
        for ci in range(nc):
            for h in range(DN_HEADS):
                aq_ref[h, ci * C:(ci + 1) * C, :] = c["aq"][ci * DN_HEADS + h]
                t_ref[h, ci * C:(ci + 1) * C, :] = c["t"][ci * DN_HEADS + h]
            egl_ref[ci * 8:(ci + 1) * 8, :] = jnp.concatenate(
                [egl[ci * DN_HEADS + h] for h in range(DN_HEADS)] + [jnp.zeros((8 - DN_HEADS, 128), F32)], axis=0)

    tok = lambda w: pl.BlockSpec((nc * C, w), lambda n: (n, 0))
    sq = pl.BlockSpec((DN_HEADS, nc * C, C), lambda n: (0, n, 0))
    vec = pl.BlockSpec((1, 128), lambda n: (0, 0))
    return pl.pallas_call(
        body, name="dn_prep", grid=(N // nc,),
        in_specs=[tok(DN_WIDTH)] * 3 + [tok(128), vec, vec],
        out_specs=[tok(DN_WIDTH)] * 4 + [sq, sq, pl.BlockSpec((nc * 8, 128), lambda n: (n, 0))],
        out_shape=[jax.ShapeDtypeStruct((S, DN_WIDTH), F32)] * 4 + [jax.ShapeDtypeStruct((DN_HEADS, S, C), F32)] * 2
                  + [jax.ShapeDtypeStruct((N * 8, 128), F32)],
        compiler_params=_params(1),
    )(qn, kn, v, bd, avec, dvec)


def _dn_scan_fwd(u, w, qg, kd, aq, egl, gate, dn_gain):
    S = u.shape[0]
    C = DN_CHUNK
    N = S // C
    HD = DN_HEAD_DIM
    nc = SCAN_CHUNKS

    def body(u_ref, w_ref, qg_ref, kd_ref, aq_ref, egl_ref, gate_ref, gain_ref, dn_ref, o_ref, vn_ref, st_ref, state_ref):
        @pl.when(pl.program_id(0) == 0)
        def _():
            state_ref[...] = jnp.zeros_like(state_ref)

        gain = gain_ref[...]
        for ci in range(nc):
            rows = slice(ci * C, (ci + 1) * C)
            st = state_ref[...]
            for h in range(DN_HEADS):
                st_ref[ci * DN_WIDTH + h * HD:ci * DN_WIDTH + (h + 1) * HD, :] = st[h]
            v_new = _stack_heads(u_ref, rows) - _bnn(_stack_heads(w_ref, rows), st)
            o = _bnn(_stack_heads(qg_ref, rows), st) + _bnn(aq_ref[:, rows, :], v_new)
            egl = jnp.stack([egl_ref[ci * 8 + h:ci * 8 + h + 1, :] for h in range(DN_HEADS)])
            state_ref[...] = st * egl + _btn(_stack_heads(kd_ref, rows), v_new)
            r = lax.rsqrt(jnp.mean(o * o, axis=-1, keepdims=True) + NORM_EPS)
            gt = _stack_heads(gate_ref, rows)
            dn = o * r * gain * (gt * _sigmoid(gt))
            for h in range(DN_HEADS):
                sl = slice(h * HD, (h + 1) * HD)
                vn_ref[rows, sl] = v_new[h]
                o_ref[rows, sl] = o[h]
                dn_ref[rows, sl] = dn[h]

    tok = lambda wd: pl.BlockSpec((nc * C, wd), lambda n: (n, 0))
    sq = pl.BlockSpec((DN_HEADS, nc * C, C), lambda n: (0, n, 0))
    vec = pl.BlockSpec((1, 128), lambda n: (0, 0))
    return pl.pallas_call(
        body, name="dn_scan_fwd", grid=(N // nc,),
        in_specs=[tok(DN_WIDTH)] * 4 + [sq, pl.BlockSpec((nc * 8, 128), lambda n: (n, 0)), tok(DN_WIDTH), vec],
        out_specs=[tok(DN_WIDTH)] * 3 + [pl.BlockSpec((nc * DN_WIDTH, HD), lambda n: (n, 0))],
        out_shape=[jax.ShapeDtypeStruct((S, DN_WIDTH), F32)] * 3 + [jax.ShapeDtypeStruct((N * DN_WIDTH, HD), F32)],
        scratch_shapes=[pltpu.VMEM((DN_HEADS, HD, HD), F32)],
        compiler_params=_params(1),
    )(u, w, qg, kd, aq, egl, gate, dn_gain)


def _dn_scan_bwd(w, qg, kd, aq, egl, gate, dn_gain, o, ddn):
    S = w.shape[0]
    C = DN_CHUNK
    N = S // C
    HD = DN_HEAD_DIM
    nc = SCAN_CHUNKS

    def body(w_ref, qg_ref, kd_ref, aq_ref, egl_ref, gate_ref, gain_ref, o_ref, ddn_ref,
             do_ref, dvn_ref, dgate_ref, dst_ref, small_ref, dstate_ref):
        @pl.when(pl.program_id(0) == 0)
        def _():
            dstate_ref[...] = jnp.zeros_like(dstate_ref)
            small_ref[...] = jnp.zeros_like(small_ref)

        gain = gain_ref[...]
        d_gain = jnp.zeros((1, 128), F32)
        for ci in reversed(range(nc)):
            rows = slice(ci * C, (ci + 1) * C)
            dsn = dstate_ref[...]
            for h in range(DN_HEADS):
                dst_ref[ci * DN_WIDTH + h * HD:ci * DN_WIDTH + (h + 1) * HD, :] = dsn[h]
            ov = _stack_heads(o_ref, rows)
            r = lax.rsqrt(jnp.mean(ov * ov, axis=-1, keepdims=True) + NORM_EPS)
            on = ov * r
            gt = _stack_heads(gate_ref, rows)
            sgt = _sigmoid(gt)
            silu_g = gt * sgt
            dy = _stack_heads(ddn_ref, rows)
            d_gain = d_gain + jnp.sum(jnp.sum(dy * on * silu_g, axis=1, keepdims=True), axis=0)
            dgate = dy * on * gain * (sgt * (1.0 + gt * (1.0 - sgt)))
            don = dy * gain * silu_g
            do = r * (don - on * jnp.mean(don * on, axis=-1, keepdims=True))
            d_vnew = _btn(aq_ref[:, rows, :], do) + _bnn(_stack_heads(kd_ref, rows), dsn)
            egl = jnp.stack([egl_ref[ci * 8 + h:ci * 8 + h + 1, :] for h in range(DN_HEADS)])
            dstate_ref[...] = _btn(_stack_heads(qg_ref, rows), do) + dsn * egl - _btn(_stack_heads(w_ref, rows), d_vnew)
            for h in range(DN_HEADS):
                sl = slice(h * HD, (h + 1) * HD)
                do_ref[rows, sl] = do[h]
                dvn_ref[rows, sl] = d_vnew[h]
                dgate_ref[rows, sl] = dgate[h]
        small_ref[...] += jnp.concatenate([d_gain, jnp.zeros((7, 128), F32)], axis=0)

    nb = N // nc
    tok = lambda wd: pl.BlockSpec((nc * C, wd), lambda i: (nb - 1 - i, 0))
    sq = pl.BlockSpec((DN_HEADS, nc * C, C), lambda i: (0, nb - 1 - i, 0))
    vec = pl.BlockSpec((1, 128), lambda i: (0, 0))
    return pl.pallas_call(
        body, name="dn_scan_bwd", grid=(nb,),
        in_specs=[tok(DN_WIDTH)] * 3 + [sq, pl.BlockSpec((nc * 8, 128), lambda i: (nb - 1 - i, 0)), tok(DN_WIDTH), vec,
                                       tok(DN_WIDTH), tok(DN_WIDTH)],
        out_specs=[tok(DN_WIDTH)] * 3 + [pl.BlockSpec((nc * DN_WIDTH, HD), lambda i: (nb - 1 - i, 0)),
                                        pl.BlockSpec((8, 128), lambda i: (0, 0))],
        out_shape=[jax.ShapeDtypeStruct((S, DN_WIDTH), F32)] * 3 + [jax.ShapeDtypeStruct((N * DN_WIDTH, HD), F32),
                                                                  jax.ShapeDtypeStruct((8, 128), F32)],
        scratch_shapes=[pltpu.VMEM((DN_HEADS, HD, HD), F32)],
        compiler_params=_params(1),
    )(w, qg, kd, aq, egl, gate, dn_gain, o, ddn)


def _dn_post(qn, kn, v, bd, avec, dvec, t_inv, v_new_all, states, dstates, do_all, dvn_all, comm=None):
    S = qn.shape[0]
    C = DN_CHUNK
    N = S // C
    HD = DN_HEAD_DIM
    nc = PREP_CHUNKS
    B = nc * DN_HEADS

    def body(q_ref, k_ref, v_ref, bd_ref, a_ref, d_ref, t_ref, vn_ref, st_ref, dst_ref, do_ref, dvn_ref,
             dq_ref, dk_ref, dv_ref, dbd_ref, small_ref):
        @pl.when(pl.program_id(0) == 0)
        def _():
            small_ref[...] = jnp.zeros_like(small_ref)

        avec = a_ref[...]
        bds = [bd_ref[ci * C:(ci + 1) * C, :] for ci in range(nc)]
        k = _stack_units(k_ref, nc)
        vv = _stack_units(v_ref, nc)
        t = jnp.concatenate([t_ref[:, ci * C:(ci + 1) * C, :] for ci in range(nc)], axis=0)
        c = _dn_common_b(bds, avec, d_ref[...], _stack_units(q_ref, nc), k, vv, t=t)
        q, kb, eg, u, w = c["q"], c["kb"], c["eg"], c["u"], c["w"]
        beta, decay, incl, strict, eye = c["beta"], c["decay"], c["incl"], c["strict"], c["eye"]
        st = jnp.stack([st_ref[b * HD:(b + 1) * HD, :] for b in range(B)])
        dsn = jnp.stack([dst_ref[b * HD:(b + 1) * HD, :] for b in range(B)])
        v_new = _stack_units(vn_ref, nc)
        do = _stack_units(do_ref, nc)
        d_vnew = _stack_units(dvn_ref, nc)
        egl = jnp.exp(c["g_last"])
        daq = jnp.where(incl, _bnt(do, v_new), 0.0)
        d_qg = _bnt(do, st)
        d_kd = _bnt(v_new, dsn)
        d_glast = jnp.sum(jnp.sum(dsn * st, axis=-1, keepdims=True), axis=1, keepdims=True) * egl
        d_w = -_bnt(d_vnew, st)
        d_ru = _btn(t, d_vnew)
        d_rw = _btn(t, d_w)
        da = -jnp.where(strict, _bnt(d_ru, u) + _bnt(d_rw, w), 0.0)
        dv = d_ru * beta
        dbeta = jnp.sum(d_ru * vv, axis=-1, keepdims=True)
        dkb = d_rw * eg
        dgc = jnp.sum(d_rw * c["rhs_w"], axis=-1, keepdims=True)
        dkk = da * decay
        ddecay = da * c["kk"]
        dkb = dkb + _bnn(dkk, k)
        dk = _btn(dkk, kb)
        dqk = daq * decay
        ddecay = ddecay + daq * c["qk"]
        dq = _bnn(dqk, k)
        dk = dk + _btn(dqk, q)
        m = ddecay * decay
        col_sum = jnp.sum(m, axis=1, keepdims=True)
        dgc = dgc + jnp.sum(m, axis=-1, keepdims=True) - jnp.sum(eye * col_sum, axis=-1, keepdims=True)
        dq = dq + d_qg * eg
        dgc = dgc + jnp.sum(d_qg * c["qg"], axis=-1, keepdims=True)
        dk = dk + d_kd * c["ekd"]
        tk = jnp.sum(d_kd * c["kd"], axis=-1, keepdims=True)
        dgc = dgc - tk
        d_glast = d_glast + jnp.sum(tk, axis=1, keepdims=True)
        dk = dk + dkb * beta
        dbeta = dbeta + jnp.sum(dkb * k, axis=-1, keepdims=True)
        dgc = dgc + jnp.where(c["last"], d_glast, 0.0)
        dgc_row = jnp.sum(eye * dgc, axis=1, keepdims=True)
        dgraw = jnp.sum(jnp.where(c["col"] >= c["row"], dgc_row, 0.0), axis=-1, keepdims=True)
        _store_units(dq_ref, dq * (HD ** -0.5), nc)
        _store_units(dk_ref, dk, nc)
        _store_units(dv_ref, dv, nc)
        dbraw = dbeta * beta * (1.0 - beta)
        dzc = dgraw * _sigmoid(c["zc"])
        ga = dgraw * c["graw"]
        lane = c["lane"]
        lane1 = lax.broadcasted_iota(jnp.int32, (1, 128), 1)
        neg_ea = -jnp.exp(avec)
        d_alog = jnp.zeros((1, 128), F32)
        d_dt = jnp.zeros((1, 128), F32)
        for ci in range(nc):
            dbd = jnp.zeros((C, 128), F32)
            for h in range(DN_HEADS):
                b = ci * DN_HEADS + h
                dz = dzc[b] * neg_ea
                dbd = dbd + jnp.where(lane == h, dbraw[b], 0.0) + jnp.where(lane == DN_HEADS + h, dz, 0.0)
                d_alog = d_alog + jnp.where(lane1 == DN_HEADS + h, jnp.sum(ga[b], axis=0, keepdims=True), 0.0)
                d_dt = d_dt + jnp.where(lane1 == DN_HEADS + h, jnp.sum(dz, axis=0, keepdims=True), 0.0)
            dbd_ref[ci * C:(ci + 1) * C, :] = dbd
        small_ref[...] += jnp.concatenate([d_alog, d_dt, jnp.zeros((6, 128), F32)], axis=0)

    tok = lambda wd: pl.BlockSpec((nc * C, wd), lambda n: (n, 0))
    big = pl.BlockSpec((nc * DN_WIDTH, HD), lambda n: (n, 0))
    sq = pl.BlockSpec((DN_HEADS, nc * C, C), lambda n: (0, n, 0))
    vec = pl.BlockSpec((1, 128), lambda n: (0, 0))
    return _call(
        body, (qn, kn, v, bd, avec, dvec, t_inv, v_new_all, states, dstates, do_all, dvn_all),
        name="dn_post", grid=(N // nc,), comm=comm,
        in_specs=[tok(DN_WIDTH)] * 3 + [tok(128), vec, vec, sq, tok(DN_WIDTH), big, big, tok(DN_WIDTH), tok(DN_WIDTH)],
        out_specs=[tok(DN_WIDTH)] * 3 + [tok(128), pl.BlockSpec((8, 128), lambda n: (0, 0))],
        out_shape=[jax.ShapeDtypeStruct((S, DN_WIDTH), F32)] * 3 + [jax.ShapeDtypeStruct((S, 128), F32),
                                                                  jax.ShapeDtypeStruct((8, 128), F32)])


def _outproj_fwd(x, attn, dn, w_out):
    S, D = x.shape
    tm = 512

    def body(x_ref, a_ref, d_ref, w_ref, xo_ref, mix_ref):
        a = a_ref[...].astype(BF16)
        dd = d_ref[...].astype(BF16)
        mix_ref[:, 0:ATTN_WIDTH] = a
        mix_ref[:, ATTN_WIDTH:] = dd
        xo_ref[...] = x_ref[...] + _nn(a, w_ref[0:ATTN_WIDTH, :]) + _nn(dd, w_ref[ATTN_WIDTH:, :])

    tok = lambda w: pl.BlockSpec((tm, w), lambda i: (i, 0))
    return pl.pallas_call(
        body, name="outproj_fwd", grid=(S // tm,),
        in_specs=[tok(D), tok(ATTN_WIDTH), tok(DN_WIDTH), pl.BlockSpec((D, D), lambda i: (0, 0))],
        out_specs=[tok(D), tok(D)],
        out_shape=[jax.ShapeDtypeStruct((S, D), F32), jax.ShapeDtypeStruct((S, D), BF16)],
        compiler_params=_params(1),
    )(x, attn, dn, w_out)


def _outproj_bwd(dx, w_out, attn):
    S, D = dx.shape
    tm = VIEW_TILE

    def body(dx_ref, w_ref, attn_ref, da1, da4, da16, dl1, dl4, dl16, ddn_ref, dxb_ref, planes):
        d = dx_ref[...].astype(BF16)
        dxb_ref[...] = d
        da = _nt(d, w_ref[0:ATTN_WIDTH, :])
        ddn_ref[...] = _nt(d, w_ref[ATTN_WIDTH:, :])
        _tile_to_views(da, planes, (da1, da4, da16))
        lo = lax.broadcasted_iota(jnp.int32, (tm, 128), 1) < 64
        cols = []
        for G in range(4):
            sl = slice(G * 128, (G + 1) * 128)
            t = da[:, sl] * attn_ref[:, sl]
            d0 = jnp.sum(jnp.where(lo, t, 0.0), axis=-1, keepdims=True)
            d1 = jnp.sum(jnp.where(lo, 0.0, t), axis=-1, keepdims=True)
            cols.append(jnp.where(lo, d0, d1))
        _tile_to_views(jnp.concatenate(cols, axis=1), planes, (dl1, dl4, dl16))

    tok = lambda w: pl.BlockSpec((tm, w), lambda i: (i, 0))
    views = [_view_spec(d) for d in DILATIONS]
    return pl.pallas_call(
        body, name="outproj_bwd", grid=(S // tm,),
        in_specs=[tok(D), pl.BlockSpec((D, D), lambda i: (0, 0)), tok(ATTN_WIDTH)],
        out_specs=views + views + [tok(DN_WIDTH), tok(D)],
        out_shape=[_view_shape(S, d, F32) for d in DILATIONS] * 2
                  + [jax.ShapeDtypeStruct((S, DN_WIDTH), F32), jax.ShapeDtypeStruct((S, D), BF16)],
        scratch_shapes=[pltpu.VMEM((4, tm, 128), F32)],
        compiler_params=_params(1),
    )(dx, w_out, attn)


def _loss_head(x, gain, target):
    S, D = x.shape
    tm = 512

    def body(x_ref, gain_ref, t_ref, loss_ref, dx_ref, dgain_ref):
        @pl.when(pl.program_id(0) == 0)
        def _():
            loss_ref[...] = jnp.zeros_like(loss_ref)
            dgain_ref[...] = jnp.zeros_like(dgain_ref)

        xf = x_ref[...]
        gain = gain_ref[...]
        r = lax.rsqrt(jnp.mean(xf * xf, axis=-1, keepdims=True) + NORM_EPS)
        xhat = xf * r
        err = xhat * gain - t_ref[...]
        part = 0.5 * jnp.sum(jnp.mean(err * err, axis=-1, keepdims=True), axis=0, keepdims=True)
        first = (lax.broadcasted_iota(jnp.int32, (8, 128), 0) == 0) & (lax.broadcasted_iota(jnp.int32, (8, 128), 1) == 0)
        loss_ref[...] += jnp.where(first, part, 0.0)
        dy = err * (1.0 / D)
        dgain_ref[...] += jnp.sum(dy * xhat, axis=0, keepdims=True)
        dxh = dy * gain
        dx_ref[...] = r * (dxh - xhat * jnp.mean(dxh * xhat, axis=-1, keepdims=True))

    tok = pl.BlockSpec((tm, D), lambda i: (i, 0))
    row = pl.BlockSpec((1, D), lambda i: (0, 0))
    return pl.pallas_call(
        body, name="loss_head", grid=(S // tm,),
        in_specs=[tok, row, tok],
        out_specs=[pl.BlockSpec((8, 128), lambda i: (0, 0)), tok, row],
        out_shape=[jax.ShapeDtypeStruct((8, 128), F32), jax.ShapeDtypeStruct((S, D), F32),
                   jax.ShapeDtypeStruct((1, D), F32)],
        compiler_params=_params(1),
    )(x, gain, target)


def _adamw(w, g, m, v, name):
    R, Ccols = w.shape[0], w.shape[-1]
    tr = next((t for t in range(512, 7, -8) if R % t == 0), R)
    c1 = 1.0 - ADAM_B1 ** ADAM_STEP
    c2 = 1.0 - ADAM_B2 ** ADAM_STEP

    def body(w_ref, g_ref, m_ref, v_ref, d_ref, nm_ref, nv_ref):
        gv = g_ref[...]
        mn = ADAM_B1 * m_ref[...] + (1.0 - ADAM_B1) * gv
        vn = ADAM_B2 * v_ref[...] + (1.0 - ADAM_B2) * (gv * gv)
        nm_ref[...] = mn
        nv_ref[...] = vn
        d_ref[...] = -ADAM_LR * ((mn / c1) / (jnp.sqrt(vn / c2) + ADAM_EPS) + ADAM_WD * w_ref[...])

    if w.ndim == 2:
        grid, spec = (R // tr,), pl.BlockSpec((tr, Ccols), lambda i: (i, 0))
    else:
        grid, spec = (2,), pl.BlockSpec((R // 2, 1, Ccols), lambda i: (i, 0, 0))
    return pl.pallas_call(
        body, name=name, grid=grid, in_specs=[spec] * 4, out_specs=[spec] * 3,
        out_shape=[jax.ShapeDtypeStruct(w.shape, F32)] * 3, compiler_params=_params(1),
    )(w, g, m, v)


LATE_WEIGHTS = ("w_out", "ffn2_gate", "ffn2_up", "ffn2_down")


def _local_step(x, target, wts, small, dist=None):
    g1, g2, gm, gf = small["norm_ffn1"], small["norm_ffn2"], small["norm_mix"], small["norm_final"]
    wts = dict(wts)

    def reduce_start(gs, tag):
        return _rs_add_pairs(gs, _swap_sibling(gs, True, "rs_swap_halves_" + tag), dist["c"], "rs_add_pairs_" + tag)

    (x1, h1, fg1, fu1), late = _ffn_fwd(x, g1, wts["ffn1_gate"], wts["ffn1_up"], wts["ffn1_down"], "ffn1_fwd",
                                        comm=_ag_comm(dist["late"]) if dist else None)
    if dist:
        wts.update(zip(LATE_WEIGHTS, late))
        wts["w_out"] = wts["w_out"].reshape(D_MODEL, D_MODEL)
    h2, *qkv, xq, xk, xv, gate, bd = _inproj_fwd(x1, gm, wts["w_in"])
    aq, ak, av = qkv[0:3], qkv[3:6], qkv[6:9]
    parts = [_attn_fwd(aq[p], ak[p], av[p], d, f"attn_fwd_d{d}") for p, d in enumerate(DILATIONS)]
    attn, *lse = _attn_merge(parts)
    conv_w = small["conv_w"]
    qn, kn, vv = _conv_fwd(xq, xk, xv, conv_w)
    dn_u, dn_w, dn_qg, dn_kd, dn_aq, dn_t, dn_egl = _dn_prep(qn, kn, vv, bd, small["avec"], small["dvec"])
    dn, o_dn, v_new, states = _dn_scan_fwd(dn_u, dn_w, dn_qg, dn_kd, dn_aq, dn_egl, gate, small["dn_norm"])
    x2, mix = _outproj_fwd(x1, attn, dn, wts["w_out"])
    (x3, h3, fg2, fu2), _ = _ffn_fwd(x2, g2, wts["ffn2_gate"], wts["ffn2_up"], wts["ffn2_down"], "ffn2_fwd")
    loss, dx3, d_gf = _loss_head(x3, gf, target)

    grads = {}
    (dx2, d_g2, dfg2, dfu2, act2, dout2), _ = _ffn_bwd(dx3, x2, g2, fg2, fu2, wts["ffn2_down"], wts["ffn2_gate"],
                                                      wts["ffn2_up"], "ffn2_bwd")
    tk = 2048
    grads["ffn2_gate"], _ = _dw_chunks(dfg2, h3, tk, "dw_ffn2_gate")
    grads["ffn2_up"], _ = _dw_chunks(dfu2, h3, tk, "dw_ffn2_up")
    grads["ffn2_down"], _ = _dw_chunks(act2, dout2, tk, "dw_ffn2_down")
    group_a = ("ffn2_gate", "ffn2_up", "ffn2_down")
    parts_a = reduce_start([grads[n] for n in group_a], "a") if dist else None

    *dviews, ddn, dx2b = _outproj_bwd(dx2, wts["w_out"], attn)
    dattn, dd = dviews[0:3], dviews[3:6]
    grads["w_out"] = _matmul_tn(mix, dx2b, D_MODEL, tk, "dw_out").reshape(N_CHIPS, D_MODEL // N_CHIPS, D_MODEL)

    daq, dak, dav = [], [], []
    for p, d in enumerate(DILATIONS):
        daq.append(_attn_bwd_q(aq[p], ak[p], av[p], dattn[p], lse[p], dd[p], d, f"attn_bwd_q_d{d}"))
        dk_p, dv_p = _attn_bwd_kv(aq[p], ak[p], av[p], dattn[p], lse[p], dd[p], d, f"attn_bwd_kv_d{d}")
        dak.append(dk_p)
        dav.append(dv_p)

    do_dn, dvn, dgate, dstates, d_dn_gain = _dn_scan_bwd(dn_w, dn_qg, dn_kd, dn_aq, dn_egl, gate, small["dn_norm"], o_dn, ddn)
    (dqn, dkn, dvv, dbd, dn_small), recv_a = _dn_post(qn, kn, vv, bd, small["avec"], small["dvec"], dn_t, v_new, states,
                                                      dstates, do_dn, dvn, comm=_rsx_comm(parts_a) if dist else None)
    dcq, dck, dcv, dwq, dwk, dwv = _conv_bwd_pre(xq, xk, xv, conv_w, dqn, dkn, dvv)
    dxq, dxk, dxv = _conv_bwd_x(dcq, dck, dcv, conv_w)
    d_conv = jnp.concatenate([dwq[:CONV_WIDTH], dwk[:CONV_WIDTH], dwv[:CONV_WIDTH]], axis=1)

    dx1, d_gm, dproj = _inproj_bwd(dx2, x1, gm, [daq, dak, dav], [dxq, dxk, dxv, dgate], dbd, wts["w_in"])
    gi = _matmul_tn(dproj, h2, IN_COLS_PADDED, 512, "dw_in")
    if dist:
        gate_end = QKV_COLS + DN_WIDTH
        gi = jnp.concatenate([gi[:QKV_COLS], gi[gate_end:gate_end + LOGIT_COLS], gi[QKV_COLS:gate_end]], axis=0)
        gi = gi.reshape(N_CHIPS, IN_COLS // N_CHIPS, D_MODEL)
        gi = jnp.pad(gi, ((0, 0), (0, W_IN_ROWS - IN_COLS // N_CHIPS), (0, 0)))
    grads["w_in"] = gi
    group_b = ("w_in", "w_out")
    parts_b = reduce_start([grads[n] for n in group_b], "b") if dist else None

    (dx0, d_g1, dfg1, dfu1, act1, dout1), _ = _ffn_bwd(dx1, x, g1, fg1, fu1, wts["ffn1_down"], wts["ffn1_gate"],
                                                      wts["ffn1_up"], "ffn1_bwd")
    group_c = ("ffn1_gate", "ffn1_up", "ffn1_down")
    pending = parts_b if dist else []
    parts_c, recv_bc = [], []
    for n, (lhs, rhs) in zip(group_c, ((dfg1, h1), (dfu1, h1), (act1, dout1))):
        grads[n], landed = _dw_chunks(lhs, rhs, tk, "dw_" + n, comm=_rsx_comm(pending) if dist else None)
        recv_bc += list(landed)
        if dist:
            pending = reduce_start([grads[n]], n)
            parts_c += pending

    small_grads = dict(norm_ffn1=d_g1, norm_mix=d_gm, norm_ffn2=d_g2, norm_final=d_gf, conv_w=d_conv,
                       a_log=dn_small[0:1], dt_bias=dn_small[1:2], dn_norm=d_dn_gain[0:1])
    if dist:
        recv_bc += _rs_exchange_arrays(pending)
        recv_b, recv_c = recv_bc[:len(parts_b)], recv_bc[len(parts_b):]
        names = group_a + group_b + group_c
        totals = _rs_add_totals(list(parts_a) + list(parts_b) + list(parts_c), list(recv_a) + list(recv_b) + list(recv_c),
                                dist["chip"])
        theirs = _swap_sibling(totals, False, "rs_share_total")
        grads = {n: (mine, other) for n, mine, other in zip(names, totals, theirs)}
    return loss, dx0, grads, small_grads


HBM =pl.BlockSpec(memory_space=pl.ANY)
VMEM_SPEC = pl.BlockSpec(memory_space=pltpu.VMEM)


def _coords():
    return lax.axis_index("x"), lax.axis_index("y"), lax.axis_index("c")


def _remote(src, dst, send_sems, recv_sems, k, dev):
    return pltpu.make_async_remote_copy(src_ref=src, dst_ref=dst, send_sem=send_sems.at[k], recv_sem=recv_sems.at[k],
                                        device_id=dev, device_id_type=MESH)


def _allreduce_small(buf, name):
    R, Cc = buf.shape

    def body(src_ref, out_ref, recv_ref, send_sems, recv_sems):
        x, y, c = _coords()
        copies = []
        for m in range(1, 8):
            fx, fy, fc = (m >> 2) & 1, (m >> 1) & 1, m & 1
            dev = (x ^ fx if fx else x, y ^ fy if fy else y, c ^ fc if fc else c)
            cp = _remote(src_ref, recv_ref.at[m - 1], send_sems, recv_sems, m - 1, dev)
            cp.start()
            copies.append(cp)
        for cp in copies:
            cp.wait()
        r = [src_ref[...]] + [recv_ref[m] for m in range(7)]
        out_ref[...] = ((r[0] + r[1]) + (r[2] + r[3])) + ((r[4] + r[5]) + (r[6] + r[7]))

    return pl.pallas_call(
        body, name=name, out_shape=jax.ShapeDtypeStruct((R, Cc), F32),
        in_specs=[VMEM_SPEC], out_specs=VMEM_SPEC,
        scratch_shapes=[pltpu.VMEM((7, R, Cc), F32), pltpu.SemaphoreType.DMA((7,)), pltpu.SemaphoreType.DMA((7,))],
    )(buf)


BIG = ("ffn1_gate", "ffn1_up", "ffn1_down", "w_in", "w_out", "ffn2_gate", "ffn2_up", "ffn2_down")
ROW_SHARDED = ("ffn1_down", "w_out", "ffn2_down")
W_IN_ROWS = 960


def _rows(ref, start, size):
    return ref.at[pl.ds(pl.multiple_of(start, 16), size)]


def _allgather_arrays(shards):
    n = len(shards)

    def body(*refs):
        srcs, outs, send_sems, recv_sems = refs[:n], refs[n:2 * n], refs[2 * n], refs[2 * n + 1]
        x, y, c = _coords()
        sib = (x, y, 1 - c)
        xn, yn, dg = (1 - x, y), (x, 1 - y), (1 - x, 1 - y)
        slot = lambda out, chip: out.at[2 * chip[0] + chip[1]]
        started = []

        def go(cp):
            cp.start()
            started.append(cp)

        for a, (src, out) in enumerate(zip(srcs, outs)):
            h = src.shape[0] // 2
            cp = lambda s, d, k, dev: _remote(s, d, send_sems, recv_sems, 8 * a + k, dev)
            go(cp(src, slot(out, (x, y)), 6, sib))
            mine, dst = _rows(src, c * h, h), _rows(slot(out, (x, y)), c * h, h)
            go(cp(mine, dst, 0, (*xn, c)))
            go(cp(mine, dst, 1, (*yn, c)))
        for a, (src, out) in enumerate(zip(srcs, outs)):
            h = src.shape[0] // 2
            q = h // 2
            cp = lambda s, d, k, dev: _remote(s, d, send_sems, recv_sems, 8 * a + k, dev)
            from_x, from_y = _rows(slot(out, xn), c * h, h), _rows(slot(out, yn), c * h, h)
            cp(from_x, from_x, 0, sib).wait_recv()
            first = _rows(slot(out, xn), c * h, q)
            go(cp(first, first, 2, (*yn, c)))
            go(cp(from_x, from_x, 3, sib))
            cp(from_y, from_y, 1, sib).wait_recv()
            second = _rows(slot(out, yn), c * h + q, q)
            go(cp(second, second, 7, (*xn, c)))
            go(cp(from_y, from_y, 4, sib))
        for a, (src, out) in enumerate(zip(srcs, outs)):
            h = src.shape[0] // 2
            q = h // 2
            cp = lambda s, d, k, dev: _remote(s, d, send_sems, recv_sems, 8 * a + k, dev)
            first, second = _rows(slot(out, dg), c * h, q), _rows(slot(out, dg), c * h + q, q)
            cp(first, first, 2, sib).wait_recv()
            cp(second, second, 7, sib).wait_recv()
            from_d = _rows(slot(out, dg), c * h, h)
            go(cp(from_d, from_d, 5, sib))
        for a, (src, out) in enumerate(zip(srcs, outs)):
            h = src.shape[0] // 2
            cp = lambda s, d, k, dev: _remote(s, d, send_sems, recv_sems, 8 * a + k, dev)
            for k, chip in ((3, xn), (4, yn), (5, dg)):
                theirs = _rows(slot(out, chip), (1 - c) * h, h)
                cp(theirs, theirs, k, sib).wait_recv()
            cp(src, slot(out, (x, y)), 6, sib).wait_recv()
        for cp in started:
            cp.wait_send()

    shapes = [jax.ShapeDtypeStruct((N_CHIPS,) + s.shape, s.dtype) for s in shards]
    return pl.pallas_call(
        body, name="allgather_weights", out_shape=shapes, in_specs=[HBM] * n, out_specs=[HBM] * n,
        scratch_shapes=[pltpu.SemaphoreType.DMA((8 * n,)), pltpu.SemaphoreType.DMA((8 * n,))],
    )(*shards)


def _ag_copies(srcs, outs, send_sems, recv_sems):
    x, y, c = _coords()
    sib = (x, y, 1 - c)
    me = 2 * x + y
    others = [(1 - x, y), (x, 1 - y), (1 - x, 1 - y)]
    plan = []
    for a, (src, out) in enumerate(zip(srcs, outs)):
        h = src.shape[0] // 2
        cp = lambda s, d, k, dev: _remote(s, d, send_sems, recv_sems, 7 * a + k, dev)
        own = cp(src, out.at[me], 6, sib)
        sends = [cp(_rows(src, c * h, h), _rows(out.at[me], c * h, h), j, (ox, oy, c)) for j, (ox, oy) in enumerate(others)]
        mine = [_rows(out.at[2 * ox + oy], c * h, h) for ox, oy in others]
        theirs = [_rows(out.at[2 * ox + oy], (1 - c) * h, h) for ox, oy in others]
        arrivals = [cp(m, m, j, sib) for j, m in enumerate(mine)]
        forwards = [cp(m, m, 3 + j, sib) for j, m in enumerate(mine)]
        forwarded = [cp(t, t, 3 + j, sib) for j, t in enumerate(theirs)]
        plan.append((own, sends, forwards, arrivals, forwarded))
    return plan


def _ag_start(srcs, outs, send_sems, recv_sems):
    for own, sends, _, _, _ in _ag_copies(srcs, outs, send_sems, recv_sems):
        own.start()
        for cp in sends:
            cp.start()


def _ag_finish(srcs, outs, send_sems, recv_sems):
    plan = _ag_copies(srcs, outs, send_sems, recv_sems)
    for _, _, forwards, arrivals, _ in plan:
        for arrived, fwd in zip(arrivals, forwards):
            arrived.wait_recv()
            fwd.start()
    for own, sends, forwards, _, forwarded in plan:
        for cp in forwarded:
            cp.wait_recv()
        own.wait_recv()
        for cp in [own] + sends + forwards:
            cp.wait_send()


def _ag_comm(shards):
    shapes = [jax.ShapeDtypeStruct((N_CHIPS,) + s.shape, s.dtype) for s in shards]
    return (list(shards), shapes, 7 * len(shards), _ag_start, _ag_finish)


def _swap_sibling(arrs, pick_other_half, name):
    n = len(arrs)
    outs = [jax.ShapeDtypeStruct((a.shape[0], a.shape[1] // 2) + a.shape[2:] if pick_other_half else a.shape, a.dtype) for a in arrs]

    def body(*refs):
        srcs, dsts, send_sems, recv_sems = refs[:n], refs[n:2 * n], refs[2 * n], refs[2 * n + 1]
        x, y, c = _coords()
        cps = []
        for a in range(n):
            src = srcs[a]
            if pick_other_half:
                h = src.shape[1] // 2
                src = src.at[:, pl.ds(pl.multiple_of((1 - c) * h, 16), h)]
            cp = _remote(src, dsts[a], send_sems, recv_sems, a, (x, y, 1 - c))
            cp.start()
            cps.append(cp)
        for cp in cps:
            cp.wait()

    return pl.pallas_call(
        body, name=name, out_shape=outs, in_specs=[HBM] * n, out_specs=[HBM] * n,
        scratch_shapes=[pltpu.SemaphoreType.DMA((n,)), pltpu.SemaphoreType.DMA((n,))],
    )(*arrs)


def _rs_add_pairs(gs, others, c, name):
    n = len(gs)
    blocks = [(g.shape[1] // 4, g.shape[2]) for g in gs]

    def body(c_ref, *refs):
        for a in range(n):
            refs[2 * n + a][...] = (refs[a][...] + refs[n + a][...]).astype(BF16)

    mine = lambda b: pl.BlockSpec((None,) + b, lambda j, s, c_ref: (j, c_ref[0] * 2 + s, 0))
    flat = lambda b: pl.BlockSpec((None,) + b, lambda j, s, c_ref: (j, s, 0))
    return pl.pallas_call(
        body, name=name,
        grid_spec=pltpu.PrefetchScalarGridSpec(
            num_scalar_prefetch=1, grid=(N_CHIPS, 2),
            in_specs=[mine(b) for b in blocks] + [flat(b) for b in blocks],
            out_specs=[flat(b) for b in blocks]),
        out_shape=[jax.ShapeDtypeStruct(o.shape, BF16) for o in others],
        compiler_params=_params(2),
    )(c, *gs, *others)


def _rs_exchange_arrays(parts):
    n = len(parts)

    def body(*refs):
        _rsx_start(refs[:n], refs[n:2 * n], refs[2 * n], refs[2 * n + 1])
        _rsx_finish(refs[:n], refs[n:2 * n], refs[2 * n], refs[2 * n + 1])

    _, shapes, n_sems, _, _ = _rsx_comm(parts)
    return pl.pallas_call(
        body, name="rs_exchange_chips", out_shape=shapes, in_specs=[HBM] * n, out_specs=[HBM] * n,
        scratch_shapes=[pltpu.SemaphoreType.DMA((n_sems,)), pltpu.SemaphoreType.DMA((n_sems,))],
    )(*parts)


def _rsx_copies(srcs, dsts, send_sems, recv_sems):
    x, y, c = _coords()
    others = [(1 - x, y), (x, 1 - y), (1 - x, 1 - y)]
    return [_remote(src.at[2 * ox + oy], dst.at[k], send_sems, recv_sems, 3 * a + k, (ox, oy, c))
            for a, (src, dst) in enumerate(zip(srcs, dsts)) for k, (ox, oy) in enumerate(others)]


def _rsx_start(srcs, dsts, send_sems, recv_sems):
    for cp in _rsx_copies(srcs, dsts, send_sems, recv_sems):
        cp.start()


def _rsx_finish(srcs, dsts, send_sems, recv_sems):
    for cp in _rsx_copies(srcs, dsts, send_sems, recv_sems):
        cp.wait()


def _rsx_comm(parts):
    shapes = [jax.ShapeDtypeStruct((3,) + p.shape[1:], p.dtype) for p in parts]
    return (list(parts), shapes, 3 * len(parts), _rsx_start, _rsx_finish)


def _rs_add_totals(parts, recvs, chip):
    n = len(parts)
    blocks = [(p.shape[1] // 2, p.shape[2]) for p in parts]

    def body(chip_ref, *refs):
        f = lambda r: r[...].astype(F32)
        for a in range(n):
            p, r0, r1, r2 = refs[a], refs[n + 3 * a], refs[n + 3 * a + 1], refs[n + 3 * a + 2]
            refs[4 * n + a][...] = (f(p) + f(r0)) + (f(r1) + f(r2))

    own = lambda b: pl.BlockSpec((None,) + b, lambda s, chip_ref: (chip_ref[0], s, 0))
    slot = lambda b, k: pl.BlockSpec((None,) + b, lambda s, chip_ref, k=k: (k, s, 0))
    recv_specs = [slot(b, k) for b in blocks for k in range(3)]
    recv_args = [r for r in recvs for _ in range(3)]
    return pl.pallas_call(
        body, name="rs_add_totals",
        grid_spec=pltpu.PrefetchScalarGridSpec(
            num_scalar_prefetch=1, grid=(2,),
            in_specs=[own(b) for b in blocks] + recv_specs,
            out_specs=[pl.BlockSpec(b, lambda s, chip_ref: (s, 0)) for b in blocks]),
        out_shape=[jax.ShapeDtypeStruct(p.shape[1:], F32) for p in parts],
        compiler_params=_params(1),
    )(chip, *parts, *recv_args)


def _permute_w_in(wt):
    return jnp.concatenate([wt[:QKV_COLS], wt[QKV_COLS + LOGIT_COLS:IN_COLS], wt[QKV_COLS:QKV_COLS + LOGIT_COLS],
                            jnp.zeros((IN_COLS_PADDED - IN_COLS, wt.shape[1]), wt.dtype)], axis=0)


def _pad_row(v):
    v = v.reshape(1, -1)
    return jnp.pad(v, ((0, 0), (0, D_MODEL - v.shape[1])))


def kernel(x, norm_ffn1, ffn1_gate, ffn1_up, ffn1_down, norm_mix, w_in, conv_w, a_log, dt_bias, dn_norm, w_out, norm_ffn2, ffn2_gate, ffn2_up, ffn2_down, norm_final, loss_target, m_norm_ffn1, m_ffn1_gate, m_ffn1_up, m_ffn1_down, m_norm_mix, m_w_in, m_conv_w, m_a_log, m_dt_bias, m_dn_norm, m_w_out, m_norm_ffn2, m_ffn2_gate, m_ffn2_up, m_ffn2_down, m_norm_final, v_norm_ffn1, v_ffn1_gate, v_ffn1_up, v_ffn1_down, v_norm_mix, v_w_in, v_conv_w, v_a_log, v_dt_bias, v_dn_norm, v_w_out, v_norm_ffn2, v_ffn2_gate, v_ffn2_up, v_ffn2_down, v_norm_final):
    cx, cy, cc = _coords()
    chip = 2 * cx + cy
    stored = lambda t, n: t[0] if n in ROW_SHARDED else t[0].T
    big_w = {n: stored(t, n) for n, t in dict(
        ffn1_gate=ffn1_gate, ffn1_up=ffn1_up, ffn1_down=ffn1_down, w_in=w_in, w_out=w_out,
        ffn2_gate=ffn2_gate, ffn2_up=ffn2_up, ffn2_down=ffn2_down).items()}
    big_m = {n: stored(t, n) for n, t in dict(
        ffn1_gate=m_ffn1_gate, ffn1_up=m_ffn1_up, ffn1_down=m_ffn1_down, w_in=m_w_in, w_out=m_w_out,
        ffn2_gate=m_ffn2_gate, ffn2_up=m_ffn2_up, ffn2_down=m_ffn2_down).items()}
    big_v = {n: stored(t, n) for n, t in dict(
        ffn1_gate=v_ffn1_gate, ffn1_up=v_ffn1_up, ffn1_down=v_ffn1_down, w_in=v_w_in, w_out=v_w_out,
        ffn2_gate=v_ffn2_gate, ffn2_up=v_ffn2_up, ffn2_down=v_ffn2_down).items()}

    cols = IN_COLS // N_CHIPS
    send = {n: big_w[n].astype(BF16) for n in BIG}
    send["w_in"] = jnp.pad(send["w_in"], ((0, W_IN_ROWS - cols), (0, 0)))
    early = tuple(n for n in BIG if n not in LATE_WEIGHTS)
    wts = dict(zip(early, _allgather_arrays([send[n] for n in early])))
    wts["w_in"] = _permute_w_in(wts["w_in"][:, :cols].reshape(IN_COLS, D_MODEL))
    dist = dict(late=[send[n] for n in LATE_WEIGHTS], c=cc.reshape(1).astype(jnp.int32),
                chip=chip.reshape(1).astype(jnp.int32))

    conv_shard = conv_w[0]
    emb = jnp.concatenate([jnp.where((chip == j) & (cc == 0), conv_shard, 0.0) for j in range(N_CHIPS)], axis=1)
    emb = jnp.pad(emb.reshape(6, D_MODEL), ((0, 2), (0, 0)))
    conv_full = _allreduce_small(emb, "allgather_conv_w")[:6].reshape(CONV_WIDTH, 3 * DN_WIDTH)

    zvec = jnp.zeros((1, 128), F32)
    small = dict(norm_ffn1=norm_ffn1, norm_mix=norm_mix, norm_ffn2=norm_ffn2, norm_final=norm_final[None],
                 conv_w=conv_full, avec=zvec.at[0, DN_HEADS:2 * DN_HEADS].set(a_log[0]),
                 dvec=zvec.at[0, DN_HEADS:2 * DN_HEADS].set(dt_bias[0]), dn_norm=dn_norm)

    loss, grad_x, reduced, sg = _local_step(x[0], loss_target[0], wts, small, dist)

    rows = [sg["norm_ffn1"], sg["norm_mix"], sg["norm_ffn2"], sg["norm_final"], _pad_row(sg["a_log"]), _pad_row(sg["dt_bias"]),
            _pad_row(sg["dn_norm"]), _pad_row(loss[0:1]), sg["conv_w"].reshape(6, D_MODEL), jnp.zeros((2, D_MODEL), F32)]
    red = _allreduce_small(jnp.concatenate(rows, axis=0), "allreduce_small")
    loss_out = red[7, 0]
    g_conv_full = red[8:14].reshape(CONV_WIDTH, 3 * DN_WIDTH)
    g_conv = lax.dynamic_slice_in_dim(g_conv_full, chip * (3 * DN_WIDTH // N_CHIPS), 3 * DN_WIDTH // N_CHIPS, axis=1)
    g_small = dict(norm_ffn1=red[0:1], norm_mix=red[1:2], norm_ffn2=red[2:3], norm_final=red[3],
                   a_log=red[4:5, DN_HEADS:2 * DN_HEADS], dt_bias=red[5:6, DN_HEADS:2 * DN_HEADS], dn_norm=red[6:7, :DN_HEAD_DIM])

    out_g, out_d, out_m, out_v = {}, {}, {}, {}
    for n in BIG:
        mine, other = reduced[n]
        g = jnp.where(cc == 0, jnp.concatenate([mine, other], axis=0), jnp.concatenate([other, mine], axis=0))
        if n == "w_in":
            to3 = lambda t: jnp.transpose(t, (2, 0, 1))
            g = g[:cols].reshape(cols, 1, D_MODEL)
            results = (g,) + tuple(_adamw(to3(w_in), g, to3(m_w_in), to3(v_w_in), "adamw_w_in"))
            out_g[n], out_d[n], out_m[n], out_v[n] = (jnp.transpose(t, (1, 2, 0)) for t in results)
            continue
        results = (g,) + tuple(_adamw(big_w[n], g, big_m[n], big_v[n], "adamw_" + n))
        out_g[n], out_d[n], out_m[n], out_v[n] = ((t if n in ROW_SHARDED else t.T)[None] for t in results)
    d, nm, nv = _adamw(conv_w[0], g_conv, m_conv_w[0], v_conv_w[0], "adamw_conv_w")
    out_g["conv_w"], out_d["conv_w"], out_m["conv_w"], out_v["conv_w"] = g_conv[None], d[None], nm[None], nv[None]

    small_names = ("norm_ffn1", "norm_mix", "norm_ffn2", "norm_final", "a_log", "dt_bias", "dn_norm")
    small_w = dict(norm_ffn1=norm_ffn1, norm_mix=norm_mix, norm_ffn2=norm_ffn2, norm_final=norm_final, a_log=a_log,
                   dt_bias=dt_bias, dn_norm=dn_norm)
    small_m = dict(norm_ffn1=m_norm_ffn1, norm_mix=m_norm_mix, norm_ffn2=m_norm_ffn2, norm_final=m_norm_final, a_log=m_a_log,
                   dt_bias=m_dt_bias, dn_norm=m_dn_norm)
    small_v = dict(norm_ffn1=v_norm_ffn1, norm_mix=v_norm_mix, norm_ffn2=v_norm_ffn2, norm_final=v_norm_final, a_log=v_a_log,
                   dt_bias=v_dt_bias, dn_norm=v_dn_norm)
    stack = lambda dct: jnp.concatenate([_pad_row(dct[n]) for n in small_names] + [jnp.zeros((1, D_MODEL), F32)], axis=0)
    d, nm, nv = _adamw(stack(small_w), stack(g_small), stack(small_m), stack(small_v), "adamw_small")
    for k, n in enumerate(small_names):
        shape = small_w[n].shape
        size = math.prod(shape)
        out_g[n] = g_small[n].reshape(shape)
        out_d[n], out_m[n], out_v[n] = (t[k, :size].reshape(shape) for t in (d, nm, nv))

    order = ("norm_ffn1", "ffn1_gate", "ffn1_up", "ffn1_down", "norm_mix", "w_in", "conv_w", "a_log", "dt_bias", "dn_norm",
             "w_out", "norm_ffn2", "ffn2_gate", "ffn2_up", "ffn2_down", "norm_final")
    return (loss_out, grad_x[None], *[out_g[n] for n in order], *[out_d[n] for n in order],
            *[out_m[n] for n in order], *[out_v[n] for n in order])
```

```python
import functools
import math

import jax
import jax.numpy as jnp
from jax import lax
from jax.experimental import pallas as pl
from jax.experimental.pallas import tpu as pltpu

F32 = jnp.float32
BF16 = jnp.bfloat16
HI = lax.Precision.HIGH

D_MODEL = 1024
ATTN_HEADS = 8
ATTN_WIDTH = 512
ATTN_BLOCK = 128
ATTN_SCALE = (ATTN_WIDTH // ATTN_HEADS) ** -0.5
DILATIONS = (1, 4, 16)
DN_HEADS = 4
DN_HEAD_DIM = 128
DN_WIDTH = 512
DN_CHUNK = 64
CONV_WIDTH = 4
NORM_EPS = 1e-6
L2_EPS = 1e-6
QKV_COLS = 3 * ATTN_WIDTH + 3 * DN_WIDTH
LOGIT_COLS = 2 * DN_HEADS
IN_COLS = QKV_COLS + LOGIT_COLS + DN_WIDTH
IN_COLS_PADDED = 3712
N_CHIPS = 4

ADAM_LR = 0.001
ADAM_B1 = 0.9
ADAM_B2 = 0.999
ADAM_EPS = 1e-08
ADAM_WD = 0.01
ADAM_STEP = 10

VMEM_LIMIT = 56 * 1024 * 1024
NEG_BIG = -1e30
MESH = pl.DeviceIdType.MESH


def _params(n_grid, vmem=VMEM_LIMIT):
    return pltpu.CompilerParams(dimension_semantics=("arbitrary",) * n_grid, vmem_limit_bytes=vmem)


def _call(body, args, *, name, grid, in_specs, out_specs, out_shape, scratch_shapes=(), comm=None):
    n_in, n_out, n_scr = len(in_specs), len(out_specs), len(scratch_shapes)
    hbm = pl.BlockSpec(memory_space=pl.ANY)
    srcs, dst_shapes, n_sems, start, finish = comm if comm is not None else ((), (), 0, None, None)
    ns, nd = len(srcs), len(dst_shapes)

    def full(*refs):
        ins, c_src = refs[:n_in], refs[n_in:n_in + ns]
        at = n_in + ns
        outs, c_dst = refs[at:at + n_out], refs[at + n_out:at + n_out + nd]
        scr = refs[at + n_out + nd:at + n_out + nd + n_scr]
        if comm is not None:
            ids = [pl.program_id(a) for a in range(len(grid))]
            first = functools.reduce(jnp.logical_and, [i == 0 for i in ids])
            last = functools.reduce(jnp.logical_and, [i == g - 1 for i, g in zip(ids, grid)])

            @pl.when(first)
            def _():
                start(c_src, c_dst, refs[-2], refs[-1])

        body(*ins, *outs, *scr)
        if comm is not None:
            @pl.when(last)
            def _():
                finish(c_src, c_dst, refs[-2], refs[-1])

    sems = [pltpu.SemaphoreType.DMA((n_sems,)), pltpu.SemaphoreType.DMA((n_sems,))] if comm is not None else []
    res = pl.pallas_call(
        full, name=name, grid=grid, in_specs=list(in_specs) + [hbm] * ns, out_specs=list(out_specs) + [hbm] * nd,
        out_shape=list(out_shape) + list(dst_shapes), scratch_shapes=list(scratch_shapes) + sems,
        compiler_params=_params(len(grid)),
    )(*args, *srcs)
    return res[:n_out], res[n_out:]


def _nt(a, b, precision=None):
    return lax.dot_general(a, b, (((1,), (1,)), ((), ())), preferred_element_type=F32, precision=precision)


def _tn(a, b, precision=None):
    return lax.dot_general(a, b, (((0,), (0,)), ((), ())), preferred_element_type=F32, precision=precision)


def _nn(a, b, precision=None):
    return jnp.dot(a, b, preferred_element_type=F32, precision=precision)


def _sigmoid(x):
    return 1.0 / (1.0 + jnp.exp(-x))


def _loss_head(xf, gain, target):
    r = lax.rsqrt(jnp.mean(xf * xf, axis=-1, keepdims=True) + NORM_EPS)
    xhat = xf * r
    err = xhat * gain - target
    part = 0.5 * jnp.sum(jnp.mean(err * err, axis=-1, keepdims=True), axis=0, keepdims=True)
    dy = err * (1.0 / xf.shape[-1])
    dgain = jnp.sum(dy * xhat, axis=0, keepdims=True)
    dxh = dy * gain
    return part, r * (dxh - xhat * jnp.mean(dxh * xhat, axis=-1, keepdims=True)), dgain


def _ffn_fwd(x, gain, wg, wu, wd, name, comm=None, head=None):
    S, D = x.shape
    nf, tf, _ = wg.shape
    tm = 512
    n_in = 5 if head is None else 7

    def body(*refs):
        x_ref, gain_ref, wg_ref, wu_ref, wd_ref = refs[:5]
        xo_ref, h_ref, g_ref, u_ref = refs[n_in:n_in + 4]
        acc_ref, hs_ref = refs[-2:]
        i = pl.program_id(0)
        j = pl.program_id(1)

        @pl.when(j == 0)
        def _():
            xf = x_ref[...]
            r = lax.rsqrt(jnp.mean(xf * xf, axis=-1, keepdims=True) + NORM_EPS)
            h = (xf * r * gain_ref[...]).astype(BF16)
            hs_ref[...] = h
            h_ref[...] = h
            acc_ref[...] = jnp.zeros_like(acc_ref)

        h = hs_ref[...]
        g = _nt(h, wg_ref[...])
        u = _nt(h, wu_ref[...])
        g_ref[...] = g.astype(BF16)
        u_ref[...] = u.astype(BF16)
        act = g * _sigmoid(g) * u
        acc_ref[...] += _nn(act.astype(BF16), wd_ref[...])

        if head is not None:
            hgain_ref, t_ref = refs[5:7]
            loss_ref, dgain_ref = refs[n_in + 4:n_in + 6]

            @pl.when((i == 0) & (j == 0))
            def _():
                loss_ref[...] = jnp.zeros_like(loss_ref)
                dgain_ref[...] = jnp.zeros_like(dgain_ref)

        @pl.when(j == nf - 1)
        def _():
            xo = x_ref[...] + 0.5 * acc_ref[...]
            if head is None:
                xo_ref[...] = xo
            else:
                part, dxo, dgain = _loss_head(xo, hgain_ref[...], t_ref[...])
                first = ((lax.broadcasted_iota(jnp.int32, (8, 128), 0) == 0)
                         & (lax.broadcasted_iota(jnp.int32, (8, 128), 1) == 0))
                loss_ref[...] += jnp.where(first, part, 0.0)
                dgain_ref[...] += dgain
                xo_ref[...] = dxo

    tok = pl.BlockSpec((tm, D), lambda i, j: (i, 0))
    row = pl.BlockSpec((1, D), lambda i, j: (0, 0))
    chunk = pl.BlockSpec((None, tf, D), lambda i, j: (j, 0, 0))
    act = pl.BlockSpec((None, tm, tf), lambda i, j: (j, i, 0))
    extra_in = [] if head is None else [row, tok]
    extra_out = [] if head is None else [pl.BlockSpec((8, 128), lambda i, j: (0, 0)), row]
    extra_shape = [] if head is None else [jax.ShapeDtypeStruct((8, 128), F32), jax.ShapeDtypeStruct((1, D), F32)]
    return _call(
        body, (x, gain, wg, wu, wd) + (() if head is None else tuple(head)), name=name, grid=(S // tm, nf), comm=comm,
        in_specs=[tok, row, chunk, chunk, chunk] + extra_in,
        out_specs=[tok, tok, act, act] + extra_out,
        out_shape=[jax.ShapeDtypeStruct((S, D), F32), jax.ShapeDtypeStruct((S, D), BF16),
                   jax.ShapeDtypeStruct((nf, S, tf), BF16), jax.ShapeDtypeStruct((nf, S, tf), BF16)] + extra_shape,
        scratch_shapes=[pltpu.VMEM((tm, D), F32), pltpu.VMEM((tm, D), BF16)])


def _rmsnorm_bwd(dh, xf, gain):
    r = lax.rsqrt(jnp.mean(xf * xf, axis=-1, keepdims=True) + NORM_EPS)
    xhat = xf * r
    dgain = jnp.sum(dh * xhat, axis=0, keepdims=True)
    dxh = dh * gain
    dx = r * (dxh - xhat * jnp.mean(dxh * xhat, axis=-1, keepdims=True))
    return dx, dgain


def _ffn_bwd(dxo, x, gain, g, u, wd, wg, wu, name, comm=None):
    S, D = x.shape
    nf, _, tf = g.shape
    tm = 512

    def body(dxo_ref, x_ref, gain_ref, g_ref, u_ref, wd_ref, wg_ref, wu_ref,
             dx_ref, dgain_ref, dg_ref, du_ref, act_ref, dout_ref, acc_ref, ds_ref):
        i = pl.program_id(0)
        j = pl.program_id(1)

        @pl.when(j == 0)
        def _():
            d = (0.5 * dxo_ref[...]).astype(BF16)
            ds_ref[...] = d
            dout_ref[...] = d
            acc_ref[...] = jnp.zeros_like(acc_ref)

        @pl.when((i == 0) & (j == 0))
        def _():
            dgain_ref[...] = jnp.zeros_like(dgain_ref)

        for half in range(2):
            rows = slice(half * (tm // 2), (half + 1) * (tm // 2))
            dact = _nt(ds_ref[rows, :], wd_ref[...])
            gv = g_ref[rows, :].astype(F32)
            uv = u_ref[rows, :].astype(F32)
            sg = _sigmoid(gv)
            silu = gv * sg
            act_ref[rows, :] = (silu * uv).astype(BF16)
            dgv = (dact * uv * (sg * (1.0 + gv * (1.0 - sg)))).astype(BF16)
            duv = (dact * silu).astype(BF16)
            dg_ref[rows, :] = dgv
            du_ref[rows, :] = duv
            acc_ref[rows, :] += _nn(dgv, wg_ref[...]) + _nn(duv, wu_ref[...])

        @pl.when(j == nf - 1)
        def _():
            dx, dgain = _rmsnorm_bwd(acc_ref[...], x_ref[...], gain_ref[...])
            dx_ref[...] = dxo_ref[...] + dx
            dgain_ref[...] += dgain

    return _call(
        body, (dxo, x, gain, g, u, wd, wg, wu), name=name, grid=(S // tm, nf), comm=comm,
        in_specs=[pl.BlockSpec((tm, D), lambda i, j: (i, 0)),
                  pl.BlockSpec((tm, D), lambda i, j: (i, 0)),
                  pl.BlockSpec((1, D), lambda i, j: (0, 0)),
                  pl.BlockSpec((None, tm, tf), lambda i, j: (j, i, 0)),
                  pl.BlockSpec((None, tm, tf), lambda i, j: (j, i, 0)),
                  pl.BlockSpec((None, tf, D), lambda i, j: (j, 0, 0)),
                  pl.BlockSpec((None, tf, D), lambda i, j: (j, 0, 0)),
                  pl.BlockSpec((None, tf, D), lambda i, j: (j, 0, 0))],
        out_specs=[pl.BlockSpec((tm, D), lambda i, j: (i, 0)),
                   pl.BlockSpec((1, D), lambda i, j: (0, 0)),
                   pl.BlockSpec((None, tm, tf), lambda i, j: (j, i, 0)),
                   pl.BlockSpec((None, tm, tf), lambda i, j: (j, i, 0)),
                   pl.BlockSpec((None, tm, tf), lambda i, j: (j, i, 0)),
                   pl.BlockSpec((tm, D), lambda i, j: (i, 0))],
        out_shape=[jax.ShapeDtypeStruct((S, D), F32), jax.ShapeDtypeStruct((1, D), F32),
                   jax.ShapeDtypeStruct((nf, S, tf), BF16), jax.ShapeDtypeStruct((nf, S, tf), BF16),
                   jax.ShapeDtypeStruct((nf, S, tf), BF16), jax.ShapeDtypeStruct((S, D), BF16)],
        scratch_shapes=[pltpu.VMEM((tm, D), F32), pltpu.VMEM((tm, D), BF16)])


def _matmul_tn(a, b, tm, tk, name):
    K, M = a.shape
    N = b.shape[1]

    def body(a_ref, b_ref, o_ref):
        @pl.when(pl.program_id(1) == 0)
        def _():
            o_ref[...] = jnp.zeros_like(o_ref)

        o_ref[...] += _tn(a_ref[...], b_ref[...])

    return pl.pallas_call(
        body, name=name, grid=(M // tm, K // tk),
        in_specs=[pl.BlockSpec((tk, tm), lambda i, k: (k, i)),
                  pl.BlockSpec((tk, N), lambda i, k: (k, 0))],
        out_specs=pl.BlockSpec((tm, N), lambda i, k: (i, 0)),
        out_shape=jax.ShapeDtypeStruct((M, N), F32),
        compiler_params=_params(2),
    )(a, b)


def _dw_chunks(a, b, tk, name, comm=None):
    nf, S, tf = a.shape
    N = b.shape[1]

    def body(a_ref, b_ref, o_ref):
        @pl.when(pl.program_id(1) == 0)
        def _():
            o_ref[...] = jnp.zeros_like(o_ref)

        o_ref[...] += _tn(a_ref[...], b_ref[...])

    (out,), landed = _call(
        body, (a, b), name=name, grid=(nf, S // tk), comm=comm,
        in_specs=[pl.BlockSpec((None, tk, tf), lambda j, k: (j, k, 0)),
                  pl.BlockSpec((tk, N), lambda j, k: (k, 0))],
        out_specs=[pl.BlockSpec((None, tf, N), lambda j, k: (j, 0, 0))],
        out_shape=[jax.ShapeDtypeStruct((nf, tf, N), F32)])
    return out, landed


VIEW_TILE = 512


def _view_spec(d, tile=VIEW_TILE):
    return pl.BlockSpec((tile // d, d * ATTN_WIDTH), lambda i: (i, 0))


def _view_shape(S, d, dtype):
    return jax.ShapeDtypeStruct((S // d, d * ATTN_WIDTH), dtype)


def _tile_to_views(val, planes, out_refs):
    for g in range(4):
        planes[g] = val[:, g * 128:(g + 1) * 128]
    for d, ref in zip(DILATIONS, out_refs):
        if d == 1:
            ref[...] = val.astype(ref.dtype)
            continue
        for r in range(d):
            for g in range(4):
                ref[:, r * ATTN_WIDTH + g * 128:r * ATTN_WIDTH + (g + 1) * 128] = (
                    planes[g, pl.ds(r, planes.shape[1] // d, stride=d), :].astype(ref.dtype))


def _view_to_tile(ref, d, planes):
    if d == 1:
        return ref[...].astype(F32)
    for r in range(d):
        for g in range(4):
            planes[g, pl.ds(r, planes.shape[1] // d, stride=d), :] = (
                ref[:, r * ATTN_WIDTH + g * 128:r * ATTN_WIDTH + (g + 1) * 128].astype(F32))
    return jnp.concatenate([planes[g] for g in range(4)], axis=1)


def _inproj_fwd(x, gain, w_in_p):
    S, D = x.shape
    tm = VIEW_TILE
    W = ATTN_WIDTH

    def body(x_ref, gain_ref, w_ref, h_ref, q1, q4, q16, k1, k4, k16, v1, v4, v16, dq_ref, dk_ref, dv_ref, gate_ref, bd_ref,
             planes):
        xf = x_ref[...]
        r = lax.rsqrt(jnp.mean(xf * xf, axis=-1, keepdims=True) + NORM_EPS)
        h = (xf * r * gain_ref[...]).astype(BF16)
        h_ref[...] = h
        _tile_to_views(_nt(h, w_ref[0:W, :]) * ATTN_SCALE, planes, (q1, q4, q16))
        _tile_to_views(_nt(h, w_ref[W:2 * W, :]), planes, (k1, k4, k16))
        _tile_to_views(_nt(h, w_ref[2 * W:3 * W, :]), planes, (v1, v4, v16))
        dq_ref[...] = _nt(h, w_ref[3 * W:4 * W, :])
        dk_ref[...] = _nt(h, w_ref[4 * W:5 * W, :])
        dv_ref[...] = _nt(h, w_ref[5 * W:6 * W, :])
        gate_ref[...] = _nt(h, w_ref[6 * W:7 * W, :])
        bd_ref[...] = _nt(h, w_ref[7 * W:7 * W + 128, :])

    tok = lambda w: pl.BlockSpec((tm, w), lambda i: (i, 0))
    return pl.pallas_call(
        body, name="inproj_fwd", grid=(S // tm,),
        in_specs=[tok(D), pl.BlockSpec((1, D), lambda i: (0, 0)),
                  pl.BlockSpec((IN_COLS_PADDED, D), lambda i: (0, 0))],
        out_specs=[tok(D)] + [_view_spec(d) for d in DILATIONS] * 3 + [tok(W)] * 4 + [tok(128)],
        out_shape=[jax.ShapeDtypeStruct((S, D), BF16)] + [_view_shape(S, d, BF16) for d in DILATIONS] * 3
                  + [jax.ShapeDtypeStruct((S, W), F32)] * 4 + [jax.ShapeDtypeStruct((S, 128), F32)],
        scratch_shapes=[pltpu.VMEM((4, tm, 128), F32)],
        compiler_params=_params(1),
    )(x, gain, w_in_p)


def _inproj_bwd(dxo, x, gain, attn_grads, dsecs, dbd, w_in_p):
    S, D = x.shape
    tm = VIEW_TILE
    W = ATTN_WIDTH

    def body(dxo_ref, x_ref, gain_ref, *rest):
        views, (s3, s4, s5, s6, dbd_ref, w_ref, dx_ref, dgain_ref, dproj_ref, planes) = rest[:9], rest[9:]

        @pl.when(pl.program_id(0) == 0)
        def _():
            dgain_ref[...] = jnp.zeros_like(dgain_ref)

        secs = []
        for k in range(3):
            parts = [_view_to_tile(views[3 * k + p], d, planes) for p, d in enumerate(DILATIONS)]
            secs.append(parts[0] + parts[1] + parts[2])
        secs += [s3[...], s4[...], s5[...], s6[...]]
        dh = jnp.zeros((tm, D), F32)
        for k, s in enumerate(secs):
            d = s.astype(BF16)
            dproj_ref[:, k * W:(k + 1) * W] = d
            dh += _nn(d, w_ref[k * W:(k + 1) * W, :])
        d = dbd_ref[...].astype(BF16)
        dproj_ref[:, 7 * W:7 * W + 128] = d
        dh += _nn(d, w_ref[7 * W:7 * W + 128, :])
        dx, dgain = _rmsnorm_bwd(dh, x_ref[...], gain_ref[...])
        dx_ref[...] = dxo_ref[...] + dx
        dgain_ref[...] += dgain

    tok = lambda w: pl.BlockSpec((tm, w), lambda i: (i, 0))
    return pl.pallas_call(
        body, name="inproj_bwd", grid=(S // tm,),
        in_specs=[tok(D), tok(D), pl.BlockSpec((1, D), lambda i: (0, 0))] + [_view_spec(d, tm) for d in DILATIONS] * 3
                 + [tok(W)] * 4 + [tok(128)] + [pl.BlockSpec((IN_COLS_PADDED, D), lambda i: (0, 0))],
        out_specs=[tok(D), pl.BlockSpec((1, D), lambda i: (0, 0)), tok(IN_COLS_PADDED)],
        out_shape=[jax.ShapeDtypeStruct((S, D), F32), jax.ShapeDtypeStruct((1, D), F32),
                   jax.ShapeDtypeStruct((S, IN_COLS_PADDED), BF16)],
        scratch_shapes=[pltpu.VMEM((4, tm, 128), F32)],
        compiler_params=_params(1),
    )(dxo, x, gain, *[g for grads in attn_grads for g in grads], *dsecs, dbd, w_in_p)


def _slope(h):
    return 2.0 ** (-8.0 * (h + 1) / ATTN_HEADS)


def _head_bias(steps, d, heads=tuple(range(ATTN_HEADS))):
    stepsf = steps.astype(F32)
    return jnp.stack([stepsf * (-_slope(h) * d) for h in heads])


def _hnt(a, b):
    return lax.dot_general(a, b, (((2,), (2,)), ((0,), (0,))), preferred_element_type=F32)


def _hnn(a, b):
    return lax.dot_general(a, b, (((2,), (1,)), ((0,), (0,))), preferred_element_type=F32)


def _blocks_per_step(nb):
    return next(n for n in (4, 2, 1) if nb % n == 0)


def _query_step_specs(qb):
    B = ATTN_BLOCK
    cur = pl.BlockSpec((qb * B, ATTN_WIDTH), lambda r, n: (n, r))
    prev = pl.BlockSpec((B, ATTN_WIDTH), lambda r, n: (jnp.maximum(qb * n - 1, 0), r))
    return cur, prev


def _prev_block(prev_ref, cur_ref, sub, sl):
    B = ATTN_BLOCK
    return prev_ref[:, sl] if sub == 0 else cur_ref[(sub - 1) * B:sub * B, sl]


def _head_cols(tile, lo, big):
    return [_head_col(tile, lo, big), _head_col(tile, jnp.logical_not(lo), big)]


def _attn_fwd(q, k, v, d, name):
    L = q.shape[0]
    nb = L // ATTN_BLOCK
    B = ATTN_BLOCK
    QB = _blocks_per_step(nb)

    def body(q_ref, kp_ref, kc_ref, vp_ref, vc_ref, o_ref, lse_ref):
        n = pl.program_id(1)
        qi = lax.broadcasted_iota(jnp.int32, (B, 2 * B), 0)
        kj = lax.broadcasted_iota(jnp.int32, (B, 2 * B), 1)
        steps = qi + B - kj
        band = (steps >= 0) & (steps <= B)
        lo = lax.broadcasted_iota(jnp.int32, (B, 128), 1) < 64
        bias = _head_bias(steps, d)
        for sub in range(QB):
            rows = slice(sub * B, (sub + 1) * B)
            valid = band & ((kj >= B) | (n > 0)) if sub == 0 else band
            qs, ks, vs = [], [], []
            for G in range(4):
                sl = slice(G * 128, (G + 1) * 128)
                qg = q_ref[rows, sl]
                kg = jnp.concatenate([_prev_block(kp_ref, kc_ref, sub, sl), kc_ref[rows, sl]], axis=0)
                vg = jnp.concatenate([_prev_block(vp_ref, vc_ref, sub, sl), vc_ref[rows, sl]], axis=0)
                qs += [jnp.where(lo, qg, jnp.zeros_like(qg)), jnp.where(lo, jnp.zeros_like(qg), qg)]
                ks += [kg, kg]
                vs += [vg, vg]
            s = jnp.where(valid, _hnt(jnp.stack(qs), jnp.stack(ks)) + bias, NEG_BIG)
            m = jnp.max(s, axis=-1, keepdims=True)
            p = jnp.exp(s - m)
            l = jnp.sum(p, axis=-1, keepdims=True)
            o = _hnn(p.astype(BF16), jnp.stack(vs)) / l
            lse = m + jnp.log(l)
            for G in range(4):
                sl = slice(G * 128, (G + 1) * 128)
                o_ref[rows, sl] = jnp.where(lo, o[2 * G], o[2 * G + 1])
                lse_ref[rows, sl] = jnp.where(lo, lse[2 * G], lse[2 * G + 1])

    cur, prev = _query_step_specs(QB)
    return pl.pallas_call(
        body, name=name, grid=(d, nb // QB),
        in_specs=[cur, prev, cur, prev, cur],
        out_specs=[cur, cur],
        out_shape=[jax.ShapeDtypeStruct((L, d * ATTN_WIDTH), F32)] * 2,
        compiler_params=_params(2),
    )(q, k, k, v, v)


def _attn_merge(parts):
    S = parts[0][0].shape[0]
    tm = VIEW_TILE

    def body(o1, s1, o2, s2, o3, s3, o_ref, lse1, lse4, lse16, planes):
        outs, lses = [], []
        for d, (o, s) in zip(DILATIONS, ((o1, s1), (o2, s2), (o3, s3))):
            outs.append(_view_to_tile(o, d, planes))
            lses.append(_view_to_tile(s, d, planes))
        mx = jnp.maximum(jnp.maximum(lses[0], lses[1]), lses[2])
        es = [jnp.exp(s - mx) for s in lses]
        den = es[0] + es[1] + es[2]
        o_ref[...] = (es[0] * outs[0] + es[1] * outs[1] + es[2] * outs[2]) / den
        _tile_to_views(mx + jnp.log(den), planes, (lse1, lse4, lse16))

    views = [_view_spec(d) for d in DILATIONS]
    flat = [t for p in parts for t in p]
    return pl.pallas_call(
        body, name="attn_merge", grid=(S // tm,),
        in_specs=[views[p] for p in range(3) for _ in range(2)],
        out_specs=[views[0]] + views,
        out_shape=[jax.ShapeDtypeStruct((S, ATTN_WIDTH), F32)] + [_view_shape(S, d, F32) for d in DILATIONS],
        scratch_shapes=[pltpu.VMEM((4, tm, 128), F32)],
        compiler_params=_params(1),
    )(*flat)


def _head_col(t, msk, big):
    if big:
        return jnp.max(jnp.where(msk, t, NEG_BIG), axis=-1, keepdims=True)
    return jnp.sum(jnp.where(msk, t, 0.0), axis=-1, keepdims=True) * (1.0 / 64.0)


def _attn_bwd_q(q, k, v, do, lse, dd, d, name):
    L = q.shape[0]
    nb = L // ATTN_BLOCK
    B = ATTN_BLOCK
    QB = _blocks_per_step(nb)

    def body(q_ref, kp_ref, kc_ref, vp_ref, vc_ref, do_ref, lse_ref, dd_ref, dq_ref):
        n = pl.program_id(1)
        qi = lax.broadcasted_iota(jnp.int32, (B, 2 * B), 0)
        kj = lax.broadcasted_iota(jnp.int32, (B, 2 * B), 1)
        steps = qi + B - kj
        band = (steps >= 0) & (steps <= B)
        lo = lax.broadcasted_iota(jnp.int32, (B, 128), 1) < 64
        bias = _head_bias(steps, d)
        for sub in range(QB):
            rows = slice(sub * B, (sub + 1) * B)
            valid = band & ((kj >= B) | (n > 0)) if sub == 0 else band
            qs, ks, vs, dos, lses, dcols = [], [], [], [], [], []
            for G in range(4):
                sl = slice(G * 128, (G + 1) * 128)
                qg = q_ref[rows, sl]
                kg = jnp.concatenate([_prev_block(kp_ref, kc_ref, sub, sl), kc_ref[rows, sl]], axis=0)
                vg = jnp.concatenate([_prev_block(vp_ref, vc_ref, sub, sl), vc_ref[rows, sl]], axis=0)
                dog = do_ref[rows, sl]
                qs += [jnp.where(lo, qg, jnp.zeros_like(qg)), jnp.where(lo, jnp.zeros_like(qg), qg)]
                dos += [jnp.where(lo, dog, 0.0).astype(BF16), jnp.where(lo, 0.0, dog).astype(BF16)]
                ks += [kg, kg]
                vs += [vg, vg]
                lses += _head_cols(lse_ref[rows, sl], lo, True)
                dcols += _head_cols(dd_ref[rows, sl], lo, False)
            kb = jnp.stack(ks)
            s = _hnt(jnp.stack(qs), kb) + bias
            p = jnp.where(valid, jnp.exp(jnp.where(valid, s, NEG_BIG) - jnp.stack(lses)), 0.0)
            dp = _hnt(jnp.stack(dos), jnp.stack(vs))
            ds = p * (dp - jnp.stack(dcols))
            dq = _hnn(ds.astype(BF16), kb) * ATTN_SCALE
            for G in range(4):
                dq_ref[rows, G * 128:(G + 1) * 128] = jnp.where(lo, dq[2 * G], dq[2 * G + 1]).astype(BF16)

    cur, prev = _query_step_specs(QB)
    return pl.pallas_call(
        body, name=name, grid=(d, nb // QB), in_specs=[cur, prev, cur, prev, cur, cur, cur, cur], out_specs=cur,
        out_shape=jax.ShapeDtypeStruct((L, d * ATTN_WIDTH), BF16), compiler_params=_params(2),
    )(q, k, k, v, v, do, lse, dd)


def _attn_bwd_kv(q, k, v, do, lse, dd, d, name):
    L = q.shape[0]
    nb = L // ATTN_BLOCK
    B = ATTN_BLOCK
    KB = _blocks_per_step(nb)
    n_steps = nb // KB

    def body(k_ref, v_ref, qc_ref, qn_ref, doc_ref, don_ref, lsec_ref, lsen_ref, ddc_ref, ddn_ref, dk_ref, dv_ref):
        j = pl.program_id(1)
        qrow = lax.broadcasted_iota(jnp.int32, (2 * B, B), 0)
        kk = lax.broadcasted_iota(jnp.int32, (2 * B, B), 1)
        steps = qrow - kk
        band = (steps >= 0) & (steps <= B)
        lo2 = lax.broadcasted_iota(jnp.int32, (2 * B, 128), 1) < 64
        lo = lax.broadcasted_iota(jnp.int32, (B, 128), 1) < 64
        stepsf = steps.astype(F32)
        for sub in range(KB):
            rows = slice(sub * B, (sub + 1) * B)
            last = sub == KB - 1
            valid = band & ((qrow < B) | (j < n_steps - 1)) if last else band
            after = lambda cur_ref, nxt_ref, sl: nxt_ref[:, sl] if last else cur_ref[(sub + 1) * B:(sub + 2) * B, sl]
            for G in range(4):
                sl = slice(G * 128, (G + 1) * 128)
                kg = k_ref[rows, sl]
                vg = v_ref[rows, sl]
                qq = jnp.concatenate([qc_ref[rows, sl], after(qc_ref, qn_ref, sl)], axis=0)
                doo = jnp.concatenate([doc_ref[rows, sl], after(doc_ref, don_ref, sl)], axis=0)
                lse2 = jnp.concatenate([lsec_ref[rows, sl], after(lsec_ref, lsen_ref, sl)], axis=0)
                dd2 = jnp.concatenate([ddc_ref[rows, sl], after(ddc_ref, ddn_ref, sl)], axis=0)
                doo_b = doo.astype(BF16)
                dks, dvs = [], []
                for half in (0, 1):
                    msk = lo2 if half == 0 else jnp.logical_not(lo2)
                    qm = jnp.where(msk, qq, jnp.zeros_like(qq))
                    s = _nt(qm, kg) - (_slope(2 * G + half) * d) * stepsf
                    lse_c = _head_col(lse2, msk, True)
                    p = jnp.where(valid, jnp.exp(jnp.where(valid, s, NEG_BIG) - lse_c), 0.0)
                    dvs.append(_tn(p.astype(BF16), doo_b))
                    dom = jnp.where(msk, doo, 0.0).astype(BF16)
                    dp = _nt(dom, vg)
                    dcol = _head_col(dd2, msk, False)
                    ds = p * (dp - dcol)
                    dks.append(_tn(ds.astype(BF16), qq))
                dk_ref[rows, sl] = jnp.where(lo, dks[0], dks[1]).astype(BF16)
                dv_ref[rows, sl] = jnp.where(lo, dvs[0], dvs[1]).astype(BF16)

    cur = pl.BlockSpec((KB * B, ATTN_WIDTH), lambda r, j: (j, r))
    nxt = pl.BlockSpec((B, ATTN_WIDTH), lambda r, j: (jnp.minimum(KB * (j + 1), nb - 1), r))
    return pl.pallas_call(
        body, name=name, grid=(d, n_steps), in_specs=[cur, cur, cur, nxt, cur, nxt, cur, nxt, cur, nxt],
        out_specs=[cur, cur],
        out_shape=[jax.ShapeDtypeStruct((L, d * ATTN_WIDTH), BF16)] * 2, compiler_params=_params(2),
    )(k, v, q, q, do, do, lse, lse, dd, dd)


CONV_T = 512
HALO = 8


def _per_head(head, refs):
    for h in range(DN_HEADS):
        lanes = pl.ds(h * DN_HEAD_DIM, DN_HEAD_DIM)
        head(*[r.at[:, lanes] for r in refs[:-1]], refs[-1])


def _conv_taps(pad_ref, w, T):
    acc = pad_ref[pl.ds(HALO - 3, T), :] * w[0:1, :]
    for j in range(1, CONV_WIDTH):
        acc = acc + pad_ref[pl.ds(HALO - 3 + j, T), :] * w[j:j + 1, :]
    return acc


def _conv_fwd(xq, xk, xv, conv_w):
    S = xq.shape[0]
    T = CONV_T

    def body(*refs):
        _per_head(head, refs)

    def head(xq_ref, xqh_ref, xk_ref, xkh_ref, xv_ref, xvh_ref, wq_ref, wk_ref, wv_ref,
             qn_ref, kn_ref, v_ref, pad_ref):
        i = pl.program_id(0)

        def act(x_ref, xh_ref, w_ref):
            pad_ref[pl.ds(0, HALO), :] = jnp.where(i > 0, xh_ref[...], 0.0)
            pad_ref[pl.ds(HALO, T), :] = x_ref[...]
            c = _conv_taps(pad_ref, w_ref[...], T)
            return c * _sigmoid(c)

        def l2n(t):
            return t * lax.rsqrt(jnp.sum(t * t, axis=-1, keepdims=True) + L2_EPS)

        qn_ref[...] = l2n(act(xq_ref, xqh_ref, wq_ref))
        kn_ref[...] = l2n(act(xk_ref, xkh_ref, wk_ref))
        v_ref[...] = act(xv_ref, xvh_ref, wv_ref)

    tile = pl.BlockSpec((T, DN_WIDTH), lambda i: (i, 0))
    halo = pl.BlockSpec((HALO, DN_WIDTH), lambda i: (jnp.maximum(i * (T // HALO) - 1, 0), 0))
    wspec = lambda sec: pl.BlockSpec((CONV_WIDTH, DN_WIDTH), lambda i, sec=sec: (0, sec))
    return pl.pallas_call(
        body, name="dn_conv_fwd", grid=(S // T,),
        in_specs=[tile, halo, tile, halo, tile, halo, wspec(0), wspec(1), wspec(2)],
        out_specs=[tile, tile, tile],
        out_shape=[jax.ShapeDtypeStruct((S, DN_WIDTH), F32)] * 3,
        scratch_shapes=[pltpu.VMEM((T + HALO, 128), F32)],
        compiler_params=_params(1),
    )(xq, xq, xk, xk, xv, xv, conv_w, conv_w, conv_w)


def _conv_bwd_pre(xq, xk, xv, conv_w, dqn, dkn, dv):
    S = xq.shape[0]
    T = CONV_T

    def body(*refs):
        _per_head(head, refs)

    def head(xq_ref, xqh_ref, xk_ref, xkh_ref, xv_ref, xvh_ref, wq_ref, wk_ref, wv_ref,
             dqn_ref, dkn_ref, dv_ref, dcq_ref, dck_ref, dcv_ref, dwq_ref, dwk_ref, dwv_ref, pad_ref):
        i = pl.program_id(0)

        def one(x_ref, xh_ref, w_ref, dy_ref, dc_ref, dw_ref, normed):
            pad_ref[pl.ds(0, HALO), :] = jnp.where(i > 0, xh_ref[...], 0.0)
            pad_ref[pl.ds(HALO, T), :] = x_ref[...]
            c = _conv_taps(pad_ref, w_ref[...], T)
            sg = _sigmoid(c)
            a = c * sg
            dy = dy_ref[...]
            if normed:
                r = lax.rsqrt(jnp.sum(a * a, axis=-1, keepdims=True) + L2_EPS)
                y = a * r
                da = r * (dy - y * jnp.sum(dy * y, axis=-1, keepdims=True))
            else:
                da = dy
            dc = da * (sg * (1.0 + c * (1.0 - sg)))
            dc_ref[...] = dc

            @pl.when(i == 0)
            def _():
                dw_ref[...] = jnp.zeros_like(dw_ref)

            rows = [jnp.sum(dc * pad_ref[pl.ds(HALO - 3 + j, T), :], axis=0, keepdims=True) for j in range(CONV_WIDTH)]
            dw_ref[...] += jnp.concatenate(rows + [jnp.zeros((8 - CONV_WIDTH, 128), F32)], axis=0)

        one(xq_ref, xqh_ref, wq_ref, dqn_ref, dcq_ref, dwq_ref, True)
        one(xk_ref, xkh_ref, wk_ref, dkn_ref, dck_ref, dwk_ref, True)
        one(xv_ref, xvh_ref, wv_ref, dv_ref, dcv_ref, dwv_ref, False)

    tile = pl.BlockSpec((T, DN_WIDTH), lambda i: (i, 0))
    halo = pl.BlockSpec((HALO, DN_WIDTH), lambda i: (jnp.maximum(i * (T // HALO) - 1, 0), 0))
    wspec = lambda sec: pl.BlockSpec((CONV_WIDTH, DN_WIDTH), lambda i, sec=sec: (0, sec))
    dwspec = pl.BlockSpec((8, DN_WIDTH), lambda i: (0, 0))
    return pl.pallas_call(
        body, name="dn_conv_bwd_pre", grid=(S // T,),
        in_specs=[tile, halo, tile, halo, tile, halo, wspec(0), wspec(1), wspec(2), tile, tile, tile],
        out_specs=[tile, tile, tile, dwspec, dwspec, dwspec],
        out_shape=[jax.ShapeDtypeStruct((S, DN_WIDTH), F32)] * 3 + [jax.ShapeDtypeStruct((8, DN_WIDTH), F32)] * 3,
        scratch_shapes=[pltpu.VMEM((T + HALO, 128), F32)],
        compiler_params=_params(1),
    )(xq, xq, xk, xk, xv, xv, conv_w, conv_w, conv_w, dqn, dkn, dv)


def _conv_bwd_x(dcq, dck, dcv, conv_w):
    S = dcq.shape[0]
    T = CONV_T
    nt = S // T

    def body(*refs):
        _per_head(head, refs)

    def head(dq_ref, dqh_ref, dk_ref, dkh_ref, dv_ref, dvh_ref, wq_ref, wk_ref, wv_ref,
             oq_ref, ok_ref, ov_ref, pad_ref):
        i = pl.program_id(0)

        def one(d_ref, dh_ref, w_ref, o_ref):
            pad_ref[pl.ds(0, T), :] = d_ref[...]
            pad_ref[pl.ds(T, HALO), :] = jnp.where(i < nt - 1, dh_ref[...], 0.0)
            w = w_ref[...]
            acc = pad_ref[pl.ds(3, T), :] * w[0:1, :]
            for j in range(1, CONV_WIDTH):
                acc = acc + pad_ref[pl.ds(3 - j, T), :] * w[j:j + 1, :]
            o_ref[...] = acc

        one(dq_ref, dqh_ref, wq_ref, oq_ref)
        one(dk_ref, dkh_ref, wk_ref, ok_ref)
        one(dv_ref, dvh_ref, wv_ref, ov_ref)

    tile = pl.BlockSpec((T, DN_WIDTH), lambda i: (i, 0))
    halo = pl.BlockSpec((HALO, DN_WIDTH), lambda i: (jnp.minimum((i + 1) * (T // HALO), S // HALO - 1), 0))
    wspec = lambda sec: pl.BlockSpec((CONV_WIDTH, DN_WIDTH), lambda i, sec=sec: (0, sec))
    return pl.pallas_call(
        body, name="dn_conv_bwd_x", grid=(nt,),
        in_specs=[tile, halo, tile, halo, tile, halo, wspec(0), wspec(1), wspec(2)],
        out_specs=[tile, tile, tile],
        out_shape=[jax.ShapeDtypeStruct((S, DN_WIDTH), F32)] * 3,
        scratch_shapes=[pltpu.VMEM((T + HALO, 128), F32)],
        compiler_params=_params(1),
    )(dcq, dcq, dck, dck, dcv, dcv, conv_w, conv_w, conv_w)


PREP_CHUNKS = 4
SCAN_CHUNKS = 8


def _bnn(a, b):
    return lax.dot_general(a, b, (((2,), (1,)), ((0,), (0,))), preferred_element_type=F32, precision=HI)


def _bnt(a, b):
    return lax.dot_general(a, b, (((2,), (2,)), ((0,), (0,))), preferred_element_type=F32, precision=HI)


def _btn(a, b):
    return lax.dot_general(a, b, (((1,), (1,)), ((0,), (0,))), preferred_element_type=F32, precision=HI)


def _tri_inverse_b(a, blk, eye):
    dg = jnp.where(blk, a, 0.0)
    lo = a - dg
    d2 = _bnn(dg, dg)
    d4 = _bnn(d2, d2)
    d8 = _bnn(d4, d4)
    td = _bnn(_bnn(_bnn(eye - dg, eye + d2), eye + d4), eye + d8)
    b = _bnn(td, lo)
    b2 = _bnn(b, b)
    return _bnn(_bnn(eye - b, eye + b2), td)


def _dn_common_b(bds, avec, dvec, q_raw, k, v, t=None):
    C = DN_CHUNK
    lane = lax.broadcasted_iota(jnp.int32, (C, 128), 1)
    row = lax.broadcasted_iota(jnp.int32, (1, C, C), 1)
    col = lax.broadcasted_iota(jnp.int32, (1, C, C), 2)
    incl = row >= col
    strict = row > col
    eye = (row == col).astype(F32)
    blk = (row // 16) == (col // 16)
    pick = lambda tile, ln: jnp.sum(jnp.where(lane == ln, tile, 0.0), axis=-1, keepdims=True)
    betas, graws, zcs = [], [], []
    for bd in bds:
        z = bd + dvec
        g_all = -jnp.exp(avec) * (jnp.maximum(z, 0.0) + jnp.log(1.0 + jnp.exp(-jnp.abs(z))))
        beta_all = _sigmoid(bd)
        for h in range(DN_HEADS):
            betas.append(pick(beta_all, h))
            graws.append(pick(g_all, DN_HEADS + h))
            zcs.append(pick(z, DN_HEADS + h))
    beta, graw, zc = jnp.stack(betas), jnp.stack(graws), jnp.stack(zcs)
    to_row = lambda c: jnp.sum(eye * c, axis=1, keepdims=True)
    gc = jnp.sum(jnp.where(incl, to_row(graw), 0.0), axis=-1, keepdims=True)
    decay = jnp.exp(jnp.where(incl, gc - to_row(gc), NEG_BIG))
    q = q_raw * (DN_HEAD_DIM ** -0.5)
    kb = k * beta
    kk = _bnt(kb, k)
    if t is None:
        t = _tri_inverse_b(jnp.where(strict, kk * decay, 0.0), blk, eye)
    eg = jnp.exp(gc)
    rhs_w = kb * eg
    u = _bnn(t, v * beta)
    w = _bnn(t, rhs_w)
    qk = _bnt(q, k)
    aq = jnp.where(incl, qk * decay, 0.0)
    last = lax.broadcasted_iota(jnp.int32, (1, C, 1), 1) == C - 1
    g_last = jnp.sum(jnp.where(last, gc, 0.0), axis=1, keepdims=True)
    ekd = jnp.exp(g_last - gc)
    return dict(beta=beta, graw=graw, zc=zc, gc=gc, decay=decay, q=q, kb=kb, kk=kk, t=t, eg=eg, rhs_w=rhs_w,
                u=u, w=w, qk=qk, aq=aq, g_last=g_last, ekd=ekd, kd=k * ekd, qg=q * eg,
                incl=incl, strict=strict, eye=eye, lane=lane, row=row, col=col, last=last)


def _stack_heads(ref, rows):
    return jnp.stack([ref[rows, h * DN_HEAD_DIM:(h + 1) * DN_HEAD_DIM] for h in range(DN_HEADS)])


def _stack_units(ref, nc):
    C = DN_CHUNK
    return jnp.concatenate([_stack_heads(ref, slice(ci * C, (ci + 1) * C)) for ci in range(nc)], axis=0)


def _store_units(ref, val, nc):
    C = DN_CHUNK
    for ci in range(nc):
        for h in range(DN_HEADS):
            ref[ci * C:(ci + 1) * C, h * DN_HEAD_DIM:(h + 1) * DN_HEAD_DIM] = val[ci * DN_HEADS + h]


def _dn_prep(qn, kn, v, bd, avec, dvec):
    S = qn.shape[0]
    C = DN_CHUNK
    N = S // C
    nc = PREP_CHUNKS

    def body(q_ref, k_ref, v_ref, bd_ref, a_ref, d_ref, u_ref, w_ref, qg_ref, kd_ref, aq_ref, t_ref, egl_ref):
        bds = [bd_ref[ci * C:(ci + 1) * C, :] for ci in range(nc)]
        c = _dn_common_b(bds, a_ref[...], d_ref[...], _stack_units(q_ref, nc), _stack_units(k_ref, nc), _stack_units(v_ref, nc))
        _store_units(u_ref, c["u"], nc)
        _store_units(w_ref, c["w"], nc)
        _store_units(qg_ref, c["qg"], nc)
        _store_units(kd_ref, c["kd"], nc)
        egl = jnp.broadcast_to(jnp.exp(c["g_last"]), (nc * DN_HEADS, 1, 128))
        for ci in range(nc):
            for h in range(DN_HEADS):
                aq_ref[h, ci * C:(ci + 1) * C, :] = c["aq"][ci * DN_HEADS + h]
                t_ref[h, ci * C:(ci + 1) * C, :] = c["t"][ci * DN_HEADS + h]
            egl_ref[ci * 8:(ci + 1) * 8, :] = jnp.concatenate(
                [egl[ci * DN_HEADS + h] for h in range(DN_HEADS)] + [jnp.zeros((8 - DN_HEADS, 128), F32)], axis=0)

    tok = lambda w: pl.BlockSpec((nc * C, w), lambda n: (n, 0))
    sq = pl.BlockSpec((DN_HEADS, nc * C, C), lambda n: (0, n, 0))
    vec = pl.BlockSpec((1, 128), lambda n: (0, 0))
    return pl.pallas_call(
        body, name="dn_prep", grid=(N // nc,),
        in_specs=[tok(DN_WIDTH)] * 3 + [tok(128), vec, vec],
        out_specs=[tok(DN_WIDTH)] * 4 + [sq, sq, pl.BlockSpec((nc * 8, 128), lambda n: (n, 0))],
        out_shape=[jax.ShapeDtypeStruct((S, DN_WIDTH), F32)] * 4 + [jax.ShapeDtypeStruct((DN_HEADS, S, C), F32)] * 2
                  + [jax.ShapeDtypeStruct((N * 8, 128), F32)],
        compiler_params=_params(1),
    )(qn, kn, v, bd, avec, dvec)


def _dn_scan_fwd(u, w, qg, kd, aq, egl, gate, dn_gain):
    S = u.shape[0]
    C = DN_CHUNK
    N = S // C
    HD = DN_HEAD_DIM
    nc = SCAN_CHUNKS

    def body(u_ref, w_ref, qg_ref, kd_ref, aq_ref, egl_ref, gate_ref, gain_ref, dn_ref, o_ref, vn_ref, st_ref, state_ref):
        @pl.when(pl.program_id(0) == 0)
        def _():
            state_ref[...] = jnp.zeros_like(state_ref)

        gain = gain_ref[...]
        for ci in range(nc):
            rows = slice(ci * C, (ci + 1) * C)
            st = state_ref[...]
            for h in range(DN_HEADS):
                st_ref[ci * DN_WIDTH + h * HD:ci * DN_WIDTH + (h + 1) * HD, :] = st[h]
            v_new = _stack_heads(u_ref, rows) - _bnn(_stack_heads(w_ref, rows), st)
            o = _bnn(_stack_heads(qg_ref, rows), st) + _bnn(aq_ref[:, rows, :], v_new)
            egl = jnp.stack([egl_ref[ci * 8 + h:ci * 8 + h + 1, :] for h in range(DN_HEADS)])
            state_ref[...] = st * egl + _btn(_stack_heads(kd_ref, rows), v_new)
            r = lax.rsqrt(jnp.mean(o * o, axis=-1, keepdims=True) + NORM_EPS)
            gt = _stack_heads(gate_ref, rows)
            dn = o * r * gain * (gt * _sigmoid(gt))
            for h in range(DN_HEADS):
                sl = slice(h * HD, (h + 1) * HD)
                vn_ref[rows, sl] = v_new[h]
                o_ref[rows, sl] = o[h]
                dn_ref[rows, sl] = dn[h]

    tok = lambda wd: pl.BlockSpec((nc * C, wd), lambda n: (n, 0))
    sq = pl.BlockSpec((DN_HEADS, nc * C, C), lambda n: (0, n, 0))
    vec = pl.BlockSpec((1, 128), lambda n: (0, 0))
    return pl.pallas_call(
        body, name="dn_scan_fwd", grid=(N // nc,),
        in_specs=[tok(DN_WIDTH)] * 4 + [sq, pl.BlockSpec((nc * 8, 128), lambda n: (n, 0)), tok(DN_WIDTH), vec],
        out_specs=[tok(DN_WIDTH)] * 3 + [pl.BlockSpec((nc * DN_WIDTH, HD), lambda n: (n, 0))],
        out_shape=[jax.ShapeDtypeStruct((S, DN_WIDTH), F32)] * 3 + [jax.ShapeDtypeStruct((N * DN_WIDTH, HD), F32)],
        scratch_shapes=[pltpu.VMEM((DN_HEADS, HD, HD), F32)],
        compiler_params=_params(1),
    )(u, w, qg, kd, aq, egl, gate, dn_gain)


def _dn_scan_bwd(w, qg, kd, aq, egl, gate, dn_gain, o, ddn):
    S = w.shape[0]
    C = DN_CHUNK
    N = S // C
    HD = DN_HEAD_DIM
    nc = SCAN_CHUNKS

    def body(w_ref, qg_ref, kd_ref, aq_ref, egl_ref, gate_ref, gain_ref, o_ref, ddn_ref,
             do_ref, dvn_ref, dgate_ref, dst_ref, small_ref, dstate_ref):
        @pl.when(pl.program_id(0) == 0)
        def _():
            dstate_ref[...] = jnp.zeros_like(dstate_ref)
            small_ref[...] = jnp.zeros_like(small_ref)

        gain = gain_ref[...]
        d_gain = jnp.zeros((1, 128), F32)
        for ci in reversed(range(nc)):
            rows = slice(ci * C, (ci + 1) * C)
            dsn = dstate_ref[...]
            for h in range(DN_HEADS):
                dst_ref[ci * DN_WIDTH + h * HD:ci * DN_WIDTH + (h + 1) * HD, :] = dsn[h]
            ov = _stack_heads(o_ref, rows)
            r = lax.rsqrt(jnp.mean(ov * ov, axis=-1, keepdims=True) + NORM_EPS)
            on = ov * r
            gt = _stack_heads(gate_ref, rows)
            sgt = _sigmoid(gt)
            silu_g = gt * sgt
            dy = _stack_heads(ddn_ref, rows)
            d_gain = d_gain + jnp.sum(jnp.sum(dy * on * silu_g, axis=1, keepdims=True), axis=0)
            dgate = dy * on * gain * (sgt * (1.0 + gt * (1.0 - sgt)))
            don = dy * gain * silu_g
            do = r * (don - on * jnp.mean(don * on, axis=-1, keepdims=True))
            d_vnew = _btn(aq_ref[:, rows, :], do) + _bnn(_stack_heads(kd_ref, rows), dsn)
            egl = jnp.stack([egl_ref[ci * 8 + h:ci * 8 + h + 1, :] for h in range(DN_HEADS)])
            dstate_ref[...] = _btn(_stack_heads(qg_ref, rows), do) + dsn * egl - _btn(_stack_heads(w_ref, rows), d_vnew)
            for h in range(DN_HEADS):
                sl = slice(h * HD, (h + 1) * HD)
                do_ref[rows, sl] = do[h]
                dvn_ref[rows, sl] = d_vnew[h]
                dgate_ref[rows, sl] = dgate[h]
        small_ref[...] += jnp.concatenate([d_gain, jnp.zeros((7, 128), F32)], axis=0)

    nb = N // nc
    tok = lambda wd: pl.BlockSpec((nc * C, wd), lambda i: (nb - 1 - i, 0))
    sq = pl.BlockSpec((DN_HEADS, nc * C, C), lambda i: (0, nb - 1 - i, 0))
    vec = pl.BlockSpec((1, 128), lambda i: (0, 0))
    return pl.pallas_call(
        body, name="dn_scan_bwd", grid=(nb,),
        in_specs=[tok(DN_WIDTH)] * 3 + [sq, pl.BlockSpec((nc * 8, 128), lambda i: (nb - 1 - i, 0)), tok(DN_WIDTH), vec,
                                       tok(DN_WIDTH), tok(DN_WIDTH)],
        out_specs=[tok(DN_WIDTH)] * 3 + [pl.BlockSpec((nc * DN_WIDTH, HD), lambda i: (nb - 1 - i, 0)),
                                        pl.BlockSpec((8, 128), lambda i: (0, 0))],
        out_shape=[jax.ShapeDtypeStruct((S, DN_WIDTH), F32)] * 3 + [jax.ShapeDtypeStruct((N * DN_WIDTH, HD), F32),
                                                                  jax.ShapeDtypeStruct((8, 128), F32)],
        scratch_shapes=[pltpu.VMEM((DN_HEADS, HD, HD), F32)],
        compiler_params=_params(1),
    )(w, qg, kd, aq, egl, gate, dn_gain, o, ddn)


def _dn_post(qn, kn, v, bd, avec, dvec, t_inv, v_new_all, states, dstates, do_all, dvn_all, comm=None):
    S = qn.shape[0]
    C = DN_CHUNK
    N = S // C
    HD = DN_HEAD_DIM
    nc = PREP_CHUNKS
    B = nc * DN_HEADS

    def body(q_ref, k_ref, v_ref, bd_ref, a_ref, d_ref, t_ref, vn_ref, st_ref, dst_ref, do_ref, dvn_ref,
             dq_ref, dk_ref, dv_ref, dbd_ref, small_ref):
        @pl.when(pl.program_id(0) == 0)
        def _():
            small_ref[...] = jnp.zeros_like(small_ref)

        avec = a_ref[...]
        bds = [bd_ref[ci * C:(ci + 1) * C, :] for ci in range(nc)]
        k = _stack_units(k_ref, nc)
        vv = _stack_units(v_ref, nc)
        t = jnp.concatenate([t_ref[:, ci * C:(ci + 1) * C, :] for ci in range(nc)], axis=0)
        c = _dn_common_b(bds, avec, d_ref[...], _stack_units(q_ref, nc), k, vv, t=t)
        q, kb, eg, u, w = c["q"], c["kb"], c["eg"], c["u"], c["w"]
        beta, decay, incl, strict, eye = c["beta"], c["decay"], c["incl"], c["strict"], c["eye"]
        st = jnp.stack([st_ref[b * HD:(b + 1) * HD, :] for b in range(B)])
        dsn = jnp.stack([dst_ref[b * HD:(b + 1) * HD, :] for b in range(B)])
        v_new = _stack_units(vn_ref, nc)
        do = _stack_units(do_ref, nc)
        d_vnew = _stack_units(dvn_ref, nc)
        egl = jnp.exp(c["g_last"])
        daq = jnp.where(incl, _bnt(do, v_new), 0.0)
        d_qg = _bnt(do, st)
        d_kd = _bnt(v_new, dsn)
        d_glast = jnp.sum(jnp.sum(dsn * st, axis=-1, keepdims=True), axis=1, keepdims=True) * egl
        d_w = -_bnt(d_vnew, st)
        d_ru = _btn(t, d_vnew)
        d_rw = _btn(t, d_w)
        da = -jnp.where(strict, _bnt(d_ru, u) + _bnt(d_rw, w), 0.0)
        dv = d_ru * beta
        dbeta = jnp.sum(d_ru * vv, axis=-1, keepdims=True)
        dkb = d_rw * eg
        dgc = jnp.sum(d_rw * c["rhs_w"], axis=-1, keepdims=True)
        dkk = da * decay
        ddecay = da * c["kk"]
        dkb = dkb + _bnn(dkk, k)
        dk = _btn(dkk, kb)
        dqk = daq * decay
        ddecay = ddecay + daq * c["qk"]
        dq = _bnn(dqk, k)
        dk = dk + _btn(dqk, q)
        m = ddecay * decay
        col_sum = jnp.sum(m, axis=1, keepdims=True)
        dgc = dgc + jnp.sum(m, axis=-1, keepdims=True) - jnp.sum(eye * col_sum, axis=-1, keepdims=True)
        dq = dq + d_qg * eg
        dgc = dgc + jnp.sum(d_qg * c["qg"], axis=-1, keepdims=True)
        dk = dk + d_kd * c["ekd"]
        tk = jnp.sum(d_kd * c["kd"], axis=-1, keepdims=True)
        dgc = dgc - tk
        d_glast = d_glast + jnp.sum(tk, axis=1, keepdims=True)
        dk = dk + dkb * beta
        dbeta = dbeta + jnp.sum(dkb * k, axis=-1, keepdims=True)
        dgc = dgc + jnp.where(c["last"], d_glast, 0.0)
        dgc_row = jnp.sum(eye * dgc, axis=1, keepdims=True)
        dgraw = jnp.sum(jnp.where(c["col"] >= c["row"], dgc_row, 0.0), axis=-1, keepdims=True)
        _store_units(dq_ref, dq * (HD ** -0.5), nc)
        _store_units(dk_ref, dk, nc)
        _store_units(dv_ref, dv, nc)
        dbraw = dbeta * beta * (1.0 - beta)
        dzc = dgraw * _sigmoid(c["zc"])
        ga = dgraw * c["graw"]
        lane = c["lane"]
        lane1 = lax.broadcasted_iota(jnp.int32, (1, 128), 1)
        neg_ea = -jnp.exp(avec)
        d_alog = jnp.zeros((1, 128), F32)
        d_dt = jnp.zeros((1, 128), F32)
        for ci in range(nc):
            dbd = jnp.zeros((C, 128), F32)
            for h in range(DN_HEADS):
                b = ci * DN_HEADS + h
                dz = dzc[b] * neg_ea
                dbd = dbd + jnp.where(lane == h, dbraw[b], 0.0) + jnp.where(lane == DN_HEADS + h, dz, 0.0)
                d_alog = d_alog + jnp.where(lane1 == DN_HEADS + h, jnp.sum(ga[b], axis=0, keepdims=True), 0.0)
                d_dt = d_dt + jnp.where(lane1 == DN_HEADS + h, jnp.sum(dz, axis=0, keepdims=True), 0.0)
            dbd_ref[ci * C:(ci + 1) * C, :] = dbd
        small_ref[...] += jnp.concatenate([d_alog, d_dt, jnp.zeros((6, 128), F32)], axis=0)

    tok = lambda wd: pl.BlockSpec((nc * C, wd), lambda n: (n, 0))
    big = pl.BlockSpec((nc * DN_WIDTH, HD), lambda n: (n, 0))
    sq = pl.BlockSpec((DN_HEADS, nc * C, C), lambda n: (0, n, 0))
    vec = pl.BlockSpec((1, 128), lambda n: (0, 0))
    return _call(
        body, (qn, kn, v, bd, avec, dvec, t_inv, v_new_all, states, dstates, do_all, dvn_all),
        name="dn_post", grid=(N // nc,), comm=comm,
        in_specs=[tok(DN_WIDTH)] * 3 + [tok(128), vec, vec, sq, tok(DN_WIDTH), big, big, tok(DN_WIDTH), tok(DN_WIDTH)],
        out_specs=[tok(DN_WIDTH)] * 3 + [tok(128), pl.BlockSpec((8, 128), lambda n: (0, 0))],
        out_shape=[jax.ShapeDtypeStruct((S, DN_WIDTH), F32)] * 3 + [jax.ShapeDtypeStruct((S, 128), F32),
                                                                  jax.ShapeDtypeStruct((8, 128), F32)])


def _outproj_fwd(x, attn, dn, w_out):
    S, D = x.shape
    tm = 512

    def body(x_ref, a_ref, d_ref, w_ref, xo_ref, mix_ref):
        a = a_ref[...].astype(BF16)
        dd = d_ref[...].astype(BF16)
        mix_ref[:, 0:ATTN_WIDTH] = a
        mix_ref[:, ATTN_WIDTH:] = dd
        xo_ref[...] = x_ref[...] + _nn(a, w_ref[0:ATTN_WIDTH, :]) + _nn(dd, w_ref[ATTN_WIDTH:, :])

    tok = lambda w: pl.BlockSpec((tm, w), lambda i: (i, 0))
    return pl.pallas_call(
        body, name="outproj_fwd", grid=(S // tm,),
        in_specs=[tok(D), tok(ATTN_WIDTH), tok(DN_WIDTH), pl.BlockSpec((D, D), lambda i: (0, 0))],
        out_specs=[tok(D), tok(D)],
        out_shape=[jax.ShapeDtypeStruct((S, D), F32), jax.ShapeDtypeStruct((S, D), BF16)],
        compiler_params=_params(1),
    )(x, attn, dn, w_out)


def _outproj_bwd(dx, w_out, attn):
    S, D = dx.shape
    tm = VIEW_TILE

    def body(dx_ref, w_ref, attn_ref, da1, da4, da16, dl1, dl4, dl16, ddn_ref, dxb_ref, planes):
        d = dx_ref[...].astype(BF16)
        dxb_ref[...] = d
        da = _nt(d, w_ref[0:ATTN_WIDTH, :])
        ddn_ref[...] = _nt(d, w_ref[ATTN_WIDTH:, :])
        _tile_to_views(da, planes, (da1, da4, da16))
        lo = lax.broadcasted_iota(jnp.int32, (tm, 128), 1) < 64
        cols = []
        for G in range(4):
            sl = slice(G * 128, (G + 1) * 128)
            t = da[:, sl] * attn_ref[:, sl]
            d0 = jnp.sum(jnp.where(lo, t, 0.0), axis=-1, keepdims=True)
            d1 = jnp.sum(jnp.where(lo, 0.0, t), axis=-1, keepdims=True)
            cols.append(jnp.where(lo, d0, d1))
        _tile_to_views(jnp.concatenate(cols, axis=1), planes, (dl1, dl4, dl16))

    tok = lambda w: pl.BlockSpec((tm, w), lambda i: (i, 0))
    views = [_view_spec(d) for d in DILATIONS]
    return pl.pallas_call(
        body, name="outproj_bwd", grid=(S // tm,),
        in_specs=[tok(D), pl.BlockSpec((D, D), lambda i: (0, 0)), tok(ATTN_WIDTH)],
        out_specs=views + views + [tok(DN_WIDTH), tok(D)],
        out_shape=[_view_shape(S, d, F32) for d in DILATIONS] * 2
                  + [jax.ShapeDtypeStruct((S, DN_WIDTH), F32), jax.ShapeDtypeStruct((S, D), BF16)],
        scratch_shapes=[pltpu.VMEM((4, tm, 128), F32)],
        compiler_params=_params(1),
    )(dx, w_out, attn)


def _adamw(w, g, m, v, name):
    R, Ccols = w.shape[0], w.shape[-1]
    tr = next((t for t in range(512, 7, -8) if R % t == 0), R)
    c1 = 1.0 - ADAM_B1 ** ADAM_STEP
    c2 = 1.0 - ADAM_B2 ** ADAM_STEP

    def body(w_ref, g_ref, m_ref, v_ref, d_ref, nm_ref, nv_ref):
        gv = g_ref[...]
        mn = ADAM_B1 * m_ref[...] + (1.0 - ADAM_B1) * gv
        vn = ADAM_B2 * v_ref[...] + (1.0 - ADAM_B2) * (gv * gv)
        nm_ref[...] = mn
        nv_ref[...] = vn
        d_ref[...] = -ADAM_LR * ((mn / c1) / (jnp.sqrt(vn / c2) + ADAM_EPS) + ADAM_WD * w_ref[...])

    if w.ndim == 2:
        grid, spec = (R // tr,), pl.BlockSpec((tr, Ccols), lambda i: (i, 0))
    else:
        grid, spec = (2,), pl.BlockSpec((R // 2, 1, Ccols), lambda i: (i, 0, 0))
    return pl.pallas_call(
        body, name=name, grid=grid, in_specs=[spec] * 4, out_specs=[spec] * 3,
        out_shape=[jax.ShapeDtypeStruct(w.shape, F32)] * 3, compiler_params=_params(1),
    )(w, g, m, v)


LATE_WEIGHTS = ("w_out", "ffn2_gate", "ffn2_up", "ffn2_down")


def _local_step(x, target, wts, small, dist=None):
    g1, g2, gm, gf = small["norm_ffn1"], small["norm_ffn2"], small["norm_mix"], small["norm_final"]
    wts = dict(wts)

    def reduce_start(gs, tag):
        return _rs_add_pairs(gs, _swap_sibling(gs, True, "rs_swap_halves_" + tag), dist["c"], "rs_add_pairs_" + tag)

    (x1, h1, fg1, fu1), late = _ffn_fwd(x, g1, wts["ffn1_gate"], wts["ffn1_up"], wts["ffn1_down"], "ffn1_fwd",
                                        comm=_ag_comm(dist["late"]) if dist else None)
    if dist:
        wts.update(zip(LATE_WEIGHTS, late))
        wts["w_out"] = wts["w_out"].reshape(D_MODEL, D_MODEL)
    h2, *qkv, xq, xk, xv, gate, bd = _inproj_fwd(x1, gm, wts["w_in"])
    aq, ak, av = qkv[0:3], qkv[3:6], qkv[6:9]
    parts = [_attn_fwd(aq[p], ak[p], av[p], d, f"attn_fwd_d{d}") for p, d in enumerate(DILATIONS)]
    attn, *lse = _attn_merge(parts)
    conv_w = small["conv_w"]
    qn, kn, vv = _conv_fwd(xq, xk, xv, conv_w)
    dn_u, dn_w, dn_qg, dn_kd, dn_aq, dn_t, dn_egl = _dn_prep(qn, kn, vv, bd, small["avec"], small["dvec"])
    dn, o_dn, v_new, states = _dn_scan_fwd(dn_u, dn_w, dn_qg, dn_kd, dn_aq, dn_egl, gate, small["dn_norm"])
    x2, mix = _outproj_fwd(x1, attn, dn, wts["w_out"])
    (dx3, h3, fg2, fu2, loss, d_gf), _ = _ffn_fwd(x2, g2, wts["ffn2_gate"], wts["ffn2_up"], wts["ffn2_down"], "ffn2_fwd",
                                                 head=(gf, target))

    grads = {}
    (dx2, d_g2, dfg2, dfu2, act2, dout2), _ = _ffn_bwd(dx3, x2, g2, fg2, fu2, wts["ffn2_down"], wts["ffn2_gate"],
                                                      wts["ffn2_up"], "ffn2_bwd")
    tk = 2048
    grads["ffn2_gate"], _ = _dw_chunks(dfg2, h3, tk, "dw_ffn2_gate")
    grads["ffn2_up"], _ = _dw_chunks(dfu2, h3, tk, "dw_ffn2_up")
    grads["ffn2_down"], _ = _dw_chunks(act2, dout2, tk, "dw_ffn2_down")
    group_a = ("ffn2_gate", "ffn2_up", "ffn2_down")
    parts_a = reduce_start([grads[n] for n in group_a], "a") if dist else None

    *dviews, ddn, dx2b = _outproj_bwd(dx2, wts["w_out"], attn)
    dattn, dd = dviews[0:3], dviews[3:6]
    grads["w_out"] = _matmul_tn(mix, dx2b, D_MODEL, tk, "dw_out").reshape(N_CHIPS, D_MODEL // N_CHIPS, D_MODEL)

    daq, dak, dav = [], [], []
    for p, d in enumerate(DILATIONS):
        daq.append(_attn_bwd_q(aq[p], ak[p], av[p], dattn[p], lse[p], dd[p], d, f"attn_bwd_q_d{d}"))
        dk_p, dv_p = _attn_bwd_kv(aq[p], ak[p], av[p], dattn[p], lse[p], dd[p], d, f"attn_bwd_kv_d{d}")
        dak.append(dk_p)
        dav.append(dv_p)

    do_dn, dvn, dgate, dstates, d_dn_gain = _dn_scan_bwd(dn_w, dn_qg, dn_kd, dn_aq, dn_egl, gate, small["dn_norm"], o_dn, ddn)
    (dqn, dkn, dvv, dbd, dn_small), recv_a = _dn_post(qn, kn, vv, bd, small["avec"], small["dvec"], dn_t, v_new, states,
                                                      dstates, do_dn, dvn, comm=_rsx_comm(parts_a) if dist else None)
    dcq, dck, dcv, dwq, dwk, dwv = _conv_bwd_pre(xq, xk, xv, conv_w, dqn, dkn, dvv)
    dxq, dxk, dxv = _conv_bwd_x(dcq, dck, dcv, conv_w)
    d_conv = jnp.concatenate([dwq[:CONV_WIDTH], dwk[:CONV_WIDTH], dwv[:CONV_WIDTH]], axis=1)

    dx1, d_gm, dproj = _inproj_bwd(dx2, x1, gm, [daq, dak, dav], [dxq, dxk, dxv, dgate], dbd, wts["w_in"])
    gi = _matmul_tn(dproj, h2, IN_COLS_PADDED, 512, "dw_in")
    if dist:
        gate_end = QKV_COLS + DN_WIDTH
        gi = jnp.concatenate([gi[:QKV_COLS], gi[gate_end:gate_end + LOGIT_COLS], gi[QKV_COLS:gate_end]], axis=0)
        gi = gi.reshape(N_CHIPS, IN_COLS // N_CHIPS, D_MODEL)
        gi = jnp.pad(gi, ((0, 0), (0, W_IN_ROWS - IN_COLS // N_CHIPS), (0, 0)))
    grads["w_in"] = gi
    group_b = ("w_in", "w_out")
    parts_b = reduce_start([grads[n] for n in group_b], "b") if dist else None

    (dx0, d_g1, dfg1, dfu1, act1, dout1), _ = _ffn_bwd(dx1, x, g1, fg1, fu1, wts["ffn1_down"], wts["ffn1_gate"],
                                                      wts["ffn1_up"], "ffn1_bwd")
    group_c = ("ffn1_gate", "ffn1_up", "ffn1_down")
    pending = parts_b if dist else []
    parts_c, recv_bc = [], []
    for n, (lhs, rhs) in zip(group_c, ((dfg1, h1), (dfu1, h1), (act1, dout1))):
        grads[n], landed = _dw_chunks(lhs, rhs, tk, "dw_" + n, comm=_rsx_comm(pending) if dist else None)
        recv_bc += list(landed)
        if dist:
            pending = reduce_start([grads[n]], n)
            parts_c += pending

    small_grads = dict(norm_ffn1=d_g1, norm_mix=d_gm, norm_ffn2=d_g2, norm_final=d_gf, conv_w=d_conv,
                       a_log=dn_small[0:1], dt_bias=dn_small[1:2], dn_norm=d_dn_gain[0:1])
    if dist:
        recv_bc += _rs_exchange_arrays(pending)
        recv_b, recv_c = recv_bc[:len(parts_b)], recv_bc[len(parts_b):]
        names = group_a + group_b + group_c
        totals = _rs_add_totals(list(parts_a) + list(parts_b) + list(parts_c), list(recv_a) + list(recv_b) + list(recv_c),
                                dist["chip"])
        theirs = _swap_sibling(totals, False, "rs_share_total")
        grads = {n: (mine, other) for n, mine, other in zip(names, totals, theirs)}
    return loss, dx0, grads, small_grads


HBM =pl.BlockSpec(memory_space=pl.ANY)
VMEM_SPEC = pl.BlockSpec(memory_space=pltpu.VMEM)


def _coords():
    return lax.axis_index("x"), lax.axis_index("y"), lax.axis_index("c")


def _remote(src, dst, send_sems, recv_sems, k, dev):
    return pltpu.make_async_remote_copy(src_ref=src, dst_ref=dst, send_sem=send_sems.at[k], recv_sem=recv_sems.at[k],
                                        device_id=dev, device_id_type=MESH)


def _allreduce_small(buf, name):
    R, Cc = buf.shape

    def body(src_ref, out_ref, recv_ref, send_sems, recv_sems):
        x, y, c = _coords()
        copies = []
        for m in range(1, 8):
            fx, fy, fc = (m >> 2) & 1, (m >> 1) & 1, m & 1
            dev = (x ^ fx if fx else x, y ^ fy if fy else y, c ^ fc if fc else c)
            cp = _remote(src_ref, recv_ref.at[m - 1], send_sems, recv_sems, m - 1, dev)
            cp.start()
            copies.append(cp)
        for cp in copies:
            cp.wait()
        r = [src_ref[...]] + [recv_ref[m] for m in range(7)]
        out_ref[...] = ((r[0] + r[1]) + (r[2] + r[3])) + ((r[4] + r[5]) + (r[6] + r[7]))

    return pl.pallas_call(
        body, name=name, out_shape=jax.ShapeDtypeStruct((R, Cc), F32),
        in_specs=[VMEM_SPEC], out_specs=VMEM_SPEC,
        scratch_shapes=[pltpu.VMEM((7, R, Cc), F32), pltpu.SemaphoreType.DMA((7,)), pltpu.SemaphoreType.DMA((7,))],
    )(buf)


BIG = ("ffn1_gate", "ffn1_up", "ffn1_down", "w_in", "w_out", "ffn2_gate", "ffn2_up", "ffn2_down")
ROW_SHARDED = ("ffn1_down", "w_out", "ffn2_down")
W_IN_ROWS = 960


def _rows(ref, start, size):
    return ref.at[pl.ds(pl.multiple_of(start, 16), size)]


def _allgather_arrays(shards):
    n = len(shards)

    def body(*refs):
        srcs, outs, send_sems, recv_sems = refs[:n], refs[n:2 * n], refs[2 * n], refs[2 * n + 1]
        x, y, c = _coords()
        sib = (x, y, 1 - c)
        xn, yn, dg = (1 - x, y), (x, 1 - y), (1 - x, 1 - y)
        slot = lambda out, chip: out.at[2 * chip[0] + chip[1]]
        started = []

        def go(cp):
            cp.start()
            started.append(cp)

        for a, (src, out) in enumerate(zip(srcs, outs)):
            h = src.shape[0] // 2
            cp = lambda s, d, k, dev: _remote(s, d, send_sems, recv_sems, 8 * a + k, dev)
            go(cp(src, slot(out, (x, y)), 6, sib))
            mine, dst = _rows(src, c * h, h), _rows(slot(out, (x, y)), c * h, h)
            go(cp(mine, dst, 0, (*xn, c)))
            go(cp(mine, dst, 1, (*yn, c)))
        for a, (src, out) in enumerate(zip(srcs, outs)):
            h = src.shape[0] // 2
            q = h // 2
            cp = lambda s, d, k, dev: _remote(s, d, send_sems, recv_sems, 8 * a + k, dev)
            from_x, from_y = _rows(slot(out, xn), c * h, h), _rows(slot(out, yn), c * h, h)
            cp(from_x, from_x, 0, sib).wait_recv()
            first = _rows(slot(out, xn), c * h, q)
            go(cp(first, first, 2, (*yn, c)))
            go(cp(from_x, from_x, 3, sib))
            cp(from_y, from_y, 1, sib).wait_recv()
            second = _rows(slot(out, yn), c * h + q, q)
            go(cp(second, second, 7, (*xn, c)))
            go(cp(from_y, from_y, 4, sib))
        for a, (src, out) in enumerate(zip(srcs, outs)):
            h = src.shape[0] // 2
            q = h // 2
            cp = lambda s, d, k, dev: _remote(s, d, send_sems, recv_sems, 8 * a + k, dev)
            first, second = _rows(slot(out, dg), c * h, q), _rows(slot(out, dg), c * h + q, q)
            cp(first, first, 2, sib).wait_recv()
            cp(second, second, 7, sib).wait_recv()
            from_d = _rows(slot(out, dg), c * h, h)
            go(cp(from_d, from_d, 5, sib))
        for a, (src, out) in enumerate(zip(srcs, outs)):
            h = src.shape[0] // 2
            cp = lambda s, d, k, dev: _remote(s, d, send_sems, recv_sems, 8 * a + k, dev)
            for k, chip in ((3, xn), (4, yn), (5, dg)):
                theirs = _rows(slot(out, chip), (1 - c) * h, h)
                cp(theirs, theirs, k, sib).wait_recv()
            cp(src, slot(out, (x, y)), 6, sib).wait_recv()
        for cp in started:
            cp.wait_send()

    shapes = [jax.ShapeDtypeStruct((N_CHIPS,) + s.shape, s.dtype) for s in shards]
    return pl.pallas_call(
        body, name="allgather_weights", out_shape=shapes, in_specs=[HBM] * n, out_specs=[HBM] * n,
        scratch_shapes=[pltpu.SemaphoreType.DMA((8 * n,)), pltpu.SemaphoreType.DMA((8 * n,))],
    )(*shards)


def _ag_copies(srcs, outs, send_sems, recv_sems):
    x, y, c = _coords()
    sib = (x, y, 1 - c)
    me = 2 * x + y
    others = [(1 - x, y), (x, 1 - y), (1 - x, 1 - y)]
    plan = []
    for a, (src, out) in enumerate(zip(srcs, outs)):
        h = src.shape[0] // 2
        cp = lambda s, d, k, dev: _remote(s, d, send_sems, recv_sems, 7 * a + k, dev)
        own = cp(src, out.at[me], 6, sib)
        sends = [cp(_rows(src, c * h, h), _rows(out.at[me], c * h, h), j, (ox, oy, c)) for j, (ox, oy) in enumerate(others)]
        mine = [_rows(out.at[2 * ox + oy], c * h, h) for ox, oy in others]
        theirs = [_rows(out.at[2 * ox + oy], (1 - c) * h, h) for ox, oy in others]
        arrivals = [cp(m, m, j, sib) for j, m in enumerate(mine)]
        forwards = [cp(m, m, 3 + j, sib) for j, m in enumerate(mine)]
        forwarded = [cp(t, t, 3 + j, sib) for j, t in enumerate(theirs)]
        plan.append((own, sends, forwards, arrivals, forwarded))
    return plan


def _ag_start(srcs, outs, send_sems, recv_sems):
    for own, sends, _, _, _ in _ag_copies(srcs, outs, send_sems, recv_sems):
        own.start()
        for cp in sends:
            cp.start()


def _ag_finish(srcs, outs, send_sems, recv_sems):
    plan = _ag_copies(srcs, outs, send_sems, recv_sems)
    for _, _, forwards, arrivals, _ in plan:
        for arrived, fwd in zip(arrivals, forwards):
            arrived.wait_recv()
            fwd.start()
    for own, sends, forwards, _, forwarded in plan:
        for cp in forwarded:
            cp.wait_recv()
        own.wait_recv()
        for cp in [own] + sends + forwards:
            cp.wait_send()


def _ag_comm(shards):
    shapes = [jax.ShapeDtypeStruct((N_CHIPS,) + s.shape, s.dtype) for s in shards]
    return (list(shards), shapes, 7 * len(shards), _ag_start, _ag_finish)


def _swap_sibling(arrs, pick_other_half, name):
    n = len(arrs)
    outs = [jax.ShapeDtypeStruct((a.shape[0], a.shape[1] // 2) + a.shape[2:] if pick_other_half else a.shape, a.dtype) for a in arrs]

    def body(*refs):
        srcs, dsts, send_sems, recv_sems = refs[:n], refs[n:2 * n], refs[2 * n], refs[2 * n + 1]
        x, y, c = _coords()
        cps = []
        for a in range(n):
            src = srcs[a]
            if pick_other_half:
                h = src.shape[1] // 2
                src = src.at[:, pl.ds(pl.multiple_of((1 - c) * h, 16), h)]
            cp = _remote(src, dsts[a], send_sems, recv_sems, a, (x, y, 1 - c))
            cp.start()
            cps.append(cp)
        for cp in cps:
            cp.wait()

    return pl.pallas_call(
        body, name=name, out_shape=outs, in_specs=[HBM] * n, out_specs=[HBM] * n,
        scratch_shapes=[pltpu.SemaphoreType.DMA((n,)), pltpu.SemaphoreType.DMA((n,))],
    )(*arrs)


def _rs_add_pairs(gs, others, c, name):
    n = len(gs)
    blocks = [(g.shape[1] // 4, g.shape[2]) for g in gs]

    def body(c_ref, *refs):
        for a in range(n):
            refs[2 * n + a][...] = (refs[a][...] + refs[n + a][...]).astype(BF16)

    mine = lambda b: pl.BlockSpec((None,) + b, lambda j, s, c_ref: (j, c_ref[0] * 2 + s, 0))
    flat = lambda b: pl.BlockSpec((None,) + b, lambda j, s, c_ref: (j, s, 0))
    return pl.pallas_call(
        body, name=name,
        grid_spec=pltpu.PrefetchScalarGridSpec(
            num_scalar_prefetch=1, grid=(N_CHIPS, 2),
            in_specs=[mine(b) for b in blocks] + [flat(b) for b in blocks],
            out_specs=[flat(b) for b in blocks]),
        out_shape=[jax.ShapeDtypeStruct(o.shape, BF16) for o in others],
        compiler_params=_params(2),
    )(c, *gs, *others)


def _rs_exchange_arrays(parts):
    n = len(parts)

    def body(*refs):
        _rsx_start(refs[:n], refs[n:2 * n], refs[2 * n], refs[2 * n + 1])
        _rsx_finish(refs[:n], refs[n:2 * n], refs[2 * n], refs[2 * n + 1])

    _, shapes, n_sems, _, _ = _rsx_comm(parts)
    return pl.pallas_call(
        body, name="rs_exchange_chips", out_shape=shapes, in_specs=[HBM] * n, out_specs=[HBM] * n,
        scratch_shapes=[pltpu.SemaphoreType.DMA((n_sems,)), pltpu.SemaphoreType.DMA((n_sems,))],
    )(*parts)


def _rsx_copies(srcs, dsts, send_sems, recv_sems):
    x, y, c = _coords()
    others = [(1 - x, y), (x, 1 - y), (1 - x, 1 - y)]
    return [_remote(src.at[2 * ox + oy], dst.at[k], send_sems, recv_sems, 3 * a + k, (ox, oy, c))
            for a, (src, dst) in enumerate(zip(srcs, dsts)) for k, (ox, oy) in enumerate(others)]


def _rsx_start(srcs, dsts, send_sems, recv_sems):
    for cp in _rsx_copies(srcs, dsts, send_sems, recv_sems):
        cp.start()


def _rsx_finish(srcs, dsts, send_sems, recv_sems):
    for cp in _rsx_copies(srcs, dsts, send_sems, recv_sems):
        cp.wait()


def _rsx_comm(parts):
    shapes = [jax.ShapeDtypeStruct((3,) + p.shape[1:], p.dtype) for p in parts]
    return (list(parts), shapes, 3 * len(parts), _rsx_start, _rsx_finish)


def _rs_add_totals(parts, recvs, chip):
    n = len(parts)
    blocks = [(p.shape[1] // 2, p.shape[2]) for p in parts]

    def body(chip_ref, *refs):
        f = lambda r: r[...].astype(F32)
        for a in range(n):
            p, r0, r1, r2 = refs[a], refs[n + 3 * a], refs[n + 3 * a + 1], refs[n + 3 * a + 2]
            refs[4 * n + a][...] = (f(p) + f(r0)) + (f(r1) + f(r2))

    own = lambda b: pl.BlockSpec((None,) + b, lambda s, chip_ref: (chip_ref[0], s, 0))
    slot = lambda b, k: pl.BlockSpec((None,) + b, lambda s, chip_ref, k=k: (k, s, 0))
    recv_specs = [slot(b, k) for b in blocks for k in range(3)]
    recv_args = [r for r in recvs for _ in range(3)]
    return pl.pallas_call(
        body, name="rs_add_totals",
        grid_spec=pltpu.PrefetchScalarGridSpec(
            num_scalar_prefetch=1, grid=(2,),
            in_specs=[own(b) for b in blocks] + recv_specs,
            out_specs=[pl.BlockSpec(b, lambda s, chip_ref: (s, 0)) for b in blocks]),
        out_shape=[jax.ShapeDtypeStruct(p.shape[1:], F32) for p in parts],
        compiler_params=_params(1),
    )(chip, *parts, *recv_args)


def _permute_w_in(wt):
    return jnp.concatenate([wt[:QKV_COLS], wt[QKV_COLS + LOGIT_COLS:IN_COLS], wt[QKV_COLS:QKV_COLS + LOGIT_COLS],
                            jnp.zeros((IN_COLS_PADDED - IN_COLS, wt.shape[1]), wt.dtype)], axis=0)


def _pad_row(v):
    v = v.reshape(1, -1)
    return jnp.pad(v, ((0, 0), (0, D_MODEL - v.shape[1])))


def kernel(x, norm_ffn1, ffn1_gate, ffn1_up, ffn1_down, norm_mix, w_in, conv_w, a_log, dt_bias, dn_norm, w_out, norm_ffn2, ffn2_gate, ffn2_up, ffn2_down, norm_final, loss_target, m_norm_ffn1, m_ffn1_gate, m_ffn1_up, m_ffn1_down, m_norm_mix, m_w_in, m_conv_w, m_a_log, m_dt_bias, m_dn_norm, m_w_out, m_norm_ffn2, m_ffn2_gate, m_ffn2_up, m_ffn2_down, m_norm_final, v_norm_ffn1, v_ffn1_gate, v_ffn1_up, v_ffn1_down, v_norm_mix, v_w_in, v_conv_w, v_a_log, v_dt_bias, v_dn_norm, v_w_out, v_norm_ffn2, v_ffn2_gate, v_ffn2_up, v_ffn2_down, v_norm_final):
    cx, cy, cc = _coords()
    chip = 2 * cx + cy
    stored = lambda t, n: t[0] if n in ROW_SHARDED else t[0].T
    big_w = {n: stored(t, n) for n, t in dict(
        ffn1_gate=ffn1_gate, ffn1_up=ffn1_up, ffn1_down=ffn1_down, w_in=w_in, w_out=w_out,
        ffn2_gate=ffn2_gate, ffn2_up=ffn2_up, ffn2_down=ffn2_down).items()}
    big_m = {n: stored(t, n) for n, t in dict(
        ffn1_gate=m_ffn1_gate, ffn1_up=m_ffn1_up, ffn1_down=m_ffn1_down, w_in=m_w_in, w_out=m_w_out,
        ffn2_gate=m_ffn2_gate, ffn2_up=m_ffn2_up, ffn2_down=m_ffn2_down).items()}
    big_v = {n: stored(t, n) for n, t in dict(
        ffn1_gate=v_ffn1_gate, ffn1_up=v_ffn1_up, ffn1_down=v_ffn1_down, w_in=v_w_in, w_out=v_w_out,
        ffn2_gate=v_ffn2_gate, ffn2_up=v_ffn2_up, ffn2_down=v_ffn2_down).items()}

    cols = IN_COLS // N_CHIPS
    send = {n: big_w[n].astype(BF16) for n in BIG}
    send["w_in"] = jnp.pad(send["w_in"], ((0, W_IN_ROWS - cols), (0, 0)))
    early = tuple(n for n in BIG if n not in LATE_WEIGHTS)
    wts = dict(zip(early, _allgather_arrays([send[n] for n in early])))
    wts["w_in"] = _permute_w_in(wts["w_in"][:, :cols].reshape(IN_COLS, D_MODEL))
    dist = dict(late=[send[n] for n in LATE_WEIGHTS], c=cc.reshape(1).astype(jnp.int32),
                chip=chip.reshape(1).astype(jnp.int32))

    conv_shard = conv_w[0]
    emb = jnp.concatenate([jnp.where((chip == j) & (cc == 0), conv_shard, 0.0) for j in range(N_CHIPS)], axis=1)
    emb = jnp.pad(emb.reshape(6, D_MODEL), ((0, 2), (0, 0)))
    conv_full = _allreduce_small(emb, "allgather_conv_w")[:6].reshape(CONV_WIDTH, 3 * DN_WIDTH)

    zvec = jnp.zeros((1, 128), F32)
    small = dict(norm_ffn1=norm_ffn1, norm_mix=norm_mix, norm_ffn2=norm_ffn2, norm_final=norm_final[None],
                 conv_w=conv_full, avec=zvec.at[0, DN_HEADS:2 * DN_HEADS].set(a_log[0]),
                 dvec=zvec.at[0, DN_HEADS:2 * DN_HEADS].set(dt_bias[0]), dn_norm=dn_norm)

    loss, grad_x, reduced, sg = _local_step(x[0], loss_target[0], wts, small, dist)

    rows = [sg["norm_ffn1"], sg["norm_mix"], sg["norm_ffn2"], sg["norm_final"], _pad_row(sg["a_log"]), _pad_row(sg["dt_bias"]),
            _pad_row(sg["dn_norm"]), _pad_row(loss[0:1]), sg["conv_w"].reshape(6, D_MODEL), jnp.zeros((2, D_MODEL), F32)]
    red = _allreduce_small(jnp.concatenate(rows, axis=0), "allreduce_small")
    loss_out = red[7, 0]
    g_conv_full = red[8:14].reshape(CONV_WIDTH, 3 * DN_WIDTH)
    g_conv = lax.dynamic_slice_in_dim(g_conv_full, chip * (3 * DN_WIDTH // N_CHIPS), 3 * DN_WIDTH // N_CHIPS, axis=1)
    g_small = dict(norm_ffn1=red[0:1], norm_mix=red[1:2], norm_ffn2=red[2:3], norm_final=red[3],
                   a_log=red[4:5, DN_HEADS:2 * DN_HEADS], dt_bias=red[5:6, DN_HEADS:2 * DN_HEADS], dn_norm=red[6:7, :DN_HEAD_DIM])

    out_g, out_d, out_m, out_v = {}, {}, {}, {}
    for n in BIG:
        mine, other = reduced[n]
        g = jnp.where(cc == 0, jnp.concatenate([mine, other], axis=0), jnp.concatenate([other, mine], axis=0))
        if n == "w_in":
            to3 = lambda t: jnp.transpose(t, (2, 0, 1))
            g = g[:cols].reshape(cols, 1, D_MODEL)
            results = (g,) + tuple(_adamw(to3(w_in), g, to3(m_w_in), to3(v_w_in), "adamw_w_in"))
            out_g[n], out_d[n], out_m[n], out_v[n] = (jnp.transpose(t, (1, 2, 0)) for t in results)
            continue
        results = (g,) + tuple(_adamw(big_w[n], g, big_m[n], big_v[n], "adamw_" + n))
        out_g[n], out_d[n], out_m[n], out_v[n] = ((t if n in ROW_SHARDED else t.T)[None] for t in results)
    d, nm, nv = _adamw(conv_w[0], g_conv, m_conv_w[0], v_conv_w[0], "adamw_conv_w")
    out_g["conv_w"], out_d["conv_w"], out_m["conv_w"], out_v["conv_w"] = g_conv[None], d[None], nm[None], nv[None]

    small_names = ("norm_ffn1", "norm_mix", "norm_ffn2", "norm_final", "a_log", "dt_bias", "dn_norm")
    small_w = dict(norm_ffn1=norm_ffn1, norm_mix=norm_mix, norm_ffn2=norm_ffn2, norm_final=norm_final, a_log=a_log,
                   dt_bias=dt_bias, dn_norm=dn_norm)
    small_m = dict(norm_ffn1=m_norm_ffn1, norm_mix=m_norm_mix, norm_ffn2=m_norm_ffn2, norm_final=m_norm_final, a_log=m_a_log,
                   dt_bias=m_dt_bias, dn_norm=m_dn_norm)
    small_v = dict(norm_ffn1=v_norm_ffn1, norm_mix=v_norm_mix, norm_ffn2=v_norm_ffn2, norm_final=v_norm_final, a_log=v_a_log,
                   dt_bias=v_dt_bias, dn_norm=v_dn_norm)
    stack = lambda dct: jnp.concatenate([_pad_row(dct[n]) for n in small_names] + [jnp.zeros((1, D_MODEL), F32)], axis=0)
    d, nm, nv = _adamw(stack(small_w), stack(g_small), stack(small_m), stack(small_v), "adamw_small")
    for k, n in enumerate(small_names):
        shape = small_w[n].shape
        size = math.prod(shape)
        out_g[n] = g_small[n].reshape(shape)
        out_d[n], out_m[n], out_v[n] = (t[k, :size].reshape(shape) for t in (d, nm, nv))

    order = ("norm_ffn1", "ffn1_gate", "ffn1_up", "ffn1_down", "norm_mix", "w_in", "conv_w", "a_log", "dt_bias", "dn_norm",
             "w_out", "norm_ffn2", "ffn2_gate", "ffn2_up", "ffn2_down", "norm_final")
    return (loss_out, grad_x[None], *[out_g[n] for n in order], *[out_d[n] for n in order],
            *[out_m[n] for n in order], *[out_v[n] for n in order])
```

```python
import functools
import math

import jax
import jax.numpy as jnp
from jax import lax
from jax.experimental import pallas as pl
from jax.experimental.pallas import tpu as pltpu

F32 = jnp.float32
BF16 = jnp.bfloat16
HI = lax.Precision.HIGH

D_MODEL = 1024
ATTN_HEADS = 8
ATTN_WIDTH = 512
ATTN_BLOCK = 128
ATTN_SCALE = (ATTN_WIDTH // ATTN_HEADS) ** -0.5
DILATIONS = (1, 4, 16)
DN_HEADS = 4
DN_HEAD_DIM = 128
DN_WIDTH = 512
DN_CHUNK = 64
CONV_WIDTH = 4
NORM_EPS = 1e-6
L2_EPS = 1e-6
QKV_COLS = 3 * ATTN_WIDTH + 3 * DN_WIDTH
LOGIT_COLS = 2 * DN_HEADS
IN_COLS = QKV_COLS + LOGIT_COLS + DN_WIDTH
IN_COLS_PADDED = 3712
N_CHIPS = 4

ADAM_LR = 0.001
ADAM_B1 = 0.9
ADAM_B2 = 0.999
ADAM_EPS = 1e-08
ADAM_WD = 0.01
ADAM_STEP = 10

VMEM_LIMIT = 56 * 1024 * 1024
NEG_BIG = -1e30
MESH = pl.DeviceIdType.MESH


def _params(n_grid, vmem=VMEM_LIMIT):
    return pltpu.CompilerParams(dimension_semantics=("arbitrary",) * n_grid, vmem_limit_bytes=vmem)


def _call(body, args, *, name, grid, in_specs, out_specs, out_shape, scratch_shapes=(), comm=None):
    n_in, n_out, n_scr = len(in_specs), len(out_specs), len(scratch_shapes)
    hbm = pl.BlockSpec(memory_space=pl.ANY)
    srcs, dst_shapes, n_sems, start, finish = comm if comm is not None else ((), (), 0, None, None)
    ns, nd = len(srcs), len(dst_shapes)

    def full(*refs):
        ins, c_src = refs[:n_in], refs[n_in:n_in + ns]
        at = n_in + ns
        outs, c_dst = refs[at:at + n_out], refs[at + n_out:at + n_out + nd]
        scr = refs[at + n_out + nd:at + n_out + nd + n_scr]
        if comm is not None:
            ids = [pl.program_id(a) for a in range(len(grid))]
            first = functools.reduce(jnp.logical_and, [i == 0 for i in ids])
            last = functools.reduce(jnp.logical_and, [i == g - 1 for i, g in zip(ids, grid)])

            @pl.when(first)
            def _():
                start(c_src, c_dst, refs[-2], refs[-1])

        body(*ins, *outs, *scr)
        if comm is not None:
            @pl.when(last)
            def _():
                finish(c_src, c_dst, refs[-2], refs[-1])

    sems = [pltpu.SemaphoreType.DMA((n_sems,)), pltpu.SemaphoreType.DMA((n_sems,))] if comm is not None else []
    res = pl.pallas_call(
        full, name=name, grid=grid, in_specs=list(in_specs) + [hbm] * ns, out_specs=list(out_specs) + [hbm] * nd,
        out_shape=list(out_shape) + list(dst_shapes), scratch_shapes=list(scratch_shapes) + sems,
        compiler_params=_params(len(grid)),
    )(*args, *srcs)
    return res[:n_out], res[n_out:]


def _nt(a, b, precision=None):
    return lax.dot_general(a, b, (((1,), (1,)), ((), ())), preferred_element_type=F32, precision=precision)


def _tn(a, b, precision=None):
    return lax.dot_general(a, b, (((0,), (0,)), ((), ())), preferred_element_type=F32, precision=precision)


def _nn(a, b, precision=None):
    return jnp.dot(a, b, preferred_element_type=F32, precision=precision)


def _sigmoid(x):
    return 1.0 / (1.0 + jnp.exp(-x))


def _loss_head(xf, gain, target):
    r = lax.rsqrt(jnp.mean(xf * xf, axis=-1, keepdims=True) + NORM_EPS)
    xhat = xf * r
    err = xhat * gain - target
    part = 0.5 * jnp.sum(jnp.mean(err * err, axis=-1, keepdims=True), axis=0, keepdims=True)
    dy = err * (1.0 / xf.shape[-1])
    dgain = jnp.sum(dy * xhat, axis=0, keepdims=True)
    dxh = dy * gain
    return part, r * (dxh - xhat * jnp.mean(dxh * xhat, axis=-1, keepdims=True)), dgain


def _ffn_fwd(x, gain, wg, wu, wd, name, comm=None, head=None):
    S, D = x.shape
    nf, tf, _ = wg.shape
    tm = 512
    n_in = 5 if head is None else 7

    def body(*refs):
        x_ref, gain_ref, wg_ref, wu_ref, wd_ref = refs[:5]
        xo_ref, h_ref, g_ref, u_ref = refs[n_in:n_in + 4]
        acc_ref, hs_ref = refs[-2:]
        i = pl.program_id(0)
        j = pl.program_id(1)

        @pl.when(j == 0)
        def _():
            xf = x_ref[...]
            r = lax.rsqrt(jnp.mean(xf * xf, axis=-1, keepdims=True) + NORM_EPS)
            h = (xf * r * gain_ref[...]).astype(BF16)
            hs_ref[...] = h
            h_ref[...] = h
            acc_ref[...] = jnp.zeros_like(acc_ref)

        h = hs_ref[...]
        g = _nt(h, wg_ref[...])
        u = _nt(h, wu_ref[...])
        g_ref[...] = g.astype(BF16)
        u_ref[...] = u.astype(BF16)
        act = g * _sigmoid(g) * u
        acc_ref[...] += _nn(act.astype(BF16), wd_ref[...])

        if head is not None:
            hgain_ref, t_ref = refs[5:7]
            loss_ref, dgain_ref = refs[n_in + 4:n_in + 6]

            @pl.when((i == 0) & (j == 0))
            def _():
                loss_ref[...] = jnp.zeros_like(loss_ref)
                dgain_ref[...] = jnp.zeros_like(dgain_ref)

        @pl.when(j == nf - 1)
        def _():
            xo = x_ref[...] + 0.5 * acc_ref[...]
            if head is None:
                xo_ref[...] = xo
            else:
                part, dxo, dgain = _loss_head(xo, hgain_ref[...], t_ref[...])
                first = ((lax.broadcasted_iota(jnp.int32, (8, 128), 0) == 0)
                         & (lax.broadcasted_iota(jnp.int32, (8, 128), 1) == 0))
                loss_ref[...] += jnp.where(first, part, 0.0)
                dgain_ref[...] += dgain
                xo_ref[...] = dxo

    tok = pl.BlockSpec((tm, D), lambda i, j: (i, 0))
    row = pl.BlockSpec((1, D), lambda i, j: (0, 0))
    chunk = pl.BlockSpec((None, tf, D), lambda i, j: (j, 0, 0))
    act = pl.BlockSpec((None, tm, tf), lambda i, j: (j, i, 0))
    extra_in = [] if head is None else [row, tok]
    extra_out = [] if head is None else [pl.BlockSpec((8, 128), lambda i, j: (0, 0)), row]
    extra_shape = [] if head is None else [jax.ShapeDtypeStruct((8, 128), F32), jax.ShapeDtypeStruct((1, D), F32)]
    return _call(
        body, (x, gain, wg, wu, wd) + (() if head is None else tuple(head)), name=name, grid=(S // tm, nf), comm=comm,
        in_specs=[tok, row, chunk, chunk, chunk] + extra_in,
        out_specs=[tok, tok, act, act] + extra_out,
        out_shape=[jax.ShapeDtypeStruct((S, D), F32), jax.ShapeDtypeStruct((S, D), BF16),
                   jax.ShapeDtypeStruct((nf, S, tf), BF16), jax.ShapeDtypeStruct((nf, S, tf), BF16)] + extra_shape,
        scratch_shapes=[pltpu.VMEM((tm, D), F32), pltpu.VMEM((tm, D), BF16)])


def _rmsnorm_bwd(dh, xf, gain):
    r = lax.rsqrt(jnp.mean(xf * xf, axis=-1, keepdims=True) + NORM_EPS)
    xhat = xf * r
    dgain = jnp.sum(dh * xhat, axis=0, keepdims=True)
    dxh = dh * gain
    dx = r * (dxh - xhat * jnp.mean(dxh * xhat, axis=-1, keepdims=True))
    return dx, dgain


def _ffn_bwd(dxo, x, gain, g, u, wd, wg, wu, name, comm=None):
    S, D = x.shape
    nf, _, tf = g.shape
    tm = 512

    def body(dxo_ref, x_ref, gain_ref, g_ref, u_ref, wd_ref, wg_ref, wu_ref,
             dx_ref, dgain_ref, dg_ref, du_ref, act_ref, dout_ref, acc_ref, ds_ref):
        i = pl.program_id(0)
        j = pl.program_id(1)

        @pl.when(j == 0)
        def _():
            d = (0.5 * dxo_ref[...]).astype(BF16)
            ds_ref[...] = d
            dout_ref[...] = d
            acc_ref[...] = jnp.zeros_like(acc_ref)

        @pl.when((i == 0) & (j == 0))
        def _():
            dgain_ref[...] = jnp.zeros_like(dgain_ref)

        for half in range(2):
            rows = slice(half * (tm // 2), (half + 1) * (tm // 2))
            dact = _nt(ds_ref[rows, :], wd_ref[...])
            gv = g_ref[rows, :].astype(F32)
            uv = u_ref[rows, :].astype(F32)
            sg = _sigmoid(gv)
            silu = gv * sg
            act_ref[rows, :] = (silu * uv).astype(BF16)
            dgv = (dact * uv * (sg * (1.0 + gv * (1.0 - sg)))).astype(BF16)
            duv = (dact * silu).astype(BF16)
            dg_ref[rows, :] = dgv
            du_ref[rows, :] = duv
            acc_ref[rows, :] += _nn(dgv, wg_ref[...]) + _nn(duv, wu_ref[...])

        @pl.when(j == nf - 1)
        def _():
            dx, dgain = _rmsnorm_bwd(acc_ref[...], x_ref[...], gain_ref[...])
            dx_ref[...] = dxo_ref[...] + dx
            dgain_ref[...] += dgain

    return _call(
        body, (dxo, x, gain, g, u, wd, wg, wu), name=name, grid=(S // tm, nf), comm=comm,
        in_specs=[pl.BlockSpec((tm, D), lambda i, j: (i, 0)),
                  pl.BlockSpec((tm, D), lambda i, j: (i, 0)),
                  pl.BlockSpec((1, D), lambda i, j: (0, 0)),
                  pl.BlockSpec((None, tm, tf), lambda i, j: (j, i, 0)),
                  pl.BlockSpec((None, tm, tf), lambda i, j: (j, i, 0)),
                  pl.BlockSpec((None, tf, D), lambda i, j: (j, 0, 0)),
                  pl.BlockSpec((None, tf, D), lambda i, j: (j, 0, 0)),
                  pl.BlockSpec((None, tf, D), lambda i, j: (j, 0, 0))],
        out_specs=[pl.BlockSpec((tm, D), lambda i, j: (i, 0)),
                   pl.BlockSpec((1, D), lambda i, j: (0, 0)),
                   pl.BlockSpec((None, tm, tf), lambda i, j: (j, i, 0)),
                   pl.BlockSpec((None, tm, tf), lambda i, j: (j, i, 0)),
                   pl.BlockSpec((None, tm, tf), lambda i, j: (j, i, 0)),
                   pl.BlockSpec((tm, D), lambda i, j: (i, 0))],
        out_shape=[jax.ShapeDtypeStruct((S, D), F32), jax.ShapeDtypeStruct((1, D), F32),
                   jax.ShapeDtypeStruct((nf, S, tf), BF16), jax.ShapeDtypeStruct((nf, S, tf), BF16),
                   jax.ShapeDtypeStruct((nf, S, tf), BF16), jax.ShapeDtypeStruct((S, D), BF16)],
        scratch_shapes=[pltpu.VMEM((tm, D), F32), pltpu.VMEM((tm, D), BF16)])


def _matmul_tn(a, b, tm, tk, name):
    K, M = a.shape
    N = b.shape[1]

    def body(a_ref, b_ref, o_ref):
        @pl.when(pl.program_id(1) == 0)
        def _():
            o_ref[...] = jnp.zeros_like(o_ref)

        o_ref[...] += _tn(a_ref[...], b_ref[...])

    return pl.pallas_call(
        body, name=name, grid=(M // tm, K // tk),
        in_specs=[pl.BlockSpec((tk, tm), lambda i, k: (k, i)),
                  pl.BlockSpec((tk, N), lambda i, k: (k, 0))],
        out_specs=pl.BlockSpec((tm, N), lambda i, k: (i, 0)),
        out_shape=jax.ShapeDtypeStruct((M, N), F32),
        compiler_params=_params(2),
    )(a, b)


def _dw_chunks(a, b, tk, name, comm=None):
    nf, S, tf = a.shape
    N = b.shape[1]

    def body(a_ref, b_ref, o_ref):
        @pl.when(pl.program_id(1) == 0)
        def _():
            o_ref[...] = jnp.zeros_like(o_ref)

        o_ref[...] += _tn(a_ref[...], b_ref[...])

    (out,), landed = _call(
        body, (a, b), name=name, grid=(nf, S // tk), comm=comm,
        in_specs=[pl.BlockSpec((None, tk, tf), lambda j, k: (j, k, 0)),
                  pl.BlockSpec((tk, N), lambda j, k: (k, 0))],
        out_specs=[pl.BlockSpec((None, tf, N), lambda j, k: (j, 0, 0))],
        out_shape=[jax.ShapeDtypeStruct((nf, tf, N), F32)])
    return out, landed


VIEW_TILE = 512


def _view_spec(d, tile=VIEW_TILE):
    return pl.BlockSpec((tile // d, d * ATTN_WIDTH), lambda i: (i, 0))


def _view_shape(S, d, dtype):
    return jax.ShapeDtypeStruct((S // d, d * ATTN_WIDTH), dtype)


def _tile_to_views(val, planes, out_refs):
    for g in range(4):
        planes[g] = val[:, g * 128:(g + 1) * 128]
    for d, ref in zip(DILATIONS, out_refs):
        if d == 1:
            ref[...] = val.astype(ref.dtype)
            continue
        for r in range(d):
            for g in range(4):
                ref[:, r * ATTN_WIDTH + g * 128:r * ATTN_WIDTH + (g + 1) * 128] = (
                    planes[g, pl.ds(r, planes.shape[1] // d, stride=d), :].astype(ref.dtype))


def _view_to_tile(ref, d, planes):
    if d == 1:
        return ref[...].astype(F32)
    for r in range(d):
        for g in range(4):
            planes[g, pl.ds(r, planes.shape[1] // d, stride=d), :] = (
                ref[:, r * ATTN_WIDTH + g * 128:r * ATTN_WIDTH + (g + 1) * 128].astype(F32))
    return jnp.concatenate([planes[g] for g in range(4)], axis=1)


def _inproj_fwd(x, gain, w_in_p):
    S, D = x.shape
    tm = VIEW_TILE
    W = ATTN_WIDTH

    def body(x_ref, gain_ref, w_ref, h_ref, q1, q4, q16, k1, k4, k16, v1, v4, v16, dq_ref, dk_ref, dv_ref, gate_ref, bd_ref,
             planes):
        xf = x_ref[...]
        r = lax.rsqrt(jnp.mean(xf * xf, axis=-1, keepdims=True) + NORM_EPS)
        h = (xf * r * gain_ref[...]).astype(BF16)
        h_ref[...] = h
        _tile_to_views(_nt(h, w_ref[0:W, :]) * ATTN_SCALE, planes, (q1, q4, q16))
        _tile_to_views(_nt(h, w_ref[W:2 * W, :]), planes, (k1, k4, k16))
        _tile_to_views(_nt(h, w_ref[2 * W:3 * W, :]), planes, (v1, v4, v16))
        dq_ref[...] = _nt(h, w_ref[3 * W:4 * W, :])
        dk_ref[...] = _nt(h, w_ref[4 * W:5 * W, :])
        dv_ref[...] = _nt(h, w_ref[5 * W:6 * W, :])
        gate_ref[...] = _nt(h, w_ref[6 * W:7 * W, :])
        bd_ref[...] = _nt(h, w_ref[7 * W:7 * W + 128, :])

    tok = lambda w: pl.BlockSpec((tm, w), lambda i: (i, 0))
    return pl.pallas_call(
        body, name="inproj_fwd", grid=(S // tm,),
        in_specs=[tok(D), pl.BlockSpec((1, D), lambda i: (0, 0)),
                  pl.BlockSpec((IN_COLS_PADDED, D), lambda i: (0, 0))],
        out_specs=[tok(D)] + [_view_spec(d) for d in DILATIONS] * 3 + [tok(W)] * 4 + [tok(128)],
        out_shape=[jax.ShapeDtypeStruct((S, D), BF16)] + [_view_shape(S, d, BF16) for d in DILATIONS] * 3
                  + [jax.ShapeDtypeStruct((S, W), F32)] * 4 + [jax.ShapeDtypeStruct((S, 128), F32)],
        scratch_shapes=[pltpu.VMEM((4, tm, 128), F32)],
        compiler_params=_params(1),
    )(x, gain, w_in_p)


def _inproj_bwd(dxo, x, gain, attn_grads, dsecs, dbd, w_in_p):
    S, D = x.shape
    tm = VIEW_TILE
    W = ATTN_WIDTH

    def body(dxo_ref, x_ref, gain_ref, *rest):
        views, (s3, s4, s5, s6, dbd_ref, w_ref, dx_ref, dgain_ref, dproj_ref, planes) = rest[:9], rest[9:]

        @pl.when(pl.program_id(0) == 0)
        def _():
            dgain_ref[...] = jnp.zeros_like(dgain_ref)

        secs = []
        for k in range(3):
            parts = [_view_to_tile(views[3 * k + p], d, planes) for p, d in enumerate(DILATIONS)]
            secs.append(parts[0] + parts[1] + parts[2])
        secs += [s3[...], s4[...], s5[...], s6[...]]
        dh = jnp.zeros((tm, D), F32)
        for k, s in enumerate(secs):
            d = s.astype(BF16)
            dproj_ref[:, k * W:(k + 1) * W] = d
            dh += _nn(d, w_ref[k * W:(k + 1) * W, :])
        d = dbd_ref[...].astype(BF16)
        dproj_ref[:, 7 * W:7 * W + 128] = d
        dh += _nn(d, w_ref[7 * W:7 * W + 128, :])
        dx, dgain = _rmsnorm_bwd(dh, x_ref[...], gain_ref[...])
        dx_ref[...] = dxo_ref[...] + dx
        dgain_ref[...] += dgain

    tok = lambda w: pl.BlockSpec((tm, w), lambda i: (i, 0))
    return pl.pallas_call(
        body, name="inproj_bwd", grid=(S // tm,),
        in_specs=[tok(D), tok(D), pl.BlockSpec((1, D), lambda i: (0, 0))] + [_view_spec(d, tm) for d in DILATIONS] * 3
                 + [tok(W)] * 4 + [tok(128)] + [pl.BlockSpec((IN_COLS_PADDED, D), lambda i: (0, 0))],
        out_specs=[tok(D), pl.BlockSpec((1, D), lambda i: (0, 0)), tok(IN_COLS_PADDED)],
        out_shape=[jax.ShapeDtypeStruct((S, D), F32), jax.ShapeDtypeStruct((1, D), F32),
                   jax.ShapeDtypeStruct((S, IN_COLS_PADDED), BF16)],
        scratch_shapes=[pltpu.VMEM((4, tm, 128), F32)],
        compiler_params=_params(1),
    )(dxo, x, gain, *[g for grads in attn_grads for g in grads], *dsecs, dbd, w_in_p)


def _slope(h):
    return 2.0 ** (-8.0 * (h + 1) / ATTN_HEADS)


def _head_bias(steps, d, heads=tuple(range(ATTN_HEADS))):
    stepsf = steps.astype(F32)
    return jnp.stack([stepsf * (-_slope(h) * d) for h in heads])


def _hnt(a, b):
    return lax.dot_general(a, b, (((2,), (2,)), ((0,), (0,))), preferred_element_type=F32)


def _hnn(a, b):
    return lax.dot_general(a, b, (((2,), (1,)), ((0,), (0,))), preferred_element_type=F32)


def _blocks_per_step(nb):
    return next(n for n in (4, 2, 1) if nb % n == 0)


def _query_step_specs(qb):
    B = ATTN_BLOCK
    cur = pl.BlockSpec((qb * B, ATTN_WIDTH), lambda r, n: (n, r))
    prev = pl.BlockSpec((B, ATTN_WIDTH), lambda r, n: (jnp.maximum(qb * n - 1, 0), r))
    return cur, prev


def _prev_block(prev_ref, cur_ref, sub, sl):
    B = ATTN_BLOCK
    return prev_ref[:, sl] if sub == 0 else cur_ref[(sub - 1) * B:sub * B, sl]


def _head_cols(tile, lo, big):
    return [_head_col(tile, lo, big), _head_col(tile, jnp.logical_not(lo), big)]


def _attn_fwd(q, k, v, d, name):
    L = q.shape[0]
    nb = L // ATTN_BLOCK
    B = ATTN_BLOCK
    QB = _blocks_per_step(nb)

    def body(q_ref, kp_ref, kc_ref, vp_ref, vc_ref, o_ref, lse_ref):
        n = pl.program_id(1)
        qi = lax.broadcasted_iota(jnp.int32, (B, 2 * B), 0)
        kj = lax.broadcasted_iota(jnp.int32, (B, 2 * B), 1)
        steps = qi + B - kj
        band = (steps >= 0) & (steps <= B)
        lo = lax.broadcasted_iota(jnp.int32, (B, 128), 1) < 64
        bias = _head_bias(steps, d)
        for sub in range(QB):
            rows = slice(sub * B, (sub + 1) * B)
            valid = band & ((kj >= B) | (n > 0)) if sub == 0 else band
            qs, ks, vs = [], [], []
            for G in range(4):
                sl = slice(G * 128, (G + 1) * 128)
                qg = q_ref[rows, sl]
                kg = jnp.concatenate([_prev_block(kp_ref, kc_ref, sub, sl), kc_ref[rows, sl]], axis=0)
                vg = jnp.concatenate([_prev_block(vp_ref, vc_ref, sub, sl), vc_ref[rows, sl]], axis=0)
                qs += [jnp.where(lo, qg, jnp.zeros_like(qg)), jnp.where(lo, jnp.zeros_like(qg), qg)]
                ks += [kg, kg]
                vs += [vg, vg]
            s = jnp.where(valid, _hnt(jnp.stack(qs), jnp.stack(ks)) + bias, NEG_BIG)
            m = jnp.max(s, axis=-1, keepdims=True)
            p = jnp.exp(s - m)
            l = jnp.sum(p, axis=-1, keepdims=True)
            o = _hnn(p.astype(BF16), jnp.stack(vs)) / l
            lse = m + jnp.log(l)
            for G in range(4):
                sl = slice(G * 128, (G + 1) * 128)
                o_ref[rows, sl] = jnp.where(lo, o[2 * G], o[2 * G + 1])
                lse_ref[rows, sl] = jnp.where(lo, lse[2 * G], lse[2 * G + 1])

    cur, prev = _query_step_specs(QB)
    return pl.pallas_call(
        body, name=name, grid=(d, nb // QB),
        in_specs=[cur, prev, cur, prev, cur],
        out_specs=[cur, cur],
        out_shape=[jax.ShapeDtypeStruct((L, d * ATTN_WIDTH), F32)] * 2,
        compiler_params=_params(2),
    )(q, k, k, v, v)


def _attn_merge(parts):
    S = parts[0][0].shape[0]
    tm = VIEW_TILE

    def body(o1, s1, o2, s2, o3, s3, o_ref, lse1, lse4, lse16, planes):
        outs, lses = [], []
        for d, (o, s) in zip(DILATIONS, ((o1, s1), (o2, s2), (o3, s3))):
            outs.append(_view_to_tile(o, d, planes))
            lses.append(_view_to_tile(s, d, planes))
        mx = jnp.maximum(jnp.maximum(lses[0], lses[1]), lses[2])
        es = [jnp.exp(s - mx) for s in lses]
        den = es[0] + es[1] + es[2]
        o_ref[...] = (es[0] * outs[0] + es[1] * outs[1] + es[2] * outs[2]) / den
        _tile_to_views(mx + jnp.log(den), planes, (lse1, lse4, lse16))

    views = [_view_spec(d) for d in DILATIONS]
    flat = [t for p in parts for t in p]
    return pl.pallas_call(
        body, name="attn_merge", grid=(S // tm,),
        in_specs=[views[p] for p in range(3) for _ in range(2)],
        out_specs=[views[0]] + views,
        out_shape=[jax.ShapeDtypeStruct((S, ATTN_WIDTH), F32)] + [_view_shape(S, d, F32) for d in DILATIONS],
        scratch_shapes=[pltpu.VMEM((4, tm, 128), F32)],
        compiler_params=_params(1),
    )(*flat)


def _head_col(t, msk, big):
    if big:
        return jnp.max(jnp.where(msk, t, NEG_BIG), axis=-1, keepdims=True)
    return jnp.sum(jnp.where(msk, t, 0.0), axis=-1, keepdims=True) * (1.0 / 64.0)


def _attn_bwd_q(q, k, v, do, lse, dd, d, name):
    L = q.shape[0]
    nb = L // ATTN_BLOCK
    B = ATTN_BLOCK
    QB = _blocks_per_step(nb)

    def body(q_ref, kp_ref, kc_ref, vp_ref, vc_ref, do_ref, lse_ref, dd_ref, dq_ref):
        n = pl.program_id(1)
        qi = lax.broadcasted_iota(jnp.int32, (B, 2 * B), 0)
        kj = lax.broadcasted_iota(jnp.int32, (B, 2 * B), 1)
        steps = qi + B - kj
        band = (steps >= 0) & (steps <= B)
        lo = lax.broadcasted_iota(jnp.int32, (B, 128), 1) < 64
        bias = _head_bias(steps, d)
        for sub in range(QB):
            rows = slice(sub * B, (sub + 1) * B)
            valid = band & ((kj >= B) | (n > 0)) if sub == 0 else band
            qs, ks, vs, dos, lses, dcols = [], [], [], [], [], []
            for G in range(4):
                sl = slice(G * 128, (G + 1) * 128)
                qg = q_ref[rows, sl]
                kg = jnp.concatenate([_prev_block(kp_ref, kc_ref, sub, sl), kc_ref[rows, sl]], axis=0)
                vg = jnp.concatenate([_prev_block(vp_ref, vc_ref, sub, sl), vc_ref[rows, sl]], axis=0)
                dog = do_ref[rows, sl]
                qs += [jnp.where(lo, qg, jnp.zeros_like(qg)), jnp.where(lo, jnp.zeros_like(qg), qg)]
                dos += [jnp.where(lo, dog, 0.0).astype(BF16), jnp.where(lo, 0.0, dog).astype(BF16)]
                ks += [kg, kg]
                vs += [vg, vg]
                lses += _head_cols(lse_ref[rows, sl], lo, True)
                dcols += _head_cols(dd_ref[rows, sl], lo, False)
            kb = jnp.stack(ks)
            s = _hnt(jnp.stack(qs), kb) + bias
            p = jnp.where(valid, jnp.exp(jnp.where(valid, s, NEG_BIG) - jnp.stack(lses)), 0.0)
            dp = _hnt(jnp.stack(dos), jnp.stack(vs))
            ds = p * (dp - jnp.stack(dcols))
            dq = _hnn(ds.astype(BF16), kb) * ATTN_SCALE
            for G in range(4):
                dq_ref[rows, G * 128:(G + 1) * 128] = jnp.where(lo, dq[2 * G], dq[2 * G + 1]).astype(BF16)

    cur, prev = _query_step_specs(QB)
    return pl.pallas_call(
        body, name=name, grid=(d, nb // QB), in_specs=[cur, prev, cur, prev, cur, cur, cur, cur], out_specs=cur,
        out_shape=jax.ShapeDtypeStruct((L, d * ATTN_WIDTH), BF16), compiler_params=_params(2),
    )(q, k, k, v, v, do, lse, dd)


def _attn_bwd_kv(q, k, v, do, lse, dd, d, name):
    L = q.shape[0]
    nb = L // ATTN_BLOCK
    B = ATTN_BLOCK
    KB = _blocks_per_step(nb)
    n_steps = nb // KB

    def body(k_ref, v_ref, qc_ref, qn_ref, doc_ref, don_ref, lsec_ref, lsen_ref, ddc_ref, ddn_ref, dk_ref, dv_ref):
        j = pl.program_id(1)
        qrow = lax.broadcasted_iota(jnp.int32, (2 * B, B), 0)
        kk = lax.broadcasted_iota(jnp.int32, (2 * B, B), 1)
        steps = qrow - kk
        band = (steps >= 0) & (steps <= B)
        lo2 = lax.broadcasted_iota(jnp.int32, (2 * B, 128), 1) < 64
        lo = lax.broadcasted_iota(jnp.int32, (B, 128), 1) < 64
        stepsf = steps.astype(F32)
        for sub in range(KB):
            rows = slice(sub * B, (sub + 1) * B)
            last = sub == KB - 1
            valid = band & ((qrow < B) | (j < n_steps - 1)) if last else band
            after = lambda cur_ref, nxt_ref, sl: nxt_ref[:, sl] if last else cur_ref[(sub + 1) * B:(sub + 2) * B, sl]
            for G in range(4):
                sl = slice(G * 128, (G + 1) * 128)
                kg = k_ref[rows, sl]
                vg = v_ref[rows, sl]
                qq = jnp.concatenate([qc_ref[rows, sl], after(qc_ref, qn_ref, sl)], axis=0)
                doo = jnp.concatenate([doc_ref[rows, sl], after(doc_ref, don_ref, sl)], axis=0)
                lse2 = jnp.concatenate([lsec_ref[rows, sl], after(lsec_ref, lsen_ref, sl)], axis=0)
                dd2 = jnp.concatenate([ddc_ref[rows, sl], after(ddc_ref, ddn_ref, sl)], axis=0)
                doo_b = doo.astype(BF16)
                dks, dvs = [], []
                for half in (0, 1):
                    msk = lo2 if half == 0 else jnp.logical_not(lo2)
                    qm = jnp.where(msk, qq, jnp.zeros_like(qq))
                    s = _nt(qm, kg) - (_slope(2 * G + half) * d) * stepsf
                    lse_c = _head_col(lse2, msk, True)
                    p = jnp.where(valid, jnp.exp(jnp.where(valid, s, NEG_BIG) - lse_c), 0.0)
                    dvs.append(_tn(p.astype(BF16), doo_b))
                    dom = jnp.where(msk, doo, 0.0).astype(BF16)
                    dp = _nt(dom, vg)
                    dcol = _head_col(dd2, msk, False)
                    ds = p * (dp - dcol)
                    dks.append(_tn(ds.astype(BF16), qq))
                dk_ref[rows, sl] = jnp.where(lo, dks[0], dks[1]).astype(BF16)
                dv_ref[rows, sl] = jnp.where(lo, dvs[0], dvs[1]).astype(BF16)

    cur = pl.BlockSpec((KB * B, ATTN_WIDTH), lambda r, j: (j, r))
    nxt = pl.BlockSpec((B, ATTN_WIDTH), lambda r, j: (jnp.minimum(KB * (j + 1), nb - 1), r))
    return pl.pallas_call(
        body, name=name, grid=(d, n_steps), in_specs=[cur, cur, cur, nxt, cur, nxt, cur, nxt, cur, nxt],
        out_specs=[cur, cur],
        out_shape=[jax.ShapeDtypeStruct((L, d * ATTN_WIDTH), BF16)] * 2, compiler_params=_params(2),
    )(k, v, q, q, do, do, lse, lse, dd, dd)


CONV_T = 512
HALO = 8


def _per_head(head, refs):
    for h in range(DN_HEADS):
        lanes = pl.ds(h * DN_HEAD_DIM, DN_HEAD_DIM)
        head(*[r.at[:, lanes] for r in refs[:-1]], refs[-1])


def _conv_taps(pad_ref, w, T):
    acc = pad_ref[pl.ds(HALO - 3, T), :] * w[0:1, :]
    for j in range(1, CONV_WIDTH):
        acc = acc + pad_ref[pl.ds(HALO - 3 + j, T), :] * w[j:j + 1, :]
    return acc


def _conv_fwd(xq, xk, xv, conv_w):
    S = xq.shape[0]
    T = CONV_T

    def body(*refs):
        _per_head(head, refs)

    def head(xq_ref, xqh_ref, xk_ref, xkh_ref, xv_ref, xvh_ref, wq_ref, wk_ref, wv_ref,
             qn_ref, kn_ref, v_ref, pad_ref):
        i = pl.program_id(0)

        def act(x_ref, xh_ref, w_ref):
            pad_ref[pl.ds(0, HALO), :] = jnp.where(i > 0, xh_ref[...], 0.0)
            pad_ref[pl.ds(HALO, T), :] = x_ref[...]
            c = _conv_taps(pad_ref, w_ref[...], T)
            return c * _sigmoid(c)

        def l2n(t):
            return t * lax.rsqrt(jnp.sum(t * t, axis=-1, keepdims=True) + L2_EPS)

        qn_ref[...] = l2n(act(xq_ref, xqh_ref, wq_ref))
        kn_ref[...] = l2n(act(xk_ref, xkh_ref, wk_ref))
        v_ref[...] = act(xv_ref, xvh_ref, wv_ref)

    tile = pl.BlockSpec((T, DN_WIDTH), lambda i: (i, 0))
    halo = pl.BlockSpec((HALO, DN_WIDTH), lambda i: (jnp.maximum(i * (T // HALO) - 1, 0), 0))
    wspec = lambda sec: pl.BlockSpec((CONV_WIDTH, DN_WIDTH), lambda i, sec=sec: (0, sec))
    return pl.pallas_call(
        body, name="dn_conv_fwd", grid=(S // T,),
        in_specs=[tile, halo, tile, halo, tile, halo, wspec(0), wspec(1), wspec(2)],
        out_specs=[tile, tile, tile],
        out_shape=[jax.ShapeDtypeStruct((S, DN_WIDTH), F32)] * 3,
        scratch_shapes=[pltpu.VMEM((T + HALO, 128), F32)],
        compiler_params=_params(1),
    )(xq, xq, xk, xk, xv, xv, conv_w, conv_w, conv_w)


def _conv_bwd_pre(xq, xk, xv, conv_w, dqn, dkn, dv):
    S = xq.shape[0]
    T = CONV_T

    def body(*refs):
        _per_head(head, refs)

    def head(xq_ref, xqh_ref, xk_ref, xkh_ref, xv_ref, xvh_ref, wq_ref, wk_ref, wv_ref,
             dqn_ref, dkn_ref, dv_ref, dcq_ref, dck_ref, dcv_ref, dwq_ref, dwk_ref, dwv_ref, pad_ref):
        i = pl.program_id(0)

        def one(x_ref, xh_ref, w_ref, dy_ref, dc_ref, dw_ref, normed):
            pad_ref[pl.ds(0, HALO), :] = jnp.where(i > 0, xh_ref[...], 0.0)
            pad_ref[pl.ds(HALO, T), :] = x_ref[...]
            c = _conv_taps(pad_ref, w_ref[...], T)
            sg = _sigmoid(c)
            a = c * sg
            dy = dy_ref[...]
            if normed:
                r = lax.rsqrt(jnp.sum(a * a, axis=-1, keepdims=True) + L2_EPS)
                y = a * r
                da = r * (dy - y * jnp.sum(dy * y, axis=-1, keepdims=True))
            else:
                da = dy
            dc = da * (sg * (1.0 + c * (1.0 - sg)))
            dc_ref[...] = dc

            @pl.when(i == 0)
            def _():
                dw_ref[...] = jnp.zeros_like(dw_ref)

            rows = [jnp.sum(dc * pad_ref[pl.ds(HALO - 3 + j, T), :], axis=0, keepdims=True) for j in range(CONV_WIDTH)]
            dw_ref[...] += jnp.concatenate(rows + [jnp.zeros((8 - CONV_WIDTH, 128), F32)], axis=0)

        one(xq_ref, xqh_ref, wq_ref, dqn_ref, dcq_ref, dwq_ref, True)
        one(xk_ref, xkh_ref, wk_ref, dkn_ref, dck_ref, dwk_ref, True)
        one(xv_ref, xvh_ref, wv_ref, dv_ref, dcv_ref, dwv_ref, False)

    tile = pl.BlockSpec((T, DN_WIDTH), lambda i: (i, 0))
    halo = pl.BlockSpec((HALO, DN_WIDTH), lambda i: (jnp.maximum(i * (T // HALO) - 1, 0), 0))
    wspec = lambda sec: pl.BlockSpec((CONV_WIDTH, DN_WIDTH), lambda i, sec=sec: (0, sec))
    dwspec = pl.BlockSpec((8, DN_WIDTH), lambda i: (0, 0))
    return pl.pallas_call(
        body, name="dn_conv_bwd_pre", grid=(S // T,),
        in_specs=[tile, halo, tile, halo, tile, halo, wspec(0), wspec(1), wspec(2), tile, tile, tile],
        out_specs=[tile, tile, tile, dwspec, dwspec, dwspec],
        out_shape=[jax.ShapeDtypeStruct((S, DN_WIDTH), F32)] * 3 + [jax.ShapeDtypeStruct((8, DN_WIDTH), F32)] * 3,
        scratch_shapes=[pltpu.VMEM((T + HALO, 128), F32)],
        compiler_params=_params(1),
    )(xq, xq, xk, xk, xv, xv, conv_w, conv_w, conv_w, dqn, dkn, dv)


def _conv_bwd_x(dcq, dck, dcv, conv_w):
    S = dcq.shape[0]
    T = CONV_T
    nt = S // T

    def body(*refs):
        _per_head(head, refs)

    def head(dq_ref, dqh_ref, dk_ref, dkh_ref, dv_ref, dvh_ref, wq_ref, wk_ref, wv_ref,
             oq_ref, ok_ref, ov_ref, pad_ref):
        i = pl.program_id(0)

        def one(d_ref, dh_ref, w_ref, o_ref):
            pad_ref[pl.ds(0, T), :] = d_ref[...]
            pad_ref[pl.ds(T, HALO), :] = jnp.where(i < nt - 1, dh_ref[...], 0.0)
            w = w_ref[...]
            acc = pad_ref[pl.ds(3, T), :] * w[0:1, :]
            for j in range(1, CONV_WIDTH):
                acc = acc + pad_ref[pl.ds(3 - j, T), :] * w[j:j + 1, :]
            o_ref[...] = acc

        one(dq_ref, dqh_ref, wq_ref, oq_ref)
        one(dk_ref, dkh_ref, wk_ref, ok_ref)
        one(dv_ref, dvh_ref, wv_ref, ov_ref)

    tile = pl.BlockSpec((T, DN_WIDTH), lambda i: (i, 0))
    halo = pl.BlockSpec((HALO, DN_WIDTH), lambda i: (jnp.minimum((i + 1) * (T // HALO), S // HALO - 1), 0))
    wspec = lambda sec: pl.BlockSpec((CONV_WIDTH, DN_WIDTH), lambda i, sec=sec: (0, sec))
    return pl.pallas_call(
        body, name="dn_conv_bwd_x", grid=(nt,),
        in_specs=[tile, halo, tile, halo, tile, halo, wspec(0), wspec(1), wspec(2)],
        out_specs=[tile, tile, tile],
        out_shape=[jax.ShapeDtypeStruct((S, DN_WIDTH), F32)] * 3,
        scratch_shapes=[pltpu.VMEM((T + HALO, 128), F32)],
        compiler_params=_params(1),
    )(dcq, dcq, dck, dck, dcv, dcv, conv_w, conv_w, conv_w)


PREP_CHUNKS = 4
SCAN_CHUNKS = 8


def _bnn(a, b):
    return lax.dot_general(a, b, (((2,), (1,)), ((0,), (0,))), preferred_element_type=F32, precision=HI)


def _bnt(a, b):
    return lax.dot_general(a, b, (((2,), (2,)), ((0,), (0,))), preferred_element_type=F32, precision=HI)


def _btn(a, b):
    return lax.dot_general(a, b, (((1,), (1,)), ((0,), (0,))), preferred_element_type=F32, precision=HI)


def _tri_inverse_b(a, blk, eye):
    dg = jnp.where(blk, a, 0.0)
    lo = a - dg
    d2 = _bnn(dg, dg)
    d4 = _bnn(d2, d2)
    d8 = _bnn(d4, d4)
    td = _bnn(_bnn(_bnn(eye - dg, eye + d2), eye + d4), eye + d8)
    b = _bnn(td, lo)
    b2 = _bnn(b, b)
    return _bnn(_bnn(eye - b, eye + b2), td)


def _dn_common_b(bds, avec, dvec, q_raw, k, v, t=None):
    C = DN_CHUNK
    lane = lax.broadcasted_iota(jnp.int32, (C, 128), 1)
    row = lax.broadcasted_iota(jnp.int32, (1, C, C), 1)
    col = lax.broadcasted_iota(jnp.int32, (1, C, C), 2)
    incl = row >= col
    strict = row > col
    eye = (row == col).astype(F32)
    blk = (row // 16) == (col // 16)
    pick = lambda tile, ln: jnp.sum(jnp.where(lane == ln, tile, 0.0), axis=-1, keepdims=True)
    betas, graws, zcs = [], [], []
    for bd in bds:
        z = bd + dvec
        g_all = -jnp.exp(avec) * (jnp.maximum(z, 0.0) + jnp.log(1.0 + jnp.exp(-jnp.abs(z))))
        beta_all = _sigmoid(bd)
        for h in range(DN_HEADS):
            betas.append(pick(beta_all, h))
            graws.append(pick(g_all, DN_HEADS + h))
            zcs.append(pick(z, DN_HEADS + h))
    beta, graw, zc = jnp.stack(betas), jnp.stack(graws), jnp.stack(zcs)
    to_row = lambda c: jnp.sum(eye * c, axis=1, keepdims=True)
    gc = jnp.sum(jnp.where(incl, to_row(graw), 0.0), axis=-1, keepdims=True)
    decay = jnp.exp(jnp.where(incl, gc - to_row(gc), NEG_BIG))
    q = q_raw * (DN_HEAD_DIM ** -0.5)
    kb = k * beta
    kk = _bnt(kb, k)
    if t is None:
        t = _tri_inverse_b(jnp.where(strict, kk * decay, 0.0), blk, eye)
    eg = jnp.exp(gc)
    rhs_w = kb * eg
    u = _bnn(t, v * beta)
    w = _bnn(t, rhs_w)
    qk = _bnt(q, k)
    aq = jnp.where(incl, qk * decay, 0.0)
    last = lax.broadcasted_iota(jnp.int32, (1, C, 1), 1) == C - 1
    g_last = jnp.sum(jnp.where(last, gc, 0.0), axis=1, keepdims=True)
    ekd = jnp.exp(g_last - gc)
    return dict(beta=beta, graw=graw, zc=zc, gc=gc, decay=decay, q=q, kb=kb, kk=kk, t=t, eg=eg, rhs_w=rhs_w,
                u=u, w=w, qk=qk, aq=aq, g_last=g_last, ekd=ekd, kd=k * ekd, qg=q * eg,
                incl=incl, strict=strict, eye=eye, lane=lane, row=row, col=col, last=last)


def _stack_heads(ref, rows):
    return jnp.stack([ref[rows, h * DN_HEAD_DIM:(h + 1) * DN_HEAD_DIM] for h in range(DN_HEADS)])


def _stack_units(ref, nc):
    C = DN_CHUNK
    return jnp.concatenate([_stack_heads(ref, slice(ci * C, (ci + 1) * C)) for ci in range(nc)], axis=0)


def _store_units(ref, val, nc):
    C = DN_CHUNK
    for ci in range(nc):
        for h in range(DN_HEADS):
            ref[ci * C:(ci + 1) * C, h * DN_HEAD_DIM:(h + 1) * DN_HEAD_DIM] = val[ci * DN_HEADS + h]


def _dn_prep(qn, kn, v, bd, avec, dvec):
    S = qn.shape[0]
    C = DN_CHUNK
    N = S // C
    nc = PREP_CHUNKS

    def body(q_ref, k_ref, v_ref, bd_ref, a_ref, d_ref, u_ref, w_ref, qg_ref, kd_ref, aq_ref, t_ref, egl_ref):
        bds = [bd_ref[ci * C:(ci + 1) * C, :] for ci in range(nc)]
        c = _dn_common_b(bds, a_ref[...], d_ref[...], _stack_units(q_ref, nc), _stack_units(k_ref, nc), _stack_units(v_ref, nc))
        _store_units(u_ref, c["u"], nc)
        _store_units(w_ref, c["w"], nc)
        _store_units(qg_ref, c["qg"], nc)
        _store_units(kd_ref, c["kd"], nc)
        egl = jnp.broadcast_to(jnp.exp(c["g_last"]), (nc * DN_HEADS, 1, 128))
        for ci in range(nc):
            for h in range(DN_HEADS):
                aq_ref[h, ci * C:(ci + 1) * C, :] = c["aq"][ci * DN_HEADS + h]
                t_ref[h, ci * C:(ci + 1) * C, :] = c["t"][ci * DN_HEADS + h]
            egl_ref[ci * 8:(ci + 1) * 8, :] = jnp.concatenate(
                [egl[ci * DN_HEADS + h] for h in range(DN_HEADS)] + [jnp.zeros((8 - DN_HEADS, 128), F32)], axis=0)

    tok = lambda w: pl.BlockSpec((nc * C, w), lambda n: (n, 0))
    sq = pl.BlockSpec((DN_HEADS, nc * C, C), lambda n: (0, n, 0))
    vec = pl.BlockSpec((1, 128), lambda n: (0, 0))
    return pl.pallas_call(
        body, name="dn_prep", grid=(N // nc,),
        in_specs=[tok(DN_WIDTH)] * 3 + [tok(128), vec, vec],
        out_specs=[tok(DN_WIDTH)] * 4 + [sq, sq, pl.BlockSpec((nc * 8, 128), lambda n: (n, 0))],
        out_shape=[jax.ShapeDtypeStruct((S, DN_WIDTH), F32)] * 4 + [jax.ShapeDtypeStruct((DN_HEADS, S, C), F32)] * 2
                  + [jax.ShapeDtypeStruct((N * 8, 128), F32)],
        compiler_params=_params(1),
    )(qn, kn, v, bd, avec, dvec)


def _dn_scan_fwd(u, w, qg, kd, aq, egl, gate, dn_gain):
    S = u.shape[0]
    C = DN_CHUNK
    N = S // C
    HD = DN_HEAD_DIM
    nc = SCAN_CHUNKS

    def body(u_ref, w_ref, qg_ref, kd_ref, aq_ref, egl_ref, gate_ref, gain_ref, dn_ref, o_ref, vn_ref, st_ref, state_ref):
        @pl.when(pl.program_id(0) == 0)
        def _():
            state_ref[...] = jnp.zeros_like(state_ref)

        gain = gain_ref[...]
        for ci in range(nc):
            rows = slice(ci * C, (ci + 1) * C)
            st = state_ref[...]
            for h in range(DN_HEADS):
                st_ref[ci * DN_WIDTH + h * HD:ci * DN_WIDTH + (h + 1) * HD, :] = st[h]
            v_new = _stack_heads(u_ref, rows) - _bnn(_stack_heads(w_ref, rows), st)
            o = _bnn(_stack_heads(qg_ref, rows), st) + _bnn(aq_ref[:, rows, :], v_new)
            egl = jnp.stack([egl_ref[ci * 8 + h:ci * 8 + h + 1, :] for h in range(DN_HEADS)])
            state_ref[...] = st * egl + _btn(_stack_heads(kd_ref, rows), v_new)
            r = lax.rsqrt(jnp.mean(o * o, axis=-1, keepdims=True) + NORM_EPS)
            gt = _stack_heads(gate_ref, rows)
            dn = o * r * gain * (gt * _sigmoid(gt))
            for h in range(DN_HEADS):
                sl = slice(h * HD, (h + 1) * HD)
                vn_ref[rows, sl] = v_new[h]
                o_ref[rows, sl] = o[h]
                dn_ref[rows, sl] = dn[h]

    tok = lambda wd: pl.BlockSpec((nc * C, wd), lambda n: (n, 0))
    sq = pl.BlockSpec((DN_HEADS, nc * C, C), lambda n: (0, n, 0))
    vec = pl.BlockSpec((1, 128), lambda n: (0, 0))
    return pl.pallas_call(
        body, name="dn_scan_fwd", grid=(N // nc,),
        in_specs=[tok(DN_WIDTH)] * 4 + [sq, pl.BlockSpec((nc * 8, 128), lambda n: (n, 0)), tok(DN_WIDTH), vec],
        out_specs=[tok(DN_WIDTH)] * 3 + [pl.BlockSpec((nc * DN_WIDTH, HD), lambda n: (n, 0))],
        out_shape=[jax.ShapeDtypeStruct((S, DN_WIDTH), F32)] * 3 + [jax.ShapeDtypeStruct((N * DN_WIDTH, HD), F32)],
        scratch_shapes=[pltpu.VMEM((DN_HEADS, HD, HD), F32)],
        compiler_params=_params(1),
    )(u, w, qg, kd, aq, egl, gate, dn_gain)


def _dn_scan_bwd(w, qg, kd, aq, egl, gate, dn_gain, o, ddn):
    S = w.shape[0]
    C = DN_CHUNK
    N = S // C
    HD = DN_HEAD_DIM
    nc = SCAN_CHUNKS

    def body(w_ref, qg_ref, kd_ref, aq_ref, egl_ref, gate_ref, gain_ref, o_ref, ddn_ref,
             do_ref, dvn_ref, dgate_ref, dst_ref, small_ref, dstate_ref):
        @pl.when(pl.program_id(0) == 0)
        def _():
            dstate_ref[...] = jnp.zeros_like(dstate_ref)
            small_ref[...] = jnp.zeros_like(small_ref)

        gain = gain_ref[...]
        d_gain = jnp.zeros((1, 128), F32)
        for ci in reversed(range(nc)):
            rows = slice(ci * C, (ci + 1) * C)
            dsn = dstate_ref[...]
            for h in range(DN_HEADS):
                dst_ref[ci * DN_WIDTH + h * HD:ci * DN_WIDTH + (h + 1) * HD, :] = dsn[h]
            ov = _stack_heads(o_ref, rows)
            r = lax.rsqrt(jnp.mean(ov * ov, axis=-1, keepdims=True) + NORM_EPS)
            on = ov * r
            gt = _stack_heads(gate_ref, rows)
            sgt = _sigmoid(gt)
            silu_g = gt * sgt
            dy = _stack_heads(ddn_ref, rows)
            d_gain = d_gain + jnp.sum(jnp.sum(dy * on * silu_g, axis=1, keepdims=True), axis=0)
            dgate = dy * on * gain * (sgt * (1.0 + gt * (1.0 - sgt)))
            don = dy * gain * silu_g
            do = r * (don - on * jnp.mean(don * on, axis=-1, keepdims=True))
            d_vnew = _btn(aq_ref[:, rows, :], do) + _bnn(_stack_heads(kd_ref, rows), dsn)
            egl = jnp.stack([egl_ref[ci * 8 + h:ci * 8 + h + 1, :] for h in range(DN_HEADS)])
            dstate_ref[...] = _btn(_stack_heads(qg_ref, rows), do) + dsn * egl - _btn(_stack_heads(w_ref, rows), d_vnew)
            for h in range(DN_HEADS):
                sl = slice(h * HD, (h + 1) * HD)
                do_ref[rows, sl] = do[h]
                dvn_ref[rows, sl] = d_vnew[h]
                dgate_ref[rows, sl] = dgate[h]
        small_ref[...] += jnp.concatenate([d_gain, jnp.zeros((7, 128), F32)], axis=0)

    nb = N // nc
    tok = lambda wd: pl.BlockSpec((nc * C, wd), lambda i: (nb - 1 - i, 0))
    sq = pl.BlockSpec((DN_HEADS, nc * C, C), lambda i: (0, nb - 1 - i, 0))
    vec = pl.BlockSpec((1, 128), lambda i: (0, 0))
    return pl.pallas_call(
        body, name="dn_scan_bwd", grid=(nb,),
        in_specs=[tok(DN_WIDTH)] * 3 + [sq, pl.BlockSpec((nc * 8, 128), lambda i: (nb - 1 - i, 0)), tok(DN_WIDTH), vec,
                                       tok(DN_WIDTH), tok(DN_WIDTH)],
        out_specs=[tok(DN_WIDTH)] * 3 + [pl.BlockSpec((nc * DN_WIDTH, HD), lambda i: (nb - 1 - i, 0)),
                                        pl.BlockSpec((8, 128), lambda i: (0, 0))],
        out_shape=[jax.ShapeDtypeStruct((S, DN_WIDTH), F32)] * 3 + [jax.ShapeDtypeStruct((N * DN_WIDTH, HD), F32),
                                                                  jax.ShapeDtypeStruct((8, 128), F32)],
        scratch_shapes=[pltpu.VMEM((DN_HEADS, HD, HD), F32)],
        compiler_params=_params(1),
    )(w, qg, kd, aq, egl, gate, dn_gain, o, ddn)


def _dn_post(qn, kn, v, bd, avec, dvec, t_inv, v_new_all, states, dstates, do_all, dvn_all, comm=None):
    S = qn.shape[0]
    C = DN_CHUNK
    N = S // C
    HD = DN_HEAD_DIM
    nc = PREP_CHUNKS
    B = nc * DN_HEADS

    def body(q_ref, k_ref, v_ref, bd_ref, a_ref, d_ref, t_ref, vn_ref, st_ref, dst_ref, do_ref, dvn_ref,
             dq_ref, dk_ref, dv_ref, dbd_ref, small_ref):
        @pl.when(pl.program_id(0) == 0)
        def _():
            small_ref[...] = jnp.zeros_like(small_ref)

        avec = a_ref[...]
        bds = [bd_ref[ci * C:(ci + 1) * C, :] for ci in range(nc)]
        k = _stack_units(k_ref, nc)
        vv = _stack_units(v_ref, nc)
        t = jnp.concatenate([t_ref[:, ci * C:(ci + 1) * C, :] for ci in range(nc)], axis=0)
        c = _dn_common_b(bds, avec, d_ref[...], _stack_units(q_ref, nc), k, vv, t=t)
        q, kb, eg, u, w = c["q"], c["kb"], c["eg"], c["u"], c["w"]
        beta, decay, incl, strict, eye = c["beta"], c["decay"], c["incl"], c["strict"], c["eye"]
        st = jnp.stack([st_ref[b * HD:(b + 1) * HD, :] for b in range(B)])
        dsn = jnp.stack([dst_ref[b * HD:(b + 1) * HD, :] for b in range(B)])
        v_new = _stack_units(vn_ref, nc)
        do = _stack_units(do_ref, nc)
        d_vnew = _stack_units(dvn_ref, nc)
        egl = jnp.exp(c["g_last"])
        daq = jnp.where(incl, _bnt(do, v_new), 0.0)
        d_qg = _bnt(do, st)
        d_kd = _bnt(v_new, dsn)
        d_glast = jnp.sum(jnp.sum(dsn * st, axis=-1, keepdims=True), axis=1, keepdims=True) * egl
        d_w = -_bnt(d_vnew, st)
        d_ru = _btn(t, d_vnew)
        d_rw = _btn(t, d_w)
        da = -jnp.where(strict, _bnt(d_ru, u) + _bnt(d_rw, w), 0.0)
        dv = d_ru * beta
        dbeta = jnp.sum(d_ru * vv, axis=-1, keepdims=True)
        dkb = d_rw * eg
        dgc = jnp.sum(d_rw * c["rhs_w"], axis=-1, keepdims=True)
        dkk = da * decay
        ddecay = da * c["kk"]
        dkb = dkb + _bnn(dkk, k)
        dk = _btn(dkk, kb)
        dqk = daq * decay
        ddecay = ddecay + daq * c["qk"]
        dq = _bnn(dqk, k)
        dk = dk + _btn(dqk, q)
        m = ddecay * decay
        col_sum = jnp.sum(m, axis=1, keepdims=True)
        dgc = dgc + jnp.sum(m, axis=-1, keepdims=True) - jnp.sum(eye * col_sum, axis=-1, keepdims=True)
        dq = dq + d_qg * eg
        dgc = dgc + jnp.sum(d_qg * c["qg"], axis=-1, keepdims=True)
        dk = dk + d_kd * c["ekd"]
        tk = jnp.sum(d_kd * c["kd"], axis=-1, keepdims=True)
        dgc = dgc - tk
        d_glast = d_glast + jnp.sum(tk, axis=1, keepdims=True)
        dk = dk + dkb * beta
        dbeta = dbeta + jnp.sum(dkb * k, axis=-1, keepdims=True)
        dgc = dgc + jnp.where(c["last"], d_glast, 0.0)
        dgc_row = jnp.sum(eye * dgc, axis=1, keepdims=True)
        dgraw = jnp.sum(jnp.where(c["col"] >= c["row"], dgc_row, 0.0), axis=-1, keepdims=True)
        _store_units(dq_ref, dq * (HD ** -0.5), nc)
        _store_units(dk_ref, dk, nc)
        _store_units(dv_ref, dv, nc)
        dbraw = dbeta * beta * (1.0 - beta)
        dzc = dgraw * _sigmoid(c["zc"])
        ga = dgraw * c["graw"]
        lane = c["lane"]
        lane1 = lax.broadcasted_iota(jnp.int32, (1, 128), 1)
        neg_ea = -jnp.exp(avec)
        d_alog = jnp.zeros((1, 128), F32)
        d_dt = jnp.zeros((1, 128), F32)
        for ci in range(nc):
            dbd = jnp.zeros((C, 128), F32)
            for h in range(DN_HEADS):
                b = ci * DN_HEADS + h
                dz = dzc[b] * neg_ea
                dbd = dbd + jnp.where(lane == h, dbraw[b], 0.0) + jnp.where(lane == DN_HEADS + h, dz, 0.0)
                d_alog = d_alog + jnp.where(lane1 == DN_HEADS + h, jnp.sum(ga[b], axis=0, keepdims=True), 0.0)
                d_dt = d_dt + jnp.where(lane1 == DN_HEADS + h, jnp.sum(dz, axis=0, keepdims=True), 0.0)
            dbd_ref[ci * C:(ci + 1) * C, :] = dbd
        small_ref[...] += jnp.concatenate([d_alog, d_dt, jnp.zeros((6, 128), F32)], axis=0)

    tok = lambda wd: pl.BlockSpec((nc * C, wd), lambda n: (n, 0))
    big = pl.BlockSpec((nc * DN_WIDTH, HD), lambda n: (n, 0))
    sq = pl.BlockSpec((DN_HEADS, nc * C, C), lambda n: (0, n, 0))
    vec = pl.BlockSpec((1, 128), lambda n: (0, 0))
    return _call(
        body, (qn, kn, v, bd, avec, dvec, t_inv, v_new_all, states, dstates, do_all, dvn_all),
        name="dn_post", grid=(N // nc,), comm=comm,
        in_specs=[tok(DN_WIDTH)] * 3 + [tok(128), vec, vec, sq, tok(DN_WIDTH), big, big, tok(DN_WIDTH), tok(DN_WIDTH)],
        out_specs=[tok(DN_WIDTH)] * 3 + [tok(128), pl.BlockSpec((8, 128), lambda n: (0, 0))],
        out_shape=[jax.ShapeDtypeStruct((S, DN_WIDTH), F32)] * 3 + [jax.ShapeDtypeStruct((S, 128), F32),
                                                                  jax.ShapeDtypeStruct((8, 128), F32)])


def _outproj_fwd(x, attn, dn, w_out):
    S, D = x.shape
    tm = 512

    def body(x_ref, a_ref, d_ref, w_ref, xo_ref, mix_ref):
        a = a_ref[...].astype(BF16)
        dd = d_ref[...].astype(BF16)
        mix_ref[:, 0:ATTN_WIDTH] = a
        mix_ref[:, ATTN_WIDTH:] = dd
        xo_ref[...] = x_ref[...] + _nn(a, w_ref[0:ATTN_WIDTH, :]) + _nn(dd, w_ref[ATTN_WIDTH:, :])

    tok = lambda w: pl.BlockSpec((tm, w), lambda i: (i, 0))
    return pl.pallas_call(
        body, name="outproj_fwd", grid=(S // tm,),
        in_specs=[tok(D), tok(ATTN_WIDTH), tok(DN_WIDTH), pl.BlockSpec((D, D), lambda i: (0, 0))],
        out_specs=[tok(D), tok(D)],
        out_shape=[jax.ShapeDtypeStruct((S, D), F32), jax.ShapeDtypeStruct((S, D), BF16)],
        compiler_params=_params(1),
    )(x, attn, dn, w_out)


def _outproj_bwd(dx, w_out, attn):
    S, D = dx.shape
    tm = VIEW_TILE

    def body(dx_ref, w_ref, attn_ref, da1, da4, da16, dl1, dl4, dl16, ddn_ref, dxb_ref, planes):
        d = dx_ref[...].astype(BF16)
        dxb_ref[...] = d
        da = _nt(d, w_ref[0:ATTN_WIDTH, :])
        ddn_ref[...] = _nt(d, w_ref[ATTN_WIDTH:, :])
        _tile_to_views(da, planes, (da1, da4, da16))
        lo = lax.broadcasted_iota(jnp.int32, (tm, 128), 1) < 64
        cols = []
        for G in range(4):
            sl = slice(G * 128, (G + 1) * 128)
            t = da[:, sl] * attn_ref[:, sl]
            d0 = jnp.sum(jnp.where(lo, t, 0.0), axis=-1, keepdims=True)
            d1 = jnp.sum(jnp.where(lo, 0.0, t), axis=-1, keepdims=True)
            cols.append(jnp.where(lo, d0, d1))
        _tile_to_views(jnp.concatenate(cols, axis=1), planes, (dl1, dl4, dl16))

    tok = lambda w: pl.BlockSpec((tm, w), lambda i: (i, 0))
    views = [_view_spec(d) for d in DILATIONS]
    return pl.pallas_call(
        body, name="outproj_bwd", grid=(S // tm,),
        in_specs=[tok(D), pl.BlockSpec((D, D), lambda i: (0, 0)), tok(ATTN_WIDTH)],
        out_specs=views + views + [tok(DN_WIDTH), tok(D)],
        out_shape=[_view_shape(S, d, F32) for d in DILATIONS] * 2
                  + [jax.ShapeDtypeStruct((S, DN_WIDTH), F32), jax.ShapeDtypeStruct((S, D), BF16)],
        scratch_shapes=[pltpu.VMEM((4, tm, 128), F32)],
        compiler_params=_params(1),
    )(dx, w_out, attn)


def _adamw(w, g, m, v, name):
    R, Ccols = w.shape[0], w.shape[-1]
    tr = next((t for t in range(512, 7, -8) if R % t == 0), R)
    c1 = 1.0 - ADAM_B1 ** ADAM_STEP
    c2 = 1.0 - ADAM_B2 ** ADAM_STEP

    def body(w_ref, g_ref, m_ref, v_ref, d_ref, nm_ref, nv_ref):
        gv = g_ref[...]
        mn = ADAM_B1 * m_ref[...] + (1.0 - ADAM_B1) * gv
        vn = ADAM_B2 * v_ref[...] + (1.0 - ADAM_B2) * (gv * gv)
        nm_ref[...] = mn
        nv_ref[...] = vn
        d_ref[...] = -ADAM_LR * ((mn / c1) / (jnp.sqrt(vn / c2) + ADAM_EPS) + ADAM_WD * w_ref[...])

    if w.ndim == 2:
        grid, spec = (R // tr,), pl.BlockSpec((tr, Ccols), lambda i: (i, 0))
    else:
        grid, spec = (2,), pl.BlockSpec((R // 2, 1, Ccols), lambda i: (i, 0, 0))
    return pl.pallas_call(
        body, name=name, grid=grid, in_specs=[spec] * 4, out_specs=[spec] * 3,
        out_shape=[jax.ShapeDtypeStruct(w.shape, F32)] * 3, compiler_params=_params(1),
    )(w, g, m, v)


LATE_WEIGHTS = ("w_out", "ffn2_gate", "ffn2_up", "ffn2_down")


def _local_step(x, target, wts, small, dist=None):
    g1, g2, gm, gf = small["norm_ffn1"], small["norm_ffn2"], small["norm_mix"], small["norm_final"]
    wts = dict(wts)

    def reduce_start(gs, tag):
        return _rs_add_pairs(gs, _swap_sibling(gs, True, "rs_swap_halves_" + tag), dist["c"], "rs_add_pairs_" + tag)

    (x1, h1, fg1, fu1), late = _ffn_fwd(x, g1, wts["ffn1_gate"], wts["ffn1_up"], wts["ffn1_down"], "ffn1_fwd",
                                        comm=_ag_comm(dist["late"]) if dist else None)
    if dist:
        wts.update(zip(LATE_WEIGHTS, late))
        wts["w_out"] = wts["w_out"].reshape(D_MODEL, D_MODEL)
    h2, *qkv, xq, xk, xv, gate, bd = _inproj_fwd(x1, gm, wts["w_in"])
    aq, ak, av = qkv[0:3], qkv[3:6], qkv[6:9]
    parts = [_attn_fwd(aq[p], ak[p], av[p], d, f"attn_fwd_d{d}") for p, d in enumerate(DILATIONS)]
    attn, *lse = _attn_merge(parts)
    conv_w = small["conv_w"]
    qn, kn, vv = _conv_fwd(xq, xk, xv, conv_w)
    dn_u, dn_w, dn_qg, dn_kd, dn_aq, dn_t, dn_egl = _dn_prep(qn, kn, vv, bd, small["avec"], small["dvec"])
    dn, o_dn, v_new, states = _dn_scan_fwd(dn_u, dn_w, dn_qg, dn_kd, dn_aq, dn_egl, gate, small["dn_norm"])
    x2, mix = _outproj_fwd(x1, attn, dn, wts["w_out"])
    (dx3, h3, fg2, fu2, loss, d_gf), _ = _ffn_fwd(x2, g2, wts["ffn2_gate"], wts["ffn2_up"], wts["ffn2_down"], "ffn2_fwd",
                                                 head=(gf, target))

    grads = {}
    (dx2, d_g2, dfg2, dfu2, act2, dout2), _ = _ffn_bwd(dx3, x2, g2, fg2, fu2, wts["ffn2_down"], wts["ffn2_gate"],
                                                      wts["ffn2_up"], "ffn2_bwd")
    tk = 2048
    grads["ffn2_gate"], _ = _dw_chunks(dfg2, h3, tk, "dw_ffn2_gate")
    grads["ffn2_up"], _ = _dw_chunks(dfu2, h3, tk, "dw_ffn2_up")
    grads["ffn2_down"], _ = _dw_chunks(act2, dout2, tk, "dw_ffn2_down")
    group_a = ("ffn2_gate", "ffn2_up", "ffn2_down")
    parts_a = reduce_start([grads[n] for n in group_a], "a") if dist else None

    *dviews, ddn, dx2b = _outproj_bwd(dx2, wts["w_out"], attn)
    dattn, dd = dviews[0:3], dviews[3:6]
    grads["w_out"] = _matmul_tn(mix, dx2b, D_MODEL, tk, "dw_out").reshape(N_CHIPS, D_MODEL // N_CHIPS, D_MODEL)

    daq, dak, dav = [], [], []
    for p, d in enumerate(DILATIONS):
        daq.append(_attn_bwd_q(aq[p], ak[p], av[p], dattn[p], lse[p], dd[p], d, f"attn_bwd_q_d{d}"))
        dk_p, dv_p = _attn_bwd_kv(aq[p], ak[p], av[p], dattn[p], lse[p], dd[p], d, f"attn_bwd_kv_d{d}")
        dak.append(dk_p)
        dav.append(dv_p)

    do_dn, dvn, dgate, dstates, d_dn_gain = _dn_scan_bwd(dn_w, dn_qg, dn_kd, dn_aq, dn_egl, gate, small["dn_norm"], o_dn, ddn)
    (dqn, dkn, dvv, dbd, dn_small), recv_a = _dn_post(qn, kn, vv, bd, small["avec"], small["dvec"], dn_t, v_new, states,
                                                      dstates, do_dn, dvn, comm=_rsx_comm(parts_a) if dist else None)
    dcq, dck, dcv, dwq, dwk, dwv = _conv_bwd_pre(xq, xk, xv, conv_w, dqn, dkn, dvv)
    dxq, dxk, dxv = _conv_bwd_x(dcq, dck, dcv, conv_w)
    d_conv = jnp.concatenate([dwq[:CONV_WIDTH], dwk[:CONV_WIDTH], dwv[:CONV_WIDTH]], axis=1)

    dx1, d_gm, dproj = _inproj_bwd(dx2, x1, gm, [daq, dak, dav], [dxq, dxk, dxv, dgate], dbd, wts["w_in"])
    gi = _matmul_tn(dproj, h2, IN_COLS_PADDED, 512, "dw_in")
    if dist:
        gate_end = QKV_COLS + DN_WIDTH
        gi = jnp.concatenate([gi[:QKV_COLS], gi[gate_end:gate_end + LOGIT_COLS], gi[QKV_COLS:gate_end]], axis=0)
        gi = gi.reshape(N_CHIPS, IN_COLS // N_CHIPS, D_MODEL)
        gi = jnp.pad(gi, ((0, 0), (0, W_IN_ROWS - IN_COLS // N_CHIPS), (0, 0)))
    grads["w_in"] = gi
    group_b = ("w_in", "w_out")
    parts_b = reduce_start([grads[n] for n in group_b], "b") if dist else None

    (dx0, d_g1, dfg1, dfu1, act1, dout1), recv_b = _ffn_bwd(dx1, x, g1, fg1, fu1, wts["ffn1_down"], wts["ffn1_gate"],
                                                           wts["ffn1_up"], "ffn1_bwd",
                                                           comm=_rsx_comm(parts_b) if dist else None)
    group_c = ("ffn1_gate", "ffn1_up", "ffn1_down")
    pending, parts_c, recv_c = [], [], []
    for n, (lhs, rhs) in zip(group_c, ((dfg1, h1), (dfu1, h1), (act1, dout1))):
        grads[n], landed = _dw_chunks(lhs, rhs, tk, "dw_" + n, comm=_rsx_comm(pending) if pending else None)
        recv_c += list(landed)
        if dist:
            pending = reduce_start([grads[n]], n)
            parts_c += pending

    small_grads = dict(norm_ffn1=d_g1, norm_mix=d_gm, norm_ffn2=d_g2, norm_final=d_gf, conv_w=d_conv,
                       a_log=dn_small[0:1], dt_bias=dn_small[1:2], dn_norm=d_dn_gain[0:1])
    if dist:
        recv_c += _rs_exchange_arrays(pending)
        names = group_a + group_b + group_c
        totals = _rs_add_totals(list(parts_a) + list(parts_b) + list(parts_c), list(recv_a) + list(recv_b) + list(recv_c),
                                dist["chip"])
        theirs = _swap_sibling(totals, False, "rs_share_total")
        grads = {n: (mine, other) for n, mine, other in zip(names, totals, theirs)}
    return loss, dx0, grads, small_grads


HBM =pl.BlockSpec(memory_space=pl.ANY)
VMEM_SPEC = pl.BlockSpec(memory_space=pltpu.VMEM)


def _coords():
    return lax.axis_index("x"), lax.axis_index("y"), lax.axis_index("c")


def _remote(src, dst, send_sems, recv_sems, k, dev):
    return pltpu.make_async_remote_copy(src_ref=src, dst_ref=dst, send_sem=send_sems.at[k], recv_sem=recv_sems.at[k],
                                        device_id=dev, device_id_type=MESH)


def _allreduce_small(buf, name):
    R, Cc = buf.shape

    def body(src_ref, out_ref, recv_ref, send_sems, recv_sems):
        x, y, c = _coords()
        copies = []
        for m in range(1, 8):
            fx, fy, fc = (m >> 2) & 1, (m >> 1) & 1, m & 1
            dev = (x ^ fx if fx else x, y ^ fy if fy else y, c ^ fc if fc else c)
            cp = _remote(src_ref, recv_ref.at[m - 1], send_sems, recv_sems, m - 1, dev)
            cp.start()
            copies.append(cp)
        for cp in copies:
            cp.wait()
        r = [src_ref[...]] + [recv_ref[m] for m in range(7)]
        out_ref[...] = ((r[0] + r[1]) + (r[2] + r[3])) + ((r[4] + r[5]) + (r[6] + r[7]))

    return pl.pallas_call(
        body, name=name, out_shape=jax.ShapeDtypeStruct((R, Cc), F32),
        in_specs=[VMEM_SPEC], out_specs=VMEM_SPEC,
        scratch_shapes=[pltpu.VMEM((7, R, Cc), F32), pltpu.SemaphoreType.DMA((7,)), pltpu.SemaphoreType.DMA((7,))],
    )(buf)


BIG = ("ffn1_gate", "ffn1_up", "ffn1_down", "w_in", "w_out", "ffn2_gate", "ffn2_up", "ffn2_down")
ROW_SHARDED = ("ffn1_down", "w_out", "ffn2_down")
W_IN_ROWS = 960


def _rows(ref, start, size):
    return ref.at[pl.ds(pl.multiple_of(start, 16), size)]


def _allgather_arrays(shards):
    n = len(shards)

    def body(*refs):
        srcs, outs, send_sems, recv_sems = refs[:n], refs[n:2 * n], refs[2 * n], refs[2 * n + 1]
        x, y, c = _coords()
        sib = (x, y, 1 - c)
        xn, yn, dg = (1 - x, y), (x, 1 - y), (1 - x, 1 - y)
        slot = lambda out, chip: out.at[2 * chip[0] + chip[1]]
        started = []

        def go(cp):
            cp.start()
            started.append(cp)

        for a, (src, out) in enumerate(zip(srcs, outs)):
            h = src.shape[0] // 2
            cp = lambda s, d, k, dev: _remote(s, d, send_sems, recv_sems, 8 * a + k, dev)
            go(cp(src, slot(out, (x, y)), 6, sib))
            mine, dst = _rows(src, c * h, h), _rows(slot(out, (x, y)), c * h, h)
            go(cp(mine, dst, 0, (*xn, c)))
            go(cp(mine, dst, 1, (*yn, c)))
        for a, (src, out) in enumerate(zip(srcs, outs)):
            h = src.shape[0] // 2
            q = h // 2
            cp = lambda s, d, k, dev: _remote(s, d, send_sems, recv_sems, 8 * a + k, dev)
            from_x, from_y = _rows(slot(out, xn), c * h, h), _rows(slot(out, yn), c * h, h)
            cp(from_x, from_x, 0, sib).wait_recv()
            first = _rows(slot(out, xn), c * h, q)
            go(cp(first, first, 2, (*yn, c)))
            go(cp(from_x, from_x, 3, sib))
            cp(from_y, from_y, 1, sib).wait_recv()
            second = _rows(slot(out, yn), c * h + q, q)
            go(cp(second, second, 7, (*xn, c)))
            go(cp(from_y, from_y, 4, sib))
        for a, (src, out) in enumerate(zip(srcs, outs)):
            h = src.shape[0] // 2
            q = h // 2
            cp = lambda s, d, k, dev: _remote(s, d, send_sems, recv_sems, 8 * a + k, dev)
            first, second = _rows(slot(out, dg), c * h, q), _rows(slot(out, dg), c * h + q, q)
            cp(first, first, 2, sib).wait_recv()
            cp(second, second, 7, sib).wait_recv()
            from_d = _rows(slot(out, dg), c * h, h)
            go(cp(from_d, from_d, 5, sib))
        for a, (src, out) in enumerate(zip(srcs, outs)):
            h = src.shape[0] // 2
            cp = lambda s, d, k, dev: _remote(s, d, send_sems, recv_sems, 8 * a + k, dev)
            for k, chip in ((3, xn), (4, yn), (5, dg)):
                theirs = _rows(slot(out, chip), (1 - c) * h, h)
                cp(theirs, theirs, k, sib).wait_recv()
            cp(src, slot(out, (x, y)), 6, sib).wait_recv()
        for cp in started:
            cp.wait_send()

    shapes = [jax.ShapeDtypeStruct((N_CHIPS,) + s.shape, s.dtype) for s in shards]
    return pl.pallas_call(
        body, name="allgather_weights", out_shape=shapes, in_specs=[HBM] * n, out_specs=[HBM] * n,
        scratch_shapes=[pltpu.SemaphoreType.DMA((8 * n,)), pltpu.SemaphoreType.DMA((8 * n,))],
    )(*shards)


def _ag_copies(srcs, outs, send_sems, recv_sems):
    x, y, c = _coords()
    sib = (x, y, 1 - c)
    me = 2 * x + y
    others = [(1 - x, y), (x, 1 - y), (1 - x, 1 - y)]
    plan = []
    for a, (src, out) in enumerate(zip(srcs, outs)):
        h = src.shape[0] // 2
        cp = lambda s, d, k, dev: _remote(s, d, send_sems, recv_sems, 7 * a + k, dev)
        own = cp(src, out.at[me], 6, sib)
        sends = [cp(_rows(src, c * h, h), _rows(out.at[me], c * h, h), j, (ox, oy, c)) for j, (ox, oy) in enumerate(others)]
        mine = [_rows(out.at[2 * ox + oy], c * h, h) for ox, oy in others]
        theirs = [_rows(out.at[2 * ox + oy], (1 - c) * h, h) for ox, oy in others]
        arrivals = [cp(m, m, j, sib) for j, m in enumerate(mine)]
        forwards = [cp(m, m, 3 + j, sib) for j, m in enumerate(mine)]
        forwarded = [cp(t, t, 3 + j, sib) for j, t in enumerate(theirs)]
        plan.append((own, sends, forwards, arrivals, forwarded))
    return plan


def _ag_start(srcs, outs, send_sems, recv_sems):
    for own, sends, _, _, _ in _ag_copies(srcs, outs, send_sems, recv_sems):
        own.start()
        for cp in sends:
            cp.start()


def _ag_finish(srcs, outs, send_sems, recv_sems):
    plan = _ag_copies(srcs, outs, send_sems, recv_sems)
    for _, _, forwards, arrivals, _ in plan:
        for arrived, fwd in zip(arrivals, forwards):
            arrived.wait_recv()
            fwd.start()
    for own, sends, forwards, _, forwarded in plan:
        for cp in forwarded:
            cp.wait_recv()
        own.wait_recv()
        for cp in [own] + sends + forwards:
            cp.wait_send()


def _ag_comm(shards):
    shapes = [jax.ShapeDtypeStruct((N_CHIPS,) + s.shape, s.dtype) for s in shards]
    return (list(shards), shapes, 7 * len(shards), _ag_start, _ag_finish)


def _swap_sibling(arrs, pick_other_half, name):
    n = len(arrs)
    outs = [jax.ShapeDtypeStruct((a.shape[0], a.shape[1] // 2) + a.shape[2:] if pick_other_half else a.shape, a.dtype) for a in arrs]

    def body(*refs):
        srcs, dsts, send_sems, recv_sems = refs[:n], refs[n:2 * n], refs[2 * n], refs[2 * n + 1]
        x, y, c = _coords()
        cps = []
        for a in range(n):
            src = srcs[a]
            if pick_other_half:
                h = src.shape[1] // 2
                src = src.at[:, pl.ds(pl.multiple_of((1 - c) * h, 16), h)]
            cp = _remote(src, dsts[a], send_sems, recv_sems, a, (x, y, 1 - c))
            cp.start()
            cps.append(cp)
        for cp in cps:
            cp.wait()

    return pl.pallas_call(
        body, name=name, out_shape=outs, in_specs=[HBM] * n, out_specs=[HBM] * n,
        scratch_shapes=[pltpu.SemaphoreType.DMA((n,)), pltpu.SemaphoreType.DMA((n,))],
    )(*arrs)


def _rs_add_pairs(gs, others, c, name):
    n = len(gs)
    blocks = [(g.shape[1] // 4, g.shape[2]) for g in gs]

    def body(c_ref, *refs):
        for a in range(n):
            refs[2 * n + a][...] = (refs[a][...] + refs[n + a][...]).astype(BF16)

    mine = lambda b: pl.BlockSpec((None,) + b, lambda j, s, c_ref: (j, c_ref[0] * 2 + s, 0))
    flat = lambda b: pl.BlockSpec((None,) + b, lambda j, s, c_ref: (j, s, 0))
    return pl.pallas_call(
        body, name=name,
        grid_spec=pltpu.PrefetchScalarGridSpec(
            num_scalar_prefetch=1, grid=(N_CHIPS, 2),
            in_specs=[mine(b) for b in blocks] + [flat(b) for b in blocks],
            out_specs=[flat(b) for b in blocks]),
        out_shape=[jax.ShapeDtypeStruct(o.shape, BF16) for o in others],
        compiler_params=_params(2),
    )(c, *gs, *others)


def _rs_exchange_arrays(parts):
    n = len(parts)

    def body(*refs):
        _rsx_start(refs[:n], refs[n:2 * n], refs[2 * n], refs[2 * n + 1])
        _rsx_finish(refs[:n], refs[n:2 * n], refs[2 * n], refs[2 * n + 1])

    _, shapes, n_sems, _, _ = _rsx_comm(parts)
    return pl.pallas_call(
        body, name="rs_exchange_chips", out_shape=shapes, in_specs=[HBM] * n, out_specs=[HBM] * n,
        scratch_shapes=[pltpu.SemaphoreType.DMA((n_sems,)), pltpu.SemaphoreType.DMA((n_sems,))],
    )(*parts)


def _rsx_copies(srcs, dsts, send_sems, recv_sems):
    x, y, c = _coords()
    others = [(1 - x, y), (x, 1 - y), (1 - x, 1 - y)]
    return [_remote(src.at[2 * ox + oy], dst.at[k], send_sems, recv_sems, 3 * a + k, (ox, oy, c))
            for a, (src, dst) in enumerate(zip(srcs, dsts)) for k, (ox, oy) in enumerate(others)]


def _rsx_start(srcs, dsts, send_sems, recv_sems):
    for cp in _rsx_copies(srcs, dsts, send_sems, recv_sems):
        cp.start()


def _rsx_finish(srcs, dsts, send_sems, recv_sems):
    for cp in _rsx_copies(srcs, dsts, send_sems, recv_sems):
        cp.wait()


def _rsx_comm(parts):
    shapes = [jax.ShapeDtypeStruct((3,) + p.shape[1:], p.dtype) for p in parts]
    return (list(parts), shapes, 3 * len(parts), _rsx_start, _rsx_finish)


def _rs_add_totals(parts, recvs, chip):
    n = len(parts)
    blocks = [(p.shape[1] // 2, p.shape[2]) for p in parts]

    def body(chip_ref, *refs):
        f = lambda r: r[...].astype(F32)
        for a in range(n):
            p, r0, r1, r2 = refs[a], refs[n + 3 * a], refs[n + 3 * a + 1], refs[n + 3 * a + 2]
            refs[4 * n + a][...] = (f(p) + f(r0)) + (f(r1) + f(r2))

    own = lambda b: pl.BlockSpec((None,) + b, lambda s, chip_ref: (chip_ref[0], s, 0))
    slot = lambda b, k: pl.BlockSpec((None,) + b, lambda s, chip_ref, k=k: (k, s, 0))
    recv_specs = [slot(b, k) for b in blocks for k in range(3)]
    recv_args = [r for r in recvs for _ in range(3)]
    return pl.pallas_call(
        body, name="rs_add_totals",
        grid_spec=pltpu.PrefetchScalarGridSpec(
            num_scalar_prefetch=1, grid=(2,),
            in_specs=[own(b) for b in blocks] + recv_specs,
            out_specs=[pl.BlockSpec(b, lambda s, chip_ref: (s, 0)) for b in blocks]),
        out_shape=[jax.ShapeDtypeStruct(p.shape[1:], F32) for p in parts],
        compiler_params=_params(1),
    )(chip, *parts, *recv_args)


def _permute_w_in(wt):
    return jnp.concatenate([wt[:QKV_COLS], wt[QKV_COLS + LOGIT_COLS:IN_COLS], wt[QKV_COLS:QKV_COLS + LOGIT_COLS],
                            jnp.zeros((IN_COLS_PADDED - IN_COLS, wt.shape[1]), wt.dtype)], axis=0)


def _pad_row(v):
    v = v.reshape(1, -1)
    return jnp.pad(v, ((0, 0), (0, D_MODEL - v.shape[1])))


def kernel(x, norm_ffn1, ffn1_gate, ffn1_up, ffn1_down, norm_mix, w_in, conv_w, a_log, dt_bias, dn_norm, w_out, norm_ffn2, ffn2_gate, ffn2_up, ffn2_down, norm_final, loss_target, m_norm_ffn1, m_ffn1_gate, m_ffn1_up, m_ffn1_down, m_norm_mix, m_w_in, m_conv_w, m_a_log, m_dt_bias, m_dn_norm, m_w_out, m_norm_ffn2, m_ffn2_gate, m_ffn2_up, m_ffn2_down, m_norm_final, v_norm_ffn1, v_ffn1_gate, v_ffn1_up, v_ffn1_down, v_norm_mix, v_w_in, v_conv_w, v_a_log, v_dt_bias, v_dn_norm, v_w_out, v_norm_ffn2, v_ffn2_gate, v_ffn2_up, v_ffn2_down, v_norm_final):
    cx, cy, cc = _coords()
    chip = 2 * cx + cy
    stored = lambda t, n: t[0] if n in ROW_SHARDED else t[0].T
    big_w = {n: stored(t, n) for n, t in dict(
        ffn1_gate=ffn1_gate, ffn1_up=ffn1_up, ffn1_down=ffn1_down, w_in=w_in, w_out=w_out,
        ffn2_gate=ffn2_gate, ffn2_up=ffn2_up, ffn2_down=ffn2_down).items()}
    big_m = {n: stored(t, n) for n, t in dict(
        ffn1_gate=m_ffn1_gate, ffn1_up=m_ffn1_up, ffn1_down=m_ffn1_down, w_in=m_w_in, w_out=m_w_out,
        ffn2_gate=m_ffn2_gate, ffn2_up=m_ffn2_up, ffn2_down=m_ffn2_down).items()}
    big_v = {n: stored(t, n) for n, t in dict(
        ffn1_gate=v_ffn1_gate, ffn1_up=v_ffn1_up, ffn1_down=v_ffn1_down, w_in=v_w_in, w_out=v_w_out,
        ffn2_gate=v_ffn2_gate, ffn2_up=v_ffn2_up, ffn2_down=v_ffn2_down).items()}

    cols = IN_COLS // N_CHIPS
    send = {n: big_w[n].astype(BF16) for n in BIG}
    send["w_in"] = jnp.pad(send["w_in"], ((0, W_IN_ROWS - cols), (0, 0)))
    early = tuple(n for n in BIG if n not in LATE_WEIGHTS)
    wts = dict(zip(early, _allgather_arrays([send[n] for n in early])))
    wts["w_in"] = _permute_w_in(wts["w_in"][:, :cols].reshape(IN_COLS, D_MODEL))
    dist = dict(late=[send[n] for n in LATE_WEIGHTS], c=cc.reshape(1).astype(jnp.int32),
                chip=chip.reshape(1).astype(jnp.int32))

    conv_shard = conv_w[0]
    emb = jnp.concatenate([jnp.where((chip == j) & (cc == 0), conv_shard, 0.0) for j in range(N_CHIPS)], axis=1)
    emb = jnp.pad(emb.reshape(6, D_MODEL), ((0, 2), (0, 0)))
    conv_full = _allreduce_small(emb, "allgather_conv_w")[:6].reshape(CONV_WIDTH, 3 * DN_WIDTH)

    zvec = jnp.zeros((1, 128), F32)
    small = dict(norm_ffn1=norm_ffn1, norm_mix=norm_mix, norm_ffn2=norm_ffn2, norm_final=norm_final[None],
                 conv_w=conv_full, avec=zvec.at[0, DN_HEADS:2 * DN_HEADS].set(a_log[0]),
                 dvec=zvec.at[0, DN_HEADS:2 * DN_HEADS].set(dt_bias[0]), dn_norm=dn_norm)

    loss, grad_x, reduced, sg = _local_step(x[0], loss_target[0], wts, small, dist)

    rows = [sg["norm_ffn1"], sg["norm_mix"], sg["norm_ffn2"], sg["norm_final"], _pad_row(sg["a_log"]), _pad_row(sg["dt_bias"]),
            _pad_row(sg["dn_norm"]), _pad_row(loss[0:1]), sg["conv_w"].reshape(6, D_MODEL), jnp.zeros((2, D_MODEL), F32)]
    red = _allreduce_small(jnp.concatenate(rows, axis=0), "allreduce_small")
    loss_out = red[7, 0]
    g_conv_full = red[8:14].reshape(CONV_WIDTH, 3 * DN_WIDTH)
    g_conv = lax.dynamic_slice_in_dim(g_conv_full, chip * (3 * DN_WIDTH // N_CHIPS), 3 * DN_WIDTH // N_CHIPS, axis=1)
    g_small = dict(norm_ffn1=red[0:1], norm_mix=red[1:2], norm_ffn2=red[2:3], norm_final=red[3],
                   a_log=red[4:5, DN_HEADS:2 * DN_HEADS], dt_bias=red[5:6, DN_HEADS:2 * DN_HEADS], dn_norm=red[6:7, :DN_HEAD_DIM])

    out_g, out_d, out_m, out_v = {}, {}, {}, {}
    for n in BIG:
        mine, other = reduced[n]
        g = jnp.where(cc == 0, jnp.concatenate([mine, other], axis=0), jnp.concatenate([other, mine], axis=0))
        if n == "w_in":
            to3 = lambda t: jnp.transpose(t, (2, 0, 1))
            g = g[:cols].reshape(cols, 1, D_MODEL)
            results = (g,) + tuple(_adamw(to3(w_in), g, to3(m_w_in), to3(v_w_in), "adamw_w_in"))
            out_g[n], out_d[n], out_m[n], out_v[n] = (jnp.transpose(t, (1, 2, 0)) for t in results)
            continue
        results = (g,) + tuple(_adamw(big_w[n], g, big_m[n], big_v[n], "adamw_" + n))
        out_g[n], out_d[n], out_m[n], out_v[n] = ((t if n in ROW_SHARDED else t.T)[None] for t in results)
    d, nm, nv = _adamw(conv_w[0], g_conv, m_conv_w[0], v_conv_w[0], "adamw_conv_w")
    out_g["conv_w"], out_d["conv_w"], out_m["conv_w"], out_v["conv_w"] = g_conv[None], d[None], nm[None], nv[None]

    small_names = ("norm_ffn1", "norm_mix", "norm_ffn2", "norm_final", "a_log", "dt_bias", "dn_norm")
    small_w = dict(norm_ffn1=norm_ffn1, norm_mix=norm_mix, norm_ffn2=norm_ffn2, norm_final=norm_final, a_log=a_log,
                   dt_bias=dt_bias, dn_norm=dn_norm)
    small_m = dict(norm_ffn1=m_norm_ffn1, norm_mix=m_norm_mix, norm_ffn2=m_norm_ffn2, norm_final=m_norm_final, a_log=m_a_log,
                   dt_bias=m_dt_bias, dn_norm=m_dn_norm)
    small_v = dict(norm_ffn1=v_norm_ffn1, norm_mix=v_norm_mix, norm_ffn2=v_norm_ffn2, norm_final=v_norm_final, a_log=v_a_log,
                   dt_bias=v_dt_bias, dn_norm=v_dn_norm)
    stack = lambda dct: jnp.concatenate([_pad_row(dct[n]) for n in small_names] + [jnp.zeros((1, D_MODEL), F32)], axis=0)
    d, nm, nv = _adamw(stack(small_w), stack(g_small), stack(small_m), stack(small_v), "adamw_small")
    for k, n in enumerate(small_names):
        shape = small_w[n].shape
        size = math.prod(shape)
        out_g[n] = g_small[n].reshape(shape)
        out_d[n], out_m[n], out_v[n] = (t[k, :size].reshape(shape) for t in (d, nm, nv))

    order = ("norm_ffn1", "ffn1_gate", "ffn1_up", "ffn1_down", "norm_mix", "w_in", "conv_w", "a_log", "dt_bias", "dn_norm",
             "w_out", "norm_ffn2", "ffn2_gate", "ffn2_up", "ffn2_down", "norm_final")
    return (loss_out, grad_x[None], *[out_g[n] for n in order], *[out_d[n] for n in order],
            *[out_m[n] for n in order], *[out_v[n] for n in order])
```

```python
import functools
import math

import jax
import jax.numpy as jnp
from jax import lax
from jax.experimental import pallas as pl
from jax.experimental.pallas import tpu as pltpu

F32 = jnp.float32
BF16 = jnp.bfloat16
HI = lax.Precision.HIGH

D_MODEL = 1024
ATTN_HEADS = 8
ATTN_WIDTH = 512
ATTN_BLOCK = 128
ATTN_SCALE = (ATTN_WIDTH // ATTN_HEADS) ** -0.5
DILATIONS = (1, 4, 16)
DN_HEADS = 4
DN_HEAD_DIM = 128
DN_WIDTH = 512
DN_CHUNK = 64
CONV_WIDTH = 4
NORM_EPS = 1e-6
L2_EPS = 1e-6
QKV_COLS = 3 * ATTN_WIDTH + 3 * DN_WIDTH
LOGIT_COLS = 2 * DN_HEADS
IN_COLS = QKV_COLS + LOGIT_COLS + DN_WIDTH
IN_COLS_PADDED = 3712
N_CHIPS = 4

ADAM_LR = 0.001
ADAM_B1 = 0.9
ADAM_B2 = 0.999
ADAM_EPS = 1e-08
ADAM_WD = 0.01
ADAM_STEP = 10

VMEM_LIMIT = 56 * 1024 * 1024
NEG_BIG = -1e30
MESH = pl.DeviceIdType.MESH


def _params(n_grid, vmem=VMEM_LIMIT):
    return pltpu.CompilerParams(dimension_semantics=("arbitrary",) * n_grid, vmem_limit_bytes=vmem)


def _call(body, args, *, name, grid, in_specs, out_specs, out_shape, scratch_shapes=(), comm=None):
    n_in, n_out, n_scr = len(in_specs), len(out_specs), len(scratch_shapes)
    hbm = pl.BlockSpec(memory_space=pl.ANY)
    srcs, dst_shapes, n_sems, start, finish = comm if comm is not None else ((), (), 0, None, None)
    ns, nd = len(srcs), len(dst_shapes)

    def full(*refs):
        ins, c_src = refs[:n_in], refs[n_in:n_in + ns]
        at = n_in + ns
        outs, c_dst = refs[at:at + n_out], refs[at + n_out:at + n_out + nd]
        scr = refs[at + n_out + nd:at + n_out + nd + n_scr]
        if comm is not None:
            ids = [pl.program_id(a) for a in range(len(grid))]
            first = functools.reduce(jnp.logical_and, [i == 0 for i in ids])
            last = functools.reduce(jnp.logical_and, [i == g - 1 for i, g in zip(ids, grid)])

            @pl.when(first)
            def _():
                start(c_src, c_dst, refs[-2], refs[-1])

        body(*ins, *outs, *scr)
        if comm is not None:
            @pl.when(last)
            def _():
                finish(c_src, c_dst, refs[-2], refs[-1])

    sems = [pltpu.SemaphoreType.DMA((n_sems,)), pltpu.SemaphoreType.DMA((n_sems,))] if comm is not None else []
    res = pl.pallas_call(
        full, name=name, grid=grid, in_specs=list(in_specs) + [hbm] * ns, out_specs=list(out_specs) + [hbm] * nd,
        out_shape=list(out_shape) + list(dst_shapes), scratch_shapes=list(scratch_shapes) + sems,
        compiler_params=_params(len(grid)),
    )(*args, *srcs)
    return res[:n_out], res[n_out:]


def _nt(a, b, precision=None):
    return lax.dot_general(a, b, (((1,), (1,)), ((), ())), preferred_element_type=F32, precision=precision)


def _tn(a, b, precision=None):
    return lax.dot_general(a, b, (((0,), (0,)), ((), ())), preferred_element_type=F32, precision=precision)


def _nn(a, b, precision=None):
    return jnp.dot(a, b, preferred_element_type=F32, precision=precision)


def _sigmoid(x):
    return 1.0 / (1.0 + jnp.exp(-x))


def _loss_head(xf, gain, target):
    r = lax.rsqrt(jnp.mean(xf * xf, axis=-1, keepdims=True) + NORM_EPS)
    xhat = xf * r
    err = xhat * gain - target
    part = 0.5 * jnp.sum(jnp.mean(err * err, axis=-1, keepdims=True), axis=0, keepdims=True)
    dy = err * (1.0 / xf.shape[-1])
    dgain = jnp.sum(dy * xhat, axis=0, keepdims=True)
    dxh = dy * gain
    return part, r * (dxh - xhat * jnp.mean(dxh * xhat, axis=-1, keepdims=True)), dgain


def _ffn_fwd(x, gain, wg, wu, wd, name, comm=None, head=None):
    S, D = x.shape
    nf, tf, _ = wg.shape
    tm = 512
    n_in = 5 if head is None else 7

    def body(*refs):
        x_ref, gain_ref, wg_ref, wu_ref, wd_ref = refs[:5]
        xo_ref, h_ref, g_ref, u_ref = refs[n_in:n_in + 4]
        acc_ref, hs_ref = refs[-2:]
        i = pl.program_id(0)
        j = pl.program_id(1)

        @pl.when(j == 0)
        def _():
            xf = x_ref[...]
            r = lax.rsqrt(jnp.mean(xf * xf, axis=-1, keepdims=True) + NORM_EPS)
            h = (xf * r * gain_ref[...]).astype(BF16)
            hs_ref[...] = h
            h_ref[...] = h
            acc_ref[...] = jnp.zeros_like(acc_ref)

        h = hs_ref[...]
        g = _nt(h, wg_ref[...])
        u = _nt(h, wu_ref[...])
        g_ref[...] = g.astype(BF16)
        u_ref[...] = u.astype(BF16)
        act = g * _sigmoid(g) * u
        acc_ref[...] += _nn(act.astype(BF16), wd_ref[...])

        if head is not None:
            hgain_ref, t_ref = refs[5:7]
            loss_ref, dgain_ref = refs[n_in + 4:n_in + 6]

            @pl.when((i == 0) & (j == 0))
            def _():
                loss_ref[...] = jnp.zeros_like(loss_ref)
                dgain_ref[...] = jnp.zeros_like(dgain_ref)

        @pl.when(j == nf - 1)
        def _():
            xo = x_ref[...] + 0.5 * acc_ref[...]
            if head is None:
                xo_ref[...] = xo
            else:
                part, dxo, dgain = _loss_head(xo, hgain_ref[...], t_ref[...])
                first = ((lax.broadcasted_iota(jnp.int32, (8, 128), 0) == 0)
                         & (lax.broadcasted_iota(jnp.int32, (8, 128), 1) == 0))
                loss_ref[...] += jnp.where(first, part, 0.0)
                dgain_ref[...] += dgain
                xo_ref[...] = dxo

    tok = pl.BlockSpec((tm, D), lambda i, j: (i, 0))
    row = pl.BlockSpec((1, D), lambda i, j: (0, 0))
    chunk = pl.BlockSpec((None, tf, D), lambda i, j: (j, 0, 0))
    act = pl.BlockSpec((None, tm, tf), lambda i, j: (j, i, 0))
    extra_in = [] if head is None else [row, tok]
    extra_out = [] if head is None else [pl.BlockSpec((8, 128), lambda i, j: (0, 0)), row]
    extra_shape = [] if head is None else [jax.ShapeDtypeStruct((8, 128), F32), jax.ShapeDtypeStruct((1, D), F32)]
    return _call(
        body, (x, gain, wg, wu, wd) + (() if head is None else tuple(head)), name=name, grid=(S // tm, nf), comm=comm,
        in_specs=[tok, row, chunk, chunk, chunk] + extra_in,
        out_specs=[tok, tok, act, act] + extra_out,
        out_shape=[jax.ShapeDtypeStruct((S, D), F32), jax.ShapeDtypeStruct((S, D), BF16),
                   jax.ShapeDtypeStruct((nf, S, tf), BF16), jax.ShapeDtypeStruct((nf, S, tf), BF16)] + extra_shape,
        scratch_shapes=[pltpu.VMEM((tm, D), F32), pltpu.VMEM((tm, D), BF16)])


def _rmsnorm_bwd(dh, xf, gain):
    r = lax.rsqrt(jnp.mean(xf * xf, axis=-1, keepdims=True) + NORM_EPS)
    xhat = xf * r
    dgain = jnp.sum(dh * xhat, axis=0, keepdims=True)
    dxh = dh * gain
    dx = r * (dxh - xhat * jnp.mean(dxh * xhat, axis=-1, keepdims=True))
    return dx, dgain


def _ffn_bwd(dxo, x, gain, g, u, wd, wg, wu, name, comm=None):
    S, D = x.shape
    nf, _, tf = g.shape
    tm = 512

    def body(dxo_ref, x_ref, gain_ref, g_ref, u_ref, wd_ref, wg_ref, wu_ref,
             dx_ref, dgain_ref, dg_ref, du_ref, act_ref, dout_ref, acc_ref, ds_ref):
        i = pl.program_id(0)
        j = pl.program_id(1)

        @pl.when(j == 0)
        def _():
            d = (0.5 * dxo_ref[...]).astype(BF16)
            ds_ref[...] = d
            dout_ref[...] = d
            acc_ref[...] = jnp.zeros_like(acc_ref)

        @pl.when((i == 0) & (j == 0))
        def _():
            dgain_ref[...] = jnp.zeros_like(dgain_ref)

        for half in range(2):
            rows = slice(half * (tm // 2), (half + 1) * (tm // 2))
            dact = _nt(ds_ref[rows, :], wd_ref[...])
            gv = g_ref[rows, :].astype(F32)
            uv = u_ref[rows, :].astype(F32)
            sg = _sigmoid(gv)
            silu = gv * sg
            act_ref[rows, :] = (silu * uv).astype(BF16)
            dgv = (dact * uv * (sg * (1.0 + gv * (1.0 - sg)))).astype(BF16)
            duv = (dact * silu).astype(BF16)
            dg_ref[rows, :] = dgv
            du_ref[rows, :] = duv
            acc_ref[rows, :] += _nn(dgv, wg_ref[...]) + _nn(duv, wu_ref[...])

        @pl.when(j == nf - 1)
        def _():
            dx, dgain = _rmsnorm_bwd(acc_ref[...], x_ref[...], gain_ref[...])
            dx_ref[...] = dxo_ref[...] + dx
            dgain_ref[...] += dgain

    return _call(
        body, (dxo, x, gain, g, u, wd, wg, wu), name=name, grid=(S // tm, nf), comm=comm,
        in_specs=[pl.BlockSpec((tm, D), lambda i, j: (i, 0)),
                  pl.BlockSpec((tm, D), lambda i, j: (i, 0)),
                  pl.BlockSpec((1, D), lambda i, j: (0, 0)),
                  pl.BlockSpec((None, tm, tf), lambda i, j: (j, i, 0)),
                  pl.BlockSpec((None, tm, tf), lambda i, j: (j, i, 0)),
                  pl.BlockSpec((None, tf, D), lambda i, j: (j, 0, 0)),
                  pl.BlockSpec((None, tf, D), lambda i, j: (j, 0, 0)),
                  pl.BlockSpec((None, tf, D), lambda i, j: (j, 0, 0))],
        out_specs=[pl.BlockSpec((tm, D), lambda i, j: (i, 0)),
                   pl.BlockSpec((1, D), lambda i, j: (0, 0)),
                   pl.BlockSpec((None, tm, tf), lambda i, j: (j, i, 0)),
                   pl.BlockSpec((None, tm, tf), lambda i, j: (j, i, 0)),
                   pl.BlockSpec((None, tm, tf), lambda i, j: (j, i, 0)),
                   pl.BlockSpec((tm, D), lambda i, j: (i, 0))],
        out_shape=[jax.ShapeDtypeStruct((S, D), F32), jax.ShapeDtypeStruct((1, D), F32),
                   jax.ShapeDtypeStruct((nf, S, tf), BF16), jax.ShapeDtypeStruct((nf, S, tf), BF16),
                   jax.ShapeDtypeStruct((nf, S, tf), BF16), jax.ShapeDtypeStruct((S, D), BF16)],
        scratch_shapes=[pltpu.VMEM((tm, D), F32), pltpu.VMEM((tm, D), BF16)])


def _matmul_tn(a, b, tm, tk, name):
    K, M = a.shape
    N = b.shape[1]

    def body(a_ref, b_ref, o_ref):
        @pl.when(pl.program_id(1) == 0)
        def _():
            o_ref[...] = jnp.zeros_like(o_ref)

        o_ref[...] += _tn(a_ref[...], b_ref[...])

    return pl.pallas_call(
        body, name=name, grid=(M // tm, K // tk),
        in_specs=[pl.BlockSpec((tk, tm), lambda i, k: (k, i)),
                  pl.BlockSpec((tk, N), lambda i, k: (k, 0))],
        out_specs=pl.BlockSpec((tm, N), lambda i, k: (i, 0)),
        out_shape=jax.ShapeDtypeStruct((M, N), F32),
        compiler_params=_params(2),
    )(a, b)


def _dw_chunks(a, b, tk, name, comm=None):
    nf, S, tf = a.shape
    N = b.shape[1]

    def body(a_ref, b_ref, o_ref):
        @pl.when(pl.program_id(1) == 0)
        def _():
            o_ref[...] = jnp.zeros_like(o_ref)

        o_ref[...] += _tn(a_ref[...], b_ref[...])

    (out,), landed = _call(
        body, (a, b), name=name, grid=(nf, S // tk), comm=comm,
        in_specs=[pl.BlockSpec((None, tk, tf), lambda j, k: (j, k, 0)),
                  pl.BlockSpec((tk, N), lambda j, k: (k, 0))],
        out_specs=[pl.BlockSpec((None, tf, N), lambda j, k: (j, 0, 0))],
        out_shape=[jax.ShapeDtypeStruct((nf, tf, N), F32)])
    return out, landed


VIEW_TILE = 512


def _view_spec(d, tile=VIEW_TILE):
    return pl.BlockSpec((tile // d, d * ATTN_WIDTH), lambda i: (i, 0))


def _view_shape(S, d, dtype):
    return jax.ShapeDtypeStruct((S // d, d * ATTN_WIDTH), dtype)


def _tile_to_views(val, planes, out_refs):
    for g in range(4):
        planes[g] = val[:, g * 128:(g + 1) * 128]
    for d, ref in zip(DILATIONS, out_refs):
        if d == 1:
            ref[...] = val.astype(ref.dtype)
            continue
        for r in range(d):
            for g in range(4):
                ref[:, r * ATTN_WIDTH + g * 128:r * ATTN_WIDTH + (g + 1) * 128] = (
                    planes[g, pl.ds(r, planes.shape[1] // d, stride=d), :].astype(ref.dtype))


def _view_to_tile(ref, d, planes):
    if d == 1:
        return ref[...].astype(F32)
    for r in range(d):
        for g in range(4):
            planes[g, pl.ds(r, planes.shape[1] // d, stride=d), :] = (
                ref[:, r * ATTN_WIDTH + g * 128:r * ATTN_WIDTH + (g + 1) * 128].astype(F32))
    return jnp.concatenate([planes[g] for g in range(4)], axis=1)


def _inproj_fwd(x, gain, w_in_p):
    S, D = x.shape
    tm = VIEW_TILE
    W = ATTN_WIDTH

    def body(x_ref, gain_ref, w_ref, h_ref, q1, q4, q16, k1, k4, k16, v1, v4, v16, dq_ref, dk_ref, dv_ref, gate_ref, bd_ref,
             planes):
        xf = x_ref[...]
        r = lax.rsqrt(jnp.mean(xf * xf, axis=-1, keepdims=True) + NORM_EPS)
        h = (xf * r * gain_ref[...]).astype(BF16)
        h_ref[...] = h
        _tile_to_views(_nt(h, w_ref[0:W, :]) * ATTN_SCALE, planes, (q1, q4, q16))
        _tile_to_views(_nt(h, w_ref[W:2 * W, :]), planes, (k1, k4, k16))
        _tile_to_views(_nt(h, w_ref[2 * W:3 * W, :]), planes, (v1, v4, v16))
        dq_ref[...] = _nt(h, w_ref[3 * W:4 * W, :])
        dk_ref[...] = _nt(h, w_ref[4 * W:5 * W, :])
        dv_ref[...] = _nt(h, w_ref[5 * W:6 * W, :])
        gate_ref[...] = _nt(h, w_ref[6 * W:7 * W, :])
        bd_ref[...] = _nt(h, w_ref[7 * W:7 * W + 128, :])

    tok = lambda w: pl.BlockSpec((tm, w), lambda i: (i, 0))
    return pl.pallas_call(
        body, name="inproj_fwd", grid=(S // tm,),
        in_specs=[tok(D), pl.BlockSpec((1, D), lambda i: (0, 0)),
                  pl.BlockSpec((IN_COLS_PADDED, D), lambda i: (0, 0))],
        out_specs=[tok(D)] + [_view_spec(d) for d in DILATIONS] * 3 + [tok(W)] * 4 + [tok(128)],
        out_shape=[jax.ShapeDtypeStruct((S, D), BF16)] + [_view_shape(S, d, BF16) for d in DILATIONS] * 3
                  + [jax.ShapeDtypeStruct((S, W), F32)] * 4 + [jax.ShapeDtypeStruct((S, 128), F32)],
        scratch_shapes=[pltpu.VMEM((4, tm, 128), F32)],
        compiler_params=_params(1),
    )(x, gain, w_in_p)


def _inproj_bwd(dxo, x, gain, attn_grads, dsecs, dbd, w_in_p):
    S, D = x.shape
    tm = VIEW_TILE
    W = ATTN_WIDTH

    def body(dxo_ref, x_ref, gain_ref, *rest):
        views, (s3, s4, s5, s6, dbd_ref, w_ref, dx_ref, dgain_ref, dproj_ref, planes) = rest[:9], rest[9:]

        @pl.when(pl.program_id(0) == 0)
        def _():
            dgain_ref[...] = jnp.zeros_like(dgain_ref)

        secs = []
        for k in range(3):
            parts = [_view_to_tile(views[3 * k + p], d, planes) for p, d in enumerate(DILATIONS)]
            secs.append(parts[0] + parts[1] + parts[2])
        secs += [s3[...], s4[...], s5[...], s6[...]]
        dh = jnp.zeros((tm, D), F32)
        for k, s in enumerate(secs):
            d = s.astype(BF16)
            dproj_ref[:, k * W:(k + 1) * W] = d
            dh += _nn(d, w_ref[k * W:(k + 1) * W, :])
        d = dbd_ref[...].astype(BF16)
        dproj_ref[:, 7 * W:7 * W + 128] = d
        dh += _nn(d, w_ref[7 * W:7 * W + 128, :])
        dx, dgain = _rmsnorm_bwd(dh, x_ref[...], gain_ref[...])
        dx_ref[...] = dxo_ref[...] + dx
        dgain_ref[...] += dgain

    tok = lambda w: pl.BlockSpec((tm, w), lambda i: (i, 0))
    return pl.pallas_call(
        body, name="inproj_bwd", grid=(S // tm,),
        in_specs=[tok(D), tok(D), pl.BlockSpec((1, D), lambda i: (0, 0))] + [_view_spec(d, tm) for d in DILATIONS] * 3
                 + [tok(W)] * 4 + [tok(128)] + [pl.BlockSpec((IN_COLS_PADDED, D), lambda i: (0, 0))],
        out_specs=[tok(D), pl.BlockSpec((1, D), lambda i: (0, 0)), tok(IN_COLS_PADDED)],
        out_shape=[jax.ShapeDtypeStruct((S, D), F32), jax.ShapeDtypeStruct((1, D), F32),
                   jax.ShapeDtypeStruct((S, IN_COLS_PADDED), BF16)],
        scratch_shapes=[pltpu.VMEM((4, tm, 128), F32)],
        compiler_params=_params(1),
    )(dxo, x, gain, *[g for grads in attn_grads for g in grads], *dsecs, dbd, w_in_p)


def _slope(h):
    return 2.0 ** (-8.0 * (h + 1) / ATTN_HEADS)


def _head_bias(steps, d, heads=tuple(range(ATTN_HEADS))):
    stepsf = steps.astype(F32)
    return jnp.stack([stepsf * (-_slope(h) * d) for h in heads])


def _hnt(a, b):
    return lax.dot_general(a, b, (((2,), (2,)), ((0,), (0,))), preferred_element_type=F32)


def _hnn(a, b):
    return lax.dot_general(a, b, (((2,), (1,)), ((0,), (0,))), preferred_element_type=F32)


def _blocks_per_step(nb):
    return next(n for n in (4, 2, 1) if nb % n == 0)


def _query_step_specs(qb):
    B = ATTN_BLOCK
    cur = pl.BlockSpec((qb * B, ATTN_WIDTH), lambda r, n: (n, r))
    prev = pl.BlockSpec((B, ATTN_WIDTH), lambda r, n: (jnp.maximum(qb * n - 1, 0), r))
    return cur, prev


def _prev_block(prev_ref, cur_ref, sub, sl):
    B = ATTN_BLOCK
    return prev_ref[:, sl] if sub == 0 else cur_ref[(sub - 1) * B:sub * B, sl]


def _head_cols(tile, lo, big):
    return [_head_col(tile, lo, big), _head_col(tile, jnp.logical_not(lo), big)]


def _attn_fwd(q, k, v, d, name):
    L = q.shape[0]
    nb = L // ATTN_BLOCK
    B = ATTN_BLOCK
    QB = _blocks_per_step(nb)

    def body(q_ref, kp_ref, kc_ref, vp_ref, vc_ref, o_ref, lse_ref):
        n = pl.program_id(1)
        qi = lax.broadcasted_iota(jnp.int32, (B, 2 * B), 0)
        kj = lax.broadcasted_iota(jnp.int32, (B, 2 * B), 1)
        steps = qi + B - kj
        band = (steps >= 0) & (steps <= B)
        lo = lax.broadcasted_iota(jnp.int32, (B, 128), 1) < 64
        bias = _head_bias(steps, d)
        for sub in range(QB):
            rows = slice(sub * B, (sub + 1) * B)
            valid = band & ((kj >= B) | (n > 0)) if sub == 0 else band
            qs, ks, vs = [], [], []
            for G in range(4):
                sl = slice(G * 128, (G + 1) * 128)
                qg = q_ref[rows, sl]
                kg = jnp.concatenate([_prev_block(kp_ref, kc_ref, sub, sl), kc_ref[rows, sl]], axis=0)
                vg = jnp.concatenate([_prev_block(vp_ref, vc_ref, sub, sl), vc_ref[rows, sl]], axis=0)
                qs += [jnp.where(lo, qg, jnp.zeros_like(qg)), jnp.where(lo, jnp.zeros_like(qg), qg)]
                ks += [kg, kg]
                vs += [vg, vg]
            s = jnp.where(valid, _hnt(jnp.stack(qs), jnp.stack(ks)) + bias, NEG_BIG)
            m = jnp.max(s, axis=-1, keepdims=True)
            p = jnp.exp(s - m)
            l = jnp.sum(p, axis=-1, keepdims=True)
            o = _hnn(p.astype(BF16), jnp.stack(vs)) / l
            lse = m + jnp.log(l)
            for G in range(4):
                sl = slice(G * 128, (G + 1) * 128)
                o_ref[rows, sl] = jnp.where(lo, o[2 * G], o[2 * G + 1])
                lse_ref[rows, sl] = jnp.where(lo, lse[2 * G], lse[2 * G + 1])

    cur, prev = _query_step_specs(QB)
    return pl.pallas_call(
        body, name=name, grid=(d, nb // QB),
        in_specs=[cur, prev, cur, prev, cur],
        out_specs=[cur, cur],
        out_shape=[jax.ShapeDtypeStruct((L, d * ATTN_WIDTH), F32)] * 2,
        compiler_params=_params(2),
    )(q, k, k, v, v)


def _attn_merge(parts):
    S = parts[0][0].shape[0]
    tm = VIEW_TILE

    def body(o1, s1, o2, s2, o3, s3, o_ref, lse1, lse4, lse16, planes):
        outs, lses = [], []
        for d, (o, s) in zip(DILATIONS, ((o1, s1), (o2, s2), (o3, s3))):
            outs.append(_view_to_tile(o, d, planes))
            lses.append(_view_to_tile(s, d, planes))
        mx = jnp.maximum(jnp.maximum(lses[0], lses[1]), lses[2])
        es = [jnp.exp(s - mx) for s in lses]
        den = es[0] + es[1] + es[2]
        o_ref[...] = (es[0] * outs[0] + es[1] * outs[1] + es[2] * outs[2]) / den
        _tile_to_views(mx + jnp.log(den), planes, (lse1, lse4, lse16))

    views = [_view_spec(d) for d in DILATIONS]
    flat = [t for p in parts for t in p]
    return pl.pallas_call(
        body, name="attn_merge", grid=(S // tm,),
        in_specs=[views[p] for p in range(3) for _ in range(2)],
        out_specs=[views[0]] + views,
        out_shape=[jax.ShapeDtypeStruct((S, ATTN_WIDTH), F32)] + [_view_shape(S, d, F32) for d in DILATIONS],
        scratch_shapes=[pltpu.VMEM((4, tm, 128), F32)],
        compiler_params=_params(1),
    )(*flat)


def _head_col(t, msk, big):
    if big:
        return jnp.max(jnp.where(msk, t, NEG_BIG), axis=-1, keepdims=True)
    return jnp.sum(jnp.where(msk, t, 0.0), axis=-1, keepdims=True) * (1.0 / 64.0)


def _attn_bwd_q(q, k, v, do, lse, dd, d, name):
    L = q.shape[0]
    nb = L // ATTN_BLOCK
    B = ATTN_BLOCK
    QB = _blocks_per_step(nb)

    def body(q_ref, kp_ref, kc_ref, vp_ref, vc_ref, do_ref, lse_ref, dd_ref, dq_ref):
        n = pl.program_id(1)
        qi = lax.broadcasted_iota(jnp.int32, (B, 2 * B), 0)
        kj = lax.broadcasted_iota(jnp.int32, (B, 2 * B), 1)
        steps = qi + B - kj
        band = (steps >= 0) & (steps <= B)
        lo = lax.broadcasted_iota(jnp.int32, (B, 128), 1) < 64
        bias = _head_bias(steps, d)
        for sub in range(QB):
            rows = slice(sub * B, (sub + 1) * B)
            valid = band & ((kj >= B) | (n > 0)) if sub == 0 else band
            qs, ks, vs, dos, lses, dcols = [], [], [], [], [], []
            for G in range(4):
                sl = slice(G * 128, (G + 1) * 128)
                qg = q_ref[rows, sl]
                kg = jnp.concatenate([_prev_block(kp_ref, kc_ref, sub, sl), kc_ref[rows, sl]], axis=0)
                vg = jnp.concatenate([_prev_block(vp_ref, vc_ref, sub, sl), vc_ref[rows, sl]], axis=0)
                dog = do_ref[rows, sl]
                qs += [jnp.where(lo, qg, jnp.zeros_like(qg)), jnp.where(lo, jnp.zeros_like(qg), qg)]
                dos += [jnp.where(lo, dog, 0.0).astype(BF16), jnp.where(lo, 0.0, dog).astype(BF16)]
                ks += [kg, kg]
                vs += [vg, vg]
                lses += _head_cols(lse_ref[rows, sl], lo, True)
                dcols += _head_cols(dd_ref[rows, sl], lo, False)
            kb = jnp.stack(ks)
            s = _hnt(jnp.stack(qs), kb) + bias
            p = jnp.where(valid, jnp.exp(jnp.where(valid, s, NEG_BIG) - jnp.stack(lses)), 0.0)
            dp = _hnt(jnp.stack(dos), jnp.stack(vs))
            ds = p * (dp - jnp.stack(dcols))
            dq = _hnn(ds.astype(BF16), kb) * ATTN_SCALE
            for G in range(4):
                dq_ref[rows, G * 128:(G + 1) * 128] = jnp.where(lo, dq[2 * G], dq[2 * G + 1]).astype(BF16)

    cur, prev = _query_step_specs(QB)
    return pl.pallas_call(
        body, name=name, grid=(d, nb // QB), in_specs=[cur, prev, cur, prev, cur, cur, cur, cur], out_specs=cur,
        out_shape=jax.ShapeDtypeStruct((L, d * ATTN_WIDTH), BF16), compiler_params=_params(2),
    )(q, k, k, v, v, do, lse, dd)


def _attn_bwd_kv(q, k, v, do, lse, dd, d, name):
    L = q.shape[0]
    nb = L // ATTN_BLOCK
    B = ATTN_BLOCK
    KB = _blocks_per_step(nb)
    n_steps = nb // KB

    def body(k_ref, v_ref, qc_ref, qn_ref, doc_ref, don_ref, lsec_ref, lsen_ref, ddc_ref, ddn_ref, dk_ref, dv_ref):
        j = pl.program_id(1)
        qrow = lax.broadcasted_iota(jnp.int32, (2 * B, B), 0)
        kk = lax.broadcasted_iota(jnp.int32, (2 * B, B), 1)
        steps = qrow - kk
        band = (steps >= 0) & (steps <= B)
        lo2 = lax.broadcasted_iota(jnp.int32, (2 * B, 128), 1) < 64
        lo = lax.broadcasted_iota(jnp.int32, (B, 128), 1) < 64
        stepsf = steps.astype(F32)
        for sub in range(KB):
            rows = slice(sub * B, (sub + 1) * B)
            last = sub == KB - 1
            valid = band & ((qrow < B) | (j < n_steps - 1)) if last else band
            after = lambda cur_ref, nxt_ref, sl: nxt_ref[:, sl] if last else cur_ref[(sub + 1) * B:(sub + 2) * B, sl]
            for G in range(4):
                sl = slice(G * 128, (G + 1) * 128)
                kg = k_ref[rows, sl]
                vg = v_ref[rows, sl]
                qq = jnp.concatenate([qc_ref[rows, sl], after(qc_ref, qn_ref, sl)], axis=0)
                doo = jnp.concatenate([doc_ref[rows, sl], after(doc_ref, don_ref, sl)], axis=0)
                lse2 = jnp.concatenate([lsec_ref[rows, sl], after(lsec_ref, lsen_ref, sl)], axis=0)
                dd2 = jnp.concatenate([ddc_ref[rows, sl], after(ddc_ref, ddn_ref, sl)], axis=0)
                doo_b = doo.astype(BF16)
                dks, dvs = [], []
                for half in (0, 1):
                    msk = lo2 if half == 0 else jnp.logical_not(lo2)
                    qm = jnp.where(msk, qq, jnp.zeros_like(qq))
                    s = _nt(qm, kg) - (_slope(2 * G + half) * d) * stepsf
                    lse_c = _head_col(lse2, msk, True)
                    p = jnp.where(valid, jnp.exp(jnp.where(valid, s, NEG_BIG) - lse_c), 0.0)
                    dvs.append(_tn(p.astype(BF16), doo_b))
                    dom = jnp.where(msk, doo, 0.0).astype(BF16)
                    dp = _nt(dom, vg)
                    dcol = _head_col(dd2, msk, False)
                    ds = p * (dp - dcol)
                    dks.append(_tn(ds.astype(BF16), qq))
                dk_ref[rows, sl] = jnp.where(lo, dks[0], dks[1]).astype(BF16)
                dv_ref[rows, sl] = jnp.where(lo, dvs[0], dvs[1]).astype(BF16)

    cur = pl.BlockSpec((KB * B, ATTN_WIDTH), lambda r, j: (j, r))
    nxt = pl.BlockSpec((B, ATTN_WIDTH), lambda r, j: (jnp.minimum(KB * (j + 1), nb - 1), r))
    return pl.pallas_call(
        body, name=name, grid=(d, n_steps), in_specs=[cur, cur, cur, nxt, cur, nxt, cur, nxt, cur, nxt],
        out_specs=[cur, cur],
        out_shape=[jax.ShapeDtypeStruct((L, d * ATTN_WIDTH), BF16)] * 2, compiler_params=_params(2),
    )(k, v, q, q, do, do, lse, lse, dd, dd)


CONV_T = 512
HALO = 8


def _per_head(head, refs):
    for h in range(DN_HEADS):
        lanes = pl.ds(h * DN_HEAD_DIM, DN_HEAD_DIM)
        head(*[r.at[:, lanes] for r in refs[:-1]], refs[-1])


def _conv_taps(pad_ref, w, T):
    acc = pad_ref[pl.ds(HALO - 3, T), :] * w[0:1, :]
    for j in range(1, CONV_WIDTH):
        acc = acc + pad_ref[pl.ds(HALO - 3 + j, T), :] * w[j:j + 1, :]
    return acc


def _conv_fwd(xq, xk, xv, conv_w):
    S = xq.shape[0]
    T = CONV_T

    def body(*refs):
        _per_head(head, refs)

    def head(xq_ref, xqh_ref, xk_ref, xkh_ref, xv_ref, xvh_ref, wq_ref, wk_ref, wv_ref,
             qn_ref, kn_ref, v_ref, pad_ref):
        i = pl.program_id(0)

        def act(x_ref, xh_ref, w_ref):
            pad_ref[pl.ds(0, HALO), :] = jnp.where(i > 0, xh_ref[...], 0.0)
            pad_ref[pl.ds(HALO, T), :] = x_ref[...]
            c = _conv_taps(pad_ref, w_ref[...], T)
            return c * _sigmoid(c)

        def l2n(t):
            return t * lax.rsqrt(jnp.sum(t * t, axis=-1, keepdims=True) + L2_EPS)

        qn_ref[...] = l2n(act(xq_ref, xqh_ref, wq_ref))
        kn_ref[...] = l2n(act(xk_ref, xkh_ref, wk_ref))
        v_ref[...] = act(xv_ref, xvh_ref, wv_ref)

    tile = pl.BlockSpec((T, DN_WIDTH), lambda i: (i, 0))
    halo = pl.BlockSpec((HALO, DN_WIDTH), lambda i: (jnp.maximum(i * (T // HALO) - 1, 0), 0))
    wspec = lambda sec: pl.BlockSpec((CONV_WIDTH, DN_WIDTH), lambda i, sec=sec: (0, sec))
    return pl.pallas_call(
        body, name="dn_conv_fwd", grid=(S // T,),
        in_specs=[tile, halo, tile, halo, tile, halo, wspec(0), wspec(1), wspec(2)],
        out_specs=[tile, tile, tile],
        out_shape=[jax.ShapeDtypeStruct((S, DN_WIDTH), F32)] * 3,
        scratch_shapes=[pltpu.VMEM((T + HALO, 128), F32)],
        compiler_params=_params(1),
    )(xq, xq, xk, xk, xv, xv, conv_w, conv_w, conv_w)


def _conv_bwd_pre(xq, xk, xv, conv_w, dqn, dkn, dv):
    S = xq.shape[0]
    T = CONV_T

    def body(*refs):
        _per_head(head, refs)

    def head(xq_ref, xqh_ref, xk_ref, xkh_ref, xv_ref, xvh_ref, wq_ref, wk_ref, wv_ref,
             dqn_ref, dkn_ref, dv_ref, dcq_ref, dck_ref, dcv_ref, dwq_ref, dwk_ref, dwv_ref, pad_ref):
        i = pl.program_id(0)

        def one(x_ref, xh_ref, w_ref, dy_ref, dc_ref, dw_ref, normed):
            pad_ref[pl.ds(0, HALO), :] = jnp.where(i > 0, xh_ref[...], 0.0)
            pad_ref[pl.ds(HALO, T), :] = x_ref[...]
            c = _conv_taps(pad_ref, w_ref[...], T)
            sg = _sigmoid(c)
            a = c * sg
            dy = dy_ref[...]
            if normed:
                r = lax.rsqrt(jnp.sum(a * a, axis=-1, keepdims=True) + L2_EPS)
                y = a * r
                da = r * (dy - y * jnp.sum(dy * y, axis=-1, keepdims=True))
            else:
                da = dy
            dc = da * (sg * (1.0 + c * (1.0 - sg)))
            dc_ref[...] = dc

            @pl.when(i == 0)
            def _():
                dw_ref[...] = jnp.zeros_like(dw_ref)

            rows = [jnp.sum(dc * pad_ref[pl.ds(HALO - 3 + j, T), :], axis=0, keepdims=True) for j in range(CONV_WIDTH)]
            dw_ref[...] += jnp.concatenate(rows + [jnp.zeros((8 - CONV_WIDTH, 128), F32)], axis=0)

        one(xq_ref, xqh_ref, wq_ref, dqn_ref, dcq_ref, dwq_ref, True)
        one(xk_ref, xkh_ref, wk_ref, dkn_ref, dck_ref, dwk_ref, True)
        one(xv_ref, xvh_ref, wv_ref, dv_ref, dcv_ref, dwv_ref, False)

    tile = pl.BlockSpec((T, DN_WIDTH), lambda i: (i, 0))
    halo = pl.BlockSpec((HALO, DN_WIDTH), lambda i: (jnp.maximum(i * (T // HALO) - 1, 0), 0))
    wspec = lambda sec: pl.BlockSpec((CONV_WIDTH, DN_WIDTH), lambda i, sec=sec: (0, sec))
    dwspec = pl.BlockSpec((8, DN_WIDTH), lambda i: (0, 0))
    return pl.pallas_call(
        body, name="dn_conv_bwd_pre", grid=(S // T,),
        in_specs=[tile, halo, tile, halo, tile, halo, wspec(0), wspec(1), wspec(2), tile, tile, tile],
        out_specs=[tile, tile, tile, dwspec, dwspec, dwspec],
        out_shape=[jax.ShapeDtypeStruct((S, DN_WIDTH), F32)] * 3 + [jax.ShapeDtypeStruct((8, DN_WIDTH), F32)] * 3,
        scratch_shapes=[pltpu.VMEM((T + HALO, 128), F32)],
        compiler_params=_params(1),
    )(xq, xq, xk, xk, xv, xv, conv_w, conv_w, conv_w, dqn, dkn, dv)


def _conv_bwd_x(dcq, dck, dcv, conv_w):
    S = dcq.shape[0]
    T = CONV_T
    nt = S // T

    def body(*refs):
        _per_head(head, refs)

    def head(dq_ref, dqh_ref, dk_ref, dkh_ref, dv_ref, dvh_ref, wq_ref, wk_ref, wv_ref,
             oq_ref, ok_ref, ov_ref, pad_ref):
        i = pl.program_id(0)

        def one(d_ref, dh_ref, w_ref, o_ref):
            pad_ref[pl.ds(0, T), :] = d_ref[...]
            pad_ref[pl.ds(T, HALO), :] = jnp.where(i < nt - 1, dh_ref[...], 0.0)
            w = w_ref[...]
            acc = pad_ref[pl.ds(3, T), :] * w[0:1, :]
            for j in range(1, CONV_WIDTH):
                acc = acc + pad_ref[pl.ds(3 - j, T), :] * w[j:j + 1, :]
            o_ref[...] = acc

        one(dq_ref, dqh_ref, wq_ref, oq_ref)
        one(dk_ref, dkh_ref, wk_ref, ok_ref)
        one(dv_ref, dvh_ref, wv_ref, ov_ref)

    tile = pl.BlockSpec((T, DN_WIDTH), lambda i: (i, 0))
    halo = pl.BlockSpec((HALO, DN_WIDTH), lambda i: (jnp.minimum((i + 1) * (T // HALO), S // HALO - 1), 0))
    wspec = lambda sec: pl.BlockSpec((CONV_WIDTH, DN_WIDTH), lambda i, sec=sec: (0, sec))
    return pl.pallas_call(
        body, name="dn_conv_bwd_x", grid=(nt,),
        in_specs=[tile, halo, tile, halo, tile, halo, wspec(0), wspec(1), wspec(2)],
        out_specs=[tile, tile, tile],
        out_shape=[jax.ShapeDtypeStruct((S, DN_WIDTH), F32)] * 3,
        scratch_shapes=[pltpu.VMEM((T + HALO, 128), F32)],
        compiler_params=_params(1),
    )(dcq, dcq, dck, dck, dcv, dcv, conv_w, conv_w, conv_w)


PREP_CHUNKS = 4
SCAN_CHUNKS = 8


def _bnn(a, b):
    return lax.dot_general(a, b, (((2,), (1,)), ((0,), (0,))), preferred_element_type=F32, precision=HI)


def _bnt(a, b):
    return lax.dot_general(a, b, (((2,), (2,)), ((0,), (0,))), preferred_element_type=F32, precision=HI)


def _btn(a, b):
    return lax.dot_general(a, b, (((1,), (1,)), ((0,), (0,))), preferred_element_type=F32, precision=HI)


def _tri_inverse_b(a, blk, eye):
    dg = jnp.where(blk, a, 0.0)
    lo = a - dg
    d2 = _bnn(dg, dg)
    d4 = _bnn(d2, d2)
    d8 = _bnn(d4, d4)
    td = _bnn(_bnn(_bnn(eye - dg, eye + d2), eye + d4), eye + d8)
    b = _bnn(td, lo)
    b2 = _bnn(b, b)
    return _bnn(_bnn(eye - b, eye + b2), td)


def _dn_common_b(bds, avec, dvec, q_raw, k, v, t=None):
    C = DN_CHUNK
    lane = lax.broadcasted_iota(jnp.int32, (C, 128), 1)
    row = lax.broadcasted_iota(jnp.int32, (1, C, C), 1)
    col = lax.broadcasted_iota(jnp.int32, (1, C, C), 2)
    incl = row >= col
    strict = row > col
    eye = (row == col).astype(F32)
    blk = (row // 16) == (col // 16)
    pick = lambda tile, ln: jnp.sum(jnp.where(lane == ln, tile, 0.0), axis=-1, keepdims=True)
    betas, graws, zcs = [], [], []
    for bd in bds:
        z = bd + dvec
        g_all = -jnp.exp(avec) * (jnp.maximum(z, 0.0) + jnp.log(1.0 + jnp.exp(-jnp.abs(z))))
        beta_all = _sigmoid(bd)
        for h in range(DN_HEADS):
            betas.append(pick(beta_all, h))
            graws.append(pick(g_all, DN_HEADS + h))
            zcs.append(pick(z, DN_HEADS + h))
    beta, graw, zc = jnp.stack(betas), jnp.stack(graws), jnp.stack(zcs)
    to_row = lambda c: jnp.sum(eye * c, axis=1, keepdims=True)
    gc = jnp.sum(jnp.where(incl, to_row(graw), 0.0), axis=-1, keepdims=True)
    decay = jnp.exp(jnp.where(incl, gc - to_row(gc), NEG_BIG))
    q = q_raw * (DN_HEAD_DIM ** -0.5)
    kb = k * beta
    kk = _bnt(kb, k)
    if t is None:
        t = _tri_inverse_b(jnp.where(strict, kk * decay, 0.0), blk, eye)
    eg = jnp.exp(gc)
    rhs_w = kb * eg
    u = _bnn(t, v * beta)
    w = _bnn(t, rhs_w)
    qk = _bnt(q, k)
    aq = jnp.where(incl, qk * decay, 0.0)
    last = lax.broadcasted_iota(jnp.int32, (1, C, 1), 1) == C - 1
    g_last = jnp.sum(jnp.where(last, gc, 0.0), axis=1, keepdims=True)
    ekd = jnp.exp(g_last - gc)
    return dict(beta=beta, graw=graw, zc=zc, gc=gc, decay=decay, q=q, kb=kb, kk=kk, t=t, eg=eg, rhs_w=rhs_w,
                u=u, w=w, qk=qk, aq=aq, g_last=g_last, ekd=ekd, kd=k * ekd, qg=q * eg,
                incl=incl, strict=strict, eye=eye, lane=lane, row=row, col=col, last=last)


def _stack_heads(ref, rows):
    return jnp.stack([ref[rows, h * DN_HEAD_DIM:(h + 1) * DN_HEAD_DIM] for h in range(DN_HEADS)])


def _stack_units(ref, nc):
    C = DN_CHUNK
    return jnp.concatenate([_stack_heads(ref, slice(ci * C, (ci + 1) * C)) for ci in range(nc)], axis=0)


def _store_units(ref, val, nc):
    C = DN_CHUNK
    for ci in range(nc):
        for h in range(DN_HEADS):
            ref[ci * C:(ci + 1) * C, h * DN_HEAD_DIM:(h + 1) * DN_HEAD_DIM] = val[ci * DN_HEADS + h]


def _dn_prep(qn, kn, v, bd, avec, dvec):
    S = qn.shape[0]
    C = DN_CHUNK
    N = S // C
    nc = PREP_CHUNKS

    def body(q_ref, k_ref, v_ref, bd_ref, a_ref, d_ref, u_ref, w_ref, qg_ref, kd_ref, aq_ref, t_ref, egl_ref):
        bds = [bd_ref[ci * C:(ci + 1) * C, :] for ci in range(nc)]
        c = _dn_common_b(bds, a_ref[...], d_ref[...], _stack_units(q_ref, nc), _stack_units(k_ref, nc), _stack_units(v_ref, nc))
        _store_units(u_ref, c["u"], nc)
        _store_units(w_ref, c["w"], nc)
        _store_units(qg_ref, c["qg"], nc)
        _store_units(kd_ref, c["kd"], nc)
        egl = jnp.broadcast_to(jnp.exp(c["g_last"]), (nc * DN_HEADS, 1, 128))
        for ci in range(nc):
            for h in range(DN_HEADS):
                aq_ref[h, ci * C:(ci + 1) * C, :] = c["aq"][ci * DN_HEADS + h]
                t_ref[h, ci * C:(ci + 1) * C, :] = c["t"][ci * DN_HEADS + h]
            egl_ref[ci * 8:(ci + 1) * 8, :] = jnp.concatenate(
                [egl[ci * DN_HEADS + h] for h in range(DN_HEADS)] + [jnp.zeros((8 - DN_HEADS, 128), F32)], axis=0)

    tok = lambda w: pl.BlockSpec((nc * C, w), lambda n: (n, 0))
    sq = pl.BlockSpec((DN_HEADS, nc * C, C), lambda n: (0, n, 0))
    vec = pl.BlockSpec((1, 128), lambda n: (0, 0))
    return pl.pallas_call(
        body, name="dn_prep", grid=(N // nc,),
        in_specs=[tok(DN_WIDTH)] * 3 + [tok(128), vec, vec],
        out_specs=[tok(DN_WIDTH)] * 4 + [sq, sq, pl.BlockSpec((nc * 8, 128), lambda n: (n, 0))],
        out_shape=[jax.ShapeDtypeStruct((S, DN_WIDTH), F32)] * 4 + [jax.ShapeDtypeStruct((DN_HEADS, S, C), F32)] * 2
                  + [jax.ShapeDtypeStruct((N * 8, 128), F32)],
        compiler_params=_params(1),
    )(qn, kn, v, bd, avec, dvec)


def _dn_scan_fwd(u, w, qg, kd, aq, egl, gate, dn_gain):
    S = u.shape[0]
    C = DN_CHUNK
    N = S // C
    HD = DN_HEAD_DIM
    nc = SCAN_CHUNKS

    def body(u_ref, w_ref, qg_ref, kd_ref, aq_ref, egl_ref, gate_ref, gain_ref, dn_ref, o_ref, vn_ref, st_ref, state_ref):
        @pl.when(pl.program_id(0) == 0)
        def _():
            state_ref[...] = jnp.zeros_like(state_ref)

        gain = gain_ref[...]
        for ci in range(nc):
            rows = slice(ci * C, (ci + 1) * C)
            st = state_ref[...]
            for h in range(DN_HEADS):
                st_ref[ci * DN_WIDTH + h * HD:ci * DN_WIDTH + (h + 1) * HD, :] = st[h]
            v_new = _stack_heads(u_ref, rows) - _bnn(_stack_heads(w_ref, rows), st)
            o = _bnn(_stack_heads(qg_ref, rows), st) + _bnn(aq_ref[:, rows, :], v_new)
            egl = jnp.stack([egl_ref[ci * 8 + h:ci * 8 + h + 1, :] for h in range(DN_HEADS)])
            state_ref[...] = st * egl + _btn(_stack_heads(kd_ref, rows), v_new)
            r = lax.rsqrt(jnp.mean(o * o, axis=-1, keepdims=True) + NORM_EPS)
            gt = _stack_heads(gate_ref, rows)
            dn = o * r * gain * (gt * _sigmoid(gt))
            for h in range(DN_HEADS):
                sl = slice(h * HD, (h + 1) * HD)
                vn_ref[rows, sl] = v_new[h]
                o_ref[rows, sl] = o[h]
                dn_ref[rows, sl] = dn[h]

    tok = lambda wd: pl.BlockSpec((nc * C, wd), lambda n: (n, 0))
    sq = pl.BlockSpec((DN_HEADS, nc * C, C), lambda n: (0, n, 0))
    vec = pl.BlockSpec((1, 128), lambda n: (0, 0))
    return pl.pallas_call(
        body, name="dn_scan_fwd", grid=(N // nc,),
        in_specs=[tok(DN_WIDTH)] * 4 + [sq, pl.BlockSpec((nc * 8, 128), lambda n: (n, 0)), tok(DN_WIDTH), vec],
        out_specs=[tok(DN_WIDTH)] * 3 + [pl.BlockSpec((nc * DN_WIDTH, HD), lambda n: (n, 0))],
        out_shape=[jax.ShapeDtypeStruct((S, DN_WIDTH), F32)] * 3 + [jax.ShapeDtypeStruct((N * DN_WIDTH, HD), F32)],
        scratch_shapes=[pltpu.VMEM((DN_HEADS, HD, HD), F32)],
        compiler_params=_params(1),
    )(u, w, qg, kd, aq, egl, gate, dn_gain)


def _dn_scan_bwd(w, qg, kd, aq, egl, gate, dn_gain, o, ddn):
    S = w.shape[0]
    C = DN_CHUNK
    N = S // C
    HD = DN_HEAD_DIM
    nc = SCAN_CHUNKS

    def body(w_ref, qg_ref, kd_ref, aq_ref, egl_ref, gate_ref, gain_ref, o_ref, ddn_ref,
             do_ref, dvn_ref, dgate_ref, dst_ref, small_ref, dstate_ref):
        @pl.when(pl.program_id(0) == 0)
        def _():
            dstate_ref[...] = jnp.zeros_like(dstate_ref)
            small_ref[...] = jnp.zeros_like(small_ref)

        gain = gain_ref[...]
        d_gain = jnp.zeros((1, 128), F32)
        for ci in reversed(range(nc)):
            rows = slice(ci * C, (ci + 1) * C)
            dsn = dstate_ref[...]
            for h in range(DN_HEADS):
                dst_ref[ci * DN_WIDTH + h * HD:ci * DN_WIDTH + (h + 1) * HD, :] = dsn[h]
            ov = _stack_heads(o_ref, rows)
            r = lax.rsqrt(jnp.mean(ov * ov, axis=-1, keepdims=True) + NORM_EPS)
            on = ov * r
            gt = _stack_heads(gate_ref, rows)
            sgt = _sigmoid(gt)
            silu_g = gt * sgt
            dy = _stack_heads(ddn_ref, rows)
            d_gain = d_gain + jnp.sum(jnp.sum(dy * on * silu_g, axis=1, keepdims=True), axis=0)
            dgate = dy * on * gain * (sgt * (1.0 + gt * (1.0 - sgt)))
            don = dy * gain * silu_g
            do = r * (don - on * jnp.mean(don * on, axis=-1, keepdims=True))
            d_vnew = _btn(aq_ref[:, rows, :], do) + _bnn(_stack_heads(kd_ref, rows), dsn)
            egl = jnp.stack([egl_ref[ci * 8 + h:ci * 8 + h + 1, :] for h in range(DN_HEADS)])
            dstate_ref[...] = _btn(_stack_heads(qg_ref, rows), do) + dsn * egl - _btn(_stack_heads(w_ref, rows), d_vnew)
            for h in range(DN_HEADS):
                sl = slice(h * HD, (h + 1) * HD)
                do_ref[rows, sl] = do[h]
                dvn_ref[rows, sl] = d_vnew[h]
                dgate_ref[rows, sl] = dgate[h]
        small_ref[...] += jnp.concatenate([d_gain, jnp.zeros((7, 128), F32)], axis=0)

    nb = N // nc
    tok = lambda wd: pl.BlockSpec((nc * C, wd), lambda i: (nb - 1 - i, 0))
    sq = pl.BlockSpec((DN_HEADS, nc * C, C), lambda i: (0, nb - 1 - i, 0))
    vec = pl.BlockSpec((1, 128), lambda i: (0, 0))
    return pl.pallas_call(
        body, name="dn_scan_bwd", grid=(nb,),
        in_specs=[tok(DN_WIDTH)] * 3 + [sq, pl.BlockSpec((nc * 8, 128), lambda i: (nb - 1 - i, 0)), tok(DN_WIDTH), vec,
                                       tok(DN_WIDTH), tok(DN_WIDTH)],
        out_specs=[tok(DN_WIDTH)] * 3 + [pl.BlockSpec((nc * DN_WIDTH, HD), lambda i: (nb - 1 - i, 0)),
                                        pl.BlockSpec((8, 128), lambda i: (0, 0))],
        out_shape=[jax.ShapeDtypeStruct((S, DN_WIDTH), F32)] * 3 + [jax.ShapeDtypeStruct((N * DN_WIDTH, HD), F32),
                                                                  jax.ShapeDtypeStruct((8, 128), F32)],
        scratch_shapes=[pltpu.VMEM((DN_HEADS, HD, HD), F32)],
        compiler_params=_params(1),
    )(w, qg, kd, aq, egl, gate, dn_gain, o, ddn)


def _dn_post(qn, kn, v, bd, avec, dvec, t_inv, v_new_all, states, dstates, do_all, dvn_all, comm=None):
    S = qn.shape[0]
    C = DN_CHUNK
    N = S // C
    HD = DN_HEAD_DIM
    nc = PREP_CHUNKS
    B = nc * DN_HEADS

    def body(q_ref, k_ref, v_ref, bd_ref, a_ref, d_ref, t_ref, vn_ref, st_ref, dst_ref, do_ref, dvn_ref,
             dq_ref, dk_ref, dv_ref, dbd_ref, small_ref):
        @pl.when(pl.program_id(0) == 0)
        def _():
            small_ref[...] = jnp.zeros_like(small_ref)

        avec = a_ref[...]
        bds = [bd_ref[ci * C:(ci + 1) * C, :] for ci in range(nc)]
        k = _stack_units(k_ref, nc)
        vv = _stack_units(v_ref, nc)
        t = jnp.concatenate([t_ref[:, ci * C:(ci + 1) * C, :] for ci in range(nc)], axis=0)
        c = _dn_common_b(bds, avec, d_ref[...], _stack_units(q_ref, nc), k, vv, t=t)
        q, kb, eg, u, w = c["q"], c["kb"], c["eg"], c["u"], c["w"]
        beta, decay, incl, strict, eye = c["beta"], c["decay"], c["incl"], c["strict"], c["eye"]
        st = jnp.stack([st_ref[b * HD:(b + 1) * HD, :] for b in range(B)])
        dsn = jnp.stack([dst_ref[b * HD:(b + 1) * HD, :] for b in range(B)])
        v_new = _stack_units(vn_ref, nc)
        do = _stack_units(do_ref, nc)
        d_vnew = _stack_units(dvn_ref, nc)
        egl = jnp.exp(c["g_last"])
        daq = jnp.where(incl, _bnt(do, v_new), 0.0)
        d_qg = _bnt(do, st)
        d_kd = _bnt(v_new, dsn)
        d_glast = jnp.sum(jnp.sum(dsn * st, axis=-1, keepdims=True), axis=1, keepdims=True) * egl
        d_w = -_bnt(d_vnew, st)
        d_ru = _btn(t, d_vnew)
        d_rw = _btn(t, d_w)
        da = -jnp.where(strict, _bnt(d_ru, u) + _bnt(d_rw, w), 0.0)
        dv = d_ru * beta
        dbeta = jnp.sum(d_ru * vv, axis=-1, keepdims=True)
        dkb = d_rw * eg
        dgc = jnp.sum(d_rw * c["rhs_w"], axis=-1, keepdims=True)
        dkk = da * decay
        ddecay = da * c["kk"]
        dkb = dkb + _bnn(dkk, k)
        dk = _btn(dkk, kb)
        dqk = daq * decay
        ddecay = ddecay + daq * c["qk"]
        dq = _bnn(dqk, k)
        dk = dk + _btn(dqk, q)
        m = ddecay * decay
        col_sum = jnp.sum(m, axis=1, keepdims=True)
        dgc = dgc + jnp.sum(m, axis=-1, keepdims=True) - jnp.sum(eye * col_sum, axis=-1, keepdims=True)
        dq = dq + d_qg * eg
        dgc = dgc + jnp.sum(d_qg * c["qg"], axis=-1, keepdims=True)
        dk = dk + d_kd * c["ekd"]
        tk = jnp.sum(d_kd * c["kd"], axis=-1, keepdims=True)
        dgc = dgc - tk
        d_glast = d_glast + jnp.sum(tk, axis=1, keepdims=True)
        dk = dk + dkb * beta
        dbeta = dbeta + jnp.sum(dkb * k, axis=-1, keepdims=True)
        dgc = dgc + jnp.where(c["last"], d_glast, 0.0)
        dgc_row = jnp.sum(eye * dgc, axis=1, keepdims=True)
        dgraw = jnp.sum(jnp.where(c["col"] >= c["row"], dgc_row, 0.0), axis=-1, keepdims=True)
        _store_units(dq_ref, dq * (HD ** -0.5), nc)
        _store_units(dk_ref, dk, nc)
        _store_units(dv_ref, dv, nc)
        dbraw = dbeta * beta * (1.0 - beta)
        dzc = dgraw * _sigmoid(c["zc"])
        ga = dgraw * c["graw"]
        lane = c["lane"]
        lane1 = lax.broadcasted_iota(jnp.int32, (1, 128), 1)
        neg_ea = -jnp.exp(avec)
        d_alog = jnp.zeros((1, 128), F32)
        d_dt = jnp.zeros((1, 128), F32)
        for ci in range(nc):
            dbd = jnp.zeros((C, 128), F32)
            for h in range(DN_HEADS):
                b = ci * DN_HEADS + h
                dz = dzc[b] * neg_ea
                dbd = dbd + jnp.where(lane == h, dbraw[b], 0.0) + jnp.where(lane == DN_HEADS + h, dz, 0.0)
                d_alog = d_alog + jnp.where(lane1 == DN_HEADS + h, jnp.sum(ga[b], axis=0, keepdims=True), 0.0)
                d_dt = d_dt + jnp.where(lane1 == DN_HEADS + h, jnp.sum(dz, axis=0, keepdims=True), 0.0)
            dbd_ref[ci * C:(ci + 1) * C, :] = dbd
        small_ref[...] += jnp.concatenate([d_alog, d_dt, jnp.zeros((6, 128), F32)], axis=0)

    tok = lambda wd: pl.BlockSpec((nc * C, wd), lambda n: (n, 0))
    big = pl.BlockSpec((nc * DN_WIDTH, HD), lambda n: (n, 0))
    sq = pl.BlockSpec((DN_HEADS, nc * C, C), lambda n: (0, n, 0))
    vec = pl.BlockSpec((1, 128), lambda n: (0, 0))
    return _call(
        body, (qn, kn, v, bd, avec, dvec, t_inv, v_new_all, states, dstates, do_all, dvn_all),
        name="dn_post", grid=(N // nc,), comm=comm,
        in_specs=[tok(DN_WIDTH)] * 3 + [tok(128), vec, vec, sq, tok(DN_WIDTH), big, big, tok(DN_WIDTH), tok(DN_WIDTH)],
        out_specs=[tok(DN_WIDTH)] * 3 + [tok(128), pl.BlockSpec((8, 128), lambda n: (0, 0))],
        out_shape=[jax.ShapeDtypeStruct((S, DN_WIDTH), F32)] * 3 + [jax.ShapeDtypeStruct((S, 128), F32),
                                                                  jax.ShapeDtypeStruct((8, 128), F32)])


def _outproj_fwd(x, attn, dn, w_out):
    S, D = x.shape
    tm = 512

    def body(x_ref, a_ref, d_ref, w_ref, xo_ref, mix_ref):
        a = a_ref[...].astype(BF16)
        dd = d_ref[...].astype(BF16)
        mix_ref[:, 0:ATTN_WIDTH] = a
        mix_ref[:, ATTN_WIDTH:] = dd
        xo_ref[...] = x_ref[...] + _nn(a, w_ref[0:ATTN_WIDTH, :]) + _nn(dd, w_ref[ATTN_WIDTH:, :])

    tok = lambda w: pl.BlockSpec((tm, w), lambda i: (i, 0))
    return pl.pallas_call(
        body, name="outproj_fwd", grid=(S // tm,),
        in_specs=[tok(D), tok(ATTN_WIDTH), tok(DN_WIDTH), pl.BlockSpec((D, D), lambda i: (0, 0))],
        out_specs=[tok(D), tok(D)],
        out_shape=[jax.ShapeDtypeStruct((S, D), F32), jax.ShapeDtypeStruct((S, D), BF16)],
        compiler_params=_params(1),
    )(x, attn, dn, w_out)


def _outproj_bwd(dx, w_out, attn, comm=None):
    S, D = dx.shape
    tm = VIEW_TILE

    def body(dx_ref, w_ref, attn_ref, da1, da4, da16, dl1, dl4, dl16, ddn_ref, dxb_ref, planes):
        d = dx_ref[...].astype(BF16)
        dxb_ref[...] = d
        da = _nt(d, w_ref[0:ATTN_WIDTH, :])
        ddn_ref[...] = _nt(d, w_ref[ATTN_WIDTH:, :])
        _tile_to_views(da, planes, (da1, da4, da16))
        lo = lax.broadcasted_iota(jnp.int32, (tm, 128), 1) < 64
        cols = []
        for G in range(4):
            sl = slice(G * 128, (G + 1) * 128)
            t = da[:, sl] * attn_ref[:, sl]
            d0 = jnp.sum(jnp.where(lo, t, 0.0), axis=-1, keepdims=True)
            d1 = jnp.sum(jnp.where(lo, 0.0, t), axis=-1, keepdims=True)
            cols.append(jnp.where(lo, d0, d1))
        _tile_to_views(jnp.concatenate(cols, axis=1), planes, (dl1, dl4, dl16))

    tok = lambda w: pl.BlockSpec((tm, w), lambda i: (i, 0))
    views = [_view_spec(d) for d in DILATIONS]
    return _call(
        body, (dx, w_out, attn), name="outproj_bwd", grid=(S // tm,), comm=comm,
        in_specs=[tok(D), pl.BlockSpec((D, D), lambda i: (0, 0)), tok(ATTN_WIDTH)],
        out_specs=views + views + [tok(DN_WIDTH), tok(D)],
        out_shape=[_view_shape(S, d, F32) for d in DILATIONS] * 2
                  + [jax.ShapeDtypeStruct((S, DN_WIDTH), F32), jax.ShapeDtypeStruct((S, D), BF16)],
        scratch_shapes=[pltpu.VMEM((4, tm, 128), F32)])


def _adamw(w, g, m, v, name):
    R, Ccols = w.shape[0], w.shape[-1]
    tr = next((t for t in range(512, 7, -8) if R % t == 0), R)
    c1 = 1.0 - ADAM_B1 ** ADAM_STEP
    c2 = 1.0 - ADAM_B2 ** ADAM_STEP

    def body(w_ref, g_ref, m_ref, v_ref, d_ref, nm_ref, nv_ref):
        gv = g_ref[...]
        mn = ADAM_B1 * m_ref[...] + (1.0 - ADAM_B1) * gv
        vn = ADAM_B2 * v_ref[...] + (1.0 - ADAM_B2) * (gv * gv)
        nm_ref[...] = mn
        nv_ref[...] = vn
        d_ref[...] = -ADAM_LR * ((mn / c1) / (jnp.sqrt(vn / c2) + ADAM_EPS) + ADAM_WD * w_ref[...])

    if w.ndim == 2:
        grid, spec = (R // tr,), pl.BlockSpec((tr, Ccols), lambda i: (i, 0))
    else:
        grid, spec = (2,), pl.BlockSpec((R // 2, 1, Ccols), lambda i: (i, 0, 0))
    return pl.pallas_call(
        body, name=name, grid=grid, in_specs=[spec] * 4, out_specs=[spec] * 3,
        out_shape=[jax.ShapeDtypeStruct(w.shape, F32)] * 3, compiler_params=_params(1),
    )(w, g, m, v)


LATE_WEIGHTS = ("w_out", "ffn2_gate", "ffn2_up", "ffn2_down")


def _local_step(x, target, wts, small, dist=None):
    g1, g2, gm, gf = small["norm_ffn1"], small["norm_ffn2"], small["norm_mix"], small["norm_final"]
    wts = dict(wts)

    def reduce_start(gs, tag):
        return _rs_add_pairs(gs, _swap_sibling(gs, True, "rs_swap_halves_" + tag), dist["c"], "rs_add_pairs_" + tag)

    (x1, h1, fg1, fu1), late = _ffn_fwd(x, g1, wts["ffn1_gate"], wts["ffn1_up"], wts["ffn1_down"], "ffn1_fwd",
                                        comm=_ag_comm(dist["late"]) if dist else None)
    if dist:
        wts.update(zip(LATE_WEIGHTS, late))
        wts["w_out"] = wts["w_out"].reshape(D_MODEL, D_MODEL)
    h2, *qkv, xq, xk, xv, gate, bd = _inproj_fwd(x1, gm, wts["w_in"])
    aq, ak, av = qkv[0:3], qkv[3:6], qkv[6:9]
    parts = [_attn_fwd(aq[p], ak[p], av[p], d, f"attn_fwd_d{d}") for p, d in enumerate(DILATIONS)]
    attn, *lse = _attn_merge(parts)
    conv_w = small["conv_w"]
    qn, kn, vv = _conv_fwd(xq, xk, xv, conv_w)
    dn_u, dn_w, dn_qg, dn_kd, dn_aq, dn_t, dn_egl = _dn_prep(qn, kn, vv, bd, small["avec"], small["dvec"])
    dn, o_dn, v_new, states = _dn_scan_fwd(dn_u, dn_w, dn_qg, dn_kd, dn_aq, dn_egl, gate, small["dn_norm"])
    x2, mix = _outproj_fwd(x1, attn, dn, wts["w_out"])
    (dx3, h3, fg2, fu2, loss, d_gf), _ = _ffn_fwd(x2, g2, wts["ffn2_gate"], wts["ffn2_up"], wts["ffn2_down"], "ffn2_fwd",
                                                 head=(gf, target))

    grads = {}
    (dx2, d_g2, dfg2, dfu2, act2, dout2), _ = _ffn_bwd(dx3, x2, g2, fg2, fu2, wts["ffn2_down"], wts["ffn2_gate"],
                                                      wts["ffn2_up"], "ffn2_bwd")
    tk = 2048
    grads["ffn2_gate"], _ = _dw_chunks(dfg2, h3, tk, "dw_ffn2_gate")
    grads["ffn2_up"], _ = _dw_chunks(dfu2, h3, tk, "dw_ffn2_up")
    grads["ffn2_down"], _ = _dw_chunks(act2, dout2, tk, "dw_ffn2_down")
    group_a = ("ffn2_gate", "ffn2_up", "ffn2_down")
    gs_a = [grads[n] for n in group_a]

    (*dviews, ddn, dx2b), swapped_a = _outproj_bwd(dx2, wts["w_out"], attn, comm=_swap_comm(gs_a, True) if dist else None)
    parts_a = _rs_add_pairs(gs_a, swapped_a, dist["c"], "rs_add_pairs_a") if dist else None
    dattn, dd = dviews[0:3], dviews[3:6]
    grads["w_out"] = _matmul_tn(mix, dx2b, D_MODEL, tk, "dw_out").reshape(N_CHIPS, D_MODEL // N_CHIPS, D_MODEL)

    daq, dak, dav = [], [], []
    for p, d in enumerate(DILATIONS):
        daq.append(_attn_bwd_q(aq[p], ak[p], av[p], dattn[p], lse[p], dd[p], d, f"attn_bwd_q_d{d}"))
        dk_p, dv_p = _attn_bwd_kv(aq[p], ak[p], av[p], dattn[p], lse[p], dd[p], d, f"attn_bwd_kv_d{d}")
        dak.append(dk_p)
        dav.append(dv_p)

    do_dn, dvn, dgate, dstates, d_dn_gain = _dn_scan_bwd(dn_w, dn_qg, dn_kd, dn_aq, dn_egl, gate, small["dn_norm"], o_dn, ddn)
    (dqn, dkn, dvv, dbd, dn_small), recv_a = _dn_post(qn, kn, vv, bd, small["avec"], small["dvec"], dn_t, v_new, states,
                                                      dstates, do_dn, dvn, comm=_rsx_comm(parts_a) if dist else None)
    dcq, dck, dcv, dwq, dwk, dwv = _conv_bwd_pre(xq, xk, xv, conv_w, dqn, dkn, dvv)
    dxq, dxk, dxv = _conv_bwd_x(dcq, dck, dcv, conv_w)
    d_conv = jnp.concatenate([dwq[:CONV_WIDTH], dwk[:CONV_WIDTH], dwv[:CONV_WIDTH]], axis=1)

    dx1, d_gm, dproj = _inproj_bwd(dx2, x1, gm, [daq, dak, dav], [dxq, dxk, dxv, dgate], dbd, wts["w_in"])
    gi = _matmul_tn(dproj, h2, IN_COLS_PADDED, 512, "dw_in")
    if dist:
        gate_end = QKV_COLS + DN_WIDTH
        gi = jnp.concatenate([gi[:QKV_COLS], gi[gate_end:gate_end + LOGIT_COLS], gi[QKV_COLS:gate_end]], axis=0)
        gi = gi.reshape(N_CHIPS, IN_COLS // N_CHIPS, D_MODEL)
        gi = jnp.pad(gi, ((0, 0), (0, W_IN_ROWS - IN_COLS // N_CHIPS), (0, 0)))
    grads["w_in"] = gi
    group_b = ("w_in", "w_out")
    parts_b = reduce_start([grads[n] for n in group_b], "b") if dist else None

    (dx0, d_g1, dfg1, dfu1, act1, dout1), recv_b = _ffn_bwd(dx1, x, g1, fg1, fu1, wts["ffn1_down"], wts["ffn1_gate"],
                                                           wts["ffn1_up"], "ffn1_bwd",
                                                           comm=_rsx_comm(parts_b) if dist else None)
    group_c = ("ffn1_gate", "ffn1_up", "ffn1_down")
    pending, parts_c, recv_c = [], [], []
    for n, (lhs, rhs) in zip(group_c, ((dfg1, h1), (dfu1, h1), (act1, dout1))):
        grads[n], landed = _dw_chunks(lhs, rhs, tk, "dw_" + n, comm=_rsx_comm(pending) if pending else None)
        recv_c += list(landed)
        if dist:
            pending = reduce_start([grads[n]], n)
            parts_c += pending

    small_grads = dict(norm_ffn1=d_g1, norm_mix=d_gm, norm_ffn2=d_g2, norm_final=d_gf, conv_w=d_conv,
                       a_log=dn_small[0:1], dt_bias=dn_small[1:2], dn_norm=d_dn_gain[0:1])
    if dist:
        recv_c += _rs_exchange_arrays(pending)
        names = group_a + group_b + group_c
        totals = _rs_add_totals(list(parts_a) + list(parts_b) + list(parts_c), list(recv_a) + list(recv_b) + list(recv_c),
                                dist["chip"])
        theirs = _swap_sibling(totals, False, "rs_share_total")
        grads = {n: (mine, other) for n, mine, other in zip(names, totals, theirs)}
    return loss, dx0, grads, small_grads


HBM =pl.BlockSpec(memory_space=pl.ANY)
VMEM_SPEC = pl.BlockSpec(memory_space=pltpu.VMEM)


def _coords():
    return lax.axis_index("x"), lax.axis_index("y"), lax.axis_index("c")


def _remote(src, dst, send_sems, recv_sems, k, dev):
    return pltpu.make_async_remote_copy(src_ref=src, dst_ref=dst, send_sem=send_sems.at[k], recv_sem=recv_sems.at[k],
                                        device_id=dev, device_id_type=MESH)


def _allreduce_small(buf, name):
    R, Cc = buf.shape

    def body(src_ref, out_ref, recv_ref, send_sems, recv_sems):
        x, y, c = _coords()
        copies = []
        for m in range(1, 8):
            fx, fy, fc = (m >> 2) & 1, (m >> 1) & 1, m & 1
            dev = (x ^ fx if fx else x, y ^ fy if fy else y, c ^ fc if fc else c)
            cp = _remote(src_ref, recv_ref.at[m - 1], send_sems, recv_sems, m - 1, dev)
            cp.start()
            copies.append(cp)
        for cp in copies:
            cp.wait()
        r = [src_ref[...]] + [recv_ref[m] for m in range(7)]
        out_ref[...] = ((r[0] + r[1]) + (r[2] + r[3])) + ((r[4] + r[5]) + (r[6] + r[7]))

    return pl.pallas_call(
        body, name=name, out_shape=jax.ShapeDtypeStruct((R, Cc), F32),
        in_specs=[VMEM_SPEC], out_specs=VMEM_SPEC,
        scratch_shapes=[pltpu.VMEM((7, R, Cc), F32), pltpu.SemaphoreType.DMA((7,)), pltpu.SemaphoreType.DMA((7,))],
    )(buf)


BIG = ("ffn1_gate", "ffn1_up", "ffn1_down", "w_in", "w_out", "ffn2_gate", "ffn2_up", "ffn2_down")
ROW_SHARDED = ("ffn1_down", "w_out", "ffn2_down")
W_IN_ROWS = 960


def _rows(ref, start, size):
    return ref.at[pl.ds(pl.multiple_of(start, 16), size)]


def _allgather_arrays(shards):
    n = len(shards)

    def body(*refs):
        srcs, outs, send_sems, recv_sems = refs[:n], refs[n:2 * n], refs[2 * n], refs[2 * n + 1]
        x, y, c = _coords()
        sib = (x, y, 1 - c)
        xn, yn, dg = (1 - x, y), (x, 1 - y), (1 - x, 1 - y)
        slot = lambda out, chip: out.at[2 * chip[0] + chip[1]]
        started = []

        def go(cp):
            cp.start()
            started.append(cp)

        for a, (src, out) in enumerate(zip(srcs, outs)):
            h = src.shape[0] // 2
            cp = lambda s, d, k, dev: _remote(s, d, send_sems, recv_sems, 8 * a + k, dev)
            go(cp(src, slot(out, (x, y)), 6, sib))
            mine, dst = _rows(src, c * h, h), _rows(slot(out, (x, y)), c * h, h)
            go(cp(mine, dst, 0, (*xn, c)))
            go(cp(mine, dst, 1, (*yn, c)))
        for a, (src, out) in enumerate(zip(srcs, outs)):
            h = src.shape[0] // 2
            q = h // 2
            cp = lambda s, d, k, dev: _remote(s, d, send_sems, recv_sems, 8 * a + k, dev)
            from_x, from_y = _rows(slot(out, xn), c * h, h), _rows(slot(out, yn), c * h, h)
            cp(from_x, from_x, 0, sib).wait_recv()
            first = _rows(slot(out, xn), c * h, q)
            go(cp(first, first, 2, (*yn, c)))
            go(cp(from_x, from_x, 3, sib))
            cp(from_y, from_y, 1, sib).wait_recv()
            second = _rows(slot(out, yn), c * h + q, q)
            go(cp(second, second, 7, (*xn, c)))
            go(cp(from_y, from_y, 4, sib))
        for a, (src, out) in enumerate(zip(srcs, outs)):
            h = src.shape[0] // 2
            q = h // 2
            cp = lambda s, d, k, dev: _remote(s, d, send_sems, recv_sems, 8 * a + k, dev)
            first, second = _rows(slot(out, dg), c * h, q), _rows(slot(out, dg), c * h + q, q)
            cp(first, first, 2, sib).wait_recv()
            cp(second, second, 7, sib).wait_recv()
            from_d = _rows(slot(out, dg), c * h, h)
            go(cp(from_d, from_d, 5, sib))
        for a, (src, out) in enumerate(zip(srcs, outs)):
            h = src.shape[0] // 2
            cp = lambda s, d, k, dev: _remote(s, d, send_sems, recv_sems, 8 * a + k, dev)
            for k, chip in ((3, xn), (4, yn), (5, dg)):
                theirs = _rows(slot(out, chip), (1 - c) * h, h)
                cp(theirs, theirs, k, sib).wait_recv()
            cp(src, slot(out, (x, y)), 6, sib).wait_recv()
        for cp in started:
            cp.wait_send()

    shapes = [jax.ShapeDtypeStruct((N_CHIPS,) + s.shape, s.dtype) for s in shards]
    return pl.pallas_call(
        body, name="allgather_weights", out_shape=shapes, in_specs=[HBM] * n, out_specs=[HBM] * n,
        scratch_shapes=[pltpu.SemaphoreType.DMA((8 * n,)), pltpu.SemaphoreType.DMA((8 * n,))],
    )(*shards)


def _ag_copies(srcs, outs, send_sems, recv_sems):
    x, y, c = _coords()
    sib = (x, y, 1 - c)
    me = 2 * x + y
    others = [(1 - x, y), (x, 1 - y), (1 - x, 1 - y)]
    plan = []
    for a, (src, out) in enumerate(zip(srcs, outs)):
        h = src.shape[0] // 2
        cp = lambda s, d, k, dev: _remote(s, d, send_sems, recv_sems, 7 * a + k, dev)
        own = cp(src, out.at[me], 6, sib)
        sends = [cp(_rows(src, c * h, h), _rows(out.at[me], c * h, h), j, (ox, oy, c)) for j, (ox, oy) in enumerate(others)]
        mine = [_rows(out.at[2 * ox + oy], c * h, h) for ox, oy in others]
        theirs = [_rows(out.at[2 * ox + oy], (1 - c) * h, h) for ox, oy in others]
        arrivals = [cp(m, m, j, sib) for j, m in enumerate(mine)]
        forwards = [cp(m, m, 3 + j, sib) for j, m in enumerate(mine)]
        forwarded = [cp(t, t, 3 + j, sib) for j, t in enumerate(theirs)]
        plan.append((own, sends, forwards, arrivals, forwarded))
    return plan


def _ag_start(srcs, outs, send_sems, recv_sems):
    for own, sends, _, _, _ in _ag_copies(srcs, outs, send_sems, recv_sems):
        own.start()
        for cp in sends:
            cp.start()


def _ag_finish(srcs, outs, send_sems, recv_sems):
    plan = _ag_copies(srcs, outs, send_sems, recv_sems)
    for _, _, forwards, arrivals, _ in plan:
        for arrived, fwd in zip(arrivals, forwards):
            arrived.wait_recv()
            fwd.start()
    for own, sends, forwards, _, forwarded in plan:
        for cp in forwarded:
            cp.wait_recv()
        own.wait_recv()
        for cp in [own] + sends + forwards:
            cp.wait_send()


def _ag_comm(shards):
    shapes = [jax.ShapeDtypeStruct((N_CHIPS,) + s.shape, s.dtype) for s in shards]
    return (list(shards), shapes, 7 * len(shards), _ag_start, _ag_finish)


def _swap_sibling(arrs, pick_other_half, name):
    n = len(arrs)
    _, outs, n_sems, start, finish = _swap_comm(arrs, pick_other_half)

    def body(*refs):
        start(refs[:n], refs[n:2 * n], refs[2 * n], refs[2 * n + 1])
        finish(refs[:n], refs[n:2 * n], refs[2 * n], refs[2 * n + 1])

    return pl.pallas_call(
        body, name=name, out_shape=outs, in_specs=[HBM] * n, out_specs=[HBM] * n,
        scratch_shapes=[pltpu.SemaphoreType.DMA((n_sems,)), pltpu.SemaphoreType.DMA((n_sems,))],
    )(*arrs)


def _swap_comm(arrs, pick_other_half):
    def copies(srcs, dsts, send_sems, recv_sems):
        x, y, c = _coords()
        cps = []
        for a, (src, dst) in enumerate(zip(srcs, dsts)):
            if pick_other_half:
                h = src.shape[1] // 2
                src = src.at[:, pl.ds(pl.multiple_of((1 - c) * h, 16), h)]
            cps.append(_remote(src, dst, send_sems, recv_sems, a, (x, y, 1 - c)))
        return cps

    def start(*refs):
        for cp in copies(*refs):
            cp.start()

    def finish(*refs):
        for cp in copies(*refs):
            cp.wait()

    shapes = [jax.ShapeDtypeStruct((a.shape[0], a.shape[1] // 2) + a.shape[2:] if pick_other_half else a.shape, a.dtype)
              for a in arrs]
    return (list(arrs), shapes, len(arrs), start, finish)


def _rs_add_pairs(gs, others, c, name):
    n = len(gs)
    blocks = [(g.shape[1] // 4, g.shape[2]) for g in gs]

    def body(c_ref, *refs):
        for a in range(n):
            refs[2 * n + a][...] = (refs[a][...] + refs[n + a][...]).astype(BF16)

    mine = lambda b: pl.BlockSpec((None,) + b, lambda j, s, c_ref: (j, c_ref[0] * 2 + s, 0))
    flat = lambda b: pl.BlockSpec((None,) + b, lambda j, s, c_ref: (j, s, 0))
    return pl.pallas_call(
        body, name=name,
        grid_spec=pltpu.PrefetchScalarGridSpec(
            num_scalar_prefetch=1, grid=(N_CHIPS, 2),
            in_specs=[mine(b) for b in blocks] + [flat(b) for b in blocks],
            out_specs=[flat(b) for b in blocks]),
        out_shape=[jax.ShapeDtypeStruct(o.shape, BF16) for o in others],
        compiler_params=_params(2),
    )(c, *gs, *others)


def _rs_exchange_arrays(parts):
    n = len(parts)

    def body(*refs):
        _rsx_start(refs[:n], refs[n:2 * n], refs[2 * n], refs[2 * n + 1])
        _rsx_finish(refs[:n], refs[n:2 * n], refs[2 * n], refs[2 * n + 1])

    _, shapes, n_sems, _, _ = _rsx_comm(parts)
    return pl.pallas_call(
        body, name="rs_exchange_chips", out_shape=shapes, in_specs=[HBM] * n, out_specs=[HBM] * n,
        scratch_shapes=[pltpu.SemaphoreType.DMA((n_sems,)), pltpu.SemaphoreType.DMA((n_sems,))],
    )(*parts)


def _rsx_copies(srcs, dsts, send_sems, recv_sems):
    x, y, c = _coords()
    others = [(1 - x, y), (x, 1 - y), (1 - x, 1 - y)]
    return [_remote(src.at[2 * ox + oy], dst.at[k], send_sems, recv_sems, 3 * a + k, (ox, oy, c))
            for a, (src, dst) in enumerate(zip(srcs, dsts)) for k, (ox, oy) in enumerate(others)]


def _rsx_start(srcs, dsts, send_sems, recv_sems):
    for cp in _rsx_copies(srcs, dsts, send_sems, recv_sems):
        cp.start()


def _rsx_finish(srcs, dsts, send_sems, recv_sems):
    for cp in _rsx_copies(srcs, dsts, send_sems, recv_sems):
        cp.wait()


def _rsx_comm(parts):
    shapes = [jax.ShapeDtypeStruct((3,) + p.shape[1:], p.dtype) for p in parts]
    return (list(parts), shapes, 3 * len(parts), _rsx_start, _rsx_finish)


def _rs_add_totals(parts, recvs, chip):
    n = len(parts)
    blocks = [(p.shape[1] // 2, p.shape[2]) for p in parts]

    def body(chip_ref, *refs):
        f = lambda r: r[...].astype(F32)
        for a in range(n):
            p, r0, r1, r2 = refs[a], refs[n + 3 * a], refs[n + 3 * a + 1], refs[n + 3 * a + 2]
            refs[4 * n + a][...] = (f(p) + f(r0)) + (f(r1) + f(r2))

    own = lambda b: pl.BlockSpec((None,) + b, lambda s, chip_ref: (chip_ref[0], s, 0))
    slot = lambda b, k: pl.BlockSpec((None,) + b, lambda s, chip_ref, k=k: (k, s, 0))
    recv_specs = [slot(b, k) for b in blocks for k in range(3)]
    recv_args = [r for r in recvs for _ in range(3)]
    return pl.pallas_call(
        body, name="rs_add_totals",
        grid_spec=pltpu.PrefetchScalarGridSpec(
            num_scalar_prefetch=1, grid=(2,),
            in_specs=[own(b) for b in blocks] + recv_specs,
            out_specs=[pl.BlockSpec(b, lambda s, chip_ref: (s, 0)) for b in blocks]),
        out_shape=[jax.ShapeDtypeStruct(p.shape[1:], F32) for p in parts],
        compiler_params=_params(1),
    )(chip, *parts, *recv_args)


def _permute_w_in(wt):
    return jnp.concatenate([wt[:QKV_COLS], wt[QKV_COLS + LOGIT_COLS:IN_COLS], wt[QKV_COLS:QKV_COLS + LOGIT_COLS],
                            jnp.zeros((IN_COLS_PADDED - IN_COLS, wt.shape[1]), wt.dtype)], axis=0)


def _pad_row(v):
    v = v.reshape(1, -1)
    return jnp.pad(v, ((0, 0), (0, D_MODEL - v.shape[1])))


def kernel(x, norm_ffn1, ffn1_gate, ffn1_up, ffn1_down, norm_mix, w_in, conv_w, a_log, dt_bias, dn_norm, w_out, norm_ffn2, ffn2_gate, ffn2_up, ffn2_down, norm_final, loss_target, m_norm_ffn1, m_ffn1_gate, m_ffn1_up, m_ffn1_down, m_norm_mix, m_w_in, m_conv_w, m_a_log, m_dt_bias, m_dn_norm, m_w_out, m_norm_ffn2, m_ffn2_gate, m_ffn2_up, m_ffn2_down, m_norm_final, v_norm_ffn1, v_ffn1_gate, v_ffn1_up, v_ffn1_down, v_norm_mix, v_w_in, v_conv_w, v_a_log, v_dt_bias, v_dn_norm, v_w_out, v_norm_ffn2, v_ffn2_gate, v_ffn2_up, v_ffn2_down, v_norm_final):
    cx, cy, cc = _coords()
    chip = 2 * cx + cy
    stored = lambda t, n: t[0] if n in ROW_SHARDED else t[0].T
    big_w = {n: stored(t, n) for n, t in dict(
        ffn1_gate=ffn1_gate, ffn1_up=ffn1_up, ffn1_down=ffn1_down, w_in=w_in, w_out=w_out,
        ffn2_gate=ffn2_gate, ffn2_up=ffn2_up, ffn2_down=ffn2_down).items()}
    big_m = {n: stored(t, n) for n, t in dict(
        ffn1_gate=m_ffn1_gate, ffn1_up=m_ffn1_up, ffn1_down=m_ffn1_down, w_in=m_w_in, w_out=m_w_out,
        ffn2_gate=m_ffn2_gate, ffn2_up=m_ffn2_up, ffn2_down=m_ffn2_down).items()}
    big_v = {n: stored(t, n) for n, t in dict(
        ffn1_gate=v_ffn1_gate, ffn1_up=v_ffn1_up, ffn1_down=v_ffn1_down, w_in=v_w_in, w_out=v_w_out,
        ffn2_gate=v_ffn2_gate, ffn2_up=v_ffn2_up, ffn2_down=v_ffn2_down).items()}

    cols = IN_COLS // N_CHIPS
    send = {n: big_w[n].astype(BF16) for n in BIG}
    send["w_in"] = jnp.pad(send["w_in"], ((0, W_IN_ROWS - cols), (0, 0)))
    early = tuple(n for n in BIG if n not in LATE_WEIGHTS)
    wts = dict(zip(early, _allgather_arrays([send[n] for n in early])))
    wts["w_in"] = _permute_w_in(wts["w_in"][:, :cols].reshape(IN_COLS, D_MODEL))
    dist = dict(late=[send[n] for n in LATE_WEIGHTS], c=cc.reshape(1).astype(jnp.int32),
                chip=chip.reshape(1).astype(jnp.int32))

    conv_shard = conv_w[0]
    emb = jnp.concatenate([jnp.where((chip == j) & (cc == 0), conv_shard, 0.0) for j in range(N_CHIPS)], axis=1)
    emb = jnp.pad(emb.reshape(6, D_MODEL), ((0, 2), (0, 0)))
    conv_full = _allreduce_small(emb, "allgather_conv_w")[:6].reshape(CONV_WIDTH, 3 * DN_WIDTH)

    zvec = jnp.zeros((1, 128), F32)
    small = dict(norm_ffn1=norm_ffn1, norm_mix=norm_mix, norm_ffn2=norm_ffn2, norm_final=norm_final[None],
                 conv_w=conv_full, avec=zvec.at[0, DN_HEADS:2 * DN_HEADS].set(a_log[0]),
                 dvec=zvec.at[0, DN_HEADS:2 * DN_HEADS].set(dt_bias[0]), dn_norm=dn_norm)

    loss, grad_x, reduced, sg = _local_step(x[0], loss_target[0], wts, small, dist)

    rows = [sg["norm_ffn1"], sg["norm_mix"], sg["norm_ffn2"], sg["norm_final"], _pad_row(sg["a_log"]), _pad_row(sg["dt_bias"]),
            _pad_row(sg["dn_norm"]), _pad_row(loss[0:1]), sg["conv_w"].reshape(6, D_MODEL), jnp.zeros((2, D_MODEL), F32)]
    red = _allreduce_small(jnp.concatenate(rows, axis=0), "allreduce_small")
    loss_out = red[7, 0]
    g_conv_full = red[8:14].reshape(CONV_WIDTH, 3 * DN_WIDTH)
    g_conv = lax.dynamic_slice_in_dim(g_conv_full, chip * (3 * DN_WIDTH // N_CHIPS), 3 * DN_WIDTH // N_CHIPS, axis=1)
    g_small = dict(norm_ffn1=red[0:1], norm_mix=red[1:2], norm_ffn2=red[2:3], norm_final=red[3],
                   a_log=red[4:5, DN_HEADS:2 * DN_HEADS], dt_bias=red[5:6, DN_HEADS:2 * DN_HEADS], dn_norm=red[6:7, :DN_HEAD_DIM])

    out_g, out_d, out_m, out_v = {}, {}, {}, {}
    for n in BIG:
        mine, other = reduced[n]
        g = jnp.where(cc == 0, jnp.concatenate([mine, other], axis=0), jnp.concatenate([other, mine], axis=0))
        if n == "w_in":
            to3 = lambda t: jnp.transpose(t, (2, 0, 1))
            g = g[:cols].reshape(cols, 1, D_MODEL)
            results = (g,) + tuple(_adamw(to3(w_in), g, to3(m_w_in), to3(v_w_in), "adamw_w_in"))
            out_g[n], out_d[n], out_m[n], out_v[n] = (jnp.transpose(t, (1, 2, 0)) for t in results)
            continue
        results = (g,) + tuple(_adamw(big_w[n], g, big_m[n], big_v[n], "adamw_" + n))
        out_g[n], out_d[n], out_m[n], out_v[n] = ((t if n in ROW_SHARDED else t.T)[None] for t in results)
    d, nm, nv = _adamw(conv_w[0], g_conv, m_conv_w[0], v_conv_w[0], "adamw_conv_w")
    out_g["conv_w"], out_d["conv_w"], out_m["conv_w"], out_v["conv_w"] = g_conv[None], d[None], nm[None], nv[None]

    small_names = ("norm_ffn1", "norm_mix", "norm_ffn2", "norm_final", "a_log", "dt_bias", "dn_norm")
    small_w = dict(norm_ffn1=norm_ffn1, norm_mix=norm_mix, norm_ffn2=norm_ffn2, norm_final=norm_final, a_log=a_log,
                   dt_bias=dt_bias, dn_norm=dn_norm)
    small_m = dict(norm_ffn1=m_norm_ffn1, norm_mix=m_norm_mix, norm_ffn2=m_norm_ffn2, norm_final=m_norm_final, a_log=m_a_log,
                   dt_bias=m_dt_bias, dn_norm=m_dn_norm)
    small_v = dict(norm_ffn1=v_norm_ffn1, norm_mix=v_norm_mix, norm_ffn2=v_norm_ffn2, norm_final=v_norm_final, a_log=v_a_log,
                   dt_bias=v_dt_bias, dn_norm=v_dn_norm)
    stack = lambda dct: jnp.concatenate([_pad_row(dct[n]) for n in small_names] + [jnp.zeros((1, D_MODEL), F32)], axis=0)
    d, nm, nv = _adamw(stack(small_w), stack(g_small), stack(small_m), stack(small_v), "adamw_small")
    for k, n in enumerate(small_names):
        shape = small_w[n].shape
        size = math.prod(shape)
        out_g[n] = g_small[n].reshape(shape)
        out_d[n], out_m[n], out_v[n] = (t[k, :size].reshape(shape) for t in (d, nm, nv))

    order = ("norm_ffn1", "ffn1_gate", "ffn1_up", "ffn1_down", "norm_mix", "w_in", "conv_w", "a_log", "dt_bias", "dn_norm",
             "w_out", "norm_ffn2", "ffn2_gate", "ffn2_up", "ffn2_down", "norm_final")
    return (loss_out, grad_x[None], *[out_g[n] for n in order], *[out_d[n] for n in order],
            *[out_m[n] for n in order], *[out_v[n] for n in order])
```

```python
import functools
import math

import jax
import jax.numpy as jnp
from jax import lax
from jax.experimental import pallas as pl
from jax.experimental.pallas import tpu as pltpu

F32 = jnp.float32
BF16 = jnp.bfloat16
HI = lax.Precision.HIGH

D_MODEL = 1024
ATTN_HEADS = 8
ATTN_WIDTH = 512
ATTN_BLOCK = 128
ATTN_SCALE = (ATTN_WIDTH // ATTN_HEADS) ** -0.5
DILATIONS = (1, 4, 16)
DN_HEADS = 4
DN_HEAD_DIM = 128
DN_WIDTH = 512
DN_CHUNK = 64
CONV_WIDTH = 4
NORM_EPS = 1e-6
L2_EPS = 1e-6
QKV_COLS = 3 * ATTN_WIDTH + 3 * DN_WIDTH
LOGIT_COLS = 2 * DN_HEADS
IN_COLS = QKV_COLS + LOGIT_COLS + DN_WIDTH
IN_COLS_PADDED = 3712
N_CHIPS = 4

ADAM_LR = 0.001
ADAM_B1 = 0.9
ADAM_B2 = 0.999
ADAM_EPS = 1e-08
ADAM_WD = 0.01
ADAM_STEP = 10

VMEM_LIMIT = 56 * 1024 * 1024
NEG_BIG = -1e30
MESH = pl.DeviceIdType.MESH


def _params(n_grid, vmem=VMEM_LIMIT):
    return pltpu.CompilerParams(dimension_semantics=("arbitrary",) * n_grid, vmem_limit_bytes=vmem)


def _call(body, args, *, name, grid, in_specs, out_specs, out_shape, scratch_shapes=(), comm=None):
    n_in, n_out, n_scr = len(in_specs), len(out_specs), len(scratch_shapes)
    hbm = pl.BlockSpec(memory_space=pl.ANY)
    srcs, dst_shapes, n_sems, start, finish = comm[:5] if comm is not None else ((), (), 0, None, None)
    middle = comm[5] if comm is not None and len(comm) > 5 else None
    ns, nd = len(srcs), len(dst_shapes)

    def full(*refs):
        ins, c_src = refs[:n_in], refs[n_in:n_in + ns]
        at = n_in + ns
        outs, c_dst = refs[at:at + n_out], refs[at + n_out:at + n_out + nd]
        scr = refs[at + n_out + nd:at + n_out + nd + n_scr]
        if comm is not None:
            ids = [pl.program_id(a) for a in range(len(grid))]
            first = functools.reduce(jnp.logical_and, [i == 0 for i in ids])
            last = functools.reduce(jnp.logical_and, [i == g - 1 for i, g in zip(ids, grid)])

            @pl.when(first)
            def _():
                start(c_src, c_dst, refs[-2], refs[-1])

            if middle is not None:
                half_way = functools.reduce(jnp.logical_and, [ids[0] == grid[0] // 2] + [i == 0 for i in ids[1:]])

                @pl.when(half_way)
                def _():
                    middle(c_src, c_dst, refs[-2], refs[-1])

        body(*ins, *outs, *scr)
        if comm is not None:
            @pl.when(last)
            def _():
                finish(c_src, c_dst, refs[-2], refs[-1])

    sems = [pltpu.SemaphoreType.DMA((n_sems,)), pltpu.SemaphoreType.DMA((n_sems,))] if comm is not None else []
    res = pl.pallas_call(
        full, name=name, grid=grid, in_specs=list(in_specs) + [hbm] * ns, out_specs=list(out_specs) + [hbm] * nd,
        out_shape=list(out_shape) + list(dst_shapes), scratch_shapes=list(scratch_shapes) + sems,
        compiler_params=_params(len(grid)),
    )(*args, *srcs)
    return res[:n_out], res[n_out:]


def _nt(a, b, precision=None):
    return lax.dot_general(a, b, (((1,), (1,)), ((), ())), preferred_element_type=F32, precision=precision)


def _tn(a, b, precision=None):
    return lax.dot_general(a, b, (((0,), (0,)), ((), ())), preferred_element_type=F32, precision=precision)


def _nn(a, b, precision=None):
    return jnp.dot(a, b, preferred_element_type=F32, precision=precision)


def _sigmoid(x):
    return 1.0 / (1.0 + jnp.exp(-x))


def _loss_head(xf, gain, target):
    r = lax.rsqrt(jnp.mean(xf * xf, axis=-1, keepdims=True) + NORM_EPS)
    xhat = xf * r
    err = xhat * gain - target
    part = 0.5 * jnp.sum(jnp.mean(err * err, axis=-1, keepdims=True), axis=0, keepdims=True)
    dy = err * (1.0 / xf.shape[-1])
    dgain = jnp.sum(dy * xhat, axis=0, keepdims=True)
    dxh = dy * gain
    return part, r * (dxh - xhat * jnp.mean(dxh * xhat, axis=-1, keepdims=True)), dgain


def _ffn_fwd(x, gain, wg, wu, wd, name, comm=None, head=None):
    S, D = x.shape
    nf, tf, _ = wg.shape
    tm = 512
    n_in = 5 if head is None else 7

    def body(*refs):
        x_ref, gain_ref, wg_ref, wu_ref, wd_ref = refs[:5]
        xo_ref, h_ref, g_ref, u_ref = refs[n_in:n_in + 4]
        acc_ref, hs_ref = refs[-2:]
        i = pl.program_id(0)
        j = pl.program_id(1)

        @pl.when(j == 0)
        def _():
            xf = x_ref[...]
            r = lax.rsqrt(jnp.mean(xf * xf, axis=-1, keepdims=True) + NORM_EPS)
            h = (xf * r * gain_ref[...]).astype(BF16)
            hs_ref[...] = h
            h_ref[...] = h
            acc_ref[...] = jnp.zeros_like(acc_ref)

        h = hs_ref[...]
        g = _nt(h, wg_ref[...])
        u = _nt(h, wu_ref[...])
        g_ref[...] = g.astype(BF16)
        u_ref[...] = u.astype(BF16)
        act = g * _sigmoid(g) * u
        acc_ref[...] += _nn(act.astype(BF16), wd_ref[...])

        if head is not None:
            hgain_ref, t_ref = refs[5:7]
            loss_ref, dgain_ref = refs[n_in + 4:n_in + 6]

            @pl.when((i == 0) & (j == 0))
            def _():
                loss_ref[...] = jnp.zeros_like(loss_ref)
                dgain_ref[...] = jnp.zeros_like(dgain_ref)

        @pl.when(j == nf - 1)
        def _():
            xo = x_ref[...] + 0.5 * acc_ref[...]
            if head is None:
                xo_ref[...] = xo
            else:
                part, dxo, dgain = _loss_head(xo, hgain_ref[...], t_ref[...])
                first = ((lax.broadcasted_iota(jnp.int32, (8, 128), 0) == 0)
                         & (lax.broadcasted_iota(jnp.int32, (8, 128), 1) == 0))
                loss_ref[...] += jnp.where(first, part, 0.0)
                dgain_ref[...] += dgain
                xo_ref[...] = dxo

    tok = pl.BlockSpec((tm, D), lambda i, j: (i, 0))
    row = pl.BlockSpec((1, D), lambda i, j: (0, 0))
    chunk = pl.BlockSpec((None, tf, D), lambda i, j: (j, 0, 0))
    act = pl.BlockSpec((None, tm, tf), lambda i, j: (j, i, 0))
    extra_in = [] if head is None else [row, tok]
    extra_out = [] if head is None else [pl.BlockSpec((8, 128), lambda i, j: (0, 0)), row]
    extra_shape = [] if head is None else [jax.ShapeDtypeStruct((8, 128), F32), jax.ShapeDtypeStruct((1, D), F32)]
    return _call(
        body, (x, gain, wg, wu, wd) + (() if head is None else tuple(head)), name=name, grid=(S // tm, nf), comm=comm,
        in_specs=[tok, row, chunk, chunk, chunk] + extra_in,
        out_specs=[tok, tok, act, act] + extra_out,
        out_shape=[jax.ShapeDtypeStruct((S, D), F32), jax.ShapeDtypeStruct((S, D), BF16),
                   jax.ShapeDtypeStruct((nf, S, tf), BF16), jax.ShapeDtypeStruct((nf, S, tf), BF16)] + extra_shape,
        scratch_shapes=[pltpu.VMEM((tm, D), F32), pltpu.VMEM((tm, D), BF16)])


def _rmsnorm_bwd(dh, xf, gain):
    r = lax.rsqrt(jnp.mean(xf * xf, axis=-1, keepdims=True) + NORM_EPS)
    xhat = xf * r
    dgain = jnp.sum(dh * xhat, axis=0, keepdims=True)
    dxh = dh * gain
    dx = r * (dxh - xhat * jnp.mean(dxh * xhat, axis=-1, keepdims=True))
    return dx, dgain


def _ffn_bwd(dxo, x, gain, g, u, wd, wg, wu, name, comm=None):
    S, D = x.shape
    nf, _, tf = g.shape
    tm = 512

    def body(dxo_ref, x_ref, gain_ref, g_ref, u_ref, wd_ref, wg_ref, wu_ref,
             dx_ref, dgain_ref, dg_ref, du_ref, act_ref, dout_ref, acc_ref, ds_ref):
        i = pl.program_id(0)
        j = pl.program_id(1)

        @pl.when(j == 0)
        def _():
            d = (0.5 * dxo_ref[...]).astype(BF16)
            ds_ref[...] = d
            dout_ref[...] = d
            acc_ref[...] = jnp.zeros_like(acc_ref)

        @pl.when((i == 0) & (j == 0))
        def _():
            dgain_ref[...] = jnp.zeros_like(dgain_ref)

        for half in range(2):
            rows = slice(half * (tm // 2), (half + 1) * (tm // 2))
            dact = _nt(ds_ref[rows, :], wd_ref[...])
            gv = g_ref[rows, :].astype(F32)
            uv = u_ref[rows, :].astype(F32)
            sg = _sigmoid(gv)
            silu = gv * sg
            act_ref[rows, :] = (silu * uv).astype(BF16)
            dgv = (dact * uv * (sg * (1.0 + gv * (1.0 - sg)))).astype(BF16)
            duv = (dact * silu).astype(BF16)
            dg_ref[rows, :] = dgv
            du_ref[rows, :] = duv
            acc_ref[rows, :] += _nn(dgv, wg_ref[...]) + _nn(duv, wu_ref[...])

        @pl.when(j == nf - 1)
        def _():
            dx, dgain = _rmsnorm_bwd(acc_ref[...], x_ref[...], gain_ref[...])
            dx_ref[...] = dxo_ref[...] + dx
            dgain_ref[...] += dgain

    return _call(
        body, (dxo, x, gain, g, u, wd, wg, wu), name=name, grid=(S // tm, nf), comm=comm,
        in_specs=[pl.BlockSpec((tm, D), lambda i, j: (i, 0)),
                  pl.BlockSpec((tm, D), lambda i, j: (i, 0)),
                  pl.BlockSpec((1, D), lambda i, j: (0, 0)),
                  pl.BlockSpec((None, tm, tf), lambda i, j: (j, i, 0)),
                  pl.BlockSpec((None, tm, tf), lambda i, j: (j, i, 0)),
                  pl.BlockSpec((None, tf, D), lambda i, j: (j, 0, 0)),
                  pl.BlockSpec((None, tf, D), lambda i, j: (j, 0, 0)),
                  pl.BlockSpec((None, tf, D), lambda i, j: (j, 0, 0))],
        out_specs=[pl.BlockSpec((tm, D), lambda i, j: (i, 0)),
                   pl.BlockSpec((1, D), lambda i, j: (0, 0)),
                   pl.BlockSpec((None, tm, tf), lambda i, j: (j, i, 0)),
                   pl.BlockSpec((None, tm, tf), lambda i, j: (j, i, 0)),
                   pl.BlockSpec((None, tm, tf), lambda i, j: (j, i, 0)),
                   pl.BlockSpec((tm, D), lambda i, j: (i, 0))],
        out_shape=[jax.ShapeDtypeStruct((S, D), F32), jax.ShapeDtypeStruct((1, D), F32),
                   jax.ShapeDtypeStruct((nf, S, tf), BF16), jax.ShapeDtypeStruct((nf, S, tf), BF16),
                   jax.ShapeDtypeStruct((nf, S, tf), BF16), jax.ShapeDtypeStruct((S, D), BF16)],
        scratch_shapes=[pltpu.VMEM((tm, D), F32), pltpu.VMEM((tm, D), BF16)])


def _matmul_tn(a, b, tm, tk, name):
    K, M = a.shape
    N = b.shape[1]

    def body(a_ref, b_ref, o_ref):
        @pl.when(pl.program_id(1) == 0)
        def _():
            o_ref[...] = jnp.zeros_like(o_ref)

        o_ref[...] += _tn(a_ref[...], b_ref[...])

    return pl.pallas_call(
        body, name=name, grid=(M // tm, K // tk),
        in_specs=[pl.BlockSpec((tk, tm), lambda i, k: (k, i)),
                  pl.BlockSpec((tk, N), lambda i, k: (k, 0))],
        out_specs=pl.BlockSpec((tm, N), lambda i, k: (i, 0)),
        out_shape=jax.ShapeDtypeStruct((M, N), F32),
        compiler_params=_params(2),
    )(a, b)


def _dw_chunks(a, b, tk, name, comm=None):
    nf, S, tf = a.shape
    N = b.shape[1]

    def body(a_ref, b_ref, o_ref):
        @pl.when(pl.program_id(1) == 0)
        def _():
            o_ref[...] = jnp.zeros_like(o_ref)

        o_ref[...] += _tn(a_ref[...], b_ref[...])

    (out,), landed = _call(
        body, (a, b), name=name, grid=(nf, S // tk), comm=comm,
        in_specs=[pl.BlockSpec((None, tk, tf), lambda j, k: (j, k, 0)),
                  pl.BlockSpec((tk, N), lambda j, k: (k, 0))],
        out_specs=[pl.BlockSpec((None, tf, N), lambda j, k: (j, 0, 0))],
        out_shape=[jax.ShapeDtypeStruct((nf, tf, N), F32)])
    return out, landed


VIEW_TILE = 512


def _view_spec(d, tile=VIEW_TILE):
    return pl.BlockSpec((tile // d, d * ATTN_WIDTH), lambda i: (i, 0))


def _view_shape(S, d, dtype):
    return jax.ShapeDtypeStruct((S // d, d * ATTN_WIDTH), dtype)


def _tile_to_views(val, planes, out_refs):
    for g in range(4):
        planes[g] = val[:, g * 128:(g + 1) * 128]
    for d, ref in zip(DILATIONS, out_refs):
        if d == 1:
            ref[...] = val.astype(ref.dtype)
            continue
        for r in range(d):
            for g in range(4):
                ref[:, r * ATTN_WIDTH + g * 128:r * ATTN_WIDTH + (g + 1) * 128] = (
                    planes[g, pl.ds(r, planes.shape[1] // d, stride=d), :].astype(ref.dtype))


def _view_to_tile(ref, d, planes):
    if d == 1:
        return ref[...].astype(F32)
    for r in range(d):
        for g in range(4):
            planes[g, pl.ds(r, planes.shape[1] // d, stride=d), :] = (
                ref[:, r * ATTN_WIDTH + g * 128:r * ATTN_WIDTH + (g + 1) * 128].astype(F32))
    return jnp.concatenate([planes[g] for g in range(4)], axis=1)


def _inproj_fwd(x, gain, w_in_p):
    S, D = x.shape
    tm = VIEW_TILE
    W = ATTN_WIDTH

    def body(x_ref, gain_ref, w_ref, h_ref, q1, q4, q16, k1, k4, k16, v1, v4, v16, dq_ref, dk_ref, dv_ref, gate_ref, bd_ref,
             planes):
        xf = x_ref[...]
        r = lax.rsqrt(jnp.mean(xf * xf, axis=-1, keepdims=True) + NORM_EPS)
        h = (xf * r * gain_ref[...]).astype(BF16)
        h_ref[...] = h
        _tile_to_views(_nt(h, w_ref[0:W, :]) * ATTN_SCALE, planes, (q1, q4, q16))
        _tile_to_views(_nt(h, w_ref[W:2 * W, :]), planes, (k1, k4, k16))
        _tile_to_views(_nt(h, w_ref[2 * W:3 * W, :]), planes, (v1, v4, v16))
        dq_ref[...] = _nt(h, w_ref[3 * W:4 * W, :])
        dk_ref[...] = _nt(h, w_ref[4 * W:5 * W, :])
        dv_ref[...] = _nt(h, w_ref[5 * W:6 * W, :])
        gate_ref[...] = _nt(h, w_ref[6 * W:7 * W, :])
        bd_ref[...] = _nt(h, w_ref[7 * W:7 * W + 128, :])

    tok = lambda w: pl.BlockSpec((tm, w), lambda i: (i, 0))
    return pl.pallas_call(
        body, name="inproj_fwd", grid=(S // tm,),
        in_specs=[tok(D), pl.BlockSpec((1, D), lambda i: (0, 0)),
                  pl.BlockSpec((IN_COLS_PADDED, D), lambda i: (0, 0))],
        out_specs=[tok(D)] + [_view_spec(d) for d in DILATIONS] * 3 + [tok(W)] * 4 + [tok(128)],
        out_shape=[jax.ShapeDtypeStruct((S, D), BF16)] + [_view_shape(S, d, BF16) for d in DILATIONS] * 3
                  + [jax.ShapeDtypeStruct((S, W), F32)] * 4 + [jax.ShapeDtypeStruct((S, 128), F32)],
        scratch_shapes=[pltpu.VMEM((4, tm, 128), F32)],
        compiler_params=_params(1),
    )(x, gain, w_in_p)


def _inproj_bwd(dxo, x, gain, attn_grads, dsecs, dbd, w_in_p):
    S, D = x.shape
    tm = VIEW_TILE
    W = ATTN_WIDTH

    def body(dxo_ref, x_ref, gain_ref, *rest):
        views, (s3, s4, s5, s6, dbd_ref, w_ref, dx_ref, dgain_ref, dproj_ref, planes) = rest[:9], rest[9:]

        @pl.when(pl.program_id(0) == 0)
        def _():
            dgain_ref[...] = jnp.zeros_like(dgain_ref)

        secs = []
        for k in range(3):
            parts = [_view_to_tile(views[3 * k + p], d, planes) for p, d in enumerate(DILATIONS)]
            secs.append(parts[0] + parts[1] + parts[2])
        secs += [s3[...], s4[...], s5[...], s6[...]]
        dh = jnp.zeros((tm, D), F32)
        for k, s in enumerate(secs):
            d = s.astype(BF16)
            dproj_ref[:, k * W:(k + 1) * W] = d
            dh += _nn(d, w_ref[k * W:(k + 1) * W, :])
        d = dbd_ref[...].astype(BF16)
        dproj_ref[:, 7 * W:7 * W + 128] = d
        dh += _nn(d, w_ref[7 * W:7 * W + 128, :])
        dx, dgain = _rmsnorm_bwd(dh, x_ref[...], gain_ref[...])
        dx_ref[...] = dxo_ref[...] + dx
        dgain_ref[...] += dgain

    tok = lambda w: pl.BlockSpec((tm, w), lambda i: (i, 0))
    return pl.pallas_call(
        body, name="inproj_bwd", grid=(S // tm,),
        in_specs=[tok(D), tok(D), pl.BlockSpec((1, D), lambda i: (0, 0))] + [_view_spec(d, tm) for d in DILATIONS] * 3
                 + [tok(W)] * 4 + [tok(128)] + [pl.BlockSpec((IN_COLS_PADDED, D), lambda i: (0, 0))],
        out_specs=[tok(D), pl.BlockSpec((1, D), lambda i: (0, 0)), tok(IN_COLS_PADDED)],
        out_shape=[jax.ShapeDtypeStruct((S, D), F32), jax.ShapeDtypeStruct((1, D), F32),
                   jax.ShapeDtypeStruct((S, IN_COLS_PADDED), BF16)],
        scratch_shapes=[pltpu.VMEM((4, tm, 128), F32)],
        compiler_params=_params(1),
    )(dxo, x, gain, *[g for grads in attn_grads for g in grads], *dsecs, dbd, w_in_p)


def _slope(h):
    return 2.0 ** (-8.0 * (h + 1) / ATTN_HEADS)


def _head_bias(steps, d, heads=tuple(range(ATTN_HEADS))):
    stepsf = steps.astype(F32)
    return jnp.stack([stepsf * (-_slope(h) * d) for h in heads])


def _hnt(a, b):
    return lax.dot_general(a, b, (((2,), (2,)), ((0,), (0,))), preferred_element_type=F32)


def _hnn(a, b):
    return lax.dot_general(a, b, (((2,), (1,)), ((0,), (0,))), preferred_element_type=F32)


def _blocks_per_step(nb):
    return next(n for n in (4, 2, 1) if nb % n == 0)


def _query_step_specs(qb):
    B = ATTN_BLOCK
    cur = pl.BlockSpec((qb * B, ATTN_WIDTH), lambda r, n: (n, r))
    prev = pl.BlockSpec((B, ATTN_WIDTH), lambda r, n: (jnp.maximum(qb * n - 1, 0), r))
    return cur, prev


def _prev_block(prev_ref, cur_ref, sub, sl):
    B = ATTN_BLOCK
    return prev_ref[:, sl] if sub == 0 else cur_ref[(sub - 1) * B:sub * B, sl]


def _head_cols(tile, lo, big):
    return [_head_col(tile, lo, big), _head_col(tile, jnp.logical_not(lo), big)]


def _attn_fwd(q, k, v, d, name):
    L = q.shape[0]
    nb = L // ATTN_BLOCK
    B = ATTN_BLOCK
    QB = _blocks_per_step(nb)

    def body(q_ref, kp_ref, kc_ref, vp_ref, vc_ref, o_ref, lse_ref):
        n = pl.program_id(1)
        qi = lax.broadcasted_iota(jnp.int32, (B, 2 * B), 0)
        kj = lax.broadcasted_iota(jnp.int32, (B, 2 * B), 1)
        steps = qi + B - kj
        band = (steps >= 0) & (steps <= B)
        lo = lax.broadcasted_iota(jnp.int32, (B, 128), 1) < 64
        bias = _head_bias(steps, d)
        for sub in range(QB):
            rows = slice(sub * B, (sub + 1) * B)
            valid = band & ((kj >= B) | (n > 0)) if sub == 0 else band
            qs, ks, vs = [], [], []
            for G in range(4):
                sl = slice(G * 128, (G + 1) * 128)
                qg = q_ref[rows, sl]
                kg = jnp.concatenate([_prev_block(kp_ref, kc_ref, sub, sl), kc_ref[rows, sl]], axis=0)
                vg = jnp.concatenate([_prev_block(vp_ref, vc_ref, sub, sl), vc_ref[rows, sl]], axis=0)
                qs += [jnp.where(lo, qg, jnp.zeros_like(qg)), jnp.where(lo, jnp.zeros_like(qg), qg)]
                ks += [kg, kg]
                vs += [vg, vg]
            s = jnp.where(valid, _hnt(jnp.stack(qs), jnp.stack(ks)) + bias, NEG_BIG)
            m = jnp.max(s, axis=-1, keepdims=True)
            p = jnp.exp(s - m)
            l = jnp.sum(p, axis=-1, keepdims=True)
            o = _hnn(p.astype(BF16), jnp.stack(vs)) / l
            lse = m + jnp.log(l)
            for G in range(4):
                sl = slice(G * 128, (G + 1) * 128)
                o_ref[rows, sl] = jnp.where(lo, o[2 * G], o[2 * G + 1])
                lse_ref[rows, sl] = jnp.where(lo, lse[2 * G], lse[2 * G + 1])

    cur, prev = _query_step_specs(QB)
    return pl.pallas_call(
        body, name=name, grid=(d, nb // QB),
        in_specs=[cur, prev, cur, prev, cur],
        out_specs=[cur, cur],
        out_shape=[jax.ShapeDtypeStruct((L, d * ATTN_WIDTH), F32)] * 2,
        compiler_params=_params(2),
    )(q, k, k, v, v)


def _attn_merge(parts):
    S = parts[0][0].shape[0]
    tm = VIEW_TILE

    def body(o1, s1, o2, s2, o3, s3, o_ref, lse1, lse4, lse16, planes):
        outs, lses = [], []
        for d, (o, s) in zip(DILATIONS, ((o1, s1), (o2, s2), (o3, s3))):
            outs.append(_view_to_tile(o, d, planes))
            lses.append(_view_to_tile(s, d, planes))
        mx = jnp.maximum(jnp.maximum(lses[0], lses[1]), lses[2])
        es = [jnp.exp(s - mx) for s in lses]
        den = es[0] + es[1] + es[2]
        o_ref[...] = (es[0] * outs[0] + es[1] * outs[1] + es[2] * outs[2]) / den
        _tile_to_views(mx + jnp.log(den), planes, (lse1, lse4, lse16))

    views = [_view_spec(d) for d in DILATIONS]
    flat = [t for p in parts for t in p]
    return pl.pallas_call(
        body, name="attn_merge", grid=(S // tm,),
        in_specs=[views[p] for p in range(3) for _ in range(2)],
        out_specs=[views[0]] + views,
        out_shape=[jax.ShapeDtypeStruct((S, ATTN_WIDTH), F32)] + [_view_shape(S, d, F32) for d in DILATIONS],
        scratch_shapes=[pltpu.VMEM((4, tm, 128), F32)],
        compiler_params=_params(1),
    )(*flat)


def _head_col(t, msk, big):
    if big:
        return jnp.max(jnp.where(msk, t, NEG_BIG), axis=-1, keepdims=True)
    return jnp.sum(jnp.where(msk, t, 0.0), axis=-1, keepdims=True) * (1.0 / 64.0)


def _attn_bwd_q(q, k, v, do, lse, dd, d, name):
    L = q.shape[0]
    nb = L // ATTN_BLOCK
    B = ATTN_BLOCK
    QB = _blocks_per_step(nb)

    def body(q_ref, kp_ref, kc_ref, vp_ref, vc_ref, do_ref, lse_ref, dd_ref, dq_ref):
        n = pl.program_id(1)
        qi = lax.broadcasted_iota(jnp.int32, (B, 2 * B), 0)
        kj = lax.broadcasted_iota(jnp.int32, (B, 2 * B), 1)
        steps = qi + B - kj
        band = (steps >= 0) & (steps <= B)
        lo = lax.broadcasted_iota(jnp.int32, (B, 128), 1) < 64
        bias = _head_bias(steps, d)
        for sub in range(QB):
            rows = slice(sub * B, (sub + 1) * B)
            valid = band & ((kj >= B) | (n > 0)) if sub == 0 else band
            qs, ks, vs, dos, lses, dcols = [], [], [], [], [], []
            for G in range(4):
                sl = slice(G * 128, (G + 1) * 128)
                qg = q_ref[rows, sl]
                kg = jnp.concatenate([_prev_block(kp_ref, kc_ref, sub, sl), kc_ref[rows, sl]], axis=0)
                vg = jnp.concatenate([_prev_block(vp_ref, vc_ref, sub, sl), vc_ref[rows, sl]], axis=0)
                dog = do_ref[rows, sl]
                qs += [jnp.where(lo, qg, jnp.zeros_like(qg)), jnp.where(lo, jnp.zeros_like(qg), qg)]
                dos += [jnp.where(lo, dog, 0.0).astype(BF16), jnp.where(lo, 0.0, dog).astype(BF16)]
                ks += [kg, kg]
                vs += [vg, vg]
                lses += _head_cols(lse_ref[rows, sl], lo, True)
                dcols += _head_cols(dd_ref[rows, sl], lo, False)
            kb = jnp.stack(ks)
            s = _hnt(jnp.stack(qs), kb) + bias
            p = jnp.where(valid, jnp.exp(jnp.where(valid, s, NEG_BIG) - jnp.stack(lses)), 0.0)
            dp = _hnt(jnp.stack(dos), jnp.stack(vs))
            ds = p * (dp - jnp.stack(dcols))
            dq = _hnn(ds.astype(BF16), kb) * ATTN_SCALE
            for G in range(4):
                dq_ref[rows, G * 128:(G + 1) * 128] = jnp.where(lo, dq[2 * G], dq[2 * G + 1]).astype(BF16)

    cur, prev = _query_step_specs(QB)
    return pl.pallas_call(
        body, name=name, grid=(d, nb // QB), in_specs=[cur, prev, cur, prev, cur, cur, cur, cur], out_specs=cur,
        out_shape=jax.ShapeDtypeStruct((L, d * ATTN_WIDTH), BF16), compiler_params=_params(2),
    )(q, k, k, v, v, do, lse, dd)


def _attn_bwd_kv(q, k, v, do, lse, dd, d, name):
    L = q.shape[0]
    nb = L // ATTN_BLOCK
    B = ATTN_BLOCK
    KB = _blocks_per_step(nb)
    n_steps = nb // KB

    def body(k_ref, v_ref, qc_ref, qn_ref, doc_ref, don_ref, lsec_ref, lsen_ref, ddc_ref, ddn_ref, dk_ref, dv_ref):
        j = pl.program_id(1)
        qrow = lax.broadcasted_iota(jnp.int32, (2 * B, B), 0)
        kk = lax.broadcasted_iota(jnp.int32, (2 * B, B), 1)
        steps = qrow - kk
        band = (steps >= 0) & (steps <= B)
        lo2 = lax.broadcasted_iota(jnp.int32, (2 * B, 128), 1) < 64
        lo = lax.broadcasted_iota(jnp.int32, (B, 128), 1) < 64
        stepsf = steps.astype(F32)
        for sub in range(KB):
            rows = slice(sub * B, (sub + 1) * B)
            last = sub == KB - 1
            valid = band & ((qrow < B) | (j < n_steps - 1)) if last else band
            after = lambda cur_ref, nxt_ref, sl: nxt_ref[:, sl] if last else cur_ref[(sub + 1) * B:(sub + 2) * B, sl]
            for G in range(4):
                sl = slice(G * 128, (G + 1) * 128)
                kg = k_ref[rows, sl]
                vg = v_ref[rows, sl]
                qq = jnp.concatenate([qc_ref[rows, sl], after(qc_ref, qn_ref, sl)], axis=0)
                doo = jnp.concatenate([doc_ref[rows, sl], after(doc_ref, don_ref, sl)], axis=0)
                lse2 = jnp.concatenate([lsec_ref[rows, sl], after(lsec_ref, lsen_ref, sl)], axis=0)
                dd2 = jnp.concatenate([ddc_ref[rows, sl], after(ddc_ref, ddn_ref, sl)], axis=0)
                doo_b = doo.astype(BF16)
                dks, dvs = [], []
                for half in (0, 1):
                    msk = lo2 if half == 0 else jnp.logical_not(lo2)
                    qm = jnp.where(msk, qq, jnp.zeros_like(qq))
                    s = _nt(qm, kg) - (_slope(2 * G + half) * d) * stepsf
                    lse_c = _head_col(lse2, msk, True)
                    p = jnp.where(valid, jnp.exp(jnp.where(valid, s, NEG_BIG) - lse_c), 0.0)
                    dvs.append(_tn(p.astype(BF16), doo_b))
                    dom = jnp.where(msk, doo, 0.0).astype(BF16)
                    dp = _nt(dom, vg)
                    dcol = _head_col(dd2, msk, False)
                    ds = p * (dp - dcol)
                    dks.append(_tn(ds.astype(BF16), qq))
                dk_ref[rows, sl] = jnp.where(lo, dks[0], dks[1]).astype(BF16)
                dv_ref[rows, sl] = jnp.where(lo, dvs[0], dvs[1]).astype(BF16)

    cur = pl.BlockSpec((KB * B, ATTN_WIDTH), lambda r, j: (j, r))
    nxt = pl.BlockSpec((B, ATTN_WIDTH), lambda r, j: (jnp.minimum(KB * (j + 1), nb - 1), r))
    return pl.pallas_call(
        body, name=name, grid=(d, n_steps), in_specs=[cur, cur, cur, nxt, cur, nxt, cur, nxt, cur, nxt],
        out_specs=[cur, cur],
        out_shape=[jax.ShapeDtypeStruct((L, d * ATTN_WIDTH), BF16)] * 2, compiler_params=_params(2),
    )(k, v, q, q, do, do, lse, lse, dd, dd)


CONV_T = 512
HALO = 8


def _per_head(head, refs):
    for h in range(DN_HEADS):
        lanes = pl.ds(h * DN_HEAD_DIM, DN_HEAD_DIM)
        head(*[r.at[:, lanes] for r in refs[:-1]], refs[-1])


def _conv_taps(pad_ref, w, T):
    acc = pad_ref[pl.ds(HALO - 3, T), :] * w[0:1, :]
    for j in range(1, CONV_WIDTH):
        acc = acc + pad_ref[pl.ds(HALO - 3 + j, T), :] * w[j:j + 1, :]
    return acc


def _conv_fwd(xq, xk, xv, conv_w):
    S = xq.shape[0]
    T = CONV_T

    def body(*refs):
        _per_head(head, refs)

    def head(xq_ref, xqh_ref, xk_ref, xkh_ref, xv_ref, xvh_ref, wq_ref, wk_ref, wv_ref,
             qn_ref, kn_ref, v_ref, pad_ref):
        i = pl.program_id(0)

        def act(x_ref, xh_ref, w_ref):
            pad_ref[pl.ds(0, HALO), :] = jnp.where(i > 0, xh_ref[...], 0.0)
            pad_ref[pl.ds(HALO, T), :] = x_ref[...]
            c = _conv_taps(pad_ref, w_ref[...], T)
            return c * _sigmoid(c)

        def l2n(t):
            return t * lax.rsqrt(jnp.sum(t * t, axis=-1, keepdims=True) + L2_EPS)

        qn_ref[...] = l2n(act(xq_ref, xqh_ref, wq_ref))
        kn_ref[...] = l2n(act(xk_ref, xkh_ref, wk_ref))
        v_ref[...] = act(xv_ref, xvh_ref, wv_ref)

    tile = pl.BlockSpec((T, DN_WIDTH), lambda i: (i, 0))
    halo = pl.BlockSpec((HALO, DN_WIDTH), lambda i: (jnp.maximum(i * (T // HALO) - 1, 0), 0))
    wspec = lambda sec: pl.BlockSpec((CONV_WIDTH, DN_WIDTH), lambda i, sec=sec: (0, sec))
    return pl.pallas_call(
        body, name="dn_conv_fwd", grid=(S // T,),
        in_specs=[tile, halo, tile, halo, tile, halo, wspec(0), wspec(1), wspec(2)],
        out_specs=[tile, tile, tile],
        out_shape=[jax.ShapeDtypeStruct((S, DN_WIDTH), F32)] * 3,
        scratch_shapes=[pltpu.VMEM((T + HALO, 128), F32)],
        compiler_params=_params(1),
    )(xq, xq, xk, xk, xv, xv, conv_w, conv_w, conv_w)


def _conv_bwd_pre(xq, xk, xv, conv_w, dqn, dkn, dv):
    S = xq.shape[0]
    T = CONV_T

    def body(*refs):
        _per_head(head, refs)

    def head(xq_ref, xqh_ref, xk_ref, xkh_ref, xv_ref, xvh_ref, wq_ref, wk_ref, wv_ref,
             dqn_ref, dkn_ref, dv_ref, dcq_ref, dck_ref, dcv_ref, dwq_ref, dwk_ref, dwv_ref, pad_ref):
        i = pl.program_id(0)

        def one(x_ref, xh_ref, w_ref, dy_ref, dc_ref, dw_ref, normed):
            pad_ref[pl.ds(0, HALO), :] = jnp.where(i > 0, xh_ref[...], 0.0)
            pad_ref[pl.ds(HALO, T), :] = x_ref[...]
            c = _conv_taps(pad_ref, w_ref[...], T)
            sg = _sigmoid(c)
            a = c * sg
            dy = dy_ref[...]
            if normed:
                r = lax.rsqrt(jnp.sum(a * a, axis=-1, keepdims=True) + L2_EPS)
                y = a * r
                da = r * (dy - y * jnp.sum(dy * y, axis=-1, keepdims=True))
            else:
                da = dy
            dc = da * (sg * (1.0 + c * (1.0 - sg)))
            dc_ref[...] = dc

            @pl.when(i == 0)
            def _():
                dw_ref[...] = jnp.zeros_like(dw_ref)

            rows = [jnp.sum(dc * pad_ref[pl.ds(HALO - 3 + j, T), :], axis=0, keepdims=True) for j in range(CONV_WIDTH)]
            dw_ref[...] += jnp.concatenate(rows + [jnp.zeros((8 - CONV_WIDTH, 128), F32)], axis=0)

        one(xq_ref, xqh_ref, wq_ref, dqn_ref, dcq_ref, dwq_ref, True)
        one(xk_ref, xkh_ref, wk_ref, dkn_ref, dck_ref, dwk_ref, True)
        one(xv_ref, xvh_ref, wv_ref, dv_ref, dcv_ref, dwv_ref, False)

    tile = pl.BlockSpec((T, DN_WIDTH), lambda i: (i, 0))
    halo = pl.BlockSpec((HALO, DN_WIDTH), lambda i: (jnp.maximum(i * (T // HALO) - 1, 0), 0))
    wspec = lambda sec: pl.BlockSpec((CONV_WIDTH, DN_WIDTH), lambda i, sec=sec: (0, sec))
    dwspec = pl.BlockSpec((8, DN_WIDTH), lambda i: (0, 0))
    return pl.pallas_call(
        body, name="dn_conv_bwd_pre", grid=(S // T,),
        in_specs=[tile, halo, tile, halo, tile, halo, wspec(0), wspec(1), wspec(2), tile, tile, tile],
        out_specs=[tile, tile, tile, dwspec, dwspec, dwspec],
        out_shape=[jax.ShapeDtypeStruct((S, DN_WIDTH), F32)] * 3 + [jax.ShapeDtypeStruct((8, DN_WIDTH), F32)] * 3,
        scratch_shapes=[pltpu.VMEM((T + HALO, 128), F32)],
        compiler_params=_params(1),
    )(xq, xq, xk, xk, xv, xv, conv_w, conv_w, conv_w, dqn, dkn, dv)


def _conv_bwd_x(dcq, dck, dcv, conv_w):
    S = dcq.shape[0]
    T = CONV_T
    nt = S // T

    def body(*refs):
        _per_head(head, refs)

    def head(dq_ref, dqh_ref, dk_ref, dkh_ref, dv_ref, dvh_ref, wq_ref, wk_ref, wv_ref,
             oq_ref, ok_ref, ov_ref, pad_ref):
        i = pl.program_id(0)

        def one(d_ref, dh_ref, w_ref, o_ref):
            pad_ref[pl.ds(0, T), :] = d_ref[...]
            pad_ref[pl.ds(T, HALO), :] = jnp.where(i < nt - 1, dh_ref[...], 0.0)
            w = w_ref[...]
            acc = pad_ref[pl.ds(3, T), :] * w[0:1, :]
            for j in range(1, CONV_WIDTH):
                acc = acc + pad_ref[pl.ds(3 - j, T), :] * w[j:j + 1, :]
            o_ref[...] = acc

        one(dq_ref, dqh_ref, wq_ref, oq_ref)
        one(dk_ref, dkh_ref, wk_ref, ok_ref)
        one(dv_ref, dvh_ref, wv_ref, ov_ref)

    tile = pl.BlockSpec((T, DN_WIDTH), lambda i: (i, 0))
    halo = pl.BlockSpec((HALO, DN_WIDTH), lambda i: (jnp.minimum((i + 1) * (T // HALO), S // HALO - 1), 0))
    wspec = lambda sec: pl.BlockSpec((CONV_WIDTH, DN_WIDTH), lambda i, sec=sec: (0, sec))
    return pl.pallas_call(
        body, name="dn_conv_bwd_x", grid=(nt,),
        in_specs=[tile, halo, tile, halo, tile, halo, wspec(0), wspec(1), wspec(2)],
        out_specs=[tile, tile, tile],
        out_shape=[jax.ShapeDtypeStruct((S, DN_WIDTH), F32)] * 3,
        scratch_shapes=[pltpu.VMEM((T + HALO, 128), F32)],
        compiler_params=_params(1),
    )(dcq, dcq, dck, dck, dcv, dcv, conv_w, conv_w, conv_w)


PREP_CHUNKS = 4
SCAN_CHUNKS = 8


def _bnn(a, b):
    return lax.dot_general(a, b, (((2,), (1,)), ((0,), (0,))), preferred_element_type=F32, precision=HI)


def _bnt(a, b):
    return lax.dot_general(a, b, (((2,), (2,)), ((0,), (0,))), preferred_element_type=F32, precision=HI)


def _btn(a, b):
    return lax.dot_general(a, b, (((1,), (1,)), ((0,), (0,))), preferred_element_type=F32, precision=HI)


def _tri_inverse_b(a, blk, eye):
    dg = jnp.where(blk, a, 0.0)
    lo = a - dg
    d2 = _bnn(dg, dg)
    d4 = _bnn(d2, d2)
    d8 = _bnn(d4, d4)
    td = _bnn(_bnn(_bnn(eye - dg, eye + d2), eye + d4), eye + d8)
    b = _bnn(td, lo)
    b2 = _bnn(b, b)
    return _bnn(_bnn(eye - b, eye + b2), td)


def _dn_common_b(bds, avec, dvec, q_raw, k, v, t=None):
    C = DN_CHUNK
    lane = lax.broadcasted_iota(jnp.int32, (C, 128), 1)
    row = lax.broadcasted_iota(jnp.int32, (1, C, C), 1)
    col = lax.broadcasted_iota(jnp.int32, (1, C, C), 2)
    incl = row >= col
    strict = row > col
    eye = (row == col).astype(F32)
    blk = (row // 16) == (col // 16)
    pick = lambda tile, ln: jnp.sum(jnp.where(lane == ln, tile, 0.0), axis=-1, keepdims=True)
    betas, graws, zcs = [], [], []
    for bd in bds:
        z = bd + dvec
        g_all = -jnp.exp(avec) * (jnp.maximum(z, 0.0) + jnp.log(1.0 + jnp.exp(-jnp.abs(z))))
        beta_all = _sigmoid(bd)
        for h in range(DN_HEADS):
            betas.append(pick(beta_all, h))
            graws.append(pick(g_all, DN_HEADS + h))
            zcs.append(pick(z, DN_HEADS + h))
    beta, graw, zc = jnp.stack(betas), jnp.stack(graws), jnp.stack(zcs)
    to_row = lambda c: jnp.sum(eye * c, axis=1, keepdims=True)
    gc = jnp.sum(jnp.where(incl, to_row(graw), 0.0), axis=-1, keepdims=True)
    decay = jnp.exp(jnp.where(incl, gc - to_row(gc), NEG_BIG))
    q = q_raw * (DN_HEAD_DIM ** -0.5)
    kb = k * beta
    kk = _bnt(kb, k)
    if t is None:
        t = _tri_inverse_b(jnp.where(strict, kk * decay, 0.0), blk, eye)
    eg = jnp.exp(gc)
    rhs_w = kb * eg
    u = _bnn(t, v * beta)
    w = _bnn(t, rhs_w)
    qk = _bnt(q, k)
    aq = jnp.where(incl, qk * decay, 0.0)
    last = lax.broadcasted_iota(jnp.int32, (1, C, 1), 1) == C - 1
    g_last = jnp.sum(jnp.where(last, gc, 0.0), axis=1, keepdims=True)
    ekd = jnp.exp(g_last - gc)
    return dict(beta=beta, graw=graw, zc=zc, gc=gc, decay=decay, q=q, kb=kb, kk=kk, t=t, eg=eg, rhs_w=rhs_w,
                u=u, w=w, qk=qk, aq=aq, g_last=g_last, ekd=ekd, kd=k * ekd, qg=q * eg,
                incl=incl, strict=strict, eye=eye, lane=lane, row=row, col=col, last=last)


def _stack_heads(ref, rows):
    return jnp.stack([ref[rows, h * DN_HEAD_DIM:(h + 1) * DN_HEAD_DIM] for h in range(DN_HEADS)])


def _stack_units(ref, nc):
    C = DN_CHUNK
    return jnp.concatenate([_stack_heads(ref, slice(ci * C, (ci + 1) * C)) for ci in range(nc)], axis=0)


def _store_units(ref, val, nc):
    C = DN_CHUNK
    for ci in range(nc):
        for h in range(DN_HEADS):
            ref[ci * C:(ci + 1) * C, h * DN_HEAD_DIM:(h + 1) * DN_HEAD_DIM] = val[ci * DN_HEADS + h]


def _dn_prep(qn, kn, v, bd, avec, dvec):
    S = qn.shape[0]
    C = DN_CHUNK
    N = S // C
    nc = PREP_CHUNKS

    def body(q_ref, k_ref, v_ref, bd_ref, a_ref, d_ref, u_ref, w_ref, qg_ref, kd_ref, aq_ref, t_ref, egl_ref):
        bds = [bd_ref[ci * C:(ci + 1) * C, :] for ci in range(nc)]
        c = _dn_common_b(bds, a_ref[...], d_ref[...], _stack_units(q_ref, nc), _stack_units(k_ref, nc), _stack_units(v_ref, nc))
        _store_units(u_ref, c["u"], nc)
        _store_units(w_ref, c["w"], nc)
        _store_units(qg_ref, c["qg"], nc)
        _store_units(kd_ref, c["kd"], nc)
        egl = jnp.broadcast_to(jnp.exp(c["g_last"]), (nc * DN_HEADS, 1, 128))
        for ci in range(nc):
            for h in range(DN_HEADS):
                aq_ref[h, ci * C:(ci + 1) * C, :] = c["aq"][ci * DN_HEADS + h]
                t_ref[h, ci * C:(ci + 1) * C, :] = c["t"][ci * DN_HEADS + h]
            egl_ref[ci * 8:(ci + 1) * 8, :] = jnp.concatenate(
                [egl[ci * DN_HEADS + h] for h in range(DN_HEADS)] + [jnp.zeros((8 - DN_HEADS, 128), F32)], axis=0)

    tok = lambda w: pl.BlockSpec((nc * C, w), lambda n: (n, 0))
    sq = pl.BlockSpec((DN_HEADS, nc * C, C), lambda n: (0, n, 0))
    vec = pl.BlockSpec((1, 128), lambda n: (0, 0))
    return pl.pallas_call(
        body, name="dn_prep", grid=(N // nc,),
        in_specs=[tok(DN_WIDTH)] * 3 + [tok(128), vec, vec],
        out_specs=[tok(DN_WIDTH)] * 4 + [sq, sq, pl.BlockSpec((nc * 8, 128), lambda n: (n, 0))],
        out_shape=[jax.ShapeDtypeStruct((S, DN_WIDTH), F32)] * 4 + [jax.ShapeDtypeStruct((DN_HEADS, S, C), F32)] * 2
                  + [jax.ShapeDtypeStruct((N * 8, 128), F32)],
        compiler_params=_params(1),
    )(qn, kn, v, bd, avec, dvec)


def _dn_scan_fwd(u, w, qg, kd, aq, egl, gate, dn_gain):
    S = u.shape[0]
    C = DN_CHUNK
    N = S // C
    HD = DN_HEAD_DIM
    nc = SCAN_CHUNKS

    def body(u_ref, w_ref, qg_ref, kd_ref, aq_ref, egl_ref, gate_ref, gain_ref, dn_ref, o_ref, vn_ref, st_ref, state_ref):
        @pl.when(pl.program_id(0) == 0)
        def _():
            state_ref[...] = jnp.zeros_like(state_ref)

        gain = gain_ref[...]
        for ci in range(nc):
            rows = slice(ci * C, (ci + 1) * C)
            st = state_ref[...]
            for h in range(DN_HEADS):
                st_ref[ci * DN_WIDTH + h * HD:ci * DN_WIDTH + (h + 1) * HD, :] = st[h]
            v_new = _stack_heads(u_ref, rows) - _bnn(_stack_heads(w_ref, rows), st)
            o = _bnn(_stack_heads(qg_ref, rows), st) + _bnn(aq_ref[:, rows, :], v_new)
            egl = jnp.stack([egl_ref[ci * 8 + h:ci * 8 + h + 1, :] for h in range(DN_HEADS)])
            state_ref[...] = st * egl + _btn(_stack_heads(kd_ref, rows), v_new)
            r = lax.rsqrt(jnp.mean(o * o, axis=-1, keepdims=True) + NORM_EPS)
            gt = _stack_heads(gate_ref, rows)
            dn = o * r * gain * (gt * _sigmoid(gt))
            for h in range(DN_HEADS):
                sl = slice(h * HD, (h + 1) * HD)
                vn_ref[rows, sl] = v_new[h]
                o_ref[rows, sl] = o[h]
                dn_ref[rows, sl] = dn[h]

    tok = lambda wd: pl.BlockSpec((nc * C, wd), lambda n: (n, 0))
    sq = pl.BlockSpec((DN_HEADS, nc * C, C), lambda n: (0, n, 0))
    vec = pl.BlockSpec((1, 128), lambda n: (0, 0))
    return pl.pallas_call(
        body, name="dn_scan_fwd", grid=(N // nc,),
        in_specs=[tok(DN_WIDTH)] * 4 + [sq, pl.BlockSpec((nc * 8, 128), lambda n: (n, 0)), tok(DN_WIDTH), vec],
        out_specs=[tok(DN_WIDTH)] * 3 + [pl.BlockSpec((nc * DN_WIDTH, HD), lambda n: (n, 0))],
        out_shape=[jax.ShapeDtypeStruct((S, DN_WIDTH), F32)] * 3 + [jax.ShapeDtypeStruct((N * DN_WIDTH, HD), F32)],
        scratch_shapes=[pltpu.VMEM((DN_HEADS, HD, HD), F32)],
        compiler_params=_params(1),
    )(u, w, qg, kd, aq, egl, gate, dn_gain)


def _dn_scan_bwd(w, qg, kd, aq, egl, gate, dn_gain, o, ddn):
    S = w.shape[0]
    C = DN_CHUNK
    N = S // C
    HD = DN_HEAD_DIM
    nc = SCAN_CHUNKS

    def body(w_ref, qg_ref, kd_ref, aq_ref, egl_ref, gate_ref, gain_ref, o_ref, ddn_ref,
             do_ref, dvn_ref, dgate_ref, dst_ref, small_ref, dstate_ref):
        @pl.when(pl.program_id(0) == 0)
        def _():
            dstate_ref[...] = jnp.zeros_like(dstate_ref)
            small_ref[...] = jnp.zeros_like(small_ref)

        gain = gain_ref[...]
        d_gain = jnp.zeros((1, 128), F32)
        for ci in reversed(range(nc)):
            rows = slice(ci * C, (ci + 1) * C)
            dsn = dstate_ref[...]
            for h in range(DN_HEADS):
                dst_ref[ci * DN_WIDTH + h * HD:ci * DN_WIDTH + (h + 1) * HD, :] = dsn[h]
            ov = _stack_heads(o_ref, rows)
            r = lax.rsqrt(jnp.mean(ov * ov, axis=-1, keepdims=True) + NORM_EPS)
            on = ov * r
            gt = _stack_heads(gate_ref, rows)
            sgt = _sigmoid(gt)
            silu_g = gt * sgt
            dy = _stack_heads(ddn_ref, rows)
            d_gain = d_gain + jnp.sum(jnp.sum(dy * on * silu_g, axis=1, keepdims=True), axis=0)
            dgate = dy * on * gain * (sgt * (1.0 + gt * (1.0 - sgt)))
            don = dy * gain * silu_g
            do = r * (don - on * jnp.mean(don * on, axis=-1, keepdims=True))
            d_vnew = _btn(aq_ref[:, rows, :], do) + _bnn(_stack_heads(kd_ref, rows), dsn)
            egl = jnp.stack([egl_ref[ci * 8 + h:ci * 8 + h + 1, :] for h in range(DN_HEADS)])
            dstate_ref[...] = _btn(_stack_heads(qg_ref, rows), do) + dsn * egl - _btn(_stack_heads(w_ref, rows), d_vnew)
            for h in range(DN_HEADS):
                sl = slice(h * HD, (h + 1) * HD)
                do_ref[rows, sl] = do[h]
                dvn_ref[rows, sl] = d_vnew[h]
                dgate_ref[rows, sl] = dgate[h]
        small_ref[...] += jnp.concatenate([d_gain, jnp.zeros((7, 128), F32)], axis=0)

    nb = N // nc
    tok = lambda wd: pl.BlockSpec((nc * C, wd), lambda i: (nb - 1 - i, 0))
    sq = pl.BlockSpec((DN_HEADS, nc * C, C), lambda i: (0, nb - 1 - i, 0))
    vec = pl.BlockSpec((1, 128), lambda i: (0, 0))
    return pl.pallas_call(
        body, name="dn_scan_bwd", grid=(nb,),
        in_specs=[tok(DN_WIDTH)] * 3 + [sq, pl.BlockSpec((nc * 8, 128), lambda i: (nb - 1 - i, 0)), tok(DN_WIDTH), vec,
                                       tok(DN_WIDTH), tok(DN_WIDTH)],
        out_specs=[tok(DN_WIDTH)] * 3 + [pl.BlockSpec((nc * DN_WIDTH, HD), lambda i: (nb - 1 - i, 0)),
                                        pl.BlockSpec((8, 128), lambda i: (0, 0))],
        out_shape=[jax.ShapeDtypeStruct((S, DN_WIDTH), F32)] * 3 + [jax.ShapeDtypeStruct((N * DN_WIDTH, HD), F32),
                                                                  jax.ShapeDtypeStruct((8, 128), F32)],
        scratch_shapes=[pltpu.VMEM((DN_HEADS, HD, HD), F32)],
        compiler_params=_params(1),
    )(w, qg, kd, aq, egl, gate, dn_gain, o, ddn)


def _dn_post(qn, kn, v, bd, avec, dvec, t_inv, v_new_all, states, dstates, do_all, dvn_all, comm=None):
    S = qn.shape[0]
    C = DN_CHUNK
    N = S // C
    HD = DN_HEAD_DIM
    nc = PREP_CHUNKS
    B = nc * DN_HEADS

    def body(q_ref, k_ref, v_ref, bd_ref, a_ref, d_ref, t_ref, vn_ref, st_ref, dst_ref, do_ref, dvn_ref,
             dq_ref, dk_ref, dv_ref, dbd_ref, small_ref):
        @pl.when(pl.program_id(0) == 0)
        def _():
            small_ref[...] = jnp.zeros_like(small_ref)

        avec = a_ref[...]
        bds = [bd_ref[ci * C:(ci + 1) * C, :] for ci in range(nc)]
        k = _stack_units(k_ref, nc)
        vv = _stack_units(v_ref, nc)
        t = jnp.concatenate([t_ref[:, ci * C:(ci + 1) * C, :] for ci in range(nc)], axis=0)
        c = _dn_common_b(bds, avec, d_ref[...], _stack_units(q_ref, nc), k, vv, t=t)
        q, kb, eg, u, w = c["q"], c["kb"], c["eg"], c["u"], c["w"]
        beta, decay, incl, strict, eye = c["beta"], c["decay"], c["incl"], c["strict"], c["eye"]
        st = jnp.stack([st_ref[b * HD:(b + 1) * HD, :] for b in range(B)])
        dsn = jnp.stack([dst_ref[b * HD:(b + 1) * HD, :] for b in range(B)])
        v_new = _stack_units(vn_ref, nc)
        do = _stack_units(do_ref, nc)
        d_vnew = _stack_units(dvn_ref, nc)
        egl = jnp.exp(c["g_last"])
        daq = jnp.where(incl, _bnt(do, v_new), 0.0)
        d_qg = _bnt(do, st)
        d_kd = _bnt(v_new, dsn)
        d_glast = jnp.sum(jnp.sum(dsn * st, axis=-1, keepdims=True), axis=1, keepdims=True) * egl
        d_w = -_bnt(d_vnew, st)
        d_ru = _btn(t, d_vnew)
        d_rw = _btn(t, d_w)
        da = -jnp.where(strict, _bnt(d_ru, u) + _bnt(d_rw, w), 0.0)
        dv = d_ru * beta
        dbeta = jnp.sum(d_ru * vv, axis=-1, keepdims=True)
        dkb = d_rw * eg
        dgc = jnp.sum(d_rw * c["rhs_w"], axis=-1, keepdims=True)
        dkk = da * decay
        ddecay = da * c["kk"]
        dkb = dkb + _bnn(dkk, k)
        dk = _btn(dkk, kb)
        dqk = daq * decay
        ddecay = ddecay + daq * c["qk"]
        dq = _bnn(dqk, k)
        dk = dk + _btn(dqk, q)
        m = ddecay * decay
        col_sum = jnp.sum(m, axis=1, keepdims=True)
        dgc = dgc + jnp.sum(m, axis=-1, keepdims=True) - jnp.sum(eye * col_sum, axis=-1, keepdims=True)
        dq = dq + d_qg * eg
        dgc = dgc + jnp.sum(d_qg * c["qg"], axis=-1, keepdims=True)
        dk = dk + d_kd * c["ekd"]
        tk = jnp.sum(d_kd * c["kd"], axis=-1, keepdims=True)
        dgc = dgc - tk
        d_glast = d_glast + jnp.sum(tk, axis=1, keepdims=True)
        dk = dk + dkb * beta
        dbeta = dbeta + jnp.sum(dkb * k, axis=-1, keepdims=True)
        dgc = dgc + jnp.where(c["last"], d_glast, 0.0)
        dgc_row = jnp.sum(eye * dgc, axis=1, keepdims=True)
        dgraw = jnp.sum(jnp.where(c["col"] >= c["row"], dgc_row, 0.0), axis=-1, keepdims=True)
        _store_units(dq_ref, dq * (HD ** -0.5), nc)
        _store_units(dk_ref, dk, nc)
        _store_units(dv_ref, dv, nc)
        dbraw = dbeta * beta * (1.0 - beta)
        dzc = dgraw * _sigmoid(c["zc"])
        ga = dgraw * c["graw"]
        lane = c["lane"]
        lane1 = lax.broadcasted_iota(jnp.int32, (1, 128), 1)
        neg_ea = -jnp.exp(avec)
        d_alog = jnp.zeros((1, 128), F32)
        d_dt = jnp.zeros((1, 128), F32)
        for ci in range(nc):
            dbd = jnp.zeros((C, 128), F32)
            for h in range(DN_HEADS):
                b = ci * DN_HEADS + h
                dz = dzc[b] * neg_ea
                dbd = dbd + jnp.where(lane == h, dbraw[b], 0.0) + jnp.where(lane == DN_HEADS + h, dz, 0.0)
                d_alog = d_alog + jnp.where(lane1 == DN_HEADS + h, jnp.sum(ga[b], axis=0, keepdims=True), 0.0)
                d_dt = d_dt + jnp.where(lane1 == DN_HEADS + h, jnp.sum(dz, axis=0, keepdims=True), 0.0)
            dbd_ref[ci * C:(ci + 1) * C, :] = dbd
        small_ref[...] += jnp.concatenate([d_alog, d_dt, jnp.zeros((6, 128), F32)], axis=0)

    tok = lambda wd: pl.BlockSpec((nc * C, wd), lambda n: (n, 0))
    big = pl.BlockSpec((nc * DN_WIDTH, HD), lambda n: (n, 0))
    sq = pl.BlockSpec((DN_HEADS, nc * C, C), lambda n: (0, n, 0))
    vec = pl.BlockSpec((1, 128), lambda n: (0, 0))
    return _call(
        body, (qn, kn, v, bd, avec, dvec, t_inv, v_new_all, states, dstates, do_all, dvn_all),
        name="dn_post", grid=(N // nc,), comm=comm,
        in_specs=[tok(DN_WIDTH)] * 3 + [tok(128), vec, vec, sq, tok(DN_WIDTH), big, big, tok(DN_WIDTH), tok(DN_WIDTH)],
        out_specs=[tok(DN_WIDTH)] * 3 + [tok(128), pl.BlockSpec((8, 128), lambda n: (0, 0))],
        out_shape=[jax.ShapeDtypeStruct((S, DN_WIDTH), F32)] * 3 + [jax.ShapeDtypeStruct((S, 128), F32),
                                                                  jax.ShapeDtypeStruct((8, 128), F32)])


def _outproj_fwd(x, attn, dn, w_out):
    S, D = x.shape
    tm = 512

    def body(x_ref, a_ref, d_ref, w_ref, xo_ref, mix_ref):
        a = a_ref[...].astype(BF16)
        dd = d_ref[...].astype(BF16)
        mix_ref[:, 0:ATTN_WIDTH] = a
        mix_ref[:, ATTN_WIDTH:] = dd
        xo_ref[...] = x_ref[...] + _nn(a, w_ref[0:ATTN_WIDTH, :]) + _nn(dd, w_ref[ATTN_WIDTH:, :])

    tok = lambda w: pl.BlockSpec((tm, w), lambda i: (i, 0))
    return pl.pallas_call(
        body, name="outproj_fwd", grid=(S // tm,),
        in_specs=[tok(D), tok(ATTN_WIDTH), tok(DN_WIDTH), pl.BlockSpec((D, D), lambda i: (0, 0))],
        out_specs=[tok(D), tok(D)],
        out_shape=[jax.ShapeDtypeStruct((S, D), F32), jax.ShapeDtypeStruct((S, D), BF16)],
        compiler_params=_params(1),
    )(x, attn, dn, w_out)


def _outproj_bwd(dx, w_out, attn, comm=None):
    S, D = dx.shape
    tm = VIEW_TILE

    def body(dx_ref, w_ref, attn_ref, da1, da4, da16, dl1, dl4, dl16, ddn_ref, dxb_ref, planes):
        d = dx_ref[...].astype(BF16)
        dxb_ref[...] = d
        da = _nt(d, w_ref[0:ATTN_WIDTH, :])
        ddn_ref[...] = _nt(d, w_ref[ATTN_WIDTH:, :])
        _tile_to_views(da, planes, (da1, da4, da16))
        lo = lax.broadcasted_iota(jnp.int32, (tm, 128), 1) < 64
        cols = []
        for G in range(4):
            sl = slice(G * 128, (G + 1) * 128)
            t = da[:, sl] * attn_ref[:, sl]
            d0 = jnp.sum(jnp.where(lo, t, 0.0), axis=-1, keepdims=True)
            d1 = jnp.sum(jnp.where(lo, 0.0, t), axis=-1, keepdims=True)
            cols.append(jnp.where(lo, d0, d1))
        _tile_to_views(jnp.concatenate(cols, axis=1), planes, (dl1, dl4, dl16))

    tok = lambda w: pl.BlockSpec((tm, w), lambda i: (i, 0))
    views = [_view_spec(d) for d in DILATIONS]
    return _call(
        body, (dx, w_out, attn), name="outproj_bwd", grid=(S // tm,), comm=comm,
        in_specs=[tok(D), pl.BlockSpec((D, D), lambda i: (0, 0)), tok(ATTN_WIDTH)],
        out_specs=views + views + [tok(DN_WIDTH), tok(D)],
        out_shape=[_view_shape(S, d, F32) for d in DILATIONS] * 2
                  + [jax.ShapeDtypeStruct((S, DN_WIDTH), F32), jax.ShapeDtypeStruct((S, D), BF16)],
        scratch_shapes=[pltpu.VMEM((4, tm, 128), F32)])


def _adamw(w, g, m, v, name):
    R, Ccols = w.shape[0], w.shape[-1]
    tr = next((t for t in range(512, 7, -8) if R % t == 0), R)
    c1 = 1.0 - ADAM_B1 ** ADAM_STEP
    c2 = 1.0 - ADAM_B2 ** ADAM_STEP

    def body(w_ref, g_ref, m_ref, v_ref, d_ref, nm_ref, nv_ref):
        gv = g_ref[...]
        mn = ADAM_B1 * m_ref[...] + (1.0 - ADAM_B1) * gv
        vn = ADAM_B2 * v_ref[...] + (1.0 - ADAM_B2) * (gv * gv)
        nm_ref[...] = mn
        nv_ref[...] = vn
        d_ref[...] = -ADAM_LR * ((mn / c1) / (jnp.sqrt(vn / c2) + ADAM_EPS) + ADAM_WD * w_ref[...])

    if w.ndim == 2:
        grid, spec = (R // tr,), pl.BlockSpec((tr, Ccols), lambda i: (i, 0))
    else:
        grid, spec = (2,), pl.BlockSpec((R // 2, 1, Ccols), lambda i: (i, 0, 0))
    return pl.pallas_call(
        body, name=name, grid=grid, in_specs=[spec] * 4, out_specs=[spec] * 3,
        out_shape=[jax.ShapeDtypeStruct(w.shape, F32)] * 3, compiler_params=_params(1),
    )(w, g, m, v)


LATE_WEIGHTS = ("w_in", "w_out", "ffn2_gate", "ffn2_up", "ffn2_down")


def _local_step(x, target, wts, small, dist=None):
    g1, g2, gm, gf = small["norm_ffn1"], small["norm_ffn2"], small["norm_mix"], small["norm_final"]
    wts = dict(wts)

    def reduce_start(gs, tag):
        return _rs_add_pairs(gs, _swap_sibling(gs, True, "rs_swap_halves_" + tag), dist["c"], "rs_add_pairs_" + tag)

    (x1, h1, fg1, fu1), late = _ffn_fwd(x, g1, wts["ffn1_gate"], wts["ffn1_up"], wts["ffn1_down"], "ffn1_fwd",
                                        comm=_ag_comm(dist["late"]) if dist else None)
    if dist:
        wts.update(zip(LATE_WEIGHTS, late))
        wts["w_out"] = wts["w_out"].reshape(D_MODEL, D_MODEL)
        wts["w_in"] = _permute_w_in(wts["w_in"][:, :IN_COLS // N_CHIPS].reshape(IN_COLS, D_MODEL))
    h2, *qkv, xq, xk, xv, gate, bd = _inproj_fwd(x1, gm, wts["w_in"])
    aq, ak, av = qkv[0:3], qkv[3:6], qkv[6:9]
    parts = [_attn_fwd(aq[p], ak[p], av[p], d, f"attn_fwd_d{d}") for p, d in enumerate(DILATIONS)]
    attn, *lse = _attn_merge(parts)
    conv_w = small["conv_w"]
    qn, kn, vv = _conv_fwd(xq, xk, xv, conv_w)
    dn_u, dn_w, dn_qg, dn_kd, dn_aq, dn_t, dn_egl = _dn_prep(qn, kn, vv, bd, small["avec"], small["dvec"])
    dn, o_dn, v_new, states = _dn_scan_fwd(dn_u, dn_w, dn_qg, dn_kd, dn_aq, dn_egl, gate, small["dn_norm"])
    x2, mix = _outproj_fwd(x1, attn, dn, wts["w_out"])
    (dx3, h3, fg2, fu2, loss, d_gf), _ = _ffn_fwd(x2, g2, wts["ffn2_gate"], wts["ffn2_up"], wts["ffn2_down"], "ffn2_fwd",
                                                 head=(gf, target))

    grads = {}
    (dx2, d_g2, dfg2, dfu2, act2, dout2), _ = _ffn_bwd(dx3, x2, g2, fg2, fu2, wts["ffn2_down"], wts["ffn2_gate"],
                                                      wts["ffn2_up"], "ffn2_bwd")
    tk = 2048
    grads["ffn2_gate"], _ = _dw_chunks(dfg2, h3, tk, "dw_ffn2_gate")
    grads["ffn2_up"], _ = _dw_chunks(dfu2, h3, tk, "dw_ffn2_up")
    grads["ffn2_down"], _ = _dw_chunks(act2, dout2, tk, "dw_ffn2_down")
    group_a = ("ffn2_gate", "ffn2_up", "ffn2_down")
    gs_a = [grads[n] for n in group_a]

    (*dviews, ddn, dx2b), swapped_a = _outproj_bwd(dx2, wts["w_out"], attn, comm=_swap_comm(gs_a, True) if dist else None)
    parts_a = _rs_add_pairs(gs_a, swapped_a, dist["c"], "rs_add_pairs_a") if dist else None
    dattn, dd = dviews[0:3], dviews[3:6]
    grads["w_out"] = _matmul_tn(mix, dx2b, D_MODEL, tk, "dw_out").reshape(N_CHIPS, D_MODEL // N_CHIPS, D_MODEL)

    daq, dak, dav = [], [], []
    for p, d in enumerate(DILATIONS):
        daq.append(_attn_bwd_q(aq[p], ak[p], av[p], dattn[p], lse[p], dd[p], d, f"attn_bwd_q_d{d}"))
        dk_p, dv_p = _attn_bwd_kv(aq[p], ak[p], av[p], dattn[p], lse[p], dd[p], d, f"attn_bwd_kv_d{d}")
        dak.append(dk_p)
        dav.append(dv_p)

    do_dn, dvn, dgate, dstates, d_dn_gain = _dn_scan_bwd(dn_w, dn_qg, dn_kd, dn_aq, dn_egl, gate, small["dn_norm"], o_dn, ddn)
    (dqn, dkn, dvv, dbd, dn_small), recv_a = _dn_post(qn, kn, vv, bd, small["avec"], small["dvec"], dn_t, v_new, states,
                                                      dstates, do_dn, dvn, comm=_rsx_comm(parts_a) if dist else None)
    dcq, dck, dcv, dwq, dwk, dwv = _conv_bwd_pre(xq, xk, xv, conv_w, dqn, dkn, dvv)
    dxq, dxk, dxv = _conv_bwd_x(dcq, dck, dcv, conv_w)
    d_conv = jnp.concatenate([dwq[:CONV_WIDTH], dwk[:CONV_WIDTH], dwv[:CONV_WIDTH]], axis=1)

    dx1, d_gm, dproj = _inproj_bwd(dx2, x1, gm, [daq, dak, dav], [dxq, dxk, dxv, dgate], dbd, wts["w_in"])
    gi = _matmul_tn(dproj, h2, IN_COLS_PADDED, 512, "dw_in")
    if dist:
        gate_end = QKV_COLS + DN_WIDTH
        gi = jnp.concatenate([gi[:QKV_COLS], gi[gate_end:gate_end + LOGIT_COLS], gi[QKV_COLS:gate_end]], axis=0)
        gi = gi.reshape(N_CHIPS, IN_COLS // N_CHIPS, D_MODEL)
        gi = jnp.pad(gi, ((0, 0), (0, W_IN_ROWS - IN_COLS // N_CHIPS), (0, 0)))
    grads["w_in"] = gi
    group_b = ("w_in", "w_out")
    parts_b = reduce_start([grads[n] for n in group_b], "b") if dist else None

    (dx0, d_g1, dfg1, dfu1, act1, dout1), recv_b = _ffn_bwd(dx1, x, g1, fg1, fu1, wts["ffn1_down"], wts["ffn1_gate"],
                                                           wts["ffn1_up"], "ffn1_bwd",
                                                           comm=_rsx_comm(parts_b) if dist else None)
    group_c = ("ffn1_gate", "ffn1_up", "ffn1_down")
    pending, parts_c, recv_c = [], [], []
    for n, (lhs, rhs) in zip(group_c, ((dfg1, h1), (dfu1, h1), (act1, dout1))):
        grads[n], landed = _dw_chunks(lhs, rhs, tk, "dw_" + n, comm=_rsx_comm(pending) if pending else None)
        recv_c += list(landed)
        if dist:
            pending = reduce_start([grads[n]], n)
            parts_c += pending

    small_grads = dict(norm_ffn1=d_g1, norm_mix=d_gm, norm_ffn2=d_g2, norm_final=d_gf, conv_w=d_conv,
                       a_log=dn_small[0:1], dt_bias=dn_small[1:2], dn_norm=d_dn_gain[0:1])
    if dist:
        recv_c += _rs_exchange_arrays(pending)
        names = group_a + group_b + group_c
        totals = _rs_add_totals(list(parts_a) + list(parts_b) + list(parts_c), list(recv_a) + list(recv_b) + list(recv_c),
                                dist["chip"])
        theirs = _swap_sibling(totals, False, "rs_share_total")
        grads = {n: (mine, other) for n, mine, other in zip(names, totals, theirs)}
    return loss, dx0, grads, small_grads


HBM =pl.BlockSpec(memory_space=pl.ANY)
VMEM_SPEC = pl.BlockSpec(memory_space=pltpu.VMEM)


def _coords():
    return lax.axis_index("x"), lax.axis_index("y"), lax.axis_index("c")


def _remote(src, dst, send_sems, recv_sems, k, dev):
    return pltpu.make_async_remote_copy(src_ref=src, dst_ref=dst, send_sem=send_sems.at[k], recv_sem=recv_sems.at[k],
                                        device_id=dev, device_id_type=MESH)


def _allreduce_small(buf, name):
    R, Cc = buf.shape

    def body(src_ref, out_ref, recv_ref, send_sems, recv_sems):
        x, y, c = _coords()
        copies = []
        for m in range(1, 8):
            fx, fy, fc = (m >> 2) & 1, (m >> 1) & 1, m & 1
            dev = (x ^ fx if fx else x, y ^ fy if fy else y, c ^ fc if fc else c)
            cp = _remote(src_ref, recv_ref.at[m - 1], send_sems, recv_sems, m - 1, dev)
            cp.start()
            copies.append(cp)
        for cp in copies:
            cp.wait()
        r = [src_ref[...]] + [recv_ref[m] for m in range(7)]
        out_ref[...] = ((r[0] + r[1]) + (r[2] + r[3])) + ((r[4] + r[5]) + (r[6] + r[7]))

    return pl.pallas_call(
        body, name=name, out_shape=jax.ShapeDtypeStruct((R, Cc), F32),
        in_specs=[VMEM_SPEC], out_specs=VMEM_SPEC,
        scratch_shapes=[pltpu.VMEM((7, R, Cc), F32), pltpu.SemaphoreType.DMA((7,)), pltpu.SemaphoreType.DMA((7,))],
    )(buf)


BIG = ("ffn1_gate", "ffn1_up", "ffn1_down", "w_in", "w_out", "ffn2_gate", "ffn2_up", "ffn2_down")
ROW_SHARDED = ("ffn1_down", "w_out", "ffn2_down")
W_IN_ROWS = 960


def _rows(ref, start, size):
    return ref.at[pl.ds(pl.multiple_of(start, 16), size)]


def _allgather_arrays(shards):
    n = len(shards)
    _, shapes, n_sems, start, finish, middle = _ag_comm(shards)

    def body(*refs):
        for phase in (start, middle, finish):
            phase(refs[:n], refs[n:2 * n], refs[2 * n], refs[2 * n + 1])

    return pl.pallas_call(
        body, name="allgather_weights", out_shape=shapes, in_specs=[HBM] * n, out_specs=[HBM] * n,
        scratch_shapes=[pltpu.SemaphoreType.DMA((n_sems,)), pltpu.SemaphoreType.DMA((n_sems,))],
    )(*shards)


def _ag_copies(srcs, outs, send_sems, recv_sems):
    x, y, c = _coords()
    sib = (x, y, 1 - c)
    xn, yn, dg = (1 - x, y), (x, 1 - y), (1 - x, 1 - y)
    plan = []
    for a, (src, out) in enumerate(zip(srcs, outs)):
        h = src.shape[0] // 2
        q = h // 2
        cp = lambda s, d, k, dev: _remote(s, d, send_sems, recv_sems, 8 * a + k, dev)
        slot = lambda chip: out.at[2 * chip[0] + chip[1]]
        mine, dst = _rows(src, c * h, h), _rows(slot((x, y)), c * h, h)
        piece = lambda chip, start, size, k, dev: cp(_rows(slot(chip), start, size), _rows(slot(chip), start, size), k, dev)
        plan.append(dict(
            own=cp(src, slot((x, y)), 6, sib),
            to_x=cp(mine, dst, 0, (*xn, c)), to_y=cp(mine, dst, 1, (*yn, c)),
            from_x=piece(xn, c * h, h, 0, sib), from_y=piece(yn, c * h, h, 1, sib),
            relay_y=piece(xn, c * h, q, 2, (*yn, c)), relay_x=piece(yn, c * h + q, q, 7, (*xn, c)),
            pass_x=piece(xn, c * h, h, 3, sib), pass_y=piece(yn, c * h, h, 4, sib), pass_d=piece(dg, c * h, h, 5, sib),
            diag_1=piece(dg, c * h, q, 2, sib), diag_2=piece(dg, c * h + q, q, 7, sib),
            got=[piece(chip, (1 - c) * h, h, k, sib) for k, chip in ((3, xn), (4, yn), (5, dg))]))
    return plan


def _ag_start(*refs):
    for p in _ag_copies(*refs):
        for k in ("own", "to_x", "to_y"):
            p[k].start()


def _ag_middle(*refs):
    for p in _ag_copies(*refs):
        p["from_x"].wait_recv()
        p["relay_y"].start()
        p["pass_x"].start()
        p["from_y"].wait_recv()
        p["relay_x"].start()
        p["pass_y"].start()


def _ag_finish(*refs):
    plan = _ag_copies(*refs)
    for p in plan:
        p["diag_1"].wait_recv()
        p["diag_2"].wait_recv()
        p["pass_d"].start()
    for p in plan:
        for cp in p["got"]:
            cp.wait_recv()
        p["own"].wait_recv()
        for k in ("own", "to_x", "to_y", "relay_y", "relay_x", "pass_x", "pass_y", "pass_d"):
            p[k].wait_send()


def _ag_comm(shards):
    shapes = [jax.ShapeDtypeStruct((N_CHIPS,) + s.shape, s.dtype) for s in shards]
    return (list(shards), shapes, 8 * len(shards), _ag_start, _ag_finish, _ag_middle)


def _swap_sibling(arrs, pick_other_half, name):
    n = len(arrs)
    _, outs, n_sems, start, finish = _swap_comm(arrs, pick_other_half)

    def body(*refs):
        start(refs[:n], refs[n:2 * n], refs[2 * n], refs[2 * n + 1])
        finish(refs[:n], refs[n:2 * n], refs[2 * n], refs[2 * n + 1])

    return pl.pallas_call(
        body, name=name, out_shape=outs, in_specs=[HBM] * n, out_specs=[HBM] * n,
        scratch_shapes=[pltpu.SemaphoreType.DMA((n_sems,)), pltpu.SemaphoreType.DMA((n_sems,))],
    )(*arrs)


def _swap_comm(arrs, pick_other_half):
    def copies(srcs, dsts, send_sems, recv_sems):
        x, y, c = _coords()
        cps = []
        for a, (src, dst) in enumerate(zip(srcs, dsts)):
            if pick_other_half:
                h = src.shape[1] // 2
                src = src.at[:, pl.ds(pl.multiple_of((1 - c) * h, 16), h)]
            cps.append(_remote(src, dst, send_sems, recv_sems, a, (x, y, 1 - c)))
        return cps

    def start(*refs):
        for cp in copies(*refs):
            cp.start()

    def finish(*refs):
        for cp in copies(*refs):
            cp.wait()

    shapes = [jax.ShapeDtypeStruct((a.shape[0], a.shape[1] // 2) + a.shape[2:] if pick_other_half else a.shape, a.dtype)
              for a in arrs]
    return (list(arrs), shapes, len(arrs), start, finish)


def _rs_add_pairs(gs, others, c, name):
    n = len(gs)
    blocks = [(g.shape[1] // 4, g.shape[2]) for g in gs]

    def body(c_ref, *refs):
        for a in range(n):
            refs[2 * n + a][...] = (refs[a][...] + refs[n + a][...]).astype(BF16)

    mine = lambda b: pl.BlockSpec((None,) + b, lambda j, s, c_ref: (j, c_ref[0] * 2 + s, 0))
    flat = lambda b: pl.BlockSpec((None,) + b, lambda j, s, c_ref: (j, s, 0))
    return pl.pallas_call(
        body, name=name,
        grid_spec=pltpu.PrefetchScalarGridSpec(
            num_scalar_prefetch=1, grid=(N_CHIPS, 2),
            in_specs=[mine(b) for b in blocks] + [flat(b) for b in blocks],
            out_specs=[flat(b) for b in blocks]),
        out_shape=[jax.ShapeDtypeStruct(o.shape, BF16) for o in others],
        compiler_params=_params(2),
    )(c, *gs, *others)


def _rs_exchange_arrays(parts):
    n = len(parts)

    def body(*refs):
        _rsx_start(refs[:n], refs[n:2 * n], refs[2 * n], refs[2 * n + 1])
        _rsx_finish(refs[:n], refs[n:2 * n], refs[2 * n], refs[2 * n + 1])

    _, shapes, n_sems, _, _ = _rsx_comm(parts)
    return pl.pallas_call(
        body, name="rs_exchange_chips", out_shape=shapes, in_specs=[HBM] * n, out_specs=[HBM] * n,
        scratch_shapes=[pltpu.SemaphoreType.DMA((n_sems,)), pltpu.SemaphoreType.DMA((n_sems,))],
    )(*parts)


def _rsx_copies(srcs, dsts, send_sems, recv_sems):
    x, y, c = _coords()
    others = [(1 - x, y), (x, 1 - y), (1 - x, 1 - y)]
    return [_remote(src.at[2 * ox + oy], dst.at[k], send_sems, recv_sems, 3 * a + k, (ox, oy, c))
            for a, (src, dst) in enumerate(zip(srcs, dsts)) for k, (ox, oy) in enumerate(others)]


def _rsx_start(srcs, dsts, send_sems, recv_sems):
    for cp in _rsx_copies(srcs, dsts, send_sems, recv_sems):
        cp.start()


def _rsx_finish(srcs, dsts, send_sems, recv_sems):
    for cp in _rsx_copies(srcs, dsts, send_sems, recv_sems):
        cp.wait()


def _rsx_comm(parts):
    shapes = [jax.ShapeDtypeStruct((3,) + p.shape[1:], p.dtype) for p in parts]
    return (list(parts), shapes, 3 * len(parts), _rsx_start, _rsx_finish)


def _rs_add_totals(parts, recvs, chip):
    n = len(parts)
    blocks = [(p.shape[1] // 2, p.shape[2]) for p in parts]

    def body(chip_ref, *refs):
        f = lambda r: r[...].astype(F32)
        for a in range(n):
            p, r0, r1, r2 = refs[a], refs[n + 3 * a], refs[n + 3 * a + 1], refs[n + 3 * a + 2]
            refs[4 * n + a][...] = (f(p) + f(r0)) + (f(r1) + f(r2))

    own = lambda b: pl.BlockSpec((None,) + b, lambda s, chip_ref: (chip_ref[0], s, 0))
    slot = lambda b, k: pl.BlockSpec((None,) + b, lambda s, chip_ref, k=k: (k, s, 0))
    recv_specs = [slot(b, k) for b in blocks for k in range(3)]
    recv_args = [r for r in recvs for _ in range(3)]
    return pl.pallas_call(
        body, name="rs_add_totals",
        grid_spec=pltpu.PrefetchScalarGridSpec(
            num_scalar_prefetch=1, grid=(2,),
            in_specs=[own(b) for b in blocks] + recv_specs,
            out_specs=[pl.BlockSpec(b, lambda s, chip_ref: (s, 0)) for b in blocks]),
        out_shape=[jax.ShapeDtypeStruct(p.shape[1:], F32) for p in parts],
        compiler_params=_params(1),
    )(chip, *parts, *recv_args)


def _permute_w_in(wt):
    return jnp.concatenate([wt[:QKV_COLS], wt[QKV_COLS + LOGIT_COLS:IN_COLS], wt[QKV_COLS:QKV_COLS + LOGIT_COLS],
                            jnp.zeros((IN_COLS_PADDED - IN_COLS, wt.shape[1]), wt.dtype)], axis=0)


def _pad_row(v):
    v = v.reshape(1, -1)
    return jnp.pad(v, ((0, 0), (0, D_MODEL - v.shape[1])))


def kernel(x, norm_ffn1, ffn1_gate, ffn1_up, ffn1_down, norm_mix, w_in, conv_w, a_log, dt_bias, dn_norm, w_out, norm_ffn2, ffn2_gate, ffn2_up, ffn2_down, norm_final, loss_target, m_norm_ffn1, m_ffn1_gate, m_ffn1_up, m_ffn1_down, m_norm_mix, m_w_in, m_conv_w, m_a_log, m_dt_bias, m_dn_norm, m_w_out, m_norm_ffn2, m_ffn2_gate, m_ffn2_up, m_ffn2_down, m_norm_final, v_norm_ffn1, v_ffn1_gate, v_ffn1_up, v_ffn1_down, v_norm_mix, v_w_in, v_conv_w, v_a_log, v_dt_bias, v_dn_norm, v_w_out, v_norm_ffn2, v_ffn2_gate, v_ffn2_up, v_ffn2_down, v_norm_final):
    cx, cy, cc = _coords()
    chip = 2 * cx + cy
    stored = lambda t, n: t[0] if n in ROW_SHARDED else t[0].T
    big_w = {n: stored(t, n) for n, t in dict(
        ffn1_gate=ffn1_gate, ffn1_up=ffn1_up, ffn1_down=ffn1_down, w_in=w_in, w_out=w_out,
        ffn2_gate=ffn2_gate, ffn2_up=ffn2_up, ffn2_down=ffn2_down).items()}
    big_m = {n: stored(t, n) for n, t in dict(
        ffn1_gate=m_ffn1_gate, ffn1_up=m_ffn1_up, ffn1_down=m_ffn1_down, w_in=m_w_in, w_out=m_w_out,
        ffn2_gate=m_ffn2_gate, ffn2_up=m_ffn2_up, ffn2_down=m_ffn2_down).items()}
    big_v = {n: stored(t, n) for n, t in dict(
        ffn1_gate=v_ffn1_gate, ffn1_up=v_ffn1_up, ffn1_down=v_ffn1_down, w_in=v_w_in, w_out=v_w_out,
        ffn2_gate=v_ffn2_gate, ffn2_up=v_ffn2_up, ffn2_down=v_ffn2_down).items()}

    cols = IN_COLS // N_CHIPS
    send = {n: big_w[n].astype(BF16) for n in BIG}
    send["w_in"] = jnp.pad(send["w_in"], ((0, W_IN_ROWS - cols), (0, 0)))
    early = tuple(n for n in BIG if n not in LATE_WEIGHTS)
    wts = dict(zip(early, _allgather_arrays([send[n] for n in early])))
    dist =dict(late=[send[n] for n in LATE_WEIGHTS], c=cc.reshape(1).astype(jnp.int32),
                chip=chip.reshape(1).astype(jnp.int32))

    conv_shard = conv_w[0]
    emb = jnp.concatenate([jnp.where((chip == j) & (cc == 0), conv_shard, 0.0) for j in range(N_CHIPS)], axis=1)
    emb = jnp.pad(emb.reshape(6, D_MODEL), ((0, 2), (0, 0)))
    conv_full = _allreduce_small(emb, "allgather_conv_w")[:6].reshape(CONV_WIDTH, 3 * DN_WIDTH)

    zvec = jnp.zeros((1, 128), F32)
    small = dict(norm_ffn1=norm_ffn1, norm_mix=norm_mix, norm_ffn2=norm_ffn2, norm_final=norm_final[None],
                 conv_w=conv_full, avec=zvec.at[0, DN_HEADS:2 * DN_HEADS].set(a_log[0]),
                 dvec=zvec.at[0, DN_HEADS:2 * DN_HEADS].set(dt_bias[0]), dn_norm=dn_norm)

    loss, grad_x, reduced, sg = _local_step(x[0], loss_target[0], wts, small, dist)

    rows = [sg["norm_ffn1"], sg["norm_mix"], sg["norm_ffn2"], sg["norm_final"], _pad_row(sg["a_log"]), _pad_row(sg["dt_bias"]),
            _pad_row(sg["dn_norm"]), _pad_row(loss[0:1]), sg["conv_w"].reshape(6, D_MODEL), jnp.zeros((2, D_MODEL), F32)]
    red = _allreduce_small(jnp.concatenate(rows, axis=0), "allreduce_small")
    loss_out = red[7, 0]
    g_conv_full = red[8:14].reshape(CONV_WIDTH, 3 * DN_WIDTH)
    g_conv = lax.dynamic_slice_in_dim(g_conv_full, chip * (3 * DN_WIDTH // N_CHIPS), 3 * DN_WIDTH // N_CHIPS, axis=1)
    g_small = dict(norm_ffn1=red[0:1], norm_mix=red[1:2], norm_ffn2=red[2:3], norm_final=red[3],
                   a_log=red[4:5, DN_HEADS:2 * DN_HEADS], dt_bias=red[5:6, DN_HEADS:2 * DN_HEADS], dn_norm=red[6:7, :DN_HEAD_DIM])

    out_g, out_d, out_m, out_v = {}, {}, {}, {}
    for n in BIG:
        mine, other = reduced[n]
        g = jnp.where(cc == 0, jnp.concatenate([mine, other], axis=0), jnp.concatenate([other, mine], axis=0))
        if n == "w_in":
            to3 = lambda t: jnp.transpose(t, (2, 0, 1))
            g = g[:cols].reshape(cols, 1, D_MODEL)
            results = (g,) + tuple(_adamw(to3(w_in), g, to3(m_w_in), to3(v_w_in), "adamw_w_in"))
            out_g[n], out_d[n], out_m[n], out_v[n] = (jnp.transpose(t, (1, 2, 0)) for t in results)
            continue
        results = (g,) + tuple(_adamw(big_w[n], g, big_m[n], big_v[n], "adamw_" + n))
        out_g[n], out_d[n], out_m[n], out_v[n] = ((t if n in ROW_SHARDED else t.T)[None] for t in results)
    d, nm, nv = _adamw(conv_w[0], g_conv, m_conv_w[0], v_conv_w[0], "adamw_conv_w")
    out_g["conv_w"], out_d["conv_w"], out_m["conv_w"], out_v["conv_w"] = g_conv[None], d[None], nm[None], nv[None]

    small_names = ("norm_ffn1", "norm_mix", "norm_ffn2", "norm_final", "a_log", "dt_bias", "dn_norm")
    small_w = dict(norm_ffn1=norm_ffn1, norm_mix=norm_mix, norm_ffn2=norm_ffn2, norm_final=norm_final, a_log=a_log,
                   dt_bias=dt_bias, dn_norm=dn_norm)
    small_m = dict(norm_ffn1=m_norm_ffn1, norm_mix=m_norm_mix, norm_ffn2=m_norm_ffn2, norm_final=m_norm_final, a_log=m_a_log,
                   dt_bias=m_dt_bias, dn_norm=m_dn_norm)
    small_v = dict(norm_ffn1=v_norm_ffn1, norm_mix=v_norm_mix, norm_ffn2=v_norm_ffn2, norm_final=v_norm_final, a_log=v_a_log,
                   dt_bias=v_dt_bias, dn_norm=v_dn_norm)
    stack = lambda dct: jnp.concatenate([_pad_row(dct[n]) for n in small_names] + [jnp.zeros((1, D_MODEL), F32)], axis=0)
    d, nm, nv = _adamw(stack(small_w), stack(g_small), stack(small_m), stack(small_v), "adamw_small")
    for k, n in enumerate(small_names):
        shape = small_w[n].shape
        size = math.prod(shape)
        out_g[n] = g_small[n].reshape(shape)
        out_d[n], out_m[n], out_v[n] = (t[k, :size].reshape(shape) for t in (d, nm, nv))

    order = ("norm_ffn1", "ffn1_gate", "ffn1_up", "ffn1_down", "norm_mix", "w_in", "conv_w", "a_log", "dt_bias", "dn_norm",
             "w_out", "norm_ffn2", "ffn2_gate", "ffn2_up", "ffn2_down", "norm_final")
    return (loss_out, grad_x[None], *[out_g[n] for n in order], *[out_d[n] for n in order],
            *[out_m[n] for n in order], *[out_v[n] for n in order])
```

```python
import functools
import math

import jax
import jax.numpy as jnp
from jax import lax
from jax.experimental import pallas as pl
from jax.experimental.pallas import tpu as pltpu

F32 = jnp.float32
BF16 = jnp.bfloat16
HI = lax.Precision.HIGH

D_MODEL = 1024
ATTN_HEADS = 8
ATTN_WIDTH = 512
ATTN_BLOCK = 128
ATTN_SCALE = (ATTN_WIDTH // ATTN_HEADS) ** -0.5
DILATIONS = (1, 4, 16)
DN_HEADS = 4
DN_HEAD_DIM = 128
DN_WIDTH = 512
DN_CHUNK = 64
CONV_WIDTH = 4
NORM_EPS = 1e-6
L2_EPS = 1e-6
QKV_COLS = 3 * ATTN_WIDTH + 3 * DN_WIDTH
LOGIT_COLS = 2 * DN_HEADS
IN_COLS = QKV_COLS + LOGIT_COLS + DN_WIDTH
IN_COLS_PADDED = 3712
N_CHIPS = 4

ADAM_LR = 0.001
ADAM_B1 = 0.9
ADAM_B2 = 0.999
ADAM_EPS = 1e-08
ADAM_WD = 0.01
ADAM_STEP = 10

VMEM_LIMIT = 56 * 1024 * 1024
NEG_BIG = -1e30
MESH = pl.DeviceIdType.MESH


def _params(n_grid, vmem=VMEM_LIMIT):
    return pltpu.CompilerParams(dimension_semantics=("arbitrary",) * n_grid, vmem_limit_bytes=vmem)


def _call(body, args, *, name, grid, in_specs, out_specs, out_shape, scratch_shapes=(), comm=None):
    n_in, n_out, n_scr = len(in_specs), len(out_specs), len(scratch_shapes)
    hbm = pl.BlockSpec(memory_space=pl.ANY)
    srcs, dst_shapes, n_sems, start, finish = comm[:5] if comm is not None else ((), (), 0, None, None)
    middle = comm[5] if comm is not None and len(comm) > 5 else None
    ns, nd = len(srcs), len(dst_shapes)

    def full(*refs):
        ins, c_src = refs[:n_in], refs[n_in:n_in + ns]
        at = n_in + ns
        outs, c_dst = refs[at:at + n_out], refs[at + n_out:at + n_out + nd]
        scr = refs[at + n_out + nd:at + n_out + nd + n_scr]
        if comm is not None:
            ids = [pl.program_id(a) for a in range(len(grid))]
            first = functools.reduce(jnp.logical_and, [i == 0 for i in ids])
            last = functools.reduce(jnp.logical_and, [i == g - 1 for i, g in zip(ids, grid)])

            @pl.when(first)
            def _():
                start(c_src, c_dst, refs[-2], refs[-1])

            if middle is not None:
                relay_step = functools.reduce(jnp.logical_and, [ids[0] == (3 * grid[0]) // 4] + [i == 0 for i in ids[1:]])

                @pl.when(relay_step)
                def _():
                    middle(c_src, c_dst, refs[-2], refs[-1])

        body(*ins, *outs, *scr)
        if comm is not None:
            @pl.when(last)
            def _():
                finish(c_src, c_dst, refs[-2], refs[-1])

    sems = [pltpu.SemaphoreType.DMA((n_sems,)), pltpu.SemaphoreType.DMA((n_sems,))] if comm is not None else []
    res = pl.pallas_call(
        full, name=name, grid=grid, in_specs=list(in_specs) + [hbm] * ns, out_specs=list(out_specs) + [hbm] * nd,
        out_shape=list(out_shape) + list(dst_shapes), scratch_shapes=list(scratch_shapes) + sems,
        compiler_params=_params(len(grid)),
    )(*args, *srcs)
    return res[:n_out], res[n_out:]


def _nt(a, b, precision=None):
    return lax.dot_general(a, b, (((1,), (1,)), ((), ())), preferred_element_type=F32, precision=precision)


def _tn(a, b, precision=None):
    return lax.dot_general(a, b, (((0,), (0,)), ((), ())), preferred_element_type=F32, precision=precision)


def _nn(a, b, precision=None):
    return jnp.dot(a, b, preferred_element_type=F32, precision=precision)


def _sigmoid(x):
    return 1.0 / (1.0 + jnp.exp(-x))


def _loss_head(xf, gain, target):
    r = lax.rsqrt(jnp.mean(xf * xf, axis=-1, keepdims=True) + NORM_EPS)
    xhat = xf * r
    err = xhat * gain - target
    part = 0.5 * jnp.sum(jnp.mean(err * err, axis=-1, keepdims=True), axis=0, keepdims=True)
    dy = err * (1.0 / xf.shape[-1])
    dgain = jnp.sum(dy * xhat, axis=0, keepdims=True)
    dxh = dy * gain
    return part, r * (dxh - xhat * jnp.mean(dxh * xhat, axis=-1, keepdims=True)), dgain


def _ffn_fwd(x, gain, wg, wu, wd, name, comm=None, head=None):
    S, D = x.shape
    nf, tf, _ = wg.shape
    tm = 512
    n_in = 5 if head is None else 7

    def body(*refs):
        x_ref, gain_ref, wg_ref, wu_ref, wd_ref = refs[:5]
        xo_ref, h_ref, g_ref, u_ref = refs[n_in:n_in + 4]
        acc_ref, hs_ref = refs[-2:]
        i = pl.program_id(0)
        j = pl.program_id(1)

        @pl.when(j == 0)
        def _():
            xf = x_ref[...]
            r = lax.rsqrt(jnp.mean(xf * xf, axis=-1, keepdims=True) + NORM_EPS)
            h = (xf * r * gain_ref[...]).astype(BF16)
            hs_ref[...] = h
            h_ref[...] = h
            acc_ref[...] = jnp.zeros_like(acc_ref)

        h = hs_ref[...]
        g = _nt(h, wg_ref[...])
        u = _nt(h, wu_ref[...])
        g_ref[...] = g.astype(BF16)
        u_ref[...] = u.astype(BF16)
        act = g * _sigmoid(g) * u
        acc_ref[...] += _nn(act.astype(BF16), wd_ref[...])

        if head is not None:
            hgain_ref, t_ref = refs[5:7]
            loss_ref, dgain_ref = refs[n_in + 4:n_in + 6]

            @pl.when((i == 0) & (j == 0))
            def _():
                loss_ref[...] = jnp.zeros_like(loss_ref)
                dgain_ref[...] = jnp.zeros_like(dgain_ref)

        @pl.when(j == nf - 1)
        def _():
            xo = x_ref[...] + 0.5 * acc_ref[...]
            if head is None:
                xo_ref[...] = xo
            else:
                part, dxo, dgain = _loss_head(xo, hgain_ref[...], t_ref[...])
                first = ((lax.broadcasted_iota(jnp.int32, (8, 128), 0) == 0)
                         & (lax.broadcasted_iota(jnp.int32, (8, 128), 1) == 0))
                loss_ref[...] += jnp.where(first, part, 0.0)
                dgain_ref[...] += dgain
                xo_ref[...] = dxo

    tok = pl.BlockSpec((tm, D), lambda i, j: (i, 0))
    row = pl.BlockSpec((1, D), lambda i, j: (0, 0))
    chunk = pl.BlockSpec((None, tf, D), lambda i, j: (j, 0, 0))
    act = pl.BlockSpec((None, tm, tf), lambda i, j: (j, i, 0))
    extra_in = [] if head is None else [row, tok]
    extra_out = [] if head is None else [pl.BlockSpec((8, 128), lambda i, j: (0, 0)), row]
    extra_shape = [] if head is None else [jax.ShapeDtypeStruct((8, 128), F32), jax.ShapeDtypeStruct((1, D), F32)]
    return _call(
        body, (x, gain, wg, wu, wd) + (() if head is None else tuple(head)), name=name, grid=(S // tm, nf), comm=comm,
        in_specs=[tok, row, chunk, chunk, chunk] + extra_in,
        out_specs=[tok, tok, act, act] + extra_out,
        out_shape=[jax.ShapeDtypeStruct((S, D), F32), jax.ShapeDtypeStruct((S, D), BF16),
                   jax.ShapeDtypeStruct((nf, S, tf), BF16), jax.ShapeDtypeStruct((nf, S, tf), BF16)] + extra_shape,
        scratch_shapes=[pltpu.VMEM((tm, D), F32), pltpu.VMEM((tm, D), BF16)])


def _rmsnorm_bwd(dh, xf, gain):
    r = lax.rsqrt(jnp.mean(xf * xf, axis=-1, keepdims=True) + NORM_EPS)
    xhat = xf * r
    dgain = jnp.sum(dh * xhat, axis=0, keepdims=True)
    dxh = dh * gain
    dx = r * (dxh - xhat * jnp.mean(dxh * xhat, axis=-1, keepdims=True))
    return dx, dgain


def _ffn_bwd(dxo, x, gain, g, u, wd, wg, wu, name, comm=None):
    S, D = x.shape
    nf, _, tf = g.shape
    tm = 512

    def body(dxo_ref, x_ref, gain_ref, g_ref, u_ref, wd_ref, wg_ref, wu_ref,
             dx_ref, dgain_ref, dg_ref, du_ref, act_ref, dout_ref, acc_ref, ds_ref):
        i = pl.program_id(0)
        j = pl.program_id(1)

        @pl.when(j == 0)
        def _():
            d = (0.5 * dxo_ref[...]).astype(BF16)
            ds_ref[...] = d
            dout_ref[...] = d
            acc_ref[...] = jnp.zeros_like(acc_ref)

        @pl.when((i == 0) & (j == 0))
        def _():
            dgain_ref[...] = jnp.zeros_like(dgain_ref)

        for half in range(2):
            rows = slice(half * (tm // 2), (half + 1) * (tm // 2))
            dact = _nt(ds_ref[rows, :], wd_ref[...])
            gv = g_ref[rows, :].astype(F32)
            uv = u_ref[rows, :].astype(F32)
            sg = _sigmoid(gv)
            silu = gv * sg
            act_ref[rows, :] = (silu * uv).astype(BF16)
            dgv = (dact * uv * (sg * (1.0 + gv * (1.0 - sg)))).astype(BF16)
            duv = (dact * silu).astype(BF16)
            dg_ref[rows, :] = dgv
            du_ref[rows, :] = duv
            acc_ref[rows, :] += _nn(dgv, wg_ref[...]) + _nn(duv, wu_ref[...])

        @pl.when(j == nf - 1)
        def _():
            dx, dgain = _rmsnorm_bwd(acc_ref[...], x_ref[...], gain_ref[...])
            dx_ref[...] = dxo_ref[...] + dx
            dgain_ref[...] += dgain

    return _call(
        body, (dxo, x, gain, g, u, wd, wg, wu), name=name, grid=(S // tm, nf), comm=comm,
        in_specs=[pl.BlockSpec((tm, D), lambda i, j: (i, 0)),
                  pl.BlockSpec((tm, D), lambda i, j: (i, 0)),
                  pl.BlockSpec((1, D), lambda i, j: (0, 0)),
                  pl.BlockSpec((None, tm, tf), lambda i, j: (j, i, 0)),
                  pl.BlockSpec((None, tm, tf), lambda i, j: (j, i, 0)),
                  pl.BlockSpec((None, tf, D), lambda i, j: (j, 0, 0)),
                  pl.BlockSpec((None, tf, D), lambda i, j: (j, 0, 0)),
                  pl.BlockSpec((None, tf, D), lambda i, j: (j, 0, 0))],
        out_specs=[pl.BlockSpec((tm, D), lambda i, j: (i, 0)),
                   pl.BlockSpec((1, D), lambda i, j: (0, 0)),
                   pl.BlockSpec((None, tm, tf), lambda i, j: (j, i, 0)),
                   pl.BlockSpec((None, tm, tf), lambda i, j: (j, i, 0)),
                   pl.BlockSpec((None, tm, tf), lambda i, j: (j, i, 0)),
                   pl.BlockSpec((tm, D), lambda i, j: (i, 0))],
        out_shape=[jax.ShapeDtypeStruct((S, D), F32), jax.ShapeDtypeStruct((1, D), F32),
                   jax.ShapeDtypeStruct((nf, S, tf), BF16), jax.ShapeDtypeStruct((nf, S, tf), BF16),
                   jax.ShapeDtypeStruct((nf, S, tf), BF16), jax.ShapeDtypeStruct((S, D), BF16)],
        scratch_shapes=[pltpu.VMEM((tm, D), F32), pltpu.VMEM((tm, D), BF16)])


def _matmul_tn(a, b, tm, tk, name):
    K, M = a.shape
    N = b.shape[1]

    def body(a_ref, b_ref, o_ref):
        @pl.when(pl.program_id(1) == 0)
        def _():
            o_ref[...] = jnp.zeros_like(o_ref)

        o_ref[...] += _tn(a_ref[...], b_ref[...])

    return pl.pallas_call(
        body, name=name, grid=(M // tm, K // tk),
        in_specs=[pl.BlockSpec((tk, tm), lambda i, k: (k, i)),
                  pl.BlockSpec((tk, N), lambda i, k: (k, 0))],
        out_specs=pl.BlockSpec((tm, N), lambda i, k: (i, 0)),
        out_shape=jax.ShapeDtypeStruct((M, N), F32),
        compiler_params=_params(2),
    )(a, b)


def _dw_chunks(a, b, tk, name, comm=None):
    nf, S, tf = a.shape
    N = b.shape[1]

    def body(a_ref, b_ref, o_ref):
        @pl.when(pl.program_id(1) == 0)
        def _():
            o_ref[...] = jnp.zeros_like(o_ref)

        o_ref[...] += _tn(a_ref[...], b_ref[...])

    (out,), landed = _call(
        body, (a, b), name=name, grid=(nf, S // tk), comm=comm,
        in_specs=[pl.BlockSpec((None, tk, tf), lambda j, k: (j, k, 0)),
                  pl.BlockSpec((tk, N), lambda j, k: (k, 0))],
        out_specs=[pl.BlockSpec((None, tf, N), lambda j, k: (j, 0, 0))],
        out_shape=[jax.ShapeDtypeStruct((nf, tf, N), F32)])
    return out, landed


VIEW_TILE = 512


def _view_spec(d, tile=VIEW_TILE):
    return pl.BlockSpec((tile // d, d * ATTN_WIDTH), lambda i: (i, 0))


def _view_shape(S, d, dtype):
    return jax.ShapeDtypeStruct((S // d, d * ATTN_WIDTH), dtype)


def _tile_to_views(val, planes, out_refs):
    for g in range(4):
        planes[g] = val[:, g * 128:(g + 1) * 128]
    for d, ref in zip(DILATIONS, out_refs):
        if d == 1:
            ref[...] = val.astype(ref.dtype)
            continue
        for r in range(d):
            for g in range(4):
                ref[:, r * ATTN_WIDTH + g * 128:r * ATTN_WIDTH + (g + 1) * 128] = (
                    planes[g, pl.ds(r, planes.shape[1] // d, stride=d), :].astype(ref.dtype))


def _view_to_tile(ref, d, planes):
    if d == 1:
        return ref[...].astype(F32)
    for r in range(d):
        for g in range(4):
            planes[g, pl.ds(r, planes.shape[1] // d, stride=d), :] = (
                ref[:, r * ATTN_WIDTH + g * 128:r * ATTN_WIDTH + (g + 1) * 128].astype(F32))
    return jnp.concatenate([planes[g] for g in range(4)], axis=1)


def _inproj_fwd(x, gain, w_in_p):
    S, D = x.shape
    tm = VIEW_TILE
    W = ATTN_WIDTH

    def body(x_ref, gain_ref, w_ref, h_ref, q1, q4, q16, k1, k4, k16, v1, v4, v16, dq_ref, dk_ref, dv_ref, gate_ref, bd_ref,
             planes):
        xf = x_ref[...]
        r = lax.rsqrt(jnp.mean(xf * xf, axis=-1, keepdims=True) + NORM_EPS)
        h = (xf * r * gain_ref[...]).astype(BF16)
        h_ref[...] = h
        _tile_to_views(_nt(h, w_ref[0:W, :]) * ATTN_SCALE, planes, (q1, q4, q16))
        _tile_to_views(_nt(h, w_ref[W:2 * W, :]), planes, (k1, k4, k16))
        _tile_to_views(_nt(h, w_ref[2 * W:3 * W, :]), planes, (v1, v4, v16))
        dq_ref[...] = _nt(h, w_ref[3 * W:4 * W, :])
        dk_ref[...] = _nt(h, w_ref[4 * W:5 * W, :])
        dv_ref[...] = _nt(h, w_ref[5 * W:6 * W, :])
        gate_ref[...] = _nt(h, w_ref[6 * W:7 * W, :])
        bd_ref[...] = _nt(h, w_ref[7 * W:7 * W + 128, :])

    tok = lambda w: pl.BlockSpec((tm, w), lambda i: (i, 0))
    return pl.pallas_call(
        body, name="inproj_fwd", grid=(S // tm,),
        in_specs=[tok(D), pl.BlockSpec((1, D), lambda i: (0, 0)),
                  pl.BlockSpec((IN_COLS_PADDED, D), lambda i: (0, 0))],
        out_specs=[tok(D)] + [_view_spec(d) for d in DILATIONS] * 3 + [tok(W)] * 4 + [tok(128)],
        out_shape=[jax.ShapeDtypeStruct((S, D), BF16)] + [_view_shape(S, d, BF16) for d in DILATIONS] * 3
                  + [jax.ShapeDtypeStruct((S, W), F32)] * 4 + [jax.ShapeDtypeStruct((S, 128), F32)],
        scratch_shapes=[pltpu.VMEM((4, tm, 128), F32)],
        compiler_params=_params(1),
    )(x, gain, w_in_p)


def _inproj_bwd(dxo, x, gain, attn_grads, dsecs, dbd, w_in_p):
    S, D = x.shape
    tm = VIEW_TILE
    W = ATTN_WIDTH

    def body(dxo_ref, x_ref, gain_ref, *rest):
        views, (s3, s4, s5, s6, dbd_ref, w_ref, dx_ref, dgain_ref, dproj_ref, planes) = rest[:9], rest[9:]

        @pl.when(pl.program_id(0) == 0)
        def _():
            dgain_ref[...] = jnp.zeros_like(dgain_ref)

        secs = []
        for k in range(3):
            parts = [_view_to_tile(views[3 * k + p], d, planes) for p, d in enumerate(DILATIONS)]
            secs.append(parts[0] + parts[1] + parts[2])
        secs += [s3[...], s4[...], s5[...], s6[...]]
        dh = jnp.zeros((tm, D), F32)
        for k, s in enumerate(secs):
            d = s.astype(BF16)
            dproj_ref[:, k * W:(k + 1) * W] = d
            dh += _nn(d, w_ref[k * W:(k + 1) * W, :])
        d = dbd_ref[...].astype(BF16)
        dproj_ref[:, 7 * W:7 * W + 128] = d
        dh += _nn(d, w_ref[7 * W:7 * W + 128, :])
        dx, dgain = _rmsnorm_bwd(dh, x_ref[...], gain_ref[...])
        dx_ref[...] = dxo_ref[...] + dx
        dgain_ref[...] += dgain

    tok = lambda w: pl.BlockSpec((tm, w), lambda i: (i, 0))
    return pl.pallas_call(
        body, name="inproj_bwd", grid=(S // tm,),
        in_specs=[tok(D), tok(D), pl.BlockSpec((1, D), lambda i: (0, 0))] + [_view_spec(d, tm) for d in DILATIONS] * 3
                 + [tok(W)] * 4 + [tok(128)] + [pl.BlockSpec((IN_COLS_PADDED, D), lambda i: (0, 0))],
        out_specs=[tok(D), pl.BlockSpec((1, D), lambda i: (0, 0)), tok(IN_COLS_PADDED)],
        out_shape=[jax.ShapeDtypeStruct((S, D), F32), jax.ShapeDtypeStruct((1, D), F32),
                   jax.ShapeDtypeStruct((S, IN_COLS_PADDED), BF16)],
        scratch_shapes=[pltpu.VMEM((4, tm, 128), F32)],
        compiler_params=_params(1),
    )(dxo, x, gain, *[g for grads in attn_grads for g in grads], *dsecs, dbd, w_in_p)


def _slope(h):
    return 2.0 ** (-8.0 * (h + 1) / ATTN_HEADS)


def _head_bias(steps, d, heads=tuple(range(ATTN_HEADS))):
    stepsf = steps.astype(F32)
    return jnp.stack([stepsf * (-_slope(h) * d) for h in heads])


def _hnt(a, b):
    return lax.dot_general(a, b, (((2,), (2,)), ((0,), (0,))), preferred_element_type=F32)


def _hnn(a, b):
    return lax.dot_general(a, b, (((2,), (1,)), ((0,), (0,))), preferred_element_type=F32)


def _blocks_per_step(nb):
    return next(n for n in (4, 2, 1) if nb % n == 0)


def _query_step_specs(qb):
    B = ATTN_BLOCK
    cur = pl.BlockSpec((qb * B, ATTN_WIDTH), lambda r, n: (n, r))
    prev = pl.BlockSpec((B, ATTN_WIDTH), lambda r, n: (jnp.maximum(qb * n - 1, 0), r))
    return cur, prev


def _prev_block(prev_ref, cur_ref, sub, sl):
    B = ATTN_BLOCK
    return prev_ref[:, sl] if sub == 0 else cur_ref[(sub - 1) * B:sub * B, sl]


def _head_cols(tile, lo, big):
    return [_head_col(tile, lo, big), _head_col(tile, jnp.logical_not(lo), big)]


def _attn_fwd(q, k, v, d, name):
    L = q.shape[0]
    nb = L // ATTN_BLOCK
    B = ATTN_BLOCK
    QB = _blocks_per_step(nb)

    def body(q_ref, kp_ref, kc_ref, vp_ref, vc_ref, o_ref, lse_ref):
        n = pl.program_id(1)
        qi = lax.broadcasted_iota(jnp.int32, (B, 2 * B), 0)
        kj = lax.broadcasted_iota(jnp.int32, (B, 2 * B), 1)
        steps = qi + B - kj
        band = (steps >= 0) & (steps <= B)
        lo = lax.broadcasted_iota(jnp.int32, (B, 128), 1) < 64
        bias = _head_bias(steps, d)
        for sub in range(QB):
            rows = slice(sub * B, (sub + 1) * B)
            valid = band & ((kj >= B) | (n > 0)) if sub == 0 else band
            qs, ks, vs = [], [], []
            for G in range(4):
                sl = slice(G * 128, (G + 1) * 128)
                qg = q_ref[rows, sl]
                kg = jnp.concatenate([_prev_block(kp_ref, kc_ref, sub, sl), kc_ref[rows, sl]], axis=0)
                vg = jnp.concatenate([_prev_block(vp_ref, vc_ref, sub, sl), vc_ref[rows, sl]], axis=0)
                qs += [jnp.where(lo, qg, jnp.zeros_like(qg)), jnp.where(lo, jnp.zeros_like(qg), qg)]
                ks += [kg, kg]
                vs += [vg, vg]
            s = jnp.where(valid, _hnt(jnp.stack(qs), jnp.stack(ks)) + bias, NEG_BIG)
            m = jnp.max(s, axis=-1, keepdims=True)
            p = jnp.exp(s - m)
            l = jnp.sum(p, axis=-1, keepdims=True)
            o = _hnn(p.astype(BF16), jnp.stack(vs)) / l
            lse = m + jnp.log(l)
            for G in range(4):
                sl = slice(G * 128, (G + 1) * 128)
                o_ref[rows, sl] = jnp.where(lo, o[2 * G], o[2 * G + 1])
                lse_ref[rows, sl] = jnp.where(lo, lse[2 * G], lse[2 * G + 1])

    cur, prev = _query_step_specs(QB)
    return pl.pallas_call(
        body, name=name, grid=(d, nb // QB),
        in_specs=[cur, prev, cur, prev, cur],
        out_specs=[cur, cur],
        out_shape=[jax.ShapeDtypeStruct((L, d * ATTN_WIDTH), F32)] * 2,
        compiler_params=_params(2),
    )(q, k, k, v, v)


def _attn_merge(parts):
    S = parts[0][0].shape[0]
    tm = VIEW_TILE

    def body(o1, s1, o2, s2, o3, s3, o_ref, lse1, lse4, lse16, planes):
        outs, lses = [], []
        for d, (o, s) in zip(DILATIONS, ((o1, s1), (o2, s2), (o3, s3))):
            outs.append(_view_to_tile(o, d, planes))
            lses.append(_view_to_tile(s, d, planes))
        mx = jnp.maximum(jnp.maximum(lses[0], lses[1]), lses[2])
        es = [jnp.exp(s - mx) for s in lses]
        den = es[0] + es[1] + es[2]
        o_ref[...] = (es[0] * outs[0] + es[1] * outs[1] + es[2] * outs[2]) / den
        _tile_to_views(mx + jnp.log(den), planes, (lse1, lse4, lse16))

    views = [_view_spec(d) for d in DILATIONS]
    flat = [t for p in parts for t in p]
    return pl.pallas_call(
        body, name="attn_merge", grid=(S // tm,),
        in_specs=[views[p] for p in range(3) for _ in range(2)],
        out_specs=[views[0]] + views,
        out_shape=[jax.ShapeDtypeStruct((S, ATTN_WIDTH), F32)] + [_view_shape(S, d, F32) for d in DILATIONS],
        scratch_shapes=[pltpu.VMEM((4, tm, 128), F32)],
        compiler_params=_params(1),
    )(*flat)


def _head_col(t, msk, big):
    if big:
        return jnp.max(jnp.where(msk, t, NEG_BIG), axis=-1, keepdims=True)
    return jnp.sum(jnp.where(msk, t, 0.0), axis=-1, keepdims=True) * (1.0 / 64.0)


def _attn_bwd_q(q, k, v, do, lse, dd, d, name):
    L = q.shape[0]
    nb = L // ATTN_BLOCK
    B = ATTN_BLOCK
    QB = _blocks_per_step(nb)

    def body(q_ref, kp_ref, kc_ref, vp_ref, vc_ref, do_ref, lse_ref, dd_ref, dq_ref):
        n = pl.program_id(1)
        qi = lax.broadcasted_iota(jnp.int32, (B, 2 * B), 0)
        kj = lax.broadcasted_iota(jnp.int32, (B, 2 * B), 1)
        steps = qi + B - kj
        band = (steps >= 0) & (steps <= B)
        lo = lax.broadcasted_iota(jnp.int32, (B, 128), 1) < 64
        bias = _head_bias(steps, d)
        for sub in range(QB):
            rows = slice(sub * B, (sub + 1) * B)
            valid = band & ((kj >= B) | (n > 0)) if sub == 0 else band
            qs, ks, vs, dos, lses, dcols = [], [], [], [], [], []
            for G in range(4):
                sl = slice(G * 128, (G + 1) * 128)
                qg = q_ref[rows, sl]
                kg = jnp.concatenate([_prev_block(kp_ref, kc_ref, sub, sl), kc_ref[rows, sl]], axis=0)
                vg = jnp.concatenate([_prev_block(vp_ref, vc_ref, sub, sl), vc_ref[rows, sl]], axis=0)
                dog = do_ref[rows, sl]
                qs += [jnp.where(lo, qg, jnp.zeros_like(qg)), jnp.where(lo, jnp.zeros_like(qg), qg)]
                dos += [jnp.where(lo, dog, 0.0).astype(BF16), jnp.where(lo, 0.0, dog).astype(BF16)]
                ks += [kg, kg]
                vs += [vg, vg]
                lses += _head_cols(lse_ref[rows, sl], lo, True)
                dcols += _head_cols(dd_ref[rows, sl], lo, False)
            kb = jnp.stack(ks)
            s = _hnt(jnp.stack(qs), kb) + bias
            p = jnp.where(valid, jnp.exp(jnp.where(valid, s, NEG_BIG) - jnp.stack(lses)), 0.0)
            dp = _hnt(jnp.stack(dos), jnp.stack(vs))
            ds = p * (dp - jnp.stack(dcols))
            dq = _hnn(ds.astype(BF16), kb) * ATTN_SCALE
            for G in range(4):
                dq_ref[rows, G * 128:(G + 1) * 128] = jnp.where(lo, dq[2 * G], dq[2 * G + 1]).astype(BF16)

    cur, prev = _query_step_specs(QB)
    return pl.pallas_call(
        body, name=name, grid=(d, nb // QB), in_specs=[cur, prev, cur, prev, cur, cur, cur, cur], out_specs=cur,
        out_shape=jax.ShapeDtypeStruct((L, d * ATTN_WIDTH), BF16), compiler_params=_params(2),
    )(q, k, k, v, v, do, lse, dd)


def _attn_bwd_kv(q, k, v, do, lse, dd, d, name):
    L = q.shape[0]
    nb = L // ATTN_BLOCK
    B = ATTN_BLOCK
    KB = _blocks_per_step(nb)
    n_steps = nb // KB

    def body(k_ref, v_ref, qc_ref, qn_ref, doc_ref, don_ref, lsec_ref, lsen_ref, ddc_ref, ddn_ref, dk_ref, dv_ref):
        j = pl.program_id(1)
        qrow = lax.broadcasted_iota(jnp.int32, (2 * B, B), 0)
        kk = lax.broadcasted_iota(jnp.int32, (2 * B, B), 1)
        steps = qrow - kk
        band = (steps >= 0) & (steps <= B)
        lo2 = lax.broadcasted_iota(jnp.int32, (2 * B, 128), 1) < 64
        lo = lax.broadcasted_iota(jnp.int32, (B, 128), 1) < 64
        stepsf = steps.astype(F32)
        for sub in range(KB):
            rows = slice(sub * B, (sub + 1) * B)
            last = sub == KB - 1
            valid = band & ((qrow < B) | (j < n_steps - 1)) if last else band
            after = lambda cur_ref, nxt_ref, sl: nxt_ref[:, sl] if last else cur_ref[(sub + 1) * B:(sub + 2) * B, sl]
            for G in range(4):
                sl = slice(G * 128, (G + 1) * 128)
                kg = k_ref[rows, sl]
                vg = v_ref[rows, sl]
                qq = jnp.concatenate([qc_ref[rows, sl], after(qc_ref, qn_ref, sl)], axis=0)
                doo = jnp.concatenate([doc_ref[rows, sl], after(doc_ref, don_ref, sl)], axis=0)
                lse2 = jnp.concatenate([lsec_ref[rows, sl], after(lsec_ref, lsen_ref, sl)], axis=0)
                dd2 = jnp.concatenate([ddc_ref[rows, sl], after(ddc_ref, ddn_ref, sl)], axis=0)
                doo_b = doo.astype(BF16)
                dks, dvs = [], []
                for half in (0, 1):
                    msk = lo2 if half == 0 else jnp.logical_not(lo2)
                    qm = jnp.where(msk, qq, jnp.zeros_like(qq))
                    s = _nt(qm, kg) - (_slope(2 * G + half) * d) * stepsf
                    lse_c = _head_col(lse2, msk, True)
                    p = jnp.where(valid, jnp.exp(jnp.where(valid, s, NEG_BIG) - lse_c), 0.0)
                    dvs.append(_tn(p.astype(BF16), doo_b))
                    dom = jnp.where(msk, doo, 0.0).astype(BF16)
                    dp = _nt(dom, vg)
                    dcol = _head_col(dd2, msk, False)
                    ds = p * (dp - dcol)
                    dks.append(_tn(ds.astype(BF16), qq))
                dk_ref[rows, sl] = jnp.where(lo, dks[0], dks[1]).astype(BF16)
                dv_ref[rows, sl] = jnp.where(lo, dvs[0], dvs[1]).astype(BF16)

    cur = pl.BlockSpec((KB * B, ATTN_WIDTH), lambda r, j: (j, r))
    nxt = pl.BlockSpec((B, ATTN_WIDTH), lambda r, j: (jnp.minimum(KB * (j + 1), nb - 1), r))
    return pl.pallas_call(
        body, name=name, grid=(d, n_steps), in_specs=[cur, cur, cur, nxt, cur, nxt, cur, nxt, cur, nxt],
        out_specs=[cur, cur],
        out_shape=[jax.ShapeDtypeStruct((L, d * ATTN_WIDTH), BF16)] * 2, compiler_params=_params(2),
    )(k, v, q, q, do, do, lse, lse, dd, dd)


CONV_T = 512
HALO = 8


def _per_head(head, refs):
    for h in range(DN_HEADS):
        lanes = pl.ds(h * DN_HEAD_DIM, DN_HEAD_DIM)
        head(*[r.at[:, lanes] for r in refs[:-1]], refs[-1])


def _conv_taps(pad_ref, w, T):
    acc = pad_ref[pl.ds(HALO - 3, T), :] * w[0:1, :]
    for j in range(1, CONV_WIDTH):
        acc = acc + pad_ref[pl.ds(HALO - 3 + j, T), :] * w[j:j + 1, :]
    return acc


def _conv_fwd(xq, xk, xv, conv_w):
    S = xq.shape[0]
    T = CONV_T

    def body(*refs):
        _per_head(head, refs)

    def head(xq_ref, xqh_ref, xk_ref, xkh_ref, xv_ref, xvh_ref, wq_ref, wk_ref, wv_ref,
             qn_ref, kn_ref, v_ref, pad_ref):
        i = pl.program_id(0)

        def act(x_ref, xh_ref, w_ref):
            pad_ref[pl.ds(0, HALO), :] = jnp.where(i > 0, xh_ref[...], 0.0)
            pad_ref[pl.ds(HALO, T), :] = x_ref[...]
            c = _conv_taps(pad_ref, w_ref[...], T)
            return c * _sigmoid(c)

        def l2n(t):
            return t * lax.rsqrt(jnp.sum(t * t, axis=-1, keepdims=True) + L2_EPS)

        qn_ref[...] = l2n(act(xq_ref, xqh_ref, wq_ref))
        kn_ref[...] = l2n(act(xk_ref, xkh_ref, wk_ref))
        v_ref[...] = act(xv_ref, xvh_ref, wv_ref)

    tile = pl.BlockSpec((T, DN_WIDTH), lambda i: (i, 0))
    halo = pl.BlockSpec((HALO, DN_WIDTH), lambda i: (jnp.maximum(i * (T // HALO) - 1, 0), 0))
    wspec = lambda sec: pl.BlockSpec((CONV_WIDTH, DN_WIDTH), lambda i, sec=sec: (0, sec))
    return pl.pallas_call(
        body, name="dn_conv_fwd", grid=(S // T,),
        in_specs=[tile, halo, tile, halo, tile, halo, wspec(0), wspec(1), wspec(2)],
        out_specs=[tile, tile, tile],
        out_shape=[jax.ShapeDtypeStruct((S, DN_WIDTH), F32)] * 3,
        scratch_shapes=[pltpu.VMEM((T + HALO, 128), F32)],
        compiler_params=_params(1),
    )(xq, xq, xk, xk, xv, xv, conv_w, conv_w, conv_w)


def _conv_bwd_pre(xq, xk, xv, conv_w, dqn, dkn, dv):
    S = xq.shape[0]
    T = CONV_T

    def body(*refs):
        _per_head(head, refs)

    def head(xq_ref, xqh_ref, xk_ref, xkh_ref, xv_ref, xvh_ref, wq_ref, wk_ref, wv_ref,
             dqn_ref, dkn_ref, dv_ref, dcq_ref, dck_ref, dcv_ref, dwq_ref, dwk_ref, dwv_ref, pad_ref):
        i = pl.program_id(0)

        def one(x_ref, xh_ref, w_ref, dy_ref, dc_ref, dw_ref, normed):
            pad_ref[pl.ds(0, HALO), :] = jnp.where(i > 0, xh_ref[...], 0.0)
            pad_ref[pl.ds(HALO, T), :] = x_ref[...]
            c = _conv_taps(pad_ref, w_ref[...], T)
            sg = _sigmoid(c)
            a = c * sg
            dy = dy_ref[...]
            if normed:
                r = lax.rsqrt(jnp.sum(a * a, axis=-1, keepdims=True) + L2_EPS)
                y = a * r
                da = r * (dy - y * jnp.sum(dy * y, axis=-1, keepdims=True))
            else:
                da = dy
            dc = da * (sg * (1.0 + c * (1.0 - sg)))
            dc_ref[...] = dc

            @pl.when(i == 0)
            def _():
                dw_ref[...] = jnp.zeros_like(dw_ref)

            rows = [jnp.sum(dc * pad_ref[pl.ds(HALO - 3 + j, T), :], axis=0, keepdims=True) for j in range(CONV_WIDTH)]
            dw_ref[...] += jnp.concatenate(rows + [jnp.zeros((8 - CONV_WIDTH, 128), F32)], axis=0)

        one(xq_ref, xqh_ref, wq_ref, dqn_ref, dcq_ref, dwq_ref, True)
        one(xk_ref, xkh_ref, wk_ref, dkn_ref, dck_ref, dwk_ref, True)
        one(xv_ref, xvh_ref, wv_ref, dv_ref, dcv_ref, dwv_ref, False)

    tile = pl.BlockSpec((T, DN_WIDTH), lambda i: (i, 0))
    halo = pl.BlockSpec((HALO, DN_WIDTH), lambda i: (jnp.maximum(i * (T // HALO) - 1, 0), 0))
    wspec = lambda sec: pl.BlockSpec((CONV_WIDTH, DN_WIDTH), lambda i, sec=sec: (0, sec))
    dwspec = pl.BlockSpec((8, DN_WIDTH), lambda i: (0, 0))
    return pl.pallas_call(
        body, name="dn_conv_bwd_pre", grid=(S // T,),
        in_specs=[tile, halo, tile, halo, tile, halo, wspec(0), wspec(1), wspec(2), tile, tile, tile],
        out_specs=[tile, tile, tile, dwspec, dwspec, dwspec],
        out_shape=[jax.ShapeDtypeStruct((S, DN_WIDTH), F32)] * 3 + [jax.ShapeDtypeStruct((8, DN_WIDTH), F32)] * 3,
        scratch_shapes=[pltpu.VMEM((T + HALO, 128), F32)],
        compiler_params=_params(1),
    )(xq, xq, xk, xk, xv, xv, conv_w, conv_w, conv_w, dqn, dkn, dv)


def _conv_bwd_x(dcq, dck, dcv, conv_w):
    S = dcq.shape[0]
    T = CONV_T
    nt = S // T

    def body(*refs):
        _per_head(head, refs)

    def head(dq_ref, dqh_ref, dk_ref, dkh_ref, dv_ref, dvh_ref, wq_ref, wk_ref, wv_ref,
             oq_ref, ok_ref, ov_ref, pad_ref):
        i = pl.program_id(0)

        def one(d_ref, dh_ref, w_ref, o_ref):
            pad_ref[pl.ds(0, T), :] = d_ref[...]
            pad_ref[pl.ds(T, HALO), :] = jnp.where(i < nt - 1, dh_ref[...], 0.0)
            w = w_ref[...]
            acc = pad_ref[pl.ds(3, T), :] * w[0:1, :]
            for j in range(1, CONV_WIDTH):
                acc = acc + pad_ref[pl.ds(3 - j, T), :] * w[j:j + 1, :]
            o_ref[...] = acc

        one(dq_ref, dqh_ref, wq_ref, oq_ref)
        one(dk_ref, dkh_ref, wk_ref, ok_ref)
        one(dv_ref, dvh_ref, wv_ref, ov_ref)

    tile = pl.BlockSpec((T, DN_WIDTH), lambda i: (i, 0))
    halo = pl.BlockSpec((HALO, DN_WIDTH), lambda i: (jnp.minimum((i + 1) * (T // HALO), S // HALO - 1), 0))
    wspec = lambda sec: pl.BlockSpec((CONV_WIDTH, DN_WIDTH), lambda i, sec=sec: (0, sec))
    return pl.pallas_call(
        body, name="dn_conv_bwd_x", grid=(nt,),
        in_specs=[tile, halo, tile, halo, tile, halo, wspec(0), wspec(1), wspec(2)],
        out_specs=[tile, tile, tile],
        out_shape=[jax.ShapeDtypeStruct((S, DN_WIDTH), F32)] * 3,
        scratch_shapes=[pltpu.VMEM((T + HALO, 128), F32)],
        compiler_params=_params(1),
    )(dcq, dcq, dck, dck, dcv, dcv, conv_w, conv_w, conv_w)


PREP_CHUNKS = 4
SCAN_CHUNKS = 8


def _bnn(a, b):
    return lax.dot_general(a, b, (((2,), (1,)), ((0,), (0,))), preferred_element_type=F32, precision=HI)


def _bnt(a, b):
    return lax.dot_general(a, b, (((2,), (2,)), ((0,), (0,))), preferred_element_type=F32, precision=HI)


def _btn(a, b):
    return lax.dot_general(a, b, (((1,), (1,)), ((0,), (0,))), preferred_element_type=F32, precision=HI)


def _tri_inverse_b(a, blk, eye):
    dg = jnp.where(blk, a, 0.0)
    lo = a - dg
    d2 = _bnn(dg, dg)
    d4 = _bnn(d2, d2)
    d8 = _bnn(d4, d4)
    td = _bnn(_bnn(_bnn(eye - dg, eye + d2), eye + d4), eye + d8)
    b = _bnn(td, lo)
    b2 = _bnn(b, b)
    return _bnn(_bnn(eye - b, eye + b2), td)


def _dn_common_b(bds, avec, dvec, q_raw, k, v, t=None):
    C = DN_CHUNK
    lane = lax.broadcasted_iota(jnp.int32, (C, 128), 1)
    row = lax.broadcasted_iota(jnp.int32, (1, C, C), 1)
    col = lax.broadcasted_iota(jnp.int32, (1, C, C), 2)
    incl = row >= col
    strict = row > col
    eye = (row == col).astype(F32)
    blk = (row // 16) == (col // 16)
    pick = lambda tile, ln: jnp.sum(jnp.where(lane == ln, tile, 0.0), axis=-1, keepdims=True)
    betas, graws, zcs = [], [], []
    for bd in bds:
        z = bd + dvec
        g_all = -jnp.exp(avec) * (jnp.maximum(z, 0.0) + jnp.log(1.0 + jnp.exp(-jnp.abs(z))))
        beta_all = _sigmoid(bd)
        for h in range(DN_HEADS):
            betas.append(pick(beta_all, h))
            graws.append(pick(g_all, DN_HEADS + h))
            zcs.append(pick(z, DN_HEADS + h))
    beta, graw, zc = jnp.stack(betas), jnp.stack(graws), jnp.stack(zcs)
    to_row = lambda c: jnp.sum(eye * c, axis=1, keepdims=True)
    gc = jnp.sum(jnp.where(incl, to_row(graw), 0.0), axis=-1, keepdims=True)
    decay = jnp.exp(jnp.where(incl, gc - to_row(gc), NEG_BIG))
    q = q_raw * (DN_HEAD_DIM ** -0.5)
    kb = k * beta
    kk = _bnt(kb, k)
    if t is None:
        t = _tri_inverse_b(jnp.where(strict, kk * decay, 0.0), blk, eye)
    eg = jnp.exp(gc)
    rhs_w = kb * eg
    u = _bnn(t, v * beta)
    w = _bnn(t, rhs_w)
    qk = _bnt(q, k)
    aq = jnp.where(incl, qk * decay, 0.0)
    last = lax.broadcasted_iota(jnp.int32, (1, C, 1), 1) == C - 1
    g_last = jnp.sum(jnp.where(last, gc, 0.0), axis=1, keepdims=True)
    ekd = jnp.exp(g_last - gc)
    return dict(beta=beta, graw=graw, zc=zc, gc=gc, decay=decay, q=q, kb=kb, kk=kk, t=t, eg=eg, rhs_w=rhs_w,
                u=u, w=w, qk=qk, aq=aq, g_last=g_last, ekd=ekd, kd=k * ekd, qg=q * eg,
                incl=incl, strict=strict, eye=eye, lane=lane, row=row, col=col, last=last)


def _stack_heads(ref, rows):
    return jnp.stack([ref[rows, h * DN_HEAD_DIM:(h + 1) * DN_HEAD_DIM] for h in range(DN_HEADS)])


def _stack_units(ref, nc):
    C = DN_CHUNK
    return jnp.concatenate([_stack_heads(ref, slice(ci * C, (ci + 1) * C)) for ci in range(nc)], axis=0)


def _store_units(ref, val, nc):
    C = DN_CHUNK
    for ci in range(nc):
        for h in range(DN_HEADS):
            ref[ci * C:(ci + 1) * C, h * DN_HEAD_DIM:(h + 1) * DN_HEAD_DIM] = val[ci * DN_HEADS + h]


def _dn_prep(qn, kn, v, bd, avec, dvec):
    S = qn.shape[0]
    C = DN_CHUNK
    N = S // C
    nc = PREP_CHUNKS

    def body(q_ref, k_ref, v_ref, bd_ref, a_ref, d_ref, u_ref, w_ref, qg_ref, kd_ref, aq_ref, t_ref, egl_ref):
        bds = [bd_ref[ci * C:(ci + 1) * C, :] for ci in range(nc)]
        c = _dn_common_b(bds, a_ref[...], d_ref[...], _stack_units(q_ref, nc), _stack_units(k_ref, nc), _stack_units(v_ref, nc))
        _store_units(u_ref, c["u"], nc)
        _store_units(w_ref, c["w"], nc)
        _store_units(qg_ref, c["qg"], nc)
        _store_units(kd_ref, c["kd"], nc)
        egl = jnp.broadcast_to(jnp.exp(c["g_last"]), (nc * DN_HEADS, 1, 128))
        for ci in range(nc):
            for h in range(DN_HEADS):
                aq_ref[h, ci * C:(ci + 1) * C, :] = c["aq"][ci * DN_HEADS + h]
                t_ref[h, ci * C:(ci + 1) * C, :] = c["t"][ci * DN_HEADS + h]
            egl_ref[ci * 8:(ci + 1) * 8, :] = jnp.concatenate(
                [egl[ci * DN_HEADS + h] for h in range(DN_HEADS)] + [jnp.zeros((8 - DN_HEADS, 128), F32)], axis=0)

    tok = lambda w: pl.BlockSpec((nc * C, w), lambda n: (n, 0))
    sq = pl.BlockSpec((DN_HEADS, nc * C, C), lambda n: (0, n, 0))
    vec = pl.BlockSpec((1, 128), lambda n: (0, 0))
    return pl.pallas_call(
        body, name="dn_prep", grid=(N // nc,),
        in_specs=[tok(DN_WIDTH)] * 3 + [tok(128), vec, vec],
        out_specs=[tok(DN_WIDTH)] * 4 + [sq, sq, pl.BlockSpec((nc * 8, 128), lambda n: (n, 0))],
        out_shape=[jax.ShapeDtypeStruct((S, DN_WIDTH), F32)] * 4 + [jax.ShapeDtypeStruct((DN_HEADS, S, C), F32)] * 2
                  + [jax.ShapeDtypeStruct((N * 8, 128), F32)],
        compiler_params=_params(1),
    )(qn, kn, v, bd, avec, dvec)


def _dn_scan_fwd(u, w, qg, kd, aq, egl, gate, dn_gain):
    S = u.shape[0]
    C = DN_CHUNK
    N = S // C
    HD = DN_HEAD_DIM
    nc = SCAN_CHUNKS

    def body(u_ref, w_ref, qg_ref, kd_ref, aq_ref, egl_ref, gate_ref, gain_ref, dn_ref, o_ref, vn_ref, st_ref, state_ref):
        @pl.when(pl.program_id(0) == 0)
        def _():
            state_ref[...] = jnp.zeros_like(state_ref)

        gain = gain_ref[...]
        for ci in range(nc):
            rows = slice(ci * C, (ci + 1) * C)
            st = state_ref[...]
            for h in range(DN_HEADS):
                st_ref[ci * DN_WIDTH + h * HD:ci * DN_WIDTH + (h + 1) * HD, :] = st[h]
            v_new = _stack_heads(u_ref, rows) - _bnn(_stack_heads(w_ref, rows), st)
            o = _bnn(_stack_heads(qg_ref, rows), st) + _bnn(aq_ref[:, rows, :], v_new)
            egl = jnp.stack([egl_ref[ci * 8 + h:ci * 8 + h + 1, :] for h in range(DN_HEADS)])
            state_ref[...] = st * egl + _btn(_stack_heads(kd_ref, rows), v_new)
            r = lax.rsqrt(jnp.mean(o * o, axis=-1, keepdims=True) + NORM_EPS)
            gt = _stack_heads(gate_ref, rows)
            dn = o * r * gain * (gt * _sigmoid(gt))
            for h in range(DN_HEADS):
                sl = slice(h * HD, (h + 1) * HD)
                vn_ref[rows, sl] = v_new[h]
                o_ref[rows, sl] = o[h]
                dn_ref[rows, sl] = dn[h]

    tok = lambda wd: pl.BlockSpec((nc * C, wd), lambda n: (n, 0))
    sq = pl.BlockSpec((DN_HEADS, nc * C, C), lambda n: (0, n, 0))
    vec = pl.BlockSpec((1, 128), lambda n: (0, 0))
    return pl.pallas_call(
        body, name="dn_scan_fwd", grid=(N // nc,),
        in_specs=[tok(DN_WIDTH)] * 4 + [sq, pl.BlockSpec((nc * 8, 128), lambda n: (n, 0)), tok(DN_WIDTH), vec],
        out_specs=[tok(DN_WIDTH)] * 3 + [pl.BlockSpec((nc * DN_WIDTH, HD), lambda n: (n, 0))],
        out_shape=[jax.ShapeDtypeStruct((S, DN_WIDTH), F32)] * 3 + [jax.ShapeDtypeStruct((N * DN_WIDTH, HD), F32)],
        scratch_shapes=[pltpu.VMEM((DN_HEADS, HD, HD), F32)],
        compiler_params=_params(1),
    )(u, w, qg, kd, aq, egl, gate, dn_gain)


def _dn_scan_bwd(w, qg, kd, aq, egl, gate, dn_gain, o, ddn):
    S = w.shape[0]
    C = DN_CHUNK
    N = S // C
    HD = DN_HEAD_DIM
    nc = SCAN_CHUNKS

    def body(w_ref, qg_ref, kd_ref, aq_ref, egl_ref, gate_ref, gain_ref, o_ref, ddn_ref,
             do_ref, dvn_ref, dgate_ref, dst_ref, small_ref, dstate_ref):
        @pl.when(pl.program_id(0) == 0)
        def _():
            dstate_ref[...] = jnp.zeros_like(dstate_ref)
            small_ref[...] = jnp.zeros_like(small_ref)

        gain = gain_ref[...]
        d_gain = jnp.zeros((1, 128), F32)
        for ci in reversed(range(nc)):
            rows = slice(ci * C, (ci + 1) * C)
            dsn = dstate_ref[...]
            for h in range(DN_HEADS):
                dst_ref[ci * DN_WIDTH + h * HD:ci * DN_WIDTH + (h + 1) * HD, :] = dsn[h]
            ov = _stack_heads(o_ref, rows)
            r = lax.rsqrt(jnp.mean(ov * ov, axis=-1, keepdims=True) + NORM_EPS)
            on = ov * r
            gt = _stack_heads(gate_ref, rows)
            sgt = _sigmoid(gt)
            silu_g = gt * sgt
            dy = _stack_heads(ddn_ref, rows)
            d_gain = d_gain + jnp.sum(jnp.sum(dy * on * silu_g, axis=1, keepdims=True), axis=0)
            dgate = dy * on * gain * (sgt * (1.0 + gt * (1.0 - sgt)))
            don = dy * gain * silu_g
            do = r * (don - on * jnp.mean(don * on, axis=-1, keepdims=True))
            d_vnew = _btn(aq_ref[:, rows, :], do) + _bnn(_stack_heads(kd_ref, rows), dsn)
            egl = jnp.stack([egl_ref[ci * 8 + h:ci * 8 + h + 1, :] for h in range(DN_HEADS)])
            dstate_ref[...] = _btn(_stack_heads(qg_ref, rows), do) + dsn * egl - _btn(_stack_heads(w_ref, rows), d_vnew)
            for h in range(DN_HEADS):
                sl = slice(h * HD, (h + 1) * HD)
                do_ref[rows, sl] = do[h]
                dvn_ref[rows, sl] = d_vnew[h]
                dgate_ref[rows, sl] = dgate[h]
        small_ref[...] += jnp.concatenate([d_gain, jnp.zeros((7, 128), F32)], axis=0)

    nb = N // nc
    tok = lambda wd: pl.BlockSpec((nc * C, wd), lambda i: (nb - 1 - i, 0))
    sq = pl.BlockSpec((DN_HEADS, nc * C, C), lambda i: (0, nb - 1 - i, 0))
    vec = pl.BlockSpec((1, 128), lambda i: (0, 0))
    return pl.pallas_call(
        body, name="dn_scan_bwd", grid=(nb,),
        in_specs=[tok(DN_WIDTH)] * 3 + [sq, pl.BlockSpec((nc * 8, 128), lambda i: (nb - 1 - i, 0)), tok(DN_WIDTH), vec,
                                       tok(DN_WIDTH), tok(DN_WIDTH)],
        out_specs=[tok(DN_WIDTH)] * 3 + [pl.BlockSpec((nc * DN_WIDTH, HD), lambda i: (nb - 1 - i, 0)),
                                        pl.BlockSpec((8, 128), lambda i: (0, 0))],
        out_shape=[jax.ShapeDtypeStruct((S, DN_WIDTH), F32)] * 3 + [jax.ShapeDtypeStruct((N * DN_WIDTH, HD), F32),
                                                                  jax.ShapeDtypeStruct((8, 128), F32)],
        scratch_shapes=[pltpu.VMEM((DN_HEADS, HD, HD), F32)],
        compiler_params=_params(1),
    )(w, qg, kd, aq, egl, gate, dn_gain, o, ddn)


def _dn_post(qn, kn, v, bd, avec, dvec, t_inv, v_new_all, states, dstates, do_all, dvn_all, comm=None):
    S = qn.shape[0]
    C = DN_CHUNK
    N = S // C
    HD = DN_HEAD_DIM
    nc = PREP_CHUNKS
    B = nc * DN_HEADS

    def body(q_ref, k_ref, v_ref, bd_ref, a_ref, d_ref, t_ref, vn_ref, st_ref, dst_ref, do_ref, dvn_ref,
             dq_ref, dk_ref, dv_ref, dbd_ref, small_ref):
        @pl.when(pl.program_id(0) == 0)
        def _():
            small_ref[...] = jnp.zeros_like(small_ref)

        avec = a_ref[...]
        bds = [bd_ref[ci * C:(ci + 1) * C, :] for ci in range(nc)]
        k = _stack_units(k_ref, nc)
        vv = _stack_units(v_ref, nc)
        t = jnp.concatenate([t_ref[:, ci * C:(ci + 1) * C, :] for ci in range(nc)], axis=0)
        c = _dn_common_b(bds, avec, d_ref[...], _stack_units(q_ref, nc), k, vv, t=t)
        q, kb, eg, u, w = c["q"], c["kb"], c["eg"], c["u"], c["w"]
        beta, decay, incl, strict, eye = c["beta"], c["decay"], c["incl"], c["strict"], c["eye"]
        st = jnp.stack([st_ref[b * HD:(b + 1) * HD, :] for b in range(B)])
        dsn = jnp.stack([dst_ref[b * HD:(b + 1) * HD, :] for b in range(B)])
        v_new = _stack_units(vn_ref, nc)
        do = _stack_units(do_ref, nc)
        d_vnew = _stack_units(dvn_ref, nc)
        egl = jnp.exp(c["g_last"])
        daq = jnp.where(incl, _bnt(do, v_new), 0.0)
        d_qg = _bnt(do, st)
        d_kd = _bnt(v_new, dsn)
        d_glast = jnp.sum(jnp.sum(dsn * st, axis=-1, keepdims=True), axis=1, keepdims=True) * egl
        d_w = -_bnt(d_vnew, st)
        d_ru = _btn(t, d_vnew)
        d_rw = _btn(t, d_w)
        da = -jnp.where(strict, _bnt(d_ru, u) + _bnt(d_rw, w), 0.0)
        dv = d_ru * beta
        dbeta = jnp.sum(d_ru * vv, axis=-1, keepdims=True)
        dkb = d_rw * eg
        dgc = jnp.sum(d_rw * c["rhs_w"], axis=-1, keepdims=True)
        dkk = da * decay
        ddecay = da * c["kk"]
        dkb = dkb + _bnn(dkk, k)
        dk = _btn(dkk, kb)
        dqk = daq * decay
        ddecay = ddecay + daq * c["qk"]
        dq = _bnn(dqk, k)
        dk = dk + _btn(dqk, q)
        m = ddecay * decay
        col_sum = jnp.sum(m, axis=1, keepdims=True)
        dgc = dgc + jnp.sum(m, axis=-1, keepdims=True) - jnp.sum(eye * col_sum, axis=-1, keepdims=True)
        dq = dq + d_qg * eg
        dgc = dgc + jnp.sum(d_qg * c["qg"], axis=-1, keepdims=True)
        dk = dk + d_kd * c["ekd"]
        tk = jnp.sum(d_kd * c["kd"], axis=-1, keepdims=True)
        dgc = dgc - tk
        d_glast = d_glast + jnp.sum(tk, axis=1, keepdims=True)
        dk = dk + dkb * beta
        dbeta = dbeta + jnp.sum(dkb * k, axis=-1, keepdims=True)
        dgc = dgc + jnp.where(c["last"], d_glast, 0.0)
        dgc_row = jnp.sum(eye * dgc, axis=1, keepdims=True)
        dgraw = jnp.sum(jnp.where(c["col"] >= c["row"], dgc_row, 0.0), axis=-1, keepdims=True)
        _store_units(dq_ref, dq * (HD ** -0.5), nc)
        _store_units(dk_ref, dk, nc)
        _store_units(dv_ref, dv, nc)
        dbraw = dbeta * beta * (1.0 - beta)
        dzc = dgraw * _sigmoid(c["zc"])
        ga = dgraw * c["graw"]
        lane = c["lane"]
        lane1 = lax.broadcasted_iota(jnp.int32, (1, 128), 1)
        neg_ea = -jnp.exp(avec)
        d_alog = jnp.zeros((1, 128), F32)
        d_dt = jnp.zeros((1, 128), F32)
        for ci in range(nc):
            dbd = jnp.zeros((C, 128), F32)
            for h in range(DN_HEADS):
                b = ci * DN_HEADS + h
                dz = dzc[b] * neg_ea
                dbd = dbd + jnp.where(lane == h, dbraw[b], 0.0) + jnp.where(lane == DN_HEADS + h, dz, 0.0)
                d_alog = d_alog + jnp.where(lane1 == DN_HEADS + h, jnp.sum(ga[b], axis=0, keepdims=True), 0.0)
                d_dt = d_dt + jnp.where(lane1 == DN_HEADS + h, jnp.sum(dz, axis=0, keepdims=True), 0.0)
            dbd_ref[ci * C:(ci + 1) * C, :] = dbd
        small_ref[...] += jnp.concatenate([d_alog, d_dt, jnp.zeros((6, 128), F32)], axis=0)

    tok = lambda wd: pl.BlockSpec((nc * C, wd), lambda n: (n, 0))
    big = pl.BlockSpec((nc * DN_WIDTH, HD), lambda n: (n, 0))
    sq = pl.BlockSpec((DN_HEADS, nc * C, C), lambda n: (0, n, 0))
    vec = pl.BlockSpec((1, 128), lambda n: (0, 0))
    return _call(
        body, (qn, kn, v, bd, avec, dvec, t_inv, v_new_all, states, dstates, do_all, dvn_all),
        name="dn_post", grid=(N // nc,), comm=comm,
        in_specs=[tok(DN_WIDTH)] * 3 + [tok(128), vec, vec, sq, tok(DN_WIDTH), big, big, tok(DN_WIDTH), tok(DN_WIDTH)],
        out_specs=[tok(DN_WIDTH)] * 3 + [tok(128), pl.BlockSpec((8, 128), lambda n: (0, 0))],
        out_shape=[jax.ShapeDtypeStruct((S, DN_WIDTH), F32)] * 3 + [jax.ShapeDtypeStruct((S, 128), F32),
                                                                  jax.ShapeDtypeStruct((8, 128), F32)])


def _outproj_fwd(x, attn, dn, w_out):
    S, D = x.shape
    tm = 512

    def body(x_ref, a_ref, d_ref, w_ref, xo_ref, mix_ref):
        a = a_ref[...].astype(BF16)
        dd = d_ref[...].astype(BF16)
        mix_ref[:, 0:ATTN_WIDTH] = a
        mix_ref[:, ATTN_WIDTH:] = dd
        xo_ref[...] = x_ref[...] + _nn(a, w_ref[0:ATTN_WIDTH, :]) + _nn(dd, w_ref[ATTN_WIDTH:, :])

    tok = lambda w: pl.BlockSpec((tm, w), lambda i: (i, 0))
    return pl.pallas_call(
        body, name="outproj_fwd", grid=(S // tm,),
        in_specs=[tok(D), tok(ATTN_WIDTH), tok(DN_WIDTH), pl.BlockSpec((D, D), lambda i: (0, 0))],
        out_specs=[tok(D), tok(D)],
        out_shape=[jax.ShapeDtypeStruct((S, D), F32), jax.ShapeDtypeStruct((S, D), BF16)],
        compiler_params=_params(1),
    )(x, attn, dn, w_out)


def _outproj_bwd(dx, w_out, attn, comm=None):
    S, D = dx.shape
    tm = VIEW_TILE

    def body(dx_ref, w_ref, attn_ref, da1, da4, da16, dl1, dl4, dl16, ddn_ref, dxb_ref, planes):
        d = dx_ref[...].astype(BF16)
        dxb_ref[...] = d
        da = _nt(d, w_ref[0:ATTN_WIDTH, :])
        ddn_ref[...] = _nt(d, w_ref[ATTN_WIDTH:, :])
        _tile_to_views(da, planes, (da1, da4, da16))
        lo = lax.broadcasted_iota(jnp.int32, (tm, 128), 1) < 64
        cols = []
        for G in range(4):
            sl = slice(G * 128, (G + 1) * 128)
            t = da[:, sl] * attn_ref[:, sl]
            d0 = jnp.sum(jnp.where(lo, t, 0.0), axis=-1, keepdims=True)
            d1 = jnp.sum(jnp.where(lo, 0.0, t), axis=-1, keepdims=True)
            cols.append(jnp.where(lo, d0, d1))
        _tile_to_views(jnp.concatenate(cols, axis=1), planes, (dl1, dl4, dl16))

    tok = lambda w: pl.BlockSpec((tm, w), lambda i: (i, 0))
    views = [_view_spec(d) for d in DILATIONS]
    return _call(
        body, (dx, w_out, attn), name="outproj_bwd", grid=(S // tm,), comm=comm,
        in_specs=[tok(D), pl.BlockSpec((D, D), lambda i: (0, 0)), tok(ATTN_WIDTH)],
        out_specs=views + views + [tok(DN_WIDTH), tok(D)],
        out_shape=[_view_shape(S, d, F32) for d in DILATIONS] * 2
                  + [jax.ShapeDtypeStruct((S, DN_WIDTH), F32), jax.ShapeDtypeStruct((S, D), BF16)],
        scratch_shapes=[pltpu.VMEM((4, tm, 128), F32)])


def _adamw(w, g, m, v, name):
    R, Ccols = w.shape[0], w.shape[-1]
    tr = next((t for t in range(512, 7, -8) if R % t == 0), R)
    c1 = 1.0 - ADAM_B1 ** ADAM_STEP
    c2 = 1.0 - ADAM_B2 ** ADAM_STEP

    def body(w_ref, g_ref, m_ref, v_ref, d_ref, nm_ref, nv_ref):
        gv = g_ref[...]
        mn = ADAM_B1 * m_ref[...] + (1.0 - ADAM_B1) * gv
        vn = ADAM_B2 * v_ref[...] + (1.0 - ADAM_B2) * (gv * gv)
        nm_ref[...] = mn
        nv_ref[...] = vn
        d_ref[...] = -ADAM_LR * ((mn / c1) / (jnp.sqrt(vn / c2) + ADAM_EPS) + ADAM_WD * w_ref[...])

    if w.ndim == 2:
        grid, spec = (R // tr,), pl.BlockSpec((tr, Ccols), lambda i: (i, 0))
    else:
        grid, spec = (2,), pl.BlockSpec((R // 2, 1, Ccols), lambda i: (i, 0, 0))
    return pl.pallas_call(
        body, name=name, grid=grid, in_specs=[spec] * 4, out_specs=[spec] * 3,
        out_shape=[jax.ShapeDtypeStruct(w.shape, F32)] * 3, compiler_params=_params(1),
    )(w, g, m, v)


LATE_WEIGHTS = ("w_in", "w_out", "ffn2_gate", "ffn2_up", "ffn2_down")


def _local_step(x, target, wts, small, dist=None):
    g1, g2, gm, gf = small["norm_ffn1"], small["norm_ffn2"], small["norm_mix"], small["norm_final"]
    wts = dict(wts)

    def reduce_start(gs, tag):
        return _rs_add_pairs(gs, _swap_sibling(gs, True, "rs_swap_halves_" + tag), dist["c"], "rs_add_pairs_" + tag)

    (x1, h1, fg1, fu1), late = _ffn_fwd(x, g1, wts["ffn1_gate"], wts["ffn1_up"], wts["ffn1_down"], "ffn1_fwd",
                                        comm=_ag_comm(dist["late"]) if dist else None)
    if dist:
        wts.update(zip(LATE_WEIGHTS, late))
        wts["w_out"] = wts["w_out"].reshape(D_MODEL, D_MODEL)
        wts["w_in"] = _permute_w_in(wts["w_in"][:, :IN_COLS // N_CHIPS].reshape(IN_COLS, D_MODEL))
    h2, *qkv, xq, xk, xv, gate, bd = _inproj_fwd(x1, gm, wts["w_in"])
    aq, ak, av = qkv[0:3], qkv[3:6], qkv[6:9]
    parts = [_attn_fwd(aq[p], ak[p], av[p], d, f"attn_fwd_d{d}") for p, d in enumerate(DILATIONS)]
    attn, *lse = _attn_merge(parts)
    conv_w = small["conv_w"]
    qn, kn, vv = _conv_fwd(xq, xk, xv, conv_w)
    dn_u, dn_w, dn_qg, dn_kd, dn_aq, dn_t, dn_egl = _dn_prep(qn, kn, vv, bd, small["avec"], small["dvec"])
    dn, o_dn, v_new, states = _dn_scan_fwd(dn_u, dn_w, dn_qg, dn_kd, dn_aq, dn_egl, gate, small["dn_norm"])
    x2, mix = _outproj_fwd(x1, attn, dn, wts["w_out"])
    (dx3, h3, fg2, fu2, loss, d_gf), _ = _ffn_fwd(x2, g2, wts["ffn2_gate"], wts["ffn2_up"], wts["ffn2_down"], "ffn2_fwd",
                                                 head=(gf, target))

    grads = {}
    (dx2, d_g2, dfg2, dfu2, act2, dout2), _ = _ffn_bwd(dx3, x2, g2, fg2, fu2, wts["ffn2_down"], wts["ffn2_gate"],
                                                      wts["ffn2_up"], "ffn2_bwd")
    tk = 2048
    grads["ffn2_gate"], _ = _dw_chunks(dfg2, h3, tk, "dw_ffn2_gate")
    grads["ffn2_up"], _ = _dw_chunks(dfu2, h3, tk, "dw_ffn2_up")
    grads["ffn2_down"], _ = _dw_chunks(act2, dout2, tk, "dw_ffn2_down")
    group_a = ("ffn2_gate", "ffn2_up", "ffn2_down")
    gs_a = [grads[n] for n in group_a]

    (*dviews, ddn, dx2b), swapped_a = _outproj_bwd(dx2, wts["w_out"], attn, comm=_swap_comm(gs_a, True) if dist else None)
    parts_a = _rs_add_pairs(gs_a, swapped_a, dist["c"], "rs_add_pairs_a") if dist else None
    dattn, dd = dviews[0:3], dviews[3:6]
    grads["w_out"] = _matmul_tn(mix, dx2b, D_MODEL, tk, "dw_out").reshape(N_CHIPS, D_MODEL // N_CHIPS, D_MODEL)

    daq, dak, dav = [], [], []
    for p, d in enumerate(DILATIONS):
        daq.append(_attn_bwd_q(aq[p], ak[p], av[p], dattn[p], lse[p], dd[p], d, f"attn_bwd_q_d{d}"))
        dk_p, dv_p = _attn_bwd_kv(aq[p], ak[p], av[p], dattn[p], lse[p], dd[p], d, f"attn_bwd_kv_d{d}")
        dak.append(dk_p)
        dav.append(dv_p)

    do_dn, dvn, dgate, dstates, d_dn_gain = _dn_scan_bwd(dn_w, dn_qg, dn_kd, dn_aq, dn_egl, gate, small["dn_norm"], o_dn, ddn)
    (dqn, dkn, dvv, dbd, dn_small), recv_a = _dn_post(qn, kn, vv, bd, small["avec"], small["dvec"], dn_t, v_new, states,
                                                      dstates, do_dn, dvn, comm=_rsx_comm(parts_a) if dist else None)
    dcq, dck, dcv, dwq, dwk, dwv = _conv_bwd_pre(xq, xk, xv, conv_w, dqn, dkn, dvv)
    dxq, dxk, dxv = _conv_bwd_x(dcq, dck, dcv, conv_w)
    d_conv = jnp.concatenate([dwq[:CONV_WIDTH], dwk[:CONV_WIDTH], dwv[:CONV_WIDTH]], axis=1)

    dx1, d_gm, dproj = _inproj_bwd(dx2, x1, gm, [daq, dak, dav], [dxq, dxk, dxv, dgate], dbd, wts["w_in"])
    gi = _matmul_tn(dproj, h2, IN_COLS_PADDED, 512, "dw_in")
    if dist:
        gate_end = QKV_COLS + DN_WIDTH
        gi = jnp.concatenate([gi[:QKV_COLS], gi[gate_end:gate_end + LOGIT_COLS], gi[QKV_COLS:gate_end]], axis=0)
        gi = gi.reshape(N_CHIPS, IN_COLS // N_CHIPS, D_MODEL)
        gi = jnp.pad(gi, ((0, 0), (0, W_IN_ROWS - IN_COLS // N_CHIPS), (0, 0)))
    grads["w_in"] = gi
    group_b = ("w_in", "w_out")
    parts_b = reduce_start([grads[n] for n in group_b], "b") if dist else None

    (dx0, d_g1, dfg1, dfu1, act1, dout1), recv_b = _ffn_bwd(dx1, x, g1, fg1, fu1, wts["ffn1_down"], wts["ffn1_gate"],
                                                           wts["ffn1_up"], "ffn1_bwd",
                                                           comm=_rsx_comm(parts_b) if dist else None)
    group_c = ("ffn1_gate", "ffn1_up", "ffn1_down")
    pending, parts_c, recv_c = [], [], []
    for n, (lhs, rhs) in zip(group_c, ((dfg1, h1), (dfu1, h1), (act1, dout1))):
        grads[n], landed = _dw_chunks(lhs, rhs, tk, "dw_" + n, comm=_rsx_comm(pending) if pending else None)
        recv_c += list(landed)
        if dist:
            pending = reduce_start([grads[n]], n)
            parts_c += pending

    small_grads = dict(norm_ffn1=d_g1, norm_mix=d_gm, norm_ffn2=d_g2, norm_final=d_gf, conv_w=d_conv,
                       a_log=dn_small[0:1], dt_bias=dn_small[1:2], dn_norm=d_dn_gain[0:1])
    if dist:
        recv_c += _rs_exchange_arrays(pending)
        names = group_a + group_b + group_c
        totals = _rs_add_totals(list(parts_a) + list(parts_b) + list(parts_c), list(recv_a) + list(recv_b) + list(recv_c),
                                dist["chip"])
        theirs = _swap_sibling(totals, False, "rs_share_total")
        grads = {n: (mine, other) for n, mine, other in zip(names, totals, theirs)}
    return loss, dx0, grads, small_grads


HBM =pl.BlockSpec(memory_space=pl.ANY)
VMEM_SPEC = pl.BlockSpec(memory_space=pltpu.VMEM)


def _coords():
    return lax.axis_index("x"), lax.axis_index("y"), lax.axis_index("c")


def _remote(src, dst, send_sems, recv_sems, k, dev):
    return pltpu.make_async_remote_copy(src_ref=src, dst_ref=dst, send_sem=send_sems.at[k], recv_sem=recv_sems.at[k],
                                        device_id=dev, device_id_type=MESH)


def _allreduce_small(buf, name):
    R, Cc = buf.shape

    def body(src_ref, out_ref, recv_ref, send_sems, recv_sems):
        x, y, c = _coords()
        copies = []
        for m in range(1, 8):
            fx, fy, fc = (m >> 2) & 1, (m >> 1) & 1, m & 1
            dev = (x ^ fx if fx else x, y ^ fy if fy else y, c ^ fc if fc else c)
            cp = _remote(src_ref, recv_ref.at[m - 1], send_sems, recv_sems, m - 1, dev)
            cp.start()
            copies.append(cp)
        for cp in copies:
            cp.wait()
        r = [src_ref[...]] + [recv_ref[m] for m in range(7)]
        out_ref[...] = ((r[0] + r[1]) + (r[2] + r[3])) + ((r[4] + r[5]) + (r[6] + r[7]))

    return pl.pallas_call(
        body, name=name, out_shape=jax.ShapeDtypeStruct((R, Cc), F32),
        in_specs=[VMEM_SPEC], out_specs=VMEM_SPEC,
        scratch_shapes=[pltpu.VMEM((7, R, Cc), F32), pltpu.SemaphoreType.DMA((7,)), pltpu.SemaphoreType.DMA((7,))],
    )(buf)


BIG = ("ffn1_gate", "ffn1_up", "ffn1_down", "w_in", "w_out", "ffn2_gate", "ffn2_up", "ffn2_down")
ROW_SHARDED = ("ffn1_down", "w_out", "ffn2_down")
W_IN_ROWS = 960


def _rows(ref, start, size):
    return ref.at[pl.ds(pl.multiple_of(start, 16), size)]


def _allgather_arrays(shards):
    n = len(shards)
    _, shapes, n_sems, start, finish, middle = _ag_comm(shards)

    def body(*refs):
        for phase in (start, middle, finish):
            phase(refs[:n], refs[n:2 * n], refs[2 * n], refs[2 * n + 1])

    return pl.pallas_call(
        body, name="allgather_weights", out_shape=shapes, in_specs=[HBM] * n, out_specs=[HBM] * n,
        scratch_shapes=[pltpu.SemaphoreType.DMA((n_sems,)), pltpu.SemaphoreType.DMA((n_sems,))],
    )(*shards)


def _ag_copies(srcs, outs, send_sems, recv_sems):
    x, y, c = _coords()
    sib = (x, y, 1 - c)
    xn, yn, dg = (1 - x, y), (x, 1 - y), (1 - x, 1 - y)
    plan = []
    for a, (src, out) in enumerate(zip(srcs, outs)):
        h = src.shape[0] // 2
        q = h // 2
        cp = lambda s, d, k, dev: _remote(s, d, send_sems, recv_sems, 8 * a + k, dev)
        slot = lambda chip: out.at[2 * chip[0] + chip[1]]
        mine, dst = _rows(src, c * h, h), _rows(slot((x, y)), c * h, h)
        piece = lambda chip, start, size, k, dev: cp(_rows(slot(chip), start, size), _rows(slot(chip), start, size), k, dev)
        plan.append(dict(
            own=cp(src, slot((x, y)), 6, sib),
            to_x=cp(mine, dst, 0, (*xn, c)), to_y=cp(mine, dst, 1, (*yn, c)),
            from_x=piece(xn, c * h, h, 0, sib), from_y=piece(yn, c * h, h, 1, sib),
            relay_y=piece(xn, c * h, q, 2, (*yn, c)), relay_x=piece(yn, c * h + q, q, 7, (*xn, c)),
            pass_x=piece(xn, c * h, h, 3, sib), pass_y=piece(yn, c * h, h, 4, sib), pass_d=piece(dg, c * h, h, 5, sib),
            diag_1=piece(dg, c * h, q, 2, sib), diag_2=piece(dg, c * h + q, q, 7, sib),
            got=[piece(chip, (1 - c) * h, h, k, sib) for k, chip in ((3, xn), (4, yn), (5, dg))]))
    return plan


def _ag_start(*refs):
    for p in _ag_copies(*refs):
        for k in ("own", "to_x", "to_y"):
            p[k].start()


def _ag_middle(*refs):
    for p in _ag_copies(*refs):
        p["from_x"].wait_recv()
        p["relay_y"].start()
        p["pass_x"].start()
        p["from_y"].wait_recv()
        p["relay_x"].start()
        p["pass_y"].start()


def _ag_finish(*refs):
    plan = _ag_copies(*refs)
    for p in plan:
        p["diag_1"].wait_recv()
        p["diag_2"].wait_recv()
        p["pass_d"].start()
    for p in plan:
        for cp in p["got"]:
            cp.wait_recv()
        p["own"].wait_recv()
        for k in ("own", "to_x", "to_y", "relay_y", "relay_x", "pass_x", "pass_y", "pass_d"):
            p[k].wait_send()


def _ag_comm(shards):
    shapes = [jax.ShapeDtypeStruct((N_CHIPS,) + s.shape, s.dtype) for s in shards]
    return (list(shards), shapes, 8 * len(shards), _ag_start, _ag_finish, _ag_middle)


def _swap_sibling(arrs, pick_other_half, name):
    n = len(arrs)
    _, outs, n_sems, start, finish = _swap_comm(arrs, pick_other_half)

    def body(*refs):
        start(refs[:n], refs[n:2 * n], refs[2 * n], refs[2 * n + 1])
        finish(refs[:n], refs[n:2 * n], refs[2 * n], refs[2 * n + 1])

    return pl.pallas_call(
        body, name=name, out_shape=outs, in_specs=[HBM] * n, out_specs=[HBM] * n,
        scratch_shapes=[pltpu.SemaphoreType.DMA((n_sems,)), pltpu.SemaphoreType.DMA((n_sems,))],
    )(*arrs)


def _swap_comm(arrs, pick_other_half):
    def copies(srcs, dsts, send_sems, recv_sems):
        x, y, c = _coords()
        cps = []
        for a, (src, dst) in enumerate(zip(srcs, dsts)):
            if pick_other_half:
                h = src.shape[1] // 2
                src = src.at[:, pl.ds(pl.multiple_of((1 - c) * h, 16), h)]
            cps.append(_remote(src, dst, send_sems, recv_sems, a, (x, y, 1 - c)))
        return cps

    def start(*refs):
        for cp in copies(*refs):
            cp.start()

    def finish(*refs):
        for cp in copies(*refs):
            cp.wait()

    shapes = [jax.ShapeDtypeStruct((a.shape[0], a.shape[1] // 2) + a.shape[2:] if pick_other_half else a.shape, a.dtype)
              for a in arrs]
    return (list(arrs), shapes, len(arrs), start, finish)


def _rs_add_pairs(gs, others, c, name):
    n = len(gs)
    blocks = [(g.shape[1] // 4, g.shape[2]) for g in gs]

    def body(c_ref, *refs):
        for a in range(n):
            refs[2 * n + a][...] = (refs[a][...] + refs[n + a][...]).astype(BF16)

    mine = lambda b: pl.BlockSpec((None,) + b, lambda j, s, c_ref: (j, c_ref[0] * 2 + s, 0))
    flat = lambda b: pl.BlockSpec((None,) + b, lambda j, s, c_ref: (j, s, 0))
    return pl.pallas_call(
        body, name=name,
        grid_spec=pltpu.PrefetchScalarGridSpec(
            num_scalar_prefetch=1, grid=(N_CHIPS, 2),
            in_specs=[mine(b) for b in blocks] + [flat(b) for b in blocks],
            out_specs=[flat(b) for b in blocks]),
        out_shape=[jax.ShapeDtypeStruct(o.shape, BF16) for o in others],
        compiler_params=_params(2),
    )(c, *gs, *others)


def _rs_exchange_arrays(parts):
    n = len(parts)

    def body(*refs):
        _rsx_start(refs[:n], refs[n:2 * n], refs[2 * n], refs[2 * n + 1])
        _rsx_finish(refs[:n], refs[n:2 * n], refs[2 * n], refs[2 * n + 1])

    _, shapes, n_sems, _, _ = _rsx_comm(parts)
    return pl.pallas_call(
        body, name="rs_exchange_chips", out_shape=shapes, in_specs=[HBM] * n, out_specs=[HBM] * n,
        scratch_shapes=[pltpu.SemaphoreType.DMA((n_sems,)), pltpu.SemaphoreType.DMA((n_sems,))],
    )(*parts)


def _rsx_copies(srcs, dsts, send_sems, recv_sems):
    x, y, c = _coords()
    others = [(1 - x, y), (x, 1 - y), (1 - x, 1 - y)]
    return [_remote(src.at[2 * ox + oy], dst.at[k], send_sems, recv_sems, 3 * a + k, (ox, oy, c))
            for a, (src, dst) in enumerate(zip(srcs, dsts)) for k, (ox, oy) in enumerate(others)]


def _rsx_start(srcs, dsts, send_sems, recv_sems):
    for cp in _rsx_copies(srcs, dsts, send_sems, recv_sems):
        cp.start()


def _rsx_finish(srcs, dsts, send_sems, recv_sems):
    for cp in _rsx_copies(srcs, dsts, send_sems, recv_sems):
        cp.wait()


def _rsx_comm(parts):
    shapes = [jax.ShapeDtypeStruct((3,) + p.shape[1:], p.dtype) for p in parts]
    return (list(parts), shapes, 3 * len(parts), _rsx_start, _rsx_finish)


def _rs_add_totals(parts, recvs, chip):
    n = len(parts)
    blocks = [(p.shape[1] // 2, p.shape[2]) for p in parts]

    def body(chip_ref, *refs):
        f = lambda r: r[...].astype(F32)
        for a in range(n):
            p, r0, r1, r2 = refs[a], refs[n + 3 * a], refs[n + 3 * a + 1], refs[n + 3 * a + 2]
            refs[4 * n + a][...] = (f(p) + f(r0)) + (f(r1) + f(r2))

    own = lambda b: pl.BlockSpec((None,) + b, lambda s, chip_ref: (chip_ref[0], s, 0))
    slot = lambda b, k: pl.BlockSpec((None,) + b, lambda s, chip_ref, k=k: (k, s, 0))
    recv_specs = [slot(b, k) for b in blocks for k in range(3)]
    recv_args = [r for r in recvs for _ in range(3)]
    return pl.pallas_call(
        body, name="rs_add_totals",
        grid_spec=pltpu.PrefetchScalarGridSpec(
            num_scalar_prefetch=1, grid=(2,),
            in_specs=[own(b) for b in blocks] + recv_specs,
            out_specs=[pl.BlockSpec(b, lambda s, chip_ref: (s, 0)) for b in blocks]),
        out_shape=[jax.ShapeDtypeStruct(p.shape[1:], F32) for p in parts],
        compiler_params=_params(1),
    )(chip, *parts, *recv_args)


def _permute_w_in(wt):
    return jnp.concatenate([wt[:QKV_COLS], wt[QKV_COLS + LOGIT_COLS:IN_COLS], wt[QKV_COLS:QKV_COLS + LOGIT_COLS],
                            jnp.zeros((IN_COLS_PADDED - IN_COLS, wt.shape[1]), wt.dtype)], axis=0)


def _pad_row(v):
    v = v.reshape(1, -1)
    return jnp.pad(v, ((0, 0), (0, D_MODEL - v.shape[1])))


def kernel(x, norm_ffn1, ffn1_gate, ffn1_up, ffn1_down, norm_mix, w_in, conv_w, a_log, dt_bias, dn_norm, w_out, norm_ffn2, ffn2_gate, ffn2_up, ffn2_down, norm_final, loss_target, m_norm_ffn1, m_ffn1_gate, m_ffn1_up, m_ffn1_down, m_norm_mix, m_w_in, m_conv_w, m_a_log, m_dt_bias, m_dn_norm, m_w_out, m_norm_ffn2, m_ffn2_gate, m_ffn2_up, m_ffn2_down, m_norm_final, v_norm_ffn1, v_ffn1_gate, v_ffn1_up, v_ffn1_down, v_norm_mix, v_w_in, v_conv_w, v_a_log, v_dt_bias, v_dn_norm, v_w_out, v_norm_ffn2, v_ffn2_gate, v_ffn2_up, v_ffn2_down, v_norm_final):
    cx, cy, cc = _coords()
    chip = 2 * cx + cy
    stored = lambda t, n: t[0] if n in ROW_SHARDED else t[0].T
    big_w = {n: stored(t, n) for n, t in dict(
        ffn1_gate=ffn1_gate, ffn1_up=ffn1_up, ffn1_down=ffn1_down, w_in=w_in, w_out=w_out,
        ffn2_gate=ffn2_gate, ffn2_up=ffn2_up, ffn2_down=ffn2_down).items()}
    big_m = {n: stored(t, n) for n, t in dict(
        ffn1_gate=m_ffn1_gate, ffn1_up=m_ffn1_up, ffn1_down=m_ffn1_down, w_in=m_w_in, w_out=m_w_out,
        ffn2_gate=m_ffn2_gate, ffn2_up=m_ffn2_up, ffn2_down=m_ffn2_down).items()}
    big_v = {n: stored(t, n) for n, t in dict(
        ffn1_gate=v_ffn1_gate, ffn1_up=v_ffn1_up, ffn1_down=v_ffn1_down, w_in=v_w_in, w_out=v_w_out,
        ffn2_gate=v_ffn2_gate, ffn2_up=v_ffn2_up, ffn2_down=v_ffn2_down).items()}

    cols = IN_COLS // N_CHIPS
    send = {n: big_w[n].astype(BF16) for n in BIG}
    send["w_in"] = jnp.pad(send["w_in"], ((0, W_IN_ROWS - cols), (0, 0)))
    early = tuple(n for n in BIG if n not in LATE_WEIGHTS)
    wts = dict(zip(early, _allgather_arrays([send[n] for n in early])))
    dist =dict(late=[send[n] for n in LATE_WEIGHTS], c=cc.reshape(1).astype(jnp.int32),
                chip=chip.reshape(1).astype(jnp.int32))

    conv_shard = conv_w[0]
    emb = jnp.concatenate([jnp.where((chip == j) & (cc == 0), conv_shard, 0.0) for j in range(N_CHIPS)], axis=1)
    emb = jnp.pad(emb.reshape(6, D_MODEL), ((0, 2), (0, 0)))
    conv_full = _allreduce_small(emb, "allgather_conv_w")[:6].reshape(CONV_WIDTH, 3 * DN_WIDTH)

    zvec = jnp.zeros((1, 128), F32)
    small = dict(norm_ffn1=norm_ffn1, norm_mix=norm_mix, norm_ffn2=norm_ffn2, norm_final=norm_final[None],
                 conv_w=conv_full, avec=zvec.at[0, DN_HEADS:2 * DN_HEADS].set(a_log[0]),
                 dvec=zvec.at[0, DN_HEADS:2 * DN_HEADS].set(dt_bias[0]), dn_norm=dn_norm)

    loss, grad_x, reduced, sg = _local_step(x[0], loss_target[0], wts, small, dist)

    rows = [sg["norm_ffn1"], sg["norm_mix"], sg["norm_ffn2"], sg["norm_final"], _pad_row(sg["a_log"]), _pad_row(sg["dt_bias"]),
            _pad_row(sg["dn_norm"]), _pad_row(loss[0:1]), sg["conv_w"].reshape(6, D_MODEL), jnp.zeros((2, D_MODEL), F32)]
    red = _allreduce_small(jnp.concatenate(rows, axis=0), "allreduce_small")
    loss_out = red[7, 0]
    g_conv_full = red[8:14].reshape(CONV_WIDTH, 3 * DN_WIDTH)
    g_conv = lax.dynamic_slice_in_dim(g_conv_full, chip * (3 * DN_WIDTH // N_CHIPS), 3 * DN_WIDTH // N_CHIPS, axis=1)
    g_small = dict(norm_ffn1=red[0:1], norm_mix=red[1:2], norm_ffn2=red[2:3], norm_final=red[3],
                   a_log=red[4:5, DN_HEADS:2 * DN_HEADS], dt_bias=red[5:6, DN_HEADS:2 * DN_HEADS], dn_norm=red[6:7, :DN_HEAD_DIM])

    out_g, out_d, out_m, out_v = {}, {}, {}, {}
    for n in BIG:
        mine, other = reduced[n]
        g = jnp.where(cc == 0, jnp.concatenate([mine, other], axis=0), jnp.concatenate([other, mine], axis=0))
        if n == "w_in":
            to3 = lambda t: jnp.transpose(t, (2, 0, 1))
            g = g[:cols].reshape(cols, 1, D_MODEL)
            results = (g,) + tuple(_adamw(to3(w_in), g, to3(m_w_in), to3(v_w_in), "adamw_w_in"))
            out_g[n], out_d[n], out_m[n], out_v[n] = (jnp.transpose(t, (1, 2, 0)) for t in results)
            continue
        results = (g,) + tuple(_adamw(big_w[n], g, big_m[n], big_v[n], "adamw_" + n))
        out_g[n], out_d[n], out_m[n], out_v[n] = ((t if n in ROW_SHARDED else t.T)[None] for t in results)
    d, nm, nv = _adamw(conv_w[0], g_conv, m_conv_w[0], v_conv_w[0], "adamw_conv_w")
    out_g["conv_w"], out_d["conv_w"], out_m["conv_w"], out_v["conv_w"] = g_conv[None], d[None], nm[None], nv[None]

    small_names = ("norm_ffn1", "norm_mix", "norm_ffn2", "norm_final", "a_log", "dt_bias", "dn_norm")
    small_w = dict(norm_ffn1=norm_ffn1, norm_mix=norm_mix, norm_ffn2=norm_ffn2, norm_final=norm_final, a_log=a_log,
                   dt_bias=dt_bias, dn_norm=dn_norm)
    small_m = dict(norm_ffn1=m_norm_ffn1, norm_mix=m_norm_mix, norm_ffn2=m_norm_ffn2, norm_final=m_norm_final, a_log=m_a_log,
                   dt_bias=m_dt_bias, dn_norm=m_dn_norm)
    small_v = dict(norm_ffn1=v_norm_ffn1, norm_mix=v_norm_mix, norm_ffn2=v_norm_ffn2, norm_final=v_norm_final, a_log=v_a_log,
                   dt_bias=v_dt_bias, dn_norm=v_dn_norm)
    stack = lambda dct: jnp.concatenate([_pad_row(dct[n]) for n in small_names] + [jnp.zeros((1, D_MODEL), F32)], axis=0)
    d, nm, nv = _adamw(stack(small_w), stack(g_small), stack(small_m), stack(small_v), "adamw_small")
    for k, n in enumerate(small_names):
        shape = small_w[n].shape
        size = math.prod(shape)
        out_g[n] = g_small[n].reshape(shape)
        out_d[n], out_m[n], out_v[n] = (t[k, :size].reshape(shape) for t in (d, nm, nv))

    order = ("norm_ffn1", "ffn1_gate", "ffn1_up", "ffn1_down", "norm_mix", "w_in", "conv_w", "a_log", "dt_bias", "dn_norm",
             "w_out", "norm_ffn2", "ffn2_gate", "ffn2_up", "ffn2_down", "norm_final")
    return (loss_out, grad_x[None], *[out_g[n] for n in order], *[out_d[n] for n in order],
            *[out_m[n] for n in order], *[out_v[n] for n in order])
```

```python
import functools
import math

import jax
import jax.numpy as jnp
from jax import lax
from jax.experimental import pallas as pl
from jax.experimental.pallas import tpu as pltpu

F32 = jnp.float32
BF16 = jnp.bfloat16
HI = lax.Precision.HIGH

D_MODEL = 1024
ATTN_HEADS = 8
ATTN_WIDTH = 512
ATTN_BLOCK = 128
ATTN_SCALE = (ATTN_WIDTH // ATTN_HEADS) ** -0.5
DILATIONS = (1, 4, 16)
DN_HEADS = 4
DN_HEAD_DIM = 128
DN_WIDTH = 512
DN_CHUNK = 64
CONV_WIDTH = 4
NORM_EPS = 1e-6
L2_EPS = 1e-6
QKV_COLS = 3 * ATTN_WIDTH + 3 * DN_WIDTH
LOGIT_COLS = 2 * DN_HEADS
IN_COLS = QKV_COLS + LOGIT_COLS + DN_WIDTH
IN_COLS_PADDED = 3712
N_CHIPS = 4

ADAM_LR = 0.001
ADAM_B1 = 0.9
ADAM_B2 = 0.999
ADAM_EPS = 1e-08
ADAM_WD = 0.01
ADAM_STEP = 10

VMEM_LIMIT = 56 * 1024 * 1024
NEG_BIG = -1e30
MESH = pl.DeviceIdType.MESH


def _params(n_grid, vmem=VMEM_LIMIT):
    return pltpu.CompilerParams(dimension_semantics=("arbitrary",) * n_grid, vmem_limit_bytes=vmem)


def _call(body, args, *, name, grid, in_specs, out_specs, out_shape, scratch_shapes=(), comm=None):
    n_in, n_out, n_scr = len(in_specs), len(out_specs), len(scratch_shapes)
    hbm = pl.BlockSpec(memory_space=pl.ANY)
    srcs, dst_shapes, n_sems, start, finish = comm[:5] if comm is not None else ((), (), 0, None, None)
    middle = comm[5] if comm is not None and len(comm) > 5 else None
    ns, nd = len(srcs), len(dst_shapes)

    def full(*refs):
        ins, c_src = refs[:n_in], refs[n_in:n_in + ns]
        at = n_in + ns
        outs, c_dst = refs[at:at + n_out], refs[at + n_out:at + n_out + nd]
        scr = refs[at + n_out + nd:at + n_out + nd + n_scr]
        if comm is not None:
            ids = [pl.program_id(a) for a in range(len(grid))]
            first = functools.reduce(jnp.logical_and, [i == 0 for i in ids])
            last = functools.reduce(jnp.logical_and, [i == g - 1 for i, g in zip(ids, grid)])

            @pl.when(first)
            def _():
                start(c_src, c_dst, refs[-2], refs[-1])

            if middle is not None:
                relay_step = functools.reduce(jnp.logical_and, [ids[0] == (5 * grid[0]) // 8] + [i == 0 for i in ids[1:]])

                @pl.when(relay_step)
                def _():
                    middle(c_src, c_dst, refs[-2], refs[-1])

        body(*ins, *outs, *scr)
        if comm is not None:
            @pl.when(last)
            def _():
                finish(c_src, c_dst, refs[-2], refs[-1])

    sems = [pltpu.SemaphoreType.DMA((n_sems,)), pltpu.SemaphoreType.DMA((n_sems,))] if comm is not None else []
    res = pl.pallas_call(
        full, name=name, grid=grid, in_specs=list(in_specs) + [hbm] * ns, out_specs=list(out_specs) + [hbm] * nd,
        out_shape=list(out_shape) + list(dst_shapes), scratch_shapes=list(scratch_shapes) + sems,
        compiler_params=_params(len(grid)),
    )(*args, *srcs)
    return res[:n_out], res[n_out:]


def _nt(a, b, precision=None):
    return lax.dot_general(a, b, (((1,), (1,)), ((), ())), preferred_element_type=F32, precision=precision)


def _tn(a, b, precision=None):
    return lax.dot_general(a, b, (((0,), (0,)), ((), ())), preferred_element_type=F32, precision=precision)


def _nn(a, b, precision=None):
    return jnp.dot(a, b, preferred_element_type=F32, precision=precision)


def _sigmoid(x):
    return 1.0 / (1.0 + jnp.exp(-x))


def _loss_head(xf, gain, target):
    r = lax.rsqrt(jnp.mean(xf * xf, axis=-1, keepdims=True) + NORM_EPS)
    xhat = xf * r
    err = xhat * gain - target
    part = 0.5 * jnp.sum(jnp.mean(err * err, axis=-1, keepdims=True), axis=0, keepdims=True)
    dy = err * (1.0 / xf.shape[-1])
    dgain = jnp.sum(dy * xhat, axis=0, keepdims=True)
    dxh = dy * gain
    return part, r * (dxh - xhat * jnp.mean(dxh * xhat, axis=-1, keepdims=True)), dgain


def _ffn_fwd(x, gain, wg, wu, wd, name, comm=None, head=None):
    S, D = x.shape
    nf, tf, _ = wg.shape
    tm = 512
    n_in = 5 if head is None else 7

    def body(*refs):
        x_ref, gain_ref, wg_ref, wu_ref, wd_ref = refs[:5]
        xo_ref, h_ref, g_ref, u_ref = refs[n_in:n_in + 4]
        acc_ref, hs_ref = refs[-2:]
        i = pl.program_id(0)
        j = pl.program_id(1)

        @pl.when(j == 0)
        def _():
            xf = x_ref[...]
            r = lax.rsqrt(jnp.mean(xf * xf, axis=-1, keepdims=True) + NORM_EPS)
            h = (xf * r * gain_ref[...]).astype(BF16)
            hs_ref[...] = h
            h_ref[...] = h
            acc_ref[...] = jnp.zeros_like(acc_ref)

        h = hs_ref[...]
        g = _nt(h, wg_ref[...])
        u = _nt(h, wu_ref[...])
        g_ref[...] = g.astype(BF16)
        u_ref[...] = u.astype(BF16)
        act = g * _sigmoid(g) * u
        acc_ref[...] += _nn(act.astype(BF16), wd_ref[...])

        if head is not None:
            hgain_ref, t_ref = refs[5:7]
            loss_ref, dgain_ref = refs[n_in + 4:n_in + 6]

            @pl.when((i == 0) & (j == 0))
            def _():
                loss_ref[...] = jnp.zeros_like(loss_ref)
                dgain_ref[...] = jnp.zeros_like(dgain_ref)

        @pl.when(j == nf - 1)
        def _():
            xo = x_ref[...] + 0.5 * acc_ref[...]
            if head is None:
                xo_ref[...] = xo
            else:
                part, dxo, dgain = _loss_head(xo, hgain_ref[...], t_ref[...])
                first = ((lax.broadcasted_iota(jnp.int32, (8, 128), 0) == 0)
                         & (lax.broadcasted_iota(jnp.int32, (8, 128), 1) == 0))
                loss_ref[...] += jnp.where(first, part, 0.0)
                dgain_ref[...] += dgain
                xo_ref[...] = dxo

    tok = pl.BlockSpec((tm, D), lambda i, j: (i, 0))
    row = pl.BlockSpec((1, D), lambda i, j: (0, 0))
    chunk = pl.BlockSpec((None, tf, D), lambda i, j: (j, 0, 0))
    act = pl.BlockSpec((None, tm, tf), lambda i, j: (j, i, 0))
    extra_in = [] if head is None else [row, tok]
    extra_out = [] if head is None else [pl.BlockSpec((8, 128), lambda i, j: (0, 0)), row]
    extra_shape = [] if head is None else [jax.ShapeDtypeStruct((8, 128), F32), jax.ShapeDtypeStruct((1, D), F32)]
    return _call(
        body, (x, gain, wg, wu, wd) + (() if head is None else tuple(head)), name=name, grid=(S // tm, nf), comm=comm,
        in_specs=[tok, row, chunk, chunk, chunk] + extra_in,
        out_specs=[tok, tok, act, act] + extra_out,
        out_shape=[jax.ShapeDtypeStruct((S, D), F32), jax.ShapeDtypeStruct((S, D), BF16),
                   jax.ShapeDtypeStruct((nf, S, tf), BF16), jax.ShapeDtypeStruct((nf, S, tf), BF16)] + extra_shape,
        scratch_shapes=[pltpu.VMEM((tm, D), F32), pltpu.VMEM((tm, D), BF16)])


def _rmsnorm_bwd(dh, xf, gain):
    r = lax.rsqrt(jnp.mean(xf * xf, axis=-1, keepdims=True) + NORM_EPS)
    xhat = xf * r
    dgain = jnp.sum(dh * xhat, axis=0, keepdims=True)
    dxh = dh * gain
    dx = r * (dxh - xhat * jnp.mean(dxh * xhat, axis=-1, keepdims=True))
    return dx, dgain


def _ffn_bwd(dxo, x, gain, g, u, wd, wg, wu, name, comm=None):
    S, D = x.shape
    nf, _, tf = g.shape
    tm = 512

    def body(dxo_ref, x_ref, gain_ref, g_ref, u_ref, wd_ref, wg_ref, wu_ref,
             dx_ref, dgain_ref, dg_ref, du_ref, act_ref, dout_ref, acc_ref, ds_ref):
        i = pl.program_id(0)
        j = pl.program_id(1)

        @pl.when(j == 0)
        def _():
            d = (0.5 * dxo_ref[...]).astype(BF16)
            ds_ref[...] = d
            dout_ref[...] = d
            acc_ref[...] = jnp.zeros_like(acc_ref)

        @pl.when((i == 0) & (j == 0))
        def _():
            dgain_ref[...] = jnp.zeros_like(dgain_ref)

        for half in range(2):
            rows = slice(half * (tm // 2), (half + 1) * (tm // 2))
            dact = _nt(ds_ref[rows, :], wd_ref[...])
            gv = g_ref[rows, :].astype(F32)
            uv = u_ref[rows, :].astype(F32)
            sg = _sigmoid(gv)
            silu = gv * sg
            act_ref[rows, :] = (silu * uv).astype(BF16)
            dgv = (dact * uv * (sg * (1.0 + gv * (1.0 - sg)))).astype(BF16)
            duv = (dact * silu).astype(BF16)
            dg_ref[rows, :] = dgv
            du_ref[rows, :] = duv
            acc_ref[rows, :] += _nn(dgv, wg_ref[...]) + _nn(duv, wu_ref[...])

        @pl.when(j == nf - 1)
        def _():
            dx, dgain = _rmsnorm_bwd(acc_ref[...], x_ref[...], gain_ref[...])
            dx_ref[...] = dxo_ref[...] + dx
            dgain_ref[...] += dgain

    return _call(
        body, (dxo, x, gain, g, u, wd, wg, wu), name=name, grid=(S // tm, nf), comm=comm,
        in_specs=[pl.BlockSpec((tm, D), lambda i, j: (i, 0)),
                  pl.BlockSpec((tm, D), lambda i, j: (i, 0)),
                  pl.BlockSpec((1, D), lambda i, j: (0, 0)),
                  pl.BlockSpec((None, tm, tf), lambda i, j: (j, i, 0)),
                  pl.BlockSpec((None, tm, tf), lambda i, j: (j, i, 0)),
                  pl.BlockSpec((None, tf, D), lambda i, j: (j, 0, 0)),
                  pl.BlockSpec((None, tf, D), lambda i, j: (j, 0, 0)),
                  pl.BlockSpec((None, tf, D), lambda i, j: (j, 0, 0))],
        out_specs=[pl.BlockSpec((tm, D), lambda i, j: (i, 0)),
                   pl.BlockSpec((1, D), lambda i, j: (0, 0)),
                   pl.BlockSpec((None, tm, tf), lambda i, j: (j, i, 0)),
                   pl.BlockSpec((None, tm, tf), lambda i, j: (j, i, 0)),
                   pl.BlockSpec((None, tm, tf), lambda i, j: (j, i, 0)),
                   pl.BlockSpec((tm, D), lambda i, j: (i, 0))],
        out_shape=[jax.ShapeDtypeStruct((S, D), F32), jax.ShapeDtypeStruct((1, D), F32),
                   jax.ShapeDtypeStruct((nf, S, tf), BF16), jax.ShapeDtypeStruct((nf, S, tf), BF16),
                   jax.ShapeDtypeStruct((nf, S, tf), BF16), jax.ShapeDtypeStruct((S, D), BF16)],
        scratch_shapes=[pltpu.VMEM((tm, D), F32), pltpu.VMEM((tm, D), BF16)])


def _matmul_tn(a, b, tm, tk, name):
    K, M = a.shape
    N = b.shape[1]

    def body(a_ref, b_ref, o_ref):
        @pl.when(pl.program_id(1) == 0)
        def _():
            o_ref[...] = jnp.zeros_like(o_ref)

        o_ref[...] += _tn(a_ref[...], b_ref[...])

    return pl.pallas_call(
        body, name=name, grid=(M // tm, K // tk),
        in_specs=[pl.BlockSpec((tk, tm), lambda i, k: (k, i)),
                  pl.BlockSpec((tk, N), lambda i, k: (k, 0))],
        out_specs=pl.BlockSpec((tm, N), lambda i, k: (i, 0)),
        out_shape=jax.ShapeDtypeStruct((M, N), F32),
        compiler_params=_params(2),
    )(a, b)


def _dw_chunks(a, b, tk, name, comm=None):
    nf, S, tf = a.shape
    N = b.shape[1]

    def body(a_ref, b_ref, o_ref):
        @pl.when(pl.program_id(1) == 0)
        def _():
            o_ref[...] = jnp.zeros_like(o_ref)

        o_ref[...] += _tn(a_ref[...], b_ref[...])

    (out,), landed = _call(
        body, (a, b), name=name, grid=(nf, S // tk), comm=comm,
        in_specs=[pl.BlockSpec((None, tk, tf), lambda j, k: (j, k, 0)),
                  pl.BlockSpec((tk, N), lambda j, k: (k, 0))],
        out_specs=[pl.BlockSpec((None, tf, N), lambda j, k: (j, 0, 0))],
        out_shape=[jax.ShapeDtypeStruct((nf, tf, N), F32)])
    return out, landed


VIEW_TILE = 512


def _view_spec(d, tile=VIEW_TILE):
    return pl.BlockSpec((tile // d, d * ATTN_WIDTH), lambda i: (i, 0))


def _view_shape(S, d, dtype):
    return jax.ShapeDtypeStruct((S // d, d * ATTN_WIDTH), dtype)


def _tile_to_views(val, planes, out_refs):
    for g in range(4):
        planes[g] = val[:, g * 128:(g + 1) * 128]
    for d, ref in zip(DILATIONS, out_refs):
        if d == 1:
            ref[...] = val.astype(ref.dtype)
            continue
        for r in range(d):
            for g in range(4):
                ref[:, r * ATTN_WIDTH + g * 128:r * ATTN_WIDTH + (g + 1) * 128] = (
                    planes[g, pl.ds(r, planes.shape[1] // d, stride=d), :].astype(ref.dtype))


def _view_to_tile(ref, d, planes):
    if d == 1:
        return ref[...].astype(F32)
    for r in range(d):
        for g in range(4):
            planes[g, pl.ds(r, planes.shape[1] // d, stride=d), :] = (
                ref[:, r * ATTN_WIDTH + g * 128:r * ATTN_WIDTH + (g + 1) * 128].astype(F32))
    return jnp.concatenate([planes[g] for g in range(4)], axis=1)


def _inproj_fwd(x, gain, w_in_p):
    S, D = x.shape
    tm = VIEW_TILE
    W = ATTN_WIDTH

    def body(x_ref, gain_ref, w_ref, h_ref, q1, q4, q16, k1, k4, k16, v1, v4, v16, dq_ref, dk_ref, dv_ref, gate_ref, bd_ref,
             planes):
        xf = x_ref[...]
        r = lax.rsqrt(jnp.mean(xf * xf, axis=-1, keepdims=True) + NORM_EPS)
        h = (xf * r * gain_ref[...]).astype(BF16)
        h_ref[...] = h
        _tile_to_views(_nt(h, w_ref[0:W, :]) * ATTN_SCALE, planes, (q1, q4, q16))
        _tile_to_views(_nt(h, w_ref[W:2 * W, :]), planes, (k1, k4, k16))
        _tile_to_views(_nt(h, w_ref[2 * W:3 * W, :]), planes, (v1, v4, v16))
        dq_ref[...] = _nt(h, w_ref[3 * W:4 * W, :])
        dk_ref[...] = _nt(h, w_ref[4 * W:5 * W, :])
        dv_ref[...] = _nt(h, w_ref[5 * W:6 * W, :])
        gate_ref[...] = _nt(h, w_ref[6 * W:7 * W, :])
        bd_ref[...] = _nt(h, w_ref[7 * W:7 * W + 128, :])

    tok = lambda w: pl.BlockSpec((tm, w), lambda i: (i, 0))
    return pl.pallas_call(
        body, name="inproj_fwd", grid=(S // tm,),
        in_specs=[tok(D), pl.BlockSpec((1, D), lambda i: (0, 0)),
                  pl.BlockSpec((IN_COLS_PADDED, D), lambda i: (0, 0))],
        out_specs=[tok(D)] + [_view_spec(d) for d in DILATIONS] * 3 + [tok(W)] * 4 + [tok(128)],
        out_shape=[jax.ShapeDtypeStruct((S, D), BF16)] + [_view_shape(S, d, BF16) for d in DILATIONS] * 3
                  + [jax.ShapeDtypeStruct((S, W), F32)] * 4 + [jax.ShapeDtypeStruct((S, 128), F32)],
        scratch_shapes=[pltpu.VMEM((4, tm, 128), F32)],
        compiler_params=_params(1),
    )(x, gain, w_in_p)


def _inproj_bwd(dxo, x, gain, attn_grads, dsecs, dbd, w_in_p):
    S, D = x.shape
    tm = VIEW_TILE
    W = ATTN_WIDTH

    def body(dxo_ref, x_ref, gain_ref, *rest):
        views, (s3, s4, s5, s6, dbd_ref, w_ref, dx_ref, dgain_ref, dproj_ref, planes) = rest[:9], rest[9:]

        @pl.when(pl.program_id(0) == 0)
        def _():
            dgain_ref[...] = jnp.zeros_like(dgain_ref)

        secs = []
        for k in range(3):
            parts = [_view_to_tile(views[3 * k + p], d, planes) for p, d in enumerate(DILATIONS)]
            secs.append(parts[0] + parts[1] + parts[2])
        secs += [s3[...], s4[...], s5[...], s6[...]]
        dh = jnp.zeros((tm, D), F32)
        for k, s in enumerate(secs):
            d = s.astype(BF16)
            dproj_ref[:, k * W:(k + 1) * W] = d
            dh += _nn(d, w_ref[k * W:(k + 1) * W, :])
        d = dbd_ref[...].astype(BF16)
        dproj_ref[:, 7 * W:7 * W + 128] = d
        dh += _nn(d, w_ref[7 * W:7 * W + 128, :])
        dx, dgain = _rmsnorm_bwd(dh, x_ref[...], gain_ref[...])
        dx_ref[...] = dxo_ref[...] + dx
        dgain_ref[...] += dgain

    tok = lambda w: pl.BlockSpec((tm, w), lambda i: (i, 0))
    return pl.pallas_call(
        body, name="inproj_bwd", grid=(S // tm,),
        in_specs=[tok(D), tok(D), pl.BlockSpec((1, D), lambda i: (0, 0))] + [_view_spec(d, tm) for d in DILATIONS] * 3
                 + [tok(W)] * 4 + [tok(128)] + [pl.BlockSpec((IN_COLS_PADDED, D), lambda i: (0, 0))],
        out_specs=[tok(D), pl.BlockSpec((1, D), lambda i: (0, 0)), tok(IN_COLS_PADDED)],
        out_shape=[jax.ShapeDtypeStruct((S, D), F32), jax.ShapeDtypeStruct((1, D), F32),
                   jax.ShapeDtypeStruct((S, IN_COLS_PADDED), BF16)],
        scratch_shapes=[pltpu.VMEM((4, tm, 128), F32)],
        compiler_params=_params(1),
    )(dxo, x, gain, *[g for grads in attn_grads for g in grads], *dsecs, dbd, w_in_p)


def _slope(h):
    return 2.0 ** (-8.0 * (h + 1) / ATTN_HEADS)


def _head_bias(steps, d, heads=tuple(range(ATTN_HEADS))):
    stepsf = steps.astype(F32)
    return jnp.stack([stepsf * (-_slope(h) * d) for h in heads])


def _hnt(a, b):
    return lax.dot_general(a, b, (((2,), (2,)), ((0,), (0,))), preferred_element_type=F32)


def _hnn(a, b):
    return lax.dot_general(a, b, (((2,), (1,)), ((0,), (0,))), preferred_element_type=F32)


def _blocks_per_step(nb):
    return next(n for n in (4, 2, 1) if nb % n == 0)


def _query_step_specs(qb):
    B = ATTN_BLOCK
    cur = pl.BlockSpec((qb * B, ATTN_WIDTH), lambda r, n: (n, r))
    prev = pl.BlockSpec((B, ATTN_WIDTH), lambda r, n: (jnp.maximum(qb * n - 1, 0), r))
    return cur, prev


def _prev_block(prev_ref, cur_ref, sub, sl):
    B = ATTN_BLOCK
    return prev_ref[:, sl] if sub == 0 else cur_ref[(sub - 1) * B:sub * B, sl]


def _head_cols(tile, lo, big):
    return [_head_col(tile, lo, big), _head_col(tile, jnp.logical_not(lo), big)]


def _attn_fwd(q, k, v, d, name):
    L = q.shape[0]
    nb = L // ATTN_BLOCK
    B = ATTN_BLOCK
    QB = _blocks_per_step(nb)

    def body(q_ref, kp_ref, kc_ref, vp_ref, vc_ref, o_ref, lse_ref):
        n = pl.program_id(1)
        qi = lax.broadcasted_iota(jnp.int32, (B, 2 * B), 0)
        kj = lax.broadcasted_iota(jnp.int32, (B, 2 * B), 1)
        steps = qi + B - kj
        band = (steps >= 0) & (steps <= B)
        lo = lax.broadcasted_iota(jnp.int32, (B, 128), 1) < 64
        bias = _head_bias(steps, d)
        for sub in range(QB):
            rows = slice(sub * B, (sub + 1) * B)
            valid = band & ((kj >= B) | (n > 0)) if sub == 0 else band
            qs, ks, vs = [], [], []
            for G in range(4):
                sl = slice(G * 128, (G + 1) * 128)
                qg = q_ref[rows, sl]
                kg = jnp.concatenate([_prev_block(kp_ref, kc_ref, sub, sl), kc_ref[rows, sl]], axis=0)
                vg = jnp.concatenate([_prev_block(vp_ref, vc_ref, sub, sl), vc_ref[rows, sl]], axis=0)
                qs += [jnp.where(lo, qg, jnp.zeros_like(qg)), jnp.where(lo, jnp.zeros_like(qg), qg)]
                ks += [kg, kg]
                vs += [vg, vg]
            s = jnp.where(valid, _hnt(jnp.stack(qs), jnp.stack(ks)) + bias, NEG_BIG)
            m = jnp.max(s, axis=-1, keepdims=True)
            p = jnp.exp(s - m)
            l = jnp.sum(p, axis=-1, keepdims=True)
            o = _hnn(p.astype(BF16), jnp.stack(vs)) / l
            lse = m + jnp.log(l)
            for G in range(4):
                sl = slice(G * 128, (G + 1) * 128)
                o_ref[rows, sl] = jnp.where(lo, o[2 * G], o[2 * G + 1])
                lse_ref[rows, sl] = jnp.where(lo, lse[2 * G], lse[2 * G + 1])

    cur, prev = _query_step_specs(QB)
    return pl.pallas_call(
        body, name=name, grid=(d, nb // QB),
        in_specs=[cur, prev, cur, prev, cur],
        out_specs=[cur, cur],
        out_shape=[jax.ShapeDtypeStruct((L, d * ATTN_WIDTH), F32)] * 2,
        compiler_params=_params(2),
    )(q, k, k, v, v)


def _attn_merge(parts):
    S = parts[0][0].shape[0]
    tm = VIEW_TILE

    def body(o1, s1, o2, s2, o3, s3, o_ref, lse1, lse4, lse16, planes):
        outs, lses = [], []
        for d, (o, s) in zip(DILATIONS, ((o1, s1), (o2, s2), (o3, s3))):
            outs.append(_view_to_tile(o, d, planes))
            lses.append(_view_to_tile(s, d, planes))
        mx = jnp.maximum(jnp.maximum(lses[0], lses[1]), lses[2])
        es = [jnp.exp(s - mx) for s in lses]
        den = es[0] + es[1] + es[2]
        o_ref[...] = (es[0] * outs[0] + es[1] * outs[1] + es[2] * outs[2]) / den
        _tile_to_views(mx + jnp.log(den), planes, (lse1, lse4, lse16))

    views = [_view_spec(d) for d in DILATIONS]
    flat = [t for p in parts for t in p]
    return pl.pallas_call(
        body, name="attn_merge", grid=(S // tm,),
        in_specs=[views[p] for p in range(3) for _ in range(2)],
        out_specs=[views[0]] + views,
        out_shape=[jax.ShapeDtypeStruct((S, ATTN_WIDTH), F32)] + [_view_shape(S, d, F32) for d in DILATIONS],
        scratch_shapes=[pltpu.VMEM((4, tm, 128), F32)],
        compiler_params=_params(1),
    )(*flat)


def _head_col(t, msk, big):
    if big:
        return jnp.max(jnp.where(msk, t, NEG_BIG), axis=-1, keepdims=True)
    return jnp.sum(jnp.where(msk, t, 0.0), axis=-1, keepdims=True) * (1.0 / 64.0)


def _attn_bwd_q(q, k, v, do, lse, dd, d, name):
    L = q.shape[0]
    nb = L // ATTN_BLOCK
    B = ATTN_BLOCK
    QB = _blocks_per_step(nb)

    def body(q_ref, kp_ref, kc_ref, vp_ref, vc_ref, do_ref, lse_ref, dd_ref, dq_ref):
        n = pl.program_id(1)
        qi = lax.broadcasted_iota(jnp.int32, (B, 2 * B), 0)
        kj = lax.broadcasted_iota(jnp.int32, (B, 2 * B), 1)
        steps = qi + B - kj
        band = (steps >= 0) & (steps <= B)
        lo = lax.broadcasted_iota(jnp.int32, (B, 128), 1) < 64
        bias = _head_bias(steps, d)
        for sub in range(QB):
            rows = slice(sub * B, (sub + 1) * B)
            valid = band & ((kj >= B) | (n > 0)) if sub == 0 else band
            qs, ks, vs, dos, lses, dcols = [], [], [], [], [], []
            for G in range(4):
                sl = slice(G * 128, (G + 1) * 128)
                qg = q_ref[rows, sl]
                kg = jnp.concatenate([_prev_block(kp_ref, kc_ref, sub, sl), kc_ref[rows, sl]], axis=0)
                vg = jnp.concatenate([_prev_block(vp_ref, vc_ref, sub, sl), vc_ref[rows, sl]], axis=0)
                dog = do_ref[rows, sl]
                qs += [jnp.where(lo, qg, jnp.zeros_like(qg)), jnp.where(lo, jnp.zeros_like(qg), qg)]
                dos += [jnp.where(lo, dog, 0.0).astype(BF16), jnp.where(lo, 0.0, dog).astype(BF16)]
                ks += [kg, kg]
                vs += [vg, vg]
                lses += _head_cols(lse_ref[rows, sl], lo, True)
                dcols += _head_cols(dd_ref[rows, sl], lo, False)
            kb = jnp.stack(ks)
            s = _hnt(jnp.stack(qs), kb) + bias
            p = jnp.where(valid, jnp.exp(jnp.where(valid, s, NEG_BIG) - jnp.stack(lses)), 0.0)
            dp = _hnt(jnp.stack(dos), jnp.stack(vs))
            ds = p * (dp - jnp.stack(dcols))
            dq = _hnn(ds.astype(BF16), kb) * ATTN_SCALE
            for G in range(4):
                dq_ref[rows, G * 128:(G + 1) * 128] = jnp.where(lo, dq[2 * G], dq[2 * G + 1]).astype(BF16)

    cur, prev = _query_step_specs(QB)
    return pl.pallas_call(
        body, name=name, grid=(d, nb // QB), in_specs=[cur, prev, cur, prev, cur, cur, cur, cur], out_specs=cur,
        out_shape=jax.ShapeDtypeStruct((L, d * ATTN_WIDTH), BF16), compiler_params=_params(2),
    )(q, k, k, v, v, do, lse, dd)


def _attn_bwd_kv(q, k, v, do, lse, dd, d, name):
    L = q.shape[0]
    nb = L // ATTN_BLOCK
    B = ATTN_BLOCK
    KB = _blocks_per_step(nb)
    n_steps = nb // KB

    def body(k_ref, v_ref, qc_ref, qn_ref, doc_ref, don_ref, lsec_ref, lsen_ref, ddc_ref, ddn_ref, dk_ref, dv_ref):
        j = pl.program_id(1)
        qrow = lax.broadcasted_iota(jnp.int32, (2 * B, B), 0)
        kk = lax.broadcasted_iota(jnp.int32, (2 * B, B), 1)
        steps = qrow - kk
        band = (steps >= 0) & (steps <= B)
        lo2 = lax.broadcasted_iota(jnp.int32, (2 * B, 128), 1) < 64
        lo = lax.broadcasted_iota(jnp.int32, (B, 128), 1) < 64
        stepsf = steps.astype(F32)
        for sub in range(KB):
            rows = slice(sub * B, (sub + 1) * B)
            last = sub == KB - 1
            valid = band & ((qrow < B) | (j < n_steps - 1)) if last else band
            after = lambda cur_ref, nxt_ref, sl: nxt_ref[:, sl] if last else cur_ref[(sub + 1) * B:(sub + 2) * B, sl]
            for G in range(4):
                sl = slice(G * 128, (G + 1) * 128)
                kg = k_ref[rows, sl]
                vg = v_ref[rows, sl]
                qq = jnp.concatenate([qc_ref[rows, sl], after(qc_ref, qn_ref, sl)], axis=0)
                doo = jnp.concatenate([doc_ref[rows, sl], after(doc_ref, don_ref, sl)], axis=0)
                lse2 = jnp.concatenate([lsec_ref[rows, sl], after(lsec_ref, lsen_ref, sl)], axis=0)
                dd2 = jnp.concatenate([ddc_ref[rows, sl], after(ddc_ref, ddn_ref, sl)], axis=0)
                doo_b = doo.astype(BF16)
                dks, dvs = [], []
                for half in (0, 1):
                    msk = lo2 if half == 0 else jnp.logical_not(lo2)
                    qm = jnp.where(msk, qq, jnp.zeros_like(qq))
                    s = _nt(qm, kg) - (_slope(2 * G + half) * d) * stepsf
                    lse_c = _head_col(lse2, msk, True)
                    p = jnp.where(valid, jnp.exp(jnp.where(valid, s, NEG_BIG) - lse_c), 0.0)
                    dvs.append(_tn(p.astype(BF16), doo_b))
                    dom = jnp.where(msk, doo, 0.0).astype(BF16)
                    dp = _nt(dom, vg)
                    dcol = _head_col(dd2, msk, False)
                    ds = p * (dp - dcol)
                    dks.append(_tn(ds.astype(BF16), qq))
                dk_ref[rows, sl] = jnp.where(lo, dks[0], dks[1]).astype(BF16)
                dv_ref[rows, sl] = jnp.where(lo, dvs[0], dvs[1]).astype(BF16)

    cur = pl.BlockSpec((KB * B, ATTN_WIDTH), lambda r, j: (j, r))
    nxt = pl.BlockSpec((B, ATTN_WIDTH), lambda r, j: (jnp.minimum(KB * (j + 1), nb - 1), r))
    return pl.pallas_call(
        body, name=name, grid=(d, n_steps), in_specs=[cur, cur, cur, nxt, cur, nxt, cur, nxt, cur, nxt],
        out_specs=[cur, cur],
        out_shape=[jax.ShapeDtypeStruct((L, d * ATTN_WIDTH), BF16)] * 2, compiler_params=_params(2),
    )(k, v, q, q, do, do, lse, lse, dd, dd)


CONV_T = 512
HALO = 8


def _per_head(head, refs):
    for h in range(DN_HEADS):
        lanes = pl.ds(h * DN_HEAD_DIM, DN_HEAD_DIM)
        head(*[r.at[:, lanes] for r in refs[:-1]], refs[-1])


def _conv_taps(pad_ref, w, T):
    acc = pad_ref[pl.ds(HALO - 3, T), :] * w[0:1, :]
    for j in range(1, CONV_WIDTH):
        acc = acc + pad_ref[pl.ds(HALO - 3 + j, T), :] * w[j:j + 1, :]
    return acc


def _conv_fwd(xq, xk, xv, conv_w):
    S = xq.shape[0]
    T = CONV_T

    def body(*refs):
        _per_head(head, refs)

    def head(xq_ref, xqh_ref, xk_ref, xkh_ref, xv_ref, xvh_ref, wq_ref, wk_ref, wv_ref,
             qn_ref, kn_ref, v_ref, pad_ref):
        i = pl.program_id(0)

        def act(x_ref, xh_ref, w_ref):
            pad_ref[pl.ds(0, HALO), :] = jnp.where(i > 0, xh_ref[...], 0.0)
            pad_ref[pl.ds(HALO, T), :] = x_ref[...]
            c = _conv_taps(pad_ref, w_ref[...], T)
            return c * _sigmoid(c)

        def l2n(t):
            return t * lax.rsqrt(jnp.sum(t * t, axis=-1, keepdims=True) + L2_EPS)

        qn_ref[...] = l2n(act(xq_ref, xqh_ref, wq_ref))
        kn_ref[...] = l2n(act(xk_ref, xkh_ref, wk_ref))
        v_ref[...] = act(xv_ref, xvh_ref, wv_ref)

    tile = pl.BlockSpec((T, DN_WIDTH), lambda i: (i, 0))
    halo = pl.BlockSpec((HALO, DN_WIDTH), lambda i: (jnp.maximum(i * (T // HALO) - 1, 0), 0))
    wspec = lambda sec: pl.BlockSpec((CONV_WIDTH, DN_WIDTH), lambda i, sec=sec: (0, sec))
    return pl.pallas_call(
        body, name="dn_conv_fwd", grid=(S // T,),
        in_specs=[tile, halo, tile, halo, tile, halo, wspec(0), wspec(1), wspec(2)],
        out_specs=[tile, tile, tile],
        out_shape=[jax.ShapeDtypeStruct((S, DN_WIDTH), F32)] * 3,
        scratch_shapes=[pltpu.VMEM((T + HALO, 128), F32)],
        compiler_params=_params(1),
    )(xq, xq, xk, xk, xv, xv, conv_w, conv_w, conv_w)


def _conv_bwd_pre(xq, xk, xv, conv_w, dqn, dkn, dv):
    S = xq.shape[0]
    T = CONV_T

    def body(*refs):
        _per_head(head, refs)

    def head(xq_ref, xqh_ref, xk_ref, xkh_ref, xv_ref, xvh_ref, wq_ref, wk_ref, wv_ref,
             dqn_ref, dkn_ref, dv_ref, dcq_ref, dck_ref, dcv_ref, dwq_ref, dwk_ref, dwv_ref, pad_ref):
        i = pl.program_id(0)

        def one(x_ref, xh_ref, w_ref, dy_ref, dc_ref, dw_ref, normed):
            pad_ref[pl.ds(0, HALO), :] = jnp.where(i > 0, xh_ref[...], 0.0)
            pad_ref[pl.ds(HALO, T), :] = x_ref[...]
            c = _conv_taps(pad_ref, w_ref[...], T)
            sg = _sigmoid(c)
            a = c * sg
            dy = dy_ref[...]
            if normed:
                r = lax.rsqrt(jnp.sum(a * a, axis=-1, keepdims=True) + L2_EPS)
                y = a * r
                da = r * (dy - y * jnp.sum(dy * y, axis=-1, keepdims=True))
            else:
                da = dy
            dc = da * (sg * (1.0 + c * (1.0 - sg)))
            dc_ref[...] = dc

            @pl.when(i == 0)
            def _():
                dw_ref[...] = jnp.zeros_like(dw_ref)

            rows = [jnp.sum(dc * pad_ref[pl.ds(HALO - 3 + j, T), :], axis=0, keepdims=True) for j in range(CONV_WIDTH)]
            dw_ref[...] += jnp.concatenate(rows + [jnp.zeros((8 - CONV_WIDTH, 128), F32)], axis=0)

        one(xq_ref, xqh_ref, wq_ref, dqn_ref, dcq_ref, dwq_ref, True)
        one(xk_ref, xkh_ref, wk_ref, dkn_ref, dck_ref, dwk_ref, True)
        one(xv_ref, xvh_ref, wv_ref, dv_ref, dcv_ref, dwv_ref, False)

    tile = pl.BlockSpec((T, DN_WIDTH), lambda i: (i, 0))
    halo = pl.BlockSpec((HALO, DN_WIDTH), lambda i: (jnp.maximum(i * (T // HALO) - 1, 0), 0))
    wspec = lambda sec: pl.BlockSpec((CONV_WIDTH, DN_WIDTH), lambda i, sec=sec: (0, sec))
    dwspec = pl.BlockSpec((8, DN_WIDTH), lambda i: (0, 0))
    return pl.pallas_call(
        body, name="dn_conv_bwd_pre", grid=(S // T,),
        in_specs=[tile, halo, tile, halo, tile, halo, wspec(0), wspec(1), wspec(2), tile, tile, tile],
        out_specs=[tile, tile, tile, dwspec, dwspec, dwspec],
        out_shape=[jax.ShapeDtypeStruct((S, DN_WIDTH), F32)] * 3 + [jax.ShapeDtypeStruct((8, DN_WIDTH), F32)] * 3,
        scratch_shapes=[pltpu.VMEM((T + HALO, 128), F32)],
        compiler_params=_params(1),
    )(xq, xq, xk, xk, xv, xv, conv_w, conv_w, conv_w, dqn, dkn, dv)


def _conv_bwd_x(dcq, dck, dcv, conv_w):
    S = dcq.shape[0]
    T = CONV_T
    nt = S // T

    def body(*refs):
        _per_head(head, refs)

    def head(dq_ref, dqh_ref, dk_ref, dkh_ref, dv_ref, dvh_ref, wq_ref, wk_ref, wv_ref,
             oq_ref, ok_ref, ov_ref, pad_ref):
        i = pl.program_id(0)

        def one(d_ref, dh_ref, w_ref, o_ref):
            pad_ref[pl.ds(0, T), :] = d_ref[...]
            pad_ref[pl.ds(T, HALO), :] = jnp.where(i < nt - 1, dh_ref[...], 0.0)
            w = w_ref[...]
            acc = pad_ref[pl.ds(3, T), :] * w[0:1, :]
            for j in range(1, CONV_WIDTH):
                acc = acc + pad_ref[pl.ds(3 - j, T), :] * w[j:j + 1, :]
            o_ref[...] = acc

        one(dq_ref, dqh_ref, wq_ref, oq_ref)
        one(dk_ref, dkh_ref, wk_ref, ok_ref)
        one(dv_ref, dvh_ref, wv_ref, ov_ref)

    tile = pl.BlockSpec((T, DN_WIDTH), lambda i: (i, 0))
    halo = pl.BlockSpec((HALO, DN_WIDTH), lambda i: (jnp.minimum((i + 1) * (T // HALO), S // HALO - 1), 0))
    wspec = lambda sec: pl.BlockSpec((CONV_WIDTH, DN_WIDTH), lambda i, sec=sec: (0, sec))
    return pl.pallas_call(
        body, name="dn_conv_bwd_x", grid=(nt,),
        in_specs=[tile, halo, tile, halo, tile, halo, wspec(0), wspec(1), wspec(2)],
        out_specs=[tile, tile, tile],
        out_shape=[jax.ShapeDtypeStruct((S, DN_WIDTH), F32)] * 3,
        scratch_shapes=[pltpu.VMEM((T + HALO, 128), F32)],
        compiler_params=_params(1),
    )(dcq, dcq, dck, dck, dcv, dcv, conv_w, conv_w, conv_w)


PREP_CHUNKS = 4
SCAN_CHUNKS = 8


def _bnn(a, b):
    return lax.dot_general(a, b, (((2,), (1,)), ((0,), (0,))), preferred_element_type=F32, precision=HI)


def _bnt(a, b):
    return lax.dot_general(a, b, (((2,), (2,)), ((0,), (0,))), preferred_element_type=F32, precision=HI)


def _btn(a, b):
    return lax.dot_general(a, b, (((1,), (1,)), ((0,), (0,))), preferred_element_type=F32, precision=HI)


def _tri_inverse_b(a, blk, eye):
    dg = jnp.where(blk, a, 0.0)
    lo = a - dg
    d2 = _bnn(dg, dg)
    d4 = _bnn(d2, d2)
    d8 = _bnn(d4, d4)
    td = _bnn(_bnn(_bnn(eye - dg, eye + d2), eye + d4), eye + d8)
    b = _bnn(td, lo)
    b2 = _bnn(b, b)
    return _bnn(_bnn(eye - b, eye + b2), td)


def _dn_common_b(bds, avec, dvec, q_raw, k, v, t=None):
    C = DN_CHUNK
    lane = lax.broadcasted_iota(jnp.int32, (C, 128), 1)
    row = lax.broadcasted_iota(jnp.int32, (1, C, C), 1)
    col = lax.broadcasted_iota(jnp.int32, (1, C, C), 2)
    incl = row >= col
    strict = row > col
    eye = (row == col).astype(F32)
    blk = (row // 16) == (col // 16)
    pick = lambda tile, ln: jnp.sum(jnp.where(lane == ln, tile, 0.0), axis=-1, keepdims=True)
    betas, graws, zcs = [], [], []
    for bd in bds:
        z = bd + dvec
        g_all = -jnp.exp(avec) * (jnp.maximum(z, 0.0) + jnp.log(1.0 + jnp.exp(-jnp.abs(z))))
        beta_all = _sigmoid(bd)
        for h in range(DN_HEADS):
            betas.append(pick(beta_all, h))
            graws.append(pick(g_all, DN_HEADS + h))
            zcs.append(pick(z, DN_HEADS + h))
    beta, graw, zc = jnp.stack(betas), jnp.stack(graws), jnp.stack(zcs)
    to_row = lambda c: jnp.sum(eye * c, axis=1, keepdims=True)
    gc = jnp.sum(jnp.where(incl, to_row(graw), 0.0), axis=-1, keepdims=True)
    decay = jnp.exp(jnp.where(incl, gc - to_row(gc), NEG_BIG))
    q = q_raw * (DN_HEAD_DIM ** -0.5)
    kb = k * beta
    kk = _bnt(kb, k)
    if t is None:
        t = _tri_inverse_b(jnp.where(strict, kk * decay, 0.0), blk, eye)
    eg = jnp.exp(gc)
    rhs_w = kb * eg
    u = _bnn(t, v * beta)
    w = _bnn(t, rhs_w)
    qk = _bnt(q, k)
    aq = jnp.where(incl, qk * decay, 0.0)
    last = lax.broadcasted_iota(jnp.int32, (1, C, 1), 1) == C - 1
    g_last = jnp.sum(jnp.where(last, gc, 0.0), axis=1, keepdims=True)
    ekd = jnp.exp(g_last - gc)
    return dict(beta=beta, graw=graw, zc=zc, gc=gc, decay=decay, q=q, kb=kb, kk=kk, t=t, eg=eg, rhs_w=rhs_w,
                u=u, w=w, qk=qk, aq=aq, g_last=g_last, ekd=ekd, kd=k * ekd, qg=q * eg,
                incl=incl, strict=strict, eye=eye, lane=lane, row=row, col=col, last=last)


def _stack_heads(ref, rows):
    return jnp.stack([ref[rows, h * DN_HEAD_DIM:(h + 1) * DN_HEAD_DIM] for h in range(DN_HEADS)])


def _stack_units(ref, nc):
    C = DN_CHUNK
    return jnp.concatenate([_stack_heads(ref, slice(ci * C, (ci + 1) * C)) for ci in range(nc)], axis=0)


def _store_units(ref, val, nc):
    C = DN_CHUNK
    for ci in range(nc):
        for h in range(DN_HEADS):
            ref[ci * C:(ci + 1) * C, h * DN_HEAD_DIM:(h + 1) * DN_HEAD_DIM] = val[ci * DN_HEADS + h]


def _dn_prep(qn, kn, v, bd, avec, dvec):
    S = qn.shape[0]
    C = DN_CHUNK
    N = S // C
    nc = PREP_CHUNKS

    def body(q_ref, k_ref, v_ref, bd_ref, a_ref, d_ref, u_ref, w_ref, qg_ref, kd_ref, aq_ref, t_ref, egl_ref):
        bds = [bd_ref[ci * C:(ci + 1) * C, :] for ci in range(nc)]
        c = _dn_common_b(bds, a_ref[...], d_ref[...], _stack_units(q_ref, nc), _stack_units(k_ref, nc), _stack_units(v_ref, nc))
        _store_units(u_ref, c["u"], nc)
        _store_units(w_ref, c["w"], nc)
        _store_units(qg_ref, c["qg"], nc)
        _store_units(kd_ref, c["kd"], nc)
        egl = jnp.broadcast_to(jnp.exp(c["g_last"]), (nc * DN_HEADS, 1, 128))
        for ci in range(nc):
            for h in range(DN_HEADS):
                aq_ref[h, ci * C:(ci + 1) * C, :] = c["aq"][ci * DN_HEADS + h]
                t_ref[h, ci * C:(ci + 1) * C, :] = c["t"][ci * DN_HEADS + h]
            egl_ref[ci * 8:(ci + 1) * 8, :] = jnp.concatenate(
                [egl[ci * DN_HEADS + h] for h in range(DN_HEADS)] + [jnp.zeros((8 - DN_HEADS, 128), F32)], axis=0)

    tok = lambda w: pl.BlockSpec((nc * C, w), lambda n: (n, 0))
    sq = pl.BlockSpec((DN_HEADS, nc * C, C), lambda n: (0, n, 0))
    vec = pl.BlockSpec((1, 128), lambda n: (0, 0))
    return pl.pallas_call(
        body, name="dn_prep", grid=(N // nc,),
        in_specs=[tok(DN_WIDTH)] * 3 + [tok(128), vec, vec],
        out_specs=[tok(DN_WIDTH)] * 4 + [sq, sq, pl.BlockSpec((nc * 8, 128), lambda n: (n, 0))],
        out_shape=[jax.ShapeDtypeStruct((S, DN_WIDTH), F32)] * 4 + [jax.ShapeDtypeStruct((DN_HEADS, S, C), F32)] * 2
                  + [jax.ShapeDtypeStruct((N * 8, 128), F32)],
        compiler_params=_params(1),
    )(qn, kn, v, bd, avec, dvec)


def _dn_scan_fwd(u, w, qg, kd, aq, egl, gate, dn_gain):
    S = u.shape[0]
    C = DN_CHUNK
    N = S // C
    HD = DN_HEAD_DIM
    nc = SCAN_CHUNKS

    def body(u_ref, w_ref, qg_ref, kd_ref, aq_ref, egl_ref, gate_ref, gain_ref, dn_ref, o_ref, vn_ref, st_ref, state_ref):
        @pl.when(pl.program_id(0) == 0)
        def _():
            state_ref[...] = jnp.zeros_like(state_ref)

        gain = gain_ref[...]
        for ci in range(nc):
            rows = slice(ci * C, (ci + 1) * C)
            st = state_ref[...]
            for h in range(DN_HEADS):
                st_ref[ci * DN_WIDTH + h * HD:ci * DN_WIDTH + (h + 1) * HD, :] = st[h]
            v_new = _stack_heads(u_ref, rows) - _bnn(_stack_heads(w_ref, rows), st)
            o = _bnn(_stack_heads(qg_ref, rows), st) + _bnn(aq_ref[:, rows, :], v_new)
            egl = jnp.stack([egl_ref[ci * 8 + h:ci * 8 + h + 1, :] for h in range(DN_HEADS)])
            state_ref[...] = st * egl + _btn(_stack_heads(kd_ref, rows), v_new)
            r = lax.rsqrt(jnp.mean(o * o, axis=-1, keepdims=True) + NORM_EPS)
            gt = _stack_heads(gate_ref, rows)
            dn = o * r * gain * (gt * _sigmoid(gt))
            for h in range(DN_HEADS):
                sl = slice(h * HD, (h + 1) * HD)
                vn_ref[rows, sl] = v_new[h]
                o_ref[rows, sl] = o[h]
                dn_ref[rows, sl] = dn[h]

    tok = lambda wd: pl.BlockSpec((nc * C, wd), lambda n: (n, 0))
    sq = pl.BlockSpec((DN_HEADS, nc * C, C), lambda n: (0, n, 0))
    vec = pl.BlockSpec((1, 128), lambda n: (0, 0))
    return pl.pallas_call(
        body, name="dn_scan_fwd", grid=(N // nc,),
        in_specs=[tok(DN_WIDTH)] * 4 + [sq, pl.BlockSpec((nc * 8, 128), lambda n: (n, 0)), tok(DN_WIDTH), vec],
        out_specs=[tok(DN_WIDTH)] * 3 + [pl.BlockSpec((nc * DN_WIDTH, HD), lambda n: (n, 0))],
        out_shape=[jax.ShapeDtypeStruct((S, DN_WIDTH), F32)] * 3 + [jax.ShapeDtypeStruct((N * DN_WIDTH, HD), F32)],
        scratch_shapes=[pltpu.VMEM((DN_HEADS, HD, HD), F32)],
        compiler_params=_params(1),
    )(u, w, qg, kd, aq, egl, gate, dn_gain)


def _dn_scan_bwd(w, qg, kd, aq, egl, gate, dn_gain, o, ddn):
    S = w.shape[0]
    C = DN_CHUNK
    N = S // C
    HD = DN_HEAD_DIM
    nc = SCAN_CHUNKS

    def body(w_ref, qg_ref, kd_ref, aq_ref, egl_ref, gate_ref, gain_ref, o_ref, ddn_ref,
             do_ref, dvn_ref, dgate_ref, dst_ref, small_ref, dstate_ref):
        @pl.when(pl.program_id(0) == 0)
        def _():
            dstate_ref[...] = jnp.zeros_like(dstate_ref)
            small_ref[...] = jnp.zeros_like(small_ref)

        gain = gain_ref[...]
        d_gain = jnp.zeros((1, 128), F32)
        for ci in reversed(range(nc)):
            rows = slice(ci * C, (ci + 1) * C)
            dsn = dstate_ref[...]
            for h in range(DN_HEADS):
                dst_ref[ci * DN_WIDTH + h * HD:ci * DN_WIDTH + (h + 1) * HD, :] = dsn[h]
            ov = _stack_heads(o_ref, rows)
            r = lax.rsqrt(jnp.mean(ov * ov, axis=-1, keepdims=True) + NORM_EPS)
            on = ov * r
            gt = _stack_heads(gate_ref, rows)
            sgt = _sigmoid(gt)
            silu_g = gt * sgt
            dy = _stack_heads(ddn_ref, rows)
            d_gain = d_gain + jnp.sum(jnp.sum(dy * on * silu_g, axis=1, keepdims=True), axis=0)
            dgate = dy * on * gain * (sgt * (1.0 + gt * (1.0 - sgt)))
            don = dy * gain * silu_g
            do = r * (don - on * jnp.mean(don * on, axis=-1, keepdims=True))
            d_vnew = _btn(aq_ref[:, rows, :], do) + _bnn(_stack_heads(kd_ref, rows), dsn)
            egl = jnp.stack([egl_ref[ci * 8 + h:ci * 8 + h + 1, :] for h in range(DN_HEADS)])
            dstate_ref[...] = _btn(_stack_heads(qg_ref, rows), do) + dsn * egl - _btn(_stack_heads(w_ref, rows), d_vnew)
            for h in range(DN_HEADS):
                sl = slice(h * HD, (h + 1) * HD)
                do_ref[rows, sl] = do[h]
                dvn_ref[rows, sl] = d_vnew[h]
                dgate_ref[rows, sl] = dgate[h]
        small_ref[...] += jnp.concatenate([d_gain, jnp.zeros((7, 128), F32)], axis=0)

    nb = N // nc
    tok = lambda wd: pl.BlockSpec((nc * C, wd), lambda i: (nb - 1 - i, 0))
    sq = pl.BlockSpec((DN_HEADS, nc * C, C), lambda i: (0, nb - 1 - i, 0))
    vec = pl.BlockSpec((1, 128), lambda i: (0, 0))
    return pl.pallas_call(
        body, name="dn_scan_bwd", grid=(nb,),
        in_specs=[tok(DN_WIDTH)] * 3 + [sq, pl.BlockSpec((nc * 8, 128), lambda i: (nb - 1 - i, 0)), tok(DN_WIDTH), vec,
                                       tok(DN_WIDTH), tok(DN_WIDTH)],
        out_specs=[tok(DN_WIDTH)] * 3 + [pl.BlockSpec((nc * DN_WIDTH, HD), lambda i: (nb - 1 - i, 0)),
                                        pl.BlockSpec((8, 128), lambda i: (0, 0))],
        out_shape=[jax.ShapeDtypeStruct((S, DN_WIDTH), F32)] * 3 + [jax.ShapeDtypeStruct((N * DN_WIDTH, HD), F32),
                                                                  jax.ShapeDtypeStruct((8, 128), F32)],
        scratch_shapes=[pltpu.VMEM((DN_HEADS, HD, HD), F32)],
        compiler_params=_params(1),
    )(w, qg, kd, aq, egl, gate, dn_gain, o, ddn)


def _dn_post(qn, kn, v, bd, avec, dvec, t_inv, v_new_all, states, dstates, do_all, dvn_all, comm=None):
    S = qn.shape[0]
    C = DN_CHUNK
    N = S // C
    HD = DN_HEAD_DIM
    nc = PREP_CHUNKS
    B = nc * DN_HEADS

    def body(q_ref, k_ref, v_ref, bd_ref, a_ref, d_ref, t_ref, vn_ref, st_ref, dst_ref, do_ref, dvn_ref,
             dq_ref, dk_ref, dv_ref, dbd_ref, small_ref):
        @pl.when(pl.program_id(0) == 0)
        def _():
            small_ref[...] = jnp.zeros_like(small_ref)

        avec = a_ref[...]
        bds = [bd_ref[ci * C:(ci + 1) * C, :] for ci in range(nc)]
        k = _stack_units(k_ref, nc)
        vv = _stack_units(v_ref, nc)
        t = jnp.concatenate([t_ref[:, ci * C:(ci + 1) * C, :] for ci in range(nc)], axis=0)
        c = _dn_common_b(bds, avec, d_ref[...], _stack_units(q_ref, nc), k, vv, t=t)
        q, kb, eg, u, w = c["q"], c["kb"], c["eg"], c["u"], c["w"]
        beta, decay, incl, strict, eye = c["beta"], c["decay"], c["incl"], c["strict"], c["eye"]
        st = jnp.stack([st_ref[b * HD:(b + 1) * HD, :] for b in range(B)])
        dsn = jnp.stack([dst_ref[b * HD:(b + 1) * HD, :] for b in range(B)])
        v_new = _stack_units(vn_ref, nc)
        do = _stack_units(do_ref, nc)
        d_vnew = _stack_units(dvn_ref, nc)
        egl = jnp.exp(c["g_last"])
        daq = jnp.where(incl, _bnt(do, v_new), 0.0)
        d_qg = _bnt(do, st)
        d_kd = _bnt(v_new, dsn)
        d_glast = jnp.sum(jnp.sum(dsn * st, axis=-1, keepdims=True), axis=1, keepdims=True) * egl
        d_w = -_bnt(d_vnew, st)
        d_ru = _btn(t, d_vnew)
        d_rw = _btn(t, d_w)
        da = -jnp.where(strict, _bnt(d_ru, u) + _bnt(d_rw, w), 0.0)
        dv = d_ru * beta
        dbeta = jnp.sum(d_ru * vv, axis=-1, keepdims=True)
        dkb = d_rw * eg
        dgc = jnp.sum(d_rw * c["rhs_w"], axis=-1, keepdims=True)
        dkk = da * decay
        ddecay = da * c["kk"]
        dkb = dkb + _bnn(dkk, k)
        dk = _btn(dkk, kb)
        dqk = daq * decay
        ddecay = ddecay + daq * c["qk"]
        dq = _bnn(dqk, k)
        dk = dk + _btn(dqk, q)
        m = ddecay * decay
        col_sum = jnp.sum(m, axis=1, keepdims=True)
        dgc = dgc + jnp.sum(m, axis=-1, keepdims=True) - jnp.sum(eye * col_sum, axis=-1, keepdims=True)
        dq = dq + d_qg * eg
        dgc = dgc + jnp.sum(d_qg * c["qg"], axis=-1, keepdims=True)
        dk = dk + d_kd * c["ekd"]
        tk = jnp.sum(d_kd * c["kd"], axis=-1, keepdims=True)
        dgc = dgc - tk
        d_glast = d_glast + jnp.sum(tk, axis=1, keepdims=True)
        dk = dk + dkb * beta
        dbeta = dbeta + jnp.sum(dkb * k, axis=-1, keepdims=True)
        dgc = dgc + jnp.where(c["last"], d_glast, 0.0)
        dgc_row = jnp.sum(eye * dgc, axis=1, keepdims=True)
        dgraw = jnp.sum(jnp.where(c["col"] >= c["row"], dgc_row, 0.0), axis=-1, keepdims=True)
        _store_units(dq_ref, dq * (HD ** -0.5), nc)
        _store_units(dk_ref, dk, nc)
        _store_units(dv_ref, dv, nc)
        dbraw = dbeta * beta * (1.0 - beta)
        dzc = dgraw * _sigmoid(c["zc"])
        ga = dgraw * c["graw"]
        lane = c["lane"]
        lane1 = lax.broadcasted_iota(jnp.int32, (1, 128), 1)
        neg_ea = -jnp.exp(avec)
        d_alog = jnp.zeros((1, 128), F32)
        d_dt = jnp.zeros((1, 128), F32)
        for ci in range(nc):
            dbd = jnp.zeros((C, 128), F32)
            for h in range(DN_HEADS):
                b = ci * DN_HEADS + h
                dz = dzc[b] * neg_ea
                dbd = dbd + jnp.where(lane == h, dbraw[b], 0.0) + jnp.where(lane == DN_HEADS + h, dz, 0.0)
                d_alog = d_alog + jnp.where(lane1 == DN_HEADS + h, jnp.sum(ga[b], axis=0, keepdims=True), 0.0)
                d_dt = d_dt + jnp.where(lane1 == DN_HEADS + h, jnp.sum(dz, axis=0, keepdims=True), 0.0)
            dbd_ref[ci * C:(ci + 1) * C, :] = dbd
        small_ref[...] += jnp.concatenate([d_alog, d_dt, jnp.zeros((6, 128), F32)], axis=0)

    tok = lambda wd: pl.BlockSpec((nc * C, wd), lambda n: (n, 0))
    big = pl.BlockSpec((nc * DN_WIDTH, HD), lambda n: (n, 0))
    sq = pl.BlockSpec((DN_HEADS, nc * C, C), lambda n: (0, n, 0))
    vec = pl.BlockSpec((1, 128), lambda n: (0, 0))
    return _call(
        body, (qn, kn, v, bd, avec, dvec, t_inv, v_new_all, states, dstates, do_all, dvn_all),
        name="dn_post", grid=(N // nc,), comm=comm,
        in_specs=[tok(DN_WIDTH)] * 3 + [tok(128), vec, vec, sq, tok(DN_WIDTH), big, big, tok(DN_WIDTH), tok(DN_WIDTH)],
        out_specs=[tok(DN_WIDTH)] * 3 + [tok(128), pl.BlockSpec((8, 128), lambda n: (0, 0))],
        out_shape=[jax.ShapeDtypeStruct((S, DN_WIDTH), F32)] * 3 + [jax.ShapeDtypeStruct((S, 128), F32),
                                                                  jax.ShapeDtypeStruct((8, 128), F32)])


def _outproj_fwd(x, attn, dn, w_out):
    S, D = x.shape
    tm = 512

    def body(x_ref, a_ref, d_ref, w_ref, xo_ref, mix_ref):
        a = a_ref[...].astype(BF16)
        dd = d_ref[...].astype(BF16)
        mix_ref[:, 0:ATTN_WIDTH] = a
        mix_ref[:, ATTN_WIDTH:] = dd
        xo_ref[...] = x_ref[...] + _nn(a, w_ref[0:ATTN_WIDTH, :]) + _nn(dd, w_ref[ATTN_WIDTH:, :])

    tok = lambda w: pl.BlockSpec((tm, w), lambda i: (i, 0))
    return pl.pallas_call(
        body, name="outproj_fwd", grid=(S // tm,),
        in_specs=[tok(D), tok(ATTN_WIDTH), tok(DN_WIDTH), pl.BlockSpec((D, D), lambda i: (0, 0))],
        out_specs=[tok(D), tok(D)],
        out_shape=[jax.ShapeDtypeStruct((S, D), F32), jax.ShapeDtypeStruct((S, D), BF16)],
        compiler_params=_params(1),
    )(x, attn, dn, w_out)


def _outproj_bwd(dx, w_out, attn, comm=None):
    S, D = dx.shape
    tm = VIEW_TILE

    def body(dx_ref, w_ref, attn_ref, da1, da4, da16, dl1, dl4, dl16, ddn_ref, dxb_ref, planes):
        d = dx_ref[...].astype(BF16)
        dxb_ref[...] = d
        da = _nt(d, w_ref[0:ATTN_WIDTH, :])
        ddn_ref[...] = _nt(d, w_ref[ATTN_WIDTH:, :])
        _tile_to_views(da, planes, (da1, da4, da16))
        lo = lax.broadcasted_iota(jnp.int32, (tm, 128), 1) < 64
        cols = []
        for G in range(4):
            sl = slice(G * 128, (G + 1) * 128)
            t = da[:, sl] * attn_ref[:, sl]
            d0 = jnp.sum(jnp.where(lo, t, 0.0), axis=-1, keepdims=True)
            d1 = jnp.sum(jnp.where(lo, 0.0, t), axis=-1, keepdims=True)
            cols.append(jnp.where(lo, d0, d1))
        _tile_to_views(jnp.concatenate(cols, axis=1), planes, (dl1, dl4, dl16))

    tok = lambda w: pl.BlockSpec((tm, w), lambda i: (i, 0))
    views = [_view_spec(d) for d in DILATIONS]
    return _call(
        body, (dx, w_out, attn), name="outproj_bwd", grid=(S // tm,), comm=comm,
        in_specs=[tok(D), pl.BlockSpec((D, D), lambda i: (0, 0)), tok(ATTN_WIDTH)],
        out_specs=views + views + [tok(DN_WIDTH), tok(D)],
        out_shape=[_view_shape(S, d, F32) for d in DILATIONS] * 2
                  + [jax.ShapeDtypeStruct((S, DN_WIDTH), F32), jax.ShapeDtypeStruct((S, D), BF16)],
        scratch_shapes=[pltpu.VMEM((4, tm, 128), F32)])


def _adamw(w, g, m, v, name):
    R, Ccols = w.shape[0], w.shape[-1]
    tr = next((t for t in range(512, 7, -8) if R % t == 0), R)
    c1 = 1.0 - ADAM_B1 ** ADAM_STEP
    c2 = 1.0 - ADAM_B2 ** ADAM_STEP

    def body(w_ref, g_ref, m_ref, v_ref, d_ref, nm_ref, nv_ref):
        gv = g_ref[...]
        mn = ADAM_B1 * m_ref[...] + (1.0 - ADAM_B1) * gv
        vn = ADAM_B2 * v_ref[...] + (1.0 - ADAM_B2) * (gv * gv)
        nm_ref[...] = mn
        nv_ref[...] = vn
        d_ref[...] = -ADAM_LR * ((mn / c1) / (jnp.sqrt(vn / c2) + ADAM_EPS) + ADAM_WD * w_ref[...])

    if w.ndim == 2:
        grid, spec = (R // tr,), pl.BlockSpec((tr, Ccols), lambda i: (i, 0))
    else:
        grid, spec = (2,), pl.BlockSpec((R // 2, 1, Ccols), lambda i: (i, 0, 0))
    return pl.pallas_call(
        body, name=name, grid=grid, in_specs=[spec] * 4, out_specs=[spec] * 3,
        out_shape=[jax.ShapeDtypeStruct(w.shape, F32)] * 3, compiler_params=_params(1),
    )(w, g, m, v)


LATE_WEIGHTS = ("w_in", "w_out", "ffn2_gate", "ffn2_up", "ffn2_down")


def _local_step(x, target, wts, small, dist=None):
    g1, g2, gm, gf = small["norm_ffn1"], small["norm_ffn2"], small["norm_mix"], small["norm_final"]
    wts = dict(wts)

    def reduce_start(gs, tag):
        return _rs_add_pairs(gs, _swap_sibling(gs, True, "rs_swap_halves_" + tag), dist["c"], "rs_add_pairs_" + tag)

    (x1, h1, fg1, fu1), late = _ffn_fwd(x, g1, wts["ffn1_gate"], wts["ffn1_up"], wts["ffn1_down"], "ffn1_fwd",
                                        comm=_ag_comm(dist["late"]) if dist else None)
    if dist:
        wts.update(zip(LATE_WEIGHTS, late))
        wts["w_out"] = wts["w_out"].reshape(D_MODEL, D_MODEL)
        wts["w_in"] = _permute_w_in(wts["w_in"][:, :IN_COLS // N_CHIPS].reshape(IN_COLS, D_MODEL))
    h2, *qkv, xq, xk, xv, gate, bd = _inproj_fwd(x1, gm, wts["w_in"])
    aq, ak, av = qkv[0:3], qkv[3:6], qkv[6:9]
    parts = [_attn_fwd(aq[p], ak[p], av[p], d, f"attn_fwd_d{d}") for p, d in enumerate(DILATIONS)]
    attn, *lse = _attn_merge(parts)
    conv_w = small["conv_w"]
    qn, kn, vv = _conv_fwd(xq, xk, xv, conv_w)
    dn_u, dn_w, dn_qg, dn_kd, dn_aq, dn_t, dn_egl = _dn_prep(qn, kn, vv, bd, small["avec"], small["dvec"])
    dn, o_dn, v_new, states = _dn_scan_fwd(dn_u, dn_w, dn_qg, dn_kd, dn_aq, dn_egl, gate, small["dn_norm"])
    x2, mix = _outproj_fwd(x1, attn, dn, wts["w_out"])
    (dx3, h3, fg2, fu2, loss, d_gf), _ = _ffn_fwd(x2, g2, wts["ffn2_gate"], wts["ffn2_up"], wts["ffn2_down"], "ffn2_fwd",
                                                 head=(gf, target))

    grads = {}
    (dx2, d_g2, dfg2, dfu2, act2, dout2), _ = _ffn_bwd(dx3, x2, g2, fg2, fu2, wts["ffn2_down"], wts["ffn2_gate"],
                                                      wts["ffn2_up"], "ffn2_bwd")
    tk = 2048
    grads["ffn2_gate"], _ = _dw_chunks(dfg2, h3, tk, "dw_ffn2_gate")
    grads["ffn2_up"], _ = _dw_chunks(dfu2, h3, tk, "dw_ffn2_up")
    grads["ffn2_down"], _ = _dw_chunks(act2, dout2, tk, "dw_ffn2_down")
    group_a = ("ffn2_gate", "ffn2_up", "ffn2_down")
    gs_a = [grads[n] for n in group_a]

    (*dviews, ddn, dx2b), swapped_a = _outproj_bwd(dx2, wts["w_out"], attn, comm=_swap_comm(gs_a, True) if dist else None)
    parts_a = _rs_add_pairs(gs_a, swapped_a, dist["c"], "rs_add_pairs_a") if dist else None
    dattn, dd = dviews[0:3], dviews[3:6]
    grads["w_out"] = _matmul_tn(mix, dx2b, D_MODEL, tk, "dw_out").reshape(N_CHIPS, D_MODEL // N_CHIPS, D_MODEL)

    daq, dak, dav = [], [], []
    for p, d in enumerate(DILATIONS):
        daq.append(_attn_bwd_q(aq[p], ak[p], av[p], dattn[p], lse[p], dd[p], d, f"attn_bwd_q_d{d}"))
        dk_p, dv_p = _attn_bwd_kv(aq[p], ak[p], av[p], dattn[p], lse[p], dd[p], d, f"attn_bwd_kv_d{d}")
        dak.append(dk_p)
        dav.append(dv_p)

    do_dn, dvn, dgate, dstates, d_dn_gain = _dn_scan_bwd(dn_w, dn_qg, dn_kd, dn_aq, dn_egl, gate, small["dn_norm"], o_dn, ddn)
    (dqn, dkn, dvv, dbd, dn_small), recv_a = _dn_post(qn, kn, vv, bd, small["avec"], small["dvec"], dn_t, v_new, states,
                                                      dstates, do_dn, dvn, comm=_rsx_comm(parts_a) if dist else None)
    dcq, dck, dcv, dwq, dwk, dwv = _conv_bwd_pre(xq, xk, xv, conv_w, dqn, dkn, dvv)
    dxq, dxk, dxv = _conv_bwd_x(dcq, dck, dcv, conv_w)
    d_conv = jnp.concatenate([dwq[:CONV_WIDTH], dwk[:CONV_WIDTH], dwv[:CONV_WIDTH]], axis=1)

    dx1, d_gm, dproj = _inproj_bwd(dx2, x1, gm, [daq, dak, dav], [dxq, dxk, dxv, dgate], dbd, wts["w_in"])
    gi = _matmul_tn(dproj, h2, IN_COLS_PADDED, 512, "dw_in")
    if dist:
        gate_end = QKV_COLS + DN_WIDTH
        gi = jnp.concatenate([gi[:QKV_COLS], gi[gate_end:gate_end + LOGIT_COLS], gi[QKV_COLS:gate_end]], axis=0)
        gi = gi.reshape(N_CHIPS, IN_COLS // N_CHIPS, D_MODEL)
        gi = jnp.pad(gi, ((0, 0), (0, W_IN_ROWS - IN_COLS // N_CHIPS), (0, 0)))
    grads["w_in"] = gi
    group_b = ("w_in", "w_out")
    parts_b = reduce_start([grads[n] for n in group_b], "b") if dist else None

    (dx0, d_g1, dfg1, dfu1, act1, dout1), recv_b = _ffn_bwd(dx1, x, g1, fg1, fu1, wts["ffn1_down"], wts["ffn1_gate"],
                                                           wts["ffn1_up"], "ffn1_bwd",
                                                           comm=_rsx_comm(parts_b) if dist else None)
    group_c = ("ffn1_gate", "ffn1_up", "ffn1_down")
    pending, parts_c, recv_c = [], [], []
    for n, (lhs, rhs) in zip(group_c, ((dfg1, h1), (dfu1, h1), (act1, dout1))):
        grads[n], landed = _dw_chunks(lhs, rhs, tk, "dw_" + n, comm=_rsx_comm(pending) if pending else None)
        recv_c += list(landed)
        if dist:
            pending = reduce_start([grads[n]], n)
            parts_c += pending

    small_grads = dict(norm_ffn1=d_g1, norm_mix=d_gm, norm_ffn2=d_g2, norm_final=d_gf, conv_w=d_conv,
                       a_log=dn_small[0:1], dt_bias=dn_small[1:2], dn_norm=d_dn_gain[0:1])
    if dist:
        recv_c += _rs_exchange_arrays(pending)
        names = group_a + group_b + group_c
        totals = _rs_add_totals(list(parts_a) + list(parts_b) + list(parts_c), list(recv_a) + list(recv_b) + list(recv_c),
                                dist["chip"])
        theirs = _swap_sibling(totals, False, "rs_share_total")
        grads = {n: (mine, other) for n, mine, other in zip(names, totals, theirs)}
    return loss, dx0, grads, small_grads


HBM =pl.BlockSpec(memory_space=pl.ANY)
VMEM_SPEC = pl.BlockSpec(memory_space=pltpu.VMEM)


def _coords():
    return lax.axis_index("x"), lax.axis_index("y"), lax.axis_index("c")


def _remote(src, dst, send_sems, recv_sems, k, dev):
    return pltpu.make_async_remote_copy(src_ref=src, dst_ref=dst, send_sem=send_sems.at[k], recv_sem=recv_sems.at[k],
                                        device_id=dev, device_id_type=MESH)


def _allreduce_small(buf, name):
    R, Cc = buf.shape

    def body(src_ref, out_ref, recv_ref, send_sems, recv_sems):
        x, y, c = _coords()
        copies = []
        for m in range(1, 8):
            fx, fy, fc = (m >> 2) & 1, (m >> 1) & 1, m & 1
            dev = (x ^ fx if fx else x, y ^ fy if fy else y, c ^ fc if fc else c)
            cp = _remote(src_ref, recv_ref.at[m - 1], send_sems, recv_sems, m - 1, dev)
            cp.start()
            copies.append(cp)
        for cp in copies:
            cp.wait()
        r = [src_ref[...]] + [recv_ref[m] for m in range(7)]
        out_ref[...] = ((r[0] + r[1]) + (r[2] + r[3])) + ((r[4] + r[5]) + (r[6] + r[7]))

    return pl.pallas_call(
        body, name=name, out_shape=jax.ShapeDtypeStruct((R, Cc), F32),
        in_specs=[VMEM_SPEC], out_specs=VMEM_SPEC,
        scratch_shapes=[pltpu.VMEM((7, R, Cc), F32), pltpu.SemaphoreType.DMA((7,)), pltpu.SemaphoreType.DMA((7,))],
    )(buf)


BIG = ("ffn1_gate", "ffn1_up", "ffn1_down", "w_in", "w_out", "ffn2_gate", "ffn2_up", "ffn2_down")
ROW_SHARDED = ("ffn1_down", "w_out", "ffn2_down")
W_IN_ROWS = 960


def _rows(ref, start, size):
    return ref.at[pl.ds(pl.multiple_of(start, 16), size)]


def _allgather_arrays(shards):
    n = len(shards)
    _, shapes, n_sems, start, finish, middle = _ag_comm(shards)

    def body(*refs):
        for phase in (start, middle, finish):
            phase(refs[:n], refs[n:2 * n], refs[2 * n], refs[2 * n + 1])

    return pl.pallas_call(
        body, name="allgather_weights", out_shape=shapes, in_specs=[HBM] * n, out_specs=[HBM] * n,
        scratch_shapes=[pltpu.SemaphoreType.DMA((n_sems,)), pltpu.SemaphoreType.DMA((n_sems,))],
    )(*shards)


def _ag_copies(srcs, outs, send_sems, recv_sems):
    x, y, c = _coords()
    sib = (x, y, 1 - c)
    xn, yn, dg = (1 - x, y), (x, 1 - y), (1 - x, 1 - y)
    plan = []
    for a, (src, out) in enumerate(zip(srcs, outs)):
        h = src.shape[0] // 2
        q = h // 2
        cp = lambda s, d, k, dev: _remote(s, d, send_sems, recv_sems, 8 * a + k, dev)
        slot = lambda chip: out.at[2 * chip[0] + chip[1]]
        mine, dst = _rows(src, c * h, h), _rows(slot((x, y)), c * h, h)
        piece = lambda chip, start, size, k, dev: cp(_rows(slot(chip), start, size), _rows(slot(chip), start, size), k, dev)
        plan.append(dict(
            own=cp(src, slot((x, y)), 6, sib),
            to_x=cp(mine, dst, 0, (*xn, c)), to_y=cp(mine, dst, 1, (*yn, c)),
            from_x=piece(xn, c * h, h, 0, sib), from_y=piece(yn, c * h, h, 1, sib),
            relay_y=piece(xn, c * h, q, 2, (*yn, c)), relay_x=piece(yn, c * h + q, q, 7, (*xn, c)),
            pass_x=piece(xn, c * h, h, 3, sib), pass_y=piece(yn, c * h, h, 4, sib), pass_d=piece(dg, c * h, h, 5, sib),
            diag_1=piece(dg, c * h, q, 2, sib), diag_2=piece(dg, c * h + q, q, 7, sib),
            got=[piece(chip, (1 - c) * h, h, k, sib) for k, chip in ((3, xn), (4, yn), (5, dg))]))
    return plan


def _ag_start(*refs):
    for p in _ag_copies(*refs):
        for k in ("own", "to_x", "to_y"):
            p[k].start()


def _ag_middle(*refs):
    for p in _ag_copies(*refs):
        p["from_x"].wait_recv()
        p["relay_y"].start()
        p["pass_x"].start()
        p["from_y"].wait_recv()
        p["relay_x"].start()
        p["pass_y"].start()


def _ag_finish(*refs):
    plan = _ag_copies(*refs)
    for p in plan:
        p["diag_1"].wait_recv()
        p["diag_2"].wait_recv()
        p["pass_d"].start()
    for p in plan:
        for cp in p["got"]:
            cp.wait_recv()
        p["own"].wait_recv()
        for k in ("own", "to_x", "to_y", "relay_y", "relay_x", "pass_x", "pass_y", "pass_d"):
            p[k].wait_send()


def _ag_comm(shards):
    shapes = [jax.ShapeDtypeStruct((N_CHIPS,) + s.shape, s.dtype) for s in shards]
    return (list(shards), shapes, 8 * len(shards), _ag_start, _ag_finish, _ag_middle)


def _swap_sibling(arrs, pick_other_half, name):
    n = len(arrs)
    _, outs, n_sems, start, finish = _swap_comm(arrs, pick_other_half)

    def body(*refs):
        start(refs[:n], refs[n:2 * n], refs[2 * n], refs[2 * n + 1])
        finish(refs[:n], refs[n:2 * n], refs[2 * n], refs[2 * n + 1])

    return pl.pallas_call(
        body, name=name, out_shape=outs, in_specs=[HBM] * n, out_specs=[HBM] * n,
        scratch_shapes=[pltpu.SemaphoreType.DMA((n_sems,)), pltpu.SemaphoreType.DMA((n_sems,))],
    )(*arrs)


def _swap_comm(arrs, pick_other_half):
    def copies(srcs, dsts, send_sems, recv_sems):
        x, y, c = _coords()
        cps = []
        for a, (src, dst) in enumerate(zip(srcs, dsts)):
            if pick_other_half:
                h = src.shape[1] // 2
                src = src.at[:, pl.ds(pl.multiple_of((1 - c) * h, 16), h)]
            cps.append(_remote(src, dst, send_sems, recv_sems, a, (x, y, 1 - c)))
        return cps

    def start(*refs):
        for cp in copies(*refs):
            cp.start()

    def finish(*refs):
        for cp in copies(*refs):
            cp.wait()

    shapes = [jax.ShapeDtypeStruct((a.shape[0], a.shape[1] // 2) + a.shape[2:] if pick_other_half else a.shape, a.dtype)
              for a in arrs]
    return (list(arrs), shapes, len(arrs), start, finish)


def _rs_add_pairs(gs, others, c, name):
    n = len(gs)
    blocks = [(g.shape[1] // 4, g.shape[2]) for g in gs]

    def body(c_ref, *refs):
        for a in range(n):
            refs[2 * n + a][...] = (refs[a][...] + refs[n + a][...]).astype(BF16)

    mine = lambda b: pl.BlockSpec((None,) + b, lambda j, s, c_ref: (j, c_ref[0] * 2 + s, 0))
    flat = lambda b: pl.BlockSpec((None,) + b, lambda j, s, c_ref: (j, s, 0))
    return pl.pallas_call(
        body, name=name,
        grid_spec=pltpu.PrefetchScalarGridSpec(
            num_scalar_prefetch=1, grid=(N_CHIPS, 2),
            in_specs=[mine(b) for b in blocks] + [flat(b) for b in blocks],
            out_specs=[flat(b) for b in blocks]),
        out_shape=[jax.ShapeDtypeStruct(o.shape, BF16) for o in others],
        compiler_params=_params(2),
    )(c, *gs, *others)


def _rs_exchange_arrays(parts):
    n = len(parts)

    def body(*refs):
        _rsx_start(refs[:n], refs[n:2 * n], refs[2 * n], refs[2 * n + 1])
        _rsx_finish(refs[:n], refs[n:2 * n], refs[2 * n], refs[2 * n + 1])

    _, shapes, n_sems, _, _ = _rsx_comm(parts)
    return pl.pallas_call(
        body, name="rs_exchange_chips", out_shape=shapes, in_specs=[HBM] * n, out_specs=[HBM] * n,
        scratch_shapes=[pltpu.SemaphoreType.DMA((n_sems,)), pltpu.SemaphoreType.DMA((n_sems,))],
    )(*parts)


def _rsx_copies(srcs, dsts, send_sems, recv_sems):
    x, y, c = _coords()
    others = [(1 - x, y), (x, 1 - y), (1 - x, 1 - y)]
    return [_remote(src.at[2 * ox + oy], dst.at[k], send_sems, recv_sems, 3 * a + k, (ox, oy, c))
            for a, (src, dst) in enumerate(zip(srcs, dsts)) for k, (ox, oy) in enumerate(others)]


def _rsx_start(srcs, dsts, send_sems, recv_sems):
    for cp in _rsx_copies(srcs, dsts, send_sems, recv_sems):
        cp.start()


def _rsx_finish(srcs, dsts, send_sems, recv_sems):
    for cp in _rsx_copies(srcs, dsts, send_sems, recv_sems):
        cp.wait()


def _rsx_comm(parts):
    shapes = [jax.ShapeDtypeStruct((3,) + p.shape[1:], p.dtype) for p in parts]
    return (list(parts), shapes, 3 * len(parts), _rsx_start, _rsx_finish)


def _rs_add_totals(parts, recvs, chip):
    n = len(parts)
    blocks = [(p.shape[1] // 2, p.shape[2]) for p in parts]

    def body(chip_ref, *refs):
        f = lambda r: r[...].astype(F32)
        for a in range(n):
            p, r0, r1, r2 = refs[a], refs[n + 3 * a], refs[n + 3 * a + 1], refs[n + 3 * a + 2]
            refs[4 * n + a][...] = (f(p) + f(r0)) + (f(r1) + f(r2))

    own = lambda b: pl.BlockSpec((None,) + b, lambda s, chip_ref: (chip_ref[0], s, 0))
    slot = lambda b, k: pl.BlockSpec((None,) + b, lambda s, chip_ref, k=k: (k, s, 0))
    recv_specs = [slot(b, k) for b in blocks for k in range(3)]
    recv_args = [r for r in recvs for _ in range(3)]
    return pl.pallas_call(
        body, name="rs_add_totals",
        grid_spec=pltpu.PrefetchScalarGridSpec(
            num_scalar_prefetch=1, grid=(2,),
            in_specs=[own(b) for b in blocks] + recv_specs,
            out_specs=[pl.BlockSpec(b, lambda s, chip_ref: (s, 0)) for b in blocks]),
        out_shape=[jax.ShapeDtypeStruct(p.shape[1:], F32) for p in parts],
        compiler_params=_params(1),
    )(chip, *parts, *recv_args)


def _permute_w_in(wt):
    return jnp.concatenate([wt[:QKV_COLS], wt[QKV_COLS + LOGIT_COLS:IN_COLS], wt[QKV_COLS:QKV_COLS + LOGIT_COLS],
                            jnp.zeros((IN_COLS_PADDED - IN_COLS, wt.shape[1]), wt.dtype)], axis=0)


def _pad_row(v):
    v = v.reshape(1, -1)
    return jnp.pad(v, ((0, 0), (0, D_MODEL - v.shape[1])))


def kernel(x, norm_ffn1, ffn1_gate, ffn1_up, ffn1_down, norm_mix, w_in, conv_w, a_log, dt_bias, dn_norm, w_out, norm_ffn2, ffn2_gate, ffn2_up, ffn2_down, norm_final, loss_target, m_norm_ffn1, m_ffn1_gate, m_ffn1_up, m_ffn1_down, m_norm_mix, m_w_in, m_conv_w, m_a_log, m_dt_bias, m_dn_norm, m_w_out, m_norm_ffn2, m_ffn2_gate, m_ffn2_up, m_ffn2_down, m_norm_final, v_norm_ffn1, v_ffn1_gate, v_ffn1_up, v_ffn1_down, v_norm_mix, v_w_in, v_conv_w, v_a_log, v_dt_bias, v_dn_norm, v_w_out, v_norm_ffn2, v_ffn2_gate, v_ffn2_up, v_ffn2_down, v_norm_final):
    cx, cy, cc = _coords()
    chip = 2 * cx + cy
    stored = lambda t, n: t[0] if n in ROW_SHARDED else t[0].T
    big_w = {n: stored(t, n) for n, t in dict(
        ffn1_gate=ffn1_gate, ffn1_up=ffn1_up, ffn1_down=ffn1_down, w_in=w_in, w_out=w_out,
        ffn2_gate=ffn2_gate, ffn2_up=ffn2_up, ffn2_down=ffn2_down).items()}
    big_m = {n: stored(t, n) for n, t in dict(
        ffn1_gate=m_ffn1_gate, ffn1_up=m_ffn1_up, ffn1_down=m_ffn1_down, w_in=m_w_in, w_out=m_w_out,
        ffn2_gate=m_ffn2_gate, ffn2_up=m_ffn2_up, ffn2_down=m_ffn2_down).items()}
    big_v = {n: stored(t, n) for n, t in dict(
        ffn1_gate=v_ffn1_gate, ffn1_up=v_ffn1_up, ffn1_down=v_ffn1_down, w_in=v_w_in, w_out=v_w_out,
        ffn2_gate=v_ffn2_gate, ffn2_up=v_ffn2_up, ffn2_down=v_ffn2_down).items()}

    cols = IN_COLS // N_CHIPS
    send = {n: big_w[n].astype(BF16) for n in BIG}
    send["w_in"] = jnp.pad(send["w_in"], ((0, W_IN_ROWS - cols), (0, 0)))
    early = tuple(n for n in BIG if n not in LATE_WEIGHTS)
    wts = dict(zip(early, _allgather_arrays([send[n] for n in early])))
    dist =dict(late=[send[n] for n in LATE_WEIGHTS], c=cc.reshape(1).astype(jnp.int32),
                chip=chip.reshape(1).astype(jnp.int32))

    conv_shard = conv_w[0]
    emb = jnp.concatenate([jnp.where((chip == j) & (cc == 0), conv_shard, 0.0) for j in range(N_CHIPS)], axis=1)
    emb = jnp.pad(emb.reshape(6, D_MODEL), ((0, 2), (0, 0)))
    conv_full = _allreduce_small(emb, "allgather_conv_w")[:6].reshape(CONV_WIDTH, 3 * DN_WIDTH)

    zvec = jnp.zeros((1, 128), F32)
    small = dict(norm_ffn1=norm_ffn1, norm_mix=norm_mix, norm_ffn2=norm_ffn2, norm_final=norm_final[None],
                 conv_w=conv_full, avec=zvec.at[0, DN_HEADS:2 * DN_HEADS].set(a_log[0]),
                 dvec=zvec.at[0, DN_HEADS:2 * DN_HEADS].set(dt_bias[0]), dn_norm=dn_norm)

    loss, grad_x, reduced, sg = _local_step(x[0], loss_target[0], wts, small, dist)

    rows = [sg["norm_ffn1"], sg["norm_mix"], sg["norm_ffn2"], sg["norm_final"], _pad_row(sg["a_log"]), _pad_row(sg["dt_bias"]),
            _pad_row(sg["dn_norm"]), _pad_row(loss[0:1]), sg["conv_w"].reshape(6, D_MODEL), jnp.zeros((2, D_MODEL), F32)]
    red = _allreduce_small(jnp.concatenate(rows, axis=0), "allreduce_small")
    loss_out = red[7, 0]
    g_conv_full = red[8:14].reshape(CONV_WIDTH, 3 * DN_WIDTH)
    g_conv = lax.dynamic_slice_in_dim(g_conv_full, chip * (3 * DN_WIDTH // N_CHIPS), 3 * DN_WIDTH // N_CHIPS, axis=1)
    g_small = dict(norm_ffn1=red[0:1], norm_mix=red[1:2], norm_ffn2=red[2:3], norm_final=red[3],
                   a_log=red[4:5, DN_HEADS:2 * DN_HEADS], dt_bias=red[5:6, DN_HEADS:2 * DN_HEADS], dn_norm=red[6:7, :DN_HEAD_DIM])

    out_g, out_d, out_m, out_v = {}, {}, {}, {}
    for n in BIG:
        mine, other = reduced[n]
        g = jnp.where(cc == 0, jnp.concatenate([mine, other], axis=0), jnp.concatenate([other, mine], axis=0))
        if n == "w_in":
            to3 = lambda t: jnp.transpose(t, (2, 0, 1))
            g = g[:cols].reshape(cols, 1, D_MODEL)
            results = (g,) + tuple(_adamw(to3(w_in), g, to3(m_w_in), to3(v_w_in), "adamw_w_in"))
            out_g[n], out_d[n], out_m[n], out_v[n] = (jnp.transpose(t, (1, 2, 0)) for t in results)
            continue
        results = (g,) + tuple(_adamw(big_w[n], g, big_m[n], big_v[n], "adamw_" + n))
        out_g[n], out_d[n], out_m[n], out_v[n] = ((t if n in ROW_SHARDED else t.T)[None] for t in results)
    d, nm, nv = _adamw(conv_w[0], g_conv, m_conv_w[0], v_conv_w[0], "adamw_conv_w")
    out_g["conv_w"], out_d["conv_w"], out_m["conv_w"], out_v["conv_w"] = g_conv[None], d[None], nm[None], nv[None]

    small_names = ("norm_ffn1", "norm_mix", "norm_ffn2", "norm_final", "a_log", "dt_bias", "dn_norm")
    small_w = dict(norm_ffn1=norm_ffn1, norm_mix=norm_mix, norm_ffn2=norm_ffn2, norm_final=norm_final, a_log=a_log,
                   dt_bias=dt_bias, dn_norm=dn_norm)
    small_m = dict(norm_ffn1=m_norm_ffn1, norm_mix=m_norm_mix, norm_ffn2=m_norm_ffn2, norm_final=m_norm_final, a_log=m_a_log,
                   dt_bias=m_dt_bias, dn_norm=m_dn_norm)
    small_v = dict(norm_ffn1=v_norm_ffn1, norm_mix=v_norm_mix, norm_ffn2=v_norm_ffn2, norm_final=v_norm_final, a_log=v_a_log,
                   dt_bias=v_dt_bias, dn_norm=v_dn_norm)
    stack = lambda dct: jnp.concatenate([_pad_row(dct[n]) for n in small_names] + [jnp.zeros((1, D_MODEL), F32)], axis=0)
    d, nm, nv = _adamw(stack(small_w), stack(g_small), stack(small_m), stack(small_v), "adamw_small")
    for k, n in enumerate(small_names):
        shape = small_w[n].shape
        size = math.prod(shape)
        out_g[n] = g_small[n].reshape(shape)
        out_d[n], out_m[n], out_v[n] = (t[k, :size].reshape(shape) for t in (d, nm, nv))

    order = ("norm_ffn1", "ffn1_gate", "ffn1_up", "ffn1_down", "norm_mix", "w_in", "conv_w", "a_log", "dt_bias", "dn_norm",
             "w_out", "norm_ffn2", "ffn2_gate", "ffn2_up", "ffn2_down", "norm_final")
    return (loss_out, grad_x[None], *[out_g[n] for n in order], *[out_d[n] for n in order],
            *[out_m[n] for n in order], *[out_v[n] for n in order])
```

```python
import functools
import math

import jax
import jax.numpy as jnp
from jax import lax
from jax.experimental import pallas as pl
from jax.experimental.pallas import tpu as pltpu

F32 = jnp.float32
BF16 = jnp.bfloat16
HI = lax.Precision.HIGH

D_MODEL = 1024
ATTN_HEADS = 8
ATTN_WIDTH = 512
ATTN_BLOCK = 128
ATTN_SCALE = (ATTN_WIDTH // ATTN_HEADS) ** -0.5
DILATIONS = (1, 4, 16)
DN_HEADS = 4
DN_HEAD_DIM = 128
DN_WIDTH = 512
DN_CHUNK = 64
CONV_WIDTH = 4
NORM_EPS = 1e-6
L2_EPS = 1e-6
QKV_COLS = 3 * ATTN_WIDTH + 3 * DN_WIDTH
LOGIT_COLS = 2 * DN_HEADS
IN_COLS = QKV_COLS + LOGIT_COLS + DN_WIDTH
IN_COLS_PADDED = 3712
N_CHIPS = 4

ADAM_LR = 0.001
ADAM_B1 = 0.9
ADAM_B2 = 0.999
ADAM_EPS = 1e-08
ADAM_WD = 0.01
ADAM_STEP = 10

VMEM_LIMIT = 56 * 1024 * 1024
NEG_BIG = -1e30
MESH = pl.DeviceIdType.MESH


def _params(n_grid, vmem=VMEM_LIMIT):
    return pltpu.CompilerParams(dimension_semantics=("arbitrary",) * n_grid, vmem_limit_bytes=vmem)


def _call(body, args, *, name, grid, in_specs, out_specs, out_shape, scratch_shapes=(), comm=None):
    n_in, n_out, n_scr = len(in_specs), len(out_specs), len(scratch_shapes)
    hbm = pl.BlockSpec(memory_space=pl.ANY)
    srcs, dst_shapes, n_sems, start, finish = comm[:5] if comm is not None else ((), (), 0, None, None)
    middle = comm[5] if comm is not None and len(comm) > 5 else None
    ns, nd = len(srcs), len(dst_shapes)

    def full(*refs):
        ins, c_src = refs[:n_in], refs[n_in:n_in + ns]
        at = n_in + ns
        outs, c_dst = refs[at:at + n_out], refs[at + n_out:at + n_out + nd]
        scr = refs[at + n_out + nd:at + n_out + nd + n_scr]
        if comm is not None:
            ids = [pl.program_id(a) for a in range(len(grid))]
            first = functools.reduce(jnp.logical_and, [i == 0 for i in ids])
            last = functools.reduce(jnp.logical_and, [i == g - 1 for i, g in zip(ids, grid)])

            @pl.when(first)
            def _():
                start(c_src, c_dst, refs[-2], refs[-1])

            if middle is not None:
                relay_step = functools.reduce(jnp.logical_and, [ids[0] == (5 * grid[0]) // 8] + [i == 0 for i in ids[1:]])

                @pl.when(relay_step)
                def _():
                    middle(c_src, c_dst, refs[-2], refs[-1])

        body(*ins, *outs, *scr)
        if comm is not None:
            @pl.when(last)
            def _():
                finish(c_src, c_dst, refs[-2], refs[-1])

    sems = [pltpu.SemaphoreType.DMA((n_sems,)), pltpu.SemaphoreType.DMA((n_sems,))] if comm is not None else []
    res = pl.pallas_call(
        full, name=name, grid=grid, in_specs=list(in_specs) + [hbm] * ns, out_specs=list(out_specs) + [hbm] * nd,
        out_shape=list(out_shape) + list(dst_shapes), scratch_shapes=list(scratch_shapes) + sems,
        compiler_params=_params(len(grid)),
    )(*args, *srcs)
    return res[:n_out], res[n_out:]


def _nt(a, b, precision=None):
    return lax.dot_general(a, b, (((1,), (1,)), ((), ())), preferred_element_type=F32, precision=precision)


def _tn(a, b, precision=None):
    return lax.dot_general(a, b, (((0,), (0,)), ((), ())), preferred_element_type=F32, precision=precision)


def _nn(a, b, precision=None):
    return jnp.dot(a, b, preferred_element_type=F32, precision=precision)


def _sigmoid(x):
    return 1.0 / (1.0 + jnp.exp(-x))


def _loss_head(xf, gain, target):
    r = lax.rsqrt(jnp.mean(xf * xf, axis=-1, keepdims=True) + NORM_EPS)
    xhat = xf * r
    err = xhat * gain - target
    part = 0.5 * jnp.sum(jnp.mean(err * err, axis=-1, keepdims=True), axis=0, keepdims=True)
    dy = err * (1.0 / xf.shape[-1])
    dgain = jnp.sum(dy * xhat, axis=0, keepdims=True)
    dxh = dy * gain
    return part, r * (dxh - xhat * jnp.mean(dxh * xhat, axis=-1, keepdims=True)), dgain


def _ffn_fwd(x, gain, wg, wu, wd, name, comm=None, head=None):
    S, D = x.shape
    nf, tf, _ = wg.shape
    tm = 512
    n_in = 5 if head is None else 7

    def body(*refs):
        x_ref, gain_ref, wg_ref, wu_ref, wd_ref = refs[:5]
        xo_ref, h_ref, g_ref, u_ref = refs[n_in:n_in + 4]
        acc_ref, hs_ref = refs[-2:]
        i = pl.program_id(0)
        j = pl.program_id(1)

        @pl.when(j == 0)
        def _():
            xf = x_ref[...]
            r = lax.rsqrt(jnp.mean(xf * xf, axis=-1, keepdims=True) + NORM_EPS)
            h = (xf * r * gain_ref[...]).astype(BF16)
            hs_ref[...] = h
            h_ref[...] = h
            acc_ref[...] = jnp.zeros_like(acc_ref)

        h = hs_ref[...]
        g = _nt(h, wg_ref[...])
        u = _nt(h, wu_ref[...])
        g_ref[...] = g.astype(BF16)
        u_ref[...] = u.astype(BF16)
        act = g * _sigmoid(g) * u
        acc_ref[...] += _nn(act.astype(BF16), wd_ref[...])

        if head is not None:
            hgain_ref, t_ref = refs[5:7]
            loss_ref, dgain_ref = refs[n_in + 4:n_in + 6]

            @pl.when((i == 0) & (j == 0))
            def _():
                loss_ref[...] = jnp.zeros_like(loss_ref)
                dgain_ref[...] = jnp.zeros_like(dgain_ref)

        @pl.when(j == nf - 1)
        def _():
            xo = x_ref[...] + 0.5 * acc_ref[...]
            if head is None:
                xo_ref[...] = xo
            else:
                part, dxo, dgain = _loss_head(xo, hgain_ref[...], t_ref[...])
                first = ((lax.broadcasted_iota(jnp.int32, (8, 128), 0) == 0)
                         & (lax.broadcasted_iota(jnp.int32, (8, 128), 1) == 0))
                loss_ref[...] += jnp.where(first, part, 0.0)
                dgain_ref[...] += dgain
                xo_ref[...] = dxo

    tok = pl.BlockSpec((tm, D), lambda i, j: (i, 0))
    row = pl.BlockSpec((1, D), lambda i, j: (0, 0))
    chunk = pl.BlockSpec((None, tf, D), lambda i, j: (j, 0, 0))
    act = pl.BlockSpec((None, tm, tf), lambda i, j: (j, i, 0))
    extra_in = [] if head is None else [row, tok]
    extra_out = [] if head is None else [pl.BlockSpec((8, 128), lambda i, j: (0, 0)), row]
    extra_shape = [] if head is None else [jax.ShapeDtypeStruct((8, 128), F32), jax.ShapeDtypeStruct((1, D), F32)]
    return _call(
        body, (x, gain, wg, wu, wd) + (() if head is None else tuple(head)), name=name, grid=(S // tm, nf), comm=comm,
        in_specs=[tok, row, chunk, chunk, chunk] + extra_in,
        out_specs=[tok, tok, act, act] + extra_out,
        out_shape=[jax.ShapeDtypeStruct((S, D), F32), jax.ShapeDtypeStruct((S, D), BF16),
                   jax.ShapeDtypeStruct((nf, S, tf), BF16), jax.ShapeDtypeStruct((nf, S, tf), BF16)] + extra_shape,
        scratch_shapes=[pltpu.VMEM((tm, D), F32), pltpu.VMEM((tm, D), BF16)])


def _rmsnorm_bwd(dh, xf, gain):
    r = lax.rsqrt(jnp.mean(xf * xf, axis=-1, keepdims=True) + NORM_EPS)
    xhat = xf * r
    dgain = jnp.sum(dh * xhat, axis=0, keepdims=True)
    dxh = dh * gain
    dx = r * (dxh - xhat * jnp.mean(dxh * xhat, axis=-1, keepdims=True))
    return dx, dgain


def _ffn_bwd(dxo, x, gain, g, u, wd, wg, wu, name, comm=None):
    S, D = x.shape
    nf, _, tf = g.shape
    tm = 512

    def body(dxo_ref, x_ref, gain_ref, g_ref, u_ref, wd_ref, wg_ref, wu_ref,
             dx_ref, dgain_ref, dg_ref, du_ref, act_ref, dout_ref, acc_ref, ds_ref):
        i = pl.program_id(0)
        j = pl.program_id(1)

        @pl.when(j == 0)
        def _():
            d = (0.5 * dxo_ref[...]).astype(BF16)
            ds_ref[...] = d
            dout_ref[...] = d
            acc_ref[...] = jnp.zeros_like(acc_ref)

        @pl.when((i == 0) & (j == 0))
        def _():
            dgain_ref[...] = jnp.zeros_like(dgain_ref)

        for half in range(2):
            rows = slice(half * (tm // 2), (half + 1) * (tm // 2))
            dact = _nt(ds_ref[rows, :], wd_ref[...])
            gv = g_ref[rows, :].astype(F32)
            uv = u_ref[rows, :].astype(F32)
            sg = _sigmoid(gv)
            silu = gv * sg
            act_ref[rows, :] = (silu * uv).astype(BF16)
            dgv = (dact * uv * (sg * (1.0 + gv * (1.0 - sg)))).astype(BF16)
            duv = (dact * silu).astype(BF16)
            dg_ref[rows, :] = dgv
            du_ref[rows, :] = duv
            acc_ref[rows, :] += _nn(dgv, wg_ref[...]) + _nn(duv, wu_ref[...])

        @pl.when(j == nf - 1)
        def _():
            dx, dgain = _rmsnorm_bwd(acc_ref[...], x_ref[...], gain_ref[...])
            dx_ref[...] = dxo_ref[...] + dx
            dgain_ref[...] += dgain

    return _call(
        body, (dxo, x, gain, g, u, wd, wg, wu), name=name, grid=(S // tm, nf), comm=comm,
        in_specs=[pl.BlockSpec((tm, D), lambda i, j: (i, 0)),
                  pl.BlockSpec((tm, D), lambda i, j: (i, 0)),
                  pl.BlockSpec((1, D), lambda i, j: (0, 0)),
                  pl.BlockSpec((None, tm, tf), lambda i, j: (j, i, 0)),
                  pl.BlockSpec((None, tm, tf), lambda i, j: (j, i, 0)),
                  pl.BlockSpec((None, tf, D), lambda i, j: (j, 0, 0)),
                  pl.BlockSpec((None, tf, D), lambda i, j: (j, 0, 0)),
                  pl.BlockSpec((None, tf, D), lambda i, j: (j, 0, 0))],
        out_specs=[pl.BlockSpec((tm, D), lambda i, j: (i, 0)),
                   pl.BlockSpec((1, D), lambda i, j: (0, 0)),
                   pl.BlockSpec((None, tm, tf), lambda i, j: (j, i, 0)),
                   pl.BlockSpec((None, tm, tf), lambda i, j: (j, i, 0)),
                   pl.BlockSpec((None, tm, tf), lambda i, j: (j, i, 0)),
                   pl.BlockSpec((tm, D), lambda i, j: (i, 0))],
        out_shape=[jax.ShapeDtypeStruct((S, D), F32), jax.ShapeDtypeStruct((1, D), F32),
                   jax.ShapeDtypeStruct((nf, S, tf), BF16), jax.ShapeDtypeStruct((nf, S, tf), BF16),
                   jax.ShapeDtypeStruct((nf, S, tf), BF16), jax.ShapeDtypeStruct((S, D), BF16)],
        scratch_shapes=[pltpu.VMEM((tm, D), F32), pltpu.VMEM((tm, D), BF16)])


def _matmul_tn(a, b, tm, tk, name):
    K, M = a.shape
    N = b.shape[1]

    def body(a_ref, b_ref, o_ref):
        @pl.when(pl.program_id(1) == 0)
        def _():
            o_ref[...] = jnp.zeros_like(o_ref)

        o_ref[...] += _tn(a_ref[...], b_ref[...])

    return pl.pallas_call(
        body, name=name, grid=(M // tm, K // tk),
        in_specs=[pl.BlockSpec((tk, tm), lambda i, k: (k, i)),
                  pl.BlockSpec((tk, N), lambda i, k: (k, 0))],
        out_specs=pl.BlockSpec((tm, N), lambda i, k: (i, 0)),
        out_shape=jax.ShapeDtypeStruct((M, N), F32),
        compiler_params=_params(2),
    )(a, b)


def _dw_chunks(a, b, tk, name, comm=None):
    nf, S, tf = a.shape
    N = b.shape[1]

    def body(a_ref, b_ref, o_ref):
        @pl.when(pl.program_id(1) == 0)
        def _():
            o_ref[...] = jnp.zeros_like(o_ref)

        o_ref[...] += _tn(a_ref[...], b_ref[...])

    (out,), landed = _call(
        body, (a, b), name=name, grid=(nf, S // tk), comm=comm,
        in_specs=[pl.BlockSpec((None, tk, tf), lambda j, k: (j, k, 0)),
                  pl.BlockSpec((tk, N), lambda j, k: (k, 0))],
        out_specs=[pl.BlockSpec((None, tf, N), lambda j, k: (j, 0, 0))],
        out_shape=[jax.ShapeDtypeStruct((nf, tf, N), F32)])
    return out, landed


VIEW_TILE = 512


def _view_spec(d, tile=VIEW_TILE):
    return pl.BlockSpec((tile // d, d * ATTN_WIDTH), lambda i: (i, 0))


def _view_shape(S, d, dtype):
    return jax.ShapeDtypeStruct((S // d, d * ATTN_WIDTH), dtype)


def _tile_to_views(val, planes, out_refs):
    for g in range(4):
        planes[g] = val[:, g * 128:(g + 1) * 128]
    for d, ref in zip(DILATIONS, out_refs):
        if d == 1:
            ref[...] = val.astype(ref.dtype)
            continue
        for r in range(d):
            for g in range(4):
                ref[:, r * ATTN_WIDTH + g * 128:r * ATTN_WIDTH + (g + 1) * 128] = (
                    planes[g, pl.ds(r, planes.shape[1] // d, stride=d), :].astype(ref.dtype))


def _view_to_tile(ref, d, planes):
    if d == 1:
        return ref[...].astype(F32)
    for r in range(d):
        for g in range(4):
            planes[g, pl.ds(r, planes.shape[1] // d, stride=d), :] = (
                ref[:, r * ATTN_WIDTH + g * 128:r * ATTN_WIDTH + (g + 1) * 128].astype(F32))
    return jnp.concatenate([planes[g] for g in range(4)], axis=1)


def _inproj_fwd(x, gain, w_in_p):
    S, D = x.shape
    tm = VIEW_TILE
    W = ATTN_WIDTH

    def body(x_ref, gain_ref, w_ref, h_ref, q1, q4, q16, k1, k4, k16, v1, v4, v16, dq_ref, dk_ref, dv_ref, gate_ref, bd_ref,
             planes):
        xf = x_ref[...]
        r = lax.rsqrt(jnp.mean(xf * xf, axis=-1, keepdims=True) + NORM_EPS)
        h = (xf * r * gain_ref[...]).astype(BF16)
        h_ref[...] = h
        _tile_to_views(_nt(h, w_ref[0:W, :]) * ATTN_SCALE, planes, (q1, q4, q16))
        _tile_to_views(_nt(h, w_ref[W:2 * W, :]), planes, (k1, k4, k16))
        _tile_to_views(_nt(h, w_ref[2 * W:3 * W, :]), planes, (v1, v4, v16))
        dq_ref[...] = _nt(h, w_ref[3 * W:4 * W, :])
        dk_ref[...] = _nt(h, w_ref[4 * W:5 * W, :])
        dv_ref[...] = _nt(h, w_ref[5 * W:6 * W, :])
        gate_ref[...] = _nt(h, w_ref[6 * W:7 * W, :])
        bd_ref[...] = _nt(h, w_ref[7 * W:7 * W + 128, :])

    tok = lambda w: pl.BlockSpec((tm, w), lambda i: (i, 0))
    return pl.pallas_call(
        body, name="inproj_fwd", grid=(S // tm,),
        in_specs=[tok(D), pl.BlockSpec((1, D), lambda i: (0, 0)),
                  pl.BlockSpec((IN_COLS_PADDED, D), lambda i: (0, 0))],
        out_specs=[tok(D)] + [_view_spec(d) for d in DILATIONS] * 3 + [tok(W)] * 4 + [tok(128)],
        out_shape=[jax.ShapeDtypeStruct((S, D), BF16)] + [_view_shape(S, d, BF16) for d in DILATIONS] * 3
                  + [jax.ShapeDtypeStruct((S, W), F32)] * 4 + [jax.ShapeDtypeStruct((S, 128), F32)],
        scratch_shapes=[pltpu.VMEM((4, tm, 128), F32)],
        compiler_params=_params(1),
    )(x, gain, w_in_p)


def _inproj_bwd(dxo, x, gain, attn_grads, dsecs, dbd, w_in_p):
    S, D = x.shape
    tm = VIEW_TILE
    W = ATTN_WIDTH

    def body(dxo_ref, x_ref, gain_ref, *rest):
        views, (s3, s4, s5, s6, dbd_ref, w_ref, dx_ref, dgain_ref, dproj_ref, planes) = rest[:9], rest[9:]

        @pl.when(pl.program_id(0) == 0)
        def _():
            dgain_ref[...] = jnp.zeros_like(dgain_ref)

        secs = []
        for k in range(3):
            parts = [_view_to_tile(views[3 * k + p], d, planes) for p, d in enumerate(DILATIONS)]
            secs.append(parts[0] + parts[1] + parts[2])
        secs += [s3[...], s4[...], s5[...], s6[...]]
        dh = jnp.zeros((tm, D), F32)
        for k, s in enumerate(secs):
            d = s.astype(BF16)
            dproj_ref[:, k * W:(k + 1) * W] = d
            dh += _nn(d, w_ref[k * W:(k + 1) * W, :])
        d = dbd_ref[...].astype(BF16)
        dproj_ref[:, 7 * W:7 * W + 128] = d
        dh += _nn(d, w_ref[7 * W:7 * W + 128, :])
        dx, dgain = _rmsnorm_bwd(dh, x_ref[...], gain_ref[...])
        dx_ref[...] = dxo_ref[...] + dx
        dgain_ref[...] += dgain

    tok = lambda w: pl.BlockSpec((tm, w), lambda i: (i, 0))
    return pl.pallas_call(
        body, name="inproj_bwd", grid=(S // tm,),
        in_specs=[tok(D), tok(D), pl.BlockSpec((1, D), lambda i: (0, 0))] + [_view_spec(d, tm) for d in DILATIONS] * 3
                 + [tok(W)] * 4 + [tok(128)] + [pl.BlockSpec((IN_COLS_PADDED, D), lambda i: (0, 0))],
        out_specs=[tok(D), pl.BlockSpec((1, D), lambda i: (0, 0)), tok(IN_COLS_PADDED)],
        out_shape=[jax.ShapeDtypeStruct((S, D), F32), jax.ShapeDtypeStruct((1, D), F32),
                   jax.ShapeDtypeStruct((S, IN_COLS_PADDED), BF16)],
        scratch_shapes=[pltpu.VMEM((4, tm, 128), F32)],
        compiler_params=_params(1),
    )(dxo, x, gain, *[g for grads in attn_grads for g in grads], *dsecs, dbd, w_in_p)


def _slope(h):
    return 2.0 ** (-8.0 * (h + 1) / ATTN_HEADS)


def _head_bias(steps, d, heads=tuple(range(ATTN_HEADS))):
    stepsf = steps.astype(F32)
    return jnp.stack([stepsf * (-_slope(h) * d) for h in heads])


def _hnt(a, b):
    return lax.dot_general(a, b, (((2,), (2,)), ((0,), (0,))), preferred_element_type=F32)


def _hnn(a, b):
    return lax.dot_general(a, b, (((2,), (1,)), ((0,), (0,))), preferred_element_type=F32)


def _blocks_per_step(nb):
    return next(n for n in (8, 4, 2, 1) if nb % n == 0)


def _query_step_specs(qb):
    B = ATTN_BLOCK
    cur = pl.BlockSpec((qb * B, ATTN_WIDTH), lambda r, n: (n, r))
    prev = pl.BlockSpec((B, ATTN_WIDTH), lambda r, n: (jnp.maximum(qb * n - 1, 0), r))
    return cur, prev


def _prev_block(prev_ref, cur_ref, sub, sl):
    B = ATTN_BLOCK
    return prev_ref[:, sl] if sub == 0 else cur_ref[(sub - 1) * B:sub * B, sl]


def _head_cols(tile, lo, big):
    return [_head_col(tile, lo, big), _head_col(tile, jnp.logical_not(lo), big)]


def _attn_fwd(q, k, v, d, name):
    L = q.shape[0]
    nb = L // ATTN_BLOCK
    B = ATTN_BLOCK
    QB = _blocks_per_step(nb)

    def body(q_ref, kp_ref, kc_ref, vp_ref, vc_ref, o_ref, lse_ref):
        n = pl.program_id(1)
        qi = lax.broadcasted_iota(jnp.int32, (B, 2 * B), 0)
        kj = lax.broadcasted_iota(jnp.int32, (B, 2 * B), 1)
        steps = qi + B - kj
        band = (steps >= 0) & (steps <= B)
        lo = lax.broadcasted_iota(jnp.int32, (B, 128), 1) < 64
        bias = _head_bias(steps, d)
        for sub in range(QB):
            rows = slice(sub * B, (sub + 1) * B)
            valid = band & ((kj >= B) | (n > 0)) if sub == 0 else band
            qs, ks, vs = [], [], []
            for G in range(4):
                sl = slice(G * 128, (G + 1) * 128)
                qg = q_ref[rows, sl]
                kg = jnp.concatenate([_prev_block(kp_ref, kc_ref, sub, sl), kc_ref[rows, sl]], axis=0)
                vg = jnp.concatenate([_prev_block(vp_ref, vc_ref, sub, sl), vc_ref[rows, sl]], axis=0)
                qs += [jnp.where(lo, qg, jnp.zeros_like(qg)), jnp.where(lo, jnp.zeros_like(qg), qg)]
                ks += [kg, kg]
                vs += [vg, vg]
            s = jnp.where(valid, _hnt(jnp.stack(qs), jnp.stack(ks)) + bias, NEG_BIG)
            m = jnp.max(s, axis=-1, keepdims=True)
            p = jnp.exp(s - m)
            l = jnp.sum(p, axis=-1, keepdims=True)
            o = _hnn(p.astype(BF16), jnp.stack(vs)) / l
            lse = m + jnp.log(l)
            for G in range(4):
                sl = slice(G * 128, (G + 1) * 128)
                o_ref[rows, sl] = jnp.where(lo, o[2 * G], o[2 * G + 1])
                lse_ref[rows, sl] = jnp.where(lo, lse[2 * G], lse[2 * G + 1])

    cur, prev = _query_step_specs(QB)
    return pl.pallas_call(
        body, name=name, grid=(d, nb // QB),
        in_specs=[cur, prev, cur, prev, cur],
        out_specs=[cur, cur],
        out_shape=[jax.ShapeDtypeStruct((L, d * ATTN_WIDTH), F32)] * 2,
        compiler_params=_params(2),
    )(q, k, k, v, v)


def _attn_merge(parts):
    S = parts[0][0].shape[0]
    tm = VIEW_TILE

    def body(o1, s1, o2, s2, o3, s3, o_ref, lse1, lse4, lse16, planes):
        outs, lses = [], []
        for d, (o, s) in zip(DILATIONS, ((o1, s1), (o2, s2), (o3, s3))):
            outs.append(_view_to_tile(o, d, planes))
            lses.append(_view_to_tile(s, d, planes))
        mx = jnp.maximum(jnp.maximum(lses[0], lses[1]), lses[2])
        es = [jnp.exp(s - mx) for s in lses]
        den = es[0] + es[1] + es[2]
        o_ref[...] = (es[0] * outs[0] + es[1] * outs[1] + es[2] * outs[2]) / den
        _tile_to_views(mx + jnp.log(den), planes, (lse1, lse4, lse16))

    views = [_view_spec(d) for d in DILATIONS]
    flat = [t for p in parts for t in p]
    return pl.pallas_call(
        body, name="attn_merge", grid=(S // tm,),
        in_specs=[views[p] for p in range(3) for _ in range(2)],
        out_specs=[views[0]] + views,
        out_shape=[jax.ShapeDtypeStruct((S, ATTN_WIDTH), F32)] + [_view_shape(S, d, F32) for d in DILATIONS],
        scratch_shapes=[pltpu.VMEM((4, tm, 128), F32)],
        compiler_params=_params(1),
    )(*flat)


def _head_col(t, msk, big):
    if big:
        return jnp.max(jnp.where(msk, t, NEG_BIG), axis=-1, keepdims=True)
    return jnp.sum(jnp.where(msk, t, 0.0), axis=-1, keepdims=True) * (1.0 / 64.0)


def _attn_bwd_q(q, k, v, do, lse, dd, d, name):
    L = q.shape[0]
    nb = L // ATTN_BLOCK
    B = ATTN_BLOCK
    QB = _blocks_per_step(nb)

    def body(q_ref, kp_ref, kc_ref, vp_ref, vc_ref, do_ref, lse_ref, dd_ref, dq_ref):
        n = pl.program_id(1)
        qi = lax.broadcasted_iota(jnp.int32, (B, 2 * B), 0)
        kj = lax.broadcasted_iota(jnp.int32, (B, 2 * B), 1)
        steps = qi + B - kj
        band = (steps >= 0) & (steps <= B)
        lo = lax.broadcasted_iota(jnp.int32, (B, 128), 1) < 64
        bias = _head_bias(steps, d)
        for sub in range(QB):
            rows = slice(sub * B, (sub + 1) * B)
            valid = band & ((kj >= B) | (n > 0)) if sub == 0 else band
            qs, ks, vs, dos, lses, dcols = [], [], [], [], [], []
            for G in range(4):
                sl = slice(G * 128, (G + 1) * 128)
                qg = q_ref[rows, sl]
                kg = jnp.concatenate([_prev_block(kp_ref, kc_ref, sub, sl), kc_ref[rows, sl]], axis=0)
                vg = jnp.concatenate([_prev_block(vp_ref, vc_ref, sub, sl), vc_ref[rows, sl]], axis=0)
                dog = do_ref[rows, sl]
                qs += [jnp.where(lo, qg, jnp.zeros_like(qg)), jnp.where(lo, jnp.zeros_like(qg), qg)]
                dos += [jnp.where(lo, dog, 0.0).astype(BF16), jnp.where(lo, 0.0, dog).astype(BF16)]
                ks += [kg, kg]
                vs += [vg, vg]
                lses += _head_cols(lse_ref[rows, sl], lo, True)
                dcols += _head_cols(dd_ref[rows, sl], lo, False)
            kb = jnp.stack(ks)
            s = _hnt(jnp.stack(qs), kb) + bias
            p = jnp.where(valid, jnp.exp(jnp.where(valid, s, NEG_BIG) - jnp.stack(lses)), 0.0)
            dp = _hnt(jnp.stack(dos), jnp.stack(vs))
            ds = p * (dp - jnp.stack(dcols))
            dq = _hnn(ds.astype(BF16), kb) * ATTN_SCALE
            for G in range(4):
                dq_ref[rows, G * 128:(G + 1) * 128] = jnp.where(lo, dq[2 * G], dq[2 * G + 1]).astype(BF16)

    cur, prev = _query_step_specs(QB)
    return pl.pallas_call(
        body, name=name, grid=(d, nb // QB), in_specs=[cur, prev, cur, prev, cur, cur, cur, cur], out_specs=cur,
        out_shape=jax.ShapeDtypeStruct((L, d * ATTN_WIDTH), BF16), compiler_params=_params(2),
    )(q, k, k, v, v, do, lse, dd)


def _attn_bwd_kv(q, k, v, do, lse, dd, d, name):
    L = q.shape[0]
    nb = L // ATTN_BLOCK
    B = ATTN_BLOCK
    KB = _blocks_per_step(nb)
    n_steps = nb // KB

    def body(k_ref, v_ref, qc_ref, qn_ref, doc_ref, don_ref, lsec_ref, lsen_ref, ddc_ref, ddn_ref, dk_ref, dv_ref):
        j = pl.program_id(1)
        qrow = lax.broadcasted_iota(jnp.int32, (2 * B, B), 0)
        kk = lax.broadcasted_iota(jnp.int32, (2 * B, B), 1)
        steps = qrow - kk
        band = (steps >= 0) & (steps <= B)
        lo2 = lax.broadcasted_iota(jnp.int32, (2 * B, 128), 1) < 64
        lo = lax.broadcasted_iota(jnp.int32, (B, 128), 1) < 64
        stepsf = steps.astype(F32)
        for sub in range(KB):
            rows = slice(sub * B, (sub + 1) * B)
            last = sub == KB - 1
            valid = band & ((qrow < B) | (j < n_steps - 1)) if last else band
            after = lambda cur_ref, nxt_ref, sl: nxt_ref[:, sl] if last else cur_ref[(sub + 1) * B:(sub + 2) * B, sl]
            for G in range(4):
                sl = slice(G * 128, (G + 1) * 128)
                kg = k_ref[rows, sl]
                vg = v_ref[rows, sl]
                qq = jnp.concatenate([qc_ref[rows, sl], after(qc_ref, qn_ref, sl)], axis=0)
                doo = jnp.concatenate([doc_ref[rows, sl], after(doc_ref, don_ref, sl)], axis=0)
                lse2 = jnp.concatenate([lsec_ref[rows, sl], after(lsec_ref, lsen_ref, sl)], axis=0)
                dd2 = jnp.concatenate([ddc_ref[rows, sl], after(ddc_ref, ddn_ref, sl)], axis=0)
                doo_b = doo.astype(BF16)
                dks, dvs = [], []
                for half in (0, 1):
                    msk = lo2 if half == 0 else jnp.logical_not(lo2)
                    qm = jnp.where(msk, qq, jnp.zeros_like(qq))
                    s = _nt(qm, kg) - (_slope(2 * G + half) * d) * stepsf
                    lse_c = _head_col(lse2, msk, True)
                    p = jnp.where(valid, jnp.exp(jnp.where(valid, s, NEG_BIG) - lse_c), 0.0)
                    dvs.append(_tn(p.astype(BF16), doo_b))
                    dom = jnp.where(msk, doo, 0.0).astype(BF16)
                    dp = _nt(dom, vg)
                    dcol = _head_col(dd2, msk, False)
                    ds = p * (dp - dcol)
                    dks.append(_tn(ds.astype(BF16), qq))
                dk_ref[rows, sl] = jnp.where(lo, dks[0], dks[1]).astype(BF16)
                dv_ref[rows, sl] = jnp.where(lo, dvs[0], dvs[1]).astype(BF16)

    cur = pl.BlockSpec((KB * B, ATTN_WIDTH), lambda r, j: (j, r))
    nxt = pl.BlockSpec((B, ATTN_WIDTH), lambda r, j: (jnp.minimum(KB * (j + 1), nb - 1), r))
    return pl.pallas_call(
        body, name=name, grid=(d, n_steps), in_specs=[cur, cur, cur, nxt, cur, nxt, cur, nxt, cur, nxt],
        out_specs=[cur, cur],
        out_shape=[jax.ShapeDtypeStruct((L, d * ATTN_WIDTH), BF16)] * 2, compiler_params=_params(2),
    )(k, v, q, q, do, do, lse, lse, dd, dd)


CONV_T = 512
HALO = 8


def _per_head(head, refs):
    for h in range(DN_HEADS):
        lanes = pl.ds(h * DN_HEAD_DIM, DN_HEAD_DIM)
        head(*[r.at[:, lanes] for r in refs[:-1]], refs[-1])


def _conv_taps(pad_ref, w, T):
    acc = pad_ref[pl.ds(HALO - 3, T), :] * w[0:1, :]
    for j in range(1, CONV_WIDTH):
        acc = acc + pad_ref[pl.ds(HALO - 3 + j, T), :] * w[j:j + 1, :]
    return acc


def _conv_fwd(xq, xk, xv, conv_w):
    S = xq.shape[0]
    T = CONV_T

    def body(*refs):
        _per_head(head, refs)

    def head(xq_ref, xqh_ref, xk_ref, xkh_ref, xv_ref, xvh_ref, wq_ref, wk_ref, wv_ref,
             qn_ref, kn_ref, v_ref, pad_ref):
        i = pl.program_id(0)

        def act(x_ref, xh_ref, w_ref):
            pad_ref[pl.ds(0, HALO), :] = jnp.where(i > 0, xh_ref[...], 0.0)
            pad_ref[pl.ds(HALO, T), :] = x_ref[...]
            c = _conv_taps(pad_ref, w_ref[...], T)
            return c * _sigmoid(c)

        def l2n(t):
            return t * lax.rsqrt(jnp.sum(t * t, axis=-1, keepdims=True) + L2_EPS)

        qn_ref[...] = l2n(act(xq_ref, xqh_ref, wq_ref))
        kn_ref[...] = l2n(act(xk_ref, xkh_ref, wk_ref))
        v_ref[...] = act(xv_ref, xvh_ref, wv_ref)

    tile = pl.BlockSpec((T, DN_WIDTH), lambda i: (i, 0))
    halo = pl.BlockSpec((HALO, DN_WIDTH), lambda i: (jnp.maximum(i * (T // HALO) - 1, 0), 0))
    wspec = lambda sec: pl.BlockSpec((CONV_WIDTH, DN_WIDTH), lambda i, sec=sec: (0, sec))
    return pl.pallas_call(
        body, name="dn_conv_fwd", grid=(S // T,),
        in_specs=[tile, halo, tile, halo, tile, halo, wspec(0), wspec(1), wspec(2)],
        out_specs=[tile, tile, tile],
        out_shape=[jax.ShapeDtypeStruct((S, DN_WIDTH), F32)] * 3,
        scratch_shapes=[pltpu.VMEM((T + HALO, 128), F32)],
        compiler_params=_params(1),
    )(xq, xq, xk, xk, xv, xv, conv_w, conv_w, conv_w)


def _conv_bwd_pre(xq, xk, xv, conv_w, dqn, dkn, dv):
    S = xq.shape[0]
    T = CONV_T

    def body(*refs):
        _per_head(head, refs)

    def head(xq_ref, xqh_ref, xk_ref, xkh_ref, xv_ref, xvh_ref, wq_ref, wk_ref, wv_ref,
             dqn_ref, dkn_ref, dv_ref, dcq_ref, dck_ref, dcv_ref, dwq_ref, dwk_ref, dwv_ref, pad_ref):
        i = pl.program_id(0)

        def one(x_ref, xh_ref, w_ref, dy_ref, dc_ref, dw_ref, normed):
            pad_ref[pl.ds(0, HALO), :] = jnp.where(i > 0, xh_ref[...], 0.0)
            pad_ref[pl.ds(HALO, T), :] = x_ref[...]
            c = _conv_taps(pad_ref, w_ref[...], T)
            sg = _sigmoid(c)
            a = c * sg
            dy = dy_ref[...]
            if normed:
                r = lax.rsqrt(jnp.sum(a * a, axis=-1, keepdims=True) + L2_EPS)
                y = a * r
                da = r * (dy - y * jnp.sum(dy * y, axis=-1, keepdims=True))
            else:
                da = dy
            dc = da * (sg * (1.0 + c * (1.0 - sg)))
            dc_ref[...] = dc

            @pl.when(i == 0)
            def _():
                dw_ref[...] = jnp.zeros_like(dw_ref)

            rows = [jnp.sum(dc * pad_ref[pl.ds(HALO - 3 + j, T), :], axis=0, keepdims=True) for j in range(CONV_WIDTH)]
            dw_ref[...] += jnp.concatenate(rows + [jnp.zeros((8 - CONV_WIDTH, 128), F32)], axis=0)

        one(xq_ref, xqh_ref, wq_ref, dqn_ref, dcq_ref, dwq_ref, True)
        one(xk_ref, xkh_ref, wk_ref, dkn_ref, dck_ref, dwk_ref, True)
        one(xv_ref, xvh_ref, wv_ref, dv_ref, dcv_ref, dwv_ref, False)

    tile = pl.BlockSpec((T, DN_WIDTH), lambda i: (i, 0))
    halo = pl.BlockSpec((HALO, DN_WIDTH), lambda i: (jnp.maximum(i * (T // HALO) - 1, 0), 0))
    wspec = lambda sec: pl.BlockSpec((CONV_WIDTH, DN_WIDTH), lambda i, sec=sec: (0, sec))
    dwspec = pl.BlockSpec((8, DN_WIDTH), lambda i: (0, 0))
    return pl.pallas_call(
        body, name="dn_conv_bwd_pre", grid=(S // T,),
        in_specs=[tile, halo, tile, halo, tile, halo, wspec(0), wspec(1), wspec(2), tile, tile, tile],
        out_specs=[tile, tile, tile, dwspec, dwspec, dwspec],
        out_shape=[jax.ShapeDtypeStruct((S, DN_WIDTH), F32)] * 3 + [jax.ShapeDtypeStruct((8, DN_WIDTH), F32)] * 3,
        scratch_shapes=[pltpu.VMEM((T + HALO, 128), F32)],
        compiler_params=_params(1),
    )(xq, xq, xk, xk, xv, xv, conv_w, conv_w, conv_w, dqn, dkn, dv)


def _conv_bwd_x(dcq, dck, dcv, conv_w):
    S = dcq.shape[0]
    T = CONV_T
    nt = S // T

    def body(*refs):
        _per_head(head, refs)

    def head(dq_ref, dqh_ref, dk_ref, dkh_ref, dv_ref, dvh_ref, wq_ref, wk_ref, wv_ref,
             oq_ref, ok_ref, ov_ref, pad_ref):
        i = pl.program_id(0)

        def one(d_ref, dh_ref, w_ref, o_ref):
            pad_ref[pl.ds(0, T), :] = d_ref[...]
            pad_ref[pl.ds(T, HALO), :] = jnp.where(i < nt - 1, dh_ref[...], 0.0)
            w = w_ref[...]
            acc = pad_ref[pl.ds(3, T), :] * w[0:1, :]
            for j in range(1, CONV_WIDTH):
                acc = acc + pad_ref[pl.ds(3 - j, T), :] * w[j:j + 1, :]
            o_ref[...] = acc

        one(dq_ref, dqh_ref, wq_ref, oq_ref)
        one(dk_ref, dkh_ref, wk_ref, ok_ref)
        one(dv_ref, dvh_ref, wv_ref, ov_ref)

    tile = pl.BlockSpec((T, DN_WIDTH), lambda i: (i, 0))
    halo = pl.BlockSpec((HALO, DN_WIDTH), lambda i: (jnp.minimum((i + 1) * (T // HALO), S // HALO - 1), 0))
    wspec = lambda sec: pl.BlockSpec((CONV_WIDTH, DN_WIDTH), lambda i, sec=sec: (0, sec))
    return pl.pallas_call(
        body, name="dn_conv_bwd_x", grid=(nt,),
        in_specs=[tile, halo, tile, halo, tile, halo, wspec(0), wspec(1), wspec(2)],
        out_specs=[tile, tile, tile],
        out_shape=[jax.ShapeDtypeStruct((S, DN_WIDTH), F32)] * 3,
        scratch_shapes=[pltpu.VMEM((T + HALO, 128), F32)],
        compiler_params=_params(1),
    )(dcq, dcq, dck, dck, dcv, dcv, conv_w, conv_w, conv_w)


PREP_CHUNKS = 4
SCAN_CHUNKS = 8


def _bnn(a, b):
    return lax.dot_general(a, b, (((2,), (1,)), ((0,), (0,))), preferred_element_type=F32, precision=HI)


def _bnt(a, b):
    return lax.dot_general(a, b, (((2,), (2,)), ((0,), (0,))), preferred_element_type=F32, precision=HI)


def _btn(a, b):
    return lax.dot_general(a, b, (((1,), (1,)), ((0,), (0,))), preferred_element_type=F32, precision=HI)


def _tri_inverse_b(a, blk, eye):
    dg = jnp.where(blk, a, 0.0)
    lo = a - dg
    d2 = _bnn(dg, dg)
    d4 = _bnn(d2, d2)
    d8 = _bnn(d4, d4)
    td = _bnn(_bnn(_bnn(eye - dg, eye + d2), eye + d4), eye + d8)
    b = _bnn(td, lo)
    b2 = _bnn(b, b)
    return _bnn(_bnn(eye - b, eye + b2), td)


def _dn_common_b(bds, avec, dvec, q_raw, k, v, t=None):
    C = DN_CHUNK
    lane = lax.broadcasted_iota(jnp.int32, (C, 128), 1)
    row = lax.broadcasted_iota(jnp.int32, (1, C, C), 1)
    col = lax.broadcasted_iota(jnp.int32, (1, C, C), 2)
    incl = row >= col
    strict = row > col
    eye = (row == col).astype(F32)
    blk = (row // 16) == (col // 16)
    pick = lambda tile, ln: jnp.sum(jnp.where(lane == ln, tile, 0.0), axis=-1, keepdims=True)
    betas, graws, zcs = [], [], []
    for bd in bds:
        z = bd + dvec
        g_all = -jnp.exp(avec) * (jnp.maximum(z, 0.0) + jnp.log(1.0 + jnp.exp(-jnp.abs(z))))
        beta_all = _sigmoid(bd)
        for h in range(DN_HEADS):
            betas.append(pick(beta_all, h))
            graws.append(pick(g_all, DN_HEADS + h))
            zcs.append(pick(z, DN_HEADS + h))
    beta, graw, zc = jnp.stack(betas), jnp.stack(graws), jnp.stack(zcs)
    to_row = lambda c: jnp.sum(eye * c, axis=1, keepdims=True)
    gc = jnp.sum(jnp.where(incl, to_row(graw), 0.0), axis=-1, keepdims=True)
    decay = jnp.exp(jnp.where(incl, gc - to_row(gc), NEG_BIG))
    q = q_raw * (DN_HEAD_DIM ** -0.5)
    kb = k * beta
    kk = _bnt(kb, k)
    if t is None:
        t = _tri_inverse_b(jnp.where(strict, kk * decay, 0.0), blk, eye)
    eg = jnp.exp(gc)
    rhs_w = kb * eg
    u = _bnn(t, v * beta)
    w = _bnn(t, rhs_w)
    qk = _bnt(q, k)
    aq = jnp.where(incl, qk * decay, 0.0)
    last = lax.broadcasted_iota(jnp.int32, (1, C, 1), 1) == C - 1
    g_last = jnp.sum(jnp.where(last, gc, 0.0), axis=1, keepdims=True)
    ekd = jnp.exp(g_last - gc)
    return dict(beta=beta, graw=graw, zc=zc, gc=gc, decay=decay, q=q, kb=kb, kk=kk, t=t, eg=eg, rhs_w=rhs_w,
                u=u, w=w, qk=qk, aq=aq, g_last=g_last, ekd=ekd, kd=k * ekd, qg=q * eg,
                incl=incl, strict=strict, eye=eye, lane=lane, row=row, col=col, last=last)


def _stack_heads(ref, rows):
    return jnp.stack([ref[rows, h * DN_HEAD_DIM:(h + 1) * DN_HEAD_DIM] for h in range(DN_HEADS)])


def _stack_units(ref, nc):
    C = DN_CHUNK
    return jnp.concatenate([_stack_heads(ref, slice(ci * C, (ci + 1) * C)) for ci in range(nc)], axis=0)


def _store_units(ref, val, nc):
    C = DN_CHUNK
    for ci in range(nc):
        for h in range(DN_HEADS):
            ref[ci * C:(ci + 1) * C, h * DN_HEAD_DIM:(h + 1) * DN_HEAD_DIM] = val[ci * DN_HEADS + h]


def _dn_prep(qn, kn, v, bd, avec, dvec):
    S = qn.shape[0]
    C = DN_CHUNK
    N = S // C
    nc = PREP_CHUNKS

    def body(q_ref, k_ref, v_ref, bd_ref, a_ref, d_ref, u_ref, w_ref, qg_ref, kd_ref, aq_ref, t_ref, egl_ref):
        bds = [bd_ref[ci * C:(ci + 1) * C, :] for ci in range(nc)]
        c = _dn_common_b(bds, a_ref[...], d_ref[...], _stack_units(q_ref, nc), _stack_units(k_ref, nc), _stack_units(v_ref, nc))
        _store_units(u_ref, c["u"], nc)
        _store_units(w_ref, c["w"], nc)
        _store_units(qg_ref, c["qg"], nc)
        _store_units(kd_ref, c["kd"], nc)
        egl = jnp.broadcast_to(jnp.exp(c["g_last"]), (nc * DN_HEADS, 1, 128))
        for ci in range(nc):
            for h in range(DN_HEADS):
                aq_ref[h, ci * C:(ci + 1) * C, :] = c["aq"][ci * DN_HEADS + h]
                t_ref[h, ci * C:(ci + 1) * C, :] = c["t"][ci * DN_HEADS + h]
            egl_ref[ci * 8:(ci + 1) * 8, :] = jnp.concatenate(
                [egl[ci * DN_HEADS + h] for h in range(DN_HEADS)] + [jnp.zeros((8 - DN_HEADS, 128), F32)], axis=0)

    tok = lambda w: pl.BlockSpec((nc * C, w), lambda n: (n, 0))
    sq = pl.BlockSpec((DN_HEADS, nc * C, C), lambda n: (0, n, 0))
    vec = pl.BlockSpec((1, 128), lambda n: (0, 0))
    return pl.pallas_call(
        body, name="dn_prep", grid=(N // nc,),
        in_specs=[tok(DN_WIDTH)] * 3 + [tok(128), vec, vec],
        out_specs=[tok(DN_WIDTH)] * 4 + [sq, sq, pl.BlockSpec((nc * 8, 128), lambda n: (n, 0))],
        out_shape=[jax.ShapeDtypeStruct((S, DN_WIDTH), F32)] * 4 + [jax.ShapeDtypeStruct((DN_HEADS, S, C), F32)] * 2
                  + [jax.ShapeDtypeStruct((N * 8, 128), F32)],
        compiler_params=_params(1),
    )(qn, kn, v, bd, avec, dvec)


def _dn_scan_fwd(u, w, qg, kd, aq, egl, gate, dn_gain):
    S = u.shape[0]
    C = DN_CHUNK
    N = S // C
    HD = DN_HEAD_DIM
    nc = SCAN_CHUNKS

    def body(u_ref, w_ref, qg_ref, kd_ref, aq_ref, egl_ref, gate_ref, gain_ref, dn_ref, o_ref, vn_ref, st_ref, state_ref):
        @pl.when(pl.program_id(0) == 0)
        def _():
            state_ref[...] = jnp.zeros_like(state_ref)

        gain = gain_ref[...]
        for ci in range(nc):
            rows = slice(ci * C, (ci + 1) * C)
            st = state_ref[...]
            for h in range(DN_HEADS):
                st_ref[ci * DN_WIDTH + h * HD:ci * DN_WIDTH + (h + 1) * HD, :] = st[h]
            v_new = _stack_heads(u_ref, rows) - _bnn(_stack_heads(w_ref, rows), st)
            o = _bnn(_stack_heads(qg_ref, rows), st) + _bnn(aq_ref[:, rows, :], v_new)
            egl = jnp.stack([egl_ref[ci * 8 + h:ci * 8 + h + 1, :] for h in range(DN_HEADS)])
            state_ref[...] = st * egl + _btn(_stack_heads(kd_ref, rows), v_new)
            r = lax.rsqrt(jnp.mean(o * o, axis=-1, keepdims=True) + NORM_EPS)
            gt = _stack_heads(gate_ref, rows)
            dn = o * r * gain * (gt * _sigmoid(gt))
            for h in range(DN_HEADS):
                sl = slice(h * HD, (h + 1) * HD)
                vn_ref[rows, sl] = v_new[h]
                o_ref[rows, sl] = o[h]
                dn_ref[rows, sl] = dn[h]

    tok = lambda wd: pl.BlockSpec((nc * C, wd), lambda n: (n, 0))
    sq = pl.BlockSpec((DN_HEADS, nc * C, C), lambda n: (0, n, 0))
    vec = pl.BlockSpec((1, 128), lambda n: (0, 0))
    return pl.pallas_call(
        body, name="dn_scan_fwd", grid=(N // nc,),
        in_specs=[tok(DN_WIDTH)] * 4 + [sq, pl.BlockSpec((nc * 8, 128), lambda n: (n, 0)), tok(DN_WIDTH), vec],
        out_specs=[tok(DN_WIDTH)] * 3 + [pl.BlockSpec((nc * DN_WIDTH, HD), lambda n: (n, 0))],
        out_shape=[jax.ShapeDtypeStruct((S, DN_WIDTH), F32)] * 3 + [jax.ShapeDtypeStruct((N * DN_WIDTH, HD), F32)],
        scratch_shapes=[pltpu.VMEM((DN_HEADS, HD, HD), F32)],
        compiler_params=_params(1),
    )(u, w, qg, kd, aq, egl, gate, dn_gain)


def _dn_scan_bwd(w, qg, kd, aq, egl, gate, dn_gain, o, ddn):
    S = w.shape[0]
    C = DN_CHUNK
    N = S // C
    HD = DN_HEAD_DIM
    nc = SCAN_CHUNKS

    def body(w_ref, qg_ref, kd_ref, aq_ref, egl_ref, gate_ref, gain_ref, o_ref, ddn_ref,
             do_ref, dvn_ref, dgate_ref, dst_ref, small_ref, dstate_ref):
        @pl.when(pl.program_id(0) == 0)
        def _():
            dstate_ref[...] = jnp.zeros_like(dstate_ref)
            small_ref[...] = jnp.zeros_like(small_ref)

        gain = gain_ref[...]
        d_gain = jnp.zeros((1, 128), F32)
        for ci in reversed(range(nc)):
            rows = slice(ci * C, (ci + 1) * C)
            dsn = dstate_ref[...]
            for h in range(DN_HEADS):
                dst_ref[ci * DN_WIDTH + h * HD:ci * DN_WIDTH + (h + 1) * HD, :] = dsn[h]
            ov = _stack_heads(o_ref, rows)
            r = lax.rsqrt(jnp.mean(ov * ov, axis=-1, keepdims=True) + NORM_EPS)
            on = ov * r
            gt = _stack_heads(gate_ref, rows)
            sgt = _sigmoid(gt)
            silu_g = gt * sgt
            dy = _stack_heads(ddn_ref, rows)
            d_gain = d_gain + jnp.sum(jnp.sum(dy * on * silu_g, axis=1, keepdims=True), axis=0)
            dgate = dy * on * gain * (sgt * (1.0 + gt * (1.0 - sgt)))
            don = dy * gain * silu_g
            do = r * (don - on * jnp.mean(don * on, axis=-1, keepdims=True))
            d_vnew = _btn(aq_ref[:, rows, :], do) + _bnn(_stack_heads(kd_ref, rows), dsn)
            egl = jnp.stack([egl_ref[ci * 8 + h:ci * 8 + h + 1, :] for h in range(DN_HEADS)])
            dstate_ref[...] = _btn(_stack_heads(qg_ref, rows), do) + dsn * egl - _btn(_stack_heads(w_ref, rows), d_vnew)
            for h in range(DN_HEADS):
                sl = slice(h * HD, (h + 1) * HD)
                do_ref[rows, sl] = do[h]
                dvn_ref[rows, sl] = d_vnew[h]
                dgate_ref[rows, sl] = dgate[h]
        small_ref[...] += jnp.concatenate([d_gain, jnp.zeros((7, 128), F32)], axis=0)

    nb = N // nc
    tok = lambda wd: pl.BlockSpec((nc * C, wd), lambda i: (nb - 1 - i, 0))
    sq = pl.BlockSpec((DN_HEADS, nc * C, C), lambda i: (0, nb - 1 - i, 0))
    vec = pl.BlockSpec((1, 128), lambda i: (0, 0))
    return pl.pallas_call(
        body, name="dn_scan_bwd", grid=(nb,),
        in_specs=[tok(DN_WIDTH)] * 3 + [sq, pl.BlockSpec((nc * 8, 128), lambda i: (nb - 1 - i, 0)), tok(DN_WIDTH), vec,
                                       tok(DN_WIDTH), tok(DN_WIDTH)],
        out_specs=[tok(DN_WIDTH)] * 3 + [pl.BlockSpec((nc * DN_WIDTH, HD), lambda i: (nb - 1 - i, 0)),
                                        pl.BlockSpec((8, 128), lambda i: (0, 0))],
        out_shape=[jax.ShapeDtypeStruct((S, DN_WIDTH), F32)] * 3 + [jax.ShapeDtypeStruct((N * DN_WIDTH, HD), F32),
                                                                  jax.ShapeDtypeStruct((8, 128), F32)],
        scratch_shapes=[pltpu.VMEM((DN_HEADS, HD, HD), F32)],
        compiler_params=_params(1),
    )(w, qg, kd, aq, egl, gate, dn_gain, o, ddn)


def _dn_post(qn, kn, v, bd, avec, dvec, t_inv, v_new_all, states, dstates, do_all, dvn_all, comm=None):
    S = qn.shape[0]
    C = DN_CHUNK
    N = S // C
    HD = DN_HEAD_DIM
    nc = PREP_CHUNKS
    B = nc * DN_HEADS

    def body(q_ref, k_ref, v_ref, bd_ref, a_ref, d_ref, t_ref, vn_ref, st_ref, dst_ref, do_ref, dvn_ref,
             dq_ref, dk_ref, dv_ref, dbd_ref, small_ref):
        @pl.when(pl.program_id(0) == 0)
        def _():
            small_ref[...] = jnp.zeros_like(small_ref)

        avec = a_ref[...]
        bds = [bd_ref[ci * C:(ci + 1) * C, :] for ci in range(nc)]
        k = _stack_units(k_ref, nc)
        vv = _stack_units(v_ref, nc)
        t = jnp.concatenate([t_ref[:, ci * C:(ci + 1) * C, :] for ci in range(nc)], axis=0)
        c = _dn_common_b(bds, avec, d_ref[...], _stack_units(q_ref, nc), k, vv, t=t)
        q, kb, eg, u, w = c["q"], c["kb"], c["eg"], c["u"], c["w"]
        beta, decay, incl, strict, eye = c["beta"], c["decay"], c["incl"], c["strict"], c["eye"]
        st = jnp.stack([st_ref[b * HD:(b + 1) * HD, :] for b in range(B)])
        dsn = jnp.stack([dst_ref[b * HD:(b + 1) * HD, :] for b in range(B)])
        v_new = _stack_units(vn_ref, nc)
        do = _stack_units(do_ref, nc)
        d_vnew = _stack_units(dvn_ref, nc)
        egl = jnp.exp(c["g_last"])
        daq = jnp.where(incl, _bnt(do, v_new), 0.0)
        d_qg = _bnt(do, st)
        d_kd = _bnt(v_new, dsn)
        d_glast = jnp.sum(jnp.sum(dsn * st, axis=-1, keepdims=True), axis=1, keepdims=True) * egl
        d_w = -_bnt(d_vnew, st)
        d_ru = _btn(t, d_vnew)
        d_rw = _btn(t, d_w)
        da = -jnp.where(strict, _bnt(d_ru, u) + _bnt(d_rw, w), 0.0)
        dv = d_ru * beta
        dbeta = jnp.sum(d_ru * vv, axis=-1, keepdims=True)
        dkb = d_rw * eg
        dgc = jnp.sum(d_rw * c["rhs_w"], axis=-1, keepdims=True)
        dkk = da * decay
        ddecay = da * c["kk"]
        dkb = dkb + _bnn(dkk, k)
        dk = _btn(dkk, kb)
        dqk = daq * decay
        ddecay = ddecay + daq * c["qk"]
        dq = _bnn(dqk, k)
        dk = dk + _btn(dqk, q)
        m = ddecay * decay
        col_sum = jnp.sum(m, axis=1, keepdims=True)
        dgc = dgc + jnp.sum(m, axis=-1, keepdims=True) - jnp.sum(eye * col_sum, axis=-1, keepdims=True)
        dq = dq + d_qg * eg
        dgc = dgc + jnp.sum(d_qg * c["qg"], axis=-1, keepdims=True)
        dk = dk + d_kd * c["ekd"]
        tk = jnp.sum(d_kd * c["kd"], axis=-1, keepdims=True)
        dgc = dgc - tk
        d_glast = d_glast + jnp.sum(tk, axis=1, keepdims=True)
        dk = dk + dkb * beta
        dbeta = dbeta + jnp.sum(dkb * k, axis=-1, keepdims=True)
        dgc = dgc + jnp.where(c["last"], d_glast, 0.0)
        dgc_row = jnp.sum(eye * dgc, axis=1, keepdims=True)
        dgraw = jnp.sum(jnp.where(c["col"] >= c["row"], dgc_row, 0.0), axis=-1, keepdims=True)
        _store_units(dq_ref, dq * (HD ** -0.5), nc)
        _store_units(dk_ref, dk, nc)
        _store_units(dv_ref, dv, nc)
        dbraw = dbeta * beta * (1.0 - beta)
        dzc = dgraw * _sigmoid(c["zc"])
        ga = dgraw * c["graw"]
        lane = c["lane"]
        lane1 = lax.broadcasted_iota(jnp.int32, (1, 128), 1)
        neg_ea = -jnp.exp(avec)
        d_alog = jnp.zeros((1, 128), F32)
        d_dt = jnp.zeros((1, 128), F32)
        for ci in range(nc):
            dbd = jnp.zeros((C, 128), F32)
            for h in range(DN_HEADS):
                b = ci * DN_HEADS + h
                dz = dzc[b] * neg_ea
                dbd = dbd + jnp.where(lane == h, dbraw[b], 0.0) + jnp.where(lane == DN_HEADS + h, dz, 0.0)
                d_alog = d_alog + jnp.where(lane1 == DN_HEADS + h, jnp.sum(ga[b], axis=0, keepdims=True), 0.0)
                d_dt = d_dt + jnp.where(lane1 == DN_HEADS + h, jnp.sum(dz, axis=0, keepdims=True), 0.0)
            dbd_ref[ci * C:(ci + 1) * C, :] = dbd
        small_ref[...] += jnp.concatenate([d_alog, d_dt, jnp.zeros((6, 128), F32)], axis=0)

    tok = lambda wd: pl.BlockSpec((nc * C, wd), lambda n: (n, 0))
    big = pl.BlockSpec((nc * DN_WIDTH, HD), lambda n: (n, 0))
    sq = pl.BlockSpec((DN_HEADS, nc * C, C), lambda n: (0, n, 0))
    vec = pl.BlockSpec((1, 128), lambda n: (0, 0))
    return _call(
        body, (qn, kn, v, bd, avec, dvec, t_inv, v_new_all, states, dstates, do_all, dvn_all),
        name="dn_post", grid=(N // nc,), comm=comm,
        in_specs=[tok(DN_WIDTH)] * 3 + [tok(128), vec, vec, sq, tok(DN_WIDTH), big, big, tok(DN_WIDTH), tok(DN_WIDTH)],
        out_specs=[tok(DN_WIDTH)] * 3 + [tok(128), pl.BlockSpec((8, 128), lambda n: (0, 0))],
        out_shape=[jax.ShapeDtypeStruct((S, DN_WIDTH), F32)] * 3 + [jax.ShapeDtypeStruct((S, 128), F32),
                                                                  jax.ShapeDtypeStruct((8, 128), F32)])


def _outproj_fwd(x, attn, dn, w_out):
    S, D = x.shape
    tm = 512

    def body(x_ref, a_ref, d_ref, w_ref, xo_ref, mix_ref):
        a = a_ref[...].astype(BF16)
        dd = d_ref[...].astype(BF16)
        mix_ref[:, 0:ATTN_WIDTH] = a
        mix_ref[:, ATTN_WIDTH:] = dd
        xo_ref[...] = x_ref[...] + _nn(a, w_ref[0:ATTN_WIDTH, :]) + _nn(dd, w_ref[ATTN_WIDTH:, :])

    tok = lambda w: pl.BlockSpec((tm, w), lambda i: (i, 0))
    return pl.pallas_call(
        body, name="outproj_fwd", grid=(S // tm,),
        in_specs=[tok(D), tok(ATTN_WIDTH), tok(DN_WIDTH), pl.BlockSpec((D, D), lambda i: (0, 0))],
        out_specs=[tok(D), tok(D)],
        out_shape=[jax.ShapeDtypeStruct((S, D), F32), jax.ShapeDtypeStruct((S, D), BF16)],
        compiler_params=_params(1),
    )(x, attn, dn, w_out)


def _outproj_bwd(dx, w_out, attn, comm=None):
    S, D = dx.shape
    tm = VIEW_TILE

    def body(dx_ref, w_ref, attn_ref, da1, da4, da16, dl1, dl4, dl16, ddn_ref, dxb_ref, planes):
        d = dx_ref[...].astype(BF16)
        dxb_ref[...] = d
        da = _nt(d, w_ref[0:ATTN_WIDTH, :])
        ddn_ref[...] = _nt(d, w_ref[ATTN_WIDTH:, :])
        _tile_to_views(da, planes, (da1, da4, da16))
        lo = lax.broadcasted_iota(jnp.int32, (tm, 128), 1) < 64
        cols = []
        for G in range(4):
            sl = slice(G * 128, (G + 1) * 128)
            t = da[:, sl] * attn_ref[:, sl]
            d0 = jnp.sum(jnp.where(lo, t, 0.0), axis=-1, keepdims=True)
            d1 = jnp.sum(jnp.where(lo, 0.0, t), axis=-1, keepdims=True)
            cols.append(jnp.where(lo, d0, d1))
        _tile_to_views(jnp.concatenate(cols, axis=1), planes, (dl1, dl4, dl16))

    tok = lambda w: pl.BlockSpec((tm, w), lambda i: (i, 0))
    views = [_view_spec(d) for d in DILATIONS]
    return _call(
        body, (dx, w_out, attn), name="outproj_bwd", grid=(S // tm,), comm=comm,
        in_specs=[tok(D), pl.BlockSpec((D, D), lambda i: (0, 0)), tok(ATTN_WIDTH)],
        out_specs=views + views + [tok(DN_WIDTH), tok(D)],
        out_shape=[_view_shape(S, d, F32) for d in DILATIONS] * 2
                  + [jax.ShapeDtypeStruct((S, DN_WIDTH), F32), jax.ShapeDtypeStruct((S, D), BF16)],
        scratch_shapes=[pltpu.VMEM((4, tm, 128), F32)])


def _adamw(w, g, m, v, name):
    R, Ccols = w.shape[0], w.shape[-1]
    tr = next((t for t in range(512, 7, -8) if R % t == 0), R)
    c1 = 1.0 - ADAM_B1 ** ADAM_STEP
    c2 = 1.0 - ADAM_B2 ** ADAM_STEP

    def body(w_ref, g_ref, m_ref, v_ref, d_ref, nm_ref, nv_ref):
        gv = g_ref[...]
        mn = ADAM_B1 * m_ref[...] + (1.0 - ADAM_B1) * gv
        vn = ADAM_B2 * v_ref[...] + (1.0 - ADAM_B2) * (gv * gv)
        nm_ref[...] = mn
        nv_ref[...] = vn
        d_ref[...] = -ADAM_LR * ((mn / c1) / (jnp.sqrt(vn / c2) + ADAM_EPS) + ADAM_WD * w_ref[...])

    if w.ndim == 2:
        grid, spec = (R // tr,), pl.BlockSpec((tr, Ccols), lambda i: (i, 0))
    else:
        grid, spec = (2,), pl.BlockSpec((R // 2, 1, Ccols), lambda i: (i, 0, 0))
    return pl.pallas_call(
        body, name=name, grid=grid, in_specs=[spec] * 4, out_specs=[spec] * 3,
        out_shape=[jax.ShapeDtypeStruct(w.shape, F32)] * 3, compiler_params=_params(1),
    )(w, g, m, v)


LATE_WEIGHTS = ("w_in", "w_out", "ffn2_gate", "ffn2_up", "ffn2_down")


def _local_step(x, target, wts, small, dist=None):
    g1, g2, gm, gf = small["norm_ffn1"], small["norm_ffn2"], small["norm_mix"], small["norm_final"]
    wts = dict(wts)

    def reduce_start(gs, tag):
        return _rs_add_pairs(gs, _swap_sibling(gs, True, "rs_swap_halves_" + tag), dist["c"], "rs_add_pairs_" + tag)

    (x1, h1, fg1, fu1), late = _ffn_fwd(x, g1, wts["ffn1_gate"], wts["ffn1_up"], wts["ffn1_down"], "ffn1_fwd",
                                        comm=_ag_comm(dist["late"]) if dist else None)
    if dist:
        wts.update(zip(LATE_WEIGHTS, late))
        wts["w_out"] = wts["w_out"].reshape(D_MODEL, D_MODEL)
        wts["w_in"] = _permute_w_in(wts["w_in"][:, :IN_COLS // N_CHIPS].reshape(IN_COLS, D_MODEL))
    h2, *qkv, xq, xk, xv, gate, bd = _inproj_fwd(x1, gm, wts["w_in"])
    aq, ak, av = qkv[0:3], qkv[3:6], qkv[6:9]
    parts = [_attn_fwd(aq[p], ak[p], av[p], d, f"attn_fwd_d{d}") for p, d in enumerate(DILATIONS)]
    attn, *lse = _attn_merge(parts)
    conv_w = small["conv_w"]
    qn, kn, vv = _conv_fwd(xq, xk, xv, conv_w)
    dn_u, dn_w, dn_qg, dn_kd, dn_aq, dn_t, dn_egl = _dn_prep(qn, kn, vv, bd, small["avec"], small["dvec"])
    dn, o_dn, v_new, states = _dn_scan_fwd(dn_u, dn_w, dn_qg, dn_kd, dn_aq, dn_egl, gate, small["dn_norm"])
    x2, mix = _outproj_fwd(x1, attn, dn, wts["w_out"])
    (dx3, h3, fg2, fu2, loss, d_gf), _ = _ffn_fwd(x2, g2, wts["ffn2_gate"], wts["ffn2_up"], wts["ffn2_down"], "ffn2_fwd",
                                                 head=(gf, target))

    grads = {}
    (dx2, d_g2, dfg2, dfu2, act2, dout2), _ = _ffn_bwd(dx3, x2, g2, fg2, fu2, wts["ffn2_down"], wts["ffn2_gate"],
                                                      wts["ffn2_up"], "ffn2_bwd")
    tk = 2048
    grads["ffn2_gate"], _ = _dw_chunks(dfg2, h3, tk, "dw_ffn2_gate")
    grads["ffn2_up"], _ = _dw_chunks(dfu2, h3, tk, "dw_ffn2_up")
    grads["ffn2_down"], _ = _dw_chunks(act2, dout2, tk, "dw_ffn2_down")
    group_a = ("ffn2_gate", "ffn2_up", "ffn2_down")
    gs_a = [grads[n] for n in group_a]

    (*dviews, ddn, dx2b), swapped_a = _outproj_bwd(dx2, wts["w_out"], attn, comm=_swap_comm(gs_a, True) if dist else None)
    parts_a = _rs_add_pairs(gs_a, swapped_a, dist["c"], "rs_add_pairs_a") if dist else None
    dattn, dd = dviews[0:3], dviews[3:6]
    grads["w_out"] = _matmul_tn(mix, dx2b, D_MODEL, tk, "dw_out").reshape(N_CHIPS, D_MODEL // N_CHIPS, D_MODEL)

    daq, dak, dav = [], [], []
    for p, d in enumerate(DILATIONS):
        daq.append(_attn_bwd_q(aq[p], ak[p], av[p], dattn[p], lse[p], dd[p], d, f"attn_bwd_q_d{d}"))
        dk_p, dv_p = _attn_bwd_kv(aq[p], ak[p], av[p], dattn[p], lse[p], dd[p], d, f"attn_bwd_kv_d{d}")
        dak.append(dk_p)
        dav.append(dv_p)

    do_dn, dvn, dgate, dstates, d_dn_gain = _dn_scan_bwd(dn_w, dn_qg, dn_kd, dn_aq, dn_egl, gate, small["dn_norm"], o_dn, ddn)
    (dqn, dkn, dvv, dbd, dn_small), recv_a = _dn_post(qn, kn, vv, bd, small["avec"], small["dvec"], dn_t, v_new, states,
                                                      dstates, do_dn, dvn, comm=_rsx_comm(parts_a) if dist else None)
    dcq, dck, dcv, dwq, dwk, dwv = _conv_bwd_pre(xq, xk, xv, conv_w, dqn, dkn, dvv)
    dxq, dxk, dxv = _conv_bwd_x(dcq, dck, dcv, conv_w)
    d_conv = jnp.concatenate([dwq[:CONV_WIDTH], dwk[:CONV_WIDTH], dwv[:CONV_WIDTH]], axis=1)

    dx1, d_gm, dproj = _inproj_bwd(dx2, x1, gm, [daq, dak, dav], [dxq, dxk, dxv, dgate], dbd, wts["w_in"])
    gi = _matmul_tn(dproj, h2, IN_COLS_PADDED, 512, "dw_in")
    if dist:
        gate_end = QKV_COLS + DN_WIDTH
        gi = jnp.concatenate([gi[:QKV_COLS], gi[gate_end:gate_end + LOGIT_COLS], gi[QKV_COLS:gate_end]], axis=0)
        gi = gi.reshape(N_CHIPS, IN_COLS // N_CHIPS, D_MODEL)
        gi = jnp.pad(gi, ((0, 0), (0, W_IN_ROWS - IN_COLS // N_CHIPS), (0, 0)))
    grads["w_in"] = gi
    group_b = ("w_in", "w_out")
    parts_b = reduce_start([grads[n] for n in group_b], "b") if dist else None

    (dx0, d_g1, dfg1, dfu1, act1, dout1), recv_b = _ffn_bwd(dx1, x, g1, fg1, fu1, wts["ffn1_down"], wts["ffn1_gate"],
                                                           wts["ffn1_up"], "ffn1_bwd",
                                                           comm=_rsx_comm(parts_b) if dist else None)
    group_c = ("ffn1_gate", "ffn1_up", "ffn1_down")
    pending, parts_c, recv_c = [], [], []
    for n, (lhs, rhs) in zip(group_c, ((dfg1, h1), (dfu1, h1), (act1, dout1))):
        grads[n], landed = _dw_chunks(lhs, rhs, tk, "dw_" + n, comm=_rsx_comm(pending) if pending else None)
        recv_c += list(landed)
        if dist:
            pending = reduce_start([grads[n]], n)
            parts_c += pending

    small_grads = dict(norm_ffn1=d_g1, norm_mix=d_gm, norm_ffn2=d_g2, norm_final=d_gf, conv_w=d_conv,
                       a_log=dn_small[0:1], dt_bias=dn_small[1:2], dn_norm=d_dn_gain[0:1])
    if dist:
        recv_c += _rs_exchange_arrays(pending)
        names = group_a + group_b + group_c
        totals = _rs_add_totals(list(parts_a) + list(parts_b) + list(parts_c), list(recv_a) + list(recv_b) + list(recv_c),
                                dist["chip"])
        theirs = _swap_sibling(totals, False, "rs_share_total")
        grads = {n: (mine, other) for n, mine, other in zip(names, totals, theirs)}
    return loss, dx0, grads, small_grads


HBM =pl.BlockSpec(memory_space=pl.ANY)
VMEM_SPEC = pl.BlockSpec(memory_space=pltpu.VMEM)


def _coords():
    return lax.axis_index("x"), lax.axis_index("y"), lax.axis_index("c")


def _remote(src, dst, send_sems, recv_sems, k, dev):
    return pltpu.make_async_remote_copy(src_ref=src, dst_ref=dst, send_sem=send_sems.at[k], recv_sem=recv_sems.at[k],
                                        device_id=dev, device_id_type=MESH)


def _allreduce_small(buf, name):
    R, Cc = buf.shape

    def body(src_ref, out_ref, recv_ref, send_sems, recv_sems):
        x, y, c = _coords()
        copies = []
        for m in range(1, 8):
            fx, fy, fc = (m >> 2) & 1, (m >> 1) & 1, m & 1
            dev = (x ^ fx if fx else x, y ^ fy if fy else y, c ^ fc if fc else c)
            cp = _remote(src_ref, recv_ref.at[m - 1], send_sems, recv_sems, m - 1, dev)
            cp.start()
            copies.append(cp)
        for cp in copies:
            cp.wait()
        r = [src_ref[...]] + [recv_ref[m] for m in range(7)]
        out_ref[...] = ((r[0] + r[1]) + (r[2] + r[3])) + ((r[4] + r[5]) + (r[6] + r[7]))

    return pl.pallas_call(
        body, name=name, out_shape=jax.ShapeDtypeStruct((R, Cc), F32),
        in_specs=[VMEM_SPEC], out_specs=VMEM_SPEC,
        scratch_shapes=[pltpu.VMEM((7, R, Cc), F32), pltpu.SemaphoreType.DMA((7,)), pltpu.SemaphoreType.DMA((7,))],
    )(buf)


BIG = ("ffn1_gate", "ffn1_up", "ffn1_down", "w_in", "w_out", "ffn2_gate", "ffn2_up", "ffn2_down")
ROW_SHARDED = ("ffn1_down", "w_out", "ffn2_down")
W_IN_ROWS = 960


def _rows(ref, start, size):
    return ref.at[pl.ds(pl.multiple_of(start, 16), size)]


def _allgather_arrays(shards):
    n = len(shards)
    _, shapes, n_sems, start, finish, middle = _ag_comm(shards)

    def body(*refs):
        for phase in (start, middle, finish):
            phase(refs[:n], refs[n:2 * n], refs[2 * n], refs[2 * n + 1])

    return pl.pallas_call(
        body, name="allgather_weights", out_shape=shapes, in_specs=[HBM] * n, out_specs=[HBM] * n,
        scratch_shapes=[pltpu.SemaphoreType.DMA((n_sems,)), pltpu.SemaphoreType.DMA((n_sems,))],
    )(*shards)


def _ag_copies(srcs, outs, send_sems, recv_sems):
    x, y, c = _coords()
    sib = (x, y, 1 - c)
    xn, yn, dg = (1 - x, y), (x, 1 - y), (1 - x, 1 - y)
    plan = []
    for a, (src, out) in enumerate(zip(srcs, outs)):
        h = src.shape[0] // 2
        q = h // 2
        cp = lambda s, d, k, dev: _remote(s, d, send_sems, recv_sems, 8 * a + k, dev)
        slot = lambda chip: out.at[2 * chip[0] + chip[1]]
        mine, dst = _rows(src, c * h, h), _rows(slot((x, y)), c * h, h)
        piece = lambda chip, start, size, k, dev: cp(_rows(slot(chip), start, size), _rows(slot(chip), start, size), k, dev)
        plan.append(dict(
            own=cp(src, slot((x, y)), 6, sib),
            to_x=cp(mine, dst, 0, (*xn, c)), to_y=cp(mine, dst, 1, (*yn, c)),
            from_x=piece(xn, c * h, h, 0, sib), from_y=piece(yn, c * h, h, 1, sib),
            relay_y=piece(xn, c * h, q, 2, (*yn, c)), relay_x=piece(yn, c * h + q, q, 7, (*xn, c)),
            pass_x=piece(xn, c * h, h, 3, sib), pass_y=piece(yn, c * h, h, 4, sib), pass_d=piece(dg, c * h, h, 5, sib),
            diag_1=piece(dg, c * h, q, 2, sib), diag_2=piece(dg, c * h + q, q, 7, sib),
            got=[piece(chip, (1 - c) * h, h, k, sib) for k, chip in ((3, xn), (4, yn), (5, dg))]))
    return plan


def _ag_start(*refs):
    for p in _ag_copies(*refs):
        for k in ("own", "to_x", "to_y"):
            p[k].start()


def _ag_middle(*refs):
    for p in _ag_copies(*refs):
        p["from_x"].wait_recv()
        p["relay_y"].start()
        p["pass_x"].start()
        p["from_y"].wait_recv()
        p["relay_x"].start()
        p["pass_y"].start()


def _ag_finish(*refs):
    plan = _ag_copies(*refs)
    for p in plan:
        p["diag_1"].wait_recv()
        p["diag_2"].wait_recv()
        p["pass_d"].start()
    for p in plan:
        for cp in p["got"]:
            cp.wait_recv()
        p["own"].wait_recv()
        for k in ("own", "to_x", "to_y", "relay_y", "relay_x", "pass_x", "pass_y", "pass_d"):
            p[k].wait_send()


def _ag_comm(shards):
    shapes = [jax.ShapeDtypeStruct((N_CHIPS,) + s.shape, s.dtype) for s in shards]
    return (list(shards), shapes, 8 * len(shards), _ag_start, _ag_finish, _ag_middle)


def _swap_sibling(arrs, pick_other_half, name):
    n = len(arrs)
    _, outs, n_sems, start, finish = _swap_comm(arrs, pick_other_half)

    def body(*refs):
        start(refs[:n], refs[n:2 * n], refs[2 * n], refs[2 * n + 1])
        finish(refs[:n], refs[n:2 * n], refs[2 * n], refs[2 * n + 1])

    return pl.pallas_call(
        body, name=name, out_shape=outs, in_specs=[HBM] * n, out_specs=[HBM] * n,
        scratch_shapes=[pltpu.SemaphoreType.DMA((n_sems,)), pltpu.SemaphoreType.DMA((n_sems,))],
    )(*arrs)


def _swap_comm(arrs, pick_other_half):
    def copies(srcs, dsts, send_sems, recv_sems):
        x, y, c = _coords()
        cps = []
        for a, (src, dst) in enumerate(zip(srcs, dsts)):
            if pick_other_half:
                h = src.shape[1] // 2
                src = src.at[:, pl.ds(pl.multiple_of((1 - c) * h, 16), h)]
            cps.append(_remote(src, dst, send_sems, recv_sems, a, (x, y, 1 - c)))
        return cps

    def start(*refs):
        for cp in copies(*refs):
            cp.start()

    def finish(*refs):
        for cp in copies(*refs):
            cp.wait()

    shapes = [jax.ShapeDtypeStruct((a.shape[0], a.shape[1] // 2) + a.shape[2:] if pick_other_half else a.shape, a.dtype)
              for a in arrs]
    return (list(arrs), shapes, len(arrs), start, finish)


def _rs_add_pairs(gs, others, c, name):
    n = len(gs)
    blocks = [(g.shape[1] // 4, g.shape[2]) for g in gs]

    def body(c_ref, *refs):
        for a in range(n):
            refs[2 * n + a][...] = (refs[a][...] + refs[n + a][...]).astype(BF16)

    mine = lambda b: pl.BlockSpec((None,) + b, lambda j, s, c_ref: (j, c_ref[0] * 2 + s, 0))
    flat = lambda b: pl.BlockSpec((None,) + b, lambda j, s, c_ref: (j, s, 0))
    return pl.pallas_call(
        body, name=name,
        grid_spec=pltpu.PrefetchScalarGridSpec(
            num_scalar_prefetch=1, grid=(N_CHIPS, 2),
            in_specs=[mine(b) for b in blocks] + [flat(b) for b in blocks],
            out_specs=[flat(b) for b in blocks]),
        out_shape=[jax.ShapeDtypeStruct(o.shape, BF16) for o in others],
        compiler_params=_params(2),
    )(c, *gs, *others)


def _rs_exchange_arrays(parts):
    n = len(parts)

    def body(*refs):
        _rsx_start(refs[:n], refs[n:2 * n], refs[2 * n], refs[2 * n + 1])
        _rsx_finish(refs[:n], refs[n:2 * n], refs[2 * n], refs[2 * n + 1])

    _, shapes, n_sems, _, _ = _rsx_comm(parts)
    return pl.pallas_call(
        body, name="rs_exchange_chips", out_shape=shapes, in_specs=[HBM] * n, out_specs=[HBM] * n,
        scratch_shapes=[pltpu.SemaphoreType.DMA((n_sems,)), pltpu.SemaphoreType.DMA((n_sems,))],
    )(*parts)


def _rsx_copies(srcs, dsts, send_sems, recv_sems):
    x, y, c = _coords()
    others = [(1 - x, y), (x, 1 - y), (1 - x, 1 - y)]
    return [_remote(src.at[2 * ox + oy], dst.at[k], send_sems, recv_sems, 3 * a + k, (ox, oy, c))
            for a, (src, dst) in enumerate(zip(srcs, dsts)) for k, (ox, oy) in enumerate(others)]


def _rsx_start(srcs, dsts, send_sems, recv_sems):
    for cp in _rsx_copies(srcs, dsts, send_sems, recv_sems):
        cp.start()


def _rsx_finish(srcs, dsts, send_sems, recv_sems):
    for cp in _rsx_copies(srcs, dsts, send_sems, recv_sems):
        cp.wait()


def _rsx_comm(parts):
    shapes = [jax.ShapeDtypeStruct((3,) + p.shape[1:], p.dtype) for p in parts]
    return (list(parts), shapes, 3 * len(parts), _rsx_start, _rsx_finish)


def _rs_add_totals(parts, recvs, chip):
    n = len(parts)
    blocks = [(p.shape[1] // 2, p.shape[2]) for p in parts]

    def body(chip_ref, *refs):
        f = lambda r: r[...].astype(F32)
        for a in range(n):
            p, r0, r1, r2 = refs[a], refs[n + 3 * a], refs[n + 3 * a + 1], refs[n + 3 * a + 2]
            refs[4 * n + a][...] = (f(p) + f(r0)) + (f(r1) + f(r2))

    own = lambda b: pl.BlockSpec((None,) + b, lambda s, chip_ref: (chip_ref[0], s, 0))
    slot = lambda b, k: pl.BlockSpec((None,) + b, lambda s, chip_ref, k=k: (k, s, 0))
    recv_specs = [slot(b, k) for b in blocks for k in range(3)]
    recv_args = [r for r in recvs for _ in range(3)]
    return pl.pallas_call(
        body, name="rs_add_totals",
        grid_spec=pltpu.PrefetchScalarGridSpec(
            num_scalar_prefetch=1, grid=(2,),
            in_specs=[own(b) for b in blocks] + recv_specs,
            out_specs=[pl.BlockSpec(b, lambda s, chip_ref: (s, 0)) for b in blocks]),
        out_shape=[jax.ShapeDtypeStruct(p.shape[1:], F32) for p in parts],
        compiler_params=_params(1),
    )(chip, *parts, *recv_args)


def _permute_w_in(wt):
    return jnp.concatenate([wt[:QKV_COLS], wt[QKV_COLS + LOGIT_COLS:IN_COLS], wt[QKV_COLS:QKV_COLS + LOGIT_COLS],
                            jnp.zeros((IN_COLS_PADDED - IN_COLS, wt.shape[1]), wt.dtype)], axis=0)


def _pad_row(v):
    v = v.reshape(1, -1)
    return jnp.pad(v, ((0, 0), (0, D_MODEL - v.shape[1])))


def kernel(x, norm_ffn1, ffn1_gate, ffn1_up, ffn1_down, norm_mix, w_in, conv_w, a_log, dt_bias, dn_norm, w_out, norm_ffn2, ffn2_gate, ffn2_up, ffn2_down, norm_final, loss_target, m_norm_ffn1, m_ffn1_gate, m_ffn1_up, m_ffn1_down, m_norm_mix, m_w_in, m_conv_w, m_a_log, m_dt_bias, m_dn_norm, m_w_out, m_norm_ffn2, m_ffn2_gate, m_ffn2_up, m_ffn2_down, m_norm_final, v_norm_ffn1, v_ffn1_gate, v_ffn1_up, v_ffn1_down, v_norm_mix, v_w_in, v_conv_w, v_a_log, v_dt_bias, v_dn_norm, v_w_out, v_norm_ffn2, v_ffn2_gate, v_ffn2_up, v_ffn2_down, v_norm_final):
    cx, cy, cc = _coords()
    chip = 2 * cx + cy
    stored = lambda t, n: t[0] if n in ROW_SHARDED else t[0].T
    big_w = {n: stored(t, n) for n, t in dict(
        ffn1_gate=ffn1_gate, ffn1_up=ffn1_up, ffn1_down=ffn1_down, w_in=w_in, w_out=w_out,
        ffn2_gate=ffn2_gate, ffn2_up=ffn2_up, ffn2_down=ffn2_down).items()}
    big_m = {n: stored(t, n) for n, t in dict(
        ffn1_gate=m_ffn1_gate, ffn1_up=m_ffn1_up, ffn1_down=m_ffn1_down, w_in=m_w_in, w_out=m_w_out,
        ffn2_gate=m_ffn2_gate, ffn2_up=m_ffn2_up, ffn2_down=m_ffn2_down).items()}
    big_v = {n: stored(t, n) for n, t in dict(
        ffn1_gate=v_ffn1_gate, ffn1_up=v_ffn1_up, ffn1_down=v_ffn1_down, w_in=v_w_in, w_out=v_w_out,
        ffn2_gate=v_ffn2_gate, ffn2_up=v_ffn2_up, ffn2_down=v_ffn2_down).items()}

    cols = IN_COLS // N_CHIPS
    send = {n: big_w[n].astype(BF16) for n in BIG}
    send["w_in"] = jnp.pad(send["w_in"], ((0, W_IN_ROWS - cols), (0, 0)))
    early = tuple(n for n in BIG if n not in LATE_WEIGHTS)
    wts = dict(zip(early, _allgather_arrays([send[n] for n in early])))
    dist =dict(late=[send[n] for n in LATE_WEIGHTS], c=cc.reshape(1).astype(jnp.int32),
                chip=chip.reshape(1).astype(jnp.int32))

    conv_shard = conv_w[0]
    emb = jnp.concatenate([jnp.where((chip == j) & (cc == 0), conv_shard, 0.0) for j in range(N_CHIPS)], axis=1)
    emb = jnp.pad(emb.reshape(6, D_MODEL), ((0, 2), (0, 0)))
    conv_full = _allreduce_small(emb, "allgather_conv_w")[:6].reshape(CONV_WIDTH, 3 * DN_WIDTH)

    zvec = jnp.zeros((1, 128), F32)
    small = dict(norm_ffn1=norm_ffn1, norm_mix=norm_mix, norm_ffn2=norm_ffn2, norm_final=norm_final[None],
                 conv_w=conv_full, avec=zvec.at[0, DN_HEADS:2 * DN_HEADS].set(a_log[0]),
                 dvec=zvec.at[0, DN_HEADS:2 * DN_HEADS].set(dt_bias[0]), dn_norm=dn_norm)

    loss, grad_x, reduced, sg = _local_step(x[0], loss_target[0], wts, small, dist)

    rows = [sg["norm_ffn1"], sg["norm_mix"], sg["norm_ffn2"], sg["norm_final"], _pad_row(sg["a_log"]), _pad_row(sg["dt_bias"]),
            _pad_row(sg["dn_norm"]), _pad_row(loss[0:1]), sg["conv_w"].reshape(6, D_MODEL), jnp.zeros((2, D_MODEL), F32)]
    red = _allreduce_small(jnp.concatenate(rows, axis=0), "allreduce_small")
    loss_out = red[7, 0]
    g_conv_full = red[8:14].reshape(CONV_WIDTH, 3 * DN_WIDTH)
    g_conv = lax.dynamic_slice_in_dim(g_conv_full, chip * (3 * DN_WIDTH // N_CHIPS), 3 * DN_WIDTH // N_CHIPS, axis=1)
    g_small = dict(norm_ffn1=red[0:1], norm_mix=red[1:2], norm_ffn2=red[2:3], norm_final=red[3],
                   a_log=red[4:5, DN_HEADS:2 * DN_HEADS], dt_bias=red[5:6, DN_HEADS:2 * DN_HEADS], dn_norm=red[6:7, :DN_HEAD_DIM])

    out_g, out_d, out_m, out_v = {}, {}, {}, {}
    for n in BIG:
        mine, other = reduced[n]
        g = jnp.where(cc == 0, jnp.concatenate([mine, other], axis=0), jnp.concatenate([other, mine], axis=0))
        if n == "w_in":
            to3 = lambda t: jnp.transpose(t, (2, 0, 1))
            g = g[:cols].reshape(cols, 1, D_MODEL)
            results = (g,) + tuple(_adamw(to3(w_in), g, to3(m_w_in), to3(v_w_in), "adamw_w_in"))
            out_g[n], out_d[n], out_m[n], out_v[n] = (jnp.transpose(t, (1, 2, 0)) for t in results)
            continue
        results = (g,) + tuple(_adamw(big_w[n], g, big_m[n], big_v[n], "adamw_" + n))
        out_g[n], out_d[n], out_m[n], out_v[n] = ((t if n in ROW_SHARDED else t.T)[None] for t in results)
    d, nm, nv = _adamw(conv_w[0], g_conv, m_conv_w[0], v_conv_w[0], "adamw_conv_w")
    out_g["conv_w"], out_d["conv_w"], out_m["conv_w"], out_v["conv_w"] = g_conv[None], d[None], nm[None], nv[None]

    small_names = ("norm_ffn1", "norm_mix", "norm_ffn2", "norm_final", "a_log", "dt_bias", "dn_norm")
    small_w = dict(norm_ffn1=norm_ffn1, norm_mix=norm_mix, norm_ffn2=norm_ffn2, norm_final=norm_final, a_log=a_log,
                   dt_bias=dt_bias, dn_norm=dn_norm)
    small_m = dict(norm_ffn1=m_norm_ffn1, norm_mix=m_norm_mix, norm_ffn2=m_norm_ffn2, norm_final=m_norm_final, a_log=m_a_log,
                   dt_bias=m_dt_bias, dn_norm=m_dn_norm)
    small_v = dict(norm_ffn1=v_norm_ffn1, norm_mix=v_norm_mix, norm_ffn2=v_norm_ffn2, norm_final=v_norm_final, a_log=v_a_log,
                   dt_bias=v_dt_bias, dn_norm=v_dn_norm)
    stack = lambda dct: jnp.concatenate([_pad_row(dct[n]) for n in small_names] + [jnp.zeros((1, D_MODEL), F32)], axis=0)
    d, nm, nv = _adamw(stack(small_w), stack(g_small), stack(small_m), stack(small_v), "adamw_small")
    for k, n in enumerate(small_names):
        shape = small_w[n].shape
        size = math.prod(shape)
        out_g[n] = g_small[n].reshape(shape)
        out_d[n], out_m[n], out_v[n] = (t[k, :size].reshape(shape) for t in (d, nm, nv))

    order = ("norm_ffn1", "ffn1_gate", "ffn1_up", "ffn1_down", "norm_mix", "w_in", "conv_w", "a_log", "dt_bias", "dn_norm",
             "w_out", "norm_ffn2", "ffn2_gate", "ffn2_up", "ffn2_down", "norm_final")
    return (loss_out, grad_x[None], *[out_g[n] for n in order], *[out_d[n] for n in order],
            *[out_m[n] for n in order], *[out_v[n] for n in order])
```

```python
import functools
import math

import jax
import jax.numpy as jnp
from jax import lax
from jax.experimental import pallas as pl
from jax.experimental.pallas import tpu as pltpu

F32 = jnp.float32
BF16 = jnp.bfloat16
HI = lax.Precision.HIGH

D_MODEL = 1024
ATTN_HEADS = 8
ATTN_WIDTH = 512
ATTN_BLOCK = 128
ATTN_SCALE = (ATTN_WIDTH // ATTN_HEADS) ** -0.5
DILATIONS = (1, 4, 16)
DN_HEADS = 4
DN_HEAD_DIM = 128
DN_WIDTH = 512
DN_CHUNK = 64
CONV_WIDTH = 4
NORM_EPS = 1e-6
L2_EPS = 1e-6
QKV_COLS = 3 * ATTN_WIDTH + 3 * DN_WIDTH
LOGIT_COLS = 2 * DN_HEADS
IN_COLS = QKV_COLS + LOGIT_COLS + DN_WIDTH
IN_COLS_PADDED = 3712
N_CHIPS = 4

ADAM_LR = 0.001
ADAM_B1 = 0.9
ADAM_B2 = 0.999
ADAM_EPS = 1e-08
ADAM_WD = 0.01
ADAM_STEP = 10

VMEM_LIMIT = 56 * 1024 * 1024
NEG_BIG = -1e30
MESH = pl.DeviceIdType.MESH


def _params(n_grid, vmem=VMEM_LIMIT):
    return pltpu.CompilerParams(dimension_semantics=("arbitrary",) * n_grid, vmem_limit_bytes=vmem)


def _call(body, args, *, name, grid, in_specs, out_specs, out_shape, scratch_shapes=(), comm=None):
    n_in, n_out, n_scr = len(in_specs), len(out_specs), len(scratch_shapes)
    hbm = pl.BlockSpec(memory_space=pl.ANY)
    srcs, dst_shapes, n_sems, start, finish = comm[:5] if comm is not None else ((), (), 0, None, None)
    middle = comm[5] if comm is not None and len(comm) > 5 else None
    ns, nd = len(srcs), len(dst_shapes)

    def full(*refs):
        ins, c_src = refs[:n_in], refs[n_in:n_in + ns]
        at = n_in + ns
        outs, c_dst = refs[at:at + n_out], refs[at + n_out:at + n_out + nd]
        scr = refs[at + n_out + nd:at + n_out + nd + n_scr]
        if comm is not None:
            ids = [pl.program_id(a) for a in range(len(grid))]
            first = functools.reduce(jnp.logical_and, [i == 0 for i in ids])
            last = functools.reduce(jnp.logical_and, [i == g - 1 for i, g in zip(ids, grid)])

            @pl.when(first)
            def _():
                start(c_src, c_dst, refs[-2], refs[-1])

            if middle is not None:
                relay_step = functools.reduce(jnp.logical_and, [ids[0] == (5 * grid[0]) // 8] + [i == 0 for i in ids[1:]])

                @pl.when(relay_step)
                def _():
                    middle(c_src, c_dst, refs[-2], refs[-1])

        body(*ins, *outs, *scr)
        if comm is not None:
            @pl.when(last)
            def _():
                finish(c_src, c_dst, refs[-2], refs[-1])

    sems = [pltpu.SemaphoreType.DMA((n_sems,)), pltpu.SemaphoreType.DMA((n_sems,))] if comm is not None else []
    res = pl.pallas_call(
        full, name=name, grid=grid, in_specs=list(in_specs) + [hbm] * ns, out_specs=list(out_specs) + [hbm] * nd,
        out_shape=list(out_shape) + list(dst_shapes), scratch_shapes=list(scratch_shapes) + sems,
        compiler_params=_params(len(grid)),
    )(*args, *srcs)
    return res[:n_out], res[n_out:]


def _nt(a, b, precision=None):
    return lax.dot_general(a, b, (((1,), (1,)), ((), ())), preferred_element_type=F32, precision=precision)


def _tn(a, b, precision=None):
    return lax.dot_general(a, b, (((0,), (0,)), ((), ())), preferred_element_type=F32, precision=precision)


def _nn(a, b, precision=None):
    return jnp.dot(a, b, preferred_element_type=F32, precision=precision)


def _sigmoid(x):
    return 1.0 / (1.0 + jnp.exp(-x))


def _loss_head(xf, gain, target):
    r = lax.rsqrt(jnp.mean(xf * xf, axis=-1, keepdims=True) + NORM_EPS)
    xhat = xf * r
    err = xhat * gain - target
    part = 0.5 * jnp.sum(jnp.mean(err * err, axis=-1, keepdims=True), axis=0, keepdims=True)
    dy = err * (1.0 / xf.shape[-1])
    dgain = jnp.sum(dy * xhat, axis=0, keepdims=True)
    dxh = dy * gain
    return part, r * (dxh - xhat * jnp.mean(dxh * xhat, axis=-1, keepdims=True)), dgain


def _ffn_fwd(x, gain, wg, wu, wd, name, comm=None, head=None):
    S, D = x.shape
    nf, tf, _ = wg.shape
    tm = 512
    n_in = 5 if head is None else 7

    def body(*refs):
        x_ref, gain_ref, wg_ref, wu_ref, wd_ref = refs[:5]
        xo_ref, h_ref, g_ref, u_ref = refs[n_in:n_in + 4]
        acc_ref, hs_ref = refs[-2:]
        i = pl.program_id(0)
        j = pl.program_id(1)

        @pl.when(j == 0)
        def _():
            xf = x_ref[...]
            r = lax.rsqrt(jnp.mean(xf * xf, axis=-1, keepdims=True) + NORM_EPS)
            h = (xf * r * gain_ref[...]).astype(BF16)
            hs_ref[...] = h
            h_ref[...] = h
            acc_ref[...] = jnp.zeros_like(acc_ref)

        h = hs_ref[...]
        g = _nt(h, wg_ref[...])
        u = _nt(h, wu_ref[...])
        g_ref[...] = g.astype(BF16)
        u_ref[...] = u.astype(BF16)
        act = g * _sigmoid(g) * u
        acc_ref[...] += _nn(act.astype(BF16), wd_ref[...])

        if head is not None:
            hgain_ref, t_ref = refs[5:7]
            loss_ref, dgain_ref = refs[n_in + 4:n_in + 6]

            @pl.when((i == 0) & (j == 0))
            def _():
                loss_ref[...] = jnp.zeros_like(loss_ref)
                dgain_ref[...] = jnp.zeros_like(dgain_ref)

        @pl.when(j == nf - 1)
        def _():
            xo = x_ref[...] + 0.5 * acc_ref[...]
            if head is None:
                xo_ref[...] = xo
            else:
                part, dxo, dgain = _loss_head(xo, hgain_ref[...], t_ref[...])
                first = ((lax.broadcasted_iota(jnp.int32, (8, 128), 0) == 0)
                         & (lax.broadcasted_iota(jnp.int32, (8, 128), 1) == 0))
                loss_ref[...] += jnp.where(first, part, 0.0)
                dgain_ref[...] += dgain
                xo_ref[...] = dxo

    tok = pl.BlockSpec((tm, D), lambda i, j: (i, 0))
    row = pl.BlockSpec((1, D), lambda i, j: (0, 0))
    chunk = pl.BlockSpec((None, tf, D), lambda i, j: (j, 0, 0))
    act = pl.BlockSpec((None, tm, tf), lambda i, j: (j, i, 0))
    extra_in = [] if head is None else [row, tok]
    extra_out = [] if head is None else [pl.BlockSpec((8, 128), lambda i, j: (0, 0)), row]
    extra_shape = [] if head is None else [jax.ShapeDtypeStruct((8, 128), F32), jax.ShapeDtypeStruct((1, D), F32)]
    return _call(
        body, (x, gain, wg, wu, wd) + (() if head is None else tuple(head)), name=name, grid=(S // tm, nf), comm=comm,
        in_specs=[tok, row, chunk, chunk, chunk] + extra_in,
        out_specs=[tok, tok, act, act] + extra_out,
        out_shape=[jax.ShapeDtypeStruct((S, D), F32), jax.ShapeDtypeStruct((S, D), BF16),
                   jax.ShapeDtypeStruct((nf, S, tf), BF16), jax.ShapeDtypeStruct((nf, S, tf), BF16)] + extra_shape,
        scratch_shapes=[pltpu.VMEM((tm, D), F32), pltpu.VMEM((tm, D), BF16)])


def _rmsnorm_bwd(dh, xf, gain):
    r = lax.rsqrt(jnp.mean(xf * xf, axis=-1, keepdims=True) + NORM_EPS)
    xhat = xf * r
    dgain = jnp.sum(dh * xhat, axis=0, keepdims=True)
    dxh = dh * gain
    dx = r * (dxh - xhat * jnp.mean(dxh * xhat, axis=-1, keepdims=True))
    return dx, dgain


def _ffn_bwd(dxo, x, gain, g, u, wd, wg, wu, name, comm=None):
    S, D = x.shape
    nf, _, tf = g.shape
    tm = 512

    def body(dxo_ref, x_ref, gain_ref, g_ref, u_ref, wd_ref, wg_ref, wu_ref,
             dx_ref, dgain_ref, dg_ref, du_ref, act_ref, dout_ref, acc_ref, ds_ref):
        i = pl.program_id(0)
        j = pl.program_id(1)

        @pl.when(j == 0)
        def _():
            d = (0.5 * dxo_ref[...]).astype(BF16)
            ds_ref[...] = d
            dout_ref[...] = d
            acc_ref[...] = jnp.zeros_like(acc_ref)

        @pl.when((i == 0) & (j == 0))
        def _():
            dgain_ref[...] = jnp.zeros_like(dgain_ref)

        for half in range(2):
            rows = slice(half * (tm // 2), (half + 1) * (tm // 2))
            dact = _nt(ds_ref[rows, :], wd_ref[...])
            gv = g_ref[rows, :].astype(F32)
            uv = u_ref[rows, :].astype(F32)
            sg = _sigmoid(gv)
            silu = gv * sg
            act_ref[rows, :] = (silu * uv).astype(BF16)
            dgv = (dact * uv * (sg * (1.0 + gv * (1.0 - sg)))).astype(BF16)
            duv = (dact * silu).astype(BF16)
            dg_ref[rows, :] = dgv
            du_ref[rows, :] = duv
            acc_ref[rows, :] += _nn(dgv, wg_ref[...]) + _nn(duv, wu_ref[...])

        @pl.when(j == nf - 1)
        def _():
            dx, dgain = _rmsnorm_bwd(acc_ref[...], x_ref[...], gain_ref[...])
            dx_ref[...] = dxo_ref[...] + dx
            dgain_ref[...] += dgain

    return _call(
        body, (dxo, x, gain, g, u, wd, wg, wu), name=name, grid=(S // tm, nf), comm=comm,
        in_specs=[pl.BlockSpec((tm, D), lambda i, j: (i, 0)),
                  pl.BlockSpec((tm, D), lambda i, j: (i, 0)),
                  pl.BlockSpec((1, D), lambda i, j: (0, 0)),
                  pl.BlockSpec((None, tm, tf), lambda i, j: (j, i, 0)),
                  pl.BlockSpec((None, tm, tf), lambda i, j: (j, i, 0)),
                  pl.BlockSpec((None, tf, D), lambda i, j: (j, 0, 0)),
                  pl.BlockSpec((None, tf, D), lambda i, j: (j, 0, 0)),
                  pl.BlockSpec((None, tf, D), lambda i, j: (j, 0, 0))],
        out_specs=[pl.BlockSpec((tm, D), lambda i, j: (i, 0)),
                   pl.BlockSpec((1, D), lambda i, j: (0, 0)),
                   pl.BlockSpec((None, tm, tf), lambda i, j: (j, i, 0)),
                   pl.BlockSpec((None, tm, tf), lambda i, j: (j, i, 0)),
                   pl.BlockSpec((None, tm, tf), lambda i, j: (j, i, 0)),
                   pl.BlockSpec((tm, D), lambda i, j: (i, 0))],
        out_shape=[jax.ShapeDtypeStruct((S, D), F32), jax.ShapeDtypeStruct((1, D), F32),
                   jax.ShapeDtypeStruct((nf, S, tf), BF16), jax.ShapeDtypeStruct((nf, S, tf), BF16),
                   jax.ShapeDtypeStruct((nf, S, tf), BF16), jax.ShapeDtypeStruct((S, D), BF16)],
        scratch_shapes=[pltpu.VMEM((tm, D), F32), pltpu.VMEM((tm, D), BF16)])


def _matmul_tn(a, b, tm, tk, name):
    K, M = a.shape
    N = b.shape[1]

    def body(a_ref, b_ref, o_ref):
        @pl.when(pl.program_id(1) == 0)
        def _():
            o_ref[...] = jnp.zeros_like(o_ref)

        o_ref[...] += _tn(a_ref[...], b_ref[...])

    return pl.pallas_call(
        body, name=name, grid=(M // tm, K // tk),
        in_specs=[pl.BlockSpec((tk, tm), lambda i, k: (k, i)),
                  pl.BlockSpec((tk, N), lambda i, k: (k, 0))],
        out_specs=pl.BlockSpec((tm, N), lambda i, k: (i, 0)),
        out_shape=jax.ShapeDtypeStruct((M, N), F32),
        compiler_params=_params(2),
    )(a, b)


def _dw_chunks(a, b, tk, name, comm=None):
    nf, S, tf = a.shape
    N = b.shape[1]

    def body(a_ref, b_ref, o_ref):
        @pl.when(pl.program_id(1) == 0)
        def _():
            o_ref[...] = jnp.zeros_like(o_ref)

        o_ref[...] += _tn(a_ref[...], b_ref[...])

    (out,), landed = _call(
        body, (a, b), name=name, grid=(nf, S // tk), comm=comm,
        in_specs=[pl.BlockSpec((None, tk, tf), lambda j, k: (j, k, 0)),
                  pl.BlockSpec((tk, N), lambda j, k: (k, 0))],
        out_specs=[pl.BlockSpec((None, tf, N), lambda j, k: (j, 0, 0))],
        out_shape=[jax.ShapeDtypeStruct((nf, tf, N), F32)])
    return out, landed


VIEW_TILE = 512


def _view_spec(d, tile=VIEW_TILE):
    return pl.BlockSpec((tile // d, d * ATTN_WIDTH), lambda i: (i, 0))


def _view_shape(S, d, dtype):
    return jax.ShapeDtypeStruct((S // d, d * ATTN_WIDTH), dtype)


def _tile_to_views(val, planes, out_refs):
    for g in range(4):
        planes[g] = val[:, g * 128:(g + 1) * 128]
    for d, ref in zip(DILATIONS, out_refs):
        if d == 1:
            ref[...] = val.astype(ref.dtype)
            continue
        for r in range(d):
            for g in range(4):
                ref[:, r * ATTN_WIDTH + g * 128:r * ATTN_WIDTH + (g + 1) * 128] = (
                    planes[g, pl.ds(r, planes.shape[1] // d, stride=d), :].astype(ref.dtype))


def _view_to_tile(ref, d, planes):
    if d == 1:
        return ref[...].astype(F32)
    for r in range(d):
        for g in range(4):
            planes[g, pl.ds(r, planes.shape[1] // d, stride=d), :] = (
                ref[:, r * ATTN_WIDTH + g * 128:r * ATTN_WIDTH + (g + 1) * 128].astype(F32))
    return jnp.concatenate([planes[g] for g in range(4)], axis=1)


def _inproj_fwd(x, gain, w_in_p):
    S, D = x.shape
    tm = VIEW_TILE
    W = ATTN_WIDTH

    def body(x_ref, gain_ref, w_ref, h_ref, q1, q4, q16, k1, k4, k16, v1, v4, v16, dq_ref, dk_ref, dv_ref, gate_ref, bd_ref,
             planes):
        xf = x_ref[...]
        r = lax.rsqrt(jnp.mean(xf * xf, axis=-1, keepdims=True) + NORM_EPS)
        h = (xf * r * gain_ref[...]).astype(BF16)
        h_ref[...] = h
        _tile_to_views(_nt(h, w_ref[0:W, :]) * ATTN_SCALE, planes, (q1, q4, q16))
        _tile_to_views(_nt(h, w_ref[W:2 * W, :]), planes, (k1, k4, k16))
        _tile_to_views(_nt(h, w_ref[2 * W:3 * W, :]), planes, (v1, v4, v16))
        dq_ref[...] = _nt(h, w_ref[3 * W:4 * W, :])
        dk_ref[...] = _nt(h, w_ref[4 * W:5 * W, :])
        dv_ref[...] = _nt(h, w_ref[5 * W:6 * W, :])
        gate_ref[...] = _nt(h, w_ref[6 * W:7 * W, :])
        bd_ref[...] = _nt(h, w_ref[7 * W:7 * W + 128, :])

    tok = lambda w: pl.BlockSpec((tm, w), lambda i: (i, 0))
    return pl.pallas_call(
        body, name="inproj_fwd", grid=(S // tm,),
        in_specs=[tok(D), pl.BlockSpec((1, D), lambda i: (0, 0)),
                  pl.BlockSpec((IN_COLS_PADDED, D), lambda i: (0, 0))],
        out_specs=[tok(D)] + [_view_spec(d) for d in DILATIONS] * 3 + [tok(W)] * 4 + [tok(128)],
        out_shape=[jax.ShapeDtypeStruct((S, D), BF16)] + [_view_shape(S, d, BF16) for d in DILATIONS] * 3
                  + [jax.ShapeDtypeStruct((S, W), F32)] * 4 + [jax.ShapeDtypeStruct((S, 128), F32)],
        scratch_shapes=[pltpu.VMEM((4, tm, 128), F32)],
        compiler_params=_params(1),
    )(x, gain, w_in_p)


def _inproj_bwd(dxo, x, gain, attn_grads, dsecs, dbd, w_in_p):
    S, D = x.shape
    tm = VIEW_TILE
    W = ATTN_WIDTH

    def body(dxo_ref, x_ref, gain_ref, *rest):
        views, (s3, s4, s5, s6, dbd_ref, w_ref, dx_ref, dgain_ref, dproj_ref, planes) = rest[:9], rest[9:]

        @pl.when(pl.program_id(0) == 0)
        def _():
            dgain_ref[...] = jnp.zeros_like(dgain_ref)

        secs = []
        for k in range(3):
            parts = [_view_to_tile(views[3 * k + p], d, planes) for p, d in enumerate(DILATIONS)]
            secs.append(parts[0] + parts[1] + parts[2])
        secs += [s3[...], s4[...], s5[...], s6[...]]
        dh = jnp.zeros((tm, D), F32)
        for k, s in enumerate(secs):
            d = s.astype(BF16)
            dproj_ref[:, k * W:(k + 1) * W] = d
            dh += _nn(d, w_ref[k * W:(k + 1) * W, :])
        d = dbd_ref[...].astype(BF16)
        dproj_ref[:, 7 * W:7 * W + 128] = d
        dh += _nn(d, w_ref[7 * W:7 * W + 128, :])
        dx, dgain = _rmsnorm_bwd(dh, x_ref[...], gain_ref[...])
        dx_ref[...] = dxo_ref[...] + dx
        dgain_ref[...] += dgain

    tok = lambda w: pl.BlockSpec((tm, w), lambda i: (i, 0))
    return pl.pallas_call(
        body, name="inproj_bwd", grid=(S // tm,),
        in_specs=[tok(D), tok(D), pl.BlockSpec((1, D), lambda i: (0, 0))] + [_view_spec(d, tm) for d in DILATIONS] * 3
                 + [tok(W)] * 4 + [tok(128)] + [pl.BlockSpec((IN_COLS_PADDED, D), lambda i: (0, 0))],
        out_specs=[tok(D), pl.BlockSpec((1, D), lambda i: (0, 0)), tok(IN_COLS_PADDED)],
        out_shape=[jax.ShapeDtypeStruct((S, D), F32), jax.ShapeDtypeStruct((1, D), F32),
                   jax.ShapeDtypeStruct((S, IN_COLS_PADDED), BF16)],
        scratch_shapes=[pltpu.VMEM((4, tm, 128), F32)],
        compiler_params=_params(1),
    )(dxo, x, gain, *[g for grads in attn_grads for g in grads], *dsecs, dbd, w_in_p)


def _slope(h):
    return 2.0 ** (-8.0 * (h + 1) / ATTN_HEADS)


def _head_bias(steps, d, heads=tuple(range(ATTN_HEADS))):
    stepsf = steps.astype(F32)
    return jnp.stack([stepsf * (-_slope(h) * d) for h in heads])


def _hnt(a, b):
    return lax.dot_general(a, b, (((2,), (2,)), ((0,), (0,))), preferred_element_type=F32)


def _hnn(a, b):
    return lax.dot_general(a, b, (((2,), (1,)), ((0,), (0,))), preferred_element_type=F32)


def _blocks_per_step(nb):
    return next(n for n in (8, 4, 2, 1) if nb % n == 0)


def _query_step_specs(qb):
    B = ATTN_BLOCK
    cur = pl.BlockSpec((qb * B, ATTN_WIDTH), lambda r, n: (n, r))
    prev = pl.BlockSpec((B, ATTN_WIDTH), lambda r, n: (jnp.maximum(qb * n - 1, 0), r))
    return cur, prev


def _prev_block(prev_ref, cur_ref, sub, sl):
    B = ATTN_BLOCK
    return prev_ref[:, sl] if sub == 0 else cur_ref[(sub - 1) * B:sub * B, sl]


def _head_cols(tile, lo, big):
    return [_head_col(tile, lo, big), _head_col(tile, jnp.logical_not(lo), big)]


def _attn_fwd(q, k, v, d, name):
    L = q.shape[0]
    nb = L // ATTN_BLOCK
    B = ATTN_BLOCK
    QB = _blocks_per_step(nb)

    def body(q_ref, kp_ref, kc_ref, vp_ref, vc_ref, o_ref, lse_ref):
        n = pl.program_id(1)
        qi = lax.broadcasted_iota(jnp.int32, (B, 2 * B), 0)
        kj = lax.broadcasted_iota(jnp.int32, (B, 2 * B), 1)
        steps = qi + B - kj
        band = (steps >= 0) & (steps <= B)
        lo = lax.broadcasted_iota(jnp.int32, (B, 128), 1) < 64
        bias = _head_bias(steps, d)
        for sub in range(QB):
            rows = slice(sub * B, (sub + 1) * B)
            valid = band & ((kj >= B) | (n > 0)) if sub == 0 else band
            qs, ks, vs = [], [], []
            for G in range(4):
                sl = slice(G * 128, (G + 1) * 128)
                qg = q_ref[rows, sl]
                kg = jnp.concatenate([_prev_block(kp_ref, kc_ref, sub, sl), kc_ref[rows, sl]], axis=0)
                vg = jnp.concatenate([_prev_block(vp_ref, vc_ref, sub, sl), vc_ref[rows, sl]], axis=0)
                qs += [jnp.where(lo, qg, jnp.zeros_like(qg)), jnp.where(lo, jnp.zeros_like(qg), qg)]
                ks += [kg, kg]
                vs += [vg, vg]
            s = jnp.where(valid, _hnt(jnp.stack(qs), jnp.stack(ks)) + bias, NEG_BIG)
            m = jnp.max(s, axis=-1, keepdims=True)
            p = jnp.exp(s - m)
            l = jnp.sum(p, axis=-1, keepdims=True)
            o = _hnn(p.astype(BF16), jnp.stack(vs)) / l
            lse = m + jnp.log(l)
            for G in range(4):
                sl = slice(G * 128, (G + 1) * 128)
                o_ref[rows, sl] = jnp.where(lo, o[2 * G], o[2 * G + 1])
                lse_ref[rows, sl] = jnp.where(lo, lse[2 * G], lse[2 * G + 1])

    cur, prev = _query_step_specs(QB)
    return pl.pallas_call(
        body, name=name, grid=(d, nb // QB),
        in_specs=[cur, prev, cur, prev, cur],
        out_specs=[cur, cur],
        out_shape=[jax.ShapeDtypeStruct((L, d * ATTN_WIDTH), F32)] * 2,
        compiler_params=_params(2),
    )(q, k, k, v, v)


def _attn_merge(parts):
    S = parts[0][0].shape[0]
    tm = VIEW_TILE

    def body(o1, s1, o2, s2, o3, s3, o_ref, lse1, lse4, lse16, planes):
        outs, lses = [], []
        for d, (o, s) in zip(DILATIONS, ((o1, s1), (o2, s2), (o3, s3))):
            outs.append(_view_to_tile(o, d, planes))
            lses.append(_view_to_tile(s, d, planes))
        mx = jnp.maximum(jnp.maximum(lses[0], lses[1]), lses[2])
        es = [jnp.exp(s - mx) for s in lses]
        den = es[0] + es[1] + es[2]
        o_ref[...] = (es[0] * outs[0] + es[1] * outs[1] + es[2] * outs[2]) / den
        _tile_to_views(mx + jnp.log(den), planes, (lse1, lse4, lse16))

    views = [_view_spec(d) for d in DILATIONS]
    flat = [t for p in parts for t in p]
    return pl.pallas_call(
        body, name="attn_merge", grid=(S // tm,),
        in_specs=[views[p] for p in range(3) for _ in range(2)],
        out_specs=[views[0]] + views,
        out_shape=[jax.ShapeDtypeStruct((S, ATTN_WIDTH), F32)] + [_view_shape(S, d, F32) for d in DILATIONS],
        scratch_shapes=[pltpu.VMEM((4, tm, 128), F32)],
        compiler_params=_params(1),
    )(*flat)


def _head_col(t, msk, big):
    if big:
        return jnp.max(jnp.where(msk, t, NEG_BIG), axis=-1, keepdims=True)
    return jnp.sum(jnp.where(msk, t, 0.0), axis=-1, keepdims=True) * (1.0 / 64.0)


def _attn_bwd_q(q, k, v, do, lse, dd, d, name):
    L = q.shape[0]
    nb = L // ATTN_BLOCK
    B = ATTN_BLOCK
    QB = _blocks_per_step(nb)

    def body(q_ref, kp_ref, kc_ref, vp_ref, vc_ref, do_ref, lse_ref, dd_ref, dq_ref):
        n = pl.program_id(1)
        qi = lax.broadcasted_iota(jnp.int32, (B, 2 * B), 0)
        kj = lax.broadcasted_iota(jnp.int32, (B, 2 * B), 1)
        steps = qi + B - kj
        band = (steps >= 0) & (steps <= B)
        lo = lax.broadcasted_iota(jnp.int32, (B, 128), 1) < 64
        bias = _head_bias(steps, d)
        for sub in range(QB):
            rows = slice(sub * B, (sub + 1) * B)
            valid = band & ((kj >= B) | (n > 0)) if sub == 0 else band
            qs, ks, vs, dos, lses, dcols = [], [], [], [], [], []
            for G in range(4):
                sl = slice(G * 128, (G + 1) * 128)
                qg = q_ref[rows, sl]
                kg = jnp.concatenate([_prev_block(kp_ref, kc_ref, sub, sl), kc_ref[rows, sl]], axis=0)
                vg = jnp.concatenate([_prev_block(vp_ref, vc_ref, sub, sl), vc_ref[rows, sl]], axis=0)
                dog = do_ref[rows, sl]
                qs += [jnp.where(lo, qg, jnp.zeros_like(qg)), jnp.where(lo, jnp.zeros_like(qg), qg)]
                dos += [jnp.where(lo, dog, 0.0).astype(BF16), jnp.where(lo, 0.0, dog).astype(BF16)]
                ks += [kg, kg]
                vs += [vg, vg]
                lses += _head_cols(lse_ref[rows, sl], lo, True)
                dcols += _head_cols(dd_ref[rows, sl], lo, False)
            kb = jnp.stack(ks)
            s = _hnt(jnp.stack(qs), kb) + bias
            p = jnp.where(valid, jnp.exp(jnp.where(valid, s, NEG_BIG) - jnp.stack(lses)), 0.0)
            dp = _hnt(jnp.stack(dos), jnp.stack(vs))
            ds = p * (dp - jnp.stack(dcols))
            dq = _hnn(ds.astype(BF16), kb) * ATTN_SCALE
            for G in range(4):
                dq_ref[rows, G * 128:(G + 1) * 128] = jnp.where(lo, dq[2 * G], dq[2 * G + 1]).astype(BF16)

    cur, prev = _query_step_specs(QB)
    return pl.pallas_call(
        body, name=name, grid=(d, nb // QB), in_specs=[cur, prev, cur, prev, cur, cur, cur, cur], out_specs=cur,
        out_shape=jax.ShapeDtypeStruct((L, d * ATTN_WIDTH), BF16), compiler_params=_params(2),
    )(q, k, k, v, v, do, lse, dd)


def _attn_bwd_kv(q, k, v, do, lse, dd, d, name):
    L = q.shape[0]
    nb = L // ATTN_BLOCK
    B = ATTN_BLOCK
    KB = _blocks_per_step(nb)
    n_steps = nb // KB

    def body(k_ref, v_ref, qc_ref, qn_ref, doc_ref, don_ref, lsec_ref, lsen_ref, ddc_ref, ddn_ref, dk_ref, dv_ref):
        j = pl.program_id(1)
        qrow = lax.broadcasted_iota(jnp.int32, (2 * B, B), 0)
        kk = lax.broadcasted_iota(jnp.int32, (2 * B, B), 1)
        steps = qrow - kk
        band = (steps >= 0) & (steps <= B)
        lo2 = lax.broadcasted_iota(jnp.int32, (2 * B, 128), 1) < 64
        lo = lax.broadcasted_iota(jnp.int32, (B, 128), 1) < 64
        stepsf = steps.astype(F32)
        for sub in range(KB):
            rows = slice(sub * B, (sub + 1) * B)
            last = sub == KB - 1
            valid = band & ((qrow < B) | (j < n_steps - 1)) if last else band
            after = lambda cur_ref, nxt_ref, sl: nxt_ref[:, sl] if last else cur_ref[(sub + 1) * B:(sub + 2) * B, sl]
            for G in range(4):
                sl = slice(G * 128, (G + 1) * 128)
                kg = k_ref[rows, sl]
                vg = v_ref[rows, sl]
                qq = jnp.concatenate([qc_ref[rows, sl], after(qc_ref, qn_ref, sl)], axis=0)
                doo = jnp.concatenate([doc_ref[rows, sl], after(doc_ref, don_ref, sl)], axis=0)
                lse2 = jnp.concatenate([lsec_ref[rows, sl], after(lsec_ref, lsen_ref, sl)], axis=0)
                dd2 = jnp.concatenate([ddc_ref[rows, sl], after(ddc_ref, ddn_ref, sl)], axis=0)
                doo_b = doo.astype(BF16)
                dks, dvs = [], []
                for half in (0, 1):
                    msk = lo2 if half == 0 else jnp.logical_not(lo2)
                    qm = jnp.where(msk, qq, jnp.zeros_like(qq))
                    s = _nt(qm, kg) - (_slope(2 * G + half) * d) * stepsf
                    lse_c = _head_col(lse2, msk, True)
                    p = jnp.where(valid, jnp.exp(jnp.where(valid, s, NEG_BIG) - lse_c), 0.0)
                    dvs.append(_tn(p.astype(BF16), doo_b))
                    dom = jnp.where(msk, doo, 0.0).astype(BF16)
                    dp = _nt(dom, vg)
                    dcol = _head_col(dd2, msk, False)
                    ds = p * (dp - dcol)
                    dks.append(_tn(ds.astype(BF16), qq))
                dk_ref[rows, sl] = jnp.where(lo, dks[0], dks[1]).astype(BF16)
                dv_ref[rows, sl] = jnp.where(lo, dvs[0], dvs[1]).astype(BF16)

    cur = pl.BlockSpec((KB * B, ATTN_WIDTH), lambda r, j: (j, r))
    nxt = pl.BlockSpec((B, ATTN_WIDTH), lambda r, j: (jnp.minimum(KB * (j + 1), nb - 1), r))
    return pl.pallas_call(
        body, name=name, grid=(d, n_steps), in_specs=[cur, cur, cur, nxt, cur, nxt, cur, nxt, cur, nxt],
        out_specs=[cur, cur],
        out_shape=[jax.ShapeDtypeStruct((L, d * ATTN_WIDTH), BF16)] * 2, compiler_params=_params(2),
    )(k, v, q, q, do, do, lse, lse, dd, dd)


CONV_T = 512
HALO = 8


def _per_head(head, refs):
    for h in range(DN_HEADS):
        lanes = pl.ds(h * DN_HEAD_DIM, DN_HEAD_DIM)
        head(*[r.at[:, lanes] for r in refs[:-1]], refs[-1])


def _conv_taps(pad_ref, w, T):
    acc = pad_ref[pl.ds(HALO - 3, T), :] * w[0:1, :]
    for j in range(1, CONV_WIDTH):
        acc = acc + pad_ref[pl.ds(HALO - 3 + j, T), :] * w[j:j + 1, :]
    return acc


def _conv_fwd(xq, xk, xv, conv_w):
    S = xq.shape[0]
    T = CONV_T

    def body(*refs):
        _per_head(head, refs)

    def head(xq_ref, xqh_ref, xk_ref, xkh_ref, xv_ref, xvh_ref, wq_ref, wk_ref, wv_ref,
             qn_ref, kn_ref, v_ref, pad_ref):
        i = pl.program_id(0)

        def act(x_ref, xh_ref, w_ref):
            pad_ref[pl.ds(0, HALO), :] = jnp.where(i > 0, xh_ref[...], 0.0)
            pad_ref[pl.ds(HALO, T), :] = x_ref[...]
            c = _conv_taps(pad_ref, w_ref[...], T)
            return c * _sigmoid(c)

        def l2n(t):
            return t * lax.rsqrt(jnp.sum(t * t, axis=-1, keepdims=True) + L2_EPS)

        qn_ref[...] = l2n(act(xq_ref, xqh_ref, wq_ref))
        kn_ref[...] = l2n(act(xk_ref, xkh_ref, wk_ref))
        v_ref[...] = act(xv_ref, xvh_ref, wv_ref)

    tile = pl.BlockSpec((T, DN_WIDTH), lambda i: (i, 0))
    halo = pl.BlockSpec((HALO, DN_WIDTH), lambda i: (jnp.maximum(i * (T // HALO) - 1, 0), 0))
    wspec = lambda sec: pl.BlockSpec((CONV_WIDTH, DN_WIDTH), lambda i, sec=sec: (0, sec))
    return pl.pallas_call(
        body, name="dn_conv_fwd", grid=(S // T,),
        in_specs=[tile, halo, tile, halo, tile, halo, wspec(0), wspec(1), wspec(2)],
        out_specs=[tile, tile, tile],
        out_shape=[jax.ShapeDtypeStruct((S, DN_WIDTH), F32)] * 3,
        scratch_shapes=[pltpu.VMEM((T + HALO, 128), F32)],
        compiler_params=_params(1),
    )(xq, xq, xk, xk, xv, xv, conv_w, conv_w, conv_w)


def _conv_bwd_pre(xq, xk, xv, conv_w, dqn, dkn, dv):
    S = xq.shape[0]
    T = CONV_T

    def body(*refs):
        _per_head(head, refs)

    def head(xq_ref, xqh_ref, xk_ref, xkh_ref, xv_ref, xvh_ref, wq_ref, wk_ref, wv_ref,
             dqn_ref, dkn_ref, dv_ref, dcq_ref, dck_ref, dcv_ref, dwq_ref, dwk_ref, dwv_ref, pad_ref):
        i = pl.program_id(0)

        def one(x_ref, xh_ref, w_ref, dy_ref, dc_ref, dw_ref, normed):
            pad_ref[pl.ds(0, HALO), :] = jnp.where(i > 0, xh_ref[...], 0.0)
            pad_ref[pl.ds(HALO, T), :] = x_ref[...]
            c = _conv_taps(pad_ref, w_ref[...], T)
            sg = _sigmoid(c)
            a = c * sg
            dy = dy_ref[...]
            if normed:
                r = lax.rsqrt(jnp.sum(a * a, axis=-1, keepdims=True) + L2_EPS)
                y = a * r
                da = r * (dy - y * jnp.sum(dy * y, axis=-1, keepdims=True))
            else:
                da = dy
            dc = da * (sg * (1.0 + c * (1.0 - sg)))
            dc_ref[...] = dc

            @pl.when(i == 0)
            def _():
                dw_ref[...] = jnp.zeros_like(dw_ref)

            rows = [jnp.sum(dc * pad_ref[pl.ds(HALO - 3 + j, T), :], axis=0, keepdims=True) for j in range(CONV_WIDTH)]
            dw_ref[...] += jnp.concatenate(rows + [jnp.zeros((8 - CONV_WIDTH, 128), F32)], axis=0)

        one(xq_ref, xqh_ref, wq_ref, dqn_ref, dcq_ref, dwq_ref, True)
        one(xk_ref, xkh_ref, wk_ref, dkn_ref, dck_ref, dwk_ref, True)
        one(xv_ref, xvh_ref, wv_ref, dv_ref, dcv_ref, dwv_ref, False)

    tile = pl.BlockSpec((T, DN_WIDTH), lambda i: (i, 0))
    halo = pl.BlockSpec((HALO, DN_WIDTH), lambda i: (jnp.maximum(i * (T // HALO) - 1, 0), 0))
    wspec = lambda sec: pl.BlockSpec((CONV_WIDTH, DN_WIDTH), lambda i, sec=sec: (0, sec))
    dwspec = pl.BlockSpec((8, DN_WIDTH), lambda i: (0, 0))
    return pl.pallas_call(
        body, name="dn_conv_bwd_pre", grid=(S // T,),
        in_specs=[tile, halo, tile, halo, tile, halo, wspec(0), wspec(1), wspec(2), tile, tile, tile],
        out_specs=[tile, tile, tile, dwspec, dwspec, dwspec],
        out_shape=[jax.ShapeDtypeStruct((S, DN_WIDTH), F32)] * 3 + [jax.ShapeDtypeStruct((8, DN_WIDTH), F32)] * 3,
        scratch_shapes=[pltpu.VMEM((T + HALO, 128), F32)],
        compiler_params=_params(1),
    )(xq, xq, xk, xk, xv, xv, conv_w, conv_w, conv_w, dqn, dkn, dv)


def _conv_bwd_x(dcq, dck, dcv, conv_w):
    S = dcq.shape[0]
    T = CONV_T
    nt = S // T

    def body(*refs):
        _per_head(head, refs)

    def head(dq_ref, dqh_ref, dk_ref, dkh_ref, dv_ref, dvh_ref, wq_ref, wk_ref, wv_ref,
             oq_ref, ok_ref, ov_ref, pad_ref):
        i = pl.program_id(0)

        def one(d_ref, dh_ref, w_ref, o_ref):
            pad_ref[pl.ds(0, T), :] = d_ref[...]
            pad_ref[pl.ds(T, HALO), :] = jnp.where(i < nt - 1, dh_ref[...], 0.0)
            w = w_ref[...]
            acc = pad_ref[pl.ds(3, T), :] * w[0:1, :]
            for j in range(1, CONV_WIDTH):
                acc = acc + pad_ref[pl.ds(3 - j, T), :] * w[j:j + 1, :]
            o_ref[...] = acc

        one(dq_ref, dqh_ref, wq_ref, oq_ref)
        one(dk_ref, dkh_ref, wk_ref, ok_ref)
        one(dv_ref, dvh_ref, wv_ref, ov_ref)

    tile = pl.BlockSpec((T, DN_WIDTH), lambda i: (i, 0))
    halo = pl.BlockSpec((HALO, DN_WIDTH), lambda i: (jnp.minimum((i + 1) * (T // HALO), S // HALO - 1), 0))
    wspec = lambda sec: pl.BlockSpec((CONV_WIDTH, DN_WIDTH), lambda i, sec=sec: (0, sec))
    return pl.pallas_call(
        body, name="dn_conv_bwd_x", grid=(nt,),
        in_specs=[tile, halo, tile, halo, tile, halo, wspec(0), wspec(1), wspec(2)],
        out_specs=[tile, tile, tile],
        out_shape=[jax.ShapeDtypeStruct((S, DN_WIDTH), F32)] * 3,
        scratch_shapes=[pltpu.VMEM((T + HALO, 128), F32)],
        compiler_params=_params(1),
    )(dcq, dcq, dck, dck, dcv, dcv, conv_w, conv_w, conv_w)


PREP_CHUNKS = 4
SCAN_CHUNKS = 8


def _bnn(a, b):
    return lax.dot_general(a, b, (((2,), (1,)), ((0,), (0,))), preferred_element_type=F32, precision=HI)


def _bnt(a, b):
    return lax.dot_general(a, b, (((2,), (2,)), ((0,), (0,))), preferred_element_type=F32, precision=HI)


def _btn(a, b):
    return lax.dot_general(a, b, (((1,), (1,)), ((0,), (0,))), preferred_element_type=F32, precision=HI)


def _tri_inverse_b(a, blk, eye):
    dg = jnp.where(blk, a, 0.0)
    lo = a - dg
    d2 = _bnn(dg, dg)
    d4 = _bnn(d2, d2)
    d8 = _bnn(d4, d4)
    td = _bnn(_bnn(_bnn(eye - dg, eye + d2), eye + d4), eye + d8)
    b = _bnn(td, lo)
    b2 = _bnn(b, b)
    return _bnn(_bnn(eye - b, eye + b2), td)


def _dn_common_b(bds, avec, dvec, q_raw, k, v, t=None):
    C = DN_CHUNK
    lane = lax.broadcasted_iota(jnp.int32, (C, 128), 1)
    row = lax.broadcasted_iota(jnp.int32, (1, C, C), 1)
    col = lax.broadcasted_iota(jnp.int32, (1, C, C), 2)
    incl = row >= col
    strict = row > col
    eye = (row == col).astype(F32)
    blk = (row // 16) == (col // 16)
    pick = lambda tile, ln: jnp.sum(jnp.where(lane == ln, tile, 0.0), axis=-1, keepdims=True)
    betas, graws, zcs = [], [], []
    for bd in bds:
        z = bd + dvec
        g_all = -jnp.exp(avec) * (jnp.maximum(z, 0.0) + jnp.log(1.0 + jnp.exp(-jnp.abs(z))))
        beta_all = _sigmoid(bd)
        for h in range(DN_HEADS):
            betas.append(pick(beta_all, h))
            graws.append(pick(g_all, DN_HEADS + h))
            zcs.append(pick(z, DN_HEADS + h))
    beta, graw, zc = jnp.stack(betas), jnp.stack(graws), jnp.stack(zcs)
    to_row = lambda c: jnp.sum(eye * c, axis=1, keepdims=True)
    gc = jnp.sum(jnp.where(incl, to_row(graw), 0.0), axis=-1, keepdims=True)
    decay = jnp.exp(jnp.where(incl, gc - to_row(gc), NEG_BIG))
    q = q_raw * (DN_HEAD_DIM ** -0.5)
    kb = k * beta
    kk = _bnt(kb, k)
    if t is None:
        t = _tri_inverse_b(jnp.where(strict, kk * decay, 0.0), blk, eye)
    eg = jnp.exp(gc)
    rhs_w = kb * eg
    u = _bnn(t, v * beta)
    w = _bnn(t, rhs_w)
    qk = _bnt(q, k)
    aq = jnp.where(incl, qk * decay, 0.0)
    last = lax.broadcasted_iota(jnp.int32, (1, C, 1), 1) == C - 1
    g_last = jnp.sum(jnp.where(last, gc, 0.0), axis=1, keepdims=True)
    ekd = jnp.exp(g_last - gc)
    return dict(beta=beta, graw=graw, zc=zc, gc=gc, decay=decay, q=q, kb=kb, kk=kk, t=t, eg=eg, rhs_w=rhs_w,
                u=u, w=w, qk=qk, aq=aq, g_last=g_last, ekd=ekd, kd=k * ekd, qg=q * eg,
                incl=incl, strict=strict, eye=eye, lane=lane, row=row, col=col, last=last)


def _stack_heads(ref, rows):
    return jnp.stack([ref[rows, h * DN_HEAD_DIM:(h + 1) * DN_HEAD_DIM] for h in range(DN_HEADS)])


def _stack_units(ref, nc):
    C = DN_CHUNK
    return jnp.concatenate([_stack_heads(ref, slice(ci * C, (ci + 1) * C)) for ci in range(nc)], axis=0)


def _store_units(ref, val, nc):
    C = DN_CHUNK
    for ci in range(nc):
        for h in range(DN_HEADS):
            ref[ci * C:(ci + 1) * C, h * DN_HEAD_DIM:(h + 1) * DN_HEAD_DIM] = val[ci * DN_HEADS + h]


def _dn_prep(qn, kn, v, bd, avec, dvec):
    S = qn.shape[0]
    C = DN_CHUNK
    N = S // C
    nc = PREP_CHUNKS

    def body(q_ref, k_ref, v_ref, bd_ref, a_ref, d_ref, u_ref, w_ref, qg_ref, kd_ref, aq_ref, t_ref, egl_ref):
        bds = [bd_ref[ci * C:(ci + 1) * C, :] for ci in range(nc)]
        c = _dn_common_b(bds, a_ref[...], d_ref[...], _stack_units(q_ref, nc), _stack_units(k_ref, nc), _stack_units(v_ref, nc))
        _store_units(u_ref, c["u"], nc)
        _store_units(w_ref, c["w"], nc)
        _store_units(qg_ref, c["qg"], nc)
        _store_units(kd_ref, c["kd"], nc)
        egl = jnp.broadcast_to(jnp.exp(c["g_last"]), (nc * DN_HEADS, 1, 128))
        for ci in range(nc):
            for h in range(DN_HEADS):
                aq_ref[h, ci * C:(ci + 1) * C, :] = c["aq"][ci * DN_HEADS + h]
                t_ref[h, ci * C:(ci + 1) * C, :] = c["t"][ci * DN_HEADS + h]
            egl_ref[ci * 8:(ci + 1) * 8, :] = jnp.concatenate(
                [egl[ci * DN_HEADS + h] for h in range(DN_HEADS)] + [jnp.zeros((8 - DN_HEADS, 128), F32)], axis=0)

    tok = lambda w: pl.BlockSpec((nc * C, w), lambda n: (n, 0))
    sq = pl.BlockSpec((DN_HEADS, nc * C, C), lambda n: (0, n, 0))
    vec = pl.BlockSpec((1, 128), lambda n: (0, 0))
    return pl.pallas_call(
        body, name="dn_prep", grid=(N // nc,),
        in_specs=[tok(DN_WIDTH)] * 3 + [tok(128), vec, vec],
        out_specs=[tok(DN_WIDTH)] * 4 + [sq, sq, pl.BlockSpec((nc * 8, 128), lambda n: (n, 0))],
        out_shape=[jax.ShapeDtypeStruct((S, DN_WIDTH), F32)] * 4 + [jax.ShapeDtypeStruct((DN_HEADS, S, C), F32)] * 2
                  + [jax.ShapeDtypeStruct((N * 8, 128), F32)],
        compiler_params=_params(1),
    )(qn, kn, v, bd, avec, dvec)


def _dn_scan_fwd(u, w, qg, kd, aq, egl, gate, dn_gain):
    S = u.shape[0]
    C = DN_CHUNK
    N = S // C
    HD = DN_HEAD_DIM
    nc = SCAN_CHUNKS

    def body(u_ref, w_ref, qg_ref, kd_ref, aq_ref, egl_ref, gate_ref, gain_ref, dn_ref, o_ref, vn_ref, st_ref, state_ref):
        @pl.when(pl.program_id(0) == 0)
        def _():
            state_ref[...] = jnp.zeros_like(state_ref)

        gain = gain_ref[...]
        for ci in range(nc):
            rows = slice(ci * C, (ci + 1) * C)
            st = state_ref[...]
            for h in range(DN_HEADS):
                st_ref[ci * DN_WIDTH + h * HD:ci * DN_WIDTH + (h + 1) * HD, :] = st[h]
            v_new = _stack_heads(u_ref, rows) - _bnn(_stack_heads(w_ref, rows), st)
            o = _bnn(_stack_heads(qg_ref, rows), st) + _bnn(aq_ref[:, rows, :], v_new)
            egl = jnp.stack([egl_ref[ci * 8 + h:ci * 8 + h + 1, :] for h in range(DN_HEADS)])
            state_ref[...] = st * egl + _btn(_stack_heads(kd_ref, rows), v_new)
            r = lax.rsqrt(jnp.mean(o * o, axis=-1, keepdims=True) + NORM_EPS)
            gt = _stack_heads(gate_ref, rows)
            dn = o * r * gain * (gt * _sigmoid(gt))
            for h in range(DN_HEADS):
                sl = slice(h * HD, (h + 1) * HD)
                vn_ref[rows, sl] = v_new[h]
                o_ref[rows, sl] = o[h]
                dn_ref[rows, sl] = dn[h]

    tok = lambda wd: pl.BlockSpec((nc * C, wd), lambda n: (n, 0))
    sq = pl.BlockSpec((DN_HEADS, nc * C, C), lambda n: (0, n, 0))
    vec = pl.BlockSpec((1, 128), lambda n: (0, 0))
    return pl.pallas_call(
        body, name="dn_scan_fwd", grid=(N // nc,),
        in_specs=[tok(DN_WIDTH)] * 4 + [sq, pl.BlockSpec((nc * 8, 128), lambda n: (n, 0)), tok(DN_WIDTH), vec],
        out_specs=[tok(DN_WIDTH)] * 3 + [pl.BlockSpec((nc * DN_WIDTH, HD), lambda n: (n, 0))],
        out_shape=[jax.ShapeDtypeStruct((S, DN_WIDTH), F32)] * 3 + [jax.ShapeDtypeStruct((N * DN_WIDTH, HD), F32)],
        scratch_shapes=[pltpu.VMEM((DN_HEADS, HD, HD), F32)],
        compiler_params=_params(1),
    )(u, w, qg, kd, aq, egl, gate, dn_gain)


def _dn_scan_bwd(w, qg, kd, aq, egl, gate, dn_gain, o, ddn):
    S = w.shape[0]
    C = DN_CHUNK
    N = S // C
    HD = DN_HEAD_DIM
    nc = SCAN_CHUNKS

    def body(w_ref, qg_ref, kd_ref, aq_ref, egl_ref, gate_ref, gain_ref, o_ref, ddn_ref,
             do_ref, dvn_ref, dgate_ref, dst_ref, small_ref, dstate_ref):
        @pl.when(pl.program_id(0) == 0)
        def _():
            dstate_ref[...] = jnp.zeros_like(dstate_ref)
            small_ref[...] = jnp.zeros_like(small_ref)

        gain = gain_ref[...]
        d_gain = jnp.zeros((1, 128), F32)
        for ci in reversed(range(nc)):
            rows = slice(ci * C, (ci + 1) * C)
            dsn = dstate_ref[...]
            for h in range(DN_HEADS):
                dst_ref[ci * DN_WIDTH + h * HD:ci * DN_WIDTH + (h + 1) * HD, :] = dsn[h]
            ov = _stack_heads(o_ref, rows)
            r = lax.rsqrt(jnp.mean(ov * ov, axis=-1, keepdims=True) + NORM_EPS)
            on = ov * r
            gt = _stack_heads(gate_ref, rows)
            sgt = _sigmoid(gt)
            silu_g = gt * sgt
            dy = _stack_heads(ddn_ref, rows)
            d_gain = d_gain + jnp.sum(jnp.sum(dy * on * silu_g, axis=1, keepdims=True), axis=0)
            dgate = dy * on * gain * (sgt * (1.0 + gt * (1.0 - sgt)))
            don = dy * gain * silu_g
            do = r * (don - on * jnp.mean(don * on, axis=-1, keepdims=True))
            d_vnew = _btn(aq_ref[:, rows, :], do) + _bnn(_stack_heads(kd_ref, rows), dsn)
            egl = jnp.stack([egl_ref[ci * 8 + h:ci * 8 + h + 1, :] for h in range(DN_HEADS)])
            dstate_ref[...] = _btn(_stack_heads(qg_ref, rows), do) + dsn * egl - _btn(_stack_heads(w_ref, rows), d_vnew)
            for h in range(DN_HEADS):
                sl = slice(h * HD, (h + 1) * HD)
                do_ref[rows, sl] = do[h]
                dvn_ref[rows, sl] = d_vnew[h]
                dgate_ref[rows, sl] = dgate[h]
        small_ref[...] += jnp.concatenate([d_gain, jnp.zeros((7, 128), F32)], axis=0)

    nb = N // nc
    tok = lambda wd: pl.BlockSpec((nc * C, wd), lambda i: (nb - 1 - i, 0))
    sq = pl.BlockSpec((DN_HEADS, nc * C, C), lambda i: (0, nb - 1 - i, 0))
    vec = pl.BlockSpec((1, 128), lambda i: (0, 0))
    return pl.pallas_call(
        body, name="dn_scan_bwd", grid=(nb,),
        in_specs=[tok(DN_WIDTH)] * 3 + [sq, pl.BlockSpec((nc * 8, 128), lambda i: (nb - 1 - i, 0)), tok(DN_WIDTH), vec,
                                       tok(DN_WIDTH), tok(DN_WIDTH)],
        out_specs=[tok(DN_WIDTH)] * 3 + [pl.BlockSpec((nc * DN_WIDTH, HD), lambda i: (nb - 1 - i, 0)),
                                        pl.BlockSpec((8, 128), lambda i: (0, 0))],
        out_shape=[jax.ShapeDtypeStruct((S, DN_WIDTH), F32)] * 3 + [jax.ShapeDtypeStruct((N * DN_WIDTH, HD), F32),
                                                                  jax.ShapeDtypeStruct((8, 128), F32)],
        scratch_shapes=[pltpu.VMEM((DN_HEADS, HD, HD), F32)],
        compiler_params=_params(1),
    )(w, qg, kd, aq, egl, gate, dn_gain, o, ddn)


def _dn_post(qn, kn, v, bd, avec, dvec, t_inv, v_new_all, states, dstates, do_all, dvn_all, comm=None):
    S = qn.shape[0]
    C = DN_CHUNK
    N = S // C
    HD = DN_HEAD_DIM
    nc = PREP_CHUNKS
    B = nc * DN_HEADS

    def body(q_ref, k_ref, v_ref, bd_ref, a_ref, d_ref, t_ref, vn_ref, st_ref, dst_ref, do_ref, dvn_ref,
             dq_ref, dk_ref, dv_ref, dbd_ref, small_ref):
        @pl.when(pl.program_id(0) == 0)
        def _():
            small_ref[...] = jnp.zeros_like(small_ref)

        avec = a_ref[...]
        bds = [bd_ref[ci * C:(ci + 1) * C, :] for ci in range(nc)]
        k = _stack_units(k_ref, nc)
        vv = _stack_units(v_ref, nc)
        t = jnp.concatenate([t_ref[:, ci * C:(ci + 1) * C, :] for ci in range(nc)], axis=0)
        c = _dn_common_b(bds, avec, d_ref[...], _stack_units(q_ref, nc), k, vv, t=t)
        q, kb, eg, u, w = c["q"], c["kb"], c["eg"], c["u"], c["w"]
        beta, decay, incl, strict, eye = c["beta"], c["decay"], c["incl"], c["strict"], c["eye"]
        st = jnp.stack([st_ref[b * HD:(b + 1) * HD, :] for b in range(B)])
        dsn = jnp.stack([dst_ref[b * HD:(b + 1) * HD, :] for b in range(B)])
        v_new = _stack_units(vn_ref, nc)
        do = _stack_units(do_ref, nc)
        d_vnew = _stack_units(dvn_ref, nc)
        egl = jnp.exp(c["g_last"])
        daq = jnp.where(incl, _bnt(do, v_new), 0.0)
        d_qg = _bnt(do, st)
        d_kd = _bnt(v_new, dsn)
        d_glast = jnp.sum(jnp.sum(dsn * st, axis=-1, keepdims=True), axis=1, keepdims=True) * egl
        d_w = -_bnt(d_vnew, st)
        d_ru = _btn(t, d_vnew)
        d_rw = _btn(t, d_w)
        da = -jnp.where(strict, _bnt(d_ru, u) + _bnt(d_rw, w), 0.0)
        dv = d_ru * beta
        dbeta = jnp.sum(d_ru * vv, axis=-1, keepdims=True)
        dkb = d_rw * eg
        dgc = jnp.sum(d_rw * c["rhs_w"], axis=-1, keepdims=True)
        dkk = da * decay
        ddecay = da * c["kk"]
        dkb = dkb + _bnn(dkk, k)
        dk = _btn(dkk, kb)
        dqk = daq * decay
        ddecay = ddecay + daq * c["qk"]
        dq = _bnn(dqk, k)
        dk = dk + _btn(dqk, q)
        m = ddecay * decay
        col_sum = jnp.sum(m, axis=1, keepdims=True)
        dgc = dgc + jnp.sum(m, axis=-1, keepdims=True) - jnp.sum(eye * col_sum, axis=-1, keepdims=True)
        dq = dq + d_qg * eg
        dgc = dgc + jnp.sum(d_qg * c["qg"], axis=-1, keepdims=True)
        dk = dk + d_kd * c["ekd"]
        tk = jnp.sum(d_kd * c["kd"], axis=-1, keepdims=True)
        dgc = dgc - tk
        d_glast = d_glast + jnp.sum(tk, axis=1, keepdims=True)
        dk = dk + dkb * beta
        dbeta = dbeta + jnp.sum(dkb * k, axis=-1, keepdims=True)
        dgc = dgc + jnp.where(c["last"], d_glast, 0.0)
        dgc_row = jnp.sum(eye * dgc, axis=1, keepdims=True)
        dgraw = jnp.sum(jnp.where(c["col"] >= c["row"], dgc_row, 0.0), axis=-1, keepdims=True)
        _store_units(dq_ref, dq * (HD ** -0.5), nc)
        _store_units(dk_ref, dk, nc)
        _store_units(dv_ref, dv, nc)
        dbraw = dbeta * beta * (1.0 - beta)
        dzc = dgraw * _sigmoid(c["zc"])
        ga = dgraw * c["graw"]
        lane = c["lane"]
        lane1 = lax.broadcasted_iota(jnp.int32, (1, 128), 1)
        neg_ea = -jnp.exp(avec)
        d_alog = jnp.zeros((1, 128), F32)
        d_dt = jnp.zeros((1, 128), F32)
        for ci in range(nc):
            dbd = jnp.zeros((C, 128), F32)
            for h in range(DN_HEADS):
                b = ci * DN_HEADS + h
                dz = dzc[b] * neg_ea
                dbd = dbd + jnp.where(lane == h, dbraw[b], 0.0) + jnp.where(lane == DN_HEADS + h, dz, 0.0)
                d_alog = d_alog + jnp.where(lane1 == DN_HEADS + h, jnp.sum(ga[b], axis=0, keepdims=True), 0.0)
                d_dt = d_dt + jnp.where(lane1 == DN_HEADS + h, jnp.sum(dz, axis=0, keepdims=True), 0.0)
            dbd_ref[ci * C:(ci + 1) * C, :] = dbd
        small_ref[...] += jnp.concatenate([d_alog, d_dt, jnp.zeros((6, 128), F32)], axis=0)

    tok = lambda wd: pl.BlockSpec((nc * C, wd), lambda n: (n, 0))
    big = pl.BlockSpec((nc * DN_WIDTH, HD), lambda n: (n, 0))
    sq = pl.BlockSpec((DN_HEADS, nc * C, C), lambda n: (0, n, 0))
    vec = pl.BlockSpec((1, 128), lambda n: (0, 0))
    return _call(
        body, (qn, kn, v, bd, avec, dvec, t_inv, v_new_all, states, dstates, do_all, dvn_all),
        name="dn_post", grid=(N // nc,), comm=comm,
        in_specs=[tok(DN_WIDTH)] * 3 + [tok(128), vec, vec, sq, tok(DN_WIDTH), big, big, tok(DN_WIDTH), tok(DN_WIDTH)],
        out_specs=[tok(DN_WIDTH)] * 3 + [tok(128), pl.BlockSpec((8, 128), lambda n: (0, 0))],
        out_shape=[jax.ShapeDtypeStruct((S, DN_WIDTH), F32)] * 3 + [jax.ShapeDtypeStruct((S, 128), F32),
                                                                  jax.ShapeDtypeStruct((8, 128), F32)])


def _outproj_fwd(x, attn, dn, w_out):
    S, D = x.shape
    tm = 512

    def body(x_ref, a_ref, d_ref, w_ref, xo_ref, mix_ref):
        a = a_ref[...].astype(BF16)
        dd = d_ref[...].astype(BF16)
        mix_ref[:, 0:ATTN_WIDTH] = a
        mix_ref[:, ATTN_WIDTH:] = dd
        xo_ref[...] = x_ref[...] + _nn(a, w_ref[0:ATTN_WIDTH, :]) + _nn(dd, w_ref[ATTN_WIDTH:, :])

    tok = lambda w: pl.BlockSpec((tm, w), lambda i: (i, 0))
    return pl.pallas_call(
        body, name="outproj_fwd", grid=(S // tm,),
        in_specs=[tok(D), tok(ATTN_WIDTH), tok(DN_WIDTH), pl.BlockSpec((D, D), lambda i: (0, 0))],
        out_specs=[tok(D), tok(D)],
        out_shape=[jax.ShapeDtypeStruct((S, D), F32), jax.ShapeDtypeStruct((S, D), BF16)],
        compiler_params=_params(1),
    )(x, attn, dn, w_out)


def _outproj_bwd(dx, w_out, attn, comm=None):
    S, D = dx.shape
    tm = VIEW_TILE

    def body(dx_ref, w_ref, attn_ref, da1, da4, da16, dl1, dl4, dl16, ddn_ref, dxb_ref, planes):
        d = dx_ref[...].astype(BF16)
        dxb_ref[...] = d
        da = _nt(d, w_ref[0:ATTN_WIDTH, :])
        ddn_ref[...] = _nt(d, w_ref[ATTN_WIDTH:, :])
        _tile_to_views(da, planes, (da1, da4, da16))
        lo = lax.broadcasted_iota(jnp.int32, (tm, 128), 1) < 64
        cols = []
        for G in range(4):
            sl = slice(G * 128, (G + 1) * 128)
            t = da[:, sl] * attn_ref[:, sl]
            d0 = jnp.sum(jnp.where(lo, t, 0.0), axis=-1, keepdims=True)
            d1 = jnp.sum(jnp.where(lo, 0.0, t), axis=-1, keepdims=True)
            cols.append(jnp.where(lo, d0, d1))
        _tile_to_views(jnp.concatenate(cols, axis=1), planes, (dl1, dl4, dl16))

    tok = lambda w: pl.BlockSpec((tm, w), lambda i: (i, 0))
    views = [_view_spec(d) for d in DILATIONS]
    return _call(
        body, (dx, w_out, attn), name="outproj_bwd", grid=(S // tm,), comm=comm,
        in_specs=[tok(D), pl.BlockSpec((D, D), lambda i: (0, 0)), tok(ATTN_WIDTH)],
        out_specs=views + views + [tok(DN_WIDTH), tok(D)],
        out_shape=[_view_shape(S, d, F32) for d in DILATIONS] * 2
                  + [jax.ShapeDtypeStruct((S, DN_WIDTH), F32), jax.ShapeDtypeStruct((S, D), BF16)],
        scratch_shapes=[pltpu.VMEM((4, tm, 128), F32)])


def _adamw(w, g, m, v, name):
    R, Ccols = w.shape[0], w.shape[-1]
    tr = next((t for t in range(512, 7, -8) if R % t == 0), R)
    c1 = 1.0 - ADAM_B1 ** ADAM_STEP
    c2 = 1.0 - ADAM_B2 ** ADAM_STEP

    def body(w_ref, g_ref, m_ref, v_ref, d_ref, nm_ref, nv_ref):
        gv = g_ref[...]
        mn = ADAM_B1 * m_ref[...] + (1.0 - ADAM_B1) * gv
        vn = ADAM_B2 * v_ref[...] + (1.0 - ADAM_B2) * (gv * gv)
        nm_ref[...] = mn
        nv_ref[...] = vn
        d_ref[...] = -ADAM_LR * ((mn / c1) / (jnp.sqrt(vn / c2) + ADAM_EPS) + ADAM_WD * w_ref[...])

    if w.ndim == 2:
        grid, spec = (R // tr,), pl.BlockSpec((tr, Ccols), lambda i: (i, 0))
    else:
        grid, spec = (2,), pl.BlockSpec((R // 2, 1, Ccols), lambda i: (i, 0, 0))
    return pl.pallas_call(
        body, name=name, grid=grid, in_specs=[spec] * 4, out_specs=[spec] * 3,
        out_shape=[jax.ShapeDtypeStruct(w.shape, F32)] * 3, compiler_params=_params(1),
    )(w, g, m, v)


LATE_WEIGHTS = ("w_in", "w_out", "ffn2_gate", "ffn2_up", "ffn2_down")


def _local_step(x, target, wts, small, dist=None):
    g1, g2, gm, gf = small["norm_ffn1"], small["norm_ffn2"], small["norm_mix"], small["norm_final"]
    wts = dict(wts)

    def reduce_start(gs, tag):
        return _rs_add_pairs(gs, _swap_sibling(gs, True, "rs_swap_halves_" + tag), dist["c"], "rs_add_pairs_" + tag)

    (x1, h1, fg1, fu1), late = _ffn_fwd(x, g1, wts["ffn1_gate"], wts["ffn1_up"], wts["ffn1_down"], "ffn1_fwd",
                                        comm=_ag_comm(dist["late"]) if dist else None)
    if dist:
        wts.update(zip(LATE_WEIGHTS, late))
        wts["w_out"] = wts["w_out"].reshape(D_MODEL, D_MODEL)
        wts["w_in"] = _permute_w_in(wts["w_in"][:, :IN_COLS // N_CHIPS].reshape(IN_COLS, D_MODEL))
    h2, *qkv, xq, xk, xv, gate, bd = _inproj_fwd(x1, gm, wts["w_in"])
    aq, ak, av = qkv[0:3], qkv[3:6], qkv[6:9]
    parts = [_attn_fwd(aq[p], ak[p], av[p], d, f"attn_fwd_d{d}") for p, d in enumerate(DILATIONS)]
    attn, *lse = _attn_merge(parts)
    conv_w = small["conv_w"]
    qn, kn, vv = _conv_fwd(xq, xk, xv, conv_w)
    dn_u, dn_w, dn_qg, dn_kd, dn_aq, dn_t, dn_egl = _dn_prep(qn, kn, vv, bd, small["avec"], small["dvec"])
    dn, o_dn, v_new, states = _dn_scan_fwd(dn_u, dn_w, dn_qg, dn_kd, dn_aq, dn_egl, gate, small["dn_norm"])
    x2, mix = _outproj_fwd(x1, attn, dn, wts["w_out"])
    (dx3, h3, fg2, fu2, loss, d_gf), _ = _ffn_fwd(x2, g2, wts["ffn2_gate"], wts["ffn2_up"], wts["ffn2_down"], "ffn2_fwd",
                                                 head=(gf, target))

    grads = {}
    (dx2, d_g2, dfg2, dfu2, act2, dout2), _ = _ffn_bwd(dx3, x2, g2, fg2, fu2, wts["ffn2_down"], wts["ffn2_gate"],
                                                      wts["ffn2_up"], "ffn2_bwd")
    tk = 2048
    grads["ffn2_gate"], _ = _dw_chunks(dfg2, h3, tk, "dw_ffn2_gate")
    grads["ffn2_up"], _ = _dw_chunks(dfu2, h3, tk, "dw_ffn2_up")
    grads["ffn2_down"], _ = _dw_chunks(act2, dout2, tk, "dw_ffn2_down")
    group_a = ("ffn2_gate", "ffn2_up", "ffn2_down")
    gs_a = [grads[n] for n in group_a]

    (*dviews, ddn, dx2b), swapped_a = _outproj_bwd(dx2, wts["w_out"], attn, comm=_swap_comm(gs_a, True) if dist else None)
    parts_a = _rs_add_pairs(gs_a, swapped_a, dist["c"], "rs_add_pairs_a") if dist else None
    dattn, dd = dviews[0:3], dviews[3:6]
    grads["w_out"] = _matmul_tn(mix, dx2b, D_MODEL, tk, "dw_out").reshape(N_CHIPS, D_MODEL // N_CHIPS, D_MODEL)

    daq, dak, dav = [], [], []
    for p, d in enumerate(DILATIONS):
        daq.append(_attn_bwd_q(aq[p], ak[p], av[p], dattn[p], lse[p], dd[p], d, f"attn_bwd_q_d{d}"))
        dk_p, dv_p = _attn_bwd_kv(aq[p], ak[p], av[p], dattn[p], lse[p], dd[p], d, f"attn_bwd_kv_d{d}")
        dak.append(dk_p)
        dav.append(dv_p)

    do_dn, dvn, dgate, dstates, d_dn_gain = _dn_scan_bwd(dn_w, dn_qg, dn_kd, dn_aq, dn_egl, gate, small["dn_norm"], o_dn, ddn)
    (dqn, dkn, dvv, dbd, dn_small), recv_a = _dn_post(qn, kn, vv, bd, small["avec"], small["dvec"], dn_t, v_new, states,
                                                      dstates, do_dn, dvn, comm=_rsx_comm(parts_a) if dist else None)
    dcq, dck, dcv, dwq, dwk, dwv = _conv_bwd_pre(xq, xk, xv, conv_w, dqn, dkn, dvv)
    dxq, dxk, dxv = _conv_bwd_x(dcq, dck, dcv, conv_w)
    d_conv = jnp.concatenate([dwq[:CONV_WIDTH], dwk[:CONV_WIDTH], dwv[:CONV_WIDTH]], axis=1)

    dx1, d_gm, dproj = _inproj_bwd(dx2, x1, gm, [daq, dak, dav], [dxq, dxk, dxv, dgate], dbd, wts["w_in"])
    gi = _matmul_tn(dproj, h2, IN_COLS_PADDED, 512, "dw_in")
    if dist:
        gate_end = QKV_COLS + DN_WIDTH
        gi = jnp.concatenate([gi[:QKV_COLS], gi[gate_end:gate_end + LOGIT_COLS], gi[QKV_COLS:gate_end]], axis=0)
        gi = gi.reshape(N_CHIPS, IN_COLS // N_CHIPS, D_MODEL)
        gi = jnp.pad(gi, ((0, 0), (0, W_IN_ROWS - IN_COLS // N_CHIPS), (0, 0)))
    grads["w_in"] = gi
    group_b = ("w_in", "w_out")
    parts_b = reduce_start([grads[n] for n in group_b], "b") if dist else None

    (dx0, d_g1, dfg1, dfu1, act1, dout1), recv_b = _ffn_bwd(dx1, x, g1, fg1, fu1, wts["ffn1_down"], wts["ffn1_gate"],
                                                           wts["ffn1_up"], "ffn1_bwd",
                                                           comm=_rsx_comm(parts_b) if dist else None)
    group_c = ("ffn1_gate", "ffn1_up", "ffn1_down")
    carried, parts_c, landed_all = None, [], []
    if dist:
        totals_ab = _rs_add_totals(list(parts_a) + list(parts_b), list(recv_a) + list(recv_b), dist["chip"],
                                   "rs_add_totals_ab")
        carried = _swap_comm(totals_ab, False)
    for n, (lhs, rhs) in zip(group_c, ((dfg1, h1), (dfu1, h1), (act1, dout1))):
        grads[n], landed = _dw_chunks(lhs, rhs, tk, "dw_" + n, comm=carried)
        landed_all.append(list(landed))
        if dist:
            parts = reduce_start([grads[n]], n)
            parts_c += parts
            carried = _rsx_comm(parts)

    small_grads = dict(norm_ffn1=d_g1, norm_mix=d_gm, norm_ffn2=d_g2, norm_final=d_gf, conv_w=d_conv,
                       a_log=dn_small[0:1], dt_bias=dn_small[1:2], dn_norm=d_dn_gain[0:1])
    if dist:
        theirs_ab = landed_all[0]
        recv_c = landed_all[1] + landed_all[2] + list(_rs_exchange_arrays(parts_c[-1:]))
        totals_c = _rs_add_totals(parts_c, recv_c, dist["chip"], "rs_add_totals_c")
        theirs_c = _swap_sibling(totals_c, False, "rs_share_total")
        names = group_a + group_b + group_c
        grads = {n: (mine, other) for n, mine, other in zip(names, list(totals_ab) + list(totals_c), theirs_ab + list(theirs_c))}
    return loss, dx0, grads, small_grads


HBM =pl.BlockSpec(memory_space=pl.ANY)
VMEM_SPEC = pl.BlockSpec(memory_space=pltpu.VMEM)


def _coords():
    return lax.axis_index("x"), lax.axis_index("y"), lax.axis_index("c")


def _remote(src, dst, send_sems, recv_sems, k, dev):
    return pltpu.make_async_remote_copy(src_ref=src, dst_ref=dst, send_sem=send_sems.at[k], recv_sem=recv_sems.at[k],
                                        device_id=dev, device_id_type=MESH)


def _allreduce_small(buf, name):
    R, Cc = buf.shape

    def body(src_ref, out_ref, recv_ref, send_sems, recv_sems):
        x, y, c = _coords()
        copies = []
        for m in range(1, 8):
            fx, fy, fc = (m >> 2) & 1, (m >> 1) & 1, m & 1
            dev = (x ^ fx if fx else x, y ^ fy if fy else y, c ^ fc if fc else c)
            cp = _remote(src_ref, recv_ref.at[m - 1], send_sems, recv_sems, m - 1, dev)
            cp.start()
            copies.append(cp)
        for cp in copies:
            cp.wait()
        r = [src_ref[...]] + [recv_ref[m] for m in range(7)]
        out_ref[...] = ((r[0] + r[1]) + (r[2] + r[3])) + ((r[4] + r[5]) + (r[6] + r[7]))

    return pl.pallas_call(
        body, name=name, out_shape=jax.ShapeDtypeStruct((R, Cc), F32),
        in_specs=[VMEM_SPEC], out_specs=VMEM_SPEC,
        scratch_shapes=[pltpu.VMEM((7, R, Cc), F32), pltpu.SemaphoreType.DMA((7,)), pltpu.SemaphoreType.DMA((7,))],
    )(buf)


BIG = ("ffn1_gate", "ffn1_up", "ffn1_down", "w_in", "w_out", "ffn2_gate", "ffn2_up", "ffn2_down")
ROW_SHARDED = ("ffn1_down", "w_out", "ffn2_down")
W_IN_ROWS = 960


def _rows(ref, start, size):
    return ref.at[pl.ds(pl.multiple_of(start, 16), size)]


def _allgather_arrays(shards):
    n = len(shards)
    _, shapes, n_sems, start, finish, middle = _ag_comm(shards)

    def body(*refs):
        for phase in (start, middle, finish):
            phase(refs[:n], refs[n:2 * n], refs[2 * n], refs[2 * n + 1])

    return pl.pallas_call(
        body, name="allgather_weights", out_shape=shapes, in_specs=[HBM] * n, out_specs=[HBM] * n,
        scratch_shapes=[pltpu.SemaphoreType.DMA((n_sems,)), pltpu.SemaphoreType.DMA((n_sems,))],
    )(*shards)


def _ag_copies(srcs, outs, send_sems, recv_sems):
    x, y, c = _coords()
    sib = (x, y, 1 - c)
    xn, yn, dg = (1 - x, y), (x, 1 - y), (1 - x, 1 - y)
    plan = []
    for a, (src, out) in enumerate(zip(srcs, outs)):
        h = src.shape[0] // 2
        q = h // 2
        cp = lambda s, d, k, dev: _remote(s, d, send_sems, recv_sems, 8 * a + k, dev)
        slot = lambda chip: out.at[2 * chip[0] + chip[1]]
        mine, dst = _rows(src, c * h, h), _rows(slot((x, y)), c * h, h)
        piece = lambda chip, start, size, k, dev: cp(_rows(slot(chip), start, size), _rows(slot(chip), start, size), k, dev)
        plan.append(dict(
            own=cp(src, slot((x, y)), 6, sib),
            to_x=cp(mine, dst, 0, (*xn, c)), to_y=cp(mine, dst, 1, (*yn, c)),
            from_x=piece(xn, c * h, h, 0, sib), from_y=piece(yn, c * h, h, 1, sib),
            relay_y=piece(xn, c * h, q, 2, (*yn, c)), relay_x=piece(yn, c * h + q, q, 7, (*xn, c)),
            pass_x=piece(xn, c * h, h, 3, sib), pass_y=piece(yn, c * h, h, 4, sib), pass_d=piece(dg, c * h, h, 5, sib),
            diag_1=piece(dg, c * h, q, 2, sib), diag_2=piece(dg, c * h + q, q, 7, sib),
            got=[piece(chip, (1 - c) * h, h, k, sib) for k, chip in ((3, xn), (4, yn), (5, dg))]))
    return plan


def _ag_start(*refs):
    for p in _ag_copies(*refs):
        for k in ("own", "to_x", "to_y"):
            p[k].start()


def _ag_middle(*refs):
    for p in _ag_copies(*refs):
        p["from_x"].wait_recv()
        p["relay_y"].start()
        p["pass_x"].start()
        p["from_y"].wait_recv()
        p["relay_x"].start()
        p["pass_y"].start()


def _ag_finish(*refs):
    plan = _ag_copies(*refs)
    for p in plan:
        p["diag_1"].wait_recv()
        p["diag_2"].wait_recv()
        p["pass_d"].start()
    for p in plan:
        for cp in p["got"]:
            cp.wait_recv()
        p["own"].wait_recv()
        for k in ("own", "to_x", "to_y", "relay_y", "relay_x", "pass_x", "pass_y", "pass_d"):
            p[k].wait_send()


def _ag_comm(shards):
    shapes = [jax.ShapeDtypeStruct((N_CHIPS,) + s.shape, s.dtype) for s in shards]
    return (list(shards), shapes, 8 * len(shards), _ag_start, _ag_finish, _ag_middle)


def _swap_sibling(arrs, pick_other_half, name):
    n = len(arrs)
    _, outs, n_sems, start, finish = _swap_comm(arrs, pick_other_half)

    def body(*refs):
        start(refs[:n], refs[n:2 * n], refs[2 * n], refs[2 * n + 1])
        finish(refs[:n], refs[n:2 * n], refs[2 * n], refs[2 * n + 1])

    return pl.pallas_call(
        body, name=name, out_shape=outs, in_specs=[HBM] * n, out_specs=[HBM] * n,
        scratch_shapes=[pltpu.SemaphoreType.DMA((n_sems,)), pltpu.SemaphoreType.DMA((n_sems,))],
    )(*arrs)


def _swap_comm(arrs, pick_other_half):
    def copies(srcs, dsts, send_sems, recv_sems):
        x, y, c = _coords()
        cps = []
        for a, (src, dst) in enumerate(zip(srcs, dsts)):
            if pick_other_half:
                h = src.shape[1] // 2
                src = src.at[:, pl.ds(pl.multiple_of((1 - c) * h, 16), h)]
            cps.append(_remote(src, dst, send_sems, recv_sems, a, (x, y, 1 - c)))
        return cps

    def start(*refs):
        for cp in copies(*refs):
            cp.start()

    def finish(*refs):
        for cp in copies(*refs):
            cp.wait()

    shapes = [jax.ShapeDtypeStruct((a.shape[0], a.shape[1] // 2) + a.shape[2:] if pick_other_half else a.shape, a.dtype)
              for a in arrs]
    return (list(arrs), shapes, len(arrs), start, finish)


def _rs_add_pairs(gs, others, c, name):
    n = len(gs)
    blocks = [(g.shape[1] // 4, g.shape[2]) for g in gs]

    def body(c_ref, *refs):
        for a in range(n):
            refs[2 * n + a][...] = (refs[a][...] + refs[n + a][...]).astype(BF16)

    mine = lambda b: pl.BlockSpec((None,) + b, lambda j, s, c_ref: (j, c_ref[0] * 2 + s, 0))
    flat = lambda b: pl.BlockSpec((None,) + b, lambda j, s, c_ref: (j, s, 0))
    return pl.pallas_call(
        body, name=name,
        grid_spec=pltpu.PrefetchScalarGridSpec(
            num_scalar_prefetch=1, grid=(N_CHIPS, 2),
            in_specs=[mine(b) for b in blocks] + [flat(b) for b in blocks],
            out_specs=[flat(b) for b in blocks]),
        out_shape=[jax.ShapeDtypeStruct(o.shape, BF16) for o in others],
        compiler_params=_params(2),
    )(c, *gs, *others)


def _rs_exchange_arrays(parts):
    n = len(parts)

    def body(*refs):
        _rsx_start(refs[:n], refs[n:2 * n], refs[2 * n], refs[2 * n + 1])
        _rsx_finish(refs[:n], refs[n:2 * n], refs[2 * n], refs[2 * n + 1])

    _, shapes, n_sems, _, _ = _rsx_comm(parts)
    return pl.pallas_call(
        body, name="rs_exchange_chips", out_shape=shapes, in_specs=[HBM] * n, out_specs=[HBM] * n,
        scratch_shapes=[pltpu.SemaphoreType.DMA((n_sems,)), pltpu.SemaphoreType.DMA((n_sems,))],
    )(*parts)


def _rsx_copies(srcs, dsts, send_sems, recv_sems):
    x, y, c = _coords()
    others = [(1 - x, y), (x, 1 - y), (1 - x, 1 - y)]
    return [_remote(src.at[2 * ox + oy], dst.at[k], send_sems, recv_sems, 3 * a + k, (ox, oy, c))
            for a, (src, dst) in enumerate(zip(srcs, dsts)) for k, (ox, oy) in enumerate(others)]


def _rsx_start(srcs, dsts, send_sems, recv_sems):
    for cp in _rsx_copies(srcs, dsts, send_sems, recv_sems):
        cp.start()


def _rsx_finish(srcs, dsts, send_sems, recv_sems):
    for cp in _rsx_copies(srcs, dsts, send_sems, recv_sems):
        cp.wait()


def _rsx_comm(parts):
    shapes = [jax.ShapeDtypeStruct((3,) + p.shape[1:], p.dtype) for p in parts]
    return (list(parts), shapes, 3 * len(parts), _rsx_start, _rsx_finish)


def _rs_add_totals(parts, recvs, chip, name):
    n = len(parts)
    blocks = [(p.shape[1] // 2, p.shape[2]) for p in parts]

    def body(chip_ref, *refs):
        f = lambda r: r[...].astype(F32)
        for a in range(n):
            p, r0, r1, r2 = refs[a], refs[n + 3 * a], refs[n + 3 * a + 1], refs[n + 3 * a + 2]
            refs[4 * n + a][...] = (f(p) + f(r0)) + (f(r1) + f(r2))

    own = lambda b: pl.BlockSpec((None,) + b, lambda s, chip_ref: (chip_ref[0], s, 0))
    slot = lambda b, k: pl.BlockSpec((None,) + b, lambda s, chip_ref, k=k: (k, s, 0))
    recv_specs = [slot(b, k) for b in blocks for k in range(3)]
    recv_args = [r for r in recvs for _ in range(3)]
    return pl.pallas_call(
        body, name=name,
        grid_spec=pltpu.PrefetchScalarGridSpec(
            num_scalar_prefetch=1, grid=(2,),
            in_specs=[own(b) for b in blocks] + recv_specs,
            out_specs=[pl.BlockSpec(b, lambda s, chip_ref: (s, 0)) for b in blocks]),
        out_shape=[jax.ShapeDtypeStruct(p.shape[1:], F32) for p in parts],
        compiler_params=_params(1),
    )(chip, *parts, *recv_args)


def _permute_w_in(wt):
    return jnp.concatenate([wt[:QKV_COLS], wt[QKV_COLS + LOGIT_COLS:IN_COLS], wt[QKV_COLS:QKV_COLS + LOGIT_COLS],
                            jnp.zeros((IN_COLS_PADDED - IN_COLS, wt.shape[1]), wt.dtype)], axis=0)


def _pad_row(v):
    v = v.reshape(1, -1)
    return jnp.pad(v, ((0, 0), (0, D_MODEL - v.shape[1])))


def kernel(x, norm_ffn1, ffn1_gate, ffn1_up, ffn1_down, norm_mix, w_in, conv_w, a_log, dt_bias, dn_norm, w_out, norm_ffn2, ffn2_gate, ffn2_up, ffn2_down, norm_final, loss_target, m_norm_ffn1, m_ffn1_gate, m_ffn1_up, m_ffn1_down, m_norm_mix, m_w_in, m_conv_w, m_a_log, m_dt_bias, m_dn_norm, m_w_out, m_norm_ffn2, m_ffn2_gate, m_ffn2_up, m_ffn2_down, m_norm_final, v_norm_ffn1, v_ffn1_gate, v_ffn1_up, v_ffn1_down, v_norm_mix, v_w_in, v_conv_w, v_a_log, v_dt_bias, v_dn_norm, v_w_out, v_norm_ffn2, v_ffn2_gate, v_ffn2_up, v_ffn2_down, v_norm_final):
    cx, cy, cc = _coords()
    chip = 2 * cx + cy
    stored = lambda t, n: t[0] if n in ROW_SHARDED else t[0].T
    big_w = {n: stored(t, n) for n, t in dict(
        ffn1_gate=ffn1_gate, ffn1_up=ffn1_up, ffn1_down=ffn1_down, w_in=w_in, w_out=w_out,
        ffn2_gate=ffn2_gate, ffn2_up=ffn2_up, ffn2_down=ffn2_down).items()}
    big_m = {n: stored(t, n) for n, t in dict(
        ffn1_gate=m_ffn1_gate, ffn1_up=m_ffn1_up, ffn1_down=m_ffn1_down, w_in=m_w_in, w_out=m_w_out,
        ffn2_gate=m_ffn2_gate, ffn2_up=m_ffn2_up, ffn2_down=m_ffn2_down).items()}
    big_v = {n: stored(t, n) for n, t in dict(
        ffn1_gate=v_ffn1_gate, ffn1_up=v_ffn1_up, ffn1_down=v_ffn1_down, w_in=v_w_in, w_out=v_w_out,
        ffn2_gate=v_ffn2_gate, ffn2_up=v_ffn2_up, ffn2_down=v_ffn2_down).items()}

    cols = IN_COLS // N_CHIPS
    send = {n: big_w[n].astype(BF16) for n in BIG}
    send["w_in"] = jnp.pad(send["w_in"], ((0, W_IN_ROWS - cols), (0, 0)))
    early = tuple(n for n in BIG if n not in LATE_WEIGHTS)
    wts = dict(zip(early, _allgather_arrays([send[n] for n in early])))
    dist =dict(late=[send[n] for n in LATE_WEIGHTS], c=cc.reshape(1).astype(jnp.int32),
                chip=chip.reshape(1).astype(jnp.int32))

    conv_shard = conv_w[0]
    emb = jnp.concatenate([jnp.where((chip == j) & (cc == 0), conv_shard, 0.0) for j in range(N_CHIPS)], axis=1)
    emb = jnp.pad(emb.reshape(6, D_MODEL), ((0, 2), (0, 0)))
    conv_full = _allreduce_small(emb, "allgather_conv_w")[:6].reshape(CONV_WIDTH, 3 * DN_WIDTH)

    zvec = jnp.zeros((1, 128), F32)
    small = dict(norm_ffn1=norm_ffn1, norm_mix=norm_mix, norm_ffn2=norm_ffn2, norm_final=norm_final[None],
                 conv_w=conv_full, avec=zvec.at[0, DN_HEADS:2 * DN_HEADS].set(a_log[0]),
                 dvec=zvec.at[0, DN_HEADS:2 * DN_HEADS].set(dt_bias[0]), dn_norm=dn_norm)

    loss, grad_x, reduced, sg = _local_step(x[0], loss_target[0], wts, small, dist)

    rows = [sg["norm_ffn1"], sg["norm_mix"], sg["norm_ffn2"], sg["norm_final"], _pad_row(sg["a_log"]), _pad_row(sg["dt_bias"]),
            _pad_row(sg["dn_norm"]), _pad_row(loss[0:1]), sg["conv_w"].reshape(6, D_MODEL), jnp.zeros((2, D_MODEL), F32)]
    red = _allreduce_small(jnp.concatenate(rows, axis=0), "allreduce_small")
    loss_out = red[7, 0]
    g_conv_full = red[8:14].reshape(CONV_WIDTH, 3 * DN_WIDTH)
    g_conv = lax.dynamic_slice_in_dim(g_conv_full, chip * (3 * DN_WIDTH // N_CHIPS), 3 * DN_WIDTH // N_CHIPS, axis=1)
    g_small = dict(norm_ffn1=red[0:1], norm_mix=red[1:2], norm_ffn2=red[2:3], norm_final=red[3],
                   a_log=red[4:5, DN_HEADS:2 * DN_HEADS], dt_bias=red[5:6, DN_HEADS:2 * DN_HEADS], dn_norm=red[6:7, :DN_HEAD_DIM])

    out_g, out_d, out_m, out_v = {}, {}, {}, {}
    for n in BIG:
        mine, other = reduced[n]
        g = jnp.where(cc == 0, jnp.concatenate([mine, other], axis=0), jnp.concatenate([other, mine], axis=0))
        if n == "w_in":
            to3 = lambda t: jnp.transpose(t, (2, 0, 1))
            g = g[:cols].reshape(cols, 1, D_MODEL)
            results = (g,) + tuple(_adamw(to3(w_in), g, to3(m_w_in), to3(v_w_in), "adamw_w_in"))
            out_g[n], out_d[n], out_m[n], out_v[n] = (jnp.transpose(t, (1, 2, 0)) for t in results)
            continue
        results = (g,) + tuple(_adamw(big_w[n], g, big_m[n], big_v[n], "adamw_" + n))
        out_g[n], out_d[n], out_m[n], out_v[n] = ((t if n in ROW_SHARDED else t.T)[None] for t in results)
    d, nm, nv = _adamw(conv_w[0], g_conv, m_conv_w[0], v_conv_w[0], "adamw_conv_w")
    out_g["conv_w"], out_d["conv_w"], out_m["conv_w"], out_v["conv_w"] = g_conv[None], d[None], nm[None], nv[None]

    small_names = ("norm_ffn1", "norm_mix", "norm_ffn2", "norm_final", "a_log", "dt_bias", "dn_norm")
    small_w = dict(norm_ffn1=norm_ffn1, norm_mix=norm_mix, norm_ffn2=norm_ffn2, norm_final=norm_final, a_log=a_log,
                   dt_bias=dt_bias, dn_norm=dn_norm)
    small_m = dict(norm_ffn1=m_norm_ffn1, norm_mix=m_norm_mix, norm_ffn2=m_norm_ffn2, norm_final=m_norm_final, a_log=m_a_log,
                   dt_bias=m_dt_bias, dn_norm=m_dn_norm)
    small_v = dict(norm_ffn1=v_norm_ffn1, norm_mix=v_norm_mix, norm_ffn2=v_norm_ffn2, norm_final=v_norm_final, a_log=v_a_log,
                   dt_bias=v_dt_bias, dn_norm=v_dn_norm)
    stack = lambda dct: jnp.concatenate([_pad_row(dct[n]) for n in small_names] + [jnp.zeros((1, D_MODEL), F32)], axis=0)
    d, nm, nv = _adamw(stack(small_w), stack(g_small), stack(small_m), stack(small_v), "adamw_small")
    for k, n in enumerate(small_names):
        shape = small_w[n].shape
        size = math.prod(shape)
        out_g[n] = g_small[n].reshape(shape)
        out_d[n], out_m[n], out_v[n] = (t[k, :size].reshape(shape) for t in (d, nm, nv))

    order = ("norm_ffn1", "ffn1_gate", "ffn1_up", "ffn1_down", "norm_mix", "w_in", "conv_w", "a_log", "dt_bias", "dn_norm",
             "w_out", "norm_ffn2", "ffn2_gate", "ffn2_up", "ffn2_down", "norm_final")
    return (loss_out, grad_x[None], *[out_g[n] for n in order], *[out_d[n] for n in order],
            *[out_m[n] for n in order], *[out_v[n] for n in order])
```

```python
import functools
import math

import jax
import jax.numpy as jnp
from jax import lax
from jax.experimental import pallas as pl
from jax.experimental.pallas import tpu as pltpu

F32 = jnp.float32
BF16 = jnp.bfloat16
HI = lax.Precision.HIGH

D_MODEL = 1024
ATTN_HEADS = 8
ATTN_WIDTH = 512
ATTN_BLOCK = 128
ATTN_SCALE = (ATTN_WIDTH // ATTN_HEADS) ** -0.5
DILATIONS = (1, 4, 16)
DN_HEADS = 4
DN_HEAD_DIM = 128
DN_WIDTH = 512
DN_CHUNK = 64
CONV_WIDTH = 4
NORM_EPS = 1e-6
L2_EPS = 1e-6
QKV_COLS = 3 * ATTN_WIDTH + 3 * DN_WIDTH
LOGIT_COLS = 2 * DN_HEADS
IN_COLS = QKV_COLS + LOGIT_COLS + DN_WIDTH
IN_COLS_PADDED = 3712
N_CHIPS = 4

ADAM_LR = 0.001
ADAM_B1 = 0.9
ADAM_B2 = 0.999
ADAM_EPS = 1e-08
ADAM_WD = 0.01
ADAM_STEP = 10

VMEM_LIMIT = 56 * 1024 * 1024
NEG_BIG = -1e30
MESH = pl.DeviceIdType.MESH


def _params(n_grid, vmem=VMEM_LIMIT):
    return pltpu.CompilerParams(dimension_semantics=("arbitrary",) * n_grid, vmem_limit_bytes=vmem)


def _call(body, args, *, name, grid, in_specs, out_specs, out_shape, scratch_shapes=(), comm=None):
    n_in, n_out, n_scr = len(in_specs), len(out_specs), len(scratch_shapes)
    hbm = pl.BlockSpec(memory_space=pl.ANY)
    srcs, dst_shapes, n_sems, start, finish = comm[:5] if comm is not None else ((), (), 0, None, None)
    middle = comm[5] if comm is not None and len(comm) > 5 else None
    ns, nd = len(srcs), len(dst_shapes)

    def full(*refs):
        ins, c_src = refs[:n_in], refs[n_in:n_in + ns]
        at = n_in + ns
        outs, c_dst = refs[at:at + n_out], refs[at + n_out:at + n_out + nd]
        scr = refs[at + n_out + nd:at + n_out + nd + n_scr]
        if comm is not None:
            ids = [pl.program_id(a) for a in range(len(grid))]
            first = functools.reduce(jnp.logical_and, [i == 0 for i in ids])
            last = functools.reduce(jnp.logical_and, [i == g - 1 for i, g in zip(ids, grid)])

            @pl.when(first)
            def _():
                start(c_src, c_dst, refs[-2], refs[-1])

            if middle is not None:
                relay_step = functools.reduce(jnp.logical_and, [ids[0] == (5 * grid[0]) // 8] + [i == 0 for i in ids[1:]])

                @pl.when(relay_step)
                def _():
                    middle(c_src, c_dst, refs[-2], refs[-1])

        body(*ins, *outs, *scr)
        if comm is not None:
            @pl.when(last)
            def _():
                finish(c_src, c_dst, refs[-2], refs[-1])

    sems = [pltpu.SemaphoreType.DMA((n_sems,)), pltpu.SemaphoreType.DMA((n_sems,))] if comm is not None else []
    res = pl.pallas_call(
        full, name=name, grid=grid, in_specs=list(in_specs) + [hbm] * ns, out_specs=list(out_specs) + [hbm] * nd,
        out_shape=list(out_shape) + list(dst_shapes), scratch_shapes=list(scratch_shapes) + sems,
        compiler_params=_params(len(grid)),
    )(*args, *srcs)
    return res[:n_out], res[n_out:]


def _nt(a, b, precision=None):
    return lax.dot_general(a, b, (((1,), (1,)), ((), ())), preferred_element_type=F32, precision=precision)


def _tn(a, b, precision=None):
    return lax.dot_general(a, b, (((0,), (0,)), ((), ())), preferred_element_type=F32, precision=precision)


def _nn(a, b, precision=None):
    return jnp.dot(a, b, preferred_element_type=F32, precision=precision)


def _sigmoid(x):
    return 1.0 / (1.0 + jnp.exp(-x))


def _loss_head(xf, gain, target):
    r = lax.rsqrt(jnp.mean(xf * xf, axis=-1, keepdims=True) + NORM_EPS)
    xhat = xf * r
    err = xhat * gain - target
    part = 0.5 * jnp.sum(jnp.mean(err * err, axis=-1, keepdims=True), axis=0, keepdims=True)
    dy = err * (1.0 / xf.shape[-1])
    dgain = jnp.sum(dy * xhat, axis=0, keepdims=True)
    dxh = dy * gain
    return part, r * (dxh - xhat * jnp.mean(dxh * xhat, axis=-1, keepdims=True)), dgain


def _ffn_fwd(x, gain, wg, wu, wd, name, comm=None, head=None):
    S, D = x.shape
    nf, tf, _ = wg.shape
    tm = 512
    n_in = 5 if head is None else 7

    def body(*refs):
        x_ref, gain_ref, wg_ref, wu_ref, wd_ref = refs[:5]
        xo_ref, h_ref, g_ref, u_ref = refs[n_in:n_in + 4]
        acc_ref, hs_ref = refs[-2:]
        i = pl.program_id(0)
        j = pl.program_id(1)

        @pl.when(j == 0)
        def _():
            xf = x_ref[...]
            r = lax.rsqrt(jnp.mean(xf * xf, axis=-1, keepdims=True) + NORM_EPS)
            h = (xf * r * gain_ref[...]).astype(BF16)
            hs_ref[...] = h
            h_ref[...] = h
            acc_ref[...] = jnp.zeros_like(acc_ref)

        h = hs_ref[...]
        g = _nt(h, wg_ref[...])
        u = _nt(h, wu_ref[...])
        g_ref[...] = g.astype(BF16)
        u_ref[...] = u.astype(BF16)
        act = g * _sigmoid(g) * u
        acc_ref[...] += _nn(act.astype(BF16), wd_ref[...])

        if head is not None:
            hgain_ref, t_ref = refs[5:7]
            loss_ref, dgain_ref = refs[n_in + 4:n_in + 6]

            @pl.when((i == 0) & (j == 0))
            def _():
                loss_ref[...] = jnp.zeros_like(loss_ref)
                dgain_ref[...] = jnp.zeros_like(dgain_ref)

        @pl.when(j == nf - 1)
        def _():
            xo = x_ref[...] + 0.5 * acc_ref[...]
            if head is None:
                xo_ref[...] = xo
            else:
                part, dxo, dgain = _loss_head(xo, hgain_ref[...], t_ref[...])
                first = ((lax.broadcasted_iota(jnp.int32, (8, 128), 0) == 0)
                         & (lax.broadcasted_iota(jnp.int32, (8, 128), 1) == 0))
                loss_ref[...] += jnp.where(first, part, 0.0)
                dgain_ref[...] += dgain
                xo_ref[...] = dxo

    tok = pl.BlockSpec((tm, D), lambda i, j: (i, 0))
    row = pl.BlockSpec((1, D), lambda i, j: (0, 0))
    chunk = pl.BlockSpec((None, tf, D), lambda i, j: (j, 0, 0))
    act = pl.BlockSpec((None, tm, tf), lambda i, j: (j, i, 0))
    extra_in = [] if head is None else [row, tok]
    extra_out = [] if head is None else [pl.BlockSpec((8, 128), lambda i, j: (0, 0)), row]
    extra_shape = [] if head is None else [jax.ShapeDtypeStruct((8, 128), F32), jax.ShapeDtypeStruct((1, D), F32)]
    return _call(
        body, (x, gain, wg, wu, wd) + (() if head is None else tuple(head)), name=name, grid=(S // tm, nf), comm=comm,
        in_specs=[tok, row, chunk, chunk, chunk] + extra_in,
        out_specs=[tok, tok, act, act] + extra_out,
        out_shape=[jax.ShapeDtypeStruct((S, D), F32), jax.ShapeDtypeStruct((S, D), BF16),
                   jax.ShapeDtypeStruct((nf, S, tf), BF16), jax.ShapeDtypeStruct((nf, S, tf), BF16)] + extra_shape,
        scratch_shapes=[pltpu.VMEM((tm, D), F32), pltpu.VMEM((tm, D), BF16)])


def _rmsnorm_bwd(dh, xf, gain):
    r = lax.rsqrt(jnp.mean(xf * xf, axis=-1, keepdims=True) + NORM_EPS)
    xhat = xf * r
    dgain = jnp.sum(dh * xhat, axis=0, keepdims=True)
    dxh = dh * gain
    dx = r * (dxh - xhat * jnp.mean(dxh * xhat, axis=-1, keepdims=True))
    return dx, dgain


def _ffn_bwd(dxo, x, gain, g, u, wd, wg, wu, name, comm=None):
    S, D = x.shape
    nf, _, tf = g.shape
    tm = 512

    def body(dxo_ref, x_ref, gain_ref, g_ref, u_ref, wd_ref, wg_ref, wu_ref,
             dx_ref, dgain_ref, dg_ref, du_ref, act_ref, dout_ref, acc_ref, ds_ref):
        i = pl.program_id(0)
        j = pl.program_id(1)

        @pl.when(j == 0)
        def _():
            d = (0.5 * dxo_ref[...]).astype(BF16)
            ds_ref[...] = d
            dout_ref[...] = d
            acc_ref[...] = jnp.zeros_like(acc_ref)

        @pl.when((i == 0) & (j == 0))
        def _():
            dgain_ref[...] = jnp.zeros_like(dgain_ref)

        for half in range(2):
            rows = slice(half * (tm // 2), (half + 1) * (tm // 2))
            dact = _nt(ds_ref[rows, :], wd_ref[...])
            gv = g_ref[rows, :].astype(F32)
            uv = u_ref[rows, :].astype(F32)
            sg = _sigmoid(gv)
            silu = gv * sg
            act_ref[rows, :] = (silu * uv).astype(BF16)
            dgv = (dact * uv * (sg * (1.0 + gv * (1.0 - sg)))).astype(BF16)
            duv = (dact * silu).astype(BF16)
            dg_ref[rows, :] = dgv
            du_ref[rows, :] = duv
            acc_ref[rows, :] += _nn(dgv, wg_ref[...]) + _nn(duv, wu_ref[...])

        @pl.when(j == nf - 1)
        def _():
            dx, dgain = _rmsnorm_bwd(acc_ref[...], x_ref[...], gain_ref[...])
            dx_ref[...] = dxo_ref[...] + dx
            dgain_ref[...] += dgain

    return _call(
        body, (dxo, x, gain, g, u, wd, wg, wu), name=name, grid=(S // tm, nf), comm=comm,
        in_specs=[pl.BlockSpec((tm, D), lambda i, j: (i, 0)),
                  pl.BlockSpec((tm, D), lambda i, j: (i, 0)),
                  pl.BlockSpec((1, D), lambda i, j: (0, 0)),
                  pl.BlockSpec((None, tm, tf), lambda i, j: (j, i, 0)),
                  pl.BlockSpec((None, tm, tf), lambda i, j: (j, i, 0)),
                  pl.BlockSpec((None, tf, D), lambda i, j: (j, 0, 0)),
                  pl.BlockSpec((None, tf, D), lambda i, j: (j, 0, 0)),
                  pl.BlockSpec((None, tf, D), lambda i, j: (j, 0, 0))],
        out_specs=[pl.BlockSpec((tm, D), lambda i, j: (i, 0)),
                   pl.BlockSpec((1, D), lambda i, j: (0, 0)),
                   pl.BlockSpec((None, tm, tf), lambda i, j: (j, i, 0)),
                   pl.BlockSpec((None, tm, tf), lambda i, j: (j, i, 0)),
                   pl.BlockSpec((None, tm, tf), lambda i, j: (j, i, 0)),
                   pl.BlockSpec((tm, D), lambda i, j: (i, 0))],
        out_shape=[jax.ShapeDtypeStruct((S, D), F32), jax.ShapeDtypeStruct((1, D), F32),
                   jax.ShapeDtypeStruct((nf, S, tf), BF16), jax.ShapeDtypeStruct((nf, S, tf), BF16),
                   jax.ShapeDtypeStruct((nf, S, tf), BF16), jax.ShapeDtypeStruct((S, D), BF16)],
        scratch_shapes=[pltpu.VMEM((tm, D), F32), pltpu.VMEM((tm, D), BF16)])


def _matmul_tn(a, b, tm, tk, name):
    K, M = a.shape
    N = b.shape[1]

    def body(a_ref, b_ref, o_ref):
        @pl.when(pl.program_id(1) == 0)
        def _():
            o_ref[...] = jnp.zeros_like(o_ref)

        o_ref[...] += _tn(a_ref[...], b_ref[...])

    return pl.pallas_call(
        body, name=name, grid=(M // tm, K // tk),
        in_specs=[pl.BlockSpec((tk, tm), lambda i, k: (k, i)),
                  pl.BlockSpec((tk, N), lambda i, k: (k, 0))],
        out_specs=pl.BlockSpec((tm, N), lambda i, k: (i, 0)),
        out_shape=jax.ShapeDtypeStruct((M, N), F32),
        compiler_params=_params(2),
    )(a, b)


def _dw_chunks(a, b, tk, name, comm=None):
    nf, S, tf = a.shape
    N = b.shape[1]

    def body(a_ref, b_ref, o_ref):
        @pl.when(pl.program_id(1) == 0)
        def _():
            o_ref[...] = jnp.zeros_like(o_ref)

        o_ref[...] += _tn(a_ref[...], b_ref[...])

    (out,), landed = _call(
        body, (a, b), name=name, grid=(nf, S // tk), comm=comm,
        in_specs=[pl.BlockSpec((None, tk, tf), lambda j, k: (j, k, 0)),
                  pl.BlockSpec((tk, N), lambda j, k: (k, 0))],
        out_specs=[pl.BlockSpec((None, tf, N), lambda j, k: (j, 0, 0))],
        out_shape=[jax.ShapeDtypeStruct((nf, tf, N), F32)])
    return out, landed


VIEW_TILE = 512


def _view_spec(d, tile=VIEW_TILE):
    return pl.BlockSpec((tile // d, d * ATTN_WIDTH), lambda i: (i, 0))


def _view_shape(S, d, dtype):
    return jax.ShapeDtypeStruct((S // d, d * ATTN_WIDTH), dtype)


def _tile_to_views(val, planes, out_refs):
    for g in range(4):
        planes[g] = val[:, g * 128:(g + 1) * 128]
    for d, ref in zip(DILATIONS, out_refs):
        if d == 1:
            ref[...] = val.astype(ref.dtype)
            continue
        for r in range(d):
            for g in range(4):
                ref[:, r * ATTN_WIDTH + g * 128:r * ATTN_WIDTH + (g + 1) * 128] = (
                    planes[g, pl.ds(r, planes.shape[1] // d, stride=d), :].astype(ref.dtype))


def _view_to_tile(ref, d, planes):
    if d == 1:
        return ref[...].astype(F32)
    for r in range(d):
        for g in range(4):
            planes[g, pl.ds(r, planes.shape[1] // d, stride=d), :] = (
                ref[:, r * ATTN_WIDTH + g * 128:r * ATTN_WIDTH + (g + 1) * 128].astype(F32))
    return jnp.concatenate([planes[g] for g in range(4)], axis=1)


def _inproj_fwd(x, gain, w_in_p):
    S, D = x.shape
    tm = VIEW_TILE
    W = ATTN_WIDTH

    def body(x_ref, gain_ref, w_ref, h_ref, q1, q4, q16, k1, k4, k16, v1, v4, v16, dq_ref, dk_ref, dv_ref, gate_ref, bd_ref,
             planes):
        xf = x_ref[...]
        r = lax.rsqrt(jnp.mean(xf * xf, axis=-1, keepdims=True) + NORM_EPS)
        h = (xf * r * gain_ref[...]).astype(BF16)
        h_ref[...] = h
        _tile_to_views(_nt(h, w_ref[0:W, :]) * ATTN_SCALE, planes, (q1, q4, q16))
        _tile_to_views(_nt(h, w_ref[W:2 * W, :]), planes, (k1, k4, k16))
        _tile_to_views(_nt(h, w_ref[2 * W:3 * W, :]), planes, (v1, v4, v16))
        dq_ref[...] = _nt(h, w_ref[3 * W:4 * W, :])
        dk_ref[...] = _nt(h, w_ref[4 * W:5 * W, :])
        dv_ref[...] = _nt(h, w_ref[5 * W:6 * W, :])
        gate_ref[...] = _nt(h, w_ref[6 * W:7 * W, :])
        bd_ref[...] = _nt(h, w_ref[7 * W:7 * W + 128, :])

    tok = lambda w: pl.BlockSpec((tm, w), lambda i: (i, 0))
    return pl.pallas_call(
        body, name="inproj_fwd", grid=(S // tm,),
        in_specs=[tok(D), pl.BlockSpec((1, D), lambda i: (0, 0)),
                  pl.BlockSpec((IN_COLS_PADDED, D), lambda i: (0, 0))],
        out_specs=[tok(D)] + [_view_spec(d) for d in DILATIONS] * 3 + [tok(W)] * 4 + [tok(128)],
        out_shape=[jax.ShapeDtypeStruct((S, D), BF16)] + [_view_shape(S, d, BF16) for d in DILATIONS] * 3
                  + [jax.ShapeDtypeStruct((S, W), F32)] * 4 + [jax.ShapeDtypeStruct((S, 128), F32)],
        scratch_shapes=[pltpu.VMEM((4, tm, 128), F32)],
        compiler_params=_params(1),
    )(x, gain, w_in_p)


def _inproj_bwd(dxo, x, gain, attn_grads, dsecs, dbd, w_in_p):
    S, D = x.shape
    tm = VIEW_TILE
    W = ATTN_WIDTH

    def body(dxo_ref, x_ref, gain_ref, *rest):
        views, (s3, s4, s5, s6, dbd_ref, w_ref, dx_ref, dgain_ref, dproj_ref, planes) = rest[:9], rest[9:]

        @pl.when(pl.program_id(0) == 0)
        def _():
            dgain_ref[...] = jnp.zeros_like(dgain_ref)

        secs = []
        for k in range(3):
            parts = [_view_to_tile(views[3 * k + p], d, planes) for p, d in enumerate(DILATIONS)]
            secs.append(parts[0] + parts[1] + parts[2])
        secs += [s3[...], s4[...], s5[...], s6[...]]
        dh = jnp.zeros((tm, D), F32)
        for k, s in enumerate(secs):
            d = s.astype(BF16)
            dproj_ref[:, k * W:(k + 1) * W] = d
            dh += _nn(d, w_ref[k * W:(k + 1) * W, :])
        d = dbd_ref[...].astype(BF16)
        dproj_ref[:, 7 * W:7 * W + 128] = d
        dh += _nn(d, w_ref[7 * W:7 * W + 128, :])
        dx, dgain = _rmsnorm_bwd(dh, x_ref[...], gain_ref[...])
        dx_ref[...] = dxo_ref[...] + dx
        dgain_ref[...] += dgain

    tok = lambda w: pl.BlockSpec((tm, w), lambda i: (i, 0))
    return pl.pallas_call(
        body, name="inproj_bwd", grid=(S // tm,),
        in_specs=[tok(D), tok(D), pl.BlockSpec((1, D), lambda i: (0, 0))] + [_view_spec(d, tm) for d in DILATIONS] * 3
                 + [tok(W)] * 4 + [tok(128)] + [pl.BlockSpec((IN_COLS_PADDED, D), lambda i: (0, 0))],
        out_specs=[tok(D), pl.BlockSpec((1, D), lambda i: (0, 0)), tok(IN_COLS_PADDED)],
        out_shape=[jax.ShapeDtypeStruct((S, D), F32), jax.ShapeDtypeStruct((1, D), F32),
                   jax.ShapeDtypeStruct((S, IN_COLS_PADDED), BF16)],
        scratch_shapes=[pltpu.VMEM((4, tm, 128), F32)],
        compiler_params=_params(1),
    )(dxo, x, gain, *[g for grads in attn_grads for g in grads], *dsecs, dbd, w_in_p)


def _slope(h):
    return 2.0 ** (-8.0 * (h + 1) / ATTN_HEADS)


def _head_bias(steps, d, heads=tuple(range(ATTN_HEADS))):
    stepsf = steps.astype(F32)
    return jnp.stack([stepsf * (-_slope(h) * d) for h in heads])


def _hnt(a, b):
    return lax.dot_general(a, b, (((2,), (2,)), ((0,), (0,))), preferred_element_type=F32)


def _hnn(a, b):
    return lax.dot_general(a, b, (((2,), (1,)), ((0,), (0,))), preferred_element_type=F32)


def _blocks_per_step(nb):
    return next(n for n in (8, 4, 2, 1) if nb % n == 0)


def _query_step_specs(qb):
    B = ATTN_BLOCK
    cur = pl.BlockSpec((qb * B, ATTN_WIDTH), lambda r, n: (n, r))
    prev = pl.BlockSpec((B, ATTN_WIDTH), lambda r, n: (jnp.maximum(qb * n - 1, 0), r))
    return cur, prev


def _prev_block(prev_ref, cur_ref, sub, sl):
    B = ATTN_BLOCK
    return prev_ref[:, sl] if sub == 0 else cur_ref[(sub - 1) * B:sub * B, sl]


def _head_cols(tile, lo, big):
    return [_head_col(tile, lo, big), _head_col(tile, jnp.logical_not(lo), big)]


def _attn_fwd(q, k, v, d, name):
    L = q.shape[0]
    nb = L // ATTN_BLOCK
    B = ATTN_BLOCK
    QB = _blocks_per_step(nb)

    def body(q_ref, kp_ref, kc_ref, vp_ref, vc_ref, o_ref, lse_ref):
        n = pl.program_id(1)
        qi = lax.broadcasted_iota(jnp.int32, (B, 2 * B), 0)
        kj = lax.broadcasted_iota(jnp.int32, (B, 2 * B), 1)
        steps = qi + B - kj
        band = (steps >= 0) & (steps <= B)
        lo = lax.broadcasted_iota(jnp.int32, (B, 128), 1) < 64
        bias = _head_bias(steps, d)
        for sub in range(QB):
            rows = slice(sub * B, (sub + 1) * B)
            valid = band & ((kj >= B) | (n > 0)) if sub == 0 else band
            qs, ks, vs = [], [], []
            for G in range(4):
                sl = slice(G * 128, (G + 1) * 128)
                qg = q_ref[rows, sl]
                kg = jnp.concatenate([_prev_block(kp_ref, kc_ref, sub, sl), kc_ref[rows, sl]], axis=0)
                vg = jnp.concatenate([_prev_block(vp_ref, vc_ref, sub, sl), vc_ref[rows, sl]], axis=0)
                qs += [jnp.where(lo, qg, jnp.zeros_like(qg)), jnp.where(lo, jnp.zeros_like(qg), qg)]
                ks += [kg, kg]
                vs += [vg, vg]
            s = jnp.where(valid, _hnt(jnp.stack(qs), jnp.stack(ks)) + bias, NEG_BIG)
            m = jnp.max(s, axis=-1, keepdims=True)
            p = jnp.exp(s - m)
            l = jnp.sum(p, axis=-1, keepdims=True)
            o = _hnn(p.astype(BF16), jnp.stack(vs)) / l
            lse = m + jnp.log(l)
            for G in range(4):
                sl = slice(G * 128, (G + 1) * 128)
                o_ref[rows, sl] = jnp.where(lo, o[2 * G], o[2 * G + 1])
                lse_ref[rows, sl] = jnp.where(lo, lse[2 * G], lse[2 * G + 1])

    cur, prev = _query_step_specs(QB)
    return pl.pallas_call(
        body, name=name, grid=(d, nb // QB),
        in_specs=[cur, prev, cur, prev, cur],
        out_specs=[cur, cur],
        out_shape=[jax.ShapeDtypeStruct((L, d * ATTN_WIDTH), F32)] * 2,
        compiler_params=_params(2),
    )(q, k, k, v, v)


def _attn_merge(parts):
    S = parts[0][0].shape[0]
    tm = VIEW_TILE

    def body(o1, s1, o2, s2, o3, s3, o_ref, lse1, lse4, lse16, planes):
        outs, lses = [], []
        for d, (o, s) in zip(DILATIONS, ((o1, s1), (o2, s2), (o3, s3))):
            outs.append(_view_to_tile(o, d, planes))
            lses.append(_view_to_tile(s, d, planes))
        mx = jnp.maximum(jnp.maximum(lses[0], lses[1]), lses[2])
        es = [jnp.exp(s - mx) for s in lses]
        den = es[0] + es[1] + es[2]
        o_ref[...] = (es[0] * outs[0] + es[1] * outs[1] + es[2] * outs[2]) / den
        _tile_to_views(mx + jnp.log(den), planes, (lse1, lse4, lse16))

    views = [_view_spec(d) for d in DILATIONS]
    flat = [t for p in parts for t in p]
    return pl.pallas_call(
        body, name="attn_merge", grid=(S // tm,),
        in_specs=[views[p] for p in range(3) for _ in range(2)],
        out_specs=[views[0]] + views,
        out_shape=[jax.ShapeDtypeStruct((S, ATTN_WIDTH), F32)] + [_view_shape(S, d, F32) for d in DILATIONS],
        scratch_shapes=[pltpu.VMEM((4, tm, 128), F32)],
        compiler_params=_params(1),
    )(*flat)


def _head_col(t, msk, big):
    if big:
        return jnp.max(jnp.where(msk, t, NEG_BIG), axis=-1, keepdims=True)
    return jnp.sum(jnp.where(msk, t, 0.0), axis=-1, keepdims=True) * (1.0 / 64.0)


def _attn_bwd_q(q, k, v, do, lse, dd, d, name):
    L = q.shape[0]
    nb = L // ATTN_BLOCK
    B = ATTN_BLOCK
    QB = _blocks_per_step(nb)

    def body(q_ref, kp_ref, kc_ref, vp_ref, vc_ref, do_ref, lse_ref, dd_ref, dq_ref):
        n = pl.program_id(1)
        qi = lax.broadcasted_iota(jnp.int32, (B, 2 * B), 0)
        kj = lax.broadcasted_iota(jnp.int32, (B, 2 * B), 1)
        steps = qi + B - kj
        band = (steps >= 0) & (steps <= B)
        lo = lax.broadcasted_iota(jnp.int32, (B, 128), 1) < 64
        bias = _head_bias(steps, d)
        for sub in range(QB):
            rows = slice(sub * B, (sub + 1) * B)
            valid = band & ((kj >= B) | (n > 0)) if sub == 0 else band
            qs, ks, vs, dos, lses, dcols = [], [], [], [], [], []
            for G in range(4):
                sl = slice(G * 128, (G + 1) * 128)
                qg = q_ref[rows, sl]
                kg = jnp.concatenate([_prev_block(kp_ref, kc_ref, sub, sl), kc_ref[rows, sl]], axis=0)
                vg = jnp.concatenate([_prev_block(vp_ref, vc_ref, sub, sl), vc_ref[rows, sl]], axis=0)
                dog = do_ref[rows, sl]
                qs += [jnp.where(lo, qg, jnp.zeros_like(qg)), jnp.where(lo, jnp.zeros_like(qg), qg)]
                dos += [jnp.where(lo, dog, 0.0).astype(BF16), jnp.where(lo, 0.0, dog).astype(BF16)]
                ks += [kg, kg]
                vs += [vg, vg]
                lses += _head_cols(lse_ref[rows, sl], lo, True)
                dcols += _head_cols(dd_ref[rows, sl], lo, False)
            kb = jnp.stack(ks)
            s = _hnt(jnp.stack(qs), kb) + bias
            p = jnp.where(valid, jnp.exp(jnp.where(valid, s, NEG_BIG) - jnp.stack(lses)), 0.0)
            dp = _hnt(jnp.stack(dos), jnp.stack(vs))
            ds = p * (dp - jnp.stack(dcols))
            dq = _hnn(ds.astype(BF16), kb) * ATTN_SCALE
            for G in range(4):
                dq_ref[rows, G * 128:(G + 1) * 128] = jnp.where(lo, dq[2 * G], dq[2 * G + 1]).astype(BF16)

    cur, prev = _query_step_specs(QB)
    return pl.pallas_call(
        body, name=name, grid=(d, nb // QB), in_specs=[cur, prev, cur, prev, cur, cur, cur, cur], out_specs=cur,
        out_shape=jax.ShapeDtypeStruct((L, d * ATTN_WIDTH), BF16), compiler_params=_params(2),
    )(q, k, k, v, v, do, lse, dd)


def _attn_bwd_kv(q, k, v, do, lse, dd, d, name):
    L = q.shape[0]
    nb = L // ATTN_BLOCK
    B = ATTN_BLOCK
    KB = _blocks_per_step(nb)
    n_steps = nb // KB

    def body(k_ref, v_ref, qc_ref, qn_ref, doc_ref, don_ref, lsec_ref, lsen_ref, ddc_ref, ddn_ref, dk_ref, dv_ref):
        j = pl.program_id(1)
        qrow = lax.broadcasted_iota(jnp.int32, (2 * B, B), 0)
        kk = lax.broadcasted_iota(jnp.int32, (2 * B, B), 1)
        steps = qrow - kk
        band = (steps >= 0) & (steps <= B)
        lo2 = lax.broadcasted_iota(jnp.int32, (2 * B, 128), 1) < 64
        lo = lax.broadcasted_iota(jnp.int32, (B, 128), 1) < 64
        stepsf = steps.astype(F32)
        for sub in range(KB):
            rows = slice(sub * B, (sub + 1) * B)
            last = sub == KB - 1
            valid = band & ((qrow < B) | (j < n_steps - 1)) if last else band
            after = lambda cur_ref, nxt_ref, sl: nxt_ref[:, sl] if last else cur_ref[(sub + 1) * B:(sub + 2) * B, sl]
            for G in range(4):
                sl = slice(G * 128, (G + 1) * 128)
                kg = k_ref[rows, sl]
                vg = v_ref[rows, sl]
                qq = jnp.concatenate([qc_ref[rows, sl], after(qc_ref, qn_ref, sl)], axis=0)
                doo = jnp.concatenate([doc_ref[rows, sl], after(doc_ref, don_ref, sl)], axis=0)
                lse2 = jnp.concatenate([lsec_ref[rows, sl], after(lsec_ref, lsen_ref, sl)], axis=0)
                dd2 = jnp.concatenate([ddc_ref[rows, sl], after(ddc_ref, ddn_ref, sl)], axis=0)
                doo_b = doo.astype(BF16)
                dks, dvs = [], []
                for half in (0, 1):
                    msk = lo2 if half == 0 else jnp.logical_not(lo2)
                    qm = jnp.where(msk, qq, jnp.zeros_like(qq))
                    s = _nt(qm, kg) - (_slope(2 * G + half) * d) * stepsf
                    lse_c = _head_col(lse2, msk, True)
                    p = jnp.where(valid, jnp.exp(jnp.where(valid, s, NEG_BIG) - lse_c), 0.0)
                    dvs.append(_tn(p.astype(BF16), doo_b))
                    dom = jnp.where(msk, doo, 0.0).astype(BF16)
                    dp = _nt(dom, vg)
                    dcol = _head_col(dd2, msk, False)
                    ds = p * (dp - dcol)
                    dks.append(_tn(ds.astype(BF16), qq))
                dk_ref[rows, sl] = jnp.where(lo, dks[0], dks[1]).astype(BF16)
                dv_ref[rows, sl] = jnp.where(lo, dvs[0], dvs[1]).astype(BF16)

    cur = pl.BlockSpec((KB * B, ATTN_WIDTH), lambda r, j: (j, r))
    nxt = pl.BlockSpec((B, ATTN_WIDTH), lambda r, j: (jnp.minimum(KB * (j + 1), nb - 1), r))
    return pl.pallas_call(
        body, name=name, grid=(d, n_steps), in_specs=[cur, cur, cur, nxt, cur, nxt, cur, nxt, cur, nxt],
        out_specs=[cur, cur],
        out_shape=[jax.ShapeDtypeStruct((L, d * ATTN_WIDTH), BF16)] * 2, compiler_params=_params(2),
    )(k, v, q, q, do, do, lse, lse, dd, dd)


CONV_T = 512
HALO = 8


def _per_head(head, refs):
    for h in range(DN_HEADS):
        lanes = pl.ds(h * DN_HEAD_DIM, DN_HEAD_DIM)
        head(*[r.at[:, lanes] for r in refs[:-1]], refs[-1])


def _conv_taps(pad_ref, w, T):
    acc = pad_ref[pl.ds(HALO - 3, T), :] * w[0:1, :]
    for j in range(1, CONV_WIDTH):
        acc = acc + pad_ref[pl.ds(HALO - 3 + j, T), :] * w[j:j + 1, :]
    return acc


def _conv_fwd(xq, xk, xv, conv_w):
    S = xq.shape[0]
    T = CONV_T

    def body(*refs):
        _per_head(head, refs)

    def head(xq_ref, xqh_ref, xk_ref, xkh_ref, xv_ref, xvh_ref, wq_ref, wk_ref, wv_ref,
             qn_ref, kn_ref, v_ref, pad_ref):
        i = pl.program_id(0)

        def act(x_ref, xh_ref, w_ref):
            pad_ref[pl.ds(0, HALO), :] = jnp.where(i > 0, xh_ref[...], 0.0)
            pad_ref[pl.ds(HALO, T), :] = x_ref[...]
            c = _conv_taps(pad_ref, w_ref[...], T)
            return c * _sigmoid(c)

        def l2n(t):
            return t * lax.rsqrt(jnp.sum(t * t, axis=-1, keepdims=True) + L2_EPS)

        qn_ref[...] = l2n(act(xq_ref, xqh_ref, wq_ref))
        kn_ref[...] = l2n(act(xk_ref, xkh_ref, wk_ref))
        v_ref[...] = act(xv_ref, xvh_ref, wv_ref)

    tile = pl.BlockSpec((T, DN_WIDTH), lambda i: (i, 0))
    halo = pl.BlockSpec((HALO, DN_WIDTH), lambda i: (jnp.maximum(i * (T // HALO) - 1, 0), 0))
    wspec = lambda sec: pl.BlockSpec((CONV_WIDTH, DN_WIDTH), lambda i, sec=sec: (0, sec))
    return pl.pallas_call(
        body, name="dn_conv_fwd", grid=(S // T,),
        in_specs=[tile, halo, tile, halo, tile, halo, wspec(0), wspec(1), wspec(2)],
        out_specs=[tile, tile, tile],
        out_shape=[jax.ShapeDtypeStruct((S, DN_WIDTH), F32)] * 3,
        scratch_shapes=[pltpu.VMEM((T + HALO, 128), F32)],
        compiler_params=_params(1),
    )(xq, xq, xk, xk, xv, xv, conv_w, conv_w, conv_w)


def _conv_bwd_pre(xq, xk, xv, conv_w, dqn, dkn, dv):
    S = xq.shape[0]
    T = CONV_T

    def body(*refs):
        _per_head(head, refs)

    def head(xq_ref, xqh_ref, xk_ref, xkh_ref, xv_ref, xvh_ref, wq_ref, wk_ref, wv_ref,
             dqn_ref, dkn_ref, dv_ref, dcq_ref, dck_ref, dcv_ref, dwq_ref, dwk_ref, dwv_ref, pad_ref):
        i = pl.program_id(0)

        def one(x_ref, xh_ref, w_ref, dy_ref, dc_ref, dw_ref, normed):
            pad_ref[pl.ds(0, HALO), :] = jnp.where(i > 0, xh_ref[...], 0.0)
            pad_ref[pl.ds(HALO, T), :] = x_ref[...]
            c = _conv_taps(pad_ref, w_ref[...], T)
            sg = _sigmoid(c)
            a = c * sg
            dy = dy_ref[...]
            if normed:
                r = lax.rsqrt(jnp.sum(a * a, axis=-1, keepdims=True) + L2_EPS)
                y = a * r
                da = r * (dy - y * jnp.sum(dy * y, axis=-1, keepdims=True))
            else:
                da = dy
            dc = da * (sg * (1.0 + c * (1.0 - sg)))
            dc_ref[...] = dc

            @pl.when(i == 0)
            def _():
                dw_ref[...] = jnp.zeros_like(dw_ref)

            rows = [jnp.sum(dc * pad_ref[pl.ds(HALO - 3 + j, T), :], axis=0, keepdims=True) for j in range(CONV_WIDTH)]
            dw_ref[...] += jnp.concatenate(rows + [jnp.zeros((8 - CONV_WIDTH, 128), F32)], axis=0)

        one(xq_ref, xqh_ref, wq_ref, dqn_ref, dcq_ref, dwq_ref, True)
        one(xk_ref, xkh_ref, wk_ref, dkn_ref, dck_ref, dwk_ref, True)
        one(xv_ref, xvh_ref, wv_ref, dv_ref, dcv_ref, dwv_ref, False)

    tile = pl.BlockSpec((T, DN_WIDTH), lambda i: (i, 0))
    halo = pl.BlockSpec((HALO, DN_WIDTH), lambda i: (jnp.maximum(i * (T // HALO) - 1, 0), 0))
    wspec = lambda sec: pl.BlockSpec((CONV_WIDTH, DN_WIDTH), lambda i, sec=sec: (0, sec))
    dwspec = pl.BlockSpec((8, DN_WIDTH), lambda i: (0, 0))
    return pl.pallas_call(
        body, name="dn_conv_bwd_pre", grid=(S // T,),
        in_specs=[tile, halo, tile, halo, tile, halo, wspec(0), wspec(1), wspec(2), tile, tile, tile],
        out_specs=[tile, tile, tile, dwspec, dwspec, dwspec],
        out_shape=[jax.ShapeDtypeStruct((S, DN_WIDTH), F32)] * 3 + [jax.ShapeDtypeStruct((8, DN_WIDTH), F32)] * 3,
        scratch_shapes=[pltpu.VMEM((T + HALO, 128), F32)],
        compiler_params=_params(1),
    )(xq, xq, xk, xk, xv, xv, conv_w, conv_w, conv_w, dqn, dkn, dv)


def _conv_bwd_x(dcq, dck, dcv, conv_w):
    S = dcq.shape[0]
    T = CONV_T
    nt = S // T

    def body(*refs):
        _per_head(head, refs)

    def head(dq_ref, dqh_ref, dk_ref, dkh_ref, dv_ref, dvh_ref, wq_ref, wk_ref, wv_ref,
             oq_ref, ok_ref, ov_ref, pad_ref):
        i = pl.program_id(0)

        def one(d_ref, dh_ref, w_ref, o_ref):
            pad_ref[pl.ds(0, T), :] = d_ref[...]
            pad_ref[pl.ds(T, HALO), :] = jnp.where(i < nt - 1, dh_ref[...], 0.0)
            w = w_ref[...]
            acc = pad_ref[pl.ds(3, T), :] * w[0:1, :]
            for j in range(1, CONV_WIDTH):
                acc = acc + pad_ref[pl.ds(3 - j, T), :] * w[j:j + 1, :]
            o_ref[...] = acc

        one(dq_ref, dqh_ref, wq_ref, oq_ref)
        one(dk_ref, dkh_ref, wk_ref, ok_ref)
        one(dv_ref, dvh_ref, wv_ref, ov_ref)

    tile = pl.BlockSpec((T, DN_WIDTH), lambda i: (i, 0))
    halo = pl.BlockSpec((HALO, DN_WIDTH), lambda i: (jnp.minimum((i + 1) * (T // HALO), S // HALO - 1), 0))
    wspec = lambda sec: pl.BlockSpec((CONV_WIDTH, DN_WIDTH), lambda i, sec=sec: (0, sec))
    return pl.pallas_call(
        body, name="dn_conv_bwd_x", grid=(nt,),
        in_specs=[tile, halo, tile, halo, tile, halo, wspec(0), wspec(1), wspec(2)],
        out_specs=[tile, tile, tile],
        out_shape=[jax.ShapeDtypeStruct((S, DN_WIDTH), F32)] * 3,
        scratch_shapes=[pltpu.VMEM((T + HALO, 128), F32)],
        compiler_params=_params(1),
    )(dcq, dcq, dck, dck, dcv, dcv, conv_w, conv_w, conv_w)


PREP_CHUNKS = 4
SCAN_CHUNKS = 8


def _bnn(a, b):
    return lax.dot_general(a, b, (((2,), (1,)), ((0,), (0,))), preferred_element_type=F32, precision=HI)


def _bnt(a, b):
    return lax.dot_general(a, b, (((2,), (2,)), ((0,), (0,))), preferred_element_type=F32, precision=HI)


def _btn(a, b):
    return lax.dot_general(a, b, (((1,), (1,)), ((0,), (0,))), preferred_element_type=F32, precision=HI)


def _tri_inverse_b(a, blk, eye):
    dg = jnp.where(blk, a, 0.0)
    lo = a - dg
    d2 = _bnn(dg, dg)
    d4 = _bnn(d2, d2)
    d8 = _bnn(d4, d4)
    td = _bnn(_bnn(_bnn(eye - dg, eye + d2), eye + d4), eye + d8)
    b = _bnn(td, lo)
    b2 = _bnn(b, b)
    return _bnn(_bnn(eye - b, eye + b2), td)


def _dn_common_b(bds, avec, dvec, q_raw, k, v, saved=None):
    C = DN_CHUNK
    lane = lax.broadcasted_iota(jnp.int32, (C, 128), 1)
    row = lax.broadcasted_iota(jnp.int32, (1, C, C), 1)
    col = lax.broadcasted_iota(jnp.int32, (1, C, C), 2)
    incl = row >= col
    strict = row > col
    eye = (row == col).astype(F32)
    blk = (row // 16) == (col // 16)
    pick = lambda tile, ln: jnp.sum(jnp.where(lane == ln, tile, 0.0), axis=-1, keepdims=True)
    betas, graws, zcs = [], [], []
    for bd in bds:
        z = bd + dvec
        g_all = -jnp.exp(avec) * (jnp.maximum(z, 0.0) + jnp.log(1.0 + jnp.exp(-jnp.abs(z))))
        beta_all = _sigmoid(bd)
        for h in range(DN_HEADS):
            betas.append(pick(beta_all, h))
            graws.append(pick(g_all, DN_HEADS + h))
            zcs.append(pick(z, DN_HEADS + h))
    beta, graw, zc = jnp.stack(betas), jnp.stack(graws), jnp.stack(zcs)
    to_row = lambda c: jnp.sum(eye * c, axis=1, keepdims=True)
    gc = jnp.sum(jnp.where(incl, to_row(graw), 0.0), axis=-1, keepdims=True)
    decay = jnp.exp(jnp.where(incl, gc - to_row(gc), NEG_BIG))
    q = q_raw * (DN_HEAD_DIM ** -0.5)
    kb = k * beta
    kk = _bnt(kb, k)
    eg = jnp.exp(gc)
    rhs_w = kb * eg
    if saved is None:
        t = _tri_inverse_b(jnp.where(strict, kk * decay, 0.0), blk, eye)
        u = _bnn(t, v * beta)
        w = _bnn(t, rhs_w)
        aq = jnp.where(incl, _bnt(q, k) * decay, 0.0)
    else:
        t, u, w, aq = saved
    last = lax.broadcasted_iota(jnp.int32, (1, C, 1), 1) == C - 1
    g_last = jnp.sum(jnp.where(last, gc, 0.0), axis=1, keepdims=True)
    ekd = jnp.exp(g_last - gc)
    return dict(beta=beta, graw=graw, zc=zc, gc=gc, decay=decay, q=q, kb=kb, kk=kk, t=t, eg=eg, rhs_w=rhs_w,
                u=u, w=w, aq=aq, g_last=g_last, ekd=ekd, kd=k * ekd, qg=q * eg,
                incl=incl, strict=strict, eye=eye, lane=lane, row=row, col=col, last=last)


def _stack_heads(ref, rows):
    return jnp.stack([ref[rows, h * DN_HEAD_DIM:(h + 1) * DN_HEAD_DIM] for h in range(DN_HEADS)])


def _stack_units(ref, nc):
    C = DN_CHUNK
    return jnp.concatenate([_stack_heads(ref, slice(ci * C, (ci + 1) * C)) for ci in range(nc)], axis=0)


def _store_units(ref, val, nc):
    C = DN_CHUNK
    for ci in range(nc):
        for h in range(DN_HEADS):
            ref[ci * C:(ci + 1) * C, h * DN_HEAD_DIM:(h + 1) * DN_HEAD_DIM] = val[ci * DN_HEADS + h]


def _dn_prep(qn, kn, v, bd, avec, dvec):
    S = qn.shape[0]
    C = DN_CHUNK
    N = S // C
    nc = PREP_CHUNKS

    def body(q_ref, k_ref, v_ref, bd_ref, a_ref, d_ref, u_ref, w_ref, qg_ref, kd_ref, aq_ref, t_ref, egl_ref):
        bds = [bd_ref[ci * C:(ci + 1) * C, :] for ci in range(nc)]
        c = _dn_common_b(bds, a_ref[...], d_ref[...], _stack_units(q_ref, nc), _stack_units(k_ref, nc), _stack_units(v_ref, nc))
        _store_units(u_ref, c["u"], nc)
        _store_units(w_ref, c["w"], nc)
        _store_units(qg_ref, c["qg"], nc)
        _store_units(kd_ref, c["kd"], nc)
        egl = jnp.broadcast_to(jnp.exp(c["g_last"]), (nc * DN_HEADS, 1, 128))
        for ci in range(nc):
            for h in range(DN_HEADS):
                aq_ref[h, ci * C:(ci + 1) * C, :] = c["aq"][ci * DN_HEADS + h]
                t_ref[h, ci * C:(ci + 1) * C, :] = c["t"][ci * DN_HEADS + h]
            egl_ref[ci * 8:(ci + 1) * 8, :] = jnp.concatenate(
                [egl[ci * DN_HEADS + h] for h in range(DN_HEADS)] + [jnp.zeros((8 - DN_HEADS, 128), F32)], axis=0)

    tok = lambda w: pl.BlockSpec((nc * C, w), lambda n: (n, 0))
    sq = pl.BlockSpec((DN_HEADS, nc * C, C), lambda n: (0, n, 0))
    vec = pl.BlockSpec((1, 128), lambda n: (0, 0))
    return pl.pallas_call(
        body, name="dn_prep", grid=(N // nc,),
        in_specs=[tok(DN_WIDTH)] * 3 + [tok(128), vec, vec],
        out_specs=[tok(DN_WIDTH)] * 4 + [sq, sq, pl.BlockSpec((nc * 8, 128), lambda n: (n, 0))],
        out_shape=[jax.ShapeDtypeStruct((S, DN_WIDTH), F32)] * 4 + [jax.ShapeDtypeStruct((DN_HEADS, S, C), F32)] * 2
                  + [jax.ShapeDtypeStruct((N * 8, 128), F32)],
        compiler_params=_params(1),
    )(qn, kn, v, bd, avec, dvec)


def _dn_scan_fwd(u, w, qg, kd, aq, egl, gate, dn_gain):
    S = u.shape[0]
    C = DN_CHUNK
    N = S // C
    HD = DN_HEAD_DIM
    nc = SCAN_CHUNKS

    def body(u_ref, w_ref, qg_ref, kd_ref, aq_ref, egl_ref, gate_ref, gain_ref, dn_ref, o_ref, vn_ref, st_ref, state_ref):
        @pl.when(pl.program_id(0) == 0)
        def _():
            state_ref[...] = jnp.zeros_like(state_ref)

        gain = gain_ref[...]
        for ci in range(nc):
            rows = slice(ci * C, (ci + 1) * C)
            st = state_ref[...]
            for h in range(DN_HEADS):
                st_ref[ci * DN_WIDTH + h * HD:ci * DN_WIDTH + (h + 1) * HD, :] = st[h]
            v_new = _stack_heads(u_ref, rows) - _bnn(_stack_heads(w_ref, rows), st)
            o = _bnn(_stack_heads(qg_ref, rows), st) + _bnn(aq_ref[:, rows, :], v_new)
            egl = jnp.stack([egl_ref[ci * 8 + h:ci * 8 + h + 1, :] for h in range(DN_HEADS)])
            state_ref[...] = st * egl + _btn(_stack_heads(kd_ref, rows), v_new)
            r = lax.rsqrt(jnp.mean(o * o, axis=-1, keepdims=True) + NORM_EPS)
            gt = _stack_heads(gate_ref, rows)
            dn = o * r * gain * (gt * _sigmoid(gt))
            for h in range(DN_HEADS):
                sl = slice(h * HD, (h + 1) * HD)
                vn_ref[rows, sl] = v_new[h]
                o_ref[rows, sl] = o[h]
                dn_ref[rows, sl] = dn[h]

    tok = lambda wd: pl.BlockSpec((nc * C, wd), lambda n: (n, 0))
    sq = pl.BlockSpec((DN_HEADS, nc * C, C), lambda n: (0, n, 0))
    vec = pl.BlockSpec((1, 128), lambda n: (0, 0))
    return pl.pallas_call(
        body, name="dn_scan_fwd", grid=(N // nc,),
        in_specs=[tok(DN_WIDTH)] * 4 + [sq, pl.BlockSpec((nc * 8, 128), lambda n: (n, 0)), tok(DN_WIDTH), vec],
        out_specs=[tok(DN_WIDTH)] * 3 + [pl.BlockSpec((nc * DN_WIDTH, HD), lambda n: (n, 0))],
        out_shape=[jax.ShapeDtypeStruct((S, DN_WIDTH), F32)] * 3 + [jax.ShapeDtypeStruct((N * DN_WIDTH, HD), F32)],
        scratch_shapes=[pltpu.VMEM((DN_HEADS, HD, HD), F32)],
        compiler_params=_params(1),
    )(u, w, qg, kd, aq, egl, gate, dn_gain)


def _dn_scan_bwd(w, qg, kd, aq, egl, gate, dn_gain, o, ddn):
    S = w.shape[0]
    C = DN_CHUNK
    N = S // C
    HD = DN_HEAD_DIM
    nc = SCAN_CHUNKS

    def body(w_ref, qg_ref, kd_ref, aq_ref, egl_ref, gate_ref, gain_ref, o_ref, ddn_ref,
             do_ref, dvn_ref, dgate_ref, dst_ref, small_ref, dstate_ref):
        @pl.when(pl.program_id(0) == 0)
        def _():
            dstate_ref[...] = jnp.zeros_like(dstate_ref)
            small_ref[...] = jnp.zeros_like(small_ref)

        gain = gain_ref[...]
        d_gain = jnp.zeros((1, 128), F32)
        for ci in reversed(range(nc)):
            rows = slice(ci * C, (ci + 1) * C)
            dsn = dstate_ref[...]
            for h in range(DN_HEADS):
                dst_ref[ci * DN_WIDTH + h * HD:ci * DN_WIDTH + (h + 1) * HD, :] = dsn[h]
            ov = _stack_heads(o_ref, rows)
            r = lax.rsqrt(jnp.mean(ov * ov, axis=-1, keepdims=True) + NORM_EPS)
            on = ov * r
            gt = _stack_heads(gate_ref, rows)
            sgt = _sigmoid(gt)
            silu_g = gt * sgt
            dy = _stack_heads(ddn_ref, rows)
            d_gain = d_gain + jnp.sum(jnp.sum(dy * on * silu_g, axis=1, keepdims=True), axis=0)
            dgate = dy * on * gain * (sgt * (1.0 + gt * (1.0 - sgt)))
            don = dy * gain * silu_g
            do = r * (don - on * jnp.mean(don * on, axis=-1, keepdims=True))
            d_vnew = _btn(aq_ref[:, rows, :], do) + _bnn(_stack_heads(kd_ref, rows), dsn)
            egl = jnp.stack([egl_ref[ci * 8 + h:ci * 8 + h + 1, :] for h in range(DN_HEADS)])
            dstate_ref[...] = _btn(_stack_heads(qg_ref, rows), do) + dsn * egl - _btn(_stack_heads(w_ref, rows), d_vnew)
            for h in range(DN_HEADS):
                sl = slice(h * HD, (h + 1) * HD)
                do_ref[rows, sl] = do[h]
                dvn_ref[rows, sl] = d_vnew[h]
                dgate_ref[rows, sl] = dgate[h]
        small_ref[...] += jnp.concatenate([d_gain, jnp.zeros((7, 128), F32)], axis=0)

    nb = N // nc
    tok = lambda wd: pl.BlockSpec((nc * C, wd), lambda i: (nb - 1 - i, 0))
    sq = pl.BlockSpec((DN_HEADS, nc * C, C), lambda i: (0, nb - 1 - i, 0))
    vec = pl.BlockSpec((1, 128), lambda i: (0, 0))
    return pl.pallas_call(
        body, name="dn_scan_bwd", grid=(nb,),
        in_specs=[tok(DN_WIDTH)] * 3 + [sq, pl.BlockSpec((nc * 8, 128), lambda i: (nb - 1 - i, 0)), tok(DN_WIDTH), vec,
                                       tok(DN_WIDTH), tok(DN_WIDTH)],
        out_specs=[tok(DN_WIDTH)] * 3 + [pl.BlockSpec((nc * DN_WIDTH, HD), lambda i: (nb - 1 - i, 0)),
                                        pl.BlockSpec((8, 128), lambda i: (0, 0))],
        out_shape=[jax.ShapeDtypeStruct((S, DN_WIDTH), F32)] * 3 + [jax.ShapeDtypeStruct((N * DN_WIDTH, HD), F32),
                                                                  jax.ShapeDtypeStruct((8, 128), F32)],
        scratch_shapes=[pltpu.VMEM((DN_HEADS, HD, HD), F32)],
        compiler_params=_params(1),
    )(w, qg, kd, aq, egl, gate, dn_gain, o, ddn)


def _dn_post(qn, kn, v, bd, avec, dvec, t_inv, u_all, w_all, aq_all, v_new_all, states, dstates, do_all, dvn_all, comm=None):
    S = qn.shape[0]
    C = DN_CHUNK
    N = S // C
    HD = DN_HEAD_DIM
    nc = PREP_CHUNKS
    B = nc * DN_HEADS

    def body(q_ref, k_ref, v_ref, bd_ref, a_ref, d_ref, t_ref, u_ref, w_ref, aq_ref, vn_ref, st_ref, dst_ref, do_ref, dvn_ref,
             dq_ref, dk_ref, dv_ref, dbd_ref, small_ref):
        @pl.when(pl.program_id(0) == 0)
        def _():
            small_ref[...] = jnp.zeros_like(small_ref)

        avec = a_ref[...]
        bds = [bd_ref[ci * C:(ci + 1) * C, :] for ci in range(nc)]
        k = _stack_units(k_ref, nc)
        vv = _stack_units(v_ref, nc)
        squares = lambda ref: jnp.concatenate([ref[:, ci * C:(ci + 1) * C, :] for ci in range(nc)], axis=0)
        saved = (squares(t_ref), _stack_units(u_ref, nc), _stack_units(w_ref, nc), squares(aq_ref))
        c = _dn_common_b(bds, avec, d_ref[...], _stack_units(q_ref, nc), k, vv, saved=saved)
        q, kb, eg, t, u, w = c["q"], c["kb"], c["eg"], c["t"], c["u"], c["w"]
        beta, decay, incl, strict, eye = c["beta"], c["decay"], c["incl"], c["strict"], c["eye"]
        st = jnp.stack([st_ref[b * HD:(b + 1) * HD, :] for b in range(B)])
        dsn = jnp.stack([dst_ref[b * HD:(b + 1) * HD, :] for b in range(B)])
        v_new = _stack_units(vn_ref, nc)
        do = _stack_units(do_ref, nc)
        d_vnew = _stack_units(dvn_ref, nc)
        egl = jnp.exp(c["g_last"])
        daq = jnp.where(incl, _bnt(do, v_new), 0.0)
        d_qg = _bnt(do, st)
        d_kd = _bnt(v_new, dsn)
        d_glast = jnp.sum(jnp.sum(dsn * st, axis=-1, keepdims=True), axis=1, keepdims=True) * egl
        d_w = -_bnt(d_vnew, st)
        d_ru = _btn(t, d_vnew)
        d_rw = _btn(t, d_w)
        da = -jnp.where(strict, _bnt(d_ru, u) + _bnt(d_rw, w), 0.0)
        dv = d_ru * beta
        dbeta = jnp.sum(d_ru * vv, axis=-1, keepdims=True)
        dkb = d_rw * eg
        dgc = jnp.sum(d_rw * c["rhs_w"], axis=-1, keepdims=True)
        dkk = da * decay
        ddecay = da * c["kk"]
        dkb = dkb + _bnn(dkk, k)
        dk = _btn(dkk, kb)
        dqk = daq * decay
        dq = _bnn(dqk, k)
        dk = dk + _btn(dqk, q)
        m = ddecay * decay + daq * c["aq"]
        col_sum = jnp.sum(m, axis=1, keepdims=True)
        dgc = dgc + jnp.sum(m, axis=-1, keepdims=True) - jnp.sum(eye * col_sum, axis=-1, keepdims=True)
        dq = dq + d_qg * eg
        dgc = dgc + jnp.sum(d_qg * c["qg"], axis=-1, keepdims=True)
        dk = dk + d_kd * c["ekd"]
        tk = jnp.sum(d_kd * c["kd"], axis=-1, keepdims=True)
        dgc = dgc - tk
        d_glast = d_glast + jnp.sum(tk, axis=1, keepdims=True)
        dk = dk + dkb * beta
        dbeta = dbeta + jnp.sum(dkb * k, axis=-1, keepdims=True)
        dgc = dgc + jnp.where(c["last"], d_glast, 0.0)
        dgc_row = jnp.sum(eye * dgc, axis=1, keepdims=True)
        dgraw = jnp.sum(jnp.where(c["col"] >= c["row"], dgc_row, 0.0), axis=-1, keepdims=True)
        _store_units(dq_ref, dq * (HD ** -0.5), nc)
        _store_units(dk_ref, dk, nc)
        _store_units(dv_ref, dv, nc)
        dbraw = dbeta * beta * (1.0 - beta)
        dzc = dgraw * _sigmoid(c["zc"])
        ga = dgraw * c["graw"]
        lane = c["lane"]
        lane1 = lax.broadcasted_iota(jnp.int32, (1, 128), 1)
        neg_ea = -jnp.exp(avec)
        d_alog = jnp.zeros((1, 128), F32)
        d_dt = jnp.zeros((1, 128), F32)
        for ci in range(nc):
            dbd = jnp.zeros((C, 128), F32)
            for h in range(DN_HEADS):
                b = ci * DN_HEADS + h
                dz = dzc[b] * neg_ea
                dbd = dbd + jnp.where(lane == h, dbraw[b], 0.0) + jnp.where(lane == DN_HEADS + h, dz, 0.0)
                d_alog = d_alog + jnp.where(lane1 == DN_HEADS + h, jnp.sum(ga[b], axis=0, keepdims=True), 0.0)
                d_dt = d_dt + jnp.where(lane1 == DN_HEADS + h, jnp.sum(dz, axis=0, keepdims=True), 0.0)
            dbd_ref[ci * C:(ci + 1) * C, :] = dbd
        small_ref[...] += jnp.concatenate([d_alog, d_dt, jnp.zeros((6, 128), F32)], axis=0)

    tok = lambda wd: pl.BlockSpec((nc * C, wd), lambda n: (n, 0))
    big = pl.BlockSpec((nc * DN_WIDTH, HD), lambda n: (n, 0))
    sq = pl.BlockSpec((DN_HEADS, nc * C, C), lambda n: (0, n, 0))
    vec = pl.BlockSpec((1, 128), lambda n: (0, 0))
    return _call(
        body, (qn, kn, v, bd, avec, dvec, t_inv, u_all, w_all, aq_all, v_new_all, states, dstates, do_all, dvn_all),
        name="dn_post", grid=(N // nc,), comm=comm,
        in_specs=[tok(DN_WIDTH)] * 3 + [tok(128), vec, vec, sq, tok(DN_WIDTH), tok(DN_WIDTH), sq, tok(DN_WIDTH), big, big,
                  tok(DN_WIDTH), tok(DN_WIDTH)],
        out_specs=[tok(DN_WIDTH)] * 3 + [tok(128), pl.BlockSpec((8, 128), lambda n: (0, 0))],
        out_shape=[jax.ShapeDtypeStruct((S, DN_WIDTH), F32)] * 3 + [jax.ShapeDtypeStruct((S, 128), F32),
                                                                  jax.ShapeDtypeStruct((8, 128), F32)])


def _outproj_fwd(x, attn, dn, w_out):
    S, D = x.shape
    tm = 512

    def body(x_ref, a_ref, d_ref, w_ref, xo_ref, mix_ref):
        a = a_ref[...].astype(BF16)
        dd = d_ref[...].astype(BF16)
        mix_ref[:, 0:ATTN_WIDTH] = a
        mix_ref[:, ATTN_WIDTH:] = dd
        xo_ref[...] = x_ref[...] + _nn(a, w_ref[0:ATTN_WIDTH, :]) + _nn(dd, w_ref[ATTN_WIDTH:, :])

    tok = lambda w: pl.BlockSpec((tm, w), lambda i: (i, 0))
    return pl.pallas_call(
        body, name="outproj_fwd", grid=(S // tm,),
        in_specs=[tok(D), tok(ATTN_WIDTH), tok(DN_WIDTH), pl.BlockSpec((D, D), lambda i: (0, 0))],
        out_specs=[tok(D), tok(D)],
        out_shape=[jax.ShapeDtypeStruct((S, D), F32), jax.ShapeDtypeStruct((S, D), BF16)],
        compiler_params=_params(1),
    )(x, attn, dn, w_out)


def _outproj_bwd(dx, w_out, attn, comm=None):
    S, D = dx.shape
    tm = VIEW_TILE

    def body(dx_ref, w_ref, attn_ref, da1, da4, da16, dl1, dl4, dl16, ddn_ref, dxb_ref, planes):
        d = dx_ref[...].astype(BF16)
        dxb_ref[...] = d
        da = _nt(d, w_ref[0:ATTN_WIDTH, :])
        ddn_ref[...] = _nt(d, w_ref[ATTN_WIDTH:, :])
        _tile_to_views(da, planes, (da1, da4, da16))
        lo = lax.broadcasted_iota(jnp.int32, (tm, 128), 1) < 64
        cols = []
        for G in range(4):
            sl = slice(G * 128, (G + 1) * 128)
            t = da[:, sl] * attn_ref[:, sl]
            d0 = jnp.sum(jnp.where(lo, t, 0.0), axis=-1, keepdims=True)
            d1 = jnp.sum(jnp.where(lo, 0.0, t), axis=-1, keepdims=True)
            cols.append(jnp.where(lo, d0, d1))
        _tile_to_views(jnp.concatenate(cols, axis=1), planes, (dl1, dl4, dl16))

    tok = lambda w: pl.BlockSpec((tm, w), lambda i: (i, 0))
    views = [_view_spec(d) for d in DILATIONS]
    return _call(
        body, (dx, w_out, attn), name="outproj_bwd", grid=(S // tm,), comm=comm,
        in_specs=[tok(D), pl.BlockSpec((D, D), lambda i: (0, 0)), tok(ATTN_WIDTH)],
        out_specs=views + views + [tok(DN_WIDTH), tok(D)],
        out_shape=[_view_shape(S, d, F32) for d in DILATIONS] * 2
                  + [jax.ShapeDtypeStruct((S, DN_WIDTH), F32), jax.ShapeDtypeStruct((S, D), BF16)],
        scratch_shapes=[pltpu.VMEM((4, tm, 128), F32)])


def _adamw(w, g, m, v, name):
    R, Ccols = w.shape[0], w.shape[-1]
    tr = next((t for t in range(512, 7, -8) if R % t == 0), R)
    c1 = 1.0 - ADAM_B1 ** ADAM_STEP
    c2 = 1.0 - ADAM_B2 ** ADAM_STEP

    def body(w_ref, g_ref, m_ref, v_ref, d_ref, nm_ref, nv_ref):
        gv = g_ref[...]
        mn = ADAM_B1 * m_ref[...] + (1.0 - ADAM_B1) * gv
        vn = ADAM_B2 * v_ref[...] + (1.0 - ADAM_B2) * (gv * gv)
        nm_ref[...] = mn
        nv_ref[...] = vn
        d_ref[...] = -ADAM_LR * ((mn / c1) / (jnp.sqrt(vn / c2) + ADAM_EPS) + ADAM_WD * w_ref[...])

    if w.ndim == 2:
        grid, spec = (R // tr,), pl.BlockSpec((tr, Ccols), lambda i: (i, 0))
    else:
        grid, spec = (2,), pl.BlockSpec((R // 2, 1, Ccols), lambda i: (i, 0, 0))
    return pl.pallas_call(
        body, name=name, grid=grid, in_specs=[spec] * 4, out_specs=[spec] * 3,
        out_shape=[jax.ShapeDtypeStruct(w.shape, F32)] * 3, compiler_params=_params(1),
    )(w, g, m, v)


LATE_WEIGHTS = ("w_in", "w_out", "ffn2_gate", "ffn2_up", "ffn2_down")


def _local_step(x, target, wts, small, dist=None):
    g1, g2, gm, gf = small["norm_ffn1"], small["norm_ffn2"], small["norm_mix"], small["norm_final"]
    wts = dict(wts)

    def reduce_start(gs, tag):
        return _rs_add_pairs(gs, _swap_sibling(gs, True, "rs_swap_halves_" + tag), dist["c"], "rs_add_pairs_" + tag)

    (x1, h1, fg1, fu1), late = _ffn_fwd(x, g1, wts["ffn1_gate"], wts["ffn1_up"], wts["ffn1_down"], "ffn1_fwd",
                                        comm=_ag_comm(dist["late"]) if dist else None)
    if dist:
        wts.update(zip(LATE_WEIGHTS, late))
        wts["w_out"] = wts["w_out"].reshape(D_MODEL, D_MODEL)
        wts["w_in"] = _permute_w_in(wts["w_in"][:, :IN_COLS // N_CHIPS].reshape(IN_COLS, D_MODEL))
    h2, *qkv, xq, xk, xv, gate, bd = _inproj_fwd(x1, gm, wts["w_in"])
    aq, ak, av = qkv[0:3], qkv[3:6], qkv[6:9]
    parts = [_attn_fwd(aq[p], ak[p], av[p], d, f"attn_fwd_d{d}") for p, d in enumerate(DILATIONS)]
    attn, *lse = _attn_merge(parts)
    conv_w = small["conv_w"]
    qn, kn, vv = _conv_fwd(xq, xk, xv, conv_w)
    dn_u, dn_w, dn_qg, dn_kd, dn_aq, dn_t, dn_egl = _dn_prep(qn, kn, vv, bd, small["avec"], small["dvec"])
    dn, o_dn, v_new, states = _dn_scan_fwd(dn_u, dn_w, dn_qg, dn_kd, dn_aq, dn_egl, gate, small["dn_norm"])
    x2, mix = _outproj_fwd(x1, attn, dn, wts["w_out"])
    (dx3, h3, fg2, fu2, loss, d_gf), _ = _ffn_fwd(x2, g2, wts["ffn2_gate"], wts["ffn2_up"], wts["ffn2_down"], "ffn2_fwd",
                                                 head=(gf, target))

    grads = {}
    (dx2, d_g2, dfg2, dfu2, act2, dout2), _ = _ffn_bwd(dx3, x2, g2, fg2, fu2, wts["ffn2_down"], wts["ffn2_gate"],
                                                      wts["ffn2_up"], "ffn2_bwd")
    tk = 2048
    grads["ffn2_gate"], _ = _dw_chunks(dfg2, h3, tk, "dw_ffn2_gate")
    grads["ffn2_up"], _ = _dw_chunks(dfu2, h3, tk, "dw_ffn2_up")
    grads["ffn2_down"], _ = _dw_chunks(act2, dout2, tk, "dw_ffn2_down")
    group_a = ("ffn2_gate", "ffn2_up", "ffn2_down")
    gs_a = [grads[n] for n in group_a]

    (*dviews, ddn, dx2b), swapped_a = _outproj_bwd(dx2, wts["w_out"], attn, comm=_swap_comm(gs_a, True) if dist else None)
    parts_a = _rs_add_pairs(gs_a, swapped_a, dist["c"], "rs_add_pairs_a") if dist else None
    dattn, dd = dviews[0:3], dviews[3:6]
    grads["w_out"] = _matmul_tn(mix, dx2b, D_MODEL, tk, "dw_out").reshape(N_CHIPS, D_MODEL // N_CHIPS, D_MODEL)

    daq, dak, dav = [], [], []
    for p, d in enumerate(DILATIONS):
        daq.append(_attn_bwd_q(aq[p], ak[p], av[p], dattn[p], lse[p], dd[p], d, f"attn_bwd_q_d{d}"))
        dk_p, dv_p = _attn_bwd_kv(aq[p], ak[p], av[p], dattn[p], lse[p], dd[p], d, f"attn_bwd_kv_d{d}")
        dak.append(dk_p)
        dav.append(dv_p)

    do_dn, dvn, dgate, dstates, d_dn_gain = _dn_scan_bwd(dn_w, dn_qg, dn_kd, dn_aq, dn_egl, gate, small["dn_norm"], o_dn, ddn)
    (dqn, dkn, dvv, dbd, dn_small), recv_a = _dn_post(qn, kn, vv, bd, small["avec"], small["dvec"], dn_t, dn_u, dn_w, dn_aq, v_new, states,
                                                      dstates, do_dn, dvn, comm=_rsx_comm(parts_a) if dist else None)
    dcq, dck, dcv, dwq, dwk, dwv = _conv_bwd_pre(xq, xk, xv, conv_w, dqn, dkn, dvv)
    dxq, dxk, dxv = _conv_bwd_x(dcq, dck, dcv, conv_w)
    d_conv = jnp.concatenate([dwq[:CONV_WIDTH], dwk[:CONV_WIDTH], dwv[:CONV_WIDTH]], axis=1)

    dx1, d_gm, dproj = _inproj_bwd(dx2, x1, gm, [daq, dak, dav], [dxq, dxk, dxv, dgate], dbd, wts["w_in"])
    gi = _matmul_tn(dproj, h2, IN_COLS_PADDED, 512, "dw_in")
    if dist:
        gate_end = QKV_COLS + DN_WIDTH
        gi = jnp.concatenate([gi[:QKV_COLS], gi[gate_end:gate_end + LOGIT_COLS], gi[QKV_COLS:gate_end]], axis=0)
        gi = gi.reshape(N_CHIPS, IN_COLS // N_CHIPS, D_MODEL)
        gi = jnp.pad(gi, ((0, 0), (0, W_IN_ROWS - IN_COLS // N_CHIPS), (0, 0)))
    grads["w_in"] = gi
    group_b = ("w_in", "w_out")
    parts_b = reduce_start([grads[n] for n in group_b], "b") if dist else None

    (dx0, d_g1, dfg1, dfu1, act1, dout1), recv_b = _ffn_bwd(dx1, x, g1, fg1, fu1, wts["ffn1_down"], wts["ffn1_gate"],
                                                           wts["ffn1_up"], "ffn1_bwd",
                                                           comm=_rsx_comm(parts_b) if dist else None)
    group_c = ("ffn1_gate", "ffn1_up", "ffn1_down")
    pending, parts_c, recv_c = [], [], []
    for n, (lhs, rhs) in zip(group_c, ((dfg1, h1), (dfu1, h1), (act1, dout1))):
        grads[n], landed = _dw_chunks(lhs, rhs, tk, "dw_" + n, comm=_rsx_comm(pending) if pending else None)
        recv_c += list(landed)
        if dist:
            pending = reduce_start([grads[n]], n)
            parts_c += pending

    small_grads = dict(norm_ffn1=d_g1, norm_mix=d_gm, norm_ffn2=d_g2, norm_final=d_gf, conv_w=d_conv,
                       a_log=dn_small[0:1], dt_bias=dn_small[1:2], dn_norm=d_dn_gain[0:1])
    if dist:
        recv_c += _rs_exchange_arrays(pending)
        names = group_a + group_b + group_c
        totals = _rs_add_totals(list(parts_a) + list(parts_b) + list(parts_c), list(recv_a) + list(recv_b) + list(recv_c),
                                dist["chip"])
        theirs = _swap_sibling(totals, False, "rs_share_total")
        grads = {n: (mine, other) for n, mine, other in zip(names, totals, theirs)}
    return loss, dx0, grads, small_grads


HBM =pl.BlockSpec(memory_space=pl.ANY)
VMEM_SPEC = pl.BlockSpec(memory_space=pltpu.VMEM)


def _coords():
    return lax.axis_index("x"), lax.axis_index("y"), lax.axis_index("c")


def _remote(src, dst, send_sems, recv_sems, k, dev):
    return pltpu.make_async_remote_copy(src_ref=src, dst_ref=dst, send_sem=send_sems.at[k], recv_sem=recv_sems.at[k],
                                        device_id=dev, device_id_type=MESH)


def _allreduce_small(buf, name):
    R, Cc = buf.shape

    def body(src_ref, out_ref, recv_ref, send_sems, recv_sems):
        x, y, c = _coords()
        copies = []
        for m in range(1, 8):
            fx, fy, fc = (m >> 2) & 1, (m >> 1) & 1, m & 1
            dev = (x ^ fx if fx else x, y ^ fy if fy else y, c ^ fc if fc else c)
            cp = _remote(src_ref, recv_ref.at[m - 1], send_sems, recv_sems, m - 1, dev)
            cp.start()
            copies.append(cp)
        for cp in copies:
            cp.wait()
        r = [src_ref[...]] + [recv_ref[m] for m in range(7)]
        out_ref[...] = ((r[0] + r[1]) + (r[2] + r[3])) + ((r[4] + r[5]) + (r[6] + r[7]))

    return pl.pallas_call(
        body, name=name, out_shape=jax.ShapeDtypeStruct((R, Cc), F32),
        in_specs=[VMEM_SPEC], out_specs=VMEM_SPEC,
        scratch_shapes=[pltpu.VMEM((7, R, Cc), F32), pltpu.SemaphoreType.DMA((7,)), pltpu.SemaphoreType.DMA((7,))],
    )(buf)


BIG = ("ffn1_gate", "ffn1_up", "ffn1_down", "w_in", "w_out", "ffn2_gate", "ffn2_up", "ffn2_down")
ROW_SHARDED = ("ffn1_down", "w_out", "ffn2_down")
W_IN_ROWS = 960


def _rows(ref, start, size):
    return ref.at[pl.ds(pl.multiple_of(start, 16), size)]


def _allgather_arrays(shards):
    n = len(shards)
    _, shapes, n_sems, start, finish, middle = _ag_comm(shards)

    def body(*refs):
        for phase in (start, middle, finish):
            phase(refs[:n], refs[n:2 * n], refs[2 * n], refs[2 * n + 1])

    return pl.pallas_call(
        body, name="allgather_weights", out_shape=shapes, in_specs=[HBM] * n, out_specs=[HBM] * n,
        scratch_shapes=[pltpu.SemaphoreType.DMA((n_sems,)), pltpu.SemaphoreType.DMA((n_sems,))],
    )(*shards)


def _ag_copies(srcs, outs, send_sems, recv_sems):
    x, y, c = _coords()
    sib = (x, y, 1 - c)
    xn, yn, dg = (1 - x, y), (x, 1 - y), (1 - x, 1 - y)
    plan = []
    for a, (src, out) in enumerate(zip(srcs, outs)):
        h = src.shape[0] // 2
        q = h // 2
        cp = lambda s, d, k, dev: _remote(s, d, send_sems, recv_sems, 8 * a + k, dev)
        slot = lambda chip: out.at[2 * chip[0] + chip[1]]
        mine, dst = _rows(src, c * h, h), _rows(slot((x, y)), c * h, h)
        piece = lambda chip, start, size, k, dev: cp(_rows(slot(chip), start, size), _rows(slot(chip), start, size), k, dev)
        plan.append(dict(
            own=cp(src, slot((x, y)), 6, sib),
            to_x=cp(mine, dst, 0, (*xn, c)), to_y=cp(mine, dst, 1, (*yn, c)),
            from_x=piece(xn, c * h, h, 0, sib), from_y=piece(yn, c * h, h, 1, sib),
            relay_y=piece(xn, c * h, q, 2, (*yn, c)), relay_x=piece(yn, c * h + q, q, 7, (*xn, c)),
            pass_x=piece(xn, c * h, h, 3, sib), pass_y=piece(yn, c * h, h, 4, sib), pass_d=piece(dg, c * h, h, 5, sib),
            diag_1=piece(dg, c * h, q, 2, sib), diag_2=piece(dg, c * h + q, q, 7, sib),
            got=[piece(chip, (1 - c) * h, h, k, sib) for k, chip in ((3, xn), (4, yn), (5, dg))]))
    return plan


def _ag_start(*refs):
    for p in _ag_copies(*refs):
        for k in ("own", "to_x", "to_y"):
            p[k].start()


def _ag_middle(*refs):
    for p in _ag_copies(*refs):
        p["from_x"].wait_recv()
        p["relay_y"].start()
        p["pass_x"].start()
        p["from_y"].wait_recv()
        p["relay_x"].start()
        p["pass_y"].start()


def _ag_finish(*refs):
    plan = _ag_copies(*refs)
    for p in plan:
        p["diag_1"].wait_recv()
        p["diag_2"].wait_recv()
        p["pass_d"].start()
    for p in plan:
        for cp in p["got"]:
            cp.wait_recv()
        p["own"].wait_recv()
        for k in ("own", "to_x", "to_y", "relay_y", "relay_x", "pass_x", "pass_y", "pass_d"):
            p[k].wait_send()


def _ag_comm(shards):
    shapes = [jax.ShapeDtypeStruct((N_CHIPS,) + s.shape, s.dtype) for s in shards]
    return (list(shards), shapes, 8 * len(shards), _ag_start, _ag_finish, _ag_middle)


def _swap_sibling(arrs, pick_other_half, name):
    n = len(arrs)
    _, outs, n_sems, start, finish = _swap_comm(arrs, pick_other_half)

    def body(*refs):
        start(refs[:n], refs[n:2 * n], refs[2 * n], refs[2 * n + 1])
        finish(refs[:n], refs[n:2 * n], refs[2 * n], refs[2 * n + 1])

    return pl.pallas_call(
        body, name=name, out_shape=outs, in_specs=[HBM] * n, out_specs=[HBM] * n,
        scratch_shapes=[pltpu.SemaphoreType.DMA((n_sems,)), pltpu.SemaphoreType.DMA((n_sems,))],
    )(*arrs)


def _swap_comm(arrs, pick_other_half):
    def copies(srcs, dsts, send_sems, recv_sems):
        x, y, c = _coords()
        cps = []
        for a, (src, dst) in enumerate(zip(srcs, dsts)):
            if pick_other_half:
                h = src.shape[1] // 2
                src = src.at[:, pl.ds(pl.multiple_of((1 - c) * h, 16), h)]
            cps.append(_remote(src, dst, send_sems, recv_sems, a, (x, y, 1 - c)))
        return cps

    def start(*refs):
        for cp in copies(*refs):
            cp.start()

    def finish(*refs):
        for cp in copies(*refs):
            cp.wait()

    shapes = [jax.ShapeDtypeStruct((a.shape[0], a.shape[1] // 2) + a.shape[2:] if pick_other_half else a.shape, a.dtype)
              for a in arrs]
    return (list(arrs), shapes, len(arrs), start, finish)


def _rs_add_pairs(gs, others, c, name):
    n = len(gs)
    blocks = [(g.shape[1] // 4, g.shape[2]) for g in gs]

    def body(c_ref, *refs):
        for a in range(n):
            refs[2 * n + a][...] = (refs[a][...] + refs[n + a][...]).astype(BF16)

    mine = lambda b: pl.BlockSpec((None,) + b, lambda j, s, c_ref: (j, c_ref[0] * 2 + s, 0))
    flat = lambda b: pl.BlockSpec((None,) + b, lambda j, s, c_ref: (j, s, 0))
    return pl.pallas_call(
        body, name=name,
        grid_spec=pltpu.PrefetchScalarGridSpec(
            num_scalar_prefetch=1, grid=(N_CHIPS, 2),
            in_specs=[mine(b) for b in blocks] + [flat(b) for b in blocks],
            out_specs=[flat(b) for b in blocks]),
        out_shape=[jax.ShapeDtypeStruct(o.shape, BF16) for o in others],
        compiler_params=_params(2),
    )(c, *gs, *others)


def _rs_exchange_arrays(parts):
    n = len(parts)

    def body(*refs):
        _rsx_start(refs[:n], refs[n:2 * n], refs[2 * n], refs[2 * n + 1])
        _rsx_finish(refs[:n], refs[n:2 * n], refs[2 * n], refs[2 * n + 1])

    _, shapes, n_sems, _, _ = _rsx_comm(parts)
    return pl.pallas_call(
        body, name="rs_exchange_chips", out_shape=shapes, in_specs=[HBM] * n, out_specs=[HBM] * n,
        scratch_shapes=[pltpu.SemaphoreType.DMA((n_sems,)), pltpu.SemaphoreType.DMA((n_sems,))],
    )(*parts)


def _rsx_copies(srcs, dsts, send_sems, recv_sems):
    x, y, c = _coords()
    others = [(1 - x, y), (x, 1 - y), (1 - x, 1 - y)]
    return [_remote(src.at[2 * ox + oy], dst.at[k], send_sems, recv_sems, 3 * a + k, (ox, oy, c))
            for a, (src, dst) in enumerate(zip(srcs, dsts)) for k, (ox, oy) in enumerate(others)]


def _rsx_start(srcs, dsts, send_sems, recv_sems):
    for cp in _rsx_copies(srcs, dsts, send_sems, recv_sems):
        cp.start()


def _rsx_finish(srcs, dsts, send_sems, recv_sems):
    for cp in _rsx_copies(srcs, dsts, send_sems, recv_sems):
        cp.wait()


def _rsx_comm(parts):
    shapes = [jax.ShapeDtypeStruct((3,) + p.shape[1:], p.dtype) for p in parts]
    return (list(parts), shapes, 3 * len(parts), _rsx_start, _rsx_finish)


def _rs_add_totals(parts, recvs, chip):
    n = len(parts)
    blocks = [(p.shape[1] // 2, p.shape[2]) for p in parts]

    def body(chip_ref, *refs):
        f = lambda r: r[...].astype(F32)
        for a in range(n):
            p, r0, r1, r2 = refs[a], refs[n + 3 * a], refs[n + 3 * a + 1], refs[n + 3 * a + 2]
            refs[4 * n + a][...] = (f(p) + f(r0)) + (f(r1) + f(r2))

    own = lambda b: pl.BlockSpec((None,) + b, lambda s, chip_ref: (chip_ref[0], s, 0))
    slot = lambda b, k: pl.BlockSpec((None,) + b, lambda s, chip_ref, k=k: (k, s, 0))
    recv_specs = [slot(b, k) for b in blocks for k in range(3)]
    recv_args = [r for r in recvs for _ in range(3)]
    return pl.pallas_call(
        body, name="rs_add_totals",
        grid_spec=pltpu.PrefetchScalarGridSpec(
            num_scalar_prefetch=1, grid=(2,),
            in_specs=[own(b) for b in blocks] + recv_specs,
            out_specs=[pl.BlockSpec(b, lambda s, chip_ref: (s, 0)) for b in blocks]),
        out_shape=[jax.ShapeDtypeStruct(p.shape[1:], F32) for p in parts],
        compiler_params=_params(1),
    )(chip, *parts, *recv_args)


def _permute_w_in(wt):
    return jnp.concatenate([wt[:QKV_COLS], wt[QKV_COLS + LOGIT_COLS:IN_COLS], wt[QKV_COLS:QKV_COLS + LOGIT_COLS],
                            jnp.zeros((IN_COLS_PADDED - IN_COLS, wt.shape[1]), wt.dtype)], axis=0)


def _pad_row(v):
    v = v.reshape(1, -1)
    return jnp.pad(v, ((0, 0), (0, D_MODEL - v.shape[1])))


def kernel(x, norm_ffn1, ffn1_gate, ffn1_up, ffn1_down, norm_mix, w_in, conv_w, a_log, dt_bias, dn_norm, w_out, norm_ffn2, ffn2_gate, ffn2_up, ffn2_down, norm_final, loss_target, m_norm_ffn1, m_ffn1_gate, m_ffn1_up, m_ffn1_down, m_norm_mix, m_w_in, m_conv_w, m_a_log, m_dt_bias, m_dn_norm, m_w_out, m_norm_ffn2, m_ffn2_gate, m_ffn2_up, m_ffn2_down, m_norm_final, v_norm_ffn1, v_ffn1_gate, v_ffn1_up, v_ffn1_down, v_norm_mix, v_w_in, v_conv_w, v_a_log, v_dt_bias, v_dn_norm, v_w_out, v_norm_ffn2, v_ffn2_gate, v_ffn2_up, v_ffn2_down, v_norm_final):
    cx, cy, cc = _coords()
    chip = 2 * cx + cy
    stored = lambda t, n: t[0] if n in ROW_SHARDED else t[0].T
    big_w = {n: stored(t, n) for n, t in dict(
        ffn1_gate=ffn1_gate, ffn1_up=ffn1_up, ffn1_down=ffn1_down, w_in=w_in, w_out=w_out,
        ffn2_gate=ffn2_gate, ffn2_up=ffn2_up, ffn2_down=ffn2_down).items()}
    big_m = {n: stored(t, n) for n, t in dict(
        ffn1_gate=m_ffn1_gate, ffn1_up=m_ffn1_up, ffn1_down=m_ffn1_down, w_in=m_w_in, w_out=m_w_out,
        ffn2_gate=m_ffn2_gate, ffn2_up=m_ffn2_up, ffn2_down=m_ffn2_down).items()}
    big_v = {n: stored(t, n) for n, t in dict(
        ffn1_gate=v_ffn1_gate, ffn1_up=v_ffn1_up, ffn1_down=v_ffn1_down, w_in=v_w_in, w_out=v_w_out,
        ffn2_gate=v_ffn2_gate, ffn2_up=v_ffn2_up, ffn2_down=v_ffn2_down).items()}

    cols = IN_COLS // N_CHIPS
    send = {n: big_w[n].astype(BF16) for n in BIG}
    send["w_in"] = jnp.pad(send["w_in"], ((0, W_IN_ROWS - cols), (0, 0)))
    early = tuple(n for n in BIG if n not in LATE_WEIGHTS)
    wts = dict(zip(early, _allgather_arrays([send[n] for n in early])))
    dist =dict(late=[send[n] for n in LATE_WEIGHTS], c=cc.reshape(1).astype(jnp.int32),
                chip=chip.reshape(1).astype(jnp.int32))

    conv_shard = conv_w[0]
    emb = jnp.concatenate([jnp.where((chip == j) & (cc == 0), conv_shard, 0.0) for j in range(N_CHIPS)], axis=1)
    emb = jnp.pad(emb.reshape(6, D_MODEL), ((0, 2), (0, 0)))
    conv_full = _allreduce_small(emb, "allgather_conv_w")[:6].reshape(CONV_WIDTH, 3 * DN_WIDTH)

    zvec = jnp.zeros((1, 128), F32)
    small = dict(norm_ffn1=norm_ffn1, norm_mix=norm_mix, norm_ffn2=norm_ffn2, norm_final=norm_final[None],
                 conv_w=conv_full, avec=zvec.at[0, DN_HEADS:2 * DN_HEADS].set(a_log[0]),
                 dvec=zvec.at[0, DN_HEADS:2 * DN_HEADS].set(dt_bias[0]), dn_norm=dn_norm)

    loss, grad_x, reduced, sg = _local_step(x[0], loss_target[0], wts, small, dist)

    rows = [sg["norm_ffn1"], sg["norm_mix"], sg["norm_ffn2"], sg["norm_final"], _pad_row(sg["a_log"]), _pad_row(sg["dt_bias"]),
            _pad_row(sg["dn_norm"]), _pad_row(loss[0:1]), sg["conv_w"].reshape(6, D_MODEL), jnp.zeros((2, D_MODEL), F32)]
    red = _allreduce_small(jnp.concatenate(rows, axis=0), "allreduce_small")
    loss_out = red[7, 0]
    g_conv_full = red[8:14].reshape(CONV_WIDTH, 3 * DN_WIDTH)
    g_conv = lax.dynamic_slice_in_dim(g_conv_full, chip * (3 * DN_WIDTH // N_CHIPS), 3 * DN_WIDTH // N_CHIPS, axis=1)
    g_small = dict(norm_ffn1=red[0:1], norm_mix=red[1:2], norm_ffn2=red[2:3], norm_final=red[3],
                   a_log=red[4:5, DN_HEADS:2 * DN_HEADS], dt_bias=red[5:6, DN_HEADS:2 * DN_HEADS], dn_norm=red[6:7, :DN_HEAD_DIM])

    out_g, out_d, out_m, out_v = {}, {}, {}, {}
    for n in BIG:
        mine, other = reduced[n]
        g = jnp.where(cc == 0, jnp.concatenate([mine, other], axis=0), jnp.concatenate([other, mine], axis=0))
        if n == "w_in":
            to3 = lambda t: jnp.transpose(t, (2, 0, 1))
            g = g[:cols].reshape(cols, 1, D_MODEL)
            results = (g,) + tuple(_adamw(to3(w_in), g, to3(m_w_in), to3(v_w_in), "adamw_w_in"))
            out_g[n], out_d[n], out_m[n], out_v[n] = (jnp.transpose(t, (1, 2, 0)) for t in results)
            continue
        results = (g,) + tuple(_adamw(big_w[n], g, big_m[n], big_v[n], "adamw_" + n))
        out_g[n], out_d[n], out_m[n], out_v[n] = ((t if n in ROW_SHARDED else t.T)[None] for t in results)
    d, nm, nv = _adamw(conv_w[0], g_conv, m_conv_w[0], v_conv_w[0], "adamw_conv_w")
    out_g["conv_w"], out_d["conv_w"], out_m["conv_w"], out_v["conv_w"] = g_conv[None], d[None], nm[None], nv[None]

    small_names = ("norm_ffn1", "norm_mix", "norm_ffn2", "norm_final", "a_log", "dt_bias", "dn_norm")
    small_w = dict(norm_ffn1=norm_ffn1, norm_mix=norm_mix, norm_ffn2=norm_ffn2, norm_final=norm_final, a_log=a_log,
                   dt_bias=dt_bias, dn_norm=dn_norm)
    small_m = dict(norm_ffn1=m_norm_ffn1, norm_mix=m_norm_mix, norm_ffn2=m_norm_ffn2, norm_final=m_norm_final, a_log=m_a_log,
                   dt_bias=m_dt_bias, dn_norm=m_dn_norm)
    small_v = dict(norm_ffn1=v_norm_ffn1, norm_mix=v_norm_mix, norm_ffn2=v_norm_ffn2, norm_final=v_norm_final, a_log=v_a_log,
                   dt_bias=v_dt_bias, dn_norm=v_dn_norm)
    stack = lambda dct: jnp.concatenate([_pad_row(dct[n]) for n in small_names] + [jnp.zeros((1, D_MODEL), F32)], axis=0)
    d, nm, nv = _adamw(stack(small_w), stack(g_small), stack(small_m), stack(small_v), "adamw_small")
    for k, n in enumerate(small_names):
        shape = small_w[n].shape
        size = math.prod(shape)
        out_g[n] = g_small[n].reshape(shape)
        out_d[n], out_m[n], out_v[n] = (t[k, :size].reshape(shape) for t in (d, nm, nv))

    order = ("norm_ffn1", "ffn1_gate", "ffn1_up", "ffn1_down", "norm_mix", "w_in", "conv_w", "a_log", "dt_bias", "dn_norm",
             "w_out", "norm_ffn2", "ffn2_gate", "ffn2_up", "ffn2_down", "norm_final")
    return (loss_out, grad_x[None], *[out_g[n] for n in order], *[out_d[n] for n in order],
            *[out_m[n] for n in order], *[out_v[n] for n in order])
```

```python
import functools
import math

import jax
import jax.numpy as jnp
from jax import lax
from jax.experimental import pallas as pl
from jax.experimental.pallas import tpu as pltpu

F32 = jnp.float32
BF16 = jnp.bfloat16
HI = lax.Precision.HIGH

D_MODEL = 1024
ATTN_HEADS = 8
ATTN_WIDTH = 512
ATTN_BLOCK = 128
ATTN_SCALE = (ATTN_WIDTH // ATTN_HEADS) ** -0.5
DILATIONS = (1, 4, 16)
DN_HEADS = 4
DN_HEAD_DIM = 128
DN_WIDTH = 512
DN_CHUNK = 64
CONV_WIDTH = 4
NORM_EPS = 1e-6
L2_EPS = 1e-6
QKV_COLS = 3 * ATTN_WIDTH + 3 * DN_WIDTH
LOGIT_COLS = 2 * DN_HEADS
IN_COLS = QKV_COLS + LOGIT_COLS + DN_WIDTH
IN_COLS_PADDED = 3712
N_CHIPS = 4

ADAM_LR = 0.001
ADAM_B1 = 0.9
ADAM_B2 = 0.999
ADAM_EPS = 1e-08
ADAM_WD = 0.01
ADAM_STEP = 10

VMEM_LIMIT = 56 * 1024 * 1024
NEG_BIG = -1e30
MESH = pl.DeviceIdType.MESH


def _params(n_grid, vmem=VMEM_LIMIT):
    return pltpu.CompilerParams(dimension_semantics=("arbitrary",) * n_grid, vmem_limit_bytes=vmem)


def _call(body, args, *, name, grid, in_specs, out_specs, out_shape, scratch_shapes=(), comm=None):
    n_in, n_out, n_scr = len(in_specs), len(out_specs), len(scratch_shapes)
    hbm = pl.BlockSpec(memory_space=pl.ANY)
    srcs, dst_shapes, n_sems, start, finish = comm[:5] if comm is not None else ((), (), 0, None, None)
    middle = comm[5] if comm is not None and len(comm) > 5 else None
    ns, nd = len(srcs), len(dst_shapes)

    def full(*refs):
        ins, c_src = refs[:n_in], refs[n_in:n_in + ns]
        at = n_in + ns
        outs, c_dst = refs[at:at + n_out], refs[at + n_out:at + n_out + nd]
        scr = refs[at + n_out + nd:at + n_out + nd + n_scr]
        if comm is not None:
            ids = [pl.program_id(a) for a in range(len(grid))]
            first = functools.reduce(jnp.logical_and, [i == 0 for i in ids])
            last = functools.reduce(jnp.logical_and, [i == g - 1 for i, g in zip(ids, grid)])

            @pl.when(first)
            def _():
                start(c_src, c_dst, refs[-2], refs[-1])

            if middle is not None:
                relay_step = functools.reduce(jnp.logical_and, [ids[0] == (5 * grid[0]) // 8] + [i == 0 for i in ids[1:]])

                @pl.when(relay_step)
                def _():
                    middle(c_src, c_dst, refs[-2], refs[-1])

        body(*ins, *outs, *scr)
        if comm is not None:
            @pl.when(last)
            def _():
                finish(c_src, c_dst, refs[-2], refs[-1])

    sems = [pltpu.SemaphoreType.DMA((n_sems,)), pltpu.SemaphoreType.DMA((n_sems,))] if comm is not None else []
    res = pl.pallas_call(
        full, name=name, grid=grid, in_specs=list(in_specs) + [hbm] * ns, out_specs=list(out_specs) + [hbm] * nd,
        out_shape=list(out_shape) + list(dst_shapes), scratch_shapes=list(scratch_shapes) + sems,
        compiler_params=_params(len(grid)),
    )(*args, *srcs)
    return res[:n_out], res[n_out:]


def _nt(a, b, precision=None):
    return lax.dot_general(a, b, (((1,), (1,)), ((), ())), preferred_element_type=F32, precision=precision)


def _tn(a, b, precision=None):
    return lax.dot_general(a, b, (((0,), (0,)), ((), ())), preferred_element_type=F32, precision=precision)


def _nn(a, b, precision=None):
    return jnp.dot(a, b, preferred_element_type=F32, precision=precision)


def _sigmoid(x):
    return 1.0 / (1.0 + jnp.exp(-x))


def _loss_head(xf, gain, target):
    r = lax.rsqrt(jnp.mean(xf * xf, axis=-1, keepdims=True) + NORM_EPS)
    xhat = xf * r
    err = xhat * gain - target
    part = 0.5 * jnp.sum(jnp.mean(err * err, axis=-1, keepdims=True), axis=0, keepdims=True)
    dy = err * (1.0 / xf.shape[-1])
    dgain = jnp.sum(dy * xhat, axis=0, keepdims=True)
    dxh = dy * gain
    return part, r * (dxh - xhat * jnp.mean(dxh * xhat, axis=-1, keepdims=True)), dgain


def _ffn_fwd(x, gain, wg, wu, wd, name, comm=None, head=None):
    S, D = x.shape
    nf, tf, _ = wg.shape
    tm = 512
    n_in = 5 if head is None else 7

    def body(*refs):
        x_ref, gain_ref, wg_ref, wu_ref, wd_ref = refs[:5]
        xo_ref, h_ref, g_ref, u_ref = refs[n_in:n_in + 4]
        acc_ref, hs_ref = refs[-2:]
        i = pl.program_id(0)
        j = pl.program_id(1)

        @pl.when(j == 0)
        def _():
            xf = x_ref[...]
            r = lax.rsqrt(jnp.mean(xf * xf, axis=-1, keepdims=True) + NORM_EPS)
            h = (xf * r * gain_ref[...]).astype(BF16)
            hs_ref[...] = h
            h_ref[...] = h
            acc_ref[...] = jnp.zeros_like(acc_ref)

        h = hs_ref[...]
        g = _nt(h, wg_ref[...])
        u = _nt(h, wu_ref[...])
        g_ref[...] = g.astype(BF16)
        u_ref[...] = u.astype(BF16)
        act = g * _sigmoid(g) * u
        acc_ref[...] += _nn(act.astype(BF16), wd_ref[...])

        if head is not None:
            hgain_ref, t_ref = refs[5:7]
            loss_ref, dgain_ref = refs[n_in + 4:n_in + 6]

            @pl.when((i == 0) & (j == 0))
            def _():
                loss_ref[...] = jnp.zeros_like(loss_ref)
                dgain_ref[...] = jnp.zeros_like(dgain_ref)

        @pl.when(j == nf - 1)
        def _():
            xo = x_ref[...] + 0.5 * acc_ref[...]
            if head is None:
                xo_ref[...] = xo
            else:
                part, dxo, dgain = _loss_head(xo, hgain_ref[...], t_ref[...])
                first = ((lax.broadcasted_iota(jnp.int32, (8, 128), 0) == 0)
                         & (lax.broadcasted_iota(jnp.int32, (8, 128), 1) == 0))
                loss_ref[...] += jnp.where(first, part, 0.0)
                dgain_ref[...] += dgain
                xo_ref[...] = dxo

    tok = pl.BlockSpec((tm, D), lambda i, j: (i, 0))
    row = pl.BlockSpec((1, D), lambda i, j: (0, 0))
    chunk = pl.BlockSpec((None, tf, D), lambda i, j: (j, 0, 0))
    act = pl.BlockSpec((None, tm, tf), lambda i, j: (j, i, 0))
    extra_in = [] if head is None else [row, tok]
    extra_out = [] if head is None else [pl.BlockSpec((8, 128), lambda i, j: (0, 0)), row]
    extra_shape = [] if head is None else [jax.ShapeDtypeStruct((8, 128), F32), jax.ShapeDtypeStruct((1, D), F32)]
    return _call(
        body, (x, gain, wg, wu, wd) + (() if head is None else tuple(head)), name=name, grid=(S // tm, nf), comm=comm,
        in_specs=[tok, row, chunk, chunk, chunk] + extra_in,
        out_specs=[tok, tok, act, act] + extra_out,
        out_shape=[jax.ShapeDtypeStruct((S, D), F32), jax.ShapeDtypeStruct((S, D), BF16),
                   jax.ShapeDtypeStruct((nf, S, tf), BF16), jax.ShapeDtypeStruct((nf, S, tf), BF16)] + extra_shape,
        scratch_shapes=[pltpu.VMEM((tm, D), F32), pltpu.VMEM((tm, D), BF16)])


def _rmsnorm_bwd(dh, xf, gain):
    r = lax.rsqrt(jnp.mean(xf * xf, axis=-1, keepdims=True) + NORM_EPS)
    xhat = xf * r
    dgain = jnp.sum(dh * xhat, axis=0, keepdims=True)
    dxh = dh * gain
    dx = r * (dxh - xhat * jnp.mean(dxh * xhat, axis=-1, keepdims=True))
    return dx, dgain


def _ffn_bwd(dxo, x, gain, g, u, wd, wg, wu, name, comm=None):
    S, D = x.shape
    nf, _, tf = g.shape
    tm = 512

    def body(dxo_ref, x_ref, gain_ref, g_ref, u_ref, wd_ref, wg_ref, wu_ref,
             dx_ref, dgain_ref, dg_ref, du_ref, act_ref, dout_ref, acc_ref, ds_ref):
        i = pl.program_id(0)
        j = pl.program_id(1)

        @pl.when(j == 0)
        def _():
            d = (0.5 * dxo_ref[...]).astype(BF16)
            ds_ref[...] = d
            dout_ref[...] = d
            acc_ref[...] = jnp.zeros_like(acc_ref)

        @pl.when((i == 0) & (j == 0))
        def _():
            dgain_ref[...] = jnp.zeros_like(dgain_ref)

        for half in range(2):
            rows = slice(half * (tm // 2), (half + 1) * (tm // 2))
            dact = _nt(ds_ref[rows, :], wd_ref[...])
            gv = g_ref[rows, :].astype(F32)
            uv = u_ref[rows, :].astype(F32)
            sg = _sigmoid(gv)
            silu = gv * sg
            act_ref[rows, :] = (silu * uv).astype(BF16)
            dgv = (dact * uv * (sg * (1.0 + gv * (1.0 - sg)))).astype(BF16)
            duv = (dact * silu).astype(BF16)
            dg_ref[rows, :] = dgv
            du_ref[rows, :] = duv
            acc_ref[rows, :] += _nn(dgv, wg_ref[...]) + _nn(duv, wu_ref[...])

        @pl.when(j == nf - 1)
        def _():
            dx, dgain = _rmsnorm_bwd(acc_ref[...], x_ref[...], gain_ref[...])
            dx_ref[...] = dxo_ref[...] + dx
            dgain_ref[...] += dgain

    return _call(
        body, (dxo, x, gain, g, u, wd, wg, wu), name=name, grid=(S // tm, nf), comm=comm,
        in_specs=[pl.BlockSpec((tm, D), lambda i, j: (i, 0)),
                  pl.BlockSpec((tm, D), lambda i, j: (i, 0)),
                  pl.BlockSpec((1, D), lambda i, j: (0, 0)),
                  pl.BlockSpec((None, tm, tf), lambda i, j: (j, i, 0)),
                  pl.BlockSpec((None, tm, tf), lambda i, j: (j, i, 0)),
                  pl.BlockSpec((None, tf, D), lambda i, j: (j, 0, 0)),
                  pl.BlockSpec((None, tf, D), lambda i, j: (j, 0, 0)),
                  pl.BlockSpec((None, tf, D), lambda i, j: (j, 0, 0))],
        out_specs=[pl.BlockSpec((tm, D), lambda i, j: (i, 0)),
                   pl.BlockSpec((1, D), lambda i, j: (0, 0)),
                   pl.BlockSpec((None, tm, tf), lambda i, j: (j, i, 0)),
                   pl.BlockSpec((None, tm, tf), lambda i, j: (j, i, 0)),
                   pl.BlockSpec((None, tm, tf), lambda i, j: (j, i, 0)),
                   pl.BlockSpec((tm, D), lambda i, j: (i, 0))],
        out_shape=[jax.ShapeDtypeStruct((S, D), F32), jax.ShapeDtypeStruct((1, D), F32),
                   jax.ShapeDtypeStruct((nf, S, tf), BF16), jax.ShapeDtypeStruct((nf, S, tf), BF16),
                   jax.ShapeDtypeStruct((nf, S, tf), BF16), jax.ShapeDtypeStruct((S, D), BF16)],
        scratch_shapes=[pltpu.VMEM((tm, D), F32), pltpu.VMEM((tm, D), BF16)])


def _matmul_tn(a, b, tm, tk, name):
    K, M = a.shape
    N = b.shape[1]

    def body(a_ref, b_ref, o_ref):
        @pl.when(pl.program_id(1) == 0)
        def _():
            o_ref[...] = jnp.zeros_like(o_ref)

        o_ref[...] += _tn(a_ref[...], b_ref[...])

    return pl.pallas_call(
        body, name=name, grid=(M // tm, K // tk),
        in_specs=[pl.BlockSpec((tk, tm), lambda i, k: (k, i)),
                  pl.BlockSpec((tk, N), lambda i, k: (k, 0))],
        out_specs=pl.BlockSpec((tm, N), lambda i, k: (i, 0)),
        out_shape=jax.ShapeDtypeStruct((M, N), F32),
        compiler_params=_params(2),
    )(a, b)


def _dw_chunks(a, b, tk, name, comm=None):
    nf, S, tf = a.shape
    N = b.shape[1]

    def body(a_ref, b_ref, o_ref):
        @pl.when(pl.program_id(1) == 0)
        def _():
            o_ref[...] = jnp.zeros_like(o_ref)

        o_ref[...] += _tn(a_ref[...], b_ref[...])

    (out,), landed = _call(
        body, (a, b), name=name, grid=(nf, S // tk), comm=comm,
        in_specs=[pl.BlockSpec((None, tk, tf), lambda j, k: (j, k, 0)),
                  pl.BlockSpec((tk, N), lambda j, k: (k, 0))],
        out_specs=[pl.BlockSpec((None, tf, N), lambda j, k: (j, 0, 0))],
        out_shape=[jax.ShapeDtypeStruct((nf, tf, N), F32)])
    return out, landed


VIEW_TILE = 512


def _view_spec(d, tile=VIEW_TILE):
    return pl.BlockSpec((tile // d, d * ATTN_WIDTH), lambda i: (i, 0))


def _view_shape(S, d, dtype):
    return jax.ShapeDtypeStruct((S // d, d * ATTN_WIDTH), dtype)


def _tile_to_views(val, planes, out_refs):
    for g in range(4):
        planes[g] = val[:, g * 128:(g + 1) * 128]
    for d, ref in zip(DILATIONS, out_refs):
        if d == 1:
            ref[...] = val.astype(ref.dtype)
            continue
        for r in range(d):
            for g in range(4):
                ref[:, r * ATTN_WIDTH + g * 128:r * ATTN_WIDTH + (g + 1) * 128] = (
                    planes[g, pl.ds(r, planes.shape[1] // d, stride=d), :].astype(ref.dtype))


def _view_to_tile(ref, d, planes):
    if d == 1:
        return ref[...].astype(F32)
    for r in range(d):
        for g in range(4):
            planes[g, pl.ds(r, planes.shape[1] // d, stride=d), :] = (
                ref[:, r * ATTN_WIDTH + g * 128:r * ATTN_WIDTH + (g + 1) * 128].astype(F32))
    return jnp.concatenate([planes[g] for g in range(4)], axis=1)


def _inproj_fwd(x, gain, w_in_p):
    S, D = x.shape
    tm = VIEW_TILE
    W = ATTN_WIDTH

    def body(x_ref, gain_ref, w_ref, h_ref, q1, q4, q16, k1, k4, k16, v1, v4, v16, dq_ref, dk_ref, dv_ref, gate_ref, bd_ref,
             planes):
        xf = x_ref[...]
        r = lax.rsqrt(jnp.mean(xf * xf, axis=-1, keepdims=True) + NORM_EPS)
        h = (xf * r * gain_ref[...]).astype(BF16)
        h_ref[...] = h
        _tile_to_views(_nt(h, w_ref[0:W, :]) * ATTN_SCALE, planes, (q1, q4, q16))
        _tile_to_views(_nt(h, w_ref[W:2 * W, :]), planes, (k1, k4, k16))
        _tile_to_views(_nt(h, w_ref[2 * W:3 * W, :]), planes, (v1, v4, v16))
        dq_ref[...] = _nt(h, w_ref[3 * W:4 * W, :])
        dk_ref[...] = _nt(h, w_ref[4 * W:5 * W, :])
        dv_ref[...] = _nt(h, w_ref[5 * W:6 * W, :])
        gate_ref[...] = _nt(h, w_ref[6 * W:7 * W, :])
        bd_ref[...] = _nt(h, w_ref[7 * W:7 * W + 128, :])

    tok = lambda w: pl.BlockSpec((tm, w), lambda i: (i, 0))
    return pl.pallas_call(
        body, name="inproj_fwd", grid=(S // tm,),
        in_specs=[tok(D), pl.BlockSpec((1, D), lambda i: (0, 0)),
                  pl.BlockSpec((IN_COLS_PADDED, D), lambda i: (0, 0))],
        out_specs=[tok(D)] + [_view_spec(d) for d in DILATIONS] * 3 + [tok(W)] * 4 + [tok(128)],
        out_shape=[jax.ShapeDtypeStruct((S, D), BF16)] + [_view_shape(S, d, BF16) for d in DILATIONS] * 3
                  + [jax.ShapeDtypeStruct((S, W), F32)] * 4 + [jax.ShapeDtypeStruct((S, 128), F32)],
        scratch_shapes=[pltpu.VMEM((4, tm, 128), F32)],
        compiler_params=_params(1),
    )(x, gain, w_in_p)


def _inproj_bwd(dxo, x, gain, attn_grads, dsecs, dbd, w_in_p):
    S, D = x.shape
    tm = VIEW_TILE
    W = ATTN_WIDTH

    def body(dxo_ref, x_ref, gain_ref, *rest):
        views, (s3, s4, s5, s6, dbd_ref, w_ref, dx_ref, dgain_ref, dproj_ref, planes) = rest[:9], rest[9:]

        @pl.when(pl.program_id(0) == 0)
        def _():
            dgain_ref[...] = jnp.zeros_like(dgain_ref)

        secs = []
        for k in range(3):
            parts = [_view_to_tile(views[3 * k + p], d, planes) for p, d in enumerate(DILATIONS)]
            secs.append(parts[0] + parts[1] + parts[2])
        secs += [s3[...], s4[...], s5[...], s6[...]]
        dh = jnp.zeros((tm, D), F32)
        for k, s in enumerate(secs):
            d = s.astype(BF16)
            dproj_ref[:, k * W:(k + 1) * W] = d
            dh += _nn(d, w_ref[k * W:(k + 1) * W, :])
        d = dbd_ref[...].astype(BF16)
        dproj_ref[:, 7 * W:7 * W + 128] = d
        dh += _nn(d, w_ref[7 * W:7 * W + 128, :])
        dx, dgain = _rmsnorm_bwd(dh, x_ref[...], gain_ref[...])
        dx_ref[...] = dxo_ref[...] + dx
        dgain_ref[...] += dgain

    tok = lambda w: pl.BlockSpec((tm, w), lambda i: (i, 0))
    return pl.pallas_call(
        body, name="inproj_bwd", grid=(S // tm,),
        in_specs=[tok(D), tok(D), pl.BlockSpec((1, D), lambda i: (0, 0))] + [_view_spec(d, tm) for d in DILATIONS] * 3
                 + [tok(W)] * 4 + [tok(128)] + [pl.BlockSpec((IN_COLS_PADDED, D), lambda i: (0, 0))],
        out_specs=[tok(D), pl.BlockSpec((1, D), lambda i: (0, 0)), tok(IN_COLS_PADDED)],
        out_shape=[jax.ShapeDtypeStruct((S, D), F32), jax.ShapeDtypeStruct((1, D), F32),
                   jax.ShapeDtypeStruct((S, IN_COLS_PADDED), BF16)],
        scratch_shapes=[pltpu.VMEM((4, tm, 128), F32)],
        compiler_params=_params(1),
    )(dxo, x, gain, *[g for grads in attn_grads for g in grads], *dsecs, dbd, w_in_p)


def _slope(h):
    return 2.0 ** (-8.0 * (h + 1) / ATTN_HEADS)


def _head_bias(steps, d, heads=tuple(range(ATTN_HEADS))):
    stepsf = steps.astype(F32)
    return jnp.stack([stepsf * (-_slope(h) * d) for h in heads])


def _hnt(a, b):
    return lax.dot_general(a, b, (((2,), (2,)), ((0,), (0,))), preferred_element_type=F32)


def _hnn(a, b):
    return lax.dot_general(a, b, (((2,), (1,)), ((0,), (0,))), preferred_element_type=F32)


def _blocks_per_step(nb):
    return next(n for n in (8, 4, 2, 1) if nb % n == 0)


def _query_step_specs(qb):
    B = ATTN_BLOCK
    cur = pl.BlockSpec((qb * B, ATTN_WIDTH), lambda r, n: (n, r))
    prev = pl.BlockSpec((B, ATTN_WIDTH), lambda r, n: (jnp.maximum(qb * n - 1, 0), r))
    return cur, prev


def _prev_block(prev_ref, cur_ref, sub, sl):
    B = ATTN_BLOCK
    return prev_ref[:, sl] if sub == 0 else cur_ref[(sub - 1) * B:sub * B, sl]


def _head_cols(tile, lo, big):
    return [_head_col(tile, lo, big), _head_col(tile, jnp.logical_not(lo), big)]


def _attn_fwd(q, k, v, d, name):
    L = q.shape[0]
    nb = L // ATTN_BLOCK
    B = ATTN_BLOCK
    QB = _blocks_per_step(nb)

    def body(q_ref, kp_ref, kc_ref, vp_ref, vc_ref, o_ref, lse_ref):
        n = pl.program_id(1)
        qi = lax.broadcasted_iota(jnp.int32, (B, 2 * B), 0)
        kj = lax.broadcasted_iota(jnp.int32, (B, 2 * B), 1)
        steps = qi + B - kj
        band = (steps >= 0) & (steps <= B)
        lo = lax.broadcasted_iota(jnp.int32, (B, 128), 1) < 64
        bias = _head_bias(steps, d)
        for sub in range(QB):
            rows = slice(sub * B, (sub + 1) * B)
            valid = band & ((kj >= B) | (n > 0)) if sub == 0 else band
            qs, ks, vs = [], [], []
            for G in range(4):
                sl = slice(G * 128, (G + 1) * 128)
                qg = q_ref[rows, sl]
                kg = jnp.concatenate([_prev_block(kp_ref, kc_ref, sub, sl), kc_ref[rows, sl]], axis=0)
                vg = jnp.concatenate([_prev_block(vp_ref, vc_ref, sub, sl), vc_ref[rows, sl]], axis=0)
                qs += [jnp.where(lo, qg, jnp.zeros_like(qg)), jnp.where(lo, jnp.zeros_like(qg), qg)]
                ks += [kg, kg]
                vs += [vg, vg]
            s = jnp.where(valid, _hnt(jnp.stack(qs), jnp.stack(ks)) + bias, NEG_BIG)
            m = jnp.max(s, axis=-1, keepdims=True)
            p = jnp.exp(s - m)
            l = jnp.sum(p, axis=-1, keepdims=True)
            o = _hnn(p.astype(BF16), jnp.stack(vs)) / l
            lse = m + jnp.log(l)
            for G in range(4):
                sl = slice(G * 128, (G + 1) * 128)
                o_ref[rows, sl] = jnp.where(lo, o[2 * G], o[2 * G + 1])
                lse_ref[rows, sl] = jnp.where(lo, lse[2 * G], lse[2 * G + 1])

    cur, prev = _query_step_specs(QB)
    return pl.pallas_call(
        body, name=name, grid=(d, nb // QB),
        in_specs=[cur, prev, cur, prev, cur],
        out_specs=[cur, cur],
        out_shape=[jax.ShapeDtypeStruct((L, d * ATTN_WIDTH), F32)] * 2,
        compiler_params=_params(2),
    )(q, k, k, v, v)


def _attn_merge(parts):
    S = parts[0][0].shape[0]
    tm = VIEW_TILE

    def body(o1, s1, o2, s2, o3, s3, o_ref, lse1, lse4, lse16, planes):
        outs, lses = [], []
        for d, (o, s) in zip(DILATIONS, ((o1, s1), (o2, s2), (o3, s3))):
            outs.append(_view_to_tile(o, d, planes))
            lses.append(_view_to_tile(s, d, planes))
        mx = jnp.maximum(jnp.maximum(lses[0], lses[1]), lses[2])
        es = [jnp.exp(s - mx) for s in lses]
        den = es[0] + es[1] + es[2]
        o_ref[...] = (es[0] * outs[0] + es[1] * outs[1] + es[2] * outs[2]) / den
        _tile_to_views(mx + jnp.log(den), planes, (lse1, lse4, lse16))

    views = [_view_spec(d) for d in DILATIONS]
    flat = [t for p in parts for t in p]
    return pl.pallas_call(
        body, name="attn_merge", grid=(S // tm,),
        in_specs=[views[p] for p in range(3) for _ in range(2)],
        out_specs=[views[0]] + views,
        out_shape=[jax.ShapeDtypeStruct((S, ATTN_WIDTH), F32)] + [_view_shape(S, d, F32) for d in DILATIONS],
        scratch_shapes=[pltpu.VMEM((4, tm, 128), F32)],
        compiler_params=_params(1),
    )(*flat)


def _head_col(t, msk, big):
    if big:
        return jnp.max(jnp.where(msk, t, NEG_BIG), axis=-1, keepdims=True)
    return jnp.sum(jnp.where(msk, t, 0.0), axis=-1, keepdims=True) * (1.0 / 64.0)


def _attn_bwd_q(q, k, v, do, lse, dd, d, name):
    L = q.shape[0]
    nb = L // ATTN_BLOCK
    B = ATTN_BLOCK
    QB = _blocks_per_step(nb)

    def body(q_ref, kp_ref, kc_ref, vp_ref, vc_ref, do_ref, lse_ref, dd_ref, dq_ref):
        n = pl.program_id(1)
        qi = lax.broadcasted_iota(jnp.int32, (B, 2 * B), 0)
        kj = lax.broadcasted_iota(jnp.int32, (B, 2 * B), 1)
        steps = qi + B - kj
        band = (steps >= 0) & (steps <= B)
        lo = lax.broadcasted_iota(jnp.int32, (B, 128), 1) < 64
        bias = _head_bias(steps, d)
        for sub in range(QB):
            rows = slice(sub * B, (sub + 1) * B)
            valid = band & ((kj >= B) | (n > 0)) if sub == 0 else band
            qs, ks, vs, dos, lses, dcols = [], [], [], [], [], []
            for G in range(4):
                sl = slice(G * 128, (G + 1) * 128)
                qg = q_ref[rows, sl]
                kg = jnp.concatenate([_prev_block(kp_ref, kc_ref, sub, sl), kc_ref[rows, sl]], axis=0)
                vg = jnp.concatenate([_prev_block(vp_ref, vc_ref, sub, sl), vc_ref[rows, sl]], axis=0)
                dog = do_ref[rows, sl]
                qs += [jnp.where(lo, qg, jnp.zeros_like(qg)), jnp.where(lo, jnp.zeros_like(qg), qg)]
                dos += [jnp.where(lo, dog, 0.0).astype(BF16), jnp.where(lo, 0.0, dog).astype(BF16)]
                ks += [kg, kg]
                vs += [vg, vg]
                lses += _head_cols(lse_ref[rows, sl], lo, True)
                dcols += _head_cols(dd_ref[rows, sl], lo, False)
            kb = jnp.stack(ks)
            s = _hnt(jnp.stack(qs), kb) + bias
            p = jnp.where(valid, jnp.exp(jnp.where(valid, s, NEG_BIG) - jnp.stack(lses)), 0.0)
            dp = _hnt(jnp.stack(dos), jnp.stack(vs))
            ds = p * (dp - jnp.stack(dcols))
            dq = _hnn(ds.astype(BF16), kb) * ATTN_SCALE
            for G in range(4):
                dq_ref[rows, G * 128:(G + 1) * 128] = jnp.where(lo, dq[2 * G], dq[2 * G + 1]).astype(BF16)

    cur, prev = _query_step_specs(QB)
    return pl.pallas_call(
        body, name=name, grid=(d, nb // QB), in_specs=[cur, prev, cur, prev, cur, cur, cur, cur], out_specs=cur,
        out_shape=jax.ShapeDtypeStruct((L, d * ATTN_WIDTH), BF16), compiler_params=_params(2),
    )(q, k, k, v, v, do, lse, dd)


def _attn_bwd_kv(q, k, v, do, lse, dd, d, name):
    L = q.shape[0]
    nb = L // ATTN_BLOCK
    B = ATTN_BLOCK
    KB = _blocks_per_step(nb)
    n_steps = nb // KB

    def body(k_ref, v_ref, qc_ref, qn_ref, doc_ref, don_ref, lsec_ref, lsen_ref, ddc_ref, ddn_ref, dk_ref, dv_ref):
        j = pl.program_id(1)
        qrow = lax.broadcasted_iota(jnp.int32, (2 * B, B), 0)
        kk = lax.broadcasted_iota(jnp.int32, (2 * B, B), 1)
        steps = qrow - kk
        band = (steps >= 0) & (steps <= B)
        lo2 = lax.broadcasted_iota(jnp.int32, (2 * B, 128), 1) < 64
        lo = lax.broadcasted_iota(jnp.int32, (B, 128), 1) < 64
        stepsf = steps.astype(F32)
        for sub in range(KB):
            rows = slice(sub * B, (sub + 1) * B)
            last = sub == KB - 1
            valid = band & ((qrow < B) | (j < n_steps - 1)) if last else band
            after = lambda cur_ref, nxt_ref, sl: nxt_ref[:, sl] if last else cur_ref[(sub + 1) * B:(sub + 2) * B, sl]
            for G in range(4):
                sl = slice(G * 128, (G + 1) * 128)
                kg = k_ref[rows, sl]
                vg = v_ref[rows, sl]
                qq = jnp.concatenate([qc_ref[rows, sl], after(qc_ref, qn_ref, sl)], axis=0)
                doo = jnp.concatenate([doc_ref[rows, sl], after(doc_ref, don_ref, sl)], axis=0)
                lse2 = jnp.concatenate([lsec_ref[rows, sl], after(lsec_ref, lsen_ref, sl)], axis=0)
                dd2 = jnp.concatenate([ddc_ref[rows, sl], after(ddc_ref, ddn_ref, sl)], axis=0)
                doo_b = doo.astype(BF16)
                dks, dvs = [], []
                for half in (0, 1):
                    msk = lo2 if half == 0 else jnp.logical_not(lo2)
                    qm = jnp.where(msk, qq, jnp.zeros_like(qq))
                    s = _nt(qm, kg) - (_slope(2 * G + half) * d) * stepsf
                    lse_c = _head_col(lse2, msk, True)
                    p = jnp.where(valid, jnp.exp(jnp.where(valid, s, NEG_BIG) - lse_c), 0.0)
                    dvs.append(_tn(p.astype(BF16), doo_b))
                    dom = jnp.where(msk, doo, 0.0).astype(BF16)
                    dp = _nt(dom, vg)
                    dcol = _head_col(dd2, msk, False)
                    ds = p * (dp - dcol)
                    dks.append(_tn(ds.astype(BF16), qq))
                dk_ref[rows, sl] = jnp.where(lo, dks[0], dks[1]).astype(BF16)
                dv_ref[rows, sl] = jnp.where(lo, dvs[0], dvs[1]).astype(BF16)

    cur = pl.BlockSpec((KB * B, ATTN_WIDTH), lambda r, j: (j, r))
    nxt = pl.BlockSpec((B, ATTN_WIDTH), lambda r, j: (jnp.minimum(KB * (j + 1), nb - 1), r))
    return pl.pallas_call(
        body, name=name, grid=(d, n_steps), in_specs=[cur, cur, cur, nxt, cur, nxt, cur, nxt, cur, nxt],
        out_specs=[cur, cur],
        out_shape=[jax.ShapeDtypeStruct((L, d * ATTN_WIDTH), BF16)] * 2, compiler_params=_params(2),
    )(k, v, q, q, do, do, lse, lse, dd, dd)


CONV_T = 512
HALO = 8


def _per_head(head, refs):
    for h in range(DN_HEADS):
        lanes = pl.ds(h * DN_HEAD_DIM, DN_HEAD_DIM)
        head(*[r.at[:, lanes] for r in refs[:-1]], refs[-1])


def _conv_taps(pad_ref, w, T):
    acc = pad_ref[pl.ds(HALO - 3, T), :] * w[0:1, :]
    for j in range(1, CONV_WIDTH):
        acc = acc + pad_ref[pl.ds(HALO - 3 + j, T), :] * w[j:j + 1, :]
    return acc


def _conv_fwd(xq, xk, xv, conv_w):
    S = xq.shape[0]
    T = CONV_T

    def body(*refs):
        _per_head(head, refs)

    def head(xq_ref, xqh_ref, xk_ref, xkh_ref, xv_ref, xvh_ref, wq_ref, wk_ref, wv_ref,
             qn_ref, kn_ref, v_ref, pad_ref):
        i = pl.program_id(0)

        def act(x_ref, xh_ref, w_ref):
            pad_ref[pl.ds(0, HALO), :] = jnp.where(i > 0, xh_ref[...], 0.0)
            pad_ref[pl.ds(HALO, T), :] = x_ref[...]
            c = _conv_taps(pad_ref, w_ref[...], T)
            return c * _sigmoid(c)

        def l2n(t):
            return t * lax.rsqrt(jnp.sum(t * t, axis=-1, keepdims=True) + L2_EPS)

        qn_ref[...] = l2n(act(xq_ref, xqh_ref, wq_ref))
        kn_ref[...] = l2n(act(xk_ref, xkh_ref, wk_ref))
        v_ref[...] = act(xv_ref, xvh_ref, wv_ref)

    tile = pl.BlockSpec((T, DN_WIDTH), lambda i: (i, 0))
    halo = pl.BlockSpec((HALO, DN_WIDTH), lambda i: (jnp.maximum(i * (T // HALO) - 1, 0), 0))
    wspec = lambda sec: pl.BlockSpec((CONV_WIDTH, DN_WIDTH), lambda i, sec=sec: (0, sec))
    return pl.pallas_call(
        body, name="dn_conv_fwd", grid=(S // T,),
        in_specs=[tile, halo, tile, halo, tile, halo, wspec(0), wspec(1), wspec(2)],
        out_specs=[tile, tile, tile],
        out_shape=[jax.ShapeDtypeStruct((S, DN_WIDTH), F32)] * 3,
        scratch_shapes=[pltpu.VMEM((T + HALO, 128), F32)],
        compiler_params=_params(1),
    )(xq, xq, xk, xk, xv, xv, conv_w, conv_w, conv_w)


def _conv_bwd_pre(xq, xk, xv, conv_w, dqn, dkn, dv):
    S = xq.shape[0]
    T = CONV_T

    def body(*refs):
        _per_head(head, refs)

    def head(xq_ref, xqh_ref, xk_ref, xkh_ref, xv_ref, xvh_ref, wq_ref, wk_ref, wv_ref,
             dqn_ref, dkn_ref, dv_ref, dcq_ref, dck_ref, dcv_ref, dwq_ref, dwk_ref, dwv_ref, pad_ref):
        i = pl.program_id(0)

        def one(x_ref, xh_ref, w_ref, dy_ref, dc_ref, dw_ref, normed):
            pad_ref[pl.ds(0, HALO), :] = jnp.where(i > 0, xh_ref[...], 0.0)
            pad_ref[pl.ds(HALO, T), :] = x_ref[...]
            c = _conv_taps(pad_ref, w_ref[...], T)
            sg = _sigmoid(c)
            a = c * sg
            dy = dy_ref[...]
            if normed:
                r = lax.rsqrt(jnp.sum(a * a, axis=-1, keepdims=True) + L2_EPS)
                y = a * r
                da = r * (dy - y * jnp.sum(dy * y, axis=-1, keepdims=True))
            else:
                da = dy
            dc = da * (sg * (1.0 + c * (1.0 - sg)))
            dc_ref[...] = dc

            @pl.when(i == 0)
            def _():
                dw_ref[...] = jnp.zeros_like(dw_ref)

            rows = [jnp.sum(dc * pad_ref[pl.ds(HALO - 3 + j, T), :], axis=0, keepdims=True) for j in range(CONV_WIDTH)]
            dw_ref[...] += jnp.concatenate(rows + [jnp.zeros((8 - CONV_WIDTH, 128), F32)], axis=0)

        one(xq_ref, xqh_ref, wq_ref, dqn_ref, dcq_ref, dwq_ref, True)
        one(xk_ref, xkh_ref, wk_ref, dkn_ref, dck_ref, dwk_ref, True)
        one(xv_ref, xvh_ref, wv_ref, dv_ref, dcv_ref, dwv_ref, False)

    tile = pl.BlockSpec((T, DN_WIDTH), lambda i: (i, 0))
    halo = pl.BlockSpec((HALO, DN_WIDTH), lambda i: (jnp.maximum(i * (T // HALO) - 1, 0), 0))
    wspec = lambda sec: pl.BlockSpec((CONV_WIDTH, DN_WIDTH), lambda i, sec=sec: (0, sec))
    dwspec = pl.BlockSpec((8, DN_WIDTH), lambda i: (0, 0))
    return pl.pallas_call(
        body, name="dn_conv_bwd_pre", grid=(S // T,),
        in_specs=[tile, halo, tile, halo, tile, halo, wspec(0), wspec(1), wspec(2), tile, tile, tile],
        out_specs=[tile, tile, tile, dwspec, dwspec, dwspec],
        out_shape=[jax.ShapeDtypeStruct((S, DN_WIDTH), F32)] * 3 + [jax.ShapeDtypeStruct((8, DN_WIDTH), F32)] * 3,
        scratch_shapes=[pltpu.VMEM((T + HALO, 128), F32)],
        compiler_params=_params(1),
    )(xq, xq, xk, xk, xv, xv, conv_w, conv_w, conv_w, dqn, dkn, dv)


def _conv_bwd_x(dcq, dck, dcv, conv_w):
    S = dcq.shape[0]
    T = CONV_T
    nt = S // T

    def body(*refs):
        _per_head(head, refs)

    def head(dq_ref, dqh_ref, dk_ref, dkh_ref, dv_ref, dvh_ref, wq_ref, wk_ref, wv_ref,
             oq_ref, ok_ref, ov_ref, pad_ref):
        i = pl.program_id(0)

        def one(d_ref, dh_ref, w_ref, o_ref):
            pad_ref[pl.ds(0, T), :] = d_ref[...]
            pad_ref[pl.ds(T, HALO), :] = jnp.where(i < nt - 1, dh_ref[...], 0.0)
            w = w_ref[...]
            acc = pad_ref[pl.ds(3, T), :] * w[0:1, :]
            for j in range(1, CONV_WIDTH):
                acc = acc + pad_ref[pl.ds(3 - j, T), :] * w[j:j + 1, :]
            o_ref[...] = acc

        one(dq_ref, dqh_ref, wq_ref, oq_ref)
        one(dk_ref, dkh_ref, wk_ref, ok_ref)
        one(dv_ref, dvh_ref, wv_ref, ov_ref)

    tile = pl.BlockSpec((T, DN_WIDTH), lambda i: (i, 0))
    halo = pl.BlockSpec((HALO, DN_WIDTH), lambda i: (jnp.minimum((i + 1) * (T // HALO), S // HALO - 1), 0))
    wspec = lambda sec: pl.BlockSpec((CONV_WIDTH, DN_WIDTH), lambda i, sec=sec: (0, sec))
    return pl.pallas_call(
        body, name="dn_conv_bwd_x", grid=(nt,),
        in_specs=[tile, halo, tile, halo, tile, halo, wspec(0), wspec(1), wspec(2)],
        out_specs=[tile, tile, tile],
        out_shape=[jax.ShapeDtypeStruct((S, DN_WIDTH), F32)] * 3,
        scratch_shapes=[pltpu.VMEM((T + HALO, 128), F32)],
        compiler_params=_params(1),
    )(dcq, dcq, dck, dck, dcv, dcv, conv_w, conv_w, conv_w)


PREP_CHUNKS = 4
SCAN_CHUNKS = 8


def _bnn(a, b):
    return lax.dot_general(a, b, (((2,), (1,)), ((0,), (0,))), preferred_element_type=F32, precision=HI)


def _bnt(a, b):
    return lax.dot_general(a, b, (((2,), (2,)), ((0,), (0,))), preferred_element_type=F32, precision=HI)


def _btn(a, b):
    return lax.dot_general(a, b, (((1,), (1,)), ((0,), (0,))), preferred_element_type=F32, precision=HI)


def _tri_inverse_b(a, blk, eye):
    dg = jnp.where(blk, a, 0.0)
    lo = a - dg
    d2 = _bnn(dg, dg)
    d4 = _bnn(d2, d2)
    d8 = _bnn(d4, d4)
    td = _bnn(_bnn(_bnn(eye - dg, eye + d2), eye + d4), eye + d8)
    b = _bnn(td, lo)
    b2 = _bnn(b, b)
    return _bnn(_bnn(eye - b, eye + b2), td)


def _dn_common_b(bds, avec, dvec, q_raw, k, v, saved=None):
    C = DN_CHUNK
    lane = lax.broadcasted_iota(jnp.int32, (C, 128), 1)
    row = lax.broadcasted_iota(jnp.int32, (1, C, C), 1)
    col = lax.broadcasted_iota(jnp.int32, (1, C, C), 2)
    incl = row >= col
    strict = row > col
    eye = (row == col).astype(F32)
    blk = (row // 16) == (col // 16)
    pick = lambda tile, ln: jnp.sum(jnp.where(lane == ln, tile, 0.0), axis=-1, keepdims=True)
    betas, graws, zcs = [], [], []
    for bd in bds:
        z = bd + dvec
        g_all = -jnp.exp(avec) * (jnp.maximum(z, 0.0) + jnp.log(1.0 + jnp.exp(-jnp.abs(z))))
        beta_all = _sigmoid(bd)
        for h in range(DN_HEADS):
            betas.append(pick(beta_all, h))
            graws.append(pick(g_all, DN_HEADS + h))
            zcs.append(pick(z, DN_HEADS + h))
    beta, graw, zc = jnp.stack(betas), jnp.stack(graws), jnp.stack(zcs)
    to_row = lambda c: jnp.sum(eye * c, axis=1, keepdims=True)
    gc = jnp.sum(jnp.where(incl, to_row(graw), 0.0), axis=-1, keepdims=True)
    decay = jnp.exp(jnp.where(incl, gc - to_row(gc), NEG_BIG))
    q = q_raw * (DN_HEAD_DIM ** -0.5)
    kb = k * beta
    kk = _bnt(kb, k)
    eg = jnp.exp(gc)
    rhs_w = kb * eg
    if saved is None:
        t = _tri_inverse_b(jnp.where(strict, kk * decay, 0.0), blk, eye)
        u = _bnn(t, v * beta)
        w = _bnn(t, rhs_w)
        aq = jnp.where(incl, _bnt(q, k) * decay, 0.0)
    else:
        t, u, w, aq = saved
    last = lax.broadcasted_iota(jnp.int32, (1, C, 1), 1) == C - 1
    g_last = jnp.sum(jnp.where(last, gc, 0.0), axis=1, keepdims=True)
    ekd = jnp.exp(g_last - gc)
    return dict(beta=beta, graw=graw, zc=zc, gc=gc, decay=decay, q=q, kb=kb, kk=kk, t=t, eg=eg, rhs_w=rhs_w,
                u=u, w=w, aq=aq, g_last=g_last, ekd=ekd, kd=k * ekd, qg=q * eg,
                incl=incl, strict=strict, eye=eye, lane=lane, row=row, col=col, last=last)


def _stack_heads(ref, rows):
    return jnp.stack([ref[rows, h * DN_HEAD_DIM:(h + 1) * DN_HEAD_DIM] for h in range(DN_HEADS)])


def _stack_units(ref, nc):
    C = DN_CHUNK
    return jnp.concatenate([_stack_heads(ref, slice(ci * C, (ci + 1) * C)) for ci in range(nc)], axis=0)


def _store_units(ref, val, nc):
    C = DN_CHUNK
    for ci in range(nc):
        for h in range(DN_HEADS):
            ref[ci * C:(ci + 1) * C, h * DN_HEAD_DIM:(h + 1) * DN_HEAD_DIM] = val[ci * DN_HEADS + h]


def _dn_prep(qn, kn, v, bd, avec, dvec):
    S = qn.shape[0]
    C = DN_CHUNK
    N = S // C
    nc = PREP_CHUNKS

    def body(q_ref, k_ref, v_ref, bd_ref, a_ref, d_ref, u_ref, w_ref, qg_ref, kd_ref, aq_ref, t_ref, egl_ref):
        bds = [bd_ref[ci * C:(ci + 1) * C, :] for ci in range(nc)]
        c = _dn_common_b(bds, a_ref[...], d_ref[...], _stack_units(q_ref, nc), _stack_units(k_ref, nc), _stack_units(v_ref, nc))
        _store_units(u_ref, c["u"], nc)
        _store_units(w_ref, c["w"], nc)
        _store_units(qg_ref, c["qg"], nc)
        _store_units(kd_ref, c["kd"], nc)
        egl = jnp.broadcast_to(jnp.exp(c["g_last"]), (nc * DN_HEADS, 1, 128))
        for ci in range(nc):
            for h in range(DN_HEADS):
                aq_ref[h, ci * C:(ci + 1) * C, :] = c["aq"][ci * DN_HEADS + h]
                t_ref[h, ci * C:(ci + 1) * C, :] = c["t"][ci * DN_HEADS + h]
            egl_ref[ci * 8:(ci + 1) * 8, :] = jnp.concatenate(
                [egl[ci * DN_HEADS + h] for h in range(DN_HEADS)] + [jnp.zeros((8 - DN_HEADS, 128), F32)], axis=0)

    tok = lambda w: pl.BlockSpec((nc * C, w), lambda n: (n, 0))
    sq = pl.BlockSpec((DN_HEADS, nc * C, C), lambda n: (0, n, 0))
    vec = pl.BlockSpec((1, 128), lambda n: (0, 0))
    return pl.pallas_call(
        body, name="dn_prep", grid=(N // nc,),
        in_specs=[tok(DN_WIDTH)] * 3 + [tok(128), vec, vec],
        out_specs=[tok(DN_WIDTH)] * 4 + [sq, sq, pl.BlockSpec((nc * 8, 128), lambda n: (n, 0))],
        out_shape=[jax.ShapeDtypeStruct((S, DN_WIDTH), F32)] * 4 + [jax.ShapeDtypeStruct((DN_HEADS, S, C), F32)] * 2
                  + [jax.ShapeDtypeStruct((N * 8, 128), F32)],
        compiler_params=_params(1),
    )(qn, kn, v, bd, avec, dvec)


def _dn_scan_fwd(u, w, qg, kd, aq, egl, gate, dn_gain):
    S = u.shape[0]
    C = DN_CHUNK
    N = S // C
    HD = DN_HEAD_DIM
    nc = SCAN_CHUNKS

    def body(u_ref, w_ref, qg_ref, kd_ref, aq_ref, egl_ref, gate_ref, gain_ref, dn_ref, o_ref, vn_ref, st_ref, state_ref):
        @pl.when(pl.program_id(0) == 0)
        def _():
            state_ref[...] = jnp.zeros_like(state_ref)

        gain = gain_ref[...]
        for ci in range(nc):
            rows = slice(ci * C, (ci + 1) * C)
            st = state_ref[...]
            for h in range(DN_HEADS):
                st_ref[ci * DN_WIDTH + h * HD:ci * DN_WIDTH + (h + 1) * HD, :] = st[h]
            v_new = _stack_heads(u_ref, rows) - _bnn(_stack_heads(w_ref, rows), st)
            o = _bnn(_stack_heads(qg_ref, rows), st) + _bnn(aq_ref[:, rows, :], v_new)
            egl = jnp.stack([egl_ref[ci * 8 + h:ci * 8 + h + 1, :] for h in range(DN_HEADS)])
            state_ref[...] = st * egl + _btn(_stack_heads(kd_ref, rows), v_new)
            r = lax.rsqrt(jnp.mean(o * o, axis=-1, keepdims=True) + NORM_EPS)
            gt = _stack_heads(gate_ref, rows)
            dn = o * r * gain * (gt * _sigmoid(gt))
            for h in range(DN_HEADS):
                sl = slice(h * HD, (h + 1) * HD)
                vn_ref[rows, sl] = v_new[h]
                o_ref[rows, sl] = o[h]
                dn_ref[rows, sl] = dn[h]

    tok = lambda wd: pl.BlockSpec((nc * C, wd), lambda n: (n, 0))
    sq = pl.BlockSpec((DN_HEADS, nc * C, C), lambda n: (0, n, 0))
    vec = pl.BlockSpec((1, 128), lambda n: (0, 0))
    return pl.pallas_call(
        body, name="dn_scan_fwd", grid=(N // nc,),
        in_specs=[tok(DN_WIDTH)] * 4 + [sq, pl.BlockSpec((nc * 8, 128), lambda n: (n, 0)), tok(DN_WIDTH), vec],
        out_specs=[tok(DN_WIDTH)] * 3 + [pl.BlockSpec((nc * DN_WIDTH, HD), lambda n: (n, 0))],
        out_shape=[jax.ShapeDtypeStruct((S, DN_WIDTH), F32)] * 3 + [jax.ShapeDtypeStruct((N * DN_WIDTH, HD), F32)],
        scratch_shapes=[pltpu.VMEM((DN_HEADS, HD, HD), F32)],
        compiler_params=_params(1),
    )(u, w, qg, kd, aq, egl, gate, dn_gain)


def _dn_scan_bwd(w, qg, kd, aq, egl, gate, dn_gain, o, ddn):
    S = w.shape[0]
    C = DN_CHUNK
    N = S // C
    HD = DN_HEAD_DIM
    nc = SCAN_CHUNKS

    def body(w_ref, qg_ref, kd_ref, aq_ref, egl_ref, gate_ref, gain_ref, o_ref, ddn_ref,
             do_ref, dvn_ref, dgate_ref, dst_ref, small_ref, dstate_ref):
        @pl.when(pl.program_id(0) == 0)
        def _():
            dstate_ref[...] = jnp.zeros_like(dstate_ref)
            small_ref[...] = jnp.zeros_like(small_ref)

        gain = gain_ref[...]
        d_gain = jnp.zeros((1, 128), F32)
        for ci in reversed(range(nc)):
            rows = slice(ci * C, (ci + 1) * C)
            dsn = dstate_ref[...]
            for h in range(DN_HEADS):
                dst_ref[ci * DN_WIDTH + h * HD:ci * DN_WIDTH + (h + 1) * HD, :] = dsn[h]
            ov = _stack_heads(o_ref, rows)
            r = lax.rsqrt(jnp.mean(ov * ov, axis=-1, keepdims=True) + NORM_EPS)
            on = ov * r
            gt = _stack_heads(gate_ref, rows)
            sgt = _sigmoid(gt)
            silu_g = gt * sgt
            dy = _stack_heads(ddn_ref, rows)
            d_gain = d_gain + jnp.sum(jnp.sum(dy * on * silu_g, axis=1, keepdims=True), axis=0)
            dgate = dy * on * gain * (sgt * (1.0 + gt * (1.0 - sgt)))
            don = dy * gain * silu_g
            do = r * (don - on * jnp.mean(don * on, axis=-1, keepdims=True))
            wv = _stack_heads(w_ref, rows)
            kdv = _stack_heads(kd_ref, rows)
            a_do = _btn(aq_ref[:, rows, :], do)
            free = _btn(_stack_heads(qg_ref, rows), do) - _btn(wv, a_do)
            w_kd = _btn(wv, kdv)
            d_vnew = a_do + _bnn(kdv, dsn)
            egl = jnp.stack([egl_ref[ci * 8 + h:ci * 8 + h + 1, :] for h in range(DN_HEADS)])
            dstate_ref[...] = free + dsn * egl - _bnn(w_kd, dsn)
            for h in range(DN_HEADS):
                sl = slice(h * HD, (h + 1) * HD)
                do_ref[rows, sl] = do[h]
                dvn_ref[rows, sl] = d_vnew[h]
                dgate_ref[rows, sl] = dgate[h]
        small_ref[...] += jnp.concatenate([d_gain, jnp.zeros((7, 128), F32)], axis=0)

    nb = N // nc
    tok = lambda wd: pl.BlockSpec((nc * C, wd), lambda i: (nb - 1 - i, 0))
    sq = pl.BlockSpec((DN_HEADS, nc * C, C), lambda i: (0, nb - 1 - i, 0))
    vec = pl.BlockSpec((1, 128), lambda i: (0, 0))
    return pl.pallas_call(
        body, name="dn_scan_bwd", grid=(nb,),
        in_specs=[tok(DN_WIDTH)] * 3 + [sq, pl.BlockSpec((nc * 8, 128), lambda i: (nb - 1 - i, 0)), tok(DN_WIDTH), vec,
                                       tok(DN_WIDTH), tok(DN_WIDTH)],
        out_specs=[tok(DN_WIDTH)] * 3 + [pl.BlockSpec((nc * DN_WIDTH, HD), lambda i: (nb - 1 - i, 0)),
                                        pl.BlockSpec((8, 128), lambda i: (0, 0))],
        out_shape=[jax.ShapeDtypeStruct((S, DN_WIDTH), F32)] * 3 + [jax.ShapeDtypeStruct((N * DN_WIDTH, HD), F32),
                                                                  jax.ShapeDtypeStruct((8, 128), F32)],
        scratch_shapes=[pltpu.VMEM((DN_HEADS, HD, HD), F32)],
        compiler_params=_params(1),
    )(w, qg, kd, aq, egl, gate, dn_gain, o, ddn)


def _dn_post(qn, kn, v, bd, avec, dvec, t_inv, u_all, w_all, aq_all, v_new_all, states, dstates, do_all, dvn_all, comm=None):
    S = qn.shape[0]
    C = DN_CHUNK
    N = S // C
    HD = DN_HEAD_DIM
    nc = PREP_CHUNKS
    B = nc * DN_HEADS

    def body(q_ref, k_ref, v_ref, bd_ref, a_ref, d_ref, t_ref, u_ref, w_ref, aq_ref, vn_ref, st_ref, dst_ref, do_ref, dvn_ref,
             dq_ref, dk_ref, dv_ref, dbd_ref, small_ref):
        @pl.when(pl.program_id(0) == 0)
        def _():
            small_ref[...] = jnp.zeros_like(small_ref)

        avec = a_ref[...]
        bds = [bd_ref[ci * C:(ci + 1) * C, :] for ci in range(nc)]
        k = _stack_units(k_ref, nc)
        vv = _stack_units(v_ref, nc)
        squares = lambda ref: jnp.concatenate([ref[:, ci * C:(ci + 1) * C, :] for ci in range(nc)], axis=0)
        saved = (squares(t_ref), _stack_units(u_ref, nc), _stack_units(w_ref, nc), squares(aq_ref))
        c = _dn_common_b(bds, avec, d_ref[...], _stack_units(q_ref, nc), k, vv, saved=saved)
        q, kb, eg, t, u, w = c["q"], c["kb"], c["eg"], c["t"], c["u"], c["w"]
        beta, decay, incl, strict, eye = c["beta"], c["decay"], c["incl"], c["strict"], c["eye"]
        st = jnp.stack([st_ref[b * HD:(b + 1) * HD, :] for b in range(B)])
        dsn = jnp.stack([dst_ref[b * HD:(b + 1) * HD, :] for b in range(B)])
        v_new = _stack_units(vn_ref, nc)
        do = _stack_units(do_ref, nc)
        d_vnew = _stack_units(dvn_ref, nc)
        egl = jnp.exp(c["g_last"])
        daq = jnp.where(incl, _bnt(do, v_new), 0.0)
        d_qg = _bnt(do, st)
        d_kd = _bnt(v_new, dsn)
        d_glast = jnp.sum(jnp.sum(dsn * st, axis=-1, keepdims=True), axis=1, keepdims=True) * egl
        d_w = -_bnt(d_vnew, st)
        d_ru = _btn(t, d_vnew)
        d_rw = _btn(t, d_w)
        da = -jnp.where(strict, _bnt(d_ru, u) + _bnt(d_rw, w), 0.0)
        dv = d_ru * beta
        dbeta = jnp.sum(d_ru * vv, axis=-1, keepdims=True)
        dkb = d_rw * eg
        dgc = jnp.sum(d_rw * c["rhs_w"], axis=-1, keepdims=True)
        dkk = da * decay
        ddecay = da * c["kk"]
        dkb = dkb + _bnn(dkk, k)
        dk = _btn(dkk, kb)
        dqk = daq * decay
        dq = _bnn(dqk, k)
        dk = dk + _btn(dqk, q)
        m = ddecay * decay + daq * c["aq"]
        col_sum = jnp.sum(m, axis=1, keepdims=True)
        dgc = dgc + jnp.sum(m, axis=-1, keepdims=True) - jnp.sum(eye * col_sum, axis=-1, keepdims=True)
        dq = dq + d_qg * eg
        dgc = dgc + jnp.sum(d_qg * c["qg"], axis=-1, keepdims=True)
        dk = dk + d_kd * c["ekd"]
        tk = jnp.sum(d_kd * c["kd"], axis=-1, keepdims=True)
        dgc = dgc - tk
        d_glast = d_glast + jnp.sum(tk, axis=1, keepdims=True)
        dk = dk + dkb * beta
        dbeta = dbeta + jnp.sum(dkb * k, axis=-1, keepdims=True)
        dgc = dgc + jnp.where(c["last"], d_glast, 0.0)
        dgc_row = jnp.sum(eye * dgc, axis=1, keepdims=True)
        dgraw = jnp.sum(jnp.where(c["col"] >= c["row"], dgc_row, 0.0), axis=-1, keepdims=True)
        _store_units(dq_ref, dq * (HD ** -0.5), nc)
        _store_units(dk_ref, dk, nc)
        _store_units(dv_ref, dv, nc)
        dbraw = dbeta * beta * (1.0 - beta)
        dzc = dgraw * _sigmoid(c["zc"])
        ga = dgraw * c["graw"]
        lane = c["lane"]
        lane1 = lax.broadcasted_iota(jnp.int32, (1, 128), 1)
        neg_ea = -jnp.exp(avec)
        d_alog = jnp.zeros((1, 128), F32)
        d_dt = jnp.zeros((1, 128), F32)
        for ci in range(nc):
            dbd = jnp.zeros((C, 128), F32)
            for h in range(DN_HEADS):
                b = ci * DN_HEADS + h
                dz = dzc[b] * neg_ea
                dbd = dbd + jnp.where(lane == h, dbraw[b], 0.0) + jnp.where(lane == DN_HEADS + h, dz, 0.0)
                d_alog = d_alog + jnp.where(lane1 == DN_HEADS + h, jnp.sum(ga[b], axis=0, keepdims=True), 0.0)
                d_dt = d_dt + jnp.where(lane1 == DN_HEADS + h, jnp.sum(dz, axis=0, keepdims=True), 0.0)
            dbd_ref[ci * C:(ci + 1) * C, :] = dbd
        small_ref[...] += jnp.concatenate([d_alog, d_dt, jnp.zeros((6, 128), F32)], axis=0)

    tok = lambda wd: pl.BlockSpec((nc * C, wd), lambda n: (n, 0))
    big = pl.BlockSpec((nc * DN_WIDTH, HD), lambda n: (n, 0))
    sq = pl.BlockSpec((DN_HEADS, nc * C, C), lambda n: (0, n, 0))
    vec = pl.BlockSpec((1, 128), lambda n: (0, 0))
    return _call(
        body, (qn, kn, v, bd, avec, dvec, t_inv, u_all, w_all, aq_all, v_new_all, states, dstates, do_all, dvn_all),
        name="dn_post", grid=(N // nc,), comm=comm,
        in_specs=[tok(DN_WIDTH)] * 3 + [tok(128), vec, vec, sq, tok(DN_WIDTH), tok(DN_WIDTH), sq, tok(DN_WIDTH), big, big,
                  tok(DN_WIDTH), tok(DN_WIDTH)],
        out_specs=[tok(DN_WIDTH)] * 3 + [tok(128), pl.BlockSpec((8, 128), lambda n: (0, 0))],
        out_shape=[jax.ShapeDtypeStruct((S, DN_WIDTH), F32)] * 3 + [jax.ShapeDtypeStruct((S, 128), F32),
                                                                  jax.ShapeDtypeStruct((8, 128), F32)])


def _outproj_fwd(x, attn, dn, w_out):
    S, D = x.shape
    tm = 512

    def body(x_ref, a_ref, d_ref, w_ref, xo_ref, mix_ref):
        a = a_ref[...].astype(BF16)
        dd = d_ref[...].astype(BF16)
        mix_ref[:, 0:ATTN_WIDTH] = a
        mix_ref[:, ATTN_WIDTH:] = dd
        xo_ref[...] = x_ref[...] + _nn(a, w_ref[0:ATTN_WIDTH, :]) + _nn(dd, w_ref[ATTN_WIDTH:, :])

    tok = lambda w: pl.BlockSpec((tm, w), lambda i: (i, 0))
    return pl.pallas_call(
        body, name="outproj_fwd", grid=(S // tm,),
        in_specs=[tok(D), tok(ATTN_WIDTH), tok(DN_WIDTH), pl.BlockSpec((D, D), lambda i: (0, 0))],
        out_specs=[tok(D), tok(D)],
        out_shape=[jax.ShapeDtypeStruct((S, D), F32), jax.ShapeDtypeStruct((S, D), BF16)],
        compiler_params=_params(1),
    )(x, attn, dn, w_out)


def _outproj_bwd(dx, w_out, attn, comm=None):
    S, D = dx.shape
    tm = VIEW_TILE

    def body(dx_ref, w_ref, attn_ref, da1, da4, da16, dl1, dl4, dl16, ddn_ref, dxb_ref, planes):
        d = dx_ref[...].astype(BF16)
        dxb_ref[...] = d
        da = _nt(d, w_ref[0:ATTN_WIDTH, :])
        ddn_ref[...] = _nt(d, w_ref[ATTN_WIDTH:, :])
        _tile_to_views(da, planes, (da1, da4, da16))
        lo = lax.broadcasted_iota(jnp.int32, (tm, 128), 1) < 64
        cols = []
        for G in range(4):
            sl = slice(G * 128, (G + 1) * 128)
            t = da[:, sl] * attn_ref[:, sl]
            d0 = jnp.sum(jnp.where(lo, t, 0.0), axis=-1, keepdims=True)
            d1 = jnp.sum(jnp.where(lo, 0.0, t), axis=-1, keepdims=True)
            cols.append(jnp.where(lo, d0, d1))
        _tile_to_views(jnp.concatenate(cols, axis=1), planes, (dl1, dl4, dl16))

    tok = lambda w: pl.BlockSpec((tm, w), lambda i: (i, 0))
    views = [_view_spec(d) for d in DILATIONS]
    return _call(
        body, (dx, w_out, attn), name="outproj_bwd", grid=(S // tm,), comm=comm,
        in_specs=[tok(D), pl.BlockSpec((D, D), lambda i: (0, 0)), tok(ATTN_WIDTH)],
        out_specs=views + views + [tok(DN_WIDTH), tok(D)],
        out_shape=[_view_shape(S, d, F32) for d in DILATIONS] * 2
                  + [jax.ShapeDtypeStruct((S, DN_WIDTH), F32), jax.ShapeDtypeStruct((S, D), BF16)],
        scratch_shapes=[pltpu.VMEM((4, tm, 128), F32)])


def _adamw(w, g, m, v, name):
    R, Ccols = w.shape[0], w.shape[-1]
    tr = next((t for t in range(512, 7, -8) if R % t == 0), R)
    c1 = 1.0 - ADAM_B1 ** ADAM_STEP
    c2 = 1.0 - ADAM_B2 ** ADAM_STEP

    def body(w_ref, g_ref, m_ref, v_ref, d_ref, nm_ref, nv_ref):
        gv = g_ref[...]
        mn = ADAM_B1 * m_ref[...] + (1.0 - ADAM_B1) * gv
        vn = ADAM_B2 * v_ref[...] + (1.0 - ADAM_B2) * (gv * gv)
        nm_ref[...] = mn
        nv_ref[...] = vn
        d_ref[...] = -ADAM_LR * ((mn / c1) / (jnp.sqrt(vn / c2) + ADAM_EPS) + ADAM_WD * w_ref[...])

    if w.ndim == 2:
        grid, spec = (R // tr,), pl.BlockSpec((tr, Ccols), lambda i: (i, 0))
    else:
        grid, spec = (2,), pl.BlockSpec((R // 2, 1, Ccols), lambda i: (i, 0, 0))
    return pl.pallas_call(
        body, name=name, grid=grid, in_specs=[spec] * 4, out_specs=[spec] * 3,
        out_shape=[jax.ShapeDtypeStruct(w.shape, F32)] * 3, compiler_params=_params(1),
    )(w, g, m, v)


LATE_WEIGHTS = ("w_in", "w_out", "ffn2_gate", "ffn2_up", "ffn2_down")


def _local_step(x, target, wts, small, dist=None):
    g1, g2, gm, gf = small["norm_ffn1"], small["norm_ffn2"], small["norm_mix"], small["norm_final"]
    wts = dict(wts)

    def reduce_start(gs, tag):
        return _rs_add_pairs(gs, _swap_sibling(gs, True, "rs_swap_halves_" + tag), dist["c"], "rs_add_pairs_" + tag)

    (x1, h1, fg1, fu1), late = _ffn_fwd(x, g1, wts["ffn1_gate"], wts["ffn1_up"], wts["ffn1_down"], "ffn1_fwd",
                                        comm=_ag_comm(dist["late"]) if dist else None)
    if dist:
        wts.update(zip(LATE_WEIGHTS, late))
        wts["w_out"] = wts["w_out"].reshape(D_MODEL, D_MODEL)
        wts["w_in"] = _permute_w_in(wts["w_in"][:, :IN_COLS // N_CHIPS].reshape(IN_COLS, D_MODEL))
    h2, *qkv, xq, xk, xv, gate, bd = _inproj_fwd(x1, gm, wts["w_in"])
    aq, ak, av = qkv[0:3], qkv[3:6], qkv[6:9]
    parts = [_attn_fwd(aq[p], ak[p], av[p], d, f"attn_fwd_d{d}") for p, d in enumerate(DILATIONS)]
    attn, *lse = _attn_merge(parts)
    conv_w = small["conv_w"]
    qn, kn, vv = _conv_fwd(xq, xk, xv, conv_w)
    dn_u, dn_w, dn_qg, dn_kd, dn_aq, dn_t, dn_egl = _dn_prep(qn, kn, vv, bd, small["avec"], small["dvec"])
    dn, o_dn, v_new, states = _dn_scan_fwd(dn_u, dn_w, dn_qg, dn_kd, dn_aq, dn_egl, gate, small["dn_norm"])
    x2, mix = _outproj_fwd(x1, attn, dn, wts["w_out"])
    (dx3, h3, fg2, fu2, loss, d_gf), _ = _ffn_fwd(x2, g2, wts["ffn2_gate"], wts["ffn2_up"], wts["ffn2_down"], "ffn2_fwd",
                                                 head=(gf, target))

    grads = {}
    (dx2, d_g2, dfg2, dfu2, act2, dout2), _ = _ffn_bwd(dx3, x2, g2, fg2, fu2, wts["ffn2_down"], wts["ffn2_gate"],
                                                      wts["ffn2_up"], "ffn2_bwd")
    tk = 2048
    grads["ffn2_gate"], _ = _dw_chunks(dfg2, h3, tk, "dw_ffn2_gate")
    grads["ffn2_up"], _ = _dw_chunks(dfu2, h3, tk, "dw_ffn2_up")
    grads["ffn2_down"], _ = _dw_chunks(act2, dout2, tk, "dw_ffn2_down")
    group_a = ("ffn2_gate", "ffn2_up", "ffn2_down")
    gs_a = [grads[n] for n in group_a]

    (*dviews, ddn, dx2b), swapped_a = _outproj_bwd(dx2, wts["w_out"], attn, comm=_swap_comm(gs_a, True) if dist else None)
    parts_a = _rs_add_pairs(gs_a, swapped_a, dist["c"], "rs_add_pairs_a") if dist else None
    dattn, dd = dviews[0:3], dviews[3:6]
    grads["w_out"] = _matmul_tn(mix, dx2b, D_MODEL, tk, "dw_out").reshape(N_CHIPS, D_MODEL // N_CHIPS, D_MODEL)

    daq, dak, dav = [], [], []
    for p, d in enumerate(DILATIONS):
        daq.append(_attn_bwd_q(aq[p], ak[p], av[p], dattn[p], lse[p], dd[p], d, f"attn_bwd_q_d{d}"))
        dk_p, dv_p = _attn_bwd_kv(aq[p], ak[p], av[p], dattn[p], lse[p], dd[p], d, f"attn_bwd_kv_d{d}")
        dak.append(dk_p)
        dav.append(dv_p)

    do_dn, dvn, dgate, dstates, d_dn_gain = _dn_scan_bwd(dn_w, dn_qg, dn_kd, dn_aq, dn_egl, gate, small["dn_norm"], o_dn, ddn)
    (dqn, dkn, dvv, dbd, dn_small), recv_a = _dn_post(qn, kn, vv, bd, small["avec"], small["dvec"], dn_t, dn_u, dn_w, dn_aq, v_new, states,
                                                      dstates, do_dn, dvn, comm=_rsx_comm(parts_a) if dist else None)
    dcq, dck, dcv, dwq, dwk, dwv = _conv_bwd_pre(xq, xk, xv, conv_w, dqn, dkn, dvv)
    dxq, dxk, dxv = _conv_bwd_x(dcq, dck, dcv, conv_w)
    d_conv = jnp.concatenate([dwq[:CONV_WIDTH], dwk[:CONV_WIDTH], dwv[:CONV_WIDTH]], axis=1)

    dx1, d_gm, dproj = _inproj_bwd(dx2, x1, gm, [daq, dak, dav], [dxq, dxk, dxv, dgate], dbd, wts["w_in"])
    gi = _matmul_tn(dproj, h2, IN_COLS_PADDED, 512, "dw_in")
    if dist:
        gate_end = QKV_COLS + DN_WIDTH
        gi = jnp.concatenate([gi[:QKV_COLS], gi[gate_end:gate_end + LOGIT_COLS], gi[QKV_COLS:gate_end]], axis=0)
        gi = gi.reshape(N_CHIPS, IN_COLS // N_CHIPS, D_MODEL)
        gi = jnp.pad(gi, ((0, 0), (0, W_IN_ROWS - IN_COLS // N_CHIPS), (0, 0)))
    grads["w_in"] = gi
    group_b = ("w_in", "w_out")
    parts_b = reduce_start([grads[n] for n in group_b], "b") if dist else None

    (dx0, d_g1, dfg1, dfu1, act1, dout1), recv_b = _ffn_bwd(dx1, x, g1, fg1, fu1, wts["ffn1_down"], wts["ffn1_gate"],
                                                           wts["ffn1_up"], "ffn1_bwd",
                                                           comm=_rsx_comm(parts_b) if dist else None)
    group_c = ("ffn1_gate", "ffn1_up", "ffn1_down")
    pending, parts_c, recv_c = [], [], []
    for n, (lhs, rhs) in zip(group_c, ((dfg1, h1), (dfu1, h1), (act1, dout1))):
        grads[n], landed = _dw_chunks(lhs, rhs, tk, "dw_" + n, comm=_rsx_comm(pending) if pending else None)
        recv_c += list(landed)
        if dist:
            pending = reduce_start([grads[n]], n)
            parts_c += pending

    small_grads = dict(norm_ffn1=d_g1, norm_mix=d_gm, norm_ffn2=d_g2, norm_final=d_gf, conv_w=d_conv,
                       a_log=dn_small[0:1], dt_bias=dn_small[1:2], dn_norm=d_dn_gain[0:1])
    if dist:
        recv_c += _rs_exchange_arrays(pending)
        names = group_a + group_b + group_c
        totals = _rs_add_totals(list(parts_a) + list(parts_b) + list(parts_c), list(recv_a) + list(recv_b) + list(recv_c),
                                dist["chip"])
        theirs = _swap_sibling(totals, False, "rs_share_total")
        grads = {n: (mine, other) for n, mine, other in zip(names, totals, theirs)}
    return loss, dx0, grads, small_grads


HBM =pl.BlockSpec(memory_space=pl.ANY)
VMEM_SPEC = pl.BlockSpec(memory_space=pltpu.VMEM)


def _coords():
    return lax.axis_index("x"), lax.axis_index("y"), lax.axis_index("c")


def _remote(src, dst, send_sems, recv_sems, k, dev):
    return pltpu.make_async_remote_copy(src_ref=src, dst_ref=dst, send_sem=send_sems.at[k], recv_sem=recv_sems.at[k],
                                        device_id=dev, device_id_type=MESH)


def _allreduce_small(buf, name):
    R, Cc = buf.shape

    def body(src_ref, out_ref, recv_ref, send_sems, recv_sems):
        x, y, c = _coords()
        copies = []
        for m in range(1, 8):
            fx, fy, fc = (m >> 2) & 1, (m >> 1) & 1, m & 1
            dev = (x ^ fx if fx else x, y ^ fy if fy else y, c ^ fc if fc else c)
            cp = _remote(src_ref, recv_ref.at[m - 1], send_sems, recv_sems, m - 1, dev)
            cp.start()
            copies.append(cp)
        for cp in copies:
            cp.wait()
        r = [src_ref[...]] + [recv_ref[m] for m in range(7)]
        out_ref[...] = ((r[0] + r[1]) + (r[2] + r[3])) + ((r[4] + r[5]) + (r[6] + r[7]))

    return pl.pallas_call(
        body, name=name, out_shape=jax.ShapeDtypeStruct((R, Cc), F32),
        in_specs=[VMEM_SPEC], out_specs=VMEM_SPEC,
        scratch_shapes=[pltpu.VMEM((7, R, Cc), F32), pltpu.SemaphoreType.DMA((7,)), pltpu.SemaphoreType.DMA((7,))],
    )(buf)


BIG = ("ffn1_gate", "ffn1_up", "ffn1_down", "w_in", "w_out", "ffn2_gate", "ffn2_up", "ffn2_down")
ROW_SHARDED = ("ffn1_down", "w_out", "ffn2_down")
W_IN_ROWS = 960


def _rows(ref, start, size):
    return ref.at[pl.ds(pl.multiple_of(start, 16), size)]


def _allgather_arrays(shards):
    n = len(shards)
    _, shapes, n_sems, start, finish, middle = _ag_comm(shards)

    def body(*refs):
        for phase in (start, middle, finish):
            phase(refs[:n], refs[n:2 * n], refs[2 * n], refs[2 * n + 1])

    return pl.pallas_call(
        body, name="allgather_weights", out_shape=shapes, in_specs=[HBM] * n, out_specs=[HBM] * n,
        scratch_shapes=[pltpu.SemaphoreType.DMA((n_sems,)), pltpu.SemaphoreType.DMA((n_sems,))],
    )(*shards)


def _ag_copies(srcs, outs, send_sems, recv_sems):
    x, y, c = _coords()
    sib = (x, y, 1 - c)
    xn, yn, dg = (1 - x, y), (x, 1 - y), (1 - x, 1 - y)
    plan = []
    for a, (src, out) in enumerate(zip(srcs, outs)):
        h = src.shape[0] // 2
        q = h // 2
        cp = lambda s, d, k, dev: _remote(s, d, send_sems, recv_sems, 8 * a + k, dev)
        slot = lambda chip: out.at[2 * chip[0] + chip[1]]
        mine, dst = _rows(src, c * h, h), _rows(slot((x, y)), c * h, h)
        piece = lambda chip, start, size, k, dev: cp(_rows(slot(chip), start, size), _rows(slot(chip), start, size), k, dev)
        plan.append(dict(
            own=cp(src, slot((x, y)), 6, sib),
            to_x=cp(mine, dst, 0, (*xn, c)), to_y=cp(mine, dst, 1, (*yn, c)),
            from_x=piece(xn, c * h, h, 0, sib), from_y=piece(yn, c * h, h, 1, sib),
            relay_y=piece(xn, c * h, q, 2, (*yn, c)), relay_x=piece(yn, c * h + q, q, 7, (*xn, c)),
            pass_x=piece(xn, c * h, h, 3, sib), pass_y=piece(yn, c * h, h, 4, sib), pass_d=piece(dg, c * h, h, 5, sib),
            diag_1=piece(dg, c * h, q, 2, sib), diag_2=piece(dg, c * h + q, q, 7, sib),
            got=[piece(chip, (1 - c) * h, h, k, sib) for k, chip in ((3, xn), (4, yn), (5, dg))]))
    return plan


def _ag_start(*refs):
    for p in _ag_copies(*refs):
        for k in ("own", "to_x", "to_y"):
            p[k].start()


def _ag_middle(*refs):
    for p in _ag_copies(*refs):
        p["from_x"].wait_recv()
        p["relay_y"].start()
        p["pass_x"].start()
        p["from_y"].wait_recv()
        p["relay_x"].start()
        p["pass_y"].start()


def _ag_finish(*refs):
    plan = _ag_copies(*refs)
    for p in plan:
        p["diag_1"].wait_recv()
        p["diag_2"].wait_recv()
        p["pass_d"].start()
    for p in plan:
        for cp in p["got"]:
            cp.wait_recv()
        p["own"].wait_recv()
        for k in ("own", "to_x", "to_y", "relay_y", "relay_x", "pass_x", "pass_y", "pass_d"):
            p[k].wait_send()


def _ag_comm(shards):
    shapes = [jax.ShapeDtypeStruct((N_CHIPS,) + s.shape, s.dtype) for s in shards]
    return (list(shards), shapes, 8 * len(shards), _ag_start, _ag_finish, _ag_middle)


def _swap_sibling(arrs, pick_other_half, name):
    n = len(arrs)
    _, outs, n_sems, start, finish = _swap_comm(arrs, pick_other_half)

    def body(*refs):
        start(refs[:n], refs[n:2 * n], refs[2 * n], refs[2 * n + 1])
        finish(refs[:n], refs[n:2 * n], refs[2 * n], refs[2 * n + 1])

    return pl.pallas_call(
        body, name=name, out_shape=outs, in_specs=[HBM] * n, out_specs=[HBM] * n,
        scratch_shapes=[pltpu.SemaphoreType.DMA((n_sems,)), pltpu.SemaphoreType.DMA((n_sems,))],
    )(*arrs)


def _swap_comm(arrs, pick_other_half):
    def copies(srcs, dsts, send_sems, recv_sems):
        x, y, c = _coords()
        cps = []
        for a, (src, dst) in enumerate(zip(srcs, dsts)):
            if pick_other_half:
                h = src.shape[1] // 2
                src = src.at[:, pl.ds(pl.multiple_of((1 - c) * h, 16), h)]
            cps.append(_remote(src, dst, send_sems, recv_sems, a, (x, y, 1 - c)))
        return cps

    def start(*refs):
        for cp in copies(*refs):
            cp.start()

    def finish(*refs):
        for cp in copies(*refs):
            cp.wait()

    shapes = [jax.ShapeDtypeStruct((a.shape[0], a.shape[1] // 2) + a.shape[2:] if pick_other_half else a.shape, a.dtype)
              for a in arrs]
    return (list(arrs), shapes, len(arrs), start, finish)


def _rs_add_pairs(gs, others, c, name):
    n = len(gs)
    blocks = [(g.shape[1] // 4, g.shape[2]) for g in gs]

    def body(c_ref, *refs):
        for a in range(n):
            refs[2 * n + a][...] = (refs[a][...] + refs[n + a][...]).astype(BF16)

    mine = lambda b: pl.BlockSpec((None,) + b, lambda j, s, c_ref: (j, c_ref[0] * 2 + s, 0))
    flat = lambda b: pl.BlockSpec((None,) + b, lambda j, s, c_ref: (j, s, 0))
    return pl.pallas_call(
        body, name=name,
        grid_spec=pltpu.PrefetchScalarGridSpec(
            num_scalar_prefetch=1, grid=(N_CHIPS, 2),
            in_specs=[mine(b) for b in blocks] + [flat(b) for b in blocks],
            out_specs=[flat(b) for b in blocks]),
        out_shape=[jax.ShapeDtypeStruct(o.shape, BF16) for o in others],
        compiler_params=_params(2),
    )(c, *gs, *others)


def _rs_exchange_arrays(parts):
    n = len(parts)

    def body(*refs):
        _rsx_start(refs[:n], refs[n:2 * n], refs[2 * n], refs[2 * n + 1])
        _rsx_finish(refs[:n], refs[n:2 * n], refs[2 * n], refs[2 * n + 1])

    _, shapes, n_sems, _, _ = _rsx_comm(parts)
    return pl.pallas_call(
        body, name="rs_exchange_chips", out_shape=shapes, in_specs=[HBM] * n, out_specs=[HBM] * n,
        scratch_shapes=[pltpu.SemaphoreType.DMA((n_sems,)), pltpu.SemaphoreType.DMA((n_sems,))],
    )(*parts)


def _rsx_copies(srcs, dsts, send_sems, recv_sems):
    x, y, c = _coords()
    others = [(1 - x, y), (x, 1 - y), (1 - x, 1 - y)]
    return [_remote(src.at[2 * ox + oy], dst.at[k], send_sems, recv_sems, 3 * a + k, (ox, oy, c))
            for a, (src, dst) in enumerate(zip(srcs, dsts)) for k, (ox, oy) in enumerate(others)]


def _rsx_start(srcs, dsts, send_sems, recv_sems):
    for cp in _rsx_copies(srcs, dsts, send_sems, recv_sems):
        cp.start()


def _rsx_finish(srcs, dsts, send_sems, recv_sems):
    for cp in _rsx_copies(srcs, dsts, send_sems, recv_sems):
        cp.wait()


def _rsx_comm(parts):
    shapes = [jax.ShapeDtypeStruct((3,) + p.shape[1:], p.dtype) for p in parts]
    return (list(parts), shapes, 3 * len(parts), _rsx_start, _rsx_finish)


def _rs_add_totals(parts, recvs, chip):
    n = len(parts)
    blocks = [(p.shape[1] // 2, p.shape[2]) for p in parts]

    def body(chip_ref, *refs):
        f = lambda r: r[...].astype(F32)
        for a in range(n):
            p, r0, r1, r2 = refs[a], refs[n + 3 * a], refs[n + 3 * a + 1], refs[n + 3 * a + 2]
            refs[4 * n + a][...] = (f(p) + f(r0)) + (f(r1) + f(r2))

    own = lambda b: pl.BlockSpec((None,) + b, lambda s, chip_ref: (chip_ref[0], s, 0))
    slot = lambda b, k: pl.BlockSpec((None,) + b, lambda s, chip_ref, k=k: (k, s, 0))
    recv_specs = [slot(b, k) for b in blocks for k in range(3)]
    recv_args = [r for r in recvs for _ in range(3)]
    return pl.pallas_call(
        body, name="rs_add_totals",
        grid_spec=pltpu.PrefetchScalarGridSpec(
            num_scalar_prefetch=1, grid=(2,),
            in_specs=[own(b) for b in blocks] + recv_specs,
            out_specs=[pl.BlockSpec(b, lambda s, chip_ref: (s, 0)) for b in blocks]),
        out_shape=[jax.ShapeDtypeStruct(p.shape[1:], F32) for p in parts],
        compiler_params=_params(1),
    )(chip, *parts, *recv_args)


def _permute_w_in(wt):
    return jnp.concatenate([wt[:QKV_COLS], wt[QKV_COLS + LOGIT_COLS:IN_COLS], wt[QKV_COLS:QKV_COLS + LOGIT_COLS],
                            jnp.zeros((IN_COLS_PADDED - IN_COLS, wt.shape[1]), wt.dtype)], axis=0)


def _pad_row(v):
    v = v.reshape(1, -1)
    return jnp.pad(v, ((0, 0), (0, D_MODEL - v.shape[1])))


def kernel(x, norm_ffn1, ffn1_gate, ffn1_up, ffn1_down, norm_mix, w_in, conv_w, a_log, dt_bias, dn_norm, w_out, norm_ffn2, ffn2_gate, ffn2_up, ffn2_down, norm_final, loss_target, m_norm_ffn1, m_ffn1_gate, m_ffn1_up, m_ffn1_down, m_norm_mix, m_w_in, m_conv_w, m_a_log, m_dt_bias, m_dn_norm, m_w_out, m_norm_ffn2, m_ffn2_gate, m_ffn2_up, m_ffn2_down, m_norm_final, v_norm_ffn1, v_ffn1_gate, v_ffn1_up, v_ffn1_down, v_norm_mix, v_w_in, v_conv_w, v_a_log, v_dt_bias, v_dn_norm, v_w_out, v_norm_ffn2, v_ffn2_gate, v_ffn2_up, v_ffn2_down, v_norm_final):
    cx, cy, cc = _coords()
    chip = 2 * cx + cy
    stored = lambda t, n: t[0] if n in ROW_SHARDED else t[0].T
    big_w = {n: stored(t, n) for n, t in dict(
        ffn1_gate=ffn1_gate, ffn1_up=ffn1_up, ffn1_down=ffn1_down, w_in=w_in, w_out=w_out,
        ffn2_gate=ffn2_gate, ffn2_up=ffn2_up, ffn2_down=ffn2_down).items()}
    big_m = {n: stored(t, n) for n, t in dict(
        ffn1_gate=m_ffn1_gate, ffn1_up=m_ffn1_up, ffn1_down=m_ffn1_down, w_in=m_w_in, w_out=m_w_out,
        ffn2_gate=m_ffn2_gate, ffn2_up=m_ffn2_up, ffn2_down=m_ffn2_down).items()}
    big_v = {n: stored(t, n) for n, t in dict(
        ffn1_gate=v_ffn1_gate, ffn1_up=v_ffn1_up, ffn1_down=v_ffn1_down, w_in=v_w_in, w_out=v_w_out,
        ffn2_gate=v_ffn2_gate, ffn2_up=v_ffn2_up, ffn2_down=v_ffn2_down).items()}

    cols = IN_COLS // N_CHIPS
    send = {n: big_w[n].astype(BF16) for n in BIG}
    send["w_in"] = jnp.pad(send["w_in"], ((0, W_IN_ROWS - cols), (0, 0)))
    early = tuple(n for n in BIG if n not in LATE_WEIGHTS)
    wts = dict(zip(early, _allgather_arrays([send[n] for n in early])))
    dist =dict(late=[send[n] for n in LATE_WEIGHTS], c=cc.reshape(1).astype(jnp.int32),
                chip=chip.reshape(1).astype(jnp.int32))

    conv_shard = conv_w[0]
    emb = jnp.concatenate([jnp.where((chip == j) & (cc == 0), conv_shard, 0.0) for j in range(N_CHIPS)], axis=1)
    emb = jnp.pad(emb.reshape(6, D_MODEL), ((0, 2), (0, 0)))
    conv_full = _allreduce_small(emb, "allgather_conv_w")[:6].reshape(CONV_WIDTH, 3 * DN_WIDTH)

    zvec = jnp.zeros((1, 128), F32)
    small = dict(norm_ffn1=norm_ffn1, norm_mix=norm_mix, norm_ffn2=norm_ffn2, norm_final=norm_final[None],
                 conv_w=conv_full, avec=zvec.at[0, DN_HEADS:2 * DN_HEADS].set(a_log[0]),
                 dvec=zvec.at[0, DN_HEADS:2 * DN_HEADS].set(dt_bias[0]), dn_norm=dn_norm)

    loss, grad_x, reduced, sg = _local_step(x[0], loss_target[0], wts, small, dist)

    rows = [sg["norm_ffn1"], sg["norm_mix"], sg["norm_ffn2"], sg["norm_final"], _pad_row(sg["a_log"]), _pad_row(sg["dt_bias"]),
            _pad_row(sg["dn_norm"]), _pad_row(loss[0:1]), sg["conv_w"].reshape(6, D_MODEL), jnp.zeros((2, D_MODEL), F32)]
    red = _allreduce_small(jnp.concatenate(rows, axis=0), "allreduce_small")
    loss_out = red[7, 0]
    g_conv_full = red[8:14].reshape(CONV_WIDTH, 3 * DN_WIDTH)
    g_conv = lax.dynamic_slice_in_dim(g_conv_full, chip * (3 * DN_WIDTH // N_CHIPS), 3 * DN_WIDTH // N_CHIPS, axis=1)
    g_small = dict(norm_ffn1=red[0:1], norm_mix=red[1:2], norm_ffn2=red[2:3], norm_final=red[3],
                   a_log=red[4:5, DN_HEADS:2 * DN_HEADS], dt_bias=red[5:6, DN_HEADS:2 * DN_HEADS], dn_norm=red[6:7, :DN_HEAD_DIM])

    out_g, out_d, out_m, out_v = {}, {}, {}, {}
    for n in BIG:
        mine, other = reduced[n]
        g = jnp.where(cc == 0, jnp.concatenate([mine, other], axis=0), jnp.concatenate([other, mine], axis=0))
        if n == "w_in":
            to3 = lambda t: jnp.transpose(t, (2, 0, 1))
            g = g[:cols].reshape(cols, 1, D_MODEL)
            results = (g,) + tuple(_adamw(to3(w_in), g, to3(m_w_in), to3(v_w_in), "adamw_w_in"))
            out_g[n], out_d[n], out_m[n], out_v[n] = (jnp.transpose(t, (1, 2, 0)) for t in results)
            continue
        results = (g,) + tuple(_adamw(big_w[n], g, big_m[n], big_v[n], "adamw_" + n))
        out_g[n], out_d[n], out_m[n], out_v[n] = ((t if n in ROW_SHARDED else t.T)[None] for t in results)
    d, nm, nv = _adamw(conv_w[0], g_conv, m_conv_w[0], v_conv_w[0], "adamw_conv_w")
    out_g["conv_w"], out_d["conv_w"], out_m["conv_w"], out_v["conv_w"] = g_conv[None], d[None], nm[None], nv[None]

    small_names = ("norm_ffn1", "norm_mix", "norm_ffn2", "norm_final", "a_log", "dt_bias", "dn_norm")
    small_w = dict(norm_ffn1=norm_ffn1, norm_mix=norm_mix, norm_ffn2=norm_ffn2, norm_final=norm_final, a_log=a_log,
                   dt_bias=dt_bias, dn_norm=dn_norm)
    small_m = dict(norm_ffn1=m_norm_ffn1, norm_mix=m_norm_mix, norm_ffn2=m_norm_ffn2, norm_final=m_norm_final, a_log=m_a_log,
                   dt_bias=m_dt_bias, dn_norm=m_dn_norm)
    small_v = dict(norm_ffn1=v_norm_ffn1, norm_mix=v_norm_mix, norm_ffn2=v_norm_ffn2, norm_final=v_norm_final, a_log=v_a_log,
                   dt_bias=v_dt_bias, dn_norm=v_dn_norm)
    stack = lambda dct: jnp.concatenate([_pad_row(dct[n]) for n in small_names] + [jnp.zeros((1, D_MODEL), F32)], axis=0)
    d, nm, nv = _adamw(stack(small_w), stack(g_small), stack(small_m), stack(small_v), "adamw_small")
    for k, n in enumerate(small_names):
        shape = small_w[n].shape
        size = math.prod(shape)
        out_g[n] = g_small[n].reshape(shape)
        out_d[n], out_m[n], out_v[n] = (t[k, :size].reshape(shape) for t in (d, nm, nv))

    order = ("norm_ffn1", "ffn1_gate", "ffn1_up", "ffn1_down", "norm_mix", "w_in", "conv_w", "a_log", "dt_bias", "dn_norm",
             "w_out", "norm_ffn2", "ffn2_gate", "ffn2_up", "ffn2_down", "norm_final")
    return (loss_out, grad_x[None], *[out_g[n] for n in order], *[out_d[n] for n in order],
            *[out_m[n] for n in order], *[out_v[n] for n in order])
```
